```python
import math
import jax, jax.numpy as jnp
from jax import lax
import numpy as np

D_MODEL = 1024
BATCH = 8
SEQ = 2048
DEPTH = 1

CHUNK = 64
CONV_DIM = D_MODEL // 2
CONV_WIDTH = 31
N_HEADS = 8
HEAD_DIM = 64
ATT_DIM = N_HEADS * HEAD_DIM
QBLOCK = 128
D_FF = ((8 * D_MODEL + 3 * 256 - 1) // (3 * 256)) * 256
EPS = 1e-6
IN_COLS = 2 * CONV_DIM + 3 * ATT_DIM + 2 * D_MODEL

kernel_name = "hybrid_conformer_conv_stickbreaking_block"


def rmsnorm(x, g):
    xf = x.astype(jnp.float32)
    y = xf * lax.rsqrt(jnp.mean(xf * xf, axis=-1, keepdims=True) + EPS)
    return (y * g.astype(jnp.float32)).astype(x.dtype)


def layernorm(x, g, b):
    xf = x.astype(jnp.float32)
    mu = jnp.mean(xf, axis=-1, keepdims=True)
    var = jnp.mean(jnp.square(xf - mu), axis=-1, keepdims=True)
    y = (xf - mu) * lax.rsqrt(var + EPS)
    return (y * g.astype(jnp.float32) + b.astype(jnp.float32)).astype(x.dtype)


def causal_depthwise_conv(u, w, b):
    k, c = w.shape
    up = jnp.pad(u, ((0, 0), (k - 1, 0), (0, 0)))
    y = lax.conv_general_dilated(
        up, w[:, None, :].astype(u.dtype), window_strides=(1,), padding='VALID',
        dimension_numbers=('NWC', 'WIO', 'NWC'), feature_group_count=c)
    return y + b.astype(u.dtype)


def stick_breaking_attention(q, k, v):
    s_len = q.shape[1]
    scale = 1.0 / math.sqrt(q.shape[-1])
    outs = []
    for i in range(s_len // QBLOCK):
        q0 = i * QBLOCK
        q1 = q0 + QBLOCK
        qb = q[:, q0:q1].astype(jnp.float32)
        kb = k[:, :q1].astype(jnp.float32)
        vb = v[:, :q1].astype(jnp.float32)
        z = jnp.einsum('bqhd,bkhd->bhqk', qb, kb) * scale
        t_idx = q0 + jnp.arange(QBLOCK)[:, None]
        s_idx = jnp.arange(q1)[None, :]
        mask = s_idx < t_idx
        log_1m = jnp.where(mask, jax.nn.log_sigmoid(-z), 0.0)
        rev_excl = lax.cumsum(log_1m, axis=3, reverse=True) - log_1m
        a = jnp.where(mask, jnp.exp(jax.nn.log_sigmoid(z) + rev_excl), 0.0)
        outs.append(jnp.einsum('bhqk,bkhd->bqhd', a, vb))
    return jnp.concatenate(outs, axis=1).astype(q.dtype)


def _fwd_setup_inputs(seed: int = 0) -> dict:
    key = jax.random.key(seed)
    ks = jax.random.split(key, 20)
    f32 = jnp.float32

    def w(k, shape, fan_in):
        return jax.random.normal(k, shape, f32) * (fan_in ** -0.5)

    def gain(k, n):
        return 1.0 + 0.05 * jax.random.normal(k, (n,), f32)

    def bias(k, n):
        return 0.02 * jax.random.normal(k, (n,), f32)

    return {
        'x': jax.random.normal(ks[0], (BATCH, SEQ, D_MODEL), f32),
        'norm_mix_pre': gain(ks[1], D_MODEL),
        'w_in': w(ks[2], (D_MODEL, IN_COLS), D_MODEL),
        'conv_dw_w': w(ks[3], (CONV_WIDTH, CONV_DIM), CONV_WIDTH),
        'conv_dw_b': bias(ks[4], CONV_DIM),
        'conv_ln_g': gain(ks[5], CONV_DIM),
        'conv_ln_b': bias(ks[6], CONV_DIM),
        'w_conv_branch': w(ks[7], (CONV_DIM, D_MODEL), CONV_DIM),
        'b_conv_branch': bias(ks[8], D_MODEL),
        'w_att_branch': w(ks[9], (ATT_DIM, D_MODEL), ATT_DIM),
        'w_out': w(ks[10], (D_MODEL, D_MODEL), D_MODEL),
        'norm_mix_post': gain(ks[11], D_MODEL),
        'norm_ffn_pre': gain(ks[12], D_MODEL),
        'w_ffn_up': w(ks[13], (D_MODEL, 2 * D_FF), D_MODEL),
        'w_ffn_down': w(ks[14], (D_FF, D_MODEL), D_FF),
        'norm_ffn_post': gain(ks[15], D_MODEL),
    }


def _fwd_reference(x, norm_mix_pre, w_in, conv_dw_w, conv_dw_b, conv_ln_g, conv_ln_b,
              w_conv_branch, b_conv_branch, w_att_branch, w_out, norm_mix_post,
              norm_ffn_pre, w_ffn_up, w_ffn_down, norm_ffn_post):
    b, s, d = x.shape
    for _ in range(DEPTH):
        h = rmsnorm(x, norm_mix_pre)
        proj = jnp.einsum('bsd,de->bse', h, w_in)
        splits = np.cumsum([2 * CONV_DIM, ATT_DIM, ATT_DIM, ATT_DIM, D_MODEL]).tolist()
        conv_in, q, k, v, g_conv, g_att = jnp.split(proj, splits, axis=-1)

        u = jax.nn.glu(conv_in, axis=-1)
        u = causal_depthwise_conv(u, conv_dw_w, conv_dw_b)
        u = jax.nn.silu(layernorm(u, conv_ln_g, conv_ln_b))
        conv_out = jnp.einsum('bsc,cd->bsd', u, w_conv_branch) + b_conv_branch

        q = q.reshape(b, s, N_HEADS, HEAD_DIM)
        k = k.reshape(b, s, N_HEADS, HEAD_DIM)
        v = v.reshape(b, s, N_HEADS, HEAD_DIM)
        att = stick_breaking_attention(q, k, v).reshape(b, s, ATT_DIM)
        att_out = jnp.einsum('bsc,cd->bsd', att, w_att_branch)

        merged = jax.nn.sigmoid(g_conv) * conv_out + jax.nn.sigmoid(g_att) * att_out
        mix = jnp.einsum('bsd,de->bse', merged, w_out)
        x = x + rmsnorm(mix, norm_mix_post)

        h = rmsnorm(x, norm_ffn_pre)
        gu = jnp.einsum('bsd,df->bsf', h, w_ffn_up)
        gate, up = jnp.split(gu, 2, axis=-1)
        ff = jnp.einsum('bsf,fd->bsd', jax.nn.silu(gate) * up, w_ffn_down)
        x = x + rmsnorm(ff, norm_ffn_post)
    return x


import jax as _jax
import jax.numpy as _jnp

TWIN_FORMAT = 'train_step'
FWD_PARAMS = ['x', 'norm_mix_pre', 'w_in', 'conv_dw_w', 'conv_dw_b', 'conv_ln_g', 'conv_ln_b', 'w_conv_branch', 'b_conv_branch', 'w_att_branch', 'w_out', 'norm_mix_post', 'norm_ffn_pre', 'w_ffn_up', 'w_ffn_down', 'norm_ffn_post']
TWIN_WEIGHTS = ['norm_mix_pre', 'w_in', 'conv_dw_w', 'conv_dw_b', 'conv_ln_g', 'conv_ln_b', 'w_conv_branch', 'b_conv_branch', 'w_att_branch', 'w_out', 'norm_mix_post', 'norm_ffn_pre', 'w_ffn_up', 'w_ffn_down', 'norm_ffn_post']
TWIN_DIFF_INPUT = 'x'
TWIN_INPUTS = ['x', 'norm_mix_pre', 'w_in', 'conv_dw_w', 'conv_dw_b', 'conv_ln_g', 'conv_ln_b', 'w_conv_branch', 'b_conv_branch', 'w_att_branch', 'w_out', 'norm_mix_post', 'norm_ffn_pre', 'w_ffn_up', 'w_ffn_down', 'norm_ffn_post', 'loss_target', 'm_norm_mix_pre', 'm_w_in', 'm_conv_dw_w', 'm_conv_dw_b', 'm_conv_ln_g', 'm_conv_ln_b', 'm_w_conv_branch', 'm_b_conv_branch', 'm_w_att_branch', 'm_w_out', 'm_norm_mix_post', 'm_norm_ffn_pre', 'm_w_ffn_up', 'm_w_ffn_down', 'm_norm_ffn_post', 'v_norm_mix_pre', 'v_w_in', 'v_conv_dw_w', 'v_conv_dw_b', 'v_conv_ln_g', 'v_conv_ln_b', 'v_w_conv_branch', 'v_b_conv_branch', 'v_w_att_branch', 'v_w_out', 'v_norm_mix_post', 'v_norm_ffn_pre', 'v_w_ffn_up', 'v_w_ffn_down', 'v_norm_ffn_post']
TWIN_OUTPUTS = ['loss', 'grad_x', 'grad_norm_mix_pre', 'grad_w_in', 'grad_conv_dw_w', 'grad_conv_dw_b', 'grad_conv_ln_g', 'grad_conv_ln_b', 'grad_w_conv_branch', 'grad_b_conv_branch', 'grad_w_att_branch', 'grad_w_out', 'grad_norm_mix_post', 'grad_norm_ffn_pre', 'grad_w_ffn_up', 'grad_w_ffn_down', 'grad_norm_ffn_post', 'delta_norm_mix_pre', 'delta_w_in', 'delta_conv_dw_w', 'delta_conv_dw_b', 'delta_conv_ln_g', 'delta_conv_ln_b', 'delta_w_conv_branch', 'delta_b_conv_branch', 'delta_w_att_branch', 'delta_w_out', 'delta_norm_mix_post', 'delta_norm_ffn_pre', 'delta_w_ffn_up', 'delta_w_ffn_down', 'delta_norm_ffn_post', 'new_m_norm_mix_pre', 'new_m_w_in', 'new_m_conv_dw_w', 'new_m_conv_dw_b', 'new_m_conv_ln_g', 'new_m_conv_ln_b', 'new_m_w_conv_branch', 'new_m_b_conv_branch', 'new_m_w_att_branch', 'new_m_w_out', 'new_m_norm_mix_post', 'new_m_norm_ffn_pre', 'new_m_w_ffn_up', 'new_m_w_ffn_down', 'new_m_norm_ffn_post', 'new_v_norm_mix_pre', 'new_v_w_in', 'new_v_conv_dw_w', 'new_v_conv_dw_b', 'new_v_conv_ln_g', 'new_v_conv_ln_b', 'new_v_w_conv_branch', 'new_v_b_conv_branch', 'new_v_w_att_branch', 'new_v_w_out', 'new_v_norm_mix_post', 'new_v_norm_ffn_pre', 'new_v_w_ffn_up', 'new_v_w_ffn_down', 'new_v_norm_ffn_post']
TWIN_LEAF_KINDS = {'loss': 'loss', 'grad_x': 'grad_x', 'grad_norm_mix_pre': 'grad_w', 'grad_w_in': 'grad_w', 'grad_conv_dw_w': 'grad_w', 'grad_conv_dw_b': 'grad_w', 'grad_conv_ln_g': 'grad_w', 'grad_conv_ln_b': 'grad_w', 'grad_w_conv_branch': 'grad_w', 'grad_b_conv_branch': 'grad_w', 'grad_w_att_branch': 'grad_w', 'grad_w_out': 'grad_w', 'grad_norm_mix_post': 'grad_w', 'grad_norm_ffn_pre': 'grad_w', 'grad_w_ffn_up': 'grad_w', 'grad_w_ffn_down': 'grad_w', 'grad_norm_ffn_post': 'grad_w', 'delta_norm_mix_pre': 'delta_w', 'delta_w_in': 'delta_w', 'delta_conv_dw_w': 'delta_w', 'delta_conv_dw_b': 'delta_w', 'delta_conv_ln_g': 'delta_w', 'delta_conv_ln_b': 'delta_w', 'delta_w_conv_branch': 'delta_w', 'delta_b_conv_branch': 'delta_w', 'delta_w_att_branch': 'delta_w', 'delta_w_out': 'delta_w', 'delta_norm_mix_post': 'delta_w', 'delta_norm_ffn_pre': 'delta_w', 'delta_w_ffn_up': 'delta_w', 'delta_w_ffn_down': 'delta_w', 'delta_norm_ffn_post': 'delta_w', 'new_m_norm_mix_pre': 'new_m', 'new_m_w_in': 'new_m', 'new_m_conv_dw_w': 'new_m', 'new_m_conv_dw_b': 'new_m', 'new_m_conv_ln_g': 'new_m', 'new_m_conv_ln_b': 'new_m', 'new_m_w_conv_branch': 'new_m', 'new_m_b_conv_branch': 'new_m', 'new_m_w_att_branch': 'new_m', 'new_m_w_out': 'new_m', 'new_m_norm_mix_post': 'new_m', 'new_m_norm_ffn_pre': 'new_m', 'new_m_w_ffn_up': 'new_m', 'new_m_w_ffn_down': 'new_m', 'new_m_norm_ffn_post': 'new_m', 'new_v_norm_mix_pre': 'new_v', 'new_v_w_in': 'new_v', 'new_v_conv_dw_w': 'new_v', 'new_v_conv_dw_b': 'new_v', 'new_v_conv_ln_g': 'new_v', 'new_v_conv_ln_b': 'new_v', 'new_v_w_conv_branch': 'new_v', 'new_v_b_conv_branch': 'new_v', 'new_v_w_att_branch': 'new_v', 'new_v_w_out': 'new_v', 'new_v_norm_mix_post': 'new_v', 'new_v_norm_ffn_pre': 'new_v', 'new_v_w_ffn_up': 'new_v', 'new_v_w_ffn_down': 'new_v', 'new_v_norm_ffn_post': 'new_v'}


def _forward(args):
    return _fwd_reference(*[args[k] for k in FWD_PARAMS])


def _output_shape():
    out = _jax.eval_shape(lambda: _forward(_fwd_setup_inputs(0)))
    return out.shape, out.dtype

N_MICROBATCH = 1
ADAM_LR = 0.001
ADAM_B1 = 0.9
ADAM_B2 = 0.999
ADAM_EPS = 1e-08
ADAM_WD = 0.01
ADAM_STEP = 10
PER_EXAMPLE_BATCH_AXIS = {'x': 0, 'loss_target': 0}
SHARED_INPUTS = []
_WEIGHT_DTYPES = {'norm_mix_pre': _jnp.float32, 'w_in': _jnp.float32, 'conv_dw_w': _jnp.float32, 'conv_dw_b': _jnp.float32, 'conv_ln_g': _jnp.float32, 'conv_ln_b': _jnp.float32, 'w_conv_branch': _jnp.float32, 'b_conv_branch': _jnp.float32, 'w_att_branch': _jnp.float32, 'w_out': _jnp.float32, 'norm_mix_post': _jnp.float32, 'norm_ffn_pre': _jnp.float32, 'w_ffn_up': _jnp.float32, 'w_ffn_down': _jnp.float32, 'norm_ffn_post': _jnp.float32}
MOMENT_SCALE = {'norm_mix_pre': 3.808924e-01, 'w_in': 1.820621e-01, 'conv_dw_w': 3.319006e-01, 'conv_dw_b': 1.933505e+00, 'conv_ln_g': 7.803060e-01, 'conv_ln_b': 1.161238e+00, 'w_conv_branch': 3.748289e-01, 'b_conv_branch': 1.742525e+00, 'w_att_branch': 2.519333e-01, 'w_out': 4.772344e-01, 'norm_mix_post': 1.598316e+01, 'norm_ffn_pre': 4.747043e-01, 'w_ffn_up': 1.977199e-01, 'w_ffn_down': 3.845292e-01, 'norm_ffn_post': 1.594908e+01}


def _to_microbatches(a, axis):
    t = _jnp.moveaxis(a, axis, 0)
    t = t.reshape((N_MICROBATCH, t.shape[0] // N_MICROBATCH) + t.shape[1:])
    return _jnp.moveaxis(t, 1, axis + 1)


def setup_inputs(seed: int = 0) -> dict:
    inp = _fwd_setup_inputs(seed)
    key = _jax.random.fold_in(_jax.random.key(seed), 7919)
    shape, _ = _output_shape()
    out = dict(inp)
    out["loss_target"] = _jax.random.normal(_jax.random.fold_in(key, 0), shape, _jnp.float32)
    for i, name in enumerate(TWIN_WEIGHTS):
        w = inp[name].astype(_jnp.float32)
        if MOMENT_SCALE is None:
            s = _jnp.sqrt(_jnp.mean(_jnp.square(w)) + 1e-30)
        else:
            s = MOMENT_SCALE[name]
        km, kv = _jax.random.split(_jax.random.fold_in(key, i + 1))
        out[name] = w
        out["m_" + name] = s * _jax.random.normal(km, w.shape, _jnp.float32)
        out["v_" + name] = (s * s) * _jax.random.uniform(kv, w.shape, _jnp.float32, 0.5, 1.5)
    if N_MICROBATCH > 1:
        for name, axis in PER_EXAMPLE_BATCH_AXIS.items():
            out[name] = _to_microbatches(out[name], axis)
    return {'x': out['x'], 'norm_mix_pre': out['norm_mix_pre'], 'w_in': out['w_in'], 'conv_dw_w': out['conv_dw_w'], 'conv_dw_b': out['conv_dw_b'], 'conv_ln_g': out['conv_ln_g'], 'conv_ln_b': out['conv_ln_b'], 'w_conv_branch': out['w_conv_branch'], 'b_conv_branch': out['b_conv_branch'], 'w_att_branch': out['w_att_branch'], 'w_out': out['w_out'], 'norm_mix_post': out['norm_mix_post'], 'norm_ffn_pre': out['norm_ffn_pre'], 'w_ffn_up': out['w_ffn_up'], 'w_ffn_down': out['w_ffn_down'], 'norm_ffn_post': out['norm_ffn_post'], 'loss_target': out['loss_target'], 'm_norm_mix_pre': out['m_norm_mix_pre'], 'm_w_in': out['m_w_in'], 'm_conv_dw_w': out['m_conv_dw_w'], 'm_conv_dw_b': out['m_conv_dw_b'], 'm_conv_ln_g': out['m_conv_ln_g'], 'm_conv_ln_b': out['m_conv_ln_b'], 'm_w_conv_branch': out['m_w_conv_branch'], 'm_b_conv_branch': out['m_b_conv_branch'], 'm_w_att_branch': out['m_w_att_branch'], 'm_w_out': out['m_w_out'], 'm_norm_mix_post': out['m_norm_mix_post'], 'm_norm_ffn_pre': out['m_norm_ffn_pre'], 'm_w_ffn_up': out['m_w_ffn_up'], 'm_w_ffn_down': out['m_w_ffn_down'], 'm_norm_ffn_post': out['m_norm_ffn_post'], 'v_norm_mix_pre': out['v_norm_mix_pre'], 'v_w_in': out['v_w_in'], 'v_conv_dw_w': out['v_conv_dw_w'], 'v_conv_dw_b': out['v_conv_dw_b'], 'v_conv_ln_g': out['v_conv_ln_g'], 'v_conv_ln_b': out['v_conv_ln_b'], 'v_w_conv_branch': out['v_w_conv_branch'], 'v_b_conv_branch': out['v_b_conv_branch'], 'v_w_att_branch': out['v_w_att_branch'], 'v_w_out': out['v_w_out'], 'v_norm_mix_post': out['v_norm_mix_post'], 'v_norm_ffn_pre': out['v_norm_ffn_pre'], 'v_w_ffn_up': out['v_w_ffn_up'], 'v_w_ffn_down': out['v_w_ffn_down'], 'v_norm_ffn_post': out['v_norm_ffn_post']}


def _loss(weights, diff, rest, loss_target):
    with _jax.named_scope("forward"):
        args = {**rest, TWIN_DIFF_INPUT: diff, **{k: w.astype(_WEIGHT_DTYPES[k]) for k, w in weights.items()}}
        y = _forward(args)
    with _jax.named_scope("loss_head"):
        err = _jnp.square(y.astype(_jnp.float32) - loss_target)
        return 0.5 * _jnp.sum(_jnp.mean(err, axis=-1)) if err.ndim else 0.5 * err


def _adamw(w, g, m, v):
    m = ADAM_B1 * m + (1.0 - ADAM_B1) * g
    v = ADAM_B2 * v + (1.0 - ADAM_B2) * _jnp.square(g)
    m_hat = m / (1.0 - ADAM_B1 ** ADAM_STEP)
    v_hat = v / (1.0 - ADAM_B2 ** ADAM_STEP)
    delta = -ADAM_LR * (m_hat / (_jnp.sqrt(v_hat) + ADAM_EPS) + ADAM_WD * w)
    return delta, m, v


def reference(x, norm_mix_pre, w_in, conv_dw_w, conv_dw_b, conv_ln_g, conv_ln_b, w_conv_branch, b_conv_branch, w_att_branch, w_out, norm_mix_post, norm_ffn_pre, w_ffn_up, w_ffn_down, norm_ffn_post, loss_target, m_norm_mix_pre, m_w_in, m_conv_dw_w, m_conv_dw_b, m_conv_ln_g, m_conv_ln_b, m_w_conv_branch, m_b_conv_branch, m_w_att_branch, m_w_out, m_norm_mix_post, m_norm_ffn_pre, m_w_ffn_up, m_w_ffn_down, m_norm_ffn_post, v_norm_mix_pre, v_w_in, v_conv_dw_w, v_conv_dw_b, v_conv_ln_g, v_conv_ln_b, v_w_conv_branch, v_b_conv_branch, v_w_att_branch, v_w_out, v_norm_mix_post, v_norm_ffn_pre, v_w_ffn_up, v_w_ffn_down, v_norm_ffn_post):
    given = dict(x=x, norm_mix_pre=norm_mix_pre, w_in=w_in, conv_dw_w=conv_dw_w, conv_dw_b=conv_dw_b, conv_ln_g=conv_ln_g, conv_ln_b=conv_ln_b, w_conv_branch=w_conv_branch, b_conv_branch=b_conv_branch, w_att_branch=w_att_branch, w_out=w_out, norm_mix_post=norm_mix_post, norm_ffn_pre=norm_ffn_pre, w_ffn_up=w_ffn_up, w_ffn_down=w_ffn_down, norm_ffn_post=norm_ffn_post, loss_target=loss_target, m_norm_mix_pre=m_norm_mix_pre, m_w_in=m_w_in, m_conv_dw_w=m_conv_dw_w, m_conv_dw_b=m_conv_dw_b, m_conv_ln_g=m_conv_ln_g, m_conv_ln_b=m_conv_ln_b, m_w_conv_branch=m_w_conv_branch, m_b_conv_branch=m_b_conv_branch, m_w_att_branch=m_w_att_branch, m_w_out=m_w_out, m_norm_mix_post=m_norm_mix_post, m_norm_ffn_pre=m_norm_ffn_pre, m_w_ffn_up=m_w_ffn_up, m_w_ffn_down=m_w_ffn_down, m_norm_ffn_post=m_norm_ffn_post, v_norm_mix_pre=v_norm_mix_pre, v_w_in=v_w_in, v_conv_dw_w=v_conv_dw_w, v_conv_dw_b=v_conv_dw_b, v_conv_ln_g=v_conv_ln_g, v_conv_ln_b=v_conv_ln_b, v_w_conv_branch=v_w_conv_branch, v_b_conv_branch=v_b_conv_branch, v_w_att_branch=v_w_att_branch, v_w_out=v_w_out, v_norm_mix_post=v_norm_mix_post, v_norm_ffn_pre=v_norm_ffn_pre, v_w_ffn_up=v_w_ffn_up, v_w_ffn_down=v_w_ffn_down, v_norm_ffn_post=v_norm_ffn_post)
    weights = {n: given[n] for n in TWIN_WEIGHTS}
    shared = {n: given[n] for n in SHARED_INPUTS}
    per_example = {n: given[n] for n in ['x']}
    grad_fn = _jax.value_and_grad(_loss, argnums=(0, 1))

    def one_microbatch(ex, loss_target):
        ex = dict(ex)
        diff = ex.pop(TWIN_DIFF_INPUT)
        return grad_fn(weights, diff, {**shared, **ex}, loss_target)

    if N_MICROBATCH == 1:
        loss, (grad_w, grad_x) = one_microbatch(per_example, given["loss_target"])
    else:
        def body(carry, xs):
            loss_sum, grad_sum = carry
            l_k, (gw_k, gx_k) = one_microbatch(xs[0], xs[1])
            with _jax.named_scope("update"):
                return (loss_sum + l_k, _jax.tree.map(_jnp.add, grad_sum, gw_k)), gx_k

        init = (_jnp.zeros((), _jnp.float32), _jax.tree.map(_jnp.zeros_like, weights))
        (loss, grad_w), grad_x = _jax.lax.scan(body, init, (per_example, given["loss_target"]))
    with _jax.named_scope("update"):
        delta_w, new_m, new_v = {}, {}, {}
        for n in TWIN_WEIGHTS:
            delta_w[n], new_m[n], new_v[n] = _adamw(weights[n], grad_w[n], given["m_" + n], given["v_" + n])
    return (loss, grad_x, *[grad_w[n] for n in TWIN_WEIGHTS], *[delta_w[n] for n in TWIN_WEIGHTS],
            *[new_m[n] for n in TWIN_WEIGHTS], *[new_v[n] for n in TWIN_WEIGHTS])
```

```python
import functools
import math

import jax
import jax.numpy as jnp
from jax import lax
from jax.experimental import pallas as pl
from jax.experimental.pallas import tpu as pltpu

F32 = jnp.float32
BF16 = jnp.bfloat16

N_DEV = 8
D_MODEL = 1024
CONV_DIM = 512
CONV_WIDTH = 31
N_HEADS = 8
HEAD_DIM = 64
ATT_DIM = N_HEADS * HEAD_DIM
D_FF = 2816
EPS = 1e-6
IN_SPLITS = (0, 1024, 1536, 2048, 2560, 3584, 4608)

ADAM_LR = 0.001
ADAM_B1 = 0.9
ADAM_B2 = 0.999
ADAM_EPS = 1e-08
ADAM_WD = 0.01
ADAM_STEP = 10

LANES = 128
SUBLANES = 8
HALO = 32
ATT_TILE = 128
VMEM_LIMIT = 56 * 1024 * 1024
MESH = pl.DeviceIdType.MESH
ANY = pl.BlockSpec(memory_space=pl.ANY)


def _pick(dim, target, align=LANES):
    t = min(dim, target)
    t -= t % align
    while t >= align:
        if dim % t == 0:
            return t
        t -= align
    return dim


def _params(semantics):
    return pltpu.CompilerParams(dimension_semantics=semantics, vmem_limit_bytes=VMEM_LIMIT)


def _matmul(a, b, *, name, ta=False, tb=False, out_dtype=F32, tm=1024, tn=1024, tk=512):
    m, k = (a.shape[1], a.shape[0]) if ta else a.shape
    n, k2 = b.shape if tb else (b.shape[1], b.shape[0])
    assert k == k2, (a.shape, b.shape, ta, tb)
    tm, tn, tk = _pick(m, tm), _pick(n, tn), _pick(k, tk)
    nk = k // tk
    dims = (((0 if ta else 1,), (1 if tb else 0,)), ((), ()))

    def body(a_ref, b_ref, o_ref, acc_ref):
        kk = pl.program_id(2)

        @pl.when(kk == 0)
        def _():
            acc_ref[...] = jnp.zeros_like(acc_ref)

        acc_ref[...] += lax.dot_general(a_ref[...], b_ref[...], dims, preferred_element_type=F32)

        @pl.when(kk == nk - 1)
        def _():
            o_ref[...] = acc_ref[...].astype(o_ref.dtype)

    a_spec = pl.BlockSpec((tk, tm), lambda i, j, kk: (kk, i)) if ta else pl.BlockSpec((tm, tk), lambda i, j, kk: (i, kk))
    b_spec = pl.BlockSpec((tn, tk), lambda i, j, kk: (j, kk)) if tb else pl.BlockSpec((tk, tn), lambda i, j, kk: (kk, j))
    return pl.pallas_call(
        body, name=name, grid=(m // tm, n // tn, nk),
        in_specs=[a_spec, b_spec],
        out_specs=pl.BlockSpec((tm, tn), lambda i, j, kk: (i, j)),
        out_shape=jax.ShapeDtypeStruct((m, n), out_dtype),
        scratch_shapes=[pltpu.VMEM((tm, tn), F32)],
        compiler_params=_params(("parallel", "parallel", "arbitrary")),
    )(a, b)


def _rowwise(name, fn, rows, bcasts, row_outs, red_outs=(), tm=256):
    s = rows[0].shape[0]
    tm = _pick(s, tm, 16)
    nr, nb, no = len(rows), len(bcasts), len(row_outs)

    def body(*refs):
        ins = [r[...] for r in refs[:nr + nb]]
        outs, reds = fn(*ins)
        for ref, val in zip(refs[nr + nb:nr + nb + no], outs):
            ref[...] = val.astype(ref.dtype)
        i = pl.program_id(0)
        for ref, val in zip(refs[nr + nb + no:], reds):
            @pl.when(i == 0)
            def _():
                ref[...] = val

            @pl.when(i > 0)
            def _():
                ref[...] += val

    in_specs = [pl.BlockSpec((tm, r.shape[1]), lambda i: (i, 0)) for r in rows]
    in_specs += [pl.BlockSpec(b.shape, lambda i: (0, 0)) for b in bcasts]
    out_specs = [pl.BlockSpec((tm, o.shape[1]), lambda i: (i, 0)) for o in row_outs]
    out_specs += [pl.BlockSpec(d.shape, lambda i: (0, 0)) for d in red_outs]
    return pl.pallas_call(
        body, name=name, grid=(s // tm,), in_specs=in_specs, out_specs=out_specs,
        out_shape=list(row_outs) + list(red_outs),
        compiler_params=_params(("arbitrary",)),
    )(*rows, *bcasts)


def _sds(shape, dtype=F32):
    return jax.ShapeDtypeStruct(shape, dtype)


def _rms(x, g):
    y = x * lax.rsqrt(jnp.mean(x * x, axis=-1, keepdims=True) + EPS)
    return y * g


def _silu(x):
    return x * jax.nn.sigmoid(x)


def _ln_silu(u, g, b):
    mu = jnp.mean(u, axis=-1, keepdims=True)
    var = jnp.mean(jnp.square(u - mu), axis=-1, keepdims=True)
    return _silu((u - mu) * lax.rsqrt(var + EPS) * g + b)


def _merge(conv_pre, att_out, g_conv, g_att, b_cb):
    return jax.nn.sigmoid(g_conv) * (conv_pre + b_cb) + jax.nn.sigmoid(g_att) * att_out


def _glu(t):
    return t[:, :CONV_DIM] * jax.nn.sigmoid(t[:, CONV_DIM:])


def _conv_fwd(conv_in, w_pad, b, ln_g, ln_b, tm=256):
    s = conv_in.shape[0]
    tm = _pick(s, tm, HALO)
    ratio = tm // HALO

    def body(main_ref, halo_ref, w_ref, b_ref, g_ref, be_ref, u3_ref, u1_ref, buf):
        i = pl.program_id(0)
        buf[0:HALO, :] = _glu(halo_ref[...]) * (i > 0).astype(F32)
        buf[HALO:HALO + tm, :] = _glu(main_ref[...])
        acc = jnp.zeros((tm, CONV_DIM), F32) + b_ref[...]
        for j in range(CONV_WIDTH):
            acc = acc + w_ref[j:j + 1, :] * buf[pl.ds(HALO - (CONV_WIDTH - 1) + j, tm), :]
        u1_ref[...] = acc
        u3_ref[...] = _ln_silu(acc, g_ref[...], be_ref[...]).astype(u3_ref.dtype)

    return pl.pallas_call(
        body, name="conv_fwd", grid=(s // tm,),
        in_specs=[pl.BlockSpec((tm, 2 * CONV_DIM), lambda i: (i, 0)),
                  pl.BlockSpec((HALO, 2 * CONV_DIM), lambda i: (jnp.maximum(i * ratio - 1, 0), 0)),
                  pl.BlockSpec(w_pad.shape, lambda i: (0, 0)),
                  pl.BlockSpec(b.shape, lambda i: (0, 0)),
                  pl.BlockSpec(ln_g.shape, lambda i: (0, 0)),
                  pl.BlockSpec(ln_b.shape, lambda i: (0, 0))],
        out_specs=[pl.BlockSpec((tm, CONV_DIM), lambda i: (i, 0)),
                   pl.BlockSpec((tm, CONV_DIM), lambda i: (i, 0))],
        out_shape=[_sds((s, CONV_DIM), BF16), _sds((s, CONV_DIM), F32)],
        scratch_shapes=[pltpu.VMEM((tm + HALO, CONV_DIM), F32)],
        compiler_params=_params(("arbitrary",)),
    )(conv_in, conv_in, w_pad, b, ln_g, ln_b)


def _conv_bwd(conv_in, du1, w_pad, tm=256):
    s = conv_in.shape[0]
    tm = _pick(s, tm, HALO)
    ratio = tm // HALO
    nt = s // tm
    last_halo = s // HALO - 1

    def body(main_ref, halo_ref, du_ref, dun_ref, w_ref, dci_ref, dw_ref, db_ref, ubuf, dbuf):
        i = pl.program_id(0)
        main = main_ref[...]
        a = main[:, :CONV_DIM]
        sb = jax.nn.sigmoid(main[:, CONV_DIM:])
        ubuf[0:HALO, :] = _glu(halo_ref[...]) * (i > 0).astype(F32)
        ubuf[HALO:HALO + tm, :] = a * sb
        du = du_ref[...]
        dbuf[0:tm, :] = du
        dbuf[tm:tm + HALO, :] = dun_ref[...] * (i < nt - 1).astype(F32)

        @pl.when(i == 0)
        def _():
            dw_ref[...] = jnp.zeros_like(dw_ref)
            db_ref[...] = jnp.zeros_like(db_ref)

        du0 = jnp.zeros((tm, CONV_DIM), F32)
        for j in range(CONV_WIDTH):
            du0 = du0 + w_ref[j:j + 1, :] * dbuf[pl.ds(CONV_WIDTH - 1 - j, tm), :]
            dw_ref[j:j + 1, :] += jnp.sum(du * ubuf[pl.ds(HALO - (CONV_WIDTH - 1) + j, tm), :], axis=0, keepdims=True)
        db_ref[...] += jnp.sum(du, axis=0, keepdims=True)
        dci_ref[:, :CONV_DIM] = (du0 * sb).astype(dci_ref.dtype)
        dci_ref[:, CONV_DIM:] = (du0 * a * sb * (1.0 - sb)).astype(dci_ref.dtype)

    return pl.pallas_call(
        body, name="conv_bwd", grid=(nt,),
        in_specs=[pl.BlockSpec((tm, 2 * CONV_DIM), lambda i: (i, 0)),
                  pl.BlockSpec((HALO, 2 * CONV_DIM), lambda i: (jnp.maximum(i * ratio - 1, 0), 0)),
                  pl.BlockSpec((tm, CONV_DIM), lambda i: (i, 0)),
                  pl.BlockSpec((HALO, CONV_DIM), lambda i: (jnp.minimum((i + 1) * ratio, last_halo), 0)),
                  pl.BlockSpec(w_pad.shape, lambda i: (0, 0))],
        out_specs=[pl.BlockSpec((tm, 2 * CONV_DIM), lambda i: (i, 0)),
                   pl.BlockSpec(w_pad.shape, lambda i: (0, 0)),
                   pl.BlockSpec((1, CONV_DIM), lambda i: (0, 0))],
        out_shape=[_sds((s, 2 * CONV_DIM), BF16), _sds(w_pad.shape), _sds((1, CONV_DIM))],
        scratch_shapes=[pltpu.VMEM((tm + HALO, CONV_DIM), F32), pltpu.VMEM((tm + HALO, CONV_DIM), F32)],
        compiler_params=_params(("arbitrary",)),
    )(conv_in, conv_in, du1, du1, w_pad)


def _logsig_neg(z):
    return jnp.minimum(-z, 0.0) - jnp.log(1.0 + jnp.exp(-jnp.abs(z)))


def _split_dot(val, tri):
    hi = val.astype(BF16)
    lo = (val - hi.astype(F32)).astype(BF16)
    return jnp.dot(hi, tri, preferred_element_type=F32) + jnp.dot(lo, tri, preferred_element_type=F32)


def _attn_fwd(q, k, v):
    h, s, dh = q.shape
    t = ATT_TILE
    scale = 1.0 / math.sqrt(dh)

    def body(q_ref, k_ref, v_ref, o_ref, lt_ref):
        i = pl.program_id(1)
        qs = (q_ref[...].astype(F32) * scale).astype(BF16)
        row = lax.broadcasted_iota(jnp.int32, (t, t), 0)
        col = lax.broadcasted_iota(jnp.int32, (t, t), 1)
        causal = col < row
        r2 = lax.broadcasted_iota(jnp.int32, (t, 2 * t), 0)
        c2 = lax.broadcasted_iota(jnp.int32, (t, 2 * t), 1)
        tri = jnp.where((r2 > c2) | (c2 >= t), 1.0, 0.0).astype(BF16)

        def step(kb, carry, masked):
            c, acc = carry
            off = pl.multiple_of(kb * t, t)
            kblk = k_ref[pl.ds(off, t), :]
            vblk = v_ref[pl.ds(off, t), :]
            z = lax.dot_general(qs, kblk, (((1,), (1,)), ((), ())), preferred_element_type=F32)
            l = _logsig_neg(z)
            if masked:
                l = jnp.where(causal, l, 0.0)
            rs = _split_dot(l, tri)
            e = z + l + rs[:, :t] + c
            if masked:
                e = jnp.where(causal, e, -1e30)
            a = jnp.exp(e)
            acc = acc + jnp.dot(a.astype(BF16), vblk, preferred_element_type=F32)
            return c + rs[:, t:], acc

        carry = step(i, (jnp.zeros((t, t), F32), jnp.zeros((t, dh), F32)), True)
        carry = lax.fori_loop(0, i, lambda n, cr: step(i - 1 - n, cr, False), carry)
        lt_ref[...] = carry[0]
        o_ref[...] = carry[1].astype(o_ref.dtype)

    return pl.pallas_call(
        body, name="attn_fwd", grid=(h, s // t),
        in_specs=[pl.BlockSpec((None, t, dh), lambda hh, i: (hh, i, 0)),
                  pl.BlockSpec((None, s, dh), lambda hh, i: (hh, 0, 0)),
                  pl.BlockSpec((None, s, dh), lambda hh, i: (hh, 0, 0))],
        out_specs=[pl.BlockSpec((None, t, dh), lambda hh, i: (hh, i, 0)),
                   pl.BlockSpec((None, t, t), lambda hh, i: (hh, i, 0))],
        out_shape=[_sds((h, s, dh), BF16), _sds((h, s, t), F32)],
        compiler_params=_params(("parallel", "arbitrary")),
    )(q, k, v)


def _attn_bwd(q, k, v, do, ltot):
    h, s, dh = q.shape
    t = ATT_TILE
    scale = 1.0 / math.sqrt(dh)

    def body(q_ref, k_ref, v_ref, do_ref, lt_ref, dq_ref, dk_ref, dv_ref):
        i = pl.program_id(1)

        @pl.when(i == 0)
        def _():
            dk_ref[...] = jnp.zeros_like(dk_ref)
            dv_ref[...] = jnp.zeros_like(dv_ref)

        qb = q_ref[...]
        qs = (qb.astype(F32) * scale).astype(BF16)
        dob = do_ref[...]
        lt = lt_ref[...]
        row = lax.broadcasted_iota(jnp.int32, (t, t), 0)
        col = lax.broadcasted_iota(jnp.int32, (t, t), 1)
        causal = col < row
        r2 = lax.broadcasted_iota(jnp.int32, (t, 2 * t), 0)
        c2 = lax.broadcasted_iota(jnp.int32, (t, 2 * t), 1)
        tri = jnp.where((r2 <= c2) | (c2 >= t), 1.0, 0.0).astype(BF16)

        def step(kb, carry, masked):
            cl, cg, dq = carry
            off = pl.multiple_of(kb * t, t)
            kblk = k_ref[pl.ds(off, t), :]
            vblk = v_ref[pl.ds(off, t), :]
            z = lax.dot_general(qs, kblk, (((1,), (1,)), ((), ())), preferred_element_type=F32)
            l = _logsig_neg(z)
            if masked:
                l = jnp.where(causal, l, 0.0)
            ls = _split_dot(l, tri)
            e = z + l + (lt - cl - ls[:, :t])
            if masked:
                e = jnp.where(causal, e, -1e30)
            a = jnp.exp(e)
            da = lax.dot_general(dob, vblk, (((1,), (1,)), ((), ())), preferred_element_type=F32)
            g = da * a
            gs = jnp.dot(g.astype(BF16), tri, preferred_element_type=F32)
            p = cg + gs[:, :t] - g
            el = jnp.exp(l)
            dz = g * el - p * (1.0 - el)
            if masked:
                dz = jnp.where(causal, dz, 0.0)
            dzb = (dz * scale).astype(BF16)
            dq = dq + jnp.dot(dzb, kblk, preferred_element_type=F32)
            dk_ref[pl.ds(off, t), :] += lax.dot_general(dzb, qb, (((0,), (0,)), ((), ())), preferred_element_type=F32)
            dv_ref[pl.ds(off, t), :] += lax.dot_general(a.astype(BF16), dob, (((0,), (0,)), ((), ())), preferred_element_type=F32)
            return cl + ls[:, t:], cg + gs[:, t:], dq

        init = (jnp.zeros((t, t), F32), jnp.zeros((t, t), F32), jnp.zeros((t, dh), F32))
        carry = lax.fori_loop(0, i, lambda kb, cr: step(kb, cr, False), init)
        carry = step(i, carry, True)
        dq_ref[...] = carry[2]

    return pl.pallas_call(
        body, name="attn_bwd", grid=(h, s // t),
        in_specs=[pl.BlockSpec((None, t, dh), lambda hh, i: (hh, i, 0)),
                  pl.BlockSpec((None, s, dh), lambda hh, i: (hh, 0, 0)),
                  pl.BlockSpec((None, s, dh), lambda hh, i: (hh, 0, 0)),
                  pl.BlockSpec((None, t, dh), lambda hh, i: (hh, i, 0)),
                  pl.BlockSpec((None, t, t), lambda hh, i: (hh, i, 0))],
        out_specs=[pl.BlockSpec((None, t, dh), lambda hh, i: (hh, i, 0)),
                   pl.BlockSpec((None, s, dh), lambda hh, i: (hh, 0, 0)),
                   pl.BlockSpec((None, s, dh), lambda hh, i: (hh, 0, 0))],
        out_shape=[_sds((h, s, dh)), _sds((h, s, dh)), _sds((h, s, dh))],
        compiler_params=_params(("parallel", "arbitrary")),
    )(q, k, v, do, ltot)


def _to_heads(a):
    return a.reshape(a.shape[0], N_HEADS, HEAD_DIM).transpose(1, 0, 2)


def _from_heads(a):
    return a.transpose(1, 0, 2).reshape(a.shape[1], ATT_DIM)


def _local_step(x, target, w):
    s = x.shape[0]
    g1, g2, g3, g4 = w["norm_mix_pre"], w["norm_mix_post"], w["norm_ffn_pre"], w["norm_ffn_post"]

    (h1,) = _rowwise("pre_norm", lambda xt, g: ((_rms(xt, g),), ()), [x], [g1], [_sds((s, D_MODEL), BF16)])
    w_in = w["w_in"]
    cols = [w_in[:, IN_SPLITS[n]:IN_SPLITS[n + 1]] for n in range(6)]
    conv_in = _matmul(h1, cols[0], name="proj_conv")
    q = _matmul(h1, cols[1], name="proj_q", out_dtype=BF16)
    k = _matmul(h1, cols[2], name="proj_k", out_dtype=BF16)
    v = _matmul(h1, cols[3], name="proj_v", out_dtype=BF16)
    g_conv = _matmul(h1, cols[4], name="proj_gate_conv")
    g_att = _matmul(h1, cols[5], name="proj_gate_att")

    u3, u1 = _conv_fwd(conv_in, w["conv_dw_w"], w["conv_dw_b"], w["conv_ln_g"], w["conv_ln_b"])
    qh, kh, vh = _to_heads(q), _to_heads(k), _to_heads(v)
    att_h, ltot = _attn_fwd(qh, kh, vh)
    att = _from_heads(att_h)

    conv_pre = _matmul(u3, w["w_conv_branch"], name="conv_branch")
    att_out = _matmul(att, w["w_att_branch"], name="att_branch")
    (merged,) = _rowwise("merge", lambda *a: ((_merge(*a),), ()), [conv_pre, att_out, g_conv, g_att],
                         [w["b_conv_branch"]], [_sds((s, D_MODEL), BF16)])
    mix = _matmul(merged, w["w_out"], name="mix_out")

    def mid_fn(xt, mt, g2_, g3_):
        x2_ = xt + _rms(mt, g2_)
        return (x2_, _rms(x2_, g3_)), ()

    x2, h2 = _rowwise("mid_norm", mid_fn, [x, mix], [g2, g3], [_sds((s, D_MODEL)), _sds((s, D_MODEL), BF16)])
    w_up = w["w_ffn_up"]
    gate = _matmul(h2, w_up[:, :D_FF], name="ffn_gate")
    up = _matmul(h2, w_up[:, D_FF:], name="ffn_up")
    (act,) = _rowwise("swiglu", lambda g_, u_: ((_silu(g_) * u_,), ()), [gate, up], [], [_sds((s, D_FF), BF16)], tm=128)
    ff = _matmul(act, w["w_ffn_down"], name="ffn_down")

    def final_fn(x2t, fft, tgt, g4_):
        n4, vjp = jax.vjp(_rms, fft, g4_)
        err = x2t + n4 - tgt
        dy = err * (1.0 / D_MODEL)
        dff, dg4 = vjp(dy)
        return (dy, dff), (jnp.sum(err * err, axis=0, keepdims=True), dg4)

    dy, dff, loss_cols, d_g4 = _rowwise("final", final_fn, [x2, ff, target], [g4],
                                        [_sds((s, D_MODEL)), _sds((s, D_MODEL), BF16)],
                                        [_sds((1, D_MODEL)), _sds((1, D_MODEL))])
    loss = 0.5 * jnp.sum(loss_cols) / D_MODEL

    d_act = _matmul(dff, w["w_ffn_down"], tb=True, name="d_act")
    d_w_down = _matmul(act, dff, ta=True, name="d_w_down", out_dtype=BF16)

    def swiglu_bwd_fn(g_, u_, da_):
        _, vjp = jax.vjp(lambda a, b: _silu(a) * b, g_, u_)
        return vjp(da_), ()

    d_gate, d_up = _rowwise("swiglu_bwd", swiglu_bwd_fn, [gate, up, d_act], [],
                            [_sds((s, D_FF), BF16), _sds((s, D_FF), BF16)], tm=128)
    dh2a = _matmul(d_gate, w_up[:, :D_FF], tb=True, name="d_h2_gate")
    dh2b = _matmul(d_up, w_up[:, D_FF:], tb=True, name="d_h2_up")
    d_w_gate = _matmul(h2, d_gate, ta=True, name="d_w_gate", out_dtype=BF16)
    d_w_up = _matmul(h2, d_up, ta=True, name="d_w_up", out_dtype=BF16)

    def mid_bwd_fn(xt, mt, dyt, da, db, g2_, g3_):
        n2, vjp2 = jax.vjp(_rms, mt, g2_)
        x2_ = xt + n2
        _, vjp3 = jax.vjp(_rms, x2_, g3_)
        dx2_, dg3 = vjp3(da + db)
        dx2_ = dx2_ + dyt
        dmix_, dg2 = vjp2(dx2_)
        return (dx2_, dmix_), (dg2, dg3)

    dx2, dmix, d_g2, d_g3 = _rowwise("mid_bwd", mid_bwd_fn, [x, mix, dy, dh2a, dh2b], [g2, g3],
                                     [_sds((s, D_MODEL)), _sds((s, D_MODEL), BF16)],
                                     [_sds((1, D_MODEL)), _sds((1, D_MODEL))])
    d_merged = _matmul(dmix, w["w_out"], tb=True, name="d_merged")
    d_w_out = _matmul(merged, dmix, ta=True, name="d_w_out", out_dtype=BF16)

    def merge_bwd_fn(cp, ao, gc, ga, dm, b_cb):
        _, vjp = jax.vjp(_merge, cp, ao, gc, ga, b_cb)
        dcp, dao, dgc, dga, dbias = vjp(dm)
        return (dcp, dao, dgc, dga), (dbias,)

    d_conv_out, d_att_out, d_g_conv, d_g_att, d_b_cb = _rowwise(
        "merge_bwd", merge_bwd_fn, [conv_pre, att_out, g_conv, g_att, d_merged], [w["b_conv_branch"]],
        [_sds((s, D_MODEL), BF16)] * 4, [_sds((1, D_MODEL))])

    du3 = _matmul(d_conv_out, w["w_conv_branch"], tb=True, name="d_u3")
    d_w_cb = _matmul(u3, d_conv_out, ta=True, name="d_w_conv_branch", out_dtype=BF16)
    d_att = _matmul(d_att_out, w["w_att_branch"], tb=True, name="d_att", out_dtype=BF16)
    d_w_ab = _matmul(att, d_att_out, ta=True, name="d_w_att_branch", out_dtype=BF16)

    dqh, dkh, dvh = _attn_bwd(qh, kh, vh, _to_heads(d_att), ltot)

    def ln_bwd_fn(u1t, du3t, g_, b_):
        _, vjp = jax.vjp(_ln_silu, u1t, g_, b_)
        du1_, dg_, db_ = vjp(du3t)
        return (du1_,), (dg_, db_)

    du1, d_ln_g, d_ln_b = _rowwise("conv_ln_bwd", ln_bwd_fn, [u1, du3], [w["conv_ln_g"], w["conv_ln_b"]],
                                   [_sds((s, CONV_DIM))], [_sds((1, CONV_DIM)), _sds((1, CONV_DIM))])
    d_conv_in, d_dw_w, d_dw_b = _conv_bwd(conv_in, du1, w["conv_dw_w"])

    d_proj = jnp.concatenate([d_conv_in, _from_heads(dqh).astype(BF16), _from_heads(dkh).astype(BF16),
                              _from_heads(dvh).astype(BF16), d_g_conv, d_g_att], axis=1)
    dh1 = _matmul(d_proj, w_in, tb=True, name="d_h1")
    d_w_in = _matmul(h1, d_proj, ta=True, name="d_w_in", out_dtype=BF16)

    def pre_bwd_fn(xt, dh, dx2t, g_):
        _, vjp = jax.vjp(_rms, xt, g_)
        dx_, dg_ = vjp(dh)
        return (dx_ + dx2t,), (dg_,)

    grad_x, d_g1 = _rowwise("pre_bwd", pre_bwd_fn, [x, dh1, dx2], [g1], [_sds((s, D_MODEL))], [_sds((1, D_MODEL))])

    grads = {
        "norm_mix_pre": d_g1, "w_in": d_w_in, "conv_dw_w": d_dw_w, "conv_dw_b": d_dw_b,
        "conv_ln_g": d_ln_g, "conv_ln_b": d_ln_b, "w_conv_branch": d_w_cb, "b_conv_branch": d_b_cb,
        "w_att_branch": d_w_ab, "w_out": d_w_out, "norm_mix_post": d_g2, "norm_ffn_pre": d_g3,
        "w_ffn_up": jnp.concatenate([d_w_gate, d_w_up], axis=1), "w_ffn_down": d_w_down, "norm_ffn_post": d_g4,
    }
    return loss, grad_x, grads


def _place():
    x, y, c = lax.axis_index("x"), lax.axis_index("y"), lax.axis_index("c")
    return x, y, c


def _slot(px, py, pc):
    return 4 * px + 2 * py + pc


def _all_gather(arrs, name):
    n = len(arrs)

    def body(*refs):
        ins, outs = refs[:n], refs[n:2 * n]
        send_sems, recv_sems, local_sems = refs[2 * n:]
        x, y, c = _place()
        me, sibling = (x, y, c), (x, y, 1 - c)
        chips = [(1 - x, y), (x, 1 - y), (1 - x, 1 - y)]

        def copy(a, kk, block, to, src=None):
            dst = outs[a].at[_slot(*block)]
            return pltpu.make_async_remote_copy(
                src_ref=dst if src is None else src, dst_ref=dst,
                send_sem=send_sems.at[a * 7 + kk], recv_sem=recv_sems.at[a * 7 + kk],
                device_id=to, device_id_type=MESH)

        mine = [pltpu.make_async_copy(ins[a], outs[a].at[_slot(*me)], local_sems.at[a]) for a in range(n)]
        for cp in mine:
            cp.start()
        first = []
        for a in range(n):
            first.append(copy(a, 0, me, sibling, src=ins[a]))
            first += [copy(a, 1 + j, me, (*chip, c), src=ins[a]) for j, chip in enumerate(chips)]
        for cp in first:
            cp.start()
        passed = []
        for j, chip in enumerate(chips):
            for a in range(n):
                copy(a, 1 + j, (*chip, c), me).wait_recv()
                fwd = copy(a, 4 + j, (*chip, c), sibling)
                fwd.start()
                passed.append(fwd)
        for a in range(n):
            copy(a, 0, sibling, me).wait_recv()
            for j, chip in enumerate(chips):
                copy(a, 4 + j, (*chip, 1 - c), me).wait_recv()
        for cp in first + passed:
            cp.wait_send()
        for cp in mine:
            cp.wait()

    return pl.pallas_call(
        body, name=name,
        in_specs=[ANY] * n, out_specs=[ANY] * n,
        out_shape=[_sds((N_DEV,) + a.shape, a.dtype) for a in arrs],
        scratch_shapes=[pltpu.SemaphoreType.DMA((7 * n,)), pltpu.SemaphoreType.DMA((7 * n,)),
                        pltpu.SemaphoreType.DMA((n,))],
    )(*arrs)


def _all_to_all(arrs, name):
    n = len(arrs)
    flips = [(fx, fy, fc) for fx in (0, 1) for fy in (0, 1) for fc in (0, 1)][1:]

    def body(*refs):
        ins, outs = refs[:n], refs[n:2 * n]
        send_sems, recv_sems, local_sems = refs[2 * n:]
        x, y, c = _place()
        mine = _slot(x, y, c)
        local = [pltpu.make_async_copy(ins[a].at[mine], outs[a].at[mine], local_sems.at[a]) for a in range(n)]
        for cp in local:
            cp.start()
        peers = [((1 - x) if fx else x, (1 - y) if fy else y, (1 - c) if fc else c) for fx, fy, fc in flips]

        def copy(a, kk, src_slot, dst_slot):
            return pltpu.make_async_remote_copy(
                src_ref=ins[a].at[src_slot], dst_ref=outs[a].at[dst_slot],
                send_sem=send_sems.at[a * 7 + kk], recv_sem=recv_sems.at[a * 7 + kk],
                device_id=peers[kk], device_id_type=MESH)

        sends = [copy(a, kk, _slot(*peers[kk]), mine) for a in range(n) for kk in range(7)]
        for cp in sends:
            cp.start()
        for a in range(n):
            for kk in range(7):
                copy(a, kk, mine, _slot(*peers[kk])).wait_recv()
        for cp in sends:
            cp.wait_send()
        for cp in local:
            cp.wait()

    return pl.pallas_call(
        body, name=name,
        in_specs=[ANY] * n, out_specs=[ANY] * n,
        out_shape=[_sds(a.shape, a.dtype) for a in arrs],
        scratch_shapes=[pltpu.SemaphoreType.DMA((7 * n,)), pltpu.SemaphoreType.DMA((7 * n,)),
                        pltpu.SemaphoreType.DMA((n,))],
    )(*arrs)


def _adamw_math(w, g, m, v):
    m2 = ADAM_B1 * m + (1.0 - ADAM_B1) * g
    v2 = ADAM_B2 * v + (1.0 - ADAM_B2) * jnp.square(g)
    m_hat = m2 / (1.0 - ADAM_B1 ** ADAM_STEP)
    v_hat = v2 / (1.0 - ADAM_B2 ** ADAM_STEP)
    delta = -ADAM_LR * (m_hat / (jnp.sqrt(v_hat) + ADAM_EPS) + ADAM_WD * w)
    return delta, m2, v2


def _sum_adamw(name, parts, w, m, v, tr=256):
    p, r, c = parts.shape
    tr = _pick(r, tr, 16)

    def body(p_ref, w_ref, m_ref, v_ref, g_ref, d_ref, m2_ref, v2_ref):
        g = p_ref[0].astype(F32)
        for d in range(1, p):
            g = g + p_ref[d].astype(F32)
        delta, m2, v2 = _adamw_math(w_ref[...], g, m_ref[...], v_ref[...])
        g_ref[...] = g
        d_ref[...] = delta
        m2_ref[...] = m2
        v2_ref[...] = v2

    tile = pl.BlockSpec((tr, c), lambda i: (i, 0))
    return pl.pallas_call(
        body, name=name, grid=(r // tr,),
        in_specs=[pl.BlockSpec((p, tr, c), lambda i: (0, i, 0)), tile, tile, tile],
        out_specs=[tile] * 4, out_shape=[_sds((r, c))] * 4,
        compiler_params=_params(("parallel",)),
    )(parts, w, m, v)


def _sum_parts(name, parts):
    p, r, c = parts.shape

    def body(p_ref, o_ref):
        g = p_ref[0]
        for d in range(1, p):
            g = g + p_ref[d]
        o_ref[...] = g

    return pl.pallas_call(
        body, name=name, out_shape=_sds((r, c)),
        in_specs=[pl.BlockSpec(memory_space=pltpu.VMEM)], out_specs=pl.BlockSpec(memory_space=pltpu.VMEM),
    )(parts)


WEIGHTS = ["norm_mix_pre", "w_in", "conv_dw_w", "conv_dw_b", "conv_ln_g", "conv_ln_b", "w_conv_branch",
           "b_conv_branch", "w_att_branch", "w_out", "norm_mix_post", "norm_ffn_pre", "w_ffn_up", "w_ffn_down",
           "norm_ffn_post"]
COL_SHARDED = ["w_in", "w_conv_branch", "w_att_branch", "w_ffn_up"]
ROW_SHARDED = ["w_out", "w_ffn_down"]
VECTORS = ["norm_mix_pre", "conv_dw_b", "conv_ln_g", "conv_ln_b", "b_conv_branch", "norm_mix_post",
           "norm_ffn_pre", "norm_ffn_post"]


def _cols_to_full(g):
    return g.transpose(1, 0, 2).reshape(g.shape[1], N_DEV * g.shape[2])


def _full_to_cols(f):
    return f.reshape(f.shape[0], N_DEV, f.shape[1] // N_DEV).transpose(1, 0, 2)


def _pack_vectors(vecs):
    rows = [jnp.pad(vecs[nm].reshape(-1), (0, D_MODEL - vecs[nm].size)) for nm in VECTORS]
    return jnp.stack(rows)


def _unpack_vectors(packed, sizes):
    return {nm: packed[n, :sizes[nm]] for n, nm in enumerate(VECTORS)}


def kernel(x, norm_mix_pre, w_in, conv_dw_w, conv_dw_b, conv_ln_g, conv_ln_b, w_conv_branch, b_conv_branch, w_att_branch, w_out, norm_mix_post, norm_ffn_pre, w_ffn_up, w_ffn_down, norm_ffn_post, loss_target, m_norm_mix_pre, m_w_in, m_conv_dw_w, m_conv_dw_b, m_conv_ln_g, m_conv_ln_b, m_w_conv_branch, m_b_conv_branch, m_w_att_branch, m_w_out, m_norm_mix_post, m_norm_ffn_pre, m_w_ffn_up, m_w_ffn_down, m_norm_ffn_post, v_norm_mix_pre, v_w_in, v_conv_dw_w, v_conv_dw_b, v_conv_ln_g, v_conv_ln_b, v_w_conv_branch, v_b_conv_branch, v_w_att_branch, v_w_out, v_norm_mix_post, v_norm_ffn_pre, v_w_ffn_up, v_w_ffn_down, v_norm_ffn_post):
    ws = dict(zip(WEIGHTS, [norm_mix_pre, w_in, conv_dw_w, conv_dw_b, conv_ln_g, conv_ln_b, w_conv_branch,
                            b_conv_branch, w_att_branch, w_out, norm_mix_post, norm_ffn_pre, w_ffn_up, w_ffn_down,
                            norm_ffn_post]))
    ms = dict(zip(WEIGHTS, [m_norm_mix_pre, m_w_in, m_conv_dw_w, m_conv_dw_b, m_conv_ln_g, m_conv_ln_b,
                            m_w_conv_branch, m_b_conv_branch, m_w_att_branch, m_w_out, m_norm_mix_post,
                            m_norm_ffn_pre, m_w_ffn_up, m_w_ffn_down, m_norm_ffn_post]))
    vs = dict(zip(WEIGHTS, [v_norm_mix_pre, v_w_in, v_conv_dw_w, v_conv_dw_b, v_conv_ln_g, v_conv_ln_b,
                            v_w_conv_branch, v_b_conv_branch, v_w_att_branch, v_w_out, v_norm_mix_post,
                            v_norm_ffn_pre, v_w_ffn_up, v_w_ffn_down, v_norm_ffn_post]))
    big = COL_SHARDED + ROW_SHARDED

    dw_block = jnp.pad(conv_dw_w, ((0, 1), (0, 0)))
    gathered = _all_gather([ws[nm].astype(BF16) for nm in big] + [dw_block], "gather_weights")
    full = {nm: (_cols_to_full(g) if nm in COL_SHARDED else g.reshape(-1, g.shape[2]))
            for nm, g in zip(big, gathered[:-1])}
    full["conv_dw_w"] = _cols_to_full(gathered[-1])
    for nm in VECTORS:
        full[nm] = ws[nm].reshape(1, -1)

    loss_local, grad_x, grads = _local_step(x[0], loss_target[0], full)
    loss = lax.psum(loss_local, ("x", "y", "c"))

    slabs = [_full_to_cols(grads[nm]) if nm in COL_SHARDED else grads[nm].reshape(N_DEV, -1, grads[nm].shape[1])
             for nm in big]
    received = _all_to_all(slabs, "scatter_grads")
    small = _all_gather([_pack_vectors(grads), grads["conv_dw_w"]], "gather_small_grads")

    out_g, out_d, out_m, out_v = {}, {}, {}, {}
    for nm, parts in zip(big, received):
        out_g[nm], out_d[nm], out_m[nm], out_v[nm] = _sum_adamw("adamw_" + nm, parts, ws[nm], ms[nm], vs[nm])
    sizes = {nm: ws[nm].size for nm in VECTORS}
    vec = _sum_adamw("adamw_vectors", small[0], _pack_vectors(ws), _pack_vectors(ms), _pack_vectors(vs))
    for res, dst in zip(vec, (out_g, out_d, out_m, out_v)):
        dst.update(_unpack_vectors(res, sizes))
    dw_full = _sum_parts("sum_dw_grads", small[1])
    me = _slot(*_place())
    dw_mine = lax.dynamic_slice(dw_full, (0, me * (CONV_DIM // N_DEV)), (CONV_WIDTH, CONV_DIM // N_DEV))
    nm = "conv_dw_w"
    out_g[nm], out_d[nm], out_m[nm], out_v[nm] = _sum_adamw("adamw_dw", dw_mine[None], ws[nm], ms[nm], vs[nm])

    outs = [loss, grad_x[None]]
    for group in (out_g, out_d, out_m, out_v):
        outs += [group[nm] for nm in WEIGHTS]
    return tuple(outs)
```

```python
import functools
import math

import jax
import jax.numpy as jnp
from jax import lax
from jax.experimental import pallas as pl
from jax.experimental.pallas import tpu as pltpu

F32 = jnp.float32
BF16 = jnp.bfloat16

N_DEV = 8
D_MODEL = 1024
CONV_DIM = 512
CONV_WIDTH = 31
N_HEADS = 8
HEAD_DIM = 64
ATT_DIM = N_HEADS * HEAD_DIM
D_FF = 2816
EPS = 1e-6
IN_SPLITS = (0, 1024, 1536, 2048, 2560, 3584, 4608)

ADAM_LR = 0.001
ADAM_B1 = 0.9
ADAM_B2 = 0.999
ADAM_EPS = 1e-08
ADAM_WD = 0.01
ADAM_STEP = 10

LANES = 128
SUBLANES = 8
HALO = 32
ATT_TILE = 256
VMEM_LIMIT = 56 * 1024 * 1024
MESH = pl.DeviceIdType.MESH
ANY = pl.BlockSpec(memory_space=pl.ANY)


def _pick(dim, target, align=LANES):
    t = min(dim, target)
    t -= t % align
    while t >= align:
        if dim % t == 0:
            return t
        t -= align
    return dim


def _params(semantics):
    return pltpu.CompilerParams(dimension_semantics=semantics, vmem_limit_bytes=VMEM_LIMIT)


def _matmul(a, b, *, name, ta=False, tb=False, out_dtype=F32):
    m, k = (a.shape[1], a.shape[0]) if ta else a.shape
    n, k2 = b.shape if tb else (b.shape[1], b.shape[0])
    assert k == k2, (a.shape, b.shape, ta, tb)
    tm, tn, tk = _pick(m, 1408 if ta else 512), _pick(n, 1536), _pick(k, 1536)
    nk = k // tk
    dims = (((0 if ta else 1,), (1 if tb else 0,)), ((), ()))

    def body(a_ref, b_ref, o_ref, *acc):
        part = lax.dot_general(a_ref[...], b_ref[...], dims, preferred_element_type=F32)
        if nk == 1:
            o_ref[...] = part.astype(o_ref.dtype)
            return
        acc_ref, = acc
        kk = pl.program_id(2)

        @pl.when(kk == 0)
        def _():
            acc_ref[...] = part

        @pl.when((kk > 0) & (kk < nk - 1))
        def _():
            acc_ref[...] += part

        @pl.when(kk == nk - 1)
        def _():
            o_ref[...] = (acc_ref[...] + part).astype(o_ref.dtype)

    a_spec = pl.BlockSpec((tk, tm), lambda i, j, kk: (kk, i)) if ta else pl.BlockSpec((tm, tk), lambda i, j, kk: (i, kk))
    b_spec = pl.BlockSpec((tn, tk), lambda i, j, kk: (j, kk)) if tb else pl.BlockSpec((tk, tn), lambda i, j, kk: (kk, j))
    return pl.pallas_call(
        body, name=name, grid=(m // tm, n // tn, nk),
        in_specs=[a_spec, b_spec],
        out_specs=pl.BlockSpec((tm, tn), lambda i, j, kk: (i, j)),
        out_shape=jax.ShapeDtypeStruct((m, n), out_dtype),
        scratch_shapes=[pltpu.VMEM((tm, tn), F32)] if nk > 1 else [],
        compiler_params=_params(("parallel", "parallel", "arbitrary")),
    )(a, b)


def _rowwise(name, fn, rows, bcasts, row_outs, red_outs=(), tm=256):
    s = rows[0].shape[0]
    tm = _pick(s, tm, 16)
    nr, nb, no = len(rows), len(bcasts), len(row_outs)

    def body(*refs):
        ins = [r[...] for r in refs[:nr + nb]]
        outs, reds = fn(*ins)
        for ref, val in zip(refs[nr + nb:nr + nb + no], outs):
            ref[...] = val.astype(ref.dtype)
        i = pl.program_id(0)
        for ref, val in zip(refs[nr + nb + no:], reds):
            @pl.when(i == 0)
            def _():
                ref[...] = val

            @pl.when(i > 0)
            def _():
                ref[...] += val

    in_specs = [pl.BlockSpec((tm, r.shape[1]), lambda i: (i, 0)) for r in rows]
    in_specs += [pl.BlockSpec(b.shape, lambda i: (0, 0)) for b in bcasts]
    out_specs = [pl.BlockSpec((tm, o.shape[1]), lambda i: (i, 0)) for o in row_outs]
    out_specs += [pl.BlockSpec(d.shape, lambda i: (0, 0)) for d in red_outs]
    return pl.pallas_call(
        body, name=name, grid=(s // tm,), in_specs=in_specs, out_specs=out_specs,
        out_shape=list(row_outs) + list(red_outs),
        compiler_params=_params(("arbitrary",)),
    )(*rows, *bcasts)


def _sds(shape, dtype=F32):
    return jax.ShapeDtypeStruct(shape, dtype)


def _rms(x, g):
    y = x * lax.rsqrt(jnp.mean(x * x, axis=-1, keepdims=True) + EPS)
    return y * g


def _silu(x):
    return x * jax.nn.sigmoid(x)


def _ln_silu(u, g, b):
    mu = jnp.mean(u, axis=-1, keepdims=True)
    var = jnp.mean(jnp.square(u - mu), axis=-1, keepdims=True)
    return _silu((u - mu) * lax.rsqrt(var + EPS) * g + b)


def _merge(conv_pre, att_out, g_conv, g_att, b_cb):
    return jax.nn.sigmoid(g_conv) * (conv_pre + b_cb) + jax.nn.sigmoid(g_att) * att_out


def _glu(t):
    return t[:, :CONV_DIM] * jax.nn.sigmoid(t[:, CONV_DIM:])


def _conv_fwd(conv_in, w_pad, b, ln_g, ln_b, tm=256):
    s = conv_in.shape[0]
    tm = _pick(s, tm, HALO)
    ratio = tm // HALO

    def body(main_ref, halo_ref, w_ref, b_ref, g_ref, be_ref, u3_ref, u1_ref, buf):
        i = pl.program_id(0)
        buf[0:HALO, :] = _glu(halo_ref[...]) * (i > 0).astype(F32)
        buf[HALO:HALO + tm, :] = _glu(main_ref[...])
        acc = jnp.zeros((tm, CONV_DIM), F32) + b_ref[...]
        for j in range(CONV_WIDTH):
            acc = acc + w_ref[j:j + 1, :] * buf[pl.ds(HALO - (CONV_WIDTH - 1) + j, tm), :]
        u1_ref[...] = acc
        u3_ref[...] = _ln_silu(acc, g_ref[...], be_ref[...]).astype(u3_ref.dtype)

    return pl.pallas_call(
        body, name="conv_fwd", grid=(s // tm,),
        in_specs=[pl.BlockSpec((tm, 2 * CONV_DIM), lambda i: (i, 0)),
                  pl.BlockSpec((HALO, 2 * CONV_DIM), lambda i: (jnp.maximum(i * ratio - 1, 0), 0)),
                  pl.BlockSpec(w_pad.shape, lambda i: (0, 0)),
                  pl.BlockSpec(b.shape, lambda i: (0, 0)),
                  pl.BlockSpec(ln_g.shape, lambda i: (0, 0)),
                  pl.BlockSpec(ln_b.shape, lambda i: (0, 0))],
        out_specs=[pl.BlockSpec((tm, CONV_DIM), lambda i: (i, 0)),
                   pl.BlockSpec((tm, CONV_DIM), lambda i: (i, 0))],
        out_shape=[_sds((s, CONV_DIM), BF16), _sds((s, CONV_DIM), F32)],
        scratch_shapes=[pltpu.VMEM((tm + HALO, CONV_DIM), F32)],
        compiler_params=_params(("arbitrary",)),
    )(conv_in, conv_in, w_pad, b, ln_g, ln_b)


def _conv_bwd(conv_in, du1, w_pad, tm=256):
    s = conv_in.shape[0]
    tm = _pick(s, tm, HALO)
    ratio = tm // HALO
    nt = s // tm
    last_halo = s // HALO - 1

    def body(main_ref, halo_ref, du_ref, dun_ref, w_ref, dci_ref, dw_ref, db_ref, ubuf, dbuf):
        i = pl.program_id(0)
        main = main_ref[...]
        a = main[:, :CONV_DIM]
        sb = jax.nn.sigmoid(main[:, CONV_DIM:])
        ubuf[0:HALO, :] = _glu(halo_ref[...]) * (i > 0).astype(F32)
        ubuf[HALO:HALO + tm, :] = a * sb
        du = du_ref[...]
        dbuf[0:tm, :] = du
        dbuf[tm:tm + HALO, :] = dun_ref[...] * (i < nt - 1).astype(F32)

        @pl.when(i == 0)
        def _():
            dw_ref[...] = jnp.zeros_like(dw_ref)
            db_ref[...] = jnp.zeros_like(db_ref)

        du0 = jnp.zeros((tm, CONV_DIM), F32)
        for j in range(CONV_WIDTH):
            du0 = du0 + w_ref[j:j + 1, :] * dbuf[pl.ds(CONV_WIDTH - 1 - j, tm), :]
            dw_ref[j:j + 1, :] += jnp.sum(du * ubuf[pl.ds(HALO - (CONV_WIDTH - 1) + j, tm), :], axis=0, keepdims=True)
        db_ref[...] += jnp.sum(du, axis=0, keepdims=True)
        dci_ref[:, :CONV_DIM] = (du0 * sb).astype(dci_ref.dtype)
        dci_ref[:, CONV_DIM:] = (du0 * a * sb * (1.0 - sb)).astype(dci_ref.dtype)

    return pl.pallas_call(
        body, name="conv_bwd", grid=(nt,),
        in_specs=[pl.BlockSpec((tm, 2 * CONV_DIM), lambda i: (i, 0)),
                  pl.BlockSpec((HALO, 2 * CONV_DIM), lambda i: (jnp.maximum(i * ratio - 1, 0), 0)),
                  pl.BlockSpec((tm, CONV_DIM), lambda i: (i, 0)),
                  pl.BlockSpec((HALO, CONV_DIM), lambda i: (jnp.minimum((i + 1) * ratio, last_halo), 0)),
                  pl.BlockSpec(w_pad.shape, lambda i: (0, 0))],
        out_specs=[pl.BlockSpec((tm, 2 * CONV_DIM), lambda i: (i, 0)),
                   pl.BlockSpec(w_pad.shape, lambda i: (0, 0)),
                   pl.BlockSpec((1, CONV_DIM), lambda i: (0, 0))],
        out_shape=[_sds((s, 2 * CONV_DIM), BF16), _sds(w_pad.shape), _sds((1, CONV_DIM))],
        scratch_shapes=[pltpu.VMEM((tm + HALO, CONV_DIM), F32), pltpu.VMEM((tm + HALO, CONV_DIM), F32)],
        compiler_params=_params(("arbitrary",)),
    )(conv_in, conv_in, du1, du1, w_pad)


def _logsig_neg(z):
    return jnp.minimum(-z, 0.0) - jnp.log(1.0 + jnp.exp(-jnp.abs(z)))


def _split_dot(val, tri):
    hi = val.astype(BF16)
    lo = (val - hi.astype(F32)).astype(BF16)
    return jnp.dot(hi, tri, preferred_element_type=F32) + jnp.dot(lo, tri, preferred_element_type=F32)


def _attn_masks(t, later):
    row = lax.broadcasted_iota(jnp.int32, (t, t), 0)
    col = lax.broadcasted_iota(jnp.int32, (t, t), 1)
    tri = jnp.where(row > col if later else row <= col, 1.0, 0.0).astype(BF16)
    return col < row, tri


def _attn_fwd(q, k, v):
    h, s, dh = q.shape
    t = ATT_TILE
    scale = 1.0 / math.sqrt(dh)

    def body(q_ref, k_ref, v_ref, o_ref, lt_ref):
        i = pl.program_id(1)
        qs = (q_ref[...].astype(F32) * scale).astype(BF16)
        causal, tri = _attn_masks(t, later=True)

        def step(kb, carry, masked):
            c, acc = carry
            off = pl.multiple_of(kb * t, t)
            kblk = k_ref[pl.ds(off, t), :]
            vblk = v_ref[pl.ds(off, t), :]
            z = lax.dot_general(qs, kblk, (((1,), (1,)), ((), ())), preferred_element_type=F32)
            l = _logsig_neg(z)
            if masked:
                l = jnp.where(causal, l, 0.0)
            e = z + l + _split_dot(l, tri) + c
            if masked:
                e = jnp.where(causal, e, -1e30)
            a = jnp.exp(e)
            acc = acc + jnp.dot(a.astype(BF16), vblk, preferred_element_type=F32)
            return c + jnp.sum(l, axis=1, keepdims=True), acc

        carry = step(i, (jnp.zeros((t, 1), F32), jnp.zeros((t, dh), F32)), True)
        carry = lax.fori_loop(0, i, lambda n, cr: step(i - 1 - n, cr, False), carry)
        lt_ref[...] = jnp.broadcast_to(carry[0], (t, LANES))
        o_ref[...] = carry[1].astype(o_ref.dtype)

    return pl.pallas_call(
        body, name="attn_fwd", grid=(h, s // t),
        in_specs=[pl.BlockSpec((None, t, dh), lambda hh, i: (hh, i, 0)),
                  pl.BlockSpec((None, s, dh), lambda hh, i: (hh, 0, 0)),
                  pl.BlockSpec((None, s, dh), lambda hh, i: (hh, 0, 0))],
        out_specs=[pl.BlockSpec((None, t, dh), lambda hh, i: (hh, i, 0)),
                   pl.BlockSpec((None, t, LANES), lambda hh, i: (hh, i, 0))],
        out_shape=[_sds((h, s, dh), BF16), _sds((h, s, LANES), F32)],
        compiler_params=_params(("parallel", "arbitrary")),
    )(q, k, v)


def _attn_bwd(q, k, v, do, ltot):
    h, s, dh = q.shape
    t = ATT_TILE
    scale = 1.0 / math.sqrt(dh)

    def body(q_ref, k_ref, v_ref, do_ref, lt_ref, dq_ref, dk_ref, dv_ref):
        i = pl.program_id(1)

        @pl.when(i == 0)
        def _():
            dk_ref[...] = jnp.zeros_like(dk_ref)
            dv_ref[...] = jnp.zeros_like(dv_ref)

        qb = q_ref[...]
        qs = (qb.astype(F32) * scale).astype(BF16)
        dob = do_ref[...]
        lt = lt_ref[:, 0:1]
        causal, tri = _attn_masks(t, later=False)

        def step(kb, carry, masked):
            cl, cg, dq = carry
            off = pl.multiple_of(kb * t, t)
            kblk = k_ref[pl.ds(off, t), :]
            vblk = v_ref[pl.ds(off, t), :]
            z = lax.dot_general(qs, kblk, (((1,), (1,)), ((), ())), preferred_element_type=F32)
            l = _logsig_neg(z)
            if masked:
                l = jnp.where(causal, l, 0.0)
            e = z + l + ((lt - cl) - _split_dot(l, tri))
            if masked:
                e = jnp.where(causal, e, -1e30)
            a = jnp.exp(e)
            da = lax.dot_general(dob, vblk, (((1,), (1,)), ((), ())), preferred_element_type=F32)
            g = da * a
            p = cg + jnp.dot(g.astype(BF16), tri, preferred_element_type=F32) - g
            el = jnp.exp(l)
            dz = g * el - p * (1.0 - el)
            if masked:
                dz = jnp.where(causal, dz, 0.0)
            dzb = (dz * scale).astype(BF16)
            dq = dq + jnp.dot(dzb, kblk, preferred_element_type=F32)
            dk_ref[pl.ds(off, t), :] += lax.dot_general(dzb, qb, (((0,), (0,)), ((), ())), preferred_element_type=F32)
            dv_ref[pl.ds(off, t), :] += lax.dot_general(a.astype(BF16), dob, (((0,), (0,)), ((), ())), preferred_element_type=F32)
            return cl + jnp.sum(l, axis=1, keepdims=True), cg + jnp.sum(g, axis=1, keepdims=True), dq

        init = (jnp.zeros((t, 1), F32), jnp.zeros((t, 1), F32), jnp.zeros((t, dh), F32))
        carry = lax.fori_loop(0, i, lambda kb, cr: step(kb, cr, False), init)
        carry = step(i, carry, True)
        dq_ref[...] = carry[2]

    return pl.pallas_call(
        body, name="attn_bwd", grid=(h, s // t),
        in_specs=[pl.BlockSpec((None, t, dh), lambda hh, i: (hh, i, 0)),
                  pl.BlockSpec((None, s, dh), lambda hh, i: (hh, 0, 0)),
                  pl.BlockSpec((None, s, dh), lambda hh, i: (hh, 0, 0)),
                  pl.BlockSpec((None, t, dh), lambda hh, i: (hh, i, 0)),
                  pl.BlockSpec((None, t, LANES), lambda hh, i: (hh, i, 0))],
        out_specs=[pl.BlockSpec((None, t, dh), lambda hh, i: (hh, i, 0)),
                   pl.BlockSpec((None, s, dh), lambda hh, i: (hh, 0, 0)),
                   pl.BlockSpec((None, s, dh), lambda hh, i: (hh, 0, 0))],
        out_shape=[_sds((h, s, dh)), _sds((h, s, dh)), _sds((h, s, dh))],
        compiler_params=_params(("parallel", "arbitrary")),
    )(q, k, v, do, ltot)


def _to_heads(a):
    return a.reshape(a.shape[0], N_HEADS, HEAD_DIM).transpose(1, 0, 2)


def _from_heads(a):
    return a.transpose(1, 0, 2).reshape(a.shape[1], ATT_DIM)


def _local_step(x, target, w):
    s = x.shape[0]
    g1, g2, g3, g4 = w["norm_mix_pre"], w["norm_mix_post"], w["norm_ffn_pre"], w["norm_ffn_post"]

    (h1,) = _rowwise("pre_norm", lambda xt, g: ((_rms(xt, g),), ()), [x], [g1], [_sds((s, D_MODEL), BF16)])
    w_in = w["w_in"]
    cols = [w_in[:, IN_SPLITS[n]:IN_SPLITS[n + 1]] for n in range(6)]
    conv_in = _matmul(h1, cols[0], name="proj_conv")
    q = _matmul(h1, cols[1], name="proj_q", out_dtype=BF16)
    k = _matmul(h1, cols[2], name="proj_k", out_dtype=BF16)
    v = _matmul(h1, cols[3], name="proj_v", out_dtype=BF16)
    g_conv = _matmul(h1, cols[4], name="proj_gate_conv")
    g_att = _matmul(h1, cols[5], name="proj_gate_att")

    u3, u1 = _conv_fwd(conv_in, w["conv_dw_w"], w["conv_dw_b"], w["conv_ln_g"], w["conv_ln_b"])
    qh, kh, vh = _to_heads(q), _to_heads(k), _to_heads(v)
    att_h, ltot = _attn_fwd(qh, kh, vh)
    att = _from_heads(att_h)

    conv_pre = _matmul(u3, w["w_conv_branch"], name="conv_branch")
    att_out = _matmul(att, w["w_att_branch"], name="att_branch")
    (merged,) = _rowwise("merge", lambda *a: ((_merge(*a),), ()), [conv_pre, att_out, g_conv, g_att],
                         [w["b_conv_branch"]], [_sds((s, D_MODEL), BF16)])
    mix = _matmul(merged, w["w_out"], name="mix_out")

    def mid_fn(xt, mt, g2_, g3_):
        x2_ = xt + _rms(mt, g2_)
        return (x2_, _rms(x2_, g3_)), ()

    x2, h2 = _rowwise("mid_norm", mid_fn, [x, mix], [g2, g3], [_sds((s, D_MODEL)), _sds((s, D_MODEL), BF16)])
    w_up = w["w_ffn_up"]
    gate = _matmul(h2, w_up[:, :D_FF], name="ffn_gate")
    up = _matmul(h2, w_up[:, D_FF:], name="ffn_up")
    (act,) = _rowwise("swiglu", lambda g_, u_: ((_silu(g_) * u_,), ()), [gate, up], [], [_sds((s, D_FF), BF16)], tm=128)
    ff = _matmul(act, w["w_ffn_down"], name="ffn_down")

    def final_fn(x2t, fft, tgt, g4_):
        n4, vjp = jax.vjp(_rms, fft, g4_)
        err = x2t + n4 - tgt
        dy = err * (1.0 / D_MODEL)
        dff, dg4 = vjp(dy)
        return (dy, dff), (jnp.sum(err * err, axis=0, keepdims=True), dg4)

    dy, dff, loss_cols, d_g4 = _rowwise("final", final_fn, [x2, ff, target], [g4],
                                        [_sds((s, D_MODEL)), _sds((s, D_MODEL), BF16)],
                                        [_sds((1, D_MODEL)), _sds((1, D_MODEL))])
    loss = 0.5 * jnp.sum(loss_cols) / D_MODEL

    d_act = _matmul(dff, w["w_ffn_down"], tb=True, name="d_act")
    d_w_down = _matmul(act, dff, ta=True, name="d_w_down", out_dtype=BF16)

    def swiglu_bwd_fn(g_, u_, da_):
        _, vjp = jax.vjp(lambda a, b: _silu(a) * b, g_, u_)
        return vjp(da_), ()

    d_gate, d_up = _rowwise("swiglu_bwd", swiglu_bwd_fn, [gate, up, d_act], [],
                            [_sds((s, D_FF), BF16), _sds((s, D_FF), BF16)], tm=128)
    dh2a = _matmul(d_gate, w_up[:, :D_FF], tb=True, name="d_h2_gate")
    dh2b = _matmul(d_up, w_up[:, D_FF:], tb=True, name="d_h2_up")
    d_w_gate = _matmul(h2, d_gate, ta=True, name="d_w_gate", out_dtype=BF16)
    d_w_up = _matmul(h2, d_up, ta=True, name="d_w_up", out_dtype=BF16)

    def mid_bwd_fn(xt, mt, dyt, da, db, g2_, g3_):
        n2, vjp2 = jax.vjp(_rms, mt, g2_)
        x2_ = xt + n2
        _, vjp3 = jax.vjp(_rms, x2_, g3_)
        dx2_, dg3 = vjp3(da + db)
        dx2_ = dx2_ + dyt
        dmix_, dg2 = vjp2(dx2_)
        return (dx2_, dmix_), (dg2, dg3)

    dx2, dmix, d_g2, d_g3 = _rowwise("mid_bwd", mid_bwd_fn, [x, mix, dy, dh2a, dh2b], [g2, g3],
                                     [_sds((s, D_MODEL)), _sds((s, D_MODEL), BF16)],
                                     [_sds((1, D_MODEL)), _sds((1, D_MODEL))])
    d_merged = _matmul(dmix, w["w_out"], tb=True, name="d_merged")
    d_w_out = _matmul(merged, dmix, ta=True, name="d_w_out", out_dtype=BF16)

    def merge_bwd_fn(cp, ao, gc, ga, dm, b_cb):
        _, vjp = jax.vjp(_merge, cp, ao, gc, ga, b_cb)
        dcp, dao, dgc, dga, dbias = vjp(dm)
        return (dcp, dao, dgc, dga), (dbias,)

    d_conv_out, d_att_out, d_g_conv, d_g_att, d_b_cb = _rowwise(
        "merge_bwd", merge_bwd_fn, [conv_pre, att_out, g_conv, g_att, d_merged], [w["b_conv_branch"]],
        [_sds((s, D_MODEL), BF16)] * 4, [_sds((1, D_MODEL))])

    du3 = _matmul(d_conv_out, w["w_conv_branch"], tb=True, name="d_u3")
    d_w_cb = _matmul(u3, d_conv_out, ta=True, name="d_w_conv_branch", out_dtype=BF16)
    d_att = _matmul(d_att_out, w["w_att_branch"], tb=True, name="d_att", out_dtype=BF16)
    d_w_ab = _matmul(att, d_att_out, ta=True, name="d_w_att_branch", out_dtype=BF16)

    dqh, dkh, dvh = _attn_bwd(qh, kh, vh, _to_heads(d_att), ltot)

    def ln_bwd_fn(u1t, du3t, g_, b_):
        _, vjp = jax.vjp(_ln_silu, u1t, g_, b_)
        du1_, dg_, db_ = vjp(du3t)
        return (du1_,), (dg_, db_)

    du1, d_ln_g, d_ln_b = _rowwise("conv_ln_bwd", ln_bwd_fn, [u1, du3], [w["conv_ln_g"], w["conv_ln_b"]],
                                   [_sds((s, CONV_DIM))], [_sds((1, CONV_DIM)), _sds((1, CONV_DIM))])
    d_conv_in, d_dw_w, d_dw_b = _conv_bwd(conv_in, du1, w["conv_dw_w"])

    d_proj = jnp.concatenate([d_conv_in, _from_heads(dqh).astype(BF16), _from_heads(dkh).astype(BF16),
                              _from_heads(dvh).astype(BF16), d_g_conv, d_g_att], axis=1)
    dh1 = _matmul(d_proj, w_in, tb=True, name="d_h1")
    d_w_in = _matmul(h1, d_proj, ta=True, name="d_w_in", out_dtype=BF16)

    def pre_bwd_fn(xt, dh, dx2t, g_):
        _, vjp = jax.vjp(_rms, xt, g_)
        dx_, dg_ = vjp(dh)
        return (dx_ + dx2t,), (dg_,)

    grad_x, d_g1 = _rowwise("pre_bwd", pre_bwd_fn, [x, dh1, dx2], [g1], [_sds((s, D_MODEL))], [_sds((1, D_MODEL))])

    grads = {
        "norm_mix_pre": d_g1, "w_in": d_w_in, "conv_dw_w": d_dw_w, "conv_dw_b": d_dw_b,
        "conv_ln_g": d_ln_g, "conv_ln_b": d_ln_b, "w_conv_branch": d_w_cb, "b_conv_branch": d_b_cb,
        "w_att_branch": d_w_ab, "w_out": d_w_out, "norm_mix_post": d_g2, "norm_ffn_pre": d_g3,
        "w_ffn_up": jnp.concatenate([d_w_gate, d_w_up], axis=1), "w_ffn_down": d_w_down, "norm_ffn_post": d_g4,
    }
    return loss, grad_x, grads


def _place():
    x, y, c = lax.axis_index("x"), lax.axis_index("y"), lax.axis_index("c")
    return x, y, c


def _slot(px, py, pc):
    return 4 * px + 2 * py + pc


def _all_gather(arrs, name):
    n = len(arrs)

    def body(*refs):
        ins, outs = refs[:n], refs[n:2 * n]
        send_sems, recv_sems, local_sems = refs[2 * n:]
        x, y, c = _place()
        me, sibling = (x, y, c), (x, y, 1 - c)
        chips = [(1 - x, y), (x, 1 - y), (1 - x, 1 - y)]

        def copy(a, kk, block, to, src=None):
            dst = outs[a].at[_slot(*block)]
            return pltpu.make_async_remote_copy(
                src_ref=dst if src is None else src, dst_ref=dst,
                send_sem=send_sems.at[a * 7 + kk], recv_sem=recv_sems.at[a * 7 + kk],
                device_id=to, device_id_type=MESH)

        mine = [pltpu.make_async_copy(ins[a], outs[a].at[_slot(*me)], local_sems.at[a]) for a in range(n)]
        for cp in mine:
            cp.start()
        first = []
        for a in range(n):
            first.append(copy(a, 0, me, sibling, src=ins[a]))
            first += [copy(a, 1 + j, me, (*chip, c), src=ins[a]) for j, chip in enumerate(chips)]
        for cp in first:
            cp.start()
        passed = []
        for j, chip in enumerate(chips):
            for a in range(n):
                copy(a, 1 + j, (*chip, c), me).wait_recv()
                fwd = copy(a, 4 + j, (*chip, c), sibling)
                fwd.start()
                passed.append(fwd)
        for a in range(n):
            copy(a, 0, sibling, me).wait_recv()
            for j, chip in enumerate(chips):
                copy(a, 4 + j, (*chip, 1 - c), me).wait_recv()
        for cp in first + passed:
            cp.wait_send()
        for cp in mine:
            cp.wait()

    return pl.pallas_call(
        body, name=name,
        in_specs=[ANY] * n, out_specs=[ANY] * n,
        out_shape=[_sds((N_DEV,) + a.shape, a.dtype) for a in arrs],
        scratch_shapes=[pltpu.SemaphoreType.DMA((7 * n,)), pltpu.SemaphoreType.DMA((7 * n,)),
                        pltpu.SemaphoreType.DMA((n,))],
    )(*arrs)


def _all_to_all(arrs, name):
    n = len(arrs)
    flips = [(fx, fy, fc) for fx in (0, 1) for fy in (0, 1) for fc in (0, 1)][1:]

    def body(*refs):
        ins, outs = refs[:n], refs[n:2 * n]
        send_sems, recv_sems, local_sems = refs[2 * n:]
        x, y, c = _place()
        mine = _slot(x, y, c)
        local = [pltpu.make_async_copy(ins[a].at[mine], outs[a].at[mine], local_sems.at[a]) for a in range(n)]
        for cp in local:
            cp.start()
        peers = [((1 - x) if fx else x, (1 - y) if fy else y, (1 - c) if fc else c) for fx, fy, fc in flips]

        def copy(a, kk, src_slot, dst_slot):
            return pltpu.make_async_remote_copy(
                src_ref=ins[a].at[src_slot], dst_ref=outs[a].at[dst_slot],
                send_sem=send_sems.at[a * 7 + kk], recv_sem=recv_sems.at[a * 7 + kk],
                device_id=peers[kk], device_id_type=MESH)

        sends = [copy(a, kk, _slot(*peers[kk]), mine) for a in range(n) for kk in range(7)]
        for cp in sends:
            cp.start()
        for a in range(n):
            for kk in range(7):
                copy(a, kk, mine, _slot(*peers[kk])).wait_recv()
        for cp in sends:
            cp.wait_send()
        for cp in local:
            cp.wait()

    return pl.pallas_call(
        body, name=name,
        in_specs=[ANY] * n, out_specs=[ANY] * n,
        out_shape=[_sds(a.shape, a.dtype) for a in arrs],
        scratch_shapes=[pltpu.SemaphoreType.DMA((7 * n,)), pltpu.SemaphoreType.DMA((7 * n,)),
                        pltpu.SemaphoreType.DMA((n,))],
    )(*arrs)


def _adamw_math(w, g, m, v):
    m2 = ADAM_B1 * m + (1.0 - ADAM_B1) * g
    v2 = ADAM_B2 * v + (1.0 - ADAM_B2) * jnp.square(g)
    m_hat = m2 / (1.0 - ADAM_B1 ** ADAM_STEP)
    v_hat = v2 / (1.0 - ADAM_B2 ** ADAM_STEP)
    delta = -ADAM_LR * (m_hat / (jnp.sqrt(v_hat) + ADAM_EPS) + ADAM_WD * w)
    return delta, m2, v2


def _sum_adamw(name, parts, w, m, v, tr=256):
    p, r, c = parts.shape
    tr = _pick(r, tr, 16)

    def body(p_ref, w_ref, m_ref, v_ref, g_ref, d_ref, m2_ref, v2_ref):
        g = p_ref[0].astype(F32)
        for d in range(1, p):
            g = g + p_ref[d].astype(F32)
        delta, m2, v2 = _adamw_math(w_ref[...], g, m_ref[...], v_ref[...])
        g_ref[...] = g
        d_ref[...] = delta
        m2_ref[...] = m2
        v2_ref[...] = v2

    tile = pl.BlockSpec((tr, c), lambda i: (i, 0))
    return pl.pallas_call(
        body, name=name, grid=(r // tr,),
        in_specs=[pl.BlockSpec((p, tr, c), lambda i: (0, i, 0)), tile, tile, tile],
        out_specs=[tile] * 4, out_shape=[_sds((r, c))] * 4,
        compiler_params=_params(("parallel",)),
    )(parts, w, m, v)


def _sum_parts(name, parts):
    p, r, c = parts.shape

    def body(p_ref, o_ref):
        g = p_ref[0]
        for d in range(1, p):
            g = g + p_ref[d]
        o_ref[...] = g

    return pl.pallas_call(
        body, name=name, out_shape=_sds((r, c)),
        in_specs=[pl.BlockSpec(memory_space=pltpu.VMEM)], out_specs=pl.BlockSpec(memory_space=pltpu.VMEM),
    )(parts)


WEIGHTS = ["norm_mix_pre", "w_in", "conv_dw_w", "conv_dw_b", "conv_ln_g", "conv_ln_b", "w_conv_branch",
           "b_conv_branch", "w_att_branch", "w_out", "norm_mix_post", "norm_ffn_pre", "w_ffn_up", "w_ffn_down",
           "norm_ffn_post"]
COL_SHARDED = ["w_in", "w_conv_branch", "w_att_branch", "w_ffn_up"]
ROW_SHARDED = ["w_out", "w_ffn_down"]
VECTORS = ["norm_mix_pre", "conv_dw_b", "conv_ln_g", "conv_ln_b", "b_conv_branch", "norm_mix_post",
           "norm_ffn_pre", "norm_ffn_post"]


def _cols_to_full(g):
    return g.transpose(1, 0, 2).reshape(g.shape[1], N_DEV * g.shape[2])


def _full_to_cols(f):
    return f.reshape(f.shape[0], N_DEV, f.shape[1] // N_DEV).transpose(1, 0, 2)


def _pack_vectors(vecs):
    rows = [jnp.pad(vecs[nm].reshape(-1), (0, D_MODEL - vecs[nm].size)) for nm in VECTORS]
    return jnp.stack(rows)


def _unpack_vectors(packed, sizes):
    return {nm: packed[n, :sizes[nm]] for n, nm in enumerate(VECTORS)}


def kernel(x, norm_mix_pre, w_in, conv_dw_w, conv_dw_b, conv_ln_g, conv_ln_b, w_conv_branch, b_conv_branch, w_att_branch, w_out, norm_mix_post, norm_ffn_pre, w_ffn_up, w_ffn_down, norm_ffn_post, loss_target, m_norm_mix_pre, m_w_in, m_conv_dw_w, m_conv_dw_b, m_conv_ln_g, m_conv_ln_b, m_w_conv_branch, m_b_conv_branch, m_w_att_branch, m_w_out, m_norm_mix_post, m_norm_ffn_pre, m_w_ffn_up, m_w_ffn_down, m_norm_ffn_post, v_norm_mix_pre, v_w_in, v_conv_dw_w, v_conv_dw_b, v_conv_ln_g, v_conv_ln_b, v_w_conv_branch, v_b_conv_branch, v_w_att_branch, v_w_out, v_norm_mix_post, v_norm_ffn_pre, v_w_ffn_up, v_w_ffn_down, v_norm_ffn_post):
    ws = dict(zip(WEIGHTS, [norm_mix_pre, w_in, conv_dw_w, conv_dw_b, conv_ln_g, conv_ln_b, w_conv_branch,
                            b_conv_branch, w_att_branch, w_out, norm_mix_post, norm_ffn_pre, w_ffn_up, w_ffn_down,
                            norm_ffn_post]))
    ms = dict(zip(WEIGHTS, [m_norm_mix_pre, m_w_in, m_conv_dw_w, m_conv_dw_b, m_conv_ln_g, m_conv_ln_b,
                            m_w_conv_branch, m_b_conv_branch, m_w_att_branch, m_w_out, m_norm_mix_post,
                            m_norm_ffn_pre, m_w_ffn_up, m_w_ffn_down, m_norm_ffn_post]))
    vs = dict(zip(WEIGHTS, [v_norm_mix_pre, v_w_in, v_conv_dw_w, v_conv_dw_b, v_conv_ln_g, v_conv_ln_b,
                            v_w_conv_branch, v_b_conv_branch, v_w_att_branch, v_w_out, v_norm_mix_post,
                            v_norm_ffn_pre, v_w_ffn_up, v_w_ffn_down, v_norm_ffn_post]))
    big = COL_SHARDED + ROW_SHARDED

    dw_block = jnp.pad(conv_dw_w, ((0, 1), (0, 0)))
    gathered = _all_gather([ws[nm].astype(BF16) for nm in big] + [dw_block], "gather_weights")
    full = {nm: (_cols_to_full(g) if nm in COL_SHARDED else g.reshape(-1, g.shape[2]))
            for nm, g in zip(big, gathered[:-1])}
    full["conv_dw_w"] = _cols_to_full(gathered[-1])
    for nm in VECTORS:
        full[nm] = ws[nm].reshape(1, -1)

    loss_local, grad_x, grads = _local_step(x[0], loss_target[0], full)
    loss = lax.psum(loss_local, ("x", "y", "c"))

    slabs = [_full_to_cols(grads[nm]) if nm in COL_SHARDED else grads[nm].reshape(N_DEV, -1, grads[nm].shape[1])
             for nm in big]
    received = _all_to_all(slabs, "scatter_grads")
    small = _all_gather([_pack_vectors(grads), grads["conv_dw_w"]], "gather_small_grads")

    out_g, out_d, out_m, out_v = {}, {}, {}, {}
    for nm, parts in zip(big, received):
        out_g[nm], out_d[nm], out_m[nm], out_v[nm] = _sum_adamw("adamw_" + nm, parts, ws[nm], ms[nm], vs[nm])
    sizes = {nm: ws[nm].size for nm in VECTORS}
    vec = _sum_adamw("adamw_vectors", small[0], _pack_vectors(ws), _pack_vectors(ms), _pack_vectors(vs))
    for res, dst in zip(vec, (out_g, out_d, out_m, out_v)):
        dst.update(_unpack_vectors(res, sizes))
    dw_full = _sum_parts("sum_dw_grads", small[1])
    me = _slot(*_place())
    dw_mine = lax.dynamic_slice(dw_full, (0, me * (CONV_DIM // N_DEV)), (CONV_WIDTH, CONV_DIM // N_DEV))
    nm = "conv_dw_w"
    out_g[nm], out_d[nm], out_m[nm], out_v[nm] = _sum_adamw("adamw_dw", dw_mine[None], ws[nm], ms[nm], vs[nm])

    outs = [loss, grad_x[None]]
    for group in (out_g, out_d, out_m, out_v):
        outs += [group[nm] for nm in WEIGHTS]
    return tuple(outs)
```

```python
import functools
import math

import jax
import jax.numpy as jnp
from jax import lax
from jax.experimental import pallas as pl
from jax.experimental.pallas import tpu as pltpu

F32 = jnp.float32
BF16 = jnp.bfloat16

N_DEV = 8
D_MODEL = 1024
CONV_DIM = 512
CONV_WIDTH = 31
N_HEADS = 8
HEAD_DIM = 64
ATT_DIM = N_HEADS * HEAD_DIM
D_FF = 2816
EPS = 1e-6
IN_SPLITS = (0, 1024, 1536, 2048, 2560, 3584, 4608)

ADAM_LR = 0.001
ADAM_B1 = 0.9
ADAM_B2 = 0.999
ADAM_EPS = 1e-08
ADAM_WD = 0.01
ADAM_STEP = 10

LANES = 128
SUBLANES = 8
HALO = 32
ATT_TILE = 256
VMEM_LIMIT = 56 * 1024 * 1024
MESH = pl.DeviceIdType.MESH
ANY = pl.BlockSpec(memory_space=pl.ANY)


def _pick(dim, target, align=LANES):
    t = min(dim, target)
    t -= t % align
    while t >= align:
        if dim % t == 0:
            return t
        t -= align
    return dim


def _params(semantics):
    return pltpu.CompilerParams(dimension_semantics=semantics, vmem_limit_bytes=VMEM_LIMIT)


def _matmul(a, b, *, name, ta=False, tb=False, out_dtype=F32):
    m, k = (a.shape[1], a.shape[0]) if ta else a.shape
    n, k2 = b.shape if tb else (b.shape[1], b.shape[0])
    assert k == k2, (a.shape, b.shape, ta, tb)
    tm, tn, tk = _pick(m, 1408 if ta else 512), _pick(n, 1536), _pick(k, 1536)
    nk = k // tk
    dims = (((0 if ta else 1,), (1 if tb else 0,)), ((), ()))

    def body(a_ref, b_ref, o_ref, *acc):
        part = lax.dot_general(a_ref[...], b_ref[...], dims, preferred_element_type=F32)
        if nk == 1:
            o_ref[...] = part.astype(o_ref.dtype)
            return
        acc_ref, = acc
        kk = pl.program_id(2)

        @pl.when(kk == 0)
        def _():
            acc_ref[...] = part

        @pl.when((kk > 0) & (kk < nk - 1))
        def _():
            acc_ref[...] += part

        @pl.when(kk == nk - 1)
        def _():
            o_ref[...] = (acc_ref[...] + part).astype(o_ref.dtype)

    a_spec = pl.BlockSpec((tk, tm), lambda i, j, kk: (kk, i)) if ta else pl.BlockSpec((tm, tk), lambda i, j, kk: (i, kk))
    b_spec = pl.BlockSpec((tn, tk), lambda i, j, kk: (j, kk)) if tb else pl.BlockSpec((tk, tn), lambda i, j, kk: (kk, j))
    return pl.pallas_call(
        body, name=name, grid=(m // tm, n // tn, nk),
        in_specs=[a_spec, b_spec],
        out_specs=pl.BlockSpec((tm, tn), lambda i, j, kk: (i, j)),
        out_shape=jax.ShapeDtypeStruct((m, n), out_dtype),
        scratch_shapes=[pltpu.VMEM((tm, tn), F32)] if nk > 1 else [],
        compiler_params=_params(("parallel", "parallel", "arbitrary")),
    )(a, b)


def _rowwise(name, fn, rows, bcasts, row_outs, red_outs=(), tm=256):
    s = rows[0].shape[0]
    tm = _pick(s, tm, 16)
    nr, nb, no = len(rows), len(bcasts), len(row_outs)

    def body(*refs):
        ins = [r[...] for r in refs[:nr + nb]]
        outs, reds = fn(*ins)
        for ref, val in zip(refs[nr + nb:nr + nb + no], outs):
            ref[...] = val.astype(ref.dtype)
        i = pl.program_id(0)
        for ref, val in zip(refs[nr + nb + no:], reds):
            @pl.when(i == 0)
            def _():
                ref[...] = val

            @pl.when(i > 0)
            def _():
                ref[...] += val

    in_specs = [pl.BlockSpec((tm, r.shape[1]), lambda i: (i, 0)) for r in rows]
    in_specs += [pl.BlockSpec(b.shape, lambda i: (0, 0)) for b in bcasts]
    out_specs = [pl.BlockSpec((tm, o.shape[1]), lambda i: (i, 0)) for o in row_outs]
    out_specs += [pl.BlockSpec(d.shape, lambda i: (0, 0)) for d in red_outs]
    return pl.pallas_call(
        body, name=name, grid=(s // tm,), in_specs=in_specs, out_specs=out_specs,
        out_shape=list(row_outs) + list(red_outs),
        compiler_params=_params(("arbitrary",)),
    )(*rows, *bcasts)


def _sds(shape, dtype=F32):
    return jax.ShapeDtypeStruct(shape, dtype)


def _rms(x, g):
    y = x * lax.rsqrt(jnp.mean(x * x, axis=-1, keepdims=True) + EPS)
    return y * g


def _silu(x):
    return x * jax.nn.sigmoid(x)


def _ln_silu(u, g, b):
    mu = jnp.mean(u, axis=-1, keepdims=True)
    var = jnp.mean(jnp.square(u - mu), axis=-1, keepdims=True)
    return _silu((u - mu) * lax.rsqrt(var + EPS) * g + b)


def _merge(conv_pre, att_out, g_conv, g_att, b_cb):
    return jax.nn.sigmoid(g_conv) * (conv_pre + b_cb) + jax.nn.sigmoid(g_att) * att_out


def _glu(t):
    return t[:, :CONV_DIM] * jax.nn.sigmoid(t[:, CONV_DIM:])


def _conv_fwd(conv_in, w_pad, b, ln_g, ln_b, tm=256):
    s = conv_in.shape[0]
    tm = _pick(s, tm, HALO)
    ratio = tm // HALO

    def body(main_ref, halo_ref, w_ref, b_ref, g_ref, be_ref, u3_ref, u1_ref, buf):
        i = pl.program_id(0)
        buf[0:HALO, :] = _glu(halo_ref[...]) * (i > 0).astype(F32)
        buf[HALO:HALO + tm, :] = _glu(main_ref[...])
        acc = jnp.zeros((tm, CONV_DIM), F32) + b_ref[...]
        for j in range(CONV_WIDTH):
            acc = acc + w_ref[j:j + 1, :] * buf[pl.ds(HALO - (CONV_WIDTH - 1) + j, tm), :]
        u1_ref[...] = acc
        u3_ref[...] = _ln_silu(acc, g_ref[...], be_ref[...]).astype(u3_ref.dtype)

    return pl.pallas_call(
        body, name="conv_fwd", grid=(s // tm,),
        in_specs=[pl.BlockSpec((tm, 2 * CONV_DIM), lambda i: (i, 0)),
                  pl.BlockSpec((HALO, 2 * CONV_DIM), lambda i: (jnp.maximum(i * ratio - 1, 0), 0)),
                  pl.BlockSpec(w_pad.shape, lambda i: (0, 0)),
                  pl.BlockSpec(b.shape, lambda i: (0, 0)),
                  pl.BlockSpec(ln_g.shape, lambda i: (0, 0)),
                  pl.BlockSpec(ln_b.shape, lambda i: (0, 0))],
        out_specs=[pl.BlockSpec((tm, CONV_DIM), lambda i: (i, 0)),
                   pl.BlockSpec((tm, CONV_DIM), lambda i: (i, 0))],
        out_shape=[_sds((s, CONV_DIM), BF16), _sds((s, CONV_DIM), F32)],
        scratch_shapes=[pltpu.VMEM((tm + HALO, CONV_DIM), F32)],
        compiler_params=_params(("arbitrary",)),
    )(conv_in, conv_in, w_pad, b, ln_g, ln_b)


def _conv_bwd(conv_in, du1, w_pad, tm=256):
    s = conv_in.shape[0]
    tm = _pick(s, tm, HALO)
    ratio = tm // HALO
    nt = s // tm
    last_halo = s // HALO - 1

    def body(main_ref, halo_ref, du_ref, dun_ref, w_ref, dci_ref, dw_ref, db_ref, ubuf, dbuf):
        i = pl.program_id(0)
        main = main_ref[...]
        a = main[:, :CONV_DIM]
        sb = jax.nn.sigmoid(main[:, CONV_DIM:])
        ubuf[0:HALO, :] = _glu(halo_ref[...]) * (i > 0).astype(F32)
        ubuf[HALO:HALO + tm, :] = a * sb
        du = du_ref[...]
        dbuf[0:tm, :] = du
        dbuf[tm:tm + HALO, :] = dun_ref[...] * (i < nt - 1).astype(F32)

        @pl.when(i == 0)
        def _():
            dw_ref[...] = jnp.zeros_like(dw_ref)
            db_ref[...] = jnp.zeros_like(db_ref)

        du0 = jnp.zeros((tm, CONV_DIM), F32)
        for j in range(CONV_WIDTH):
            du0 = du0 + w_ref[j:j + 1, :] * dbuf[pl.ds(CONV_WIDTH - 1 - j, tm), :]
            dw_ref[j:j + 1, :] += jnp.sum(du * ubuf[pl.ds(HALO - (CONV_WIDTH - 1) + j, tm), :], axis=0, keepdims=True)
        db_ref[...] += jnp.sum(du, axis=0, keepdims=True)
        dci_ref[:, :CONV_DIM] = (du0 * sb).astype(dci_ref.dtype)
        dci_ref[:, CONV_DIM:] = (du0 * a * sb * (1.0 - sb)).astype(dci_ref.dtype)

    return pl.pallas_call(
        body, name="conv_bwd", grid=(nt,),
        in_specs=[pl.BlockSpec((tm, 2 * CONV_DIM), lambda i: (i, 0)),
                  pl.BlockSpec((HALO, 2 * CONV_DIM), lambda i: (jnp.maximum(i * ratio - 1, 0), 0)),
                  pl.BlockSpec((tm, CONV_DIM), lambda i: (i, 0)),
                  pl.BlockSpec((HALO, CONV_DIM), lambda i: (jnp.minimum((i + 1) * ratio, last_halo), 0)),
                  pl.BlockSpec(w_pad.shape, lambda i: (0, 0))],
        out_specs=[pl.BlockSpec((tm, 2 * CONV_DIM), lambda i: (i, 0)),
                   pl.BlockSpec(w_pad.shape, lambda i: (0, 0)),
                   pl.BlockSpec((1, CONV_DIM), lambda i: (0, 0))],
        out_shape=[_sds((s, 2 * CONV_DIM), BF16), _sds(w_pad.shape), _sds((1, CONV_DIM))],
        scratch_shapes=[pltpu.VMEM((tm + HALO, CONV_DIM), F32), pltpu.VMEM((tm + HALO, CONV_DIM), F32)],
        compiler_params=_params(("arbitrary",)),
    )(conv_in, conv_in, du1, du1, w_pad)


def _logsig_neg(z):
    return jnp.minimum(-z, 0.0) - jnp.log(1.0 + jnp.exp(-jnp.abs(z)))


def _split_dot(val, tri):
    hi = val.astype(BF16)
    lo = (val - hi.astype(F32)).astype(BF16)
    return jnp.dot(hi, tri, preferred_element_type=F32) + jnp.dot(lo, tri, preferred_element_type=F32)


def _attn_masks(t, later):
    row = lax.broadcasted_iota(jnp.int32, (t, t), 0)
    col = lax.broadcasted_iota(jnp.int32, (t, t), 1)
    tri = jnp.where(row > col if later else row <= col, 1.0, 0.0).astype(BF16)
    return col < row, tri


def _grid_marks(h, nq):
    hh, i = pl.program_id(0), pl.program_id(1)
    return (hh == 0) & (i == 0), (hh == h // 2) & (i == 0), (hh == h - 1) & (i == nq - 1)


def _attn_fwd(q, k, v, exchange):
    h, s, dh = q.shape
    t = ATT_TILE
    scale = 1.0 / math.sqrt(dh)
    x_arrs, x_shape, x_scratch, _ = exchange
    nx = len(x_arrs)

    def body(*refs):
        q_ref, k_ref, v_ref = refs[:3]
        o_ref, lt_ref = refs[3 + nx:5 + nx]
        finish_exchange = _carry_exchange(exchange, refs, 3, 2, *_grid_marks(h, s // t))
        i = pl.program_id(1)
        qs = (q_ref[...].astype(F32) * scale).astype(BF16)
        causal, tri = _attn_masks(t, later=True)

        def step(kb, carry, masked):
            c, acc = carry
            off = pl.multiple_of(kb * t, t)
            kblk = k_ref[pl.ds(off, t), :]
            vblk = v_ref[pl.ds(off, t), :]
            z = lax.dot_general(qs, kblk, (((1,), (1,)), ((), ())), preferred_element_type=F32)
            l = _logsig_neg(z)
            if masked:
                l = jnp.where(causal, l, 0.0)
            e = z + l + _split_dot(l, tri) + c
            if masked:
                e = jnp.where(causal, e, -1e30)
            a = jnp.exp(e)
            acc = acc + jnp.dot(a.astype(BF16), vblk, preferred_element_type=F32)
            return c + jnp.sum(l, axis=1, keepdims=True), acc

        carry = step(i, (jnp.zeros((t, 1), F32), jnp.zeros((t, dh), F32)), True)
        carry = lax.fori_loop(0, i, lambda n, cr: step(i - 1 - n, cr, False), carry)
        lt_ref[...] = jnp.broadcast_to(carry[0], (t, LANES))
        o_ref[...] = carry[1].astype(o_ref.dtype)
        finish_exchange()

    res = pl.pallas_call(
        body, name="attn_fwd", grid=(h, s // t),
        in_specs=[pl.BlockSpec((None, t, dh), lambda hh, i: (hh, i, 0)),
                  pl.BlockSpec((None, s, dh), lambda hh, i: (hh, 0, 0)),
                  pl.BlockSpec((None, s, dh), lambda hh, i: (hh, 0, 0))] + [ANY] * nx,
        out_specs=[pl.BlockSpec((None, t, dh), lambda hh, i: (hh, i, 0)),
                   pl.BlockSpec((None, t, LANES), lambda hh, i: (hh, i, 0))] + [ANY] * nx,
        out_shape=[_sds((h, s, dh), BF16), _sds((h, s, LANES), F32)] + x_shape,
        scratch_shapes=x_scratch,
        compiler_params=_params(("arbitrary", "arbitrary")),
    )(q, k, v, *x_arrs)
    return res[0], res[1], res[2:]


def _attn_bwd(q, k, v, do, ltot, exchange):
    h, s, dh = q.shape
    t = ATT_TILE
    scale = 1.0 / math.sqrt(dh)
    x_arrs, x_shape, x_scratch, _ = exchange
    nx = len(x_arrs)

    def body(*refs):
        q_ref, k_ref, v_ref, do_ref, lt_ref = refs[:5]
        dq_ref, dk_ref, dv_ref = refs[5 + nx:8 + nx]
        finish_exchange = _carry_exchange(exchange, refs, 5, 3, *_grid_marks(h, s // t))
        i = pl.program_id(1)

        @pl.when(i == 0)
        def _():
            dk_ref[...] = jnp.zeros_like(dk_ref)
            dv_ref[...] = jnp.zeros_like(dv_ref)

        qb = q_ref[...]
        qs = (qb.astype(F32) * scale).astype(BF16)
        dob = do_ref[...]
        lt = lt_ref[:, 0:1]
        causal, tri = _attn_masks(t, later=False)

        def step(kb, carry, masked):
            cl, cg, dq = carry
            off = pl.multiple_of(kb * t, t)
            kblk = k_ref[pl.ds(off, t), :]
            vblk = v_ref[pl.ds(off, t), :]
            z = lax.dot_general(qs, kblk, (((1,), (1,)), ((), ())), preferred_element_type=F32)
            l = _logsig_neg(z)
            if masked:
                l = jnp.where(causal, l, 0.0)
            e = z + l + ((lt - cl) - _split_dot(l, tri))
            if masked:
                e = jnp.where(causal, e, -1e30)
            a = jnp.exp(e)
            da = lax.dot_general(dob, vblk, (((1,), (1,)), ((), ())), preferred_element_type=F32)
            g = da * a
            p = cg + jnp.dot(g.astype(BF16), tri, preferred_element_type=F32) - g
            el = jnp.exp(l)
            dz = g * el - p * (1.0 - el)
            if masked:
                dz = jnp.where(causal, dz, 0.0)
            dzb = (dz * scale).astype(BF16)
            dq = dq + jnp.dot(dzb, kblk, preferred_element_type=F32)
            dk_ref[pl.ds(off, t), :] += lax.dot_general(dzb, qb, (((0,), (0,)), ((), ())), preferred_element_type=F32)
            dv_ref[pl.ds(off, t), :] += lax.dot_general(a.astype(BF16), dob, (((0,), (0,)), ((), ())), preferred_element_type=F32)
            return cl + jnp.sum(l, axis=1, keepdims=True), cg + jnp.sum(g, axis=1, keepdims=True), dq

        init = (jnp.zeros((t, 1), F32), jnp.zeros((t, 1), F32), jnp.zeros((t, dh), F32))
        carry = lax.fori_loop(0, i, lambda kb, cr: step(kb, cr, False), init)
        carry = step(i, carry, True)
        dq_ref[...] = carry[2]
        finish_exchange()

    res = pl.pallas_call(
        body, name="attn_bwd", grid=(h, s // t),
        in_specs=[pl.BlockSpec((None, t, dh), lambda hh, i: (hh, i, 0)),
                  pl.BlockSpec((None, s, dh), lambda hh, i: (hh, 0, 0)),
                  pl.BlockSpec((None, s, dh), lambda hh, i: (hh, 0, 0)),
                  pl.BlockSpec((None, t, dh), lambda hh, i: (hh, i, 0)),
                  pl.BlockSpec((None, t, LANES), lambda hh, i: (hh, i, 0))] + [ANY] * nx,
        out_specs=[pl.BlockSpec((None, t, dh), lambda hh, i: (hh, i, 0)),
                   pl.BlockSpec((None, s, dh), lambda hh, i: (hh, 0, 0)),
                   pl.BlockSpec((None, s, dh), lambda hh, i: (hh, 0, 0))] + [ANY] * nx,
        out_shape=[_sds((h, s, dh)), _sds((h, s, dh)), _sds((h, s, dh))] + x_shape,
        scratch_shapes=x_scratch,
        compiler_params=_params(("arbitrary", "arbitrary")),
    )(q, k, v, do, ltot, *x_arrs)
    return res[0], res[1], res[2], res[3:]


def _to_heads(a):
    return a.reshape(a.shape[0], N_HEADS, HEAD_DIM).transpose(1, 0, 2)


def _from_heads(a):
    return a.transpose(1, 0, 2).reshape(a.shape[1], ATT_DIM)


LATE = ["w_conv_branch", "w_att_branch", "w_out", "w_ffn_up", "w_ffn_down"]


def _full_weight(name, gathered):
    return _cols_to_full(gathered) if name in COL_SHARDED else gathered.reshape(-1, gathered.shape[2])


def _grad_slabs(name, grad):
    return _full_to_cols(grad) if name in COL_SHARDED else grad.reshape(N_DEV, -1, grad.shape[1])


def _local_step(x, target, w, late_blocks):
    s = x.shape[0]
    w = dict(w)
    g1, g2, g3, g4 = w["norm_mix_pre"], w["norm_mix_post"], w["norm_ffn_pre"], w["norm_ffn_post"]

    (h1,) = _rowwise("pre_norm", lambda xt, g: ((_rms(xt, g),), ()), [x], [g1], [_sds((s, D_MODEL), BF16)])
    w_in = w["w_in"]
    cols = [w_in[:, IN_SPLITS[n]:IN_SPLITS[n + 1]] for n in range(6)]
    conv_in = _matmul(h1, cols[0], name="proj_conv")
    q = _matmul(h1, cols[1], name="proj_q", out_dtype=BF16)
    k = _matmul(h1, cols[2], name="proj_k", out_dtype=BF16)
    v = _matmul(h1, cols[3], name="proj_v", out_dtype=BF16)
    g_conv = _matmul(h1, cols[4], name="proj_gate_conv")
    g_att = _matmul(h1, cols[5], name="proj_gate_att")

    u3, u1 = _conv_fwd(conv_in, w["conv_dw_w"], w["conv_dw_b"], w["conv_ln_g"], w["conv_ln_b"])
    qh, kh, vh = _to_heads(q), _to_heads(k), _to_heads(v)
    att_h, ltot, gathered = _attn_fwd(qh, kh, vh, _gather_exchange(late_blocks))
    for nm, g in zip(LATE, gathered):
        w[nm] = _full_weight(nm, g)
    att = _from_heads(att_h)

    conv_pre = _matmul(u3, w["w_conv_branch"], name="conv_branch")
    att_out = _matmul(att, w["w_att_branch"], name="att_branch")
    (merged,) = _rowwise("merge", lambda *a: ((_merge(*a),), ()), [conv_pre, att_out, g_conv, g_att],
                         [w["b_conv_branch"]], [_sds((s, D_MODEL), BF16)])
    mix = _matmul(merged, w["w_out"], name="mix_out")

    def mid_fn(xt, mt, g2_, g3_):
        x2_ = xt + _rms(mt, g2_)
        return (x2_, _rms(x2_, g3_)), ()

    x2, h2 = _rowwise("mid_norm", mid_fn, [x, mix], [g2, g3], [_sds((s, D_MODEL)), _sds((s, D_MODEL), BF16)])
    w_up = w["w_ffn_up"]
    gate = _matmul(h2, w_up[:, :D_FF], name="ffn_gate")
    up = _matmul(h2, w_up[:, D_FF:], name="ffn_up")
    (act,) = _rowwise("swiglu", lambda g_, u_: ((_silu(g_) * u_,), ()), [gate, up], [], [_sds((s, D_FF), BF16)], tm=128)
    ff = _matmul(act, w["w_ffn_down"], name="ffn_down")

    def final_fn(x2t, fft, tgt, g4_):
        n4, vjp = jax.vjp(_rms, fft, g4_)
        err = x2t + n4 - tgt
        dy = err * (1.0 / D_MODEL)
        dff, dg4 = vjp(dy)
        return (dy, dff), (jnp.sum(err * err, axis=0, keepdims=True), dg4)

    dy, dff, loss_cols, d_g4 = _rowwise("final", final_fn, [x2, ff, target], [g4],
                                        [_sds((s, D_MODEL)), _sds((s, D_MODEL), BF16)],
                                        [_sds((1, D_MODEL)), _sds((1, D_MODEL))])
    loss = 0.5 * jnp.sum(loss_cols) / D_MODEL

    d_act = _matmul(dff, w["w_ffn_down"], tb=True, name="d_act")
    d_w_down = _matmul(act, dff, ta=True, name="d_w_down", out_dtype=BF16)

    def swiglu_bwd_fn(g_, u_, da_):
        _, vjp = jax.vjp(lambda a, b: _silu(a) * b, g_, u_)
        return vjp(da_), ()

    d_gate, d_up = _rowwise("swiglu_bwd", swiglu_bwd_fn, [gate, up, d_act], [],
                            [_sds((s, D_FF), BF16), _sds((s, D_FF), BF16)], tm=128)
    dh2a = _matmul(d_gate, w_up[:, :D_FF], tb=True, name="d_h2_gate")
    dh2b = _matmul(d_up, w_up[:, D_FF:], tb=True, name="d_h2_up")
    d_w_gate = _matmul(h2, d_gate, ta=True, name="d_w_gate", out_dtype=BF16)
    d_w_up = _matmul(h2, d_up, ta=True, name="d_w_up", out_dtype=BF16)

    def mid_bwd_fn(xt, mt, dyt, da, db, g2_, g3_):
        n2, vjp2 = jax.vjp(_rms, mt, g2_)
        x2_ = xt + n2
        _, vjp3 = jax.vjp(_rms, x2_, g3_)
        dx2_, dg3 = vjp3(da + db)
        dx2_ = dx2_ + dyt
        dmix_, dg2 = vjp2(dx2_)
        return (dx2_, dmix_), (dg2, dg3)

    dx2, dmix, d_g2, d_g3 = _rowwise("mid_bwd", mid_bwd_fn, [x, mix, dy, dh2a, dh2b], [g2, g3],
                                     [_sds((s, D_MODEL)), _sds((s, D_MODEL), BF16)],
                                     [_sds((1, D_MODEL)), _sds((1, D_MODEL))])
    d_merged = _matmul(dmix, w["w_out"], tb=True, name="d_merged")
    d_w_out = _matmul(merged, dmix, ta=True, name="d_w_out", out_dtype=BF16)

    def merge_bwd_fn(cp, ao, gc, ga, dm, b_cb):
        _, vjp = jax.vjp(_merge, cp, ao, gc, ga, b_cb)
        dcp, dao, dgc, dga, dbias = vjp(dm)
        return (dcp, dao, dgc, dga), (dbias,)

    d_conv_out, d_att_out, d_g_conv, d_g_att, d_b_cb = _rowwise(
        "merge_bwd", merge_bwd_fn, [conv_pre, att_out, g_conv, g_att, d_merged], [w["b_conv_branch"]],
        [_sds((s, D_MODEL), BF16)] * 4, [_sds((1, D_MODEL))])

    du3 = _matmul(d_conv_out, w["w_conv_branch"], tb=True, name="d_u3")
    d_w_cb = _matmul(u3, d_conv_out, ta=True, name="d_w_conv_branch", out_dtype=BF16)
    d_att = _matmul(d_att_out, w["w_att_branch"], tb=True, name="d_att", out_dtype=BF16)
    d_w_ab = _matmul(att, d_att_out, ta=True, name="d_w_att_branch", out_dtype=BF16)

    late_grads = {"w_conv_branch": d_w_cb, "w_att_branch": d_w_ab, "w_out": d_w_out,
                  "w_ffn_up": jnp.concatenate([d_w_gate, d_w_up], axis=1), "w_ffn_down": d_w_down}
    dqh, dkh, dvh, received = _attn_bwd(qh, kh, vh, _to_heads(d_att), ltot,
                                        _scatter_exchange([_grad_slabs(nm, late_grads[nm]) for nm in LATE]))

    def ln_bwd_fn(u1t, du3t, g_, b_):
        _, vjp = jax.vjp(_ln_silu, u1t, g_, b_)
        du1_, dg_, db_ = vjp(du3t)
        return (du1_,), (dg_, db_)

    du1, d_ln_g, d_ln_b = _rowwise("conv_ln_bwd", ln_bwd_fn, [u1, du3], [w["conv_ln_g"], w["conv_ln_b"]],
                                   [_sds((s, CONV_DIM))], [_sds((1, CONV_DIM)), _sds((1, CONV_DIM))])
    d_conv_in, d_dw_w, d_dw_b = _conv_bwd(conv_in, du1, w["conv_dw_w"])

    d_proj = jnp.concatenate([d_conv_in, _from_heads(dqh).astype(BF16), _from_heads(dkh).astype(BF16),
                              _from_heads(dvh).astype(BF16), d_g_conv, d_g_att], axis=1)
    dh1 = _matmul(d_proj, w_in, tb=True, name="d_h1")
    d_w_in = _matmul(h1, d_proj, ta=True, name="d_w_in", out_dtype=BF16)

    def pre_bwd_fn(xt, dh, dx2t, g_):
        _, vjp = jax.vjp(_rms, xt, g_)
        dx_, dg_ = vjp(dh)
        return (dx_ + dx2t,), (dg_,)

    grad_x, d_g1 = _rowwise("pre_bwd", pre_bwd_fn, [x, dh1, dx2], [g1], [_sds((s, D_MODEL))], [_sds((1, D_MODEL))])

    grads = {
        "norm_mix_pre": d_g1, "w_in": d_w_in, "conv_dw_w": d_dw_w, "conv_dw_b": d_dw_b,
        "conv_ln_g": d_ln_g, "conv_ln_b": d_ln_b, "b_conv_branch": d_b_cb,
        "norm_mix_post": d_g2, "norm_ffn_pre": d_g3, "norm_ffn_post": d_g4,
    }
    return loss, grad_x, received, grads


def _place():
    x, y, c = lax.axis_index("x"), lax.axis_index("y"), lax.axis_index("c")
    return x, y, c


def _slot(px, py, pc):
    return 4 * px + 2 * py + pc


def _exchange_scratch(n):
    return [pltpu.SemaphoreType.DMA((7 * n,)), pltpu.SemaphoreType.DMA((7 * n,)), pltpu.SemaphoreType.DMA((n,))]


def _gather_exchange(arrs):
    n = len(arrs)

    def phases(ins, outs, send_sems, recv_sems, local_sems):
        x, y, c = _place()
        me, sibling = (x, y, c), (x, y, 1 - c)
        chips = [(1 - x, y), (x, 1 - y), (1 - x, 1 - y)]

        def copy(a, kk, block, to, src=None):
            dst = outs[a].at[_slot(*block)]
            return pltpu.make_async_remote_copy(
                src_ref=dst if src is None else src, dst_ref=dst,
                send_sem=send_sems.at[a * 7 + kk], recv_sem=recv_sems.at[a * 7 + kk],
                device_id=to, device_id_type=MESH)

        mine = [pltpu.make_async_copy(ins[a], outs[a].at[_slot(*me)], local_sems.at[a]) for a in range(n)]
        first = []
        for a in range(n):
            first.append(copy(a, 0, me, sibling, src=ins[a]))
            first += [copy(a, 1 + j, me, (*chip, c), src=ins[a]) for j, chip in enumerate(chips)]
        passed = [copy(a, 4 + j, (*chip, c), sibling) for j, chip in enumerate(chips) for a in range(n)]

        def send():
            for cp in mine + first:
                cp.start()

        def pass_on():
            for j, chip in enumerate(chips):
                for a in range(n):
                    copy(a, 1 + j, (*chip, c), me).wait_recv()
                    passed[j * n + a].start()

        def finish():
            for a in range(n):
                copy(a, 0, sibling, me).wait_recv()
                for j, chip in enumerate(chips):
                    copy(a, 4 + j, (*chip, 1 - c), me).wait_recv()
            for cp in first + passed:
                cp.wait_send()
            for cp in mine:
                cp.wait()

        return [send, pass_on, finish]

    return list(arrs), [_sds((N_DEV,) + a.shape, a.dtype) for a in arrs], _exchange_scratch(n), phases


def _scatter_exchange(arrs):
    n = len(arrs)
    flips = [(fx, fy, fc) for fx in (0, 1) for fy in (0, 1) for fc in (0, 1)][1:]

    def phases(ins, outs, send_sems, recv_sems, local_sems):
        x, y, c = _place()
        mine = _slot(x, y, c)
        local = [pltpu.make_async_copy(ins[a].at[mine], outs[a].at[mine], local_sems.at[a]) for a in range(n)]
        peers = [((1 - x) if fx else x, (1 - y) if fy else y, (1 - c) if fc else c) for fx, fy, fc in flips]

        def copy(a, kk, src_slot, dst_slot):
            return pltpu.make_async_remote_copy(
                src_ref=ins[a].at[src_slot], dst_ref=outs[a].at[dst_slot],
                send_sem=send_sems.at[a * 7 + kk], recv_sem=recv_sems.at[a * 7 + kk],
                device_id=peers[kk], device_id_type=MESH)

        sends = [copy(a, kk, _slot(*peers[kk]), mine) for a in range(n) for kk in range(7)]

        def send():
            for cp in local + sends:
                cp.start()

        def finish():
            for a in range(n):
                for kk in range(7):
                    copy(a, kk, mine, _slot(*peers[kk])).wait_recv()
            for cp in sends:
                cp.wait_send()
            for cp in local:
                cp.wait()

        return [send, finish]

    return list(arrs), [_sds(a.shape, a.dtype) for a in arrs], _exchange_scratch(n), phases


def _exchange_call(name, exchange):
    arrs, out_shape, scratch, phases = exchange
    n = len(arrs)

    def body(*refs):
        for step in phases(refs[:n], refs[n:2 * n], *refs[2 * n:]):
            step()

    return pl.pallas_call(body, name=name, in_specs=[ANY] * n, out_specs=[ANY] * n,
                          out_shape=out_shape, scratch_shapes=scratch)(*arrs)


def _carry_exchange(exchange, refs, n_in, n_out, first, middle, last):
    arrs, _, _, phases = exchange
    n = len(arrs)
    ins = refs[n_in:n_in + n]
    outs = refs[n_in + n + n_out:n_in + 2 * n + n_out]
    steps = phases(ins, outs, *refs[n_in + 2 * n + n_out:])
    pl.when(first)(steps[0])
    if len(steps) == 3:
        pl.when(middle)(steps[1])
    return lambda: pl.when(last)(steps[-1])


def _adamw_math(w, g, m, v):
    m2 = ADAM_B1 * m + (1.0 - ADAM_B1) * g
    v2 = ADAM_B2 * v + (1.0 - ADAM_B2) * jnp.square(g)
    m_hat = m2 / (1.0 - ADAM_B1 ** ADAM_STEP)
    v_hat = v2 / (1.0 - ADAM_B2 ** ADAM_STEP)
    delta = -ADAM_LR * (m_hat / (jnp.sqrt(v_hat) + ADAM_EPS) + ADAM_WD * w)
    return delta, m2, v2


def _sum_adamw(name, parts, w, m, v, tr=256):
    p, r, c = parts.shape
    tr = _pick(r, tr, 16)

    def body(p_ref, w_ref, m_ref, v_ref, g_ref, d_ref, m2_ref, v2_ref):
        g = p_ref[0].astype(F32)
        for d in range(1, p):
            g = g + p_ref[d].astype(F32)
        delta, m2, v2 = _adamw_math(w_ref[...], g, m_ref[...], v_ref[...])
        g_ref[...] = g
        d_ref[...] = delta
        m2_ref[...] = m2
        v2_ref[...] = v2

    tile = pl.BlockSpec((tr, c), lambda i: (i, 0))
    return pl.pallas_call(
        body, name=name, grid=(r // tr,),
        in_specs=[pl.BlockSpec((p, tr, c), lambda i: (0, i, 0)), tile, tile, tile],
        out_specs=[tile] * 4, out_shape=[_sds((r, c))] * 4,
        compiler_params=_params(("parallel",)),
    )(parts, w, m, v)


def _sum_parts(name, parts):
    p, r, c = parts.shape

    def body(p_ref, o_ref):
        g = p_ref[0]
        for d in range(1, p):
            g = g + p_ref[d]
        o_ref[...] = g

    return pl.pallas_call(
        body, name=name, out_shape=_sds((r, c)),
        in_specs=[pl.BlockSpec(memory_space=pltpu.VMEM)], out_specs=pl.BlockSpec(memory_space=pltpu.VMEM),
    )(parts)


WEIGHTS = ["norm_mix_pre", "w_in", "conv_dw_w", "conv_dw_b", "conv_ln_g", "conv_ln_b", "w_conv_branch",
           "b_conv_branch", "w_att_branch", "w_out", "norm_mix_post", "norm_ffn_pre", "w_ffn_up", "w_ffn_down",
           "norm_ffn_post"]
COL_SHARDED = ["w_in", "w_conv_branch", "w_att_branch", "w_ffn_up"]
ROW_SHARDED = ["w_out", "w_ffn_down"]
VECTORS = ["norm_mix_pre", "conv_dw_b", "conv_ln_g", "conv_ln_b", "b_conv_branch", "norm_mix_post",
           "norm_ffn_pre", "norm_ffn_post"]


def _cols_to_full(g):
    return g.transpose(1, 0, 2).reshape(g.shape[1], N_DEV * g.shape[2])


def _full_to_cols(f):
    return f.reshape(f.shape[0], N_DEV, f.shape[1] // N_DEV).transpose(1, 0, 2)


def _pack_vectors(vecs):
    rows = [jnp.pad(vecs[nm].reshape(-1), (0, D_MODEL - vecs[nm].size)) for nm in VECTORS]
    return jnp.stack(rows)


def _unpack_vectors(packed, sizes):
    return {nm: packed[n, :sizes[nm]] for n, nm in enumerate(VECTORS)}


def kernel(x, norm_mix_pre, w_in, conv_dw_w, conv_dw_b, conv_ln_g, conv_ln_b, w_conv_branch, b_conv_branch, w_att_branch, w_out, norm_mix_post, norm_ffn_pre, w_ffn_up, w_ffn_down, norm_ffn_post, loss_target, m_norm_mix_pre, m_w_in, m_conv_dw_w, m_conv_dw_b, m_conv_ln_g, m_conv_ln_b, m_w_conv_branch, m_b_conv_branch, m_w_att_branch, m_w_out, m_norm_mix_post, m_norm_ffn_pre, m_w_ffn_up, m_w_ffn_down, m_norm_ffn_post, v_norm_mix_pre, v_w_in, v_conv_dw_w, v_conv_dw_b, v_conv_ln_g, v_conv_ln_b, v_w_conv_branch, v_b_conv_branch, v_w_att_branch, v_w_out, v_norm_mix_post, v_norm_ffn_pre, v_w_ffn_up, v_w_ffn_down, v_norm_ffn_post):
    ws = dict(zip(WEIGHTS, [norm_mix_pre, w_in, conv_dw_w, conv_dw_b, conv_ln_g, conv_ln_b, w_conv_branch,
                            b_conv_branch, w_att_branch, w_out, norm_mix_post, norm_ffn_pre, w_ffn_up, w_ffn_down,
                            norm_ffn_post]))
    ms = dict(zip(WEIGHTS, [m_norm_mix_pre, m_w_in, m_conv_dw_w, m_conv_dw_b, m_conv_ln_g, m_conv_ln_b,
                            m_w_conv_branch, m_b_conv_branch, m_w_att_branch, m_w_out, m_norm_mix_post,
                            m_norm_ffn_pre, m_w_ffn_up, m_w_ffn_down, m_norm_ffn_post]))
    vs = dict(zip(WEIGHTS, [v_norm_mix_pre, v_w_in, v_conv_dw_w, v_conv_dw_b, v_conv_ln_g, v_conv_ln_b,
                            v_w_conv_branch, v_b_conv_branch, v_w_att_branch, v_w_out, v_norm_mix_post,
                            v_norm_ffn_pre, v_w_ffn_up, v_w_ffn_down, v_norm_ffn_post]))

    dw_block = jnp.pad(conv_dw_w, ((0, 1), (0, 0)))
    g_in, g_dw = _exchange_call("gather_first", _gather_exchange([w_in.astype(BF16), dw_block]))
    full = {"w_in": _full_weight("w_in", g_in), "conv_dw_w": _cols_to_full(g_dw)}
    for nm in VECTORS:
        full[nm] = ws[nm].reshape(1, -1)

    loss_local, grad_x, received, grads = _local_step(x[0], loss_target[0], full, [ws[nm].astype(BF16) for nm in LATE])
    loss = lax.psum(loss_local, ("x", "y", "c"))

    (recv_in,) = _exchange_call("scatter_last", _scatter_exchange([_grad_slabs("w_in", grads["w_in"])]))
    small = _exchange_call("gather_small_grads", _gather_exchange([_pack_vectors(grads), grads["conv_dw_w"]]))

    out_g, out_d, out_m, out_v = {}, {}, {}, {}
    for nm, parts in zip(LATE + ["w_in"], list(received) + [recv_in]):
        out_g[nm], out_d[nm], out_m[nm], out_v[nm] = _sum_adamw("adamw_" + nm, parts, ws[nm], ms[nm], vs[nm])
    sizes = {nm: ws[nm].size for nm in VECTORS}
    vec = _sum_adamw("adamw_vectors", small[0], _pack_vectors(ws), _pack_vectors(ms), _pack_vectors(vs))
    for res, dst in zip(vec, (out_g, out_d, out_m, out_v)):
        dst.update(_unpack_vectors(res, sizes))
    dw_full = _sum_parts("sum_dw_grads", small[1])
    me = _slot(*_place())
    dw_mine = lax.dynamic_slice(dw_full, (0, me * (CONV_DIM // N_DEV)), (CONV_WIDTH, CONV_DIM // N_DEV))
    nm = "conv_dw_w"
    out_g[nm], out_d[nm], out_m[nm], out_v[nm] = _sum_adamw("adamw_dw", dw_mine[None], ws[nm], ms[nm], vs[nm])

    outs = [loss, grad_x[None]]
    for group in (out_g, out_d, out_m, out_v):
        outs += [group[nm] for nm in WEIGHTS]
    return tuple(outs)
```

```python
import functools
import math

import jax
import jax.numpy as jnp
from jax import lax
from jax.experimental import pallas as pl
from jax.experimental.pallas import tpu as pltpu

F32 = jnp.float32
BF16 = jnp.bfloat16

N_DEV = 8
D_MODEL = 1024
CONV_DIM = 512
CONV_WIDTH = 31
N_HEADS = 8
HEAD_DIM = 64
ATT_DIM = N_HEADS * HEAD_DIM
D_FF = 2816
EPS = 1e-6
IN_SPLITS = (0, 1024, 1536, 2048, 2560, 3584, 4608)

ADAM_LR = 0.001
ADAM_B1 = 0.9
ADAM_B2 = 0.999
ADAM_EPS = 1e-08
ADAM_WD = 0.01
ADAM_STEP = 10

LANES = 128
SUBLANES = 8
HALO = 32
ATT_TILE = 256
VMEM_LIMIT = 56 * 1024 * 1024
MESH = pl.DeviceIdType.MESH
ANY = pl.BlockSpec(memory_space=pl.ANY)


def _pick(dim, target, align=LANES):
    t = min(dim, target)
    t -= t % align
    while t >= align:
        if dim % t == 0:
            return t
        t -= align
    return dim


def _params(semantics):
    return pltpu.CompilerParams(dimension_semantics=semantics, vmem_limit_bytes=VMEM_LIMIT)


def _matmul(a, b, *, name, ta=False, tb=False, out_dtype=F32):
    m, k = (a.shape[1], a.shape[0]) if ta else a.shape
    n, k2 = b.shape if tb else (b.shape[1], b.shape[0])
    assert k == k2, (a.shape, b.shape, ta, tb)
    tm, tn, tk = _pick(m, 1408 if ta else 512), _pick(n, 1536), _pick(k, 1536)
    nk = k // tk
    dims = (((0 if ta else 1,), (1 if tb else 0,)), ((), ()))

    def body(a_ref, b_ref, o_ref, *acc):
        part = lax.dot_general(a_ref[...], b_ref[...], dims, preferred_element_type=F32)
        if nk == 1:
            o_ref[...] = part.astype(o_ref.dtype)
            return
        acc_ref, = acc
        kk = pl.program_id(2)

        @pl.when(kk == 0)
        def _():
            acc_ref[...] = part

        @pl.when((kk > 0) & (kk < nk - 1))
        def _():
            acc_ref[...] += part

        @pl.when(kk == nk - 1)
        def _():
            o_ref[...] = (acc_ref[...] + part).astype(o_ref.dtype)

    a_spec = pl.BlockSpec((tk, tm), lambda i, j, kk: (kk, i)) if ta else pl.BlockSpec((tm, tk), lambda i, j, kk: (i, kk))
    b_spec = pl.BlockSpec((tn, tk), lambda i, j, kk: (j, kk)) if tb else pl.BlockSpec((tk, tn), lambda i, j, kk: (kk, j))
    return pl.pallas_call(
        body, name=name, grid=(m // tm, n // tn, nk),
        in_specs=[a_spec, b_spec],
        out_specs=pl.BlockSpec((tm, tn), lambda i, j, kk: (i, j)),
        out_shape=jax.ShapeDtypeStruct((m, n), out_dtype),
        scratch_shapes=[pltpu.VMEM((tm, tn), F32)] if nk > 1 else [],
        compiler_params=_params(("parallel", "parallel", "arbitrary")),
    )(a, b)


def _rowwise(name, fn, rows, bcasts, row_outs, red_outs=(), tm=256):
    s = rows[0].shape[0]
    tm = _pick(s, tm, 16)
    nr, nb, no = len(rows), len(bcasts), len(row_outs)

    def body(*refs):
        ins = [r[...] for r in refs[:nr + nb]]
        outs, reds = fn(*ins)
        for ref, val in zip(refs[nr + nb:nr + nb + no], outs):
            ref[...] = val.astype(ref.dtype)
        i = pl.program_id(0)
        for ref, val in zip(refs[nr + nb + no:], reds):
            @pl.when(i == 0)
            def _():
                ref[...] = val

            @pl.when(i > 0)
            def _():
                ref[...] += val

    in_specs = [pl.BlockSpec((tm, r.shape[1]), lambda i: (i, 0)) for r in rows]
    in_specs += [pl.BlockSpec(b.shape, lambda i: (0, 0)) for b in bcasts]
    out_specs = [pl.BlockSpec((tm, o.shape[1]), lambda i: (i, 0)) for o in row_outs]
    out_specs += [pl.BlockSpec(d.shape, lambda i: (0, 0)) for d in red_outs]
    return pl.pallas_call(
        body, name=name, grid=(s // tm,), in_specs=in_specs, out_specs=out_specs,
        out_shape=list(row_outs) + list(red_outs),
        compiler_params=_params(("arbitrary",)),
    )(*rows, *bcasts)


def _sds(shape, dtype=F32):
    return jax.ShapeDtypeStruct(shape, dtype)


def _rms(x, g):
    y = x * lax.rsqrt(jnp.mean(x * x, axis=-1, keepdims=True) + EPS)
    return y * g


def _silu(x):
    return x * jax.nn.sigmoid(x)


def _ln_silu(u, g, b):
    mu = jnp.mean(u, axis=-1, keepdims=True)
    var = jnp.mean(jnp.square(u - mu), axis=-1, keepdims=True)
    return _silu((u - mu) * lax.rsqrt(var + EPS) * g + b)


def _merge(conv_pre, att_out, g_conv, g_att, b_cb):
    return jax.nn.sigmoid(g_conv) * (conv_pre + b_cb) + jax.nn.sigmoid(g_att) * att_out


def _glu(t):
    return t[:, :CONV_DIM] * jax.nn.sigmoid(t[:, CONV_DIM:])


def _conv_fwd(conv_in, w_pad, b, ln_g, ln_b, tm=256):
    s = conv_in.shape[0]
    tm = _pick(s, tm, HALO)
    ratio = tm // HALO

    def body(main_ref, halo_ref, w_ref, b_ref, g_ref, be_ref, u3_ref, u1_ref, buf):
        i = pl.program_id(0)
        buf[0:HALO, :] = _glu(halo_ref[...]) * (i > 0).astype(F32)
        buf[HALO:HALO + tm, :] = _glu(main_ref[...])
        acc = jnp.zeros((tm, CONV_DIM), F32) + b_ref[...]
        for j in range(CONV_WIDTH):
            acc = acc + w_ref[j:j + 1, :] * buf[pl.ds(HALO - (CONV_WIDTH - 1) + j, tm), :]
        u1_ref[...] = acc
        u3_ref[...] = _ln_silu(acc, g_ref[...], be_ref[...]).astype(u3_ref.dtype)

    return pl.pallas_call(
        body, name="conv_fwd", grid=(s // tm,),
        in_specs=[pl.BlockSpec((tm, 2 * CONV_DIM), lambda i: (i, 0)),
                  pl.BlockSpec((HALO, 2 * CONV_DIM), lambda i: (jnp.maximum(i * ratio - 1, 0), 0)),
                  pl.BlockSpec(w_pad.shape, lambda i: (0, 0)),
                  pl.BlockSpec(b.shape, lambda i: (0, 0)),
                  pl.BlockSpec(ln_g.shape, lambda i: (0, 0)),
                  pl.BlockSpec(ln_b.shape, lambda i: (0, 0))],
        out_specs=[pl.BlockSpec((tm, CONV_DIM), lambda i: (i, 0)),
                   pl.BlockSpec((tm, CONV_DIM), lambda i: (i, 0))],
        out_shape=[_sds((s, CONV_DIM), BF16), _sds((s, CONV_DIM), F32)],
        scratch_shapes=[pltpu.VMEM((tm + HALO, CONV_DIM), F32)],
        compiler_params=_params(("arbitrary",)),
    )(conv_in, conv_in, w_pad, b, ln_g, ln_b)


def _conv_bwd(conv_in, du1, w_pad, tm=256):
    s = conv_in.shape[0]
    tm = _pick(s, tm, HALO)
    ratio = tm // HALO
    nt = s // tm
    last_halo = s // HALO - 1

    def body(main_ref, halo_ref, du_ref, dun_ref, w_ref, dci_ref, dw_ref, db_ref, ubuf, dbuf):
        i = pl.program_id(0)
        main = main_ref[...]
        a = main[:, :CONV_DIM]
        sb = jax.nn.sigmoid(main[:, CONV_DIM:])
        ubuf[0:HALO, :] = _glu(halo_ref[...]) * (i > 0).astype(F32)
        ubuf[HALO:HALO + tm, :] = a * sb
        du = du_ref[...]
        dbuf[0:tm, :] = du
        dbuf[tm:tm + HALO, :] = dun_ref[...] * (i < nt - 1).astype(F32)

        @pl.when(i == 0)
        def _():
            dw_ref[...] = jnp.zeros_like(dw_ref)
            db_ref[...] = jnp.zeros_like(db_ref)

        du0 = jnp.zeros((tm, CONV_DIM), F32)
        for j in range(CONV_WIDTH):
            du0 = du0 + w_ref[j:j + 1, :] * dbuf[pl.ds(CONV_WIDTH - 1 - j, tm), :]
            dw_ref[j:j + 1, :] += jnp.sum(du * ubuf[pl.ds(HALO - (CONV_WIDTH - 1) + j, tm), :], axis=0, keepdims=True)
        db_ref[...] += jnp.sum(du, axis=0, keepdims=True)
        dci_ref[:, :CONV_DIM] = (du0 * sb).astype(dci_ref.dtype)
        dci_ref[:, CONV_DIM:] = (du0 * a * sb * (1.0 - sb)).astype(dci_ref.dtype)

    return pl.pallas_call(
        body, name="conv_bwd", grid=(nt,),
        in_specs=[pl.BlockSpec((tm, 2 * CONV_DIM), lambda i: (i, 0)),
                  pl.BlockSpec((HALO, 2 * CONV_DIM), lambda i: (jnp.maximum(i * ratio - 1, 0), 0)),
                  pl.BlockSpec((tm, CONV_DIM), lambda i: (i, 0)),
                  pl.BlockSpec((HALO, CONV_DIM), lambda i: (jnp.minimum((i + 1) * ratio, last_halo), 0)),
                  pl.BlockSpec(w_pad.shape, lambda i: (0, 0))],
        out_specs=[pl.BlockSpec((tm, 2 * CONV_DIM), lambda i: (i, 0)),
                   pl.BlockSpec(w_pad.shape, lambda i: (0, 0)),
                   pl.BlockSpec((1, CONV_DIM), lambda i: (0, 0))],
        out_shape=[_sds((s, 2 * CONV_DIM), BF16), _sds(w_pad.shape), _sds((1, CONV_DIM))],
        scratch_shapes=[pltpu.VMEM((tm + HALO, CONV_DIM), F32), pltpu.VMEM((tm + HALO, CONV_DIM), F32)],
        compiler_params=_params(("arbitrary",)),
    )(conv_in, conv_in, du1, du1, w_pad)


def _logsig_neg(z):
    return jnp.minimum(-z, 0.0) - jnp.log(1.0 + jnp.exp(-jnp.abs(z)))


def _split_dot(val, tri):
    hi = val.astype(BF16)
    lo = (val - hi.astype(F32)).astype(BF16)
    return jnp.dot(hi, tri, preferred_element_type=F32) + jnp.dot(lo, tri, preferred_element_type=F32)


def _attn_masks(t, later):
    row = lax.broadcasted_iota(jnp.int32, (t, t), 0)
    col = lax.broadcasted_iota(jnp.int32, (t, t), 1)
    tri = jnp.where(row > col if later else row <= col, 1.0, 0.0).astype(BF16)
    return col < row, tri


def _grid_marks(h, nq):
    hh, i = pl.program_id(0), pl.program_id(1)
    return (hh == 0) & (i == 0), (hh == (3 * h) // 4) & (i == 0), (hh == h - 1) & (i == nq - 1)


def _head_masks(shape):
    lane = lax.broadcasted_iota(jnp.int32, shape, len(shape) - 1)
    return lane < HEAD_DIM, lane >= HEAD_DIM


def _per_head(blk):
    m0, m1 = _head_masks(blk.shape)
    zero = jnp.zeros_like(blk)
    return jnp.where(m0, blk, zero), jnp.where(m1, blk, zero)


NT = (((1,), (1,)), ((), ()))
TN = (((0,), (0,)), ((), ()))


def _attn_fwd(q, k, v, exchange):
    s = q.shape[0]
    hp = q.shape[1] // LANES
    t = ATT_TILE
    scale = 1.0 / math.sqrt(HEAD_DIM)
    x_arrs, x_shape, x_scratch, _ = exchange
    nx = len(x_arrs)

    def body(*refs):
        q_ref, k_ref, v_ref = refs[:3]
        o_ref, lt_ref = refs[3 + nx:5 + nx]
        finish_exchange = _carry_exchange(exchange, refs, 3, 2, *_grid_marks(hp, s // t))
        i = pl.program_id(1)
        qs = _per_head((q_ref[...].astype(F32) * scale).astype(BF16))
        causal, tri = _attn_masks(t, later=True)

        def step(kb, carry, masked):
            cs, acc = carry
            off = pl.multiple_of(kb * t, t)
            kblk = k_ref[pl.ds(off, t), :]
            vs = _per_head(v_ref[pl.ds(off, t), :])
            new_cs = []
            for hd in range(2):
                z = lax.dot_general(qs[hd], kblk, NT, preferred_element_type=F32)
                l = _logsig_neg(z)
                if masked:
                    l = jnp.where(causal, l, 0.0)
                e = z + l + _split_dot(l, tri) + cs[hd]
                if masked:
                    e = jnp.where(causal, e, -1e30)
                acc = acc + jnp.dot(jnp.exp(e).astype(BF16), vs[hd], preferred_element_type=F32)
                new_cs.append(cs[hd] + jnp.sum(l, axis=1, keepdims=True))
            return tuple(new_cs), acc

        zero = jnp.zeros((t, 1), F32)
        carry = step(i, ((zero, zero), jnp.zeros((t, LANES), F32)), True)
        carry = lax.fori_loop(0, i, lambda n, cr: step(i - 1 - n, cr, False), carry)
        m0, _ = _head_masks((t, LANES))
        lt_ref[...] = jnp.where(m0, carry[0][0], carry[0][1])
        o_ref[...] = carry[1].astype(o_ref.dtype)
        finish_exchange()

    res = pl.pallas_call(
        body, name="attn_fwd", grid=(hp, s // t),
        in_specs=[pl.BlockSpec((t, LANES), lambda p, i: (i, p)),
                  pl.BlockSpec((s, LANES), lambda p, i: (0, p)),
                  pl.BlockSpec((s, LANES), lambda p, i: (0, p))] + [ANY] * nx,
        out_specs=[pl.BlockSpec((t, LANES), lambda p, i: (i, p)),
                   pl.BlockSpec((None, t, LANES), lambda p, i: (p, i, 0))] + [ANY] * nx,
        out_shape=[_sds(q.shape, BF16), _sds((hp, s, LANES), F32)] + x_shape,
        scratch_shapes=x_scratch,
        compiler_params=_params(("arbitrary", "arbitrary")),
    )(q, k, v, *x_arrs)
    return res[0], res[1], res[2:]


def _attn_bwd(q, k, v, do, ltot, exchange):
    s = q.shape[0]
    hp = q.shape[1] // LANES
    t = ATT_TILE
    scale = 1.0 / math.sqrt(HEAD_DIM)
    x_arrs, x_shape, x_scratch, _ = exchange
    nx = len(x_arrs)

    def body(*refs):
        q_ref, k_ref, v_ref, do_ref, lt_ref = refs[:5]
        dq_ref, dk_ref, dv_ref = refs[5 + nx:8 + nx]
        finish_exchange = _carry_exchange(exchange, refs, 5, 3, *_grid_marks(hp, s // t))
        i = pl.program_id(1)

        @pl.when(i == 0)
        def _():
            dk_ref[...] = jnp.zeros_like(dk_ref)
            dv_ref[...] = jnp.zeros_like(dv_ref)

        qb = q_ref[...]
        qm = _per_head(qb)
        qs = _per_head((qb.astype(F32) * scale).astype(BF16))
        dos = _per_head(do_ref[...])
        lts = (lt_ref[:, 0:1], lt_ref[:, HEAD_DIM:HEAD_DIM + 1])
        causal, tri = _attn_masks(t, later=False)

        def step(kb, carry, masked):
            cls, cgs, dq = carry
            off = pl.multiple_of(kb * t, t)
            kblk = k_ref[pl.ds(off, t), :]
            vblk = v_ref[pl.ds(off, t), :]
            ks = _per_head(kblk)
            dk = jnp.zeros((t, LANES), F32)
            dv = jnp.zeros((t, LANES), F32)
            new_cls, new_cgs = [], []
            for hd in range(2):
                z = lax.dot_general(qs[hd], kblk, NT, preferred_element_type=F32)
                l = _logsig_neg(z)
                if masked:
                    l = jnp.where(causal, l, 0.0)
                e = z + l + ((lts[hd] - cls[hd]) - _split_dot(l, tri))
                if masked:
                    e = jnp.where(causal, e, -1e30)
                a = jnp.exp(e)
                g = lax.dot_general(dos[hd], vblk, NT, preferred_element_type=F32) * a
                p = cgs[hd] + jnp.dot(g.astype(BF16), tri, preferred_element_type=F32) - g
                el = jnp.exp(l)
                dz = g * el - p * (1.0 - el)
                if masked:
                    dz = jnp.where(causal, dz, 0.0)
                dzb = (dz * scale).astype(BF16)
                dq = dq + jnp.dot(dzb, ks[hd], preferred_element_type=F32)
                dk = dk + lax.dot_general(dzb, qm[hd], TN, preferred_element_type=F32)
                dv = dv + lax.dot_general(a.astype(BF16), dos[hd], TN, preferred_element_type=F32)
                new_cls.append(cls[hd] + jnp.sum(l, axis=1, keepdims=True))
                new_cgs.append(cgs[hd] + jnp.sum(g, axis=1, keepdims=True))
            dk_ref[pl.ds(off, t), :] += dk
            dv_ref[pl.ds(off, t), :] += dv
            return tuple(new_cls), tuple(new_cgs), dq

        zero = jnp.zeros((t, 1), F32)
        init = ((zero, zero), (zero, zero), jnp.zeros((t, LANES), F32))
        carry = lax.fori_loop(0, i, lambda kb, cr: step(kb, cr, False), init)
        carry = step(i, carry, True)
        dq_ref[...] = carry[2]
        finish_exchange()

    blk = pl.BlockSpec((t, LANES), lambda p, i: (i, p))
    whole = pl.BlockSpec((s, LANES), lambda p, i: (0, p))
    res = pl.pallas_call(
        body, name="attn_bwd", grid=(hp, s // t),
        in_specs=[blk, whole, whole, blk, pl.BlockSpec((None, t, LANES), lambda p, i: (p, i, 0))] + [ANY] * nx,
        out_specs=[blk, whole, whole] + [ANY] * nx,
        out_shape=[_sds(q.shape)] * 3 + x_shape,
        scratch_shapes=x_scratch,
        compiler_params=_params(("arbitrary", "arbitrary")),
    )(q, k, v, do, ltot, *x_arrs)
    return res[0], res[1], res[2], res[3:]


LATE = ["w_conv_branch", "w_att_branch", "w_out", "w_ffn_up", "w_ffn_down"]


def _full_weight(name, gathered):
    return _cols_to_full(gathered) if name in COL_SHARDED else gathered.reshape(-1, gathered.shape[2])


def _grad_slabs(name, grad):
    return _full_to_cols(grad) if name in COL_SHARDED else grad.reshape(N_DEV, -1, grad.shape[1])


def _local_step(x, target, w, late_blocks):
    s = x.shape[0]
    w = dict(w)
    g1, g2, g3, g4 = w["norm_mix_pre"], w["norm_mix_post"], w["norm_ffn_pre"], w["norm_ffn_post"]

    (h1,) = _rowwise("pre_norm", lambda xt, g: ((_rms(xt, g),), ()), [x], [g1], [_sds((s, D_MODEL), BF16)])
    w_in = w["w_in"]
    cols = [w_in[:, IN_SPLITS[n]:IN_SPLITS[n + 1]] for n in range(6)]
    conv_in = _matmul(h1, cols[0], name="proj_conv")
    q = _matmul(h1, cols[1], name="proj_q", out_dtype=BF16)
    k = _matmul(h1, cols[2], name="proj_k", out_dtype=BF16)
    v = _matmul(h1, cols[3], name="proj_v", out_dtype=BF16)
    g_conv = _matmul(h1, cols[4], name="proj_gate_conv")
    g_att = _matmul(h1, cols[5], name="proj_gate_att")

    u3, u1 = _conv_fwd(conv_in, w["conv_dw_w"], w["conv_dw_b"], w["conv_ln_g"], w["conv_ln_b"])
    att, ltot, gathered = _attn_fwd(q, k, v, _gather_exchange(late_blocks))
    for nm, g in zip(LATE, gathered):
        w[nm] = _full_weight(nm, g)

    conv_pre = _matmul(u3, w["w_conv_branch"], name="conv_branch")
    att_out = _matmul(att, w["w_att_branch"], name="att_branch")
    (merged,) = _rowwise("merge", lambda *a: ((_merge(*a),), ()), [conv_pre, att_out, g_conv, g_att],
                         [w["b_conv_branch"]], [_sds((s, D_MODEL), BF16)])
    mix = _matmul(merged, w["w_out"], name="mix_out")

    def mid_fn(xt, mt, g2_, g3_):
        x2_ = xt + _rms(mt, g2_)
        return (x2_, _rms(x2_, g3_)), ()

    x2, h2 = _rowwise("mid_norm", mid_fn, [x, mix], [g2, g3], [_sds((s, D_MODEL)), _sds((s, D_MODEL), BF16)])
    w_up = w["w_ffn_up"]
    gate = _matmul(h2, w_up[:, :D_FF], name="ffn_gate")
    up = _matmul(h2, w_up[:, D_FF:], name="ffn_up")
    (act,) = _rowwise("swiglu", lambda g_, u_: ((_silu(g_) * u_,), ()), [gate, up], [], [_sds((s, D_FF), BF16)], tm=128)
    ff = _matmul(act, w["w_ffn_down"], name="ffn_down")

    def final_fn(x2t, fft, tgt, g4_):
        n4, vjp = jax.vjp(_rms, fft, g4_)
        err = x2t + n4 - tgt
        dy = err * (1.0 / D_MODEL)
        dff, dg4 = vjp(dy)
        return (dy, dff), (jnp.sum(err * err, axis=0, keepdims=True), dg4)

    dy, dff, loss_cols, d_g4 = _rowwise("final", final_fn, [x2, ff, target], [g4],
                                        [_sds((s, D_MODEL)), _sds((s, D_MODEL), BF16)],
                                        [_sds((1, D_MODEL)), _sds((1, D_MODEL))])
    loss = 0.5 * jnp.sum(loss_cols) / D_MODEL

    d_act = _matmul(dff, w["w_ffn_down"], tb=True, name="d_act")
    d_w_down = _matmul(act, dff, ta=True, name="d_w_down", out_dtype=BF16)

    def swiglu_bwd_fn(g_, u_, da_):
        _, vjp = jax.vjp(lambda a, b: _silu(a) * b, g_, u_)
        return vjp(da_), ()

    d_gate, d_up = _rowwise("swiglu_bwd", swiglu_bwd_fn, [gate, up, d_act], [],
                            [_sds((s, D_FF), BF16), _sds((s, D_FF), BF16)], tm=128)
    dh2a = _matmul(d_gate, w_up[:, :D_FF], tb=True, name="d_h2_gate")
    dh2b = _matmul(d_up, w_up[:, D_FF:], tb=True, name="d_h2_up")
    d_w_gate = _matmul(h2, d_gate, ta=True, name="d_w_gate", out_dtype=BF16)
    d_w_up = _matmul(h2, d_up, ta=True, name="d_w_up", out_dtype=BF16)

    def mid_bwd_fn(xt, mt, dyt, da, db, g2_, g3_):
        n2, vjp2 = jax.vjp(_rms, mt, g2_)
        x2_ = xt + n2
        _, vjp3 = jax.vjp(_rms, x2_, g3_)
        dx2_, dg3 = vjp3(da + db)
        dx2_ = dx2_ + dyt
        dmix_, dg2 = vjp2(dx2_)
        return (dx2_, dmix_), (dg2, dg3)

    dx2, dmix, d_g2, d_g3 = _rowwise("mid_bwd", mid_bwd_fn, [x, mix, dy, dh2a, dh2b], [g2, g3],
                                     [_sds((s, D_MODEL)), _sds((s, D_MODEL), BF16)],
                                     [_sds((1, D_MODEL)), _sds((1, D_MODEL))])
    d_merged = _matmul(dmix, w["w_out"], tb=True, name="d_merged")
    d_w_out = _matmul(merged, dmix, ta=True, name="d_w_out", out_dtype=BF16)

    def merge_bwd_fn(cp, ao, gc, ga, dm, b_cb):
        _, vjp = jax.vjp(_merge, cp, ao, gc, ga, b_cb)
        dcp, dao, dgc, dga, dbias = vjp(dm)
        return (dcp, dao, dgc, dga), (dbias,)

    d_conv_out, d_att_out, d_g_conv, d_g_att, d_b_cb = _rowwise(
        "merge_bwd", merge_bwd_fn, [conv_pre, att_out, g_conv, g_att, d_merged], [w["b_conv_branch"]],
        [_sds((s, D_MODEL), BF16)] * 4, [_sds((1, D_MODEL))])

    du3 = _matmul(d_conv_out, w["w_conv_branch"], tb=True, name="d_u3")
    d_w_cb = _matmul(u3, d_conv_out, ta=True, name="d_w_conv_branch", out_dtype=BF16)
    d_att = _matmul(d_att_out, w["w_att_branch"], tb=True, name="d_att", out_dtype=BF16)
    d_w_ab = _matmul(att, d_att_out, ta=True, name="d_w_att_branch", out_dtype=BF16)

    late_grads = {"w_conv_branch": d_w_cb, "w_att_branch": d_w_ab, "w_out": d_w_out,
                  "w_ffn_up": jnp.concatenate([d_w_gate, d_w_up], axis=1), "w_ffn_down": d_w_down}
    dq, dk, dv, received = _attn_bwd(q, k, v, d_att, ltot,
                                        _scatter_exchange([_grad_slabs(nm, late_grads[nm]) for nm in LATE]))

    def ln_bwd_fn(u1t, du3t, g_, b_):
        _, vjp = jax.vjp(_ln_silu, u1t, g_, b_)
        du1_, dg_, db_ = vjp(du3t)
        return (du1_,), (dg_, db_)

    du1, d_ln_g, d_ln_b = _rowwise("conv_ln_bwd", ln_bwd_fn, [u1, du3], [w["conv_ln_g"], w["conv_ln_b"]],
                                   [_sds((s, CONV_DIM))], [_sds((1, CONV_DIM)), _sds((1, CONV_DIM))])
    d_conv_in, d_dw_w, d_dw_b = _conv_bwd(conv_in, du1, w["conv_dw_w"])

    d_proj = jnp.concatenate([d_conv_in, dq.astype(BF16), dk.astype(BF16), dv.astype(BF16), d_g_conv, d_g_att],
                             axis=1)
    dh1 = _matmul(d_proj, w_in, tb=True, name="d_h1")
    d_w_in = _matmul(h1, d_proj, ta=True, name="d_w_in", out_dtype=BF16)

    def pre_bwd_fn(xt, dh, dx2t, g_):
        _, vjp = jax.vjp(_rms, xt, g_)
        dx_, dg_ = vjp(dh)
        return (dx_ + dx2t,), (dg_,)

    grad_x, d_g1 = _rowwise("pre_bwd", pre_bwd_fn, [x, dh1, dx2], [g1], [_sds((s, D_MODEL))], [_sds((1, D_MODEL))])

    grads = {
        "norm_mix_pre": d_g1, "w_in": d_w_in, "conv_dw_w": d_dw_w, "conv_dw_b": d_dw_b,
        "conv_ln_g": d_ln_g, "conv_ln_b": d_ln_b, "b_conv_branch": d_b_cb,
        "norm_mix_post": d_g2, "norm_ffn_pre": d_g3, "norm_ffn_post": d_g4,
    }
    return loss, grad_x, received, grads


def _place():
    x, y, c = lax.axis_index("x"), lax.axis_index("y"), lax.axis_index("c")
    return x, y, c


def _slot(px, py, pc):
    return 4 * px + 2 * py + pc


def _exchange_scratch(n):
    return [pltpu.SemaphoreType.DMA((7 * n,)), pltpu.SemaphoreType.DMA((7 * n,)), pltpu.SemaphoreType.DMA((n,))]


def _gather_exchange(arrs):
    n = len(arrs)

    def phases(ins, outs, send_sems, recv_sems, local_sems):
        x, y, c = _place()
        me, sibling = (x, y, c), (x, y, 1 - c)
        chips = [(1 - x, y), (x, 1 - y), (1 - x, 1 - y)]

        def copy(a, kk, block, to, src=None):
            dst = outs[a].at[_slot(*block)]
            return pltpu.make_async_remote_copy(
                src_ref=dst if src is None else src, dst_ref=dst,
                send_sem=send_sems.at[a * 7 + kk], recv_sem=recv_sems.at[a * 7 + kk],
                device_id=to, device_id_type=MESH)

        mine = [pltpu.make_async_copy(ins[a], outs[a].at[_slot(*me)], local_sems.at[a]) for a in range(n)]
        first = []
        for a in range(n):
            first.append(copy(a, 0, me, sibling, src=ins[a]))
            first += [copy(a, 1 + j, me, (*chip, c), src=ins[a]) for j, chip in enumerate(chips)]
        passed = [copy(a, 4 + j, (*chip, c), sibling) for j, chip in enumerate(chips) for a in range(n)]

        def send():
            for cp in mine + first:
                cp.start()

        def pass_on():
            for j, chip in enumerate(chips):
                for a in range(n):
                    copy(a, 1 + j, (*chip, c), me).wait_recv()
                    passed[j * n + a].start()

        def finish():
            for a in range(n):
                copy(a, 0, sibling, me).wait_recv()
                for j, chip in enumerate(chips):
                    copy(a, 4 + j, (*chip, 1 - c), me).wait_recv()
            for cp in first + passed:
                cp.wait_send()
            for cp in mine:
                cp.wait()

        return [send, pass_on, finish]

    return list(arrs), [_sds((N_DEV,) + a.shape, a.dtype) for a in arrs], _exchange_scratch(n), phases


def _scatter_exchange(arrs):
    n = len(arrs)
    flips = [(fx, fy, fc) for fx in (0, 1) for fy in (0, 1) for fc in (0, 1)][1:]

    def phases(ins, outs, send_sems, recv_sems, local_sems):
        x, y, c = _place()
        mine = _slot(x, y, c)
        local = [pltpu.make_async_copy(ins[a].at[mine], outs[a].at[mine], local_sems.at[a]) for a in range(n)]
        peers = [((1 - x) if fx else x, (1 - y) if fy else y, (1 - c) if fc else c) for fx, fy, fc in flips]

        def copy(a, kk, src_slot, dst_slot):
            return pltpu.make_async_remote_copy(
                src_ref=ins[a].at[src_slot], dst_ref=outs[a].at[dst_slot],
                send_sem=send_sems.at[a * 7 + kk], recv_sem=recv_sems.at[a * 7 + kk],
                device_id=peers[kk], device_id_type=MESH)

        sends = [copy(a, kk, _slot(*peers[kk]), mine) for a in range(n) for kk in range(7)]

        def send():
            for cp in local + sends:
                cp.start()

        def finish():
            for a in range(n):
                for kk in range(7):
                    copy(a, kk, mine, _slot(*peers[kk])).wait_recv()
            for cp in sends:
                cp.wait_send()
            for cp in local:
                cp.wait()

        return [send, finish]

    return list(arrs), [_sds(a.shape, a.dtype) for a in arrs], _exchange_scratch(n), phases


def _exchange_call(name, exchange):
    arrs, out_shape, scratch, phases = exchange
    n = len(arrs)

    def body(*refs):
        for step in phases(refs[:n], refs[n:2 * n], *refs[2 * n:]):
            step()

    return pl.pallas_call(body, name=name, in_specs=[ANY] * n, out_specs=[ANY] * n,
                          out_shape=out_shape, scratch_shapes=scratch)(*arrs)


def _carry_exchange(exchange, refs, n_in, n_out, first, middle, last):
    arrs, _, _, phases = exchange
    n = len(arrs)
    ins = refs[n_in:n_in + n]
    outs = refs[n_in + n + n_out:n_in + 2 * n + n_out]
    steps = phases(ins, outs, *refs[n_in + 2 * n + n_out:])
    pl.when(first)(steps[0])
    if len(steps) == 3:
        pl.when(middle)(steps[1])
    return lambda: pl.when(last)(steps[-1])


def _adamw_math(w, g, m, v):
    m2 = ADAM_B1 * m + (1.0 - ADAM_B1) * g
    v2 = ADAM_B2 * v + (1.0 - ADAM_B2) * jnp.square(g)
    m_hat = m2 / (1.0 - ADAM_B1 ** ADAM_STEP)
    v_hat = v2 / (1.0 - ADAM_B2 ** ADAM_STEP)
    delta = -ADAM_LR * (m_hat / (jnp.sqrt(v_hat) + ADAM_EPS) + ADAM_WD * w)
    return delta, m2, v2


def _sum_adamw(name, parts, w, m, v, tr=256):
    p, r, c = parts.shape
    tr = _pick(r, tr, 16)

    def body(p_ref, w_ref, m_ref, v_ref, g_ref, d_ref, m2_ref, v2_ref):
        g = p_ref[0].astype(F32)
        for d in range(1, p):
            g = g + p_ref[d].astype(F32)
        delta, m2, v2 = _adamw_math(w_ref[...], g, m_ref[...], v_ref[...])
        g_ref[...] = g
        d_ref[...] = delta
        m2_ref[...] = m2
        v2_ref[...] = v2

    tile = pl.BlockSpec((tr, c), lambda i: (i, 0))
    return pl.pallas_call(
        body, name=name, grid=(r // tr,),
        in_specs=[pl.BlockSpec((p, tr, c), lambda i: (0, i, 0)), tile, tile, tile],
        out_specs=[tile] * 4, out_shape=[_sds((r, c))] * 4,
        compiler_params=_params(("parallel",)),
    )(parts, w, m, v)


def _sum_parts(name, parts):
    p, r, c = parts.shape

    def body(p_ref, o_ref):
        g = p_ref[0]
        for d in range(1, p):
            g = g + p_ref[d]
        o_ref[...] = g

    return pl.pallas_call(
        body, name=name, out_shape=_sds((r, c)),
        in_specs=[pl.BlockSpec(memory_space=pltpu.VMEM)], out_specs=pl.BlockSpec(memory_space=pltpu.VMEM),
    )(parts)


WEIGHTS = ["norm_mix_pre", "w_in", "conv_dw_w", "conv_dw_b", "conv_ln_g", "conv_ln_b", "w_conv_branch",
           "b_conv_branch", "w_att_branch", "w_out", "norm_mix_post", "norm_ffn_pre", "w_ffn_up", "w_ffn_down",
           "norm_ffn_post"]
COL_SHARDED = ["w_in", "w_conv_branch", "w_att_branch", "w_ffn_up"]
ROW_SHARDED = ["w_out", "w_ffn_down"]
VECTORS = ["norm_mix_pre", "conv_dw_b", "conv_ln_g", "conv_ln_b", "b_conv_branch", "norm_mix_post",
           "norm_ffn_pre", "norm_ffn_post"]


def _cols_to_full(g):
    return g.transpose(1, 0, 2).reshape(g.shape[1], N_DEV * g.shape[2])


def _full_to_cols(f):
    return f.reshape(f.shape[0], N_DEV, f.shape[1] // N_DEV).transpose(1, 0, 2)


def _pack_vectors(vecs):
    rows = [jnp.pad(vecs[nm].reshape(-1), (0, D_MODEL - vecs[nm].size)) for nm in VECTORS]
    return jnp.stack(rows)


def _unpack_vectors(packed, sizes):
    return {nm: packed[n, :sizes[nm]] for n, nm in enumerate(VECTORS)}


def kernel(x, norm_mix_pre, w_in, conv_dw_w, conv_dw_b, conv_ln_g, conv_ln_b, w_conv_branch, b_conv_branch, w_att_branch, w_out, norm_mix_post, norm_ffn_pre, w_ffn_up, w_ffn_down, norm_ffn_post, loss_target, m_norm_mix_pre, m_w_in, m_conv_dw_w, m_conv_dw_b, m_conv_ln_g, m_conv_ln_b, m_w_conv_branch, m_b_conv_branch, m_w_att_branch, m_w_out, m_norm_mix_post, m_norm_ffn_pre, m_w_ffn_up, m_w_ffn_down, m_norm_ffn_post, v_norm_mix_pre, v_w_in, v_conv_dw_w, v_conv_dw_b, v_conv_ln_g, v_conv_ln_b, v_w_conv_branch, v_b_conv_branch, v_w_att_branch, v_w_out, v_norm_mix_post, v_norm_ffn_pre, v_w_ffn_up, v_w_ffn_down, v_norm_ffn_post):
    ws = dict(zip(WEIGHTS, [norm_mix_pre, w_in, conv_dw_w, conv_dw_b, conv_ln_g, conv_ln_b, w_conv_branch,
                            b_conv_branch, w_att_branch, w_out, norm_mix_post, norm_ffn_pre, w_ffn_up, w_ffn_down,
                            norm_ffn_post]))
    ms = dict(zip(WEIGHTS, [m_norm_mix_pre, m_w_in, m_conv_dw_w, m_conv_dw_b, m_conv_ln_g, m_conv_ln_b,
                            m_w_conv_branch, m_b_conv_branch, m_w_att_branch, m_w_out, m_norm_mix_post,
                            m_norm_ffn_pre, m_w_ffn_up, m_w_ffn_down, m_norm_ffn_post]))
    vs = dict(zip(WEIGHTS, [v_norm_mix_pre, v_w_in, v_conv_dw_w, v_conv_dw_b, v_conv_ln_g, v_conv_ln_b,
                            v_w_conv_branch, v_b_conv_branch, v_w_att_branch, v_w_out, v_norm_mix_post,
                            v_norm_ffn_pre, v_w_ffn_up, v_w_ffn_down, v_norm_ffn_post]))

    dw_block = jnp.pad(conv_dw_w, ((0, 1), (0, 0)))
    g_in, g_dw = _exchange_call("gather_first", _gather_exchange([w_in.astype(BF16), dw_block]))
    full = {"w_in": _full_weight("w_in", g_in), "conv_dw_w": _cols_to_full(g_dw)}
    for nm in VECTORS:
        full[nm] = ws[nm].reshape(1, -1)

    loss_local, grad_x, received, grads = _local_step(x[0], loss_target[0], full, [ws[nm].astype(BF16) for nm in LATE])
    loss = lax.psum(loss_local, ("x", "y", "c"))

    (recv_in,) = _exchange_call("scatter_last", _scatter_exchange([_grad_slabs("w_in", grads["w_in"])]))
    small = _exchange_call("gather_small_grads", _gather_exchange([_pack_vectors(grads), grads["conv_dw_w"]]))

    out_g, out_d, out_m, out_v = {}, {}, {}, {}
    for nm, parts in zip(LATE + ["w_in"], list(received) + [recv_in]):
        out_g[nm], out_d[nm], out_m[nm], out_v[nm] = _sum_adamw("adamw_" + nm, parts, ws[nm], ms[nm], vs[nm])
    sizes = {nm: ws[nm].size for nm in VECTORS}
    vec = _sum_adamw("adamw_vectors", small[0], _pack_vectors(ws), _pack_vectors(ms), _pack_vectors(vs))
    for res, dst in zip(vec, (out_g, out_d, out_m, out_v)):
        dst.update(_unpack_vectors(res, sizes))
    dw_full = _sum_parts("sum_dw_grads", small[1])
    me = _slot(*_place())
    dw_mine = lax.dynamic_slice(dw_full, (0, me * (CONV_DIM // N_DEV)), (CONV_WIDTH, CONV_DIM // N_DEV))
    nm = "conv_dw_w"
    out_g[nm], out_d[nm], out_m[nm], out_v[nm] = _sum_adamw("adamw_dw", dw_mine[None], ws[nm], ms[nm], vs[nm])

    outs = [loss, grad_x[None]]
    for group in (out_g, out_d, out_m, out_v):
        outs += [group[nm] for nm in WEIGHTS]
    return tuple(outs)
```

```python
import functools
import math

import jax
import jax.numpy as jnp
from jax import lax
from jax.experimental import pallas as pl
from jax.experimental.pallas import tpu as pltpu

F32 = jnp.float32
BF16 = jnp.bfloat16

N_DEV = 8
D_MODEL = 1024
CONV_DIM = 512
CONV_WIDTH = 31
N_HEADS = 8
HEAD_DIM = 64
ATT_DIM = N_HEADS * HEAD_DIM
D_FF = 2816
EPS = 1e-6
IN_SPLITS = (0, 1024, 1536, 2048, 2560, 3584, 4608)

ADAM_LR = 0.001
ADAM_B1 = 0.9
ADAM_B2 = 0.999
ADAM_EPS = 1e-08
ADAM_WD = 0.01
ADAM_STEP = 10

LANES = 128
SUBLANES = 8
HALO = 32
ATT_TILE = 256
VMEM_LIMIT = 56 * 1024 * 1024
MESH = pl.DeviceIdType.MESH
ANY = pl.BlockSpec(memory_space=pl.ANY)


def _pick(dim, target, align=LANES):
    t = min(dim, target)
    t -= t % align
    while t >= align:
        if dim % t == 0:
            return t
        t -= align
    return dim


def _params(semantics):
    return pltpu.CompilerParams(dimension_semantics=semantics, vmem_limit_bytes=VMEM_LIMIT)


def _matmul(a, b, *, name, ta=False, tb=False, out_dtype=F32, cols=None):
    m, k = (a.shape[1], a.shape[0]) if ta else a.shape
    n, k2 = b.shape if tb else (b.shape[1], b.shape[0])
    assert k == k2, (a.shape, b.shape, ta, tb)
    col0 = 0
    if cols is not None:
        assert not tb
        col0, n = cols
    tm, tk = _pick(m, 1408 if ta else 512), _pick(k, 1536)
    tn = _pick(math.gcd(n, col0) if col0 else n, 1536)
    nk = k // tk
    j0 = col0 // tn
    dims = (((0 if ta else 1,), (1 if tb else 0,)), ((), ()))

    def body(a_ref, b_ref, o_ref, *acc):
        part = lax.dot_general(a_ref[...], b_ref[...], dims, preferred_element_type=F32)
        if nk == 1:
            o_ref[...] = part.astype(o_ref.dtype)
            return
        acc_ref, = acc
        kk = pl.program_id(2)

        @pl.when(kk == 0)
        def _():
            acc_ref[...] = part

        @pl.when((kk > 0) & (kk < nk - 1))
        def _():
            acc_ref[...] += part

        @pl.when(kk == nk - 1)
        def _():
            o_ref[...] = (acc_ref[...] + part).astype(o_ref.dtype)

    a_spec = pl.BlockSpec((tk, tm), lambda j, i, kk: (kk, i)) if ta else pl.BlockSpec((tm, tk), lambda j, i, kk: (i, kk))
    b_spec = (pl.BlockSpec((tn, tk), lambda j, i, kk: (j, kk)) if tb
              else pl.BlockSpec((tk, tn), lambda j, i, kk: (kk, j + j0)))
    return pl.pallas_call(
        body, name=name, grid=(n // tn, m // tm, nk),
        in_specs=[a_spec, b_spec],
        out_specs=pl.BlockSpec((tm, tn), lambda j, i, kk: (i, j)),
        out_shape=jax.ShapeDtypeStruct((m, n), out_dtype),
        scratch_shapes=[pltpu.VMEM((tm, tn), F32)] if nk > 1 else [],
        compiler_params=_params(("parallel", "parallel", "arbitrary")),
    )(a, b)


def _rowwise(name, fn, rows, bcasts, row_outs, red_outs=(), tm=256):
    s = rows[0].shape[0]
    tm = _pick(s, tm, 16)
    resident = pl.Buffered(1)
    nr, nb, no = len(rows), len(bcasts), len(row_outs)

    def body(*refs):
        ins = [r[...] for r in refs[:nr + nb]]
        outs, reds = fn(*ins)
        for ref, val in zip(refs[nr + nb:nr + nb + no], outs):
            ref[...] = val.astype(ref.dtype)
        i = pl.program_id(0)
        for ref, val in zip(refs[nr + nb + no:], reds):
            @pl.when(i == 0)
            def _():
                ref[...] = val

            @pl.when(i > 0)
            def _():
                ref[...] += val

    in_specs = [pl.BlockSpec((tm, r.shape[1]), lambda i: (i, 0)) for r in rows]
    in_specs += [pl.BlockSpec(b.shape, lambda i: (0, 0), pipeline_mode=resident) for b in bcasts]
    out_specs = [pl.BlockSpec((tm, o.shape[1]), lambda i: (i, 0)) for o in row_outs]
    out_specs += [pl.BlockSpec(d.shape, lambda i: (0, 0)) for d in red_outs]
    return pl.pallas_call(
        body, name=name, grid=(s // tm,), in_specs=in_specs, out_specs=out_specs,
        out_shape=list(row_outs) + list(red_outs),
        compiler_params=_params(("arbitrary",)),
    )(*rows, *bcasts)


def _sds(shape, dtype=F32):
    return jax.ShapeDtypeStruct(shape, dtype)


def _rms(x, g):
    y = x * lax.rsqrt(jnp.mean(x * x, axis=-1, keepdims=True) + EPS)
    return y * g


def _silu(x):
    return x * jax.nn.sigmoid(x)


def _swiglu(g, u):
    return _silu(g) * u


def _ln_silu(u, g, b):
    mu = jnp.mean(u, axis=-1, keepdims=True)
    var = jnp.mean(jnp.square(u - mu), axis=-1, keepdims=True)
    return _silu((u - mu) * lax.rsqrt(var + EPS) * g + b)


def _merge(conv_pre, att_out, g_conv, g_att, b_cb):
    return jax.nn.sigmoid(g_conv) * (conv_pre + b_cb) + jax.nn.sigmoid(g_att) * att_out


def _glu(t):
    return t[:, :CONV_DIM] * jax.nn.sigmoid(t[:, CONV_DIM:])


def _conv_fwd(conv_in, w_pad, b, ln_g, ln_b, tm=256):
    s = conv_in.shape[0]
    tm = _pick(s, tm, HALO)
    ratio = tm // HALO

    def body(main_ref, halo_ref, w_ref, b_ref, g_ref, be_ref, u3_ref, u1_ref, buf):
        i = pl.program_id(0)
        buf[0:HALO, :] = _glu(halo_ref[...]) * (i > 0).astype(F32)
        buf[HALO:HALO + tm, :] = _glu(main_ref[...])
        acc = jnp.zeros((tm, CONV_DIM), F32) + b_ref[...]
        for j in range(CONV_WIDTH):
            acc = acc + w_ref[j:j + 1, :] * buf[pl.ds(HALO - (CONV_WIDTH - 1) + j, tm), :]
        u1_ref[...] = acc
        u3_ref[...] = _ln_silu(acc, g_ref[...], be_ref[...]).astype(u3_ref.dtype)

    return pl.pallas_call(
        body, name="conv_fwd", grid=(s // tm,),
        in_specs=[pl.BlockSpec((tm, 2 * CONV_DIM), lambda i: (i, 0)),
                  pl.BlockSpec((HALO, 2 * CONV_DIM), lambda i: (jnp.maximum(i * ratio - 1, 0), 0)),
                  pl.BlockSpec(w_pad.shape, lambda i: (0, 0)),
                  pl.BlockSpec(b.shape, lambda i: (0, 0)),
                  pl.BlockSpec(ln_g.shape, lambda i: (0, 0)),
                  pl.BlockSpec(ln_b.shape, lambda i: (0, 0))],
        out_specs=[pl.BlockSpec((tm, CONV_DIM), lambda i: (i, 0)),
                   pl.BlockSpec((tm, CONV_DIM), lambda i: (i, 0))],
        out_shape=[_sds((s, CONV_DIM), BF16), _sds((s, CONV_DIM), F32)],
        scratch_shapes=[pltpu.VMEM((tm + HALO, CONV_DIM), F32)],
        compiler_params=_params(("arbitrary",)),
    )(conv_in, conv_in, w_pad, b, ln_g, ln_b)


def _conv_bwd(conv_in, du1, w_pad, tm=256):
    s = conv_in.shape[0]
    tm = _pick(s, tm, HALO)
    ratio = tm // HALO
    nt = s // tm
    last_halo = s // HALO - 1

    def body(main_ref, halo_ref, du_ref, dun_ref, w_ref, dci_ref, dw_ref, db_ref, ubuf, dbuf):
        i = pl.program_id(0)
        main = main_ref[...]
        a = main[:, :CONV_DIM]
        sb = jax.nn.sigmoid(main[:, CONV_DIM:])
        ubuf[0:HALO, :] = _glu(halo_ref[...]) * (i > 0).astype(F32)
        ubuf[HALO:HALO + tm, :] = a * sb
        du = du_ref[...]
        dbuf[0:tm, :] = du
        dbuf[tm:tm + HALO, :] = dun_ref[...] * (i < nt - 1).astype(F32)

        @pl.when(i == 0)
        def _():
            dw_ref[...] = jnp.zeros_like(dw_ref)
            db_ref[...] = jnp.zeros_like(db_ref)

        du0 = jnp.zeros((tm, CONV_DIM), F32)
        for j in range(CONV_WIDTH):
            du0 = du0 + w_ref[j:j + 1, :] * dbuf[pl.ds(CONV_WIDTH - 1 - j, tm), :]
            dw_ref[j:j + 1, :] += jnp.sum(du * ubuf[pl.ds(HALO - (CONV_WIDTH - 1) + j, tm), :], axis=0, keepdims=True)
        db_ref[...] += jnp.sum(du, axis=0, keepdims=True)
        dci_ref[:, :CONV_DIM] = (du0 * sb).astype(dci_ref.dtype)
        dci_ref[:, CONV_DIM:] = (du0 * a * sb * (1.0 - sb)).astype(dci_ref.dtype)

    return pl.pallas_call(
        body, name="conv_bwd", grid=(nt,),
        in_specs=[pl.BlockSpec((tm, 2 * CONV_DIM), lambda i: (i, 0)),
                  pl.BlockSpec((HALO, 2 * CONV_DIM), lambda i: (jnp.maximum(i * ratio - 1, 0), 0)),
                  pl.BlockSpec((tm, CONV_DIM), lambda i: (i, 0)),
                  pl.BlockSpec((HALO, CONV_DIM), lambda i: (jnp.minimum((i + 1) * ratio, last_halo), 0)),
                  pl.BlockSpec(w_pad.shape, lambda i: (0, 0))],
        out_specs=[pl.BlockSpec((tm, 2 * CONV_DIM), lambda i: (i, 0)),
                   pl.BlockSpec(w_pad.shape, lambda i: (0, 0)),
                   pl.BlockSpec((1, CONV_DIM), lambda i: (0, 0))],
        out_shape=[_sds((s, 2 * CONV_DIM), BF16), _sds(w_pad.shape), _sds((1, CONV_DIM))],
        scratch_shapes=[pltpu.VMEM((tm + HALO, CONV_DIM), F32), pltpu.VMEM((tm + HALO, CONV_DIM), F32)],
        compiler_params=_params(("arbitrary",)),
    )(conv_in, conv_in, du1, du1, w_pad)


def _logsig_neg(z):
    return jnp.minimum(-z, 0.0) - jnp.log(1.0 + jnp.exp(-jnp.abs(z)))


def _split_dot(val, tri):
    hi = val.astype(BF16)
    lo = (val - hi.astype(F32)).astype(BF16)
    return jnp.dot(hi, tri, preferred_element_type=F32) + jnp.dot(lo, tri, preferred_element_type=F32)


def _attn_masks(t, later):
    row = lax.broadcasted_iota(jnp.int32, (t, t), 0)
    col = lax.broadcasted_iota(jnp.int32, (t, t), 1)
    tri = jnp.where(row > col if later else row <= col, 1.0, 0.0).astype(BF16)
    return col < row, tri


def _grid_marks(h, nq):
    hh, i = pl.program_id(0), pl.program_id(1)
    return (hh == 0) & (i == 0), (hh == (3 * h) // 4) & (i == 0), (hh == h - 1) & (i == nq - 1)


def _head_masks(shape):
    lane = lax.broadcasted_iota(jnp.int32, shape, len(shape) - 1)
    return lane < HEAD_DIM, lane >= HEAD_DIM


def _per_head(blk):
    m0, m1 = _head_masks(blk.shape)
    zero = jnp.zeros_like(blk)
    return jnp.where(m0, blk, zero), jnp.where(m1, blk, zero)


NT = (((1,), (1,)), ((), ()))
TN = (((0,), (0,)), ((), ()))


def _attn_fwd(q, k, v, exchange):
    s = q.shape[0]
    hp = q.shape[1] // LANES
    t = ATT_TILE
    scale = 1.0 / math.sqrt(HEAD_DIM)
    x_arrs, x_shape, x_scratch, _ = exchange
    nx = len(x_arrs)

    def body(*refs):
        q_ref, k_ref, v_ref = refs[:3]
        o_ref, lt_ref = refs[3 + nx:5 + nx]
        finish_exchange = _carry_exchange(exchange, refs, 3, 2, *_grid_marks(hp, s // t))
        i = pl.program_id(1)
        qs = _per_head((q_ref[...].astype(F32) * scale).astype(BF16))
        causal, tri = _attn_masks(t, later=True)

        def step(kb, carry, masked):
            cs, acc = carry
            off = pl.multiple_of(kb * t, t)
            kblk = k_ref[pl.ds(off, t), :]
            vs = _per_head(v_ref[pl.ds(off, t), :])
            new_cs = []
            for hd in range(2):
                z = lax.dot_general(qs[hd], kblk, NT, preferred_element_type=F32)
                l = _logsig_neg(z)
                if masked:
                    l = jnp.where(causal, l, 0.0)
                e = z + l + _split_dot(l, tri) + cs[hd]
                if masked:
                    e = jnp.where(causal, e, -1e30)
                acc = acc + jnp.dot(jnp.exp(e).astype(BF16), vs[hd], preferred_element_type=F32)
                new_cs.append(cs[hd] + jnp.sum(l, axis=1, keepdims=True))
            return tuple(new_cs), acc

        zero = jnp.zeros((t, 1), F32)
        carry = step(i, ((zero, zero), jnp.zeros((t, LANES), F32)), True)
        carry = lax.fori_loop(0, i, lambda n, cr: step(i - 1 - n, cr, False), carry)
        m0, _ = _head_masks((t, LANES))
        lt_ref[...] = jnp.where(m0, carry[0][0], carry[0][1])
        o_ref[...] = carry[1].astype(o_ref.dtype)
        finish_exchange()

    res = pl.pallas_call(
        body, name="attn_fwd", grid=(hp, s // t),
        in_specs=[pl.BlockSpec((t, LANES), lambda p, i: (i, p)),
                  pl.BlockSpec((s, LANES), lambda p, i: (0, p)),
                  pl.BlockSpec((s, LANES), lambda p, i: (0, p))] + [ANY] * nx,
        out_specs=[pl.BlockSpec((t, LANES), lambda p, i: (i, p)),
                   pl.BlockSpec((None, t, LANES), lambda p, i: (p, i, 0))] + [ANY] * nx,
        out_shape=[_sds(q.shape, BF16), _sds((hp, s, LANES), F32)] + x_shape,
        scratch_shapes=x_scratch,
        compiler_params=_params(("arbitrary", "arbitrary")),
    )(q, k, v, *x_arrs)
    return res[0], res[1], res[2:]


def _attn_bwd(q, k, v, do, ltot, exchange):
    s = q.shape[0]
    hp = q.shape[1] // LANES
    t = ATT_TILE
    scale = 1.0 / math.sqrt(HEAD_DIM)
    x_arrs, x_shape, x_scratch, _ = exchange
    nx = len(x_arrs)

    def body(*refs):
        q_ref, k_ref, v_ref, do_ref, lt_ref = refs[:5]
        dq_ref, dk_ref, dv_ref = refs[5 + nx:8 + nx]
        finish_exchange = _carry_exchange(exchange, refs, 5, 3, *_grid_marks(hp, s // t))
        i = pl.program_id(1)

        @pl.when(i == 0)
        def _():
            dk_ref[...] = jnp.zeros_like(dk_ref)
            dv_ref[...] = jnp.zeros_like(dv_ref)

        qb = q_ref[...]
        qm = _per_head(qb)
        qs = _per_head((qb.astype(F32) * scale).astype(BF16))
        dos = _per_head(do_ref[...])
        lts = (lt_ref[:, 0:1], lt_ref[:, HEAD_DIM:HEAD_DIM + 1])
        causal, tri = _attn_masks(t, later=False)

        def step(kb, carry, masked):
            cls, cgs, dq = carry
            off = pl.multiple_of(kb * t, t)
            kblk = k_ref[pl.ds(off, t), :]
            vblk = v_ref[pl.ds(off, t), :]
            ks = _per_head(kblk)
            dk = jnp.zeros((t, LANES), F32)
            dv = jnp.zeros((t, LANES), F32)
            new_cls, new_cgs = [], []
            for hd in range(2):
                z = lax.dot_general(qs[hd], kblk, NT, preferred_element_type=F32)
                l = _logsig_neg(z)
                if masked:
                    l = jnp.where(causal, l, 0.0)
                e = z + l + ((lts[hd] - cls[hd]) - _split_dot(l, tri))
                if masked:
                    e = jnp.where(causal, e, -1e30)
                a = jnp.exp(e)
                g = lax.dot_general(dos[hd], vblk, NT, preferred_element_type=F32) * a
                p = cgs[hd] + jnp.dot(g.astype(BF16), tri, preferred_element_type=F32) - g
                el = jnp.exp(l)
                dz = g * el - p * (1.0 - el)
                if masked:
                    dz = jnp.where(causal, dz, 0.0)
                dzb = (dz * scale).astype(BF16)
                dq = dq + jnp.dot(dzb, ks[hd], preferred_element_type=F32)
                dk = dk + lax.dot_general(dzb, qm[hd], TN, preferred_element_type=F32)
                dv = dv + lax.dot_general(a.astype(BF16), dos[hd], TN, preferred_element_type=F32)
                new_cls.append(cls[hd] + jnp.sum(l, axis=1, keepdims=True))
                new_cgs.append(cgs[hd] + jnp.sum(g, axis=1, keepdims=True))
            dk_ref[pl.ds(off, t), :] += dk
            dv_ref[pl.ds(off, t), :] += dv
            return tuple(new_cls), tuple(new_cgs), dq

        zero = jnp.zeros((t, 1), F32)
        init = ((zero, zero), (zero, zero), jnp.zeros((t, LANES), F32))
        carry = lax.fori_loop(0, i, lambda kb, cr: step(kb, cr, False), init)
        carry = step(i, carry, True)
        dq_ref[...] = carry[2]
        finish_exchange()

    blk = pl.BlockSpec((t, LANES), lambda p, i: (i, p))
    whole = pl.BlockSpec((s, LANES), lambda p, i: (0, p))
    res = pl.pallas_call(
        body, name="attn_bwd", grid=(hp, s // t),
        in_specs=[blk, whole, whole, blk, pl.BlockSpec((None, t, LANES), lambda p, i: (p, i, 0))] + [ANY] * nx,
        out_specs=[blk, whole, whole] + [ANY] * nx,
        out_shape=[_sds(q.shape)] * 3 + x_shape,
        scratch_shapes=x_scratch,
        compiler_params=_params(("arbitrary", "arbitrary")),
    )(q, k, v, do, ltot, *x_arrs)
    return res[0], res[1], res[2], res[3:]


LATE = ["w_conv_branch", "w_att_branch", "w_out", "w_ffn_up", "w_ffn_down"]


def _full_weight(name, gathered):
    return _cols_to_full(gathered) if name in COL_SHARDED else gathered.reshape(-1, gathered.shape[2])


def _grad_slabs(name, grad):
    return _full_to_cols(grad) if name in COL_SHARDED else grad.reshape(N_DEV, -1, grad.shape[1])


def _local_step(x, target, w, late_blocks):
    s = x.shape[0]
    w = dict(w)
    g1, g2, g3, g4 = w["norm_mix_pre"], w["norm_mix_post"], w["norm_ffn_pre"], w["norm_ffn_post"]

    (h1,) = _rowwise("pre_norm", lambda xt, g: ((_rms(xt, g),), ()), [x], [g1], [_sds((s, D_MODEL), BF16)])
    w_in = w["w_in"]
    cols = [(IN_SPLITS[n], IN_SPLITS[n + 1] - IN_SPLITS[n]) for n in range(6)]
    conv_in = _matmul(h1, w_in, cols=cols[0], name="proj_conv")
    q = _matmul(h1, w_in, cols=cols[1], name="proj_q", out_dtype=BF16)
    k = _matmul(h1, w_in, cols=cols[2], name="proj_k", out_dtype=BF16)
    v = _matmul(h1, w_in, cols=cols[3], name="proj_v", out_dtype=BF16)
    g_conv = _matmul(h1, w_in, cols=cols[4], name="proj_gate_conv")
    g_att = _matmul(h1, w_in, cols=cols[5], name="proj_gate_att")

    u3, u1 = _conv_fwd(conv_in, w["conv_dw_w"], w["conv_dw_b"], w["conv_ln_g"], w["conv_ln_b"])
    att, ltot, gathered = _attn_fwd(q, k, v, _gather_exchange(late_blocks))
    for nm, g in zip(LATE, gathered):
        w[nm] = _full_weight(nm, g)

    def merge_fn(u3t, at, gc, ga, w_cb, w_ab, b_cb):
        cp = jnp.dot(u3t, w_cb, preferred_element_type=F32)
        ao = jnp.dot(at, w_ab, preferred_element_type=F32)
        return (_merge(cp, ao, gc, ga, b_cb), cp, ao), ()

    merged, conv_pre, att_out = _rowwise(
        "branch_merge", merge_fn, [u3, att, g_conv, g_att], [w["w_conv_branch"], w["w_att_branch"], w["b_conv_branch"]],
        [_sds((s, D_MODEL), BF16)] * 3, tm=512)

    def mid_fn(mt, xt, w_out, g2_, g3_):
        mix_ = jnp.dot(mt, w_out, preferred_element_type=F32)
        x2_ = xt + _rms(mix_, g2_)
        return (mix_, x2_, _rms(x2_, g3_)), ()

    mix, x2, h2 = _rowwise("mix_mid_norm", mid_fn, [merged, x], [w["w_out"], g2, g3],
                           [_sds((s, D_MODEL)), _sds((s, D_MODEL)), _sds((s, D_MODEL), BF16)], tm=512)

    def ffn_up_fn(ht, w_up):
        gu_ = jnp.dot(ht, w_up, preferred_element_type=F32)
        return (gu_, _swiglu(gu_[:, :D_FF], gu_[:, D_FF:])), ()

    gu, act = _rowwise("ffn_up", ffn_up_fn, [h2], [w["w_ffn_up"]],
                       [_sds((s, 2 * D_FF), BF16), _sds((s, D_FF), BF16)], tm=512)

    def final_fn(at, x2t, tgt, w_down, g4_):
        ff = jnp.dot(at, w_down, preferred_element_type=F32)
        n4, vjp = jax.vjp(_rms, ff, g4_)
        err = x2t + n4 - tgt
        dy = err * (1.0 / D_MODEL)
        dff, dg4 = vjp(dy)
        return (dy, dff), (jnp.sum(err * err, axis=0, keepdims=True), dg4)

    dy, dff, loss_cols, d_g4 = _rowwise("ffn_down_loss", final_fn, [act, x2, target], [w["w_ffn_down"], g4],
                                        [_sds((s, D_MODEL)), _sds((s, D_MODEL), BF16)],
                                        [_sds((1, D_MODEL)), _sds((1, D_MODEL))], tm=512)
    loss = 0.5 * jnp.sum(loss_cols) / D_MODEL

    d_w_down = _matmul(act, dff, ta=True, name="d_w_down", out_dtype=BF16)

    def act_bwd_fn(dfft, gut, w_down):
        d_act = lax.dot_general(dfft, w_down, NT, preferred_element_type=F32)
        gu_ = gut.astype(F32)
        _, vjp = jax.vjp(_swiglu, gu_[:, :D_FF], gu_[:, D_FF:])
        return (jnp.concatenate(vjp(d_act), axis=1),), ()

    (dgu,) = _rowwise("ffn_act_bwd", act_bwd_fn, [dff, gu], [w["w_ffn_down"]], [_sds((s, 2 * D_FF), BF16)])
    dh2 = _matmul(dgu, w["w_ffn_up"], tb=True, name="d_h2")
    d_w_up = _matmul(h2, dgu, ta=True, name="d_w_up", out_dtype=BF16)

    def mid_bwd_fn(xt, mt, dyt, dh, g2_, g3_):
        n2, vjp2 = jax.vjp(_rms, mt, g2_)
        x2_ = xt + n2
        _, vjp3 = jax.vjp(_rms, x2_, g3_)
        dx2_, dg3 = vjp3(dh)
        dx2_ = dx2_ + dyt
        dmix_, dg2 = vjp2(dx2_)
        return (dx2_, dmix_), (dg2, dg3)

    dx2, dmix, d_g2, d_g3 = _rowwise("mid_bwd", mid_bwd_fn, [x, mix, dy, dh2], [g2, g3],
                                     [_sds((s, D_MODEL)), _sds((s, D_MODEL), BF16)],
                                     [_sds((1, D_MODEL)), _sds((1, D_MODEL))])
    d_w_out = _matmul(merged, dmix, ta=True, name="d_w_out", out_dtype=BF16)

    def merge_bwd_fn(dmt, cp, ao, gc, ga, w_out, w_cb, w_ab, b_cb):
        dm = lax.dot_general(dmt, w_out, NT, preferred_element_type=F32)
        _, vjp = jax.vjp(_merge, cp.astype(F32), ao.astype(F32), gc, ga, b_cb)
        dcp, dao, dgc, dga, dbias = vjp(dm)
        dcp, dao = dcp.astype(BF16), dao.astype(BF16)
        du3_ = lax.dot_general(dcp, w_cb, NT, preferred_element_type=F32)
        datt_ = lax.dot_general(dao, w_ab, NT, preferred_element_type=F32)
        return (dcp, dao, dgc, dga, du3_, datt_), (dbias,)

    d_conv_out, d_att_out, d_g_conv, d_g_att, du3, d_att, d_b_cb = _rowwise(
        "merge_bwd", merge_bwd_fn, [dmix, conv_pre, att_out, g_conv, g_att],
        [w["w_out"], w["w_conv_branch"], w["w_att_branch"], w["b_conv_branch"]],
        [_sds((s, D_MODEL), BF16)] * 4 + [_sds((s, CONV_DIM)), _sds((s, ATT_DIM), BF16)], [_sds((1, D_MODEL))], tm=512)

    d_w_cb = _matmul(u3, d_conv_out, ta=True, name="d_w_conv_branch", out_dtype=BF16)
    d_w_ab = _matmul(att, d_att_out, ta=True, name="d_w_att_branch", out_dtype=BF16)

    late_grads = {"w_conv_branch": d_w_cb, "w_att_branch": d_w_ab, "w_out": d_w_out,
                  "w_ffn_up": d_w_up, "w_ffn_down": d_w_down}
    dq, dk, dv, received = _attn_bwd(q, k, v, d_att, ltot,
                                        _scatter_exchange([_grad_slabs(nm, late_grads[nm]) for nm in LATE]))

    def ln_bwd_fn(u1t, du3t, g_, b_):
        _, vjp = jax.vjp(_ln_silu, u1t, g_, b_)
        du1_, dg_, db_ = vjp(du3t)
        return (du1_,), (dg_, db_)

    du1, d_ln_g, d_ln_b = _rowwise("conv_ln_bwd", ln_bwd_fn, [u1, du3], [w["conv_ln_g"], w["conv_ln_b"]],
                                   [_sds((s, CONV_DIM))], [_sds((1, CONV_DIM)), _sds((1, CONV_DIM))])
    d_conv_in, d_dw_w, d_dw_b = _conv_bwd(conv_in, du1, w["conv_dw_w"])

    d_proj = jnp.concatenate([d_conv_in, dq.astype(BF16), dk.astype(BF16), dv.astype(BF16), d_g_conv, d_g_att],
                             axis=1)
    dh1 = _matmul(d_proj, w_in, tb=True, name="d_h1")
    d_w_in = _matmul(h1, d_proj, ta=True, name="d_w_in", out_dtype=BF16)

    def pre_bwd_fn(xt, dh, dx2t, g_):
        _, vjp = jax.vjp(_rms, xt, g_)
        dx_, dg_ = vjp(dh)
        return (dx_ + dx2t,), (dg_,)

    grad_x, d_g1 = _rowwise("pre_bwd", pre_bwd_fn, [x, dh1, dx2], [g1], [_sds((s, D_MODEL))], [_sds((1, D_MODEL))])

    grads = {
        "norm_mix_pre": d_g1, "w_in": d_w_in, "conv_dw_w": d_dw_w, "conv_dw_b": d_dw_b,
        "conv_ln_g": d_ln_g, "conv_ln_b": d_ln_b, "b_conv_branch": d_b_cb,
        "norm_mix_post": d_g2, "norm_ffn_pre": d_g3, "norm_ffn_post": d_g4,
    }
    return loss, grad_x, received, grads


def _place():
    x, y, c = lax.axis_index("x"), lax.axis_index("y"), lax.axis_index("c")
    return x, y, c


def _slot(px, py, pc):
    return 4 * px + 2 * py + pc


def _exchange_scratch(n):
    return [pltpu.SemaphoreType.DMA((7 * n,)), pltpu.SemaphoreType.DMA((7 * n,)), pltpu.SemaphoreType.DMA((n,))]


def _gather_exchange(arrs):
    n = len(arrs)

    def phases(ins, outs, send_sems, recv_sems, local_sems):
        x, y, c = _place()
        me, sibling = (x, y, c), (x, y, 1 - c)
        chips = [(1 - x, y), (x, 1 - y), (1 - x, 1 - y)]

        def copy(a, kk, block, to, src=None):
            dst = outs[a].at[_slot(*block)]
            return pltpu.make_async_remote_copy(
                src_ref=dst if src is None else src, dst_ref=dst,
                send_sem=send_sems.at[a * 7 + kk], recv_sem=recv_sems.at[a * 7 + kk],
                device_id=to, device_id_type=MESH)

        mine = [pltpu.make_async_copy(ins[a], outs[a].at[_slot(*me)], local_sems.at[a]) for a in range(n)]
        first = []
        for a in range(n):
            first.append(copy(a, 0, me, sibling, src=ins[a]))
            first += [copy(a, 1 + j, me, (*chip, c), src=ins[a]) for j, chip in enumerate(chips)]
        passed = [copy(a, 4 + j, (*chip, c), sibling) for j, chip in enumerate(chips) for a in range(n)]

        def send():
            for cp in mine + first:
                cp.start()

        def pass_on():
            for j, chip in enumerate(chips):
                for a in range(n):
                    copy(a, 1 + j, (*chip, c), me).wait_recv()
                    passed[j * n + a].start()

        def finish():
            for a in range(n):
                copy(a, 0, sibling, me).wait_recv()
                for j, chip in enumerate(chips):
                    copy(a, 4 + j, (*chip, 1 - c), me).wait_recv()
            for cp in first + passed:
                cp.wait_send()
            for cp in mine:
                cp.wait()

        return [send, pass_on, finish]

    return list(arrs), [_sds((N_DEV,) + a.shape, a.dtype) for a in arrs], _exchange_scratch(n), phases


def _scatter_exchange(arrs):
    n = len(arrs)
    flips = [(fx, fy, fc) for fx in (0, 1) for fy in (0, 1) for fc in (0, 1)][1:]

    def phases(ins, outs, send_sems, recv_sems, local_sems):
        x, y, c = _place()
        mine = _slot(x, y, c)
        local = [pltpu.make_async_copy(ins[a].at[mine], outs[a].at[mine], local_sems.at[a]) for a in range(n)]
        peers = [((1 - x) if fx else x, (1 - y) if fy else y, (1 - c) if fc else c) for fx, fy, fc in flips]

        def copy(a, kk, src_slot, dst_slot):
            return pltpu.make_async_remote_copy(
                src_ref=ins[a].at[src_slot], dst_ref=outs[a].at[dst_slot],
                send_sem=send_sems.at[a * 7 + kk], recv_sem=recv_sems.at[a * 7 + kk],
                device_id=peers[kk], device_id_type=MESH)

        sends = [copy(a, kk, _slot(*peers[kk]), mine) for a in range(n) for kk in range(7)]

        def send():
            for cp in local + sends:
                cp.start()

        def finish():
            for a in range(n):
                for kk in range(7):
                    copy(a, kk, mine, _slot(*peers[kk])).wait_recv()
            for cp in sends:
                cp.wait_send()
            for cp in local:
                cp.wait()

        return [send, finish]

    return list(arrs), [_sds(a.shape, a.dtype) for a in arrs], _exchange_scratch(n), phases


def _exchange_call(name, exchange):
    arrs, out_shape, scratch, phases = exchange
    n = len(arrs)

    def body(*refs):
        for step in phases(refs[:n], refs[n:2 * n], *refs[2 * n:]):
            step()

    return pl.pallas_call(body, name=name, in_specs=[ANY] * n, out_specs=[ANY] * n,
                          out_shape=out_shape, scratch_shapes=scratch)(*arrs)


def _carry_exchange(exchange, refs, n_in, n_out, first, middle, last):
    arrs, _, _, phases = exchange
    n = len(arrs)
    ins = refs[n_in:n_in + n]
    outs = refs[n_in + n + n_out:n_in + 2 * n + n_out]
    steps = phases(ins, outs, *refs[n_in + 2 * n + n_out:])
    pl.when(first)(steps[0])
    if len(steps) == 3:
        pl.when(middle)(steps[1])
    return lambda: pl.when(last)(steps[-1])


def _adamw_math(w, g, m, v):
    m2 = ADAM_B1 * m + (1.0 - ADAM_B1) * g
    v2 = ADAM_B2 * v + (1.0 - ADAM_B2) * jnp.square(g)
    m_hat = m2 / (1.0 - ADAM_B1 ** ADAM_STEP)
    v_hat = v2 / (1.0 - ADAM_B2 ** ADAM_STEP)
    delta = -ADAM_LR * (m_hat / (jnp.sqrt(v_hat) + ADAM_EPS) + ADAM_WD * w)
    return delta, m2, v2


def _sum_adamw(name, parts, w, m, v, tr=256):
    p, r, c = parts.shape
    tr = _pick(r, tr, 16)

    def body(p_ref, w_ref, m_ref, v_ref, g_ref, d_ref, m2_ref, v2_ref):
        g = p_ref[0].astype(F32)
        for d in range(1, p):
            g = g + p_ref[d].astype(F32)
        delta, m2, v2 = _adamw_math(w_ref[...], g, m_ref[...], v_ref[...])
        g_ref[...] = g
        d_ref[...] = delta
        m2_ref[...] = m2
        v2_ref[...] = v2

    tile = pl.BlockSpec((tr, c), lambda i: (i, 0))
    return pl.pallas_call(
        body, name=name, grid=(r // tr,),
        in_specs=[pl.BlockSpec((p, tr, c), lambda i: (0, i, 0)), tile, tile, tile],
        out_specs=[tile] * 4, out_shape=[_sds((r, c))] * 4,
        compiler_params=_params(("parallel",)),
    )(parts, w, m, v)


def _sum_parts(name, parts):
    p, r, c = parts.shape

    def body(p_ref, o_ref):
        g = p_ref[0]
        for d in range(1, p):
            g = g + p_ref[d]
        o_ref[...] = g

    return pl.pallas_call(
        body, name=name, out_shape=_sds((r, c)),
        in_specs=[pl.BlockSpec(memory_space=pltpu.VMEM)], out_specs=pl.BlockSpec(memory_space=pltpu.VMEM),
    )(parts)


WEIGHTS = ["norm_mix_pre", "w_in", "conv_dw_w", "conv_dw_b", "conv_ln_g", "conv_ln_b", "w_conv_branch",
           "b_conv_branch", "w_att_branch", "w_out", "norm_mix_post", "norm_ffn_pre", "w_ffn_up", "w_ffn_down",
           "norm_ffn_post"]
COL_SHARDED = ["w_in", "w_conv_branch", "w_att_branch", "w_ffn_up"]
ROW_SHARDED = ["w_out", "w_ffn_down"]
VECTORS = ["norm_mix_pre", "conv_dw_b", "conv_ln_g", "conv_ln_b", "b_conv_branch", "norm_mix_post",
           "norm_ffn_pre", "norm_ffn_post"]


def _cols_to_full(g):
    return g.transpose(1, 0, 2).reshape(g.shape[1], N_DEV * g.shape[2])


def _full_to_cols(f):
    return f.reshape(f.shape[0], N_DEV, f.shape[1] // N_DEV).transpose(1, 0, 2)


def _pack_vectors(vecs):
    rows = [jnp.pad(vecs[nm].reshape(-1), (0, D_MODEL - vecs[nm].size)) for nm in VECTORS]
    return jnp.stack(rows)


def _unpack_vectors(packed, sizes):
    return {nm: packed[n, :sizes[nm]] for n, nm in enumerate(VECTORS)}


def kernel(x, norm_mix_pre, w_in, conv_dw_w, conv_dw_b, conv_ln_g, conv_ln_b, w_conv_branch, b_conv_branch, w_att_branch, w_out, norm_mix_post, norm_ffn_pre, w_ffn_up, w_ffn_down, norm_ffn_post, loss_target, m_norm_mix_pre, m_w_in, m_conv_dw_w, m_conv_dw_b, m_conv_ln_g, m_conv_ln_b, m_w_conv_branch, m_b_conv_branch, m_w_att_branch, m_w_out, m_norm_mix_post, m_norm_ffn_pre, m_w_ffn_up, m_w_ffn_down, m_norm_ffn_post, v_norm_mix_pre, v_w_in, v_conv_dw_w, v_conv_dw_b, v_conv_ln_g, v_conv_ln_b, v_w_conv_branch, v_b_conv_branch, v_w_att_branch, v_w_out, v_norm_mix_post, v_norm_ffn_pre, v_w_ffn_up, v_w_ffn_down, v_norm_ffn_post):
    ws = dict(zip(WEIGHTS, [norm_mix_pre, w_in, conv_dw_w, conv_dw_b, conv_ln_g, conv_ln_b, w_conv_branch,
                            b_conv_branch, w_att_branch, w_out, norm_mix_post, norm_ffn_pre, w_ffn_up, w_ffn_down,
                            norm_ffn_post]))
    ms = dict(zip(WEIGHTS, [m_norm_mix_pre, m_w_in, m_conv_dw_w, m_conv_dw_b, m_conv_ln_g, m_conv_ln_b,
                            m_w_conv_branch, m_b_conv_branch, m_w_att_branch, m_w_out, m_norm_mix_post,
                            m_norm_ffn_pre, m_w_ffn_up, m_w_ffn_down, m_norm_ffn_post]))
    vs = dict(zip(WEIGHTS, [v_norm_mix_pre, v_w_in, v_conv_dw_w, v_conv_dw_b, v_conv_ln_g, v_conv_ln_b,
                            v_w_conv_branch, v_b_conv_branch, v_w_att_branch, v_w_out, v_norm_mix_post,
                            v_norm_ffn_pre, v_w_ffn_up, v_w_ffn_down, v_norm_ffn_post]))

    dw_block = jnp.pad(conv_dw_w, ((0, 1), (0, 0)))
    g_in, g_dw = _exchange_call("gather_first", _gather_exchange([w_in.astype(BF16), dw_block]))
    full = {"w_in": _full_weight("w_in", g_in), "conv_dw_w": _cols_to_full(g_dw)}
    for nm in VECTORS:
        full[nm] = ws[nm].reshape(1, -1)

    loss_local, grad_x, received, grads = _local_step(x[0], loss_target[0], full, [ws[nm].astype(BF16) for nm in LATE])
    loss = lax.psum(loss_local, ("x", "y", "c"))

    (recv_in,) = _exchange_call("scatter_last", _scatter_exchange([_grad_slabs("w_in", grads["w_in"])]))
    small = _exchange_call("gather_small_grads", _gather_exchange([_pack_vectors(grads), grads["conv_dw_w"]]))

    out_g, out_d, out_m, out_v = {}, {}, {}, {}
    for nm, parts in zip(LATE + ["w_in"], list(received) + [recv_in]):
        out_g[nm], out_d[nm], out_m[nm], out_v[nm] = _sum_adamw("adamw_" + nm, parts, ws[nm], ms[nm], vs[nm])
    sizes = {nm: ws[nm].size for nm in VECTORS}
    vec = _sum_adamw("adamw_vectors", small[0], _pack_vectors(ws), _pack_vectors(ms), _pack_vectors(vs))
    for res, dst in zip(vec, (out_g, out_d, out_m, out_v)):
        dst.update(_unpack_vectors(res, sizes))
    dw_full = _sum_parts("sum_dw_grads", small[1])
    me = _slot(*_place())
    dw_mine = lax.dynamic_slice(dw_full, (0, me * (CONV_DIM // N_DEV)), (CONV_WIDTH, CONV_DIM // N_DEV))
    nm = "conv_dw_w"
    out_g[nm], out_d[nm], out_m[nm], out_v[nm] = _sum_adamw("adamw_dw", dw_mine[None], ws[nm], ms[nm], vs[nm])

    outs = [loss, grad_x[None]]
    for group in (out_g, out_d, out_m, out_v):
        outs += [group[nm] for nm in WEIGHTS]
    return tuple(outs)
```

```python
import functools
import math

import jax
import jax.numpy as jnp
from jax import lax
from jax.experimental import pallas as pl
from jax.experimental.pallas import tpu as pltpu

F32 = jnp.float32
BF16 = jnp.bfloat16

N_DEV = 8
D_MODEL = 1024
CONV_DIM = 512
CONV_WIDTH = 31
N_HEADS = 8
HEAD_DIM = 64
ATT_DIM = N_HEADS * HEAD_DIM
D_FF = 2816
EPS = 1e-6
IN_SPLITS = (0, 1024, 1536, 2048, 2560, 3584, 4608)

ADAM_LR = 0.001
ADAM_B1 = 0.9
ADAM_B2 = 0.999
ADAM_EPS = 1e-08
ADAM_WD = 0.01
ADAM_STEP = 10

LANES = 128
SUBLANES = 8
HALO = 32
ATT_TILE = 256
VMEM_LIMIT = 56 * 1024 * 1024
MESH = pl.DeviceIdType.MESH
ANY = pl.BlockSpec(memory_space=pl.ANY)


def _pick(dim, target, align=LANES):
    t = min(dim, target)
    t -= t % align
    while t >= align:
        if dim % t == 0:
            return t
        t -= align
    return dim


def _params(semantics):
    return pltpu.CompilerParams(dimension_semantics=semantics, vmem_limit_bytes=VMEM_LIMIT)


def _matmul(a, b, *, name, ta=False, tb=False, out_dtype=F32, cols=None, deps=()):
    m, k = (a.shape[1], a.shape[0]) if ta else a.shape
    n, k2 = b.shape if tb else (b.shape[1], b.shape[0])
    assert k == k2, (a.shape, b.shape, ta, tb)
    col0 = 0
    if cols is not None:
        assert not tb
        col0, n = cols
    tm, tk = _pick(m, 1408 if ta else 512), _pick(k, 1536)
    tn = _pick(math.gcd(n, col0) if col0 else n, 1536)
    nk = k // tk
    j0 = col0 // tn
    dims = (((0 if ta else 1,), (1 if tb else 0,)), ((), ()))

    def body(a_ref, b_ref, *rest):
        o_ref = rest[len(deps)]
        part = lax.dot_general(a_ref[...], b_ref[...], dims, preferred_element_type=F32)
        if nk == 1:
            o_ref[...] = part.astype(o_ref.dtype)
            return
        acc_ref = rest[len(deps) + 1]
        kk = pl.program_id(2)

        @pl.when(kk == 0)
        def _():
            acc_ref[...] = part

        @pl.when((kk > 0) & (kk < nk - 1))
        def _():
            acc_ref[...] += part

        @pl.when(kk == nk - 1)
        def _():
            o_ref[...] = (acc_ref[...] + part).astype(o_ref.dtype)

    a_spec = pl.BlockSpec((tk, tm), lambda j, i, kk: (kk, i)) if ta else pl.BlockSpec((tm, tk), lambda j, i, kk: (i, kk))
    b_spec = (pl.BlockSpec((tn, tk), lambda j, i, kk: (j, kk)) if tb
              else pl.BlockSpec((tk, tn), lambda j, i, kk: (kk, j + j0)))
    return pl.pallas_call(
        body, name=name, grid=(n // tn, m // tm, nk),
        in_specs=[a_spec, b_spec] + [ANY] * len(deps),
        out_specs=pl.BlockSpec((tm, tn), lambda j, i, kk: (i, j)),
        out_shape=jax.ShapeDtypeStruct((m, n), out_dtype),
        scratch_shapes=[pltpu.VMEM((tm, tn), F32)] if nk > 1 else [],
        compiler_params=_params(("parallel", "parallel", "arbitrary")),
    )(a, b, *deps)


def _rowwise(name, fn, rows, bcasts, row_outs, red_outs=(), tm=256):
    s = rows[0].shape[0]
    tm = _pick(s, tm, 16)
    resident = pl.Buffered(1)
    nr, nb, no = len(rows), len(bcasts), len(row_outs)

    def body(*refs):
        ins = [r[...] for r in refs[:nr + nb]]
        outs, reds = fn(*ins)
        for ref, val in zip(refs[nr + nb:nr + nb + no], outs):
            ref[...] = val.astype(ref.dtype)
        i = pl.program_id(0)
        for ref, val in zip(refs[nr + nb + no:], reds):
            @pl.when(i == 0)
            def _():
                ref[...] = val

            @pl.when(i > 0)
            def _():
                ref[...] += val

    in_specs = [pl.BlockSpec((tm, r.shape[1]), lambda i: (i, 0)) for r in rows]
    in_specs += [pl.BlockSpec(b.shape, lambda i: (0, 0), pipeline_mode=resident) for b in bcasts]
    out_specs = [pl.BlockSpec((tm, o.shape[1]), lambda i: (i, 0)) for o in row_outs]
    out_specs += [pl.BlockSpec(d.shape, lambda i: (0, 0)) for d in red_outs]
    return pl.pallas_call(
        body, name=name, grid=(s // tm,), in_specs=in_specs, out_specs=out_specs,
        out_shape=list(row_outs) + list(red_outs),
        compiler_params=_params(("arbitrary",)),
    )(*rows, *bcasts)


def _sds(shape, dtype=F32):
    return jax.ShapeDtypeStruct(shape, dtype)


def _rms(x, g):
    y = x * lax.rsqrt(jnp.mean(x * x, axis=-1, keepdims=True) + EPS)
    return y * g


def _silu(x):
    return x * jax.nn.sigmoid(x)


def _swiglu(g, u):
    return _silu(g) * u


def _ln_silu(u, g, b):
    mu = jnp.mean(u, axis=-1, keepdims=True)
    var = jnp.mean(jnp.square(u - mu), axis=-1, keepdims=True)
    return _silu((u - mu) * lax.rsqrt(var + EPS) * g + b)


def _merge(conv_pre, att_out, g_conv, g_att, b_cb):
    return jax.nn.sigmoid(g_conv) * (conv_pre + b_cb) + jax.nn.sigmoid(g_att) * att_out


def _glu(t):
    return t[:, :CONV_DIM] * jax.nn.sigmoid(t[:, CONV_DIM:])


def _conv_fwd(conv_in, w_pad, b, ln_g, ln_b, tm=256):
    s = conv_in.shape[0]
    tm = _pick(s, tm, HALO)
    ratio = tm // HALO

    def body(main_ref, halo_ref, w_ref, b_ref, g_ref, be_ref, u3_ref, u1_ref, buf):
        i = pl.program_id(0)
        buf[0:HALO, :] = _glu(halo_ref[...]) * (i > 0).astype(F32)
        buf[HALO:HALO + tm, :] = _glu(main_ref[...])
        acc = jnp.zeros((tm, CONV_DIM), F32) + b_ref[...]
        for j in range(CONV_WIDTH):
            acc = acc + w_ref[j:j + 1, :] * buf[pl.ds(HALO - (CONV_WIDTH - 1) + j, tm), :]
        u1_ref[...] = acc
        u3_ref[...] = _ln_silu(acc, g_ref[...], be_ref[...]).astype(u3_ref.dtype)

    return pl.pallas_call(
        body, name="conv_fwd", grid=(s // tm,),
        in_specs=[pl.BlockSpec((tm, 2 * CONV_DIM), lambda i: (i, 0)),
                  pl.BlockSpec((HALO, 2 * CONV_DIM), lambda i: (jnp.maximum(i * ratio - 1, 0), 0)),
                  pl.BlockSpec(w_pad.shape, lambda i: (0, 0)),
                  pl.BlockSpec(b.shape, lambda i: (0, 0)),
                  pl.BlockSpec(ln_g.shape, lambda i: (0, 0)),
                  pl.BlockSpec(ln_b.shape, lambda i: (0, 0))],
        out_specs=[pl.BlockSpec((tm, CONV_DIM), lambda i: (i, 0)),
                   pl.BlockSpec((tm, CONV_DIM), lambda i: (i, 0))],
        out_shape=[_sds((s, CONV_DIM), BF16), _sds((s, CONV_DIM), F32)],
        scratch_shapes=[pltpu.VMEM((tm + HALO, CONV_DIM), F32)],
        compiler_params=_params(("arbitrary",)),
    )(conv_in, conv_in, w_pad, b, ln_g, ln_b)


def _conv_bwd(conv_in, du1, w_pad, tm=256):
    s = conv_in.shape[0]
    tm = _pick(s, tm, HALO)
    ratio = tm // HALO
    nt = s // tm
    last_halo = s // HALO - 1

    def body(main_ref, halo_ref, du_ref, dun_ref, w_ref, dci_ref, dw_ref, db_ref, ubuf, dbuf):
        i = pl.program_id(0)
        main = main_ref[...]
        a = main[:, :CONV_DIM]
        sb = jax.nn.sigmoid(main[:, CONV_DIM:])
        ubuf[0:HALO, :] = _glu(halo_ref[...]) * (i > 0).astype(F32)
        ubuf[HALO:HALO + tm, :] = a * sb
        du = du_ref[...]
        dbuf[0:tm, :] = du
        dbuf[tm:tm + HALO, :] = dun_ref[...] * (i < nt - 1).astype(F32)

        @pl.when(i == 0)
        def _():
            dw_ref[...] = jnp.zeros_like(dw_ref)
            db_ref[...] = jnp.zeros_like(db_ref)

        du0 = jnp.zeros((tm, CONV_DIM), F32)
        for j in range(CONV_WIDTH):
            du0 = du0 + w_ref[j:j + 1, :] * dbuf[pl.ds(CONV_WIDTH - 1 - j, tm), :]
            dw_ref[j:j + 1, :] += jnp.sum(du * ubuf[pl.ds(HALO - (CONV_WIDTH - 1) + j, tm), :], axis=0, keepdims=True)
        db_ref[...] += jnp.sum(du, axis=0, keepdims=True)
        dci_ref[:, :CONV_DIM] = (du0 * sb).astype(dci_ref.dtype)
        dci_ref[:, CONV_DIM:] = (du0 * a * sb * (1.0 - sb)).astype(dci_ref.dtype)

    return pl.pallas_call(
        body, name="conv_bwd", grid=(nt,),
        in_specs=[pl.BlockSpec((tm, 2 * CONV_DIM), lambda i: (i, 0)),
                  pl.BlockSpec((HALO, 2 * CONV_DIM), lambda i: (jnp.maximum(i * ratio - 1, 0), 0)),
                  pl.BlockSpec((tm, CONV_DIM), lambda i: (i, 0)),
                  pl.BlockSpec((HALO, CONV_DIM), lambda i: (jnp.minimum((i + 1) * ratio, last_halo), 0)),
                  pl.BlockSpec(w_pad.shape, lambda i: (0, 0))],
        out_specs=[pl.BlockSpec((tm, 2 * CONV_DIM), lambda i: (i, 0)),
                   pl.BlockSpec(w_pad.shape, lambda i: (0, 0)),
                   pl.BlockSpec((1, CONV_DIM), lambda i: (0, 0))],
        out_shape=[_sds((s, 2 * CONV_DIM), BF16), _sds(w_pad.shape), _sds((1, CONV_DIM))],
        scratch_shapes=[pltpu.VMEM((tm + HALO, CONV_DIM), F32), pltpu.VMEM((tm + HALO, CONV_DIM), F32)],
        compiler_params=_params(("arbitrary",)),
    )(conv_in, conv_in, du1, du1, w_pad)


def _logsig_neg(z):
    return jnp.minimum(-z, 0.0) - jnp.log(1.0 + jnp.exp(-jnp.abs(z)))


def _split_dot(val, tri):
    hi = val.astype(BF16)
    lo = (val - hi.astype(F32)).astype(BF16)
    return jnp.dot(hi, tri, preferred_element_type=F32) + jnp.dot(lo, tri, preferred_element_type=F32)


def _attn_masks(t, later):
    row = lax.broadcasted_iota(jnp.int32, (t, t), 0)
    col = lax.broadcasted_iota(jnp.int32, (t, t), 1)
    tri = jnp.where(row > col if later else row <= col, 1.0, 0.0).astype(BF16)
    return col < row, tri


def _grid_marks(h, nq):
    hh, i = pl.program_id(0), pl.program_id(1)
    return (hh == 0) & (i == 0), (hh == (3 * h) // 4) & (i == 0), (hh == h - 1) & (i == nq - 1)


def _head_masks(shape):
    lane = lax.broadcasted_iota(jnp.int32, shape, len(shape) - 1)
    return lane < HEAD_DIM, lane >= HEAD_DIM


def _per_head(blk):
    m0, m1 = _head_masks(blk.shape)
    zero = jnp.zeros_like(blk)
    return jnp.where(m0, blk, zero), jnp.where(m1, blk, zero)


NT = (((1,), (1,)), ((), ()))
TN = (((0,), (0,)), ((), ()))


def _attn_fwd(q, k, v, exchange):
    s = q.shape[0]
    hp = q.shape[1] // LANES
    t = ATT_TILE
    scale = 1.0 / math.sqrt(HEAD_DIM)
    x_arrs, x_shape, x_scratch, _ = exchange
    nx = len(x_arrs)

    def body(*refs):
        q_ref, k_ref, v_ref = refs[:3]
        o_ref, lt_ref = refs[3 + nx:5 + nx]
        finish_exchange = _carry_exchange(exchange, refs, 3, 2, *_grid_marks(hp, s // t))
        i = pl.program_id(1)
        qs = _per_head((q_ref[...].astype(F32) * scale).astype(BF16))
        causal, tri = _attn_masks(t, later=True)

        def step(kb, carry, masked):
            cs, acc = carry
            off = pl.multiple_of(kb * t, t)
            kblk = k_ref[pl.ds(off, t), :]
            vs = _per_head(v_ref[pl.ds(off, t), :])
            new_cs = []
            for hd in range(2):
                z = lax.dot_general(qs[hd], kblk, NT, preferred_element_type=F32)
                l = _logsig_neg(z)
                if masked:
                    l = jnp.where(causal, l, 0.0)
                e = z + l + _split_dot(l, tri) + cs[hd]
                if masked:
                    e = jnp.where(causal, e, -1e30)
                acc = acc + jnp.dot(jnp.exp(e).astype(BF16), vs[hd], preferred_element_type=F32)
                new_cs.append(cs[hd] + jnp.sum(l, axis=1, keepdims=True))
            return tuple(new_cs), acc

        zero = jnp.zeros((t, 1), F32)
        carry = step(i, ((zero, zero), jnp.zeros((t, LANES), F32)), True)
        carry = lax.fori_loop(0, i, lambda n, cr: step(i - 1 - n, cr, False), carry)
        m0, _ = _head_masks((t, LANES))
        lt_ref[...] = jnp.where(m0, carry[0][0], carry[0][1])
        o_ref[...] = carry[1].astype(o_ref.dtype)
        finish_exchange()

    res = pl.pallas_call(
        body, name="attn_fwd", grid=(hp, s // t),
        in_specs=[pl.BlockSpec((t, LANES), lambda p, i: (i, p)),
                  pl.BlockSpec((s, LANES), lambda p, i: (0, p)),
                  pl.BlockSpec((s, LANES), lambda p, i: (0, p))] + [ANY] * nx,
        out_specs=[pl.BlockSpec((t, LANES), lambda p, i: (i, p)),
                   pl.BlockSpec((None, t, LANES), lambda p, i: (p, i, 0))] + [ANY] * nx,
        out_shape=[_sds(q.shape, BF16), _sds((hp, s, LANES), F32)] + x_shape,
        scratch_shapes=x_scratch,
        compiler_params=_params(("arbitrary", "arbitrary")),
    )(q, k, v, *x_arrs)
    return res[0], res[1], res[2:]


def _attn_bwd(q, k, v, do, ltot, exchange):
    s = q.shape[0]
    hp = q.shape[1] // LANES
    t = ATT_TILE
    scale = 1.0 / math.sqrt(HEAD_DIM)
    x_arrs, x_shape, x_scratch, _ = exchange
    nx = len(x_arrs)

    def body(*refs):
        q_ref, k_ref, v_ref, do_ref, lt_ref = refs[:5]
        dq_ref, dk_ref, dv_ref = refs[5 + nx:8 + nx]
        finish_exchange = _carry_exchange(exchange, refs, 5, 3, *_grid_marks(hp, s // t))
        i = pl.program_id(1)

        @pl.when(i == 0)
        def _():
            dk_ref[...] = jnp.zeros_like(dk_ref)
            dv_ref[...] = jnp.zeros_like(dv_ref)

        qb = q_ref[...]
        qm = _per_head(qb)
        qs = _per_head((qb.astype(F32) * scale).astype(BF16))
        dos = _per_head(do_ref[...])
        lts = (lt_ref[:, 0:1], lt_ref[:, HEAD_DIM:HEAD_DIM + 1])
        causal, tri = _attn_masks(t, later=False)

        def step(kb, carry, masked):
            cls, cgs, dq = carry
            off = pl.multiple_of(kb * t, t)
            kblk = k_ref[pl.ds(off, t), :]
            vblk = v_ref[pl.ds(off, t), :]
            ks = _per_head(kblk)
            dk = jnp.zeros((t, LANES), F32)
            dv = jnp.zeros((t, LANES), F32)
            new_cls, new_cgs = [], []
            for hd in range(2):
                z = lax.dot_general(qs[hd], kblk, NT, preferred_element_type=F32)
                l = _logsig_neg(z)
                if masked:
                    l = jnp.where(causal, l, 0.0)
                e = z + l + ((lts[hd] - cls[hd]) - _split_dot(l, tri))
                if masked:
                    e = jnp.where(causal, e, -1e30)
                a = jnp.exp(e)
                g = lax.dot_general(dos[hd], vblk, NT, preferred_element_type=F32) * a
                p = cgs[hd] + jnp.dot(g.astype(BF16), tri, preferred_element_type=F32) - g
                el = jnp.exp(l)
                dz = g * el - p * (1.0 - el)
                if masked:
                    dz = jnp.where(causal, dz, 0.0)
                dzb = (dz * scale).astype(BF16)
                dq = dq + jnp.dot(dzb, ks[hd], preferred_element_type=F32)
                dk = dk + lax.dot_general(dzb, qm[hd], TN, preferred_element_type=F32)
                dv = dv + lax.dot_general(a.astype(BF16), dos[hd], TN, preferred_element_type=F32)
                new_cls.append(cls[hd] + jnp.sum(l, axis=1, keepdims=True))
                new_cgs.append(cgs[hd] + jnp.sum(g, axis=1, keepdims=True))
            dk_ref[pl.ds(off, t), :] += dk
            dv_ref[pl.ds(off, t), :] += dv
            return tuple(new_cls), tuple(new_cgs), dq

        zero = jnp.zeros((t, 1), F32)
        init = ((zero, zero), (zero, zero), jnp.zeros((t, LANES), F32))
        carry = lax.fori_loop(0, i, lambda kb, cr: step(kb, cr, False), init)
        carry = step(i, carry, True)
        dq_ref[...] = carry[2]
        finish_exchange()

    blk = pl.BlockSpec((t, LANES), lambda p, i: (i, p))
    whole = pl.BlockSpec((s, LANES), lambda p, i: (0, p))
    res = pl.pallas_call(
        body, name="attn_bwd", grid=(hp, s // t),
        in_specs=[blk, whole, whole, blk, pl.BlockSpec((None, t, LANES), lambda p, i: (p, i, 0))] + [ANY] * nx,
        out_specs=[blk, whole, whole] + [ANY] * nx,
        out_shape=[_sds(q.shape)] * 3 + x_shape,
        scratch_shapes=x_scratch,
        compiler_params=_params(("arbitrary", "arbitrary")),
    )(q, k, v, do, ltot, *x_arrs)
    return res[0], res[1], res[2], res[3:]


LATE = ["w_conv_branch", "w_att_branch", "w_out", "w_ffn_up", "w_ffn_down"]


def _full_weight(name, gathered):
    return _cols_to_full(gathered) if name in COL_SHARDED else gathered.reshape(-1, gathered.shape[2])


def _grad_slabs(name, grad):
    return _full_to_cols(grad) if name in COL_SHARDED else grad.reshape(N_DEV, -1, grad.shape[1])


def _local_step(x, target, w, late_blocks):
    s = x.shape[0]
    w = dict(w)
    g1, g2, g3, g4 = w["norm_mix_pre"], w["norm_mix_post"], w["norm_ffn_pre"], w["norm_ffn_post"]

    (h1,) = _rowwise("pre_norm", lambda xt, g: ((_rms(xt, g),), ()), [x], [g1], [_sds((s, D_MODEL), BF16)])
    w_in = w["w_in"]
    cols = [(IN_SPLITS[n], IN_SPLITS[n + 1] - IN_SPLITS[n]) for n in range(6)]
    conv_in = _matmul(h1, w_in, cols=cols[0], name="proj_conv")
    q = _matmul(h1, w_in, cols=cols[1], name="proj_q", out_dtype=BF16)
    k = _matmul(h1, w_in, cols=cols[2], name="proj_k", out_dtype=BF16)
    v = _matmul(h1, w_in, cols=cols[3], name="proj_v", out_dtype=BF16)
    g_conv = _matmul(h1, w_in, cols=cols[4], name="proj_gate_conv")
    g_att = _matmul(h1, w_in, cols=cols[5], name="proj_gate_att")

    u3, u1 = _conv_fwd(conv_in, w["conv_dw_w"], w["conv_dw_b"], w["conv_ln_g"], w["conv_ln_b"])
    att, ltot, gathered = _attn_fwd(q, k, v, _gather_exchange(late_blocks))
    for nm, g in zip(LATE, gathered):
        w[nm] = _full_weight(nm, g)

    def merge_fn(u3t, at, gc, ga, w_cb, w_ab, b_cb):
        cp = jnp.dot(u3t, w_cb, preferred_element_type=F32)
        ao = jnp.dot(at, w_ab, preferred_element_type=F32)
        return (_merge(cp, ao, gc, ga, b_cb), cp, ao), ()

    merged, conv_pre, att_out = _rowwise(
        "branch_merge", merge_fn, [u3, att, g_conv, g_att], [w["w_conv_branch"], w["w_att_branch"], w["b_conv_branch"]],
        [_sds((s, D_MODEL), BF16)] * 3, tm=512)

    def mid_fn(mt, xt, w_out, g2_, g3_):
        mix_ = jnp.dot(mt, w_out, preferred_element_type=F32)
        x2_ = xt + _rms(mix_, g2_)
        return (mix_, x2_, _rms(x2_, g3_)), ()

    mix, x2, h2 = _rowwise("mix_mid_norm", mid_fn, [merged, x], [w["w_out"], g2, g3],
                           [_sds((s, D_MODEL)), _sds((s, D_MODEL)), _sds((s, D_MODEL), BF16)], tm=512)

    def ffn_up_fn(ht, w_up):
        gu_ = jnp.dot(ht, w_up, preferred_element_type=F32)
        return (gu_, _swiglu(gu_[:, :D_FF], gu_[:, D_FF:])), ()

    gu, act = _rowwise("ffn_up", ffn_up_fn, [h2], [w["w_ffn_up"]],
                       [_sds((s, 2 * D_FF), BF16), _sds((s, D_FF), BF16)], tm=512)

    def final_fn(at, x2t, tgt, w_down, g4_):
        ff = jnp.dot(at, w_down, preferred_element_type=F32)
        n4, vjp = jax.vjp(_rms, ff, g4_)
        err = x2t + n4 - tgt
        dy = err * (1.0 / D_MODEL)
        dff, dg4 = vjp(dy)
        return (dy, dff), (jnp.sum(err * err, axis=0, keepdims=True), dg4)

    dy, dff, loss_cols, d_g4 = _rowwise("ffn_down_loss", final_fn, [act, x2, target], [w["w_ffn_down"], g4],
                                        [_sds((s, D_MODEL)), _sds((s, D_MODEL), BF16)],
                                        [_sds((1, D_MODEL)), _sds((1, D_MODEL))], tm=512)
    loss = 0.5 * jnp.sum(loss_cols) / D_MODEL

    d_w_down = _matmul(act, dff, ta=True, name="d_w_down", out_dtype=BF16)

    def act_bwd_fn(dfft, gut, w_down):
        d_act = lax.dot_general(dfft, w_down, NT, preferred_element_type=F32)
        gu_ = gut.astype(F32)
        _, vjp = jax.vjp(_swiglu, gu_[:, :D_FF], gu_[:, D_FF:])
        return (jnp.concatenate(vjp(d_act), axis=1),), ()

    (dgu,) = _rowwise("ffn_act_bwd", act_bwd_fn, [dff, gu], [w["w_ffn_down"]], [_sds((s, 2 * D_FF), BF16)])
    dh2 = _matmul(dgu, w["w_ffn_up"], tb=True, name="d_h2")
    d_w_up = _matmul(h2, dgu, ta=True, name="d_w_up", out_dtype=BF16)

    def mid_bwd_fn(xt, mt, dyt, dh, g2_, g3_):
        n2, vjp2 = jax.vjp(_rms, mt, g2_)
        x2_ = xt + n2
        _, vjp3 = jax.vjp(_rms, x2_, g3_)
        dx2_, dg3 = vjp3(dh)
        dx2_ = dx2_ + dyt
        dmix_, dg2 = vjp2(dx2_)
        return (dx2_, dmix_), (dg2, dg3)

    dx2, dmix, d_g2, d_g3 = _rowwise("mid_bwd", mid_bwd_fn, [x, mix, dy, dh2], [g2, g3],
                                     [_sds((s, D_MODEL)), _sds((s, D_MODEL), BF16)],
                                     [_sds((1, D_MODEL)), _sds((1, D_MODEL))])
    d_w_out = _matmul(merged, dmix, ta=True, name="d_w_out", out_dtype=BF16)

    def merge_bwd_fn(dmt, cp, ao, gc, ga, w_out, w_cb, w_ab, b_cb):
        dm = lax.dot_general(dmt, w_out, NT, preferred_element_type=F32)
        _, vjp = jax.vjp(_merge, cp.astype(F32), ao.astype(F32), gc, ga, b_cb)
        dcp, dao, dgc, dga, dbias = vjp(dm)
        dcp, dao = dcp.astype(BF16), dao.astype(BF16)
        du3_ = lax.dot_general(dcp, w_cb, NT, preferred_element_type=F32)
        datt_ = lax.dot_general(dao, w_ab, NT, preferred_element_type=F32)
        return (dcp, dao, dgc, dga, du3_, datt_), (dbias,)

    d_conv_out, d_att_out, d_g_conv, d_g_att, du3, d_att, d_b_cb = _rowwise(
        "merge_bwd", merge_bwd_fn, [dmix, conv_pre, att_out, g_conv, g_att],
        [w["w_out"], w["w_conv_branch"], w["w_att_branch"], w["b_conv_branch"]],
        [_sds((s, D_MODEL), BF16)] * 4 + [_sds((s, CONV_DIM)), _sds((s, ATT_DIM), BF16)], [_sds((1, D_MODEL))], tm=512)

    d_w_cb = _matmul(u3, d_conv_out, ta=True, name="d_w_conv_branch", out_dtype=BF16)
    d_w_ab = _matmul(att, d_att_out, ta=True, name="d_w_att_branch", out_dtype=BF16)

    late_grads = {"w_conv_branch": d_w_cb, "w_att_branch": d_w_ab, "w_out": d_w_out,
                  "w_ffn_up": d_w_up, "w_ffn_down": d_w_down}
    dq, dk, dv, received = _attn_bwd(q, k, v, d_att, ltot,
                                        _scatter_exchange([_grad_slabs(nm, late_grads[nm]) for nm in LATE]))

    def ln_bwd_fn(u1t, du3t, g_, b_):
        _, vjp = jax.vjp(_ln_silu, u1t, g_, b_)
        du1_, dg_, db_ = vjp(du3t)
        return (du1_,), (dg_, db_)

    du1, d_ln_g, d_ln_b = _rowwise("conv_ln_bwd", ln_bwd_fn, [u1, du3], [w["conv_ln_g"], w["conv_ln_b"]],
                                   [_sds((s, CONV_DIM))], [_sds((1, CONV_DIM)), _sds((1, CONV_DIM))])
    d_conv_in, d_dw_w, d_dw_b = _conv_bwd(conv_in, du1, w["conv_dw_w"])

    d_proj = jnp.concatenate([d_conv_in, dq.astype(BF16), dk.astype(BF16), dv.astype(BF16), d_g_conv, d_g_att],
                             axis=1)
    d_w_in = _matmul(h1, d_proj, ta=True, name="d_w_in", out_dtype=BF16)
    w_in_exchange, token = _scatter_start("scatter_w_in_start", [_grad_slabs("w_in", d_w_in)])
    dh1 = _matmul(d_proj, w_in, tb=True, name="d_h1", deps=[token])

    def pre_bwd_fn(xt, dh, dx2t, g_):
        _, vjp = jax.vjp(_rms, xt, g_)
        dx_, dg_ = vjp(dh)
        return (dx_ + dx2t,), (dg_,)

    grad_x, d_g1 = _rowwise("pre_bwd", pre_bwd_fn, [x, dh1, dx2], [g1], [_sds((s, D_MODEL))], [_sds((1, D_MODEL))])

    grads = {
        "norm_mix_pre": d_g1, "conv_dw_w": d_dw_w, "conv_dw_b": d_dw_b,
        "conv_ln_g": d_ln_g, "conv_ln_b": d_ln_b, "b_conv_branch": d_b_cb,
        "norm_mix_post": d_g2, "norm_ffn_pre": d_g3, "norm_ffn_post": d_g4,
    }
    return loss, grad_x, received, w_in_exchange, grads


def _place():
    x, y, c = lax.axis_index("x"), lax.axis_index("y"), lax.axis_index("c")
    return x, y, c


def _slot(px, py, pc):
    return 4 * px + 2 * py + pc


def _exchange_scratch(n):
    return [pltpu.SemaphoreType.DMA((7 * n,)), pltpu.SemaphoreType.DMA((7 * n,)), pltpu.SemaphoreType.DMA((n,))]


def _gather_exchange(arrs):
    n = len(arrs)

    def phases(ins, outs, send_sems, recv_sems, local_sems):
        x, y, c = _place()
        me, sibling = (x, y, c), (x, y, 1 - c)
        chips = [(1 - x, y), (x, 1 - y), (1 - x, 1 - y)]

        def copy(a, kk, block, to, src=None):
            dst = outs[a].at[_slot(*block)]
            return pltpu.make_async_remote_copy(
                src_ref=dst if src is None else src, dst_ref=dst,
                send_sem=send_sems.at[a * 7 + kk], recv_sem=recv_sems.at[a * 7 + kk],
                device_id=to, device_id_type=MESH)

        mine = [pltpu.make_async_copy(ins[a], outs[a].at[_slot(*me)], local_sems.at[a]) for a in range(n)]
        first = []
        for a in range(n):
            first.append(copy(a, 0, me, sibling, src=ins[a]))
            first += [copy(a, 1 + j, me, (*chip, c), src=ins[a]) for j, chip in enumerate(chips)]
        passed = [copy(a, 4 + j, (*chip, c), sibling) for j, chip in enumerate(chips) for a in range(n)]

        def send():
            for cp in mine + first:
                cp.start()

        def pass_on():
            for j, chip in enumerate(chips):
                for a in range(n):
                    copy(a, 1 + j, (*chip, c), me).wait_recv()
                    passed[j * n + a].start()

        def finish():
            for a in range(n):
                copy(a, 0, sibling, me).wait_recv()
                for j, chip in enumerate(chips):
                    copy(a, 4 + j, (*chip, 1 - c), me).wait_recv()
            for cp in first + passed:
                cp.wait_send()
            for cp in mine:
                cp.wait()

        return [send, pass_on, finish]

    return list(arrs), [_sds((N_DEV,) + a.shape, a.dtype) for a in arrs], _exchange_scratch(n), phases


def _scatter_exchange(arrs):
    n = len(arrs)
    flips = [(fx, fy, fc) for fx in (0, 1) for fy in (0, 1) for fc in (0, 1)][1:]

    def phases(ins, outs, send_sems, recv_sems, local_sems):
        x, y, c = _place()
        mine = _slot(x, y, c)
        local = [pltpu.make_async_copy(ins[a].at[mine], outs[a].at[mine], local_sems.at[a]) for a in range(n)]
        peers = [((1 - x) if fx else x, (1 - y) if fy else y, (1 - c) if fc else c) for fx, fy, fc in flips]

        def copy(a, kk, src_slot, dst_slot):
            return pltpu.make_async_remote_copy(
                src_ref=ins[a].at[src_slot], dst_ref=outs[a].at[dst_slot],
                send_sem=send_sems.at[a * 7 + kk], recv_sem=recv_sems.at[a * 7 + kk],
                device_id=peers[kk], device_id_type=MESH)

        sends = [copy(a, kk, _slot(*peers[kk]), mine) for a in range(n) for kk in range(7)]

        def send():
            for cp in local + sends:
                cp.start()

        def finish():
            for a in range(n):
                for kk in range(7):
                    copy(a, kk, mine, _slot(*peers[kk])).wait_recv()
            for cp in sends:
                cp.wait_send()
            for cp in local:
                cp.wait()

        return [send, finish]

    return list(arrs), [_sds(a.shape, a.dtype) for a in arrs], _exchange_scratch(n), phases


HBM = pl.BlockSpec(memory_space=pltpu.HBM)
SEM = pl.BlockSpec(memory_space=pltpu.SEMAPHORE)
EFFECT = pltpu.SideEffectType.DATAFLOW_SIDE_EFFECTING


def _scatter_copies(ins, lands, send_sems, recv_sems):
    x, y, c = _place()
    mine = _slot(x, y, c)
    flips = [(fx, fy, fc) for fx in (0, 1) for fy in (0, 1) for fc in (0, 1)][1:]
    peers = [((1 - x) if fx else x, (1 - y) if fy else y, (1 - c) if fc else c) for fx, fy, fc in flips]
    pairs = []
    for a in range(len(ins)):
        for kk, peer in enumerate(peers):
            def copy(dst_slot, a=a, kk=kk, peer=peer):
                return pltpu.make_async_remote_copy(
                    src_ref=ins[a].at[_slot(*peer)], dst_ref=lands[a].at[dst_slot],
                    send_sem=send_sems.at[a * 7 + kk], recv_sem=recv_sems.at[a * 7 + kk],
                    device_id=peer, device_id_type=MESH)
            pairs.append((copy(mine), copy(_slot(*peer))))
    return pairs


def _scatter_start(name, arrs):
    n = len(arrs)

    def body(*refs):
        for sent, _ in _scatter_copies(refs[:n], refs[n:2 * n], refs[2 * n], refs[2 * n + 1]):
            sent.start()
        token = refs[-1]
        token[...] = jnp.zeros_like(token)

    sems = pltpu.SemaphoreType.DMA((7 * n,))
    buffers = [pltpu.HBM(a.shape, a.dtype) for a in arrs]
    res = pl.pallas_call(
        body, name=name,
        out_shape=(sems, sems, *buffers, *buffers, _sds((SUBLANES, LANES))),
        in_specs=[HBM] * (2 * n),
        out_specs=(SEM, SEM, *[HBM] * (2 * n), pl.BlockSpec(memory_space=pltpu.VMEM)),
        input_output_aliases={i: 2 + i for i in range(2 * n)},
        compiler_params=pltpu.CompilerParams(has_side_effects=EFFECT),
    )(*[pltpu.with_memory_space_constraint(a, pltpu.HBM) for a in arrs],
      *[pltpu.with_memory_space_constraint(lax.empty(a.shape, a.dtype), pltpu.HBM) for a in arrs])
    return res[:-1], res[-1]


def _scatter_wait(name, state, after):
    send_sems, recv_sems, thru = state[0], state[1], state[2:]
    n = len(thru) // 2

    def body(*refs):
        for sent, landed in _scatter_copies(refs[:n], refs[n:2 * n], refs[2 * n], refs[2 * n + 1]):
            sent.wait_send()
            landed.wait_recv()

    res = pl.pallas_call(
        body, name=name,
        out_shape=[pltpu.HBM(t.shape, t.dtype) for t in thru],
        in_specs=[HBM] * (2 * n) + [SEM, SEM] + [ANY] * len(after),
        out_specs=[HBM] * (2 * n),
        input_output_aliases={i: i for i in range(2 * n)},
        compiler_params=pltpu.CompilerParams(has_side_effects=EFFECT),
    )(*thru, send_sems, recv_sems, *after)
    me = _slot(*_place())
    return [lax.dynamic_update_slice_in_dim(land, lax.dynamic_slice_in_dim(sent, me, 1, 0), me, 0)
            for sent, land in zip(res[:n], res[n:])]


def _exchange_call(name, exchange):
    arrs, out_shape, scratch, phases = exchange
    n = len(arrs)

    def body(*refs):
        for step in phases(refs[:n], refs[n:2 * n], *refs[2 * n:]):
            step()

    return pl.pallas_call(body, name=name, in_specs=[ANY] * n, out_specs=[ANY] * n,
                          out_shape=out_shape, scratch_shapes=scratch)(*arrs)


def _carry_exchange(exchange, refs, n_in, n_out, first, middle, last):
    arrs, _, _, phases = exchange
    n = len(arrs)
    ins = refs[n_in:n_in + n]
    outs = refs[n_in + n + n_out:n_in + 2 * n + n_out]
    steps = phases(ins, outs, *refs[n_in + 2 * n + n_out:])
    pl.when(first)(steps[0])
    if len(steps) == 3:
        pl.when(middle)(steps[1])
    return lambda: pl.when(last)(steps[-1])


def _adamw_math(w, g, m, v):
    m2 = ADAM_B1 * m + (1.0 - ADAM_B1) * g
    v2 = ADAM_B2 * v + (1.0 - ADAM_B2) * jnp.square(g)
    m_hat = m2 / (1.0 - ADAM_B1 ** ADAM_STEP)
    v_hat = v2 / (1.0 - ADAM_B2 ** ADAM_STEP)
    delta = -ADAM_LR * (m_hat / (jnp.sqrt(v_hat) + ADAM_EPS) + ADAM_WD * w)
    return delta, m2, v2


def _sum_adamw(name, parts, w, m, v, tr=256):
    p, r, c = parts.shape
    tr = _pick(r, tr, 16)

    def body(p_ref, w_ref, m_ref, v_ref, g_ref, d_ref, m2_ref, v2_ref):
        g = p_ref[0].astype(F32)
        for d in range(1, p):
            g = g + p_ref[d].astype(F32)
        delta, m2, v2 = _adamw_math(w_ref[...], g, m_ref[...], v_ref[...])
        g_ref[...] = g
        d_ref[...] = delta
        m2_ref[...] = m2
        v2_ref[...] = v2

    tile = pl.BlockSpec((tr, c), lambda i: (i, 0))
    return pl.pallas_call(
        body, name=name, grid=(r // tr,),
        in_specs=[pl.BlockSpec((p, tr, c), lambda i: (0, i, 0)), tile, tile, tile],
        out_specs=[tile] * 4, out_shape=[_sds((r, c))] * 4,
        compiler_params=_params(("parallel",)),
    )(parts, w, m, v)


def _sum_parts(name, parts):
    p, r, c = parts.shape

    def body(p_ref, o_ref):
        g = p_ref[0]
        for d in range(1, p):
            g = g + p_ref[d]
        o_ref[...] = g

    return pl.pallas_call(
        body, name=name, out_shape=_sds((r, c)),
        in_specs=[pl.BlockSpec(memory_space=pltpu.VMEM)], out_specs=pl.BlockSpec(memory_space=pltpu.VMEM),
    )(parts)


WEIGHTS = ["norm_mix_pre", "w_in", "conv_dw_w", "conv_dw_b", "conv_ln_g", "conv_ln_b", "w_conv_branch",
           "b_conv_branch", "w_att_branch", "w_out", "norm_mix_post", "norm_ffn_pre", "w_ffn_up", "w_ffn_down",
           "norm_ffn_post"]
COL_SHARDED = ["w_in", "w_conv_branch", "w_att_branch", "w_ffn_up"]
ROW_SHARDED = ["w_out", "w_ffn_down"]
VECTORS = ["norm_mix_pre", "conv_dw_b", "conv_ln_g", "conv_ln_b", "b_conv_branch", "norm_mix_post",
           "norm_ffn_pre", "norm_ffn_post"]


def _cols_to_full(g):
    return g.transpose(1, 0, 2).reshape(g.shape[1], N_DEV * g.shape[2])


def _full_to_cols(f):
    return f.reshape(f.shape[0], N_DEV, f.shape[1] // N_DEV).transpose(1, 0, 2)


def _pack_vectors(vecs):
    rows = [jnp.pad(vecs[nm].reshape(-1), (0, D_MODEL - vecs[nm].size)) for nm in VECTORS]
    return jnp.stack(rows)


def _unpack_vectors(packed, sizes):
    return {nm: packed[n, :sizes[nm]] for n, nm in enumerate(VECTORS)}


def kernel(x, norm_mix_pre, w_in, conv_dw_w, conv_dw_b, conv_ln_g, conv_ln_b, w_conv_branch, b_conv_branch, w_att_branch, w_out, norm_mix_post, norm_ffn_pre, w_ffn_up, w_ffn_down, norm_ffn_post, loss_target, m_norm_mix_pre, m_w_in, m_conv_dw_w, m_conv_dw_b, m_conv_ln_g, m_conv_ln_b, m_w_conv_branch, m_b_conv_branch, m_w_att_branch, m_w_out, m_norm_mix_post, m_norm_ffn_pre, m_w_ffn_up, m_w_ffn_down, m_norm_ffn_post, v_norm_mix_pre, v_w_in, v_conv_dw_w, v_conv_dw_b, v_conv_ln_g, v_conv_ln_b, v_w_conv_branch, v_b_conv_branch, v_w_att_branch, v_w_out, v_norm_mix_post, v_norm_ffn_pre, v_w_ffn_up, v_w_ffn_down, v_norm_ffn_post):
    ws = dict(zip(WEIGHTS, [norm_mix_pre, w_in, conv_dw_w, conv_dw_b, conv_ln_g, conv_ln_b, w_conv_branch,
                            b_conv_branch, w_att_branch, w_out, norm_mix_post, norm_ffn_pre, w_ffn_up, w_ffn_down,
                            norm_ffn_post]))
    ms = dict(zip(WEIGHTS, [m_norm_mix_pre, m_w_in, m_conv_dw_w, m_conv_dw_b, m_conv_ln_g, m_conv_ln_b,
                            m_w_conv_branch, m_b_conv_branch, m_w_att_branch, m_w_out, m_norm_mix_post,
                            m_norm_ffn_pre, m_w_ffn_up, m_w_ffn_down, m_norm_ffn_post]))
    vs = dict(zip(WEIGHTS, [v_norm_mix_pre, v_w_in, v_conv_dw_w, v_conv_dw_b, v_conv_ln_g, v_conv_ln_b,
                            v_w_conv_branch, v_b_conv_branch, v_w_att_branch, v_w_out, v_norm_mix_post,
                            v_norm_ffn_pre, v_w_ffn_up, v_w_ffn_down, v_norm_ffn_post]))

    dw_block = jnp.pad(conv_dw_w, ((0, 1), (0, 0)))
    g_in, g_dw = _exchange_call("gather_first", _gather_exchange([w_in.astype(BF16), dw_block]))
    full = {"w_in": _full_weight("w_in", g_in), "conv_dw_w": _cols_to_full(g_dw)}
    for nm in VECTORS:
        full[nm] = ws[nm].reshape(1, -1)

    loss_local, grad_x, received, w_in_exchange, grads = _local_step(
        x[0], loss_target[0], full, [ws[nm].astype(BF16) for nm in LATE])
    loss = lax.psum(loss_local, ("x", "y", "c"))

    out_g, out_d, out_m, out_v = {}, {}, {}, {}
    for nm, parts in zip(LATE, received):
        out_g[nm], out_d[nm], out_m[nm], out_v[nm] = _sum_adamw("adamw_" + nm, parts, ws[nm], ms[nm], vs[nm])
    small = _exchange_call("gather_small_grads", _gather_exchange([_pack_vectors(grads), grads["conv_dw_w"]]))
    (recv_in,) = _scatter_wait("scatter_w_in_wait", w_in_exchange,
                               [grad_x] + [out_v[nm] for nm in LATE] + list(small))
    nm = "w_in"
    out_g[nm], out_d[nm], out_m[nm], out_v[nm] = _sum_adamw("adamw_" + nm, recv_in, ws[nm], ms[nm], vs[nm])
    sizes = {nm: ws[nm].size for nm in VECTORS}
    vec = _sum_adamw("adamw_vectors", small[0], _pack_vectors(ws), _pack_vectors(ms), _pack_vectors(vs))
    for res, dst in zip(vec, (out_g, out_d, out_m, out_v)):
        dst.update(_unpack_vectors(res, sizes))
    dw_full = _sum_parts("sum_dw_grads", small[1])
    me = _slot(*_place())
    dw_mine = lax.dynamic_slice(dw_full, (0, me * (CONV_DIM // N_DEV)), (CONV_WIDTH, CONV_DIM // N_DEV))
    nm = "conv_dw_w"
    out_g[nm], out_d[nm], out_m[nm], out_v[nm] = _sum_adamw("adamw_dw", dw_mine[None], ws[nm], ms[nm], vs[nm])

    outs = [loss, grad_x[None]]
    for group in (out_g, out_d, out_m, out_v):
        outs += [group[nm] for nm in WEIGHTS]
    return tuple(outs)
```

```python
import functools
import math

import jax
import jax.numpy as jnp
from jax import lax
from jax.experimental import pallas as pl
from jax.experimental.pallas import tpu as pltpu

F32 = jnp.float32
BF16 = jnp.bfloat16

N_DEV = 8
D_MODEL = 1024
CONV_DIM = 512
CONV_WIDTH = 31
N_HEADS = 8
HEAD_DIM = 64
ATT_DIM = N_HEADS * HEAD_DIM
D_FF = 2816
EPS = 1e-6
IN_SPLITS = (0, 1024, 1536, 2048, 2560, 3584, 4608)

ADAM_LR = 0.001
ADAM_B1 = 0.9
ADAM_B2 = 0.999
ADAM_EPS = 1e-08
ADAM_WD = 0.01
ADAM_STEP = 10

LANES = 128
SUBLANES = 8
HALO = 32
ATT_TILE = 256
VMEM_LIMIT = 56 * 1024 * 1024
MESH = pl.DeviceIdType.MESH
ANY = pl.BlockSpec(memory_space=pl.ANY)


def _pick(dim, target, align=LANES):
    t = min(dim, target)
    t -= t % align
    while t >= align:
        if dim % t == 0:
            return t
        t -= align
    return dim


def _params(semantics):
    return pltpu.CompilerParams(dimension_semantics=semantics, vmem_limit_bytes=VMEM_LIMIT)


def _matmul(a, b, *, name, ta=False, tb=False, out_dtype=F32, cols=None, deps=()):
    m, k = (a.shape[1], a.shape[0]) if ta else a.shape
    n, k2 = b.shape if tb else (b.shape[1], b.shape[0])
    assert k == k2, (a.shape, b.shape, ta, tb)
    col0 = 0
    if cols is not None:
        assert not tb
        col0, n = cols
    tm, tk = _pick(m, 1408 if ta else 512), _pick(k, 1536)
    tn = _pick(math.gcd(n, col0) if col0 else n, 1536)
    nk = k // tk
    j0 = col0 // tn
    dims = (((0 if ta else 1,), (1 if tb else 0,)), ((), ()))

    def body(a_ref, b_ref, *rest):
        o_ref = rest[len(deps)]
        part = lax.dot_general(a_ref[...], b_ref[...], dims, preferred_element_type=F32)
        if nk == 1:
            o_ref[...] = part.astype(o_ref.dtype)
            return
        acc_ref = rest[len(deps) + 1]
        kk = pl.program_id(2)

        @pl.when(kk == 0)
        def _():
            acc_ref[...] = part

        @pl.when((kk > 0) & (kk < nk - 1))
        def _():
            acc_ref[...] += part

        @pl.when(kk == nk - 1)
        def _():
            o_ref[...] = (acc_ref[...] + part).astype(o_ref.dtype)

    a_spec = pl.BlockSpec((tk, tm), lambda j, i, kk: (kk, i)) if ta else pl.BlockSpec((tm, tk), lambda j, i, kk: (i, kk))
    b_spec = (pl.BlockSpec((tn, tk), lambda j, i, kk: (j, kk)) if tb
              else pl.BlockSpec((tk, tn), lambda j, i, kk: (kk, j + j0)))
    return pl.pallas_call(
        body, name=name, grid=(n // tn, m // tm, nk),
        in_specs=[a_spec, b_spec] + [ANY] * len(deps),
        out_specs=pl.BlockSpec((tm, tn), lambda j, i, kk: (i, j)),
        out_shape=jax.ShapeDtypeStruct((m, n), out_dtype),
        scratch_shapes=[pltpu.VMEM((tm, tn), F32)] if nk > 1 else [],
        compiler_params=_params(("parallel", "parallel", "arbitrary")),
    )(a, b, *deps)


def _rowwise(name, fn, rows, bcasts, row_outs, red_outs=(), tm=256, deps=()):
    s = rows[0].shape[0]
    tm = _pick(s, tm, 16)
    resident = pl.Buffered(1)
    nr, nb, no = len(rows), len(bcasts), len(row_outs)
    first_out = nr + nb + len(deps)

    def body(*refs):
        ins = [r[...] for r in refs[:nr + nb]]
        outs, reds = fn(*ins)
        for ref, val in zip(refs[first_out:first_out + no], outs):
            ref[...] = val.astype(ref.dtype)
        i = pl.program_id(0)
        for ref, val in zip(refs[first_out + no:], reds):
            @pl.when(i == 0)
            def _():
                ref[...] = val

            @pl.when(i > 0)
            def _():
                ref[...] += val

    in_specs = [pl.BlockSpec((tm, r.shape[1]), lambda i: (i, 0)) for r in rows]
    in_specs += [pl.BlockSpec(b.shape, lambda i: (0, 0), pipeline_mode=resident) for b in bcasts]
    in_specs += [ANY] * len(deps)
    out_specs = [pl.BlockSpec((tm, o.shape[1]), lambda i: (i, 0)) for o in row_outs]
    out_specs += [pl.BlockSpec(d.shape, lambda i: (0, 0)) for d in red_outs]
    return pl.pallas_call(
        body, name=name, grid=(s // tm,), in_specs=in_specs, out_specs=out_specs,
        out_shape=list(row_outs) + list(red_outs),
        compiler_params=_params(("arbitrary",)),
    )(*rows, *bcasts, *deps)


def _sds(shape, dtype=F32):
    return jax.ShapeDtypeStruct(shape, dtype)


def _rms(x, g):
    y = x * lax.rsqrt(jnp.mean(x * x, axis=-1, keepdims=True) + EPS)
    return y * g


def _silu(x):
    return x * jax.nn.sigmoid(x)


def _swiglu(g, u):
    return _silu(g) * u


def _ln_silu(u, g, b):
    mu = jnp.mean(u, axis=-1, keepdims=True)
    var = jnp.mean(jnp.square(u - mu), axis=-1, keepdims=True)
    return _silu((u - mu) * lax.rsqrt(var + EPS) * g + b)


def _merge(conv_pre, att_out, g_conv, g_att, b_cb):
    return jax.nn.sigmoid(g_conv) * (conv_pre + b_cb) + jax.nn.sigmoid(g_att) * att_out


def _glu(t):
    return t[:, :CONV_DIM] * jax.nn.sigmoid(t[:, CONV_DIM:])


def _conv_fwd(conv_in, w_pad, b, ln_g, ln_b, tm=256):
    s = conv_in.shape[0]
    tm = _pick(s, tm, HALO)
    ratio = tm // HALO

    def body(main_ref, halo_ref, w_ref, b_ref, g_ref, be_ref, u3_ref, u1_ref, buf):
        i = pl.program_id(0)
        buf[0:HALO, :] = _glu(halo_ref[...]) * (i > 0).astype(F32)
        buf[HALO:HALO + tm, :] = _glu(main_ref[...])
        acc = jnp.zeros((tm, CONV_DIM), F32) + b_ref[...]
        for j in range(CONV_WIDTH):
            acc = acc + w_ref[j:j + 1, :] * buf[pl.ds(HALO - (CONV_WIDTH - 1) + j, tm), :]
        u1_ref[...] = acc
        u3_ref[...] = _ln_silu(acc, g_ref[...], be_ref[...]).astype(u3_ref.dtype)

    return pl.pallas_call(
        body, name="conv_fwd", grid=(s // tm,),
        in_specs=[pl.BlockSpec((tm, 2 * CONV_DIM), lambda i: (i, 0)),
                  pl.BlockSpec((HALO, 2 * CONV_DIM), lambda i: (jnp.maximum(i * ratio - 1, 0), 0)),
                  pl.BlockSpec(w_pad.shape, lambda i: (0, 0)),
                  pl.BlockSpec(b.shape, lambda i: (0, 0)),
                  pl.BlockSpec(ln_g.shape, lambda i: (0, 0)),
                  pl.BlockSpec(ln_b.shape, lambda i: (0, 0))],
        out_specs=[pl.BlockSpec((tm, CONV_DIM), lambda i: (i, 0)),
                   pl.BlockSpec((tm, CONV_DIM), lambda i: (i, 0))],
        out_shape=[_sds((s, CONV_DIM), BF16), _sds((s, CONV_DIM), F32)],
        scratch_shapes=[pltpu.VMEM((tm + HALO, CONV_DIM), F32)],
        compiler_params=_params(("arbitrary",)),
    )(conv_in, conv_in, w_pad, b, ln_g, ln_b)


def _conv_bwd(conv_in, du1, w_pad, tm=256):
    s = conv_in.shape[0]
    tm = _pick(s, tm, HALO)
    ratio = tm // HALO
    nt = s // tm
    last_halo = s // HALO - 1

    def body(main_ref, halo_ref, du_ref, dun_ref, w_ref, dci_ref, dw_ref, db_ref, ubuf, dbuf):
        i = pl.program_id(0)
        main = main_ref[...]
        a = main[:, :CONV_DIM]
        sb = jax.nn.sigmoid(main[:, CONV_DIM:])
        ubuf[0:HALO, :] = _glu(halo_ref[...]) * (i > 0).astype(F32)
        ubuf[HALO:HALO + tm, :] = a * sb
        du = du_ref[...]
        dbuf[0:tm, :] = du
        dbuf[tm:tm + HALO, :] = dun_ref[...] * (i < nt - 1).astype(F32)

        @pl.when(i == 0)
        def _():
            dw_ref[...] = jnp.zeros_like(dw_ref)
            db_ref[...] = jnp.zeros_like(db_ref)

        du0 = jnp.zeros((tm, CONV_DIM), F32)
        for j in range(CONV_WIDTH):
            du0 = du0 + w_ref[j:j + 1, :] * dbuf[pl.ds(CONV_WIDTH - 1 - j, tm), :]
            dw_ref[j:j + 1, :] += jnp.sum(du * ubuf[pl.ds(HALO - (CONV_WIDTH - 1) + j, tm), :], axis=0, keepdims=True)
        db_ref[...] += jnp.sum(du, axis=0, keepdims=True)
        dci_ref[:, :CONV_DIM] = (du0 * sb).astype(dci_ref.dtype)
        dci_ref[:, CONV_DIM:] = (du0 * a * sb * (1.0 - sb)).astype(dci_ref.dtype)

    return pl.pallas_call(
        body, name="conv_bwd", grid=(nt,),
        in_specs=[pl.BlockSpec((tm, 2 * CONV_DIM), lambda i: (i, 0)),
                  pl.BlockSpec((HALO, 2 * CONV_DIM), lambda i: (jnp.maximum(i * ratio - 1, 0), 0)),
                  pl.BlockSpec((tm, CONV_DIM), lambda i: (i, 0)),
                  pl.BlockSpec((HALO, CONV_DIM), lambda i: (jnp.minimum((i + 1) * ratio, last_halo), 0)),
                  pl.BlockSpec(w_pad.shape, lambda i: (0, 0))],
        out_specs=[pl.BlockSpec((tm, 2 * CONV_DIM), lambda i: (i, 0)),
                   pl.BlockSpec(w_pad.shape, lambda i: (0, 0)),
                   pl.BlockSpec((1, CONV_DIM), lambda i: (0, 0))],
        out_shape=[_sds((s, 2 * CONV_DIM), BF16), _sds(w_pad.shape), _sds((1, CONV_DIM))],
        scratch_shapes=[pltpu.VMEM((tm + HALO, CONV_DIM), F32), pltpu.VMEM((tm + HALO, CONV_DIM), F32)],
        compiler_params=_params(("arbitrary",)),
    )(conv_in, conv_in, du1, du1, w_pad)


def _logsig_neg(z):
    return jnp.minimum(-z, 0.0) - jnp.log(1.0 + jnp.exp(-jnp.abs(z)))


def _split_dot(val, tri):
    hi = val.astype(BF16)
    lo = (val - hi.astype(F32)).astype(BF16)
    return jnp.dot(hi, tri, preferred_element_type=F32) + jnp.dot(lo, tri, preferred_element_type=F32)


def _attn_masks(t, later):
    row = lax.broadcasted_iota(jnp.int32, (t, t), 0)
    col = lax.broadcasted_iota(jnp.int32, (t, t), 1)
    tri = jnp.where(row > col if later else row <= col, 1.0, 0.0).astype(BF16)
    return col < row, tri


def _grid_marks(h, nq):
    hh, i = pl.program_id(0), pl.program_id(1)
    return (hh == 0) & (i == 0), (hh == (3 * h) // 4) & (i == 0), (hh == h - 1) & (i == nq - 1)


def _head_masks(shape):
    lane = lax.broadcasted_iota(jnp.int32, shape, len(shape) - 1)
    return lane < HEAD_DIM, lane >= HEAD_DIM


def _per_head(blk):
    m0, m1 = _head_masks(blk.shape)
    zero = jnp.zeros_like(blk)
    return jnp.where(m0, blk, zero), jnp.where(m1, blk, zero)


NT = (((1,), (1,)), ((), ()))
TN = (((0,), (0,)), ((), ()))


def _attn_fwd(q, k, v, exchange):
    s = q.shape[0]
    hp = q.shape[1] // LANES
    t = ATT_TILE
    scale = 1.0 / math.sqrt(HEAD_DIM)
    x_arrs, x_shape, x_scratch, _ = exchange
    nx = len(x_arrs)

    def body(*refs):
        q_ref, k_ref, v_ref = refs[:3]
        o_ref, lt_ref = refs[3 + nx:5 + nx]
        finish_exchange = _carry_exchange(exchange, refs, 3, 2, *_grid_marks(hp, s // t))
        i = pl.program_id(1)
        qs = _per_head((q_ref[...].astype(F32) * scale).astype(BF16))
        causal, tri = _attn_masks(t, later=True)

        def step(kb, carry, masked):
            cs, acc = carry
            off = pl.multiple_of(kb * t, t)
            kblk = k_ref[pl.ds(off, t), :]
            vs = _per_head(v_ref[pl.ds(off, t), :])
            new_cs = []
            for hd in range(2):
                z = lax.dot_general(qs[hd], kblk, NT, preferred_element_type=F32)
                l = _logsig_neg(z)
                if masked:
                    l = jnp.where(causal, l, 0.0)
                e = z + l + _split_dot(l, tri) + cs[hd]
                if masked:
                    e = jnp.where(causal, e, -1e30)
                acc = acc + jnp.dot(jnp.exp(e).astype(BF16), vs[hd], preferred_element_type=F32)
                new_cs.append(cs[hd] + jnp.sum(l, axis=1, keepdims=True))
            return tuple(new_cs), acc

        zero = jnp.zeros((t, 1), F32)
        carry = step(i, ((zero, zero), jnp.zeros((t, LANES), F32)), True)
        carry = lax.fori_loop(0, i, lambda n, cr: step(i - 1 - n, cr, False), carry)
        m0, _ = _head_masks((t, LANES))
        lt_ref[...] = jnp.where(m0, carry[0][0], carry[0][1])
        o_ref[...] = carry[1].astype(o_ref.dtype)
        finish_exchange()

    res = pl.pallas_call(
        body, name="attn_fwd", grid=(hp, s // t),
        in_specs=[pl.BlockSpec((t, LANES), lambda p, i: (i, p)),
                  pl.BlockSpec((s, LANES), lambda p, i: (0, p)),
                  pl.BlockSpec((s, LANES), lambda p, i: (0, p))] + [ANY] * nx,
        out_specs=[pl.BlockSpec((t, LANES), lambda p, i: (i, p)),
                   pl.BlockSpec((None, t, LANES), lambda p, i: (p, i, 0))] + [ANY] * nx,
        out_shape=[_sds(q.shape, BF16), _sds((hp, s, LANES), F32)] + x_shape,
        scratch_shapes=x_scratch,
        compiler_params=_params(("arbitrary", "arbitrary")),
    )(q, k, v, *x_arrs)
    return res[0], res[1], res[2:]


def _attn_bwd(q, k, v, do, ltot, deps=()):
    s = q.shape[0]
    hp = q.shape[1] // LANES
    t = ATT_TILE
    scale = 1.0 / math.sqrt(HEAD_DIM)
    nd = len(deps)

    def body(*refs):
        q_ref, k_ref, v_ref, do_ref, lt_ref = refs[:5]
        dq_ref, dk_ref, dv_ref = refs[5 + nd:8 + nd]
        i = pl.program_id(1)

        @pl.when(i == 0)
        def _():
            dk_ref[...] = jnp.zeros_like(dk_ref)
            dv_ref[...] = jnp.zeros_like(dv_ref)

        qb = q_ref[...]
        qm = _per_head(qb)
        qs = _per_head((qb.astype(F32) * scale).astype(BF16))
        dos = _per_head(do_ref[...])
        lts = (lt_ref[:, 0:1], lt_ref[:, HEAD_DIM:HEAD_DIM + 1])
        causal, tri = _attn_masks(t, later=False)

        def step(kb, carry, masked):
            cls, cgs, dq = carry
            off = pl.multiple_of(kb * t, t)
            kblk = k_ref[pl.ds(off, t), :]
            vblk = v_ref[pl.ds(off, t), :]
            ks = _per_head(kblk)
            dk = jnp.zeros((t, LANES), F32)
            dv = jnp.zeros((t, LANES), F32)
            new_cls, new_cgs = [], []
            for hd in range(2):
                z = lax.dot_general(qs[hd], kblk, NT, preferred_element_type=F32)
                l = _logsig_neg(z)
                if masked:
                    l = jnp.where(causal, l, 0.0)
                e = z + l + ((lts[hd] - cls[hd]) - _split_dot(l, tri))
                if masked:
                    e = jnp.where(causal, e, -1e30)
                a = jnp.exp(e)
                g = lax.dot_general(dos[hd], vblk, NT, preferred_element_type=F32) * a
                p = cgs[hd] + jnp.dot(g.astype(BF16), tri, preferred_element_type=F32) - g
                el = jnp.exp(l)
                dz = g * el - p * (1.0 - el)
                if masked:
                    dz = jnp.where(causal, dz, 0.0)
                dzb = (dz * scale).astype(BF16)
                dq = dq + jnp.dot(dzb, ks[hd], preferred_element_type=F32)
                dk = dk + lax.dot_general(dzb, qm[hd], TN, preferred_element_type=F32)
                dv = dv + lax.dot_general(a.astype(BF16), dos[hd], TN, preferred_element_type=F32)
                new_cls.append(cls[hd] + jnp.sum(l, axis=1, keepdims=True))
                new_cgs.append(cgs[hd] + jnp.sum(g, axis=1, keepdims=True))
            dk_ref[pl.ds(off, t), :] += dk
            dv_ref[pl.ds(off, t), :] += dv
            return tuple(new_cls), tuple(new_cgs), dq

        zero = jnp.zeros((t, 1), F32)
        init = ((zero, zero), (zero, zero), jnp.zeros((t, LANES), F32))
        carry = lax.fori_loop(0, i, lambda kb, cr: step(kb, cr, False), init)
        carry = step(i, carry, True)
        dq_ref[...] = carry[2]

    blk = pl.BlockSpec((t, LANES), lambda p, i: (i, p))
    whole = pl.BlockSpec((s, LANES), lambda p, i: (0, p))
    return pl.pallas_call(
        body, name="attn_bwd", grid=(hp, s // t),
        in_specs=[blk, whole, whole, blk, pl.BlockSpec((None, t, LANES), lambda p, i: (p, i, 0))] + [ANY] * nd,
        out_specs=[blk, whole, whole],
        out_shape=[_sds(q.shape)] * 3,
        compiler_params=_params(("parallel", "arbitrary")),
    )(q, k, v, do, ltot, *deps)


LATE = ["w_conv_branch", "w_att_branch", "w_out", "w_ffn_up", "w_ffn_down"]


def _full_weight(name, gathered):
    return _cols_to_full(gathered) if name in COL_SHARDED else gathered.reshape(-1, gathered.shape[2])


def _grad_slabs(name, grad):
    return _full_to_cols(grad) if name in COL_SHARDED else grad.reshape(N_DEV, -1, grad.shape[1])


def _local_step(x, target, w, late_blocks):
    s = x.shape[0]
    w = dict(w)
    g1, g2, g3, g4 = w["norm_mix_pre"], w["norm_mix_post"], w["norm_ffn_pre"], w["norm_ffn_post"]

    (h1,) = _rowwise("pre_norm", lambda xt, g: ((_rms(xt, g),), ()), [x], [g1], [_sds((s, D_MODEL), BF16)])
    w_in = w["w_in"]
    cols = [(IN_SPLITS[n], IN_SPLITS[n + 1] - IN_SPLITS[n]) for n in range(6)]
    conv_in = _matmul(h1, w_in, cols=cols[0], name="proj_conv")
    q = _matmul(h1, w_in, cols=cols[1], name="proj_q", out_dtype=BF16)
    k = _matmul(h1, w_in, cols=cols[2], name="proj_k", out_dtype=BF16)
    v = _matmul(h1, w_in, cols=cols[3], name="proj_v", out_dtype=BF16)
    g_conv = _matmul(h1, w_in, cols=cols[4], name="proj_gate_conv")
    g_att = _matmul(h1, w_in, cols=cols[5], name="proj_gate_att")

    u3, u1 = _conv_fwd(conv_in, w["conv_dw_w"], w["conv_dw_b"], w["conv_ln_g"], w["conv_ln_b"])
    att, ltot, gathered = _attn_fwd(q, k, v, _gather_exchange(late_blocks))
    for nm, g in zip(LATE, gathered):
        w[nm] = _full_weight(nm, g)

    def merge_fn(u3t, at, gc, ga, w_cb, w_ab, b_cb):
        cp = jnp.dot(u3t, w_cb, preferred_element_type=F32)
        ao = jnp.dot(at, w_ab, preferred_element_type=F32)
        return (_merge(cp, ao, gc, ga, b_cb), cp, ao), ()

    merged, conv_pre, att_out = _rowwise(
        "branch_merge", merge_fn, [u3, att, g_conv, g_att], [w["w_conv_branch"], w["w_att_branch"], w["b_conv_branch"]],
        [_sds((s, D_MODEL), BF16)] * 3, tm=512)

    def mid_fn(mt, xt, w_out, g2_, g3_):
        mix_ = jnp.dot(mt, w_out, preferred_element_type=F32)
        x2_ = xt + _rms(mix_, g2_)
        return (mix_, x2_, _rms(x2_, g3_)), ()

    mix, x2, h2 = _rowwise("mix_mid_norm", mid_fn, [merged, x], [w["w_out"], g2, g3],
                           [_sds((s, D_MODEL)), _sds((s, D_MODEL)), _sds((s, D_MODEL), BF16)], tm=512)

    def ffn_up_fn(ht, w_up):
        gu_ = jnp.dot(ht, w_up, preferred_element_type=F32)
        return (gu_, _swiglu(gu_[:, :D_FF], gu_[:, D_FF:])), ()

    gu, act = _rowwise("ffn_up", ffn_up_fn, [h2], [w["w_ffn_up"]],
                       [_sds((s, 2 * D_FF), BF16), _sds((s, D_FF), BF16)], tm=512)

    def final_fn(at, x2t, tgt, w_down, g4_):
        ff = jnp.dot(at, w_down, preferred_element_type=F32)
        n4, vjp = jax.vjp(_rms, ff, g4_)
        err = x2t + n4 - tgt
        dy = err * (1.0 / D_MODEL)
        dff, dg4 = vjp(dy)
        return (dy, dff), (jnp.sum(err * err, axis=0, keepdims=True), dg4)

    dy, dff, loss_cols, d_g4 = _rowwise("ffn_down_loss", final_fn, [act, x2, target], [w["w_ffn_down"], g4],
                                        [_sds((s, D_MODEL)), _sds((s, D_MODEL), BF16)],
                                        [_sds((1, D_MODEL)), _sds((1, D_MODEL))], tm=512)
    loss = 0.5 * jnp.sum(loss_cols) / D_MODEL

    d_w_down = _matmul(act, dff, ta=True, name="d_w_down", out_dtype=BF16)

    def act_bwd_fn(dfft, gut, w_down):
        d_act = lax.dot_general(dfft, w_down, NT, preferred_element_type=F32)
        gu_ = gut.astype(F32)
        _, vjp = jax.vjp(_swiglu, gu_[:, :D_FF], gu_[:, D_FF:])
        return (jnp.concatenate(vjp(d_act), axis=1),), ()

    (dgu,) = _rowwise("ffn_act_bwd", act_bwd_fn, [dff, gu], [w["w_ffn_down"]], [_sds((s, 2 * D_FF), BF16)])
    dh2 = _matmul(dgu, w["w_ffn_up"], tb=True, name="d_h2")
    d_w_up = _matmul(h2, dgu, ta=True, name="d_w_up", out_dtype=BF16)
    ffn_exchange, ffn_token = _split_start(
        "scatter_ffn_start", [_grad_slabs("w_ffn_up", d_w_up), _grad_slabs("w_ffn_down", d_w_down)], scatter=True)

    def mid_bwd_fn(xt, mt, dyt, dh, g2_, g3_):
        n2, vjp2 = jax.vjp(_rms, mt, g2_)
        x2_ = xt + n2
        _, vjp3 = jax.vjp(_rms, x2_, g3_)
        dx2_, dg3 = vjp3(dh)
        dx2_ = dx2_ + dyt
        dmix_, dg2 = vjp2(dx2_)
        return (dx2_, dmix_), (dg2, dg3)

    dx2, dmix, d_g2, d_g3 = _rowwise("mid_bwd", mid_bwd_fn, [x, mix, dy, dh2], [g2, g3],
                                     [_sds((s, D_MODEL)), _sds((s, D_MODEL), BF16)],
                                     [_sds((1, D_MODEL)), _sds((1, D_MODEL))], deps=[ffn_token])
    d_w_out = _matmul(merged, dmix, ta=True, name="d_w_out", out_dtype=BF16)

    def merge_bwd_fn(dmt, cp, ao, gc, ga, w_out, w_cb, w_ab, b_cb):
        dm = lax.dot_general(dmt, w_out, NT, preferred_element_type=F32)
        _, vjp = jax.vjp(_merge, cp.astype(F32), ao.astype(F32), gc, ga, b_cb)
        dcp, dao, dgc, dga, dbias = vjp(dm)
        dcp, dao = dcp.astype(BF16), dao.astype(BF16)
        du3_ = lax.dot_general(dcp, w_cb, NT, preferred_element_type=F32)
        datt_ = lax.dot_general(dao, w_ab, NT, preferred_element_type=F32)
        return (dcp, dao, dgc, dga, du3_, datt_), (dbias,)

    d_conv_out, d_att_out, d_g_conv, d_g_att, du3, d_att, d_b_cb = _rowwise(
        "merge_bwd", merge_bwd_fn, [dmix, conv_pre, att_out, g_conv, g_att],
        [w["w_out"], w["w_conv_branch"], w["w_att_branch"], w["b_conv_branch"]],
        [_sds((s, D_MODEL), BF16)] * 4 + [_sds((s, CONV_DIM)), _sds((s, ATT_DIM), BF16)], [_sds((1, D_MODEL))], tm=512)

    d_w_cb = _matmul(u3, d_conv_out, ta=True, name="d_w_conv_branch", out_dtype=BF16)
    d_w_ab = _matmul(att, d_att_out, ta=True, name="d_w_att_branch", out_dtype=BF16)

    mix_exchange, mix_token = _split_start(
        "scatter_mix_start", [_grad_slabs("w_conv_branch", d_w_cb), _grad_slabs("w_att_branch", d_w_ab),
                              _grad_slabs("w_out", d_w_out)], scatter=True)
    dq, dk, dv = _attn_bwd(q, k, v, d_att, ltot, deps=[mix_token])

    def ln_bwd_fn(u1t, du3t, g_, b_):
        _, vjp = jax.vjp(_ln_silu, u1t, g_, b_)
        du1_, dg_, db_ = vjp(du3t)
        return (du1_,), (dg_, db_)

    du1, d_ln_g, d_ln_b = _rowwise("conv_ln_bwd", ln_bwd_fn, [u1, du3], [w["conv_ln_g"], w["conv_ln_b"]],
                                   [_sds((s, CONV_DIM))], [_sds((1, CONV_DIM)), _sds((1, CONV_DIM))])
    d_conv_in, d_dw_w, d_dw_b = _conv_bwd(conv_in, du1, w["conv_dw_w"])

    d_proj = jnp.concatenate([d_conv_in, dq.astype(BF16), dk.astype(BF16), dv.astype(BF16), d_g_conv, d_g_att],
                             axis=1)
    received = (_split_wait("scatter_mix_wait", mix_exchange, [d_proj], scatter=True)
                + _split_wait("scatter_ffn_wait", ffn_exchange, [d_proj], scatter=True))
    d_w_in = _matmul(h1, d_proj, ta=True, name="d_w_in", out_dtype=BF16, deps=received)
    w_in_exchange, w_in_token = _split_start("scatter_w_in_start", [_grad_slabs("w_in", d_w_in)], scatter=True)
    dh1 = _matmul(d_proj, w_in, tb=True, name="d_h1", deps=[w_in_token])

    def pre_bwd_fn(xt, dh, dx2t, g_):
        _, vjp = jax.vjp(_rms, xt, g_)
        dx_, dg_ = vjp(dh)
        return (dx_ + dx2t,), (dg_,)

    grad_x, d_g1 = _rowwise("pre_bwd", pre_bwd_fn, [x, dh1, dx2], [g1], [_sds((s, D_MODEL))], [_sds((1, D_MODEL))])

    grads = {
        "norm_mix_pre": d_g1, "conv_dw_w": d_dw_w, "conv_dw_b": d_dw_b,
        "conv_ln_g": d_ln_g, "conv_ln_b": d_ln_b, "b_conv_branch": d_b_cb,
        "norm_mix_post": d_g2, "norm_ffn_pre": d_g3, "norm_ffn_post": d_g4,
    }
    return loss, grad_x, received, w_in_exchange, grads


def _place():
    x, y, c = lax.axis_index("x"), lax.axis_index("y"), lax.axis_index("c")
    return x, y, c


def _slot(px, py, pc):
    return 4 * px + 2 * py + pc


def _exchange_scratch(n):
    return [pltpu.SemaphoreType.DMA((7 * n,)), pltpu.SemaphoreType.DMA((7 * n,)), pltpu.SemaphoreType.DMA((n,))]


def _gather_exchange(arrs):
    n = len(arrs)

    def phases(ins, outs, send_sems, recv_sems, local_sems):
        x, y, c = _place()
        me, sibling = (x, y, c), (x, y, 1 - c)
        chips = [(1 - x, y), (x, 1 - y), (1 - x, 1 - y)]

        def copy(a, kk, block, to, src=None):
            dst = outs[a].at[_slot(*block)]
            return pltpu.make_async_remote_copy(
                src_ref=dst if src is None else src, dst_ref=dst,
                send_sem=send_sems.at[a * 7 + kk], recv_sem=recv_sems.at[a * 7 + kk],
                device_id=to, device_id_type=MESH)

        mine = [pltpu.make_async_copy(ins[a], outs[a].at[_slot(*me)], local_sems.at[a]) for a in range(n)]
        first = []
        for a in range(n):
            first.append(copy(a, 0, me, sibling, src=ins[a]))
            first += [copy(a, 1 + j, me, (*chip, c), src=ins[a]) for j, chip in enumerate(chips)]
        passed = [copy(a, 4 + j, (*chip, c), sibling) for j, chip in enumerate(chips) for a in range(n)]

        def send():
            for cp in mine + first:
                cp.start()

        def pass_on():
            for j, chip in enumerate(chips):
                for a in range(n):
                    copy(a, 1 + j, (*chip, c), me).wait_recv()
                    passed[j * n + a].start()

        def finish():
            for a in range(n):
                copy(a, 0, sibling, me).wait_recv()
                for j, chip in enumerate(chips):
                    copy(a, 4 + j, (*chip, 1 - c), me).wait_recv()
            for cp in first + passed:
                cp.wait_send()
            for cp in mine:
                cp.wait()

        return [send, pass_on, finish]

    return list(arrs), [_sds((N_DEV,) + a.shape, a.dtype) for a in arrs], _exchange_scratch(n), phases


HBM = pl.BlockSpec(memory_space=pltpu.HBM)
SEM = pl.BlockSpec(memory_space=pltpu.SEMAPHORE)
EFFECT = pltpu.SideEffectType.DATAFLOW_SIDE_EFFECTING


def _split_copies(scatter, ins, lands, send_sems, recv_sems):
    x, y, c = _place()
    mine = _slot(x, y, c)
    flips = [(fx, fy, fc) for fx in (0, 1) for fy in (0, 1) for fc in (0, 1)][1:]
    peers = [((1 - x) if fx else x, (1 - y) if fy else y, (1 - c) if fc else c) for fx, fy, fc in flips]
    pairs = []
    for a in range(len(ins)):
        for kk, peer in enumerate(peers):
            def copy(dst_slot, a=a, kk=kk, peer=peer):
                return pltpu.make_async_remote_copy(
                    src_ref=ins[a].at[_slot(*peer)] if scatter else ins[a], dst_ref=lands[a].at[dst_slot],
                    send_sem=send_sems.at[a * 7 + kk], recv_sem=recv_sems.at[a * 7 + kk],
                    device_id=peer, device_id_type=MESH)
            pairs.append((copy(mine), copy(_slot(*peer))))
    return pairs


def _split_start(name, arrs, scatter):
    n = len(arrs)

    def body(*refs):
        for sent, _ in _split_copies(scatter, refs[:n], refs[n:2 * n], refs[2 * n], refs[2 * n + 1]):
            sent.start()
        token = refs[-1]
        token[...] = jnp.zeros_like(token)

    sems = pltpu.SemaphoreType.DMA((7 * n,))
    land_shapes = [a.shape if scatter else (N_DEV,) + a.shape for a in arrs]
    res = pl.pallas_call(
        body, name=name,
        out_shape=(sems, sems, *[pltpu.HBM(a.shape, a.dtype) for a in arrs],
                   *[pltpu.HBM(shp, a.dtype) for shp, a in zip(land_shapes, arrs)], _sds((SUBLANES, LANES))),
        in_specs=[HBM] * (2 * n),
        out_specs=(SEM, SEM, *[HBM] * (2 * n), pl.BlockSpec(memory_space=pltpu.VMEM)),
        input_output_aliases={i: 2 + i for i in range(2 * n)},
        compiler_params=pltpu.CompilerParams(has_side_effects=EFFECT),
    )(*[pltpu.with_memory_space_constraint(a, pltpu.HBM) for a in arrs],
      *[pltpu.with_memory_space_constraint(lax.empty(shp, a.dtype), pltpu.HBM) for shp, a in zip(land_shapes, arrs)])
    return res[:-1], res[-1]


def _split_wait(name, state, after, scatter):
    send_sems, recv_sems, thru = state[0], state[1], state[2:]
    n = len(thru) // 2

    def body(*refs):
        for sent, landed in _split_copies(scatter, refs[:n], refs[n:2 * n], refs[2 * n], refs[2 * n + 1]):
            sent.wait_send()
            landed.wait_recv()

    res = pl.pallas_call(
        body, name=name,
        out_shape=[pltpu.HBM(t.shape, t.dtype) for t in thru],
        in_specs=[HBM] * (2 * n) + [SEM, SEM] + [ANY] * len(after),
        out_specs=[HBM] * (2 * n),
        input_output_aliases={i: i for i in range(2 * n)},
        compiler_params=pltpu.CompilerParams(has_side_effects=EFFECT),
    )(*thru, send_sems, recv_sems, *after)
    me = _slot(*_place())
    return [lax.dynamic_update_slice_in_dim(land, lax.dynamic_slice_in_dim(sent, me, 1, 0) if scatter else sent[None], me, 0)
            for sent, land in zip(res[:n], res[n:])]


def _exchange_call(name, exchange):
    arrs, out_shape, scratch, phases = exchange
    n = len(arrs)

    def body(*refs):
        for step in phases(refs[:n], refs[n:2 * n], *refs[2 * n:]):
            step()

    return pl.pallas_call(body, name=name, in_specs=[ANY] * n, out_specs=[ANY] * n,
                          out_shape=out_shape, scratch_shapes=scratch)(*arrs)


def _carry_exchange(exchange, refs, n_in, n_out, first, middle, last):
    arrs, _, _, phases = exchange
    n = len(arrs)
    ins = refs[n_in:n_in + n]
    outs = refs[n_in + n + n_out:n_in + 2 * n + n_out]
    steps = phases(ins, outs, *refs[n_in + 2 * n + n_out:])
    pl.when(first)(steps[0])
    if len(steps) == 3:
        pl.when(middle)(steps[1])
    return lambda: pl.when(last)(steps[-1])


def _adamw_math(w, g, m, v):
    m2 = ADAM_B1 * m + (1.0 - ADAM_B1) * g
    v2 = ADAM_B2 * v + (1.0 - ADAM_B2) * jnp.square(g)
    m_hat = m2 / (1.0 - ADAM_B1 ** ADAM_STEP)
    v_hat = v2 / (1.0 - ADAM_B2 ** ADAM_STEP)
    delta = -ADAM_LR * (m_hat / (jnp.sqrt(v_hat) + ADAM_EPS) + ADAM_WD * w)
    return delta, m2, v2


def _sum_adamw(name, parts, w, m, v, tr=256, deps=()):
    p, r, c = parts.shape
    tr = _pick(r, tr, 16)

    def body(p_ref, w_ref, m_ref, v_ref, *rest):
        g_ref, d_ref, m2_ref, v2_ref = rest[len(deps):]
        g = p_ref[0].astype(F32)
        for d in range(1, p):
            g = g + p_ref[d].astype(F32)
        delta, m2, v2 = _adamw_math(w_ref[...], g, m_ref[...], v_ref[...])
        g_ref[...] = g
        d_ref[...] = delta
        m2_ref[...] = m2
        v2_ref[...] = v2

    tile = pl.BlockSpec((tr, c), lambda i: (i, 0))
    return pl.pallas_call(
        body, name=name, grid=(r // tr,),
        in_specs=[pl.BlockSpec((p, tr, c), lambda i: (0, i, 0)), tile, tile, tile] + [ANY] * len(deps),
        out_specs=[tile] * 4, out_shape=[_sds((r, c))] * 4,
        compiler_params=_params(("parallel",)),
    )(parts, w, m, v, *deps)


def _sum_parts(name, parts):
    p, r, c = parts.shape

    def body(p_ref, o_ref):
        g = p_ref[0]
        for d in range(1, p):
            g = g + p_ref[d]
        o_ref[...] = g

    return pl.pallas_call(
        body, name=name, out_shape=_sds((r, c)),
        in_specs=[pl.BlockSpec(memory_space=pltpu.VMEM)], out_specs=pl.BlockSpec(memory_space=pltpu.VMEM),
    )(parts)


WEIGHTS = ["norm_mix_pre", "w_in", "conv_dw_w", "conv_dw_b", "conv_ln_g", "conv_ln_b", "w_conv_branch",
           "b_conv_branch", "w_att_branch", "w_out", "norm_mix_post", "norm_ffn_pre", "w_ffn_up", "w_ffn_down",
           "norm_ffn_post"]
COL_SHARDED = ["w_in", "w_conv_branch", "w_att_branch", "w_ffn_up"]
ROW_SHARDED = ["w_out", "w_ffn_down"]
VECTORS = ["norm_mix_pre", "conv_dw_b", "conv_ln_g", "conv_ln_b", "b_conv_branch", "norm_mix_post",
           "norm_ffn_pre", "norm_ffn_post"]


def _cols_to_full(g):
    return g.transpose(1, 0, 2).reshape(g.shape[1], N_DEV * g.shape[2])


def _full_to_cols(f):
    return f.reshape(f.shape[0], N_DEV, f.shape[1] // N_DEV).transpose(1, 0, 2)


def _pack_vectors(vecs):
    rows = [jnp.pad(vecs[nm].reshape(-1), (0, D_MODEL - vecs[nm].size)) for nm in VECTORS]
    return jnp.stack(rows)


def _unpack_vectors(packed, sizes):
    return {nm: packed[n, :sizes[nm]] for n, nm in enumerate(VECTORS)}


def kernel(x, norm_mix_pre, w_in, conv_dw_w, conv_dw_b, conv_ln_g, conv_ln_b, w_conv_branch, b_conv_branch, w_att_branch, w_out, norm_mix_post, norm_ffn_pre, w_ffn_up, w_ffn_down, norm_ffn_post, loss_target, m_norm_mix_pre, m_w_in, m_conv_dw_w, m_conv_dw_b, m_conv_ln_g, m_conv_ln_b, m_w_conv_branch, m_b_conv_branch, m_w_att_branch, m_w_out, m_norm_mix_post, m_norm_ffn_pre, m_w_ffn_up, m_w_ffn_down, m_norm_ffn_post, v_norm_mix_pre, v_w_in, v_conv_dw_w, v_conv_dw_b, v_conv_ln_g, v_conv_ln_b, v_w_conv_branch, v_b_conv_branch, v_w_att_branch, v_w_out, v_norm_mix_post, v_norm_ffn_pre, v_w_ffn_up, v_w_ffn_down, v_norm_ffn_post):
    ws = dict(zip(WEIGHTS, [norm_mix_pre, w_in, conv_dw_w, conv_dw_b, conv_ln_g, conv_ln_b, w_conv_branch,
                            b_conv_branch, w_att_branch, w_out, norm_mix_post, norm_ffn_pre, w_ffn_up, w_ffn_down,
                            norm_ffn_post]))
    ms = dict(zip(WEIGHTS, [m_norm_mix_pre, m_w_in, m_conv_dw_w, m_conv_dw_b, m_conv_ln_g, m_conv_ln_b,
                            m_w_conv_branch, m_b_conv_branch, m_w_att_branch, m_w_out, m_norm_mix_post,
                            m_norm_ffn_pre, m_w_ffn_up, m_w_ffn_down, m_norm_ffn_post]))
    vs = dict(zip(WEIGHTS, [v_norm_mix_pre, v_w_in, v_conv_dw_w, v_conv_dw_b, v_conv_ln_g, v_conv_ln_b,
                            v_w_conv_branch, v_b_conv_branch, v_w_att_branch, v_w_out, v_norm_mix_post,
                            v_norm_ffn_pre, v_w_ffn_up, v_w_ffn_down, v_norm_ffn_post]))

    dw_block = jnp.pad(conv_dw_w, ((0, 1), (0, 0)))
    g_in, g_dw = _exchange_call("gather_first", _gather_exchange([w_in.astype(BF16), dw_block]))
    full = {"w_in": _full_weight("w_in", g_in), "conv_dw_w": _cols_to_full(g_dw)}
    for nm in VECTORS:
        full[nm] = ws[nm].reshape(1, -1)

    loss_local, grad_x, received, w_in_exchange, grads = _local_step(
        x[0], loss_target[0], full, [ws[nm].astype(BF16) for nm in LATE])
    loss = lax.psum(loss_local, ("x", "y", "c"))

    small_exchange, small_token = _split_start(
        "gather_small_start", [_pack_vectors(grads), grads["conv_dw_w"]], scatter=False)
    out_g, out_d, out_m, out_v = {}, {}, {}, {}
    for nm, parts in zip(LATE, received):
        out_g[nm], out_d[nm], out_m[nm], out_v[nm] = _sum_adamw("adamw_" + nm, parts, ws[nm], ms[nm], vs[nm],
                                                                deps=[small_token])
    done = [grad_x] + [out_v[nm] for nm in LATE]
    (recv_in,) = _split_wait("scatter_w_in_wait", w_in_exchange, done, scatter=True)
    small = _split_wait("gather_small_wait", small_exchange, done, scatter=False)
    nm = "w_in"
    out_g[nm], out_d[nm], out_m[nm], out_v[nm] = _sum_adamw("adamw_" + nm, recv_in, ws[nm], ms[nm], vs[nm])
    sizes = {nm: ws[nm].size for nm in VECTORS}
    vec = _sum_adamw("adamw_vectors", small[0], _pack_vectors(ws), _pack_vectors(ms), _pack_vectors(vs))
    for res, dst in zip(vec, (out_g, out_d, out_m, out_v)):
        dst.update(_unpack_vectors(res, sizes))
    dw_full = _sum_parts("sum_dw_grads", small[1])
    me = _slot(*_place())
    dw_mine = lax.dynamic_slice(dw_full, (0, me * (CONV_DIM // N_DEV)), (CONV_WIDTH, CONV_DIM // N_DEV))
    nm = "conv_dw_w"
    out_g[nm], out_d[nm], out_m[nm], out_v[nm] = _sum_adamw("adamw_dw", dw_mine[None], ws[nm], ms[nm], vs[nm])

    outs = [loss, grad_x[None]]
    for group in (out_g, out_d, out_m, out_v):
        outs += [group[nm] for nm in WEIGHTS]
    return tuple(outs)
```

```python
import functools
import math

import jax
import jax.numpy as jnp
from jax import lax
from jax.experimental import pallas as pl
from jax.experimental.pallas import tpu as pltpu

F32 = jnp.float32
BF16 = jnp.bfloat16

N_DEV = 8
D_MODEL = 1024
CONV_DIM = 512
CONV_WIDTH = 31
N_HEADS = 8
HEAD_DIM = 64
ATT_DIM = N_HEADS * HEAD_DIM
D_FF = 2816
EPS = 1e-6
IN_SPLITS = (0, 1024, 1536, 2048, 2560, 3584, 4608)

ADAM_LR = 0.001
ADAM_B1 = 0.9
ADAM_B2 = 0.999
ADAM_EPS = 1e-08
ADAM_WD = 0.01
ADAM_STEP = 10

LANES = 128
SUBLANES = 8
HALO = 32
ATT_TILE = 256
VMEM_LIMIT = 56 * 1024 * 1024
MESH = pl.DeviceIdType.MESH
ANY = pl.BlockSpec(memory_space=pl.ANY)


def _pick(dim, target, align=LANES):
    t = min(dim, target)
    t -= t % align
    while t >= align:
        if dim % t == 0:
            return t
        t -= align
    return dim


def _params(semantics):
    return pltpu.CompilerParams(dimension_semantics=semantics, vmem_limit_bytes=VMEM_LIMIT)


def _matmul(a, b, *, name, ta=False, tb=False, out_dtype=F32, deps=()):
    m, k = (a.shape[1], a.shape[0]) if ta else a.shape
    n, k2 = b.shape if tb else (b.shape[1], b.shape[0])
    assert k == k2, (a.shape, b.shape, ta, tb)
    tm, tn, tk = _pick(m, 1408 if ta else 512), _pick(n, 1536), _pick(k, 1536)
    nk = k // tk
    dims = (((0 if ta else 1,), (1 if tb else 0,)), ((), ()))

    def body(a_ref, b_ref, *rest):
        o_ref = rest[len(deps)]
        part = lax.dot_general(a_ref[...], b_ref[...], dims, preferred_element_type=F32)
        if nk == 1:
            o_ref[...] = part.astype(o_ref.dtype)
            return
        acc_ref = rest[len(deps) + 1]
        kk = pl.program_id(2)

        @pl.when(kk == 0)
        def _():
            acc_ref[...] = part

        @pl.when((kk > 0) & (kk < nk - 1))
        def _():
            acc_ref[...] += part

        @pl.when(kk == nk - 1)
        def _():
            o_ref[...] = (acc_ref[...] + part).astype(o_ref.dtype)

    a_spec = pl.BlockSpec((tk, tm), lambda j, i, kk: (kk, i)) if ta else pl.BlockSpec((tm, tk), lambda j, i, kk: (i, kk))
    b_spec = (pl.BlockSpec((tn, tk), lambda j, i, kk: (j, kk)) if tb
              else pl.BlockSpec((tk, tn), lambda j, i, kk: (kk, j)))
    return pl.pallas_call(
        body, name=name, grid=(n // tn, m // tm, nk),
        in_specs=[a_spec, b_spec] + [ANY] * len(deps),
        out_specs=pl.BlockSpec((tm, tn), lambda j, i, kk: (i, j)),
        out_shape=jax.ShapeDtypeStruct((m, n), out_dtype),
        scratch_shapes=[pltpu.VMEM((tm, tn), F32)] if nk > 1 else [],
        compiler_params=_params(("parallel", "parallel", "arbitrary")),
    )(a, b, *deps)


def _rowwise(name, fn, rows, bcasts, row_outs, red_outs=(), tm=256, deps=()):
    s = rows[0].shape[0]
    tm = _pick(s, tm, 16)
    resident = pl.Buffered(1)
    nr, nb, no = len(rows), len(bcasts), len(row_outs)
    first_out = nr + nb + len(deps)

    def body(*refs):
        ins = [r[...] for r in refs[:nr + nb]]
        outs, reds = fn(*ins)
        for ref, val in zip(refs[first_out:first_out + no], outs):
            ref[...] = val.astype(ref.dtype)
        i = pl.program_id(0)
        for ref, val in zip(refs[first_out + no:], reds):
            @pl.when(i == 0)
            def _():
                ref[...] = val

            @pl.when(i > 0)
            def _():
                ref[...] += val

    in_specs = [pl.BlockSpec((tm, r.shape[1]), lambda i: (i, 0)) for r in rows]
    in_specs += [pl.BlockSpec(b.shape, lambda i: (0, 0), pipeline_mode=resident) for b in bcasts]
    in_specs += [ANY] * len(deps)
    out_specs = [pl.BlockSpec((tm, o.shape[1]), lambda i: (i, 0)) for o in row_outs]
    out_specs += [pl.BlockSpec(d.shape, lambda i: (0, 0)) for d in red_outs]
    return pl.pallas_call(
        body, name=name, grid=(s // tm,), in_specs=in_specs, out_specs=out_specs,
        out_shape=list(row_outs) + list(red_outs),
        compiler_params=_params(("arbitrary",)),
    )(*rows, *bcasts, *deps)


def _sds(shape, dtype=F32):
    return jax.ShapeDtypeStruct(shape, dtype)


def _rms(x, g):
    y = x * lax.rsqrt(jnp.mean(x * x, axis=-1, keepdims=True) + EPS)
    return y * g


def _silu(x):
    return x * jax.nn.sigmoid(x)


def _swiglu(g, u):
    return _silu(g) * u


def _ln_silu(u, g, b):
    mu = jnp.mean(u, axis=-1, keepdims=True)
    var = jnp.mean(jnp.square(u - mu), axis=-1, keepdims=True)
    return _silu((u - mu) * lax.rsqrt(var + EPS) * g + b)


def _merge(conv_pre, att_out, g_conv, g_att, b_cb):
    return jax.nn.sigmoid(g_conv) * (conv_pre + b_cb) + jax.nn.sigmoid(g_att) * att_out


def _glu(t):
    return t[:, :CONV_DIM] * jax.nn.sigmoid(t[:, CONV_DIM:])


def _conv_fwd(conv_in, w_pad, b, ln_g, ln_b, tm=256):
    s = conv_in.shape[0]
    tm = _pick(s, tm, HALO)
    ratio = tm // HALO

    def body(main_ref, halo_ref, w_ref, b_ref, g_ref, be_ref, u3_ref, u1_ref, buf):
        i = pl.program_id(0)
        buf[0:HALO, :] = _glu(halo_ref[...]) * (i > 0).astype(F32)
        buf[HALO:HALO + tm, :] = _glu(main_ref[...])
        acc = jnp.zeros((tm, CONV_DIM), F32) + b_ref[...]
        for j in range(CONV_WIDTH):
            acc = acc + w_ref[j:j + 1, :] * buf[pl.ds(HALO - (CONV_WIDTH - 1) + j, tm), :]
        u1_ref[...] = acc
        u3_ref[...] = _ln_silu(acc, g_ref[...], be_ref[...]).astype(u3_ref.dtype)

    return pl.pallas_call(
        body, name="conv_fwd", grid=(s // tm,),
        in_specs=[pl.BlockSpec((tm, 2 * CONV_DIM), lambda i: (i, 0)),
                  pl.BlockSpec((HALO, 2 * CONV_DIM), lambda i: (jnp.maximum(i * ratio - 1, 0), 0)),
                  pl.BlockSpec(w_pad.shape, lambda i: (0, 0)),
                  pl.BlockSpec(b.shape, lambda i: (0, 0)),
                  pl.BlockSpec(ln_g.shape, lambda i: (0, 0)),
                  pl.BlockSpec(ln_b.shape, lambda i: (0, 0))],
        out_specs=[pl.BlockSpec((tm, CONV_DIM), lambda i: (i, 0)),
                   pl.BlockSpec((tm, CONV_DIM), lambda i: (i, 0))],
        out_shape=[_sds((s, CONV_DIM), BF16), _sds((s, CONV_DIM), F32)],
        scratch_shapes=[pltpu.VMEM((tm + HALO, CONV_DIM), F32)],
        compiler_params=_params(("arbitrary",)),
    )(conv_in, conv_in, w_pad, b, ln_g, ln_b)


def _conv_bwd(conv_in, du1, w_pad, tm=256):
    s = conv_in.shape[0]
    tm = _pick(s, tm, HALO)
    ratio = tm // HALO
    nt = s // tm
    last_halo = s // HALO - 1

    def body(main_ref, halo_ref, du_ref, dun_ref, w_ref, dci_ref, dw_ref, db_ref, ubuf, dbuf):
        i = pl.program_id(0)
        main = main_ref[...]
        a = main[:, :CONV_DIM]
        sb = jax.nn.sigmoid(main[:, CONV_DIM:])
        ubuf[0:HALO, :] = _glu(halo_ref[...]) * (i > 0).astype(F32)
        ubuf[HALO:HALO + tm, :] = a * sb
        du = du_ref[...]
        dbuf[0:tm, :] = du
        dbuf[tm:tm + HALO, :] = dun_ref[...] * (i < nt - 1).astype(F32)

        @pl.when(i == 0)
        def _():
            dw_ref[...] = jnp.zeros_like(dw_ref)
            db_ref[...] = jnp.zeros_like(db_ref)

        du0 = jnp.zeros((tm, CONV_DIM), F32)
        for j in range(CONV_WIDTH):
            du0 = du0 + w_ref[j:j + 1, :] * dbuf[pl.ds(CONV_WIDTH - 1 - j, tm), :]
            dw_ref[j:j + 1, :] += jnp.sum(du * ubuf[pl.ds(HALO - (CONV_WIDTH - 1) + j, tm), :], axis=0, keepdims=True)
        db_ref[...] += jnp.sum(du, axis=0, keepdims=True)
        dci_ref[:, :CONV_DIM] = (du0 * sb).astype(dci_ref.dtype)
        dci_ref[:, CONV_DIM:] = (du0 * a * sb * (1.0 - sb)).astype(dci_ref.dtype)

    return pl.pallas_call(
        body, name="conv_bwd", grid=(nt,),
        in_specs=[pl.BlockSpec((tm, 2 * CONV_DIM), lambda i: (i, 0)),
                  pl.BlockSpec((HALO, 2 * CONV_DIM), lambda i: (jnp.maximum(i * ratio - 1, 0), 0)),
                  pl.BlockSpec((tm, CONV_DIM), lambda i: (i, 0)),
                  pl.BlockSpec((HALO, CONV_DIM), lambda i: (jnp.minimum((i + 1) * ratio, last_halo), 0)),
                  pl.BlockSpec(w_pad.shape, lambda i: (0, 0))],
        out_specs=[pl.BlockSpec((tm, 2 * CONV_DIM), lambda i: (i, 0)),
                   pl.BlockSpec(w_pad.shape, lambda i: (0, 0)),
                   pl.BlockSpec((1, CONV_DIM), lambda i: (0, 0))],
        out_shape=[_sds((s, 2 * CONV_DIM), BF16), _sds(w_pad.shape), _sds((1, CONV_DIM))],
        scratch_shapes=[pltpu.VMEM((tm + HALO, CONV_DIM), F32), pltpu.VMEM((tm + HALO, CONV_DIM), F32)],
        compiler_params=_params(("arbitrary",)),
    )(conv_in, conv_in, du1, du1, w_pad)


def _logsig_neg(z):
    return jnp.minimum(-z, 0.0) - jnp.log(1.0 + jnp.exp(-jnp.abs(z)))


def _split_dot(val, tri):
    hi = val.astype(BF16)
    lo = (val - hi.astype(F32)).astype(BF16)
    return jnp.dot(hi, tri, preferred_element_type=F32) + jnp.dot(lo, tri, preferred_element_type=F32)


def _attn_masks(t, later):
    row = lax.broadcasted_iota(jnp.int32, (t, t), 0)
    col = lax.broadcasted_iota(jnp.int32, (t, t), 1)
    tri = jnp.where(row > col if later else row <= col, 1.0, 0.0).astype(BF16)
    return col < row, tri


def _grid_marks(h, nq):
    hh, i = pl.program_id(0), pl.program_id(1)
    return (hh == 0) & (i == 0), (hh == (3 * h) // 4) & (i == 0), (hh == h - 1) & (i == nq - 1)


def _head_masks(shape):
    lane = lax.broadcasted_iota(jnp.int32, shape, len(shape) - 1)
    return lane < HEAD_DIM, lane >= HEAD_DIM


def _per_head(blk):
    m0, m1 = _head_masks(blk.shape)
    zero = jnp.zeros_like(blk)
    return jnp.where(m0, blk, zero), jnp.where(m1, blk, zero)


NT = (((1,), (1,)), ((), ()))
TN = (((0,), (0,)), ((), ()))


def _attn_fwd(q, k, v, exchange):
    s = q.shape[0]
    hp = q.shape[1] // LANES
    t = ATT_TILE
    scale = 1.0 / math.sqrt(HEAD_DIM)
    x_arrs, x_shape, x_scratch, _ = exchange
    nx = len(x_arrs)

    def body(*refs):
        q_ref, k_ref, v_ref = refs[:3]
        o_ref, lt_ref = refs[3 + nx:5 + nx]
        finish_exchange = _carry_exchange(exchange, refs, 3, 2, *_grid_marks(hp, s // t))
        i = pl.program_id(1)
        qs = _per_head((q_ref[...].astype(F32) * scale).astype(BF16))
        causal, tri = _attn_masks(t, later=True)

        def step(kb, carry, masked):
            cs, acc = carry
            off = pl.multiple_of(kb * t, t)
            kblk = k_ref[pl.ds(off, t), :]
            vs = _per_head(v_ref[pl.ds(off, t), :])
            new_cs = []
            for hd in range(2):
                z = lax.dot_general(qs[hd], kblk, NT, preferred_element_type=F32)
                l = _logsig_neg(z)
                if masked:
                    l = jnp.where(causal, l, 0.0)
                e = z + l + _split_dot(l, tri) + cs[hd]
                if masked:
                    e = jnp.where(causal, e, -1e30)
                acc = acc + jnp.dot(jnp.exp(e).astype(BF16), vs[hd], preferred_element_type=F32)
                new_cs.append(cs[hd] + jnp.sum(l, axis=1, keepdims=True))
            return tuple(new_cs), acc

        zero = jnp.zeros((t, 1), F32)
        carry = step(i, ((zero, zero), jnp.zeros((t, LANES), F32)), True)
        carry = lax.fori_loop(0, i, lambda n, cr: step(i - 1 - n, cr, False), carry)
        m0, _ = _head_masks((t, LANES))
        lt_ref[...] = jnp.where(m0, carry[0][0], carry[0][1])
        o_ref[...] = carry[1].astype(o_ref.dtype)
        finish_exchange()

    res = pl.pallas_call(
        body, name="attn_fwd", grid=(hp, s // t),
        in_specs=[pl.BlockSpec((t, LANES), lambda p, i: (i, p)),
                  pl.BlockSpec((s, LANES), lambda p, i: (0, p)),
                  pl.BlockSpec((s, LANES), lambda p, i: (0, p))] + [ANY] * nx,
        out_specs=[pl.BlockSpec((t, LANES), lambda p, i: (i, p)),
                   pl.BlockSpec((None, t, LANES), lambda p, i: (p, i, 0))] + [ANY] * nx,
        out_shape=[_sds(q.shape, BF16), _sds((hp, s, LANES), F32)] + x_shape,
        scratch_shapes=x_scratch,
        compiler_params=_params(("arbitrary", "arbitrary")),
    )(q, k, v, *x_arrs)
    return res[0], res[1], res[2:]


def _attn_bwd(q, k, v, do, ltot, deps=()):
    s = q.shape[0]
    hp = q.shape[1] // LANES
    t = ATT_TILE
    scale = 1.0 / math.sqrt(HEAD_DIM)
    nd = len(deps)

    def body(*refs):
        q_ref, k_ref, v_ref, do_ref, lt_ref = refs[:5]
        dq_ref, dk_ref, dv_ref = refs[5 + nd:8 + nd]
        i = pl.program_id(1)

        @pl.when(i == 0)
        def _():
            dk_ref[...] = jnp.zeros_like(dk_ref)
            dv_ref[...] = jnp.zeros_like(dv_ref)

        qb = q_ref[...]
        qm = _per_head(qb)
        qs = _per_head((qb.astype(F32) * scale).astype(BF16))
        dos = _per_head(do_ref[...])
        lts = (lt_ref[:, 0:1], lt_ref[:, HEAD_DIM:HEAD_DIM + 1])
        causal, tri = _attn_masks(t, later=False)

        def step(kb, carry, masked):
            cls, cgs, dq = carry
            off = pl.multiple_of(kb * t, t)
            kblk = k_ref[pl.ds(off, t), :]
            vblk = v_ref[pl.ds(off, t), :]
            ks = _per_head(kblk)
            dk = jnp.zeros((t, LANES), F32)
            dv = jnp.zeros((t, LANES), F32)
            new_cls, new_cgs = [], []
            for hd in range(2):
                z = lax.dot_general(qs[hd], kblk, NT, preferred_element_type=F32)
                l = _logsig_neg(z)
                if masked:
                    l = jnp.where(causal, l, 0.0)
                e = z + l + ((lts[hd] - cls[hd]) - _split_dot(l, tri))
                if masked:
                    e = jnp.where(causal, e, -1e30)
                a = jnp.exp(e)
                g = lax.dot_general(dos[hd], vblk, NT, preferred_element_type=F32) * a
                p = cgs[hd] + jnp.dot(g.astype(BF16), tri, preferred_element_type=F32) - g
                el = jnp.exp(l)
                dz = g * el - p * (1.0 - el)
                if masked:
                    dz = jnp.where(causal, dz, 0.0)
                dzb = (dz * scale).astype(BF16)
                dq = dq + jnp.dot(dzb, ks[hd], preferred_element_type=F32)
                dk = dk + lax.dot_general(dzb, qm[hd], TN, preferred_element_type=F32)
                dv = dv + lax.dot_general(a.astype(BF16), dos[hd], TN, preferred_element_type=F32)
                new_cls.append(cls[hd] + jnp.sum(l, axis=1, keepdims=True))
                new_cgs.append(cgs[hd] + jnp.sum(g, axis=1, keepdims=True))
            dk_ref[pl.ds(off, t), :] += dk
            dv_ref[pl.ds(off, t), :] += dv
            return tuple(new_cls), tuple(new_cgs), dq

        zero = jnp.zeros((t, 1), F32)
        init = ((zero, zero), (zero, zero), jnp.zeros((t, LANES), F32))
        carry = lax.fori_loop(0, i, lambda kb, cr: step(kb, cr, False), init)
        carry = step(i, carry, True)
        dq_ref[...] = carry[2]

    blk = pl.BlockSpec((t, LANES), lambda p, i: (i, p))
    whole = pl.BlockSpec((s, LANES), lambda p, i: (0, p))
    return pl.pallas_call(
        body, name="attn_bwd", grid=(hp, s // t),
        in_specs=[blk, whole, whole, blk, pl.BlockSpec((None, t, LANES), lambda p, i: (p, i, 0))] + [ANY] * nd,
        out_specs=[blk, whole, whole],
        out_shape=[_sds(q.shape)] * 3,
        compiler_params=_params(("parallel", "arbitrary")),
    )(q, k, v, do, ltot, *deps)


LATE = ["w_conv_branch", "w_att_branch", "w_out", "w_ffn_up", "w_ffn_down"]


def _full_weight(name, gathered):
    return _cols_to_full(gathered) if name in COL_SHARDED else gathered.reshape(-1, gathered.shape[2])


def _grad_slabs(name, grad):
    return _full_to_cols(grad) if name in COL_SHARDED else grad.reshape(N_DEV, -1, grad.shape[1])


def _local_step(x, target, w, late_blocks):
    s = x.shape[0]
    w = dict(w)
    g1, g2, g3, g4 = w["norm_mix_pre"], w["norm_mix_post"], w["norm_ffn_pre"], w["norm_ffn_post"]

    w_in = w["w_in"]

    def proj_fn(xt, g1_, w_in_):
        h = _rms(xt, g1_).astype(BF16)
        proj = jnp.dot(h, w_in_, preferred_element_type=F32)
        return (h, *[proj[:, IN_SPLITS[n]:IN_SPLITS[n + 1]] for n in range(6)]), ()

    h1, conv_in, q, k, v, g_conv, g_att = _rowwise(
        "norm_proj", proj_fn, [x], [g1, w_in],
        [_sds((s, D_MODEL), BF16), _sds((s, 2 * CONV_DIM)), _sds((s, ATT_DIM), BF16), _sds((s, ATT_DIM), BF16),
         _sds((s, ATT_DIM), BF16), _sds((s, D_MODEL)), _sds((s, D_MODEL))], tm=512)

    u3, u1 = _conv_fwd(conv_in, w["conv_dw_w"], w["conv_dw_b"], w["conv_ln_g"], w["conv_ln_b"])
    att, ltot, gathered = _attn_fwd(q, k, v, _gather_exchange(late_blocks))
    for nm, g in zip(LATE, gathered):
        w[nm] = _full_weight(nm, g)

    def merge_fn(u3t, at, gc, ga, w_cb, w_ab, b_cb):
        cp = jnp.dot(u3t, w_cb, preferred_element_type=F32)
        ao = jnp.dot(at, w_ab, preferred_element_type=F32)
        return (_merge(cp, ao, gc, ga, b_cb), cp, ao), ()

    merged, conv_pre, att_out = _rowwise(
        "branch_merge", merge_fn, [u3, att, g_conv, g_att], [w["w_conv_branch"], w["w_att_branch"], w["b_conv_branch"]],
        [_sds((s, D_MODEL), BF16)] * 3, tm=512)

    def mid_fn(mt, xt, w_out, g2_, g3_):
        mix_ = jnp.dot(mt, w_out, preferred_element_type=F32)
        x2_ = xt + _rms(mix_, g2_)
        return (mix_, x2_, _rms(x2_, g3_)), ()

    mix, x2, h2 = _rowwise("mix_mid_norm", mid_fn, [merged, x], [w["w_out"], g2, g3],
                           [_sds((s, D_MODEL)), _sds((s, D_MODEL)), _sds((s, D_MODEL), BF16)], tm=512)

    def ffn_up_fn(ht, w_up):
        gu_ = jnp.dot(ht, w_up, preferred_element_type=F32)
        return (gu_, _swiglu(gu_[:, :D_FF], gu_[:, D_FF:])), ()

    gu, act = _rowwise("ffn_up", ffn_up_fn, [h2], [w["w_ffn_up"]],
                       [_sds((s, 2 * D_FF), BF16), _sds((s, D_FF), BF16)], tm=512)

    def final_fn(at, x2t, tgt, w_down, g4_):
        ff = jnp.dot(at, w_down, preferred_element_type=F32)
        n4, vjp = jax.vjp(_rms, ff, g4_)
        err = x2t + n4 - tgt
        dy = err * (1.0 / D_MODEL)
        dff, dg4 = vjp(dy)
        return (dy, dff), (jnp.sum(err * err, axis=0, keepdims=True), dg4)

    dy, dff, loss_cols, d_g4 = _rowwise("ffn_down_loss", final_fn, [act, x2, target], [w["w_ffn_down"], g4],
                                        [_sds((s, D_MODEL)), _sds((s, D_MODEL), BF16)],
                                        [_sds((1, D_MODEL)), _sds((1, D_MODEL))], tm=512)
    loss = 0.5 * jnp.sum(loss_cols) / D_MODEL

    d_w_down = _matmul(act, dff, ta=True, name="d_w_down", out_dtype=BF16)

    def act_bwd_fn(dfft, gut, w_down):
        d_act = lax.dot_general(dfft, w_down, NT, preferred_element_type=F32)
        gu_ = gut.astype(F32)
        _, vjp = jax.vjp(_swiglu, gu_[:, :D_FF], gu_[:, D_FF:])
        return (jnp.concatenate(vjp(d_act), axis=1),), ()

    (dgu,) = _rowwise("ffn_act_bwd", act_bwd_fn, [dff, gu], [w["w_ffn_down"]], [_sds((s, 2 * D_FF), BF16)])
    d_w_up = _matmul(h2, dgu, ta=True, name="d_w_up", out_dtype=BF16)
    ffn_exchange, ffn_token = _split_start(
        "scatter_ffn_start", [_grad_slabs("w_ffn_up", d_w_up), _grad_slabs("w_ffn_down", d_w_down)], scatter=True)

    def mid_bwd_fn(dgut, xt, mt, dyt, w_up, g2_, g3_):
        dh = lax.dot_general(dgut, w_up, NT, preferred_element_type=F32)
        n2, vjp2 = jax.vjp(_rms, mt, g2_)
        x2_ = xt + n2
        _, vjp3 = jax.vjp(_rms, x2_, g3_)
        dx2_, dg3 = vjp3(dh)
        dx2_ = dx2_ + dyt
        dmix_, dg2 = vjp2(dx2_)
        return (dx2_, dmix_), (dg2, dg3)

    dx2, dmix, d_g2, d_g3 = _rowwise("ffn_up_mid_bwd", mid_bwd_fn, [dgu, x, mix, dy], [w["w_ffn_up"], g2, g3],
                                     [_sds((s, D_MODEL)), _sds((s, D_MODEL), BF16)],
                                     [_sds((1, D_MODEL)), _sds((1, D_MODEL))], tm=512, deps=[ffn_token])
    d_w_out = _matmul(merged, dmix, ta=True, name="d_w_out", out_dtype=BF16)

    def merge_bwd_fn(dmt, cp, ao, gc, ga, w_out, w_cb, w_ab, b_cb):
        dm = lax.dot_general(dmt, w_out, NT, preferred_element_type=F32)
        _, vjp = jax.vjp(_merge, cp.astype(F32), ao.astype(F32), gc, ga, b_cb)
        dcp, dao, dgc, dga, dbias = vjp(dm)
        dcp, dao = dcp.astype(BF16), dao.astype(BF16)
        du3_ = lax.dot_general(dcp, w_cb, NT, preferred_element_type=F32)
        datt_ = lax.dot_general(dao, w_ab, NT, preferred_element_type=F32)
        return (dcp, dao, dgc, dga, du3_, datt_), (dbias,)

    d_conv_out, d_att_out, d_g_conv, d_g_att, du3, d_att, d_b_cb = _rowwise(
        "merge_bwd", merge_bwd_fn, [dmix, conv_pre, att_out, g_conv, g_att],
        [w["w_out"], w["w_conv_branch"], w["w_att_branch"], w["b_conv_branch"]],
        [_sds((s, D_MODEL), BF16)] * 4 + [_sds((s, CONV_DIM)), _sds((s, ATT_DIM), BF16)], [_sds((1, D_MODEL))], tm=512)

    d_w_cb = _matmul(u3, d_conv_out, ta=True, name="d_w_conv_branch", out_dtype=BF16)
    d_w_ab = _matmul(att, d_att_out, ta=True, name="d_w_att_branch", out_dtype=BF16)

    mix_exchange, mix_token = _split_start(
        "scatter_mix_start", [_grad_slabs("w_conv_branch", d_w_cb), _grad_slabs("w_att_branch", d_w_ab),
                              _grad_slabs("w_out", d_w_out)], scatter=True)
    dq, dk, dv = _attn_bwd(q, k, v, d_att, ltot, deps=[mix_token])

    def ln_bwd_fn(u1t, du3t, g_, b_):
        _, vjp = jax.vjp(_ln_silu, u1t, g_, b_)
        du1_, dg_, db_ = vjp(du3t)
        return (du1_,), (dg_, db_)

    du1, d_ln_g, d_ln_b = _rowwise("conv_ln_bwd", ln_bwd_fn, [u1, du3], [w["conv_ln_g"], w["conv_ln_b"]],
                                   [_sds((s, CONV_DIM))], [_sds((1, CONV_DIM)), _sds((1, CONV_DIM))])
    d_conv_in, d_dw_w, d_dw_b = _conv_bwd(conv_in, du1, w["conv_dw_w"])

    d_proj = jnp.concatenate([d_conv_in, dq.astype(BF16), dk.astype(BF16), dv.astype(BF16), d_g_conv, d_g_att],
                             axis=1)
    received = (_split_wait("scatter_mix_wait", mix_exchange, [d_proj], scatter=True)
                + _split_wait("scatter_ffn_wait", ffn_exchange, [d_proj], scatter=True))
    d_w_in = _matmul(h1, d_proj, ta=True, name="d_w_in", out_dtype=BF16, deps=received)
    w_in_exchange, w_in_token = _split_start("scatter_w_in_start", [_grad_slabs("w_in", d_w_in)], scatter=True)

    def pre_bwd_fn(dpt, xt, dx2t, w_in_, g_):
        dh = lax.dot_general(dpt, w_in_, NT, preferred_element_type=F32)
        _, vjp = jax.vjp(_rms, xt, g_)
        dx_, dg_ = vjp(dh)
        return (dx_ + dx2t,), (dg_,)

    grad_x, d_g1 = _rowwise("proj_norm_bwd", pre_bwd_fn, [d_proj, x, dx2], [w_in, g1], [_sds((s, D_MODEL))],
                            [_sds((1, D_MODEL))], tm=512, deps=[w_in_token])

    grads = {
        "norm_mix_pre": d_g1, "conv_dw_w": d_dw_w, "conv_dw_b": d_dw_b,
        "conv_ln_g": d_ln_g, "conv_ln_b": d_ln_b, "b_conv_branch": d_b_cb,
        "norm_mix_post": d_g2, "norm_ffn_pre": d_g3, "norm_ffn_post": d_g4,
    }
    return loss, grad_x, received, w_in_exchange, grads


def _place():
    x, y, c = lax.axis_index("x"), lax.axis_index("y"), lax.axis_index("c")
    return x, y, c


def _slot(px, py, pc):
    return 4 * px + 2 * py + pc


def _exchange_scratch(n):
    return [pltpu.SemaphoreType.DMA((7 * n,)), pltpu.SemaphoreType.DMA((7 * n,)), pltpu.SemaphoreType.DMA((n,))]


def _gather_exchange(arrs):
    n = len(arrs)

    def phases(ins, outs, send_sems, recv_sems, local_sems):
        x, y, c = _place()
        me, sibling = (x, y, c), (x, y, 1 - c)
        chips = [(1 - x, y), (x, 1 - y), (1 - x, 1 - y)]

        def copy(a, kk, block, to, src=None):
            dst = outs[a].at[_slot(*block)]
            return pltpu.make_async_remote_copy(
                src_ref=dst if src is None else src, dst_ref=dst,
                send_sem=send_sems.at[a * 7 + kk], recv_sem=recv_sems.at[a * 7 + kk],
                device_id=to, device_id_type=MESH)

        mine = [pltpu.make_async_copy(ins[a], outs[a].at[_slot(*me)], local_sems.at[a]) for a in range(n)]
        first = []
        for a in range(n):
            first.append(copy(a, 0, me, sibling, src=ins[a]))
            first += [copy(a, 1 + j, me, (*chip, c), src=ins[a]) for j, chip in enumerate(chips)]
        passed = [copy(a, 4 + j, (*chip, c), sibling) for j, chip in enumerate(chips) for a in range(n)]

        def send():
            for cp in mine + first:
                cp.start()

        def pass_on():
            for j, chip in enumerate(chips):
                for a in range(n):
                    copy(a, 1 + j, (*chip, c), me).wait_recv()
                    passed[j * n + a].start()

        def finish():
            for a in range(n):
                copy(a, 0, sibling, me).wait_recv()
                for j, chip in enumerate(chips):
                    copy(a, 4 + j, (*chip, 1 - c), me).wait_recv()
            for cp in first + passed:
                cp.wait_send()
            for cp in mine:
                cp.wait()

        return [send, pass_on, finish]

    return list(arrs), [_sds((N_DEV,) + a.shape, a.dtype) for a in arrs], _exchange_scratch(n), phases


HBM = pl.BlockSpec(memory_space=pltpu.HBM)
SEM = pl.BlockSpec(memory_space=pltpu.SEMAPHORE)
EFFECT = pltpu.SideEffectType.DATAFLOW_SIDE_EFFECTING


def _split_copies(scatter, ins, lands, send_sems, recv_sems):
    x, y, c = _place()
    mine = _slot(x, y, c)
    flips = [(fx, fy, fc) for fx in (0, 1) for fy in (0, 1) for fc in (0, 1)][1:]
    peers = [((1 - x) if fx else x, (1 - y) if fy else y, (1 - c) if fc else c) for fx, fy, fc in flips]
    pairs = []
    for a in range(len(ins)):
        for kk, peer in enumerate(peers):
            def copy(dst_slot, a=a, kk=kk, peer=peer):
                return pltpu.make_async_remote_copy(
                    src_ref=ins[a].at[_slot(*peer)] if scatter else ins[a], dst_ref=lands[a].at[dst_slot],
                    send_sem=send_sems.at[a * 7 + kk], recv_sem=recv_sems.at[a * 7 + kk],
                    device_id=peer, device_id_type=MESH)
            pairs.append((copy(mine), copy(_slot(*peer))))
    return pairs


def _split_start(name, arrs, scatter):
    n = len(arrs)

    def body(*refs):
        for sent, _ in _split_copies(scatter, refs[:n], refs[n:2 * n], refs[2 * n], refs[2 * n + 1]):
            sent.start()
        token = refs[-1]
        token[...] = jnp.zeros_like(token)

    sems = pltpu.SemaphoreType.DMA((7 * n,))
    land_shapes = [a.shape if scatter else (N_DEV,) + a.shape for a in arrs]
    res = pl.pallas_call(
        body, name=name,
        out_shape=(sems, sems, *[pltpu.HBM(a.shape, a.dtype) for a in arrs],
                   *[pltpu.HBM(shp, a.dtype) for shp, a in zip(land_shapes, arrs)], _sds((SUBLANES, LANES))),
        in_specs=[HBM] * (2 * n),
        out_specs=(SEM, SEM, *[HBM] * (2 * n), pl.BlockSpec(memory_space=pltpu.VMEM)),
        input_output_aliases={i: 2 + i for i in range(2 * n)},
        compiler_params=pltpu.CompilerParams(has_side_effects=EFFECT),
    )(*[pltpu.with_memory_space_constraint(a, pltpu.HBM) for a in arrs],
      *[pltpu.with_memory_space_constraint(lax.empty(shp, a.dtype), pltpu.HBM) for shp, a in zip(land_shapes, arrs)])
    return res[:-1], res[-1]


def _split_wait(name, state, after, scatter):
    send_sems, recv_sems, thru = state[0], state[1], state[2:]
    n = len(thru) // 2

    def body(*refs):
        for sent, landed in _split_copies(scatter, refs[:n], refs[n:2 * n], refs[2 * n], refs[2 * n + 1]):
            sent.wait_send()
            landed.wait_recv()

    res = pl.pallas_call(
        body, name=name,
        out_shape=[pltpu.HBM(t.shape, t.dtype) for t in thru],
        in_specs=[HBM] * (2 * n) + [SEM, SEM] + [ANY] * len(after),
        out_specs=[HBM] * (2 * n),
        input_output_aliases={i: i for i in range(2 * n)},
        compiler_params=pltpu.CompilerParams(has_side_effects=EFFECT),
    )(*thru, send_sems, recv_sems, *after)
    me = _slot(*_place())
    return [lax.dynamic_update_slice_in_dim(land, lax.dynamic_slice_in_dim(sent, me, 1, 0) if scatter else sent[None], me, 0)
            for sent, land in zip(res[:n], res[n:])]


def _exchange_call(name, exchange):
    arrs, out_shape, scratch, phases = exchange
    n = len(arrs)

    def body(*refs):
        for step in phases(refs[:n], refs[n:2 * n], *refs[2 * n:]):
            step()

    return pl.pallas_call(body, name=name, in_specs=[ANY] * n, out_specs=[ANY] * n,
                          out_shape=out_shape, scratch_shapes=scratch)(*arrs)


def _carry_exchange(exchange, refs, n_in, n_out, first, middle, last):
    arrs, _, _, phases = exchange
    n = len(arrs)
    ins = refs[n_in:n_in + n]
    outs = refs[n_in + n + n_out:n_in + 2 * n + n_out]
    steps = phases(ins, outs, *refs[n_in + 2 * n + n_out:])
    pl.when(first)(steps[0])
    if len(steps) == 3:
        pl.when(middle)(steps[1])
    return lambda: pl.when(last)(steps[-1])


def _adamw_math(w, g, m, v):
    m2 = ADAM_B1 * m + (1.0 - ADAM_B1) * g
    v2 = ADAM_B2 * v + (1.0 - ADAM_B2) * jnp.square(g)
    m_hat = m2 / (1.0 - ADAM_B1 ** ADAM_STEP)
    v_hat = v2 / (1.0 - ADAM_B2 ** ADAM_STEP)
    delta = -ADAM_LR * (m_hat / (jnp.sqrt(v_hat) + ADAM_EPS) + ADAM_WD * w)
    return delta, m2, v2


def _sum_adamw(name, parts, w, m, v, tr=256, deps=()):
    p, r, c = parts.shape
    tr = _pick(r, tr, 16)

    def body(p_ref, w_ref, m_ref, v_ref, *rest):
        g_ref, d_ref, m2_ref, v2_ref = rest[len(deps):]
        g = p_ref[0].astype(F32)
        for d in range(1, p):
            g = g + p_ref[d].astype(F32)
        delta, m2, v2 = _adamw_math(w_ref[...], g, m_ref[...], v_ref[...])
        g_ref[...] = g
        d_ref[...] = delta
        m2_ref[...] = m2
        v2_ref[...] = v2

    tile = pl.BlockSpec((tr, c), lambda i: (i, 0))
    return pl.pallas_call(
        body, name=name, grid=(r // tr,),
        in_specs=[pl.BlockSpec((p, tr, c), lambda i: (0, i, 0)), tile, tile, tile] + [ANY] * len(deps),
        out_specs=[tile] * 4, out_shape=[_sds((r, c))] * 4,
        compiler_params=_params(("parallel",)),
    )(parts, w, m, v, *deps)


def _sum_parts(name, parts):
    p, r, c = parts.shape

    def body(p_ref, o_ref):
        g = p_ref[0]
        for d in range(1, p):
            g = g + p_ref[d]
        o_ref[...] = g

    return pl.pallas_call(
        body, name=name, out_shape=_sds((r, c)),
        in_specs=[pl.BlockSpec(memory_space=pltpu.VMEM)], out_specs=pl.BlockSpec(memory_space=pltpu.VMEM),
    )(parts)


WEIGHTS = ["norm_mix_pre", "w_in", "conv_dw_w", "conv_dw_b", "conv_ln_g", "conv_ln_b", "w_conv_branch",
           "b_conv_branch", "w_att_branch", "w_out", "norm_mix_post", "norm_ffn_pre", "w_ffn_up", "w_ffn_down",
           "norm_ffn_post"]
COL_SHARDED = ["w_in", "w_conv_branch", "w_att_branch", "w_ffn_up"]
ROW_SHARDED = ["w_out", "w_ffn_down"]
VECTORS = ["norm_mix_pre", "conv_dw_b", "conv_ln_g", "conv_ln_b", "b_conv_branch", "norm_mix_post",
           "norm_ffn_pre", "norm_ffn_post"]


def _cols_to_full(g):
    return g.transpose(1, 0, 2).reshape(g.shape[1], N_DEV * g.shape[2])


def _full_to_cols(f):
    return f.reshape(f.shape[0], N_DEV, f.shape[1] // N_DEV).transpose(1, 0, 2)


def _pack_vectors(vecs):
    rows = [jnp.pad(vecs[nm].reshape(-1), (0, D_MODEL - vecs[nm].size)) for nm in VECTORS]
    return jnp.stack(rows)


def _unpack_vectors(packed, sizes):
    return {nm: packed[n, :sizes[nm]] for n, nm in enumerate(VECTORS)}


def kernel(x, norm_mix_pre, w_in, conv_dw_w, conv_dw_b, conv_ln_g, conv_ln_b, w_conv_branch, b_conv_branch, w_att_branch, w_out, norm_mix_post, norm_ffn_pre, w_ffn_up, w_ffn_down, norm_ffn_post, loss_target, m_norm_mix_pre, m_w_in, m_conv_dw_w, m_conv_dw_b, m_conv_ln_g, m_conv_ln_b, m_w_conv_branch, m_b_conv_branch, m_w_att_branch, m_w_out, m_norm_mix_post, m_norm_ffn_pre, m_w_ffn_up, m_w_ffn_down, m_norm_ffn_post, v_norm_mix_pre, v_w_in, v_conv_dw_w, v_conv_dw_b, v_conv_ln_g, v_conv_ln_b, v_w_conv_branch, v_b_conv_branch, v_w_att_branch, v_w_out, v_norm_mix_post, v_norm_ffn_pre, v_w_ffn_up, v_w_ffn_down, v_norm_ffn_post):
    ws = dict(zip(WEIGHTS, [norm_mix_pre, w_in, conv_dw_w, conv_dw_b, conv_ln_g, conv_ln_b, w_conv_branch,
                            b_conv_branch, w_att_branch, w_out, norm_mix_post, norm_ffn_pre, w_ffn_up, w_ffn_down,
                            norm_ffn_post]))
    ms = dict(zip(WEIGHTS, [m_norm_mix_pre, m_w_in, m_conv_dw_w, m_conv_dw_b, m_conv_ln_g, m_conv_ln_b,
                            m_w_conv_branch, m_b_conv_branch, m_w_att_branch, m_w_out, m_norm_mix_post,
                            m_norm_ffn_pre, m_w_ffn_up, m_w_ffn_down, m_norm_ffn_post]))
    vs = dict(zip(WEIGHTS, [v_norm_mix_pre, v_w_in, v_conv_dw_w, v_conv_dw_b, v_conv_ln_g, v_conv_ln_b,
                            v_w_conv_branch, v_b_conv_branch, v_w_att_branch, v_w_out, v_norm_mix_post,
                            v_norm_ffn_pre, v_w_ffn_up, v_w_ffn_down, v_norm_ffn_post]))

    dw_block = jnp.pad(conv_dw_w, ((0, 1), (0, 0)))
    g_in, g_dw = _exchange_call("gather_first", _gather_exchange([w_in.astype(BF16), dw_block]))
    full = {"w_in": _full_weight("w_in", g_in), "conv_dw_w": _cols_to_full(g_dw)}
    for nm in VECTORS:
        full[nm] = ws[nm].reshape(1, -1)

    loss_local, grad_x, received, w_in_exchange, grads = _local_step(
        x[0], loss_target[0], full, [ws[nm].astype(BF16) for nm in LATE])
    loss = lax.psum(loss_local, ("x", "y", "c"))

    small_exchange, small_token = _split_start(
        "gather_small_start", [_pack_vectors(grads), grads["conv_dw_w"]], scatter=False)
    out_g, out_d, out_m, out_v = {}, {}, {}, {}
    for nm, parts in zip(LATE, received):
        out_g[nm], out_d[nm], out_m[nm], out_v[nm] = _sum_adamw("adamw_" + nm, parts, ws[nm], ms[nm], vs[nm],
                                                                deps=[small_token])
    done = [grad_x] + [out_v[nm] for nm in LATE]
    (recv_in,) = _split_wait("scatter_w_in_wait", w_in_exchange, done, scatter=True)
    small = _split_wait("gather_small_wait", small_exchange, done, scatter=False)
    nm = "w_in"
    out_g[nm], out_d[nm], out_m[nm], out_v[nm] = _sum_adamw("adamw_" + nm, recv_in, ws[nm], ms[nm], vs[nm])
    sizes = {nm: ws[nm].size for nm in VECTORS}
    vec = _sum_adamw("adamw_vectors", small[0], _pack_vectors(ws), _pack_vectors(ms), _pack_vectors(vs))
    for res, dst in zip(vec, (out_g, out_d, out_m, out_v)):
        dst.update(_unpack_vectors(res, sizes))
    dw_full = _sum_parts("sum_dw_grads", small[1])
    me = _slot(*_place())
    dw_mine = lax.dynamic_slice(dw_full, (0, me * (CONV_DIM // N_DEV)), (CONV_WIDTH, CONV_DIM // N_DEV))
    nm = "conv_dw_w"
    out_g[nm], out_d[nm], out_m[nm], out_v[nm] = _sum_adamw("adamw_dw", dw_mine[None], ws[nm], ms[nm], vs[nm])

    outs = [loss, grad_x[None]]
    for group in (out_g, out_d, out_m, out_v):
        outs += [group[nm] for nm in WEIGHTS]
    return tuple(outs)
```

```python
import math

import jax
import jax.numpy as jnp
from jax import lax
from jax.experimental import pallas as pl
from jax.experimental.pallas import tpu as pltpu

F32 = jnp.float32
BF16 = jnp.bfloat16

N_DEV = 8
D_MODEL = 1024
CONV_DIM = 512
CONV_WIDTH = 31
N_HEADS = 8
HEAD_DIM = 64
ATT_DIM = N_HEADS * HEAD_DIM
D_FF = 2816
EPS = 1e-6
IN_SPLITS = (0, 1024, 1536, 2048, 2560, 3584, 4608)

ADAM_LR = 0.001
ADAM_B1 = 0.9
ADAM_B2 = 0.999
ADAM_EPS = 1e-08
ADAM_WD = 0.01
ADAM_STEP = 10

LANES = 128
SUBLANES = 8
HALO = 32
ATT_TILE = 256
DEAD_SUM = -120.0
VMEM_LIMIT = 56 * 1024 * 1024
MESH = pl.DeviceIdType.MESH
ANY = pl.BlockSpec(memory_space=pl.ANY)


def _pick(dim, target, align=LANES):
    t = min(dim, target)
    t -= t % align
    while t >= align:
        if dim % t == 0:
            return t
        t -= align
    return dim


def _params(semantics):
    return pltpu.CompilerParams(dimension_semantics=semantics, vmem_limit_bytes=VMEM_LIMIT)


def _matmul(a, b, *, name, ta=False, tb=False, out_dtype=F32):
    m, k = (a.shape[1], a.shape[0]) if ta else a.shape
    n, k2 = b.shape if tb else (b.shape[1], b.shape[0])
    assert k == k2, (a.shape, b.shape, ta, tb)
    tm, tn, tk = _pick(m, 1408 if ta else 512), _pick(n, 1536), _pick(k, 1536)
    nk = k // tk
    dims = (((0 if ta else 1,), (1 if tb else 0,)), ((), ()))

    def body(a_ref, b_ref, o_ref, *acc):
        part = lax.dot_general(a_ref[...], b_ref[...], dims, preferred_element_type=F32)
        if nk == 1:
            o_ref[...] = part.astype(o_ref.dtype)
            return
        acc_ref, = acc
        kk = pl.program_id(2)

        @pl.when(kk == 0)
        def _():
            acc_ref[...] = part

        @pl.when((kk > 0) & (kk < nk - 1))
        def _():
            acc_ref[...] += part

        @pl.when(kk == nk - 1)
        def _():
            o_ref[...] = (acc_ref[...] + part).astype(o_ref.dtype)

    a_spec = pl.BlockSpec((tk, tm), lambda j, i, kk: (kk, i)) if ta else pl.BlockSpec((tm, tk), lambda j, i, kk: (i, kk))
    b_spec = (pl.BlockSpec((tn, tk), lambda j, i, kk: (j, kk)) if tb
              else pl.BlockSpec((tk, tn), lambda j, i, kk: (kk, j)))
    return pl.pallas_call(
        body, name=name, grid=(n // tn, m // tm, nk),
        in_specs=[a_spec, b_spec],
        out_specs=pl.BlockSpec((tm, tn), lambda j, i, kk: (i, j)),
        out_shape=jax.ShapeDtypeStruct((m, n), out_dtype),
        scratch_shapes=[pltpu.VMEM((tm, tn), F32)] if nk > 1 else [],
        compiler_params=_params(("parallel", "parallel", "arbitrary")),
    )(a, b)


NO_EXCHANGE = ([], [], [], None)


def _sweep_marks(nt):
    i = pl.program_id(0)
    return i == 0, i == (3 * nt) // 4, i == nt - 1


def _rowwise(name, fn, rows, bcasts, row_outs, red_outs=(), tm=256, exchange=NO_EXCHANGE):
    s = rows[0].shape[0]
    tm = _pick(s, tm, 16)
    nt = s // tm
    resident = pl.Buffered(1)
    nr, nb, no, nd = len(rows), len(bcasts), len(row_outs), len(red_outs)
    x_arrs, x_shape, x_scratch, _ = exchange
    nx = len(x_arrs)
    first_out = nr + nb + nx

    def body(*refs):
        finish_exchange = _carry_exchange(exchange, refs, nr + nb, no + nd, *_sweep_marks(nt))
        ins = [r[...] for r in refs[:nr + nb]]
        outs, reds = fn(*ins)
        for ref, val in zip(refs[first_out:first_out + no], outs):
            ref[...] = val.astype(ref.dtype)
        i = pl.program_id(0)
        for ref, val in zip(refs[first_out + no:first_out + no + nd], reds):
            @pl.when(i == 0)
            def _():
                ref[...] = val

            @pl.when(i > 0)
            def _():
                ref[...] += val
        finish_exchange()

    in_specs = [pl.BlockSpec((tm, r.shape[1]), lambda i: (i, 0)) for r in rows]
    in_specs += [pl.BlockSpec(b.shape, lambda i: (0, 0), pipeline_mode=resident) for b in bcasts]
    out_specs = [pl.BlockSpec((tm, o.shape[1]), lambda i: (i, 0)) for o in row_outs]
    out_specs += [pl.BlockSpec(d.shape, lambda i: (0, 0)) for d in red_outs]
    return pl.pallas_call(
        body, name=name, grid=(nt,), in_specs=in_specs + [ANY] * nx, out_specs=out_specs + [ANY] * nx,
        out_shape=list(row_outs) + list(red_outs) + x_shape, scratch_shapes=x_scratch,
        compiler_params=_params(("arbitrary",)),
    )(*rows, *bcasts, *x_arrs)


def _sds(shape, dtype=F32):
    return jax.ShapeDtypeStruct(shape, dtype)


def _rms(x, g):
    y = x * lax.rsqrt(jnp.mean(x * x, axis=-1, keepdims=True) + EPS)
    return y * g


def _silu(x):
    return x * jax.nn.sigmoid(x)


def _swiglu(g, u):
    return _silu(g) * u


def _ln_silu(u, g, b):
    mu = jnp.mean(u, axis=-1, keepdims=True)
    var = jnp.mean(jnp.square(u - mu), axis=-1, keepdims=True)
    return _silu((u - mu) * lax.rsqrt(var + EPS) * g + b)


def _merge(conv_pre, att_out, g_conv, g_att, b_cb):
    return jax.nn.sigmoid(g_conv) * (conv_pre + b_cb) + jax.nn.sigmoid(g_att) * att_out


def _glu(t):
    return t[:, :CONV_DIM] * jax.nn.sigmoid(t[:, CONV_DIM:])


def _conv_fwd(conv_in, w_pad, b, ln_g, ln_b, exchange, tm=256):
    s = conv_in.shape[0]
    tm = _pick(s, tm, HALO)
    ratio = tm // HALO
    x_arrs, x_shape, x_scratch, _ = exchange
    nx = len(x_arrs)

    def body(*refs):
        main_ref, halo_ref, w_ref, b_ref, g_ref, be_ref = refs[:6]
        u3_ref, u1_ref = refs[6 + nx:8 + nx]
        buf = refs[-1]
        finish_exchange = _carry_exchange(exchange, refs, 6, 2, *_sweep_marks(s // tm))
        i = pl.program_id(0)
        buf[0:HALO, :] = _glu(halo_ref[...]) * (i > 0).astype(F32)
        buf[HALO:HALO + tm, :] = _glu(main_ref[...])
        acc = jnp.zeros((tm, CONV_DIM), F32) + b_ref[...]
        for j in range(CONV_WIDTH):
            acc = acc + w_ref[j:j + 1, :] * buf[pl.ds(HALO - (CONV_WIDTH - 1) + j, tm), :]
        u1_ref[...] = acc
        u3_ref[...] = _ln_silu(acc, g_ref[...], be_ref[...]).astype(u3_ref.dtype)
        finish_exchange()

    res = pl.pallas_call(
        body, name="conv_fwd", grid=(s // tm,),
        in_specs=[pl.BlockSpec((tm, 2 * CONV_DIM), lambda i: (i, 0)),
                  pl.BlockSpec((HALO, 2 * CONV_DIM), lambda i: (jnp.maximum(i * ratio - 1, 0), 0)),
                  pl.BlockSpec(w_pad.shape, lambda i: (0, 0)),
                  pl.BlockSpec(b.shape, lambda i: (0, 0)),
                  pl.BlockSpec(ln_g.shape, lambda i: (0, 0)),
                  pl.BlockSpec(ln_b.shape, lambda i: (0, 0))] + [ANY] * nx,
        out_specs=[pl.BlockSpec((tm, CONV_DIM), lambda i: (i, 0)),
                   pl.BlockSpec((tm, CONV_DIM), lambda i: (i, 0))] + [ANY] * nx,
        out_shape=[_sds((s, CONV_DIM), BF16), _sds((s, CONV_DIM), F32)] + x_shape,
        scratch_shapes=x_scratch + [pltpu.VMEM((tm + HALO, CONV_DIM), F32)],
        compiler_params=_params(("arbitrary",)),
    )(conv_in, conv_in, w_pad, b, ln_g, ln_b, *x_arrs)
    return res[0], res[1], res[2:]


def _conv_bwd(conv_in, du1, w_pad, exchange, tm=256):
    s = conv_in.shape[0]
    tm = _pick(s, tm, HALO)
    ratio = tm // HALO
    nt = s // tm
    last_halo = s // HALO - 1
    x_arrs, x_shape, x_scratch, _ = exchange
    nx = len(x_arrs)

    def body(*refs):
        main_ref, halo_ref, du_ref, dun_ref, w_ref = refs[:5]
        dci_ref, dw_ref, db_ref = refs[5 + nx:8 + nx]
        ubuf, dbuf = refs[-2:]
        finish_exchange = _carry_exchange(exchange, refs, 5, 3, *_sweep_marks(nt))
        i = pl.program_id(0)
        main = main_ref[...]
        a = main[:, :CONV_DIM]
        sb = jax.nn.sigmoid(main[:, CONV_DIM:])
        ubuf[0:HALO, :] = _glu(halo_ref[...]) * (i > 0).astype(F32)
        ubuf[HALO:HALO + tm, :] = a * sb
        du = du_ref[...]
        dbuf[0:tm, :] = du
        dbuf[tm:tm + HALO, :] = dun_ref[...] * (i < nt - 1).astype(F32)

        @pl.when(i == 0)
        def _():
            dw_ref[...] = jnp.zeros_like(dw_ref)
            db_ref[...] = jnp.zeros_like(db_ref)

        du0 = jnp.zeros((tm, CONV_DIM), F32)
        for j in range(CONV_WIDTH):
            du0 = du0 + w_ref[j:j + 1, :] * dbuf[pl.ds(CONV_WIDTH - 1 - j, tm), :]
            dw_ref[j:j + 1, :] += jnp.sum(du * ubuf[pl.ds(HALO - (CONV_WIDTH - 1) + j, tm), :], axis=0, keepdims=True)
        db_ref[...] += jnp.sum(du, axis=0, keepdims=True)
        dci_ref[:, :CONV_DIM] = (du0 * sb).astype(dci_ref.dtype)
        dci_ref[:, CONV_DIM:] = (du0 * a * sb * (1.0 - sb)).astype(dci_ref.dtype)
        finish_exchange()

    res = pl.pallas_call(
        body, name="conv_bwd", grid=(nt,),
        in_specs=[pl.BlockSpec((tm, 2 * CONV_DIM), lambda i: (i, 0)),
                  pl.BlockSpec((HALO, 2 * CONV_DIM), lambda i: (jnp.maximum(i * ratio - 1, 0), 0)),
                  pl.BlockSpec((tm, CONV_DIM), lambda i: (i, 0)),
                  pl.BlockSpec((HALO, CONV_DIM), lambda i: (jnp.minimum((i + 1) * ratio, last_halo), 0)),
                  pl.BlockSpec(w_pad.shape, lambda i: (0, 0))] + [ANY] * nx,
        out_specs=[pl.BlockSpec((tm, 2 * CONV_DIM), lambda i: (i, 0)),
                   pl.BlockSpec(w_pad.shape, lambda i: (0, 0)),
                   pl.BlockSpec((1, CONV_DIM), lambda i: (0, 0))] + [ANY] * nx,
        out_shape=[_sds((s, 2 * CONV_DIM), BF16), _sds(w_pad.shape), _sds((1, CONV_DIM))] + x_shape,
        scratch_shapes=x_scratch + [pltpu.VMEM((tm + HALO, CONV_DIM), F32), pltpu.VMEM((tm + HALO, CONV_DIM), F32)],
        compiler_params=_params(("arbitrary",)),
    )(conv_in, conv_in, du1, du1, w_pad, *x_arrs)
    return res[0], res[1], res[2], res[3:]


def _logsig_neg(z):
    return jnp.minimum(-z, 0.0) - jnp.log(1.0 + jnp.exp(-jnp.abs(z)))


def _split_dot(val, tri):
    hi = val.astype(BF16)
    lo = (val - hi.astype(F32)).astype(BF16)
    return jnp.dot(hi, tri, preferred_element_type=F32) + jnp.dot(lo, tri, preferred_element_type=F32)


def _attn_masks(t, later):
    row = lax.broadcasted_iota(jnp.int32, (t, t), 0)
    col = lax.broadcasted_iota(jnp.int32, (t, t), 1)
    tri = jnp.where(row > col if later else row <= col, 1.0, 0.0).astype(BF16)
    return col < row, tri


def _grid_marks(h, nq):
    hh, i = pl.program_id(0), pl.program_id(1)
    return (hh == 0) & (i == 0), (hh == (3 * h) // 4) & (i == 0), (hh == h - 1) & (i == nq - 1)


def _head_masks(shape):
    lane = lax.broadcasted_iota(jnp.int32, shape, len(shape) - 1)
    return lane < HEAD_DIM, lane >= HEAD_DIM


def _per_head(blk):
    m0, m1 = _head_masks(blk.shape)
    zero = jnp.zeros_like(blk)
    return jnp.where(m0, blk, zero), jnp.where(m1, blk, zero)


NT = (((1,), (1,)), ((), ()))
TN = (((0,), (0,)), ((), ()))


def _attn_fwd(q, k, v, exchange):
    s = q.shape[0]
    hp = q.shape[1] // LANES
    t = ATT_TILE
    scale = 1.0 / math.sqrt(HEAD_DIM)
    x_arrs, x_shape, x_scratch, _ = exchange
    nx = len(x_arrs)

    def body(*refs):
        q_ref, k_ref, v_ref = refs[:3]
        o_ref, lt_ref, nb_ref = refs[3 + nx:6 + nx]
        finish_exchange = _carry_exchange(exchange, refs, 3, 3, *_grid_marks(hp, s // t))
        i = pl.program_id(1)
        qs = _per_head((q_ref[...].astype(F32) * scale).astype(BF16))
        causal, tri = _attn_masks(t, later=True)

        def step(kb, carry, masked):
            cs, acc = carry
            off = pl.multiple_of(kb * t, t)
            kblk = k_ref[pl.ds(off, t), :]
            vs = _per_head(v_ref[pl.ds(off, t), :])
            new_cs = []
            for hd in range(2):
                z = lax.dot_general(qs[hd], kblk, NT, preferred_element_type=F32)
                l = _logsig_neg(z)
                if masked:
                    l = jnp.where(causal, l, 0.0)
                e = z + l + _split_dot(l, tri) + cs[hd]
                if masked:
                    e = jnp.where(causal, e, -1e30)
                acc = acc + jnp.dot(jnp.exp(e).astype(BF16), vs[hd], preferred_element_type=F32)
                new_cs.append(cs[hd] + jnp.sum(l, axis=1, keepdims=True))
            return tuple(new_cs), acc

        zero = jnp.zeros((t, 1), F32)
        carry = step(i, ((zero, zero), jnp.zeros((t, LANES), F32)), True)

        def more(state):
            n, (cs, _) = state
            return (n < i) & (jnp.maximum(jnp.max(cs[0]), jnp.max(cs[1])) > DEAD_SUM)

        n_blocks, carry = lax.while_loop(more, lambda st: (st[0] + 1, step(i - 1 - st[0], st[1], False)),
                                         (jnp.int32(0), carry))
        m0, _ = _head_masks((t, LANES))
        lt_ref[...] = jnp.where(m0, carry[0][0], carry[0][1])
        o_ref[...] = carry[1].astype(o_ref.dtype)
        nb_ref[pl.program_id(0), i] = n_blocks.astype(F32)
        finish_exchange()

    res = pl.pallas_call(
        body, name="attn_fwd", grid=(hp, s // t),
        in_specs=[pl.BlockSpec((t, LANES), lambda p, i: (i, p)),
                  pl.BlockSpec((s, LANES), lambda p, i: (0, p)),
                  pl.BlockSpec((s, LANES), lambda p, i: (0, p))] + [ANY] * nx,
        out_specs=[pl.BlockSpec((t, LANES), lambda p, i: (i, p)),
                   pl.BlockSpec((None, t, LANES), lambda p, i: (p, i, 0)),
                   pl.BlockSpec(memory_space=pltpu.SMEM)] + [ANY] * nx,
        out_shape=[_sds(q.shape, BF16), _sds((hp, s, LANES), F32), _sds((hp, s // t), F32)] + x_shape,
        scratch_shapes=x_scratch,
        compiler_params=_params(("arbitrary", "arbitrary")),
    )(q, k, v, *x_arrs)
    return res[0], res[1], res[2], res[3:]


def _attn_bwd(q, k, v, do, ltot, n_blocks, exchange):
    s = q.shape[0]
    hp = q.shape[1] // LANES
    t = ATT_TILE
    scale = 1.0 / math.sqrt(HEAD_DIM)
    x_arrs, x_shape, x_scratch, _ = exchange
    nx = len(x_arrs)

    def body(*refs):
        q_ref, k_ref, v_ref, do_ref, lt_ref, nb_ref = refs[:6]
        dq_ref, dk_ref, dv_ref = refs[6 + nx:9 + nx]
        finish_exchange = _carry_exchange(exchange, refs, 6, 3, *_grid_marks(hp, s // t))
        i = pl.program_id(1)
        first = jnp.clip(i - nb_ref[pl.program_id(0), i].astype(jnp.int32), 0, i)

        @pl.when(i == 0)
        def _():
            dk_ref[...] = jnp.zeros_like(dk_ref)
            dv_ref[...] = jnp.zeros_like(dv_ref)

        qb = q_ref[...]
        qm = _per_head(qb)
        qs = _per_head((qb.astype(F32) * scale).astype(BF16))
        dos = _per_head(do_ref[...])
        lts = (lt_ref[:, 0:1], lt_ref[:, HEAD_DIM:HEAD_DIM + 1])
        causal, tri = _attn_masks(t, later=False)

        def step(kb, carry, masked):
            cls, cgs, dq = carry
            off = pl.multiple_of(kb * t, t)
            kblk = k_ref[pl.ds(off, t), :]
            vblk = v_ref[pl.ds(off, t), :]
            ks = _per_head(kblk)
            dk = jnp.zeros((t, LANES), F32)
            dv = jnp.zeros((t, LANES), F32)
            new_cls, new_cgs = [], []
            for hd in range(2):
                z = lax.dot_general(qs[hd], kblk, NT, preferred_element_type=F32)
                l = _logsig_neg(z)
                if masked:
                    l = jnp.where(causal, l, 0.0)
                e = z + l + ((lts[hd] - cls[hd]) - _split_dot(l, tri))
                if masked:
                    e = jnp.where(causal, e, -1e30)
                a = jnp.exp(e)
                g = lax.dot_general(dos[hd], vblk, NT, preferred_element_type=F32) * a
                p = cgs[hd] + jnp.dot(g.astype(BF16), tri, preferred_element_type=F32) - g
                el = jnp.exp(l)
                dz = g * el - p * (1.0 - el)
                if masked:
                    dz = jnp.where(causal, dz, 0.0)
                dzb = (dz * scale).astype(BF16)
                dq = dq + jnp.dot(dzb, ks[hd], preferred_element_type=F32)
                dk = dk + lax.dot_general(dzb, qm[hd], TN, preferred_element_type=F32)
                dv = dv + lax.dot_general(a.astype(BF16), dos[hd], TN, preferred_element_type=F32)
                new_cls.append(cls[hd] + jnp.sum(l, axis=1, keepdims=True))
                new_cgs.append(cgs[hd] + jnp.sum(g, axis=1, keepdims=True))
            dk_ref[pl.ds(off, t), :] += dk
            dv_ref[pl.ds(off, t), :] += dv
            return tuple(new_cls), tuple(new_cgs), dq

        zero = jnp.zeros((t, 1), F32)
        init = ((zero, zero), (zero, zero), jnp.zeros((t, LANES), F32))
        carry = lax.fori_loop(first, i, lambda kb, cr: step(kb, cr, False), init)
        carry = step(i, carry, True)
        dq_ref[...] = carry[2]
        finish_exchange()

    blk = pl.BlockSpec((t, LANES), lambda p, i: (i, p))
    whole = pl.BlockSpec((s, LANES), lambda p, i: (0, p))
    res = pl.pallas_call(
        body, name="attn_bwd", grid=(hp, s // t),
        in_specs=[blk, whole, whole, blk, pl.BlockSpec((None, t, LANES), lambda p, i: (p, i, 0)),
                  pl.BlockSpec(memory_space=pltpu.SMEM)] + [ANY] * nx,
        out_specs=[blk, whole, whole] + [ANY] * nx,
        out_shape=[_sds(q.shape)] * 3 + x_shape,
        scratch_shapes=x_scratch,
        compiler_params=_params(("arbitrary", "arbitrary")),
    )(q, k, v, do, ltot, n_blocks, *x_arrs)
    return res[0], res[1], res[2], res[3:]


LATE = ["w_conv_branch", "w_att_branch", "w_out", "w_ffn_up", "w_ffn_down"]


def _full_weight(name, gathered):
    return _cols_to_full(gathered) if name in COL_SHARDED else gathered.reshape(-1, gathered.shape[2])


def _grad_slabs(name, grad):
    return _full_to_cols(grad) if name in COL_SHARDED else grad.reshape(N_DEV, -1, grad.shape[1])


def _local_step(x, target, w, late_blocks):
    s = x.shape[0]
    w = dict(w)
    g1, g2, g3, g4 = w["norm_mix_pre"], w["norm_mix_post"], w["norm_ffn_pre"], w["norm_ffn_post"]

    w_in = w["w_in"]

    def proj_fn(xt, g1_, w_in_):
        h = _rms(xt, g1_).astype(BF16)
        proj = jnp.dot(h, w_in_, preferred_element_type=F32)
        return (h, *[proj[:, IN_SPLITS[n]:IN_SPLITS[n + 1]] for n in range(6)]), ()

    h1, conv_in, q, k, v, g_conv, g_att = _rowwise(
        "norm_proj", proj_fn, [x], [g1, w_in],
        [_sds((s, D_MODEL), BF16), _sds((s, 2 * CONV_DIM)), _sds((s, ATT_DIM), BF16), _sds((s, ATT_DIM), BF16),
         _sds((s, ATT_DIM), BF16), _sds((s, D_MODEL)), _sds((s, D_MODEL))], tm=512)

    mix_weights = ["w_conv_branch", "w_att_branch", "w_out"]
    u3, u1, gathered = _conv_fwd(conv_in, w["conv_dw_w"], w["conv_dw_b"], w["conv_ln_g"], w["conv_ln_b"],
                                 _gather_exchange([late_blocks[nm] for nm in mix_weights]))
    for nm, g in zip(mix_weights, gathered):
        w[nm] = _full_weight(nm, g)
    att, ltot, n_blocks, (g_up,) = _attn_fwd(q, k, v, _gather_exchange([late_blocks["w_ffn_up"]]))
    w["w_ffn_up"] = _full_weight("w_ffn_up", g_up)

    def merge_fn(u3t, at, gc, ga, w_cb, w_ab, b_cb):
        cp = jnp.dot(u3t, w_cb, preferred_element_type=F32)
        ao = jnp.dot(at, w_ab, preferred_element_type=F32)
        return (_merge(cp, ao, gc, ga, b_cb), cp, ao), ()

    merged, conv_pre, att_out = _rowwise(
        "branch_merge", merge_fn, [u3, att, g_conv, g_att], [w["w_conv_branch"], w["w_att_branch"], w["b_conv_branch"]],
        [_sds((s, D_MODEL), BF16)] * 3, tm=512)

    def mid_fn(mt, xt, w_out, g2_, g3_):
        mix_ = jnp.dot(mt, w_out, preferred_element_type=F32)
        x2_ = xt + _rms(mix_, g2_)
        return (mix_, x2_, _rms(x2_, g3_)), ()

    mix, x2, h2 = _rowwise("mix_mid_norm", mid_fn, [merged, x], [w["w_out"], g2, g3],
                           [_sds((s, D_MODEL)), _sds((s, D_MODEL)), _sds((s, D_MODEL), BF16)], tm=512)

    def ffn_up_fn(ht, w_up):
        gu_ = jnp.dot(ht, w_up, preferred_element_type=F32)
        return (gu_, _swiglu(gu_[:, :D_FF], gu_[:, D_FF:])), ()

    gu, act, g_down = _rowwise("ffn_up", ffn_up_fn, [h2], [w["w_ffn_up"]],
                               [_sds((s, 2 * D_FF), BF16), _sds((s, D_FF), BF16)], tm=512,
                               exchange=_gather_exchange([late_blocks["w_ffn_down"]]))
    w["w_ffn_down"] = _full_weight("w_ffn_down", g_down)

    def final_fn(at, x2t, tgt, w_down, g4_):
        ff = jnp.dot(at, w_down, preferred_element_type=F32)
        n4, vjp = jax.vjp(_rms, ff, g4_)
        err = x2t + n4 - tgt
        dy = err * (1.0 / D_MODEL)
        dff, dg4 = vjp(dy)
        return (dy, dff), (jnp.sum(err * err, axis=0, keepdims=True), dg4)

    dy, dff, loss_cols, d_g4 = _rowwise("ffn_down_loss", final_fn, [act, x2, target], [w["w_ffn_down"], g4],
                                        [_sds((s, D_MODEL)), _sds((s, D_MODEL), BF16)],
                                        [_sds((1, D_MODEL)), _sds((1, D_MODEL))], tm=512)
    loss = 0.5 * jnp.sum(loss_cols) / D_MODEL

    d_w_down = _matmul(act, dff, ta=True, name="d_w_down", out_dtype=BF16)

    def act_bwd_fn(dfft, gut, w_down):
        d_act = lax.dot_general(dfft, w_down, NT, preferred_element_type=F32)
        gu_ = gut.astype(F32)
        _, vjp = jax.vjp(_swiglu, gu_[:, :D_FF], gu_[:, D_FF:])
        return (jnp.concatenate(vjp(d_act), axis=1),), ()

    (dgu,) = _rowwise("ffn_act_bwd", act_bwd_fn, [dff, gu], [w["w_ffn_down"]], [_sds((s, 2 * D_FF), BF16)])
    d_w_up = _matmul(h2, dgu, ta=True, name="d_w_up", out_dtype=BF16)
    received = {}

    def mid_bwd_fn(dgut, xt, mt, dyt, w_up, g2_, g3_):
        dh = lax.dot_general(dgut, w_up, NT, preferred_element_type=F32)
        n2, vjp2 = jax.vjp(_rms, mt, g2_)
        x2_ = xt + n2
        _, vjp3 = jax.vjp(_rms, x2_, g3_)
        dx2_, dg3 = vjp3(dh)
        dx2_ = dx2_ + dyt
        dmix_, dg2 = vjp2(dx2_)
        return (dx2_, dmix_), (dg2, dg3)

    dx2, dmix, d_g2, d_g3, received["w_ffn_down"] = _rowwise(
        "ffn_up_mid_bwd", mid_bwd_fn, [dgu, x, mix, dy], [w["w_ffn_up"], g2, g3],
        [_sds((s, D_MODEL)), _sds((s, D_MODEL), BF16)], [_sds((1, D_MODEL)), _sds((1, D_MODEL))], tm=512,
        exchange=_scatter_exchange([_grad_slabs("w_ffn_down", d_w_down)]))
    d_w_out = _matmul(merged, dmix, ta=True, name="d_w_out", out_dtype=BF16)

    def merge_bwd_fn(dmt, cp, ao, gc, ga, w_out, w_cb, w_ab, b_cb):
        dm = lax.dot_general(dmt, w_out, NT, preferred_element_type=F32)
        _, vjp = jax.vjp(_merge, cp.astype(F32), ao.astype(F32), gc, ga, b_cb)
        dcp, dao, dgc, dga, dbias = vjp(dm)
        dcp, dao = dcp.astype(BF16), dao.astype(BF16)
        du3_ = lax.dot_general(dcp, w_cb, NT, preferred_element_type=F32)
        datt_ = lax.dot_general(dao, w_ab, NT, preferred_element_type=F32)
        return (dcp, dao, dgc, dga, du3_, datt_), (dbias,)

    d_conv_out, d_att_out, d_g_conv, d_g_att, du3, d_att, d_b_cb = _rowwise(
        "merge_bwd", merge_bwd_fn, [dmix, conv_pre, att_out, g_conv, g_att],
        [w["w_out"], w["w_conv_branch"], w["w_att_branch"], w["b_conv_branch"]],
        [_sds((s, D_MODEL), BF16)] * 4 + [_sds((s, CONV_DIM)), _sds((s, ATT_DIM), BF16)], [_sds((1, D_MODEL))], tm=512)

    d_w_cb = _matmul(u3, d_conv_out, ta=True, name="d_w_conv_branch", out_dtype=BF16)
    d_w_ab = _matmul(att, d_att_out, ta=True, name="d_w_att_branch", out_dtype=BF16)

    dq, dk, dv, (received["w_ffn_up"],) = _attn_bwd(
        q, k, v, d_att, ltot, n_blocks, _scatter_exchange([_grad_slabs("w_ffn_up", d_w_up)]))

    def ln_bwd_fn(u1t, du3t, g_, b_):
        _, vjp = jax.vjp(_ln_silu, u1t, g_, b_)
        du1_, dg_, db_ = vjp(du3t)
        return (du1_,), (dg_, db_)

    du1, d_ln_g, d_ln_b = _rowwise("conv_ln_bwd", ln_bwd_fn, [u1, du3], [w["conv_ln_g"], w["conv_ln_b"]],
                                   [_sds((s, CONV_DIM))], [_sds((1, CONV_DIM)), _sds((1, CONV_DIM))])
    mix_grads = {"w_conv_branch": d_w_cb, "w_att_branch": d_w_ab, "w_out": d_w_out}
    d_conv_in, d_dw_w, d_dw_b, landed = _conv_bwd(
        conv_in, du1, w["conv_dw_w"], _scatter_exchange([_grad_slabs(nm, mix_grads[nm]) for nm in mix_weights]))
    received.update(zip(mix_weights, landed))

    d_proj = jnp.concatenate([d_conv_in, dq.astype(BF16), dk.astype(BF16), dv.astype(BF16), d_g_conv, d_g_att],
                             axis=1)
    d_w_in = _matmul(h1, d_proj, ta=True, name="d_w_in", out_dtype=BF16)

    def pre_bwd_fn(dpt, xt, dx2t, w_in_, g_):
        dh = lax.dot_general(dpt, w_in_, NT, preferred_element_type=F32)
        _, vjp = jax.vjp(_rms, xt, g_)
        dx_, dg_ = vjp(dh)
        return (dx_ + dx2t,), (dg_,)

    grad_x, d_g1, received["w_in"] = _rowwise(
        "proj_norm_bwd", pre_bwd_fn, [d_proj, x, dx2], [w_in, g1], [_sds((s, D_MODEL))], [_sds((1, D_MODEL))], tm=512,
        exchange=_scatter_exchange([_grad_slabs("w_in", d_w_in)]))

    grads = {
        "norm_mix_pre": d_g1, "conv_dw_w": d_dw_w, "conv_dw_b": d_dw_b,
        "conv_ln_g": d_ln_g, "conv_ln_b": d_ln_b, "b_conv_branch": d_b_cb,
        "norm_mix_post": d_g2, "norm_ffn_pre": d_g3, "norm_ffn_post": d_g4,
    }
    return loss, grad_x, received, grads


def _place():
    x, y, c = lax.axis_index("x"), lax.axis_index("y"), lax.axis_index("c")
    return x, y, c


def _slot(px, py, pc):
    return 4 * px + 2 * py + pc


def _exchange_scratch(n):
    return [pltpu.SemaphoreType.DMA((7 * n,)), pltpu.SemaphoreType.DMA((7 * n,)), pltpu.SemaphoreType.DMA((n,))]


def _gather_exchange(arrs):
    n = len(arrs)

    def phases(ins, outs, send_sems, recv_sems, local_sems):
        x, y, c = _place()
        me, sibling = (x, y, c), (x, y, 1 - c)
        chips = [(1 - x, y), (x, 1 - y), (1 - x, 1 - y)]

        def copy(a, kk, block, to, src=None):
            dst = outs[a].at[_slot(*block)]
            return pltpu.make_async_remote_copy(
                src_ref=dst if src is None else src, dst_ref=dst,
                send_sem=send_sems.at[a * 7 + kk], recv_sem=recv_sems.at[a * 7 + kk],
                device_id=to, device_id_type=MESH)

        mine = [pltpu.make_async_copy(ins[a], outs[a].at[_slot(*me)], local_sems.at[a]) for a in range(n)]
        first = []
        for a in range(n):
            first.append(copy(a, 0, me, sibling, src=ins[a]))
            first += [copy(a, 1 + j, me, (*chip, c), src=ins[a]) for j, chip in enumerate(chips)]
        passed = [copy(a, 4 + j, (*chip, c), sibling) for j, chip in enumerate(chips) for a in range(n)]

        def send():
            for cp in mine + first:
                cp.start()

        def pass_on():
            for j, chip in enumerate(chips):
                for a in range(n):
                    copy(a, 1 + j, (*chip, c), me).wait_recv()
                    passed[j * n + a].start()

        def finish():
            for a in range(n):
                copy(a, 0, sibling, me).wait_recv()
                for j, chip in enumerate(chips):
                    copy(a, 4 + j, (*chip, 1 - c), me).wait_recv()
            for cp in first + passed:
                cp.wait_send()
            for cp in mine:
                cp.wait()

        return [send, pass_on, finish]

    return list(arrs), [_sds((N_DEV,) + a.shape, a.dtype) for a in arrs], _exchange_scratch(n), phases


def _scatter_exchange(arrs):
    n = len(arrs)
    flips = [(fx, fy, fc) for fx in (0, 1) for fy in (0, 1) for fc in (0, 1)][1:]

    def phases(ins, outs, send_sems, recv_sems, local_sems):
        x, y, c = _place()
        mine = _slot(x, y, c)
        local = [pltpu.make_async_copy(ins[a].at[mine], outs[a].at[mine], local_sems.at[a]) for a in range(n)]
        peers = [((1 - x) if fx else x, (1 - y) if fy else y, (1 - c) if fc else c) for fx, fy, fc in flips]

        def copy(a, kk, src_slot, dst_slot):
            return pltpu.make_async_remote_copy(
                src_ref=ins[a].at[src_slot], dst_ref=outs[a].at[dst_slot],
                send_sem=send_sems.at[a * 7 + kk], recv_sem=recv_sems.at[a * 7 + kk],
                device_id=peers[kk], device_id_type=MESH)

        sends = [copy(a, kk, _slot(*peers[kk]), mine) for a in range(n) for kk in range(7)]

        def send():
            for cp in local + sends:
                cp.start()

        def finish():
            for a in range(n):
                for kk in range(7):
                    copy(a, kk, mine, _slot(*peers[kk])).wait_recv()
            for cp in sends:
                cp.wait_send()
            for cp in local:
                cp.wait()

        return [send, finish]

    return list(arrs), [_sds(a.shape, a.dtype) for a in arrs], _exchange_scratch(n), phases


def _exchange_call(name, exchange):
    arrs, out_shape, scratch, phases = exchange
    n = len(arrs)

    def body(*refs):
        for step in phases(refs[:n], refs[n:2 * n], *refs[2 * n:]):
            step()

    return pl.pallas_call(body, name=name, in_specs=[ANY] * n, out_specs=[ANY] * n,
                          out_shape=out_shape, scratch_shapes=scratch)(*arrs)


def _carry_exchange(exchange, refs, n_in, n_out, first, middle, last):
    arrs, _, _, phases = exchange
    n = len(arrs)
    if n == 0:
        return lambda: None
    ins = refs[n_in:n_in + n]
    outs = refs[n_in + n + n_out:n_in + 2 * n + n_out]
    sems = n_in + 2 * n + n_out
    steps = phases(ins, outs, *refs[sems:sems + 3])
    pl.when(first)(steps[0])
    if len(steps) == 3:
        pl.when(middle)(steps[1])
    return lambda: pl.when(last)(steps[-1])


def _adamw_math(w, g, m, v):
    m2 = ADAM_B1 * m + (1.0 - ADAM_B1) * g
    v2 = ADAM_B2 * v + (1.0 - ADAM_B2) * jnp.square(g)
    m_hat = m2 / (1.0 - ADAM_B1 ** ADAM_STEP)
    v_hat = v2 / (1.0 - ADAM_B2 ** ADAM_STEP)
    delta = -ADAM_LR * (m_hat / (jnp.sqrt(v_hat) + ADAM_EPS) + ADAM_WD * w)
    return delta, m2, v2


def _sum_adamw(name, parts, w, m, v, tr=256):
    p, r, c = parts.shape
    tr = _pick(r, tr, 16)

    def body(p_ref, w_ref, m_ref, v_ref, g_ref, d_ref, m2_ref, v2_ref):
        g = p_ref[0].astype(F32)
        for d in range(1, p):
            g = g + p_ref[d].astype(F32)
        delta, m2, v2 = _adamw_math(w_ref[...], g, m_ref[...], v_ref[...])
        g_ref[...] = g
        d_ref[...] = delta
        m2_ref[...] = m2
        v2_ref[...] = v2

    tile = pl.BlockSpec((tr, c), lambda i: (i, 0))
    return pl.pallas_call(
        body, name=name, grid=(r // tr,),
        in_specs=[pl.BlockSpec((p, tr, c), lambda i: (0, i, 0)), tile, tile, tile],
        out_specs=[tile] * 4, out_shape=[_sds((r, c))] * 4,
        compiler_params=_params(("parallel",)),
    )(parts, w, m, v)


def _sum_parts(name, parts):
    p, r, c = parts.shape

    def body(p_ref, o_ref):
        g = p_ref[0]
        for d in range(1, p):
            g = g + p_ref[d]
        o_ref[...] = g

    return pl.pallas_call(
        body, name=name, out_shape=_sds((r, c)),
        in_specs=[pl.BlockSpec(memory_space=pltpu.VMEM)], out_specs=pl.BlockSpec(memory_space=pltpu.VMEM),
    )(parts)


WEIGHTS = ["norm_mix_pre", "w_in", "conv_dw_w", "conv_dw_b", "conv_ln_g", "conv_ln_b", "w_conv_branch",
           "b_conv_branch", "w_att_branch", "w_out", "norm_mix_post", "norm_ffn_pre", "w_ffn_up", "w_ffn_down",
           "norm_ffn_post"]
COL_SHARDED = ["w_in", "w_conv_branch", "w_att_branch", "w_ffn_up"]
ROW_SHARDED = ["w_out", "w_ffn_down"]
VECTORS = ["norm_mix_pre", "conv_dw_b", "conv_ln_g", "conv_ln_b", "b_conv_branch", "norm_mix_post",
           "norm_ffn_pre", "norm_ffn_post"]


def _cols_to_full(g):
    return g.transpose(1, 0, 2).reshape(g.shape[1], N_DEV * g.shape[2])


def _full_to_cols(f):
    return f.reshape(f.shape[0], N_DEV, f.shape[1] // N_DEV).transpose(1, 0, 2)


def _pack_vectors(vecs):
    rows = [jnp.pad(vecs[nm].reshape(-1), (0, D_MODEL - vecs[nm].size)) for nm in VECTORS]
    return jnp.stack(rows)


def _unpack_vectors(packed, sizes):
    return {nm: packed[n, :sizes[nm]] for n, nm in enumerate(VECTORS)}


def kernel(x, norm_mix_pre, w_in, conv_dw_w, conv_dw_b, conv_ln_g, conv_ln_b, w_conv_branch, b_conv_branch, w_att_branch, w_out, norm_mix_post, norm_ffn_pre, w_ffn_up, w_ffn_down, norm_ffn_post, loss_target, m_norm_mix_pre, m_w_in, m_conv_dw_w, m_conv_dw_b, m_conv_ln_g, m_conv_ln_b, m_w_conv_branch, m_b_conv_branch, m_w_att_branch, m_w_out, m_norm_mix_post, m_norm_ffn_pre, m_w_ffn_up, m_w_ffn_down, m_norm_ffn_post, v_norm_mix_pre, v_w_in, v_conv_dw_w, v_conv_dw_b, v_conv_ln_g, v_conv_ln_b, v_w_conv_branch, v_b_conv_branch, v_w_att_branch, v_w_out, v_norm_mix_post, v_norm_ffn_pre, v_w_ffn_up, v_w_ffn_down, v_norm_ffn_post):
    ws = dict(zip(WEIGHTS, [norm_mix_pre, w_in, conv_dw_w, conv_dw_b, conv_ln_g, conv_ln_b, w_conv_branch,
                            b_conv_branch, w_att_branch, w_out, norm_mix_post, norm_ffn_pre, w_ffn_up, w_ffn_down,
                            norm_ffn_post]))
    ms = dict(zip(WEIGHTS, [m_norm_mix_pre, m_w_in, m_conv_dw_w, m_conv_dw_b, m_conv_ln_g, m_conv_ln_b,
                            m_w_conv_branch, m_b_conv_branch, m_w_att_branch, m_w_out, m_norm_mix_post,
                            m_norm_ffn_pre, m_w_ffn_up, m_w_ffn_down, m_norm_ffn_post]))
    vs = dict(zip(WEIGHTS, [v_norm_mix_pre, v_w_in, v_conv_dw_w, v_conv_dw_b, v_conv_ln_g, v_conv_ln_b,
                            v_w_conv_branch, v_b_conv_branch, v_w_att_branch, v_w_out, v_norm_mix_post,
                            v_norm_ffn_pre, v_w_ffn_up, v_w_ffn_down, v_norm_ffn_post]))

    dw_block = jnp.pad(conv_dw_w, ((0, 1), (0, 0)))
    g_in, g_dw = _exchange_call("gather_first", _gather_exchange([w_in.astype(BF16), dw_block]))
    full = {"w_in": _full_weight("w_in", g_in), "conv_dw_w": _cols_to_full(g_dw)}
    for nm in VECTORS:
        full[nm] = ws[nm].reshape(1, -1)

    loss_local, grad_x, received, grads = _local_step(
        x[0], loss_target[0], full, {nm: ws[nm].astype(BF16) for nm in LATE})
    loss = lax.psum(loss_local, ("x", "y", "c"))

    small = _exchange_call("gather_small_grads", _gather_exchange([_pack_vectors(grads), grads["conv_dw_w"]]))
    out_g, out_d, out_m, out_v = {}, {}, {}, {}
    for nm in LATE + ["w_in"]:
        out_g[nm], out_d[nm], out_m[nm], out_v[nm] = _sum_adamw("adamw_" + nm, received[nm], ws[nm], ms[nm], vs[nm])
    sizes = {nm: ws[nm].size for nm in VECTORS}
    vec = _sum_adamw("adamw_vectors", small[0], _pack_vectors(ws), _pack_vectors(ms), _pack_vectors(vs))
    for res, dst in zip(vec, (out_g, out_d, out_m, out_v)):
        dst.update(_unpack_vectors(res, sizes))
    dw_full = _sum_parts("sum_dw_grads", small[1])
    me = _slot(*_place())
    dw_mine = lax.dynamic_slice(dw_full, (0, me * (CONV_DIM // N_DEV)), (CONV_WIDTH, CONV_DIM // N_DEV))
    nm = "conv_dw_w"
    out_g[nm], out_d[nm], out_m[nm], out_v[nm] = _sum_adamw("adamw_dw", dw_mine[None], ws[nm], ms[nm], vs[nm])

    outs = [loss, grad_x[None]]
    for group in (out_g, out_d, out_m, out_v):
        outs += [group[nm] for nm in WEIGHTS]
    return tuple(outs)
```

```python
import math

import jax
import jax.numpy as jnp
from jax import lax
from jax.experimental import pallas as pl
from jax.experimental.pallas import tpu as pltpu

F32 = jnp.float32
BF16 = jnp.bfloat16

N_DEV = 8
D_MODEL = 1024
CONV_DIM = 512
CONV_WIDTH = 31
N_HEADS = 8
HEAD_DIM = 64
ATT_DIM = N_HEADS * HEAD_DIM
D_FF = 2816
EPS = 1e-6
IN_SPLITS = (0, 1024, 1536, 2048, 2560, 3584, 4608)

ADAM_LR = 0.001
ADAM_B1 = 0.9
ADAM_B2 = 0.999
ADAM_EPS = 1e-08
ADAM_WD = 0.01
ADAM_STEP = 10

LANES = 128
SUBLANES = 8
HALO = 32
ATT_TILE = 256
DEAD_SUM = -120.0
VMEM_LIMIT = 56 * 1024 * 1024
MESH = pl.DeviceIdType.MESH
ANY = pl.BlockSpec(memory_space=pl.ANY)


def _pick(dim, target, align=LANES):
    t = min(dim, target)
    t -= t % align
    while t >= align:
        if dim % t == 0:
            return t
        t -= align
    return dim


def _params(semantics):
    return pltpu.CompilerParams(dimension_semantics=semantics, vmem_limit_bytes=VMEM_LIMIT)


def _matmul(a, b, *, name, ta=False, tb=False, out_dtype=F32):
    m, k = (a.shape[1], a.shape[0]) if ta else a.shape
    n, k2 = b.shape if tb else (b.shape[1], b.shape[0])
    assert k == k2, (a.shape, b.shape, ta, tb)
    tm, tn, tk = _pick(m, 1408 if ta else 512), _pick(n, 1536), _pick(k, 1536)
    nk = k // tk
    dims = (((0 if ta else 1,), (1 if tb else 0,)), ((), ()))

    def body(a_ref, b_ref, o_ref, *acc):
        part = lax.dot_general(a_ref[...], b_ref[...], dims, preferred_element_type=F32)
        if nk == 1:
            o_ref[...] = part.astype(o_ref.dtype)
            return
        acc_ref, = acc
        kk = pl.program_id(2)

        @pl.when(kk == 0)
        def _():
            acc_ref[...] = part

        @pl.when((kk > 0) & (kk < nk - 1))
        def _():
            acc_ref[...] += part

        @pl.when(kk == nk - 1)
        def _():
            o_ref[...] = (acc_ref[...] + part).astype(o_ref.dtype)

    a_spec = pl.BlockSpec((tk, tm), lambda j, i, kk: (kk, i)) if ta else pl.BlockSpec((tm, tk), lambda j, i, kk: (i, kk))
    b_spec = (pl.BlockSpec((tn, tk), lambda j, i, kk: (j, kk)) if tb
              else pl.BlockSpec((tk, tn), lambda j, i, kk: (kk, j)))
    return pl.pallas_call(
        body, name=name, grid=(n // tn, m // tm, nk),
        in_specs=[a_spec, b_spec],
        out_specs=pl.BlockSpec((tm, tn), lambda j, i, kk: (i, j)),
        out_shape=jax.ShapeDtypeStruct((m, n), out_dtype),
        scratch_shapes=[pltpu.VMEM((tm, tn), F32)] if nk > 1 else [],
        compiler_params=_params(("parallel", "parallel", "arbitrary")),
    )(a, b)


NO_EXCHANGE = ([], [], [], None)


def _sweep_marks(nt):
    i = pl.program_id(0)
    return i == 0, i == (3 * nt) // 4, i == nt - 1


def _rowwise(name, fn, rows, bcasts, row_outs, red_outs=(), tm=256, exchange=NO_EXCHANGE):
    s = rows[0].shape[0]
    tm = _pick(s, tm, 16)
    nt = s // tm
    resident = pl.Buffered(1)
    nr, nb, no, nd = len(rows), len(bcasts), len(row_outs), len(red_outs)
    x_arrs, x_shape, x_scratch, _ = exchange
    nx = len(x_arrs)
    first_out = nr + nb + nx

    def body(*refs):
        finish_exchange = _carry_exchange(exchange, refs, nr + nb, no + nd, *_sweep_marks(nt))
        ins = [r[...] for r in refs[:nr + nb]]
        outs, reds = fn(*ins)
        for ref, val in zip(refs[first_out:first_out + no], outs):
            ref[...] = val.astype(ref.dtype)
        i = pl.program_id(0)
        for ref, val in zip(refs[first_out + no:first_out + no + nd], reds):
            @pl.when(i == 0)
            def _():
                ref[...] = val

            @pl.when(i > 0)
            def _():
                ref[...] += val
        finish_exchange()

    in_specs = [pl.BlockSpec((tm, r.shape[1]), lambda i: (i, 0)) for r in rows]
    in_specs += [pl.BlockSpec(b.shape, lambda i: (0, 0), pipeline_mode=resident) for b in bcasts]
    out_specs = [pl.BlockSpec((tm, o.shape[1]), lambda i: (i, 0)) for o in row_outs]
    out_specs += [pl.BlockSpec(d.shape, lambda i: (0, 0)) for d in red_outs]
    return pl.pallas_call(
        body, name=name, grid=(nt,), in_specs=in_specs + [ANY] * nx, out_specs=out_specs + [ANY] * nx,
        out_shape=list(row_outs) + list(red_outs) + x_shape, scratch_shapes=x_scratch,
        compiler_params=_params(("arbitrary",)),
    )(*rows, *bcasts, *x_arrs)


def _sds(shape, dtype=F32):
    return jax.ShapeDtypeStruct(shape, dtype)


def _rms(x, g):
    y = x * lax.rsqrt(jnp.mean(x * x, axis=-1, keepdims=True) + EPS)
    return y * g


def _silu(x):
    return x * jax.nn.sigmoid(x)


def _swiglu(g, u):
    return _silu(g) * u


def _ln_silu(u, g, b):
    mu = jnp.mean(u, axis=-1, keepdims=True)
    var = jnp.mean(jnp.square(u - mu), axis=-1, keepdims=True)
    return _silu((u - mu) * lax.rsqrt(var + EPS) * g + b)


def _merge(conv_pre, att_out, g_conv, g_att, b_cb):
    return jax.nn.sigmoid(g_conv) * (conv_pre + b_cb) + jax.nn.sigmoid(g_att) * att_out


def _glu(t):
    return t[:, :CONV_DIM] * jax.nn.sigmoid(t[:, CONV_DIM:])


def _conv_fwd(conv_in, w_pad, b, ln_g, ln_b, exchange, tm=256):
    s = conv_in.shape[0]
    tm = _pick(s, tm, HALO)
    ratio = tm // HALO
    x_arrs, x_shape, x_scratch, _ = exchange
    nx = len(x_arrs)

    def body(*refs):
        main_ref, halo_ref, w_ref, b_ref, g_ref, be_ref = refs[:6]
        u3_ref, u1_ref = refs[6 + nx:8 + nx]
        buf = refs[-1]
        finish_exchange = _carry_exchange(exchange, refs, 6, 2, *_sweep_marks(s // tm))
        i = pl.program_id(0)
        buf[0:HALO, :] = _glu(halo_ref[...]) * (i > 0).astype(F32)
        buf[HALO:HALO + tm, :] = _glu(main_ref[...])
        acc = jnp.zeros((tm, CONV_DIM), F32) + b_ref[...]
        for j in range(CONV_WIDTH):
            acc = acc + w_ref[j:j + 1, :] * buf[pl.ds(HALO - (CONV_WIDTH - 1) + j, tm), :]
        u1_ref[...] = acc
        u3_ref[...] = _ln_silu(acc, g_ref[...], be_ref[...]).astype(u3_ref.dtype)
        finish_exchange()

    res = pl.pallas_call(
        body, name="conv_fwd", grid=(s // tm,),
        in_specs=[pl.BlockSpec((tm, 2 * CONV_DIM), lambda i: (i, 0)),
                  pl.BlockSpec((HALO, 2 * CONV_DIM), lambda i: (jnp.maximum(i * ratio - 1, 0), 0)),
                  pl.BlockSpec(w_pad.shape, lambda i: (0, 0)),
                  pl.BlockSpec(b.shape, lambda i: (0, 0)),
                  pl.BlockSpec(ln_g.shape, lambda i: (0, 0)),
                  pl.BlockSpec(ln_b.shape, lambda i: (0, 0))] + [ANY] * nx,
        out_specs=[pl.BlockSpec((tm, CONV_DIM), lambda i: (i, 0)),
                   pl.BlockSpec((tm, CONV_DIM), lambda i: (i, 0))] + [ANY] * nx,
        out_shape=[_sds((s, CONV_DIM), BF16), _sds((s, CONV_DIM), F32)] + x_shape,
        scratch_shapes=x_scratch + [pltpu.VMEM((tm + HALO, CONV_DIM), F32)],
        compiler_params=_params(("arbitrary",)),
    )(conv_in, conv_in, w_pad, b, ln_g, ln_b, *x_arrs)
    return res[0], res[1], res[2:]


def _conv_bwd(conv_in, du1, w_pad, exchange, tm=256):
    s = conv_in.shape[0]
    tm = _pick(s, tm, HALO)
    ratio = tm // HALO
    nt = s // tm
    last_halo = s // HALO - 1
    x_arrs, x_shape, x_scratch, _ = exchange
    nx = len(x_arrs)

    def body(*refs):
        main_ref, halo_ref, du_ref, dun_ref, w_ref = refs[:5]
        dci_ref, dw_ref, db_ref = refs[5 + nx:8 + nx]
        ubuf, dbuf = refs[-2:]
        finish_exchange = _carry_exchange(exchange, refs, 5, 3, *_sweep_marks(nt))
        i = pl.program_id(0)
        main = main_ref[...]
        a = main[:, :CONV_DIM]
        sb = jax.nn.sigmoid(main[:, CONV_DIM:])
        ubuf[0:HALO, :] = _glu(halo_ref[...]) * (i > 0).astype(F32)
        ubuf[HALO:HALO + tm, :] = a * sb
        du = du_ref[...]
        dbuf[0:tm, :] = du
        dbuf[tm:tm + HALO, :] = dun_ref[...] * (i < nt - 1).astype(F32)

        @pl.when(i == 0)
        def _():
            dw_ref[...] = jnp.zeros_like(dw_ref)
            db_ref[...] = jnp.zeros_like(db_ref)

        du0 = jnp.zeros((tm, CONV_DIM), F32)
        for j in range(CONV_WIDTH):
            du0 = du0 + w_ref[j:j + 1, :] * dbuf[pl.ds(CONV_WIDTH - 1 - j, tm), :]
            dw_ref[j:j + 1, :] += jnp.sum(du * ubuf[pl.ds(HALO - (CONV_WIDTH - 1) + j, tm), :], axis=0, keepdims=True)
        db_ref[...] += jnp.sum(du, axis=0, keepdims=True)
        dci_ref[:, :CONV_DIM] = (du0 * sb).astype(dci_ref.dtype)
        dci_ref[:, CONV_DIM:] = (du0 * a * sb * (1.0 - sb)).astype(dci_ref.dtype)
        finish_exchange()

    res = pl.pallas_call(
        body, name="conv_bwd", grid=(nt,),
        in_specs=[pl.BlockSpec((tm, 2 * CONV_DIM), lambda i: (i, 0)),
                  pl.BlockSpec((HALO, 2 * CONV_DIM), lambda i: (jnp.maximum(i * ratio - 1, 0), 0)),
                  pl.BlockSpec((tm, CONV_DIM), lambda i: (i, 0)),
                  pl.BlockSpec((HALO, CONV_DIM), lambda i: (jnp.minimum((i + 1) * ratio, last_halo), 0)),
                  pl.BlockSpec(w_pad.shape, lambda i: (0, 0))] + [ANY] * nx,
        out_specs=[pl.BlockSpec((tm, 2 * CONV_DIM), lambda i: (i, 0)),
                   pl.BlockSpec(w_pad.shape, lambda i: (0, 0)),
                   pl.BlockSpec((1, CONV_DIM), lambda i: (0, 0))] + [ANY] * nx,
        out_shape=[_sds((s, 2 * CONV_DIM), BF16), _sds(w_pad.shape), _sds((1, CONV_DIM))] + x_shape,
        scratch_shapes=x_scratch + [pltpu.VMEM((tm + HALO, CONV_DIM), F32), pltpu.VMEM((tm + HALO, CONV_DIM), F32)],
        compiler_params=_params(("arbitrary",)),
    )(conv_in, conv_in, du1, du1, w_pad, *x_arrs)
    return res[0], res[1], res[2], res[3:]


def _logsig_neg(z):
    return jnp.minimum(-z, 0.0) - jnp.log(1.0 + jnp.exp(-jnp.abs(z)))


def _split_dot(val, tri):
    hi = val.astype(BF16)
    lo = (val - hi.astype(F32)).astype(BF16)
    return jnp.dot(hi, tri, preferred_element_type=F32) + jnp.dot(lo, tri, preferred_element_type=F32)


def _attn_masks(t, later):
    row = lax.broadcasted_iota(jnp.int32, (t, t), 0)
    col = lax.broadcasted_iota(jnp.int32, (t, t), 1)
    tri = jnp.where(row > col if later else row <= col, 1.0, 0.0).astype(BF16)
    return col < row, tri


def _grid_marks(h, nq):
    hh, i = pl.program_id(0), pl.program_id(1)
    return (hh == 0) & (i == 0), (hh == (3 * h) // 4) & (i == 0), (hh == h - 1) & (i == nq - 1)


def _head_masks(shape):
    lane = lax.broadcasted_iota(jnp.int32, shape, len(shape) - 1)
    return lane < HEAD_DIM, lane >= HEAD_DIM


def _per_head(blk):
    m0, m1 = _head_masks(blk.shape)
    zero = jnp.zeros_like(blk)
    return jnp.where(m0, blk, zero), jnp.where(m1, blk, zero)


NT = (((1,), (1,)), ((), ()))
TN = (((0,), (0,)), ((), ()))


def _attn_fwd(q, k, v, exchange):
    s = q.shape[0]
    hp = q.shape[1] // LANES
    t = ATT_TILE
    scale = 1.0 / math.sqrt(HEAD_DIM)
    x_arrs, x_shape, x_scratch, _ = exchange
    nx = len(x_arrs)

    def body(*refs):
        q_ref, k_ref, v_ref = refs[:3]
        o_ref, lt_ref, nb_ref = refs[3 + nx:6 + nx]
        finish_exchange = _carry_exchange(exchange, refs, 3, 3, *_grid_marks(hp, s // t))
        i = pl.program_id(1)
        qs = _per_head((q_ref[...].astype(F32) * scale).astype(BF16))
        causal, tri = _attn_masks(t, later=True)

        def step(kb, carry, masked):
            cs, acc = carry
            off = pl.multiple_of(kb * t, t)
            kblk = k_ref[pl.ds(off, t), :]
            vs = _per_head(v_ref[pl.ds(off, t), :])
            new_cs = []
            for hd in range(2):
                z = lax.dot_general(qs[hd], kblk, NT, preferred_element_type=F32)
                l = _logsig_neg(z)
                if masked:
                    l = jnp.where(causal, l, 0.0)
                e = z + l + _split_dot(l, tri) + cs[hd]
                if masked:
                    e = jnp.where(causal, e, -1e30)
                acc = acc + jnp.dot(jnp.exp(e).astype(BF16), vs[hd], preferred_element_type=F32)
                new_cs.append(cs[hd] + jnp.sum(l, axis=1, keepdims=True))
            return tuple(new_cs), acc

        zero = jnp.zeros((t, 1), F32)
        carry = step(i, ((zero, zero), jnp.zeros((t, LANES), F32)), True)

        def more(state):
            n, (cs, _) = state
            return (n < i) & (jnp.maximum(jnp.max(cs[0]), jnp.max(cs[1])) > DEAD_SUM)

        n_blocks, carry = lax.while_loop(more, lambda st: (st[0] + 1, step(i - 1 - st[0], st[1], False)),
                                         (jnp.int32(0), carry))
        m0, _ = _head_masks((t, LANES))
        lt_ref[...] = jnp.where(m0, carry[0][0], carry[0][1])
        o_ref[...] = carry[1].astype(o_ref.dtype)
        nb_ref[pl.program_id(0), i] = n_blocks.astype(F32)
        finish_exchange()

    res = pl.pallas_call(
        body, name="attn_fwd", grid=(hp, s // t),
        in_specs=[pl.BlockSpec((t, LANES), lambda p, i: (i, p)),
                  pl.BlockSpec((s, LANES), lambda p, i: (0, p)),
                  pl.BlockSpec((s, LANES), lambda p, i: (0, p))] + [ANY] * nx,
        out_specs=[pl.BlockSpec((t, LANES), lambda p, i: (i, p)),
                   pl.BlockSpec((None, t, LANES), lambda p, i: (p, i, 0)),
                   pl.BlockSpec(memory_space=pltpu.SMEM)] + [ANY] * nx,
        out_shape=[_sds(q.shape, BF16), _sds((hp, s, LANES), F32), _sds((hp, s // t), F32)] + x_shape,
        scratch_shapes=x_scratch,
        compiler_params=_params(("arbitrary", "arbitrary")),
    )(q, k, v, *x_arrs)
    return res[0], res[1], res[2], res[3:]


def _attn_bwd(q, k, v, do, ltot, n_blocks, exchange):
    s = q.shape[0]
    hp = q.shape[1] // LANES
    t = ATT_TILE
    scale = 1.0 / math.sqrt(HEAD_DIM)
    x_arrs, x_shape, x_scratch, _ = exchange
    nx = len(x_arrs)

    def body(*refs):
        q_ref, k_ref, v_ref, do_ref, lt_ref, nb_ref = refs[:6]
        dq_ref, dk_ref, dv_ref = refs[6 + nx:9 + nx]
        finish_exchange = _carry_exchange(exchange, refs, 6, 3, *_grid_marks(hp, s // t))
        i = pl.program_id(1)
        first = jnp.clip(i - nb_ref[pl.program_id(0), i].astype(jnp.int32), 0, i)

        @pl.when(i == 0)
        def _():
            dk_ref[...] = jnp.zeros_like(dk_ref)
            dv_ref[...] = jnp.zeros_like(dv_ref)

        qb = q_ref[...]
        qm = _per_head(qb)
        qs = _per_head((qb.astype(F32) * scale).astype(BF16))
        dos = _per_head(do_ref[...])
        lts = (lt_ref[:, 0:1], lt_ref[:, HEAD_DIM:HEAD_DIM + 1])
        causal, tri = _attn_masks(t, later=False)

        def step(kb, carry, masked):
            cls, cgs, dq = carry
            off = pl.multiple_of(kb * t, t)
            kblk = k_ref[pl.ds(off, t), :]
            vblk = v_ref[pl.ds(off, t), :]
            ks = _per_head(kblk)
            dk = jnp.zeros((t, LANES), F32)
            dv = jnp.zeros((t, LANES), F32)
            new_cls, new_cgs = [], []
            for hd in range(2):
                z = lax.dot_general(qs[hd], kblk, NT, preferred_element_type=F32)
                l = _logsig_neg(z)
                if masked:
                    l = jnp.where(causal, l, 0.0)
                e = z + l + ((lts[hd] - cls[hd]) - _split_dot(l, tri))
                if masked:
                    e = jnp.where(causal, e, -1e30)
                a = jnp.exp(e)
                g = lax.dot_general(dos[hd], vblk, NT, preferred_element_type=F32) * a
                p = cgs[hd] + jnp.dot(g.astype(BF16), tri, preferred_element_type=F32) - g
                el = jnp.exp(l)
                dz = g * el - p * (1.0 - el)
                if masked:
                    dz = jnp.where(causal, dz, 0.0)
                dzb = (dz * scale).astype(BF16)
                dq = dq + jnp.dot(dzb, ks[hd], preferred_element_type=F32)
                dk = dk + lax.dot_general(dzb, qm[hd], TN, preferred_element_type=F32)
                dv = dv + lax.dot_general(a.astype(BF16), dos[hd], TN, preferred_element_type=F32)
                new_cls.append(cls[hd] + jnp.sum(l, axis=1, keepdims=True))
                new_cgs.append(cgs[hd] + jnp.sum(g, axis=1, keepdims=True))
            dk_ref[pl.ds(off, t), :] += dk
            dv_ref[pl.ds(off, t), :] += dv
            return tuple(new_cls), tuple(new_cgs), dq

        zero = jnp.zeros((t, 1), F32)
        init = ((zero, zero), (zero, zero), jnp.zeros((t, LANES), F32))
        carry = lax.fori_loop(first, i, lambda kb, cr: step(kb, cr, False), init)
        carry = step(i, carry, True)
        dq_ref[...] = carry[2]
        finish_exchange()

    blk = pl.BlockSpec((t, LANES), lambda p, i: (i, p))
    whole = pl.BlockSpec((s, LANES), lambda p, i: (0, p))
    res = pl.pallas_call(
        body, name="attn_bwd", grid=(hp, s // t),
        in_specs=[blk, whole, whole, blk, pl.BlockSpec((None, t, LANES), lambda p, i: (p, i, 0)),
                  pl.BlockSpec(memory_space=pltpu.SMEM)] + [ANY] * nx,
        out_specs=[blk, whole, whole] + [ANY] * nx,
        out_shape=[_sds(q.shape)] * 3 + x_shape,
        scratch_shapes=x_scratch,
        compiler_params=_params(("arbitrary", "arbitrary")),
    )(q, k, v, do, ltot, n_blocks, *x_arrs)
    return res[0], res[1], res[2], res[3:]


LATE = ["w_conv_branch", "w_att_branch", "w_out", "w_ffn_up", "w_ffn_down"]


def _full_weight(name, gathered):
    return _cols_to_full(gathered) if name in COL_SHARDED else gathered.reshape(-1, gathered.shape[2])


def _grad_slabs(name, grad):
    return _full_to_cols(grad) if name in COL_SHARDED else grad.reshape(N_DEV, -1, grad.shape[1])


def _chip_sums(tag, grads):
    names = list(grads)
    mine = [_grad_slabs(nm, grads[nm]) for nm in names]
    mine = [m.reshape((4, 2) + m.shape[1:]) for m in mine]
    theirs = _exchange_call("pair_swap_" + tag, _pair_exchange(mine))
    return {nm: _pair_sum("pair_sum_" + nm, m, t) for nm, m, t in zip(names, mine, theirs)}


def _local_step(x, target, w, late_blocks):
    s = x.shape[0]
    w = dict(w)
    g1, g2, g3, g4 = w["norm_mix_pre"], w["norm_mix_post"], w["norm_ffn_pre"], w["norm_ffn_post"]

    w_in = w["w_in"]

    def proj_fn(xt, g1_, w_in_):
        h = _rms(xt, g1_).astype(BF16)
        proj = jnp.dot(h, w_in_, preferred_element_type=F32)
        return (h, *[proj[:, IN_SPLITS[n]:IN_SPLITS[n + 1]] for n in range(6)]), ()

    mix_weights = ["w_conv_branch", "w_att_branch", "w_out"]
    h1, conv_in, q, k, v, g_conv, g_att, *gathered = _rowwise(
        "norm_proj", proj_fn, [x], [g1, w_in],
        [_sds((s, D_MODEL), BF16), _sds((s, 2 * CONV_DIM)), _sds((s, ATT_DIM), BF16), _sds((s, ATT_DIM), BF16),
         _sds((s, ATT_DIM), BF16), _sds((s, D_MODEL)), _sds((s, D_MODEL))], tm=512,
        exchange=_gather_exchange([late_blocks[nm] for nm in mix_weights]))
    for nm, g in zip(mix_weights, gathered):
        w[nm] = _full_weight(nm, g)

    u3, u1, (g_down,) = _conv_fwd(conv_in, w["conv_dw_w"], w["conv_dw_b"], w["conv_ln_g"], w["conv_ln_b"],
                                  _gather_exchange([late_blocks["w_ffn_down"]]))
    w["w_ffn_down"] = _full_weight("w_ffn_down", g_down)
    att, ltot, n_blocks, (g_up,) = _attn_fwd(q, k, v, _gather_exchange([late_blocks["w_ffn_up"]]))
    w["w_ffn_up"] = _full_weight("w_ffn_up", g_up)

    def merge_fn(u3t, at, gc, ga, w_cb, w_ab, b_cb):
        cp = jnp.dot(u3t, w_cb, preferred_element_type=F32)
        ao = jnp.dot(at, w_ab, preferred_element_type=F32)
        return (_merge(cp, ao, gc, ga, b_cb), cp, ao), ()

    merged, conv_pre, att_out = _rowwise(
        "branch_merge", merge_fn, [u3, att, g_conv, g_att], [w["w_conv_branch"], w["w_att_branch"], w["b_conv_branch"]],
        [_sds((s, D_MODEL), BF16)] * 3, tm=512)

    def mid_fn(mt, xt, w_out, g2_, g3_):
        mix_ = jnp.dot(mt, w_out, preferred_element_type=F32)
        x2_ = xt + _rms(mix_, g2_)
        return (mix_, x2_, _rms(x2_, g3_)), ()

    mix, x2, h2 = _rowwise("mix_mid_norm", mid_fn, [merged, x], [w["w_out"], g2, g3],
                           [_sds((s, D_MODEL)), _sds((s, D_MODEL)), _sds((s, D_MODEL), BF16)], tm=512)

    def ffn_up_fn(ht, w_up):
        gu_ = jnp.dot(ht, w_up, preferred_element_type=F32)
        return (gu_, _swiglu(gu_[:, :D_FF], gu_[:, D_FF:])), ()

    gu, act = _rowwise("ffn_up", ffn_up_fn, [h2], [w["w_ffn_up"]],
                       [_sds((s, 2 * D_FF), BF16), _sds((s, D_FF), BF16)], tm=512)

    def final_fn(at, x2t, tgt, w_down, g4_):
        ff = jnp.dot(at, w_down, preferred_element_type=F32)
        n4, vjp = jax.vjp(_rms, ff, g4_)
        err = x2t + n4 - tgt
        dy = err * (1.0 / D_MODEL)
        dff, dg4 = vjp(dy)
        return (dy, dff), (jnp.sum(err * err, axis=0, keepdims=True), dg4)

    dy, dff, loss_cols, d_g4 = _rowwise("ffn_down_loss", final_fn, [act, x2, target], [w["w_ffn_down"], g4],
                                        [_sds((s, D_MODEL)), _sds((s, D_MODEL), BF16)],
                                        [_sds((1, D_MODEL)), _sds((1, D_MODEL))], tm=512)
    loss = 0.5 * jnp.sum(loss_cols) / D_MODEL

    d_w_down = _matmul(act, dff, ta=True, name="d_w_down", out_dtype=BF16)

    def act_bwd_fn(dfft, gut, w_down):
        d_act = lax.dot_general(dfft, w_down, NT, preferred_element_type=F32)
        gu_ = gut.astype(F32)
        _, vjp = jax.vjp(_swiglu, gu_[:, :D_FF], gu_[:, D_FF:])
        return (jnp.concatenate(vjp(d_act), axis=1),), ()

    (dgu,) = _rowwise("ffn_act_bwd", act_bwd_fn, [dff, gu], [w["w_ffn_down"]], [_sds((s, 2 * D_FF), BF16)])
    d_w_up = _matmul(h2, dgu, ta=True, name="d_w_up", out_dtype=BF16)
    received = {}
    chip_sums = _chip_sums("ffn", {"w_ffn_up": d_w_up, "w_ffn_down": d_w_down})

    def mid_bwd_fn(dgut, xt, mt, dyt, w_up, g2_, g3_):
        dh = lax.dot_general(dgut, w_up, NT, preferred_element_type=F32)
        n2, vjp2 = jax.vjp(_rms, mt, g2_)
        x2_ = xt + n2
        _, vjp3 = jax.vjp(_rms, x2_, g3_)
        dx2_, dg3 = vjp3(dh)
        dx2_ = dx2_ + dyt
        dmix_, dg2 = vjp2(dx2_)
        return (dx2_, dmix_), (dg2, dg3)

    dx2, dmix, d_g2, d_g3, received["w_ffn_down"] = _rowwise(
        "ffn_up_mid_bwd", mid_bwd_fn, [dgu, x, mix, dy], [w["w_ffn_up"], g2, g3],
        [_sds((s, D_MODEL)), _sds((s, D_MODEL), BF16)], [_sds((1, D_MODEL)), _sds((1, D_MODEL))], tm=512,
        exchange=_chip_exchange([chip_sums["w_ffn_down"]]))
    d_w_out = _matmul(merged, dmix, ta=True, name="d_w_out", out_dtype=BF16)

    def merge_bwd_fn(dmt, cp, ao, gc, ga, w_out, w_cb, w_ab, b_cb):
        dm = lax.dot_general(dmt, w_out, NT, preferred_element_type=F32)
        _, vjp = jax.vjp(_merge, cp.astype(F32), ao.astype(F32), gc, ga, b_cb)
        dcp, dao, dgc, dga, dbias = vjp(dm)
        dcp, dao = dcp.astype(BF16), dao.astype(BF16)
        du3_ = lax.dot_general(dcp, w_cb, NT, preferred_element_type=F32)
        datt_ = lax.dot_general(dao, w_ab, NT, preferred_element_type=F32)
        return (dcp, dao, dgc, dga, du3_, datt_), (dbias,)

    d_conv_out, d_att_out, d_g_conv, d_g_att, du3, d_att, d_b_cb = _rowwise(
        "merge_bwd", merge_bwd_fn, [dmix, conv_pre, att_out, g_conv, g_att],
        [w["w_out"], w["w_conv_branch"], w["w_att_branch"], w["b_conv_branch"]],
        [_sds((s, D_MODEL), BF16)] * 4 + [_sds((s, CONV_DIM)), _sds((s, ATT_DIM), BF16)], [_sds((1, D_MODEL))], tm=512)

    d_w_cb = _matmul(u3, d_conv_out, ta=True, name="d_w_conv_branch", out_dtype=BF16)
    d_w_ab = _matmul(att, d_att_out, ta=True, name="d_w_att_branch", out_dtype=BF16)

    dq, dk, dv, (received["w_ffn_up"],) = _attn_bwd(
        q, k, v, d_att, ltot, n_blocks, _chip_exchange([chip_sums["w_ffn_up"]]))

    def ln_bwd_fn(u1t, du3t, g_, b_):
        _, vjp = jax.vjp(_ln_silu, u1t, g_, b_)
        du1_, dg_, db_ = vjp(du3t)
        return (du1_,), (dg_, db_)

    du1, d_ln_g, d_ln_b = _rowwise("conv_ln_bwd", ln_bwd_fn, [u1, du3], [w["conv_ln_g"], w["conv_ln_b"]],
                                   [_sds((s, CONV_DIM))], [_sds((1, CONV_DIM)), _sds((1, CONV_DIM))])
    mix_grads = {"w_conv_branch": d_w_cb, "w_att_branch": d_w_ab, "w_out": d_w_out}
    d_conv_in, d_dw_w, d_dw_b, landed = _conv_bwd(
        conv_in, du1, w["conv_dw_w"], _scatter_exchange([_grad_slabs(nm, mix_grads[nm]) for nm in mix_weights]))
    received.update(zip(mix_weights, landed))

    d_proj = jnp.concatenate([d_conv_in, dq.astype(BF16), dk.astype(BF16), dv.astype(BF16), d_g_conv, d_g_att],
                             axis=1)
    d_w_in = _matmul(h1, d_proj, ta=True, name="d_w_in", out_dtype=BF16)

    def pre_bwd_fn(dpt, xt, dx2t, w_in_, g_):
        dh = lax.dot_general(dpt, w_in_, NT, preferred_element_type=F32)
        _, vjp = jax.vjp(_rms, xt, g_)
        dx_, dg_ = vjp(dh)
        return (dx_ + dx2t,), (dg_,)

    grad_x, d_g1, received["w_in"] = _rowwise(
        "proj_norm_bwd", pre_bwd_fn, [d_proj, x, dx2], [w_in, g1], [_sds((s, D_MODEL))], [_sds((1, D_MODEL))], tm=512,
        exchange=_chip_exchange([_chip_sums("w_in", {"w_in": d_w_in})["w_in"]]))

    grads = {
        "norm_mix_pre": d_g1, "conv_dw_w": d_dw_w, "conv_dw_b": d_dw_b,
        "conv_ln_g": d_ln_g, "conv_ln_b": d_ln_b, "b_conv_branch": d_b_cb,
        "norm_mix_post": d_g2, "norm_ffn_pre": d_g3, "norm_ffn_post": d_g4,
    }
    return loss, grad_x, received, grads


def _place():
    x, y, c = lax.axis_index("x"), lax.axis_index("y"), lax.axis_index("c")
    return x, y, c


def _slot(px, py, pc):
    return 4 * px + 2 * py + pc


def _exchange_scratch(n):
    return [pltpu.SemaphoreType.DMA((7 * n,)), pltpu.SemaphoreType.DMA((7 * n,)), pltpu.SemaphoreType.DMA((n,))]


def _gather_exchange(arrs):
    n = len(arrs)

    def phases(ins, outs, send_sems, recv_sems, local_sems):
        x, y, c = _place()
        me, sibling = (x, y, c), (x, y, 1 - c)
        chips = [(1 - x, y), (x, 1 - y), (1 - x, 1 - y)]

        def copy(a, kk, block, to, src=None):
            dst = outs[a].at[_slot(*block)]
            return pltpu.make_async_remote_copy(
                src_ref=dst if src is None else src, dst_ref=dst,
                send_sem=send_sems.at[a * 7 + kk], recv_sem=recv_sems.at[a * 7 + kk],
                device_id=to, device_id_type=MESH)

        mine = [pltpu.make_async_copy(ins[a], outs[a].at[_slot(*me)], local_sems.at[a]) for a in range(n)]
        first = []
        for a in range(n):
            first.append(copy(a, 0, me, sibling, src=ins[a]))
            first += [copy(a, 1 + j, me, (*chip, c), src=ins[a]) for j, chip in enumerate(chips)]
        passed = [copy(a, 4 + j, (*chip, c), sibling) for j, chip in enumerate(chips) for a in range(n)]

        def send():
            for cp in mine + first:
                cp.start()

        def pass_on():
            for j, chip in enumerate(chips):
                for a in range(n):
                    copy(a, 1 + j, (*chip, c), me).wait_recv()
                    passed[j * n + a].start()

        def finish():
            for a in range(n):
                copy(a, 0, sibling, me).wait_recv()
                for j, chip in enumerate(chips):
                    copy(a, 4 + j, (*chip, 1 - c), me).wait_recv()
            for cp in first + passed:
                cp.wait_send()
            for cp in mine:
                cp.wait()

        return [send, pass_on, finish]

    return list(arrs), [_sds((N_DEV,) + a.shape, a.dtype) for a in arrs], _exchange_scratch(n), phases


def _scatter_exchange(arrs):
    n = len(arrs)
    flips = [(fx, fy, fc) for fx in (0, 1) for fy in (0, 1) for fc in (0, 1)][1:]

    def phases(ins, outs, send_sems, recv_sems, local_sems):
        x, y, c = _place()
        mine = _slot(x, y, c)
        local = [pltpu.make_async_copy(ins[a].at[mine], outs[a].at[mine], local_sems.at[a]) for a in range(n)]
        peers = [((1 - x) if fx else x, (1 - y) if fy else y, (1 - c) if fc else c) for fx, fy, fc in flips]

        def copy(a, kk, src_slot, dst_slot):
            return pltpu.make_async_remote_copy(
                src_ref=ins[a].at[src_slot], dst_ref=outs[a].at[dst_slot],
                send_sem=send_sems.at[a * 7 + kk], recv_sem=recv_sems.at[a * 7 + kk],
                device_id=peers[kk], device_id_type=MESH)

        sends = [copy(a, kk, _slot(*peers[kk]), mine) for a in range(n) for kk in range(7)]

        def send():
            for cp in local + sends:
                cp.start()

        def finish():
            for a in range(n):
                for kk in range(7):
                    copy(a, kk, mine, _slot(*peers[kk])).wait_recv()
            for cp in sends:
                cp.wait_send()
            for cp in local:
                cp.wait()

        return [send, finish]

    return list(arrs), [_sds(a.shape, a.dtype) for a in arrs], _exchange_scratch(n), phases


def _pair_exchange(arrs):
    n = len(arrs)

    def phases(ins, outs, send_sems, recv_sems, local_sems):
        x, y, c = _place()

        def copy(a, chip, side):
            return pltpu.make_async_remote_copy(
                src_ref=ins[a].at[chip, side], dst_ref=outs[a].at[chip],
                send_sem=send_sems.at[a * 7 + chip], recv_sem=recv_sems.at[a * 7 + chip],
                device_id=(x, y, 1 - c), device_id_type=MESH)

        sends = [copy(a, chip, 1 - c) for a in range(n) for chip in range(4)]

        def send():
            for cp in sends:
                cp.start()

        def finish():
            for a in range(n):
                for chip in range(4):
                    copy(a, chip, c).wait_recv()
            for cp in sends:
                cp.wait_send()

        return [send, finish]

    return list(arrs), [_sds((4,) + a.shape[2:], a.dtype) for a in arrs], _exchange_scratch(n), phases


def _chip_exchange(arrs):
    n = len(arrs)

    def phases(ins, outs, send_sems, recv_sems, local_sems):
        x, y, c = _place()
        mine = 2 * x + y
        chips = [(1 - x, y), (x, 1 - y), (1 - x, 1 - y)]
        local = [pltpu.make_async_copy(ins[a].at[mine], outs[a].at[mine], local_sems.at[a]) for a in range(n)]

        def copy(a, j, src_slot, dst_slot):
            return pltpu.make_async_remote_copy(
                src_ref=ins[a].at[src_slot], dst_ref=outs[a].at[dst_slot],
                send_sem=send_sems.at[a * 7 + j], recv_sem=recv_sems.at[a * 7 + j],
                device_id=(*chips[j], c), device_id_type=MESH)

        sends = [copy(a, j, 2 * chips[j][0] + chips[j][1], mine) for a in range(n) for j in range(3)]

        def send():
            for cp in local + sends:
                cp.start()

        def finish():
            for a in range(n):
                for j in range(3):
                    copy(a, j, mine, 2 * chips[j][0] + chips[j][1]).wait_recv()
            for cp in sends:
                cp.wait_send()
            for cp in local:
                cp.wait()

        return [send, finish]

    return list(arrs), [_sds(a.shape, a.dtype) for a in arrs], _exchange_scratch(n), phases


def _pair_sum(name, mine, theirs, tr=256):
    _, _, r, c = mine.shape
    tr = _pick(r, tr, 16)

    def body(m_ref, t_ref, o_ref):
        side = lax.axis_index("c")
        o_ref[...] = (m_ref[side].astype(F32) + t_ref[...].astype(F32)).astype(o_ref.dtype)

    return pl.pallas_call(
        body, name=name, grid=(4, r // tr),
        in_specs=[pl.BlockSpec((None, 2, tr, c), lambda j, i: (j, 0, i, 0)),
                  pl.BlockSpec((None, tr, c), lambda j, i: (j, i, 0))],
        out_specs=pl.BlockSpec((None, tr, c), lambda j, i: (j, i, 0)),
        out_shape=_sds(theirs.shape, theirs.dtype),
        compiler_params=_params(("parallel", "parallel")),
    )(mine, theirs)


def _exchange_call(name, exchange):
    arrs, out_shape, scratch, phases = exchange
    n = len(arrs)

    def body(*refs):
        for step in phases(refs[:n], refs[n:2 * n], *refs[2 * n:]):
            step()

    return pl.pallas_call(body, name=name, in_specs=[ANY] * n, out_specs=[ANY] * n,
                          out_shape=out_shape, scratch_shapes=scratch)(*arrs)


def _carry_exchange(exchange, refs, n_in, n_out, first, middle, last):
    arrs, _, _, phases = exchange
    n = len(arrs)
    if n == 0:
        return lambda: None
    ins = refs[n_in:n_in + n]
    outs = refs[n_in + n + n_out:n_in + 2 * n + n_out]
    sems = n_in + 2 * n + n_out
    steps = phases(ins, outs, *refs[sems:sems + 3])
    pl.when(first)(steps[0])
    if len(steps) == 3:
        pl.when(middle)(steps[1])
    return lambda: pl.when(last)(steps[-1])


def _adamw_math(w, g, m, v):
    m2 = ADAM_B1 * m + (1.0 - ADAM_B1) * g
    v2 = ADAM_B2 * v + (1.0 - ADAM_B2) * jnp.square(g)
    m_hat = m2 / (1.0 - ADAM_B1 ** ADAM_STEP)
    v_hat = v2 / (1.0 - ADAM_B2 ** ADAM_STEP)
    delta = -ADAM_LR * (m_hat / (jnp.sqrt(v_hat) + ADAM_EPS) + ADAM_WD * w)
    return delta, m2, v2


def _sum_adamw(name, parts, w, m, v, tr=256):
    p, r, c = parts.shape
    tr = _pick(r, tr, 16)

    def body(p_ref, w_ref, m_ref, v_ref, g_ref, d_ref, m2_ref, v2_ref):
        g = p_ref[0].astype(F32)
        for d in range(1, p):
            g = g + p_ref[d].astype(F32)
        delta, m2, v2 = _adamw_math(w_ref[...], g, m_ref[...], v_ref[...])
        g_ref[...] = g
        d_ref[...] = delta
        m2_ref[...] = m2
        v2_ref[...] = v2

    tile = pl.BlockSpec((tr, c), lambda i: (i, 0))
    return pl.pallas_call(
        body, name=name, grid=(r // tr,),
        in_specs=[pl.BlockSpec((p, tr, c), lambda i: (0, i, 0)), tile, tile, tile],
        out_specs=[tile] * 4, out_shape=[_sds((r, c))] * 4,
        compiler_params=_params(("parallel",)),
    )(parts, w, m, v)


def _sum_parts(name, parts):
    p, r, c = parts.shape

    def body(p_ref, o_ref):
        g = p_ref[0]
        for d in range(1, p):
            g = g + p_ref[d]
        o_ref[...] = g

    return pl.pallas_call(
        body, name=name, out_shape=_sds((r, c)),
        in_specs=[pl.BlockSpec(memory_space=pltpu.VMEM)], out_specs=pl.BlockSpec(memory_space=pltpu.VMEM),
    )(parts)


WEIGHTS = ["norm_mix_pre", "w_in", "conv_dw_w", "conv_dw_b", "conv_ln_g", "conv_ln_b", "w_conv_branch",
           "b_conv_branch", "w_att_branch", "w_out", "norm_mix_post", "norm_ffn_pre", "w_ffn_up", "w_ffn_down",
           "norm_ffn_post"]
COL_SHARDED = ["w_in", "w_conv_branch", "w_att_branch", "w_ffn_up"]
ROW_SHARDED = ["w_out", "w_ffn_down"]
VECTORS = ["norm_mix_pre", "conv_dw_b", "conv_ln_g", "conv_ln_b", "b_conv_branch", "norm_mix_post",
           "norm_ffn_pre", "norm_ffn_post"]


def _cols_to_full(g):
    return g.transpose(1, 0, 2).reshape(g.shape[1], N_DEV * g.shape[2])


def _full_to_cols(f):
    return f.reshape(f.shape[0], N_DEV, f.shape[1] // N_DEV).transpose(1, 0, 2)


def _pack_vectors(vecs):
    rows = [jnp.pad(vecs[nm].reshape(-1), (0, D_MODEL - vecs[nm].size)) for nm in VECTORS]
    return jnp.stack(rows)


def _unpack_vectors(packed, sizes):
    return {nm: packed[n, :sizes[nm]] for n, nm in enumerate(VECTORS)}


def kernel(x, norm_mix_pre, w_in, conv_dw_w, conv_dw_b, conv_ln_g, conv_ln_b, w_conv_branch, b_conv_branch, w_att_branch, w_out, norm_mix_post, norm_ffn_pre, w_ffn_up, w_ffn_down, norm_ffn_post, loss_target, m_norm_mix_pre, m_w_in, m_conv_dw_w, m_conv_dw_b, m_conv_ln_g, m_conv_ln_b, m_w_conv_branch, m_b_conv_branch, m_w_att_branch, m_w_out, m_norm_mix_post, m_norm_ffn_pre, m_w_ffn_up, m_w_ffn_down, m_norm_ffn_post, v_norm_mix_pre, v_w_in, v_conv_dw_w, v_conv_dw_b, v_conv_ln_g, v_conv_ln_b, v_w_conv_branch, v_b_conv_branch, v_w_att_branch, v_w_out, v_norm_mix_post, v_norm_ffn_pre, v_w_ffn_up, v_w_ffn_down, v_norm_ffn_post):
    ws = dict(zip(WEIGHTS, [norm_mix_pre, w_in, conv_dw_w, conv_dw_b, conv_ln_g, conv_ln_b, w_conv_branch,
                            b_conv_branch, w_att_branch, w_out, norm_mix_post, norm_ffn_pre, w_ffn_up, w_ffn_down,
                            norm_ffn_post]))
    ms = dict(zip(WEIGHTS, [m_norm_mix_pre, m_w_in, m_conv_dw_w, m_conv_dw_b, m_conv_ln_g, m_conv_ln_b,
                            m_w_conv_branch, m_b_conv_branch, m_w_att_branch, m_w_out, m_norm_mix_post,
                            m_norm_ffn_pre, m_w_ffn_up, m_w_ffn_down, m_norm_ffn_post]))
    vs = dict(zip(WEIGHTS, [v_norm_mix_pre, v_w_in, v_conv_dw_w, v_conv_dw_b, v_conv_ln_g, v_conv_ln_b,
                            v_w_conv_branch, v_b_conv_branch, v_w_att_branch, v_w_out, v_norm_mix_post,
                            v_norm_ffn_pre, v_w_ffn_up, v_w_ffn_down, v_norm_ffn_post]))

    dw_block = jnp.pad(conv_dw_w, ((0, 1), (0, 0)))
    g_in, g_dw = _exchange_call("gather_first", _gather_exchange([w_in.astype(BF16), dw_block]))
    full = {"w_in": _full_weight("w_in", g_in), "conv_dw_w": _cols_to_full(g_dw)}
    for nm in VECTORS:
        full[nm] = ws[nm].reshape(1, -1)

    loss_local, grad_x, received, grads = _local_step(
        x[0], loss_target[0], full, {nm: ws[nm].astype(BF16) for nm in LATE})
    loss = lax.psum(loss_local, ("x", "y", "c"))

    small = _exchange_call("gather_small_grads", _gather_exchange([_pack_vectors(grads), grads["conv_dw_w"]]))
    out_g, out_d, out_m, out_v = {}, {}, {}, {}
    for nm in LATE + ["w_in"]:
        out_g[nm], out_d[nm], out_m[nm], out_v[nm] = _sum_adamw("adamw_" + nm, received[nm], ws[nm], ms[nm], vs[nm])
    sizes = {nm: ws[nm].size for nm in VECTORS}
    vec = _sum_adamw("adamw_vectors", small[0], _pack_vectors(ws), _pack_vectors(ms), _pack_vectors(vs))
    for res, dst in zip(vec, (out_g, out_d, out_m, out_v)):
        dst.update(_unpack_vectors(res, sizes))
    dw_full = _sum_parts("sum_dw_grads", small[1])
    me = _slot(*_place())
    dw_mine = lax.dynamic_slice(dw_full, (0, me * (CONV_DIM // N_DEV)), (CONV_WIDTH, CONV_DIM // N_DEV))
    nm = "conv_dw_w"
    out_g[nm], out_d[nm], out_m[nm], out_v[nm] = _sum_adamw("adamw_dw", dw_mine[None], ws[nm], ms[nm], vs[nm])

    outs = [loss, grad_x[None]]
    for group in (out_g, out_d, out_m, out_v):
        outs += [group[nm] for nm in WEIGHTS]
    return tuple(outs)
```

```python
import math

import jax
import jax.numpy as jnp
from jax import lax
from jax.experimental import pallas as pl
from jax.experimental.pallas import tpu as pltpu

F32 = jnp.float32
BF16 = jnp.bfloat16

N_DEV = 8
D_MODEL = 1024
CONV_DIM = 512
CONV_WIDTH = 31
N_HEADS = 8
HEAD_DIM = 64
ATT_DIM = N_HEADS * HEAD_DIM
D_FF = 2816
EPS = 1e-6
IN_SPLITS = (0, 1024, 1536, 2048, 2560, 3584, 4608)

ADAM_LR = 0.001
ADAM_B1 = 0.9
ADAM_B2 = 0.999
ADAM_EPS = 1e-08
ADAM_WD = 0.01
ADAM_STEP = 10

LANES = 128
SUBLANES = 8
HALO = 32
ATT_TILE = 256
DEAD_SUM = -120.0
VMEM_LIMIT = 56 * 1024 * 1024
MESH = pl.DeviceIdType.MESH
ANY = pl.BlockSpec(memory_space=pl.ANY)


def _pick(dim, target, align=LANES):
    t = min(dim, target)
    t -= t % align
    while t >= align:
        if dim % t == 0:
            return t
        t -= align
    return dim


def _params(semantics):
    return pltpu.CompilerParams(dimension_semantics=semantics, vmem_limit_bytes=VMEM_LIMIT)


def _matmul(a, b, *, name, ta=False, tb=False, out_dtype=F32):
    m, k = (a.shape[1], a.shape[0]) if ta else a.shape
    n, k2 = b.shape if tb else (b.shape[1], b.shape[0])
    assert k == k2, (a.shape, b.shape, ta, tb)
    tm, tn, tk = _pick(m, 1408 if ta else 512), _pick(n, 1536), _pick(k, 1536)
    nk = k // tk
    dims = (((0 if ta else 1,), (1 if tb else 0,)), ((), ()))

    def body(a_ref, b_ref, o_ref, *acc):
        part = lax.dot_general(a_ref[...], b_ref[...], dims, preferred_element_type=F32)
        if nk == 1:
            o_ref[...] = part.astype(o_ref.dtype)
            return
        acc_ref, = acc
        kk = pl.program_id(2)

        @pl.when(kk == 0)
        def _():
            acc_ref[...] = part

        @pl.when((kk > 0) & (kk < nk - 1))
        def _():
            acc_ref[...] += part

        @pl.when(kk == nk - 1)
        def _():
            o_ref[...] = (acc_ref[...] + part).astype(o_ref.dtype)

    a_spec = pl.BlockSpec((tk, tm), lambda j, i, kk: (kk, i)) if ta else pl.BlockSpec((tm, tk), lambda j, i, kk: (i, kk))
    b_spec = (pl.BlockSpec((tn, tk), lambda j, i, kk: (j, kk)) if tb
              else pl.BlockSpec((tk, tn), lambda j, i, kk: (kk, j)))
    return pl.pallas_call(
        body, name=name, grid=(n // tn, m // tm, nk),
        in_specs=[a_spec, b_spec],
        out_specs=pl.BlockSpec((tm, tn), lambda j, i, kk: (i, j)),
        out_shape=jax.ShapeDtypeStruct((m, n), out_dtype),
        scratch_shapes=[pltpu.VMEM((tm, tn), F32)] if nk > 1 else [],
        compiler_params=_params(("parallel", "parallel", "arbitrary")),
    )(a, b)


NO_EXCHANGE = ([], [], [], None)


def _sweep_marks(nt):
    i = pl.program_id(0)
    return i == 0, i == (3 * nt) // 4, i == nt - 1


def _rowwise(name, fn, rows, bcasts, row_outs, red_outs=(), tm=256, exchange=NO_EXCHANGE):
    s = rows[0].shape[0]
    tm = _pick(s, tm, 16)
    nt = s // tm
    resident = pl.Buffered(1)
    nr, nb, no, nd = len(rows), len(bcasts), len(row_outs), len(red_outs)
    x_arrs, x_shape, x_scratch, _ = exchange
    nx = len(x_arrs)
    first_out = nr + nb + nx

    def body(*refs):
        finish_exchange = _carry_exchange(exchange, refs, nr + nb, no + nd, *_sweep_marks(nt))
        ins = [r[...] for r in refs[:nr + nb]]
        outs, reds = fn(*ins)
        for ref, val in zip(refs[first_out:first_out + no], outs):
            ref[...] = val.astype(ref.dtype)
        i = pl.program_id(0)
        for ref, val in zip(refs[first_out + no:first_out + no + nd], reds):
            @pl.when(i == 0)
            def _():
                ref[...] = val

            @pl.when(i > 0)
            def _():
                ref[...] += val
        finish_exchange()

    in_specs = [pl.BlockSpec((tm, r.shape[1]), lambda i: (i, 0)) for r in rows]
    in_specs += [pl.BlockSpec(b.shape, lambda i: (0, 0), pipeline_mode=resident) for b in bcasts]
    out_specs = [pl.BlockSpec((tm, o.shape[1]), lambda i: (i, 0)) for o in row_outs]
    out_specs += [pl.BlockSpec(d.shape, lambda i: (0, 0)) for d in red_outs]
    return pl.pallas_call(
        body, name=name, grid=(nt,), in_specs=in_specs + [ANY] * nx, out_specs=out_specs + [ANY] * nx,
        out_shape=list(row_outs) + list(red_outs) + x_shape, scratch_shapes=x_scratch,
        compiler_params=_params(("arbitrary",)),
    )(*rows, *bcasts, *x_arrs)


def _sds(shape, dtype=F32):
    return jax.ShapeDtypeStruct(shape, dtype)


def _rms(x, g):
    y = x * lax.rsqrt(jnp.mean(x * x, axis=-1, keepdims=True) + EPS)
    return y * g


def _silu(x):
    return x * jax.nn.sigmoid(x)


def _swiglu(g, u):
    return _silu(g) * u


def _ln_silu(u, g, b):
    mu = jnp.mean(u, axis=-1, keepdims=True)
    var = jnp.mean(jnp.square(u - mu), axis=-1, keepdims=True)
    return _silu((u - mu) * lax.rsqrt(var + EPS) * g + b)


def _merge(conv_pre, att_out, g_conv, g_att, b_cb):
    return jax.nn.sigmoid(g_conv) * (conv_pre + b_cb) + jax.nn.sigmoid(g_att) * att_out


def _glu(t):
    return t[:, :CONV_DIM] * jax.nn.sigmoid(t[:, CONV_DIM:])


def _conv_fwd(conv_in, w_pad, b, ln_g, ln_b, exchange, tm=256):
    s = conv_in.shape[0]
    tm = _pick(s, tm, HALO)
    ratio = tm // HALO
    x_arrs, x_shape, x_scratch, _ = exchange
    nx = len(x_arrs)

    def body(*refs):
        main_ref, halo_ref, w_ref, b_ref, g_ref, be_ref = refs[:6]
        u3_ref, u1_ref = refs[6 + nx:8 + nx]
        buf = refs[-1]
        finish_exchange = _carry_exchange(exchange, refs, 6, 2, *_sweep_marks(s // tm))
        i = pl.program_id(0)
        buf[0:HALO, :] = _glu(halo_ref[...]) * (i > 0).astype(F32)
        buf[HALO:HALO + tm, :] = _glu(main_ref[...])
        acc = jnp.zeros((tm, CONV_DIM), F32) + b_ref[...]
        for j in range(CONV_WIDTH):
            acc = acc + w_ref[j:j + 1, :] * buf[pl.ds(HALO - (CONV_WIDTH - 1) + j, tm), :]
        u1_ref[...] = acc
        u3_ref[...] = _ln_silu(acc, g_ref[...], be_ref[...]).astype(u3_ref.dtype)
        finish_exchange()

    res = pl.pallas_call(
        body, name="conv_fwd", grid=(s // tm,),
        in_specs=[pl.BlockSpec((tm, 2 * CONV_DIM), lambda i: (i, 0)),
                  pl.BlockSpec((HALO, 2 * CONV_DIM), lambda i: (jnp.maximum(i * ratio - 1, 0), 0)),
                  pl.BlockSpec(w_pad.shape, lambda i: (0, 0)),
                  pl.BlockSpec(b.shape, lambda i: (0, 0)),
                  pl.BlockSpec(ln_g.shape, lambda i: (0, 0)),
                  pl.BlockSpec(ln_b.shape, lambda i: (0, 0))] + [ANY] * nx,
        out_specs=[pl.BlockSpec((tm, CONV_DIM), lambda i: (i, 0)),
                   pl.BlockSpec((tm, CONV_DIM), lambda i: (i, 0))] + [ANY] * nx,
        out_shape=[_sds((s, CONV_DIM), BF16), _sds((s, CONV_DIM), F32)] + x_shape,
        scratch_shapes=x_scratch + [pltpu.VMEM((tm + HALO, CONV_DIM), F32)],
        compiler_params=_params(("arbitrary",)),
    )(conv_in, conv_in, w_pad, b, ln_g, ln_b, *x_arrs)
    return res[0], res[1], res[2:]


def _conv_bwd(conv_in, du1, w_pad, exchange, tm=256):
    s = conv_in.shape[0]
    tm = _pick(s, tm, HALO)
    ratio = tm // HALO
    nt = s // tm
    last_halo = s // HALO - 1
    x_arrs, x_shape, x_scratch, _ = exchange
    nx = len(x_arrs)

    def body(*refs):
        main_ref, halo_ref, du_ref, dun_ref, w_ref = refs[:5]
        dci_ref, dw_ref, db_ref = refs[5 + nx:8 + nx]
        ubuf, dbuf = refs[-2:]
        finish_exchange = _carry_exchange(exchange, refs, 5, 3, *_sweep_marks(nt))
        i = pl.program_id(0)
        main = main_ref[...]
        a = main[:, :CONV_DIM]
        sb = jax.nn.sigmoid(main[:, CONV_DIM:])
        ubuf[0:HALO, :] = _glu(halo_ref[...]) * (i > 0).astype(F32)
        ubuf[HALO:HALO + tm, :] = a * sb
        du = du_ref[...]
        dbuf[0:tm, :] = du
        dbuf[tm:tm + HALO, :] = dun_ref[...] * (i < nt - 1).astype(F32)

        @pl.when(i == 0)
        def _():
            dw_ref[...] = jnp.zeros_like(dw_ref)
            db_ref[...] = jnp.zeros_like(db_ref)

        du0 = jnp.zeros((tm, CONV_DIM), F32)
        for j in range(CONV_WIDTH):
            du0 = du0 + w_ref[j:j + 1, :] * dbuf[pl.ds(CONV_WIDTH - 1 - j, tm), :]
            dw_ref[j:j + 1, :] += jnp.sum(du * ubuf[pl.ds(HALO - (CONV_WIDTH - 1) + j, tm), :], axis=0, keepdims=True)
        db_ref[...] += jnp.sum(du, axis=0, keepdims=True)
        dci_ref[:, :CONV_DIM] = (du0 * sb).astype(dci_ref.dtype)
        dci_ref[:, CONV_DIM:] = (du0 * a * sb * (1.0 - sb)).astype(dci_ref.dtype)
        finish_exchange()

    res = pl.pallas_call(
        body, name="conv_bwd", grid=(nt,),
        in_specs=[pl.BlockSpec((tm, 2 * CONV_DIM), lambda i: (i, 0)),
                  pl.BlockSpec((HALO, 2 * CONV_DIM), lambda i: (jnp.maximum(i * ratio - 1, 0), 0)),
                  pl.BlockSpec((tm, CONV_DIM), lambda i: (i, 0)),
                  pl.BlockSpec((HALO, CONV_DIM), lambda i: (jnp.minimum((i + 1) * ratio, last_halo), 0)),
                  pl.BlockSpec(w_pad.shape, lambda i: (0, 0))] + [ANY] * nx,
        out_specs=[pl.BlockSpec((tm, 2 * CONV_DIM), lambda i: (i, 0)),
                   pl.BlockSpec(w_pad.shape, lambda i: (0, 0)),
                   pl.BlockSpec((1, CONV_DIM), lambda i: (0, 0))] + [ANY] * nx,
        out_shape=[_sds((s, 2 * CONV_DIM), BF16), _sds(w_pad.shape), _sds((1, CONV_DIM))] + x_shape,
        scratch_shapes=x_scratch + [pltpu.VMEM((tm + HALO, CONV_DIM), F32), pltpu.VMEM((tm + HALO, CONV_DIM), F32)],
        compiler_params=_params(("arbitrary",)),
    )(conv_in, conv_in, du1, du1, w_pad, *x_arrs)
    return res[0], res[1], res[2], res[3:]


def _logsig_neg(z):
    return jnp.minimum(-z, 0.0) - jnp.log(1.0 + jnp.exp(-jnp.abs(z)))


def _split_dot(val, tri):
    hi = val.astype(BF16)
    lo = (val - hi.astype(F32)).astype(BF16)
    return jnp.dot(hi, tri, preferred_element_type=F32) + jnp.dot(lo, tri, preferred_element_type=F32)


def _attn_masks(t, later):
    row = lax.broadcasted_iota(jnp.int32, (t, t), 0)
    col = lax.broadcasted_iota(jnp.int32, (t, t), 1)
    tri = jnp.where(row > col if later else row <= col, 1.0, 0.0).astype(BF16)
    return col < row, tri


def _grid_marks(h, nq):
    hh, i = pl.program_id(0), pl.program_id(1)
    return (hh == 0) & (i == 0), (hh == (3 * h) // 4) & (i == 0), (hh == h - 1) & (i == nq - 1)


def _head_masks(shape):
    lane = lax.broadcasted_iota(jnp.int32, shape, len(shape) - 1)
    return lane < HEAD_DIM, lane >= HEAD_DIM


def _per_head(blk):
    m0, m1 = _head_masks(blk.shape)
    zero = jnp.zeros_like(blk)
    return jnp.where(m0, blk, zero), jnp.where(m1, blk, zero)


NT = (((1,), (1,)), ((), ()))
TN = (((0,), (0,)), ((), ()))


def _attn_fwd(q, k, v, exchange):
    s = q.shape[0]
    hp = q.shape[1] // LANES
    t = ATT_TILE
    scale = 1.0 / math.sqrt(HEAD_DIM)
    x_arrs, x_shape, x_scratch, _ = exchange
    nx = len(x_arrs)

    def body(*refs):
        q_ref, k_ref, v_ref = refs[:3]
        o_ref, lt_ref, nb_ref = refs[3 + nx:6 + nx]
        finish_exchange = _carry_exchange(exchange, refs, 3, 3, *_grid_marks(hp, s // t))
        i = pl.program_id(1)
        qs = _per_head((q_ref[...].astype(F32) * scale).astype(BF16))
        causal, tri = _attn_masks(t, later=True)

        def step(kb, carry, masked):
            cs, acc = carry
            off = pl.multiple_of(kb * t, t)
            kblk = k_ref[pl.ds(off, t), :]
            vs = _per_head(v_ref[pl.ds(off, t), :])
            new_cs = []
            for hd in range(2):
                z = lax.dot_general(qs[hd], kblk, NT, preferred_element_type=F32)
                l = _logsig_neg(z)
                if masked:
                    l = jnp.where(causal, l, 0.0)
                e = z + l + _split_dot(l, tri) + cs[hd]
                if masked:
                    e = jnp.where(causal, e, -1e30)
                acc = acc + jnp.dot(jnp.exp(e).astype(BF16), vs[hd], preferred_element_type=F32)
                new_cs.append(cs[hd] + jnp.sum(l, axis=1, keepdims=True))
            return tuple(new_cs), acc

        zero = jnp.zeros((t, 1), F32)
        carry = step(i, ((zero, zero), jnp.zeros((t, LANES), F32)), True)

        def more(state):
            n, (cs, _) = state
            return (n < i) & (jnp.maximum(jnp.max(cs[0]), jnp.max(cs[1])) > DEAD_SUM)

        n_blocks, carry = lax.while_loop(more, lambda st: (st[0] + 1, step(i - 1 - st[0], st[1], False)),
                                         (jnp.int32(0), carry))
        m0, _ = _head_masks((t, LANES))
        lt_ref[...] = jnp.where(m0, carry[0][0], carry[0][1])
        o_ref[...] = carry[1].astype(o_ref.dtype)
        nb_ref[pl.program_id(0), i] = n_blocks.astype(F32)
        finish_exchange()

    res = pl.pallas_call(
        body, name="attn_fwd", grid=(hp, s // t),
        in_specs=[pl.BlockSpec((t, LANES), lambda p, i: (i, p)),
                  pl.BlockSpec((s, LANES), lambda p, i: (0, p)),
                  pl.BlockSpec((s, LANES), lambda p, i: (0, p))] + [ANY] * nx,
        out_specs=[pl.BlockSpec((t, LANES), lambda p, i: (i, p)),
                   pl.BlockSpec((None, t, LANES), lambda p, i: (p, i, 0)),
                   pl.BlockSpec(memory_space=pltpu.SMEM)] + [ANY] * nx,
        out_shape=[_sds(q.shape, BF16), _sds((hp, s, LANES), F32), _sds((hp, s // t), F32)] + x_shape,
        scratch_shapes=x_scratch,
        compiler_params=_params(("arbitrary", "arbitrary")),
    )(q, k, v, *x_arrs)
    return res[0], res[1], res[2], res[3:]


def _attn_bwd(q, k, v, do, ltot, n_blocks, exchange):
    s = q.shape[0]
    hp = q.shape[1] // LANES
    t = ATT_TILE
    scale = 1.0 / math.sqrt(HEAD_DIM)
    x_arrs, x_shape, x_scratch, _ = exchange
    nx = len(x_arrs)

    def body(*refs):
        q_ref, k_ref, v_ref, do_ref, lt_ref, nb_ref = refs[:6]
        dq_ref, dk_ref, dv_ref = refs[6 + nx:9 + nx]
        finish_exchange = _carry_exchange(exchange, refs, 6, 3, *_grid_marks(hp, s // t))
        i = pl.program_id(1)
        first = jnp.clip(i - nb_ref[pl.program_id(0), i].astype(jnp.int32), 0, i)

        @pl.when(i == 0)
        def _():
            dk_ref[...] = jnp.zeros_like(dk_ref)
            dv_ref[...] = jnp.zeros_like(dv_ref)

        qb = q_ref[...]
        qm = _per_head(qb)
        qs = _per_head((qb.astype(F32) * scale).astype(BF16))
        dos = _per_head(do_ref[...])
        lts = (lt_ref[:, 0:1], lt_ref[:, HEAD_DIM:HEAD_DIM + 1])
        causal, tri = _attn_masks(t, later=False)

        def step(kb, carry, masked):
            cls, cgs, dq = carry
            off = pl.multiple_of(kb * t, t)
            kblk = k_ref[pl.ds(off, t), :]
            vblk = v_ref[pl.ds(off, t), :]
            ks = _per_head(kblk)
            dk = jnp.zeros((t, LANES), F32)
            dv = jnp.zeros((t, LANES), F32)
            new_cls, new_cgs = [], []
            for hd in range(2):
                z = lax.dot_general(qs[hd], kblk, NT, preferred_element_type=F32)
                l = _logsig_neg(z)
                if masked:
                    l = jnp.where(causal, l, 0.0)
                e = z + l + ((lts[hd] - cls[hd]) - _split_dot(l, tri))
                if masked:
                    e = jnp.where(causal, e, -1e30)
                a = jnp.exp(e)
                g = lax.dot_general(dos[hd], vblk, NT, preferred_element_type=F32) * a
                p = cgs[hd] + jnp.dot(g.astype(BF16), tri, preferred_element_type=F32) - g
                el = jnp.exp(l)
                dz = g * el - p * (1.0 - el)
                if masked:
                    dz = jnp.where(causal, dz, 0.0)
                dzb = (dz * scale).astype(BF16)
                dq = dq + jnp.dot(dzb, ks[hd], preferred_element_type=F32)
                dk = dk + lax.dot_general(dzb, qm[hd], TN, preferred_element_type=F32)
                dv = dv + lax.dot_general(a.astype(BF16), dos[hd], TN, preferred_element_type=F32)
                new_cls.append(cls[hd] + jnp.sum(l, axis=1, keepdims=True))
                new_cgs.append(cgs[hd] + jnp.sum(g, axis=1, keepdims=True))
            dk_ref[pl.ds(off, t), :] += dk
            dv_ref[pl.ds(off, t), :] += dv
            return tuple(new_cls), tuple(new_cgs), dq

        zero = jnp.zeros((t, 1), F32)
        init = ((zero, zero), (zero, zero), jnp.zeros((t, LANES), F32))
        carry = lax.fori_loop(first, i, lambda kb, cr: step(kb, cr, False), init)
        carry = step(i, carry, True)
        dq_ref[...] = carry[2]
        finish_exchange()

    blk = pl.BlockSpec((t, LANES), lambda p, i: (i, p))
    whole = pl.BlockSpec((s, LANES), lambda p, i: (0, p))
    res = pl.pallas_call(
        body, name="attn_bwd", grid=(hp, s // t),
        in_specs=[blk, whole, whole, blk, pl.BlockSpec((None, t, LANES), lambda p, i: (p, i, 0)),
                  pl.BlockSpec(memory_space=pltpu.SMEM)] + [ANY] * nx,
        out_specs=[blk, whole, whole] + [ANY] * nx,
        out_shape=[_sds(q.shape)] * 3 + x_shape,
        scratch_shapes=x_scratch,
        compiler_params=_params(("arbitrary", "arbitrary")),
    )(q, k, v, do, ltot, n_blocks, *x_arrs)
    return res[0], res[1], res[2], res[3:]


LATE = ["w_conv_branch", "w_att_branch", "w_out", "w_ffn_up", "w_ffn_down"]


def _full_weight(name, gathered):
    return _cols_to_full(gathered) if name in COL_SHARDED else gathered.reshape(-1, gathered.shape[2])


def _grad_slabs(name, grad):
    return _full_to_cols(grad) if name in COL_SHARDED else grad.reshape(N_DEV, -1, grad.shape[1])


def _side_slabs(name, grad):
    slabs = _grad_slabs(name, grad)
    return slabs.reshape((4, 2) + slabs.shape[1:])


def _local_step(x, target, w, late_blocks):
    s = x.shape[0]
    w = dict(w)
    g1, g2, g3, g4 = w["norm_mix_pre"], w["norm_mix_post"], w["norm_ffn_pre"], w["norm_ffn_post"]

    w_in = w["w_in"]

    def proj_fn(xt, g1_, w_in_):
        h = _rms(xt, g1_).astype(BF16)
        proj = jnp.dot(h, w_in_, preferred_element_type=F32)
        return (h, *[proj[:, IN_SPLITS[n]:IN_SPLITS[n + 1]] for n in range(6)]), ()

    mix_weights = ["w_conv_branch", "w_att_branch", "w_out"]
    h1, conv_in, q, k, v, g_conv, g_att, *gathered = _rowwise(
        "norm_proj", proj_fn, [x], [g1, w_in],
        [_sds((s, D_MODEL), BF16), _sds((s, 2 * CONV_DIM)), _sds((s, ATT_DIM), BF16), _sds((s, ATT_DIM), BF16),
         _sds((s, ATT_DIM), BF16), _sds((s, D_MODEL)), _sds((s, D_MODEL))], tm=512,
        exchange=_gather_exchange([late_blocks[nm] for nm in mix_weights]))
    for nm, g in zip(mix_weights, gathered):
        w[nm] = _full_weight(nm, g)

    u3, u1, (g_down,) = _conv_fwd(conv_in, w["conv_dw_w"], w["conv_dw_b"], w["conv_ln_g"], w["conv_ln_b"],
                                  _gather_exchange([late_blocks["w_ffn_down"]]))
    w["w_ffn_down"] = _full_weight("w_ffn_down", g_down)
    att, ltot, n_blocks, (g_up,) = _attn_fwd(q, k, v, _gather_exchange([late_blocks["w_ffn_up"]]))
    w["w_ffn_up"] = _full_weight("w_ffn_up", g_up)

    def merge_fn(u3t, at, gc, ga, w_cb, w_ab, b_cb):
        cp = jnp.dot(u3t, w_cb, preferred_element_type=F32)
        ao = jnp.dot(at, w_ab, preferred_element_type=F32)
        return (_merge(cp, ao, gc, ga, b_cb), cp, ao), ()

    merged, conv_pre, att_out = _rowwise(
        "branch_merge", merge_fn, [u3, att, g_conv, g_att], [w["w_conv_branch"], w["w_att_branch"], w["b_conv_branch"]],
        [_sds((s, D_MODEL), BF16)] * 3, tm=512)

    def mid_fn(mt, xt, w_out, g2_, g3_):
        mix_ = jnp.dot(mt, w_out, preferred_element_type=F32)
        x2_ = xt + _rms(mix_, g2_)
        return (mix_, x2_, _rms(x2_, g3_)), ()

    mix, x2, h2 = _rowwise("mix_mid_norm", mid_fn, [merged, x], [w["w_out"], g2, g3],
                           [_sds((s, D_MODEL)), _sds((s, D_MODEL)), _sds((s, D_MODEL), BF16)], tm=512)

    def ffn_up_fn(ht, w_up):
        gu_ = jnp.dot(ht, w_up, preferred_element_type=F32)
        return (gu_, _swiglu(gu_[:, :D_FF], gu_[:, D_FF:])), ()

    gu, act = _rowwise("ffn_up", ffn_up_fn, [h2], [w["w_ffn_up"]],
                       [_sds((s, 2 * D_FF), BF16), _sds((s, D_FF), BF16)], tm=512)

    def final_fn(at, x2t, tgt, w_down, g4_):
        ff = jnp.dot(at, w_down, preferred_element_type=F32)
        n4, vjp = jax.vjp(_rms, ff, g4_)
        err = x2t + n4 - tgt
        dy = err * (1.0 / D_MODEL)
        dff, dg4 = vjp(dy)
        return (dy, dff), (jnp.sum(err * err, axis=0, keepdims=True), dg4)

    dy, dff, loss_cols, d_g4 = _rowwise("ffn_down_loss", final_fn, [act, x2, target], [w["w_ffn_down"], g4],
                                        [_sds((s, D_MODEL)), _sds((s, D_MODEL), BF16)],
                                        [_sds((1, D_MODEL)), _sds((1, D_MODEL))], tm=512)
    loss = 0.5 * jnp.sum(loss_cols) / D_MODEL

    d_w_down = _matmul(act, dff, ta=True, name="d_w_down", out_dtype=BF16)

    def act_bwd_fn(dfft, gut, w_down):
        d_act = lax.dot_general(dfft, w_down, NT, preferred_element_type=F32)
        gu_ = gut.astype(F32)
        _, vjp = jax.vjp(_swiglu, gu_[:, :D_FF], gu_[:, D_FF:])
        return (jnp.concatenate(vjp(d_act), axis=1),), ()

    down_slabs = _side_slabs("w_ffn_down", d_w_down)
    dgu, theirs = _rowwise("ffn_act_bwd", act_bwd_fn, [dff, gu], [w["w_ffn_down"]], [_sds((s, 2 * D_FF), BF16)],
                           exchange=_pair_exchange([down_slabs]))
    down_sums = _pair_sum("pair_sum_w_ffn_down", down_slabs, theirs)
    d_w_up = _matmul(h2, dgu, ta=True, name="d_w_up", out_dtype=BF16)
    received = {}
    up_slabs = _side_slabs("w_ffn_up", d_w_up)

    def mid_bwd_fn(dgut, xt, mt, dyt, w_up, g2_, g3_):
        dh = lax.dot_general(dgut, w_up, NT, preferred_element_type=F32)
        n2, vjp2 = jax.vjp(_rms, mt, g2_)
        x2_ = xt + n2
        _, vjp3 = jax.vjp(_rms, x2_, g3_)
        dx2_, dg3 = vjp3(dh)
        dx2_ = dx2_ + dyt
        dmix_, dg2 = vjp2(dx2_)
        return (dx2_, dmix_), (dg2, dg3)

    dx2, dmix, d_g2, d_g3, received["w_ffn_down"] = _rowwise(
        "ffn_up_mid_bwd", mid_bwd_fn, [dgu, x, mix, dy], [w["w_ffn_up"], g2, g3],
        [_sds((s, D_MODEL)), _sds((s, D_MODEL), BF16)], [_sds((1, D_MODEL)), _sds((1, D_MODEL))], tm=512,
        exchange=_chip_exchange([down_sums]))
    d_w_out = _matmul(merged, dmix, ta=True, name="d_w_out", out_dtype=BF16)

    def merge_bwd_fn(dmt, cp, ao, gc, ga, w_out, w_cb, w_ab, b_cb):
        dm = lax.dot_general(dmt, w_out, NT, preferred_element_type=F32)
        _, vjp = jax.vjp(_merge, cp.astype(F32), ao.astype(F32), gc, ga, b_cb)
        dcp, dao, dgc, dga, dbias = vjp(dm)
        dcp, dao = dcp.astype(BF16), dao.astype(BF16)
        du3_ = lax.dot_general(dcp, w_cb, NT, preferred_element_type=F32)
        datt_ = lax.dot_general(dao, w_ab, NT, preferred_element_type=F32)
        return (dcp, dao, dgc, dga, du3_, datt_), (dbias,)

    d_conv_out, d_att_out, d_g_conv, d_g_att, du3, d_att, d_b_cb, theirs = _rowwise(
        "merge_bwd", merge_bwd_fn, [dmix, conv_pre, att_out, g_conv, g_att],
        [w["w_out"], w["w_conv_branch"], w["w_att_branch"], w["b_conv_branch"]],
        [_sds((s, D_MODEL), BF16)] * 4 + [_sds((s, CONV_DIM)), _sds((s, ATT_DIM), BF16)], [_sds((1, D_MODEL))], tm=512,
        exchange=_pair_exchange([up_slabs]))

    d_w_cb = _matmul(u3, d_conv_out, ta=True, name="d_w_conv_branch", out_dtype=BF16)
    d_w_ab = _matmul(att, d_att_out, ta=True, name="d_w_att_branch", out_dtype=BF16)

    dq, dk, dv, (received["w_ffn_up"],) = _attn_bwd(
        q, k, v, d_att, ltot, n_blocks, _chip_exchange([_pair_sum("pair_sum_w_ffn_up", up_slabs, theirs)]))

    def ln_bwd_fn(u1t, du3t, g_, b_):
        _, vjp = jax.vjp(_ln_silu, u1t, g_, b_)
        du1_, dg_, db_ = vjp(du3t)
        return (du1_,), (dg_, db_)

    du1, d_ln_g, d_ln_b = _rowwise("conv_ln_bwd", ln_bwd_fn, [u1, du3], [w["conv_ln_g"], w["conv_ln_b"]],
                                   [_sds((s, CONV_DIM))], [_sds((1, CONV_DIM)), _sds((1, CONV_DIM))])
    mix_grads = {"w_conv_branch": d_w_cb, "w_att_branch": d_w_ab, "w_out": d_w_out}
    d_conv_in, d_dw_w, d_dw_b, landed = _conv_bwd(
        conv_in, du1, w["conv_dw_w"], _scatter_exchange([_grad_slabs(nm, mix_grads[nm]) for nm in mix_weights]))
    received.update(zip(mix_weights, landed))

    d_proj = jnp.concatenate([d_conv_in, dq.astype(BF16), dk.astype(BF16), dv.astype(BF16), d_g_conv, d_g_att],
                             axis=1)
    d_w_in = _matmul(h1, d_proj, ta=True, name="d_w_in", out_dtype=BF16)
    in_slabs = _side_slabs("w_in", d_w_in)
    (theirs,) = _exchange_call("pair_swap_w_in", _pair_exchange([in_slabs]))

    def pre_bwd_fn(dpt, xt, dx2t, w_in_, g_):
        dh = lax.dot_general(dpt, w_in_, NT, preferred_element_type=F32)
        _, vjp = jax.vjp(_rms, xt, g_)
        dx_, dg_ = vjp(dh)
        return (dx_ + dx2t,), (dg_,)

    grad_x, d_g1, received["w_in"] = _rowwise(
        "proj_norm_bwd", pre_bwd_fn, [d_proj, x, dx2], [w_in, g1], [_sds((s, D_MODEL))], [_sds((1, D_MODEL))], tm=512,
        exchange=_chip_exchange([_pair_sum("pair_sum_w_in", in_slabs, theirs)]))

    grads = {
        "norm_mix_pre": d_g1, "conv_dw_w": d_dw_w, "conv_dw_b": d_dw_b,
        "conv_ln_g": d_ln_g, "conv_ln_b": d_ln_b, "b_conv_branch": d_b_cb,
        "norm_mix_post": d_g2, "norm_ffn_pre": d_g3, "norm_ffn_post": d_g4,
    }
    return loss, grad_x, received, grads


def _place():
    x, y, c = lax.axis_index("x"), lax.axis_index("y"), lax.axis_index("c")
    return x, y, c


def _slot(px, py, pc):
    return 4 * px + 2 * py + pc


def _exchange_scratch(n):
    return [pltpu.SemaphoreType.DMA((7 * n,)), pltpu.SemaphoreType.DMA((7 * n,)), pltpu.SemaphoreType.DMA((n,))]


def _gather_exchange(arrs):
    n = len(arrs)

    def phases(ins, outs, send_sems, recv_sems, local_sems):
        x, y, c = _place()
        me, sibling = (x, y, c), (x, y, 1 - c)
        chips = [(1 - x, y), (x, 1 - y), (1 - x, 1 - y)]

        def copy(a, kk, block, to, src=None):
            dst = outs[a].at[_slot(*block)]
            return pltpu.make_async_remote_copy(
                src_ref=dst if src is None else src, dst_ref=dst,
                send_sem=send_sems.at[a * 7 + kk], recv_sem=recv_sems.at[a * 7 + kk],
                device_id=to, device_id_type=MESH)

        mine = [pltpu.make_async_copy(ins[a], outs[a].at[_slot(*me)], local_sems.at[a]) for a in range(n)]
        first = []
        for a in range(n):
            first.append(copy(a, 0, me, sibling, src=ins[a]))
            first += [copy(a, 1 + j, me, (*chip, c), src=ins[a]) for j, chip in enumerate(chips)]
        passed = [copy(a, 4 + j, (*chip, c), sibling) for j, chip in enumerate(chips) for a in range(n)]

        def send():
            for cp in mine + first:
                cp.start()

        def pass_on():
            for j, chip in enumerate(chips):
                for a in range(n):
                    copy(a, 1 + j, (*chip, c), me).wait_recv()
                    passed[j * n + a].start()

        def finish():
            for a in range(n):
                copy(a, 0, sibling, me).wait_recv()
                for j, chip in enumerate(chips):
                    copy(a, 4 + j, (*chip, 1 - c), me).wait_recv()
            for cp in first + passed:
                cp.wait_send()
            for cp in mine:
                cp.wait()

        return [send, pass_on, finish]

    return list(arrs), [_sds((N_DEV,) + a.shape, a.dtype) for a in arrs], _exchange_scratch(n), phases


def _scatter_exchange(arrs):
    n = len(arrs)
    flips = [(fx, fy, fc) for fx in (0, 1) for fy in (0, 1) for fc in (0, 1)][1:]

    def phases(ins, outs, send_sems, recv_sems, local_sems):
        x, y, c = _place()
        mine = _slot(x, y, c)
        local = [pltpu.make_async_copy(ins[a].at[mine], outs[a].at[mine], local_sems.at[a]) for a in range(n)]
        peers = [((1 - x) if fx else x, (1 - y) if fy else y, (1 - c) if fc else c) for fx, fy, fc in flips]

        def copy(a, kk, src_slot, dst_slot):
            return pltpu.make_async_remote_copy(
                src_ref=ins[a].at[src_slot], dst_ref=outs[a].at[dst_slot],
                send_sem=send_sems.at[a * 7 + kk], recv_sem=recv_sems.at[a * 7 + kk],
                device_id=peers[kk], device_id_type=MESH)

        sends = [copy(a, kk, _slot(*peers[kk]), mine) for a in range(n) for kk in range(7)]

        def send():
            for cp in local + sends:
                cp.start()

        def finish():
            for a in range(n):
                for kk in range(7):
                    copy(a, kk, mine, _slot(*peers[kk])).wait_recv()
            for cp in sends:
                cp.wait_send()
            for cp in local:
                cp.wait()

        return [send, finish]

    return list(arrs), [_sds(a.shape, a.dtype) for a in arrs], _exchange_scratch(n), phases


def _pair_exchange(arrs):
    n = len(arrs)

    def phases(ins, outs, send_sems, recv_sems, local_sems):
        x, y, c = _place()

        def copy(a, chip, side):
            return pltpu.make_async_remote_copy(
                src_ref=ins[a].at[chip, side], dst_ref=outs[a].at[chip],
                send_sem=send_sems.at[a * 7 + chip], recv_sem=recv_sems.at[a * 7 + chip],
                device_id=(x, y, 1 - c), device_id_type=MESH)

        sends = [copy(a, chip, 1 - c) for a in range(n) for chip in range(4)]

        def send():
            for cp in sends:
                cp.start()

        def finish():
            for a in range(n):
                for chip in range(4):
                    copy(a, chip, c).wait_recv()
            for cp in sends:
                cp.wait_send()

        return [send, finish]

    return list(arrs), [_sds((4,) + a.shape[2:], a.dtype) for a in arrs], _exchange_scratch(n), phases


def _chip_exchange(arrs):
    n = len(arrs)

    def phases(ins, outs, send_sems, recv_sems, local_sems):
        x, y, c = _place()
        mine = 2 * x + y
        chips = [(1 - x, y), (x, 1 - y), (1 - x, 1 - y)]
        local = [pltpu.make_async_copy(ins[a].at[mine], outs[a].at[mine], local_sems.at[a]) for a in range(n)]

        def copy(a, j, src_slot, dst_slot):
            return pltpu.make_async_remote_copy(
                src_ref=ins[a].at[src_slot], dst_ref=outs[a].at[dst_slot],
                send_sem=send_sems.at[a * 7 + j], recv_sem=recv_sems.at[a * 7 + j],
                device_id=(*chips[j], c), device_id_type=MESH)

        sends = [copy(a, j, 2 * chips[j][0] + chips[j][1], mine) for a in range(n) for j in range(3)]

        def send():
            for cp in local + sends:
                cp.start()

        def finish():
            for a in range(n):
                for j in range(3):
                    copy(a, j, mine, 2 * chips[j][0] + chips[j][1]).wait_recv()
            for cp in sends:
                cp.wait_send()
            for cp in local:
                cp.wait()

        return [send, finish]

    return list(arrs), [_sds(a.shape, a.dtype) for a in arrs], _exchange_scratch(n), phases


def _pair_sum(name, mine, theirs):
    _, _, r, c = mine.shape

    def body(m_ref, t_ref, o_ref):
        o_ref[...] = (m_ref[...].astype(F32) + t_ref[...].astype(F32)).astype(o_ref.dtype)

    return pl.pallas_call(
        body, name=name, grid=(4,),
        in_specs=[pl.BlockSpec((None, None, r, c), lambda j: (j, lax.axis_index("c"), 0, 0)),
                  pl.BlockSpec((None, r, c), lambda j: (j, 0, 0))],
        out_specs=pl.BlockSpec((None, r, c), lambda j: (j, 0, 0)),
        out_shape=_sds(theirs.shape, theirs.dtype),
        compiler_params=_params(("parallel",)),
    )(mine, theirs)


def _exchange_call(name, exchange):
    arrs, out_shape, scratch, phases = exchange
    n = len(arrs)

    def body(*refs):
        for step in phases(refs[:n], refs[n:2 * n], *refs[2 * n:]):
            step()

    return pl.pallas_call(body, name=name, in_specs=[ANY] * n, out_specs=[ANY] * n,
                          out_shape=out_shape, scratch_shapes=scratch)(*arrs)


def _carry_exchange(exchange, refs, n_in, n_out, first, middle, last):
    arrs, _, _, phases = exchange
    n = len(arrs)
    if n == 0:
        return lambda: None
    ins = refs[n_in:n_in + n]
    outs = refs[n_in + n + n_out:n_in + 2 * n + n_out]
    sems = n_in + 2 * n + n_out
    steps = phases(ins, outs, *refs[sems:sems + 3])
    pl.when(first)(steps[0])
    if len(steps) == 3:
        pl.when(middle)(steps[1])
    return lambda: pl.when(last)(steps[-1])


def _adamw_math(w, g, m, v):
    m2 = ADAM_B1 * m + (1.0 - ADAM_B1) * g
    v2 = ADAM_B2 * v + (1.0 - ADAM_B2) * jnp.square(g)
    m_hat = m2 / (1.0 - ADAM_B1 ** ADAM_STEP)
    v_hat = v2 / (1.0 - ADAM_B2 ** ADAM_STEP)
    delta = -ADAM_LR * (m_hat / (jnp.sqrt(v_hat) + ADAM_EPS) + ADAM_WD * w)
    return delta, m2, v2


def _sum_adamw(name, parts, w, m, v, tr=256):
    p, r, c = parts.shape
    tr = _pick(r, tr, 16)

    def body(p_ref, w_ref, m_ref, v_ref, g_ref, d_ref, m2_ref, v2_ref):
        g = p_ref[0].astype(F32)
        for d in range(1, p):
            g = g + p_ref[d].astype(F32)
        delta, m2, v2 = _adamw_math(w_ref[...], g, m_ref[...], v_ref[...])
        g_ref[...] = g
        d_ref[...] = delta
        m2_ref[...] = m2
        v2_ref[...] = v2

    tile = pl.BlockSpec((tr, c), lambda i: (i, 0))
    return pl.pallas_call(
        body, name=name, grid=(r // tr,),
        in_specs=[pl.BlockSpec((p, tr, c), lambda i: (0, i, 0)), tile, tile, tile],
        out_specs=[tile] * 4, out_shape=[_sds((r, c))] * 4,
        compiler_params=_params(("parallel",)),
    )(parts, w, m, v)


def _sum_parts(name, parts):
    p, r, c = parts.shape

    def body(p_ref, o_ref):
        g = p_ref[0]
        for d in range(1, p):
            g = g + p_ref[d]
        o_ref[...] = g

    return pl.pallas_call(
        body, name=name, out_shape=_sds((r, c)),
        in_specs=[pl.BlockSpec(memory_space=pltpu.VMEM)], out_specs=pl.BlockSpec(memory_space=pltpu.VMEM),
    )(parts)


WEIGHTS = ["norm_mix_pre", "w_in", "conv_dw_w", "conv_dw_b", "conv_ln_g", "conv_ln_b", "w_conv_branch",
           "b_conv_branch", "w_att_branch", "w_out", "norm_mix_post", "norm_ffn_pre", "w_ffn_up", "w_ffn_down",
           "norm_ffn_post"]
COL_SHARDED = ["w_in", "w_conv_branch", "w_att_branch", "w_ffn_up"]
ROW_SHARDED = ["w_out", "w_ffn_down"]
VECTORS = ["norm_mix_pre", "conv_dw_b", "conv_ln_g", "conv_ln_b", "b_conv_branch", "norm_mix_post",
           "norm_ffn_pre", "norm_ffn_post"]


def _cols_to_full(g):
    return g.transpose(1, 0, 2).reshape(g.shape[1], N_DEV * g.shape[2])


def _full_to_cols(f):
    return f.reshape(f.shape[0], N_DEV, f.shape[1] // N_DEV).transpose(1, 0, 2)


def _pack_vectors(vecs):
    rows = [jnp.pad(vecs[nm].reshape(-1), (0, D_MODEL - vecs[nm].size)) for nm in VECTORS]
    return jnp.stack(rows)


def _unpack_vectors(packed, sizes):
    return {nm: packed[n, :sizes[nm]] for n, nm in enumerate(VECTORS)}


def kernel(x, norm_mix_pre, w_in, conv_dw_w, conv_dw_b, conv_ln_g, conv_ln_b, w_conv_branch, b_conv_branch, w_att_branch, w_out, norm_mix_post, norm_ffn_pre, w_ffn_up, w_ffn_down, norm_ffn_post, loss_target, m_norm_mix_pre, m_w_in, m_conv_dw_w, m_conv_dw_b, m_conv_ln_g, m_conv_ln_b, m_w_conv_branch, m_b_conv_branch, m_w_att_branch, m_w_out, m_norm_mix_post, m_norm_ffn_pre, m_w_ffn_up, m_w_ffn_down, m_norm_ffn_post, v_norm_mix_pre, v_w_in, v_conv_dw_w, v_conv_dw_b, v_conv_ln_g, v_conv_ln_b, v_w_conv_branch, v_b_conv_branch, v_w_att_branch, v_w_out, v_norm_mix_post, v_norm_ffn_pre, v_w_ffn_up, v_w_ffn_down, v_norm_ffn_post):
    ws = dict(zip(WEIGHTS, [norm_mix_pre, w_in, conv_dw_w, conv_dw_b, conv_ln_g, conv_ln_b, w_conv_branch,
                            b_conv_branch, w_att_branch, w_out, norm_mix_post, norm_ffn_pre, w_ffn_up, w_ffn_down,
                            norm_ffn_post]))
    ms = dict(zip(WEIGHTS, [m_norm_mix_pre, m_w_in, m_conv_dw_w, m_conv_dw_b, m_conv_ln_g, m_conv_ln_b,
                            m_w_conv_branch, m_b_conv_branch, m_w_att_branch, m_w_out, m_norm_mix_post,
                            m_norm_ffn_pre, m_w_ffn_up, m_w_ffn_down, m_norm_ffn_post]))
    vs = dict(zip(WEIGHTS, [v_norm_mix_pre, v_w_in, v_conv_dw_w, v_conv_dw_b, v_conv_ln_g, v_conv_ln_b,
                            v_w_conv_branch, v_b_conv_branch, v_w_att_branch, v_w_out, v_norm_mix_post,
                            v_norm_ffn_pre, v_w_ffn_up, v_w_ffn_down, v_norm_ffn_post]))

    dw_block = jnp.pad(conv_dw_w, ((0, 1), (0, 0)))
    g_in, g_dw = _exchange_call("gather_first", _gather_exchange([w_in.astype(BF16), dw_block]))
    full = {"w_in": _full_weight("w_in", g_in), "conv_dw_w": _cols_to_full(g_dw)}
    for nm in VECTORS:
        full[nm] = ws[nm].reshape(1, -1)

    loss_local, grad_x, received, grads = _local_step(
        x[0], loss_target[0], full, {nm: ws[nm].astype(BF16) for nm in LATE})

    loss_at = (VECTORS.index("conv_dw_b"), CONV_DIM)
    small = _exchange_call("gather_small_grads", _gather_exchange(
        [_pack_vectors(grads).at[loss_at].set(loss_local), grads["conv_dw_w"]]))
    out_g, out_d, out_m, out_v = {}, {}, {}, {}
    for nm in LATE + ["w_in"]:
        out_g[nm], out_d[nm], out_m[nm], out_v[nm] = _sum_adamw("adamw_" + nm, received[nm], ws[nm], ms[nm], vs[nm])
    sizes = {nm: ws[nm].size for nm in VECTORS}
    vec = _sum_adamw("adamw_vectors", small[0], _pack_vectors(ws), _pack_vectors(ms), _pack_vectors(vs))
    for res, dst in zip(vec, (out_g, out_d, out_m, out_v)):
        dst.update(_unpack_vectors(res, sizes))
    loss = vec[0][loss_at]
    dw_full = _sum_parts("sum_dw_grads", small[1])
    me = _slot(*_place())
    dw_mine = lax.dynamic_slice(dw_full, (0, me * (CONV_DIM // N_DEV)), (CONV_WIDTH, CONV_DIM // N_DEV))
    nm = "conv_dw_w"
    out_g[nm], out_d[nm], out_m[nm], out_v[nm] = _sum_adamw("adamw_dw", dw_mine[None], ws[nm], ms[nm], vs[nm])

    outs = [loss, grad_x[None]]
    for group in (out_g, out_d, out_m, out_v):
        outs += [group[nm] for nm in WEIGHTS]
    return tuple(outs)
```

```python
import math

import jax
import jax.numpy as jnp
from jax import lax
from jax.experimental import pallas as pl
from jax.experimental.pallas import tpu as pltpu

F32 = jnp.float32
BF16 = jnp.bfloat16

N_DEV = 8
D_MODEL = 1024
CONV_DIM = 512
CONV_WIDTH = 31
N_HEADS = 8
HEAD_DIM = 64
ATT_DIM = N_HEADS * HEAD_DIM
D_FF = 2816
EPS = 1e-6
IN_SPLITS = (0, 1024, 1536, 2048, 2560, 3584, 4608)

ADAM_LR = 0.001
ADAM_B1 = 0.9
ADAM_B2 = 0.999
ADAM_EPS = 1e-08
ADAM_WD = 0.01
ADAM_STEP = 10

LANES = 128
SUBLANES = 8
HALO = 32
ATT_TILE = 256
DEAD_SUM = -120.0
VMEM_LIMIT = 56 * 1024 * 1024
MESH = pl.DeviceIdType.MESH
ANY = pl.BlockSpec(memory_space=pl.ANY)


def _pick(dim, target, align=LANES):
    t = min(dim, target)
    t -= t % align
    while t >= align:
        if dim % t == 0:
            return t
        t -= align
    return dim


def _params(semantics):
    return pltpu.CompilerParams(dimension_semantics=semantics, vmem_limit_bytes=VMEM_LIMIT)


def _matmul(a, b, *, name, ta=False, tb=False, out_dtype=F32):
    m, k = (a.shape[1], a.shape[0]) if ta else a.shape
    n, k2 = b.shape if tb else (b.shape[1], b.shape[0])
    assert k == k2, (a.shape, b.shape, ta, tb)
    tm, tn, tk = _pick(m, 1408 if ta else 512), _pick(n, 1536), _pick(k, 1536)
    nk = k // tk
    dims = (((0 if ta else 1,), (1 if tb else 0,)), ((), ()))

    def body(a_ref, b_ref, o_ref, *acc):
        part = lax.dot_general(a_ref[...], b_ref[...], dims, preferred_element_type=F32)
        if nk == 1:
            o_ref[...] = part.astype(o_ref.dtype)
            return
        acc_ref, = acc
        kk = pl.program_id(2)

        @pl.when(kk == 0)
        def _():
            acc_ref[...] = part

        @pl.when((kk > 0) & (kk < nk - 1))
        def _():
            acc_ref[...] += part

        @pl.when(kk == nk - 1)
        def _():
            o_ref[...] = (acc_ref[...] + part).astype(o_ref.dtype)

    a_spec = pl.BlockSpec((tk, tm), lambda j, i, kk: (kk, i)) if ta else pl.BlockSpec((tm, tk), lambda j, i, kk: (i, kk))
    b_spec = (pl.BlockSpec((tn, tk), lambda j, i, kk: (j, kk)) if tb
              else pl.BlockSpec((tk, tn), lambda j, i, kk: (kk, j)))
    return pl.pallas_call(
        body, name=name, grid=(n // tn, m // tm, nk),
        in_specs=[a_spec, b_spec],
        out_specs=pl.BlockSpec((tm, tn), lambda j, i, kk: (i, j)),
        out_shape=jax.ShapeDtypeStruct((m, n), out_dtype),
        scratch_shapes=[pltpu.VMEM((tm, tn), F32)] if nk > 1 else [],
        compiler_params=_params(("parallel", "parallel", "arbitrary")),
    )(a, b)


NO_EXCHANGE = ([], [], [], None)


def _sweep_marks(nt):
    i = pl.program_id(0)
    return i == 0, i == (3 * nt) // 4, i == nt - 1


def _rowwise(name, fn, rows, bcasts, row_outs, red_outs=(), tm=256, exchange=NO_EXCHANGE):
    s = rows[0].shape[0]
    tm = _pick(s, tm, 16)
    nt = s // tm
    resident = pl.Buffered(1)
    nr, nb, no, nd = len(rows), len(bcasts), len(row_outs), len(red_outs)
    x_arrs, x_shape, x_scratch, _ = exchange
    nx = len(x_arrs)
    first_out = nr + nb + nx

    def body(*refs):
        finish_exchange = _carry_exchange(exchange, refs, nr + nb, no + nd, *_sweep_marks(nt))
        ins = [r[...] for r in refs[:nr + nb]]
        outs, reds = fn(*ins)
        for ref, val in zip(refs[first_out:first_out + no], outs):
            ref[...] = val.astype(ref.dtype)
        i = pl.program_id(0)
        for ref, val in zip(refs[first_out + no:first_out + no + nd], reds):
            @pl.when(i == 0)
            def _():
                ref[...] = val

            @pl.when(i > 0)
            def _():
                ref[...] += val
        finish_exchange()

    in_specs = [pl.BlockSpec((tm, r.shape[1]), lambda i: (i, 0)) for r in rows]
    in_specs += [pl.BlockSpec(b.shape, lambda i: (0, 0), pipeline_mode=resident) for b in bcasts]
    out_specs = [pl.BlockSpec((tm, o.shape[1]), lambda i: (i, 0)) for o in row_outs]
    out_specs += [pl.BlockSpec(d.shape, lambda i: (0, 0)) for d in red_outs]
    return pl.pallas_call(
        body, name=name, grid=(nt,), in_specs=in_specs + [ANY] * nx, out_specs=out_specs + [ANY] * nx,
        out_shape=list(row_outs) + list(red_outs) + x_shape, scratch_shapes=x_scratch,
        compiler_params=_params(("arbitrary",)),
    )(*rows, *bcasts, *x_arrs)


def _sds(shape, dtype=F32):
    return jax.ShapeDtypeStruct(shape, dtype)


def _rms(x, g):
    y = x * lax.rsqrt(jnp.mean(x * x, axis=-1, keepdims=True) + EPS)
    return y * g


def _silu(x):
    return x * jax.nn.sigmoid(x)


def _swiglu(g, u):
    return _silu(g) * u


def _ln_silu(u, g, b):
    mu = jnp.mean(u, axis=-1, keepdims=True)
    var = jnp.mean(jnp.square(u - mu), axis=-1, keepdims=True)
    return _silu((u - mu) * lax.rsqrt(var + EPS) * g + b)


def _merge(conv_pre, att_out, g_conv, g_att, b_cb):
    return jax.nn.sigmoid(g_conv) * (conv_pre + b_cb) + jax.nn.sigmoid(g_att) * att_out


def _glu(t):
    return t[:, :CONV_DIM] * jax.nn.sigmoid(t[:, CONV_DIM:])


def _shifted_reader(buf, shifted, tm):
    for b in range(1, SUBLANES):
        shifted[b - 1, :, :] = buf[pl.ds(b, tm + HALO - SUBLANES), :]

    def read(o):
        a, b = divmod(o, SUBLANES)
        return buf[pl.ds(SUBLANES * a, tm), :] if b == 0 else shifted[b - 1, pl.ds(SUBLANES * a, tm), :]

    return read


def _conv_fwd(conv_in, w_pad, b, ln_g, ln_b, exchange, tm=256):
    s = conv_in.shape[0]
    tm = _pick(s, tm, HALO)
    ratio = tm // HALO
    x_arrs, x_shape, x_scratch, _ = exchange
    nx = len(x_arrs)

    def body(*refs):
        main_ref, halo_ref, w_ref, b_ref, g_ref, be_ref = refs[:6]
        u3_ref, u1_ref = refs[6 + nx:8 + nx]
        buf, shifted = refs[-2:]
        finish_exchange = _carry_exchange(exchange, refs, 6, 2, *_sweep_marks(s // tm))
        i = pl.program_id(0)
        buf[0:HALO, :] = _glu(halo_ref[...]) * (i > 0).astype(F32)
        buf[HALO:HALO + tm, :] = _glu(main_ref[...])
        read = _shifted_reader(buf, shifted, tm)
        acc = jnp.zeros((tm, CONV_DIM), F32) + b_ref[...]
        for j in range(CONV_WIDTH):
            acc = acc + w_ref[j:j + 1, :] * read(HALO - (CONV_WIDTH - 1) + j)
        u1_ref[...] = acc
        u3_ref[...] = _ln_silu(acc, g_ref[...], be_ref[...]).astype(u3_ref.dtype)
        finish_exchange()

    res = pl.pallas_call(
        body, name="conv_fwd", grid=(s // tm,),
        in_specs=[pl.BlockSpec((tm, 2 * CONV_DIM), lambda i: (i, 0)),
                  pl.BlockSpec((HALO, 2 * CONV_DIM), lambda i: (jnp.maximum(i * ratio - 1, 0), 0)),
                  pl.BlockSpec(w_pad.shape, lambda i: (0, 0)),
                  pl.BlockSpec(b.shape, lambda i: (0, 0)),
                  pl.BlockSpec(ln_g.shape, lambda i: (0, 0)),
                  pl.BlockSpec(ln_b.shape, lambda i: (0, 0))] + [ANY] * nx,
        out_specs=[pl.BlockSpec((tm, CONV_DIM), lambda i: (i, 0)),
                   pl.BlockSpec((tm, CONV_DIM), lambda i: (i, 0))] + [ANY] * nx,
        out_shape=[_sds((s, CONV_DIM), BF16), _sds((s, CONV_DIM), F32)] + x_shape,
        scratch_shapes=x_scratch + [pltpu.VMEM((tm + HALO, CONV_DIM), F32),
                                    pltpu.VMEM((SUBLANES - 1, tm + HALO - SUBLANES, CONV_DIM), F32)],
        compiler_params=_params(("arbitrary",)),
    )(conv_in, conv_in, w_pad, b, ln_g, ln_b, *x_arrs)
    return res[0], res[1], res[2:]


def _conv_bwd(conv_in, du1, w_pad, exchange, tm=256):
    s = conv_in.shape[0]
    tm = _pick(s, tm, HALO)
    ratio = tm // HALO
    nt = s // tm
    last_halo = s // HALO - 1
    x_arrs, x_shape, x_scratch, _ = exchange
    nx = len(x_arrs)

    def body(*refs):
        main_ref, halo_ref, du_ref, dun_ref, w_ref = refs[:5]
        dci_ref, dw_ref, db_ref = refs[5 + nx:8 + nx]
        ubuf, dbuf, ushift, dshift = refs[-4:]
        finish_exchange = _carry_exchange(exchange, refs, 5, 3, *_sweep_marks(nt))
        i = pl.program_id(0)
        main = main_ref[...]
        a = main[:, :CONV_DIM]
        sb = jax.nn.sigmoid(main[:, CONV_DIM:])
        ubuf[0:HALO, :] = _glu(halo_ref[...]) * (i > 0).astype(F32)
        ubuf[HALO:HALO + tm, :] = a * sb
        du = du_ref[...]
        dbuf[0:tm, :] = du
        dbuf[tm:tm + HALO, :] = dun_ref[...] * (i < nt - 1).astype(F32)

        @pl.when(i == 0)
        def _():
            dw_ref[...] = jnp.zeros_like(dw_ref)
            db_ref[...] = jnp.zeros_like(db_ref)

        read_u = _shifted_reader(ubuf, ushift, tm)
        read_d = _shifted_reader(dbuf, dshift, tm)
        du0 = jnp.zeros((tm, CONV_DIM), F32)
        for j in range(CONV_WIDTH):
            du0 = du0 + w_ref[j:j + 1, :] * read_d(CONV_WIDTH - 1 - j)
            dw_ref[j:j + 1, :] += jnp.sum(du * read_u(HALO - (CONV_WIDTH - 1) + j), axis=0, keepdims=True)
        db_ref[...] += jnp.sum(du, axis=0, keepdims=True)
        dci_ref[:, :CONV_DIM] = (du0 * sb).astype(dci_ref.dtype)
        dci_ref[:, CONV_DIM:] = (du0 * a * sb * (1.0 - sb)).astype(dci_ref.dtype)
        finish_exchange()

    res = pl.pallas_call(
        body, name="conv_bwd", grid=(nt,),
        in_specs=[pl.BlockSpec((tm, 2 * CONV_DIM), lambda i: (i, 0)),
                  pl.BlockSpec((HALO, 2 * CONV_DIM), lambda i: (jnp.maximum(i * ratio - 1, 0), 0)),
                  pl.BlockSpec((tm, CONV_DIM), lambda i: (i, 0)),
                  pl.BlockSpec((HALO, CONV_DIM), lambda i: (jnp.minimum((i + 1) * ratio, last_halo), 0)),
                  pl.BlockSpec(w_pad.shape, lambda i: (0, 0))] + [ANY] * nx,
        out_specs=[pl.BlockSpec((tm, 2 * CONV_DIM), lambda i: (i, 0)),
                   pl.BlockSpec(w_pad.shape, lambda i: (0, 0)),
                   pl.BlockSpec((1, CONV_DIM), lambda i: (0, 0))] + [ANY] * nx,
        out_shape=[_sds((s, 2 * CONV_DIM), BF16), _sds(w_pad.shape), _sds((1, CONV_DIM))] + x_shape,
        scratch_shapes=x_scratch + [pltpu.VMEM((tm + HALO, CONV_DIM), F32)] * 2
        + [pltpu.VMEM((SUBLANES - 1, tm + HALO - SUBLANES, CONV_DIM), F32)] * 2,
        compiler_params=_params(("arbitrary",)),
    )(conv_in, conv_in, du1, du1, w_pad, *x_arrs)
    return res[0], res[1], res[2], res[3:]


def _logsig_neg(z):
    return jnp.minimum(-z, 0.0) - jnp.log(1.0 + jnp.exp(-jnp.abs(z)))


def _split_dot(val, tri):
    hi = val.astype(BF16)
    lo = (val - hi.astype(F32)).astype(BF16)
    return jnp.dot(hi, tri, preferred_element_type=F32) + jnp.dot(lo, tri, preferred_element_type=F32)


def _attn_masks(t, later):
    row = lax.broadcasted_iota(jnp.int32, (t, t), 0)
    col = lax.broadcasted_iota(jnp.int32, (t, t), 1)
    tri = jnp.where(row > col if later else row <= col, 1.0, 0.0).astype(BF16)
    return col < row, tri


def _grid_marks(h, nq):
    hh, i = pl.program_id(0), pl.program_id(1)
    return (hh == 0) & (i == 0), (hh == (3 * h) // 4) & (i == 0), (hh == h - 1) & (i == nq - 1)


def _head_masks(shape):
    lane = lax.broadcasted_iota(jnp.int32, shape, len(shape) - 1)
    return lane < HEAD_DIM, lane >= HEAD_DIM


def _per_head(blk):
    m0, m1 = _head_masks(blk.shape)
    zero = jnp.zeros_like(blk)
    return jnp.where(m0, blk, zero), jnp.where(m1, blk, zero)


NT = (((1,), (1,)), ((), ()))
TN = (((0,), (0,)), ((), ()))


def _attn_fwd(q, k, v, exchange):
    s = q.shape[0]
    hp = q.shape[1] // LANES
    t = ATT_TILE
    scale = 1.0 / math.sqrt(HEAD_DIM)
    x_arrs, x_shape, x_scratch, _ = exchange
    nx = len(x_arrs)

    def body(*refs):
        q_ref, k_ref, v_ref = refs[:3]
        o_ref, lt_ref, nb_ref = refs[3 + nx:6 + nx]
        finish_exchange = _carry_exchange(exchange, refs, 3, 3, *_grid_marks(hp, s // t))
        i = pl.program_id(1)
        qs = _per_head((q_ref[...].astype(F32) * scale).astype(BF16))
        causal, tri = _attn_masks(t, later=True)

        def step(kb, carry, masked):
            cs, acc = carry
            off = pl.multiple_of(kb * t, t)
            kblk = k_ref[pl.ds(off, t), :]
            vs = _per_head(v_ref[pl.ds(off, t), :])
            new_cs = []
            for hd in range(2):
                z = lax.dot_general(qs[hd], kblk, NT, preferred_element_type=F32)
                l = _logsig_neg(z)
                if masked:
                    l = jnp.where(causal, l, 0.0)
                e = z + l + _split_dot(l, tri) + cs[hd]
                if masked:
                    e = jnp.where(causal, e, -1e30)
                acc = acc + jnp.dot(jnp.exp(e).astype(BF16), vs[hd], preferred_element_type=F32)
                new_cs.append(cs[hd] + jnp.sum(l, axis=1, keepdims=True))
            return tuple(new_cs), acc

        zero = jnp.zeros((t, 1), F32)
        carry = step(i, ((zero, zero), jnp.zeros((t, LANES), F32)), True)

        def more(state):
            n, (cs, _) = state
            return (n < i) & (jnp.maximum(jnp.max(cs[0]), jnp.max(cs[1])) > DEAD_SUM)

        n_blocks, carry = lax.while_loop(more, lambda st: (st[0] + 1, step(i - 1 - st[0], st[1], False)),
                                         (jnp.int32(0), carry))
        m0, _ = _head_masks((t, LANES))
        lt_ref[...] = jnp.where(m0, carry[0][0], carry[0][1])
        o_ref[...] = carry[1].astype(o_ref.dtype)
        nb_ref[pl.program_id(0), i] = n_blocks.astype(F32)
        finish_exchange()

    res = pl.pallas_call(
        body, name="attn_fwd", grid=(hp, s // t),
        in_specs=[pl.BlockSpec((t, LANES), lambda p, i: (i, p)),
                  pl.BlockSpec((s, LANES), lambda p, i: (0, p)),
                  pl.BlockSpec((s, LANES), lambda p, i: (0, p))] + [ANY] * nx,
        out_specs=[pl.BlockSpec((t, LANES), lambda p, i: (i, p)),
                   pl.BlockSpec((None, t, LANES), lambda p, i: (p, i, 0)),
                   pl.BlockSpec(memory_space=pltpu.SMEM)] + [ANY] * nx,
        out_shape=[_sds(q.shape, BF16), _sds((hp, s, LANES), F32), _sds((hp, s // t), F32)] + x_shape,
        scratch_shapes=x_scratch,
        compiler_params=_params(("arbitrary", "arbitrary")),
    )(q, k, v, *x_arrs)
    return res[0], res[1], res[2], res[3:]


def _attn_bwd(q, k, v, do, ltot, n_blocks, exchange):
    s = q.shape[0]
    hp = q.shape[1] // LANES
    t = ATT_TILE
    scale = 1.0 / math.sqrt(HEAD_DIM)
    x_arrs, x_shape, x_scratch, _ = exchange
    nx = len(x_arrs)

    def body(*refs):
        q_ref, k_ref, v_ref, do_ref, lt_ref, nb_ref = refs[:6]
        dq_ref, dk_ref, dv_ref = refs[6 + nx:9 + nx]
        finish_exchange = _carry_exchange(exchange, refs, 6, 3, *_grid_marks(hp, s // t))
        i = pl.program_id(1)
        first = jnp.clip(i - nb_ref[pl.program_id(0), i].astype(jnp.int32), 0, i)

        @pl.when(i == 0)
        def _():
            dk_ref[...] = jnp.zeros_like(dk_ref)
            dv_ref[...] = jnp.zeros_like(dv_ref)

        qb = q_ref[...]
        qm = _per_head(qb)
        qs = _per_head((qb.astype(F32) * scale).astype(BF16))
        dos = _per_head(do_ref[...])
        lts = (lt_ref[:, 0:1], lt_ref[:, HEAD_DIM:HEAD_DIM + 1])
        causal, tri = _attn_masks(t, later=False)

        def step(kb, carry, masked):
            cls, cgs, dq = carry
            off = pl.multiple_of(kb * t, t)
            kblk = k_ref[pl.ds(off, t), :]
            vblk = v_ref[pl.ds(off, t), :]
            ks = _per_head(kblk)
            dk = jnp.zeros((t, LANES), F32)
            dv = jnp.zeros((t, LANES), F32)
            new_cls, new_cgs = [], []
            for hd in range(2):
                z = lax.dot_general(qs[hd], kblk, NT, preferred_element_type=F32)
                l = _logsig_neg(z)
                if masked:
                    l = jnp.where(causal, l, 0.0)
                e = z + l + ((lts[hd] - cls[hd]) - _split_dot(l, tri))
                if masked:
                    e = jnp.where(causal, e, -1e30)
                a = jnp.exp(e)
                g = lax.dot_general(dos[hd], vblk, NT, preferred_element_type=F32) * a
                p = cgs[hd] + jnp.dot(g.astype(BF16), tri, preferred_element_type=F32) - g
                el = jnp.exp(l)
                dz = g * el - p * (1.0 - el)
                if masked:
                    dz = jnp.where(causal, dz, 0.0)
                dzb = (dz * scale).astype(BF16)
                dq = dq + jnp.dot(dzb, ks[hd], preferred_element_type=F32)
                dk = dk + lax.dot_general(dzb, qm[hd], TN, preferred_element_type=F32)
                dv = dv + lax.dot_general(a.astype(BF16), dos[hd], TN, preferred_element_type=F32)
                new_cls.append(cls[hd] + jnp.sum(l, axis=1, keepdims=True))
                new_cgs.append(cgs[hd] + jnp.sum(g, axis=1, keepdims=True))
            dk_ref[pl.ds(off, t), :] += dk
            dv_ref[pl.ds(off, t), :] += dv
            return tuple(new_cls), tuple(new_cgs), dq

        zero = jnp.zeros((t, 1), F32)
        init = ((zero, zero), (zero, zero), jnp.zeros((t, LANES), F32))
        carry = lax.fori_loop(first, i, lambda kb, cr: step(kb, cr, False), init)
        carry = step(i, carry, True)
        dq_ref[...] = carry[2]
        finish_exchange()

    blk = pl.BlockSpec((t, LANES), lambda p, i: (i, p))
    whole = pl.BlockSpec((s, LANES), lambda p, i: (0, p))
    res = pl.pallas_call(
        body, name="attn_bwd", grid=(hp, s // t),
        in_specs=[blk, whole, whole, blk, pl.BlockSpec((None, t, LANES), lambda p, i: (p, i, 0)),
                  pl.BlockSpec(memory_space=pltpu.SMEM)] + [ANY] * nx,
        out_specs=[blk, whole, whole] + [ANY] * nx,
        out_shape=[_sds(q.shape)] * 3 + x_shape,
        scratch_shapes=x_scratch,
        compiler_params=_params(("arbitrary", "arbitrary")),
    )(q, k, v, do, ltot, n_blocks, *x_arrs)
    return res[0], res[1], res[2], res[3:]


LATE = ["w_conv_branch", "w_att_branch", "w_out", "w_ffn_up", "w_ffn_down"]


def _full_weight(name, gathered):
    return _cols_to_full(gathered) if name in COL_SHARDED else gathered.reshape(-1, gathered.shape[2])


def _grad_slabs(name, grad):
    return _full_to_cols(grad) if name in COL_SHARDED else grad.reshape(N_DEV, -1, grad.shape[1])


def _side_slabs(name, grad):
    slabs = _grad_slabs(name, grad)
    return slabs.reshape((4, 2) + slabs.shape[1:])


def _local_step(x, target, w, late_blocks):
    s = x.shape[0]
    w = dict(w)
    g1, g2, g3, g4 = w["norm_mix_pre"], w["norm_mix_post"], w["norm_ffn_pre"], w["norm_ffn_post"]

    w_in = w["w_in"]

    def proj_fn(xt, g1_, w_in_):
        h = _rms(xt, g1_).astype(BF16)
        proj = jnp.dot(h, w_in_, preferred_element_type=F32)
        return (h, *[proj[:, IN_SPLITS[n]:IN_SPLITS[n + 1]] for n in range(6)]), ()

    mix_weights = ["w_conv_branch", "w_att_branch", "w_out"]
    h1, conv_in, q, k, v, g_conv, g_att, *gathered = _rowwise(
        "norm_proj", proj_fn, [x], [g1, w_in],
        [_sds((s, D_MODEL), BF16), _sds((s, 2 * CONV_DIM)), _sds((s, ATT_DIM), BF16), _sds((s, ATT_DIM), BF16),
         _sds((s, ATT_DIM), BF16), _sds((s, D_MODEL)), _sds((s, D_MODEL))], tm=512,
        exchange=_gather_exchange([late_blocks[nm] for nm in mix_weights]))
    for nm, g in zip(mix_weights, gathered):
        w[nm] = _full_weight(nm, g)

    u3, u1, (g_down,) = _conv_fwd(conv_in, w["conv_dw_w"], w["conv_dw_b"], w["conv_ln_g"], w["conv_ln_b"],
                                  _gather_exchange([late_blocks["w_ffn_down"]]))
    w["w_ffn_down"] = _full_weight("w_ffn_down", g_down)
    att, ltot, n_blocks, (g_up,) = _attn_fwd(q, k, v, _gather_exchange([late_blocks["w_ffn_up"]]))
    w["w_ffn_up"] = _full_weight("w_ffn_up", g_up)

    def merge_fn(u3t, at, gc, ga, w_cb, w_ab, b_cb):
        cp = jnp.dot(u3t, w_cb, preferred_element_type=F32)
        ao = jnp.dot(at, w_ab, preferred_element_type=F32)
        return (_merge(cp, ao, gc, ga, b_cb), cp, ao), ()

    merged, conv_pre, att_out = _rowwise(
        "branch_merge", merge_fn, [u3, att, g_conv, g_att], [w["w_conv_branch"], w["w_att_branch"], w["b_conv_branch"]],
        [_sds((s, D_MODEL), BF16)] * 3, tm=512)

    def mid_fn(mt, xt, w_out, g2_, g3_):
        mix_ = jnp.dot(mt, w_out, preferred_element_type=F32)
        x2_ = xt + _rms(mix_, g2_)
        return (mix_, x2_, _rms(x2_, g3_)), ()

    mix, x2, h2 = _rowwise("mix_mid_norm", mid_fn, [merged, x], [w["w_out"], g2, g3],
                           [_sds((s, D_MODEL)), _sds((s, D_MODEL)), _sds((s, D_MODEL), BF16)], tm=512)

    def ffn_up_fn(ht, w_up):
        gu_ = jnp.dot(ht, w_up, preferred_element_type=F32)
        return (gu_, _swiglu(gu_[:, :D_FF], gu_[:, D_FF:])), ()

    gu, act = _rowwise("ffn_up", ffn_up_fn, [h2], [w["w_ffn_up"]],
                       [_sds((s, 2 * D_FF), BF16), _sds((s, D_FF), BF16)], tm=512)

    def final_fn(at, x2t, tgt, w_down, g4_):
        ff = jnp.dot(at, w_down, preferred_element_type=F32)
        n4, vjp = jax.vjp(_rms, ff, g4_)
        err = x2t + n4 - tgt
        dy = err * (1.0 / D_MODEL)
        dff, dg4 = vjp(dy)
        return (dy, dff), (jnp.sum(err * err, axis=0, keepdims=True), dg4)

    dy, dff, loss_cols, d_g4 = _rowwise("ffn_down_loss", final_fn, [act, x2, target], [w["w_ffn_down"], g4],
                                        [_sds((s, D_MODEL)), _sds((s, D_MODEL), BF16)],
                                        [_sds((1, D_MODEL)), _sds((1, D_MODEL))], tm=512)
    loss = 0.5 * jnp.sum(loss_cols) / D_MODEL

    d_w_down = _matmul(act, dff, ta=True, name="d_w_down", out_dtype=BF16)

    def act_bwd_fn(dfft, gut, w_down):
        d_act = lax.dot_general(dfft, w_down, NT, preferred_element_type=F32)
        gu_ = gut.astype(F32)
        _, vjp = jax.vjp(_swiglu, gu_[:, :D_FF], gu_[:, D_FF:])
        return (jnp.concatenate(vjp(d_act), axis=1),), ()

    down_slabs = _side_slabs("w_ffn_down", d_w_down)
    dgu, theirs = _rowwise("ffn_act_bwd", act_bwd_fn, [dff, gu], [w["w_ffn_down"]], [_sds((s, 2 * D_FF), BF16)],
                           exchange=_pair_exchange([down_slabs]))
    down_sums = _pair_sum("pair_sum_w_ffn_down", down_slabs, theirs)
    d_w_up = _matmul(h2, dgu, ta=True, name="d_w_up", out_dtype=BF16)
    received = {}
    up_slabs = _side_slabs("w_ffn_up", d_w_up)

    def mid_bwd_fn(dgut, xt, mt, dyt, w_up, g2_, g3_):
        dh = lax.dot_general(dgut, w_up, NT, preferred_element_type=F32)
        n2, vjp2 = jax.vjp(_rms, mt, g2_)
        x2_ = xt + n2
        _, vjp3 = jax.vjp(_rms, x2_, g3_)
        dx2_, dg3 = vjp3(dh)
        dx2_ = dx2_ + dyt
        dmix_, dg2 = vjp2(dx2_)
        return (dx2_, dmix_), (dg2, dg3)

    dx2, dmix, d_g2, d_g3, received["w_ffn_down"] = _rowwise(
        "ffn_up_mid_bwd", mid_bwd_fn, [dgu, x, mix, dy], [w["w_ffn_up"], g2, g3],
        [_sds((s, D_MODEL)), _sds((s, D_MODEL), BF16)], [_sds((1, D_MODEL)), _sds((1, D_MODEL))], tm=512,
        exchange=_chip_exchange([down_sums]))
    d_w_out = _matmul(merged, dmix, ta=True, name="d_w_out", out_dtype=BF16)

    def merge_bwd_fn(dmt, cp, ao, gc, ga, w_out, w_cb, w_ab, b_cb):
        dm = lax.dot_general(dmt, w_out, NT, preferred_element_type=F32)
        _, vjp = jax.vjp(_merge, cp.astype(F32), ao.astype(F32), gc, ga, b_cb)
        dcp, dao, dgc, dga, dbias = vjp(dm)
        dcp, dao = dcp.astype(BF16), dao.astype(BF16)
        du3_ = lax.dot_general(dcp, w_cb, NT, preferred_element_type=F32)
        datt_ = lax.dot_general(dao, w_ab, NT, preferred_element_type=F32)
        return (dcp, dao, dgc, dga, du3_, datt_), (dbias,)

    d_conv_out, d_att_out, d_g_conv, d_g_att, du3, d_att, d_b_cb, theirs = _rowwise(
        "merge_bwd", merge_bwd_fn, [dmix, conv_pre, att_out, g_conv, g_att],
        [w["w_out"], w["w_conv_branch"], w["w_att_branch"], w["b_conv_branch"]],
        [_sds((s, D_MODEL), BF16)] * 4 + [_sds((s, CONV_DIM)), _sds((s, ATT_DIM), BF16)], [_sds((1, D_MODEL))], tm=512,
        exchange=_pair_exchange([up_slabs]))

    d_w_cb = _matmul(u3, d_conv_out, ta=True, name="d_w_conv_branch", out_dtype=BF16)
    d_w_ab = _matmul(att, d_att_out, ta=True, name="d_w_att_branch", out_dtype=BF16)

    dq, dk, dv, (received["w_ffn_up"],) = _attn_bwd(
        q, k, v, d_att, ltot, n_blocks, _chip_exchange([_pair_sum("pair_sum_w_ffn_up", up_slabs, theirs)]))

    def ln_bwd_fn(u1t, du3t, g_, b_):
        _, vjp = jax.vjp(_ln_silu, u1t, g_, b_)
        du1_, dg_, db_ = vjp(du3t)
        return (du1_,), (dg_, db_)

    du1, d_ln_g, d_ln_b = _rowwise("conv_ln_bwd", ln_bwd_fn, [u1, du3], [w["conv_ln_g"], w["conv_ln_b"]],
                                   [_sds((s, CONV_DIM))], [_sds((1, CONV_DIM)), _sds((1, CONV_DIM))])
    mix_grads = {"w_conv_branch": d_w_cb, "w_att_branch": d_w_ab, "w_out": d_w_out}
    d_conv_in, d_dw_w, d_dw_b, landed = _conv_bwd(
        conv_in, du1, w["conv_dw_w"], _scatter_exchange([_grad_slabs(nm, mix_grads[nm]) for nm in mix_weights]))
    received.update(zip(mix_weights, landed))

    d_proj = jnp.concatenate([d_conv_in, dq.astype(BF16), dk.astype(BF16), dv.astype(BF16), d_g_conv, d_g_att],
                             axis=1)
    d_w_in = _matmul(h1, d_proj, ta=True, name="d_w_in", out_dtype=BF16)
    in_slabs = _side_slabs("w_in", d_w_in)
    (theirs,) = _exchange_call("pair_swap_w_in", _pair_exchange([in_slabs]))

    def pre_bwd_fn(dpt, xt, dx2t, w_in_, g_):
        dh = lax.dot_general(dpt, w_in_, NT, preferred_element_type=F32)
        _, vjp = jax.vjp(_rms, xt, g_)
        dx_, dg_ = vjp(dh)
        return (dx_ + dx2t,), (dg_,)

    grad_x, d_g1, received["w_in"] = _rowwise(
        "proj_norm_bwd", pre_bwd_fn, [d_proj, x, dx2], [w_in, g1], [_sds((s, D_MODEL))], [_sds((1, D_MODEL))], tm=512,
        exchange=_chip_exchange([_pair_sum("pair_sum_w_in", in_slabs, theirs)]))

    grads = {
        "norm_mix_pre": d_g1, "conv_dw_w": d_dw_w, "conv_dw_b": d_dw_b,
        "conv_ln_g": d_ln_g, "conv_ln_b": d_ln_b, "b_conv_branch": d_b_cb,
        "norm_mix_post": d_g2, "norm_ffn_pre": d_g3, "norm_ffn_post": d_g4,
    }
    return loss, grad_x, received, grads


def _place():
    x, y, c = lax.axis_index("x"), lax.axis_index("y"), lax.axis_index("c")
    return x, y, c


def _slot(px, py, pc):
    return 4 * px + 2 * py + pc


def _exchange_scratch(n):
    return [pltpu.SemaphoreType.DMA((7 * n,)), pltpu.SemaphoreType.DMA((7 * n,)), pltpu.SemaphoreType.DMA((n,))]


def _gather_exchange(arrs):
    n = len(arrs)

    def phases(ins, outs, send_sems, recv_sems, local_sems):
        x, y, c = _place()
        me, sibling = (x, y, c), (x, y, 1 - c)
        chips = [(1 - x, y), (x, 1 - y), (1 - x, 1 - y)]

        def copy(a, kk, block, to, src=None):
            dst = outs[a].at[_slot(*block)]
            return pltpu.make_async_remote_copy(
                src_ref=dst if src is None else src, dst_ref=dst,
                send_sem=send_sems.at[a * 7 + kk], recv_sem=recv_sems.at[a * 7 + kk],
                device_id=to, device_id_type=MESH)

        mine = [pltpu.make_async_copy(ins[a], outs[a].at[_slot(*me)], local_sems.at[a]) for a in range(n)]
        first = []
        for a in range(n):
            first.append(copy(a, 0, me, sibling, src=ins[a]))
            first += [copy(a, 1 + j, me, (*chip, c), src=ins[a]) for j, chip in enumerate(chips)]
        passed = [copy(a, 4 + j, (*chip, c), sibling) for j, chip in enumerate(chips) for a in range(n)]

        def send():
            for cp in mine + first:
                cp.start()

        def pass_on():
            for j, chip in enumerate(chips):
                for a in range(n):
                    copy(a, 1 + j, (*chip, c), me).wait_recv()
                    passed[j * n + a].start()

        def finish():
            for a in range(n):
                copy(a, 0, sibling, me).wait_recv()
                for j, chip in enumerate(chips):
                    copy(a, 4 + j, (*chip, 1 - c), me).wait_recv()
            for cp in first + passed:
                cp.wait_send()
            for cp in mine:
                cp.wait()

        return [send, pass_on, finish]

    return list(arrs), [_sds((N_DEV,) + a.shape, a.dtype) for a in arrs], _exchange_scratch(n), phases


def _scatter_exchange(arrs):
    n = len(arrs)
    flips = [(fx, fy, fc) for fx in (0, 1) for fy in (0, 1) for fc in (0, 1)][1:]

    def phases(ins, outs, send_sems, recv_sems, local_sems):
        x, y, c = _place()
        mine = _slot(x, y, c)
        local = [pltpu.make_async_copy(ins[a].at[mine], outs[a].at[mine], local_sems.at[a]) for a in range(n)]
        peers = [((1 - x) if fx else x, (1 - y) if fy else y, (1 - c) if fc else c) for fx, fy, fc in flips]

        def copy(a, kk, src_slot, dst_slot):
            return pltpu.make_async_remote_copy(
                src_ref=ins[a].at[src_slot], dst_ref=outs[a].at[dst_slot],
                send_sem=send_sems.at[a * 7 + kk], recv_sem=recv_sems.at[a * 7 + kk],
                device_id=peers[kk], device_id_type=MESH)

        sends = [copy(a, kk, _slot(*peers[kk]), mine) for a in range(n) for kk in range(7)]

        def send():
            for cp in local + sends:
                cp.start()

        def finish():
            for a in range(n):
                for kk in range(7):
                    copy(a, kk, mine, _slot(*peers[kk])).wait_recv()
            for cp in sends:
                cp.wait_send()
            for cp in local:
                cp.wait()

        return [send, finish]

    return list(arrs), [_sds(a.shape, a.dtype) for a in arrs], _exchange_scratch(n), phases


def _pair_exchange(arrs):
    n = len(arrs)

    def phases(ins, outs, send_sems, recv_sems, local_sems):
        x, y, c = _place()

        def copy(a, chip, side):
            return pltpu.make_async_remote_copy(
                src_ref=ins[a].at[chip, side], dst_ref=outs[a].at[chip],
                send_sem=send_sems.at[a * 7 + chip], recv_sem=recv_sems.at[a * 7 + chip],
                device_id=(x, y, 1 - c), device_id_type=MESH)

        sends = [copy(a, chip, 1 - c) for a in range(n) for chip in range(4)]

        def send():
            for cp in sends:
                cp.start()

        def finish():
            for a in range(n):
                for chip in range(4):
                    copy(a, chip, c).wait_recv()
            for cp in sends:
                cp.wait_send()

        return [send, finish]

    return list(arrs), [_sds((4,) + a.shape[2:], a.dtype) for a in arrs], _exchange_scratch(n), phases


def _chip_exchange(arrs):
    n = len(arrs)

    def phases(ins, outs, send_sems, recv_sems, local_sems):
        x, y, c = _place()
        mine = 2 * x + y
        chips = [(1 - x, y), (x, 1 - y), (1 - x, 1 - y)]
        local = [pltpu.make_async_copy(ins[a].at[mine], outs[a].at[mine], local_sems.at[a]) for a in range(n)]

        def copy(a, j, src_slot, dst_slot):
            return pltpu.make_async_remote_copy(
                src_ref=ins[a].at[src_slot], dst_ref=outs[a].at[dst_slot],
                send_sem=send_sems.at[a * 7 + j], recv_sem=recv_sems.at[a * 7 + j],
                device_id=(*chips[j], c), device_id_type=MESH)

        sends = [copy(a, j, 2 * chips[j][0] + chips[j][1], mine) for a in range(n) for j in range(3)]

        def send():
            for cp in local + sends:
                cp.start()

        def finish():
            for a in range(n):
                for j in range(3):
                    copy(a, j, mine, 2 * chips[j][0] + chips[j][1]).wait_recv()
            for cp in sends:
                cp.wait_send()
            for cp in local:
                cp.wait()

        return [send, finish]

    return list(arrs), [_sds(a.shape, a.dtype) for a in arrs], _exchange_scratch(n), phases


def _pair_sum(name, mine, theirs):
    _, _, r, c = mine.shape

    def body(side_ref, m_ref, t_ref, o_ref):
        o_ref[...] = (m_ref[...].astype(F32) + t_ref[...].astype(F32)).astype(o_ref.dtype)

    return pl.pallas_call(
        body, name=name,
        grid_spec=pltpu.PrefetchScalarGridSpec(
            num_scalar_prefetch=1, grid=(4,),
            in_specs=[pl.BlockSpec((None, None, r, c), lambda j, side: (j, side[0], 0, 0)),
                      pl.BlockSpec((None, r, c), lambda j, side: (j, 0, 0))],
            out_specs=pl.BlockSpec((None, r, c), lambda j, side: (j, 0, 0))),
        out_shape=_sds(theirs.shape, theirs.dtype),
        compiler_params=_params(("parallel",)),
    )(lax.axis_index("c").astype(jnp.int32).reshape(1), mine, theirs)


def _exchange_call(name, exchange):
    arrs, out_shape, scratch, phases = exchange
    n = len(arrs)

    def body(*refs):
        for step in phases(refs[:n], refs[n:2 * n], *refs[2 * n:]):
            step()

    return pl.pallas_call(body, name=name, in_specs=[ANY] * n, out_specs=[ANY] * n,
                          out_shape=out_shape, scratch_shapes=scratch)(*arrs)


def _carry_exchange(exchange, refs, n_in, n_out, first, middle, last):
    arrs, _, _, phases = exchange
    n = len(arrs)
    if n == 0:
        return lambda: None
    ins = refs[n_in:n_in + n]
    outs = refs[n_in + n + n_out:n_in + 2 * n + n_out]
    sems = n_in + 2 * n + n_out
    steps = phases(ins, outs, *refs[sems:sems + 3])
    pl.when(first)(steps[0])
    if len(steps) == 3:
        pl.when(middle)(steps[1])
    return lambda: pl.when(last)(steps[-1])


def _adamw_math(w, g, m, v):
    m2 = ADAM_B1 * m + (1.0 - ADAM_B1) * g
    v2 = ADAM_B2 * v + (1.0 - ADAM_B2) * jnp.square(g)
    m_hat = m2 / (1.0 - ADAM_B1 ** ADAM_STEP)
    v_hat = v2 / (1.0 - ADAM_B2 ** ADAM_STEP)
    delta = -ADAM_LR * (m_hat / (jnp.sqrt(v_hat) + ADAM_EPS) + ADAM_WD * w)
    return delta, m2, v2


def _sum_adamw(name, parts, w, m, v, tr=256):
    p, r, c = parts.shape
    tr = _pick(r, tr, 16)

    def body(p_ref, w_ref, m_ref, v_ref, g_ref, d_ref, m2_ref, v2_ref):
        g = p_ref[0].astype(F32)
        for d in range(1, p):
            g = g + p_ref[d].astype(F32)
        delta, m2, v2 = _adamw_math(w_ref[...], g, m_ref[...], v_ref[...])
        g_ref[...] = g
        d_ref[...] = delta
        m2_ref[...] = m2
        v2_ref[...] = v2

    tile = pl.BlockSpec((tr, c), lambda i: (i, 0))
    return pl.pallas_call(
        body, name=name, grid=(r // tr,),
        in_specs=[pl.BlockSpec((p, tr, c), lambda i: (0, i, 0)), tile, tile, tile],
        out_specs=[tile] * 4, out_shape=[_sds((r, c))] * 4,
        compiler_params=_params(("parallel",)),
    )(parts, w, m, v)


def _sum_parts(name, parts):
    p, r, c = parts.shape

    def body(p_ref, o_ref):
        g = p_ref[0]
        for d in range(1, p):
            g = g + p_ref[d]
        o_ref[...] = g

    return pl.pallas_call(
        body, name=name, out_shape=_sds((r, c)),
        in_specs=[pl.BlockSpec(memory_space=pltpu.VMEM)], out_specs=pl.BlockSpec(memory_space=pltpu.VMEM),
    )(parts)


WEIGHTS = ["norm_mix_pre", "w_in", "conv_dw_w", "conv_dw_b", "conv_ln_g", "conv_ln_b", "w_conv_branch",
           "b_conv_branch", "w_att_branch", "w_out", "norm_mix_post", "norm_ffn_pre", "w_ffn_up", "w_ffn_down",
           "norm_ffn_post"]
COL_SHARDED = ["w_in", "w_conv_branch", "w_att_branch", "w_ffn_up"]
ROW_SHARDED = ["w_out", "w_ffn_down"]
VECTORS = ["norm_mix_pre", "conv_dw_b", "conv_ln_g", "conv_ln_b", "b_conv_branch", "norm_mix_post",
           "norm_ffn_pre", "norm_ffn_post"]


def _cols_to_full(g):
    return g.transpose(1, 0, 2).reshape(g.shape[1], N_DEV * g.shape[2])


def _full_to_cols(f):
    return f.reshape(f.shape[0], N_DEV, f.shape[1] // N_DEV).transpose(1, 0, 2)


def _pack_vectors(vecs):
    rows = [jnp.pad(vecs[nm].reshape(-1), (0, D_MODEL - vecs[nm].size)) for nm in VECTORS]
    return jnp.stack(rows)


def _unpack_vectors(packed, sizes):
    return {nm: packed[n, :sizes[nm]] for n, nm in enumerate(VECTORS)}


def kernel(x, norm_mix_pre, w_in, conv_dw_w, conv_dw_b, conv_ln_g, conv_ln_b, w_conv_branch, b_conv_branch, w_att_branch, w_out, norm_mix_post, norm_ffn_pre, w_ffn_up, w_ffn_down, norm_ffn_post, loss_target, m_norm_mix_pre, m_w_in, m_conv_dw_w, m_conv_dw_b, m_conv_ln_g, m_conv_ln_b, m_w_conv_branch, m_b_conv_branch, m_w_att_branch, m_w_out, m_norm_mix_post, m_norm_ffn_pre, m_w_ffn_up, m_w_ffn_down, m_norm_ffn_post, v_norm_mix_pre, v_w_in, v_conv_dw_w, v_conv_dw_b, v_conv_ln_g, v_conv_ln_b, v_w_conv_branch, v_b_conv_branch, v_w_att_branch, v_w_out, v_norm_mix_post, v_norm_ffn_pre, v_w_ffn_up, v_w_ffn_down, v_norm_ffn_post):
    ws = dict(zip(WEIGHTS, [norm_mix_pre, w_in, conv_dw_w, conv_dw_b, conv_ln_g, conv_ln_b, w_conv_branch,
                            b_conv_branch, w_att_branch, w_out, norm_mix_post, norm_ffn_pre, w_ffn_up, w_ffn_down,
                            norm_ffn_post]))
    ms = dict(zip(WEIGHTS, [m_norm_mix_pre, m_w_in, m_conv_dw_w, m_conv_dw_b, m_conv_ln_g, m_conv_ln_b,
                            m_w_conv_branch, m_b_conv_branch, m_w_att_branch, m_w_out, m_norm_mix_post,
                            m_norm_ffn_pre, m_w_ffn_up, m_w_ffn_down, m_norm_ffn_post]))
    vs = dict(zip(WEIGHTS, [v_norm_mix_pre, v_w_in, v_conv_dw_w, v_conv_dw_b, v_conv_ln_g, v_conv_ln_b,
                            v_w_conv_branch, v_b_conv_branch, v_w_att_branch, v_w_out, v_norm_mix_post,
                            v_norm_ffn_pre, v_w_ffn_up, v_w_ffn_down, v_norm_ffn_post]))

    dw_block = jnp.pad(conv_dw_w, ((0, 1), (0, 0)))
    g_in, g_dw = _exchange_call("gather_first", _gather_exchange([w_in.astype(BF16), dw_block]))
    full = {"w_in": _full_weight("w_in", g_in), "conv_dw_w": _cols_to_full(g_dw)}
    for nm in VECTORS:
        full[nm] = ws[nm].reshape(1, -1)

    loss_local, grad_x, received, grads = _local_step(
        x[0], loss_target[0], full, {nm: ws[nm].astype(BF16) for nm in LATE})

    loss_at = (VECTORS.index("conv_dw_b"), CONV_DIM)
    small = _exchange_call("gather_small_grads", _gather_exchange(
        [_pack_vectors(grads).at[loss_at].set(loss_local), grads["conv_dw_w"]]))
    out_g, out_d, out_m, out_v = {}, {}, {}, {}
    for nm in LATE + ["w_in"]:
        out_g[nm], out_d[nm], out_m[nm], out_v[nm] = _sum_adamw("adamw_" + nm, received[nm], ws[nm], ms[nm], vs[nm])
    sizes = {nm: ws[nm].size for nm in VECTORS}
    vec = _sum_adamw("adamw_vectors", small[0], _pack_vectors(ws), _pack_vectors(ms), _pack_vectors(vs))
    for res, dst in zip(vec, (out_g, out_d, out_m, out_v)):
        dst.update(_unpack_vectors(res, sizes))
    loss = vec[0][loss_at]
    dw_full = _sum_parts("sum_dw_grads", small[1])
    me = _slot(*_place())
    dw_mine = lax.dynamic_slice(dw_full, (0, me * (CONV_DIM // N_DEV)), (CONV_WIDTH, CONV_DIM // N_DEV))
    nm = "conv_dw_w"
    out_g[nm], out_d[nm], out_m[nm], out_v[nm] = _sum_adamw("adamw_dw", dw_mine[None], ws[nm], ms[nm], vs[nm])

    outs = [loss, grad_x[None]]
    for group in (out_g, out_d, out_m, out_v):
        outs += [group[nm] for nm in WEIGHTS]
    return tuple(outs)
```

```python
import math

import jax
import jax.numpy as jnp
from jax import lax
from jax.experimental import pallas as pl
from jax.experimental.pallas import tpu as pltpu

F32 = jnp.float32
BF16 = jnp.bfloat16

N_DEV = 8
D_MODEL = 1024
CONV_DIM = 512
CONV_WIDTH = 31
N_HEADS = 8
HEAD_DIM = 64
ATT_DIM = N_HEADS * HEAD_DIM
D_FF = 2816
EPS = 1e-6
IN_SPLITS = (0, 1024, 1536, 2048, 2560, 3584, 4608)

ADAM_LR = 0.001
ADAM_B1 = 0.9
ADAM_B2 = 0.999
ADAM_EPS = 1e-08
ADAM_WD = 0.01
ADAM_STEP = 10

LANES = 128
SUBLANES = 8
HALO = 32
ATT_TILE = 256
DEAD_SUM = -120.0
VMEM_LIMIT = 56 * 1024 * 1024
MESH = pl.DeviceIdType.MESH
ANY = pl.BlockSpec(memory_space=pl.ANY)


def _pick(dim, target, align=LANES):
    t = min(dim, target)
    t -= t % align
    while t >= align:
        if dim % t == 0:
            return t
        t -= align
    return dim


def _params(semantics):
    return pltpu.CompilerParams(dimension_semantics=semantics, vmem_limit_bytes=VMEM_LIMIT)


def _matmul(a, b, *, name, ta=False, tb=False, out_dtype=F32):
    m, k = (a.shape[1], a.shape[0]) if ta else a.shape
    n, k2 = b.shape if tb else (b.shape[1], b.shape[0])
    assert k == k2, (a.shape, b.shape, ta, tb)
    tm, tn, tk = _pick(m, 1408 if ta else 512), _pick(n, 1536), _pick(k, 1536)
    nk = k // tk
    dims = (((0 if ta else 1,), (1 if tb else 0,)), ((), ()))

    def body(a_ref, b_ref, o_ref, *acc):
        part = lax.dot_general(a_ref[...], b_ref[...], dims, preferred_element_type=F32)
        if nk == 1:
            o_ref[...] = part.astype(o_ref.dtype)
            return
        acc_ref, = acc
        kk = pl.program_id(2)

        @pl.when(kk == 0)
        def _():
            acc_ref[...] = part

        @pl.when((kk > 0) & (kk < nk - 1))
        def _():
            acc_ref[...] += part

        @pl.when(kk == nk - 1)
        def _():
            o_ref[...] = (acc_ref[...] + part).astype(o_ref.dtype)

    a_spec = pl.BlockSpec((tk, tm), lambda j, i, kk: (kk, i)) if ta else pl.BlockSpec((tm, tk), lambda j, i, kk: (i, kk))
    b_spec = (pl.BlockSpec((tn, tk), lambda j, i, kk: (j, kk)) if tb
              else pl.BlockSpec((tk, tn), lambda j, i, kk: (kk, j)))
    return pl.pallas_call(
        body, name=name, grid=(n // tn, m // tm, nk),
        in_specs=[a_spec, b_spec],
        out_specs=pl.BlockSpec((tm, tn), lambda j, i, kk: (i, j)),
        out_shape=jax.ShapeDtypeStruct((m, n), out_dtype),
        scratch_shapes=[pltpu.VMEM((tm, tn), F32)] if nk > 1 else [],
        compiler_params=_params(("parallel", "parallel", "arbitrary")),
    )(a, b)


NO_EXCHANGE = ([], [], [], None)


def _sweep_marks(nt):
    i = pl.program_id(0)
    return i == 0, i == (3 * nt) // 4, i == nt - 1


def _rowwise(name, fn, rows, bcasts, row_outs, red_outs=(), tm=256, exchange=NO_EXCHANGE):
    s = rows[0].shape[0]
    tm = _pick(s, tm, 16)
    nt = s // tm
    resident = pl.Buffered(1)
    nr, nb, no, nd = len(rows), len(bcasts), len(row_outs), len(red_outs)
    x_arrs, x_shape, x_scratch, _ = exchange
    nx = len(x_arrs)
    first_out = nr + nb + nx

    def body(*refs):
        finish_exchange = _carry_exchange(exchange, refs, nr + nb, no + nd, *_sweep_marks(nt))
        ins = [r[...] for r in refs[:nr + nb]]
        outs, reds = fn(*ins)
        for ref, val in zip(refs[first_out:first_out + no], outs):
            ref[...] = val.astype(ref.dtype)
        i = pl.program_id(0)
        for ref, val in zip(refs[first_out + no:first_out + no + nd], reds):
            @pl.when(i == 0)
            def _():
                ref[...] = val

            @pl.when(i > 0)
            def _():
                ref[...] += val
        finish_exchange()

    in_specs = [pl.BlockSpec((tm, r.shape[1]), lambda i: (i, 0)) for r in rows]
    in_specs += [pl.BlockSpec(b.shape, lambda i: (0, 0), pipeline_mode=resident) for b in bcasts]
    out_specs = [pl.BlockSpec((tm, o.shape[1]), lambda i: (i, 0)) for o in row_outs]
    out_specs += [pl.BlockSpec(d.shape, lambda i: (0, 0)) for d in red_outs]
    return pl.pallas_call(
        body, name=name, grid=(nt,), in_specs=in_specs + [ANY] * nx, out_specs=out_specs + [ANY] * nx,
        out_shape=list(row_outs) + list(red_outs) + x_shape, scratch_shapes=x_scratch,
        compiler_params=_params(("arbitrary",)),
    )(*rows, *bcasts, *x_arrs)


def _sds(shape, dtype=F32):
    return jax.ShapeDtypeStruct(shape, dtype)


def _rms(x, g):
    y = x * lax.rsqrt(jnp.mean(x * x, axis=-1, keepdims=True) + EPS)
    return y * g


def _silu(x):
    return x * jax.nn.sigmoid(x)


def _swiglu(g, u):
    return _silu(g) * u


def _ln_silu(u, g, b):
    mu = jnp.mean(u, axis=-1, keepdims=True)
    var = jnp.mean(jnp.square(u - mu), axis=-1, keepdims=True)
    return _silu((u - mu) * lax.rsqrt(var + EPS) * g + b)


def _merge(conv_pre, att_out, g_conv, g_att, b_cb):
    return jax.nn.sigmoid(g_conv) * (conv_pre + b_cb) + jax.nn.sigmoid(g_att) * att_out


def _glu(t):
    return t[:, :CONV_DIM] * jax.nn.sigmoid(t[:, CONV_DIM:])


def _shifted_reader(buf, shifted, tm):
    for b in range(1, SUBLANES):
        shifted[b - 1, :, :] = buf[pl.ds(b, tm + HALO - SUBLANES), :]

    def read(o):
        a, b = divmod(o, SUBLANES)
        return buf[pl.ds(SUBLANES * a, tm), :] if b == 0 else shifted[b - 1, pl.ds(SUBLANES * a, tm), :]

    return read


def _conv_fwd(conv_in, w_pad, b, ln_g, ln_b, exchange, tm=256):
    s = conv_in.shape[0]
    tm = _pick(s, tm, HALO)
    ratio = tm // HALO
    x_arrs, x_shape, x_scratch, _ = exchange
    nx = len(x_arrs)

    def body(*refs):
        main_ref, halo_ref, w_ref, b_ref, g_ref, be_ref = refs[:6]
        u3_ref, u1_ref = refs[6 + nx:8 + nx]
        buf, shifted = refs[-2:]
        finish_exchange = _carry_exchange(exchange, refs, 6, 2, *_sweep_marks(s // tm))
        i = pl.program_id(0)
        buf[0:HALO, :] = _glu(halo_ref[...]) * (i > 0).astype(F32)
        buf[HALO:HALO + tm, :] = _glu(main_ref[...])
        read = _shifted_reader(buf, shifted, tm)
        acc = jnp.zeros((tm, CONV_DIM), F32) + b_ref[...]
        for j in range(CONV_WIDTH):
            acc = acc + w_ref[j:j + 1, :] * read(HALO - (CONV_WIDTH - 1) + j)
        u1_ref[...] = acc
        u3_ref[...] = _ln_silu(acc, g_ref[...], be_ref[...]).astype(u3_ref.dtype)
        finish_exchange()

    res = pl.pallas_call(
        body, name="conv_fwd", grid=(s // tm,),
        in_specs=[pl.BlockSpec((tm, 2 * CONV_DIM), lambda i: (i, 0)),
                  pl.BlockSpec((HALO, 2 * CONV_DIM), lambda i: (jnp.maximum(i * ratio - 1, 0), 0)),
                  pl.BlockSpec(w_pad.shape, lambda i: (0, 0)),
                  pl.BlockSpec(b.shape, lambda i: (0, 0)),
                  pl.BlockSpec(ln_g.shape, lambda i: (0, 0)),
                  pl.BlockSpec(ln_b.shape, lambda i: (0, 0))] + [ANY] * nx,
        out_specs=[pl.BlockSpec((tm, CONV_DIM), lambda i: (i, 0)),
                   pl.BlockSpec((tm, CONV_DIM), lambda i: (i, 0))] + [ANY] * nx,
        out_shape=[_sds((s, CONV_DIM), BF16), _sds((s, CONV_DIM), F32)] + x_shape,
        scratch_shapes=x_scratch + [pltpu.VMEM((tm + HALO, CONV_DIM), F32),
                                    pltpu.VMEM((SUBLANES - 1, tm + HALO - SUBLANES, CONV_DIM), F32)],
        compiler_params=_params(("arbitrary",)),
    )(conv_in, conv_in, w_pad, b, ln_g, ln_b, *x_arrs)
    return res[0], res[1], res[2:]


def _conv_bwd(conv_in, du1, w_pad, exchange, tm=256):
    s = conv_in.shape[0]
    tm = _pick(s, tm, HALO)
    ratio = tm // HALO
    nt = s // tm
    last_halo = s // HALO - 1
    x_arrs, x_shape, x_scratch, _ = exchange
    nx = len(x_arrs)

    def body(*refs):
        main_ref, halo_ref, du_ref, dun_ref, w_ref = refs[:5]
        dci_ref, dw_ref, db_ref = refs[5 + nx:8 + nx]
        ubuf, dbuf, ushift, dshift = refs[-4:]
        finish_exchange = _carry_exchange(exchange, refs, 5, 3, *_sweep_marks(nt))
        i = pl.program_id(0)
        main = main_ref[...]
        a = main[:, :CONV_DIM]
        sb = jax.nn.sigmoid(main[:, CONV_DIM:])
        ubuf[0:HALO, :] = _glu(halo_ref[...]) * (i > 0).astype(F32)
        ubuf[HALO:HALO + tm, :] = a * sb
        du = du_ref[...]
        dbuf[0:tm, :] = du
        dbuf[tm:tm + HALO, :] = dun_ref[...] * (i < nt - 1).astype(F32)

        @pl.when(i == 0)
        def _():
            dw_ref[...] = jnp.zeros_like(dw_ref)
            db_ref[...] = jnp.zeros_like(db_ref)

        read_u = _shifted_reader(ubuf, ushift, tm)
        read_d = _shifted_reader(dbuf, dshift, tm)
        du0 = jnp.zeros((tm, CONV_DIM), F32)
        for j in range(CONV_WIDTH):
            du0 = du0 + w_ref[j:j + 1, :] * read_d(CONV_WIDTH - 1 - j)
            dw_ref[j:j + 1, :] += jnp.sum(du * read_u(HALO - (CONV_WIDTH - 1) + j), axis=0, keepdims=True)
        db_ref[...] += jnp.sum(du, axis=0, keepdims=True)
        dci_ref[:, :CONV_DIM] = (du0 * sb).astype(dci_ref.dtype)
        dci_ref[:, CONV_DIM:] = (du0 * a * sb * (1.0 - sb)).astype(dci_ref.dtype)
        finish_exchange()

    res = pl.pallas_call(
        body, name="conv_bwd", grid=(nt,),
        in_specs=[pl.BlockSpec((tm, 2 * CONV_DIM), lambda i: (i, 0)),
                  pl.BlockSpec((HALO, 2 * CONV_DIM), lambda i: (jnp.maximum(i * ratio - 1, 0), 0)),
                  pl.BlockSpec((tm, CONV_DIM), lambda i: (i, 0)),
                  pl.BlockSpec((HALO, CONV_DIM), lambda i: (jnp.minimum((i + 1) * ratio, last_halo), 0)),
                  pl.BlockSpec(w_pad.shape, lambda i: (0, 0))] + [ANY] * nx,
        out_specs=[pl.BlockSpec((tm, 2 * CONV_DIM), lambda i: (i, 0)),
                   pl.BlockSpec(w_pad.shape, lambda i: (0, 0)),
                   pl.BlockSpec((1, CONV_DIM), lambda i: (0, 0))] + [ANY] * nx,
        out_shape=[_sds((s, 2 * CONV_DIM), BF16), _sds(w_pad.shape), _sds((1, CONV_DIM))] + x_shape,
        scratch_shapes=x_scratch + [pltpu.VMEM((tm + HALO, CONV_DIM), F32)] * 2
        + [pltpu.VMEM((SUBLANES - 1, tm + HALO - SUBLANES, CONV_DIM), F32)] * 2,
        compiler_params=_params(("arbitrary",)),
    )(conv_in, conv_in, du1, du1, w_pad, *x_arrs)
    return res[0], res[1], res[2], res[3:]


def _logsig_neg(z):
    return jnp.minimum(-z, 0.0) - jnp.log(1.0 + jnp.exp(-jnp.abs(z)))


def _split_dot(val, tri):
    hi = val.astype(BF16)
    lo = (val - hi.astype(F32)).astype(BF16)
    return jnp.dot(hi, tri, preferred_element_type=F32) + jnp.dot(lo, tri, preferred_element_type=F32)


def _attn_masks(t, later):
    row = lax.broadcasted_iota(jnp.int32, (t, t), 0)
    col = lax.broadcasted_iota(jnp.int32, (t, t), 1)
    tri = jnp.where(row > col if later else row <= col, 1.0, 0.0).astype(BF16)
    return col < row, tri


def _grid_marks(h, nq):
    hh, i = pl.program_id(0), pl.program_id(1)
    return (hh == 0) & (i == 0), (hh == (3 * h) // 4) & (i == 0), (hh == h - 1) & (i == nq - 1)


def _head_masks(shape):
    lane = lax.broadcasted_iota(jnp.int32, shape, len(shape) - 1)
    return lane < HEAD_DIM, lane >= HEAD_DIM


def _per_head(blk):
    m0, m1 = _head_masks(blk.shape)
    zero = jnp.zeros_like(blk)
    return jnp.where(m0, blk, zero), jnp.where(m1, blk, zero)


NT = (((1,), (1,)), ((), ()))
TN = (((0,), (0,)), ((), ()))


def _attn_fwd(q, k, v, exchange):
    s = q.shape[0]
    hp = q.shape[1] // LANES
    t = ATT_TILE
    scale = 1.0 / math.sqrt(HEAD_DIM)
    x_arrs, x_shape, x_scratch, _ = exchange
    nx = len(x_arrs)

    def body(*refs):
        q_ref, k_ref, v_ref = refs[:3]
        o_ref, lt_ref, nb_ref = refs[3 + nx:6 + nx]
        finish_exchange = _carry_exchange(exchange, refs, 3, 3, *_grid_marks(hp, s // t))
        i = pl.program_id(1)
        qs = _per_head((q_ref[...].astype(F32) * scale).astype(BF16))
        causal, tri = _attn_masks(t, later=True)

        def step(kb, carry, masked):
            cs, acc = carry
            off = pl.multiple_of(kb * t, t)
            kblk = k_ref[pl.ds(off, t), :]
            vs = _per_head(v_ref[pl.ds(off, t), :])
            new_cs = []
            for hd in range(2):
                z = lax.dot_general(qs[hd], kblk, NT, preferred_element_type=F32)
                l = _logsig_neg(z)
                if masked:
                    l = jnp.where(causal, l, 0.0)
                e = z + l + _split_dot(l, tri) + cs[hd]
                if masked:
                    e = jnp.where(causal, e, -1e30)
                acc = acc + jnp.dot(jnp.exp(e).astype(BF16), vs[hd], preferred_element_type=F32)
                new_cs.append(cs[hd] + jnp.sum(l, axis=1, keepdims=True))
            return tuple(new_cs), acc

        zero = jnp.zeros((t, 1), F32)
        carry = step(i, ((zero, zero), jnp.zeros((t, LANES), F32)), True)

        def more(state):
            n, (cs, _) = state
            return (n < i) & (jnp.maximum(jnp.max(cs[0]), jnp.max(cs[1])) > DEAD_SUM)

        n_blocks, carry = lax.while_loop(more, lambda st: (st[0] + 1, step(i - 1 - st[0], st[1], False)),
                                         (jnp.int32(0), carry))
        m0, _ = _head_masks((t, LANES))
        lt_ref[...] = jnp.where(m0, carry[0][0], carry[0][1])
        o_ref[...] = carry[1].astype(o_ref.dtype)
        nb_ref[pl.program_id(0), i] = n_blocks.astype(F32)
        finish_exchange()

    res = pl.pallas_call(
        body, name="attn_fwd", grid=(hp, s // t),
        in_specs=[pl.BlockSpec((t, LANES), lambda p, i: (i, p)),
                  pl.BlockSpec((s, LANES), lambda p, i: (0, p)),
                  pl.BlockSpec((s, LANES), lambda p, i: (0, p))] + [ANY] * nx,
        out_specs=[pl.BlockSpec((t, LANES), lambda p, i: (i, p)),
                   pl.BlockSpec((None, t, LANES), lambda p, i: (p, i, 0)),
                   pl.BlockSpec(memory_space=pltpu.SMEM)] + [ANY] * nx,
        out_shape=[_sds(q.shape, BF16), _sds((hp, s, LANES), F32), _sds((hp, s // t), F32)] + x_shape,
        scratch_shapes=x_scratch,
        compiler_params=_params(("arbitrary", "arbitrary")),
    )(q, k, v, *x_arrs)
    return res[0], res[1], res[2], res[3:]


def _attn_bwd(q, k, v, do, ltot, n_blocks, exchange):
    s = q.shape[0]
    hp = q.shape[1] // LANES
    t = ATT_TILE
    scale = 1.0 / math.sqrt(HEAD_DIM)
    x_arrs, x_shape, x_scratch, _ = exchange
    nx = len(x_arrs)

    def body(*refs):
        q_ref, k_ref, v_ref, do_ref, lt_ref, nb_ref = refs[:6]
        dq_ref, dk_ref, dv_ref = refs[6 + nx:9 + nx]
        finish_exchange = _carry_exchange(exchange, refs, 6, 3, *_grid_marks(hp, s // t))
        i = pl.program_id(1)
        first = jnp.clip(i - nb_ref[pl.program_id(0), i].astype(jnp.int32), 0, i)

        @pl.when(i == 0)
        def _():
            dk_ref[...] = jnp.zeros_like(dk_ref)
            dv_ref[...] = jnp.zeros_like(dv_ref)

        qb = q_ref[...]
        qm = _per_head(qb)
        qs = _per_head((qb.astype(F32) * scale).astype(BF16))
        dos = _per_head(do_ref[...])
        lts = (lt_ref[:, 0:1], lt_ref[:, HEAD_DIM:HEAD_DIM + 1])
        causal, tri = _attn_masks(t, later=False)

        def step(kb, carry, masked):
            cls, cgs, dq = carry
            off = pl.multiple_of(kb * t, t)
            kblk = k_ref[pl.ds(off, t), :]
            vblk = v_ref[pl.ds(off, t), :]
            ks = _per_head(kblk)
            dk = jnp.zeros((t, LANES), F32)
            dv = jnp.zeros((t, LANES), F32)
            new_cls, new_cgs = [], []
            for hd in range(2):
                z = lax.dot_general(qs[hd], kblk, NT, preferred_element_type=F32)
                l = _logsig_neg(z)
                if masked:
                    l = jnp.where(causal, l, 0.0)
                e = z + l + ((lts[hd] - cls[hd]) - _split_dot(l, tri))
                if masked:
                    e = jnp.where(causal, e, -1e30)
                a = jnp.exp(e)
                g = lax.dot_general(dos[hd], vblk, NT, preferred_element_type=F32) * a
                p = cgs[hd] + jnp.dot(g.astype(BF16), tri, preferred_element_type=F32) - g
                el = jnp.exp(l)
                dz = g * el - p * (1.0 - el)
                if masked:
                    dz = jnp.where(causal, dz, 0.0)
                dzb = (dz * scale).astype(BF16)
                dq = dq + jnp.dot(dzb, ks[hd], preferred_element_type=F32)
                dk = dk + lax.dot_general(dzb, qm[hd], TN, preferred_element_type=F32)
                dv = dv + lax.dot_general(a.astype(BF16), dos[hd], TN, preferred_element_type=F32)
                new_cls.append(cls[hd] + jnp.sum(l, axis=1, keepdims=True))
                new_cgs.append(cgs[hd] + jnp.sum(g, axis=1, keepdims=True))
            dk_ref[pl.ds(off, t), :] += dk
            dv_ref[pl.ds(off, t), :] += dv
            return tuple(new_cls), tuple(new_cgs), dq

        zero = jnp.zeros((t, 1), F32)
        init = ((zero, zero), (zero, zero), jnp.zeros((t, LANES), F32))
        carry = lax.fori_loop(first, i, lambda kb, cr: step(kb, cr, False), init)
        carry = step(i, carry, True)
        dq_ref[...] = carry[2]
        finish_exchange()

    blk = pl.BlockSpec((t, LANES), lambda p, i: (i, p))
    whole = pl.BlockSpec((s, LANES), lambda p, i: (0, p))
    res = pl.pallas_call(
        body, name="attn_bwd", grid=(hp, s // t),
        in_specs=[blk, whole, whole, blk, pl.BlockSpec((None, t, LANES), lambda p, i: (p, i, 0)),
                  pl.BlockSpec(memory_space=pltpu.SMEM)] + [ANY] * nx,
        out_specs=[blk, whole, whole] + [ANY] * nx,
        out_shape=[_sds(q.shape)] * 3 + x_shape,
        scratch_shapes=x_scratch,
        compiler_params=_params(("arbitrary", "arbitrary")),
    )(q, k, v, do, ltot, n_blocks, *x_arrs)
    return res[0], res[1], res[2], res[3:]


LATE = ["w_conv_branch", "w_att_branch", "w_out", "w_ffn_up", "w_ffn_down"]


def _full_weight(name, gathered):
    return _cols_to_full(gathered) if name in COL_SHARDED else gathered.reshape(-1, gathered.shape[2])


def _grad_slabs(name, grad):
    return _full_to_cols(grad) if name in COL_SHARDED else grad.reshape(N_DEV, -1, grad.shape[1])


def _side_slabs(name, grad):
    slabs = _grad_slabs(name, grad)
    return slabs.reshape((4, 2) + slabs.shape[1:])


def _local_step(x, target, w, late_blocks):
    s = x.shape[0]
    w = dict(w)
    g1, g2, g3, g4 = w["norm_mix_pre"], w["norm_mix_post"], w["norm_ffn_pre"], w["norm_ffn_post"]

    w_in = w["w_in"]

    def proj_fn(xt, g1_, w_in_t):
        h = _rms(xt, g1_).astype(BF16)
        proj = lax.dot_general(h, w_in_t, NT, preferred_element_type=F32)
        return (h, *[proj[:, IN_SPLITS[n]:IN_SPLITS[n + 1]] for n in range(6)]), ()

    mix_weights = ["w_conv_branch", "w_att_branch", "w_out"]
    h1, conv_in, q, k, v, g_conv, g_att, *gathered = _rowwise(
        "norm_proj", proj_fn, [x], [g1, w_in],
        [_sds((s, D_MODEL), BF16), _sds((s, 2 * CONV_DIM)), _sds((s, ATT_DIM), BF16), _sds((s, ATT_DIM), BF16),
         _sds((s, ATT_DIM), BF16), _sds((s, D_MODEL)), _sds((s, D_MODEL))], tm=512,
        exchange=_gather_exchange([late_blocks[nm] for nm in mix_weights]))
    for nm, g in zip(mix_weights, gathered):
        w[nm] = _full_weight(nm, g)

    u3, u1, (g_down,) = _conv_fwd(conv_in, w["conv_dw_w"], w["conv_dw_b"], w["conv_ln_g"], w["conv_ln_b"],
                                  _gather_exchange([late_blocks["w_ffn_down"]]))
    w["w_ffn_down"] = _full_weight("w_ffn_down", g_down)
    att, ltot, n_blocks, (g_up,) = _attn_fwd(q, k, v, _gather_exchange([late_blocks["w_ffn_up"]]))
    w["w_ffn_up"] = _full_weight("w_ffn_up", g_up)

    def merge_fn(u3t, at, gc, ga, w_cb, w_ab, b_cb):
        cp = jnp.dot(u3t, w_cb, preferred_element_type=F32)
        ao = jnp.dot(at, w_ab, preferred_element_type=F32)
        return (_merge(cp, ao, gc, ga, b_cb), cp, ao), ()

    merged, conv_pre, att_out = _rowwise(
        "branch_merge", merge_fn, [u3, att, g_conv, g_att], [w["w_conv_branch"], w["w_att_branch"], w["b_conv_branch"]],
        [_sds((s, D_MODEL), BF16)] * 3, tm=512)

    def mid_fn(mt, xt, w_out, g2_, g3_):
        mix_ = jnp.dot(mt, w_out, preferred_element_type=F32)
        x2_ = xt + _rms(mix_, g2_)
        return (mix_, x2_, _rms(x2_, g3_)), ()

    mix, x2, h2 = _rowwise("mix_mid_norm", mid_fn, [merged, x], [w["w_out"], g2, g3],
                           [_sds((s, D_MODEL)), _sds((s, D_MODEL)), _sds((s, D_MODEL), BF16)], tm=512)

    def ffn_up_fn(ht, w_up_t):
        gu_ = lax.dot_general(ht, w_up_t, NT, preferred_element_type=F32)
        return (gu_, _swiglu(gu_[:, :D_FF], gu_[:, D_FF:])), ()

    gu, act = _rowwise("ffn_up", ffn_up_fn, [h2], [w["w_ffn_up"]],
                       [_sds((s, 2 * D_FF), BF16), _sds((s, D_FF), BF16)], tm=512)

    def final_fn(at, x2t, tgt, w_down, g4_):
        ff = jnp.dot(at, w_down, preferred_element_type=F32)
        n4, vjp = jax.vjp(_rms, ff, g4_)
        err = x2t + n4 - tgt
        dy = err * (1.0 / D_MODEL)
        dff, dg4 = vjp(dy)
        return (dy, dff), (jnp.sum(err * err, axis=0, keepdims=True), dg4)

    dy, dff, loss_cols, d_g4 = _rowwise("ffn_down_loss", final_fn, [act, x2, target], [w["w_ffn_down"], g4],
                                        [_sds((s, D_MODEL)), _sds((s, D_MODEL), BF16)],
                                        [_sds((1, D_MODEL)), _sds((1, D_MODEL))], tm=512)
    loss = 0.5 * jnp.sum(loss_cols) / D_MODEL

    d_w_down = _matmul(act, dff, ta=True, name="d_w_down", out_dtype=BF16)

    def act_bwd_fn(dfft, gut, w_down):
        d_act = lax.dot_general(dfft, w_down, NT, preferred_element_type=F32)
        gu_ = gut.astype(F32)
        _, vjp = jax.vjp(_swiglu, gu_[:, :D_FF], gu_[:, D_FF:])
        return (jnp.concatenate(vjp(d_act), axis=1),), ()

    down_slabs = _side_slabs("w_ffn_down", d_w_down)
    dgu, theirs = _rowwise("ffn_act_bwd", act_bwd_fn, [dff, gu], [w["w_ffn_down"]], [_sds((s, 2 * D_FF), BF16)],
                           exchange=_pair_exchange([down_slabs]))
    down_sums = _pair_sum("pair_sum_w_ffn_down", down_slabs, theirs)
    d_w_up = _matmul(dgu, h2, ta=True, name="d_w_up", out_dtype=BF16)
    received = {}
    up_slabs = _side_slabs("w_ffn_up", d_w_up)

    def mid_bwd_fn(dgut, xt, mt, dyt, w_up_t, g2_, g3_):
        dh = jnp.dot(dgut, w_up_t, preferred_element_type=F32)
        n2, vjp2 = jax.vjp(_rms, mt, g2_)
        x2_ = xt + n2
        _, vjp3 = jax.vjp(_rms, x2_, g3_)
        dx2_, dg3 = vjp3(dh)
        dx2_ = dx2_ + dyt
        dmix_, dg2 = vjp2(dx2_)
        return (dx2_, dmix_), (dg2, dg3)

    dx2, dmix, d_g2, d_g3, received["w_ffn_down"] = _rowwise(
        "ffn_up_mid_bwd", mid_bwd_fn, [dgu, x, mix, dy], [w["w_ffn_up"], g2, g3],
        [_sds((s, D_MODEL)), _sds((s, D_MODEL), BF16)], [_sds((1, D_MODEL)), _sds((1, D_MODEL))], tm=512,
        exchange=_chip_exchange([down_sums]))
    d_w_out = _matmul(merged, dmix, ta=True, name="d_w_out", out_dtype=BF16)

    def merge_bwd_fn(dmt, cp, ao, gc, ga, w_out, w_cb, w_ab, b_cb):
        dm = lax.dot_general(dmt, w_out, NT, preferred_element_type=F32)
        _, vjp = jax.vjp(_merge, cp.astype(F32), ao.astype(F32), gc, ga, b_cb)
        dcp, dao, dgc, dga, dbias = vjp(dm)
        dcp, dao = dcp.astype(BF16), dao.astype(BF16)
        du3_ = lax.dot_general(dcp, w_cb, NT, preferred_element_type=F32)
        datt_ = lax.dot_general(dao, w_ab, NT, preferred_element_type=F32)
        return (dcp, dao, dgc, dga, du3_, datt_), (dbias,)

    d_conv_out, d_att_out, d_g_conv, d_g_att, du3, d_att, d_b_cb, theirs = _rowwise(
        "merge_bwd", merge_bwd_fn, [dmix, conv_pre, att_out, g_conv, g_att],
        [w["w_out"], w["w_conv_branch"], w["w_att_branch"], w["b_conv_branch"]],
        [_sds((s, D_MODEL), BF16)] * 4 + [_sds((s, CONV_DIM)), _sds((s, ATT_DIM), BF16)], [_sds((1, D_MODEL))], tm=512,
        exchange=_pair_exchange([up_slabs]))

    d_w_cb = _matmul(u3, d_conv_out, ta=True, name="d_w_conv_branch", out_dtype=BF16)
    d_w_ab = _matmul(att, d_att_out, ta=True, name="d_w_att_branch", out_dtype=BF16)

    dq, dk, dv, (received["w_ffn_up"],) = _attn_bwd(
        q, k, v, d_att, ltot, n_blocks, _chip_exchange([_pair_sum("pair_sum_w_ffn_up", up_slabs, theirs)]))

    def ln_bwd_fn(u1t, du3t, g_, b_):
        _, vjp = jax.vjp(_ln_silu, u1t, g_, b_)
        du1_, dg_, db_ = vjp(du3t)
        return (du1_,), (dg_, db_)

    du1, d_ln_g, d_ln_b = _rowwise("conv_ln_bwd", ln_bwd_fn, [u1, du3], [w["conv_ln_g"], w["conv_ln_b"]],
                                   [_sds((s, CONV_DIM))], [_sds((1, CONV_DIM)), _sds((1, CONV_DIM))])
    mix_grads = {"w_conv_branch": d_w_cb, "w_att_branch": d_w_ab, "w_out": d_w_out}
    d_conv_in, d_dw_w, d_dw_b, landed = _conv_bwd(
        conv_in, du1, w["conv_dw_w"], _scatter_exchange([_grad_slabs(nm, mix_grads[nm]) for nm in mix_weights]))
    received.update(zip(mix_weights, landed))

    d_proj = jnp.concatenate([d_conv_in, dq.astype(BF16), dk.astype(BF16), dv.astype(BF16), d_g_conv, d_g_att],
                             axis=1)
    d_w_in = _matmul(d_proj, h1, ta=True, name="d_w_in", out_dtype=BF16)
    in_slabs = _side_slabs("w_in", d_w_in)
    (theirs,) = _exchange_call("pair_swap_w_in", _pair_exchange([in_slabs]))

    def pre_bwd_fn(dpt, xt, dx2t, w_in_t, g_):
        dh = jnp.dot(dpt, w_in_t, preferred_element_type=F32)
        _, vjp = jax.vjp(_rms, xt, g_)
        dx_, dg_ = vjp(dh)
        return (dx_ + dx2t,), (dg_,)

    grad_x, d_g1, received["w_in"] = _rowwise(
        "proj_norm_bwd", pre_bwd_fn, [d_proj, x, dx2], [w_in, g1], [_sds((s, D_MODEL))], [_sds((1, D_MODEL))], tm=512,
        exchange=_chip_exchange([_pair_sum("pair_sum_w_in", in_slabs, theirs)]))

    grads = {
        "norm_mix_pre": d_g1, "conv_dw_w": d_dw_w, "conv_dw_b": d_dw_b,
        "conv_ln_g": d_ln_g, "conv_ln_b": d_ln_b, "b_conv_branch": d_b_cb,
        "norm_mix_post": d_g2, "norm_ffn_pre": d_g3, "norm_ffn_post": d_g4,
    }
    return loss, grad_x, received, grads


def _place():
    x, y, c = lax.axis_index("x"), lax.axis_index("y"), lax.axis_index("c")
    return x, y, c


def _slot(px, py, pc):
    return 4 * px + 2 * py + pc


def _exchange_scratch(n):
    return [pltpu.SemaphoreType.DMA((7 * n,)), pltpu.SemaphoreType.DMA((7 * n,)), pltpu.SemaphoreType.DMA((n,))]


def _gather_exchange(arrs):
    n = len(arrs)

    def phases(ins, outs, send_sems, recv_sems, local_sems):
        x, y, c = _place()
        me, sibling = (x, y, c), (x, y, 1 - c)
        chips = [(1 - x, y), (x, 1 - y), (1 - x, 1 - y)]

        def copy(a, kk, block, to, src=None):
            dst = outs[a].at[_slot(*block)]
            return pltpu.make_async_remote_copy(
                src_ref=dst if src is None else src, dst_ref=dst,
                send_sem=send_sems.at[a * 7 + kk], recv_sem=recv_sems.at[a * 7 + kk],
                device_id=to, device_id_type=MESH)

        mine = [pltpu.make_async_copy(ins[a], outs[a].at[_slot(*me)], local_sems.at[a]) for a in range(n)]
        first = []
        for a in range(n):
            first.append(copy(a, 0, me, sibling, src=ins[a]))
            first += [copy(a, 1 + j, me, (*chip, c), src=ins[a]) for j, chip in enumerate(chips)]
        passed = [copy(a, 4 + j, (*chip, c), sibling) for j, chip in enumerate(chips) for a in range(n)]

        def send():
            for cp in mine + first:
                cp.start()

        def pass_on():
            for j, chip in enumerate(chips):
                for a in range(n):
                    copy(a, 1 + j, (*chip, c), me).wait_recv()
                    passed[j * n + a].start()

        def finish():
            for a in range(n):
                copy(a, 0, sibling, me).wait_recv()
                for j, chip in enumerate(chips):
                    copy(a, 4 + j, (*chip, 1 - c), me).wait_recv()
            for cp in first + passed:
                cp.wait_send()
            for cp in mine:
                cp.wait()

        return [send, pass_on, finish]

    return list(arrs), [_sds((N_DEV,) + a.shape, a.dtype) for a in arrs], _exchange_scratch(n), phases


def _scatter_exchange(arrs):
    n = len(arrs)
    flips = [(fx, fy, fc) for fx in (0, 1) for fy in (0, 1) for fc in (0, 1)][1:]

    def phases(ins, outs, send_sems, recv_sems, local_sems):
        x, y, c = _place()
        mine = _slot(x, y, c)
        local = [pltpu.make_async_copy(ins[a].at[mine], outs[a].at[mine], local_sems.at[a]) for a in range(n)]
        peers = [((1 - x) if fx else x, (1 - y) if fy else y, (1 - c) if fc else c) for fx, fy, fc in flips]

        def copy(a, kk, src_slot, dst_slot):
            return pltpu.make_async_remote_copy(
                src_ref=ins[a].at[src_slot], dst_ref=outs[a].at[dst_slot],
                send_sem=send_sems.at[a * 7 + kk], recv_sem=recv_sems.at[a * 7 + kk],
                device_id=peers[kk], device_id_type=MESH)

        sends = [copy(a, kk, _slot(*peers[kk]), mine) for a in range(n) for kk in range(7)]

        def send():
            for cp in local + sends:
                cp.start()

        def finish():
            for a in range(n):
                for kk in range(7):
                    copy(a, kk, mine, _slot(*peers[kk])).wait_recv()
            for cp in sends:
                cp.wait_send()
            for cp in local:
                cp.wait()

        return [send, finish]

    return list(arrs), [_sds(a.shape, a.dtype) for a in arrs], _exchange_scratch(n), phases


def _pair_exchange(arrs):
    n = len(arrs)

    def phases(ins, outs, send_sems, recv_sems, local_sems):
        x, y, c = _place()

        def copy(a, chip, side):
            return pltpu.make_async_remote_copy(
                src_ref=ins[a].at[chip, side], dst_ref=outs[a].at[chip],
                send_sem=send_sems.at[a * 7 + chip], recv_sem=recv_sems.at[a * 7 + chip],
                device_id=(x, y, 1 - c), device_id_type=MESH)

        sends = [copy(a, chip, 1 - c) for a in range(n) for chip in range(4)]

        def send():
            for cp in sends:
                cp.start()

        def finish():
            for a in range(n):
                for chip in range(4):
                    copy(a, chip, c).wait_recv()
            for cp in sends:
                cp.wait_send()

        return [send, finish]

    return list(arrs), [_sds((4,) + a.shape[2:], a.dtype) for a in arrs], _exchange_scratch(n), phases


def _chip_exchange(arrs):
    n = len(arrs)

    def phases(ins, outs, send_sems, recv_sems, local_sems):
        x, y, c = _place()
        mine = 2 * x + y
        chips = [(1 - x, y), (x, 1 - y), (1 - x, 1 - y)]
        local = [pltpu.make_async_copy(ins[a].at[mine], outs[a].at[mine], local_sems.at[a]) for a in range(n)]

        def copy(a, j, src_slot, dst_slot):
            return pltpu.make_async_remote_copy(
                src_ref=ins[a].at[src_slot], dst_ref=outs[a].at[dst_slot],
                send_sem=send_sems.at[a * 7 + j], recv_sem=recv_sems.at[a * 7 + j],
                device_id=(*chips[j], c), device_id_type=MESH)

        sends = [copy(a, j, 2 * chips[j][0] + chips[j][1], mine) for a in range(n) for j in range(3)]

        def send():
            for cp in local + sends:
                cp.start()

        def finish():
            for a in range(n):
                for j in range(3):
                    copy(a, j, mine, 2 * chips[j][0] + chips[j][1]).wait_recv()
            for cp in sends:
                cp.wait_send()
            for cp in local:
                cp.wait()

        return [send, finish]

    return list(arrs), [_sds(a.shape, a.dtype) for a in arrs], _exchange_scratch(n), phases


def _pair_sum(name, mine, theirs):
    _, _, r, c = mine.shape

    def body(side_ref, m_ref, t_ref, o_ref):
        o_ref[...] = (m_ref[...].astype(F32) + t_ref[...].astype(F32)).astype(o_ref.dtype)

    return pl.pallas_call(
        body, name=name,
        grid_spec=pltpu.PrefetchScalarGridSpec(
            num_scalar_prefetch=1, grid=(4,),
            in_specs=[pl.BlockSpec((None, None, r, c), lambda j, side: (j, side[0], 0, 0)),
                      pl.BlockSpec((None, r, c), lambda j, side: (j, 0, 0))],
            out_specs=pl.BlockSpec((None, r, c), lambda j, side: (j, 0, 0))),
        out_shape=_sds(theirs.shape, theirs.dtype),
        compiler_params=_params(("parallel",)),
    )(lax.axis_index("c").astype(jnp.int32).reshape(1), mine, theirs)


def _exchange_call(name, exchange):
    arrs, out_shape, scratch, phases = exchange
    n = len(arrs)

    def body(*refs):
        for step in phases(refs[:n], refs[n:2 * n], *refs[2 * n:]):
            step()

    return pl.pallas_call(body, name=name, in_specs=[ANY] * n, out_specs=[ANY] * n,
                          out_shape=out_shape, scratch_shapes=scratch)(*arrs)


def _carry_exchange(exchange, refs, n_in, n_out, first, middle, last):
    arrs, _, _, phases = exchange
    n = len(arrs)
    if n == 0:
        return lambda: None
    ins = refs[n_in:n_in + n]
    outs = refs[n_in + n + n_out:n_in + 2 * n + n_out]
    sems = n_in + 2 * n + n_out
    steps = phases(ins, outs, *refs[sems:sems + 3])
    pl.when(first)(steps[0])
    if len(steps) == 3:
        pl.when(middle)(steps[1])
    return lambda: pl.when(last)(steps[-1])


def _adamw_math(w, g, m, v):
    m2 = ADAM_B1 * m + (1.0 - ADAM_B1) * g
    v2 = ADAM_B2 * v + (1.0 - ADAM_B2) * jnp.square(g)
    m_hat = m2 / (1.0 - ADAM_B1 ** ADAM_STEP)
    v_hat = v2 / (1.0 - ADAM_B2 ** ADAM_STEP)
    delta = -ADAM_LR * (m_hat / (jnp.sqrt(v_hat) + ADAM_EPS) + ADAM_WD * w)
    return delta, m2, v2


def _sum_adamw(name, parts, w, m, v, tr=256):
    p, r, c = parts.shape
    tr = _pick(r, tr, 16)

    def body(p_ref, w_ref, m_ref, v_ref, g_ref, d_ref, m2_ref, v2_ref):
        g = p_ref[0].astype(F32)
        for d in range(1, p):
            g = g + p_ref[d].astype(F32)
        delta, m2, v2 = _adamw_math(w_ref[...], g, m_ref[...], v_ref[...])
        g_ref[...] = g
        d_ref[...] = delta
        m2_ref[...] = m2
        v2_ref[...] = v2

    tile = pl.BlockSpec((tr, c), lambda i: (i, 0))
    return pl.pallas_call(
        body, name=name, grid=(r // tr,),
        in_specs=[pl.BlockSpec((p, tr, c), lambda i: (0, i, 0)), tile, tile, tile],
        out_specs=[tile] * 4, out_shape=[_sds((r, c))] * 4,
        compiler_params=_params(("parallel",)),
    )(parts, w, m, v)


def _sum_parts(name, parts):
    p, r, c = parts.shape

    def body(p_ref, o_ref):
        g = p_ref[0]
        for d in range(1, p):
            g = g + p_ref[d]
        o_ref[...] = g

    return pl.pallas_call(
        body, name=name, out_shape=_sds((r, c)),
        in_specs=[pl.BlockSpec(memory_space=pltpu.VMEM)], out_specs=pl.BlockSpec(memory_space=pltpu.VMEM),
    )(parts)


WEIGHTS = ["norm_mix_pre", "w_in", "conv_dw_w", "conv_dw_b", "conv_ln_g", "conv_ln_b", "w_conv_branch",
           "b_conv_branch", "w_att_branch", "w_out", "norm_mix_post", "norm_ffn_pre", "w_ffn_up", "w_ffn_down",
           "norm_ffn_post"]
COL_SHARDED = ["w_conv_branch", "w_att_branch"]
ROW_SHARDED = ["w_out", "w_ffn_down"]
TRANSPOSED = ["w_in", "w_ffn_up"]
VECTORS = ["norm_mix_pre", "conv_dw_b", "conv_ln_g", "conv_ln_b", "b_conv_branch", "norm_mix_post",
           "norm_ffn_pre", "norm_ffn_post"]


def _cols_to_full(g):
    return g.transpose(1, 0, 2).reshape(g.shape[1], N_DEV * g.shape[2])


def _full_to_cols(f):
    return f.reshape(f.shape[0], N_DEV, f.shape[1] // N_DEV).transpose(1, 0, 2)


def _pack_vectors(vecs):
    rows = [jnp.pad(vecs[nm].reshape(-1), (0, D_MODEL - vecs[nm].size)) for nm in VECTORS]
    return jnp.stack(rows)


def _unpack_vectors(packed, sizes):
    return {nm: packed[n, :sizes[nm]] for n, nm in enumerate(VECTORS)}


def kernel(x, norm_mix_pre, w_in, conv_dw_w, conv_dw_b, conv_ln_g, conv_ln_b, w_conv_branch, b_conv_branch, w_att_branch, w_out, norm_mix_post, norm_ffn_pre, w_ffn_up, w_ffn_down, norm_ffn_post, loss_target, m_norm_mix_pre, m_w_in, m_conv_dw_w, m_conv_dw_b, m_conv_ln_g, m_conv_ln_b, m_w_conv_branch, m_b_conv_branch, m_w_att_branch, m_w_out, m_norm_mix_post, m_norm_ffn_pre, m_w_ffn_up, m_w_ffn_down, m_norm_ffn_post, v_norm_mix_pre, v_w_in, v_conv_dw_w, v_conv_dw_b, v_conv_ln_g, v_conv_ln_b, v_w_conv_branch, v_b_conv_branch, v_w_att_branch, v_w_out, v_norm_mix_post, v_norm_ffn_pre, v_w_ffn_up, v_w_ffn_down, v_norm_ffn_post):
    ws = dict(zip(WEIGHTS, [norm_mix_pre, w_in, conv_dw_w, conv_dw_b, conv_ln_g, conv_ln_b, w_conv_branch,
                            b_conv_branch, w_att_branch, w_out, norm_mix_post, norm_ffn_pre, w_ffn_up, w_ffn_down,
                            norm_ffn_post]))
    ms = dict(zip(WEIGHTS, [m_norm_mix_pre, m_w_in, m_conv_dw_w, m_conv_dw_b, m_conv_ln_g, m_conv_ln_b,
                            m_w_conv_branch, m_b_conv_branch, m_w_att_branch, m_w_out, m_norm_mix_post,
                            m_norm_ffn_pre, m_w_ffn_up, m_w_ffn_down, m_norm_ffn_post]))
    vs = dict(zip(WEIGHTS, [v_norm_mix_pre, v_w_in, v_conv_dw_w, v_conv_dw_b, v_conv_ln_g, v_conv_ln_b,
                            v_w_conv_branch, v_b_conv_branch, v_w_att_branch, v_w_out, v_norm_mix_post,
                            v_norm_ffn_pre, v_w_ffn_up, v_w_ffn_down, v_norm_ffn_post]))

    dw_block = jnp.pad(conv_dw_w, ((0, 1), (0, 0)))
    g_in, g_dw = _exchange_call("gather_first", _gather_exchange([w_in.T.astype(BF16), dw_block]))
    full = {"w_in": _full_weight("w_in", g_in), "conv_dw_w": _cols_to_full(g_dw)}
    for nm in VECTORS:
        full[nm] = ws[nm].reshape(1, -1)

    loss_local, grad_x, received, grads = _local_step(
        x[0], loss_target[0], full, {nm: (ws[nm].T if nm in TRANSPOSED else ws[nm]).astype(BF16) for nm in LATE})

    loss_at = (VECTORS.index("conv_dw_b"), CONV_DIM)
    small = _exchange_call("gather_small_grads", _gather_exchange(
        [_pack_vectors(grads).at[loss_at].set(loss_local), grads["conv_dw_w"]]))
    out_g, out_d, out_m, out_v = {}, {}, {}, {}
    for nm in LATE + ["w_in"]:
        if nm in TRANSPOSED:
            res = _sum_adamw("adamw_" + nm, received[nm], ws[nm].T, ms[nm].T, vs[nm].T)
            out_g[nm], out_d[nm], out_m[nm], out_v[nm] = [r.T for r in res]
        else:
            out_g[nm], out_d[nm], out_m[nm], out_v[nm] = _sum_adamw("adamw_" + nm, received[nm], ws[nm], ms[nm], vs[nm])
    sizes = {nm: ws[nm].size for nm in VECTORS}
    vec = _sum_adamw("adamw_vectors", small[0], _pack_vectors(ws), _pack_vectors(ms), _pack_vectors(vs))
    for res, dst in zip(vec, (out_g, out_d, out_m, out_v)):
        dst.update(_unpack_vectors(res, sizes))
    loss = vec[0][loss_at]
    dw_full = _sum_parts("sum_dw_grads", small[1])
    me = _slot(*_place())
    dw_mine = lax.dynamic_slice(dw_full, (0, me * (CONV_DIM // N_DEV)), (CONV_WIDTH, CONV_DIM // N_DEV))
    nm = "conv_dw_w"
    out_g[nm], out_d[nm], out_m[nm], out_v[nm] = _sum_adamw("adamw_dw", dw_mine[None], ws[nm], ms[nm], vs[nm])

    outs = [loss, grad_x[None]]
    for group in (out_g, out_d, out_m, out_v):
        outs += [group[nm] for nm in WEIGHTS]
    return tuple(outs)
```

```python
import math

import jax
import jax.numpy as jnp
from jax import lax
from jax.experimental import pallas as pl
from jax.experimental.pallas import tpu as pltpu

F32 = jnp.float32
BF16 = jnp.bfloat16

N_DEV = 8
D_MODEL = 1024
CONV_DIM = 512
CONV_WIDTH = 31
N_HEADS = 8
HEAD_DIM = 64
ATT_DIM = N_HEADS * HEAD_DIM
D_FF = 2816
EPS = 1e-6
IN_SPLITS = (0, 1024, 1536, 2048, 2560, 3584, 4608)

ADAM_LR = 0.001
ADAM_B1 = 0.9
ADAM_B2 = 0.999
ADAM_EPS = 1e-08
ADAM_WD = 0.01
ADAM_STEP = 10

LANES = 128
SUBLANES = 8
HALO = 32
ATT_TILE = 256
DEAD_SUM = -120.0
VMEM_LIMIT = 56 * 1024 * 1024
MESH = pl.DeviceIdType.MESH
ANY = pl.BlockSpec(memory_space=pl.ANY)


def _pick(dim, target, align=LANES):
    t = min(dim, target)
    t -= t % align
    while t >= align:
        if dim % t == 0:
            return t
        t -= align
    return dim


def _params(semantics):
    return pltpu.CompilerParams(dimension_semantics=semantics, vmem_limit_bytes=VMEM_LIMIT)


def _matmul(a, b, *, name, ta=False, tb=False, out_dtype=F32):
    m, k = (a.shape[1], a.shape[0]) if ta else a.shape
    n, k2 = b.shape if tb else (b.shape[1], b.shape[0])
    assert k == k2, (a.shape, b.shape, ta, tb)
    tm, tn, tk = _pick(m, 1408 if ta else 512), _pick(n, 1536), _pick(k, 1536)
    nk = k // tk
    dims = (((0 if ta else 1,), (1 if tb else 0,)), ((), ()))

    def body(a_ref, b_ref, o_ref, *acc):
        part = lax.dot_general(a_ref[...], b_ref[...], dims, preferred_element_type=F32)
        if nk == 1:
            o_ref[...] = part.astype(o_ref.dtype)
            return
        acc_ref, = acc
        kk = pl.program_id(2)

        @pl.when(kk == 0)
        def _():
            acc_ref[...] = part

        @pl.when((kk > 0) & (kk < nk - 1))
        def _():
            acc_ref[...] += part

        @pl.when(kk == nk - 1)
        def _():
            o_ref[...] = (acc_ref[...] + part).astype(o_ref.dtype)

    a_spec = pl.BlockSpec((tk, tm), lambda j, i, kk: (kk, i)) if ta else pl.BlockSpec((tm, tk), lambda j, i, kk: (i, kk))
    b_spec = (pl.BlockSpec((tn, tk), lambda j, i, kk: (j, kk)) if tb
              else pl.BlockSpec((tk, tn), lambda j, i, kk: (kk, j)))
    return pl.pallas_call(
        body, name=name, grid=(n // tn, m // tm, nk),
        in_specs=[a_spec, b_spec],
        out_specs=pl.BlockSpec((tm, tn), lambda j, i, kk: (i, j)),
        out_shape=jax.ShapeDtypeStruct((m, n), out_dtype),
        scratch_shapes=[pltpu.VMEM((tm, tn), F32)] if nk > 1 else [],
        compiler_params=_params(("parallel", "parallel", "arbitrary")),
    )(a, b)


NO_EXCHANGE = ([], [], [], None)


def _sweep_marks(nt):
    i = pl.program_id(0)
    return i == 0, i == (3 * nt) // 4, i == nt - 1


def _rowwise(name, fn, rows, bcasts, row_outs, red_outs=(), tm=256, exchange=NO_EXCHANGE):
    s = rows[0].shape[0]
    tm = _pick(s, tm, 16)
    nt = s // tm
    resident = pl.Buffered(1)
    nr, nb, no, nd = len(rows), len(bcasts), len(row_outs), len(red_outs)
    x_arrs, x_shape, x_scratch, _ = exchange
    nx = len(x_arrs)
    first_out = nr + nb + nx

    def body(*refs):
        finish_exchange = _carry_exchange(exchange, refs, nr + nb, no + nd, *_sweep_marks(nt))
        ins = [r[...] for r in refs[:nr + nb]]
        outs, reds = fn(*ins)
        for ref, val in zip(refs[first_out:first_out + no], outs):
            ref[...] = val.astype(ref.dtype)
        i = pl.program_id(0)
        for ref, val in zip(refs[first_out + no:first_out + no + nd], reds):
            @pl.when(i == 0)
            def _():
                ref[...] = val

            @pl.when(i > 0)
            def _():
                ref[...] += val
        finish_exchange()

    in_specs = [pl.BlockSpec((tm, r.shape[1]), lambda i: (i, 0)) for r in rows]
    in_specs += [pl.BlockSpec(b.shape, lambda i: (0, 0), pipeline_mode=resident) for b in bcasts]
    out_specs = [pl.BlockSpec((tm, o.shape[1]), lambda i: (i, 0)) for o in row_outs]
    out_specs += [pl.BlockSpec(d.shape, lambda i: (0, 0)) for d in red_outs]
    return pl.pallas_call(
        body, name=name, grid=(nt,), in_specs=in_specs + [ANY] * nx, out_specs=out_specs + [ANY] * nx,
        out_shape=list(row_outs) + list(red_outs) + x_shape, scratch_shapes=x_scratch,
        compiler_params=_params(("arbitrary",)),
    )(*rows, *bcasts, *x_arrs)


def _sds(shape, dtype=F32):
    return jax.ShapeDtypeStruct(shape, dtype)


def _rms(x, g):
    y = x * lax.rsqrt(jnp.mean(x * x, axis=-1, keepdims=True) + EPS)
    return y * g


def _silu(x):
    return x * jax.nn.sigmoid(x)


def _swiglu(g, u):
    return _silu(g) * u


def _ln_silu(u, g, b):
    mu = jnp.mean(u, axis=-1, keepdims=True)
    var = jnp.mean(jnp.square(u - mu), axis=-1, keepdims=True)
    return _silu((u - mu) * lax.rsqrt(var + EPS) * g + b)


def _merge(conv_pre, att_out, g_conv, g_att, b_cb):
    return jax.nn.sigmoid(g_conv) * (conv_pre + b_cb) + jax.nn.sigmoid(g_att) * att_out


def _glu(t):
    return t[:, :CONV_DIM] * jax.nn.sigmoid(t[:, CONV_DIM:])


def _shifted_reader(buf, shifted, tm):
    for b in range(1, SUBLANES):
        shifted[b - 1, :, :] = buf[pl.ds(b, tm + HALO - SUBLANES), :]

    def read(o):
        a, b = divmod(o, SUBLANES)
        return buf[pl.ds(SUBLANES * a, tm), :] if b == 0 else shifted[b - 1, pl.ds(SUBLANES * a, tm), :]

    return read


def _conv_fwd(conv_in, w_pad, b, ln_g, ln_b, exchange, tm=256):
    s = conv_in.shape[0]
    tm = _pick(s, tm, HALO)
    ratio = tm // HALO
    x_arrs, x_shape, x_scratch, _ = exchange
    nx = len(x_arrs)

    def body(*refs):
        main_ref, halo_ref, w_ref, b_ref, g_ref, be_ref = refs[:6]
        u3_ref, u1_ref = refs[6 + nx:8 + nx]
        buf, shifted = refs[-2:]
        finish_exchange = _carry_exchange(exchange, refs, 6, 2, *_sweep_marks(s // tm))
        i = pl.program_id(0)
        buf[0:HALO, :] = _glu(halo_ref[...]) * (i > 0).astype(F32)
        buf[HALO:HALO + tm, :] = _glu(main_ref[...])
        read = _shifted_reader(buf, shifted, tm)
        acc = jnp.zeros((tm, CONV_DIM), F32) + b_ref[...]
        for j in range(CONV_WIDTH):
            acc = acc + w_ref[j:j + 1, :] * read(HALO - (CONV_WIDTH - 1) + j)
        u1_ref[...] = acc
        u3_ref[...] = _ln_silu(acc, g_ref[...], be_ref[...]).astype(u3_ref.dtype)
        finish_exchange()

    res = pl.pallas_call(
        body, name="conv_fwd", grid=(s // tm,),
        in_specs=[pl.BlockSpec((tm, 2 * CONV_DIM), lambda i: (i, 0)),
                  pl.BlockSpec((HALO, 2 * CONV_DIM), lambda i: (jnp.maximum(i * ratio - 1, 0), 0)),
                  pl.BlockSpec(w_pad.shape, lambda i: (0, 0)),
                  pl.BlockSpec(b.shape, lambda i: (0, 0)),
                  pl.BlockSpec(ln_g.shape, lambda i: (0, 0)),
                  pl.BlockSpec(ln_b.shape, lambda i: (0, 0))] + [ANY] * nx,
        out_specs=[pl.BlockSpec((tm, CONV_DIM), lambda i: (i, 0)),
                   pl.BlockSpec((tm, CONV_DIM), lambda i: (i, 0))] + [ANY] * nx,
        out_shape=[_sds((s, CONV_DIM), BF16), _sds((s, CONV_DIM), F32)] + x_shape,
        scratch_shapes=x_scratch + [pltpu.VMEM((tm + HALO, CONV_DIM), F32),
                                    pltpu.VMEM((SUBLANES - 1, tm + HALO - SUBLANES, CONV_DIM), F32)],
        compiler_params=_params(("arbitrary",)),
    )(conv_in, conv_in, w_pad, b, ln_g, ln_b, *x_arrs)
    return res[0], res[1], res[2:]


def _conv_bwd(conv_in, u1, du3, ln_g, ln_b, w_pad, exchange, tm=256):
    s = conv_in.shape[0]
    tm = _pick(s, tm, HALO)
    ratio = tm // HALO
    nt = s // tm
    last_halo = s // HALO - 1
    x_arrs, x_shape, x_scratch, _ = exchange
    nx = len(x_arrs)

    def body(*refs):
        main_ref, halo_ref, u1_ref, u1n_ref, du3_ref, du3n_ref, g_ref, be_ref, w_ref = refs[:9]
        dci_ref, dw_ref, db_ref, dg_ref, dbe_ref = refs[9 + nx:14 + nx]
        ubuf, dbuf, ushift, dshift = refs[-4:]
        finish_exchange = _carry_exchange(exchange, refs, 9, 5, *_sweep_marks(nt))
        i = pl.program_id(0)
        main = main_ref[...]
        a = main[:, :CONV_DIM]
        sb = jax.nn.sigmoid(main[:, CONV_DIM:])
        ubuf[0:HALO, :] = _glu(halo_ref[...]) * (i > 0).astype(F32)
        ubuf[HALO:HALO + tm, :] = a * sb

        def ln_bwd(u1t, du3t):
            _, vjp = jax.vjp(_ln_silu, u1t, g_ref[...], be_ref[...])
            return vjp(du3t)

        du, dg, dbe = ln_bwd(u1_ref[...], du3_ref[...])
        dbuf[0:tm, :] = du
        dbuf[tm:tm + HALO, :] = ln_bwd(u1n_ref[...], du3n_ref[...])[0] * (i < nt - 1).astype(F32)

        @pl.when(i == 0)
        def _():
            dw_ref[...] = jnp.zeros_like(dw_ref)
            db_ref[...] = jnp.zeros_like(db_ref)
            dg_ref[...] = jnp.zeros_like(dg_ref)
            dbe_ref[...] = jnp.zeros_like(dbe_ref)

        dg_ref[...] += dg
        dbe_ref[...] += dbe

        read_u = _shifted_reader(ubuf, ushift, tm)
        read_d = _shifted_reader(dbuf, dshift, tm)
        du0 = jnp.zeros((tm, CONV_DIM), F32)
        for j in range(CONV_WIDTH):
            du0 = du0 + w_ref[j:j + 1, :] * read_d(CONV_WIDTH - 1 - j)
            dw_ref[j:j + 1, :] += jnp.sum(du * read_u(HALO - (CONV_WIDTH - 1) + j), axis=0, keepdims=True)
        db_ref[...] += jnp.sum(du, axis=0, keepdims=True)
        dci_ref[:, :CONV_DIM] = (du0 * sb).astype(dci_ref.dtype)
        dci_ref[:, CONV_DIM:] = (du0 * a * sb * (1.0 - sb)).astype(dci_ref.dtype)
        finish_exchange()

    res = pl.pallas_call(
        body, name="conv_bwd", grid=(nt,),
        in_specs=[pl.BlockSpec((tm, 2 * CONV_DIM), lambda i: (i, 0)),
                  pl.BlockSpec((HALO, 2 * CONV_DIM), lambda i: (jnp.maximum(i * ratio - 1, 0), 0))]
        + [pl.BlockSpec((tm, CONV_DIM), lambda i: (i, 0)),
           pl.BlockSpec((HALO, CONV_DIM), lambda i: (jnp.minimum((i + 1) * ratio, last_halo), 0))] * 2
        + [pl.BlockSpec((1, CONV_DIM), lambda i: (0, 0))] * 2 + [pl.BlockSpec(w_pad.shape, lambda i: (0, 0))]
        + [ANY] * nx,
        out_specs=[pl.BlockSpec((tm, 2 * CONV_DIM), lambda i: (i, 0)),
                   pl.BlockSpec(w_pad.shape, lambda i: (0, 0))]
        + [pl.BlockSpec((1, CONV_DIM), lambda i: (0, 0))] * 3 + [ANY] * nx,
        out_shape=[_sds((s, 2 * CONV_DIM), BF16), _sds(w_pad.shape)] + [_sds((1, CONV_DIM))] * 3 + x_shape,
        scratch_shapes=x_scratch + [pltpu.VMEM((tm + HALO, CONV_DIM), F32)] * 2
        + [pltpu.VMEM((SUBLANES - 1, tm + HALO - SUBLANES, CONV_DIM), F32)] * 2,
        compiler_params=_params(("arbitrary",)),
    )(conv_in, conv_in, u1, u1, du3, du3, ln_g, ln_b, w_pad, *x_arrs)
    return res[:5], res[5:]


def _logsig_neg(z):
    return jnp.minimum(-z, 0.0) - jnp.log(1.0 + jnp.exp(-jnp.abs(z)))


def _split_dot(val, tri):
    hi = val.astype(BF16)
    lo = (val - hi.astype(F32)).astype(BF16)
    return jnp.dot(hi, tri, preferred_element_type=F32) + jnp.dot(lo, tri, preferred_element_type=F32)


def _attn_masks(t, later):
    row = lax.broadcasted_iota(jnp.int32, (t, t), 0)
    col = lax.broadcasted_iota(jnp.int32, (t, t), 1)
    tri = jnp.where(row > col if later else row <= col, 1.0, 0.0).astype(BF16)
    return col < row, tri


def _grid_marks(h, nq):
    hh, i = pl.program_id(0), pl.program_id(1)
    return (hh == 0) & (i == 0), (hh == (3 * h) // 4) & (i == 0), (hh == h - 1) & (i == nq - 1)


def _head_masks(shape):
    lane = lax.broadcasted_iota(jnp.int32, shape, len(shape) - 1)
    return lane < HEAD_DIM, lane >= HEAD_DIM


def _per_head(blk):
    m0, m1 = _head_masks(blk.shape)
    zero = jnp.zeros_like(blk)
    return jnp.where(m0, blk, zero), jnp.where(m1, blk, zero)


NT = (((1,), (1,)), ((), ()))
TN = (((0,), (0,)), ((), ()))


def _attn_fwd(q, k, v, exchange):
    s = q.shape[0]
    hp = q.shape[1] // LANES
    t = ATT_TILE
    scale = 1.0 / math.sqrt(HEAD_DIM)
    x_arrs, x_shape, x_scratch, _ = exchange
    nx = len(x_arrs)

    def body(*refs):
        q_ref, k_ref, v_ref = refs[:3]
        o_ref, lt_ref, nb_ref = refs[3 + nx:6 + nx]
        finish_exchange = _carry_exchange(exchange, refs, 3, 3, *_grid_marks(hp, s // t))
        i = pl.program_id(1)
        qs = _per_head((q_ref[...].astype(F32) * scale).astype(BF16))
        causal, tri = _attn_masks(t, later=True)

        def step(kb, carry, masked):
            cs, acc = carry
            off = pl.multiple_of(kb * t, t)
            kblk = k_ref[pl.ds(off, t), :]
            vs = _per_head(v_ref[pl.ds(off, t), :])
            new_cs = []
            for hd in range(2):
                z = lax.dot_general(qs[hd], kblk, NT, preferred_element_type=F32)
                l = _logsig_neg(z)
                if masked:
                    l = jnp.where(causal, l, 0.0)
                e = z + l + _split_dot(l, tri) + cs[hd]
                if masked:
                    e = jnp.where(causal, e, -1e30)
                acc = acc + jnp.dot(jnp.exp(e).astype(BF16), vs[hd], preferred_element_type=F32)
                new_cs.append(cs[hd] + jnp.sum(l, axis=1, keepdims=True))
            return tuple(new_cs), acc

        zero = jnp.zeros((t, 1), F32)
        carry = step(i, ((zero, zero), jnp.zeros((t, LANES), F32)), True)

        def more(state):
            n, (cs, _) = state
            return (n < i) & (jnp.maximum(jnp.max(cs[0]), jnp.max(cs[1])) > DEAD_SUM)

        n_blocks, carry = lax.while_loop(more, lambda st: (st[0] + 1, step(i - 1 - st[0], st[1], False)),
                                         (jnp.int32(0), carry))
        m0, _ = _head_masks((t, LANES))
        lt_ref[...] = jnp.where(m0, carry[0][0], carry[0][1])
        o_ref[...] = carry[1].astype(o_ref.dtype)
        nb_ref[pl.program_id(0), i] = n_blocks.astype(F32)
        finish_exchange()

    res = pl.pallas_call(
        body, name="attn_fwd", grid=(hp, s // t),
        in_specs=[pl.BlockSpec((t, LANES), lambda p, i: (i, p)),
                  pl.BlockSpec((s, LANES), lambda p, i: (0, p)),
                  pl.BlockSpec((s, LANES), lambda p, i: (0, p))] + [ANY] * nx,
        out_specs=[pl.BlockSpec((t, LANES), lambda p, i: (i, p)),
                   pl.BlockSpec((None, t, LANES), lambda p, i: (p, i, 0)),
                   pl.BlockSpec(memory_space=pltpu.SMEM)] + [ANY] * nx,
        out_shape=[_sds(q.shape, BF16), _sds((hp, s, LANES), F32), _sds((hp, s // t), F32)] + x_shape,
        scratch_shapes=x_scratch,
        compiler_params=_params(("arbitrary", "arbitrary")),
    )(q, k, v, *x_arrs)
    return res[0], res[1], res[2], res[3:]


def _attn_bwd(q, k, v, do, ltot, n_blocks, exchange):
    s = q.shape[0]
    hp = q.shape[1] // LANES
    t = ATT_TILE
    scale = 1.0 / math.sqrt(HEAD_DIM)
    x_arrs, x_shape, x_scratch, _ = exchange
    nx = len(x_arrs)

    def body(*refs):
        q_ref, k_ref, v_ref, do_ref, lt_ref, nb_ref = refs[:6]
        dq_ref, dk_ref, dv_ref = refs[6 + nx:9 + nx]
        finish_exchange = _carry_exchange(exchange, refs, 6, 3, *_grid_marks(hp, s // t))
        i = pl.program_id(1)
        first = jnp.clip(i - nb_ref[pl.program_id(0), i].astype(jnp.int32), 0, i)

        @pl.when(i == 0)
        def _():
            dk_ref[...] = jnp.zeros_like(dk_ref)
            dv_ref[...] = jnp.zeros_like(dv_ref)

        qb = q_ref[...]
        qm = _per_head(qb)
        qs = _per_head((qb.astype(F32) * scale).astype(BF16))
        dos = _per_head(do_ref[...])
        lts = (lt_ref[:, 0:1], lt_ref[:, HEAD_DIM:HEAD_DIM + 1])
        causal, tri = _attn_masks(t, later=False)

        def step(kb, carry, masked):
            cls, cgs, dq = carry
            off = pl.multiple_of(kb * t, t)
            kblk = k_ref[pl.ds(off, t), :]
            vblk = v_ref[pl.ds(off, t), :]
            ks = _per_head(kblk)
            dk = jnp.zeros((t, LANES), F32)
            dv = jnp.zeros((t, LANES), F32)
            new_cls, new_cgs = [], []
            for hd in range(2):
                z = lax.dot_general(qs[hd], kblk, NT, preferred_element_type=F32)
                l = _logsig_neg(z)
                if masked:
                    l = jnp.where(causal, l, 0.0)
                e = z + l + ((lts[hd] - cls[hd]) - _split_dot(l, tri))
                if masked:
                    e = jnp.where(causal, e, -1e30)
                a = jnp.exp(e)
                g = lax.dot_general(dos[hd], vblk, NT, preferred_element_type=F32) * a
                p = cgs[hd] + jnp.dot(g.astype(BF16), tri, preferred_element_type=F32) - g
                el = jnp.exp(l)
                dz = g * el - p * (1.0 - el)
                if masked:
                    dz = jnp.where(causal, dz, 0.0)
                dzb = (dz * scale).astype(BF16)
                dq = dq + jnp.dot(dzb, ks[hd], preferred_element_type=F32)
                dk = dk + lax.dot_general(dzb, qm[hd], TN, preferred_element_type=F32)
                dv = dv + lax.dot_general(a.astype(BF16), dos[hd], TN, preferred_element_type=F32)
                new_cls.append(cls[hd] + jnp.sum(l, axis=1, keepdims=True))
                new_cgs.append(cgs[hd] + jnp.sum(g, axis=1, keepdims=True))
            dk_ref[pl.ds(off, t), :] += dk
            dv_ref[pl.ds(off, t), :] += dv
            return tuple(new_cls), tuple(new_cgs), dq

        zero = jnp.zeros((t, 1), F32)
        init = ((zero, zero), (zero, zero), jnp.zeros((t, LANES), F32))
        carry = lax.fori_loop(first, i, lambda kb, cr: step(kb, cr, False), init)
        carry = step(i, carry, True)
        dq_ref[...] = carry[2]
        finish_exchange()

    blk = pl.BlockSpec((t, LANES), lambda p, i: (i, p))
    whole = pl.BlockSpec((s, LANES), lambda p, i: (0, p))
    res = pl.pallas_call(
        body, name="attn_bwd", grid=(hp, s // t),
        in_specs=[blk, whole, whole, blk, pl.BlockSpec((None, t, LANES), lambda p, i: (p, i, 0)),
                  pl.BlockSpec(memory_space=pltpu.SMEM)] + [ANY] * nx,
        out_specs=[blk, whole, whole] + [ANY] * nx,
        out_shape=[_sds(q.shape)] * 3 + x_shape,
        scratch_shapes=x_scratch,
        compiler_params=_params(("arbitrary", "arbitrary")),
    )(q, k, v, do, ltot, n_blocks, *x_arrs)
    return res[0], res[1], res[2], res[3:]


LATE = ["w_conv_branch", "w_att_branch", "w_out", "w_ffn_up", "w_ffn_down"]


def _full_weight(name, gathered):
    return _cols_to_full(gathered) if name in COL_SHARDED else gathered.reshape(-1, gathered.shape[2])


def _grad_slabs(name, grad):
    return _full_to_cols(grad) if name in COL_SHARDED else grad.reshape(N_DEV, -1, grad.shape[1])


def _side_slabs(name, grad):
    slabs = _grad_slabs(name, grad)
    return slabs.reshape((4, 2) + slabs.shape[1:])


def _local_step(x, target, w, late_blocks):
    s = x.shape[0]
    w = dict(w)
    g1, g2, g3, g4 = w["norm_mix_pre"], w["norm_mix_post"], w["norm_ffn_pre"], w["norm_ffn_post"]

    w_in = w["w_in"]

    def proj_fn(xt, g1_, w_in_t):
        h = _rms(xt, g1_).astype(BF16)
        proj = lax.dot_general(h, w_in_t, NT, preferred_element_type=F32)
        return (h, *[proj[:, IN_SPLITS[n]:IN_SPLITS[n + 1]] for n in range(6)]), ()

    mix_weights = ["w_conv_branch", "w_att_branch", "w_out"]
    h1, conv_in, q, k, v, g_conv, g_att, *gathered = _rowwise(
        "norm_proj", proj_fn, [x], [g1, w_in],
        [_sds((s, D_MODEL), BF16), _sds((s, 2 * CONV_DIM)), _sds((s, ATT_DIM), BF16), _sds((s, ATT_DIM), BF16),
         _sds((s, ATT_DIM), BF16), _sds((s, D_MODEL)), _sds((s, D_MODEL))], tm=512,
        exchange=_gather_exchange([late_blocks[nm] for nm in mix_weights]))
    for nm, g in zip(mix_weights, gathered):
        w[nm] = _full_weight(nm, g)

    u3, u1, _ = _conv_fwd(conv_in, w["conv_dw_w"], w["conv_dw_b"], w["conv_ln_g"], w["conv_ln_b"], NO_EXCHANGE)
    ffn_weights = ["w_ffn_up", "w_ffn_down"]
    att, ltot, n_blocks, gathered = _attn_fwd(q, k, v, _gather_exchange([late_blocks[nm] for nm in ffn_weights]))
    for nm, g in zip(ffn_weights, gathered):
        w[nm] = _full_weight(nm, g)

    def merge_fn(u3t, at, gc, ga, xt, w_cb, w_ab, b_cb, w_out, g2_, g3_):
        cp = jnp.dot(u3t, w_cb, preferred_element_type=F32)
        ao = jnp.dot(at, w_ab, preferred_element_type=F32)
        mg = _merge(cp, ao, gc, ga, b_cb).astype(BF16)
        mix_ = jnp.dot(mg, w_out, preferred_element_type=F32)
        x2_ = xt + _rms(mix_, g2_)
        return (mg, cp, ao, mix_, x2_, _rms(x2_, g3_)), ()

    merged, conv_pre, att_out, mix, x2, h2 = _rowwise(
        "branch_merge_mix", merge_fn, [u3, att, g_conv, g_att, x],
        [w["w_conv_branch"], w["w_att_branch"], w["b_conv_branch"], w["w_out"], g2, g3],
        [_sds((s, D_MODEL), BF16)] * 3 + [_sds((s, D_MODEL)), _sds((s, D_MODEL)), _sds((s, D_MODEL), BF16)], tm=512)

    def ffn_up_fn(ht, w_up_t):
        gu_ = lax.dot_general(ht, w_up_t, NT, preferred_element_type=F32)
        return (gu_, _swiglu(gu_[:, :D_FF], gu_[:, D_FF:])), ()

    gu, act = _rowwise("ffn_up", ffn_up_fn, [h2], [w["w_ffn_up"]],
                       [_sds((s, 2 * D_FF), BF16), _sds((s, D_FF), BF16)], tm=512)

    def final_fn(at, x2t, tgt, w_down, g4_):
        ff = jnp.dot(at, w_down, preferred_element_type=F32)
        n4, vjp = jax.vjp(_rms, ff, g4_)
        err = x2t + n4 - tgt
        dy = err * (1.0 / D_MODEL)
        dff, dg4 = vjp(dy)
        return (dy, dff), (jnp.sum(err * err, axis=0, keepdims=True), dg4)

    dy, dff, loss_cols, d_g4 = _rowwise("ffn_down_loss", final_fn, [act, x2, target], [w["w_ffn_down"], g4],
                                        [_sds((s, D_MODEL)), _sds((s, D_MODEL), BF16)],
                                        [_sds((1, D_MODEL)), _sds((1, D_MODEL))], tm=512)
    loss = 0.5 * jnp.sum(loss_cols) / D_MODEL

    d_w_down = _matmul(act, dff, ta=True, name="d_w_down", out_dtype=BF16)

    def act_bwd_fn(dfft, gut, w_down):
        d_act = lax.dot_general(dfft, w_down, NT, preferred_element_type=F32)
        gu_ = gut.astype(F32)
        _, vjp = jax.vjp(_swiglu, gu_[:, :D_FF], gu_[:, D_FF:])
        return (jnp.concatenate(vjp(d_act), axis=1),), ()

    down_slabs = _side_slabs("w_ffn_down", d_w_down)
    dgu, theirs = _rowwise("ffn_act_bwd", act_bwd_fn, [dff, gu], [w["w_ffn_down"]], [_sds((s, 2 * D_FF), BF16)],
                           exchange=_pair_exchange([down_slabs]))
    down_sums = _pair_sum("pair_sum_w_ffn_down", down_slabs, theirs)
    d_w_up = _matmul(dgu, h2, ta=True, name="d_w_up", out_dtype=BF16)
    received = {}
    up_slabs = _side_slabs("w_ffn_up", d_w_up)

    def mid_bwd_fn(dgut, xt, mt, dyt, w_up_t, g2_, g3_):
        dh = jnp.dot(dgut, w_up_t, preferred_element_type=F32)
        n2, vjp2 = jax.vjp(_rms, mt, g2_)
        x2_ = xt + n2
        _, vjp3 = jax.vjp(_rms, x2_, g3_)
        dx2_, dg3 = vjp3(dh)
        dx2_ = dx2_ + dyt
        dmix_, dg2 = vjp2(dx2_)
        return (dx2_, dmix_), (dg2, dg3)

    dx2, dmix, d_g2, d_g3, received["w_ffn_down"] = _rowwise(
        "ffn_up_mid_bwd", mid_bwd_fn, [dgu, x, mix, dy], [w["w_ffn_up"], g2, g3],
        [_sds((s, D_MODEL)), _sds((s, D_MODEL), BF16)], [_sds((1, D_MODEL)), _sds((1, D_MODEL))], tm=512,
        exchange=_chip_exchange([down_sums]))
    d_w_out = _matmul(merged, dmix, ta=True, name="d_w_out", out_dtype=BF16)

    def merge_bwd_fn(dmt, cp, ao, gc, ga, w_out, w_cb, w_ab, b_cb):
        dm = lax.dot_general(dmt, w_out, NT, preferred_element_type=F32)
        _, vjp = jax.vjp(_merge, cp.astype(F32), ao.astype(F32), gc, ga, b_cb)
        dcp, dao, dgc, dga, dbias = vjp(dm)
        dcp, dao = dcp.astype(BF16), dao.astype(BF16)
        du3_ = lax.dot_general(dcp, w_cb, NT, preferred_element_type=F32)
        datt_ = lax.dot_general(dao, w_ab, NT, preferred_element_type=F32)
        return (dcp, dao, dgc, dga, du3_, datt_), (dbias,)

    d_conv_out, d_att_out, d_g_conv, d_g_att, du3, d_att, d_b_cb, theirs = _rowwise(
        "merge_bwd", merge_bwd_fn, [dmix, conv_pre, att_out, g_conv, g_att],
        [w["w_out"], w["w_conv_branch"], w["w_att_branch"], w["b_conv_branch"]],
        [_sds((s, D_MODEL), BF16)] * 4 + [_sds((s, CONV_DIM)), _sds((s, ATT_DIM), BF16)], [_sds((1, D_MODEL))], tm=512,
        exchange=_pair_exchange([up_slabs]))

    d_w_cb = _matmul(u3, d_conv_out, ta=True, name="d_w_conv_branch", out_dtype=BF16)
    d_w_ab = _matmul(att, d_att_out, ta=True, name="d_w_att_branch", out_dtype=BF16)

    dq, dk, dv, (received["w_ffn_up"],) = _attn_bwd(
        q, k, v, d_att, ltot, n_blocks, _chip_exchange([_pair_sum("pair_sum_w_ffn_up", up_slabs, theirs)]))

    mix_grads = {"w_conv_branch": d_w_cb, "w_att_branch": d_w_ab, "w_out": d_w_out}
    (d_conv_in, d_dw_w, d_dw_b, d_ln_g, d_ln_b), landed = _conv_bwd(
        conv_in, u1, du3, w["conv_ln_g"], w["conv_ln_b"], w["conv_dw_w"],
        _scatter_exchange([_grad_slabs(nm, mix_grads[nm]) for nm in mix_weights]))
    received.update(zip(mix_weights, landed))

    d_proj = jnp.concatenate([d_conv_in, dq.astype(BF16), dk.astype(BF16), dv.astype(BF16), d_g_conv, d_g_att],
                             axis=1)
    d_w_in = _matmul(d_proj, h1, ta=True, name="d_w_in", out_dtype=BF16)
    in_slabs = _side_slabs("w_in", d_w_in)
    (theirs,) = _exchange_call("pair_swap_w_in", _pair_exchange([in_slabs]))

    def pre_bwd_fn(dpt, xt, dx2t, w_in_t, g_):
        dh = jnp.dot(dpt, w_in_t, preferred_element_type=F32)
        _, vjp = jax.vjp(_rms, xt, g_)
        dx_, dg_ = vjp(dh)
        return (dx_ + dx2t,), (dg_,)

    grad_x, d_g1, received["w_in"] = _rowwise(
        "proj_norm_bwd", pre_bwd_fn, [d_proj, x, dx2], [w_in, g1], [_sds((s, D_MODEL))], [_sds((1, D_MODEL))], tm=512,
        exchange=_chip_exchange([_pair_sum("pair_sum_w_in", in_slabs, theirs)]))

    grads = {
        "norm_mix_pre": d_g1, "conv_dw_w": d_dw_w, "conv_dw_b": d_dw_b,
        "conv_ln_g": d_ln_g, "conv_ln_b": d_ln_b, "b_conv_branch": d_b_cb,
        "norm_mix_post": d_g2, "norm_ffn_pre": d_g3, "norm_ffn_post": d_g4,
    }
    return loss, grad_x, received, grads


def _place():
    x, y, c = lax.axis_index("x"), lax.axis_index("y"), lax.axis_index("c")
    return x, y, c


def _slot(px, py, pc):
    return 4 * px + 2 * py + pc


def _exchange_scratch(n):
    return [pltpu.SemaphoreType.DMA((7 * n,)), pltpu.SemaphoreType.DMA((7 * n,)), pltpu.SemaphoreType.DMA((n,))]


def _gather_exchange(arrs):
    n = len(arrs)

    def phases(ins, outs, send_sems, recv_sems, local_sems):
        x, y, c = _place()
        me, sibling = (x, y, c), (x, y, 1 - c)
        chips = [(1 - x, y), (x, 1 - y), (1 - x, 1 - y)]

        def copy(a, kk, block, to, src=None):
            dst = outs[a].at[_slot(*block)]
            return pltpu.make_async_remote_copy(
                src_ref=dst if src is None else src, dst_ref=dst,
                send_sem=send_sems.at[a * 7 + kk], recv_sem=recv_sems.at[a * 7 + kk],
                device_id=to, device_id_type=MESH)

        mine = [pltpu.make_async_copy(ins[a], outs[a].at[_slot(*me)], local_sems.at[a]) for a in range(n)]
        first = []
        for a in range(n):
            first.append(copy(a, 0, me, sibling, src=ins[a]))
            first += [copy(a, 1 + j, me, (*chip, c), src=ins[a]) for j, chip in enumerate(chips)]
        passed = [copy(a, 4 + j, (*chip, c), sibling) for j, chip in enumerate(chips) for a in range(n)]

        def send():
            for cp in mine + first:
                cp.start()

        def pass_on():
            for j, chip in enumerate(chips):
                for a in range(n):
                    copy(a, 1 + j, (*chip, c), me).wait_recv()
                    passed[j * n + a].start()

        def finish():
            for a in range(n):
                copy(a, 0, sibling, me).wait_recv()
                for j, chip in enumerate(chips):
                    copy(a, 4 + j, (*chip, 1 - c), me).wait_recv()
            for cp in first + passed:
                cp.wait_send()
            for cp in mine:
                cp.wait()

        return [send, pass_on, finish]

    return list(arrs), [_sds((N_DEV,) + a.shape, a.dtype) for a in arrs], _exchange_scratch(n), phases


def _scatter_exchange(arrs):
    n = len(arrs)
    flips = [(fx, fy, fc) for fx in (0, 1) for fy in (0, 1) for fc in (0, 1)][1:]

    def phases(ins, outs, send_sems, recv_sems, local_sems):
        x, y, c = _place()
        mine = _slot(x, y, c)
        local = [pltpu.make_async_copy(ins[a].at[mine], outs[a].at[mine], local_sems.at[a]) for a in range(n)]
        peers = [((1 - x) if fx else x, (1 - y) if fy else y, (1 - c) if fc else c) for fx, fy, fc in flips]

        def copy(a, kk, src_slot, dst_slot):
            return pltpu.make_async_remote_copy(
                src_ref=ins[a].at[src_slot], dst_ref=outs[a].at[dst_slot],
                send_sem=send_sems.at[a * 7 + kk], recv_sem=recv_sems.at[a * 7 + kk],
                device_id=peers[kk], device_id_type=MESH)

        sends = [copy(a, kk, _slot(*peers[kk]), mine) for a in range(n) for kk in range(7)]

        def send():
            for cp in local + sends:
                cp.start()

        def finish():
            for a in range(n):
                for kk in range(7):
                    copy(a, kk, mine, _slot(*peers[kk])).wait_recv()
            for cp in sends:
                cp.wait_send()
            for cp in local:
                cp.wait()

        return [send, finish]

    return list(arrs), [_sds(a.shape, a.dtype) for a in arrs], _exchange_scratch(n), phases


def _pair_exchange(arrs):
    n = len(arrs)

    def phases(ins, outs, send_sems, recv_sems, local_sems):
        x, y, c = _place()

        def copy(a, chip, side):
            return pltpu.make_async_remote_copy(
                src_ref=ins[a].at[chip, side], dst_ref=outs[a].at[chip],
                send_sem=send_sems.at[a * 7 + chip], recv_sem=recv_sems.at[a * 7 + chip],
                device_id=(x, y, 1 - c), device_id_type=MESH)

        sends = [copy(a, chip, 1 - c) for a in range(n) for chip in range(4)]

        def send():
            for cp in sends:
                cp.start()

        def finish():
            for a in range(n):
                for chip in range(4):
                    copy(a, chip, c).wait_recv()
            for cp in sends:
                cp.wait_send()

        return [send, finish]

    return list(arrs), [_sds((4,) + a.shape[2:], a.dtype) for a in arrs], _exchange_scratch(n), phases


def _chip_exchange(arrs):
    n = len(arrs)

    def phases(ins, outs, send_sems, recv_sems, local_sems):
        x, y, c = _place()
        mine = 2 * x + y
        chips = [(1 - x, y), (x, 1 - y), (1 - x, 1 - y)]
        local = [pltpu.make_async_copy(ins[a].at[mine], outs[a].at[mine], local_sems.at[a]) for a in range(n)]

        def copy(a, j, src_slot, dst_slot):
            return pltpu.make_async_remote_copy(
                src_ref=ins[a].at[src_slot], dst_ref=outs[a].at[dst_slot],
                send_sem=send_sems.at[a * 7 + j], recv_sem=recv_sems.at[a * 7 + j],
                device_id=(*chips[j], c), device_id_type=MESH)

        sends = [copy(a, j, 2 * chips[j][0] + chips[j][1], mine) for a in range(n) for j in range(3)]

        def send():
            for cp in local + sends:
                cp.start()

        def finish():
            for a in range(n):
                for j in range(3):
                    copy(a, j, mine, 2 * chips[j][0] + chips[j][1]).wait_recv()
            for cp in sends:
                cp.wait_send()
            for cp in local:
                cp.wait()

        return [send, finish]

    return list(arrs), [_sds(a.shape, a.dtype) for a in arrs], _exchange_scratch(n), phases


def _pair_sum(name, mine, theirs):
    _, _, r, c = mine.shape

    def body(side_ref, m_ref, t_ref, o_ref):
        o_ref[...] = (m_ref[...].astype(F32) + t_ref[...].astype(F32)).astype(o_ref.dtype)

    return pl.pallas_call(
        body, name=name,
        grid_spec=pltpu.PrefetchScalarGridSpec(
            num_scalar_prefetch=1, grid=(4,),
            in_specs=[pl.BlockSpec((None, None, r, c), lambda j, side: (j, side[0], 0, 0)),
                      pl.BlockSpec((None, r, c), lambda j, side: (j, 0, 0))],
            out_specs=pl.BlockSpec((None, r, c), lambda j, side: (j, 0, 0))),
        out_shape=_sds(theirs.shape, theirs.dtype),
        compiler_params=_params(("parallel",)),
    )(lax.axis_index("c").astype(jnp.int32).reshape(1), mine, theirs)


def _exchange_call(name, exchange):
    arrs, out_shape, scratch, phases = exchange
    n = len(arrs)

    def body(*refs):
        for step in phases(refs[:n], refs[n:2 * n], *refs[2 * n:]):
            step()

    return pl.pallas_call(body, name=name, in_specs=[ANY] * n, out_specs=[ANY] * n,
                          out_shape=out_shape, scratch_shapes=scratch)(*arrs)


def _carry_exchange(exchange, refs, n_in, n_out, first, middle, last):
    arrs, _, _, phases = exchange
    n = len(arrs)
    if n == 0:
        return lambda: None
    ins = refs[n_in:n_in + n]
    outs = refs[n_in + n + n_out:n_in + 2 * n + n_out]
    sems = n_in + 2 * n + n_out
    steps = phases(ins, outs, *refs[sems:sems + 3])
    pl.when(first)(steps[0])
    if len(steps) == 3:
        pl.when(middle)(steps[1])
    return lambda: pl.when(last)(steps[-1])


def _adamw_math(w, g, m, v):
    m2 = ADAM_B1 * m + (1.0 - ADAM_B1) * g
    v2 = ADAM_B2 * v + (1.0 - ADAM_B2) * jnp.square(g)
    m_hat = m2 / (1.0 - ADAM_B1 ** ADAM_STEP)
    v_hat = v2 / (1.0 - ADAM_B2 ** ADAM_STEP)
    delta = -ADAM_LR * (m_hat / (jnp.sqrt(v_hat) + ADAM_EPS) + ADAM_WD * w)
    return delta, m2, v2


def _sum_adamw(name, parts, w, m, v, tr=256):
    p, r, c = parts.shape
    tr = _pick(r, tr, 16)

    def body(p_ref, w_ref, m_ref, v_ref, g_ref, d_ref, m2_ref, v2_ref):
        g = p_ref[0].astype(F32)
        for d in range(1, p):
            g = g + p_ref[d].astype(F32)
        delta, m2, v2 = _adamw_math(w_ref[...], g, m_ref[...], v_ref[...])
        g_ref[...] = g
        d_ref[...] = delta
        m2_ref[...] = m2
        v2_ref[...] = v2

    tile = pl.BlockSpec((tr, c), lambda i: (i, 0))
    return pl.pallas_call(
        body, name=name, grid=(r // tr,),
        in_specs=[pl.BlockSpec((p, tr, c), lambda i: (0, i, 0)), tile, tile, tile],
        out_specs=[tile] * 4, out_shape=[_sds((r, c))] * 4,
        compiler_params=_params(("parallel",)),
    )(parts, w, m, v)


def _sum_parts(name, parts):
    p, r, c = parts.shape

    def body(p_ref, o_ref):
        g = p_ref[0]
        for d in range(1, p):
            g = g + p_ref[d]
        o_ref[...] = g

    return pl.pallas_call(
        body, name=name, out_shape=_sds((r, c)),
        in_specs=[pl.BlockSpec(memory_space=pltpu.VMEM)], out_specs=pl.BlockSpec(memory_space=pltpu.VMEM),
    )(parts)


WEIGHTS = ["norm_mix_pre", "w_in", "conv_dw_w", "conv_dw_b", "conv_ln_g", "conv_ln_b", "w_conv_branch",
           "b_conv_branch", "w_att_branch", "w_out", "norm_mix_post", "norm_ffn_pre", "w_ffn_up", "w_ffn_down",
           "norm_ffn_post"]
COL_SHARDED = ["w_conv_branch", "w_att_branch"]
ROW_SHARDED = ["w_out", "w_ffn_down"]
TRANSPOSED = ["w_in", "w_ffn_up"]
VECTORS = ["norm_mix_pre", "conv_dw_b", "conv_ln_g", "conv_ln_b", "b_conv_branch", "norm_mix_post",
           "norm_ffn_pre", "norm_ffn_post"]


def _cols_to_full(g):
    return g.transpose(1, 0, 2).reshape(g.shape[1], N_DEV * g.shape[2])


def _full_to_cols(f):
    return f.reshape(f.shape[0], N_DEV, f.shape[1] // N_DEV).transpose(1, 0, 2)


def _pack_vectors(vecs):
    rows = [jnp.pad(vecs[nm].reshape(-1), (0, D_MODEL - vecs[nm].size)) for nm in VECTORS]
    return jnp.stack(rows)


def _unpack_vectors(packed, sizes):
    return {nm: packed[n, :sizes[nm]] for n, nm in enumerate(VECTORS)}


def kernel(x, norm_mix_pre, w_in, conv_dw_w, conv_dw_b, conv_ln_g, conv_ln_b, w_conv_branch, b_conv_branch, w_att_branch, w_out, norm_mix_post, norm_ffn_pre, w_ffn_up, w_ffn_down, norm_ffn_post, loss_target, m_norm_mix_pre, m_w_in, m_conv_dw_w, m_conv_dw_b, m_conv_ln_g, m_conv_ln_b, m_w_conv_branch, m_b_conv_branch, m_w_att_branch, m_w_out, m_norm_mix_post, m_norm_ffn_pre, m_w_ffn_up, m_w_ffn_down, m_norm_ffn_post, v_norm_mix_pre, v_w_in, v_conv_dw_w, v_conv_dw_b, v_conv_ln_g, v_conv_ln_b, v_w_conv_branch, v_b_conv_branch, v_w_att_branch, v_w_out, v_norm_mix_post, v_norm_ffn_pre, v_w_ffn_up, v_w_ffn_down, v_norm_ffn_post):
    ws = dict(zip(WEIGHTS, [norm_mix_pre, w_in, conv_dw_w, conv_dw_b, conv_ln_g, conv_ln_b, w_conv_branch,
                            b_conv_branch, w_att_branch, w_out, norm_mix_post, norm_ffn_pre, w_ffn_up, w_ffn_down,
                            norm_ffn_post]))
    ms = dict(zip(WEIGHTS, [m_norm_mix_pre, m_w_in, m_conv_dw_w, m_conv_dw_b, m_conv_ln_g, m_conv_ln_b,
                            m_w_conv_branch, m_b_conv_branch, m_w_att_branch, m_w_out, m_norm_mix_post,
                            m_norm_ffn_pre, m_w_ffn_up, m_w_ffn_down, m_norm_ffn_post]))
    vs = dict(zip(WEIGHTS, [v_norm_mix_pre, v_w_in, v_conv_dw_w, v_conv_dw_b, v_conv_ln_g, v_conv_ln_b,
                            v_w_conv_branch, v_b_conv_branch, v_w_att_branch, v_w_out, v_norm_mix_post,
                            v_norm_ffn_pre, v_w_ffn_up, v_w_ffn_down, v_norm_ffn_post]))

    dw_block = jnp.pad(conv_dw_w, ((0, 1), (0, 0)))
    g_in, g_dw = _exchange_call("gather_first", _gather_exchange([w_in.T.astype(BF16), dw_block]))
    full = {"w_in": _full_weight("w_in", g_in), "conv_dw_w": _cols_to_full(g_dw)}
    for nm in VECTORS:
        full[nm] = ws[nm].reshape(1, -1)

    loss_local, grad_x, received, grads = _local_step(
        x[0], loss_target[0], full, {nm: (ws[nm].T if nm in TRANSPOSED else ws[nm]).astype(BF16) for nm in LATE})

    loss_at = (VECTORS.index("conv_dw_b"), CONV_DIM)
    small = _exchange_call("gather_small_grads", _gather_exchange(
        [_pack_vectors(grads).at[loss_at].set(loss_local), grads["conv_dw_w"]]))
    out_g, out_d, out_m, out_v = {}, {}, {}, {}
    for nm in LATE + ["w_in"]:
        if nm in TRANSPOSED:
            res = _sum_adamw("adamw_" + nm, received[nm], ws[nm].T, ms[nm].T, vs[nm].T)
            out_g[nm], out_d[nm], out_m[nm], out_v[nm] = [r.T for r in res]
        else:
            out_g[nm], out_d[nm], out_m[nm], out_v[nm] = _sum_adamw("adamw_" + nm, received[nm], ws[nm], ms[nm], vs[nm])
    sizes = {nm: ws[nm].size for nm in VECTORS}
    vec = _sum_adamw("adamw_vectors", small[0], _pack_vectors(ws), _pack_vectors(ms), _pack_vectors(vs))
    for res, dst in zip(vec, (out_g, out_d, out_m, out_v)):
        dst.update(_unpack_vectors(res, sizes))
    loss = vec[0][loss_at]
    dw_full = _sum_parts("sum_dw_grads", small[1])
    me = _slot(*_place())
    dw_mine = lax.dynamic_slice(dw_full, (0, me * (CONV_DIM // N_DEV)), (CONV_WIDTH, CONV_DIM // N_DEV))
    nm = "conv_dw_w"
    out_g[nm], out_d[nm], out_m[nm], out_v[nm] = _sum_adamw("adamw_dw", dw_mine[None], ws[nm], ms[nm], vs[nm])

    outs = [loss, grad_x[None]]
    for group in (out_g, out_d, out_m, out_v):
        outs += [group[nm] for nm in WEIGHTS]
    return tuple(outs)
```

```python
import math

import jax
import jax.numpy as jnp
from jax import lax
from jax.experimental import pallas as pl
from jax.experimental.pallas import tpu as pltpu

F32 = jnp.float32
BF16 = jnp.bfloat16

N_DEV = 8
D_MODEL = 1024
CONV_DIM = 512
CONV_WIDTH = 31
N_HEADS = 8
HEAD_DIM = 64
ATT_DIM = N_HEADS * HEAD_DIM
D_FF = 2816
EPS = 1e-6
IN_SPLITS = (0, 1024, 1536, 2048, 2560, 3584, 4608)

ADAM_LR = 0.001
ADAM_B1 = 0.9
ADAM_B2 = 0.999
ADAM_EPS = 1e-08
ADAM_WD = 0.01
ADAM_STEP = 10

LANES = 128
SUBLANES = 8
HALO = 32
ATT_TILE = 256
ATT_PART = 192
DEAD_SUM = -120.0
VMEM_LIMIT = 56 * 1024 * 1024
MESH = pl.DeviceIdType.MESH
ANY = pl.BlockSpec(memory_space=pl.ANY)


def _pick(dim, target, align=LANES):
    t = min(dim, target)
    t -= t % align
    while t >= align:
        if dim % t == 0:
            return t
        t -= align
    return dim


def _params(semantics):
    return pltpu.CompilerParams(dimension_semantics=semantics, vmem_limit_bytes=VMEM_LIMIT)


def _matmul(a, b, *, name, ta=False, tb=False, out_dtype=F32):
    m, k = (a.shape[1], a.shape[0]) if ta else a.shape
    n, k2 = b.shape if tb else (b.shape[1], b.shape[0])
    assert k == k2, (a.shape, b.shape, ta, tb)
    tm, tn, tk = _pick(m, 1408 if ta else 512), _pick(n, 1536), _pick(k, 1536)
    nk = k // tk
    dims = (((0 if ta else 1,), (1 if tb else 0,)), ((), ()))

    def body(a_ref, b_ref, o_ref, *acc):
        part = lax.dot_general(a_ref[...], b_ref[...], dims, preferred_element_type=F32)
        if nk == 1:
            o_ref[...] = part.astype(o_ref.dtype)
            return
        acc_ref, = acc
        kk = pl.program_id(2)

        @pl.when(kk == 0)
        def _():
            acc_ref[...] = part

        @pl.when((kk > 0) & (kk < nk - 1))
        def _():
            acc_ref[...] += part

        @pl.when(kk == nk - 1)
        def _():
            o_ref[...] = (acc_ref[...] + part).astype(o_ref.dtype)

    a_spec = pl.BlockSpec((tk, tm), lambda j, i, kk: (kk, i)) if ta else pl.BlockSpec((tm, tk), lambda j, i, kk: (i, kk))
    b_spec = (pl.BlockSpec((tn, tk), lambda j, i, kk: (j, kk)) if tb
              else pl.BlockSpec((tk, tn), lambda j, i, kk: (kk, j)))
    return pl.pallas_call(
        body, name=name, grid=(n // tn, m // tm, nk),
        in_specs=[a_spec, b_spec],
        out_specs=pl.BlockSpec((tm, tn), lambda j, i, kk: (i, j)),
        out_shape=jax.ShapeDtypeStruct((m, n), out_dtype),
        scratch_shapes=[pltpu.VMEM((tm, tn), F32)] if nk > 1 else [],
        compiler_params=_params(("parallel", "parallel", "arbitrary")),
    )(a, b)


NO_EXCHANGE = ([], [], [], None)


def _sweep_marks(nt):
    i = pl.program_id(0)
    return i == 0, i == (3 * nt) // 4, i == nt - 1


def _rowwise(name, fn, rows, bcasts, row_outs, red_outs=(), tm=256, exchange=NO_EXCHANGE):
    s = rows[0].shape[0]
    tm = _pick(s, tm, 16)
    nt = s // tm
    resident = pl.Buffered(1)
    nr, nb, no, nd = len(rows), len(bcasts), len(row_outs), len(red_outs)
    x_arrs, x_shape, x_scratch, _ = exchange
    nx = len(x_arrs)
    first_out = nr + nb + nx

    def body(*refs):
        finish_exchange = _carry_exchange(exchange, refs, nr + nb, no + nd, *_sweep_marks(nt))
        ins = [r[...] for r in refs[:nr + nb]]
        outs, reds = fn(*ins)
        for ref, val in zip(refs[first_out:first_out + no], outs):
            ref[...] = val.astype(ref.dtype)
        i = pl.program_id(0)
        for ref, val in zip(refs[first_out + no:first_out + no + nd], reds):
            @pl.when(i == 0)
            def _():
                ref[...] = val

            @pl.when(i > 0)
            def _():
                ref[...] += val
        finish_exchange()

    in_specs = [pl.BlockSpec((tm, r.shape[1]), lambda i: (i, 0)) for r in rows]
    in_specs += [pl.BlockSpec(b.shape, lambda i: (0, 0), pipeline_mode=resident) for b in bcasts]
    out_specs = [pl.BlockSpec((tm, o.shape[1]), lambda i: (i, 0)) for o in row_outs]
    out_specs += [pl.BlockSpec(d.shape, lambda i: (0, 0)) for d in red_outs]
    return pl.pallas_call(
        body, name=name, grid=(nt,), in_specs=in_specs + [ANY] * nx, out_specs=out_specs + [ANY] * nx,
        out_shape=list(row_outs) + list(red_outs) + x_shape, scratch_shapes=x_scratch,
        compiler_params=_params(("arbitrary",)),
    )(*rows, *bcasts, *x_arrs)


def _sds(shape, dtype=F32):
    return jax.ShapeDtypeStruct(shape, dtype)


def _rms(x, g):
    y = x * lax.rsqrt(jnp.mean(x * x, axis=-1, keepdims=True) + EPS)
    return y * g


def _silu(x):
    return x * jax.nn.sigmoid(x)


def _swiglu(g, u):
    return _silu(g) * u


def _ln_silu(u, g, b):
    mu = jnp.mean(u, axis=-1, keepdims=True)
    var = jnp.mean(jnp.square(u - mu), axis=-1, keepdims=True)
    return _silu((u - mu) * lax.rsqrt(var + EPS) * g + b)


def _merge(conv_pre, att_out, g_conv, g_att, b_cb):
    return jax.nn.sigmoid(g_conv) * (conv_pre + b_cb) + jax.nn.sigmoid(g_att) * att_out


def _glu(t):
    return t[:, :CONV_DIM] * jax.nn.sigmoid(t[:, CONV_DIM:])


def _shifted_reader(buf, shifted, tm):
    for b in range(1, SUBLANES):
        shifted[b - 1, :, :] = buf[pl.ds(b, tm + HALO - SUBLANES), :]

    def read(o):
        a, b = divmod(o, SUBLANES)
        return buf[pl.ds(SUBLANES * a, tm), :] if b == 0 else shifted[b - 1, pl.ds(SUBLANES * a, tm), :]

    return read


def _conv_fwd(conv_in, w_pad, b, ln_g, ln_b, exchange, tm=256):
    s = conv_in.shape[0]
    tm = _pick(s, tm, HALO)
    ratio = tm // HALO
    x_arrs, x_shape, x_scratch, _ = exchange
    nx = len(x_arrs)

    def body(*refs):
        main_ref, halo_ref, w_ref, b_ref, g_ref, be_ref = refs[:6]
        u3_ref, u1_ref = refs[6 + nx:8 + nx]
        buf, shifted = refs[-2:]
        finish_exchange = _carry_exchange(exchange, refs, 6, 2, *_sweep_marks(s // tm))
        i = pl.program_id(0)
        buf[0:HALO, :] = _glu(halo_ref[...]) * (i > 0).astype(F32)
        buf[HALO:HALO + tm, :] = _glu(main_ref[...])
        read = _shifted_reader(buf, shifted, tm)
        acc = jnp.zeros((tm, CONV_DIM), F32) + b_ref[...]
        for j in range(CONV_WIDTH):
            acc = acc + w_ref[j:j + 1, :] * read(HALO - (CONV_WIDTH - 1) + j)
        u1_ref[...] = acc
        u3_ref[...] = _ln_silu(acc, g_ref[...], be_ref[...]).astype(u3_ref.dtype)
        finish_exchange()

    res = pl.pallas_call(
        body, name="conv_fwd", grid=(s // tm,),
        in_specs=[pl.BlockSpec((tm, 2 * CONV_DIM), lambda i: (i, 0)),
                  pl.BlockSpec((HALO, 2 * CONV_DIM), lambda i: (jnp.maximum(i * ratio - 1, 0), 0)),
                  pl.BlockSpec(w_pad.shape, lambda i: (0, 0)),
                  pl.BlockSpec(b.shape, lambda i: (0, 0)),
                  pl.BlockSpec(ln_g.shape, lambda i: (0, 0)),
                  pl.BlockSpec(ln_b.shape, lambda i: (0, 0))] + [ANY] * nx,
        out_specs=[pl.BlockSpec((tm, CONV_DIM), lambda i: (i, 0)),
                   pl.BlockSpec((tm, CONV_DIM), lambda i: (i, 0))] + [ANY] * nx,
        out_shape=[_sds((s, CONV_DIM), BF16), _sds((s, CONV_DIM), F32)] + x_shape,
        scratch_shapes=x_scratch + [pltpu.VMEM((tm + HALO, CONV_DIM), F32),
                                    pltpu.VMEM((SUBLANES - 1, tm + HALO - SUBLANES, CONV_DIM), F32)],
        compiler_params=_params(("arbitrary",)),
    )(conv_in, conv_in, w_pad, b, ln_g, ln_b, *x_arrs)
    return res[0], res[1], res[2:]


def _conv_bwd(conv_in, u1, du3, ln_g, ln_b, w_pad, exchange, tm=256):
    s = conv_in.shape[0]
    tm = _pick(s, tm, HALO)
    ratio = tm // HALO
    nt = s // tm
    last_halo = s // HALO - 1
    x_arrs, x_shape, x_scratch, _ = exchange
    nx = len(x_arrs)

    def body(*refs):
        main_ref, halo_ref, u1_ref, u1n_ref, du3_ref, du3n_ref, g_ref, be_ref, w_ref = refs[:9]
        dci_ref, dw_ref, db_ref, dg_ref, dbe_ref = refs[9 + nx:14 + nx]
        ubuf, dbuf, ushift, dshift = refs[-4:]
        finish_exchange = _carry_exchange(exchange, refs, 9, 5, *_sweep_marks(nt))
        i = pl.program_id(0)
        main = main_ref[...]
        a = main[:, :CONV_DIM]
        sb = jax.nn.sigmoid(main[:, CONV_DIM:])
        ubuf[0:HALO, :] = _glu(halo_ref[...]) * (i > 0).astype(F32)
        ubuf[HALO:HALO + tm, :] = a * sb

        def ln_bwd(u1t, du3t):
            _, vjp = jax.vjp(_ln_silu, u1t, g_ref[...], be_ref[...])
            return vjp(du3t)

        du, dg, dbe = ln_bwd(u1_ref[...], du3_ref[...])
        dbuf[0:tm, :] = du
        dbuf[tm:tm + HALO, :] = ln_bwd(u1n_ref[...], du3n_ref[...])[0] * (i < nt - 1).astype(F32)

        @pl.when(i == 0)
        def _():
            dw_ref[...] = jnp.zeros_like(dw_ref)
            db_ref[...] = jnp.zeros_like(db_ref)
            dg_ref[...] = jnp.zeros_like(dg_ref)
            dbe_ref[...] = jnp.zeros_like(dbe_ref)

        dg_ref[...] += dg
        dbe_ref[...] += dbe

        read_u = _shifted_reader(ubuf, ushift, tm)
        read_d = _shifted_reader(dbuf, dshift, tm)
        du0 = jnp.zeros((tm, CONV_DIM), F32)
        for j in range(CONV_WIDTH):
            du0 = du0 + w_ref[j:j + 1, :] * read_d(CONV_WIDTH - 1 - j)
            dw_ref[j:j + 1, :] += jnp.sum(du * read_u(HALO - (CONV_WIDTH - 1) + j), axis=0, keepdims=True)
        db_ref[...] += jnp.sum(du, axis=0, keepdims=True)
        dci_ref[:, :CONV_DIM] = (du0 * sb).astype(dci_ref.dtype)
        dci_ref[:, CONV_DIM:] = (du0 * a * sb * (1.0 - sb)).astype(dci_ref.dtype)
        finish_exchange()

    res = pl.pallas_call(
        body, name="conv_bwd", grid=(nt,),
        in_specs=[pl.BlockSpec((tm, 2 * CONV_DIM), lambda i: (i, 0)),
                  pl.BlockSpec((HALO, 2 * CONV_DIM), lambda i: (jnp.maximum(i * ratio - 1, 0), 0))]
        + [pl.BlockSpec((tm, CONV_DIM), lambda i: (i, 0)),
           pl.BlockSpec((HALO, CONV_DIM), lambda i: (jnp.minimum((i + 1) * ratio, last_halo), 0))] * 2
        + [pl.BlockSpec((1, CONV_DIM), lambda i: (0, 0))] * 2 + [pl.BlockSpec(w_pad.shape, lambda i: (0, 0))]
        + [ANY] * nx,
        out_specs=[pl.BlockSpec((tm, 2 * CONV_DIM), lambda i: (i, 0)),
                   pl.BlockSpec(w_pad.shape, lambda i: (0, 0))]
        + [pl.BlockSpec((1, CONV_DIM), lambda i: (0, 0))] * 3 + [ANY] * nx,
        out_shape=[_sds((s, 2 * CONV_DIM), BF16), _sds(w_pad.shape)] + [_sds((1, CONV_DIM))] * 3 + x_shape,
        scratch_shapes=x_scratch + [pltpu.VMEM((tm + HALO, CONV_DIM), F32)] * 2
        + [pltpu.VMEM((SUBLANES - 1, tm + HALO - SUBLANES, CONV_DIM), F32)] * 2,
        compiler_params=_params(("arbitrary",)),
    )(conv_in, conv_in, u1, u1, du3, du3, ln_g, ln_b, w_pad, *x_arrs)
    return res[:5], res[5:]


def _logsig_neg(z):
    return jnp.minimum(-z, 0.0) - jnp.log(1.0 + jnp.exp(-jnp.abs(z)))


def _split_dot(val, tri):
    hi = val.astype(BF16)
    lo = (val - hi.astype(F32)).astype(BF16)
    return jnp.dot(hi, tri, preferred_element_type=F32) + jnp.dot(lo, tri, preferred_element_type=F32)


def _attn_masks(t, later):
    row = lax.broadcasted_iota(jnp.int32, (t, t), 0)
    col = lax.broadcasted_iota(jnp.int32, (t, t), 1)
    tri = jnp.where(row > col if later else row <= col, 1.0, 0.0).astype(BF16)
    return col < row, tri


def _grid_marks(h, nq):
    hh, i = pl.program_id(0), pl.program_id(1)
    return (hh == 0) & (i == 0), (hh == (3 * h) // 4) & (i == 0), (hh == h - 1) & (i == nq - 1)


def _head_masks(shape):
    lane = lax.broadcasted_iota(jnp.int32, shape, len(shape) - 1)
    return lane < HEAD_DIM, lane >= HEAD_DIM


def _per_head(blk):
    m0, m1 = _head_masks(blk.shape)
    zero = jnp.zeros_like(blk)
    return jnp.where(m0, blk, zero), jnp.where(m1, blk, zero)


NT = (((1,), (1,)), ((), ()))
TN = (((0,), (0,)), ((), ()))


def _with_top(whole, top):
    rows = top.shape[0]
    return top if rows == whole.shape[0] else jnp.concatenate([top, whole[rows:]], axis=0)


def _attn_fwd(q, k, v, exchange):
    s = q.shape[0]
    hp = q.shape[1] // LANES
    t = ATT_TILE
    scale = 1.0 / math.sqrt(HEAD_DIM)
    x_arrs, x_shape, x_scratch, _ = exchange
    nx = len(x_arrs)

    def body(*refs):
        q_ref, k_ref, v_ref = refs[:3]
        o_ref, lt_ref, nb_ref = refs[3 + nx:6 + nx]
        finish_exchange = _carry_exchange(exchange, refs, 3, 3, *_grid_marks(hp, s // t))
        i = pl.program_id(1)
        qs = _per_head((q_ref[...].astype(F32) * scale).astype(BF16))
        causal, tri = _attn_masks(t, later=True)

        def step(kb, carry, masked, rows):
            cs, acc = carry
            off = pl.multiple_of(kb * t, t)
            kblk = k_ref[pl.ds(off, t), :]
            vs = _per_head(v_ref[pl.ds(off, t), :])
            acc_top = acc[:rows]
            new_cs = []
            for hd in range(2):
                z = lax.dot_general(qs[hd][:rows], kblk, NT, preferred_element_type=F32)
                l = _logsig_neg(z)
                if masked:
                    l = jnp.where(causal, l, 0.0)
                e = z + l + _split_dot(l, tri) + cs[hd][:rows]
                if masked:
                    e = jnp.where(causal, e, -1e30)
                acc_top = acc_top + jnp.dot(jnp.exp(e).astype(BF16), vs[hd], preferred_element_type=F32)
                new_cs.append(_with_top(cs[hd], cs[hd][:rows] + jnp.sum(l, axis=1, keepdims=True)))
            return tuple(new_cs), _with_top(acc, acc_top)

        zero = jnp.zeros((t, 1), F32)
        carry = step(i, ((zero, zero), jnp.zeros((t, LANES), F32)), True, t)

        def live(cs, lo, hi):
            return jnp.maximum(jnp.max(cs[0][lo:hi]), jnp.max(cs[1][lo:hi])) > DEAD_SUM

        def more(state):
            n, _, (cs, _) = state
            return (n < i) & live(cs, 0, t)

        def sweep(state):
            n, n_full, cr = state
            whole = live(cr[0], ATT_PART, t)
            cr = lax.cond(whole, lambda c: step(i - 1 - n, c, False, t), lambda c: step(i - 1 - n, c, False, ATT_PART), cr)
            return n + 1, n_full + whole.astype(jnp.int32), cr

        n_blocks, n_full, carry = lax.while_loop(more, sweep, (jnp.int32(0), jnp.int32(0), carry))
        m0, _ = _head_masks((t, LANES))
        lt_ref[...] = jnp.where(m0, carry[0][0], carry[0][1])
        o_ref[...] = carry[1].astype(o_ref.dtype)
        nb_ref[0, pl.program_id(0), i] = n_blocks.astype(F32)
        nb_ref[1, pl.program_id(0), i] = n_full.astype(F32)
        finish_exchange()

    res = pl.pallas_call(
        body, name="attn_fwd", grid=(hp, s // t),
        in_specs=[pl.BlockSpec((t, LANES), lambda p, i: (i, p)),
                  pl.BlockSpec((s, LANES), lambda p, i: (0, p)),
                  pl.BlockSpec((s, LANES), lambda p, i: (0, p))] + [ANY] * nx,
        out_specs=[pl.BlockSpec((t, LANES), lambda p, i: (i, p)),
                   pl.BlockSpec((None, t, LANES), lambda p, i: (p, i, 0)),
                   pl.BlockSpec(memory_space=pltpu.SMEM)] + [ANY] * nx,
        out_shape=[_sds(q.shape, BF16), _sds((hp, s, LANES), F32), _sds((2, hp, s // t), F32)] + x_shape,
        scratch_shapes=x_scratch,
        compiler_params=_params(("arbitrary", "arbitrary")),
    )(q, k, v, *x_arrs)
    return res[0], res[1], res[2], res[3:]


def _attn_bwd(q, k, v, do, ltot, n_blocks, exchange):
    s = q.shape[0]
    hp = q.shape[1] // LANES
    t = ATT_TILE
    scale = 1.0 / math.sqrt(HEAD_DIM)
    x_arrs, x_shape, x_scratch, _ = exchange
    nx = len(x_arrs)

    def body(*refs):
        q_ref, k_ref, v_ref, do_ref, lt_ref, nb_ref = refs[:6]
        dq_ref, dk_ref, dv_ref = refs[6 + nx:9 + nx]
        finish_exchange = _carry_exchange(exchange, refs, 6, 3, *_grid_marks(hp, s // t))
        i = pl.program_id(1)
        n_blocks = jnp.clip(nb_ref[0, pl.program_id(0), i].astype(jnp.int32), 0, i)
        n_full = jnp.clip(nb_ref[1, pl.program_id(0), i].astype(jnp.int32), 0, n_blocks)

        @pl.when(i == 0)
        def _():
            dk_ref[...] = jnp.zeros_like(dk_ref)
            dv_ref[...] = jnp.zeros_like(dv_ref)

        qb = q_ref[...]
        qm = _per_head(qb)
        qs = _per_head((qb.astype(F32) * scale).astype(BF16))
        dos = _per_head(do_ref[...])
        lts = (lt_ref[:, 0:1], lt_ref[:, HEAD_DIM:HEAD_DIM + 1])
        causal, tri = _attn_masks(t, later=False)

        def step(kb, carry, masked, rows):
            cls, cgs, dq = carry
            off = pl.multiple_of(kb * t, t)
            kblk = k_ref[pl.ds(off, t), :]
            vblk = v_ref[pl.ds(off, t), :]
            ks = _per_head(kblk)
            dq_top = dq[:rows]
            dk = jnp.zeros((t, LANES), F32)
            dv = jnp.zeros((t, LANES), F32)
            new_cls, new_cgs = [], []
            for hd in range(2):
                z = lax.dot_general(qs[hd][:rows], kblk, NT, preferred_element_type=F32)
                l = _logsig_neg(z)
                if masked:
                    l = jnp.where(causal, l, 0.0)
                e = z + l + ((lts[hd][:rows] - cls[hd][:rows]) - _split_dot(l, tri))
                if masked:
                    e = jnp.where(causal, e, -1e30)
                a = jnp.exp(e)
                g = lax.dot_general(dos[hd][:rows], vblk, NT, preferred_element_type=F32) * a
                p = cgs[hd][:rows] + jnp.dot(g.astype(BF16), tri, preferred_element_type=F32) - g
                el = jnp.exp(l)
                dz = g * el - p * (1.0 - el)
                if masked:
                    dz = jnp.where(causal, dz, 0.0)
                dzb = (dz * scale).astype(BF16)
                dq_top = dq_top + jnp.dot(dzb, ks[hd], preferred_element_type=F32)
                dk = dk + lax.dot_general(dzb, qm[hd][:rows], TN, preferred_element_type=F32)
                dv = dv + lax.dot_general(a.astype(BF16), dos[hd][:rows], TN, preferred_element_type=F32)
                new_cls.append(_with_top(cls[hd], cls[hd][:rows] + jnp.sum(l, axis=1, keepdims=True)))
                new_cgs.append(_with_top(cgs[hd], cgs[hd][:rows] + jnp.sum(g, axis=1, keepdims=True)))
            dk_ref[pl.ds(off, t), :] += dk
            dv_ref[pl.ds(off, t), :] += dv
            return tuple(new_cls), tuple(new_cgs), _with_top(dq, dq_top)

        zero = jnp.zeros((t, 1), F32)
        init = ((zero, zero), (zero, zero), jnp.zeros((t, LANES), F32))
        carry = lax.fori_loop(i - n_blocks, i - n_full, lambda kb, cr: step(kb, cr, False, ATT_PART), init)
        carry = lax.fori_loop(i - n_full, i, lambda kb, cr: step(kb, cr, False, t), carry)
        carry = step(i, carry, True, t)
        dq_ref[...] = carry[2]
        finish_exchange()

    blk = pl.BlockSpec((t, LANES), lambda p, i: (i, p))
    whole = pl.BlockSpec((s, LANES), lambda p, i: (0, p))
    res = pl.pallas_call(
        body, name="attn_bwd", grid=(hp, s // t),
        in_specs=[blk, whole, whole, blk, pl.BlockSpec((None, t, LANES), lambda p, i: (p, i, 0)),
                  pl.BlockSpec(memory_space=pltpu.SMEM)] + [ANY] * nx,
        out_specs=[blk, whole, whole] + [ANY] * nx,
        out_shape=[_sds(q.shape)] * 3 + x_shape,
        scratch_shapes=x_scratch,
        compiler_params=_params(("arbitrary", "arbitrary")),
    )(q, k, v, do, ltot, n_blocks, *x_arrs)
    return res[0], res[1], res[2], res[3:]


LATE = ["w_conv_branch", "w_att_branch", "w_out", "w_ffn_up", "w_ffn_down"]


def _full_weight(name, gathered):
    return _cols_to_full(gathered) if name in COL_SHARDED else gathered.reshape(-1, gathered.shape[2])


def _grad_slabs(name, grad):
    return _full_to_cols(grad) if name in COL_SHARDED else grad.reshape(N_DEV, -1, grad.shape[1])


def _side_slabs(name, grad):
    slabs = _grad_slabs(name, grad)
    return slabs.reshape((4, 2) + slabs.shape[1:])


def _local_step(x, target, w, late_blocks):
    s = x.shape[0]
    w = dict(w)
    g1, g2, g3, g4 = w["norm_mix_pre"], w["norm_mix_post"], w["norm_ffn_pre"], w["norm_ffn_post"]

    w_in = w["w_in"]

    def proj_fn(xt, g1_, w_in_t):
        h = _rms(xt, g1_).astype(BF16)
        proj = lax.dot_general(h, w_in_t, NT, preferred_element_type=F32)
        return (h, *[proj[:, IN_SPLITS[n]:IN_SPLITS[n + 1]] for n in range(6)]), ()

    mix_weights = ["w_conv_branch", "w_att_branch", "w_out"]
    h1, conv_in, q, k, v, g_conv, g_att, g_down = _rowwise(
        "norm_proj", proj_fn, [x], [g1, w_in],
        [_sds((s, D_MODEL), BF16), _sds((s, 2 * CONV_DIM)), _sds((s, ATT_DIM), BF16), _sds((s, ATT_DIM), BF16),
         _sds((s, ATT_DIM), BF16), _sds((s, D_MODEL)), _sds((s, D_MODEL))], tm=512,
        exchange=_gather_exchange([late_blocks["w_ffn_down"]]))
    w["w_ffn_down"] = _full_weight("w_ffn_down", g_down)

    u3, u1, gathered = _conv_fwd(conv_in, w["conv_dw_w"], w["conv_dw_b"], w["conv_ln_g"], w["conv_ln_b"],
                                 _gather_exchange([late_blocks[nm] for nm in mix_weights]))
    for nm, g in zip(mix_weights, gathered):
        w[nm] = _full_weight(nm, g)
    att, ltot, n_blocks, (g_up,) = _attn_fwd(q, k, v, _gather_exchange([late_blocks["w_ffn_up"]]))
    w["w_ffn_up"] = _full_weight("w_ffn_up", g_up)

    def merge_fn(u3t, at, gc, ga, xt, w_cb, w_ab, b_cb, w_out, g2_, g3_):
        cp = jnp.dot(u3t, w_cb, preferred_element_type=F32)
        ao = jnp.dot(at, w_ab, preferred_element_type=F32)
        mg = _merge(cp, ao, gc, ga, b_cb).astype(BF16)
        mix_ = jnp.dot(mg, w_out, preferred_element_type=F32)
        x2_ = xt + _rms(mix_, g2_)
        return (mg, cp, ao, mix_, x2_, _rms(x2_, g3_)), ()

    merged, conv_pre, att_out, mix, x2, h2 = _rowwise(
        "branch_merge_mix", merge_fn, [u3, att, g_conv, g_att, x],
        [w["w_conv_branch"], w["w_att_branch"], w["b_conv_branch"], w["w_out"], g2, g3],
        [_sds((s, D_MODEL), BF16)] * 3 + [_sds((s, D_MODEL)), _sds((s, D_MODEL)), _sds((s, D_MODEL), BF16)], tm=512)

    def ffn_up_fn(ht, w_up_t):
        gu_ = lax.dot_general(ht, w_up_t, NT, preferred_element_type=F32)
        return (gu_, _swiglu(gu_[:, :D_FF], gu_[:, D_FF:])), ()

    gu, act = _rowwise("ffn_up", ffn_up_fn, [h2], [w["w_ffn_up"]],
                       [_sds((s, 2 * D_FF), BF16), _sds((s, D_FF), BF16)], tm=512)

    def final_fn(at, x2t, tgt, w_down, g4_):
        ff = jnp.dot(at, w_down, preferred_element_type=F32)
        n4, vjp = jax.vjp(_rms, ff, g4_)
        err = x2t + n4 - tgt
        dy = err * (1.0 / D_MODEL)
        dff, dg4 = vjp(dy)
        return (dy, dff), (jnp.sum(err * err, axis=0, keepdims=True), dg4)

    dy, dff, loss_cols, d_g4 = _rowwise("ffn_down_loss", final_fn, [act, x2, target], [w["w_ffn_down"], g4],
                                        [_sds((s, D_MODEL)), _sds((s, D_MODEL), BF16)],
                                        [_sds((1, D_MODEL)), _sds((1, D_MODEL))], tm=512)
    loss = 0.5 * jnp.sum(loss_cols) / D_MODEL

    d_w_down = _matmul(act, dff, ta=True, name="d_w_down", out_dtype=BF16)

    def act_bwd_fn(dfft, gut, w_down):
        d_act = lax.dot_general(dfft, w_down, NT, preferred_element_type=F32)
        gu_ = gut.astype(F32)
        _, vjp = jax.vjp(_swiglu, gu_[:, :D_FF], gu_[:, D_FF:])
        return (jnp.concatenate(vjp(d_act), axis=1),), ()

    down_slabs = _side_slabs("w_ffn_down", d_w_down)
    dgu, theirs = _rowwise("ffn_act_bwd", act_bwd_fn, [dff, gu], [w["w_ffn_down"]], [_sds((s, 2 * D_FF), BF16)],
                           exchange=_pair_exchange([down_slabs]))
    down_sums = _pair_sum("pair_sum_w_ffn_down", down_slabs, theirs)
    d_w_up = _matmul(dgu, h2, ta=True, name="d_w_up", out_dtype=BF16)
    received = {}
    up_slabs = _side_slabs("w_ffn_up", d_w_up)

    def mid_bwd_fn(dgut, xt, mt, dyt, w_up_t, g2_, g3_):
        dh = jnp.dot(dgut, w_up_t, preferred_element_type=F32)
        n2, vjp2 = jax.vjp(_rms, mt, g2_)
        x2_ = xt + n2
        _, vjp3 = jax.vjp(_rms, x2_, g3_)
        dx2_, dg3 = vjp3(dh)
        dx2_ = dx2_ + dyt
        dmix_, dg2 = vjp2(dx2_)
        return (dx2_, dmix_), (dg2, dg3)

    dx2, dmix, d_g2, d_g3, received["w_ffn_down"] = _rowwise(
        "ffn_up_mid_bwd", mid_bwd_fn, [dgu, x, mix, dy], [w["w_ffn_up"], g2, g3],
        [_sds((s, D_MODEL)), _sds((s, D_MODEL), BF16)], [_sds((1, D_MODEL)), _sds((1, D_MODEL))], tm=512,
        exchange=_chip_exchange([down_sums]))
    d_w_out = _matmul(merged, dmix, ta=True, name="d_w_out", out_dtype=BF16)

    def merge_bwd_fn(dmt, cp, ao, gc, ga, w_out, w_cb, w_ab, b_cb):
        dm = lax.dot_general(dmt, w_out, NT, preferred_element_type=F32)
        _, vjp = jax.vjp(_merge, cp.astype(F32), ao.astype(F32), gc, ga, b_cb)
        dcp, dao, dgc, dga, dbias = vjp(dm)
        dcp, dao = dcp.astype(BF16), dao.astype(BF16)
        du3_ = lax.dot_general(dcp, w_cb, NT, preferred_element_type=F32)
        datt_ = lax.dot_general(dao, w_ab, NT, preferred_element_type=F32)
        return (dcp, dao, dgc, dga, du3_, datt_), (dbias,)

    d_conv_out, d_att_out, d_g_conv, d_g_att, du3, d_att, d_b_cb, theirs = _rowwise(
        "merge_bwd", merge_bwd_fn, [dmix, conv_pre, att_out, g_conv, g_att],
        [w["w_out"], w["w_conv_branch"], w["w_att_branch"], w["b_conv_branch"]],
        [_sds((s, D_MODEL), BF16)] * 4 + [_sds((s, CONV_DIM)), _sds((s, ATT_DIM), BF16)], [_sds((1, D_MODEL))], tm=512,
        exchange=_pair_exchange([up_slabs]))

    d_w_cb = _matmul(u3, d_conv_out, ta=True, name="d_w_conv_branch", out_dtype=BF16)
    d_w_ab = _matmul(att, d_att_out, ta=True, name="d_w_att_branch", out_dtype=BF16)

    dq, dk, dv, (received["w_ffn_up"],) = _attn_bwd(
        q, k, v, d_att, ltot, n_blocks, _chip_exchange([_pair_sum("pair_sum_w_ffn_up", up_slabs, theirs)]))

    mix_grads = {"w_conv_branch": d_w_cb, "w_att_branch": d_w_ab, "w_out": d_w_out}
    (d_conv_in, d_dw_w, d_dw_b, d_ln_g, d_ln_b), landed = _conv_bwd(
        conv_in, u1, du3, w["conv_ln_g"], w["conv_ln_b"], w["conv_dw_w"],
        _scatter_exchange([_grad_slabs(nm, mix_grads[nm]) for nm in mix_weights]))
    received.update(zip(mix_weights, landed))

    d_proj = jnp.concatenate([d_conv_in, dq.astype(BF16), dk.astype(BF16), dv.astype(BF16), d_g_conv, d_g_att],
                             axis=1)
    d_w_in = _matmul(d_proj, h1, ta=True, name="d_w_in", out_dtype=BF16)
    in_slabs = _side_slabs("w_in", d_w_in)
    (theirs,) = _exchange_call("pair_swap_w_in", _pair_exchange([in_slabs]))

    def pre_bwd_fn(dpt, xt, dx2t, w_in_t, g_):
        dh = jnp.dot(dpt, w_in_t, preferred_element_type=F32)
        _, vjp = jax.vjp(_rms, xt, g_)
        dx_, dg_ = vjp(dh)
        return (dx_ + dx2t,), (dg_,)

    grad_x, d_g1, received["w_in"] = _rowwise(
        "proj_norm_bwd", pre_bwd_fn, [d_proj, x, dx2], [w_in, g1], [_sds((s, D_MODEL))], [_sds((1, D_MODEL))], tm=512,
        exchange=_chip_exchange([_pair_sum("pair_sum_w_in", in_slabs, theirs)]))

    grads = {
        "norm_mix_pre": d_g1, "conv_dw_w": d_dw_w, "conv_dw_b": d_dw_b,
        "conv_ln_g": d_ln_g, "conv_ln_b": d_ln_b, "b_conv_branch": d_b_cb,
        "norm_mix_post": d_g2, "norm_ffn_pre": d_g3, "norm_ffn_post": d_g4,
    }
    return loss, grad_x, received, grads


def _place():
    x, y, c = lax.axis_index("x"), lax.axis_index("y"), lax.axis_index("c")
    return x, y, c


def _slot(px, py, pc):
    return 4 * px + 2 * py + pc


def _exchange_scratch(n):
    return [pltpu.SemaphoreType.DMA((7 * n,)), pltpu.SemaphoreType.DMA((7 * n,)), pltpu.SemaphoreType.DMA((n,))]


def _gather_exchange(arrs):
    n = len(arrs)

    def phases(ins, outs, send_sems, recv_sems, local_sems):
        x, y, c = _place()
        me, sibling = (x, y, c), (x, y, 1 - c)
        chips = [(1 - x, y), (x, 1 - y), (1 - x, 1 - y)]

        def copy(a, kk, block, to, src=None):
            dst = outs[a].at[_slot(*block)]
            return pltpu.make_async_remote_copy(
                src_ref=dst if src is None else src, dst_ref=dst,
                send_sem=send_sems.at[a * 7 + kk], recv_sem=recv_sems.at[a * 7 + kk],
                device_id=to, device_id_type=MESH)

        mine = [pltpu.make_async_copy(ins[a], outs[a].at[_slot(*me)], local_sems.at[a]) for a in range(n)]
        first = []
        for a in range(n):
            first.append(copy(a, 0, me, sibling, src=ins[a]))
            first += [copy(a, 1 + j, me, (*chip, c), src=ins[a]) for j, chip in enumerate(chips)]
        passed = [copy(a, 4 + j, (*chip, c), sibling) for j, chip in enumerate(chips) for a in range(n)]

        def send():
            for cp in mine + first:
                cp.start()

        def pass_on():
            for j, chip in enumerate(chips):
                for a in range(n):
                    copy(a, 1 + j, (*chip, c), me).wait_recv()
                    passed[j * n + a].start()

        def finish():
            for a in range(n):
                copy(a, 0, sibling, me).wait_recv()
                for j, chip in enumerate(chips):
                    copy(a, 4 + j, (*chip, 1 - c), me).wait_recv()
            for cp in first + passed:
                cp.wait_send()
            for cp in mine:
                cp.wait()

        return [send, pass_on, finish]

    return list(arrs), [_sds((N_DEV,) + a.shape, a.dtype) for a in arrs], _exchange_scratch(n), phases


def _scatter_exchange(arrs):
    n = len(arrs)
    flips = [(fx, fy, fc) for fx in (0, 1) for fy in (0, 1) for fc in (0, 1)][1:]

    def phases(ins, outs, send_sems, recv_sems, local_sems):
        x, y, c = _place()
        mine = _slot(x, y, c)
        local = [pltpu.make_async_copy(ins[a].at[mine], outs[a].at[mine], local_sems.at[a]) for a in range(n)]
        peers = [((1 - x) if fx else x, (1 - y) if fy else y, (1 - c) if fc else c) for fx, fy, fc in flips]

        def copy(a, kk, src_slot, dst_slot):
            return pltpu.make_async_remote_copy(
                src_ref=ins[a].at[src_slot], dst_ref=outs[a].at[dst_slot],
                send_sem=send_sems.at[a * 7 + kk], recv_sem=recv_sems.at[a * 7 + kk],
                device_id=peers[kk], device_id_type=MESH)

        sends = [copy(a, kk, _slot(*peers[kk]), mine) for a in range(n) for kk in range(7)]

        def send():
            for cp in local + sends:
                cp.start()

        def finish():
            for a in range(n):
                for kk in range(7):
                    copy(a, kk, mine, _slot(*peers[kk])).wait_recv()
            for cp in sends:
                cp.wait_send()
            for cp in local:
                cp.wait()

        return [send, finish]

    return list(arrs), [_sds(a.shape, a.dtype) for a in arrs], _exchange_scratch(n), phases


def _pair_exchange(arrs):
    n = len(arrs)

    def phases(ins, outs, send_sems, recv_sems, local_sems):
        x, y, c = _place()

        def copy(a, chip, side):
            return pltpu.make_async_remote_copy(
                src_ref=ins[a].at[chip, side], dst_ref=outs[a].at[chip],
                send_sem=send_sems.at[a * 7 + chip], recv_sem=recv_sems.at[a * 7 + chip],
                device_id=(x, y, 1 - c), device_id_type=MESH)

        sends = [copy(a, chip, 1 - c) for a in range(n) for chip in range(4)]

        def send():
            for cp in sends:
                cp.start()

        def finish():
            for a in range(n):
                for chip in range(4):
                    copy(a, chip, c).wait_recv()
            for cp in sends:
                cp.wait_send()

        return [send, finish]

    return list(arrs), [_sds((4,) + a.shape[2:], a.dtype) for a in arrs], _exchange_scratch(n), phases


def _chip_exchange(arrs):
    n = len(arrs)

    def phases(ins, outs, send_sems, recv_sems, local_sems):
        x, y, c = _place()
        mine = 2 * x + y
        chips = [(1 - x, y), (x, 1 - y), (1 - x, 1 - y)]
        local = [pltpu.make_async_copy(ins[a].at[mine], outs[a].at[mine], local_sems.at[a]) for a in range(n)]

        def copy(a, j, src_slot, dst_slot):
            return pltpu.make_async_remote_copy(
                src_ref=ins[a].at[src_slot], dst_ref=outs[a].at[dst_slot],
                send_sem=send_sems.at[a * 7 + j], recv_sem=recv_sems.at[a * 7 + j],
                device_id=(*chips[j], c), device_id_type=MESH)

        sends = [copy(a, j, 2 * chips[j][0] + chips[j][1], mine) for a in range(n) for j in range(3)]

        def send():
            for cp in local + sends:
                cp.start()

        def finish():
            for a in range(n):
                for j in range(3):
                    copy(a, j, mine, 2 * chips[j][0] + chips[j][1]).wait_recv()
            for cp in sends:
                cp.wait_send()
            for cp in local:
                cp.wait()

        return [send, finish]

    return list(arrs), [_sds(a.shape, a.dtype) for a in arrs], _exchange_scratch(n), phases


def _pair_sum(name, mine, theirs):
    _, _, r, c = mine.shape

    def body(side_ref, m_ref, t_ref, o_ref):
        o_ref[...] = (m_ref[...].astype(F32) + t_ref[...].astype(F32)).astype(o_ref.dtype)

    return pl.pallas_call(
        body, name=name,
        grid_spec=pltpu.PrefetchScalarGridSpec(
            num_scalar_prefetch=1, grid=(4,),
            in_specs=[pl.BlockSpec((None, None, r, c), lambda j, side: (j, side[0], 0, 0)),
                      pl.BlockSpec((None, r, c), lambda j, side: (j, 0, 0))],
            out_specs=pl.BlockSpec((None, r, c), lambda j, side: (j, 0, 0))),
        out_shape=_sds(theirs.shape, theirs.dtype),
        compiler_params=_params(("parallel",)),
    )(lax.axis_index("c").astype(jnp.int32).reshape(1), mine, theirs)


def _exchange_call(name, exchange):
    arrs, out_shape, scratch, phases = exchange
    n = len(arrs)

    def body(*refs):
        for step in phases(refs[:n], refs[n:2 * n], *refs[2 * n:]):
            step()

    return pl.pallas_call(body, name=name, in_specs=[ANY] * n, out_specs=[ANY] * n,
                          out_shape=out_shape, scratch_shapes=scratch)(*arrs)


def _carry_exchange(exchange, refs, n_in, n_out, first, middle, last):
    arrs, _, _, phases = exchange
    n = len(arrs)
    if n == 0:
        return lambda: None
    ins = refs[n_in:n_in + n]
    outs = refs[n_in + n + n_out:n_in + 2 * n + n_out]
    sems = n_in + 2 * n + n_out
    steps = phases(ins, outs, *refs[sems:sems + 3])
    pl.when(first)(steps[0])
    if len(steps) == 3:
        pl.when(middle)(steps[1])
    return lambda: pl.when(last)(steps[-1])


def _adamw_math(w, g, m, v):
    m2 = ADAM_B1 * m + (1.0 - ADAM_B1) * g
    v2 = ADAM_B2 * v + (1.0 - ADAM_B2) * jnp.square(g)
    m_hat = m2 / (1.0 - ADAM_B1 ** ADAM_STEP)
    v_hat = v2 / (1.0 - ADAM_B2 ** ADAM_STEP)
    delta = -ADAM_LR * (m_hat / (jnp.sqrt(v_hat) + ADAM_EPS) + ADAM_WD * w)
    return delta, m2, v2


def _sum_adamw(name, parts, w, m, v, tr=256):
    p, r, c = parts.shape
    tr = _pick(r, tr, 16)

    def body(p_ref, w_ref, m_ref, v_ref, g_ref, d_ref, m2_ref, v2_ref):
        g = p_ref[0].astype(F32)
        for d in range(1, p):
            g = g + p_ref[d].astype(F32)
        delta, m2, v2 = _adamw_math(w_ref[...], g, m_ref[...], v_ref[...])
        g_ref[...] = g
        d_ref[...] = delta
        m2_ref[...] = m2
        v2_ref[...] = v2

    tile = pl.BlockSpec((tr, c), lambda i: (i, 0))
    return pl.pallas_call(
        body, name=name, grid=(r // tr,),
        in_specs=[pl.BlockSpec((p, tr, c), lambda i: (0, i, 0)), tile, tile, tile],
        out_specs=[tile] * 4, out_shape=[_sds((r, c))] * 4,
        compiler_params=_params(("parallel",)),
    )(parts, w, m, v)


def _sum_parts(name, parts):
    p, r, c = parts.shape

    def body(p_ref, o_ref):
        g = p_ref[0]
        for d in range(1, p):
            g = g + p_ref[d]
        o_ref[...] = g

    return pl.pallas_call(
        body, name=name, out_shape=_sds((r, c)),
        in_specs=[pl.BlockSpec(memory_space=pltpu.VMEM)], out_specs=pl.BlockSpec(memory_space=pltpu.VMEM),
    )(parts)


WEIGHTS = ["norm_mix_pre", "w_in", "conv_dw_w", "conv_dw_b", "conv_ln_g", "conv_ln_b", "w_conv_branch",
           "b_conv_branch", "w_att_branch", "w_out", "norm_mix_post", "norm_ffn_pre", "w_ffn_up", "w_ffn_down",
           "norm_ffn_post"]
COL_SHARDED = ["w_conv_branch", "w_att_branch"]
ROW_SHARDED = ["w_out", "w_ffn_down"]
TRANSPOSED = ["w_in", "w_ffn_up"]
VECTORS = ["norm_mix_pre", "conv_dw_b", "conv_ln_g", "conv_ln_b", "b_conv_branch", "norm_mix_post",
           "norm_ffn_pre", "norm_ffn_post"]


def _cols_to_full(g):
    return g.transpose(1, 0, 2).reshape(g.shape[1], N_DEV * g.shape[2])


def _full_to_cols(f):
    return f.reshape(f.shape[0], N_DEV, f.shape[1] // N_DEV).transpose(1, 0, 2)


def _pack_vectors(vecs):
    rows = [jnp.pad(vecs[nm].reshape(-1), (0, D_MODEL - vecs[nm].size)) for nm in VECTORS]
    return jnp.stack(rows)


def _unpack_vectors(packed, sizes):
    return {nm: packed[n, :sizes[nm]] for n, nm in enumerate(VECTORS)}


def kernel(x, norm_mix_pre, w_in, conv_dw_w, conv_dw_b, conv_ln_g, conv_ln_b, w_conv_branch, b_conv_branch, w_att_branch, w_out, norm_mix_post, norm_ffn_pre, w_ffn_up, w_ffn_down, norm_ffn_post, loss_target, m_norm_mix_pre, m_w_in, m_conv_dw_w, m_conv_dw_b, m_conv_ln_g, m_conv_ln_b, m_w_conv_branch, m_b_conv_branch, m_w_att_branch, m_w_out, m_norm_mix_post, m_norm_ffn_pre, m_w_ffn_up, m_w_ffn_down, m_norm_ffn_post, v_norm_mix_pre, v_w_in, v_conv_dw_w, v_conv_dw_b, v_conv_ln_g, v_conv_ln_b, v_w_conv_branch, v_b_conv_branch, v_w_att_branch, v_w_out, v_norm_mix_post, v_norm_ffn_pre, v_w_ffn_up, v_w_ffn_down, v_norm_ffn_post):
    ws = dict(zip(WEIGHTS, [norm_mix_pre, w_in, conv_dw_w, conv_dw_b, conv_ln_g, conv_ln_b, w_conv_branch,
                            b_conv_branch, w_att_branch, w_out, norm_mix_post, norm_ffn_pre, w_ffn_up, w_ffn_down,
                            norm_ffn_post]))
    ms = dict(zip(WEIGHTS, [m_norm_mix_pre, m_w_in, m_conv_dw_w, m_conv_dw_b, m_conv_ln_g, m_conv_ln_b,
                            m_w_conv_branch, m_b_conv_branch, m_w_att_branch, m_w_out, m_norm_mix_post,
                            m_norm_ffn_pre, m_w_ffn_up, m_w_ffn_down, m_norm_ffn_post]))
    vs = dict(zip(WEIGHTS, [v_norm_mix_pre, v_w_in, v_conv_dw_w, v_conv_dw_b, v_conv_ln_g, v_conv_ln_b,
                            v_w_conv_branch, v_b_conv_branch, v_w_att_branch, v_w_out, v_norm_mix_post,
                            v_norm_ffn_pre, v_w_ffn_up, v_w_ffn_down, v_norm_ffn_post]))

    dw_block = jnp.pad(conv_dw_w, ((0, 1), (0, 0)))
    g_in, g_dw = _exchange_call("gather_first", _gather_exchange([w_in.T.astype(BF16), dw_block]))
    full = {"w_in": _full_weight("w_in", g_in), "conv_dw_w": _cols_to_full(g_dw)}
    for nm in VECTORS:
        full[nm] = ws[nm].reshape(1, -1)

    loss_local, grad_x, received, grads = _local_step(
        x[0], loss_target[0], full, {nm: (ws[nm].T if nm in TRANSPOSED else ws[nm]).astype(BF16) for nm in LATE})

    loss_at = (VECTORS.index("conv_dw_b"), CONV_DIM)
    small = _exchange_call("gather_small_grads", _gather_exchange(
        [_pack_vectors(grads).at[loss_at].set(loss_local), grads["conv_dw_w"]]))
    out_g, out_d, out_m, out_v = {}, {}, {}, {}
    for nm in LATE + ["w_in"]:
        if nm in TRANSPOSED:
            res = _sum_adamw("adamw_" + nm, received[nm], ws[nm].T, ms[nm].T, vs[nm].T)
            out_g[nm], out_d[nm], out_m[nm], out_v[nm] = [r.T for r in res]
        else:
            out_g[nm], out_d[nm], out_m[nm], out_v[nm] = _sum_adamw("adamw_" + nm, received[nm], ws[nm], ms[nm], vs[nm])
    sizes = {nm: ws[nm].size for nm in VECTORS}
    vec = _sum_adamw("adamw_vectors", small[0], _pack_vectors(ws), _pack_vectors(ms), _pack_vectors(vs))
    for res, dst in zip(vec, (out_g, out_d, out_m, out_v)):
        dst.update(_unpack_vectors(res, sizes))
    loss = vec[0][loss_at]
    dw_full = _sum_parts("sum_dw_grads", small[1])
    me = _slot(*_place())
    dw_mine = lax.dynamic_slice(dw_full, (0, me * (CONV_DIM // N_DEV)), (CONV_WIDTH, CONV_DIM // N_DEV))
    nm = "conv_dw_w"
    out_g[nm], out_d[nm], out_m[nm], out_v[nm] = _sum_adamw("adamw_dw", dw_mine[None], ws[nm], ms[nm], vs[nm])

    outs = [loss, grad_x[None]]
    for group in (out_g, out_d, out_m, out_v):
        outs += [group[nm] for nm in WEIGHTS]
    return tuple(outs)
```

```python
import math

import jax
import jax.numpy as jnp
from jax import lax
from jax.experimental import pallas as pl
from jax.experimental.pallas import tpu as pltpu

F32 = jnp.float32
BF16 = jnp.bfloat16

N_DEV = 8
D_MODEL = 1024
CONV_DIM = 512
CONV_WIDTH = 31
N_HEADS = 8
HEAD_DIM = 64
ATT_DIM = N_HEADS * HEAD_DIM
D_FF = 2816
EPS = 1e-6
IN_SPLITS = (0, 1024, 1536, 2048, 2560, 3584, 4608)

ADAM_LR = 0.001
ADAM_B1 = 0.9
ADAM_B2 = 0.999
ADAM_EPS = 1e-08
ADAM_WD = 0.01
ADAM_STEP = 10

LANES = 128
SUBLANES = 8
HALO = 32
ATT_TILE = 256
ATT_PART = 192
DEAD_SUM = -120.0
VMEM_LIMIT = 56 * 1024 * 1024
MESH = pl.DeviceIdType.MESH
ANY = pl.BlockSpec(memory_space=pl.ANY)


def _pick(dim, target, align=LANES):
    t = min(dim, target)
    t -= t % align
    while t >= align:
        if dim % t == 0:
            return t
        t -= align
    return dim


def _params(semantics):
    return pltpu.CompilerParams(dimension_semantics=semantics, vmem_limit_bytes=VMEM_LIMIT)


def _matmul(a, b, *, name, ta=False, tb=False, out_dtype=F32):
    m, k = (a.shape[1], a.shape[0]) if ta else a.shape
    n, k2 = b.shape if tb else (b.shape[1], b.shape[0])
    assert k == k2, (a.shape, b.shape, ta, tb)
    tm, tn, tk = _pick(m, 1408 if ta else 512), _pick(n, 1536), _pick(k, 1536)
    nk = k // tk
    dims = (((0 if ta else 1,), (1 if tb else 0,)), ((), ()))

    def body(a_ref, b_ref, o_ref, *acc):
        part = lax.dot_general(a_ref[...], b_ref[...], dims, preferred_element_type=F32)
        if nk == 1:
            o_ref[...] = part.astype(o_ref.dtype)
            return
        acc_ref, = acc
        kk = pl.program_id(2)

        @pl.when(kk == 0)
        def _():
            acc_ref[...] = part

        @pl.when((kk > 0) & (kk < nk - 1))
        def _():
            acc_ref[...] += part

        @pl.when(kk == nk - 1)
        def _():
            o_ref[...] = (acc_ref[...] + part).astype(o_ref.dtype)

    a_spec = pl.BlockSpec((tk, tm), lambda j, i, kk: (kk, i)) if ta else pl.BlockSpec((tm, tk), lambda j, i, kk: (i, kk))
    b_spec = (pl.BlockSpec((tn, tk), lambda j, i, kk: (j, kk)) if tb
              else pl.BlockSpec((tk, tn), lambda j, i, kk: (kk, j)))
    return pl.pallas_call(
        body, name=name, grid=(n // tn, m // tm, nk),
        in_specs=[a_spec, b_spec],
        out_specs=pl.BlockSpec((tm, tn), lambda j, i, kk: (i, j)),
        out_shape=jax.ShapeDtypeStruct((m, n), out_dtype),
        scratch_shapes=[pltpu.VMEM((tm, tn), F32)] if nk > 1 else [],
        compiler_params=_params(("parallel", "parallel", "arbitrary")),
    )(a, b)


NO_EXCHANGE = ([], [], [], None)


def _sweep_marks(nt):
    i = pl.program_id(0)
    return i == 0, i == (3 * nt) // 4, i == nt - 1


def _rowwise(name, fn, rows, bcasts, row_outs, red_outs=(), tm=256, exchange=NO_EXCHANGE):
    s = rows[0].shape[0]
    tm = _pick(s, tm, 16)
    nt = s // tm
    resident = pl.Buffered(1)
    nr, nb, no, nd = len(rows), len(bcasts), len(row_outs), len(red_outs)
    x_arrs, x_shape, x_scratch, _ = exchange
    nx = len(x_arrs)
    first_out = nr + nb + nx

    def body(*refs):
        finish_exchange = _carry_exchange(exchange, refs, nr + nb, no + nd, *_sweep_marks(nt))
        ins = [r[...] for r in refs[:nr + nb]]
        outs, reds = fn(*ins)
        for ref, val in zip(refs[first_out:first_out + no], outs):
            ref[...] = val.astype(ref.dtype)
        i = pl.program_id(0)
        for ref, val in zip(refs[first_out + no:first_out + no + nd], reds):
            @pl.when(i == 0)
            def _():
                ref[...] = val

            @pl.when(i > 0)
            def _():
                ref[...] += val
        finish_exchange()

    in_specs = [pl.BlockSpec((tm, r.shape[1]), lambda i: (i, 0)) for r in rows]
    in_specs += [pl.BlockSpec(b.shape, lambda i: (0, 0), pipeline_mode=resident) for b in bcasts]
    out_specs = [pl.BlockSpec((tm, o.shape[1]), lambda i: (i, 0)) for o in row_outs]
    out_specs += [pl.BlockSpec(d.shape, lambda i: (0, 0)) for d in red_outs]
    return pl.pallas_call(
        body, name=name, grid=(nt,), in_specs=in_specs + [ANY] * nx, out_specs=out_specs + [ANY] * nx,
        out_shape=list(row_outs) + list(red_outs) + x_shape, scratch_shapes=x_scratch,
        compiler_params=_params(("arbitrary",)),
    )(*rows, *bcasts, *x_arrs)


def _sds(shape, dtype=F32):
    return jax.ShapeDtypeStruct(shape, dtype)


def _rms(x, g):
    y = x * lax.rsqrt(jnp.mean(x * x, axis=-1, keepdims=True) + EPS)
    return y * g


def _silu(x):
    return x * jax.nn.sigmoid(x)


def _swiglu(g, u):
    return _silu(g) * u


def _ln_silu(u, g, b):
    mu = jnp.mean(u, axis=-1, keepdims=True)
    var = jnp.mean(jnp.square(u - mu), axis=-1, keepdims=True)
    return _silu((u - mu) * lax.rsqrt(var + EPS) * g + b)


def _merge(conv_pre, att_out, g_conv, g_att, b_cb):
    return jax.nn.sigmoid(g_conv) * (conv_pre + b_cb) + jax.nn.sigmoid(g_att) * att_out


def _glu(t):
    return t[:, :CONV_DIM] * jax.nn.sigmoid(t[:, CONV_DIM:])


def _shifted_reader(buf, shifted, tm):
    for b in range(1, SUBLANES):
        shifted[b - 1, :, :] = buf[pl.ds(b, tm + HALO - SUBLANES), :]

    def read(o):
        a, b = divmod(o, SUBLANES)
        return buf[pl.ds(SUBLANES * a, tm), :] if b == 0 else shifted[b - 1, pl.ds(SUBLANES * a, tm), :]

    return read


def _conv_fwd(conv_in, w_pad, b, ln_g, ln_b, exchange, tm=256):
    s = conv_in.shape[0]
    tm = _pick(s, tm, HALO)
    ratio = tm // HALO
    x_arrs, x_shape, x_scratch, _ = exchange
    nx = len(x_arrs)

    def body(*refs):
        main_ref, halo_ref, w_ref, b_ref, g_ref, be_ref = refs[:6]
        u3_ref, u1_ref = refs[6 + nx:8 + nx]
        buf, shifted = refs[-2:]
        finish_exchange = _carry_exchange(exchange, refs, 6, 2, *_sweep_marks(s // tm))
        i = pl.program_id(0)
        buf[0:HALO, :] = _glu(halo_ref[...]) * (i > 0).astype(F32)
        buf[HALO:HALO + tm, :] = _glu(main_ref[...])
        read = _shifted_reader(buf, shifted, tm)
        acc = jnp.zeros((tm, CONV_DIM), F32) + b_ref[...]
        for j in range(CONV_WIDTH):
            acc = acc + w_ref[j:j + 1, :] * read(HALO - (CONV_WIDTH - 1) + j)
        u1_ref[...] = acc
        u3_ref[...] = _ln_silu(acc, g_ref[...], be_ref[...]).astype(u3_ref.dtype)
        finish_exchange()

    res = pl.pallas_call(
        body, name="conv_fwd", grid=(s // tm,),
        in_specs=[pl.BlockSpec((tm, 2 * CONV_DIM), lambda i: (i, 0)),
                  pl.BlockSpec((HALO, 2 * CONV_DIM), lambda i: (jnp.maximum(i * ratio - 1, 0), 0)),
                  pl.BlockSpec(w_pad.shape, lambda i: (0, 0)),
                  pl.BlockSpec(b.shape, lambda i: (0, 0)),
                  pl.BlockSpec(ln_g.shape, lambda i: (0, 0)),
                  pl.BlockSpec(ln_b.shape, lambda i: (0, 0))] + [ANY] * nx,
        out_specs=[pl.BlockSpec((tm, CONV_DIM), lambda i: (i, 0)),
                   pl.BlockSpec((tm, CONV_DIM), lambda i: (i, 0))] + [ANY] * nx,
        out_shape=[_sds((s, CONV_DIM), BF16), _sds((s, CONV_DIM), F32)] + x_shape,
        scratch_shapes=x_scratch + [pltpu.VMEM((tm + HALO, CONV_DIM), F32),
                                    pltpu.VMEM((SUBLANES - 1, tm + HALO - SUBLANES, CONV_DIM), F32)],
        compiler_params=_params(("arbitrary",)),
    )(conv_in, conv_in, w_pad, b, ln_g, ln_b, *x_arrs)
    return res[0], res[1], res[2:]


def _conv_bwd(conv_in, u1, du3, ln_g, ln_b, w_pad, exchange, tm=256):
    s = conv_in.shape[0]
    tm = _pick(s, tm, HALO)
    ratio = tm // HALO
    nt = s // tm
    last_halo = s // HALO - 1
    x_arrs, x_shape, x_scratch, _ = exchange
    nx = len(x_arrs)

    def body(*refs):
        main_ref, halo_ref, u1_ref, u1n_ref, du3_ref, du3n_ref, g_ref, be_ref, w_ref = refs[:9]
        dci_ref, dw_ref, db_ref, dg_ref, dbe_ref = refs[9 + nx:14 + nx]
        ubuf, dbuf, ushift, dshift = refs[-4:]
        finish_exchange = _carry_exchange(exchange, refs, 9, 5, *_sweep_marks(nt))
        i = pl.program_id(0)
        main = main_ref[...]
        a = main[:, :CONV_DIM]
        sb = jax.nn.sigmoid(main[:, CONV_DIM:])
        ubuf[0:HALO, :] = _glu(halo_ref[...]) * (i > 0).astype(F32)
        ubuf[HALO:HALO + tm, :] = a * sb

        def ln_bwd(u1t, du3t):
            _, vjp = jax.vjp(_ln_silu, u1t, g_ref[...], be_ref[...])
            return vjp(du3t)

        du, dg, dbe = ln_bwd(u1_ref[...], du3_ref[...])
        dbuf[0:tm, :] = du
        dbuf[tm:tm + HALO, :] = ln_bwd(u1n_ref[...], du3n_ref[...])[0] * (i < nt - 1).astype(F32)

        @pl.when(i == 0)
        def _():
            dw_ref[...] = jnp.zeros_like(dw_ref)
            db_ref[...] = jnp.zeros_like(db_ref)
            dg_ref[...] = jnp.zeros_like(dg_ref)
            dbe_ref[...] = jnp.zeros_like(dbe_ref)

        dg_ref[...] += dg
        dbe_ref[...] += dbe

        read_u = _shifted_reader(ubuf, ushift, tm)
        read_d = _shifted_reader(dbuf, dshift, tm)
        du0 = jnp.zeros((tm, CONV_DIM), F32)
        for j in range(CONV_WIDTH):
            du0 = du0 + w_ref[j:j + 1, :] * read_d(CONV_WIDTH - 1 - j)
            dw_ref[j:j + 1, :] += jnp.sum(du * read_u(HALO - (CONV_WIDTH - 1) + j), axis=0, keepdims=True)
        db_ref[...] += jnp.sum(du, axis=0, keepdims=True)
        dci_ref[:, :CONV_DIM] = (du0 * sb).astype(dci_ref.dtype)
        dci_ref[:, CONV_DIM:] = (du0 * a * sb * (1.0 - sb)).astype(dci_ref.dtype)
        finish_exchange()

    res = pl.pallas_call(
        body, name="conv_bwd", grid=(nt,),
        in_specs=[pl.BlockSpec((tm, 2 * CONV_DIM), lambda i: (i, 0)),
                  pl.BlockSpec((HALO, 2 * CONV_DIM), lambda i: (jnp.maximum(i * ratio - 1, 0), 0))]
        + [pl.BlockSpec((tm, CONV_DIM), lambda i: (i, 0)),
           pl.BlockSpec((HALO, CONV_DIM), lambda i: (jnp.minimum((i + 1) * ratio, last_halo), 0))] * 2
        + [pl.BlockSpec((1, CONV_DIM), lambda i: (0, 0))] * 2 + [pl.BlockSpec(w_pad.shape, lambda i: (0, 0))]
        + [ANY] * nx,
        out_specs=[pl.BlockSpec((tm, 2 * CONV_DIM), lambda i: (i, 0)),
                   pl.BlockSpec(w_pad.shape, lambda i: (0, 0))]
        + [pl.BlockSpec((1, CONV_DIM), lambda i: (0, 0))] * 3 + [ANY] * nx,
        out_shape=[_sds((s, 2 * CONV_DIM), BF16), _sds(w_pad.shape)] + [_sds((1, CONV_DIM))] * 3 + x_shape,
        scratch_shapes=x_scratch + [pltpu.VMEM((tm + HALO, CONV_DIM), F32)] * 2
        + [pltpu.VMEM((SUBLANES - 1, tm + HALO - SUBLANES, CONV_DIM), F32)] * 2,
        compiler_params=_params(("arbitrary",)),
    )(conv_in, conv_in, u1, u1, du3, du3, ln_g, ln_b, w_pad, *x_arrs)
    return res[:5], res[5:]


def _logsig_neg(z):
    return jnp.minimum(-z, 0.0) - jnp.log(1.0 + jnp.exp(-jnp.abs(z)))


def _split_dot(val, tri):
    hi = val.astype(BF16)
    lo = (val - hi.astype(F32)).astype(BF16)
    return jnp.dot(hi, tri, preferred_element_type=F32) + jnp.dot(lo, tri, preferred_element_type=F32)


def _attn_masks(t, later):
    row = lax.broadcasted_iota(jnp.int32, (t, t), 0)
    col = lax.broadcasted_iota(jnp.int32, (t, t), 1)
    tri = jnp.where(row > col if later else row <= col, 1.0, 0.0).astype(BF16)
    return col < row, tri


def _grid_marks(h, nq):
    hh, i = pl.program_id(0), pl.program_id(1)
    return (hh == 0) & (i == 0), (hh == (3 * h) // 4) & (i == 0), (hh == h - 1) & (i == nq - 1)


def _head_masks(shape):
    lane = lax.broadcasted_iota(jnp.int32, shape, len(shape) - 1)
    return lane < HEAD_DIM, lane >= HEAD_DIM


def _per_head(blk):
    m0, m1 = _head_masks(blk.shape)
    zero = jnp.zeros_like(blk)
    return jnp.where(m0, blk, zero), jnp.where(m1, blk, zero)


NT = (((1,), (1,)), ((), ()))
TN = (((0,), (0,)), ((), ()))


def _with_top(whole, top):
    rows = top.shape[0]
    return top if rows == whole.shape[0] else jnp.concatenate([top, whole[rows:]], axis=0)


def _attn_fwd(q, k, v, exchange):
    s = q.shape[0]
    hp = q.shape[1] // LANES
    t = ATT_TILE
    scale = 1.0 / math.sqrt(HEAD_DIM)
    x_arrs, x_shape, x_scratch, _ = exchange
    nx = len(x_arrs)

    def body(*refs):
        q_ref, k_ref, v_ref = refs[:3]
        o_ref, lt_ref, nb_ref = refs[3 + nx:6 + nx]
        finish_exchange = _carry_exchange(exchange, refs, 3, 3, *_grid_marks(hp, s // t))
        i = pl.program_id(1)
        qs = _per_head((q_ref[...].astype(F32) * scale).astype(BF16))
        causal, tri = _attn_masks(t, later=True)

        def step(kb, carry, masked, rows):
            cs, acc = carry
            off = pl.multiple_of(kb * t, t)
            kblk = k_ref[pl.ds(off, t), :]
            vs = _per_head(v_ref[pl.ds(off, t), :])
            acc_top = acc[:rows]
            new_cs = []
            for hd in range(2):
                z = lax.dot_general(qs[hd][:rows], kblk, NT, preferred_element_type=F32)
                l = _logsig_neg(z)
                if masked:
                    l = jnp.where(causal, l, 0.0)
                e = z + l + _split_dot(l, tri) + cs[hd][:rows]
                if masked:
                    e = jnp.where(causal, e, -1e30)
                acc_top = acc_top + jnp.dot(jnp.exp(e).astype(BF16), vs[hd], preferred_element_type=F32)
                new_cs.append(_with_top(cs[hd], cs[hd][:rows] + jnp.sum(l, axis=1, keepdims=True)))
            return tuple(new_cs), _with_top(acc, acc_top)

        zero = jnp.zeros((t, 1), F32)
        carry = step(i, ((zero, zero), jnp.zeros((t, LANES), F32)), True, t)

        def live(cs, lo, hi):
            return jnp.maximum(jnp.max(cs[0][lo:hi]), jnp.max(cs[1][lo:hi])) > DEAD_SUM

        carry = lax.fori_loop(0, i, lambda n, cr: step(i - 1 - n, cr, False, t), carry)
        n_blocks = n_full = i
        m0, _ = _head_masks((t, LANES))
        lt_ref[...] = jnp.where(m0, carry[0][0], carry[0][1])
        o_ref[...] = carry[1].astype(o_ref.dtype)
        nb_ref[0, pl.program_id(0), i] = n_blocks.astype(F32)
        nb_ref[1, pl.program_id(0), i] = n_full.astype(F32)
        finish_exchange()

    res = pl.pallas_call(
        body, name="attn_fwd", grid=(hp, s // t),
        in_specs=[pl.BlockSpec((t, LANES), lambda p, i: (i, p)),
                  pl.BlockSpec((s, LANES), lambda p, i: (0, p)),
                  pl.BlockSpec((s, LANES), lambda p, i: (0, p))] + [ANY] * nx,
        out_specs=[pl.BlockSpec((t, LANES), lambda p, i: (i, p)),
                   pl.BlockSpec((None, t, LANES), lambda p, i: (p, i, 0)),
                   pl.BlockSpec(memory_space=pltpu.SMEM)] + [ANY] * nx,
        out_shape=[_sds(q.shape, BF16), _sds((hp, s, LANES), F32), _sds((2, hp, s // t), F32)] + x_shape,
        scratch_shapes=x_scratch,
        compiler_params=_params(("arbitrary", "arbitrary")),
    )(q, k, v, *x_arrs)
    return res[0], res[1], res[2], res[3:]


def _attn_bwd(q, k, v, do, ltot, n_blocks, exchange):
    s = q.shape[0]
    hp = q.shape[1] // LANES
    t = ATT_TILE
    scale = 1.0 / math.sqrt(HEAD_DIM)
    x_arrs, x_shape, x_scratch, _ = exchange
    nx = len(x_arrs)

    def body(*refs):
        q_ref, k_ref, v_ref, do_ref, lt_ref, nb_ref = refs[:6]
        dq_ref, dk_ref, dv_ref = refs[6 + nx:9 + nx]
        finish_exchange = _carry_exchange(exchange, refs, 6, 3, *_grid_marks(hp, s // t))
        i = pl.program_id(1)
        n_blocks = n_full = i

        @pl.when(i == 0)
        def _():
            dk_ref[...] = jnp.zeros_like(dk_ref)
            dv_ref[...] = jnp.zeros_like(dv_ref)

        qb = q_ref[...]
        qm = _per_head(qb)
        qs = _per_head((qb.astype(F32) * scale).astype(BF16))
        dos = _per_head(do_ref[...])
        lts = (lt_ref[:, 0:1], lt_ref[:, HEAD_DIM:HEAD_DIM + 1])
        causal, tri = _attn_masks(t, later=False)

        def step(kb, carry, masked, rows):
            cls, cgs, dq = carry
            off = pl.multiple_of(kb * t, t)
            kblk = k_ref[pl.ds(off, t), :]
            vblk = v_ref[pl.ds(off, t), :]
            ks = _per_head(kblk)
            dq_top = dq[:rows]
            dk = jnp.zeros((t, LANES), F32)
            dv = jnp.zeros((t, LANES), F32)
            new_cls, new_cgs = [], []
            for hd in range(2):
                z = lax.dot_general(qs[hd][:rows], kblk, NT, preferred_element_type=F32)
                l = _logsig_neg(z)
                if masked:
                    l = jnp.where(causal, l, 0.0)
                e = z + l + ((lts[hd][:rows] - cls[hd][:rows]) - _split_dot(l, tri))
                if masked:
                    e = jnp.where(causal, e, -1e30)
                a = jnp.exp(e)
                g = lax.dot_general(dos[hd][:rows], vblk, NT, preferred_element_type=F32) * a
                p = cgs[hd][:rows] + jnp.dot(g.astype(BF16), tri, preferred_element_type=F32) - g
                el = jnp.exp(l)
                dz = g * el - p * (1.0 - el)
                if masked:
                    dz = jnp.where(causal, dz, 0.0)
                dzb = (dz * scale).astype(BF16)
                dq_top = dq_top + jnp.dot(dzb, ks[hd], preferred_element_type=F32)
                dk = dk + lax.dot_general(dzb, qm[hd][:rows], TN, preferred_element_type=F32)
                dv = dv + lax.dot_general(a.astype(BF16), dos[hd][:rows], TN, preferred_element_type=F32)
                new_cls.append(_with_top(cls[hd], cls[hd][:rows] + jnp.sum(l, axis=1, keepdims=True)))
                new_cgs.append(_with_top(cgs[hd], cgs[hd][:rows] + jnp.sum(g, axis=1, keepdims=True)))
            dk_ref[pl.ds(off, t), :] += dk
            dv_ref[pl.ds(off, t), :] += dv
            return tuple(new_cls), tuple(new_cgs), _with_top(dq, dq_top)

        zero = jnp.zeros((t, 1), F32)
        init = ((zero, zero), (zero, zero), jnp.zeros((t, LANES), F32))
        carry = lax.fori_loop(i - n_blocks, i - n_full, lambda kb, cr: step(kb, cr, False, ATT_PART), init)
        carry = lax.fori_loop(i - n_full, i, lambda kb, cr: step(kb, cr, False, t), carry)
        carry = step(i, carry, True, t)
        dq_ref[...] = carry[2]
        finish_exchange()

    blk = pl.BlockSpec((t, LANES), lambda p, i: (i, p))
    whole = pl.BlockSpec((s, LANES), lambda p, i: (0, p))
    res = pl.pallas_call(
        body, name="attn_bwd", grid=(hp, s // t),
        in_specs=[blk, whole, whole, blk, pl.BlockSpec((None, t, LANES), lambda p, i: (p, i, 0)),
                  pl.BlockSpec(memory_space=pltpu.SMEM)] + [ANY] * nx,
        out_specs=[blk, whole, whole] + [ANY] * nx,
        out_shape=[_sds(q.shape)] * 3 + x_shape,
        scratch_shapes=x_scratch,
        compiler_params=_params(("arbitrary", "arbitrary")),
    )(q, k, v, do, ltot, n_blocks, *x_arrs)
    return res[0], res[1], res[2], res[3:]


LATE = ["w_conv_branch", "w_att_branch", "w_out", "w_ffn_up", "w_ffn_down"]


def _full_weight(name, gathered):
    return _cols_to_full(gathered) if name in COL_SHARDED else gathered.reshape(-1, gathered.shape[2])


def _grad_slabs(name, grad):
    return _full_to_cols(grad) if name in COL_SHARDED else grad.reshape(N_DEV, -1, grad.shape[1])


def _side_slabs(name, grad):
    slabs = _grad_slabs(name, grad)
    return slabs.reshape((4, 2) + slabs.shape[1:])


def _local_step(x, target, w, late_blocks):
    s = x.shape[0]
    w = dict(w)
    g1, g2, g3, g4 = w["norm_mix_pre"], w["norm_mix_post"], w["norm_ffn_pre"], w["norm_ffn_post"]

    w_in = w["w_in"]

    def proj_fn(xt, g1_, w_in_t):
        h = _rms(xt, g1_).astype(BF16)
        proj = lax.dot_general(h, w_in_t, NT, preferred_element_type=F32)
        return (h, *[proj[:, IN_SPLITS[n]:IN_SPLITS[n + 1]] for n in range(6)]), ()

    mix_weights = ["w_conv_branch", "w_att_branch", "w_out"]
    h1, conv_in, q, k, v, g_conv, g_att, g_down = _rowwise(
        "norm_proj", proj_fn, [x], [g1, w_in],
        [_sds((s, D_MODEL), BF16), _sds((s, 2 * CONV_DIM)), _sds((s, ATT_DIM), BF16), _sds((s, ATT_DIM), BF16),
         _sds((s, ATT_DIM), BF16), _sds((s, D_MODEL)), _sds((s, D_MODEL))], tm=512,
        exchange=_gather_exchange([late_blocks["w_ffn_down"]]))
    w["w_ffn_down"] = _full_weight("w_ffn_down", g_down)

    u3, u1, gathered = _conv_fwd(conv_in, w["conv_dw_w"], w["conv_dw_b"], w["conv_ln_g"], w["conv_ln_b"],
                                 _gather_exchange([late_blocks[nm] for nm in mix_weights]))
    for nm, g in zip(mix_weights, gathered):
        w[nm] = _full_weight(nm, g)
    att, ltot, n_blocks, (g_up,) = _attn_fwd(q, k, v, _gather_exchange([late_blocks["w_ffn_up"]]))
    w["w_ffn_up"] = _full_weight("w_ffn_up", g_up)

    def merge_fn(u3t, at, gc, ga, xt, w_cb, w_ab, b_cb, w_out, g2_, g3_):
        cp = jnp.dot(u3t, w_cb, preferred_element_type=F32)
        ao = jnp.dot(at, w_ab, preferred_element_type=F32)
        mg = _merge(cp, ao, gc, ga, b_cb).astype(BF16)
        mix_ = jnp.dot(mg, w_out, preferred_element_type=F32)
        x2_ = xt + _rms(mix_, g2_)
        return (mg, cp, ao, mix_, x2_, _rms(x2_, g3_)), ()

    merged, conv_pre, att_out, mix, x2, h2 = _rowwise(
        "branch_merge_mix", merge_fn, [u3, att, g_conv, g_att, x],
        [w["w_conv_branch"], w["w_att_branch"], w["b_conv_branch"], w["w_out"], g2, g3],
        [_sds((s, D_MODEL), BF16)] * 3 + [_sds((s, D_MODEL)), _sds((s, D_MODEL)), _sds((s, D_MODEL), BF16)], tm=512)

    def ffn_up_fn(ht, w_up_t):
        gu_ = lax.dot_general(ht, w_up_t, NT, preferred_element_type=F32)
        return (gu_, _swiglu(gu_[:, :D_FF], gu_[:, D_FF:])), ()

    gu, act = _rowwise("ffn_up", ffn_up_fn, [h2], [w["w_ffn_up"]],
                       [_sds((s, 2 * D_FF), BF16), _sds((s, D_FF), BF16)], tm=512)

    def final_fn(at, x2t, tgt, w_down, g4_):
        ff = jnp.dot(at, w_down, preferred_element_type=F32)
        n4, vjp = jax.vjp(_rms, ff, g4_)
        err = x2t + n4 - tgt
        dy = err * (1.0 / D_MODEL)
        dff, dg4 = vjp(dy)
        return (dy, dff), (jnp.sum(err * err, axis=0, keepdims=True), dg4)

    dy, dff, loss_cols, d_g4 = _rowwise("ffn_down_loss", final_fn, [act, x2, target], [w["w_ffn_down"], g4],
                                        [_sds((s, D_MODEL)), _sds((s, D_MODEL), BF16)],
                                        [_sds((1, D_MODEL)), _sds((1, D_MODEL))], tm=512)
    loss = 0.5 * jnp.sum(loss_cols) / D_MODEL

    d_w_down = _matmul(act, dff, ta=True, name="d_w_down", out_dtype=BF16)

    def act_bwd_fn(dfft, gut, w_down):
        d_act = lax.dot_general(dfft, w_down, NT, preferred_element_type=F32)
        gu_ = gut.astype(F32)
        _, vjp = jax.vjp(_swiglu, gu_[:, :D_FF], gu_[:, D_FF:])
        return (jnp.concatenate(vjp(d_act), axis=1),), ()

    down_slabs = _side_slabs("w_ffn_down", d_w_down)
    dgu, theirs = _rowwise("ffn_act_bwd", act_bwd_fn, [dff, gu], [w["w_ffn_down"]], [_sds((s, 2 * D_FF), BF16)],
                           exchange=_pair_exchange([down_slabs]))
    down_sums = _pair_sum("pair_sum_w_ffn_down", down_slabs, theirs)
    d_w_up = _matmul(dgu, h2, ta=True, name="d_w_up", out_dtype=BF16)
    received = {}
    up_slabs = _side_slabs("w_ffn_up", d_w_up)

    def mid_bwd_fn(dgut, xt, mt, dyt, w_up_t, g2_, g3_):
        dh = jnp.dot(dgut, w_up_t, preferred_element_type=F32)
        n2, vjp2 = jax.vjp(_rms, mt, g2_)
        x2_ = xt + n2
        _, vjp3 = jax.vjp(_rms, x2_, g3_)
        dx2_, dg3 = vjp3(dh)
        dx2_ = dx2_ + dyt
        dmix_, dg2 = vjp2(dx2_)
        return (dx2_, dmix_), (dg2, dg3)

    dx2, dmix, d_g2, d_g3, received["w_ffn_down"] = _rowwise(
        "ffn_up_mid_bwd", mid_bwd_fn, [dgu, x, mix, dy], [w["w_ffn_up"], g2, g3],
        [_sds((s, D_MODEL)), _sds((s, D_MODEL), BF16)], [_sds((1, D_MODEL)), _sds((1, D_MODEL))], tm=512,
        exchange=_chip_exchange([down_sums]))
    d_w_out = _matmul(merged, dmix, ta=True, name="d_w_out", out_dtype=BF16)

    def merge_bwd_fn(dmt, cp, ao, gc, ga, w_out, w_cb, w_ab, b_cb):
        dm = lax.dot_general(dmt, w_out, NT, preferred_element_type=F32)
        _, vjp = jax.vjp(_merge, cp.astype(F32), ao.astype(F32), gc, ga, b_cb)
        dcp, dao, dgc, dga, dbias = vjp(dm)
        dcp, dao = dcp.astype(BF16), dao.astype(BF16)
        du3_ = lax.dot_general(dcp, w_cb, NT, preferred_element_type=F32)
        datt_ = lax.dot_general(dao, w_ab, NT, preferred_element_type=F32)
        return (dcp, dao, dgc, dga, du3_, datt_), (dbias,)

    d_conv_out, d_att_out, d_g_conv, d_g_att, du3, d_att, d_b_cb, theirs = _rowwise(
        "merge_bwd", merge_bwd_fn, [dmix, conv_pre, att_out, g_conv, g_att],
        [w["w_out"], w["w_conv_branch"], w["w_att_branch"], w["b_conv_branch"]],
        [_sds((s, D_MODEL), BF16)] * 4 + [_sds((s, CONV_DIM)), _sds((s, ATT_DIM), BF16)], [_sds((1, D_MODEL))], tm=512,
        exchange=_pair_exchange([up_slabs]))

    d_w_cb = _matmul(u3, d_conv_out, ta=True, name="d_w_conv_branch", out_dtype=BF16)
    d_w_ab = _matmul(att, d_att_out, ta=True, name="d_w_att_branch", out_dtype=BF16)

    dq, dk, dv, (received["w_ffn_up"],) = _attn_bwd(
        q, k, v, d_att, ltot, n_blocks, _chip_exchange([_pair_sum("pair_sum_w_ffn_up", up_slabs, theirs)]))

    mix_grads = {"w_conv_branch": d_w_cb, "w_att_branch": d_w_ab, "w_out": d_w_out}
    (d_conv_in, d_dw_w, d_dw_b, d_ln_g, d_ln_b), landed = _conv_bwd(
        conv_in, u1, du3, w["conv_ln_g"], w["conv_ln_b"], w["conv_dw_w"],
        _scatter_exchange([_grad_slabs(nm, mix_grads[nm]) for nm in mix_weights]))
    received.update(zip(mix_weights, landed))

    d_proj = jnp.concatenate([d_conv_in, dq.astype(BF16), dk.astype(BF16), dv.astype(BF16), d_g_conv, d_g_att],
                             axis=1)
    d_w_in = _matmul(d_proj, h1, ta=True, name="d_w_in", out_dtype=BF16)
    in_slabs = _side_slabs("w_in", d_w_in)
    (theirs,) = _exchange_call("pair_swap_w_in", _pair_exchange([in_slabs]))

    def pre_bwd_fn(dpt, xt, dx2t, w_in_t, g_):
        dh = jnp.dot(dpt, w_in_t, preferred_element_type=F32)
        _, vjp = jax.vjp(_rms, xt, g_)
        dx_, dg_ = vjp(dh)
        return (dx_ + dx2t,), (dg_,)

    grad_x, d_g1, received["w_in"] = _rowwise(
        "proj_norm_bwd", pre_bwd_fn, [d_proj, x, dx2], [w_in, g1], [_sds((s, D_MODEL))], [_sds((1, D_MODEL))], tm=512,
        exchange=_chip_exchange([_pair_sum("pair_sum_w_in", in_slabs, theirs)]))

    grads = {
        "norm_mix_pre": d_g1, "conv_dw_w": d_dw_w, "conv_dw_b": d_dw_b,
        "conv_ln_g": d_ln_g, "conv_ln_b": d_ln_b, "b_conv_branch": d_b_cb,
        "norm_mix_post": d_g2, "norm_ffn_pre": d_g3, "norm_ffn_post": d_g4,
    }
    return loss, grad_x, received, grads


def _place():
    x, y, c = lax.axis_index("x"), lax.axis_index("y"), lax.axis_index("c")
    return x, y, c


def _slot(px, py, pc):
    return 4 * px + 2 * py + pc


def _exchange_scratch(n):
    return [pltpu.SemaphoreType.DMA((7 * n,)), pltpu.SemaphoreType.DMA((7 * n,)), pltpu.SemaphoreType.DMA((n,))]


def _gather_exchange(arrs):
    n = len(arrs)

    def phases(ins, outs, send_sems, recv_sems, local_sems):
        x, y, c = _place()
        me, sibling = (x, y, c), (x, y, 1 - c)
        chips = [(1 - x, y), (x, 1 - y), (1 - x, 1 - y)]

        def copy(a, kk, block, to, src=None):
            dst = outs[a].at[_slot(*block)]
            return pltpu.make_async_remote_copy(
                src_ref=dst if src is None else src, dst_ref=dst,
                send_sem=send_sems.at[a * 7 + kk], recv_sem=recv_sems.at[a * 7 + kk],
                device_id=to, device_id_type=MESH)

        mine = [pltpu.make_async_copy(ins[a], outs[a].at[_slot(*me)], local_sems.at[a]) for a in range(n)]
        first = []
        for a in range(n):
            first.append(copy(a, 0, me, sibling, src=ins[a]))
            first += [copy(a, 1 + j, me, (*chip, c), src=ins[a]) for j, chip in enumerate(chips)]
        passed = [copy(a, 4 + j, (*chip, c), sibling) for j, chip in enumerate(chips) for a in range(n)]

        def send():
            for cp in mine + first:
                cp.start()

        def pass_on():
            for j, chip in enumerate(chips):
                for a in range(n):
                    copy(a, 1 + j, (*chip, c), me).wait_recv()
                    passed[j * n + a].start()

        def finish():
            for a in range(n):
                copy(a, 0, sibling, me).wait_recv()
                for j, chip in enumerate(chips):
                    copy(a, 4 + j, (*chip, 1 - c), me).wait_recv()
            for cp in first + passed:
                cp.wait_send()
            for cp in mine:
                cp.wait()

        return [send, pass_on, finish]

    return list(arrs), [_sds((N_DEV,) + a.shape, a.dtype) for a in arrs], _exchange_scratch(n), phases


def _scatter_exchange(arrs):
    n = len(arrs)
    flips = [(fx, fy, fc) for fx in (0, 1) for fy in (0, 1) for fc in (0, 1)][1:]

    def phases(ins, outs, send_sems, recv_sems, local_sems):
        x, y, c = _place()
        mine = _slot(x, y, c)
        local = [pltpu.make_async_copy(ins[a].at[mine], outs[a].at[mine], local_sems.at[a]) for a in range(n)]
        peers = [((1 - x) if fx else x, (1 - y) if fy else y, (1 - c) if fc else c) for fx, fy, fc in flips]

        def copy(a, kk, src_slot, dst_slot):
            return pltpu.make_async_remote_copy(
                src_ref=ins[a].at[src_slot], dst_ref=outs[a].at[dst_slot],
                send_sem=send_sems.at[a * 7 + kk], recv_sem=recv_sems.at[a * 7 + kk],
                device_id=peers[kk], device_id_type=MESH)

        sends = [copy(a, kk, _slot(*peers[kk]), mine) for a in range(n) for kk in range(7)]

        def send():
            for cp in local + sends:
                cp.start()

        def finish():
            for a in range(n):
                for kk in range(7):
                    copy(a, kk, mine, _slot(*peers[kk])).wait_recv()
            for cp in sends:
                cp.wait_send()
            for cp in local:
                cp.wait()

        return [send, finish]

    return list(arrs), [_sds(a.shape, a.dtype) for a in arrs], _exchange_scratch(n), phases


def _pair_exchange(arrs):
    n = len(arrs)

    def phases(ins, outs, send_sems, recv_sems, local_sems):
        x, y, c = _place()

        def copy(a, chip, side):
            return pltpu.make_async_remote_copy(
                src_ref=ins[a].at[chip, side], dst_ref=outs[a].at[chip],
                send_sem=send_sems.at[a * 7 + chip], recv_sem=recv_sems.at[a * 7 + chip],
                device_id=(x, y, 1 - c), device_id_type=MESH)

        sends = [copy(a, chip, 1 - c) for a in range(n) for chip in range(4)]

        def send():
            for cp in sends:
                cp.start()

        def finish():
            for a in range(n):
                for chip in range(4):
                    copy(a, chip, c).wait_recv()
            for cp in sends:
                cp.wait_send()

        return [send, finish]

    return list(arrs), [_sds((4,) + a.shape[2:], a.dtype) for a in arrs], _exchange_scratch(n), phases


def _chip_exchange(arrs):
    n = len(arrs)

    def phases(ins, outs, send_sems, recv_sems, local_sems):
        x, y, c = _place()
        mine = 2 * x + y
        chips = [(1 - x, y), (x, 1 - y), (1 - x, 1 - y)]
        local = [pltpu.make_async_copy(ins[a].at[mine], outs[a].at[mine], local_sems.at[a]) for a in range(n)]

        def copy(a, j, src_slot, dst_slot):
            return pltpu.make_async_remote_copy(
                src_ref=ins[a].at[src_slot], dst_ref=outs[a].at[dst_slot],
                send_sem=send_sems.at[a * 7 + j], recv_sem=recv_sems.at[a * 7 + j],
                device_id=(*chips[j], c), device_id_type=MESH)

        sends = [copy(a, j, 2 * chips[j][0] + chips[j][1], mine) for a in range(n) for j in range(3)]

        def send():
            for cp in local + sends:
                cp.start()

        def finish():
            for a in range(n):
                for j in range(3):
                    copy(a, j, mine, 2 * chips[j][0] + chips[j][1]).wait_recv()
            for cp in sends:
                cp.wait_send()
            for cp in local:
                cp.wait()

        return [send, finish]

    return list(arrs), [_sds(a.shape, a.dtype) for a in arrs], _exchange_scratch(n), phases


def _pair_sum(name, mine, theirs):
    _, _, r, c = mine.shape

    def body(side_ref, m_ref, t_ref, o_ref):
        o_ref[...] = (m_ref[...].astype(F32) + t_ref[...].astype(F32)).astype(o_ref.dtype)

    return pl.pallas_call(
        body, name=name,
        grid_spec=pltpu.PrefetchScalarGridSpec(
            num_scalar_prefetch=1, grid=(4,),
            in_specs=[pl.BlockSpec((None, None, r, c), lambda j, side: (j, side[0], 0, 0)),
                      pl.BlockSpec((None, r, c), lambda j, side: (j, 0, 0))],
            out_specs=pl.BlockSpec((None, r, c), lambda j, side: (j, 0, 0))),
        out_shape=_sds(theirs.shape, theirs.dtype),
        compiler_params=_params(("parallel",)),
    )(lax.axis_index("c").astype(jnp.int32).reshape(1), mine, theirs)


def _exchange_call(name, exchange):
    arrs, out_shape, scratch, phases = exchange
    n = len(arrs)

    def body(*refs):
        for step in phases(refs[:n], refs[n:2 * n], *refs[2 * n:]):
            step()

    return pl.pallas_call(body, name=name, in_specs=[ANY] * n, out_specs=[ANY] * n,
                          out_shape=out_shape, scratch_shapes=scratch)(*arrs)


def _carry_exchange(exchange, refs, n_in, n_out, first, middle, last):
    arrs, _, _, phases = exchange
    n = len(arrs)
    if n == 0:
        return lambda: None
    ins = refs[n_in:n_in + n]
    outs = refs[n_in + n + n_out:n_in + 2 * n + n_out]
    sems = n_in + 2 * n + n_out
    steps = phases(ins, outs, *refs[sems:sems + 3])
    pl.when(first)(steps[0])
    if len(steps) == 3:
        pl.when(middle)(steps[1])
    return lambda: pl.when(last)(steps[-1])


def _adamw_math(w, g, m, v):
    m2 = ADAM_B1 * m + (1.0 - ADAM_B1) * g
    v2 = ADAM_B2 * v + (1.0 - ADAM_B2) * jnp.square(g)
    m_hat = m2 / (1.0 - ADAM_B1 ** ADAM_STEP)
    v_hat = v2 / (1.0 - ADAM_B2 ** ADAM_STEP)
    delta = -ADAM_LR * (m_hat / (jnp.sqrt(v_hat) + ADAM_EPS) + ADAM_WD * w)
    return delta, m2, v2


def _sum_adamw(name, parts, w, m, v, tr=256):
    p, r, c = parts.shape
    tr = _pick(r, tr, 16)

    def body(p_ref, w_ref, m_ref, v_ref, g_ref, d_ref, m2_ref, v2_ref):
        g = p_ref[0].astype(F32)
        for d in range(1, p):
            g = g + p_ref[d].astype(F32)
        delta, m2, v2 = _adamw_math(w_ref[...], g, m_ref[...], v_ref[...])
        g_ref[...] = g
        d_ref[...] = delta
        m2_ref[...] = m2
        v2_ref[...] = v2

    tile = pl.BlockSpec((tr, c), lambda i: (i, 0))
    return pl.pallas_call(
        body, name=name, grid=(r // tr,),
        in_specs=[pl.BlockSpec((p, tr, c), lambda i: (0, i, 0)), tile, tile, tile],
        out_specs=[tile] * 4, out_shape=[_sds((r, c))] * 4,
        compiler_params=_params(("parallel",)),
    )(parts, w, m, v)


def _sum_parts(name, parts):
    p, r, c = parts.shape

    def body(p_ref, o_ref):
        g = p_ref[0]
        for d in range(1, p):
            g = g + p_ref[d]
        o_ref[...] = g

    return pl.pallas_call(
        body, name=name, out_shape=_sds((r, c)),
        in_specs=[pl.BlockSpec(memory_space=pltpu.VMEM)], out_specs=pl.BlockSpec(memory_space=pltpu.VMEM),
    )(parts)


WEIGHTS = ["norm_mix_pre", "w_in", "conv_dw_w", "conv_dw_b", "conv_ln_g", "conv_ln_b", "w_conv_branch",
           "b_conv_branch", "w_att_branch", "w_out", "norm_mix_post", "norm_ffn_pre", "w_ffn_up", "w_ffn_down",
           "norm_ffn_post"]
COL_SHARDED = ["w_conv_branch", "w_att_branch"]
ROW_SHARDED = ["w_out", "w_ffn_down"]
TRANSPOSED = ["w_in", "w_ffn_up"]
VECTORS = ["norm_mix_pre", "conv_dw_b", "conv_ln_g", "conv_ln_b", "b_conv_branch", "norm_mix_post",
           "norm_ffn_pre", "norm_ffn_post"]


def _cols_to_full(g):
    return g.transpose(1, 0, 2).reshape(g.shape[1], N_DEV * g.shape[2])


def _full_to_cols(f):
    return f.reshape(f.shape[0], N_DEV, f.shape[1] // N_DEV).transpose(1, 0, 2)


def _pack_vectors(vecs):
    rows = [jnp.pad(vecs[nm].reshape(-1), (0, D_MODEL - vecs[nm].size)) for nm in VECTORS]
    return jnp.stack(rows)


def _unpack_vectors(packed, sizes):
    return {nm: packed[n, :sizes[nm]] for n, nm in enumerate(VECTORS)}


def kernel(x, norm_mix_pre, w_in, conv_dw_w, conv_dw_b, conv_ln_g, conv_ln_b, w_conv_branch, b_conv_branch, w_att_branch, w_out, norm_mix_post, norm_ffn_pre, w_ffn_up, w_ffn_down, norm_ffn_post, loss_target, m_norm_mix_pre, m_w_in, m_conv_dw_w, m_conv_dw_b, m_conv_ln_g, m_conv_ln_b, m_w_conv_branch, m_b_conv_branch, m_w_att_branch, m_w_out, m_norm_mix_post, m_norm_ffn_pre, m_w_ffn_up, m_w_ffn_down, m_norm_ffn_post, v_norm_mix_pre, v_w_in, v_conv_dw_w, v_conv_dw_b, v_conv_ln_g, v_conv_ln_b, v_w_conv_branch, v_b_conv_branch, v_w_att_branch, v_w_out, v_norm_mix_post, v_norm_ffn_pre, v_w_ffn_up, v_w_ffn_down, v_norm_ffn_post):
    ws = dict(zip(WEIGHTS, [norm_mix_pre, w_in, conv_dw_w, conv_dw_b, conv_ln_g, conv_ln_b, w_conv_branch,
                            b_conv_branch, w_att_branch, w_out, norm_mix_post, norm_ffn_pre, w_ffn_up, w_ffn_down,
                            norm_ffn_post]))
    ms = dict(zip(WEIGHTS, [m_norm_mix_pre, m_w_in, m_conv_dw_w, m_conv_dw_b, m_conv_ln_g, m_conv_ln_b,
                            m_w_conv_branch, m_b_conv_branch, m_w_att_branch, m_w_out, m_norm_mix_post,
                            m_norm_ffn_pre, m_w_ffn_up, m_w_ffn_down, m_norm_ffn_post]))
    vs = dict(zip(WEIGHTS, [v_norm_mix_pre, v_w_in, v_conv_dw_w, v_conv_dw_b, v_conv_ln_g, v_conv_ln_b,
                            v_w_conv_branch, v_b_conv_branch, v_w_att_branch, v_w_out, v_norm_mix_post,
                            v_norm_ffn_pre, v_w_ffn_up, v_w_ffn_down, v_norm_ffn_post]))

    dw_block = jnp.pad(conv_dw_w, ((0, 1), (0, 0)))
    g_in, g_dw = _exchange_call("gather_first", _gather_exchange([w_in.T.astype(BF16), dw_block]))
    full = {"w_in": _full_weight("w_in", g_in), "conv_dw_w": _cols_to_full(g_dw)}
    for nm in VECTORS:
        full[nm] = ws[nm].reshape(1, -1)

    loss_local, grad_x, received, grads = _local_step(
        x[0], loss_target[0], full, {nm: (ws[nm].T if nm in TRANSPOSED else ws[nm]).astype(BF16) for nm in LATE})

    loss_at = (VECTORS.index("conv_dw_b"), CONV_DIM)
    small = _exchange_call("gather_small_grads", _gather_exchange(
        [_pack_vectors(grads).at[loss_at].set(loss_local), grads["conv_dw_w"]]))
    out_g, out_d, out_m, out_v = {}, {}, {}, {}
    for nm in LATE + ["w_in"]:
        if nm in TRANSPOSED:
            res = _sum_adamw("adamw_" + nm, received[nm], ws[nm].T, ms[nm].T, vs[nm].T)
            out_g[nm], out_d[nm], out_m[nm], out_v[nm] = [r.T for r in res]
        else:
            out_g[nm], out_d[nm], out_m[nm], out_v[nm] = _sum_adamw("adamw_" + nm, received[nm], ws[nm], ms[nm], vs[nm])
    sizes = {nm: ws[nm].size for nm in VECTORS}
    vec = _sum_adamw("adamw_vectors", small[0], _pack_vectors(ws), _pack_vectors(ms), _pack_vectors(vs))
    for res, dst in zip(vec, (out_g, out_d, out_m, out_v)):
        dst.update(_unpack_vectors(res, sizes))
    loss = vec[0][loss_at]
    dw_full = _sum_parts("sum_dw_grads", small[1])
    me = _slot(*_place())
    dw_mine = lax.dynamic_slice(dw_full, (0, me * (CONV_DIM // N_DEV)), (CONV_WIDTH, CONV_DIM // N_DEV))
    nm = "conv_dw_w"
    out_g[nm], out_d[nm], out_m[nm], out_v[nm] = _sum_adamw("adamw_dw", dw_mine[None], ws[nm], ms[nm], vs[nm])

    outs = [loss, grad_x[None]]
    for group in (out_g, out_d, out_m, out_v):
        outs += [group[nm] for nm in WEIGHTS]
    return tuple(outs)
```

```python
import math

import jax
import jax.numpy as jnp
from jax import lax
from jax.experimental import pallas as pl
from jax.experimental.pallas import tpu as pltpu

F32 = jnp.float32
BF16 = jnp.bfloat16

N_DEV = 8
D_MODEL = 1024
CONV_DIM = 512
CONV_WIDTH = 31
N_HEADS = 8
HEAD_DIM = 64
ATT_DIM = N_HEADS * HEAD_DIM
D_FF = 2816
EPS = 1e-6
IN_SPLITS = (0, 1024, 1536, 2048, 2560, 3584, 4608)

ADAM_LR = 0.001
ADAM_B1 = 0.9
ADAM_B2 = 0.999
ADAM_EPS = 1e-08
ADAM_WD = 0.01
ADAM_STEP = 10

LANES = 128
SUBLANES = 8
HALO = 32
ATT_TILE = 256
ATT_PART = 192
DEAD_SUM = -120.0
VMEM_LIMIT = 56 * 1024 * 1024
MESH = pl.DeviceIdType.MESH
ANY = pl.BlockSpec(memory_space=pl.ANY)


def _pick(dim, target, align=LANES):
    t = min(dim, target)
    t -= t % align
    while t >= align:
        if dim % t == 0:
            return t
        t -= align
    return dim


def _params(semantics):
    return pltpu.CompilerParams(dimension_semantics=semantics, vmem_limit_bytes=VMEM_LIMIT)


def _matmul(a, b, *, name, ta=False, tb=False, out_dtype=F32):
    m, k = (a.shape[1], a.shape[0]) if ta else a.shape
    n, k2 = b.shape if tb else (b.shape[1], b.shape[0])
    assert k == k2, (a.shape, b.shape, ta, tb)
    tm, tn, tk = _pick(m, 1408 if ta else 512), _pick(n, 1536), _pick(k, 1536)
    nk = k // tk
    dims = (((0 if ta else 1,), (1 if tb else 0,)), ((), ()))

    def body(a_ref, b_ref, o_ref, *acc):
        part = lax.dot_general(a_ref[...], b_ref[...], dims, preferred_element_type=F32)
        if nk == 1:
            o_ref[...] = part.astype(o_ref.dtype)
            return
        acc_ref, = acc
        kk = pl.program_id(2)

        @pl.when(kk == 0)
        def _():
            acc_ref[...] = part

        @pl.when((kk > 0) & (kk < nk - 1))
        def _():
            acc_ref[...] += part

        @pl.when(kk == nk - 1)
        def _():
            o_ref[...] = (acc_ref[...] + part).astype(o_ref.dtype)

    a_spec = pl.BlockSpec((tk, tm), lambda j, i, kk: (kk, i)) if ta else pl.BlockSpec((tm, tk), lambda j, i, kk: (i, kk))
    b_spec = (pl.BlockSpec((tn, tk), lambda j, i, kk: (j, kk)) if tb
              else pl.BlockSpec((tk, tn), lambda j, i, kk: (kk, j)))
    return pl.pallas_call(
        body, name=name, grid=(n // tn, m // tm, nk),
        in_specs=[a_spec, b_spec],
        out_specs=pl.BlockSpec((tm, tn), lambda j, i, kk: (i, j)),
        out_shape=jax.ShapeDtypeStruct((m, n), out_dtype),
        scratch_shapes=[pltpu.VMEM((tm, tn), F32)] if nk > 1 else [],
        compiler_params=_params(("parallel", "parallel", "arbitrary")),
    )(a, b)


NO_EXCHANGE = ([], [], [], None)


def _sweep_marks(nt):
    i = pl.program_id(0)
    return i == 0, i == (3 * nt) // 4, i == nt - 1


def _rowwise(name, fn, rows, bcasts, row_outs, red_outs=(), tm=256, exchange=NO_EXCHANGE):
    s = rows[0].shape[0]
    tm = _pick(s, tm, 16)
    nt = s // tm
    resident = pl.Buffered(1)
    nr, nb, no, nd = len(rows), len(bcasts), len(row_outs), len(red_outs)
    x_arrs, x_shape, x_scratch, _ = exchange
    nx = len(x_arrs)
    first_out = nr + nb + nx

    def body(*refs):
        finish_exchange = _carry_exchange(exchange, refs, nr + nb, no + nd, *_sweep_marks(nt))
        ins = [r[...] for r in refs[:nr + nb]]
        outs, reds = fn(*ins)
        for ref, val in zip(refs[first_out:first_out + no], outs):
            ref[...] = val.astype(ref.dtype)
        i = pl.program_id(0)
        for ref, val in zip(refs[first_out + no:first_out + no + nd], reds):
            @pl.when(i == 0)
            def _():
                ref[...] = val

            @pl.when(i > 0)
            def _():
                ref[...] += val
        finish_exchange()

    in_specs = [pl.BlockSpec((tm, r.shape[1]), lambda i: (i, 0)) for r in rows]
    in_specs += [pl.BlockSpec(b.shape, lambda i: (0, 0), pipeline_mode=resident) for b in bcasts]
    out_specs = [pl.BlockSpec((tm, o.shape[1]), lambda i: (i, 0)) for o in row_outs]
    out_specs += [pl.BlockSpec(d.shape, lambda i: (0, 0)) for d in red_outs]
    return pl.pallas_call(
        body, name=name, grid=(nt,), in_specs=in_specs + [ANY] * nx, out_specs=out_specs + [ANY] * nx,
        out_shape=list(row_outs) + list(red_outs) + x_shape, scratch_shapes=x_scratch,
        compiler_params=_params(("arbitrary",)),
    )(*rows, *bcasts, *x_arrs)


def _sds(shape, dtype=F32):
    return jax.ShapeDtypeStruct(shape, dtype)


def _rms(x, g):
    y = x * lax.rsqrt(jnp.mean(x * x, axis=-1, keepdims=True) + EPS)
    return y * g


def _silu(x):
    return x * jax.nn.sigmoid(x)


def _swiglu(g, u):
    return _silu(g) * u


def _ln_silu(u, g, b):
    mu = jnp.mean(u, axis=-1, keepdims=True)
    var = jnp.mean(jnp.square(u - mu), axis=-1, keepdims=True)
    return _silu((u - mu) * lax.rsqrt(var + EPS) * g + b)


def _merge(conv_pre, att_out, g_conv, g_att, b_cb):
    return jax.nn.sigmoid(g_conv) * (conv_pre + b_cb) + jax.nn.sigmoid(g_att) * att_out


def _glu(t):
    return t[:, :CONV_DIM] * jax.nn.sigmoid(t[:, CONV_DIM:])


def _shifted_reader(buf, shifted, tm):
    for b in range(1, SUBLANES):
        shifted[b - 1, :, :] = buf[pl.ds(b, tm + HALO - SUBLANES), :]

    def read(o):
        a, b = divmod(o, SUBLANES)
        return buf[pl.ds(SUBLANES * a, tm), :] if b == 0 else shifted[b - 1, pl.ds(SUBLANES * a, tm), :]

    return read


def _conv_fwd(conv_in, w_pad, b, ln_g, ln_b, exchange, tm=256):
    s = conv_in.shape[0]
    tm = _pick(s, tm, HALO)
    ratio = tm // HALO
    x_arrs, x_shape, x_scratch, _ = exchange
    nx = len(x_arrs)

    def body(*refs):
        main_ref, halo_ref, w_ref, b_ref, g_ref, be_ref = refs[:6]
        u3_ref, u1_ref = refs[6 + nx:8 + nx]
        buf, shifted = refs[-2:]
        finish_exchange = _carry_exchange(exchange, refs, 6, 2, *_sweep_marks(s // tm))
        i = pl.program_id(0)
        buf[0:HALO, :] = _glu(halo_ref[...]) * (i > 0).astype(F32)
        buf[HALO:HALO + tm, :] = _glu(main_ref[...])
        read = _shifted_reader(buf, shifted, tm)
        acc = jnp.zeros((tm, CONV_DIM), F32) + b_ref[...]
        for j in range(CONV_WIDTH):
            acc = acc + w_ref[j:j + 1, :] * read(HALO - (CONV_WIDTH - 1) + j)
        u1_ref[...] = acc
        u3_ref[...] = _ln_silu(acc, g_ref[...], be_ref[...]).astype(u3_ref.dtype)
        finish_exchange()

    res = pl.pallas_call(
        body, name="conv_fwd", grid=(s // tm,),
        in_specs=[pl.BlockSpec((tm, 2 * CONV_DIM), lambda i: (i, 0)),
                  pl.BlockSpec((HALO, 2 * CONV_DIM), lambda i: (jnp.maximum(i * ratio - 1, 0), 0)),
                  pl.BlockSpec(w_pad.shape, lambda i: (0, 0)),
                  pl.BlockSpec(b.shape, lambda i: (0, 0)),
                  pl.BlockSpec(ln_g.shape, lambda i: (0, 0)),
                  pl.BlockSpec(ln_b.shape, lambda i: (0, 0))] + [ANY] * nx,
        out_specs=[pl.BlockSpec((tm, CONV_DIM), lambda i: (i, 0)),
                   pl.BlockSpec((tm, CONV_DIM), lambda i: (i, 0))] + [ANY] * nx,
        out_shape=[_sds((s, CONV_DIM), BF16), _sds((s, CONV_DIM), F32)] + x_shape,
        scratch_shapes=x_scratch + [pltpu.VMEM((tm + HALO, CONV_DIM), F32),
                                    pltpu.VMEM((SUBLANES - 1, tm + HALO - SUBLANES, CONV_DIM), F32)],
        compiler_params=_params(("arbitrary",)),
    )(conv_in, conv_in, w_pad, b, ln_g, ln_b, *x_arrs)
    return res[0], res[1], res[2:]


def _conv_bwd(conv_in, u1, du3, ln_g, ln_b, w_pad, exchange, tm=256):
    s = conv_in.shape[0]
    tm = _pick(s, tm, HALO)
    ratio = tm // HALO
    nt = s // tm
    last_halo = s // HALO - 1
    x_arrs, x_shape, x_scratch, _ = exchange
    nx = len(x_arrs)

    def body(*refs):
        main_ref, halo_ref, u1_ref, u1n_ref, du3_ref, du3n_ref, g_ref, be_ref, w_ref = refs[:9]
        dci_ref, dw_ref, db_ref, dg_ref, dbe_ref = refs[9 + nx:14 + nx]
        ubuf, dbuf, ushift, dshift = refs[-4:]
        finish_exchange = _carry_exchange(exchange, refs, 9, 5, *_sweep_marks(nt))
        i = pl.program_id(0)
        main = main_ref[...]
        a = main[:, :CONV_DIM]
        sb = jax.nn.sigmoid(main[:, CONV_DIM:])
        ubuf[0:HALO, :] = _glu(halo_ref[...]) * (i > 0).astype(F32)
        ubuf[HALO:HALO + tm, :] = a * sb

        def ln_bwd(u1t, du3t):
            _, vjp = jax.vjp(_ln_silu, u1t, g_ref[...], be_ref[...])
            return vjp(du3t)

        du, dg, dbe = ln_bwd(u1_ref[...], du3_ref[...])
        dbuf[0:tm, :] = du
        dbuf[tm:tm + HALO, :] = ln_bwd(u1n_ref[...], du3n_ref[...])[0] * (i < nt - 1).astype(F32)

        @pl.when(i == 0)
        def _():
            dw_ref[...] = jnp.zeros_like(dw_ref)
            db_ref[...] = jnp.zeros_like(db_ref)
            dg_ref[...] = jnp.zeros_like(dg_ref)
            dbe_ref[...] = jnp.zeros_like(dbe_ref)

        dg_ref[...] += dg
        dbe_ref[...] += dbe

        read_u = _shifted_reader(ubuf, ushift, tm)
        read_d = _shifted_reader(dbuf, dshift, tm)
        du0 = jnp.zeros((tm, CONV_DIM), F32)
        for j in range(CONV_WIDTH):
            du0 = du0 + w_ref[j:j + 1, :] * read_d(CONV_WIDTH - 1 - j)
            dw_ref[j:j + 1, :] += jnp.sum(du * read_u(HALO - (CONV_WIDTH - 1) + j), axis=0, keepdims=True)
        db_ref[...] += jnp.sum(du, axis=0, keepdims=True)
        dci_ref[:, :CONV_DIM] = (du0 * sb).astype(dci_ref.dtype)
        dci_ref[:, CONV_DIM:] = (du0 * a * sb * (1.0 - sb)).astype(dci_ref.dtype)
        finish_exchange()

    res = pl.pallas_call(
        body, name="conv_bwd", grid=(nt,),
        in_specs=[pl.BlockSpec((tm, 2 * CONV_DIM), lambda i: (i, 0)),
                  pl.BlockSpec((HALO, 2 * CONV_DIM), lambda i: (jnp.maximum(i * ratio - 1, 0), 0))]
        + [pl.BlockSpec((tm, CONV_DIM), lambda i: (i, 0)),
           pl.BlockSpec((HALO, CONV_DIM), lambda i: (jnp.minimum((i + 1) * ratio, last_halo), 0))] * 2
        + [pl.BlockSpec((1, CONV_DIM), lambda i: (0, 0))] * 2 + [pl.BlockSpec(w_pad.shape, lambda i: (0, 0))]
        + [ANY] * nx,
        out_specs=[pl.BlockSpec((tm, 2 * CONV_DIM), lambda i: (i, 0)),
                   pl.BlockSpec(w_pad.shape, lambda i: (0, 0))]
        + [pl.BlockSpec((1, CONV_DIM), lambda i: (0, 0))] * 3 + [ANY] * nx,
        out_shape=[_sds((s, 2 * CONV_DIM), BF16), _sds(w_pad.shape)] + [_sds((1, CONV_DIM))] * 3 + x_shape,
        scratch_shapes=x_scratch + [pltpu.VMEM((tm + HALO, CONV_DIM), F32)] * 2
        + [pltpu.VMEM((SUBLANES - 1, tm + HALO - SUBLANES, CONV_DIM), F32)] * 2,
        compiler_params=_params(("arbitrary",)),
    )(conv_in, conv_in, u1, u1, du3, du3, ln_g, ln_b, w_pad, *x_arrs)
    return res[:5], res[5:]


def _logsig_neg(z):
    return jnp.minimum(-z, 0.0) - jnp.log(1.0 + jnp.exp(-jnp.abs(z)))


def _split_dot(val, tri):
    hi = val.astype(BF16)
    lo = (val - hi.astype(F32)).astype(BF16)
    return jnp.dot(hi, tri, preferred_element_type=F32) + jnp.dot(lo, tri, preferred_element_type=F32)


def _attn_masks(t, later):
    row = lax.broadcasted_iota(jnp.int32, (t, t), 0)
    col = lax.broadcasted_iota(jnp.int32, (t, t), 1)
    tri = jnp.where(row > col if later else row <= col, 1.0, 0.0).astype(BF16)
    return col < row, tri


def _grid_marks(h, nq):
    hh, i = pl.program_id(0), pl.program_id(1)
    return (hh == 0) & (i == 0), (hh == (3 * h) // 4) & (i == 0), (hh == h - 1) & (i == nq - 1)


def _head_masks(shape):
    lane = lax.broadcasted_iota(jnp.int32, shape, len(shape) - 1)
    return lane < HEAD_DIM, lane >= HEAD_DIM


def _per_head(blk):
    m0, m1 = _head_masks(blk.shape)
    zero = jnp.zeros_like(blk)
    return jnp.where(m0, blk, zero), jnp.where(m1, blk, zero)


NT = (((1,), (1,)), ((), ()))
TN = (((0,), (0,)), ((), ()))


def _with_top(whole, top):
    rows = top.shape[0]
    return top if rows == whole.shape[0] else jnp.concatenate([top, whole[rows:]], axis=0)


def _attn_fwd(q, k, v, exchange):
    s = q.shape[0]
    hp = q.shape[1] // LANES
    t = ATT_TILE
    scale = 1.0 / math.sqrt(HEAD_DIM)
    x_arrs, x_shape, x_scratch, _ = exchange
    nx = len(x_arrs)

    def body(*refs):
        q_ref, k_ref, v_ref = refs[:3]
        o_ref, lt_ref, nb_ref = refs[3 + nx:6 + nx]
        finish_exchange = _carry_exchange(exchange, refs, 3, 3, *_grid_marks(hp, s // t))
        i = pl.program_id(1)
        qs = _per_head((q_ref[...].astype(F32) * scale).astype(BF16))
        causal, tri = _attn_masks(t, later=True)

        def step(kb, carry, masked, rows):
            cs, acc = carry
            off = pl.multiple_of(kb * t, t)
            kblk = k_ref[pl.ds(off, t), :]
            vs = _per_head(v_ref[pl.ds(off, t), :])
            acc_top = acc[:rows]
            new_cs = []
            for hd in range(2):
                z = lax.dot_general(qs[hd][:rows], kblk, NT, preferred_element_type=F32)
                l = _logsig_neg(z)
                if masked:
                    l = jnp.where(causal, l, 0.0)
                e = z + l + _split_dot(l, tri) + cs[hd][:rows]
                if masked:
                    e = jnp.where(causal, e, -1e30)
                acc_top = acc_top + jnp.dot(jnp.exp(e).astype(BF16), vs[hd], preferred_element_type=F32)
                new_cs.append(_with_top(cs[hd], cs[hd][:rows] + jnp.sum(l, axis=1, keepdims=True)))
            return tuple(new_cs), _with_top(acc, acc_top)

        zero = jnp.zeros((t, 1), F32)
        carry = step(i, ((zero, zero), jnp.zeros((t, LANES), F32)), True, t)

        def live(cs, lo, hi):
            return jnp.maximum(jnp.max(cs[0][lo:hi]), jnp.max(cs[1][lo:hi])) > DEAD_SUM

        def more(state):
            n, _, (cs, _) = state
            return (n < i) & live(cs, 0, t)

        def sweep(state):
            n, n_full, cr = state
            whole = live(cr[0], ATT_PART, t)
            cr = lax.cond(whole, lambda c: step(i - 1 - n, c, False, t), lambda c: step(i - 1 - n, c, False, ATT_PART), cr)
            return n + 1, n_full + whole.astype(jnp.int32), cr

        n_blocks, n_full, carry = lax.while_loop(more, sweep, (jnp.int32(0), jnp.int32(0), carry))
        m0, _ = _head_masks((t, LANES))
        lt_ref[...] = jnp.where(m0, carry[0][0], carry[0][1])
        o_ref[...] = carry[1].astype(o_ref.dtype)
        nb_ref[0, pl.program_id(0), i] = n_blocks.astype(F32)
        nb_ref[1, pl.program_id(0), i] = n_full.astype(F32)
        finish_exchange()

    res = pl.pallas_call(
        body, name="attn_fwd", grid=(hp, s // t),
        in_specs=[pl.BlockSpec((t, LANES), lambda p, i: (i, p)),
                  pl.BlockSpec((s, LANES), lambda p, i: (0, p)),
                  pl.BlockSpec((s, LANES), lambda p, i: (0, p))] + [ANY] * nx,
        out_specs=[pl.BlockSpec((t, LANES), lambda p, i: (i, p)),
                   pl.BlockSpec((None, t, LANES), lambda p, i: (p, i, 0)),
                   pl.BlockSpec(memory_space=pltpu.SMEM)] + [ANY] * nx,
        out_shape=[_sds(q.shape, BF16), _sds((hp, s, LANES), F32), _sds((2, hp, s // t), F32)] + x_shape,
        scratch_shapes=x_scratch,
        compiler_params=_params(("arbitrary", "arbitrary")),
    )(q, k, v, *x_arrs)
    return res[0], res[1], res[2], res[3:]


def _attn_bwd(q, k, v, do, ltot, n_blocks, exchange):
    s = q.shape[0]
    hp = q.shape[1] // LANES
    t = ATT_TILE
    scale = 1.0 / math.sqrt(HEAD_DIM)
    x_arrs, x_shape, x_scratch, _ = exchange
    nx = len(x_arrs)

    def body(*refs):
        q_ref, k_ref, v_ref, do_ref, lt_ref, nb_ref = refs[:6]
        dq_ref, dk_ref, dv_ref = refs[6 + nx:9 + nx]
        finish_exchange = _carry_exchange(exchange, refs, 6, 3, *_grid_marks(hp, s // t))
        i = pl.program_id(1)
        n_blocks = jnp.clip(nb_ref[0, pl.program_id(0), i].astype(jnp.int32), 0, i)
        n_full = jnp.clip(nb_ref[1, pl.program_id(0), i].astype(jnp.int32), 0, n_blocks)

        @pl.when(i == 0)
        def _():
            dk_ref[...] = jnp.zeros_like(dk_ref)
            dv_ref[...] = jnp.zeros_like(dv_ref)

        qb = q_ref[...]
        qm = _per_head(qb)
        qs = _per_head((qb.astype(F32) * scale).astype(BF16))
        dos = _per_head(do_ref[...])
        lts = (lt_ref[:, 0:1], lt_ref[:, HEAD_DIM:HEAD_DIM + 1])
        causal, tri = _attn_masks(t, later=False)

        def step(kb, carry, masked, rows):
            cls, cgs, dq = carry
            off = pl.multiple_of(kb * t, t)
            kblk = k_ref[pl.ds(off, t), :]
            vblk = v_ref[pl.ds(off, t), :]
            ks = _per_head(kblk)
            dq_top = dq[:rows]
            dk = jnp.zeros((t, LANES), F32)
            dv = jnp.zeros((t, LANES), F32)
            new_cls, new_cgs = [], []
            for hd in range(2):
                z = lax.dot_general(qs[hd][:rows], kblk, NT, preferred_element_type=F32)
                l = _logsig_neg(z)
                if masked:
                    l = jnp.where(causal, l, 0.0)
                e = z + l + ((lts[hd][:rows] - cls[hd][:rows]) - _split_dot(l, tri))
                if masked:
                    e = jnp.where(causal, e, -1e30)
                a = jnp.exp(e)
                g = lax.dot_general(dos[hd][:rows], vblk, NT, preferred_element_type=F32) * a
                p = cgs[hd][:rows] + jnp.dot(g.astype(BF16), tri, preferred_element_type=F32) - g
                el = jnp.exp(l)
                dz = g * el - p * (1.0 - el)
                if masked:
                    dz = jnp.where(causal, dz, 0.0)
                dzb = (dz * scale).astype(BF16)
                dq_top = dq_top + jnp.dot(dzb, ks[hd], preferred_element_type=F32)
                dk = dk + lax.dot_general(dzb, qm[hd][:rows], TN, preferred_element_type=F32)
                dv = dv + lax.dot_general(a.astype(BF16), dos[hd][:rows], TN, preferred_element_type=F32)
                new_cls.append(_with_top(cls[hd], cls[hd][:rows] + jnp.sum(l, axis=1, keepdims=True)))
                new_cgs.append(_with_top(cgs[hd], cgs[hd][:rows] + jnp.sum(g, axis=1, keepdims=True)))
            dk_ref[pl.ds(off, t), :] += dk
            dv_ref[pl.ds(off, t), :] += dv
            return tuple(new_cls), tuple(new_cgs), _with_top(dq, dq_top)

        zero = jnp.zeros((t, 1), F32)
        init = ((zero, zero), (zero, zero), jnp.zeros((t, LANES), F32))
        carry = lax.fori_loop(i - n_blocks, i - n_full, lambda kb, cr: step(kb, cr, False, ATT_PART), init)
        carry = lax.fori_loop(i - n_full, i, lambda kb, cr: step(kb, cr, False, t), carry)
        carry = step(i, carry, True, t)
        dq_ref[...] = carry[2]
        finish_exchange()

    blk = pl.BlockSpec((t, LANES), lambda p, i: (i, p))
    whole = pl.BlockSpec((s, LANES), lambda p, i: (0, p))
    res = pl.pallas_call(
        body, name="attn_bwd", grid=(hp, s // t),
        in_specs=[blk, whole, whole, blk, pl.BlockSpec((None, t, LANES), lambda p, i: (p, i, 0)),
                  pl.BlockSpec(memory_space=pltpu.SMEM)] + [ANY] * nx,
        out_specs=[blk, whole, whole] + [ANY] * nx,
        out_shape=[_sds(q.shape)] * 3 + x_shape,
        scratch_shapes=x_scratch,
        compiler_params=_params(("arbitrary", "arbitrary")),
    )(q, k, v, do, ltot, n_blocks, *x_arrs)
    return res[0], res[1], res[2], res[3:]


LATE = ["w_conv_branch", "w_att_branch", "w_out", "w_ffn_up", "w_ffn_down"]


def _full_weight(name, gathered):
    return _cols_to_full(gathered) if name in COL_SHARDED else gathered.reshape(-1, gathered.shape[2])


def _grad_slabs(name, grad):
    return _full_to_cols(grad) if name in COL_SHARDED else grad.reshape(N_DEV, -1, grad.shape[1])


def _side_slabs(name, grad):
    slabs = _grad_slabs(name, grad)
    return slabs.reshape((4, 2) + slabs.shape[1:])


def _local_step(x, target, w, late_blocks):
    s = x.shape[0]
    w = dict(w)
    g1, g2, g3, g4 = w["norm_mix_pre"], w["norm_mix_post"], w["norm_ffn_pre"], w["norm_ffn_post"]

    w_in = w["w_in"]

    def proj_fn(xt, g1_, w_in_t):
        h = _rms(xt, g1_).astype(BF16)
        proj = lax.dot_general(h, w_in_t, NT, preferred_element_type=F32)
        return (h, *[proj[:, IN_SPLITS[n]:IN_SPLITS[n + 1]] for n in range(6)]), ()

    mix_weights = ["w_conv_branch", "w_att_branch", "w_out"]
    h1, conv_in, q, k, v, g_conv, g_att = _rowwise(
        "norm_proj", proj_fn, [x], [g1, w_in],
        [_sds((s, D_MODEL), BF16), _sds((s, 2 * CONV_DIM)), _sds((s, ATT_DIM), BF16), _sds((s, ATT_DIM), BF16),
         _sds((s, ATT_DIM), BF16), _sds((s, D_MODEL)), _sds((s, D_MODEL))], tm=512)

    u3, u1, gathered = _conv_fwd(conv_in, w["conv_dw_w"], w["conv_dw_b"], w["conv_ln_g"], w["conv_ln_b"],
                                 _gather_exchange([late_blocks[nm] for nm in mix_weights]))
    for nm, g in zip(mix_weights, gathered):
        w[nm] = _full_weight(nm, g)
    att, ltot, n_blocks, (g_up,) = _attn_fwd(q, k, v, _gather_exchange([late_blocks["w_ffn_up"]]))
    w["w_ffn_up"] = _full_weight("w_ffn_up", g_up)

    def merge_fn(u3t, at, gc, ga, xt, w_cb, w_ab, b_cb, w_out, g2_, g3_):
        cp = jnp.dot(u3t, w_cb, preferred_element_type=F32)
        ao = jnp.dot(at, w_ab, preferred_element_type=F32)
        mg = _merge(cp, ao, gc, ga, b_cb).astype(BF16)
        mix_ = jnp.dot(mg, w_out, preferred_element_type=F32)
        x2_ = xt + _rms(mix_, g2_)
        return (mg, cp, ao, mix_, x2_, _rms(x2_, g3_)), ()

    merged, conv_pre, att_out, mix, x2, h2 = _rowwise(
        "branch_merge_mix", merge_fn, [u3, att, g_conv, g_att, x],
        [w["w_conv_branch"], w["w_att_branch"], w["b_conv_branch"], w["w_out"], g2, g3],
        [_sds((s, D_MODEL), BF16)] * 3 + [_sds((s, D_MODEL)), _sds((s, D_MODEL)), _sds((s, D_MODEL), BF16)], tm=512)

    def ffn_up_fn(ht, w_up_t):
        gu_ = lax.dot_general(ht, w_up_t, NT, preferred_element_type=F32)
        return (gu_, _swiglu(gu_[:, :D_FF], gu_[:, D_FF:])), ()

    gu, act, g_down = _rowwise("ffn_up", ffn_up_fn, [h2], [w["w_ffn_up"]],
                               [_sds((s, 2 * D_FF), BF16), _sds((s, D_FF), BF16)], tm=512,
                               exchange=_gather_exchange([late_blocks["w_ffn_down"]]))
    w["w_ffn_down"] = _full_weight("w_ffn_down", g_down)

    def final_fn(at, x2t, tgt, w_down, g4_):
        ff = jnp.dot(at, w_down, preferred_element_type=F32)
        n4, vjp = jax.vjp(_rms, ff, g4_)
        err = x2t + n4 - tgt
        dy = err * (1.0 / D_MODEL)
        dff, dg4 = vjp(dy)
        return (dy, dff), (jnp.sum(err * err, axis=0, keepdims=True), dg4)

    dy, dff, loss_cols, d_g4 = _rowwise("ffn_down_loss", final_fn, [act, x2, target], [w["w_ffn_down"], g4],
                                        [_sds((s, D_MODEL)), _sds((s, D_MODEL), BF16)],
                                        [_sds((1, D_MODEL)), _sds((1, D_MODEL))], tm=512)
    loss = 0.5 * jnp.sum(loss_cols) / D_MODEL

    d_w_down = _matmul(act, dff, ta=True, name="d_w_down", out_dtype=BF16)

    def act_bwd_fn(dfft, gut, w_down):
        d_act = lax.dot_general(dfft, w_down, NT, preferred_element_type=F32)
        gu_ = gut.astype(F32)
        _, vjp = jax.vjp(_swiglu, gu_[:, :D_FF], gu_[:, D_FF:])
        return (jnp.concatenate(vjp(d_act), axis=1),), ()

    down_slabs = _side_slabs("w_ffn_down", d_w_down)
    dgu, theirs = _rowwise("ffn_act_bwd", act_bwd_fn, [dff, gu], [w["w_ffn_down"]], [_sds((s, 2 * D_FF), BF16)],
                           exchange=_pair_exchange([down_slabs]))
    down_sums = _pair_sum("pair_sum_w_ffn_down", down_slabs, theirs)
    d_w_up = _matmul(dgu, h2, ta=True, name="d_w_up", out_dtype=BF16)
    received = {}
    up_slabs = _side_slabs("w_ffn_up", d_w_up)

    def mid_bwd_fn(dgut, xt, mt, dyt, w_up_t, g2_, g3_):
        dh = jnp.dot(dgut, w_up_t, preferred_element_type=F32)
        n2, vjp2 = jax.vjp(_rms, mt, g2_)
        x2_ = xt + n2
        _, vjp3 = jax.vjp(_rms, x2_, g3_)
        dx2_, dg3 = vjp3(dh)
        dx2_ = dx2_ + dyt
        dmix_, dg2 = vjp2(dx2_)
        return (dx2_, dmix_), (dg2, dg3)

    dx2, dmix, d_g2, d_g3, received["w_ffn_down"] = _rowwise(
        "ffn_up_mid_bwd", mid_bwd_fn, [dgu, x, mix, dy], [w["w_ffn_up"], g2, g3],
        [_sds((s, D_MODEL)), _sds((s, D_MODEL), BF16)], [_sds((1, D_MODEL)), _sds((1, D_MODEL))], tm=512,
        exchange=_chip_exchange([down_sums]))
    d_w_out = _matmul(merged, dmix, ta=True, name="d_w_out", out_dtype=BF16)

    def merge_bwd_fn(dmt, cp, ao, gc, ga, w_out, w_cb, w_ab, b_cb):
        dm = lax.dot_general(dmt, w_out, NT, preferred_element_type=F32)
        _, vjp = jax.vjp(_merge, cp.astype(F32), ao.astype(F32), gc, ga, b_cb)
        dcp, dao, dgc, dga, dbias = vjp(dm)
        dcp, dao = dcp.astype(BF16), dao.astype(BF16)
        du3_ = lax.dot_general(dcp, w_cb, NT, preferred_element_type=F32)
        datt_ = lax.dot_general(dao, w_ab, NT, preferred_element_type=F32)
        return (dcp, dao, dgc, dga, du3_, datt_), (dbias,)

    d_conv_out, d_att_out, d_g_conv, d_g_att, du3, d_att, d_b_cb, theirs = _rowwise(
        "merge_bwd", merge_bwd_fn, [dmix, conv_pre, att_out, g_conv, g_att],
        [w["w_out"], w["w_conv_branch"], w["w_att_branch"], w["b_conv_branch"]],
        [_sds((s, D_MODEL), BF16)] * 4 + [_sds((s, CONV_DIM)), _sds((s, ATT_DIM), BF16)], [_sds((1, D_MODEL))], tm=512,
        exchange=_pair_exchange([up_slabs]))

    d_w_cb = _matmul(u3, d_conv_out, ta=True, name="d_w_conv_branch", out_dtype=BF16)
    d_w_ab = _matmul(att, d_att_out, ta=True, name="d_w_att_branch", out_dtype=BF16)

    dq, dk, dv, (received["w_ffn_up"],) = _attn_bwd(
        q, k, v, d_att, ltot, n_blocks, _chip_exchange([_pair_sum("pair_sum_w_ffn_up", up_slabs, theirs)]))

    mix_grads = {"w_conv_branch": d_w_cb, "w_att_branch": d_w_ab, "w_out": d_w_out}
    (d_conv_in, d_dw_w, d_dw_b, d_ln_g, d_ln_b), landed = _conv_bwd(
        conv_in, u1, du3, w["conv_ln_g"], w["conv_ln_b"], w["conv_dw_w"],
        _scatter_exchange([_grad_slabs(nm, mix_grads[nm]) for nm in mix_weights]))
    received.update(zip(mix_weights, landed))

    d_proj = jnp.concatenate([d_conv_in, dq.astype(BF16), dk.astype(BF16), dv.astype(BF16), d_g_conv, d_g_att],
                             axis=1)
    d_w_in = _matmul(d_proj, h1, ta=True, name="d_w_in", out_dtype=BF16)
    in_slabs = _side_slabs("w_in", d_w_in)
    (theirs,) = _exchange_call("pair_swap_w_in", _pair_exchange([in_slabs]))

    def pre_bwd_fn(dpt, xt, dx2t, w_in_t, g_):
        dh = jnp.dot(dpt, w_in_t, preferred_element_type=F32)
        _, vjp = jax.vjp(_rms, xt, g_)
        dx_, dg_ = vjp(dh)
        return (dx_ + dx2t,), (dg_,)

    grad_x, d_g1, received["w_in"] = _rowwise(
        "proj_norm_bwd", pre_bwd_fn, [d_proj, x, dx2], [w_in, g1], [_sds((s, D_MODEL))], [_sds((1, D_MODEL))], tm=512,
        exchange=_chip_exchange([_pair_sum("pair_sum_w_in", in_slabs, theirs)]))

    grads = {
        "norm_mix_pre": d_g1, "conv_dw_w": d_dw_w, "conv_dw_b": d_dw_b,
        "conv_ln_g": d_ln_g, "conv_ln_b": d_ln_b, "b_conv_branch": d_b_cb,
        "norm_mix_post": d_g2, "norm_ffn_pre": d_g3, "norm_ffn_post": d_g4,
    }
    return loss, grad_x, received, grads


def _place():
    x, y, c = lax.axis_index("x"), lax.axis_index("y"), lax.axis_index("c")
    return x, y, c


def _slot(px, py, pc):
    return 4 * px + 2 * py + pc


def _exchange_scratch(n):
    return [pltpu.SemaphoreType.DMA((7 * n,)), pltpu.SemaphoreType.DMA((7 * n,)), pltpu.SemaphoreType.DMA((n,))]


def _gather_exchange(arrs):
    n = len(arrs)

    def phases(ins, outs, send_sems, recv_sems, local_sems):
        x, y, c = _place()
        me, sibling = (x, y, c), (x, y, 1 - c)
        chips = [(1 - x, y), (x, 1 - y), (1 - x, 1 - y)]

        def copy(a, kk, block, to, src=None):
            dst = outs[a].at[_slot(*block)]
            return pltpu.make_async_remote_copy(
                src_ref=dst if src is None else src, dst_ref=dst,
                send_sem=send_sems.at[a * 7 + kk], recv_sem=recv_sems.at[a * 7 + kk],
                device_id=to, device_id_type=MESH)

        mine = [pltpu.make_async_copy(ins[a], outs[a].at[_slot(*me)], local_sems.at[a]) for a in range(n)]
        first = []
        for a in range(n):
            first.append(copy(a, 0, me, sibling, src=ins[a]))
            first += [copy(a, 1 + j, me, (*chip, c), src=ins[a]) for j, chip in enumerate(chips)]
        passed = [copy(a, 4 + j, (*chip, c), sibling) for j, chip in enumerate(chips) for a in range(n)]

        def send():
            for cp in mine + first:
                cp.start()

        def pass_on():
            for j, chip in enumerate(chips):
                for a in range(n):
                    copy(a, 1 + j, (*chip, c), me).wait_recv()
                    passed[j * n + a].start()

        def finish():
            for a in range(n):
                copy(a, 0, sibling, me).wait_recv()
                for j, chip in enumerate(chips):
                    copy(a, 4 + j, (*chip, 1 - c), me).wait_recv()
            for cp in first + passed:
                cp.wait_send()
            for cp in mine:
                cp.wait()

        return [send, pass_on, finish]

    return list(arrs), [_sds((N_DEV,) + a.shape, a.dtype) for a in arrs], _exchange_scratch(n), phases


def _scatter_exchange(arrs):
    n = len(arrs)
    flips = [(fx, fy, fc) for fx in (0, 1) for fy in (0, 1) for fc in (0, 1)][1:]

    def phases(ins, outs, send_sems, recv_sems, local_sems):
        x, y, c = _place()
        mine = _slot(x, y, c)
        local = [pltpu.make_async_copy(ins[a].at[mine], outs[a].at[mine], local_sems.at[a]) for a in range(n)]
        peers = [((1 - x) if fx else x, (1 - y) if fy else y, (1 - c) if fc else c) for fx, fy, fc in flips]

        def copy(a, kk, src_slot, dst_slot):
            return pltpu.make_async_remote_copy(
                src_ref=ins[a].at[src_slot], dst_ref=outs[a].at[dst_slot],
                send_sem=send_sems.at[a * 7 + kk], recv_sem=recv_sems.at[a * 7 + kk],
                device_id=peers[kk], device_id_type=MESH)

        sends = [copy(a, kk, _slot(*peers[kk]), mine) for a in range(n) for kk in range(7)]

        def send():
            for cp in local + sends:
                cp.start()

        def finish():
            for a in range(n):
                for kk in range(7):
                    copy(a, kk, mine, _slot(*peers[kk])).wait_recv()
            for cp in sends:
                cp.wait_send()
            for cp in local:
                cp.wait()

        return [send, finish]

    return list(arrs), [_sds(a.shape, a.dtype) for a in arrs], _exchange_scratch(n), phases


def _pair_exchange(arrs):
    n = len(arrs)

    def phases(ins, outs, send_sems, recv_sems, local_sems):
        x, y, c = _place()

        def copy(a, chip, side):
            return pltpu.make_async_remote_copy(
                src_ref=ins[a].at[chip, side], dst_ref=outs[a].at[chip],
                send_sem=send_sems.at[a * 7 + chip], recv_sem=recv_sems.at[a * 7 + chip],
                device_id=(x, y, 1 - c), device_id_type=MESH)

        sends = [copy(a, chip, 1 - c) for a in range(n) for chip in range(4)]

        def send():
            for cp in sends:
                cp.start()

        def finish():
            for a in range(n):
                for chip in range(4):
                    copy(a, chip, c).wait_recv()
            for cp in sends:
                cp.wait_send()

        return [send, finish]

    return list(arrs), [_sds((4,) + a.shape[2:], a.dtype) for a in arrs], _exchange_scratch(n), phases


def _chip_exchange(arrs):
    n = len(arrs)

    def phases(ins, outs, send_sems, recv_sems, local_sems):
        x, y, c = _place()
        mine = 2 * x + y
        chips = [(1 - x, y), (x, 1 - y), (1 - x, 1 - y)]
        local = [pltpu.make_async_copy(ins[a].at[mine], outs[a].at[mine], local_sems.at[a]) for a in range(n)]

        def copy(a, j, src_slot, dst_slot):
            return pltpu.make_async_remote_copy(
                src_ref=ins[a].at[src_slot], dst_ref=outs[a].at[dst_slot],
                send_sem=send_sems.at[a * 7 + j], recv_sem=recv_sems.at[a * 7 + j],
                device_id=(*chips[j], c), device_id_type=MESH)

        sends = [copy(a, j, 2 * chips[j][0] + chips[j][1], mine) for a in range(n) for j in range(3)]

        def send():
            for cp in local + sends:
                cp.start()

        def finish():
            for a in range(n):
                for j in range(3):
                    copy(a, j, mine, 2 * chips[j][0] + chips[j][1]).wait_recv()
            for cp in sends:
                cp.wait_send()
            for cp in local:
                cp.wait()

        return [send, finish]

    return list(arrs), [_sds(a.shape, a.dtype) for a in arrs], _exchange_scratch(n), phases


def _pair_sum(name, mine, theirs):
    _, _, r, c = mine.shape

    def body(side_ref, m_ref, t_ref, o_ref):
        o_ref[...] = (m_ref[...].astype(F32) + t_ref[...].astype(F32)).astype(o_ref.dtype)

    return pl.pallas_call(
        body, name=name,
        grid_spec=pltpu.PrefetchScalarGridSpec(
            num_scalar_prefetch=1, grid=(4,),
            in_specs=[pl.BlockSpec((None, None, r, c), lambda j, side: (j, side[0], 0, 0)),
                      pl.BlockSpec((None, r, c), lambda j, side: (j, 0, 0))],
            out_specs=pl.BlockSpec((None, r, c), lambda j, side: (j, 0, 0))),
        out_shape=_sds(theirs.shape, theirs.dtype),
        compiler_params=_params(("parallel",)),
    )(lax.axis_index("c").astype(jnp.int32).reshape(1), mine, theirs)


def _exchange_call(name, exchange):
    arrs, out_shape, scratch, phases = exchange
    n = len(arrs)

    def body(*refs):
        for step in phases(refs[:n], refs[n:2 * n], *refs[2 * n:]):
            step()

    return pl.pallas_call(body, name=name, in_specs=[ANY] * n, out_specs=[ANY] * n,
                          out_shape=out_shape, scratch_shapes=scratch)(*arrs)


def _carry_exchange(exchange, refs, n_in, n_out, first, middle, last):
    arrs, _, _, phases = exchange
    n = len(arrs)
    if n == 0:
        return lambda: None
    ins = refs[n_in:n_in + n]
    outs = refs[n_in + n + n_out:n_in + 2 * n + n_out]
    sems = n_in + 2 * n + n_out
    steps = phases(ins, outs, *refs[sems:sems + 3])
    pl.when(first)(steps[0])
    if len(steps) == 3:
        pl.when(middle)(steps[1])
    return lambda: pl.when(last)(steps[-1])


def _adamw_math(w, g, m, v):
    m2 = ADAM_B1 * m + (1.0 - ADAM_B1) * g
    v2 = ADAM_B2 * v + (1.0 - ADAM_B2) * jnp.square(g)
    m_hat = m2 / (1.0 - ADAM_B1 ** ADAM_STEP)
    v_hat = v2 / (1.0 - ADAM_B2 ** ADAM_STEP)
    delta = -ADAM_LR * (m_hat / (jnp.sqrt(v_hat) + ADAM_EPS) + ADAM_WD * w)
    return delta, m2, v2


def _sum_adamw(name, parts, w, m, v, tr=256):
    p, r, c = parts.shape
    tr = _pick(r, tr, 16)

    def body(p_ref, w_ref, m_ref, v_ref, g_ref, d_ref, m2_ref, v2_ref):
        g = p_ref[0].astype(F32)
        for d in range(1, p):
            g = g + p_ref[d].astype(F32)
        delta, m2, v2 = _adamw_math(w_ref[...], g, m_ref[...], v_ref[...])
        g_ref[...] = g
        d_ref[...] = delta
        m2_ref[...] = m2
        v2_ref[...] = v2

    tile = pl.BlockSpec((tr, c), lambda i: (i, 0))
    return pl.pallas_call(
        body, name=name, grid=(r // tr,),
        in_specs=[pl.BlockSpec((p, tr, c), lambda i: (0, i, 0)), tile, tile, tile],
        out_specs=[tile] * 4, out_shape=[_sds((r, c))] * 4,
        compiler_params=_params(("parallel",)),
    )(parts, w, m, v)


def _sum_parts(name, parts):
    p, r, c = parts.shape

    def body(p_ref, o_ref):
        g = p_ref[0]
        for d in range(1, p):
            g = g + p_ref[d]
        o_ref[...] = g

    return pl.pallas_call(
        body, name=name, out_shape=_sds((r, c)),
        in_specs=[pl.BlockSpec(memory_space=pltpu.VMEM)], out_specs=pl.BlockSpec(memory_space=pltpu.VMEM),
    )(parts)


WEIGHTS = ["norm_mix_pre", "w_in", "conv_dw_w", "conv_dw_b", "conv_ln_g", "conv_ln_b", "w_conv_branch",
           "b_conv_branch", "w_att_branch", "w_out", "norm_mix_post", "norm_ffn_pre", "w_ffn_up", "w_ffn_down",
           "norm_ffn_post"]
COL_SHARDED = ["w_conv_branch", "w_att_branch"]
ROW_SHARDED = ["w_out", "w_ffn_down"]
TRANSPOSED = ["w_in", "w_ffn_up"]
VECTORS = ["norm_mix_pre", "conv_dw_b", "conv_ln_g", "conv_ln_b", "b_conv_branch", "norm_mix_post",
           "norm_ffn_pre", "norm_ffn_post"]


def _cols_to_full(g):
    return g.transpose(1, 0, 2).reshape(g.shape[1], N_DEV * g.shape[2])


def _full_to_cols(f):
    return f.reshape(f.shape[0], N_DEV, f.shape[1] // N_DEV).transpose(1, 0, 2)


def _pack_vectors(vecs):
    rows = [jnp.pad(vecs[nm].reshape(-1), (0, D_MODEL - vecs[nm].size)) for nm in VECTORS]
    return jnp.stack(rows)


def _unpack_vectors(packed, sizes):
    return {nm: packed[n, :sizes[nm]] for n, nm in enumerate(VECTORS)}


def kernel(x, norm_mix_pre, w_in, conv_dw_w, conv_dw_b, conv_ln_g, conv_ln_b, w_conv_branch, b_conv_branch, w_att_branch, w_out, norm_mix_post, norm_ffn_pre, w_ffn_up, w_ffn_down, norm_ffn_post, loss_target, m_norm_mix_pre, m_w_in, m_conv_dw_w, m_conv_dw_b, m_conv_ln_g, m_conv_ln_b, m_w_conv_branch, m_b_conv_branch, m_w_att_branch, m_w_out, m_norm_mix_post, m_norm_ffn_pre, m_w_ffn_up, m_w_ffn_down, m_norm_ffn_post, v_norm_mix_pre, v_w_in, v_conv_dw_w, v_conv_dw_b, v_conv_ln_g, v_conv_ln_b, v_w_conv_branch, v_b_conv_branch, v_w_att_branch, v_w_out, v_norm_mix_post, v_norm_ffn_pre, v_w_ffn_up, v_w_ffn_down, v_norm_ffn_post):
    ws = dict(zip(WEIGHTS, [norm_mix_pre, w_in, conv_dw_w, conv_dw_b, conv_ln_g, conv_ln_b, w_conv_branch,
                            b_conv_branch, w_att_branch, w_out, norm_mix_post, norm_ffn_pre, w_ffn_up, w_ffn_down,
                            norm_ffn_post]))
    ms = dict(zip(WEIGHTS, [m_norm_mix_pre, m_w_in, m_conv_dw_w, m_conv_dw_b, m_conv_ln_g, m_conv_ln_b,
                            m_w_conv_branch, m_b_conv_branch, m_w_att_branch, m_w_out, m_norm_mix_post,
                            m_norm_ffn_pre, m_w_ffn_up, m_w_ffn_down, m_norm_ffn_post]))
    vs = dict(zip(WEIGHTS, [v_norm_mix_pre, v_w_in, v_conv_dw_w, v_conv_dw_b, v_conv_ln_g, v_conv_ln_b,
                            v_w_conv_branch, v_b_conv_branch, v_w_att_branch, v_w_out, v_norm_mix_post,
                            v_norm_ffn_pre, v_w_ffn_up, v_w_ffn_down, v_norm_ffn_post]))

    dw_block = jnp.pad(conv_dw_w, ((0, 1), (0, 0)))
    g_in, g_dw = _exchange_call("gather_first", _gather_exchange([w_in.T.astype(BF16), dw_block]))
    full = {"w_in": _full_weight("w_in", g_in), "conv_dw_w": _cols_to_full(g_dw)}
    for nm in VECTORS:
        full[nm] = ws[nm].reshape(1, -1)

    loss_local, grad_x, received, grads = _local_step(
        x[0], loss_target[0], full, {nm: (ws[nm].T if nm in TRANSPOSED else ws[nm]).astype(BF16) for nm in LATE})

    loss_at = (VECTORS.index("conv_dw_b"), CONV_DIM)
    small = _exchange_call("gather_small_grads", _gather_exchange(
        [_pack_vectors(grads).at[loss_at].set(loss_local), grads["conv_dw_w"]]))
    out_g, out_d, out_m, out_v = {}, {}, {}, {}
    for nm in LATE + ["w_in"]:
        if nm in TRANSPOSED:
            res = _sum_adamw("adamw_" + nm, received[nm], ws[nm].T, ms[nm].T, vs[nm].T)
            out_g[nm], out_d[nm], out_m[nm], out_v[nm] = [r.T for r in res]
        else:
            out_g[nm], out_d[nm], out_m[nm], out_v[nm] = _sum_adamw("adamw_" + nm, received[nm], ws[nm], ms[nm], vs[nm])
    sizes = {nm: ws[nm].size for nm in VECTORS}
    vec = _sum_adamw("adamw_vectors", small[0], _pack_vectors(ws), _pack_vectors(ms), _pack_vectors(vs))
    for res, dst in zip(vec, (out_g, out_d, out_m, out_v)):
        dst.update(_unpack_vectors(res, sizes))
    loss = vec[0][loss_at]
    dw_full = _sum_parts("sum_dw_grads", small[1])
    me = _slot(*_place())
    dw_mine = lax.dynamic_slice(dw_full, (0, me * (CONV_DIM // N_DEV)), (CONV_WIDTH, CONV_DIM // N_DEV))
    nm = "conv_dw_w"
    out_g[nm], out_d[nm], out_m[nm], out_v[nm] = _sum_adamw("adamw_dw", dw_mine[None], ws[nm], ms[nm], vs[nm])

    outs = [loss, grad_x[None]]
    for group in (out_g, out_d, out_m, out_v):
        outs += [group[nm] for nm in WEIGHTS]
    return tuple(outs)
```

```python
import math

import jax
import jax.numpy as jnp
from jax import lax
from jax.experimental import pallas as pl
from jax.experimental.pallas import tpu as pltpu

F32 = jnp.float32
BF16 = jnp.bfloat16

N_DEV = 8
D_MODEL = 1024
CONV_DIM = 512
CONV_WIDTH = 31
N_HEADS = 8
HEAD_DIM = 64
ATT_DIM = N_HEADS * HEAD_DIM
D_FF = 2816
EPS = 1e-6
IN_SPLITS = (0, 1024, 1536, 2048, 2560, 3584, 4608)

ADAM_LR = 0.001
ADAM_B1 = 0.9
ADAM_B2 = 0.999
ADAM_EPS = 1e-08
ADAM_WD = 0.01
ADAM_STEP = 10

LANES = 128
SUBLANES = 8
HALO = 32
ATT_TILE = 256
ATT_PART = 192
DEAD_SUM = -120.0
VMEM_LIMIT = 56 * 1024 * 1024
MESH = pl.DeviceIdType.MESH
ANY = pl.BlockSpec(memory_space=pl.ANY)


def _pick(dim, target, align=LANES):
    t = min(dim, target)
    t -= t % align
    while t >= align:
        if dim % t == 0:
            return t
        t -= align
    return dim


def _params(semantics):
    return pltpu.CompilerParams(dimension_semantics=semantics, vmem_limit_bytes=VMEM_LIMIT)


def _matmul(a, b, *, name, ta=False, tb=False, out_dtype=F32):
    m, k = (a.shape[1], a.shape[0]) if ta else a.shape
    n, k2 = b.shape if tb else (b.shape[1], b.shape[0])
    assert k == k2, (a.shape, b.shape, ta, tb)
    tm, tn, tk = _pick(m, 1408 if ta else 512), _pick(n, 1536), _pick(k, 1536)
    nk = k // tk
    dims = (((0 if ta else 1,), (1 if tb else 0,)), ((), ()))

    def body(a_ref, b_ref, o_ref, *acc):
        part = lax.dot_general(a_ref[...], b_ref[...], dims, preferred_element_type=F32)
        if nk == 1:
            o_ref[...] = part.astype(o_ref.dtype)
            return
        acc_ref, = acc
        kk = pl.program_id(2)

        @pl.when(kk == 0)
        def _():
            acc_ref[...] = part

        @pl.when((kk > 0) & (kk < nk - 1))
        def _():
            acc_ref[...] += part

        @pl.when(kk == nk - 1)
        def _():
            o_ref[...] = (acc_ref[...] + part).astype(o_ref.dtype)

    a_spec = pl.BlockSpec((tk, tm), lambda j, i, kk: (kk, i)) if ta else pl.BlockSpec((tm, tk), lambda j, i, kk: (i, kk))
    b_spec = (pl.BlockSpec((tn, tk), lambda j, i, kk: (j, kk)) if tb
              else pl.BlockSpec((tk, tn), lambda j, i, kk: (kk, j)))
    return pl.pallas_call(
        body, name=name, grid=(n // tn, m // tm, nk),
        in_specs=[a_spec, b_spec],
        out_specs=pl.BlockSpec((tm, tn), lambda j, i, kk: (i, j)),
        out_shape=jax.ShapeDtypeStruct((m, n), out_dtype),
        scratch_shapes=[pltpu.VMEM((tm, tn), F32)] if nk > 1 else [],
        compiler_params=_params(("parallel", "parallel", "arbitrary")),
    )(a, b)


NO_EXCHANGE = ([], [], [], None)


def _sweep_marks(nt):
    i = pl.program_id(0)
    return i == 0, i == (3 * nt) // 4, i == nt - 1


def _rowwise(name, fn, rows, bcasts, row_outs, red_outs=(), tm=256, exchange=NO_EXCHANGE):
    s = rows[0].shape[0]
    tm = _pick(s, tm, 16)
    nt = s // tm
    resident = pl.Buffered(1)
    nr, nb, no, nd = len(rows), len(bcasts), len(row_outs), len(red_outs)
    x_arrs, x_shape, x_scratch, _ = exchange
    nx = len(x_arrs)
    first_out = nr + nb + nx

    def body(*refs):
        finish_exchange = _carry_exchange(exchange, refs, nr + nb, no + nd, *_sweep_marks(nt))
        ins = [r[...] for r in refs[:nr + nb]]
        outs, reds = fn(*ins)
        for ref, val in zip(refs[first_out:first_out + no], outs):
            ref[...] = val.astype(ref.dtype)
        i = pl.program_id(0)
        for ref, val in zip(refs[first_out + no:first_out + no + nd], reds):
            @pl.when(i == 0)
            def _():
                ref[...] = val

            @pl.when(i > 0)
            def _():
                ref[...] += val
        finish_exchange()

    def row_spec(a):
        assert a.shape[-2] % nt == 0, (name, a.shape, nt)
        if len(a.shape) == 3:
            return pl.BlockSpec((a.shape[0], a.shape[1] // nt, a.shape[2]), lambda i: (0, i, 0))
        return pl.BlockSpec((a.shape[0] // nt, a.shape[1]), lambda i: (i, 0))

    in_specs = [row_spec(r) for r in rows]
    in_specs += [pl.BlockSpec(b.shape, lambda i: (0, 0), pipeline_mode=resident) for b in bcasts]
    out_specs = [row_spec(o) for o in row_outs]
    out_specs += [pl.BlockSpec(d.shape, lambda i: (0, 0)) for d in red_outs]
    return pl.pallas_call(
        body, name=name, grid=(nt,), in_specs=in_specs + [ANY] * nx, out_specs=out_specs + [ANY] * nx,
        out_shape=list(row_outs) + list(red_outs) + x_shape, scratch_shapes=x_scratch,
        compiler_params=_params(("arbitrary",)),
    )(*rows, *bcasts, *x_arrs)


def _sds(shape, dtype=F32):
    return jax.ShapeDtypeStruct(shape, dtype)


def _rms(x, g):
    y = x * lax.rsqrt(jnp.mean(x * x, axis=-1, keepdims=True) + EPS)
    return y * g


def _silu(x):
    return x * jax.nn.sigmoid(x)


def _swiglu(g, u):
    return _silu(g) * u


def _ln_silu(u, g, b):
    mu = jnp.mean(u, axis=-1, keepdims=True)
    var = jnp.mean(jnp.square(u - mu), axis=-1, keepdims=True)
    return _silu((u - mu) * lax.rsqrt(var + EPS) * g + b)


def _merge(conv_pre, att_out, g_conv, g_att, b_cb):
    return jax.nn.sigmoid(g_conv) * (conv_pre + b_cb) + jax.nn.sigmoid(g_att) * att_out


def _glu(t):
    return t[:, :CONV_DIM] * jax.nn.sigmoid(t[:, CONV_DIM:])


def _shifted_reader(buf, shifted, tm):
    for b in range(1, SUBLANES):
        shifted[b - 1, :, :] = buf[pl.ds(b, tm + HALO - SUBLANES), :]

    def read(o):
        a, b = divmod(o, SUBLANES)
        return buf[pl.ds(SUBLANES * a, tm), :] if b == 0 else shifted[b - 1, pl.ds(SUBLANES * a, tm), :]

    return read


def _conv_fwd(conv_in, w_pad, b, ln_g, ln_b, exchange, tm=256):
    s = conv_in.shape[0]
    tm = _pick(s, tm, HALO)
    ratio = tm // HALO
    x_arrs, x_shape, x_scratch, _ = exchange
    nx = len(x_arrs)

    def body(*refs):
        main_ref, halo_ref, w_ref, b_ref, g_ref, be_ref = refs[:6]
        u3_ref, u1_ref = refs[6 + nx:8 + nx]
        buf, shifted = refs[-2:]
        finish_exchange = _carry_exchange(exchange, refs, 6, 2, *_sweep_marks(s // tm))
        i = pl.program_id(0)
        buf[0:HALO, :] = _glu(halo_ref[...]) * (i > 0).astype(F32)
        buf[HALO:HALO + tm, :] = _glu(main_ref[...])
        read = _shifted_reader(buf, shifted, tm)
        acc = jnp.zeros((tm, CONV_DIM), F32) + b_ref[...]
        for j in range(CONV_WIDTH):
            acc = acc + w_ref[j:j + 1, :] * read(HALO - (CONV_WIDTH - 1) + j)
        u1_ref[...] = acc
        u3_ref[...] = _ln_silu(acc, g_ref[...], be_ref[...]).astype(u3_ref.dtype)
        finish_exchange()

    res = pl.pallas_call(
        body, name="conv_fwd", grid=(s // tm,),
        in_specs=[pl.BlockSpec((tm, 2 * CONV_DIM), lambda i: (i, 0)),
                  pl.BlockSpec((HALO, 2 * CONV_DIM), lambda i: (jnp.maximum(i * ratio - 1, 0), 0)),
                  pl.BlockSpec(w_pad.shape, lambda i: (0, 0)),
                  pl.BlockSpec(b.shape, lambda i: (0, 0)),
                  pl.BlockSpec(ln_g.shape, lambda i: (0, 0)),
                  pl.BlockSpec(ln_b.shape, lambda i: (0, 0))] + [ANY] * nx,
        out_specs=[pl.BlockSpec((tm, CONV_DIM), lambda i: (i, 0)),
                   pl.BlockSpec((tm, CONV_DIM), lambda i: (i, 0))] + [ANY] * nx,
        out_shape=[_sds((s, CONV_DIM), BF16), _sds((s, CONV_DIM), F32)] + x_shape,
        scratch_shapes=x_scratch + [pltpu.VMEM((tm + HALO, CONV_DIM), F32),
                                    pltpu.VMEM((SUBLANES - 1, tm + HALO - SUBLANES, CONV_DIM), F32)],
        compiler_params=_params(("arbitrary",)),
    )(conv_in, conv_in, w_pad, b, ln_g, ln_b, *x_arrs)
    return res[0], res[1], res[2:]


def _conv_bwd(conv_in, u1, du3, ln_g, ln_b, w_pad, exchange, tm=256):
    s = conv_in.shape[0]
    tm = _pick(s, tm, HALO)
    ratio = tm // HALO
    nt = s // tm
    last_halo = s // HALO - 1
    x_arrs, x_shape, x_scratch, _ = exchange
    nx = len(x_arrs)

    def body(*refs):
        main_ref, halo_ref, u1_ref, u1n_ref, du3_ref, du3n_ref, g_ref, be_ref, w_ref = refs[:9]
        dci_ref, dw_ref, db_ref, dg_ref, dbe_ref = refs[9 + nx:14 + nx]
        ubuf, dbuf, ushift, dshift = refs[-4:]
        finish_exchange = _carry_exchange(exchange, refs, 9, 5, *_sweep_marks(nt))
        i = pl.program_id(0)
        main = main_ref[...]
        a = main[:, :CONV_DIM]
        sb = jax.nn.sigmoid(main[:, CONV_DIM:])
        ubuf[0:HALO, :] = _glu(halo_ref[...]) * (i > 0).astype(F32)
        ubuf[HALO:HALO + tm, :] = a * sb

        def ln_bwd(u1t, du3t):
            _, vjp = jax.vjp(_ln_silu, u1t, g_ref[...], be_ref[...])
            return vjp(du3t)

        du, dg, dbe = ln_bwd(u1_ref[...], du3_ref[...])
        dbuf[0:tm, :] = du
        dbuf[tm:tm + HALO, :] = ln_bwd(u1n_ref[...], du3n_ref[...])[0] * (i < nt - 1).astype(F32)

        @pl.when(i == 0)
        def _():
            dw_ref[...] = jnp.zeros_like(dw_ref)
            db_ref[...] = jnp.zeros_like(db_ref)
            dg_ref[...] = jnp.zeros_like(dg_ref)
            dbe_ref[...] = jnp.zeros_like(dbe_ref)

        dg_ref[...] += dg
        dbe_ref[...] += dbe

        read_u = _shifted_reader(ubuf, ushift, tm)
        read_d = _shifted_reader(dbuf, dshift, tm)
        du0 = jnp.zeros((tm, CONV_DIM), F32)
        for j in range(CONV_WIDTH):
            du0 = du0 + w_ref[j:j + 1, :] * read_d(CONV_WIDTH - 1 - j)
            dw_ref[j:j + 1, :] += jnp.sum(du * read_u(HALO - (CONV_WIDTH - 1) + j), axis=0, keepdims=True)
        db_ref[...] += jnp.sum(du, axis=0, keepdims=True)
        dci_ref[:, :CONV_DIM] = (du0 * sb).astype(dci_ref.dtype)
        dci_ref[:, CONV_DIM:] = (du0 * a * sb * (1.0 - sb)).astype(dci_ref.dtype)
        finish_exchange()

    res = pl.pallas_call(
        body, name="conv_bwd", grid=(nt,),
        in_specs=[pl.BlockSpec((tm, 2 * CONV_DIM), lambda i: (i, 0)),
                  pl.BlockSpec((HALO, 2 * CONV_DIM), lambda i: (jnp.maximum(i * ratio - 1, 0), 0))]
        + [pl.BlockSpec((tm, CONV_DIM), lambda i: (i, 0)),
           pl.BlockSpec((HALO, CONV_DIM), lambda i: (jnp.minimum((i + 1) * ratio, last_halo), 0))] * 2
        + [pl.BlockSpec((1, CONV_DIM), lambda i: (0, 0))] * 2 + [pl.BlockSpec(w_pad.shape, lambda i: (0, 0))]
        + [ANY] * nx,
        out_specs=[pl.BlockSpec((tm, 2 * CONV_DIM), lambda i: (i, 0)),
                   pl.BlockSpec(w_pad.shape, lambda i: (0, 0))]
        + [pl.BlockSpec((1, CONV_DIM), lambda i: (0, 0))] * 3 + [ANY] * nx,
        out_shape=[_sds((s, 2 * CONV_DIM), BF16), _sds(w_pad.shape)] + [_sds((1, CONV_DIM))] * 3 + x_shape,
        scratch_shapes=x_scratch + [pltpu.VMEM((tm + HALO, CONV_DIM), F32)] * 2
        + [pltpu.VMEM((SUBLANES - 1, tm + HALO - SUBLANES, CONV_DIM), F32)] * 2,
        compiler_params=_params(("arbitrary",)),
    )(conv_in, conv_in, u1, u1, du3, du3, ln_g, ln_b, w_pad, *x_arrs)
    return res[:5], res[5:]


def _logsig_neg(z):
    return jnp.minimum(-z, 0.0) - jnp.log(1.0 + jnp.exp(-jnp.abs(z)))


def _split_dot(val, tri):
    hi = val.astype(BF16)
    lo = (val - hi.astype(F32)).astype(BF16)
    return jnp.dot(hi, tri, preferred_element_type=F32) + jnp.dot(lo, tri, preferred_element_type=F32)


def _attn_masks(t, later):
    row = lax.broadcasted_iota(jnp.int32, (t, t), 0)
    col = lax.broadcasted_iota(jnp.int32, (t, t), 1)
    tri = jnp.where(row > col if later else row <= col, 1.0, 0.0).astype(BF16)
    return col < row, tri


def _grid_marks(h, nq):
    hh, i = pl.program_id(0), pl.program_id(1)
    return (hh == 0) & (i == 0), (hh == (3 * h) // 4) & (i == 0), (hh == h - 1) & (i == nq - 1)


def _head_masks(shape):
    lane = lax.broadcasted_iota(jnp.int32, shape, len(shape) - 1)
    return lane < HEAD_DIM, lane >= HEAD_DIM


def _per_head(blk):
    m0, m1 = _head_masks(blk.shape)
    zero = jnp.zeros_like(blk)
    return jnp.where(m0, blk, zero), jnp.where(m1, blk, zero)


NT = (((1,), (1,)), ((), ()))
TN = (((0,), (0,)), ((), ()))


def _with_top(whole, top):
    rows = top.shape[0]
    return top if rows == whole.shape[0] else jnp.concatenate([top, whole[rows:]], axis=0)


def _attn_fwd(q, k, v, exchange):
    s = q.shape[0]
    hp = q.shape[1] // LANES
    t = ATT_TILE
    scale = 1.0 / math.sqrt(HEAD_DIM)
    x_arrs, x_shape, x_scratch, _ = exchange
    nx = len(x_arrs)

    def body(*refs):
        q_ref, k_ref, v_ref = refs[:3]
        o_ref, lt_ref, nb_ref = refs[3 + nx:6 + nx]
        finish_exchange = _carry_exchange(exchange, refs, 3, 3, *_grid_marks(hp, s // t))
        i = pl.program_id(1)
        qs = _per_head((q_ref[...].astype(F32) * scale).astype(BF16))
        causal, tri = _attn_masks(t, later=True)

        def step(kb, carry, masked, rows):
            cs, acc = carry
            off = pl.multiple_of(kb * t, t)
            kblk = k_ref[pl.ds(off, t), :]
            vs = _per_head(v_ref[pl.ds(off, t), :])
            acc_top = acc[:rows]
            new_cs = []
            for hd in range(2):
                z = lax.dot_general(qs[hd][:rows], kblk, NT, preferred_element_type=F32)
                l = _logsig_neg(z)
                if masked:
                    l = jnp.where(causal, l, 0.0)
                e = z + l + _split_dot(l, tri) + cs[hd][:rows]
                if masked:
                    e = jnp.where(causal, e, -1e30)
                acc_top = acc_top + jnp.dot(jnp.exp(e).astype(BF16), vs[hd], preferred_element_type=F32)
                new_cs.append(_with_top(cs[hd], cs[hd][:rows] + jnp.sum(l, axis=1, keepdims=True)))
            return tuple(new_cs), _with_top(acc, acc_top)

        zero = jnp.zeros((t, 1), F32)
        carry = step(i, ((zero, zero), jnp.zeros((t, LANES), F32)), True, t)

        def live(cs, lo, hi):
            return jnp.maximum(jnp.max(cs[0][lo:hi]), jnp.max(cs[1][lo:hi])) > DEAD_SUM

        def more(state):
            n, _, (cs, _) = state
            return (n < i) & live(cs, 0, t)

        def sweep(state):
            n, n_full, cr = state
            whole = live(cr[0], ATT_PART, t)
            cr = lax.cond(whole, lambda c: step(i - 1 - n, c, False, t), lambda c: step(i - 1 - n, c, False, ATT_PART), cr)
            return n + 1, n_full + whole.astype(jnp.int32), cr

        n_blocks, n_full, carry = lax.while_loop(more, sweep, (jnp.int32(0), jnp.int32(0), carry))
        m0, _ = _head_masks((t, LANES))
        lt_ref[...] = jnp.where(m0, carry[0][0], carry[0][1])
        o_ref[...] = carry[1].astype(o_ref.dtype)
        nb_ref[0, pl.program_id(0), i] = n_blocks.astype(F32)
        nb_ref[1, pl.program_id(0), i] = n_full.astype(F32)
        finish_exchange()

    res = pl.pallas_call(
        body, name="attn_fwd", grid=(hp, s // t),
        in_specs=[pl.BlockSpec((t, LANES), lambda p, i: (i, p)),
                  pl.BlockSpec((s, LANES), lambda p, i: (0, p)),
                  pl.BlockSpec((s, LANES), lambda p, i: (0, p))] + [ANY] * nx,
        out_specs=[pl.BlockSpec((t, LANES), lambda p, i: (i, p)),
                   pl.BlockSpec((None, t, LANES), lambda p, i: (p, i, 0)),
                   pl.BlockSpec(memory_space=pltpu.SMEM)] + [ANY] * nx,
        out_shape=[_sds(q.shape, BF16), _sds((hp, s, LANES), F32), _sds((2, hp, s // t), F32)] + x_shape,
        scratch_shapes=x_scratch,
        compiler_params=_params(("arbitrary", "arbitrary")),
    )(q, k, v, *x_arrs)
    return res[0], res[1], res[2], res[3:]


def _attn_bwd(q, k, v, do, ltot, n_blocks, exchange):
    s = q.shape[0]
    hp = q.shape[1] // LANES
    t = ATT_TILE
    scale = 1.0 / math.sqrt(HEAD_DIM)
    x_arrs, x_shape, x_scratch, _ = exchange
    nx = len(x_arrs)

    def body(*refs):
        q_ref, k_ref, v_ref, do_ref, lt_ref, nb_ref = refs[:6]
        dq_ref, dk_ref, dv_ref = refs[6 + nx:9 + nx]
        finish_exchange = _carry_exchange(exchange, refs, 6, 3, *_grid_marks(hp, s // t))
        i = pl.program_id(1)
        n_blocks = jnp.clip(nb_ref[0, pl.program_id(0), i].astype(jnp.int32), 0, i)
        n_full = jnp.clip(nb_ref[1, pl.program_id(0), i].astype(jnp.int32), 0, n_blocks)

        @pl.when(i == 0)
        def _():
            dk_ref[...] = jnp.zeros_like(dk_ref)
            dv_ref[...] = jnp.zeros_like(dv_ref)

        qb = q_ref[...]
        qm = _per_head(qb)
        qs = _per_head((qb.astype(F32) * scale).astype(BF16))
        dos = _per_head(do_ref[...])
        lts = (lt_ref[:, 0:1], lt_ref[:, HEAD_DIM:HEAD_DIM + 1])
        causal, tri = _attn_masks(t, later=False)

        def step(kb, carry, masked, rows):
            cls, cgs, dq = carry
            off = pl.multiple_of(kb * t, t)
            kblk = k_ref[pl.ds(off, t), :]
            vblk = v_ref[pl.ds(off, t), :]
            ks = _per_head(kblk)
            dq_top = dq[:rows]
            dk = jnp.zeros((t, LANES), F32)
            dv = jnp.zeros((t, LANES), F32)
            new_cls, new_cgs = [], []
            for hd in range(2):
                z = lax.dot_general(qs[hd][:rows], kblk, NT, preferred_element_type=F32)
                l = _logsig_neg(z)
                if masked:
                    l = jnp.where(causal, l, 0.0)
                e = z + l + ((lts[hd][:rows] - cls[hd][:rows]) - _split_dot(l, tri))
                if masked:
                    e = jnp.where(causal, e, -1e30)
                a = jnp.exp(e)
                g = lax.dot_general(dos[hd][:rows], vblk, NT, preferred_element_type=F32) * a
                p = cgs[hd][:rows] + jnp.dot(g.astype(BF16), tri, preferred_element_type=F32) - g
                el = jnp.exp(l)
                dz = g * el - p * (1.0 - el)
                if masked:
                    dz = jnp.where(causal, dz, 0.0)
                dzb = (dz * scale).astype(BF16)
                dq_top = dq_top + jnp.dot(dzb, ks[hd], preferred_element_type=F32)
                dk = dk + lax.dot_general(dzb, qm[hd][:rows], TN, preferred_element_type=F32)
                dv = dv + lax.dot_general(a.astype(BF16), dos[hd][:rows], TN, preferred_element_type=F32)
                new_cls.append(_with_top(cls[hd], cls[hd][:rows] + jnp.sum(l, axis=1, keepdims=True)))
                new_cgs.append(_with_top(cgs[hd], cgs[hd][:rows] + jnp.sum(g, axis=1, keepdims=True)))
            dk_ref[pl.ds(off, t), :] += dk
            dv_ref[pl.ds(off, t), :] += dv
            return tuple(new_cls), tuple(new_cgs), _with_top(dq, dq_top)

        zero = jnp.zeros((t, 1), F32)
        init = ((zero, zero), (zero, zero), jnp.zeros((t, LANES), F32))
        carry = lax.fori_loop(i - n_blocks, i - n_full, lambda kb, cr: step(kb, cr, False, ATT_PART), init)
        carry = lax.fori_loop(i - n_full, i, lambda kb, cr: step(kb, cr, False, t), carry)
        carry = step(i, carry, True, t)
        dq_ref[...] = carry[2]
        finish_exchange()

    blk = pl.BlockSpec((t, LANES), lambda p, i: (i, p))
    whole = pl.BlockSpec((s, LANES), lambda p, i: (0, p))
    res = pl.pallas_call(
        body, name="attn_bwd", grid=(hp, s // t),
        in_specs=[blk, whole, whole, blk, pl.BlockSpec((None, t, LANES), lambda p, i: (p, i, 0)),
                  pl.BlockSpec(memory_space=pltpu.SMEM)] + [ANY] * nx,
        out_specs=[blk, whole, whole] + [ANY] * nx,
        out_shape=[_sds(q.shape)] * 3 + x_shape,
        scratch_shapes=x_scratch,
        compiler_params=_params(("arbitrary", "arbitrary")),
    )(q, k, v, do, ltot, n_blocks, *x_arrs)
    return res[0], res[1], res[2], res[3:]


LATE = ["w_conv_branch", "w_att_branch", "w_out", "w_ffn_up", "w_ffn_down"]


def _full_weight(name, gathered):
    return _cols_to_full(gathered) if name in COL_SHARDED else gathered.reshape(-1, gathered.shape[2])


def _grad_slabs(name, grad):
    return _full_to_cols(grad) if name in COL_SHARDED else grad.reshape(N_DEV, -1, grad.shape[1])


def _side_slabs(name, grad):
    slabs = _grad_slabs(name, grad)
    return slabs.reshape((4, 2) + slabs.shape[1:])


def _local_step(x, target, w, late_blocks, opt):
    s = x.shape[0]
    w = dict(w)
    g1, g2, g3, g4 = w["norm_mix_pre"], w["norm_mix_post"], w["norm_ffn_pre"], w["norm_ffn_post"]

    w_in = w["w_in"]

    def proj_fn(xt, g1_, w_in_t):
        h = _rms(xt, g1_).astype(BF16)
        proj = lax.dot_general(h, w_in_t, NT, preferred_element_type=F32)
        return (h, *[proj[:, IN_SPLITS[n]:IN_SPLITS[n + 1]] for n in range(6)]), ()

    mix_weights = ["w_conv_branch", "w_att_branch", "w_out"]
    h1, conv_in, q, k, v, g_conv, g_att = _rowwise(
        "norm_proj", proj_fn, [x], [g1, w_in],
        [_sds((s, D_MODEL), BF16), _sds((s, 2 * CONV_DIM)), _sds((s, ATT_DIM), BF16), _sds((s, ATT_DIM), BF16),
         _sds((s, ATT_DIM), BF16), _sds((s, D_MODEL)), _sds((s, D_MODEL))], tm=512)

    u3, u1, gathered = _conv_fwd(conv_in, w["conv_dw_w"], w["conv_dw_b"], w["conv_ln_g"], w["conv_ln_b"],
                                 _gather_exchange([late_blocks[nm] for nm in mix_weights]))
    for nm, g in zip(mix_weights, gathered):
        w[nm] = _full_weight(nm, g)
    att, ltot, n_blocks, (g_up,) = _attn_fwd(q, k, v, _gather_exchange([late_blocks["w_ffn_up"]]))
    w["w_ffn_up"] = _full_weight("w_ffn_up", g_up)

    def merge_fn(u3t, at, gc, ga, xt, w_cb, w_ab, b_cb, w_out, g2_, g3_):
        cp = jnp.dot(u3t, w_cb, preferred_element_type=F32)
        ao = jnp.dot(at, w_ab, preferred_element_type=F32)
        mg = _merge(cp, ao, gc, ga, b_cb).astype(BF16)
        mix_ = jnp.dot(mg, w_out, preferred_element_type=F32)
        x2_ = xt + _rms(mix_, g2_)
        return (mg, cp, ao, mix_, x2_, _rms(x2_, g3_)), ()

    merged, conv_pre, att_out, mix, x2, h2 = _rowwise(
        "branch_merge_mix", merge_fn, [u3, att, g_conv, g_att, x],
        [w["w_conv_branch"], w["w_att_branch"], w["b_conv_branch"], w["w_out"], g2, g3],
        [_sds((s, D_MODEL), BF16)] * 3 + [_sds((s, D_MODEL)), _sds((s, D_MODEL)), _sds((s, D_MODEL), BF16)], tm=512)

    def ffn_up_fn(ht, w_up_t):
        gu_ = lax.dot_general(ht, w_up_t, NT, preferred_element_type=F32)
        return (gu_, _swiglu(gu_[:, :D_FF], gu_[:, D_FF:])), ()

    gu, act, g_down = _rowwise("ffn_up", ffn_up_fn, [h2], [w["w_ffn_up"]],
                               [_sds((s, 2 * D_FF), BF16), _sds((s, D_FF), BF16)], tm=512,
                               exchange=_gather_exchange([late_blocks["w_ffn_down"]]))
    w["w_ffn_down"] = _full_weight("w_ffn_down", g_down)

    def final_fn(at, x2t, tgt, w_down, g4_):
        ff = jnp.dot(at, w_down, preferred_element_type=F32)
        n4, vjp = jax.vjp(_rms, ff, g4_)
        err = x2t + n4 - tgt
        dy = err * (1.0 / D_MODEL)
        dff, dg4 = vjp(dy)
        return (dy, dff), (jnp.sum(err * err, axis=0, keepdims=True), dg4)

    dy, dff, loss_cols, d_g4 = _rowwise("ffn_down_loss", final_fn, [act, x2, target], [w["w_ffn_down"], g4],
                                        [_sds((s, D_MODEL)), _sds((s, D_MODEL), BF16)],
                                        [_sds((1, D_MODEL)), _sds((1, D_MODEL))], tm=512)
    loss = 0.5 * jnp.sum(loss_cols) / D_MODEL

    d_w_down = _matmul(act, dff, ta=True, name="d_w_down", out_dtype=BF16)

    def act_bwd_fn(dfft, gut, w_down):
        d_act = lax.dot_general(dfft, w_down, NT, preferred_element_type=F32)
        gu_ = gut.astype(F32)
        _, vjp = jax.vjp(_swiglu, gu_[:, :D_FF], gu_[:, D_FF:])
        return (jnp.concatenate(vjp(d_act), axis=1),), ()

    down_slabs = _side_slabs("w_ffn_down", d_w_down)
    dgu, theirs = _rowwise("ffn_act_bwd", act_bwd_fn, [dff, gu], [w["w_ffn_down"]], [_sds((s, 2 * D_FF), BF16)],
                           exchange=_pair_exchange([down_slabs]))
    down_sums = _pair_sum("pair_sum_w_ffn_down", down_slabs, theirs)
    d_w_up = _matmul(dgu, h2, ta=True, name="d_w_up", out_dtype=BF16)
    received = {}
    up_slabs = _side_slabs("w_ffn_up", d_w_up)

    def mid_bwd_fn(dgut, xt, mt, dyt, w_up_t, g2_, g3_):
        dh = jnp.dot(dgut, w_up_t, preferred_element_type=F32)
        n2, vjp2 = jax.vjp(_rms, mt, g2_)
        x2_ = xt + n2
        _, vjp3 = jax.vjp(_rms, x2_, g3_)
        dx2_, dg3 = vjp3(dh)
        dx2_ = dx2_ + dyt
        dmix_, dg2 = vjp2(dx2_)
        return (dx2_, dmix_), (dg2, dg3)

    dx2, dmix, d_g2, d_g3, received["w_ffn_down"] = _rowwise(
        "ffn_up_mid_bwd", mid_bwd_fn, [dgu, x, mix, dy], [w["w_ffn_up"], g2, g3],
        [_sds((s, D_MODEL)), _sds((s, D_MODEL), BF16)], [_sds((1, D_MODEL)), _sds((1, D_MODEL))], tm=512,
        exchange=_chip_exchange([down_sums]))
    d_w_out = _matmul(merged, dmix, ta=True, name="d_w_out", out_dtype=BF16)

    def merge_bwd_fn(dmt, cp, ao, gc, ga, w_out, w_cb, w_ab, b_cb):
        dm = lax.dot_general(dmt, w_out, NT, preferred_element_type=F32)
        _, vjp = jax.vjp(_merge, cp.astype(F32), ao.astype(F32), gc, ga, b_cb)
        dcp, dao, dgc, dga, dbias = vjp(dm)
        dcp, dao = dcp.astype(BF16), dao.astype(BF16)
        du3_ = lax.dot_general(dcp, w_cb, NT, preferred_element_type=F32)
        datt_ = lax.dot_general(dao, w_ab, NT, preferred_element_type=F32)
        return (dcp, dao, dgc, dga, du3_, datt_), (dbias,)

    d_conv_out, d_att_out, d_g_conv, d_g_att, du3, d_att, d_b_cb, theirs = _rowwise(
        "merge_bwd", merge_bwd_fn, [dmix, conv_pre, att_out, g_conv, g_att],
        [w["w_out"], w["w_conv_branch"], w["w_att_branch"], w["b_conv_branch"]],
        [_sds((s, D_MODEL), BF16)] * 4 + [_sds((s, CONV_DIM)), _sds((s, ATT_DIM), BF16)], [_sds((1, D_MODEL))], tm=512,
        exchange=_pair_exchange([up_slabs]))

    d_w_cb = _matmul(u3, d_conv_out, ta=True, name="d_w_conv_branch", out_dtype=BF16)
    d_w_ab = _matmul(att, d_att_out, ta=True, name="d_w_att_branch", out_dtype=BF16)

    dq, dk, dv, (received["w_ffn_up"],) = _attn_bwd(
        q, k, v, d_att, ltot, n_blocks, _chip_exchange([_pair_sum("pair_sum_w_ffn_up", up_slabs, theirs)]))

    mix_grads = {"w_conv_branch": d_w_cb, "w_att_branch": d_w_ab, "w_out": d_w_out}
    (d_conv_in, d_dw_w, d_dw_b, d_ln_g, d_ln_b), landed = _conv_bwd(
        conv_in, u1, du3, w["conv_ln_g"], w["conv_ln_b"], w["conv_dw_w"],
        _scatter_exchange([_grad_slabs(nm, mix_grads[nm]) for nm in mix_weights]))
    received.update(zip(mix_weights, landed))

    d_proj = jnp.concatenate([d_conv_in, dq.astype(BF16), dk.astype(BF16), dv.astype(BF16), d_g_conv, d_g_att],
                             axis=1)
    d_w_in = _matmul(d_proj, h1, ta=True, name="d_w_in", out_dtype=BF16)
    in_slabs = _side_slabs("w_in", d_w_in)
    (theirs,) = _exchange_call("pair_swap_w_in", _pair_exchange([in_slabs]))

    early = list(opt)

    def pre_bwd_fn(dpt, xt, dx2t, *rest):
        jobs, (w_in_t, g_) = rest[:-2], rest[-2:]
        dh = jnp.dot(dpt, w_in_t, preferred_element_type=F32)
        _, vjp = jax.vjp(_rms, xt, g_)
        dx_, dg_ = vjp(dh)
        updates = [_sum_adamw_tile(*jobs[4 * n:4 * n + 4]) for n in range(len(early))]
        return (dx_ + dx2t, *[u for four in updates for u in four]), (dg_,)

    res = _rowwise(
        "proj_norm_bwd", pre_bwd_fn,
        [d_proj, x, dx2] + [a for nm in early for a in (received[nm], *opt[nm])], [w_in, g1],
        [_sds((s, D_MODEL))] + [_sds(opt[nm][0].shape) for nm in early for _ in range(4)],
        [_sds((1, D_MODEL))], tm=512, exchange=_chip_exchange([_pair_sum("pair_sum_w_in", in_slabs, theirs)]))
    grad_x, d_g1, received["w_in"] = res[0], res[-2], res[-1]
    updated = {nm: res[1 + 4 * n:5 + 4 * n] for n, nm in enumerate(early)}

    grads = {
        "norm_mix_pre": d_g1, "conv_dw_w": d_dw_w, "conv_dw_b": d_dw_b,
        "conv_ln_g": d_ln_g, "conv_ln_b": d_ln_b, "b_conv_branch": d_b_cb,
        "norm_mix_post": d_g2, "norm_ffn_pre": d_g3, "norm_ffn_post": d_g4,
    }
    return loss, grad_x, received, updated, grads


def _place():
    x, y, c = lax.axis_index("x"), lax.axis_index("y"), lax.axis_index("c")
    return x, y, c


def _slot(px, py, pc):
    return 4 * px + 2 * py + pc


def _exchange_scratch(n):
    return [pltpu.SemaphoreType.DMA((7 * n,)), pltpu.SemaphoreType.DMA((7 * n,)), pltpu.SemaphoreType.DMA((n,))]


def _gather_exchange(arrs):
    n = len(arrs)

    def phases(ins, outs, send_sems, recv_sems, local_sems):
        x, y, c = _place()
        me, sibling = (x, y, c), (x, y, 1 - c)
        chips = [(1 - x, y), (x, 1 - y), (1 - x, 1 - y)]

        def copy(a, kk, block, to, src=None):
            dst = outs[a].at[_slot(*block)]
            return pltpu.make_async_remote_copy(
                src_ref=dst if src is None else src, dst_ref=dst,
                send_sem=send_sems.at[a * 7 + kk], recv_sem=recv_sems.at[a * 7 + kk],
                device_id=to, device_id_type=MESH)

        mine = [pltpu.make_async_copy(ins[a], outs[a].at[_slot(*me)], local_sems.at[a]) for a in range(n)]
        first = []
        for a in range(n):
            first.append(copy(a, 0, me, sibling, src=ins[a]))
            first += [copy(a, 1 + j, me, (*chip, c), src=ins[a]) for j, chip in enumerate(chips)]
        passed = [copy(a, 4 + j, (*chip, c), sibling) for j, chip in enumerate(chips) for a in range(n)]

        def send():
            for cp in mine + first:
                cp.start()

        def pass_on():
            for j, chip in enumerate(chips):
                for a in range(n):
                    copy(a, 1 + j, (*chip, c), me).wait_recv()
                    passed[j * n + a].start()

        def finish():
            for a in range(n):
                copy(a, 0, sibling, me).wait_recv()
                for j, chip in enumerate(chips):
                    copy(a, 4 + j, (*chip, 1 - c), me).wait_recv()
            for cp in first + passed:
                cp.wait_send()
            for cp in mine:
                cp.wait()

        return [send, pass_on, finish]

    return list(arrs), [_sds((N_DEV,) + a.shape, a.dtype) for a in arrs], _exchange_scratch(n), phases


def _scatter_exchange(arrs):
    n = len(arrs)
    flips = [(fx, fy, fc) for fx in (0, 1) for fy in (0, 1) for fc in (0, 1)][1:]

    def phases(ins, outs, send_sems, recv_sems, local_sems):
        x, y, c = _place()
        mine = _slot(x, y, c)
        local = [pltpu.make_async_copy(ins[a].at[mine], outs[a].at[mine], local_sems.at[a]) for a in range(n)]
        peers = [((1 - x) if fx else x, (1 - y) if fy else y, (1 - c) if fc else c) for fx, fy, fc in flips]

        def copy(a, kk, src_slot, dst_slot):
            return pltpu.make_async_remote_copy(
                src_ref=ins[a].at[src_slot], dst_ref=outs[a].at[dst_slot],
                send_sem=send_sems.at[a * 7 + kk], recv_sem=recv_sems.at[a * 7 + kk],
                device_id=peers[kk], device_id_type=MESH)

        sends = [copy(a, kk, _slot(*peers[kk]), mine) for a in range(n) for kk in range(7)]

        def send():
            for cp in local + sends:
                cp.start()

        def finish():
            for a in range(n):
                for kk in range(7):
                    copy(a, kk, mine, _slot(*peers[kk])).wait_recv()
            for cp in sends:
                cp.wait_send()
            for cp in local:
                cp.wait()

        return [send, finish]

    return list(arrs), [_sds(a.shape, a.dtype) for a in arrs], _exchange_scratch(n), phases


def _pair_exchange(arrs):
    n = len(arrs)

    def phases(ins, outs, send_sems, recv_sems, local_sems):
        x, y, c = _place()

        def copy(a, chip, side):
            return pltpu.make_async_remote_copy(
                src_ref=ins[a].at[chip, side], dst_ref=outs[a].at[chip],
                send_sem=send_sems.at[a * 7 + chip], recv_sem=recv_sems.at[a * 7 + chip],
                device_id=(x, y, 1 - c), device_id_type=MESH)

        sends = [copy(a, chip, 1 - c) for a in range(n) for chip in range(4)]

        def send():
            for cp in sends:
                cp.start()

        def finish():
            for a in range(n):
                for chip in range(4):
                    copy(a, chip, c).wait_recv()
            for cp in sends:
                cp.wait_send()

        return [send, finish]

    return list(arrs), [_sds((4,) + a.shape[2:], a.dtype) for a in arrs], _exchange_scratch(n), phases


def _chip_exchange(arrs):
    n = len(arrs)

    def phases(ins, outs, send_sems, recv_sems, local_sems):
        x, y, c = _place()
        mine = 2 * x + y
        chips = [(1 - x, y), (x, 1 - y), (1 - x, 1 - y)]
        local = [pltpu.make_async_copy(ins[a].at[mine], outs[a].at[mine], local_sems.at[a]) for a in range(n)]

        def copy(a, j, src_slot, dst_slot):
            return pltpu.make_async_remote_copy(
                src_ref=ins[a].at[src_slot], dst_ref=outs[a].at[dst_slot],
                send_sem=send_sems.at[a * 7 + j], recv_sem=recv_sems.at[a * 7 + j],
                device_id=(*chips[j], c), device_id_type=MESH)

        sends = [copy(a, j, 2 * chips[j][0] + chips[j][1], mine) for a in range(n) for j in range(3)]

        def send():
            for cp in local + sends:
                cp.start()

        def finish():
            for a in range(n):
                for j in range(3):
                    copy(a, j, mine, 2 * chips[j][0] + chips[j][1]).wait_recv()
            for cp in sends:
                cp.wait_send()
            for cp in local:
                cp.wait()

        return [send, finish]

    return list(arrs), [_sds(a.shape, a.dtype) for a in arrs], _exchange_scratch(n), phases


def _pair_sum(name, mine, theirs):
    _, _, r, c = mine.shape

    def body(side_ref, m_ref, t_ref, o_ref):
        o_ref[...] = (m_ref[...].astype(F32) + t_ref[...].astype(F32)).astype(o_ref.dtype)

    return pl.pallas_call(
        body, name=name,
        grid_spec=pltpu.PrefetchScalarGridSpec(
            num_scalar_prefetch=1, grid=(4,),
            in_specs=[pl.BlockSpec((None, None, r, c), lambda j, side: (j, side[0], 0, 0)),
                      pl.BlockSpec((None, r, c), lambda j, side: (j, 0, 0))],
            out_specs=pl.BlockSpec((None, r, c), lambda j, side: (j, 0, 0))),
        out_shape=_sds(theirs.shape, theirs.dtype),
        compiler_params=_params(("parallel",)),
    )(lax.axis_index("c").astype(jnp.int32).reshape(1), mine, theirs)


def _exchange_call(name, exchange):
    arrs, out_shape, scratch, phases = exchange
    n = len(arrs)

    def body(*refs):
        for step in phases(refs[:n], refs[n:2 * n], *refs[2 * n:]):
            step()

    return pl.pallas_call(body, name=name, in_specs=[ANY] * n, out_specs=[ANY] * n,
                          out_shape=out_shape, scratch_shapes=scratch)(*arrs)


def _carry_exchange(exchange, refs, n_in, n_out, first, middle, last):
    arrs, _, _, phases = exchange
    n = len(arrs)
    if n == 0:
        return lambda: None
    ins = refs[n_in:n_in + n]
    outs = refs[n_in + n + n_out:n_in + 2 * n + n_out]
    sems = n_in + 2 * n + n_out
    steps = phases(ins, outs, *refs[sems:sems + 3])
    pl.when(first)(steps[0])
    if len(steps) == 3:
        pl.when(middle)(steps[1])
    return lambda: pl.when(last)(steps[-1])


def _adamw_math(w, g, m, v):
    m2 = ADAM_B1 * m + (1.0 - ADAM_B1) * g
    v2 = ADAM_B2 * v + (1.0 - ADAM_B2) * jnp.square(g)
    m_hat = m2 / (1.0 - ADAM_B1 ** ADAM_STEP)
    v_hat = v2 / (1.0 - ADAM_B2 ** ADAM_STEP)
    delta = -ADAM_LR * (m_hat / (jnp.sqrt(v_hat) + ADAM_EPS) + ADAM_WD * w)
    return delta, m2, v2


def _sum_adamw_tile(parts, w, m, v):
    g = parts[0].astype(F32)
    for d in range(1, parts.shape[0]):
        g = g + parts[d].astype(F32)
    return (g, *_adamw_math(w, g, m, v))


def _sum_adamw(name, parts, w, m, v, tr=256):
    p, r, c = parts.shape
    tr = _pick(r, tr, 16)

    def body(p_ref, w_ref, m_ref, v_ref, g_ref, d_ref, m2_ref, v2_ref):
        g_ref[...], d_ref[...], m2_ref[...], v2_ref[...] = _sum_adamw_tile(p_ref[...], w_ref[...], m_ref[...], v_ref[...])

    tile = pl.BlockSpec((tr, c), lambda i: (i, 0))
    return pl.pallas_call(
        body, name=name, grid=(r // tr,),
        in_specs=[pl.BlockSpec((p, tr, c), lambda i: (0, i, 0)), tile, tile, tile],
        out_specs=[tile] * 4, out_shape=[_sds((r, c))] * 4,
        compiler_params=_params(("parallel",)),
    )(parts, w, m, v)


def _sum_parts(name, parts):
    p, r, c = parts.shape

    def body(p_ref, o_ref):
        g = p_ref[0]
        for d in range(1, p):
            g = g + p_ref[d]
        o_ref[...] = g

    return pl.pallas_call(
        body, name=name, out_shape=_sds((r, c)),
        in_specs=[pl.BlockSpec(memory_space=pltpu.VMEM)], out_specs=pl.BlockSpec(memory_space=pltpu.VMEM),
    )(parts)


WEIGHTS = ["norm_mix_pre", "w_in", "conv_dw_w", "conv_dw_b", "conv_ln_g", "conv_ln_b", "w_conv_branch",
           "b_conv_branch", "w_att_branch", "w_out", "norm_mix_post", "norm_ffn_pre", "w_ffn_up", "w_ffn_down",
           "norm_ffn_post"]
COL_SHARDED = ["w_conv_branch", "w_att_branch"]
ROW_SHARDED = ["w_out", "w_ffn_down"]
TRANSPOSED = ["w_in", "w_ffn_up"]
VECTORS = ["norm_mix_pre", "conv_dw_b", "conv_ln_g", "conv_ln_b", "b_conv_branch", "norm_mix_post",
           "norm_ffn_pre", "norm_ffn_post"]


def _cols_to_full(g):
    return g.transpose(1, 0, 2).reshape(g.shape[1], N_DEV * g.shape[2])


def _full_to_cols(f):
    return f.reshape(f.shape[0], N_DEV, f.shape[1] // N_DEV).transpose(1, 0, 2)


def _pack_vectors(vecs):
    rows = [jnp.pad(vecs[nm].reshape(-1), (0, D_MODEL - vecs[nm].size)) for nm in VECTORS]
    return jnp.stack(rows)


def _unpack_vectors(packed, sizes):
    return {nm: packed[n, :sizes[nm]] for n, nm in enumerate(VECTORS)}


def kernel(x, norm_mix_pre, w_in, conv_dw_w, conv_dw_b, conv_ln_g, conv_ln_b, w_conv_branch, b_conv_branch, w_att_branch, w_out, norm_mix_post, norm_ffn_pre, w_ffn_up, w_ffn_down, norm_ffn_post, loss_target, m_norm_mix_pre, m_w_in, m_conv_dw_w, m_conv_dw_b, m_conv_ln_g, m_conv_ln_b, m_w_conv_branch, m_b_conv_branch, m_w_att_branch, m_w_out, m_norm_mix_post, m_norm_ffn_pre, m_w_ffn_up, m_w_ffn_down, m_norm_ffn_post, v_norm_mix_pre, v_w_in, v_conv_dw_w, v_conv_dw_b, v_conv_ln_g, v_conv_ln_b, v_w_conv_branch, v_b_conv_branch, v_w_att_branch, v_w_out, v_norm_mix_post, v_norm_ffn_pre, v_w_ffn_up, v_w_ffn_down, v_norm_ffn_post):
    ws = dict(zip(WEIGHTS, [norm_mix_pre, w_in, conv_dw_w, conv_dw_b, conv_ln_g, conv_ln_b, w_conv_branch,
                            b_conv_branch, w_att_branch, w_out, norm_mix_post, norm_ffn_pre, w_ffn_up, w_ffn_down,
                            norm_ffn_post]))
    ms = dict(zip(WEIGHTS, [m_norm_mix_pre, m_w_in, m_conv_dw_w, m_conv_dw_b, m_conv_ln_g, m_conv_ln_b,
                            m_w_conv_branch, m_b_conv_branch, m_w_att_branch, m_w_out, m_norm_mix_post,
                            m_norm_ffn_pre, m_w_ffn_up, m_w_ffn_down, m_norm_ffn_post]))
    vs = dict(zip(WEIGHTS, [v_norm_mix_pre, v_w_in, v_conv_dw_w, v_conv_dw_b, v_conv_ln_g, v_conv_ln_b,
                            v_w_conv_branch, v_b_conv_branch, v_w_att_branch, v_w_out, v_norm_mix_post,
                            v_norm_ffn_pre, v_w_ffn_up, v_w_ffn_down, v_norm_ffn_post]))

    dw_block = jnp.pad(conv_dw_w, ((0, 1), (0, 0)))
    g_in, g_dw = _exchange_call("gather_first", _gather_exchange([w_in.T.astype(BF16), dw_block]))
    full = {"w_in": _full_weight("w_in", g_in), "conv_dw_w": _cols_to_full(g_dw)}
    for nm in VECTORS:
        full[nm] = ws[nm].reshape(1, -1)

    def as_kept(nm, a):
        return a.T if nm in TRANSPOSED else a

    ride_along = ["w_ffn_up", "w_out"]
    loss_local, grad_x, received, updated, grads = _local_step(
        x[0], loss_target[0], full, {nm: as_kept(nm, ws[nm]).astype(BF16) for nm in LATE},
        {nm: tuple(as_kept(nm, a[nm]) for a in (ws, ms, vs)) for nm in ride_along})

    loss_at = (VECTORS.index("conv_dw_b"), CONV_DIM)
    small = _exchange_call("gather_small_grads", _gather_exchange(
        [_pack_vectors(grads).at[loss_at].set(loss_local), grads["conv_dw_w"]]))
    out_g, out_d, out_m, out_v = {}, {}, {}, {}
    for nm in LATE + ["w_in"]:
        res = updated[nm] if nm in updated else _sum_adamw(
            "adamw_" + nm, received[nm], *[as_kept(nm, a[nm]) for a in (ws, ms, vs)])
        out_g[nm], out_d[nm], out_m[nm], out_v[nm] = [as_kept(nm, r) for r in res]
    sizes = {nm: ws[nm].size for nm in VECTORS}
    vec = _sum_adamw("adamw_vectors", small[0], _pack_vectors(ws), _pack_vectors(ms), _pack_vectors(vs))
    for res, dst in zip(vec, (out_g, out_d, out_m, out_v)):
        dst.update(_unpack_vectors(res, sizes))
    loss = vec[0][loss_at]
    dw_full = _sum_parts("sum_dw_grads", small[1])
    me = _slot(*_place())
    dw_mine = lax.dynamic_slice(dw_full, (0, me * (CONV_DIM // N_DEV)), (CONV_WIDTH, CONV_DIM // N_DEV))
    nm = "conv_dw_w"
    out_g[nm], out_d[nm], out_m[nm], out_v[nm] = _sum_adamw("adamw_dw", dw_mine[None], ws[nm], ms[nm], vs[nm])

    outs = [loss, grad_x[None]]
    for group in (out_g, out_d, out_m, out_v):
        outs += [group[nm] for nm in WEIGHTS]
    return tuple(outs)
```

```python
import math

import jax
import jax.numpy as jnp
from jax import lax
from jax.experimental import pallas as pl
from jax.experimental.pallas import tpu as pltpu

F32 = jnp.float32
BF16 = jnp.bfloat16

N_DEV = 8
D_MODEL = 1024
CONV_DIM = 512
CONV_WIDTH = 31
N_HEADS = 8
HEAD_DIM = 64
ATT_DIM = N_HEADS * HEAD_DIM
D_FF = 2816
EPS = 1e-6
IN_SPLITS = (0, 1024, 1536, 2048, 2560, 3584, 4608)

ADAM_LR = 0.001
ADAM_B1 = 0.9
ADAM_B2 = 0.999
ADAM_EPS = 1e-08
ADAM_WD = 0.01
ADAM_STEP = 10

LANES = 128
SUBLANES = 8
HALO = 32
ATT_TILE = 256
ATT_PART = 192
DEAD_SUM = -120.0
VMEM_LIMIT = 56 * 1024 * 1024
MESH = pl.DeviceIdType.MESH
ANY = pl.BlockSpec(memory_space=pl.ANY)


def _pick(dim, target, align=LANES):
    t = min(dim, target)
    t -= t % align
    while t >= align:
        if dim % t == 0:
            return t
        t -= align
    return dim


def _params(semantics):
    return pltpu.CompilerParams(dimension_semantics=semantics, vmem_limit_bytes=VMEM_LIMIT)


def _matmul(a, b, *, name, ta=False, tb=False, out_dtype=F32):
    m, k = (a.shape[1], a.shape[0]) if ta else a.shape
    n, k2 = b.shape if tb else (b.shape[1], b.shape[0])
    assert k == k2, (a.shape, b.shape, ta, tb)
    tm, tn, tk = _pick(m, 1408 if ta else 512), _pick(n, 1536), _pick(k, 1536)
    nk = k // tk
    dims = (((0 if ta else 1,), (1 if tb else 0,)), ((), ()))

    def body(a_ref, b_ref, o_ref, *acc):
        part = lax.dot_general(a_ref[...], b_ref[...], dims, preferred_element_type=F32)
        if nk == 1:
            o_ref[...] = part.astype(o_ref.dtype)
            return
        acc_ref, = acc
        kk = pl.program_id(2)

        @pl.when(kk == 0)
        def _():
            acc_ref[...] = part

        @pl.when((kk > 0) & (kk < nk - 1))
        def _():
            acc_ref[...] += part

        @pl.when(kk == nk - 1)
        def _():
            o_ref[...] = (acc_ref[...] + part).astype(o_ref.dtype)

    a_spec = pl.BlockSpec((tk, tm), lambda j, i, kk: (kk, i)) if ta else pl.BlockSpec((tm, tk), lambda j, i, kk: (i, kk))
    b_spec = (pl.BlockSpec((tn, tk), lambda j, i, kk: (j, kk)) if tb
              else pl.BlockSpec((tk, tn), lambda j, i, kk: (kk, j)))
    return pl.pallas_call(
        body, name=name, grid=(n // tn, m // tm, nk),
        in_specs=[a_spec, b_spec],
        out_specs=pl.BlockSpec((tm, tn), lambda j, i, kk: (i, j)),
        out_shape=jax.ShapeDtypeStruct((m, n), out_dtype),
        scratch_shapes=[pltpu.VMEM((tm, tn), F32)] if nk > 1 else [],
        compiler_params=_params(("parallel", "parallel", "arbitrary")),
    )(a, b)


NO_EXCHANGE = ([], [], [], None)


def _sweep_marks(nt):
    i = pl.program_id(0)
    return i == 0, i == (3 * nt) // 4, i == nt - 1


def _rowwise(name, fn, rows, bcasts, row_outs, red_outs=(), tm=256, exchange=NO_EXCHANGE):
    s = rows[0].shape[0]
    tm = _pick(s, tm, 16)
    nt = s // tm
    resident = pl.Buffered(1)
    nr, nb, no, nd = len(rows), len(bcasts), len(row_outs), len(red_outs)
    x_arrs, x_shape, x_scratch, _ = exchange
    nx = len(x_arrs)
    first_out = nr + nb + nx

    def body(*refs):
        finish_exchange = _carry_exchange(exchange, refs, nr + nb, no + nd, *_sweep_marks(nt))
        ins = [r[...] for r in refs[:nr + nb]]
        outs, reds = fn(*ins)
        for ref, val in zip(refs[first_out:first_out + no], outs):
            ref[...] = val.astype(ref.dtype)
        i = pl.program_id(0)
        for ref, val in zip(refs[first_out + no:first_out + no + nd], reds):
            @pl.when(i == 0)
            def _():
                ref[...] = val

            @pl.when(i > 0)
            def _():
                ref[...] += val
        finish_exchange()

    def row_spec(a):
        assert a.shape[-2] % nt == 0, (name, a.shape, nt)
        if len(a.shape) == 3:
            return pl.BlockSpec((a.shape[0], a.shape[1] // nt, a.shape[2]), lambda i: (0, i, 0))
        return pl.BlockSpec((a.shape[0] // nt, a.shape[1]), lambda i: (i, 0))

    in_specs = [row_spec(r) for r in rows]
    in_specs += [pl.BlockSpec(b.shape, lambda i: (0, 0), pipeline_mode=resident) for b in bcasts]
    out_specs = [row_spec(o) for o in row_outs]
    out_specs += [pl.BlockSpec(d.shape, lambda i: (0, 0)) for d in red_outs]
    return pl.pallas_call(
        body, name=name, grid=(nt,), in_specs=in_specs + [ANY] * nx, out_specs=out_specs + [ANY] * nx,
        out_shape=list(row_outs) + list(red_outs) + x_shape, scratch_shapes=x_scratch,
        compiler_params=_params(("arbitrary",)),
    )(*rows, *bcasts, *x_arrs)


def _sds(shape, dtype=F32):
    return jax.ShapeDtypeStruct(shape, dtype)


def _rms(x, g):
    y = x * lax.rsqrt(jnp.mean(x * x, axis=-1, keepdims=True) + EPS)
    return y * g


def _silu(x):
    return x * jax.nn.sigmoid(x)


def _swiglu(g, u):
    return _silu(g) * u


def _ln_silu(u, g, b):
    mu = jnp.mean(u, axis=-1, keepdims=True)
    var = jnp.mean(jnp.square(u - mu), axis=-1, keepdims=True)
    return _silu((u - mu) * lax.rsqrt(var + EPS) * g + b)


def _merge(conv_pre, att_out, g_conv, g_att, b_cb):
    return jax.nn.sigmoid(g_conv) * (conv_pre + b_cb) + jax.nn.sigmoid(g_att) * att_out


def _glu(t):
    return t[:, :CONV_DIM] * jax.nn.sigmoid(t[:, CONV_DIM:])


def _shifted_reader(buf, shifted, tm):
    for b in range(1, SUBLANES):
        shifted[b - 1, :, :] = buf[pl.ds(b, tm + HALO - SUBLANES), :]

    def read(o):
        a, b = divmod(o, SUBLANES)
        return buf[pl.ds(SUBLANES * a, tm), :] if b == 0 else shifted[b - 1, pl.ds(SUBLANES * a, tm), :]

    return read


def _conv_fwd(conv_in, w_pad, b, ln_g, ln_b, exchange, tm=256):
    s = conv_in.shape[0]
    tm = _pick(s, tm, HALO)
    ratio = tm // HALO
    x_arrs, x_shape, x_scratch, _ = exchange
    nx = len(x_arrs)

    def body(*refs):
        main_ref, halo_ref, w_ref, b_ref, g_ref, be_ref = refs[:6]
        u3_ref, u1_ref = refs[6 + nx:8 + nx]
        buf, shifted = refs[-2:]
        finish_exchange = _carry_exchange(exchange, refs, 6, 2, *_sweep_marks(s // tm))
        i = pl.program_id(0)
        buf[0:HALO, :] = _glu(halo_ref[...]) * (i > 0).astype(F32)
        buf[HALO:HALO + tm, :] = _glu(main_ref[...])
        read = _shifted_reader(buf, shifted, tm)
        acc = jnp.zeros((tm, CONV_DIM), F32) + b_ref[...]
        for j in range(CONV_WIDTH):
            acc = acc + w_ref[j:j + 1, :] * read(HALO - (CONV_WIDTH - 1) + j)
        u1_ref[...] = acc
        u3_ref[...] = _ln_silu(acc, g_ref[...], be_ref[...]).astype(u3_ref.dtype)
        finish_exchange()

    res = pl.pallas_call(
        body, name="conv_fwd", grid=(s // tm,),
        in_specs=[pl.BlockSpec((tm, 2 * CONV_DIM), lambda i: (i, 0)),
                  pl.BlockSpec((HALO, 2 * CONV_DIM), lambda i: (jnp.maximum(i * ratio - 1, 0), 0)),
                  pl.BlockSpec(w_pad.shape, lambda i: (0, 0)),
                  pl.BlockSpec(b.shape, lambda i: (0, 0)),
                  pl.BlockSpec(ln_g.shape, lambda i: (0, 0)),
                  pl.BlockSpec(ln_b.shape, lambda i: (0, 0))] + [ANY] * nx,
        out_specs=[pl.BlockSpec((tm, CONV_DIM), lambda i: (i, 0)),
                   pl.BlockSpec((tm, CONV_DIM), lambda i: (i, 0))] + [ANY] * nx,
        out_shape=[_sds((s, CONV_DIM), BF16), _sds((s, CONV_DIM), F32)] + x_shape,
        scratch_shapes=x_scratch + [pltpu.VMEM((tm + HALO, CONV_DIM), F32),
                                    pltpu.VMEM((SUBLANES - 1, tm + HALO - SUBLANES, CONV_DIM), F32)],
        compiler_params=_params(("arbitrary",)),
    )(conv_in, conv_in, w_pad, b, ln_g, ln_b, *x_arrs)
    return res[0], res[1], res[2:]


def _conv_bwd(conv_in, u1, du3, ln_g, ln_b, w_pad, exchange, tm=256):
    s = conv_in.shape[0]
    tm = _pick(s, tm, HALO)
    ratio = tm // HALO
    nt = s // tm
    last_halo = s // HALO - 1
    x_arrs, x_shape, x_scratch, _ = exchange
    nx = len(x_arrs)

    def body(*refs):
        main_ref, halo_ref, u1_ref, u1n_ref, du3_ref, du3n_ref, g_ref, be_ref, w_ref = refs[:9]
        dci_ref, dw_ref, db_ref, dg_ref, dbe_ref = refs[9 + nx:14 + nx]
        ubuf, dbuf, ushift, dshift = refs[-4:]
        finish_exchange = _carry_exchange(exchange, refs, 9, 5, *_sweep_marks(nt))
        i = pl.program_id(0)
        main = main_ref[...]
        a = main[:, :CONV_DIM]
        sb = jax.nn.sigmoid(main[:, CONV_DIM:])
        ubuf[0:HALO, :] = _glu(halo_ref[...]) * (i > 0).astype(F32)
        ubuf[HALO:HALO + tm, :] = a * sb

        def ln_bwd(u1t, du3t):
            _, vjp = jax.vjp(_ln_silu, u1t, g_ref[...], be_ref[...])
            return vjp(du3t)

        du, dg, dbe = ln_bwd(u1_ref[...], du3_ref[...])
        dbuf[0:tm, :] = du
        dbuf[tm:tm + HALO, :] = ln_bwd(u1n_ref[...], du3n_ref[...])[0] * (i < nt - 1).astype(F32)

        @pl.when(i == 0)
        def _():
            dw_ref[...] = jnp.zeros_like(dw_ref)
            db_ref[...] = jnp.zeros_like(db_ref)
            dg_ref[...] = jnp.zeros_like(dg_ref)
            dbe_ref[...] = jnp.zeros_like(dbe_ref)

        dg_ref[...] += dg
        dbe_ref[...] += dbe

        read_u = _shifted_reader(ubuf, ushift, tm)
        read_d = _shifted_reader(dbuf, dshift, tm)
        du0 = jnp.zeros((tm, CONV_DIM), F32)
        for j in range(CONV_WIDTH):
            du0 = du0 + w_ref[j:j + 1, :] * read_d(CONV_WIDTH - 1 - j)
            dw_ref[j:j + 1, :] += jnp.sum(du * read_u(HALO - (CONV_WIDTH - 1) + j), axis=0, keepdims=True)
        db_ref[...] += jnp.sum(du, axis=0, keepdims=True)
        dci_ref[:, :CONV_DIM] = (du0 * sb).astype(dci_ref.dtype)
        dci_ref[:, CONV_DIM:] = (du0 * a * sb * (1.0 - sb)).astype(dci_ref.dtype)
        finish_exchange()

    res = pl.pallas_call(
        body, name="conv_bwd", grid=(nt,),
        in_specs=[pl.BlockSpec((tm, 2 * CONV_DIM), lambda i: (i, 0)),
                  pl.BlockSpec((HALO, 2 * CONV_DIM), lambda i: (jnp.maximum(i * ratio - 1, 0), 0))]
        + [pl.BlockSpec((tm, CONV_DIM), lambda i: (i, 0)),
           pl.BlockSpec((HALO, CONV_DIM), lambda i: (jnp.minimum((i + 1) * ratio, last_halo), 0))] * 2
        + [pl.BlockSpec((1, CONV_DIM), lambda i: (0, 0))] * 2 + [pl.BlockSpec(w_pad.shape, lambda i: (0, 0))]
        + [ANY] * nx,
        out_specs=[pl.BlockSpec((tm, 2 * CONV_DIM), lambda i: (i, 0)),
                   pl.BlockSpec(w_pad.shape, lambda i: (0, 0))]
        + [pl.BlockSpec((1, CONV_DIM), lambda i: (0, 0))] * 3 + [ANY] * nx,
        out_shape=[_sds((s, 2 * CONV_DIM), BF16), _sds(w_pad.shape)] + [_sds((1, CONV_DIM))] * 3 + x_shape,
        scratch_shapes=x_scratch + [pltpu.VMEM((tm + HALO, CONV_DIM), F32)] * 2
        + [pltpu.VMEM((SUBLANES - 1, tm + HALO - SUBLANES, CONV_DIM), F32)] * 2,
        compiler_params=_params(("arbitrary",)),
    )(conv_in, conv_in, u1, u1, du3, du3, ln_g, ln_b, w_pad, *x_arrs)
    return res[:5], res[5:]


def _logsig_neg(z):
    return jnp.minimum(-z, 0.0) - jnp.log(1.0 + jnp.exp(-jnp.abs(z)))


def _split_dot(val, tri):
    hi = val.astype(BF16)
    lo = (val - hi.astype(F32)).astype(BF16)
    return jnp.dot(hi, tri, preferred_element_type=F32) + jnp.dot(lo, tri, preferred_element_type=F32)


def _attn_masks(t, later):
    row = lax.broadcasted_iota(jnp.int32, (t, t), 0)
    col = lax.broadcasted_iota(jnp.int32, (t, t), 1)
    tri = jnp.where(row > col if later else row <= col, 1.0, 0.0).astype(BF16)
    return col < row, tri


def _grid_marks(h, nq):
    hh, i = pl.program_id(0), pl.program_id(1)
    return (hh == 0) & (i == 0), (hh == (3 * h) // 4) & (i == 0), (hh == h - 1) & (i == nq - 1)


def _head_masks(shape):
    lane = lax.broadcasted_iota(jnp.int32, shape, len(shape) - 1)
    return lane < HEAD_DIM, lane >= HEAD_DIM


def _per_head(blk):
    m0, m1 = _head_masks(blk.shape)
    zero = jnp.zeros_like(blk)
    return jnp.where(m0, blk, zero), jnp.where(m1, blk, zero)


NT = (((1,), (1,)), ((), ()))
TN = (((0,), (0,)), ((), ()))


def _with_top(whole, top):
    rows = top.shape[0]
    return top if rows == whole.shape[0] else jnp.concatenate([top, whole[rows:]], axis=0)


def _attn_fwd(q, k, v, exchange):
    s = q.shape[0]
    hp = q.shape[1] // LANES
    t = ATT_TILE
    scale = 1.0 / math.sqrt(HEAD_DIM)
    x_arrs, x_shape, x_scratch, _ = exchange
    nx = len(x_arrs)

    def body(*refs):
        q_ref, k_ref, v_ref = refs[:3]
        o_ref, lt_ref, nb_ref = refs[3 + nx:6 + nx]
        finish_exchange = _carry_exchange(exchange, refs, 3, 3, *_grid_marks(hp, s // t))
        i = pl.program_id(1)
        qs = _per_head((q_ref[...].astype(F32) * scale).astype(BF16))
        causal, tri = _attn_masks(t, later=True)

        def step(kb, carry, masked, rows):
            cs, acc = carry
            off = pl.multiple_of(kb * t, t)
            kblk = k_ref[pl.ds(off, t), :]
            vs = _per_head(v_ref[pl.ds(off, t), :])
            acc_top = acc[:rows]
            new_cs = []
            for hd in range(2):
                z = lax.dot_general(qs[hd][:rows], kblk, NT, preferred_element_type=F32)
                l = _logsig_neg(z)
                if masked:
                    l = jnp.where(causal, l, 0.0)
                e = z + l + _split_dot(l, tri) + cs[hd][:rows]
                if masked:
                    e = jnp.where(causal, e, -1e30)
                acc_top = acc_top + jnp.dot(jnp.exp(e).astype(BF16), vs[hd], preferred_element_type=F32)
                new_cs.append(_with_top(cs[hd], cs[hd][:rows] + jnp.sum(l, axis=1, keepdims=True)))
            return tuple(new_cs), _with_top(acc, acc_top)

        zero = jnp.zeros((t, 1), F32)
        carry = step(i, ((zero, zero), jnp.zeros((t, LANES), F32)), True, t)

        def live(cs, lo, hi):
            return jnp.maximum(jnp.max(cs[0][lo:hi]), jnp.max(cs[1][lo:hi])) > DEAD_SUM

        def more(state):
            n, _, (cs, _) = state
            return (n < i) & live(cs, 0, t)

        def sweep(state):
            n, n_full, cr = state
            whole = live(cr[0], ATT_PART, t)
            cr = lax.cond(whole, lambda c: step(i - 1 - n, c, False, t), lambda c: step(i - 1 - n, c, False, ATT_PART), cr)
            return n + 1, n_full + whole.astype(jnp.int32), cr

        n_blocks, n_full, carry = lax.while_loop(more, sweep, (jnp.int32(0), jnp.int32(0), carry))
        m0, _ = _head_masks((t, LANES))
        lt_ref[...] = jnp.where(m0, carry[0][0], carry[0][1])
        o_ref[...] = carry[1].astype(o_ref.dtype)
        nb_ref[0, pl.program_id(0), i] = n_blocks.astype(F32)
        nb_ref[1, pl.program_id(0), i] = n_full.astype(F32)
        finish_exchange()

    res = pl.pallas_call(
        body, name="attn_fwd", grid=(hp, s // t),
        in_specs=[pl.BlockSpec((t, LANES), lambda p, i: (i, p)),
                  pl.BlockSpec((s, LANES), lambda p, i: (0, p)),
                  pl.BlockSpec((s, LANES), lambda p, i: (0, p))] + [ANY] * nx,
        out_specs=[pl.BlockSpec((t, LANES), lambda p, i: (i, p)),
                   pl.BlockSpec((None, t, LANES), lambda p, i: (p, i, 0)),
                   pl.BlockSpec(memory_space=pltpu.SMEM)] + [ANY] * nx,
        out_shape=[_sds(q.shape, BF16), _sds((hp, s, LANES), F32), _sds((2, hp, s // t), F32)] + x_shape,
        scratch_shapes=x_scratch,
        compiler_params=_params(("arbitrary", "arbitrary")),
    )(q, k, v, *x_arrs)
    return res[0], res[1], res[2], res[3:]


def _attn_bwd(q, k, v, do, ltot, n_blocks, exchange):
    s = q.shape[0]
    hp = q.shape[1] // LANES
    t = ATT_TILE
    scale = 1.0 / math.sqrt(HEAD_DIM)
    x_arrs, x_shape, x_scratch, _ = exchange
    nx = len(x_arrs)

    def body(*refs):
        q_ref, k_ref, v_ref, do_ref, lt_ref, nb_ref = refs[:6]
        dq_ref, dk_ref, dv_ref = refs[6 + nx:9 + nx]
        finish_exchange = _carry_exchange(exchange, refs, 6, 3, *_grid_marks(hp, s // t))
        i = pl.program_id(1)
        n_blocks = jnp.clip(nb_ref[0, pl.program_id(0), i].astype(jnp.int32), 0, i)
        n_full = jnp.clip(nb_ref[1, pl.program_id(0), i].astype(jnp.int32), 0, n_blocks)

        @pl.when(i == 0)
        def _():
            dk_ref[...] = jnp.zeros_like(dk_ref)
            dv_ref[...] = jnp.zeros_like(dv_ref)

        qb = q_ref[...]
        qm = _per_head(qb)
        qs = _per_head((qb.astype(F32) * scale).astype(BF16))
        dos = _per_head(do_ref[...])
        lts = (lt_ref[:, 0:1], lt_ref[:, HEAD_DIM:HEAD_DIM + 1])
        causal, tri = _attn_masks(t, later=False)

        def step(kb, carry, masked, rows):
            cls, cgs, dq = carry
            off = pl.multiple_of(kb * t, t)
            kblk = k_ref[pl.ds(off, t), :]
            vblk = v_ref[pl.ds(off, t), :]
            ks = _per_head(kblk)
            dq_top = dq[:rows]
            dk = jnp.zeros((t, LANES), F32)
            dv = jnp.zeros((t, LANES), F32)
            new_cls, new_cgs = [], []
            for hd in range(2):
                z = lax.dot_general(qs[hd][:rows], kblk, NT, preferred_element_type=F32)
                l = _logsig_neg(z)
                if masked:
                    l = jnp.where(causal, l, 0.0)
                e = z + l + ((lts[hd][:rows] - cls[hd][:rows]) - _split_dot(l, tri))
                if masked:
                    e = jnp.where(causal, e, -1e30)
                a = jnp.exp(e)
                g = lax.dot_general(dos[hd][:rows], vblk, NT, preferred_element_type=F32) * a
                p = cgs[hd][:rows] + jnp.dot(g.astype(BF16), tri, preferred_element_type=F32) - g
                el = jnp.exp(l)
                dz = g * el - p * (1.0 - el)
                if masked:
                    dz = jnp.where(causal, dz, 0.0)
                dzb = (dz * scale).astype(BF16)
                dq_top = dq_top + jnp.dot(dzb, ks[hd], preferred_element_type=F32)
                dk = dk + lax.dot_general(dzb, qm[hd][:rows], TN, preferred_element_type=F32)
                dv = dv + lax.dot_general(a.astype(BF16), dos[hd][:rows], TN, preferred_element_type=F32)
                new_cls.append(_with_top(cls[hd], cls[hd][:rows] + jnp.sum(l, axis=1, keepdims=True)))
                new_cgs.append(_with_top(cgs[hd], cgs[hd][:rows] + jnp.sum(g, axis=1, keepdims=True)))
            dk_ref[pl.ds(off, t), :] += dk
            dv_ref[pl.ds(off, t), :] += dv
            return tuple(new_cls), tuple(new_cgs), _with_top(dq, dq_top)

        zero = jnp.zeros((t, 1), F32)
        init = ((zero, zero), (zero, zero), jnp.zeros((t, LANES), F32))
        carry = lax.fori_loop(i - n_blocks, i - n_full, lambda kb, cr: step(kb, cr, False, ATT_PART), init)
        carry = lax.fori_loop(i - n_full, i, lambda kb, cr: step(kb, cr, False, t), carry)
        carry = step(i, carry, True, t)
        dq_ref[...] = carry[2]
        finish_exchange()

    blk = pl.BlockSpec((t, LANES), lambda p, i: (i, p))
    whole = pl.BlockSpec((s, LANES), lambda p, i: (0, p))
    res = pl.pallas_call(
        body, name="attn_bwd", grid=(hp, s // t),
        in_specs=[blk, whole, whole, blk, pl.BlockSpec((None, t, LANES), lambda p, i: (p, i, 0)),
                  pl.BlockSpec(memory_space=pltpu.SMEM)] + [ANY] * nx,
        out_specs=[blk, whole, whole] + [ANY] * nx,
        out_shape=[_sds(q.shape)] * 3 + x_shape,
        scratch_shapes=x_scratch,
        compiler_params=_params(("arbitrary", "arbitrary")),
    )(q, k, v, do, ltot, n_blocks, *x_arrs)
    return res[0], res[1], res[2], res[3:]


LATE = ["w_conv_branch", "w_att_branch", "w_out", "w_ffn_up", "w_ffn_down"]


def _full_weight(name, gathered):
    return _cols_to_full(gathered) if name in COL_SHARDED else gathered.reshape(-1, gathered.shape[2])


def _grad_slabs(name, grad):
    return _full_to_cols(grad) if name in COL_SHARDED else grad.reshape(N_DEV, -1, grad.shape[1])


def _side_slabs(name, grad):
    slabs = _grad_slabs(name, grad)
    return slabs.reshape((4, 2) + slabs.shape[1:])


def _local_step(x, target, w, late_blocks, opt):
    s = x.shape[0]
    w = dict(w)
    g1, g2, g3, g4 = w["norm_mix_pre"], w["norm_mix_post"], w["norm_ffn_pre"], w["norm_ffn_post"]

    w_in = w["w_in"]

    def proj_fn(xt, g1_, w_in_t):
        h = _rms(xt, g1_).astype(BF16)
        proj = lax.dot_general(h, w_in_t, NT, preferred_element_type=F32)
        return (h, *[proj[:, IN_SPLITS[n]:IN_SPLITS[n + 1]] for n in range(6)]), ()

    mix_weights = ["w_conv_branch", "w_att_branch", "w_out"]
    h1, conv_in, q, k, v, g_conv, g_att = _rowwise(
        "norm_proj", proj_fn, [x], [g1, w_in],
        [_sds((s, D_MODEL), BF16), _sds((s, 2 * CONV_DIM)), _sds((s, ATT_DIM), BF16), _sds((s, ATT_DIM), BF16),
         _sds((s, ATT_DIM), BF16), _sds((s, D_MODEL)), _sds((s, D_MODEL))], tm=512)

    u3, u1, gathered = _conv_fwd(conv_in, w["conv_dw_w"], w["conv_dw_b"], w["conv_ln_g"], w["conv_ln_b"],
                                 _gather_exchange([late_blocks[nm] for nm in mix_weights]))
    for nm, g in zip(mix_weights, gathered):
        w[nm] = _full_weight(nm, g)
    att, ltot, n_blocks, (g_up,) = _attn_fwd(q, k, v, _gather_exchange([late_blocks["w_ffn_up"]]))
    w["w_ffn_up"] = _full_weight("w_ffn_up", g_up)

    def merge_fn(u3t, at, gc, ga, xt, w_cb, w_ab, b_cb, w_out, g2_, g3_):
        cp = jnp.dot(u3t, w_cb, preferred_element_type=F32)
        ao = jnp.dot(at, w_ab, preferred_element_type=F32)
        mg = _merge(cp, ao, gc, ga, b_cb).astype(BF16)
        mix_ = jnp.dot(mg, w_out, preferred_element_type=F32)
        x2_ = xt + _rms(mix_, g2_)
        return (mg, cp, ao, mix_, x2_, _rms(x2_, g3_)), ()

    merged, conv_pre, att_out, mix, x2, h2 = _rowwise(
        "branch_merge_mix", merge_fn, [u3, att, g_conv, g_att, x],
        [w["w_conv_branch"], w["w_att_branch"], w["b_conv_branch"], w["w_out"], g2, g3],
        [_sds((s, D_MODEL), BF16)] * 3 + [_sds((s, D_MODEL)), _sds((s, D_MODEL)), _sds((s, D_MODEL), BF16)], tm=512)

    def ffn_up_fn(ht, w_up_t):
        gu_ = lax.dot_general(ht, w_up_t, NT, preferred_element_type=F32)
        return (gu_, _swiglu(gu_[:, :D_FF], gu_[:, D_FF:])), ()

    gu, act, g_down = _rowwise("ffn_up", ffn_up_fn, [h2], [w["w_ffn_up"]],
                               [_sds((s, 2 * D_FF), BF16), _sds((s, D_FF), BF16)], tm=512,
                               exchange=_gather_exchange([late_blocks["w_ffn_down"]]))
    w["w_ffn_down"] = _full_weight("w_ffn_down", g_down)

    def final_fn(at, x2t, tgt, w_down, g4_):
        ff = jnp.dot(at, w_down, preferred_element_type=F32)
        n4, vjp = jax.vjp(_rms, ff, g4_)
        err = x2t + n4 - tgt
        dy = err * (1.0 / D_MODEL)
        dff, dg4 = vjp(dy)
        return (dy, dff), (jnp.sum(err * err, axis=0, keepdims=True), dg4)

    dy, dff, loss_cols, d_g4 = _rowwise("ffn_down_loss", final_fn, [act, x2, target], [w["w_ffn_down"], g4],
                                        [_sds((s, D_MODEL)), _sds((s, D_MODEL), BF16)],
                                        [_sds((1, D_MODEL)), _sds((1, D_MODEL))], tm=512)
    loss = 0.5 * jnp.sum(loss_cols) / D_MODEL

    d_w_down = _matmul(act, dff, ta=True, name="d_w_down", out_dtype=BF16)

    def act_bwd_fn(dfft, gut, w_down):
        d_act = lax.dot_general(dfft, w_down, NT, preferred_element_type=F32)
        gu_ = gut.astype(F32)
        _, vjp = jax.vjp(_swiglu, gu_[:, :D_FF], gu_[:, D_FF:])
        return (jnp.concatenate(vjp(d_act), axis=1),), ()

    down_slabs = _side_slabs("w_ffn_down", d_w_down)
    dgu, theirs = _rowwise("ffn_act_bwd", act_bwd_fn, [dff, gu], [w["w_ffn_down"]], [_sds((s, 2 * D_FF), BF16)],
                           exchange=_pair_exchange([down_slabs]))
    down_sums = _pair_sum("pair_sum_w_ffn_down", down_slabs, theirs)
    d_w_up = _matmul(dgu, h2, ta=True, name="d_w_up", out_dtype=BF16)
    received = {}
    up_slabs = _side_slabs("w_ffn_up", d_w_up)

    def mid_bwd_fn(dgut, xt, mt, dyt, w_up_t, g2_, g3_):
        dh = jnp.dot(dgut, w_up_t, preferred_element_type=F32)
        n2, vjp2 = jax.vjp(_rms, mt, g2_)
        x2_ = xt + n2
        _, vjp3 = jax.vjp(_rms, x2_, g3_)
        dx2_, dg3 = vjp3(dh)
        dx2_ = dx2_ + dyt
        dmix_, dg2 = vjp2(dx2_)
        return (dx2_, dmix_), (dg2, dg3)

    dx2, dmix, d_g2, d_g3, received["w_ffn_down"] = _rowwise(
        "ffn_up_mid_bwd", mid_bwd_fn, [dgu, x, mix, dy], [w["w_ffn_up"], g2, g3],
        [_sds((s, D_MODEL)), _sds((s, D_MODEL), BF16)], [_sds((1, D_MODEL)), _sds((1, D_MODEL))], tm=512,
        exchange=_chip_exchange([down_sums]))
    d_w_out = _matmul(merged, dmix, ta=True, name="d_w_out", out_dtype=BF16)

    def merge_bwd_fn(dmt, cp, ao, gc, ga, w_out, w_cb, w_ab, b_cb):
        dm = lax.dot_general(dmt, w_out, NT, preferred_element_type=F32)
        _, vjp = jax.vjp(_merge, cp.astype(F32), ao.astype(F32), gc, ga, b_cb)
        dcp, dao, dgc, dga, dbias = vjp(dm)
        dcp, dao = dcp.astype(BF16), dao.astype(BF16)
        du3_ = lax.dot_general(dcp, w_cb, NT, preferred_element_type=F32)
        datt_ = lax.dot_general(dao, w_ab, NT, preferred_element_type=F32)
        return (dcp, dao, dgc, dga, du3_, datt_), (dbias,)

    d_conv_out, d_att_out, d_g_conv, d_g_att, du3, d_att, d_b_cb, theirs = _rowwise(
        "merge_bwd", merge_bwd_fn, [dmix, conv_pre, att_out, g_conv, g_att],
        [w["w_out"], w["w_conv_branch"], w["w_att_branch"], w["b_conv_branch"]],
        [_sds((s, D_MODEL), BF16)] * 4 + [_sds((s, CONV_DIM)), _sds((s, ATT_DIM), BF16)], [_sds((1, D_MODEL))], tm=512,
        exchange=_pair_exchange([up_slabs]))

    d_w_cb = _matmul(u3, d_conv_out, ta=True, name="d_w_conv_branch", out_dtype=BF16)
    d_w_ab = _matmul(att, d_att_out, ta=True, name="d_w_att_branch", out_dtype=BF16)

    dq, dk, dv, (received["w_ffn_up"],) = _attn_bwd(
        q, k, v, d_att, ltot, n_blocks, _chip_exchange([_pair_sum("pair_sum_w_ffn_up", up_slabs, theirs)]))

    mix_grads = {"w_conv_branch": d_w_cb, "w_att_branch": d_w_ab, "w_out": d_w_out}
    (d_conv_in, d_dw_w, d_dw_b, d_ln_g, d_ln_b), landed = _conv_bwd(
        conv_in, u1, du3, w["conv_ln_g"], w["conv_ln_b"], w["conv_dw_w"],
        _scatter_exchange([_grad_slabs(nm, mix_grads[nm]) for nm in mix_weights]))
    received.update(zip(mix_weights, landed))

    d_proj = jnp.concatenate([d_conv_in, dq.astype(BF16), dk.astype(BF16), dv.astype(BF16), d_g_conv, d_g_att],
                             axis=1)
    d_w_in = _matmul(d_proj, h1, ta=True, name="d_w_in", out_dtype=BF16)
    in_slabs = _side_slabs("w_in", d_w_in)
    (theirs,) = _exchange_call("pair_swap_w_in", _pair_exchange([in_slabs]))

    early = list(opt)

    def pre_bwd_fn(dpt, xt, dx2t, *rest):
        jobs, (w_in_t, g_) = rest[:-2], rest[-2:]
        dh = jnp.dot(dpt, w_in_t, preferred_element_type=F32)
        _, vjp = jax.vjp(_rms, xt, g_)
        dx_, dg_ = vjp(dh)
        updates = [_sum_adamw_tile(*jobs[4 * n:4 * n + 4]) for n in range(len(early))]
        return (dx_ + dx2t, *[u for four in updates for u in four]), (dg_,)

    res = _rowwise(
        "proj_norm_bwd", pre_bwd_fn,
        [d_proj, x, dx2] + [a for nm in early for a in (received[nm], *opt[nm])], [w_in, g1],
        [_sds((s, D_MODEL))] + [_sds(opt[nm][0].shape) for nm in early for _ in range(4)],
        [_sds((1, D_MODEL))], tm=512, exchange=_chip_exchange([_pair_sum("pair_sum_w_in", in_slabs, theirs)]))
    grad_x, d_g1, received["w_in"] = res[0], res[-2], res[-1]
    updated = {nm: res[1 + 4 * n:5 + 4 * n] for n, nm in enumerate(early)}

    grads = {
        "norm_mix_pre": d_g1, "conv_dw_w": d_dw_w, "conv_dw_b": d_dw_b,
        "conv_ln_g": d_ln_g, "conv_ln_b": d_ln_b, "b_conv_branch": d_b_cb,
        "norm_mix_post": d_g2, "norm_ffn_pre": d_g3, "norm_ffn_post": d_g4,
    }
    return loss, grad_x, received, updated, grads


def _place():
    x, y, c = lax.axis_index("x"), lax.axis_index("y"), lax.axis_index("c")
    return x, y, c


def _slot(px, py, pc):
    return 4 * px + 2 * py + pc


def _exchange_scratch(n):
    return [pltpu.SemaphoreType.DMA((7 * n,)), pltpu.SemaphoreType.DMA((7 * n,)), pltpu.SemaphoreType.DMA((n,))]


def _gather_exchange(arrs):
    n = len(arrs)

    def phases(ins, outs, send_sems, recv_sems, local_sems):
        x, y, c = _place()
        me, sibling = (x, y, c), (x, y, 1 - c)
        chips = [(1 - x, y), (x, 1 - y), (1 - x, 1 - y)]

        def copy(a, kk, block, to, src=None):
            dst = outs[a].at[_slot(*block)]
            return pltpu.make_async_remote_copy(
                src_ref=dst if src is None else src, dst_ref=dst,
                send_sem=send_sems.at[a * 7 + kk], recv_sem=recv_sems.at[a * 7 + kk],
                device_id=to, device_id_type=MESH)

        mine = [pltpu.make_async_copy(ins[a], outs[a].at[_slot(*me)], local_sems.at[a]) for a in range(n)]
        first = []
        for a in range(n):
            first.append(copy(a, 0, me, sibling, src=ins[a]))
            first += [copy(a, 1 + j, me, (*chip, c), src=ins[a]) for j, chip in enumerate(chips)]
        passed = [copy(a, 4 + j, (*chip, c), sibling) for j, chip in enumerate(chips) for a in range(n)]

        def send():
            for cp in mine + first:
                cp.start()

        def pass_on():
            for j, chip in enumerate(chips):
                for a in range(n):
                    copy(a, 1 + j, (*chip, c), me).wait_recv()
                    passed[j * n + a].start()

        def finish():
            for a in range(n):
                copy(a, 0, sibling, me).wait_recv()
                for j, chip in enumerate(chips):
                    copy(a, 4 + j, (*chip, 1 - c), me).wait_recv()
            for cp in first + passed:
                cp.wait_send()
            for cp in mine:
                cp.wait()

        return [send, pass_on, finish]

    return list(arrs), [_sds((N_DEV,) + a.shape, a.dtype) for a in arrs], _exchange_scratch(n), phases


def _scatter_exchange(arrs):
    n = len(arrs)
    flips = [(fx, fy, fc) for fx in (0, 1) for fy in (0, 1) for fc in (0, 1)][1:]

    def phases(ins, outs, send_sems, recv_sems, local_sems):
        x, y, c = _place()
        mine = _slot(x, y, c)
        local = [pltpu.make_async_copy(ins[a].at[mine], outs[a].at[mine], local_sems.at[a]) for a in range(n)]
        peers = [((1 - x) if fx else x, (1 - y) if fy else y, (1 - c) if fc else c) for fx, fy, fc in flips]

        def copy(a, kk, src_slot, dst_slot):
            return pltpu.make_async_remote_copy(
                src_ref=ins[a].at[src_slot], dst_ref=outs[a].at[dst_slot],
                send_sem=send_sems.at[a * 7 + kk], recv_sem=recv_sems.at[a * 7 + kk],
                device_id=peers[kk], device_id_type=MESH)

        sends = [copy(a, kk, _slot(*peers[kk]), mine) for a in range(n) for kk in range(7)]

        def send():
            for cp in local + sends:
                cp.start()

        def finish():
            for a in range(n):
                for kk in range(7):
                    copy(a, kk, mine, _slot(*peers[kk])).wait_recv()
            for cp in sends:
                cp.wait_send()
            for cp in local:
                cp.wait()

        return [send, finish]

    return list(arrs), [_sds(a.shape, a.dtype) for a in arrs], _exchange_scratch(n), phases


def _pair_exchange(arrs):
    n = len(arrs)

    def phases(ins, outs, send_sems, recv_sems, local_sems):
        x, y, c = _place()

        def copy(a, chip, side):
            return pltpu.make_async_remote_copy(
                src_ref=ins[a].at[chip, side], dst_ref=outs[a].at[chip],
                send_sem=send_sems.at[a * 7 + chip], recv_sem=recv_sems.at[a * 7 + chip],
                device_id=(x, y, 1 - c), device_id_type=MESH)

        sends = [copy(a, chip, 1 - c) for a in range(n) for chip in range(4)]

        def send():
            for cp in sends:
                cp.start()

        def finish():
            for a in range(n):
                for chip in range(4):
                    copy(a, chip, c).wait_recv()
            for cp in sends:
                cp.wait_send()

        return [send, finish]

    return list(arrs), [_sds((4,) + a.shape[2:], a.dtype) for a in arrs], _exchange_scratch(n), phases


def _chip_exchange(arrs):
    n = len(arrs)

    def phases(ins, outs, send_sems, recv_sems, local_sems):
        x, y, c = _place()
        mine = 2 * x + y
        chips = [(1 - x, y), (x, 1 - y), (1 - x, 1 - y)]
        local = [pltpu.make_async_copy(ins[a].at[mine], outs[a].at[mine], local_sems.at[a]) for a in range(n)]

        def copy(a, j, src_slot, dst_slot):
            return pltpu.make_async_remote_copy(
                src_ref=ins[a].at[src_slot], dst_ref=outs[a].at[dst_slot],
                send_sem=send_sems.at[a * 7 + j], recv_sem=recv_sems.at[a * 7 + j],
                device_id=(*chips[j], c), device_id_type=MESH)

        sends = [copy(a, j, 2 * chips[j][0] + chips[j][1], mine) for a in range(n) for j in range(3)]

        def send():
            for cp in local + sends:
                cp.start()

        def finish():
            for a in range(n):
                for j in range(3):
                    copy(a, j, mine, 2 * chips[j][0] + chips[j][1]).wait_recv()
            for cp in sends:
                cp.wait_send()
            for cp in local:
                cp.wait()

        return [send, finish]

    return list(arrs), [_sds(a.shape, a.dtype) for a in arrs], _exchange_scratch(n), phases


def _pair_sum(name, mine, theirs):
    _, _, r, c = mine.shape

    def body(side_ref, m_ref, t_ref, o_ref):
        o_ref[...] = (m_ref[...].astype(F32) + t_ref[...].astype(F32)).astype(o_ref.dtype)

    return pl.pallas_call(
        body, name=name,
        grid_spec=pltpu.PrefetchScalarGridSpec(
            num_scalar_prefetch=1, grid=(4,),
            in_specs=[pl.BlockSpec((None, None, r, c), lambda j, side: (j, side[0], 0, 0)),
                      pl.BlockSpec((None, r, c), lambda j, side: (j, 0, 0))],
            out_specs=pl.BlockSpec((None, r, c), lambda j, side: (j, 0, 0))),
        out_shape=_sds(theirs.shape, theirs.dtype),
        compiler_params=_params(("parallel",)),
    )(lax.axis_index("c").astype(jnp.int32).reshape(1), mine, theirs)


def _exchange_call(name, exchange):
    arrs, out_shape, scratch, phases = exchange
    n = len(arrs)

    def body(*refs):
        for step in phases(refs[:n], refs[n:2 * n], *refs[2 * n:]):
            step()

    return pl.pallas_call(body, name=name, in_specs=[ANY] * n, out_specs=[ANY] * n,
                          out_shape=out_shape, scratch_shapes=scratch)(*arrs)


def _carry_exchange(exchange, refs, n_in, n_out, first, middle, last):
    arrs, _, _, phases = exchange
    n = len(arrs)
    if n == 0:
        return lambda: None
    ins = refs[n_in:n_in + n]
    outs = refs[n_in + n + n_out:n_in + 2 * n + n_out]
    sems = n_in + 2 * n + n_out
    steps = phases(ins, outs, *refs[sems:sems + 3])
    pl.when(first)(steps[0])
    if len(steps) == 3:
        pl.when(middle)(steps[1])
    return lambda: pl.when(last)(steps[-1])


def _adamw_math(w, g, m, v):
    m2 = ADAM_B1 * m + (1.0 - ADAM_B1) * g
    v2 = ADAM_B2 * v + (1.0 - ADAM_B2) * jnp.square(g)
    m_hat = m2 / (1.0 - ADAM_B1 ** ADAM_STEP)
    v_hat = v2 / (1.0 - ADAM_B2 ** ADAM_STEP)
    delta = -ADAM_LR * (m_hat / (jnp.sqrt(v_hat) + ADAM_EPS) + ADAM_WD * w)
    return delta, m2, v2


def _sum_adamw_tile(parts, w, m, v):
    g = parts[0].astype(F32)
    for d in range(1, parts.shape[0]):
        g = g + parts[d].astype(F32)
    return (g, *_adamw_math(w, g, m, v))


def _sum_adamw(name, parts, w, m, v, tr=256):
    p, r, c = parts.shape
    tr = _pick(r, tr, 16)

    def body(p_ref, w_ref, m_ref, v_ref, g_ref, d_ref, m2_ref, v2_ref):
        g_ref[...], d_ref[...], m2_ref[...], v2_ref[...] = _sum_adamw_tile(p_ref[...], w_ref[...], m_ref[...], v_ref[...])

    tile = pl.BlockSpec((tr, c), lambda i: (i, 0))
    return pl.pallas_call(
        body, name=name, grid=(r // tr,),
        in_specs=[pl.BlockSpec((p, tr, c), lambda i: (0, i, 0)), tile, tile, tile],
        out_specs=[tile] * 4, out_shape=[_sds((r, c))] * 4,
        compiler_params=_params(("parallel",)),
    )(parts, w, m, v)


def _sum_parts(name, parts):
    p, r, c = parts.shape

    def body(p_ref, o_ref):
        g = p_ref[0]
        for d in range(1, p):
            g = g + p_ref[d]
        o_ref[...] = g

    return pl.pallas_call(
        body, name=name, out_shape=_sds((r, c)),
        in_specs=[pl.BlockSpec(memory_space=pltpu.VMEM)], out_specs=pl.BlockSpec(memory_space=pltpu.VMEM),
    )(parts)


WEIGHTS = ["norm_mix_pre", "w_in", "conv_dw_w", "conv_dw_b", "conv_ln_g", "conv_ln_b", "w_conv_branch",
           "b_conv_branch", "w_att_branch", "w_out", "norm_mix_post", "norm_ffn_pre", "w_ffn_up", "w_ffn_down",
           "norm_ffn_post"]
COL_SHARDED = ["w_conv_branch", "w_att_branch"]
ROW_SHARDED = ["w_out", "w_ffn_down"]
TRANSPOSED = ["w_in", "w_ffn_up"]
VECTORS = ["norm_mix_pre", "conv_dw_b", "conv_ln_g", "conv_ln_b", "b_conv_branch", "norm_mix_post",
           "norm_ffn_pre", "norm_ffn_post"]


def _cols_to_full(g):
    return g.transpose(1, 0, 2).reshape(g.shape[1], N_DEV * g.shape[2])


def _full_to_cols(f):
    return f.reshape(f.shape[0], N_DEV, f.shape[1] // N_DEV).transpose(1, 0, 2)


def _pack_vectors(vecs):
    rows = [jnp.pad(vecs[nm].reshape(-1), (0, D_MODEL - vecs[nm].size)) for nm in VECTORS]
    return jnp.stack(rows)


def _unpack_vectors(packed, sizes):
    return {nm: packed[n, :sizes[nm]] for n, nm in enumerate(VECTORS)}


def kernel(x, norm_mix_pre, w_in, conv_dw_w, conv_dw_b, conv_ln_g, conv_ln_b, w_conv_branch, b_conv_branch, w_att_branch, w_out, norm_mix_post, norm_ffn_pre, w_ffn_up, w_ffn_down, norm_ffn_post, loss_target, m_norm_mix_pre, m_w_in, m_conv_dw_w, m_conv_dw_b, m_conv_ln_g, m_conv_ln_b, m_w_conv_branch, m_b_conv_branch, m_w_att_branch, m_w_out, m_norm_mix_post, m_norm_ffn_pre, m_w_ffn_up, m_w_ffn_down, m_norm_ffn_post, v_norm_mix_pre, v_w_in, v_conv_dw_w, v_conv_dw_b, v_conv_ln_g, v_conv_ln_b, v_w_conv_branch, v_b_conv_branch, v_w_att_branch, v_w_out, v_norm_mix_post, v_norm_ffn_pre, v_w_ffn_up, v_w_ffn_down, v_norm_ffn_post):
    ws = dict(zip(WEIGHTS, [norm_mix_pre, w_in, conv_dw_w, conv_dw_b, conv_ln_g, conv_ln_b, w_conv_branch,
                            b_conv_branch, w_att_branch, w_out, norm_mix_post, norm_ffn_pre, w_ffn_up, w_ffn_down,
                            norm_ffn_post]))
    ms = dict(zip(WEIGHTS, [m_norm_mix_pre, m_w_in, m_conv_dw_w, m_conv_dw_b, m_conv_ln_g, m_conv_ln_b,
                            m_w_conv_branch, m_b_conv_branch, m_w_att_branch, m_w_out, m_norm_mix_post,
                            m_norm_ffn_pre, m_w_ffn_up, m_w_ffn_down, m_norm_ffn_post]))
    vs = dict(zip(WEIGHTS, [v_norm_mix_pre, v_w_in, v_conv_dw_w, v_conv_dw_b, v_conv_ln_g, v_conv_ln_b,
                            v_w_conv_branch, v_b_conv_branch, v_w_att_branch, v_w_out, v_norm_mix_post,
                            v_norm_ffn_pre, v_w_ffn_up, v_w_ffn_down, v_norm_ffn_post]))

    dw_block = jnp.pad(conv_dw_w, ((0, 1), (0, 0)))
    g_in, g_dw = _exchange_call("gather_first", _gather_exchange([w_in.T.astype(BF16), dw_block]))
    full = {"w_in": _full_weight("w_in", g_in), "conv_dw_w": _cols_to_full(g_dw)}
    for nm in VECTORS:
        full[nm] = ws[nm].reshape(1, -1)

    def as_kept(nm, a):
        return a.T if nm in TRANSPOSED else a

    ride_along = ["w_ffn_up", "w_out", "w_conv_branch", "w_att_branch"]
    loss_local, grad_x, received, updated, grads = _local_step(
        x[0], loss_target[0], full, {nm: as_kept(nm, ws[nm]).astype(BF16) for nm in LATE},
        {nm: tuple(as_kept(nm, a[nm]) for a in (ws, ms, vs)) for nm in ride_along})

    loss_at = (VECTORS.index("conv_dw_b"), CONV_DIM)
    small = _exchange_call("gather_small_grads", _gather_exchange(
        [_pack_vectors(grads).at[loss_at].set(loss_local), grads["conv_dw_w"]]))
    out_g, out_d, out_m, out_v = {}, {}, {}, {}
    for nm in LATE + ["w_in"]:
        res = updated[nm] if nm in updated else _sum_adamw(
            "adamw_" + nm, received[nm], *[as_kept(nm, a[nm]) for a in (ws, ms, vs)])
        out_g[nm], out_d[nm], out_m[nm], out_v[nm] = [as_kept(nm, r) for r in res]
    sizes = {nm: ws[nm].size for nm in VECTORS}
    vec = _sum_adamw("adamw_vectors", small[0], _pack_vectors(ws), _pack_vectors(ms), _pack_vectors(vs))
    for res, dst in zip(vec, (out_g, out_d, out_m, out_v)):
        dst.update(_unpack_vectors(res, sizes))
    loss = vec[0][loss_at]
    dw_full = _sum_parts("sum_dw_grads", small[1])
    me = _slot(*_place())
    dw_mine = lax.dynamic_slice(dw_full, (0, me * (CONV_DIM // N_DEV)), (CONV_WIDTH, CONV_DIM // N_DEV))
    nm = "conv_dw_w"
    out_g[nm], out_d[nm], out_m[nm], out_v[nm] = _sum_adamw("adamw_dw", dw_mine[None], ws[nm], ms[nm], vs[nm])

    outs = [loss, grad_x[None]]
    for group in (out_g, out_d, out_m, out_v):
        outs += [group[nm] for nm in WEIGHTS]
    return tuple(outs)
```

```python
import math

import jax
import jax.numpy as jnp
from jax import lax
from jax.experimental import pallas as pl
from jax.experimental.pallas import tpu as pltpu

F32 = jnp.float32
BF16 = jnp.bfloat16

N_DEV = 8
D_MODEL = 1024
CONV_DIM = 512
CONV_WIDTH = 31
N_HEADS = 8
HEAD_DIM = 64
ATT_DIM = N_HEADS * HEAD_DIM
D_FF = 2816
EPS = 1e-6
IN_SPLITS = (0, 1024, 1536, 2048, 2560, 3584, 4608)

ADAM_LR = 0.001
ADAM_B1 = 0.9
ADAM_B2 = 0.999
ADAM_EPS = 1e-08
ADAM_WD = 0.01
ADAM_STEP = 10

LANES = 128
SUBLANES = 8
HALO = 32
ATT_TILE = 256
ATT_PART = 192
DEAD_SUM = -120.0
VMEM_LIMIT = 56 * 1024 * 1024
MESH = pl.DeviceIdType.MESH
ANY = pl.BlockSpec(memory_space=pl.ANY)


def _pick(dim, target, align=LANES):
    t = min(dim, target)
    t -= t % align
    while t >= align:
        if dim % t == 0:
            return t
        t -= align
    return dim


def _params(semantics):
    return pltpu.CompilerParams(dimension_semantics=semantics, vmem_limit_bytes=VMEM_LIMIT)


def _matmul(a, b, *, name, ta=False, tb=False, out_dtype=F32):
    m, k = (a.shape[1], a.shape[0]) if ta else a.shape
    n, k2 = b.shape if tb else (b.shape[1], b.shape[0])
    assert k == k2, (a.shape, b.shape, ta, tb)
    tm, tn, tk = _pick(m, 1408 if ta else 512), _pick(n, 1536), _pick(k, 1536)
    nk = k // tk
    dims = (((0 if ta else 1,), (1 if tb else 0,)), ((), ()))

    def body(a_ref, b_ref, o_ref, *acc):
        part = lax.dot_general(a_ref[...], b_ref[...], dims, preferred_element_type=F32)
        if nk == 1:
            o_ref[...] = part.astype(o_ref.dtype)
            return
        acc_ref, = acc
        kk = pl.program_id(2)

        @pl.when(kk == 0)
        def _():
            acc_ref[...] = part

        @pl.when((kk > 0) & (kk < nk - 1))
        def _():
            acc_ref[...] += part

        @pl.when(kk == nk - 1)
        def _():
            o_ref[...] = (acc_ref[...] + part).astype(o_ref.dtype)

    a_spec = pl.BlockSpec((tk, tm), lambda j, i, kk: (kk, i)) if ta else pl.BlockSpec((tm, tk), lambda j, i, kk: (i, kk))
    b_spec = (pl.BlockSpec((tn, tk), lambda j, i, kk: (j, kk)) if tb
              else pl.BlockSpec((tk, tn), lambda j, i, kk: (kk, j)))
    return pl.pallas_call(
        body, name=name, grid=(n // tn, m // tm, nk),
        in_specs=[a_spec, b_spec],
        out_specs=pl.BlockSpec((tm, tn), lambda j, i, kk: (i, j)),
        out_shape=jax.ShapeDtypeStruct((m, n), out_dtype),
        scratch_shapes=[pltpu.VMEM((tm, tn), F32)] if nk > 1 else [],
        compiler_params=_params(("parallel", "parallel", "arbitrary")),
    )(a, b)


NO_EXCHANGE = ([], [], [], None)


def _sweep_marks(nt):
    i = pl.program_id(0)
    return i == 0, i == (3 * nt) // 4, i == nt - 1


def _rowwise(name, fn, rows, bcasts, row_outs, red_outs=(), tm=256, exchange=NO_EXCHANGE):
    s = rows[0].shape[0]
    tm = _pick(s, tm, 16)
    nt = s // tm
    resident = pl.Buffered(1)
    nr, nb, no, nd = len(rows), len(bcasts), len(row_outs), len(red_outs)
    x_arrs, x_shape, x_scratch, _ = exchange
    nx = len(x_arrs)
    first_out = nr + nb + nx

    def body(*refs):
        finish_exchange = _carry_exchange(exchange, refs, nr + nb, no + nd, *_sweep_marks(nt))
        ins = [r[...] for r in refs[:nr + nb]]
        outs, reds = fn(*ins)
        for ref, val in zip(refs[first_out:first_out + no], outs):
            ref[...] = val.astype(ref.dtype)
        i = pl.program_id(0)
        for ref, val in zip(refs[first_out + no:first_out + no + nd], reds):
            @pl.when(i == 0)
            def _():
                ref[...] = val

            @pl.when(i > 0)
            def _():
                ref[...] += val
        finish_exchange()

    def row_spec(a):
        assert a.shape[-2] % nt == 0, (name, a.shape, nt)
        if len(a.shape) == 3:
            return pl.BlockSpec((a.shape[0], a.shape[1] // nt, a.shape[2]), lambda i: (0, i, 0))
        return pl.BlockSpec((a.shape[0] // nt, a.shape[1]), lambda i: (i, 0))

    in_specs = [row_spec(r) for r in rows]
    in_specs += [pl.BlockSpec(b.shape, lambda i: (0, 0), pipeline_mode=resident) for b in bcasts]
    out_specs = [row_spec(o) for o in row_outs]
    out_specs += [pl.BlockSpec(d.shape, lambda i: (0, 0)) for d in red_outs]
    return pl.pallas_call(
        body, name=name, grid=(nt,), in_specs=in_specs + [ANY] * nx, out_specs=out_specs + [ANY] * nx,
        out_shape=list(row_outs) + list(red_outs) + x_shape, scratch_shapes=x_scratch,
        compiler_params=_params(("arbitrary",)),
    )(*rows, *bcasts, *x_arrs)


def _sds(shape, dtype=F32):
    return jax.ShapeDtypeStruct(shape, dtype)


def _rms(x, g):
    y = x * lax.rsqrt(jnp.mean(x * x, axis=-1, keepdims=True) + EPS)
    return y * g


def _silu(x):
    return x * jax.nn.sigmoid(x)


def _swiglu(g, u):
    return _silu(g) * u


def _ln_silu(u, g, b):
    mu = jnp.mean(u, axis=-1, keepdims=True)
    var = jnp.mean(jnp.square(u - mu), axis=-1, keepdims=True)
    return _silu((u - mu) * lax.rsqrt(var + EPS) * g + b)


def _merge(conv_pre, att_out, g_conv, g_att, b_cb):
    return jax.nn.sigmoid(g_conv) * (conv_pre + b_cb) + jax.nn.sigmoid(g_att) * att_out


def _glu(t):
    return t[:, :CONV_DIM] * jax.nn.sigmoid(t[:, CONV_DIM:])


def _shifted_reader(buf, shifted, tm):
    for b in range(1, SUBLANES):
        shifted[b - 1, :, :] = buf[pl.ds(b, tm + HALO - SUBLANES), :]

    def read(o):
        a, b = divmod(o, SUBLANES)
        return buf[pl.ds(SUBLANES * a, tm), :] if b == 0 else shifted[b - 1, pl.ds(SUBLANES * a, tm), :]

    return read


def _conv_fwd(conv_in, w_pad, b, ln_g, ln_b, exchange, tm=256):
    s = conv_in.shape[0]
    tm = _pick(s, tm, HALO)
    ratio = tm // HALO
    x_arrs, x_shape, x_scratch, _ = exchange
    nx = len(x_arrs)

    def body(*refs):
        main_ref, halo_ref, w_ref, b_ref, g_ref, be_ref = refs[:6]
        u3_ref, u1_ref = refs[6 + nx:8 + nx]
        buf, shifted = refs[-2:]
        finish_exchange = _carry_exchange(exchange, refs, 6, 2, *_sweep_marks(s // tm))
        i = pl.program_id(0)
        buf[0:HALO, :] = _glu(halo_ref[...]) * (i > 0).astype(F32)
        buf[HALO:HALO + tm, :] = _glu(main_ref[...])
        read = _shifted_reader(buf, shifted, tm)
        acc = jnp.zeros((tm, CONV_DIM), F32) + b_ref[...]
        for j in range(CONV_WIDTH):
            acc = acc + w_ref[j:j + 1, :] * read(HALO - (CONV_WIDTH - 1) + j)
        u1_ref[...] = acc
        u3_ref[...] = _ln_silu(acc, g_ref[...], be_ref[...]).astype(u3_ref.dtype)
        finish_exchange()

    res = pl.pallas_call(
        body, name="conv_fwd", grid=(s // tm,),
        in_specs=[pl.BlockSpec((tm, 2 * CONV_DIM), lambda i: (i, 0)),
                  pl.BlockSpec((HALO, 2 * CONV_DIM), lambda i: (jnp.maximum(i * ratio - 1, 0), 0)),
                  pl.BlockSpec(w_pad.shape, lambda i: (0, 0)),
                  pl.BlockSpec(b.shape, lambda i: (0, 0)),
                  pl.BlockSpec(ln_g.shape, lambda i: (0, 0)),
                  pl.BlockSpec(ln_b.shape, lambda i: (0, 0))] + [ANY] * nx,
        out_specs=[pl.BlockSpec((tm, CONV_DIM), lambda i: (i, 0)),
                   pl.BlockSpec((tm, CONV_DIM), lambda i: (i, 0))] + [ANY] * nx,
        out_shape=[_sds((s, CONV_DIM), BF16), _sds((s, CONV_DIM), F32)] + x_shape,
        scratch_shapes=x_scratch + [pltpu.VMEM((tm + HALO, CONV_DIM), F32),
                                    pltpu.VMEM((SUBLANES - 1, tm + HALO - SUBLANES, CONV_DIM), F32)],
        compiler_params=_params(("arbitrary",)),
    )(conv_in, conv_in, w_pad, b, ln_g, ln_b, *x_arrs)
    return res[0], res[1], res[2:]


def _conv_bwd(conv_in, u1, du3, ln_g, ln_b, w_pad, exchange, tm=256):
    s = conv_in.shape[0]
    tm = _pick(s, tm, HALO)
    ratio = tm // HALO
    nt = s // tm
    last_halo = s // HALO - 1
    x_arrs, x_shape, x_scratch, _ = exchange
    nx = len(x_arrs)

    def body(*refs):
        main_ref, halo_ref, u1_ref, u1n_ref, du3_ref, du3n_ref, g_ref, be_ref, w_ref = refs[:9]
        dci_ref, dw_ref, db_ref, dg_ref, dbe_ref = refs[9 + nx:14 + nx]
        ubuf, dbuf, ushift, dshift = refs[-4:]
        finish_exchange = _carry_exchange(exchange, refs, 9, 5, *_sweep_marks(nt))
        i = pl.program_id(0)
        main = main_ref[...]
        a = main[:, :CONV_DIM]
        sb = jax.nn.sigmoid(main[:, CONV_DIM:])
        ubuf[0:HALO, :] = _glu(halo_ref[...]) * (i > 0).astype(F32)
        ubuf[HALO:HALO + tm, :] = a * sb

        def ln_bwd(u1t, du3t):
            _, vjp = jax.vjp(_ln_silu, u1t, g_ref[...], be_ref[...])
            return vjp(du3t)

        du, dg, dbe = ln_bwd(u1_ref[...], du3_ref[...])
        dbuf[0:tm, :] = du
        dbuf[tm:tm + HALO, :] = ln_bwd(u1n_ref[...], du3n_ref[...])[0] * (i < nt - 1).astype(F32)

        @pl.when(i == 0)
        def _():
            dw_ref[...] = jnp.zeros_like(dw_ref)
            db_ref[...] = jnp.zeros_like(db_ref)
            dg_ref[...] = jnp.zeros_like(dg_ref)
            dbe_ref[...] = jnp.zeros_like(dbe_ref)

        dg_ref[...] += dg
        dbe_ref[...] += dbe

        read_u = _shifted_reader(ubuf, ushift, tm)
        read_d = _shifted_reader(dbuf, dshift, tm)
        du0 = jnp.zeros((tm, CONV_DIM), F32)
        for j in range(CONV_WIDTH):
            du0 = du0 + w_ref[j:j + 1, :] * read_d(CONV_WIDTH - 1 - j)
            dw_ref[j:j + 1, :] += jnp.sum(du * read_u(HALO - (CONV_WIDTH - 1) + j), axis=0, keepdims=True)
        db_ref[...] += jnp.sum(du, axis=0, keepdims=True)
        dci_ref[:, :CONV_DIM] = (du0 * sb).astype(dci_ref.dtype)
        dci_ref[:, CONV_DIM:] = (du0 * a * sb * (1.0 - sb)).astype(dci_ref.dtype)
        finish_exchange()

    res = pl.pallas_call(
        body, name="conv_bwd", grid=(nt,),
        in_specs=[pl.BlockSpec((tm, 2 * CONV_DIM), lambda i: (i, 0)),
                  pl.BlockSpec((HALO, 2 * CONV_DIM), lambda i: (jnp.maximum(i * ratio - 1, 0), 0))]
        + [pl.BlockSpec((tm, CONV_DIM), lambda i: (i, 0)),
           pl.BlockSpec((HALO, CONV_DIM), lambda i: (jnp.minimum((i + 1) * ratio, last_halo), 0))] * 2
        + [pl.BlockSpec((1, CONV_DIM), lambda i: (0, 0))] * 2 + [pl.BlockSpec(w_pad.shape, lambda i: (0, 0))]
        + [ANY] * nx,
        out_specs=[pl.BlockSpec((tm, 2 * CONV_DIM), lambda i: (i, 0)),
                   pl.BlockSpec(w_pad.shape, lambda i: (0, 0))]
        + [pl.BlockSpec((1, CONV_DIM), lambda i: (0, 0))] * 3 + [ANY] * nx,
        out_shape=[_sds((s, 2 * CONV_DIM), BF16), _sds(w_pad.shape)] + [_sds((1, CONV_DIM))] * 3 + x_shape,
        scratch_shapes=x_scratch + [pltpu.VMEM((tm + HALO, CONV_DIM), F32)] * 2
        + [pltpu.VMEM((SUBLANES - 1, tm + HALO - SUBLANES, CONV_DIM), F32)] * 2,
        compiler_params=_params(("arbitrary",)),
    )(conv_in, conv_in, u1, u1, du3, du3, ln_g, ln_b, w_pad, *x_arrs)
    return res[:5], res[5:]


def _logsig_neg(z):
    return jnp.minimum(-z, 0.0) - jnp.log(1.0 + jnp.exp(-jnp.abs(z)))


def _split_dot(val, tri):
    hi = val.astype(BF16)
    lo = (val - hi.astype(F32)).astype(BF16)
    return jnp.dot(hi, tri, preferred_element_type=F32) + jnp.dot(lo, tri, preferred_element_type=F32)


def _attn_masks(t, later):
    row = lax.broadcasted_iota(jnp.int32, (t, t), 0)
    col = lax.broadcasted_iota(jnp.int32, (t, t), 1)
    tri = jnp.where(row > col if later else row <= col, 1.0, 0.0).astype(BF16)
    return col < row, tri


def _grid_marks(h, nq):
    hh, i = pl.program_id(0), pl.program_id(1)
    return (hh == 0) & (i == 0), (hh == (3 * h) // 4) & (i == 0), (hh == h - 1) & (i == nq - 1)


def _head_masks(shape):
    lane = lax.broadcasted_iota(jnp.int32, shape, len(shape) - 1)
    return lane < HEAD_DIM, lane >= HEAD_DIM


def _per_head(blk):
    m0, m1 = _head_masks(blk.shape)
    zero = jnp.zeros_like(blk)
    return jnp.where(m0, blk, zero), jnp.where(m1, blk, zero)


NT = (((1,), (1,)), ((), ()))
TN = (((0,), (0,)), ((), ()))


def _with_top(whole, top):
    rows = top.shape[0]
    return top if rows == whole.shape[0] else jnp.concatenate([top, whole[rows:]], axis=0)


def _attn_fwd(q, k, v, exchange):
    s = q.shape[0]
    hp = q.shape[1] // LANES
    t = ATT_TILE
    scale = 1.0 / math.sqrt(HEAD_DIM)
    x_arrs, x_shape, x_scratch, _ = exchange
    nx = len(x_arrs)

    def body(*refs):
        q_ref, k_ref, v_ref = refs[:3]
        o_ref, lt_ref, nb_ref = refs[3 + nx:6 + nx]
        finish_exchange = _carry_exchange(exchange, refs, 3, 3, *_grid_marks(hp, s // t))
        i = pl.program_id(1)
        qs = _per_head((q_ref[...].astype(F32) * scale).astype(BF16))
        causal, tri = _attn_masks(t, later=True)

        def step(kb, carry, masked, rows):
            cs, acc = carry
            off = pl.multiple_of(kb * t, t)
            kblk = k_ref[pl.ds(off, t), :]
            vs = _per_head(v_ref[pl.ds(off, t), :])
            acc_top = acc[:rows]
            new_cs = []
            for hd in range(2):
                z = lax.dot_general(qs[hd][:rows], kblk, NT, preferred_element_type=F32)
                l = _logsig_neg(z)
                if masked:
                    l = jnp.where(causal, l, 0.0)
                e = z + l + _split_dot(l, tri) + cs[hd][:rows]
                if masked:
                    e = jnp.where(causal, e, -1e30)
                acc_top = acc_top + jnp.dot(jnp.exp(e).astype(BF16), vs[hd], preferred_element_type=F32)
                new_cs.append(_with_top(cs[hd], cs[hd][:rows] + jnp.sum(l, axis=1, keepdims=True)))
            return tuple(new_cs), _with_top(acc, acc_top)

        zero = jnp.zeros((t, 1), F32)
        carry = step(i, ((zero, zero), jnp.zeros((t, LANES), F32)), True, t)

        def live(cs, lo, hi):
            return jnp.maximum(jnp.max(cs[0][lo:hi]), jnp.max(cs[1][lo:hi])) > DEAD_SUM

        def more(state):
            n, _, (cs, _) = state
            return (n < i) & live(cs, 0, t)

        def sweep(state):
            n, n_full, cr = state
            whole = live(cr[0], ATT_PART, t)
            cr = lax.cond(whole, lambda c: step(i - 1 - n, c, False, t), lambda c: step(i - 1 - n, c, False, ATT_PART), cr)
            return n + 1, n_full + whole.astype(jnp.int32), cr

        n_blocks, n_full, carry = lax.while_loop(more, sweep, (jnp.int32(0), jnp.int32(0), carry))
        m0, _ = _head_masks((t, LANES))
        lt_ref[...] = jnp.where(m0, carry[0][0], carry[0][1])
        o_ref[...] = carry[1].astype(o_ref.dtype)
        nb_ref[0, pl.program_id(0), i] = n_blocks.astype(F32)
        nb_ref[1, pl.program_id(0), i] = n_full.astype(F32)
        finish_exchange()

    res = pl.pallas_call(
        body, name="attn_fwd", grid=(hp, s // t),
        in_specs=[pl.BlockSpec((t, LANES), lambda p, i: (i, p)),
                  pl.BlockSpec((s, LANES), lambda p, i: (0, p)),
                  pl.BlockSpec((s, LANES), lambda p, i: (0, p))] + [ANY] * nx,
        out_specs=[pl.BlockSpec((t, LANES), lambda p, i: (i, p)),
                   pl.BlockSpec((None, t, LANES), lambda p, i: (p, i, 0)),
                   pl.BlockSpec(memory_space=pltpu.SMEM)] + [ANY] * nx,
        out_shape=[_sds(q.shape, BF16), _sds((hp, s, LANES), F32), _sds((2, hp, s // t), F32)] + x_shape,
        scratch_shapes=x_scratch,
        compiler_params=_params(("arbitrary", "arbitrary")),
    )(q, k, v, *x_arrs)
    return res[0], res[1], res[2], res[3:]


def _attn_bwd(q, k, v, do, ltot, n_blocks, exchange):
    s = q.shape[0]
    hp = q.shape[1] // LANES
    t = ATT_TILE
    scale = 1.0 / math.sqrt(HEAD_DIM)
    x_arrs, x_shape, x_scratch, _ = exchange
    nx = len(x_arrs)

    def body(*refs):
        q_ref, k_ref, v_ref, do_ref, lt_ref, nb_ref = refs[:6]
        dq_ref, dk_ref, dv_ref = refs[6 + nx:9 + nx]
        finish_exchange = _carry_exchange(exchange, refs, 6, 3, *_grid_marks(hp, s // t))
        i = pl.program_id(1)
        n_blocks = jnp.clip(nb_ref[0, pl.program_id(0), i].astype(jnp.int32), 0, i)
        n_full = jnp.clip(nb_ref[1, pl.program_id(0), i].astype(jnp.int32), 0, n_blocks)

        @pl.when(i == 0)
        def _():
            dk_ref[...] = jnp.zeros_like(dk_ref)
            dv_ref[...] = jnp.zeros_like(dv_ref)

        qb = q_ref[...]
        qm = _per_head(qb)
        qs = _per_head((qb.astype(F32) * scale).astype(BF16))
        dos = _per_head(do_ref[...])
        lts = (lt_ref[:, 0:1], lt_ref[:, HEAD_DIM:HEAD_DIM + 1])
        causal, tri = _attn_masks(t, later=False)

        def step(kb, carry, masked, rows):
            cls, cgs, dq = carry
            off = pl.multiple_of(kb * t, t)
            kblk = k_ref[pl.ds(off, t), :]
            vblk = v_ref[pl.ds(off, t), :]
            ks = _per_head(kblk)
            dq_top = dq[:rows]
            dk = jnp.zeros((t, LANES), F32)
            dv = jnp.zeros((t, LANES), F32)
            new_cls, new_cgs = [], []
            for hd in range(2):
                z = lax.dot_general(qs[hd][:rows], kblk, NT, preferred_element_type=F32)
                l = _logsig_neg(z)
                if masked:
                    l = jnp.where(causal, l, 0.0)
                e = z + l + ((lts[hd][:rows] - cls[hd][:rows]) - _split_dot(l, tri))
                if masked:
                    e = jnp.where(causal, e, -1e30)
                a = jnp.exp(e)
                g = lax.dot_general(dos[hd][:rows], vblk, NT, preferred_element_type=F32) * a
                p = cgs[hd][:rows] + jnp.dot(g.astype(BF16), tri, preferred_element_type=F32) - g
                el = jnp.exp(l)
                dz = g * el - p * (1.0 - el)
                if masked:
                    dz = jnp.where(causal, dz, 0.0)
                dzb = (dz * scale).astype(BF16)
                dq_top = dq_top + jnp.dot(dzb, ks[hd], preferred_element_type=F32)
                dk = dk + lax.dot_general(dzb, qm[hd][:rows], TN, preferred_element_type=F32)
                dv = dv + lax.dot_general(a.astype(BF16), dos[hd][:rows], TN, preferred_element_type=F32)
                new_cls.append(_with_top(cls[hd], cls[hd][:rows] + jnp.sum(l, axis=1, keepdims=True)))
                new_cgs.append(_with_top(cgs[hd], cgs[hd][:rows] + jnp.sum(g, axis=1, keepdims=True)))
            dk_ref[pl.ds(off, t), :] += dk
            dv_ref[pl.ds(off, t), :] += dv
            return tuple(new_cls), tuple(new_cgs), _with_top(dq, dq_top)

        zero = jnp.zeros((t, 1), F32)
        init = ((zero, zero), (zero, zero), jnp.zeros((t, LANES), F32))
        carry = lax.fori_loop(i - n_blocks, i - n_full, lambda kb, cr: step(kb, cr, False, ATT_PART), init)
        carry = lax.fori_loop(i - n_full, i, lambda kb, cr: step(kb, cr, False, t), carry)
        carry = step(i, carry, True, t)
        dq_ref[...] = carry[2]
        finish_exchange()

    blk = pl.BlockSpec((t, LANES), lambda p, i: (i, p))
    whole = pl.BlockSpec((s, LANES), lambda p, i: (0, p))
    res = pl.pallas_call(
        body, name="attn_bwd", grid=(hp, s // t),
        in_specs=[blk, whole, whole, blk, pl.BlockSpec((None, t, LANES), lambda p, i: (p, i, 0)),
                  pl.BlockSpec(memory_space=pltpu.SMEM)] + [ANY] * nx,
        out_specs=[blk, whole, whole] + [ANY] * nx,
        out_shape=[_sds(q.shape)] * 3 + x_shape,
        scratch_shapes=x_scratch,
        compiler_params=_params(("arbitrary", "arbitrary")),
    )(q, k, v, do, ltot, n_blocks, *x_arrs)
    return res[0], res[1], res[2], res[3:]


LATE = ["w_conv_branch", "w_att_branch", "w_out", "w_ffn_up", "w_ffn_down"]


def _full_weight(name, gathered):
    return _cols_to_full(gathered) if name in COL_SHARDED else gathered.reshape(-1, gathered.shape[2])


def _grad_slabs(name, grad):
    return _full_to_cols(grad) if name in COL_SHARDED else grad.reshape(N_DEV, -1, grad.shape[1])


def _side_slabs(name, grad):
    slabs = _grad_slabs(name, grad)
    return slabs.reshape((4, 2) + slabs.shape[1:])


def _local_step(x, target, w, late_blocks, opt):
    s = x.shape[0]
    w = dict(w)
    g1, g2, g3, g4 = w["norm_mix_pre"], w["norm_mix_post"], w["norm_ffn_pre"], w["norm_ffn_post"]

    w_in = w["w_in"]

    def proj_fn(xt, g1_, w_in_t):
        h = _rms(xt, g1_).astype(BF16)
        proj = lax.dot_general(h, w_in_t, NT, preferred_element_type=F32)
        return (h, *[proj[:, IN_SPLITS[n]:IN_SPLITS[n + 1]] for n in range(6)]), ()

    mix_weights = ["w_conv_branch", "w_att_branch", "w_out"]
    h1, conv_in, q, k, v, g_conv, g_att = _rowwise(
        "norm_proj", proj_fn, [x], [g1, w_in],
        [_sds((s, D_MODEL), BF16), _sds((s, 2 * CONV_DIM)), _sds((s, ATT_DIM), BF16), _sds((s, ATT_DIM), BF16),
         _sds((s, ATT_DIM), BF16), _sds((s, D_MODEL)), _sds((s, D_MODEL))], tm=512)

    u3, u1, gathered = _conv_fwd(conv_in, w["conv_dw_w"], w["conv_dw_b"], w["conv_ln_g"], w["conv_ln_b"],
                                 _gather_exchange([late_blocks[nm] for nm in mix_weights]))
    for nm, g in zip(mix_weights, gathered):
        w[nm] = _full_weight(nm, g)
    att, ltot, n_blocks, (g_up,) = _attn_fwd(q, k, v, _gather_exchange([late_blocks["w_ffn_up"]]))
    w["w_ffn_up"] = _full_weight("w_ffn_up", g_up)

    def merge_fn(u3t, at, gc, ga, xt, w_cb, w_ab, b_cb, w_out, g2_, g3_):
        cp = jnp.dot(u3t, w_cb, preferred_element_type=F32)
        ao = jnp.dot(at, w_ab, preferred_element_type=F32)
        mg = _merge(cp, ao, gc, ga, b_cb).astype(BF16)
        mix_ = jnp.dot(mg, w_out, preferred_element_type=F32)
        x2_ = xt + _rms(mix_, g2_)
        return (mg, cp, ao, mix_, x2_, _rms(x2_, g3_)), ()

    merged, conv_pre, att_out, mix, x2, h2 = _rowwise(
        "branch_merge_mix", merge_fn, [u3, att, g_conv, g_att, x],
        [w["w_conv_branch"], w["w_att_branch"], w["b_conv_branch"], w["w_out"], g2, g3],
        [_sds((s, D_MODEL), BF16)] * 3 + [_sds((s, D_MODEL)), _sds((s, D_MODEL)), _sds((s, D_MODEL), BF16)], tm=512)

    def ffn_up_fn(ht, w_up_t):
        gu_ = lax.dot_general(ht, w_up_t, NT, preferred_element_type=F32)
        return (gu_, _swiglu(gu_[:, :D_FF], gu_[:, D_FF:])), ()

    gu, act, g_down = _rowwise("ffn_up", ffn_up_fn, [h2], [w["w_ffn_up"]],
                               [_sds((s, 2 * D_FF), BF16), _sds((s, D_FF), BF16)], tm=512,
                               exchange=_gather_exchange([late_blocks["w_ffn_down"]]))
    w["w_ffn_down"] = _full_weight("w_ffn_down", g_down)

    def final_fn(at, x2t, tgt, w_down, g4_):
        ff = jnp.dot(at, w_down, preferred_element_type=F32)
        n4, vjp = jax.vjp(_rms, ff, g4_)
        err = x2t + n4 - tgt
        dy = err * (1.0 / D_MODEL)
        dff, dg4 = vjp(dy)
        return (dy, dff), (jnp.sum(err * err, axis=0, keepdims=True), dg4)

    dy, dff, loss_cols, d_g4 = _rowwise("ffn_down_loss", final_fn, [act, x2, target], [w["w_ffn_down"], g4],
                                        [_sds((s, D_MODEL)), _sds((s, D_MODEL), BF16)],
                                        [_sds((1, D_MODEL)), _sds((1, D_MODEL))], tm=512)
    loss = 0.5 * jnp.sum(loss_cols) / D_MODEL

    d_w_down = _matmul(act, dff, ta=True, name="d_w_down", out_dtype=BF16)

    def act_bwd_fn(dfft, gut, w_down):
        d_act = lax.dot_general(dfft, w_down, NT, preferred_element_type=F32)
        gu_ = gut.astype(F32)
        _, vjp = jax.vjp(_swiglu, gu_[:, :D_FF], gu_[:, D_FF:])
        return (jnp.concatenate(vjp(d_act), axis=1),), ()

    down_slabs = _side_slabs("w_ffn_down", d_w_down)
    dgu, theirs = _rowwise("ffn_act_bwd", act_bwd_fn, [dff, gu], [w["w_ffn_down"]], [_sds((s, 2 * D_FF), BF16)],
                           exchange=_pair_exchange([down_slabs]))
    down_sums = _pair_sum("pair_sum_w_ffn_down", down_slabs, theirs)
    d_w_up = _matmul(dgu, h2, ta=True, name="d_w_up", out_dtype=BF16)
    received = {}
    up_slabs = _side_slabs("w_ffn_up", d_w_up)

    def mid_bwd_fn(dgut, xt, mt, dyt, w_up_t, g2_, g3_):
        dh = jnp.dot(dgut, w_up_t, preferred_element_type=F32)
        n2, vjp2 = jax.vjp(_rms, mt, g2_)
        x2_ = xt + n2
        _, vjp3 = jax.vjp(_rms, x2_, g3_)
        dx2_, dg3 = vjp3(dh)
        dx2_ = dx2_ + dyt
        dmix_, dg2 = vjp2(dx2_)
        return (dx2_, dmix_), (dg2, dg3)

    dx2, dmix, d_g2, d_g3, received["w_ffn_down"] = _rowwise(
        "ffn_up_mid_bwd", mid_bwd_fn, [dgu, x, mix, dy], [w["w_ffn_up"], g2, g3],
        [_sds((s, D_MODEL)), _sds((s, D_MODEL), BF16)], [_sds((1, D_MODEL)), _sds((1, D_MODEL))], tm=512,
        exchange=_chip_exchange([down_sums]))
    d_w_out = _matmul(merged, dmix, ta=True, name="d_w_out", out_dtype=BF16)

    def merge_bwd_fn(dmt, cp, ao, gc, ga, w_out, w_cb, w_ab, b_cb):
        dm = lax.dot_general(dmt, w_out, NT, preferred_element_type=F32)
        _, vjp = jax.vjp(_merge, cp.astype(F32), ao.astype(F32), gc, ga, b_cb)
        dcp, dao, dgc, dga, dbias = vjp(dm)
        dcp, dao = dcp.astype(BF16), dao.astype(BF16)
        du3_ = lax.dot_general(dcp, w_cb, NT, preferred_element_type=F32)
        datt_ = lax.dot_general(dao, w_ab, NT, preferred_element_type=F32)
        return (dcp, dao, dgc, dga, du3_, datt_), (dbias,)

    d_conv_out, d_att_out, d_g_conv, d_g_att, du3, d_att, d_b_cb, theirs = _rowwise(
        "merge_bwd", merge_bwd_fn, [dmix, conv_pre, att_out, g_conv, g_att],
        [w["w_out"], w["w_conv_branch"], w["w_att_branch"], w["b_conv_branch"]],
        [_sds((s, D_MODEL), BF16)] * 4 + [_sds((s, CONV_DIM)), _sds((s, ATT_DIM), BF16)], [_sds((1, D_MODEL))], tm=512,
        exchange=_pair_exchange([up_slabs]))

    d_w_cb = _matmul(u3, d_conv_out, ta=True, name="d_w_conv_branch", out_dtype=BF16)
    d_w_ab = _matmul(att, d_att_out, ta=True, name="d_w_att_branch", out_dtype=BF16)

    dq, dk, dv, (received["w_ffn_up"],) = _attn_bwd(
        q, k, v, d_att, ltot, n_blocks, _chip_exchange([_pair_sum("pair_sum_w_ffn_up", up_slabs, theirs)]))

    mix_grads = {"w_conv_branch": d_w_cb, "w_att_branch": d_w_ab, "w_out": d_w_out}
    (d_conv_in, d_dw_w, d_dw_b, d_ln_g, d_ln_b), landed = _conv_bwd(
        conv_in, u1, du3, w["conv_ln_g"], w["conv_ln_b"], w["conv_dw_w"],
        _scatter_exchange([_grad_slabs(nm, mix_grads[nm]) for nm in mix_weights]))
    received.update(zip(mix_weights, landed))

    d_proj = jnp.concatenate([d_conv_in, dq.astype(BF16), dk.astype(BF16), dv.astype(BF16), d_g_conv, d_g_att],
                             axis=1)
    d_w_in = _matmul(d_proj, h1, ta=True, name="d_w_in", out_dtype=BF16)
    in_slabs = _side_slabs("w_in", d_w_in)
    (theirs,) = _exchange_call("pair_swap_w_in", _pair_exchange([in_slabs]))

    early = list(opt)

    def pre_bwd_fn(dpt, xt, dx2t, *rest):
        jobs, (w_in_t, g_) = rest[:-2], rest[-2:]
        dh = jnp.dot(dpt, w_in_t, preferred_element_type=F32)
        _, vjp = jax.vjp(_rms, xt, g_)
        dx_, dg_ = vjp(dh)
        updates = [_sum_adamw_tile(*jobs[4 * n:4 * n + 4]) for n in range(len(early))]
        return (dx_ + dx2t, *[u for four in updates for u in four]), (dg_,)

    res = _rowwise(
        "proj_norm_bwd", pre_bwd_fn,
        [d_proj, x, dx2] + [a for nm in early for a in (received[nm], *opt[nm])], [w_in, g1],
        [_sds((s, D_MODEL))] + [_sds(opt[nm][0].shape) for nm in early for _ in range(4)],
        [_sds((1, D_MODEL))], tm=512, exchange=_chip_exchange([_pair_sum("pair_sum_w_in", in_slabs, theirs)]))
    grad_x, d_g1, received["w_in"] = res[0], res[-2], res[-1]
    updated = {nm: res[1 + 4 * n:5 + 4 * n] for n, nm in enumerate(early)}

    grads = {
        "norm_mix_pre": d_g1, "conv_dw_w": d_dw_w, "conv_dw_b": d_dw_b,
        "conv_ln_g": d_ln_g, "conv_ln_b": d_ln_b, "b_conv_branch": d_b_cb,
        "norm_mix_post": d_g2, "norm_ffn_pre": d_g3, "norm_ffn_post": d_g4,
    }
    return loss, grad_x, received, updated, grads


def _place():
    x, y, c = lax.axis_index("x"), lax.axis_index("y"), lax.axis_index("c")
    return x, y, c


def _slot(px, py, pc):
    return 4 * px + 2 * py + pc


def _exchange_scratch(n):
    return [pltpu.SemaphoreType.DMA((7 * n,)), pltpu.SemaphoreType.DMA((7 * n,)), pltpu.SemaphoreType.DMA((n,))]


def _gather_exchange(arrs):
    n = len(arrs)

    def phases(ins, outs, send_sems, recv_sems, local_sems):
        x, y, c = _place()
        me, sibling = (x, y, c), (x, y, 1 - c)
        chips = [(1 - x, y), (x, 1 - y), (1 - x, 1 - y)]

        def copy(a, kk, block, to, src=None):
            dst = outs[a].at[_slot(*block)]
            return pltpu.make_async_remote_copy(
                src_ref=dst if src is None else src, dst_ref=dst,
                send_sem=send_sems.at[a * 7 + kk], recv_sem=recv_sems.at[a * 7 + kk],
                device_id=to, device_id_type=MESH)

        mine = [pltpu.make_async_copy(ins[a], outs[a].at[_slot(*me)], local_sems.at[a]) for a in range(n)]
        first = []
        for a in range(n):
            first.append(copy(a, 0, me, sibling, src=ins[a]))
            first += [copy(a, 1 + j, me, (*chip, c), src=ins[a]) for j, chip in enumerate(chips)]
        passed = [copy(a, 4 + j, (*chip, c), sibling) for j, chip in enumerate(chips) for a in range(n)]

        def send():
            for cp in mine + first:
                cp.start()

        def pass_on():
            for j, chip in enumerate(chips):
                for a in range(n):
                    copy(a, 1 + j, (*chip, c), me).wait_recv()
                    passed[j * n + a].start()

        def finish():
            for a in range(n):
                copy(a, 0, sibling, me).wait_recv()
                for j, chip in enumerate(chips):
                    copy(a, 4 + j, (*chip, 1 - c), me).wait_recv()
            for cp in first + passed:
                cp.wait_send()
            for cp in mine:
                cp.wait()

        return [send, pass_on, finish]

    return list(arrs), [_sds((N_DEV,) + a.shape, a.dtype) for a in arrs], _exchange_scratch(n), phases


def _scatter_exchange(arrs):
    n = len(arrs)
    flips = [(fx, fy, fc) for fx in (0, 1) for fy in (0, 1) for fc in (0, 1)][1:]

    def phases(ins, outs, send_sems, recv_sems, local_sems):
        x, y, c = _place()
        mine = _slot(x, y, c)
        local = [pltpu.make_async_copy(ins[a].at[mine], outs[a].at[mine], local_sems.at[a]) for a in range(n)]
        peers = [((1 - x) if fx else x, (1 - y) if fy else y, (1 - c) if fc else c) for fx, fy, fc in flips]

        def copy(a, kk, src_slot, dst_slot):
            return pltpu.make_async_remote_copy(
                src_ref=ins[a].at[src_slot], dst_ref=outs[a].at[dst_slot],
                send_sem=send_sems.at[a * 7 + kk], recv_sem=recv_sems.at[a * 7 + kk],
                device_id=peers[kk], device_id_type=MESH)

        sends = [copy(a, kk, _slot(*peers[kk]), mine) for a in range(n) for kk in range(7)]

        def send():
            for cp in local + sends:
                cp.start()

        def finish():
            for a in range(n):
                for kk in range(7):
                    copy(a, kk, mine, _slot(*peers[kk])).wait_recv()
            for cp in sends:
                cp.wait_send()
            for cp in local:
                cp.wait()

        return [send, finish]

    return list(arrs), [_sds(a.shape, a.dtype) for a in arrs], _exchange_scratch(n), phases


def _pair_exchange(arrs):
    n = len(arrs)

    def phases(ins, outs, send_sems, recv_sems, local_sems):
        x, y, c = _place()

        def copy(a, chip, side):
            return pltpu.make_async_remote_copy(
                src_ref=ins[a].at[chip, side], dst_ref=outs[a].at[chip],
                send_sem=send_sems.at[a * 7 + chip], recv_sem=recv_sems.at[a * 7 + chip],
                device_id=(x, y, 1 - c), device_id_type=MESH)

        sends = [copy(a, chip, 1 - c) for a in range(n) for chip in range(4)]

        def send():
            for cp in sends:
                cp.start()

        def finish():
            for a in range(n):
                for chip in range(4):
                    copy(a, chip, c).wait_recv()
            for cp in sends:
                cp.wait_send()

        return [send, finish]

    return list(arrs), [_sds((4,) + a.shape[2:], a.dtype) for a in arrs], _exchange_scratch(n), phases


def _chip_exchange(arrs):
    n = len(arrs)

    def phases(ins, outs, send_sems, recv_sems, local_sems):
        x, y, c = _place()
        mine = 2 * x + y
        chips = [(1 - x, y), (x, 1 - y), (1 - x, 1 - y)]
        local = [pltpu.make_async_copy(ins[a].at[mine], outs[a].at[mine], local_sems.at[a]) for a in range(n)]

        def copy(a, j, src_slot, dst_slot):
            return pltpu.make_async_remote_copy(
                src_ref=ins[a].at[src_slot], dst_ref=outs[a].at[dst_slot],
                send_sem=send_sems.at[a * 7 + j], recv_sem=recv_sems.at[a * 7 + j],
                device_id=(*chips[j], c), device_id_type=MESH)

        sends = [copy(a, j, 2 * chips[j][0] + chips[j][1], mine) for a in range(n) for j in range(3)]

        def send():
            for cp in local + sends:
                cp.start()

        def finish():
            for a in range(n):
                for j in range(3):
                    copy(a, j, mine, 2 * chips[j][0] + chips[j][1]).wait_recv()
            for cp in sends:
                cp.wait_send()
            for cp in local:
                cp.wait()

        return [send, finish]

    return list(arrs), [_sds(a.shape, a.dtype) for a in arrs], _exchange_scratch(n), phases


def _pair_sum(name, mine, theirs):
    _, _, r, c = mine.shape

    def body(side_ref, m_ref, t_ref, o_ref):
        o_ref[...] = (m_ref[...].astype(F32) + t_ref[...].astype(F32)).astype(o_ref.dtype)

    return pl.pallas_call(
        body, name=name,
        grid_spec=pltpu.PrefetchScalarGridSpec(
            num_scalar_prefetch=1, grid=(4,),
            in_specs=[pl.BlockSpec((None, None, r, c), lambda j, side: (j, side[0], 0, 0)),
                      pl.BlockSpec((None, r, c), lambda j, side: (j, 0, 0))],
            out_specs=pl.BlockSpec((None, r, c), lambda j, side: (j, 0, 0))),
        out_shape=_sds(theirs.shape, theirs.dtype),
        compiler_params=_params(("parallel",)),
    )(lax.axis_index("c").astype(jnp.int32).reshape(1), mine, theirs)


def _exchange_call(name, exchange):
    arrs, out_shape, scratch, phases = exchange
    n = len(arrs)

    def body(*refs):
        for step in phases(refs[:n], refs[n:2 * n], *refs[2 * n:]):
            step()

    return pl.pallas_call(body, name=name, in_specs=[ANY] * n, out_specs=[ANY] * n,
                          out_shape=out_shape, scratch_shapes=scratch)(*arrs)


def _carry_exchange(exchange, refs, n_in, n_out, first, middle, last):
    arrs, _, _, phases = exchange
    n = len(arrs)
    if n == 0:
        return lambda: None
    ins = refs[n_in:n_in + n]
    outs = refs[n_in + n + n_out:n_in + 2 * n + n_out]
    sems = n_in + 2 * n + n_out
    steps = phases(ins, outs, *refs[sems:sems + 3])
    pl.when(first)(steps[0])
    if len(steps) == 3:
        pl.when(middle)(steps[1])
    return lambda: pl.when(last)(steps[-1])


def _adamw_math(w, g, m, v):
    m2 = ADAM_B1 * m + (1.0 - ADAM_B1) * g
    v2 = ADAM_B2 * v + (1.0 - ADAM_B2) * jnp.square(g)
    m_hat = m2 / (1.0 - ADAM_B1 ** ADAM_STEP)
    v_hat = v2 / (1.0 - ADAM_B2 ** ADAM_STEP)
    delta = -ADAM_LR * (m_hat / (jnp.sqrt(v_hat) + ADAM_EPS) + ADAM_WD * w)
    return delta, m2, v2


def _sum_adamw_tile(parts, w, m, v):
    g = parts[0].astype(F32)
    for d in range(1, parts.shape[0]):
        g = g + parts[d].astype(F32)
    return (g, *_adamw_math(w, g, m, v))


def _sum_adamw(name, parts, w, m, v, tr=256):
    p, r, c = parts.shape
    tr = _pick(r, tr, 16)

    def body(p_ref, w_ref, m_ref, v_ref, g_ref, d_ref, m2_ref, v2_ref):
        g_ref[...], d_ref[...], m2_ref[...], v2_ref[...] = _sum_adamw_tile(p_ref[...], w_ref[...], m_ref[...], v_ref[...])

    tile = pl.BlockSpec((tr, c), lambda i: (i, 0))
    return pl.pallas_call(
        body, name=name, grid=(r // tr,),
        in_specs=[pl.BlockSpec((p, tr, c), lambda i: (0, i, 0)), tile, tile, tile],
        out_specs=[tile] * 4, out_shape=[_sds((r, c))] * 4,
        compiler_params=_params(("parallel",)),
    )(parts, w, m, v)


def _sum_parts(name, parts):
    p, r, c = parts.shape

    def body(p_ref, o_ref):
        g = p_ref[0]
        for d in range(1, p):
            g = g + p_ref[d]
        o_ref[...] = g

    return pl.pallas_call(
        body, name=name, out_shape=_sds((r, c)),
        in_specs=[pl.BlockSpec(memory_space=pltpu.VMEM)], out_specs=pl.BlockSpec(memory_space=pltpu.VMEM),
    )(parts)


WEIGHTS = ["norm_mix_pre", "w_in", "conv_dw_w", "conv_dw_b", "conv_ln_g", "conv_ln_b", "w_conv_branch",
           "b_conv_branch", "w_att_branch", "w_out", "norm_mix_post", "norm_ffn_pre", "w_ffn_up", "w_ffn_down",
           "norm_ffn_post"]
COL_SHARDED = ["w_conv_branch", "w_att_branch"]
ROW_SHARDED = ["w_out", "w_ffn_down"]
TRANSPOSED = ["w_in", "w_ffn_up"]
VECTORS = ["norm_mix_pre", "conv_dw_b", "conv_ln_g", "conv_ln_b", "b_conv_branch", "norm_mix_post",
           "norm_ffn_pre", "norm_ffn_post"]


def _cols_to_full(g):
    return g.transpose(1, 0, 2).reshape(g.shape[1], N_DEV * g.shape[2])


def _full_to_cols(f):
    return f.reshape(f.shape[0], N_DEV, f.shape[1] // N_DEV).transpose(1, 0, 2)


PACK_ROWS = 7


def _pack_vectors(vecs, extra=None):
    parts = [vecs[nm].reshape(-1) for nm in VECTORS]
    parts.append(jnp.zeros((1,), F32) if extra is None else extra.reshape(1))
    used = sum(p.size for p in parts)
    parts.append(jnp.zeros((PACK_ROWS * D_MODEL - used,), F32))
    return jnp.concatenate(parts).reshape(PACK_ROWS, D_MODEL)


def _unpack_vectors(packed, sizes):
    flat, out, at = packed.reshape(-1), {}, 0
    for nm in VECTORS:
        out[nm] = flat[at:at + sizes[nm]]
        at += sizes[nm]
    return out, flat[at]


def kernel(x, norm_mix_pre, w_in, conv_dw_w, conv_dw_b, conv_ln_g, conv_ln_b, w_conv_branch, b_conv_branch, w_att_branch, w_out, norm_mix_post, norm_ffn_pre, w_ffn_up, w_ffn_down, norm_ffn_post, loss_target, m_norm_mix_pre, m_w_in, m_conv_dw_w, m_conv_dw_b, m_conv_ln_g, m_conv_ln_b, m_w_conv_branch, m_b_conv_branch, m_w_att_branch, m_w_out, m_norm_mix_post, m_norm_ffn_pre, m_w_ffn_up, m_w_ffn_down, m_norm_ffn_post, v_norm_mix_pre, v_w_in, v_conv_dw_w, v_conv_dw_b, v_conv_ln_g, v_conv_ln_b, v_w_conv_branch, v_b_conv_branch, v_w_att_branch, v_w_out, v_norm_mix_post, v_norm_ffn_pre, v_w_ffn_up, v_w_ffn_down, v_norm_ffn_post):
    ws = dict(zip(WEIGHTS, [norm_mix_pre, w_in, conv_dw_w, conv_dw_b, conv_ln_g, conv_ln_b, w_conv_branch,
                            b_conv_branch, w_att_branch, w_out, norm_mix_post, norm_ffn_pre, w_ffn_up, w_ffn_down,
                            norm_ffn_post]))
    ms = dict(zip(WEIGHTS, [m_norm_mix_pre, m_w_in, m_conv_dw_w, m_conv_dw_b, m_conv_ln_g, m_conv_ln_b,
                            m_w_conv_branch, m_b_conv_branch, m_w_att_branch, m_w_out, m_norm_mix_post,
                            m_norm_ffn_pre, m_w_ffn_up, m_w_ffn_down, m_norm_ffn_post]))
    vs = dict(zip(WEIGHTS, [v_norm_mix_pre, v_w_in, v_conv_dw_w, v_conv_dw_b, v_conv_ln_g, v_conv_ln_b,
                            v_w_conv_branch, v_b_conv_branch, v_w_att_branch, v_w_out, v_norm_mix_post,
                            v_norm_ffn_pre, v_w_ffn_up, v_w_ffn_down, v_norm_ffn_post]))

    dw_block = jnp.pad(conv_dw_w, ((0, 1), (0, 0)))
    g_in, g_dw = _exchange_call("gather_first", _gather_exchange([w_in.T.astype(BF16), dw_block]))
    full = {"w_in": _full_weight("w_in", g_in), "conv_dw_w": _cols_to_full(g_dw)}
    for nm in VECTORS:
        full[nm] = ws[nm].reshape(1, -1)

    def as_kept(nm, a):
        return a.T if nm in TRANSPOSED else a

    ride_along = ["w_ffn_up", "w_out", "w_conv_branch", "w_att_branch"]
    loss_local, grad_x, received, updated, grads = _local_step(
        x[0], loss_target[0], full, {nm: as_kept(nm, ws[nm]).astype(BF16) for nm in LATE},
        {nm: tuple(as_kept(nm, a[nm]) for a in (ws, ms, vs)) for nm in ride_along})

    small = _exchange_call("gather_small_grads", _gather_exchange(
        [_pack_vectors(grads, extra=loss_local), grads["conv_dw_w"]]))
    out_g, out_d, out_m, out_v = {}, {}, {}, {}
    for nm in LATE + ["w_in"]:
        res = updated[nm] if nm in updated else _sum_adamw(
            "adamw_" + nm, received[nm], *[as_kept(nm, a[nm]) for a in (ws, ms, vs)])
        out_g[nm], out_d[nm], out_m[nm], out_v[nm] = [as_kept(nm, r) for r in res]
    sizes = {nm: ws[nm].size for nm in VECTORS}
    vec = _sum_adamw("adamw_vectors", small[0], _pack_vectors(ws), _pack_vectors(ms), _pack_vectors(vs))
    for res, dst in zip(vec, (out_g, out_d, out_m, out_v)):
        dst.update(_unpack_vectors(res, sizes)[0])
    loss = _unpack_vectors(vec[0], sizes)[1]
    dw_full = _sum_parts("sum_dw_grads", small[1])
    me = _slot(*_place())
    dw_mine = lax.dynamic_slice(dw_full, (0, me * (CONV_DIM // N_DEV)), (CONV_WIDTH, CONV_DIM // N_DEV))
    nm = "conv_dw_w"
    out_g[nm], out_d[nm], out_m[nm], out_v[nm] = _sum_adamw("adamw_dw", dw_mine[None], ws[nm], ms[nm], vs[nm])

    outs = [loss, grad_x[None]]
    for group in (out_g, out_d, out_m, out_v):
        outs += [group[nm] for nm in WEIGHTS]
    return tuple(outs)
```

```python
import math

import jax
import jax.numpy as jnp
from jax import lax
from jax.experimental import pallas as pl
from jax.experimental.pallas import tpu as pltpu

F32 = jnp.float32
BF16 = jnp.bfloat16

N_DEV = 8
D_MODEL = 1024
CONV_DIM = 512
CONV_WIDTH = 31
N_HEADS = 8
HEAD_DIM = 64
ATT_DIM = N_HEADS * HEAD_DIM
D_FF = 2816
EPS = 1e-6
IN_SPLITS = (0, 1024, 1536, 2048, 2560, 3584, 4608)

ADAM_LR = 0.001
ADAM_B1 = 0.9
ADAM_B2 = 0.999
ADAM_EPS = 1e-08
ADAM_WD = 0.01
ADAM_STEP = 10

LANES = 128
SUBLANES = 8
HALO = 32
ATT_TILE = 256
ATT_PART = 192
DEAD_SUM = -120.0
VMEM_LIMIT = 56 * 1024 * 1024
MESH = pl.DeviceIdType.MESH
ANY = pl.BlockSpec(memory_space=pl.ANY)


def _pick(dim, target, align=LANES):
    t = min(dim, target)
    t -= t % align
    while t >= align:
        if dim % t == 0:
            return t
        t -= align
    return dim


def _params(semantics):
    return pltpu.CompilerParams(dimension_semantics=semantics, vmem_limit_bytes=VMEM_LIMIT)


def _matmul(a, b, *, name, ta=False, tb=False, out_dtype=F32):
    m, k = (a.shape[1], a.shape[0]) if ta else a.shape
    n, k2 = b.shape if tb else (b.shape[1], b.shape[0])
    assert k == k2, (a.shape, b.shape, ta, tb)
    tm, tn, tk = _pick(m, 1408 if ta else 512), _pick(n, 1536), _pick(k, 1536)
    nk = k // tk
    dims = (((0 if ta else 1,), (1 if tb else 0,)), ((), ()))

    def body(a_ref, b_ref, o_ref, *acc):
        part = lax.dot_general(a_ref[...], b_ref[...], dims, preferred_element_type=F32)
        if nk == 1:
            o_ref[...] = part.astype(o_ref.dtype)
            return
        acc_ref, = acc
        kk = pl.program_id(2)

        @pl.when(kk == 0)
        def _():
            acc_ref[...] = part

        @pl.when((kk > 0) & (kk < nk - 1))
        def _():
            acc_ref[...] += part

        @pl.when(kk == nk - 1)
        def _():
            o_ref[...] = (acc_ref[...] + part).astype(o_ref.dtype)

    a_spec = pl.BlockSpec((tk, tm), lambda j, i, kk: (kk, i)) if ta else pl.BlockSpec((tm, tk), lambda j, i, kk: (i, kk))
    b_spec = (pl.BlockSpec((tn, tk), lambda j, i, kk: (j, kk)) if tb
              else pl.BlockSpec((tk, tn), lambda j, i, kk: (kk, j)))
    return pl.pallas_call(
        body, name=name, grid=(n // tn, m // tm, nk),
        in_specs=[a_spec, b_spec],
        out_specs=pl.BlockSpec((tm, tn), lambda j, i, kk: (i, j)),
        out_shape=jax.ShapeDtypeStruct((m, n), out_dtype),
        scratch_shapes=[pltpu.VMEM((tm, tn), F32)] if nk > 1 else [],
        compiler_params=_params(("parallel", "parallel", "arbitrary")),
    )(a, b)


def _pieces_tn_matmul(pieces, b, *, name, tj=512):
    s, n = b.shape
    counts = [p.shape[1] // tj for p in pieces]
    starts = [sum(counts[:i]) for i in range(len(pieces))]
    assert all(p.shape == (s, c * tj) for p, c in zip(pieces, counts))

    def body(*refs):
        b_ref, o_ref = refs[len(pieces):]
        j = pl.program_id(0)
        for p_ref, first, count in zip(refs, starts, counts):
            @pl.when((j >= first) & (j < first + count))
            def _():
                o_ref[...] = lax.dot_general(p_ref[...].astype(BF16), b_ref[...], TN,
                                             preferred_element_type=F32).astype(o_ref.dtype)

    def piece_spec(first, count):
        return pl.BlockSpec((s, tj), lambda j: (0, jnp.clip(j - first, 0, count - 1)))

    return pl.pallas_call(
        body, name=name, grid=(sum(counts),),
        in_specs=[piece_spec(f, c) for f, c in zip(starts, counts)]
        + [pl.BlockSpec((s, n), lambda j: (0, 0), pipeline_mode=pl.Buffered(1))],
        out_specs=pl.BlockSpec((tj, n), lambda j: (j, 0)),
        out_shape=jax.ShapeDtypeStruct((sum(counts) * tj, n), BF16),
        compiler_params=_params(("arbitrary",)),
    )(*pieces, b)


NO_EXCHANGE = ([], [], [], None)


def _sweep_marks(nt):
    i = pl.program_id(0)
    return i == 0, i == (3 * nt) // 4, i == nt - 1


def _rowwise(name, fn, rows, bcasts, row_outs, red_outs=(), tm=256, exchange=NO_EXCHANGE):
    s = rows[0].shape[0]
    tm = _pick(s, tm, 16)
    nt = s // tm
    resident = pl.Buffered(1)
    nr, nb, no, nd = len(rows), len(bcasts), len(row_outs), len(red_outs)
    x_arrs, x_shape, x_scratch, _ = exchange
    nx = len(x_arrs)
    first_out = nr + nb + nx

    def body(*refs):
        finish_exchange = _carry_exchange(exchange, refs, nr + nb, no + nd, *_sweep_marks(nt))
        ins = [r[...] for r in refs[:nr + nb]]
        outs, reds = fn(*ins)
        for ref, val in zip(refs[first_out:first_out + no], outs):
            ref[...] = val.astype(ref.dtype)
        i = pl.program_id(0)
        for ref, val in zip(refs[first_out + no:first_out + no + nd], reds):
            @pl.when(i == 0)
            def _():
                ref[...] = val

            @pl.when(i > 0)
            def _():
                ref[...] += val
        finish_exchange()

    def row_spec(a):
        assert a.shape[-2] % nt == 0, (name, a.shape, nt)
        if len(a.shape) == 3:
            return pl.BlockSpec((a.shape[0], a.shape[1] // nt, a.shape[2]), lambda i: (0, i, 0))
        return pl.BlockSpec((a.shape[0] // nt, a.shape[1]), lambda i: (i, 0))

    in_specs = [row_spec(r) for r in rows]
    in_specs += [pl.BlockSpec(b.shape, lambda i: (0, 0), pipeline_mode=resident) for b in bcasts]
    out_specs = [row_spec(o) for o in row_outs]
    out_specs += [pl.BlockSpec(d.shape, lambda i: (0, 0)) for d in red_outs]
    return pl.pallas_call(
        body, name=name, grid=(nt,), in_specs=in_specs + [ANY] * nx, out_specs=out_specs + [ANY] * nx,
        out_shape=list(row_outs) + list(red_outs) + x_shape, scratch_shapes=x_scratch,
        compiler_params=_params(("arbitrary",)),
    )(*rows, *bcasts, *x_arrs)


def _sds(shape, dtype=F32):
    return jax.ShapeDtypeStruct(shape, dtype)


def _rms(x, g):
    y = x * lax.rsqrt(jnp.mean(x * x, axis=-1, keepdims=True) + EPS)
    return y * g


def _silu(x):
    return x * jax.nn.sigmoid(x)


def _swiglu(g, u):
    return _silu(g) * u


def _ln_silu(u, g, b):
    mu = jnp.mean(u, axis=-1, keepdims=True)
    var = jnp.mean(jnp.square(u - mu), axis=-1, keepdims=True)
    return _silu((u - mu) * lax.rsqrt(var + EPS) * g + b)


def _merge(conv_pre, att_out, g_conv, g_att, b_cb):
    return jax.nn.sigmoid(g_conv) * (conv_pre + b_cb) + jax.nn.sigmoid(g_att) * att_out


def _glu(t):
    return t[:, :CONV_DIM] * jax.nn.sigmoid(t[:, CONV_DIM:])


def _shifted_reader(buf, shifted, tm):
    for b in range(1, SUBLANES):
        shifted[b - 1, :, :] = buf[pl.ds(b, tm + HALO - SUBLANES), :]

    def read(o):
        a, b = divmod(o, SUBLANES)
        return buf[pl.ds(SUBLANES * a, tm), :] if b == 0 else shifted[b - 1, pl.ds(SUBLANES * a, tm), :]

    return read


def _conv_fwd(conv_in, w_pad, b, ln_g, ln_b, exchange, tm=256):
    s = conv_in.shape[0]
    tm = _pick(s, tm, HALO)
    ratio = tm // HALO
    x_arrs, x_shape, x_scratch, _ = exchange
    nx = len(x_arrs)

    def body(*refs):
        main_ref, halo_ref, w_ref, b_ref, g_ref, be_ref = refs[:6]
        u3_ref, u1_ref = refs[6 + nx:8 + nx]
        buf, shifted = refs[-2:]
        finish_exchange = _carry_exchange(exchange, refs, 6, 2, *_sweep_marks(s // tm))
        i = pl.program_id(0)
        buf[0:HALO, :] = _glu(halo_ref[...]) * (i > 0).astype(F32)
        buf[HALO:HALO + tm, :] = _glu(main_ref[...])
        read = _shifted_reader(buf, shifted, tm)
        acc = jnp.zeros((tm, CONV_DIM), F32) + b_ref[...]
        for j in range(CONV_WIDTH):
            acc = acc + w_ref[j:j + 1, :] * read(HALO - (CONV_WIDTH - 1) + j)
        u1_ref[...] = acc
        u3_ref[...] = _ln_silu(acc, g_ref[...], be_ref[...]).astype(u3_ref.dtype)
        finish_exchange()

    res = pl.pallas_call(
        body, name="conv_fwd", grid=(s // tm,),
        in_specs=[pl.BlockSpec((tm, 2 * CONV_DIM), lambda i: (i, 0)),
                  pl.BlockSpec((HALO, 2 * CONV_DIM), lambda i: (jnp.maximum(i * ratio - 1, 0), 0)),
                  pl.BlockSpec(w_pad.shape, lambda i: (0, 0)),
                  pl.BlockSpec(b.shape, lambda i: (0, 0)),
                  pl.BlockSpec(ln_g.shape, lambda i: (0, 0)),
                  pl.BlockSpec(ln_b.shape, lambda i: (0, 0))] + [ANY] * nx,
        out_specs=[pl.BlockSpec((tm, CONV_DIM), lambda i: (i, 0)),
                   pl.BlockSpec((tm, CONV_DIM), lambda i: (i, 0))] + [ANY] * nx,
        out_shape=[_sds((s, CONV_DIM), BF16), _sds((s, CONV_DIM), F32)] + x_shape,
        scratch_shapes=x_scratch + [pltpu.VMEM((tm + HALO, CONV_DIM), F32),
                                    pltpu.VMEM((SUBLANES - 1, tm + HALO - SUBLANES, CONV_DIM), F32)],
        compiler_params=_params(("arbitrary",)),
    )(conv_in, conv_in, w_pad, b, ln_g, ln_b, *x_arrs)
    return res[0], res[1], res[2:]


def _conv_bwd(conv_in, u1, du3, ln_g, ln_b, w_pad, exchange, tm=256):
    s = conv_in.shape[0]
    tm = _pick(s, tm, HALO)
    ratio = tm // HALO
    nt = s // tm
    last_halo = s // HALO - 1
    x_arrs, x_shape, x_scratch, _ = exchange
    nx = len(x_arrs)

    def body(*refs):
        main_ref, halo_ref, u1_ref, u1n_ref, du3_ref, du3n_ref, g_ref, be_ref, w_ref = refs[:9]
        dci_ref, dw_ref, db_ref, dg_ref, dbe_ref = refs[9 + nx:14 + nx]
        ubuf, dbuf, ushift, dshift = refs[-4:]
        finish_exchange = _carry_exchange(exchange, refs, 9, 5, *_sweep_marks(nt))
        i = pl.program_id(0)
        main = main_ref[...]
        a = main[:, :CONV_DIM]
        sb = jax.nn.sigmoid(main[:, CONV_DIM:])
        ubuf[0:HALO, :] = _glu(halo_ref[...]) * (i > 0).astype(F32)
        ubuf[HALO:HALO + tm, :] = a * sb

        def ln_bwd(u1t, du3t):
            _, vjp = jax.vjp(_ln_silu, u1t, g_ref[...], be_ref[...])
            return vjp(du3t)

        du, dg, dbe = ln_bwd(u1_ref[...], du3_ref[...])
        dbuf[0:tm, :] = du
        dbuf[tm:tm + HALO, :] = ln_bwd(u1n_ref[...], du3n_ref[...])[0] * (i < nt - 1).astype(F32)

        @pl.when(i == 0)
        def _():
            dw_ref[...] = jnp.zeros_like(dw_ref)
            db_ref[...] = jnp.zeros_like(db_ref)
            dg_ref[...] = jnp.zeros_like(dg_ref)
            dbe_ref[...] = jnp.zeros_like(dbe_ref)

        dg_ref[...] += dg
        dbe_ref[...] += dbe

        read_u = _shifted_reader(ubuf, ushift, tm)
        read_d = _shifted_reader(dbuf, dshift, tm)
        du0 = jnp.zeros((tm, CONV_DIM), F32)
        for j in range(CONV_WIDTH):
            du0 = du0 + w_ref[j:j + 1, :] * read_d(CONV_WIDTH - 1 - j)
            dw_ref[j:j + 1, :] += jnp.sum(du * read_u(HALO - (CONV_WIDTH - 1) + j), axis=0, keepdims=True)
        db_ref[...] += jnp.sum(du, axis=0, keepdims=True)
        dci_ref[:, :CONV_DIM] = (du0 * sb).astype(dci_ref.dtype)
        dci_ref[:, CONV_DIM:] = (du0 * a * sb * (1.0 - sb)).astype(dci_ref.dtype)
        finish_exchange()

    res = pl.pallas_call(
        body, name="conv_bwd", grid=(nt,),
        in_specs=[pl.BlockSpec((tm, 2 * CONV_DIM), lambda i: (i, 0)),
                  pl.BlockSpec((HALO, 2 * CONV_DIM), lambda i: (jnp.maximum(i * ratio - 1, 0), 0))]
        + [pl.BlockSpec((tm, CONV_DIM), lambda i: (i, 0)),
           pl.BlockSpec((HALO, CONV_DIM), lambda i: (jnp.minimum((i + 1) * ratio, last_halo), 0))] * 2
        + [pl.BlockSpec((1, CONV_DIM), lambda i: (0, 0))] * 2 + [pl.BlockSpec(w_pad.shape, lambda i: (0, 0))]
        + [ANY] * nx,
        out_specs=[pl.BlockSpec((tm, 2 * CONV_DIM), lambda i: (i, 0)),
                   pl.BlockSpec(w_pad.shape, lambda i: (0, 0))]
        + [pl.BlockSpec((1, CONV_DIM), lambda i: (0, 0))] * 3 + [ANY] * nx,
        out_shape=[_sds((s, 2 * CONV_DIM), BF16), _sds(w_pad.shape)] + [_sds((1, CONV_DIM))] * 3 + x_shape,
        scratch_shapes=x_scratch + [pltpu.VMEM((tm + HALO, CONV_DIM), F32)] * 2
        + [pltpu.VMEM((SUBLANES - 1, tm + HALO - SUBLANES, CONV_DIM), F32)] * 2,
        compiler_params=_params(("arbitrary",)),
    )(conv_in, conv_in, u1, u1, du3, du3, ln_g, ln_b, w_pad, *x_arrs)
    return res[:5], res[5:]


def _logsig_neg(z):
    return jnp.minimum(-z, 0.0) - jnp.log(1.0 + jnp.exp(-jnp.abs(z)))


def _split_dot(val, tri):
    hi = val.astype(BF16)
    lo = (val - hi.astype(F32)).astype(BF16)
    return jnp.dot(hi, tri, preferred_element_type=F32) + jnp.dot(lo, tri, preferred_element_type=F32)


def _attn_masks(t, later):
    row = lax.broadcasted_iota(jnp.int32, (t, t), 0)
    col = lax.broadcasted_iota(jnp.int32, (t, t), 1)
    tri = jnp.where(row > col if later else row <= col, 1.0, 0.0).astype(BF16)
    return col < row, tri


def _grid_marks(h, nq):
    hh, i = pl.program_id(0), pl.program_id(1)
    return (hh == 0) & (i == 0), (hh == (3 * h) // 4) & (i == 0), (hh == h - 1) & (i == nq - 1)


def _head_masks(shape):
    lane = lax.broadcasted_iota(jnp.int32, shape, len(shape) - 1)
    return lane < HEAD_DIM, lane >= HEAD_DIM


def _per_head(blk):
    m0, m1 = _head_masks(blk.shape)
    zero = jnp.zeros_like(blk)
    return jnp.where(m0, blk, zero), jnp.where(m1, blk, zero)


NT = (((1,), (1,)), ((), ()))
TN = (((0,), (0,)), ((), ()))


def _with_top(whole, top):
    rows = top.shape[0]
    return top if rows == whole.shape[0] else jnp.concatenate([top, whole[rows:]], axis=0)


def _attn_fwd(q, k, v, exchange):
    s = q.shape[0]
    hp = q.shape[1] // LANES
    t = ATT_TILE
    scale = 1.0 / math.sqrt(HEAD_DIM)
    x_arrs, x_shape, x_scratch, _ = exchange
    nx = len(x_arrs)

    def body(*refs):
        q_ref, k_ref, v_ref = refs[:3]
        o_ref, lt_ref, nb_ref = refs[3 + nx:6 + nx]
        finish_exchange = _carry_exchange(exchange, refs, 3, 3, *_grid_marks(hp, s // t))
        i = pl.program_id(1)
        qs = _per_head((q_ref[...].astype(F32) * scale).astype(BF16))
        causal, tri = _attn_masks(t, later=True)

        def step(kb, carry, masked, rows):
            cs, acc = carry
            off = pl.multiple_of(kb * t, t)
            kblk = k_ref[pl.ds(off, t), :]
            vs = _per_head(v_ref[pl.ds(off, t), :])
            acc_top = acc[:rows]
            new_cs = []
            for hd in range(2):
                z = lax.dot_general(qs[hd][:rows], kblk, NT, preferred_element_type=F32)
                l = _logsig_neg(z)
                if masked:
                    l = jnp.where(causal, l, 0.0)
                e = z + l + _split_dot(l, tri) + cs[hd][:rows]
                if masked:
                    e = jnp.where(causal, e, -1e30)
                acc_top = acc_top + jnp.dot(jnp.exp(e).astype(BF16), vs[hd], preferred_element_type=F32)
                new_cs.append(_with_top(cs[hd], cs[hd][:rows] + jnp.sum(l, axis=1, keepdims=True)))
            return tuple(new_cs), _with_top(acc, acc_top)

        zero = jnp.zeros((t, 1), F32)
        carry = step(i, ((zero, zero), jnp.zeros((t, LANES), F32)), True, t)

        def live(cs, lo, hi):
            return jnp.maximum(jnp.max(cs[0][lo:hi]), jnp.max(cs[1][lo:hi])) > DEAD_SUM

        def more(state):
            n, _, (cs, _) = state
            return (n < i) & live(cs, 0, t)

        def sweep(state):
            n, n_full, cr = state
            whole = live(cr[0], ATT_PART, t)
            cr = lax.cond(whole, lambda c: step(i - 1 - n, c, False, t), lambda c: step(i - 1 - n, c, False, ATT_PART), cr)
            return n + 1, n_full + whole.astype(jnp.int32), cr

        n_blocks, n_full, carry = lax.while_loop(more, sweep, (jnp.int32(0), jnp.int32(0), carry))
        m0, _ = _head_masks((t, LANES))
        lt_ref[...] = jnp.where(m0, carry[0][0], carry[0][1])
        o_ref[...] = carry[1].astype(o_ref.dtype)
        nb_ref[0, pl.program_id(0), i] = n_blocks.astype(F32)
        nb_ref[1, pl.program_id(0), i] = n_full.astype(F32)
        finish_exchange()

    res = pl.pallas_call(
        body, name="attn_fwd", grid=(hp, s // t),
        in_specs=[pl.BlockSpec((t, LANES), lambda p, i: (i, p)),
                  pl.BlockSpec((s, LANES), lambda p, i: (0, p)),
                  pl.BlockSpec((s, LANES), lambda p, i: (0, p))] + [ANY] * nx,
        out_specs=[pl.BlockSpec((t, LANES), lambda p, i: (i, p)),
                   pl.BlockSpec((None, t, LANES), lambda p, i: (p, i, 0)),
                   pl.BlockSpec(memory_space=pltpu.SMEM)] + [ANY] * nx,
        out_shape=[_sds(q.shape, BF16), _sds((hp, s, LANES), F32), _sds((2, hp, s // t), F32)] + x_shape,
        scratch_shapes=x_scratch,
        compiler_params=_params(("arbitrary", "arbitrary")),
    )(q, k, v, *x_arrs)
    return res[0], res[1], res[2], res[3:]


def _attn_bwd(q, k, v, do, ltot, n_blocks, exchange):
    s = q.shape[0]
    hp = q.shape[1] // LANES
    t = ATT_TILE
    scale = 1.0 / math.sqrt(HEAD_DIM)
    x_arrs, x_shape, x_scratch, _ = exchange
    nx = len(x_arrs)

    def body(*refs):
        q_ref, k_ref, v_ref, do_ref, lt_ref, nb_ref = refs[:6]
        dq_ref, dk_ref, dv_ref = refs[6 + nx:9 + nx]
        finish_exchange = _carry_exchange(exchange, refs, 6, 3, *_grid_marks(hp, s // t))
        i = pl.program_id(1)
        n_blocks = jnp.clip(nb_ref[0, pl.program_id(0), i].astype(jnp.int32), 0, i)
        n_full = jnp.clip(nb_ref[1, pl.program_id(0), i].astype(jnp.int32), 0, n_blocks)

        @pl.when(i == 0)
        def _():
            dk_ref[...] = jnp.zeros_like(dk_ref)
            dv_ref[...] = jnp.zeros_like(dv_ref)

        qb = q_ref[...]
        qm = _per_head(qb)
        qs = _per_head((qb.astype(F32) * scale).astype(BF16))
        dos = _per_head(do_ref[...])
        lts = (lt_ref[:, 0:1], lt_ref[:, HEAD_DIM:HEAD_DIM + 1])
        causal, tri = _attn_masks(t, later=False)

        def step(kb, carry, masked, rows):
            cls, cgs, dq = carry
            off = pl.multiple_of(kb * t, t)
            kblk = k_ref[pl.ds(off, t), :]
            vblk = v_ref[pl.ds(off, t), :]
            ks = _per_head(kblk)
            dq_top = dq[:rows]
            dk = jnp.zeros((t, LANES), F32)
            dv = jnp.zeros((t, LANES), F32)
            new_cls, new_cgs = [], []
            for hd in range(2):
                z = lax.dot_general(qs[hd][:rows], kblk, NT, preferred_element_type=F32)
                l = _logsig_neg(z)
                if masked:
                    l = jnp.where(causal, l, 0.0)
                e = z + l + ((lts[hd][:rows] - cls[hd][:rows]) - _split_dot(l, tri))
                if masked:
                    e = jnp.where(causal, e, -1e30)
                a = jnp.exp(e)
                g = lax.dot_general(dos[hd][:rows], vblk, NT, preferred_element_type=F32) * a
                p = cgs[hd][:rows] + jnp.dot(g.astype(BF16), tri, preferred_element_type=F32) - g
                el = jnp.exp(l)
                dz = g * el - p * (1.0 - el)
                if masked:
                    dz = jnp.where(causal, dz, 0.0)
                dzb = (dz * scale).astype(BF16)
                dq_top = dq_top + jnp.dot(dzb, ks[hd], preferred_element_type=F32)
                dk = dk + lax.dot_general(dzb, qm[hd][:rows], TN, preferred_element_type=F32)
                dv = dv + lax.dot_general(a.astype(BF16), dos[hd][:rows], TN, preferred_element_type=F32)
                new_cls.append(_with_top(cls[hd], cls[hd][:rows] + jnp.sum(l, axis=1, keepdims=True)))
                new_cgs.append(_with_top(cgs[hd], cgs[hd][:rows] + jnp.sum(g, axis=1, keepdims=True)))
            dk_ref[pl.ds(off, t), :] += dk
            dv_ref[pl.ds(off, t), :] += dv
            return tuple(new_cls), tuple(new_cgs), _with_top(dq, dq_top)

        zero = jnp.zeros((t, 1), F32)
        init = ((zero, zero), (zero, zero), jnp.zeros((t, LANES), F32))
        carry = lax.fori_loop(i - n_blocks, i - n_full, lambda kb, cr: step(kb, cr, False, ATT_PART), init)
        carry = lax.fori_loop(i - n_full, i, lambda kb, cr: step(kb, cr, False, t), carry)
        carry = step(i, carry, True, t)
        dq_ref[...] = carry[2]
        finish_exchange()

    blk = pl.BlockSpec((t, LANES), lambda p, i: (i, p))
    whole = pl.BlockSpec((s, LANES), lambda p, i: (0, p))
    res = pl.pallas_call(
        body, name="attn_bwd", grid=(hp, s // t),
        in_specs=[blk, whole, whole, blk, pl.BlockSpec((None, t, LANES), lambda p, i: (p, i, 0)),
                  pl.BlockSpec(memory_space=pltpu.SMEM)] + [ANY] * nx,
        out_specs=[blk, whole, whole] + [ANY] * nx,
        out_shape=[_sds(q.shape)] * 3 + x_shape,
        scratch_shapes=x_scratch,
        compiler_params=_params(("arbitrary", "arbitrary")),
    )(q, k, v, do, ltot, n_blocks, *x_arrs)
    return res[0], res[1], res[2], res[3:]


LATE = ["w_conv_branch", "w_att_branch", "w_out", "w_ffn_up", "w_ffn_down"]


def _full_weight(name, gathered):
    return _cols_to_full(gathered) if name in COL_SHARDED else gathered.reshape(-1, gathered.shape[2])


def _grad_slabs(name, grad):
    return _full_to_cols(grad) if name in COL_SHARDED else grad.reshape(N_DEV, -1, grad.shape[1])


def _side_slabs(name, grad):
    slabs = _grad_slabs(name, grad)
    return slabs.reshape((4, 2) + slabs.shape[1:])


def _local_step(x, target, w, late_blocks, opt):
    s = x.shape[0]
    w = dict(w)
    g1, g2, g3, g4 = w["norm_mix_pre"], w["norm_mix_post"], w["norm_ffn_pre"], w["norm_ffn_post"]

    w_in = w["w_in"]

    def proj_fn(xt, g1_, w_in_t):
        h = _rms(xt, g1_).astype(BF16)
        proj = lax.dot_general(h, w_in_t, NT, preferred_element_type=F32)
        return (h, *[proj[:, IN_SPLITS[n]:IN_SPLITS[n + 1]] for n in range(6)]), ()

    mix_weights = ["w_conv_branch", "w_att_branch", "w_out"]
    h1, conv_in, q, k, v, g_conv, g_att = _rowwise(
        "norm_proj", proj_fn, [x], [g1, w_in],
        [_sds((s, D_MODEL), BF16), _sds((s, 2 * CONV_DIM)), _sds((s, ATT_DIM), BF16), _sds((s, ATT_DIM), BF16),
         _sds((s, ATT_DIM), BF16), _sds((s, D_MODEL)), _sds((s, D_MODEL))], tm=512)

    u3, u1, gathered = _conv_fwd(conv_in, w["conv_dw_w"], w["conv_dw_b"], w["conv_ln_g"], w["conv_ln_b"],
                                 _gather_exchange([late_blocks[nm] for nm in mix_weights]))
    for nm, g in zip(mix_weights, gathered):
        w[nm] = _full_weight(nm, g)
    att, ltot, n_blocks, (g_up,) = _attn_fwd(q, k, v, _gather_exchange([late_blocks["w_ffn_up"]]))
    w["w_ffn_up"] = _full_weight("w_ffn_up", g_up)

    def merge_fn(u3t, at, gc, ga, xt, w_cb, w_ab, b_cb, w_out, g2_, g3_):
        cp = jnp.dot(u3t, w_cb, preferred_element_type=F32)
        ao = jnp.dot(at, w_ab, preferred_element_type=F32)
        mg = _merge(cp, ao, gc, ga, b_cb).astype(BF16)
        mix_ = jnp.dot(mg, w_out, preferred_element_type=F32)
        x2_ = xt + _rms(mix_, g2_)
        return (mg, cp, ao, mix_, x2_, _rms(x2_, g3_)), ()

    merged, conv_pre, att_out, mix, x2, h2 = _rowwise(
        "branch_merge_mix", merge_fn, [u3, att, g_conv, g_att, x],
        [w["w_conv_branch"], w["w_att_branch"], w["b_conv_branch"], w["w_out"], g2, g3],
        [_sds((s, D_MODEL), BF16)] * 3 + [_sds((s, D_MODEL)), _sds((s, D_MODEL)), _sds((s, D_MODEL), BF16)], tm=512)

    def ffn_up_fn(ht, w_up_t):
        gu_ = lax.dot_general(ht, w_up_t, NT, preferred_element_type=F32)
        return (gu_, _swiglu(gu_[:, :D_FF], gu_[:, D_FF:])), ()

    gu, act, g_down = _rowwise("ffn_up", ffn_up_fn, [h2], [w["w_ffn_up"]],
                               [_sds((s, 2 * D_FF), BF16), _sds((s, D_FF), BF16)], tm=512,
                               exchange=_gather_exchange([late_blocks["w_ffn_down"]]))
    w["w_ffn_down"] = _full_weight("w_ffn_down", g_down)

    def final_fn(at, x2t, tgt, w_down, g4_):
        ff = jnp.dot(at, w_down, preferred_element_type=F32)
        n4, vjp = jax.vjp(_rms, ff, g4_)
        err = x2t + n4 - tgt
        dy = err * (1.0 / D_MODEL)
        dff, dg4 = vjp(dy)
        return (dy, dff), (jnp.sum(err * err, axis=0, keepdims=True), dg4)

    dy, dff, loss_cols, d_g4 = _rowwise("ffn_down_loss", final_fn, [act, x2, target], [w["w_ffn_down"], g4],
                                        [_sds((s, D_MODEL)), _sds((s, D_MODEL), BF16)],
                                        [_sds((1, D_MODEL)), _sds((1, D_MODEL))], tm=512)
    loss = 0.5 * jnp.sum(loss_cols) / D_MODEL

    d_w_down = _matmul(act, dff, ta=True, name="d_w_down", out_dtype=BF16)

    def act_bwd_fn(dfft, gut, w_down):
        d_act = lax.dot_general(dfft, w_down, NT, preferred_element_type=F32)
        gu_ = gut.astype(F32)
        _, vjp = jax.vjp(_swiglu, gu_[:, :D_FF], gu_[:, D_FF:])
        return (jnp.concatenate(vjp(d_act), axis=1),), ()

    down_slabs = _side_slabs("w_ffn_down", d_w_down)
    dgu, theirs = _rowwise("ffn_act_bwd", act_bwd_fn, [dff, gu], [w["w_ffn_down"]], [_sds((s, 2 * D_FF), BF16)],
                           exchange=_pair_exchange([down_slabs]))
    down_sums = _pair_sum("pair_sum_w_ffn_down", down_slabs, theirs)
    d_w_up = _matmul(dgu, h2, ta=True, name="d_w_up", out_dtype=BF16)
    received = {}
    up_slabs = _side_slabs("w_ffn_up", d_w_up)

    def mid_bwd_fn(dgut, xt, mt, dyt, w_up_t, g2_, g3_):
        dh = jnp.dot(dgut, w_up_t, preferred_element_type=F32)
        n2, vjp2 = jax.vjp(_rms, mt, g2_)
        x2_ = xt + n2
        _, vjp3 = jax.vjp(_rms, x2_, g3_)
        dx2_, dg3 = vjp3(dh)
        dx2_ = dx2_ + dyt
        dmix_, dg2 = vjp2(dx2_)
        return (dx2_, dmix_), (dg2, dg3)

    dx2, dmix, d_g2, d_g3, received["w_ffn_down"] = _rowwise(
        "ffn_up_mid_bwd", mid_bwd_fn, [dgu, x, mix, dy], [w["w_ffn_up"], g2, g3],
        [_sds((s, D_MODEL)), _sds((s, D_MODEL), BF16)], [_sds((1, D_MODEL)), _sds((1, D_MODEL))], tm=512,
        exchange=_chip_exchange([down_sums]))
    d_w_out = _matmul(merged, dmix, ta=True, name="d_w_out", out_dtype=BF16)

    def merge_bwd_fn(dmt, cp, ao, gc, ga, w_out, w_cb, w_ab, b_cb):
        dm = lax.dot_general(dmt, w_out, NT, preferred_element_type=F32)
        _, vjp = jax.vjp(_merge, cp.astype(F32), ao.astype(F32), gc, ga, b_cb)
        dcp, dao, dgc, dga, dbias = vjp(dm)
        dcp, dao = dcp.astype(BF16), dao.astype(BF16)
        du3_ = lax.dot_general(dcp, w_cb, NT, preferred_element_type=F32)
        datt_ = lax.dot_general(dao, w_ab, NT, preferred_element_type=F32)
        return (dcp, dao, dgc, dga, du3_, datt_), (dbias,)

    d_conv_out, d_att_out, d_g_conv, d_g_att, du3, d_att, d_b_cb, theirs = _rowwise(
        "merge_bwd", merge_bwd_fn, [dmix, conv_pre, att_out, g_conv, g_att],
        [w["w_out"], w["w_conv_branch"], w["w_att_branch"], w["b_conv_branch"]],
        [_sds((s, D_MODEL), BF16)] * 4 + [_sds((s, CONV_DIM)), _sds((s, ATT_DIM), BF16)], [_sds((1, D_MODEL))], tm=512,
        exchange=_pair_exchange([up_slabs]))

    d_w_cb = _matmul(u3, d_conv_out, ta=True, name="d_w_conv_branch", out_dtype=BF16)
    d_w_ab = _matmul(att, d_att_out, ta=True, name="d_w_att_branch", out_dtype=BF16)

    dq, dk, dv, (received["w_ffn_up"],) = _attn_bwd(
        q, k, v, d_att, ltot, n_blocks, _chip_exchange([_pair_sum("pair_sum_w_ffn_up", up_slabs, theirs)]))

    mix_grads = {"w_conv_branch": d_w_cb, "w_att_branch": d_w_ab, "w_out": d_w_out}
    (d_conv_in, d_dw_w, d_dw_b, d_ln_g, d_ln_b), landed = _conv_bwd(
        conv_in, u1, du3, w["conv_ln_g"], w["conv_ln_b"], w["conv_dw_w"],
        _scatter_exchange([_grad_slabs(nm, mix_grads[nm]) for nm in mix_weights]))
    received.update(zip(mix_weights, landed))

    d_proj = [d_conv_in, dq, dk, dv, d_g_conv, d_g_att]
    d_w_in = _pieces_tn_matmul(d_proj, h1, name="d_w_in")
    in_slabs = _side_slabs("w_in", d_w_in)
    (theirs,) = _exchange_call("pair_swap_w_in", _pair_exchange([in_slabs]))

    early = list(opt)

    def pre_bwd_fn(*args):
        groups, (xt, dx2t), jobs, (w_in_t, g_) = args[:6], args[6:8], args[8:-2], args[-2:]
        dh = sum(jnp.dot(grp.astype(BF16), w_in_t[IN_SPLITS[n]:IN_SPLITS[n + 1]], preferred_element_type=F32)
                 for n, grp in enumerate(groups))
        _, vjp = jax.vjp(_rms, xt, g_)
        dx_, dg_ = vjp(dh)
        updates = [_sum_adamw_tile(*jobs[4 * n:4 * n + 4]) for n in range(len(early))]
        return (dx_ + dx2t, *[u for four in updates for u in four]), (dg_,)

    res = _rowwise(
        "proj_norm_bwd", pre_bwd_fn,
        d_proj + [x, dx2] + [a for nm in early for a in (received[nm], *opt[nm])], [w_in, g1],
        [_sds((s, D_MODEL))] + [_sds(opt[nm][0].shape) for nm in early for _ in range(4)],
        [_sds((1, D_MODEL))], tm=512, exchange=_chip_exchange([_pair_sum("pair_sum_w_in", in_slabs, theirs)]))
    grad_x, d_g1, received["w_in"] = res[0], res[-2], res[-1]
    updated = {nm: res[1 + 4 * n:5 + 4 * n] for n, nm in enumerate(early)}

    grads = {
        "norm_mix_pre": d_g1, "conv_dw_w": d_dw_w, "conv_dw_b": d_dw_b,
        "conv_ln_g": d_ln_g, "conv_ln_b": d_ln_b, "b_conv_branch": d_b_cb,
        "norm_mix_post": d_g2, "norm_ffn_pre": d_g3, "norm_ffn_post": d_g4,
    }
    return loss, grad_x, received, updated, grads


def _place():
    x, y, c = lax.axis_index("x"), lax.axis_index("y"), lax.axis_index("c")
    return x, y, c


def _slot(px, py, pc):
    return 4 * px + 2 * py + pc


def _exchange_scratch(n):
    return [pltpu.SemaphoreType.DMA((7 * n,)), pltpu.SemaphoreType.DMA((7 * n,)), pltpu.SemaphoreType.DMA((n,))]


def _gather_exchange(arrs):
    n = len(arrs)

    def phases(ins, outs, send_sems, recv_sems, local_sems):
        x, y, c = _place()
        me, sibling = (x, y, c), (x, y, 1 - c)
        chips = [(1 - x, y), (x, 1 - y), (1 - x, 1 - y)]

        def copy(a, kk, block, to, src=None):
            dst = outs[a].at[_slot(*block)]
            return pltpu.make_async_remote_copy(
                src_ref=dst if src is None else src, dst_ref=dst,
                send_sem=send_sems.at[a * 7 + kk], recv_sem=recv_sems.at[a * 7 + kk],
                device_id=to, device_id_type=MESH)

        mine = [pltpu.make_async_copy(ins[a], outs[a].at[_slot(*me)], local_sems.at[a]) for a in range(n)]
        first = []
        for a in range(n):
            first.append(copy(a, 0, me, sibling, src=ins[a]))
            first += [copy(a, 1 + j, me, (*chip, c), src=ins[a]) for j, chip in enumerate(chips)]
        passed = [copy(a, 4 + j, (*chip, c), sibling) for j, chip in enumerate(chips) for a in range(n)]

        def send():
            for cp in mine + first:
                cp.start()

        def pass_on():
            for j, chip in enumerate(chips):
                for a in range(n):
                    copy(a, 1 + j, (*chip, c), me).wait_recv()
                    passed[j * n + a].start()

        def finish():
            for a in range(n):
                copy(a, 0, sibling, me).wait_recv()
                for j, chip in enumerate(chips):
                    copy(a, 4 + j, (*chip, 1 - c), me).wait_recv()
            for cp in first + passed:
                cp.wait_send()
            for cp in mine:
                cp.wait()

        return [send, pass_on, finish]

    return list(arrs), [_sds((N_DEV,) + a.shape, a.dtype) for a in arrs], _exchange_scratch(n), phases


def _scatter_exchange(arrs):
    n = len(arrs)
    flips = [(fx, fy, fc) for fx in (0, 1) for fy in (0, 1) for fc in (0, 1)][1:]

    def phases(ins, outs, send_sems, recv_sems, local_sems):
        x, y, c = _place()
        mine = _slot(x, y, c)
        local = [pltpu.make_async_copy(ins[a].at[mine], outs[a].at[mine], local_sems.at[a]) for a in range(n)]
        peers = [((1 - x) if fx else x, (1 - y) if fy else y, (1 - c) if fc else c) for fx, fy, fc in flips]

        def copy(a, kk, src_slot, dst_slot):
            return pltpu.make_async_remote_copy(
                src_ref=ins[a].at[src_slot], dst_ref=outs[a].at[dst_slot],
                send_sem=send_sems.at[a * 7 + kk], recv_sem=recv_sems.at[a * 7 + kk],
                device_id=peers[kk], device_id_type=MESH)

        sends = [copy(a, kk, _slot(*peers[kk]), mine) for a in range(n) for kk in range(7)]

        def send():
            for cp in local + sends:
                cp.start()

        def finish():
            for a in range(n):
                for kk in range(7):
                    copy(a, kk, mine, _slot(*peers[kk])).wait_recv()
            for cp in sends:
                cp.wait_send()
            for cp in local:
                cp.wait()

        return [send, finish]

    return list(arrs), [_sds(a.shape, a.dtype) for a in arrs], _exchange_scratch(n), phases


def _pair_exchange(arrs):
    n = len(arrs)

    def phases(ins, outs, send_sems, recv_sems, local_sems):
        x, y, c = _place()

        def copy(a, chip, side):
            return pltpu.make_async_remote_copy(
                src_ref=ins[a].at[chip, side], dst_ref=outs[a].at[chip],
                send_sem=send_sems.at[a * 7 + chip], recv_sem=recv_sems.at[a * 7 + chip],
                device_id=(x, y, 1 - c), device_id_type=MESH)

        sends = [copy(a, chip, 1 - c) for a in range(n) for chip in range(4)]

        def send():
            for cp in sends:
                cp.start()

        def finish():
            for a in range(n):
                for chip in range(4):
                    copy(a, chip, c).wait_recv()
            for cp in sends:
                cp.wait_send()

        return [send, finish]

    return list(arrs), [_sds((4,) + a.shape[2:], a.dtype) for a in arrs], _exchange_scratch(n), phases


def _chip_exchange(arrs):
    n = len(arrs)

    def phases(ins, outs, send_sems, recv_sems, local_sems):
        x, y, c = _place()
        mine = 2 * x + y
        chips = [(1 - x, y), (x, 1 - y), (1 - x, 1 - y)]
        local = [pltpu.make_async_copy(ins[a].at[mine], outs[a].at[mine], local_sems.at[a]) for a in range(n)]

        def copy(a, j, src_slot, dst_slot):
            return pltpu.make_async_remote_copy(
                src_ref=ins[a].at[src_slot], dst_ref=outs[a].at[dst_slot],
                send_sem=send_sems.at[a * 7 + j], recv_sem=recv_sems.at[a * 7 + j],
                device_id=(*chips[j], c), device_id_type=MESH)

        sends = [copy(a, j, 2 * chips[j][0] + chips[j][1], mine) for a in range(n) for j in range(3)]

        def send():
            for cp in local + sends:
                cp.start()

        def finish():
            for a in range(n):
                for j in range(3):
                    copy(a, j, mine, 2 * chips[j][0] + chips[j][1]).wait_recv()
            for cp in sends:
                cp.wait_send()
            for cp in local:
                cp.wait()

        return [send, finish]

    return list(arrs), [_sds(a.shape, a.dtype) for a in arrs], _exchange_scratch(n), phases


def _pair_sum(name, mine, theirs):
    _, _, r, c = mine.shape

    def body(side_ref, m_ref, t_ref, o_ref):
        o_ref[...] = (m_ref[...].astype(F32) + t_ref[...].astype(F32)).astype(o_ref.dtype)

    return pl.pallas_call(
        body, name=name,
        grid_spec=pltpu.PrefetchScalarGridSpec(
            num_scalar_prefetch=1, grid=(4,),
            in_specs=[pl.BlockSpec((None, None, r, c), lambda j, side: (j, side[0], 0, 0)),
                      pl.BlockSpec((None, r, c), lambda j, side: (j, 0, 0))],
            out_specs=pl.BlockSpec((None, r, c), lambda j, side: (j, 0, 0))),
        out_shape=_sds(theirs.shape, theirs.dtype),
        compiler_params=_params(("parallel",)),
    )(lax.axis_index("c").astype(jnp.int32).reshape(1), mine, theirs)


def _exchange_call(name, exchange):
    arrs, out_shape, scratch, phases = exchange
    n = len(arrs)

    def body(*refs):
        for step in phases(refs[:n], refs[n:2 * n], *refs[2 * n:]):
            step()

    return pl.pallas_call(body, name=name, in_specs=[ANY] * n, out_specs=[ANY] * n,
                          out_shape=out_shape, scratch_shapes=scratch)(*arrs)


def _carry_exchange(exchange, refs, n_in, n_out, first, middle, last):
    arrs, _, _, phases = exchange
    n = len(arrs)
    if n == 0:
        return lambda: None
    ins = refs[n_in:n_in + n]
    outs = refs[n_in + n + n_out:n_in + 2 * n + n_out]
    sems = n_in + 2 * n + n_out
    steps = phases(ins, outs, *refs[sems:sems + 3])
    pl.when(first)(steps[0])
    if len(steps) == 3:
        pl.when(middle)(steps[1])
    return lambda: pl.when(last)(steps[-1])


def _adamw_math(w, g, m, v):
    m2 = ADAM_B1 * m + (1.0 - ADAM_B1) * g
    v2 = ADAM_B2 * v + (1.0 - ADAM_B2) * jnp.square(g)
    m_hat = m2 / (1.0 - ADAM_B1 ** ADAM_STEP)
    v_hat = v2 / (1.0 - ADAM_B2 ** ADAM_STEP)
    delta = -ADAM_LR * (m_hat / (jnp.sqrt(v_hat) + ADAM_EPS) + ADAM_WD * w)
    return delta, m2, v2


def _sum_adamw_tile(parts, w, m, v):
    g = parts[0].astype(F32)
    for d in range(1, parts.shape[0]):
        g = g + parts[d].astype(F32)
    return (g, *_adamw_math(w, g, m, v))


def _sum_adamw(name, parts, w, m, v, tr=256):
    p, r, c = parts.shape
    tr = _pick(r, tr, 16)

    def body(p_ref, w_ref, m_ref, v_ref, g_ref, d_ref, m2_ref, v2_ref):
        g_ref[...], d_ref[...], m2_ref[...], v2_ref[...] = _sum_adamw_tile(p_ref[...], w_ref[...], m_ref[...], v_ref[...])

    tile = pl.BlockSpec((tr, c), lambda i: (i, 0))
    return pl.pallas_call(
        body, name=name, grid=(r // tr,),
        in_specs=[pl.BlockSpec((p, tr, c), lambda i: (0, i, 0)), tile, tile, tile],
        out_specs=[tile] * 4, out_shape=[_sds((r, c))] * 4,
        compiler_params=_params(("parallel",)),
    )(parts, w, m, v)


def _sum_parts(name, parts):
    p, r, c = parts.shape

    def body(p_ref, o_ref):
        g = p_ref[0]
        for d in range(1, p):
            g = g + p_ref[d]
        o_ref[...] = g

    return pl.pallas_call(
        body, name=name, out_shape=_sds((r, c)),
        in_specs=[pl.BlockSpec(memory_space=pltpu.VMEM)], out_specs=pl.BlockSpec(memory_space=pltpu.VMEM),
    )(parts)


WEIGHTS = ["norm_mix_pre", "w_in", "conv_dw_w", "conv_dw_b", "conv_ln_g", "conv_ln_b", "w_conv_branch",
           "b_conv_branch", "w_att_branch", "w_out", "norm_mix_post", "norm_ffn_pre", "w_ffn_up", "w_ffn_down",
           "norm_ffn_post"]
COL_SHARDED = ["w_conv_branch", "w_att_branch"]
ROW_SHARDED = ["w_out", "w_ffn_down"]
TRANSPOSED = ["w_in", "w_ffn_up"]
VECTORS = ["norm_mix_pre", "conv_dw_b", "conv_ln_g", "conv_ln_b", "b_conv_branch", "norm_mix_post",
           "norm_ffn_pre", "norm_ffn_post"]


def _cols_to_full(g):
    return g.transpose(1, 0, 2).reshape(g.shape[1], N_DEV * g.shape[2])


def _full_to_cols(f):
    return f.reshape(f.shape[0], N_DEV, f.shape[1] // N_DEV).transpose(1, 0, 2)


PACK_ROWS = 7


def _pack_vectors(vecs, extra=None):
    parts = [vecs[nm].reshape(-1) for nm in VECTORS]
    parts.append(jnp.zeros((1,), F32) if extra is None else extra.reshape(1))
    used = sum(p.size for p in parts)
    parts.append(jnp.zeros((PACK_ROWS * D_MODEL - used,), F32))
    return jnp.concatenate(parts).reshape(PACK_ROWS, D_MODEL)


def _unpack_vectors(packed, sizes):
    flat, out, at = packed.reshape(-1), {}, 0
    for nm in VECTORS:
        out[nm] = flat[at:at + sizes[nm]]
        at += sizes[nm]
    return out, flat[at]


def kernel(x, norm_mix_pre, w_in, conv_dw_w, conv_dw_b, conv_ln_g, conv_ln_b, w_conv_branch, b_conv_branch, w_att_branch, w_out, norm_mix_post, norm_ffn_pre, w_ffn_up, w_ffn_down, norm_ffn_post, loss_target, m_norm_mix_pre, m_w_in, m_conv_dw_w, m_conv_dw_b, m_conv_ln_g, m_conv_ln_b, m_w_conv_branch, m_b_conv_branch, m_w_att_branch, m_w_out, m_norm_mix_post, m_norm_ffn_pre, m_w_ffn_up, m_w_ffn_down, m_norm_ffn_post, v_norm_mix_pre, v_w_in, v_conv_dw_w, v_conv_dw_b, v_conv_ln_g, v_conv_ln_b, v_w_conv_branch, v_b_conv_branch, v_w_att_branch, v_w_out, v_norm_mix_post, v_norm_ffn_pre, v_w_ffn_up, v_w_ffn_down, v_norm_ffn_post):
    ws = dict(zip(WEIGHTS, [norm_mix_pre, w_in, conv_dw_w, conv_dw_b, conv_ln_g, conv_ln_b, w_conv_branch,
                            b_conv_branch, w_att_branch, w_out, norm_mix_post, norm_ffn_pre, w_ffn_up, w_ffn_down,
                            norm_ffn_post]))
    ms = dict(zip(WEIGHTS, [m_norm_mix_pre, m_w_in, m_conv_dw_w, m_conv_dw_b, m_conv_ln_g, m_conv_ln_b,
                            m_w_conv_branch, m_b_conv_branch, m_w_att_branch, m_w_out, m_norm_mix_post,
                            m_norm_ffn_pre, m_w_ffn_up, m_w_ffn_down, m_norm_ffn_post]))
    vs = dict(zip(WEIGHTS, [v_norm_mix_pre, v_w_in, v_conv_dw_w, v_conv_dw_b, v_conv_ln_g, v_conv_ln_b,
                            v_w_conv_branch, v_b_conv_branch, v_w_att_branch, v_w_out, v_norm_mix_post,
                            v_norm_ffn_pre, v_w_ffn_up, v_w_ffn_down, v_norm_ffn_post]))

    dw_block = jnp.pad(conv_dw_w, ((0, 1), (0, 0)))
    g_in, g_dw = _exchange_call("gather_first", _gather_exchange([w_in.T.astype(BF16), dw_block]))
    full = {"w_in": _full_weight("w_in", g_in), "conv_dw_w": _cols_to_full(g_dw)}
    for nm in VECTORS:
        full[nm] = ws[nm].reshape(1, -1)

    def as_kept(nm, a):
        return a.T if nm in TRANSPOSED else a

    ride_along = ["w_ffn_up", "w_out"]
    loss_local, grad_x, received, updated, grads = _local_step(
        x[0], loss_target[0], full, {nm: as_kept(nm, ws[nm]).astype(BF16) for nm in LATE},
        {nm: tuple(as_kept(nm, a[nm]) for a in (ws, ms, vs)) for nm in ride_along})

    small = _exchange_call("gather_small_grads", _gather_exchange(
        [_pack_vectors(grads, extra=loss_local), grads["conv_dw_w"]]))
    out_g, out_d, out_m, out_v = {}, {}, {}, {}
    for nm in LATE + ["w_in"]:
        res = updated[nm] if nm in updated else _sum_adamw(
            "adamw_" + nm, received[nm], *[as_kept(nm, a[nm]) for a in (ws, ms, vs)])
        out_g[nm], out_d[nm], out_m[nm], out_v[nm] = [as_kept(nm, r) for r in res]
    sizes = {nm: ws[nm].size for nm in VECTORS}
    vec = _sum_adamw("adamw_vectors", small[0], _pack_vectors(ws), _pack_vectors(ms), _pack_vectors(vs))
    for res, dst in zip(vec, (out_g, out_d, out_m, out_v)):
        dst.update(_unpack_vectors(res, sizes)[0])
    loss = _unpack_vectors(vec[0], sizes)[1]
    dw_full = _sum_parts("sum_dw_grads", small[1])
    me = _slot(*_place())
    dw_mine = lax.dynamic_slice(dw_full, (0, me * (CONV_DIM // N_DEV)), (CONV_WIDTH, CONV_DIM // N_DEV))
    nm = "conv_dw_w"
    out_g[nm], out_d[nm], out_m[nm], out_v[nm] = _sum_adamw("adamw_dw", dw_mine[None], ws[nm], ms[nm], vs[nm])

    outs = [loss, grad_x[None]]
    for group in (out_g, out_d, out_m, out_v):
        outs += [group[nm] for nm in WEIGHTS]
    return tuple(outs)
```

```python
import math

import jax
import jax.numpy as jnp
from jax import lax
from jax.experimental import pallas as pl
from jax.experimental.pallas import tpu as pltpu

F32 = jnp.float32
BF16 = jnp.bfloat16

N_DEV = 8
D_MODEL = 1024
CONV_DIM = 512
CONV_WIDTH = 31
N_HEADS = 8
HEAD_DIM = 64
ATT_DIM = N_HEADS * HEAD_DIM
D_FF = 2816
EPS = 1e-6
IN_SPLITS = (0, 1024, 1536, 2048, 2560, 3584, 4608)

ADAM_LR = 0.001
ADAM_B1 = 0.9
ADAM_B2 = 0.999
ADAM_EPS = 1e-08
ADAM_WD = 0.01
ADAM_STEP = 10

LANES = 128
SUBLANES = 8
HALO = 32
ATT_TILE = 256
ATT_PART = 192
DEAD_SUM = -120.0
VMEM_LIMIT = 56 * 1024 * 1024
MESH = pl.DeviceIdType.MESH
ANY = pl.BlockSpec(memory_space=pl.ANY)


def _pick(dim, target, align=LANES):
    t = min(dim, target)
    t -= t % align
    while t >= align:
        if dim % t == 0:
            return t
        t -= align
    return dim


def _params(semantics):
    return pltpu.CompilerParams(dimension_semantics=semantics, vmem_limit_bytes=VMEM_LIMIT)


def _matmul(a, b, *, name, ta=False, tb=False, out_dtype=F32):
    m, k = (a.shape[1], a.shape[0]) if ta else a.shape
    n, k2 = b.shape if tb else (b.shape[1], b.shape[0])
    assert k == k2, (a.shape, b.shape, ta, tb)
    tm, tn, tk = _pick(m, 1408 if ta else 512), _pick(n, 1536), _pick(k, 1536)
    nk = k // tk
    dims = (((0 if ta else 1,), (1 if tb else 0,)), ((), ()))

    def body(a_ref, b_ref, o_ref, *acc):
        part = lax.dot_general(a_ref[...], b_ref[...], dims, preferred_element_type=F32)
        if nk == 1:
            o_ref[...] = part.astype(o_ref.dtype)
            return
        acc_ref, = acc
        kk = pl.program_id(2)

        @pl.when(kk == 0)
        def _():
            acc_ref[...] = part

        @pl.when((kk > 0) & (kk < nk - 1))
        def _():
            acc_ref[...] += part

        @pl.when(kk == nk - 1)
        def _():
            o_ref[...] = (acc_ref[...] + part).astype(o_ref.dtype)

    a_spec = pl.BlockSpec((tk, tm), lambda j, i, kk: (kk, i)) if ta else pl.BlockSpec((tm, tk), lambda j, i, kk: (i, kk))
    b_spec = (pl.BlockSpec((tn, tk), lambda j, i, kk: (j, kk)) if tb
              else pl.BlockSpec((tk, tn), lambda j, i, kk: (kk, j)))
    return pl.pallas_call(
        body, name=name, grid=(n // tn, m // tm, nk),
        in_specs=[a_spec, b_spec],
        out_specs=pl.BlockSpec((tm, tn), lambda j, i, kk: (i, j)),
        out_shape=jax.ShapeDtypeStruct((m, n), out_dtype),
        scratch_shapes=[pltpu.VMEM((tm, tn), F32)] if nk > 1 else [],
        compiler_params=_params(("parallel", "parallel", "arbitrary")),
    )(a, b)


def _pieces_tn_matmul(pieces, b, *, name, tj=512):
    s, n = b.shape
    counts = [p.shape[1] // tj for p in pieces]
    starts = [sum(counts[:i]) for i in range(len(pieces))]
    assert all(p.shape == (s, c * tj) for p, c in zip(pieces, counts))

    def body(*refs):
        b_ref, o_ref = refs[len(pieces):]
        j = pl.program_id(0)
        for p_ref, first, count in zip(refs, starts, counts):
            @pl.when((j >= first) & (j < first + count))
            def _():
                o_ref[...] = lax.dot_general(p_ref[...].astype(BF16), b_ref[...], TN,
                                             preferred_element_type=F32).astype(o_ref.dtype)

    def piece_spec(first, count):
        return pl.BlockSpec((s, tj), lambda j: (0, jnp.clip(j - first, 0, count - 1)))

    return pl.pallas_call(
        body, name=name, grid=(sum(counts),),
        in_specs=[piece_spec(f, c) for f, c in zip(starts, counts)]
        + [pl.BlockSpec((s, n), lambda j: (0, 0), pipeline_mode=pl.Buffered(1))],
        out_specs=pl.BlockSpec((tj, n), lambda j: (j, 0)),
        out_shape=jax.ShapeDtypeStruct((sum(counts) * tj, n), BF16),
        compiler_params=_params(("arbitrary",)),
    )(*pieces, b)


NO_EXCHANGE = ([], [], [], None)


def _sweep_marks(nt):
    i = pl.program_id(0)
    return i == 0, i == (3 * nt) // 4, i == nt - 1


def _rowwise(name, fn, rows, bcasts, row_outs, red_outs=(), tm=256, exchange=NO_EXCHANGE):
    s = rows[0].shape[0]
    tm = _pick(s, tm, 16)
    nt = s // tm
    resident = pl.Buffered(1)
    nr, nb, no, nd = len(rows), len(bcasts), len(row_outs), len(red_outs)
    x_arrs, x_shape, x_scratch, _ = exchange
    nx = len(x_arrs)
    first_out = nr + nb + nx

    def body(*refs):
        finish_exchange = _carry_exchange(exchange, refs, nr + nb, no + nd, *_sweep_marks(nt))
        ins = [r[...] for r in refs[:nr + nb]]
        outs, reds = fn(*ins)
        for ref, val in zip(refs[first_out:first_out + no], outs):
            ref[...] = val.astype(ref.dtype)
        i = pl.program_id(0)
        for ref, val in zip(refs[first_out + no:first_out + no + nd], reds):
            @pl.when(i == 0)
            def _():
                ref[...] = val

            @pl.when(i > 0)
            def _():
                ref[...] += val
        finish_exchange()

    def row_spec(a):
        assert a.shape[-2] % nt == 0, (name, a.shape, nt)
        if len(a.shape) == 3:
            return pl.BlockSpec((a.shape[0], a.shape[1] // nt, a.shape[2]), lambda i: (0, i, 0))
        return pl.BlockSpec((a.shape[0] // nt, a.shape[1]), lambda i: (i, 0))

    in_specs = [row_spec(r) for r in rows]
    in_specs += [pl.BlockSpec(b.shape, lambda i: (0, 0), pipeline_mode=resident) for b in bcasts]
    out_specs = [row_spec(o) for o in row_outs]
    out_specs += [pl.BlockSpec(d.shape, lambda i: (0, 0)) for d in red_outs]
    return pl.pallas_call(
        body, name=name, grid=(nt,), in_specs=in_specs + [ANY] * nx, out_specs=out_specs + [ANY] * nx,
        out_shape=list(row_outs) + list(red_outs) + x_shape, scratch_shapes=x_scratch,
        compiler_params=_params(("arbitrary",)),
    )(*rows, *bcasts, *x_arrs)


def _sds(shape, dtype=F32):
    return jax.ShapeDtypeStruct(shape, dtype)


def _rms(x, g):
    y = x * lax.rsqrt(jnp.mean(x * x, axis=-1, keepdims=True) + EPS)
    return y * g


def _silu(x):
    return x * jax.nn.sigmoid(x)


def _swiglu(g, u):
    return _silu(g) * u


def _ln_silu(u, g, b):
    mu = jnp.mean(u, axis=-1, keepdims=True)
    var = jnp.mean(jnp.square(u - mu), axis=-1, keepdims=True)
    return _silu((u - mu) * lax.rsqrt(var + EPS) * g + b)


def _merge(conv_pre, att_out, g_conv, g_att, b_cb):
    return jax.nn.sigmoid(g_conv) * (conv_pre + b_cb) + jax.nn.sigmoid(g_att) * att_out


def _glu(t):
    return t[:, :CONV_DIM] * jax.nn.sigmoid(t[:, CONV_DIM:])


def _shifted_reader(buf, shifted, tm):
    for b in range(1, SUBLANES):
        shifted[b - 1, :, :] = buf[pl.ds(b, tm + HALO - SUBLANES), :]

    def read(o):
        a, b = divmod(o, SUBLANES)
        return buf[pl.ds(SUBLANES * a, tm), :] if b == 0 else shifted[b - 1, pl.ds(SUBLANES * a, tm), :]

    return read


def _conv_fwd(conv_in, w_pad, b, ln_g, ln_b, exchange, tm=256):
    s = conv_in.shape[0]
    tm = _pick(s, tm, HALO)
    ratio = tm // HALO
    x_arrs, x_shape, x_scratch, _ = exchange
    nx = len(x_arrs)

    def body(*refs):
        main_ref, halo_ref, w_ref, b_ref, g_ref, be_ref = refs[:6]
        u3_ref, u1_ref = refs[6 + nx:8 + nx]
        buf, shifted = refs[-2:]
        finish_exchange = _carry_exchange(exchange, refs, 6, 2, *_sweep_marks(s // tm))
        i = pl.program_id(0)
        buf[0:HALO, :] = _glu(halo_ref[...]) * (i > 0).astype(F32)
        buf[HALO:HALO + tm, :] = _glu(main_ref[...])
        read = _shifted_reader(buf, shifted, tm)
        acc = jnp.zeros((tm, CONV_DIM), F32) + b_ref[...]
        for j in range(CONV_WIDTH):
            acc = acc + w_ref[j:j + 1, :] * read(HALO - (CONV_WIDTH - 1) + j)
        u1_ref[...] = acc
        u3_ref[...] = _ln_silu(acc, g_ref[...], be_ref[...]).astype(u3_ref.dtype)
        finish_exchange()

    res = pl.pallas_call(
        body, name="conv_fwd", grid=(s // tm,),
        in_specs=[pl.BlockSpec((tm, 2 * CONV_DIM), lambda i: (i, 0)),
                  pl.BlockSpec((HALO, 2 * CONV_DIM), lambda i: (jnp.maximum(i * ratio - 1, 0), 0)),
                  pl.BlockSpec(w_pad.shape, lambda i: (0, 0)),
                  pl.BlockSpec(b.shape, lambda i: (0, 0)),
                  pl.BlockSpec(ln_g.shape, lambda i: (0, 0)),
                  pl.BlockSpec(ln_b.shape, lambda i: (0, 0))] + [ANY] * nx,
        out_specs=[pl.BlockSpec((tm, CONV_DIM), lambda i: (i, 0)),
                   pl.BlockSpec((tm, CONV_DIM), lambda i: (i, 0))] + [ANY] * nx,
        out_shape=[_sds((s, CONV_DIM), BF16), _sds((s, CONV_DIM), F32)] + x_shape,
        scratch_shapes=x_scratch + [pltpu.VMEM((tm + HALO, CONV_DIM), F32),
                                    pltpu.VMEM((SUBLANES - 1, tm + HALO - SUBLANES, CONV_DIM), F32)],
        compiler_params=_params(("arbitrary",)),
    )(conv_in, conv_in, w_pad, b, ln_g, ln_b, *x_arrs)
    return res[0], res[1], res[2:]


def _conv_bwd(conv_in, u1, du3, ln_g, ln_b, w_pad, exchange, tm=256):
    s = conv_in.shape[0]
    tm = _pick(s, tm, HALO)
    ratio = tm // HALO
    nt = s // tm
    last_halo = s // HALO - 1
    x_arrs, x_shape, x_scratch, _ = exchange
    nx = len(x_arrs)

    def body(*refs):
        main_ref, halo_ref, u1_ref, u1n_ref, du3_ref, du3n_ref, g_ref, be_ref, w_ref = refs[:9]
        dci_ref, dw_ref, db_ref, dg_ref, dbe_ref = refs[9 + nx:14 + nx]
        ubuf, dbuf, ushift, dshift = refs[-4:]
        finish_exchange = _carry_exchange(exchange, refs, 9, 5, *_sweep_marks(nt))
        i = pl.program_id(0)
        main = main_ref[...]
        a = main[:, :CONV_DIM]
        sb = jax.nn.sigmoid(main[:, CONV_DIM:])
        ubuf[0:HALO, :] = _glu(halo_ref[...]) * (i > 0).astype(F32)
        ubuf[HALO:HALO + tm, :] = a * sb

        def ln_bwd(u1t, du3t):
            _, vjp = jax.vjp(_ln_silu, u1t, g_ref[...], be_ref[...])
            return vjp(du3t)

        du, dg, dbe = ln_bwd(u1_ref[...], du3_ref[...])
        dbuf[0:tm, :] = du
        dbuf[tm:tm + HALO, :] = ln_bwd(u1n_ref[...], du3n_ref[...])[0] * (i < nt - 1).astype(F32)

        @pl.when(i == 0)
        def _():
            dw_ref[...] = jnp.zeros_like(dw_ref)
            db_ref[...] = jnp.zeros_like(db_ref)
            dg_ref[...] = jnp.zeros_like(dg_ref)
            dbe_ref[...] = jnp.zeros_like(dbe_ref)

        dg_ref[...] += dg
        dbe_ref[...] += dbe

        read_u = _shifted_reader(ubuf, ushift, tm)
        read_d = _shifted_reader(dbuf, dshift, tm)
        du0 = jnp.zeros((tm, CONV_DIM), F32)
        for j in range(CONV_WIDTH):
            du0 = du0 + w_ref[j:j + 1, :] * read_d(CONV_WIDTH - 1 - j)
            dw_ref[j:j + 1, :] += jnp.sum(du * read_u(HALO - (CONV_WIDTH - 1) + j), axis=0, keepdims=True)
        db_ref[...] += jnp.sum(du, axis=0, keepdims=True)
        dci_ref[:, :CONV_DIM] = (du0 * sb).astype(dci_ref.dtype)
        dci_ref[:, CONV_DIM:] = (du0 * a * sb * (1.0 - sb)).astype(dci_ref.dtype)
        finish_exchange()

    res = pl.pallas_call(
        body, name="conv_bwd", grid=(nt,),
        in_specs=[pl.BlockSpec((tm, 2 * CONV_DIM), lambda i: (i, 0)),
                  pl.BlockSpec((HALO, 2 * CONV_DIM), lambda i: (jnp.maximum(i * ratio - 1, 0), 0))]
        + [pl.BlockSpec((tm, CONV_DIM), lambda i: (i, 0)),
           pl.BlockSpec((HALO, CONV_DIM), lambda i: (jnp.minimum((i + 1) * ratio, last_halo), 0))] * 2
        + [pl.BlockSpec((1, CONV_DIM), lambda i: (0, 0))] * 2 + [pl.BlockSpec(w_pad.shape, lambda i: (0, 0))]
        + [ANY] * nx,
        out_specs=[pl.BlockSpec((tm, 2 * CONV_DIM), lambda i: (i, 0)),
                   pl.BlockSpec(w_pad.shape, lambda i: (0, 0))]
        + [pl.BlockSpec((1, CONV_DIM), lambda i: (0, 0))] * 3 + [ANY] * nx,
        out_shape=[_sds((s, 2 * CONV_DIM), BF16), _sds(w_pad.shape)] + [_sds((1, CONV_DIM))] * 3 + x_shape,
        scratch_shapes=x_scratch + [pltpu.VMEM((tm + HALO, CONV_DIM), F32)] * 2
        + [pltpu.VMEM((SUBLANES - 1, tm + HALO - SUBLANES, CONV_DIM), F32)] * 2,
        compiler_params=_params(("arbitrary",)),
    )(conv_in, conv_in, u1, u1, du3, du3, ln_g, ln_b, w_pad, *x_arrs)
    return res[:5], res[5:]


def _logsig_neg(z):
    return jnp.minimum(-z, 0.0) - jnp.log(1.0 + jnp.exp(-jnp.abs(z)))


def _split_dot(val, tri):
    hi = val.astype(BF16)
    lo = (val - hi.astype(F32)).astype(BF16)
    return jnp.dot(hi, tri, preferred_element_type=F32) + jnp.dot(lo, tri, preferred_element_type=F32)


def _attn_masks(t, later):
    row = lax.broadcasted_iota(jnp.int32, (t, t), 0)
    col = lax.broadcasted_iota(jnp.int32, (t, t), 1)
    tri = jnp.where(row > col if later else row <= col, 1.0, 0.0).astype(BF16)
    return col < row, tri


def _grid_marks(h, nq):
    hh, i = pl.program_id(0), pl.program_id(1)
    return (hh == 0) & (i == 0), (hh == (3 * h) // 4) & (i == 0), (hh == h - 1) & (i == nq - 1)


def _head_masks(shape):
    lane = lax.broadcasted_iota(jnp.int32, shape, len(shape) - 1)
    return lane < HEAD_DIM, lane >= HEAD_DIM


def _per_head(blk):
    m0, m1 = _head_masks(blk.shape)
    zero = jnp.zeros_like(blk)
    return jnp.where(m0, blk, zero), jnp.where(m1, blk, zero)


NT = (((1,), (1,)), ((), ()))
TN = (((0,), (0,)), ((), ()))


def _with_rows(whole, r0, part):
    r1 = r0 + part.shape[0]
    pieces = ([whole[:r0]] if r0 else []) + [part] + ([whole[r1:]] if r1 < whole.shape[0] else [])
    return jnp.concatenate(pieces, axis=0) if len(pieces) > 1 else part


def _attn_fwd(q, k, v, exchange):
    s = q.shape[0]
    hp = q.shape[1] // LANES
    t = ATT_TILE
    scale = 1.0 / math.sqrt(HEAD_DIM)
    x_arrs, x_shape, x_scratch, _ = exchange
    nx = len(x_arrs)

    def body(*refs):
        q_ref, k_ref, v_ref = refs[:3]
        o_ref, lt_ref, nb_ref = refs[3 + nx:6 + nx]
        finish_exchange = _carry_exchange(exchange, refs, 3, 3, *_grid_marks(hp, s // t))
        i = pl.program_id(1)
        qs = _per_head((q_ref[...].astype(F32) * scale).astype(BF16))
        causal, tri = _attn_masks(t, later=True)

        def step(kb, carry, masked, rows, r0=0, kn=t):
            cs, acc = carry
            off = pl.multiple_of(kb * t, t)
            kblk = k_ref[pl.ds(off, kn), :]
            vs = _per_head(v_ref[pl.ds(off, kn), :])
            sl = slice(r0, r0 + rows)
            acc_part = acc[sl]
            new_cs = []
            for hd in range(2):
                z = lax.dot_general(qs[hd][sl], kblk, NT, preferred_element_type=F32)
                l = _logsig_neg(z)
                if masked:
                    l = jnp.where(causal[sl, :kn], l, 0.0)
                e = z + l + _split_dot(l, tri[:kn, :kn]) + cs[hd][sl]
                if masked:
                    e = jnp.where(causal[sl, :kn], e, -1e30)
                acc_part = acc_part + jnp.dot(jnp.exp(e).astype(BF16), vs[hd], preferred_element_type=F32)
                new_cs.append(_with_rows(cs[hd], r0, cs[hd][sl] + jnp.sum(l, axis=1, keepdims=True)))
            return tuple(new_cs), _with_rows(acc, r0, acc_part)

        zero = jnp.zeros((t, 1), F32)
        carry = step(i, ((zero, zero), jnp.zeros((t, LANES), F32)), True, t // 2, r0=t // 2)
        carry = step(i, carry, True, t // 2, kn=t // 2)

        def live(cs, lo, hi):
            return jnp.maximum(jnp.max(cs[0][lo:hi]), jnp.max(cs[1][lo:hi])) > DEAD_SUM

        def more(state):
            n, _, (cs, _) = state
            return (n < i) & live(cs, 0, t)

        def sweep(state):
            n, n_full, cr = state
            whole = live(cr[0], ATT_PART, t)
            cr = lax.cond(whole, lambda c: step(i - 1 - n, c, False, t), lambda c: step(i - 1 - n, c, False, ATT_PART), cr)
            return n + 1, n_full + whole.astype(jnp.int32), cr

        n_blocks, n_full, carry = lax.while_loop(more, sweep, (jnp.int32(0), jnp.int32(0), carry))
        m0, _ = _head_masks((t, LANES))
        lt_ref[...] = jnp.where(m0, carry[0][0], carry[0][1])
        o_ref[...] = carry[1].astype(o_ref.dtype)
        nb_ref[0, pl.program_id(0), i] = n_blocks.astype(F32)
        nb_ref[1, pl.program_id(0), i] = n_full.astype(F32)
        finish_exchange()

    res = pl.pallas_call(
        body, name="attn_fwd", grid=(hp, s // t),
        in_specs=[pl.BlockSpec((t, LANES), lambda p, i: (i, p)),
                  pl.BlockSpec((s, LANES), lambda p, i: (0, p)),
                  pl.BlockSpec((s, LANES), lambda p, i: (0, p))] + [ANY] * nx,
        out_specs=[pl.BlockSpec((t, LANES), lambda p, i: (i, p)),
                   pl.BlockSpec((None, t, LANES), lambda p, i: (p, i, 0)),
                   pl.BlockSpec(memory_space=pltpu.SMEM)] + [ANY] * nx,
        out_shape=[_sds(q.shape, BF16), _sds((hp, s, LANES), F32), _sds((2, hp, s // t), F32)] + x_shape,
        scratch_shapes=x_scratch,
        compiler_params=_params(("arbitrary", "arbitrary")),
    )(q, k, v, *x_arrs)
    return res[0], res[1], res[2], res[3:]


def _attn_bwd(q, k, v, do, ltot, n_blocks, exchange):
    s = q.shape[0]
    hp = q.shape[1] // LANES
    t = ATT_TILE
    scale = 1.0 / math.sqrt(HEAD_DIM)
    x_arrs, x_shape, x_scratch, _ = exchange
    nx = len(x_arrs)

    def body(*refs):
        q_ref, k_ref, v_ref, do_ref, lt_ref, nb_ref = refs[:6]
        dq_ref, dk_ref, dv_ref = refs[6 + nx:9 + nx]
        finish_exchange = _carry_exchange(exchange, refs, 6, 3, *_grid_marks(hp, s // t))
        i = pl.program_id(1)
        n_blocks = jnp.clip(nb_ref[0, pl.program_id(0), i].astype(jnp.int32), 0, i)
        n_full = jnp.clip(nb_ref[1, pl.program_id(0), i].astype(jnp.int32), 0, n_blocks)

        @pl.when(i == 0)
        def _():
            dk_ref[...] = jnp.zeros_like(dk_ref)
            dv_ref[...] = jnp.zeros_like(dv_ref)

        qb = q_ref[...]
        qm = _per_head(qb)
        qs = _per_head((qb.astype(F32) * scale).astype(BF16))
        dos = _per_head(do_ref[...])
        lts = (lt_ref[:, 0:1], lt_ref[:, HEAD_DIM:HEAD_DIM + 1])
        causal, tri = _attn_masks(t, later=False)

        def step(kb, carry, masked, rows, r0=0, kn=t):
            cls, cgs, dq = carry
            off = pl.multiple_of(kb * t, t)
            kblk = k_ref[pl.ds(off, kn), :]
            vblk = v_ref[pl.ds(off, kn), :]
            ks = _per_head(kblk)
            sl = slice(r0, r0 + rows)
            mask, tri_k = causal[sl, :kn], tri[:kn, :kn]
            dq_part = dq[sl]
            dk = jnp.zeros((kn, LANES), F32)
            dv = jnp.zeros((kn, LANES), F32)
            new_cls, new_cgs = [], []
            for hd in range(2):
                z = lax.dot_general(qs[hd][sl], kblk, NT, preferred_element_type=F32)
                l = _logsig_neg(z)
                if masked:
                    l = jnp.where(mask, l, 0.0)
                e = z + l + ((lts[hd][sl] - cls[hd][sl]) - _split_dot(l, tri_k))
                if masked:
                    e = jnp.where(mask, e, -1e30)
                a = jnp.exp(e)
                g = lax.dot_general(dos[hd][sl], vblk, NT, preferred_element_type=F32) * a
                p = cgs[hd][sl] + jnp.dot(g.astype(BF16), tri_k, preferred_element_type=F32) - g
                el = jnp.exp(l)
                dz = g * el - p * (1.0 - el)
                if masked:
                    dz = jnp.where(mask, dz, 0.0)
                dzb = (dz * scale).astype(BF16)
                dq_part = dq_part + jnp.dot(dzb, ks[hd], preferred_element_type=F32)
                dk = dk + lax.dot_general(dzb, qm[hd][sl], TN, preferred_element_type=F32)
                dv = dv + lax.dot_general(a.astype(BF16), dos[hd][sl], TN, preferred_element_type=F32)
                new_cls.append(_with_rows(cls[hd], r0, cls[hd][sl] + jnp.sum(l, axis=1, keepdims=True)))
                new_cgs.append(_with_rows(cgs[hd], r0, cgs[hd][sl] + jnp.sum(g, axis=1, keepdims=True)))
            dk_ref[pl.ds(off, kn), :] += dk
            dv_ref[pl.ds(off, kn), :] += dv
            return tuple(new_cls), tuple(new_cgs), _with_rows(dq, r0, dq_part)

        zero = jnp.zeros((t, 1), F32)
        init = ((zero, zero), (zero, zero), jnp.zeros((t, LANES), F32))
        carry = lax.fori_loop(i - n_blocks, i - n_full, lambda kb, cr: step(kb, cr, False, ATT_PART), init)
        carry = lax.fori_loop(i - n_full, i, lambda kb, cr: step(kb, cr, False, t), carry)
        carry = step(i, carry, True, t // 2, r0=t // 2)
        carry = step(i, carry, True, t // 2, kn=t // 2)
        dq_ref[...] = carry[2]
        finish_exchange()

    blk = pl.BlockSpec((t, LANES), lambda p, i: (i, p))
    whole = pl.BlockSpec((s, LANES), lambda p, i: (0, p))
    res = pl.pallas_call(
        body, name="attn_bwd", grid=(hp, s // t),
        in_specs=[blk, whole, whole, blk, pl.BlockSpec((None, t, LANES), lambda p, i: (p, i, 0)),
                  pl.BlockSpec(memory_space=pltpu.SMEM)] + [ANY] * nx,
        out_specs=[blk, whole, whole] + [ANY] * nx,
        out_shape=[_sds(q.shape)] * 3 + x_shape,
        scratch_shapes=x_scratch,
        compiler_params=_params(("arbitrary", "arbitrary")),
    )(q, k, v, do, ltot, n_blocks, *x_arrs)
    return res[0], res[1], res[2], res[3:]


LATE = ["w_conv_branch", "w_att_branch", "w_out", "w_ffn_up", "w_ffn_down"]


def _full_weight(name, gathered):
    return _cols_to_full(gathered) if name in COL_SHARDED else gathered.reshape(-1, gathered.shape[2])


def _grad_slabs(name, grad):
    return _full_to_cols(grad) if name in COL_SHARDED else grad.reshape(N_DEV, -1, grad.shape[1])


def _side_slabs(name, grad):
    slabs = _grad_slabs(name, grad)
    return slabs.reshape((4, 2) + slabs.shape[1:])


def _local_step(x, target, w, late_blocks, opt):
    s = x.shape[0]
    w = dict(w)
    g1, g2, g3, g4 = w["norm_mix_pre"], w["norm_mix_post"], w["norm_ffn_pre"], w["norm_ffn_post"]

    w_in = w["w_in"]

    def proj_fn(xt, g1_, w_in_t):
        h = _rms(xt, g1_).astype(BF16)
        proj = lax.dot_general(h, w_in_t, NT, preferred_element_type=F32)
        return (h, *[proj[:, IN_SPLITS[n]:IN_SPLITS[n + 1]] for n in range(6)]), ()

    mix_weights = ["w_conv_branch", "w_att_branch", "w_out"]
    h1, conv_in, q, k, v, g_conv, g_att = _rowwise(
        "norm_proj", proj_fn, [x], [g1, w_in],
        [_sds((s, D_MODEL), BF16), _sds((s, 2 * CONV_DIM)), _sds((s, ATT_DIM), BF16), _sds((s, ATT_DIM), BF16),
         _sds((s, ATT_DIM), BF16), _sds((s, D_MODEL), BF16), _sds((s, D_MODEL), BF16)], tm=512)

    u3, u1, gathered = _conv_fwd(conv_in, w["conv_dw_w"], w["conv_dw_b"], w["conv_ln_g"], w["conv_ln_b"],
                                 _gather_exchange([late_blocks[nm] for nm in mix_weights]))
    for nm, g in zip(mix_weights, gathered):
        w[nm] = _full_weight(nm, g)
    att, ltot, n_blocks, (g_up,) = _attn_fwd(q, k, v, _gather_exchange([late_blocks["w_ffn_up"]]))
    w["w_ffn_up"] = _full_weight("w_ffn_up", g_up)

    def merge_fn(u3t, at, gc, ga, xt, w_cb, w_ab, b_cb, w_out, g2_, g3_):
        cp = jnp.dot(u3t, w_cb, preferred_element_type=F32)
        ao = jnp.dot(at, w_ab, preferred_element_type=F32)
        mg = _merge(cp, ao, gc.astype(F32), ga.astype(F32), b_cb).astype(BF16)
        mix_ = jnp.dot(mg, w_out, preferred_element_type=F32)
        x2_ = xt + _rms(mix_, g2_)
        return (mg, cp, ao, mix_, x2_, _rms(x2_, g3_)), ()

    merged, conv_pre, att_out, mix, x2, h2 = _rowwise(
        "branch_merge_mix", merge_fn, [u3, att, g_conv, g_att, x],
        [w["w_conv_branch"], w["w_att_branch"], w["b_conv_branch"], w["w_out"], g2, g3],
        [_sds((s, D_MODEL), BF16)] * 3 + [_sds((s, D_MODEL)), _sds((s, D_MODEL)), _sds((s, D_MODEL), BF16)], tm=512)

    def ffn_up_fn(ht, w_up_t):
        gu_ = lax.dot_general(ht, w_up_t, NT, preferred_element_type=F32)
        return (gu_, _swiglu(gu_[:, :D_FF], gu_[:, D_FF:])), ()

    gu, act, g_down = _rowwise("ffn_up", ffn_up_fn, [h2], [w["w_ffn_up"]],
                               [_sds((s, 2 * D_FF), BF16), _sds((s, D_FF), BF16)], tm=512,
                               exchange=_gather_exchange([late_blocks["w_ffn_down"]]))
    w["w_ffn_down"] = _full_weight("w_ffn_down", g_down)

    def final_fn(at, x2t, tgt, w_down, g4_):
        ff = jnp.dot(at, w_down, preferred_element_type=F32)
        n4, vjp = jax.vjp(_rms, ff, g4_)
        err = x2t + n4 - tgt
        dy = err * (1.0 / D_MODEL)
        dff, dg4 = vjp(dy)
        return (dy, dff), (jnp.sum(err * err, axis=0, keepdims=True), dg4)

    dy, dff, loss_cols, d_g4 = _rowwise("ffn_down_loss", final_fn, [act, x2, target], [w["w_ffn_down"], g4],
                                        [_sds((s, D_MODEL)), _sds((s, D_MODEL), BF16)],
                                        [_sds((1, D_MODEL)), _sds((1, D_MODEL))], tm=512)
    loss = 0.5 * jnp.sum(loss_cols) / D_MODEL

    d_w_down = _matmul(act, dff, ta=True, name="d_w_down", out_dtype=BF16)

    def act_bwd_fn(dfft, gut, w_down):
        d_act = lax.dot_general(dfft, w_down, NT, preferred_element_type=F32)
        gu_ = gut.astype(F32)
        _, vjp = jax.vjp(_swiglu, gu_[:, :D_FF], gu_[:, D_FF:])
        return (jnp.concatenate(vjp(d_act), axis=1),), ()

    down_slabs = _side_slabs("w_ffn_down", d_w_down)
    dgu, theirs = _rowwise("ffn_act_bwd", act_bwd_fn, [dff, gu], [w["w_ffn_down"]], [_sds((s, 2 * D_FF), BF16)],
                           exchange=_pair_exchange([down_slabs]))
    down_sums = _pair_sum("pair_sum_w_ffn_down", down_slabs, theirs)
    d_w_up = _matmul(dgu, h2, ta=True, name="d_w_up", out_dtype=BF16)
    received = {}
    up_slabs = _side_slabs("w_ffn_up", d_w_up)

    def mid_bwd_fn(dgut, xt, mt, dyt, w_up_t, g2_, g3_):
        dh = jnp.dot(dgut, w_up_t, preferred_element_type=F32)
        n2, vjp2 = jax.vjp(_rms, mt, g2_)
        x2_ = xt + n2
        _, vjp3 = jax.vjp(_rms, x2_, g3_)
        dx2_, dg3 = vjp3(dh)
        dx2_ = dx2_ + dyt
        dmix_, dg2 = vjp2(dx2_)
        return (dx2_, dmix_), (dg2, dg3)

    dx2, dmix, d_g2, d_g3, received["w_ffn_down"] = _rowwise(
        "ffn_up_mid_bwd", mid_bwd_fn, [dgu, x, mix, dy], [w["w_ffn_up"], g2, g3],
        [_sds((s, D_MODEL)), _sds((s, D_MODEL), BF16)], [_sds((1, D_MODEL)), _sds((1, D_MODEL))], tm=512,
        exchange=_chip_exchange([down_sums]))
    d_w_out = _matmul(merged, dmix, ta=True, name="d_w_out", out_dtype=BF16)

    def merge_bwd_fn(dmt, cp, ao, gc, ga, w_out, w_cb, w_ab, b_cb):
        dm = lax.dot_general(dmt, w_out, NT, preferred_element_type=F32)
        _, vjp = jax.vjp(_merge, cp.astype(F32), ao.astype(F32), gc.astype(F32), ga.astype(F32), b_cb)
        dcp, dao, dgc, dga, dbias = vjp(dm)
        dcp, dao = dcp.astype(BF16), dao.astype(BF16)
        du3_ = lax.dot_general(dcp, w_cb, NT, preferred_element_type=F32)
        datt_ = lax.dot_general(dao, w_ab, NT, preferred_element_type=F32)
        return (dcp, dao, dgc, dga, du3_, datt_), (dbias,)

    d_conv_out, d_att_out, d_g_conv, d_g_att, du3, d_att, d_b_cb, theirs = _rowwise(
        "merge_bwd", merge_bwd_fn, [dmix, conv_pre, att_out, g_conv, g_att],
        [w["w_out"], w["w_conv_branch"], w["w_att_branch"], w["b_conv_branch"]],
        [_sds((s, D_MODEL), BF16)] * 4 + [_sds((s, CONV_DIM)), _sds((s, ATT_DIM), BF16)], [_sds((1, D_MODEL))], tm=512,
        exchange=_pair_exchange([up_slabs]))

    d_w_cb = _matmul(u3, d_conv_out, ta=True, name="d_w_conv_branch", out_dtype=BF16)
    d_w_ab = _matmul(att, d_att_out, ta=True, name="d_w_att_branch", out_dtype=BF16)

    dq, dk, dv, (received["w_ffn_up"],) = _attn_bwd(
        q, k, v, d_att, ltot, n_blocks, _chip_exchange([_pair_sum("pair_sum_w_ffn_up", up_slabs, theirs)]))

    mix_grads = {"w_conv_branch": d_w_cb, "w_att_branch": d_w_ab, "w_out": d_w_out}
    (d_conv_in, d_dw_w, d_dw_b, d_ln_g, d_ln_b), landed = _conv_bwd(
        conv_in, u1, du3, w["conv_ln_g"], w["conv_ln_b"], w["conv_dw_w"],
        _scatter_exchange([_grad_slabs(nm, mix_grads[nm]) for nm in mix_weights]))
    received.update(zip(mix_weights, landed))

    d_proj = [d_conv_in, dq, dk, dv, d_g_conv, d_g_att]
    d_w_in = _pieces_tn_matmul(d_proj, h1, name="d_w_in")
    in_slabs = _side_slabs("w_in", d_w_in)
    (theirs,) = _exchange_call("pair_swap_w_in", _pair_exchange([in_slabs]))

    early = list(opt)

    def pre_bwd_fn(*args):
        groups, (xt, dx2t), jobs, (w_in_t, g_) = args[:6], args[6:8], args[8:-2], args[-2:]
        dh = sum(jnp.dot(grp.astype(BF16), w_in_t[IN_SPLITS[n]:IN_SPLITS[n + 1]], preferred_element_type=F32)
                 for n, grp in enumerate(groups))
        _, vjp = jax.vjp(_rms, xt, g_)
        dx_, dg_ = vjp(dh)
        updates = [_sum_adamw_tile(*jobs[4 * n:4 * n + 4]) for n in range(len(early))]
        return (dx_ + dx2t, *[u for four in updates for u in four]), (dg_,)

    res = _rowwise(
        "proj_norm_bwd", pre_bwd_fn,
        d_proj + [x, dx2] + [a for nm in early for a in (received[nm], *opt[nm])], [w_in, g1],
        [_sds((s, D_MODEL))] + [_sds(opt[nm][0].shape) for nm in early for _ in range(4)],
        [_sds((1, D_MODEL))], tm=512, exchange=_chip_exchange([_pair_sum("pair_sum_w_in", in_slabs, theirs)]))
    grad_x, d_g1, received["w_in"] = res[0], res[-2], res[-1]
    updated = {nm: res[1 + 4 * n:5 + 4 * n] for n, nm in enumerate(early)}

    grads = {
        "norm_mix_pre": d_g1, "conv_dw_w": d_dw_w, "conv_dw_b": d_dw_b,
        "conv_ln_g": d_ln_g, "conv_ln_b": d_ln_b, "b_conv_branch": d_b_cb,
        "norm_mix_post": d_g2, "norm_ffn_pre": d_g3, "norm_ffn_post": d_g4,
    }
    return loss, grad_x, received, updated, grads


def _place():
    x, y, c = lax.axis_index("x"), lax.axis_index("y"), lax.axis_index("c")
    return x, y, c


def _slot(px, py, pc):
    return 4 * px + 2 * py + pc


def _exchange_scratch(n):
    return [pltpu.SemaphoreType.DMA((7 * n,)), pltpu.SemaphoreType.DMA((7 * n,)), pltpu.SemaphoreType.DMA((n,))]


def _gather_exchange(arrs):
    n = len(arrs)

    def phases(ins, outs, send_sems, recv_sems, local_sems):
        x, y, c = _place()
        me, sibling = (x, y, c), (x, y, 1 - c)
        chips = [(1 - x, y), (x, 1 - y), (1 - x, 1 - y)]

        def copy(a, kk, block, to, src=None):
            dst = outs[a].at[_slot(*block)]
            return pltpu.make_async_remote_copy(
                src_ref=dst if src is None else src, dst_ref=dst,
                send_sem=send_sems.at[a * 7 + kk], recv_sem=recv_sems.at[a * 7 + kk],
                device_id=to, device_id_type=MESH)

        mine = [pltpu.make_async_copy(ins[a], outs[a].at[_slot(*me)], local_sems.at[a]) for a in range(n)]
        first = []
        for a in range(n):
            first.append(copy(a, 0, me, sibling, src=ins[a]))
            first += [copy(a, 1 + j, me, (*chip, c), src=ins[a]) for j, chip in enumerate(chips)]
        passed = [copy(a, 4 + j, (*chip, c), sibling) for j, chip in enumerate(chips) for a in range(n)]

        def send():
            for cp in mine + first:
                cp.start()

        def pass_on():
            for j, chip in enumerate(chips):
                for a in range(n):
                    copy(a, 1 + j, (*chip, c), me).wait_recv()
                    passed[j * n + a].start()

        def finish():
            for a in range(n):
                copy(a, 0, sibling, me).wait_recv()
                for j, chip in enumerate(chips):
                    copy(a, 4 + j, (*chip, 1 - c), me).wait_recv()
            for cp in first + passed:
                cp.wait_send()
            for cp in mine:
                cp.wait()

        return [send, pass_on, finish]

    return list(arrs), [_sds((N_DEV,) + a.shape, a.dtype) for a in arrs], _exchange_scratch(n), phases


def _scatter_exchange(arrs):
    n = len(arrs)
    flips = [(fx, fy, fc) for fx in (0, 1) for fy in (0, 1) for fc in (0, 1)][1:]

    def phases(ins, outs, send_sems, recv_sems, local_sems):
        x, y, c = _place()
        mine = _slot(x, y, c)
        local = [pltpu.make_async_copy(ins[a].at[mine], outs[a].at[mine], local_sems.at[a]) for a in range(n)]
        peers = [((1 - x) if fx else x, (1 - y) if fy else y, (1 - c) if fc else c) for fx, fy, fc in flips]

        def copy(a, kk, src_slot, dst_slot):
            return pltpu.make_async_remote_copy(
                src_ref=ins[a].at[src_slot], dst_ref=outs[a].at[dst_slot],
                send_sem=send_sems.at[a * 7 + kk], recv_sem=recv_sems.at[a * 7 + kk],
                device_id=peers[kk], device_id_type=MESH)

        sends = [copy(a, kk, _slot(*peers[kk]), mine) for a in range(n) for kk in range(7)]

        def send():
            for cp in local + sends:
                cp.start()

        def finish():
            for a in range(n):
                for kk in range(7):
                    copy(a, kk, mine, _slot(*peers[kk])).wait_recv()
            for cp in sends:
                cp.wait_send()
            for cp in local:
                cp.wait()

        return [send, finish]

    return list(arrs), [_sds(a.shape, a.dtype) for a in arrs], _exchange_scratch(n), phases


def _pair_exchange(arrs):
    n = len(arrs)

    def phases(ins, outs, send_sems, recv_sems, local_sems):
        x, y, c = _place()

        def copy(a, chip, side):
            return pltpu.make_async_remote_copy(
                src_ref=ins[a].at[chip, side], dst_ref=outs[a].at[chip],
                send_sem=send_sems.at[a * 7 + chip], recv_sem=recv_sems.at[a * 7 + chip],
                device_id=(x, y, 1 - c), device_id_type=MESH)

        sends = [copy(a, chip, 1 - c) for a in range(n) for chip in range(4)]

        def send():
            for cp in sends:
                cp.start()

        def finish():
            for a in range(n):
                for chip in range(4):
                    copy(a, chip, c).wait_recv()
            for cp in sends:
                cp.wait_send()

        return [send, finish]

    return list(arrs), [_sds((4,) + a.shape[2:], a.dtype) for a in arrs], _exchange_scratch(n), phases


def _chip_exchange(arrs):
    n = len(arrs)

    def phases(ins, outs, send_sems, recv_sems, local_sems):
        x, y, c = _place()
        mine = 2 * x + y
        chips = [(1 - x, y), (x, 1 - y), (1 - x, 1 - y)]
        local = [pltpu.make_async_copy(ins[a].at[mine], outs[a].at[mine], local_sems.at[a]) for a in range(n)]

        def copy(a, j, src_slot, dst_slot):
            return pltpu.make_async_remote_copy(
                src_ref=ins[a].at[src_slot], dst_ref=outs[a].at[dst_slot],
                send_sem=send_sems.at[a * 7 + j], recv_sem=recv_sems.at[a * 7 + j],
                device_id=(*chips[j], c), device_id_type=MESH)

        sends = [copy(a, j, 2 * chips[j][0] + chips[j][1], mine) for a in range(n) for j in range(3)]

        def send():
            for cp in local + sends:
                cp.start()

        def finish():
            for a in range(n):
                for j in range(3):
                    copy(a, j, mine, 2 * chips[j][0] + chips[j][1]).wait_recv()
            for cp in sends:
                cp.wait_send()
            for cp in local:
                cp.wait()

        return [send, finish]

    return list(arrs), [_sds(a.shape, a.dtype) for a in arrs], _exchange_scratch(n), phases


def _pair_sum(name, mine, theirs):
    _, _, r, c = mine.shape

    def body(side_ref, m_ref, t_ref, o_ref):
        o_ref[...] = (m_ref[...].astype(F32) + t_ref[...].astype(F32)).astype(o_ref.dtype)

    return pl.pallas_call(
        body, name=name,
        grid_spec=pltpu.PrefetchScalarGridSpec(
            num_scalar_prefetch=1, grid=(4,),
            in_specs=[pl.BlockSpec((None, None, r, c), lambda j, side: (j, side[0], 0, 0)),
                      pl.BlockSpec((None, r, c), lambda j, side: (j, 0, 0))],
            out_specs=pl.BlockSpec((None, r, c), lambda j, side: (j, 0, 0))),
        out_shape=_sds(theirs.shape, theirs.dtype),
        compiler_params=_params(("parallel",)),
    )(lax.axis_index("c").astype(jnp.int32).reshape(1), mine, theirs)


def _exchange_call(name, exchange):
    arrs, out_shape, scratch, phases = exchange
    n = len(arrs)

    def body(*refs):
        for step in phases(refs[:n], refs[n:2 * n], *refs[2 * n:]):
            step()

    return pl.pallas_call(body, name=name, in_specs=[ANY] * n, out_specs=[ANY] * n,
                          out_shape=out_shape, scratch_shapes=scratch)(*arrs)


def _carry_exchange(exchange, refs, n_in, n_out, first, middle, last):
    arrs, _, _, phases = exchange
    n = len(arrs)
    if n == 0:
        return lambda: None
    ins = refs[n_in:n_in + n]
    outs = refs[n_in + n + n_out:n_in + 2 * n + n_out]
    sems = n_in + 2 * n + n_out
    steps = phases(ins, outs, *refs[sems:sems + 3])
    pl.when(first)(steps[0])
    if len(steps) == 3:
        pl.when(middle)(steps[1])
    return lambda: pl.when(last)(steps[-1])


def _adamw_math(w, g, m, v):
    m2 = ADAM_B1 * m + (1.0 - ADAM_B1) * g
    v2 = ADAM_B2 * v + (1.0 - ADAM_B2) * jnp.square(g)
    m_hat = m2 / (1.0 - ADAM_B1 ** ADAM_STEP)
    v_hat = v2 / (1.0 - ADAM_B2 ** ADAM_STEP)
    delta = -ADAM_LR * (m_hat / (jnp.sqrt(v_hat) + ADAM_EPS) + ADAM_WD * w)
    return delta, m2, v2


def _sum_adamw_tile(parts, w, m, v):
    g = parts[0].astype(F32)
    for d in range(1, parts.shape[0]):
        g = g + parts[d].astype(F32)
    return (g, *_adamw_math(w, g, m, v))


def _sum_adamw(name, parts, w, m, v, tr=256):
    p, r, c = parts.shape
    tr = _pick(r, tr, 16)

    def body(p_ref, w_ref, m_ref, v_ref, g_ref, d_ref, m2_ref, v2_ref):
        g_ref[...], d_ref[...], m2_ref[...], v2_ref[...] = _sum_adamw_tile(p_ref[...], w_ref[...], m_ref[...], v_ref[...])

    tile = pl.BlockSpec((tr, c), lambda i: (i, 0))
    return pl.pallas_call(
        body, name=name, grid=(r // tr,),
        in_specs=[pl.BlockSpec((p, tr, c), lambda i: (0, i, 0)), tile, tile, tile],
        out_specs=[tile] * 4, out_shape=[_sds((r, c))] * 4,
        compiler_params=_params(("parallel",)),
    )(parts, w, m, v)


def _sum_parts(name, parts):
    p, r, c = parts.shape

    def body(p_ref, o_ref):
        g = p_ref[0]
        for d in range(1, p):
            g = g + p_ref[d]
        o_ref[...] = g

    return pl.pallas_call(
        body, name=name, out_shape=_sds((r, c)),
        in_specs=[pl.BlockSpec(memory_space=pltpu.VMEM)], out_specs=pl.BlockSpec(memory_space=pltpu.VMEM),
    )(parts)


WEIGHTS = ["norm_mix_pre", "w_in", "conv_dw_w", "conv_dw_b", "conv_ln_g", "conv_ln_b", "w_conv_branch",
           "b_conv_branch", "w_att_branch", "w_out", "norm_mix_post", "norm_ffn_pre", "w_ffn_up", "w_ffn_down",
           "norm_ffn_post"]
COL_SHARDED = ["w_conv_branch", "w_att_branch"]
ROW_SHARDED = ["w_out", "w_ffn_down"]
TRANSPOSED = ["w_in", "w_ffn_up"]
VECTORS = ["norm_mix_pre", "conv_dw_b", "conv_ln_g", "conv_ln_b", "b_conv_branch", "norm_mix_post",
           "norm_ffn_pre", "norm_ffn_post"]


def _cols_to_full(g):
    return g.transpose(1, 0, 2).reshape(g.shape[1], N_DEV * g.shape[2])


def _full_to_cols(f):
    return f.reshape(f.shape[0], N_DEV, f.shape[1] // N_DEV).transpose(1, 0, 2)


PACK_ROWS = 7


def _pack_vectors(vecs, extra=None):
    parts = [vecs[nm].reshape(-1) for nm in VECTORS]
    parts.append(jnp.zeros((1,), F32) if extra is None else extra.reshape(1))
    used = sum(p.size for p in parts)
    parts.append(jnp.zeros((PACK_ROWS * D_MODEL - used,), F32))
    return jnp.concatenate(parts).reshape(PACK_ROWS, D_MODEL)


def _unpack_vectors(packed, sizes):
    flat, out, at = packed.reshape(-1), {}, 0
    for nm in VECTORS:
        out[nm] = flat[at:at + sizes[nm]]
        at += sizes[nm]
    return out, flat[at]


def kernel(x, norm_mix_pre, w_in, conv_dw_w, conv_dw_b, conv_ln_g, conv_ln_b, w_conv_branch, b_conv_branch, w_att_branch, w_out, norm_mix_post, norm_ffn_pre, w_ffn_up, w_ffn_down, norm_ffn_post, loss_target, m_norm_mix_pre, m_w_in, m_conv_dw_w, m_conv_dw_b, m_conv_ln_g, m_conv_ln_b, m_w_conv_branch, m_b_conv_branch, m_w_att_branch, m_w_out, m_norm_mix_post, m_norm_ffn_pre, m_w_ffn_up, m_w_ffn_down, m_norm_ffn_post, v_norm_mix_pre, v_w_in, v_conv_dw_w, v_conv_dw_b, v_conv_ln_g, v_conv_ln_b, v_w_conv_branch, v_b_conv_branch, v_w_att_branch, v_w_out, v_norm_mix_post, v_norm_ffn_pre, v_w_ffn_up, v_w_ffn_down, v_norm_ffn_post):
    ws = dict(zip(WEIGHTS, [norm_mix_pre, w_in, conv_dw_w, conv_dw_b, conv_ln_g, conv_ln_b, w_conv_branch,
                            b_conv_branch, w_att_branch, w_out, norm_mix_post, norm_ffn_pre, w_ffn_up, w_ffn_down,
                            norm_ffn_post]))
    ms = dict(zip(WEIGHTS, [m_norm_mix_pre, m_w_in, m_conv_dw_w, m_conv_dw_b, m_conv_ln_g, m_conv_ln_b,
                            m_w_conv_branch, m_b_conv_branch, m_w_att_branch, m_w_out, m_norm_mix_post,
                            m_norm_ffn_pre, m_w_ffn_up, m_w_ffn_down, m_norm_ffn_post]))
    vs = dict(zip(WEIGHTS, [v_norm_mix_pre, v_w_in, v_conv_dw_w, v_conv_dw_b, v_conv_ln_g, v_conv_ln_b,
                            v_w_conv_branch, v_b_conv_branch, v_w_att_branch, v_w_out, v_norm_mix_post,
                            v_norm_ffn_pre, v_w_ffn_up, v_w_ffn_down, v_norm_ffn_post]))

    dw_block = jnp.pad(conv_dw_w, ((0, 1), (0, 0)))
    g_in, g_dw = _exchange_call("gather_first", _gather_exchange([w_in.T.astype(BF16), dw_block]))
    full = {"w_in": _full_weight("w_in", g_in), "conv_dw_w": _cols_to_full(g_dw)}
    for nm in VECTORS:
        full[nm] = ws[nm].reshape(1, -1)

    def as_kept(nm, a):
        return a.T if nm in TRANSPOSED else a

    ride_along = ["w_ffn_up", "w_out"]
    loss_local, grad_x, received, updated, grads = _local_step(
        x[0], loss_target[0], full, {nm: as_kept(nm, ws[nm]).astype(BF16) for nm in LATE},
        {nm: tuple(as_kept(nm, a[nm]) for a in (ws, ms, vs)) for nm in ride_along})

    small = _exchange_call("gather_small_grads", _gather_exchange(
        [_pack_vectors(grads, extra=loss_local), grads["conv_dw_w"]]))
    out_g, out_d, out_m, out_v = {}, {}, {}, {}
    for nm in LATE + ["w_in"]:
        res = updated[nm] if nm in updated else _sum_adamw(
            "adamw_" + nm, received[nm], *[as_kept(nm, a[nm]) for a in (ws, ms, vs)])
        out_g[nm], out_d[nm], out_m[nm], out_v[nm] = [as_kept(nm, r) for r in res]
    sizes = {nm: ws[nm].size for nm in VECTORS}
    vec = _sum_adamw("adamw_vectors", small[0], _pack_vectors(ws), _pack_vectors(ms), _pack_vectors(vs))
    for res, dst in zip(vec, (out_g, out_d, out_m, out_v)):
        dst.update(_unpack_vectors(res, sizes)[0])
    loss = _unpack_vectors(vec[0], sizes)[1]
    dw_full = _sum_parts("sum_dw_grads", small[1])
    me = _slot(*_place())
    dw_mine = lax.dynamic_slice(dw_full, (0, me * (CONV_DIM // N_DEV)), (CONV_WIDTH, CONV_DIM // N_DEV))
    nm = "conv_dw_w"
    out_g[nm], out_d[nm], out_m[nm], out_v[nm] = _sum_adamw("adamw_dw", dw_mine[None], ws[nm], ms[nm], vs[nm])

    outs = [loss, grad_x[None]]
    for group in (out_g, out_d, out_m, out_v):
        outs += [group[nm] for nm in WEIGHTS]
    return tuple(outs)
```

```python
import math

import jax
import jax.numpy as jnp
from jax import lax
from jax.experimental import pallas as pl
from jax.experimental.pallas import tpu as pltpu

F32 = jnp.float32
BF16 = jnp.bfloat16

N_DEV = 8
D_MODEL = 1024
CONV_DIM = 512
CONV_WIDTH = 31
N_HEADS = 8
HEAD_DIM = 64
ATT_DIM = N_HEADS * HEAD_DIM
D_FF = 2816
EPS = 1e-6
IN_SPLITS = (0, 1024, 1536, 2048, 2560, 3584, 4608)

ADAM_LR = 0.001
ADAM_B1 = 0.9
ADAM_B2 = 0.999
ADAM_EPS = 1e-08
ADAM_WD = 0.01
ADAM_STEP = 10

LANES = 128
SUBLANES = 8
HALO = 32
ATT_TILE = 256
ATT_PART = 192
DEAD_SUM = -120.0
VMEM_LIMIT = 56 * 1024 * 1024
MESH = pl.DeviceIdType.MESH
ANY = pl.BlockSpec(memory_space=pl.ANY)


def _pick(dim, target, align=LANES):
    t = min(dim, target)
    t -= t % align
    while t >= align:
        if dim % t == 0:
            return t
        t -= align
    return dim


def _params(semantics):
    return pltpu.CompilerParams(dimension_semantics=semantics, vmem_limit_bytes=VMEM_LIMIT)


def _matmul(a, b, *, name, ta=False, tb=False, out_dtype=F32):
    m, k = (a.shape[1], a.shape[0]) if ta else a.shape
    n, k2 = b.shape if tb else (b.shape[1], b.shape[0])
    assert k == k2, (a.shape, b.shape, ta, tb)
    tm, tn, tk = _pick(m, 1408 if ta else 512), _pick(n, 1536), _pick(k, 1536)
    nk = k // tk
    dims = (((0 if ta else 1,), (1 if tb else 0,)), ((), ()))

    def body(a_ref, b_ref, o_ref, *acc):
        part = lax.dot_general(a_ref[...], b_ref[...], dims, preferred_element_type=F32)
        if nk == 1:
            o_ref[...] = part.astype(o_ref.dtype)
            return
        acc_ref, = acc
        kk = pl.program_id(2)

        @pl.when(kk == 0)
        def _():
            acc_ref[...] = part

        @pl.when((kk > 0) & (kk < nk - 1))
        def _():
            acc_ref[...] += part

        @pl.when(kk == nk - 1)
        def _():
            o_ref[...] = (acc_ref[...] + part).astype(o_ref.dtype)

    a_spec = pl.BlockSpec((tk, tm), lambda j, i, kk: (kk, i)) if ta else pl.BlockSpec((tm, tk), lambda j, i, kk: (i, kk))
    b_spec = (pl.BlockSpec((tn, tk), lambda j, i, kk: (j, kk)) if tb
              else pl.BlockSpec((tk, tn), lambda j, i, kk: (kk, j)))
    return pl.pallas_call(
        body, name=name, grid=(n // tn, m // tm, nk),
        in_specs=[a_spec, b_spec],
        out_specs=pl.BlockSpec((tm, tn), lambda j, i, kk: (i, j)),
        out_shape=jax.ShapeDtypeStruct((m, n), out_dtype),
        scratch_shapes=[pltpu.VMEM((tm, tn), F32)] if nk > 1 else [],
        compiler_params=_params(("parallel", "parallel", "arbitrary")),
    )(a, b)


def _pieces_tn_matmul(pieces, b, *, name, tj=512):
    s, n = b.shape
    counts = [p.shape[1] // tj for p in pieces]
    starts = [sum(counts[:i]) for i in range(len(pieces))]
    assert all(p.shape == (s, c * tj) for p, c in zip(pieces, counts))

    def body(*refs):
        b_ref, o_ref = refs[len(pieces):]
        j = pl.program_id(0)
        for p_ref, first, count in zip(refs, starts, counts):
            @pl.when((j >= first) & (j < first + count))
            def _():
                o_ref[...] = lax.dot_general(p_ref[...].astype(BF16), b_ref[...], TN,
                                             preferred_element_type=F32).astype(o_ref.dtype)

    def piece_spec(first, count):
        return pl.BlockSpec((s, tj), lambda j: (0, jnp.clip(j - first, 0, count - 1)))

    return pl.pallas_call(
        body, name=name, grid=(sum(counts),),
        in_specs=[piece_spec(f, c) for f, c in zip(starts, counts)]
        + [pl.BlockSpec((s, n), lambda j: (0, 0), pipeline_mode=pl.Buffered(1))],
        out_specs=pl.BlockSpec((tj, n), lambda j: (j, 0)),
        out_shape=jax.ShapeDtypeStruct((sum(counts) * tj, n), BF16),
        compiler_params=_params(("arbitrary",)),
    )(*pieces, b)


NO_EXCHANGE = ([], [], [], None)


def _sweep_marks(nt):
    i = pl.program_id(0)
    return i == 0, i == (3 * nt) // 4, i == nt - 1


def _rowwise(name, fn, rows, bcasts, row_outs, red_outs=(), tm=256, exchange=NO_EXCHANGE):
    s = rows[0].shape[0]
    tm = _pick(s, tm, 16)
    nt = s // tm
    resident = pl.Buffered(1)
    nr, nb, no, nd = len(rows), len(bcasts), len(row_outs), len(red_outs)
    x_arrs, x_shape, x_scratch, _ = exchange
    nx = len(x_arrs)
    first_out = nr + nb + nx

    def body(*refs):
        finish_exchange = _carry_exchange(exchange, refs, nr + nb, no + nd, *_sweep_marks(nt))
        ins = [r[...] for r in refs[:nr + nb]]
        outs, reds = fn(*ins)
        for ref, val in zip(refs[first_out:first_out + no], outs):
            ref[...] = val.astype(ref.dtype)
        i = pl.program_id(0)
        for ref, val in zip(refs[first_out + no:first_out + no + nd], reds):
            @pl.when(i == 0)
            def _():
                ref[...] = val

            @pl.when(i > 0)
            def _():
                ref[...] += val
        finish_exchange()

    def row_spec(a):
        assert a.shape[-2] % nt == 0, (name, a.shape, nt)
        if len(a.shape) == 3:
            return pl.BlockSpec((a.shape[0], a.shape[1] // nt, a.shape[2]), lambda i: (0, i, 0))
        return pl.BlockSpec((a.shape[0] // nt, a.shape[1]), lambda i: (i, 0))

    in_specs = [row_spec(r) for r in rows]
    in_specs += [pl.BlockSpec(b.shape, lambda i: (0, 0), pipeline_mode=resident) for b in bcasts]
    out_specs = [row_spec(o) for o in row_outs]
    out_specs += [pl.BlockSpec(d.shape, lambda i: (0, 0)) for d in red_outs]
    return pl.pallas_call(
        body, name=name, grid=(nt,), in_specs=in_specs + [ANY] * nx, out_specs=out_specs + [ANY] * nx,
        out_shape=list(row_outs) + list(red_outs) + x_shape, scratch_shapes=x_scratch,
        compiler_params=_params(("arbitrary",)),
    )(*rows, *bcasts, *x_arrs)


def _sds(shape, dtype=F32):
    return jax.ShapeDtypeStruct(shape, dtype)


def _rms(x, g):
    y = x * lax.rsqrt(jnp.mean(x * x, axis=-1, keepdims=True) + EPS)
    return y * g


def _silu(x):
    return x * jax.nn.sigmoid(x)


def _swiglu(g, u):
    return _silu(g) * u


def _ln_silu(u, g, b):
    mu = jnp.mean(u, axis=-1, keepdims=True)
    var = jnp.mean(jnp.square(u - mu), axis=-1, keepdims=True)
    return _silu((u - mu) * lax.rsqrt(var + EPS) * g + b)


def _merge(conv_pre, att_out, g_conv, g_att, b_cb):
    return jax.nn.sigmoid(g_conv) * (conv_pre + b_cb) + jax.nn.sigmoid(g_att) * att_out


def _glu(t):
    return t[:, :CONV_DIM] * jax.nn.sigmoid(t[:, CONV_DIM:])


def _shifted_reader(buf, shifted, tm):
    for b in range(1, SUBLANES):
        shifted[b - 1, :, :] = buf[pl.ds(b, tm + HALO - SUBLANES), :]

    def read(o):
        a, b = divmod(o, SUBLANES)
        return buf[pl.ds(SUBLANES * a, tm), :] if b == 0 else shifted[b - 1, pl.ds(SUBLANES * a, tm), :]

    return read


def _conv_fwd(conv_in, w_pad, b, ln_g, ln_b, exchange, tm=256):
    s = conv_in.shape[0]
    tm = _pick(s, tm, HALO)
    ratio = tm // HALO
    x_arrs, x_shape, x_scratch, _ = exchange
    nx = len(x_arrs)

    def body(*refs):
        main_ref, halo_ref, w_ref, b_ref, g_ref, be_ref = refs[:6]
        u3_ref, u1_ref = refs[6 + nx:8 + nx]
        buf, shifted = refs[-2:]
        finish_exchange = _carry_exchange(exchange, refs, 6, 2, *_sweep_marks(s // tm))
        i = pl.program_id(0)
        buf[0:HALO, :] = _glu(halo_ref[...]) * (i > 0).astype(F32)
        buf[HALO:HALO + tm, :] = _glu(main_ref[...])
        read = _shifted_reader(buf, shifted, tm)
        acc = jnp.zeros((tm, CONV_DIM), F32) + b_ref[...]
        for j in range(CONV_WIDTH):
            acc = acc + w_ref[j:j + 1, :] * read(HALO - (CONV_WIDTH - 1) + j)
        u1_ref[...] = acc
        u3_ref[...] = _ln_silu(acc, g_ref[...], be_ref[...]).astype(u3_ref.dtype)
        finish_exchange()

    res = pl.pallas_call(
        body, name="conv_fwd", grid=(s // tm,),
        in_specs=[pl.BlockSpec((tm, 2 * CONV_DIM), lambda i: (i, 0)),
                  pl.BlockSpec((HALO, 2 * CONV_DIM), lambda i: (jnp.maximum(i * ratio - 1, 0), 0)),
                  pl.BlockSpec(w_pad.shape, lambda i: (0, 0)),
                  pl.BlockSpec(b.shape, lambda i: (0, 0)),
                  pl.BlockSpec(ln_g.shape, lambda i: (0, 0)),
                  pl.BlockSpec(ln_b.shape, lambda i: (0, 0))] + [ANY] * nx,
        out_specs=[pl.BlockSpec((tm, CONV_DIM), lambda i: (i, 0)),
                   pl.BlockSpec((tm, CONV_DIM), lambda i: (i, 0))] + [ANY] * nx,
        out_shape=[_sds((s, CONV_DIM), BF16), _sds((s, CONV_DIM), F32)] + x_shape,
        scratch_shapes=x_scratch + [pltpu.VMEM((tm + HALO, CONV_DIM), F32),
                                    pltpu.VMEM((SUBLANES - 1, tm + HALO - SUBLANES, CONV_DIM), F32)],
        compiler_params=_params(("arbitrary",)),
    )(conv_in, conv_in, w_pad, b, ln_g, ln_b, *x_arrs)
    return res[0], res[1], res[2:]


def _conv_bwd(conv_in, u1, du3, ln_g, ln_b, w_pad, exchange, tm=256):
    s = conv_in.shape[0]
    tm = _pick(s, tm, HALO)
    ratio = tm // HALO
    nt = s // tm
    last_halo = s // HALO - 1
    x_arrs, x_shape, x_scratch, _ = exchange
    nx = len(x_arrs)

    def body(*refs):
        main_ref, halo_ref, u1_ref, u1n_ref, du3_ref, du3n_ref, g_ref, be_ref, w_ref = refs[:9]
        dci_ref, dw_ref, db_ref, dg_ref, dbe_ref = refs[9 + nx:14 + nx]
        ubuf, dbuf, ushift, dshift = refs[-4:]
        finish_exchange = _carry_exchange(exchange, refs, 9, 5, *_sweep_marks(nt))
        i = pl.program_id(0)
        main = main_ref[...]
        a = main[:, :CONV_DIM]
        sb = jax.nn.sigmoid(main[:, CONV_DIM:])
        ubuf[0:HALO, :] = _glu(halo_ref[...]) * (i > 0).astype(F32)
        ubuf[HALO:HALO + tm, :] = a * sb

        def ln_bwd(u1t, du3t):
            _, vjp = jax.vjp(_ln_silu, u1t, g_ref[...], be_ref[...])
            return vjp(du3t)

        du, dg, dbe = ln_bwd(u1_ref[...], du3_ref[...])
        dbuf[0:tm, :] = du
        dbuf[tm:tm + HALO, :] = ln_bwd(u1n_ref[...], du3n_ref[...])[0] * (i < nt - 1).astype(F32)

        @pl.when(i == 0)
        def _():
            dw_ref[...] = jnp.zeros_like(dw_ref)
            db_ref[...] = jnp.zeros_like(db_ref)
            dg_ref[...] = jnp.zeros_like(dg_ref)
            dbe_ref[...] = jnp.zeros_like(dbe_ref)

        dg_ref[...] += dg
        dbe_ref[...] += dbe

        read_u = _shifted_reader(ubuf, ushift, tm)
        read_d = _shifted_reader(dbuf, dshift, tm)
        du0 = jnp.zeros((tm, CONV_DIM), F32)
        for j in range(CONV_WIDTH):
            du0 = du0 + w_ref[j:j + 1, :] * read_d(CONV_WIDTH - 1 - j)
            dw_ref[j:j + 1, :] += jnp.sum(du * read_u(HALO - (CONV_WIDTH - 1) + j), axis=0, keepdims=True)
        db_ref[...] += jnp.sum(du, axis=0, keepdims=True)
        dci_ref[:, :CONV_DIM] = (du0 * sb).astype(dci_ref.dtype)
        dci_ref[:, CONV_DIM:] = (du0 * a * sb * (1.0 - sb)).astype(dci_ref.dtype)
        finish_exchange()

    res = pl.pallas_call(
        body, name="conv_bwd", grid=(nt,),
        in_specs=[pl.BlockSpec((tm, 2 * CONV_DIM), lambda i: (i, 0)),
                  pl.BlockSpec((HALO, 2 * CONV_DIM), lambda i: (jnp.maximum(i * ratio - 1, 0), 0))]
        + [pl.BlockSpec((tm, CONV_DIM), lambda i: (i, 0)),
           pl.BlockSpec((HALO, CONV_DIM), lambda i: (jnp.minimum((i + 1) * ratio, last_halo), 0))] * 2
        + [pl.BlockSpec((1, CONV_DIM), lambda i: (0, 0))] * 2 + [pl.BlockSpec(w_pad.shape, lambda i: (0, 0))]
        + [ANY] * nx,
        out_specs=[pl.BlockSpec((tm, 2 * CONV_DIM), lambda i: (i, 0)),
                   pl.BlockSpec(w_pad.shape, lambda i: (0, 0))]
        + [pl.BlockSpec((1, CONV_DIM), lambda i: (0, 0))] * 3 + [ANY] * nx,
        out_shape=[_sds((s, 2 * CONV_DIM), BF16), _sds(w_pad.shape)] + [_sds((1, CONV_DIM))] * 3 + x_shape,
        scratch_shapes=x_scratch + [pltpu.VMEM((tm + HALO, CONV_DIM), F32)] * 2
        + [pltpu.VMEM((SUBLANES - 1, tm + HALO - SUBLANES, CONV_DIM), F32)] * 2,
        compiler_params=_params(("arbitrary",)),
    )(conv_in, conv_in, u1, u1, du3, du3, ln_g, ln_b, w_pad, *x_arrs)
    return res[:5], res[5:]


def _logsig_neg(z):
    return jnp.minimum(-z, 0.0) - jnp.log(1.0 + jnp.exp(-jnp.abs(z)))


def _split_dot(val, tri):
    hi = val.astype(BF16)
    lo = (val - hi.astype(F32)).astype(BF16)
    return jnp.dot(hi, tri, preferred_element_type=F32) + jnp.dot(lo, tri, preferred_element_type=F32)


def _attn_masks(t, later):
    row = lax.broadcasted_iota(jnp.int32, (t, t), 0)
    col = lax.broadcasted_iota(jnp.int32, (t, t), 1)
    tri = jnp.where(row > col if later else row <= col, 1.0, 0.0).astype(BF16)
    return col < row, tri


def _grid_marks(h, nq):
    hh, i = pl.program_id(0), pl.program_id(1)
    return (hh == 0) & (i == 0), (hh == (3 * h) // 4) & (i == 0), (hh == h - 1) & (i == nq - 1)


def _head_masks(shape):
    lane = lax.broadcasted_iota(jnp.int32, shape, len(shape) - 1)
    return lane < HEAD_DIM, lane >= HEAD_DIM


def _per_head(blk):
    m0, m1 = _head_masks(blk.shape)
    zero = jnp.zeros_like(blk)
    return jnp.where(m0, blk, zero), jnp.where(m1, blk, zero)


NT = (((1,), (1,)), ((), ()))
TN = (((0,), (0,)), ((), ()))


def _with_top(whole, top):
    rows = top.shape[0]
    return top if rows == whole.shape[0] else jnp.concatenate([top, whole[rows:]], axis=0)


def _attn_fwd(q, k, v, exchange):
    s = q.shape[0]
    hp = q.shape[1] // LANES
    t = ATT_TILE
    scale = 1.0 / math.sqrt(HEAD_DIM)
    x_arrs, x_shape, x_scratch, _ = exchange
    nx = len(x_arrs)

    def body(*refs):
        q_ref, k_ref, v_ref = refs[:3]
        o_ref, lt_ref, nb_ref = refs[3 + nx:6 + nx]
        finish_exchange = _carry_exchange(exchange, refs, 3, 3, *_grid_marks(hp, s // t))
        i = pl.program_id(1)
        qs = _per_head((q_ref[...].astype(F32) * scale).astype(BF16))
        causal, tri = _attn_masks(t, later=True)

        def step(kb, carry, masked, rows):
            cs, acc = carry
            off = pl.multiple_of(kb * t, t)
            kblk = k_ref[pl.ds(off, t), :]
            vs = _per_head(v_ref[pl.ds(off, t), :])
            acc_top = acc[:rows]
            new_cs = []
            for hd in range(2):
                z = lax.dot_general(qs[hd][:rows], kblk, NT, preferred_element_type=F32)
                l = _logsig_neg(z)
                if masked:
                    l = jnp.where(causal, l, 0.0)
                e = z + l + _split_dot(l, tri) + cs[hd][:rows]
                if masked:
                    e = jnp.where(causal, e, -1e30)
                acc_top = acc_top + jnp.dot(jnp.exp(e).astype(BF16), vs[hd], preferred_element_type=F32)
                new_cs.append(_with_top(cs[hd], cs[hd][:rows] + jnp.sum(l, axis=1, keepdims=True)))
            return tuple(new_cs), _with_top(acc, acc_top)

        zero = jnp.zeros((t, 1), F32)
        carry = step(i, ((zero, zero), jnp.zeros((t, LANES), F32)), True, t)

        def live(cs, lo, hi):
            return jnp.maximum(jnp.max(cs[0][lo:hi]), jnp.max(cs[1][lo:hi])) > DEAD_SUM

        def more(state):
            n, _, (cs, _) = state
            return (n < i) & live(cs, 0, t)

        def sweep(state):
            n, n_full, cr = state
            whole = live(cr[0], ATT_PART, t)
            cr = lax.cond(whole, lambda c: step(i - 1 - n, c, False, t), lambda c: step(i - 1 - n, c, False, ATT_PART), cr)
            return n + 1, n_full + whole.astype(jnp.int32), cr

        n_blocks, n_full, carry = lax.while_loop(more, sweep, (jnp.int32(0), jnp.int32(0), carry))
        m0, _ = _head_masks((t, LANES))
        lt_ref[...] = jnp.where(m0, carry[0][0], carry[0][1])
        o_ref[...] = carry[1].astype(o_ref.dtype)
        nb_ref[0, pl.program_id(0), i] = n_blocks.astype(F32)
        nb_ref[1, pl.program_id(0), i] = n_full.astype(F32)
        finish_exchange()

    res = pl.pallas_call(
        body, name="attn_fwd", grid=(hp, s // t),
        in_specs=[pl.BlockSpec((t, LANES), lambda p, i: (i, p)),
                  pl.BlockSpec((s, LANES), lambda p, i: (0, p)),
                  pl.BlockSpec((s, LANES), lambda p, i: (0, p))] + [ANY] * nx,
        out_specs=[pl.BlockSpec((t, LANES), lambda p, i: (i, p)),
                   pl.BlockSpec((None, t, LANES), lambda p, i: (p, i, 0)),
                   pl.BlockSpec(memory_space=pltpu.SMEM)] + [ANY] * nx,
        out_shape=[_sds(q.shape, BF16), _sds((hp, s, LANES), F32), _sds((2, hp, s // t), F32)] + x_shape,
        scratch_shapes=x_scratch,
        compiler_params=_params(("arbitrary", "arbitrary")),
    )(q, k, v, *x_arrs)
    return res[0], res[1], res[2], res[3:]


def _attn_bwd(q, k, v, do, ltot, n_blocks, exchange):
    s = q.shape[0]
    hp = q.shape[1] // LANES
    t = ATT_TILE
    scale = 1.0 / math.sqrt(HEAD_DIM)
    x_arrs, x_shape, x_scratch, _ = exchange
    nx = len(x_arrs)

    def body(*refs):
        q_ref, k_ref, v_ref, do_ref, lt_ref, nb_ref = refs[:6]
        dq_ref, dk_ref, dv_ref = refs[6 + nx:9 + nx]
        finish_exchange = _carry_exchange(exchange, refs, 6, 3, *_grid_marks(hp, s // t))
        i = pl.program_id(1)
        n_blocks = jnp.clip(nb_ref[0, pl.program_id(0), i].astype(jnp.int32), 0, i)
        n_full = jnp.clip(nb_ref[1, pl.program_id(0), i].astype(jnp.int32), 0, n_blocks)

        @pl.when(i == 0)
        def _():
            dk_ref[...] = jnp.zeros_like(dk_ref)
            dv_ref[...] = jnp.zeros_like(dv_ref)

        qb = q_ref[...]
        qm = _per_head(qb)
        qs = _per_head((qb.astype(F32) * scale).astype(BF16))
        dos = _per_head(do_ref[...])
        lts = (lt_ref[:, 0:1], lt_ref[:, HEAD_DIM:HEAD_DIM + 1])
        causal, tri = _attn_masks(t, later=False)

        def step(kb, carry, masked, rows):
            cls, cgs, dq = carry
            off = pl.multiple_of(kb * t, t)
            kblk = k_ref[pl.ds(off, t), :]
            vblk = v_ref[pl.ds(off, t), :]
            ks = _per_head(kblk)
            dq_top = dq[:rows]
            dk = jnp.zeros((t, LANES), F32)
            dv = jnp.zeros((t, LANES), F32)
            new_cls, new_cgs = [], []
            for hd in range(2):
                z = lax.dot_general(qs[hd][:rows], kblk, NT, preferred_element_type=F32)
                l = _logsig_neg(z)
                if masked:
                    l = jnp.where(causal, l, 0.0)
                e = z + l + ((lts[hd][:rows] - cls[hd][:rows]) - _split_dot(l, tri))
                if masked:
                    e = jnp.where(causal, e, -1e30)
                a = jnp.exp(e)
                g = lax.dot_general(dos[hd][:rows], vblk, NT, preferred_element_type=F32) * a
                p = cgs[hd][:rows] + jnp.dot(g.astype(BF16), tri, preferred_element_type=F32) - g
                el = jnp.exp(l)
                dz = g * el - p * (1.0 - el)
                if masked:
                    dz = jnp.where(causal, dz, 0.0)
                dzb = (dz * scale).astype(BF16)
                dq_top = dq_top + jnp.dot(dzb, ks[hd], preferred_element_type=F32)
                dk = dk + lax.dot_general(dzb, qm[hd][:rows], TN, preferred_element_type=F32)
                dv = dv + lax.dot_general(a.astype(BF16), dos[hd][:rows], TN, preferred_element_type=F32)
                new_cls.append(_with_top(cls[hd], cls[hd][:rows] + jnp.sum(l, axis=1, keepdims=True)))
                new_cgs.append(_with_top(cgs[hd], cgs[hd][:rows] + jnp.sum(g, axis=1, keepdims=True)))
            dk_ref[pl.ds(off, t), :] += dk
            dv_ref[pl.ds(off, t), :] += dv
            return tuple(new_cls), tuple(new_cgs), _with_top(dq, dq_top)

        zero = jnp.zeros((t, 1), F32)
        init = ((zero, zero), (zero, zero), jnp.zeros((t, LANES), F32))
        carry = lax.fori_loop(i - n_blocks, i - n_full, lambda kb, cr: step(kb, cr, False, ATT_PART), init)
        carry = lax.fori_loop(i - n_full, i, lambda kb, cr: step(kb, cr, False, t), carry)
        carry = step(i, carry, True, t)
        dq_ref[...] = carry[2]
        finish_exchange()

    blk = pl.BlockSpec((t, LANES), lambda p, i: (i, p))
    whole = pl.BlockSpec((s, LANES), lambda p, i: (0, p))
    res = pl.pallas_call(
        body, name="attn_bwd", grid=(hp, s // t),
        in_specs=[blk, whole, whole, blk, pl.BlockSpec((None, t, LANES), lambda p, i: (p, i, 0)),
                  pl.BlockSpec(memory_space=pltpu.SMEM)] + [ANY] * nx,
        out_specs=[blk, whole, whole] + [ANY] * nx,
        out_shape=[_sds(q.shape)] * 3 + x_shape,
        scratch_shapes=x_scratch,
        compiler_params=_params(("arbitrary", "arbitrary")),
    )(q, k, v, do, ltot, n_blocks, *x_arrs)
    return res[0], res[1], res[2], res[3:]


LATE = ["w_conv_branch", "w_att_branch", "w_out", "w_ffn_up", "w_ffn_down"]


def _full_weight(name, gathered):
    return _cols_to_full(gathered) if name in COL_SHARDED else gathered.reshape(-1, gathered.shape[2])


def _grad_slabs(name, grad):
    return _full_to_cols(grad) if name in COL_SHARDED else grad.reshape(N_DEV, -1, grad.shape[1])


def _side_slabs(name, grad):
    slabs = _grad_slabs(name, grad)
    return slabs.reshape((4, 2) + slabs.shape[1:])


def _local_step(x, target, w, late_blocks, opt):
    s = x.shape[0]
    w = dict(w)
    g1, g2, g3, g4 = w["norm_mix_pre"], w["norm_mix_post"], w["norm_ffn_pre"], w["norm_ffn_post"]

    w_in = w["w_in"]

    def proj_fn(xt, g1_, w_in_t):
        h = _rms(xt, g1_).astype(BF16)
        proj = lax.dot_general(h, w_in_t, NT, preferred_element_type=F32)
        return (h, *[proj[:, IN_SPLITS[n]:IN_SPLITS[n + 1]] for n in range(6)]), ()

    mix_weights = ["w_conv_branch", "w_att_branch", "w_out"]
    h1, conv_in, q, k, v, g_conv, g_att = _rowwise(
        "norm_proj", proj_fn, [x], [g1, w_in],
        [_sds((s, D_MODEL), BF16), _sds((s, 2 * CONV_DIM)), _sds((s, ATT_DIM), BF16), _sds((s, ATT_DIM), BF16),
         _sds((s, ATT_DIM), BF16), _sds((s, D_MODEL), BF16), _sds((s, D_MODEL), BF16)], tm=512)

    u3, u1, gathered = _conv_fwd(conv_in, w["conv_dw_w"], w["conv_dw_b"], w["conv_ln_g"], w["conv_ln_b"],
                                 _gather_exchange([late_blocks[nm] for nm in mix_weights]))
    for nm, g in zip(mix_weights, gathered):
        w[nm] = _full_weight(nm, g)
    att, ltot, n_blocks, (g_up,) = _attn_fwd(q, k, v, _gather_exchange([late_blocks["w_ffn_up"]]))
    w["w_ffn_up"] = _full_weight("w_ffn_up", g_up)

    def merge_fn(u3t, at, gc, ga, xt, w_cb, w_ab, b_cb, w_out, g2_, g3_):
        cp = jnp.dot(u3t, w_cb, preferred_element_type=F32)
        ao = jnp.dot(at, w_ab, preferred_element_type=F32)
        mg = _merge(cp, ao, gc.astype(F32), ga.astype(F32), b_cb).astype(BF16)
        mix_ = jnp.dot(mg, w_out, preferred_element_type=F32)
        x2_ = xt + _rms(mix_, g2_)
        return (mg, cp, ao, mix_, x2_, _rms(x2_, g3_)), ()

    merged, conv_pre, att_out, mix, x2, h2 = _rowwise(
        "branch_merge_mix", merge_fn, [u3, att, g_conv, g_att, x],
        [w["w_conv_branch"], w["w_att_branch"], w["b_conv_branch"], w["w_out"], g2, g3],
        [_sds((s, D_MODEL), BF16)] * 3 + [_sds((s, D_MODEL)), _sds((s, D_MODEL)), _sds((s, D_MODEL), BF16)], tm=512)

    def ffn_up_fn(ht, w_up_t):
        gu_ = lax.dot_general(ht, w_up_t, NT, preferred_element_type=F32)
        return (gu_, _swiglu(gu_[:, :D_FF], gu_[:, D_FF:])), ()

    gu, act, g_down = _rowwise("ffn_up", ffn_up_fn, [h2], [w["w_ffn_up"]],
                               [_sds((s, 2 * D_FF), BF16), _sds((s, D_FF), BF16)], tm=512,
                               exchange=_gather_exchange([late_blocks["w_ffn_down"]]))
    w["w_ffn_down"] = _full_weight("w_ffn_down", g_down)

    def final_fn(at, x2t, tgt, w_down, g4_):
        ff = jnp.dot(at, w_down, preferred_element_type=F32)
        n4, vjp = jax.vjp(_rms, ff, g4_)
        err = x2t + n4 - tgt
        dy = err * (1.0 / D_MODEL)
        dff, dg4 = vjp(dy)
        return (dy, dff), (jnp.sum(err * err, axis=0, keepdims=True), dg4)

    dy, dff, loss_cols, d_g4 = _rowwise("ffn_down_loss", final_fn, [act, x2, target], [w["w_ffn_down"], g4],
                                        [_sds((s, D_MODEL)), _sds((s, D_MODEL), BF16)],
                                        [_sds((1, D_MODEL)), _sds((1, D_MODEL))], tm=512)
    loss = 0.5 * jnp.sum(loss_cols) / D_MODEL

    d_w_down = _matmul(act, dff, ta=True, name="d_w_down", out_dtype=BF16)

    def act_bwd_fn(dfft, gut, w_down):
        d_act = lax.dot_general(dfft, w_down, NT, preferred_element_type=F32)
        gu_ = gut.astype(F32)
        _, vjp = jax.vjp(_swiglu, gu_[:, :D_FF], gu_[:, D_FF:])
        return (jnp.concatenate(vjp(d_act), axis=1),), ()

    down_slabs = _side_slabs("w_ffn_down", d_w_down)
    dgu, theirs = _rowwise("ffn_act_bwd", act_bwd_fn, [dff, gu], [w["w_ffn_down"]], [_sds((s, 2 * D_FF), BF16)],
                           exchange=_pair_exchange([down_slabs]))
    down_sums = _pair_sum("pair_sum_w_ffn_down", down_slabs, theirs)
    d_w_up = _matmul(dgu, h2, ta=True, name="d_w_up", out_dtype=BF16)
    received = {}
    up_slabs = _side_slabs("w_ffn_up", d_w_up)

    def mid_bwd_fn(dgut, xt, mt, dyt, w_up_t, g2_, g3_):
        dh = jnp.dot(dgut, w_up_t, preferred_element_type=F32)
        n2, vjp2 = jax.vjp(_rms, mt, g2_)
        x2_ = xt + n2
        _, vjp3 = jax.vjp(_rms, x2_, g3_)
        dx2_, dg3 = vjp3(dh)
        dx2_ = dx2_ + dyt
        dmix_, dg2 = vjp2(dx2_)
        return (dx2_, dmix_), (dg2, dg3)

    dx2, dmix, d_g2, d_g3, received["w_ffn_down"] = _rowwise(
        "ffn_up_mid_bwd", mid_bwd_fn, [dgu, x, mix, dy], [w["w_ffn_up"], g2, g3],
        [_sds((s, D_MODEL)), _sds((s, D_MODEL), BF16)], [_sds((1, D_MODEL)), _sds((1, D_MODEL))], tm=512,
        exchange=_chip_exchange([down_sums]))
    d_w_out = _matmul(merged, dmix, ta=True, name="d_w_out", out_dtype=BF16)

    def merge_bwd_fn(dmt, cp, ao, gc, ga, w_out, w_cb, w_ab, b_cb):
        dm = lax.dot_general(dmt, w_out, NT, preferred_element_type=F32)
        _, vjp = jax.vjp(_merge, cp.astype(F32), ao.astype(F32), gc.astype(F32), ga.astype(F32), b_cb)
        dcp, dao, dgc, dga, dbias = vjp(dm)
        dcp, dao = dcp.astype(BF16), dao.astype(BF16)
        du3_ = lax.dot_general(dcp, w_cb, NT, preferred_element_type=F32)
        datt_ = lax.dot_general(dao, w_ab, NT, preferred_element_type=F32)
        return (dcp, dao, dgc, dga, du3_, datt_), (dbias,)

    d_conv_out, d_att_out, d_g_conv, d_g_att, du3, d_att, d_b_cb, theirs = _rowwise(
        "merge_bwd", merge_bwd_fn, [dmix, conv_pre, att_out, g_conv, g_att],
        [w["w_out"], w["w_conv_branch"], w["w_att_branch"], w["b_conv_branch"]],
        [_sds((s, D_MODEL), BF16)] * 4 + [_sds((s, CONV_DIM)), _sds((s, ATT_DIM), BF16)], [_sds((1, D_MODEL))], tm=512,
        exchange=_pair_exchange([up_slabs]))

    d_w_cb = _matmul(u3, d_conv_out, ta=True, name="d_w_conv_branch", out_dtype=BF16)
    d_w_ab = _matmul(att, d_att_out, ta=True, name="d_w_att_branch", out_dtype=BF16)

    dq, dk, dv, (received["w_ffn_up"],) = _attn_bwd(
        q, k, v, d_att, ltot, n_blocks, _chip_exchange([_pair_sum("pair_sum_w_ffn_up", up_slabs, theirs)]))

    mix_grads = {"w_conv_branch": d_w_cb, "w_att_branch": d_w_ab, "w_out": d_w_out}
    (d_conv_in, d_dw_w, d_dw_b, d_ln_g, d_ln_b), landed = _conv_bwd(
        conv_in, u1, du3, w["conv_ln_g"], w["conv_ln_b"], w["conv_dw_w"],
        _scatter_exchange([_grad_slabs(nm, mix_grads[nm]) for nm in mix_weights]))
    received.update(zip(mix_weights, landed))

    d_proj = [d_conv_in, dq, dk, dv, d_g_conv, d_g_att]
    d_w_in = _pieces_tn_matmul(d_proj, h1, name="d_w_in")
    in_slabs = _side_slabs("w_in", d_w_in)
    (theirs,) = _exchange_call("pair_swap_w_in", _pair_exchange([in_slabs]))

    early = list(opt)

    def pre_bwd_fn(*args):
        groups, (xt, dx2t), jobs, (w_in_t, g_) = args[:6], args[6:8], args[8:-2], args[-2:]
        dh = sum(jnp.dot(grp.astype(BF16), w_in_t[IN_SPLITS[n]:IN_SPLITS[n + 1]], preferred_element_type=F32)
                 for n, grp in enumerate(groups))
        _, vjp = jax.vjp(_rms, xt, g_)
        dx_, dg_ = vjp(dh)
        updates = [_sum_adamw_tile(*jobs[4 * n:4 * n + 4]) for n in range(len(early))]
        return (dx_ + dx2t, *[u for four in updates for u in four]), (dg_,)

    res = _rowwise(
        "proj_norm_bwd", pre_bwd_fn,
        d_proj + [x, dx2] + [a for nm in early for a in (received[nm], *opt[nm])], [w_in, g1],
        [_sds((s, D_MODEL))] + [_sds(opt[nm][0].shape) for nm in early for _ in range(4)],
        [_sds((1, D_MODEL))], tm=512, exchange=_chip_exchange([_pair_sum("pair_sum_w_in", in_slabs, theirs)]))
    grad_x, d_g1, received["w_in"] = res[0], res[-2], res[-1]
    updated = {nm: res[1 + 4 * n:5 + 4 * n] for n, nm in enumerate(early)}

    grads = {
        "norm_mix_pre": d_g1, "conv_dw_w": d_dw_w, "conv_dw_b": d_dw_b,
        "conv_ln_g": d_ln_g, "conv_ln_b": d_ln_b, "b_conv_branch": d_b_cb,
        "norm_mix_post": d_g2, "norm_ffn_pre": d_g3, "norm_ffn_post": d_g4,
    }
    return loss, grad_x, received, updated, grads


def _place():
    x, y, c = lax.axis_index("x"), lax.axis_index("y"), lax.axis_index("c")
    return x, y, c


def _slot(px, py, pc):
    return 4 * px + 2 * py + pc


def _exchange_scratch(n):
    return [pltpu.SemaphoreType.DMA((7 * n,)), pltpu.SemaphoreType.DMA((7 * n,)), pltpu.SemaphoreType.DMA((n,))]


def _gather_exchange(arrs):
    n = len(arrs)

    def phases(ins, outs, send_sems, recv_sems, local_sems):
        x, y, c = _place()
        me, sibling = (x, y, c), (x, y, 1 - c)
        chips = [(1 - x, y), (x, 1 - y), (1 - x, 1 - y)]

        def copy(a, kk, block, to, src=None):
            dst = outs[a].at[_slot(*block)]
            return pltpu.make_async_remote_copy(
                src_ref=dst if src is None else src, dst_ref=dst,
                send_sem=send_sems.at[a * 7 + kk], recv_sem=recv_sems.at[a * 7 + kk],
                device_id=to, device_id_type=MESH)

        mine = [pltpu.make_async_copy(ins[a], outs[a].at[_slot(*me)], local_sems.at[a]) for a in range(n)]
        first = []
        for a in range(n):
            first.append(copy(a, 0, me, sibling, src=ins[a]))
            first += [copy(a, 1 + j, me, (*chip, c), src=ins[a]) for j, chip in enumerate(chips)]
        passed = [copy(a, 4 + j, (*chip, c), sibling) for j, chip in enumerate(chips) for a in range(n)]

        def send():
            for cp in mine + first:
                cp.start()

        def pass_on():
            for j, chip in enumerate(chips):
                for a in range(n):
                    copy(a, 1 + j, (*chip, c), me).wait_recv()
                    passed[j * n + a].start()

        def finish():
            for a in range(n):
                copy(a, 0, sibling, me).wait_recv()
                for j, chip in enumerate(chips):
                    copy(a, 4 + j, (*chip, 1 - c), me).wait_recv()
            for cp in first + passed:
                cp.wait_send()
            for cp in mine:
                cp.wait()

        return [send, pass_on, finish]

    return list(arrs), [_sds((N_DEV,) + a.shape, a.dtype) for a in arrs], _exchange_scratch(n), phases


def _scatter_exchange(arrs):
    n = len(arrs)
    flips = [(fx, fy, fc) for fx in (0, 1) for fy in (0, 1) for fc in (0, 1)][1:]

    def phases(ins, outs, send_sems, recv_sems, local_sems):
        x, y, c = _place()
        mine = _slot(x, y, c)
        local = [pltpu.make_async_copy(ins[a].at[mine], outs[a].at[mine], local_sems.at[a]) for a in range(n)]
        peers = [((1 - x) if fx else x, (1 - y) if fy else y, (1 - c) if fc else c) for fx, fy, fc in flips]

        def copy(a, kk, src_slot, dst_slot):
            return pltpu.make_async_remote_copy(
                src_ref=ins[a].at[src_slot], dst_ref=outs[a].at[dst_slot],
                send_sem=send_sems.at[a * 7 + kk], recv_sem=recv_sems.at[a * 7 + kk],
                device_id=peers[kk], device_id_type=MESH)

        sends = [copy(a, kk, _slot(*peers[kk]), mine) for a in range(n) for kk in range(7)]

        def send():
            for cp in local + sends:
                cp.start()

        def finish():
            for a in range(n):
                for kk in range(7):
                    copy(a, kk, mine, _slot(*peers[kk])).wait_recv()
            for cp in sends:
                cp.wait_send()
            for cp in local:
                cp.wait()

        return [send, finish]

    return list(arrs), [_sds(a.shape, a.dtype) for a in arrs], _exchange_scratch(n), phases


def _pair_exchange(arrs):
    n = len(arrs)

    def phases(ins, outs, send_sems, recv_sems, local_sems):
        x, y, c = _place()

        def copy(a, chip, side):
            return pltpu.make_async_remote_copy(
                src_ref=ins[a].at[chip, side], dst_ref=outs[a].at[chip],
                send_sem=send_sems.at[a * 7 + chip], recv_sem=recv_sems.at[a * 7 + chip],
                device_id=(x, y, 1 - c), device_id_type=MESH)

        sends = [copy(a, chip, 1 - c) for a in range(n) for chip in range(4)]

        def send():
            for cp in sends:
                cp.start()

        def finish():
            for a in range(n):
                for chip in range(4):
                    copy(a, chip, c).wait_recv()
            for cp in sends:
                cp.wait_send()

        return [send, finish]

    return list(arrs), [_sds((4,) + a.shape[2:], a.dtype) for a in arrs], _exchange_scratch(n), phases


def _chip_exchange(arrs):
    n = len(arrs)

    def phases(ins, outs, send_sems, recv_sems, local_sems):
        x, y, c = _place()
        mine = 2 * x + y
        chips = [(1 - x, y), (x, 1 - y), (1 - x, 1 - y)]
        local = [pltpu.make_async_copy(ins[a].at[mine], outs[a].at[mine], local_sems.at[a]) for a in range(n)]

        def copy(a, j, src_slot, dst_slot):
            return pltpu.make_async_remote_copy(
                src_ref=ins[a].at[src_slot], dst_ref=outs[a].at[dst_slot],
                send_sem=send_sems.at[a * 7 + j], recv_sem=recv_sems.at[a * 7 + j],
                device_id=(*chips[j], c), device_id_type=MESH)

        sends = [copy(a, j, 2 * chips[j][0] + chips[j][1], mine) for a in range(n) for j in range(3)]

        def send():
            for cp in local + sends:
                cp.start()

        def finish():
            for a in range(n):
                for j in range(3):
                    copy(a, j, mine, 2 * chips[j][0] + chips[j][1]).wait_recv()
            for cp in sends:
                cp.wait_send()
            for cp in local:
                cp.wait()

        return [send, finish]

    return list(arrs), [_sds(a.shape, a.dtype) for a in arrs], _exchange_scratch(n), phases


def _pair_sum(name, mine, theirs):
    _, _, r, c = mine.shape

    def body(side_ref, m_ref, t_ref, o_ref):
        o_ref[...] = (m_ref[...].astype(F32) + t_ref[...].astype(F32)).astype(o_ref.dtype)

    return pl.pallas_call(
        body, name=name,
        grid_spec=pltpu.PrefetchScalarGridSpec(
            num_scalar_prefetch=1, grid=(4,),
            in_specs=[pl.BlockSpec((None, None, r, c), lambda j, side: (j, side[0], 0, 0)),
                      pl.BlockSpec((None, r, c), lambda j, side: (j, 0, 0))],
            out_specs=pl.BlockSpec((None, r, c), lambda j, side: (j, 0, 0))),
        out_shape=_sds(theirs.shape, theirs.dtype),
        compiler_params=_params(("parallel",)),
    )(lax.axis_index("c").astype(jnp.int32).reshape(1), mine, theirs)


def _exchange_call(name, exchange):
    arrs, out_shape, scratch, phases = exchange
    n = len(arrs)

    def body(*refs):
        for step in phases(refs[:n], refs[n:2 * n], *refs[2 * n:]):
            step()

    return pl.pallas_call(body, name=name, in_specs=[ANY] * n, out_specs=[ANY] * n,
                          out_shape=out_shape, scratch_shapes=scratch)(*arrs)


def _carry_exchange(exchange, refs, n_in, n_out, first, middle, last):
    arrs, _, _, phases = exchange
    n = len(arrs)
    if n == 0:
        return lambda: None
    ins = refs[n_in:n_in + n]
    outs = refs[n_in + n + n_out:n_in + 2 * n + n_out]
    sems = n_in + 2 * n + n_out
    steps = phases(ins, outs, *refs[sems:sems + 3])
    pl.when(first)(steps[0])
    if len(steps) == 3:
        pl.when(middle)(steps[1])
    return lambda: pl.when(last)(steps[-1])


def _adamw_math(w, g, m, v):
    m2 = ADAM_B1 * m + (1.0 - ADAM_B1) * g
    v2 = ADAM_B2 * v + (1.0 - ADAM_B2) * jnp.square(g)
    m_hat = m2 / (1.0 - ADAM_B1 ** ADAM_STEP)
    v_hat = v2 / (1.0 - ADAM_B2 ** ADAM_STEP)
    delta = -ADAM_LR * (m_hat / (jnp.sqrt(v_hat) + ADAM_EPS) + ADAM_WD * w)
    return delta, m2, v2


def _sum_adamw_tile(parts, w, m, v):
    g = parts[0].astype(F32)
    for d in range(1, parts.shape[0]):
        g = g + parts[d].astype(F32)
    return (g, *_adamw_math(w, g, m, v))


def _sum_adamw(name, parts, w, m, v, tr=256):
    p, r, c = parts.shape
    tr = _pick(r, tr, 16)

    def body(p_ref, w_ref, m_ref, v_ref, g_ref, d_ref, m2_ref, v2_ref):
        g_ref[...], d_ref[...], m2_ref[...], v2_ref[...] = _sum_adamw_tile(p_ref[...], w_ref[...], m_ref[...], v_ref[...])

    tile = pl.BlockSpec((tr, c), lambda i: (i, 0))
    return pl.pallas_call(
        body, name=name, grid=(r // tr,),
        in_specs=[pl.BlockSpec((p, tr, c), lambda i: (0, i, 0)), tile, tile, tile],
        out_specs=[tile] * 4, out_shape=[_sds((r, c))] * 4,
        compiler_params=_params(("parallel",)),
    )(parts, w, m, v)


def _sum_parts(name, parts):
    p, r, c = parts.shape

    def body(p_ref, o_ref):
        g = p_ref[0]
        for d in range(1, p):
            g = g + p_ref[d]
        o_ref[...] = g

    return pl.pallas_call(
        body, name=name, out_shape=_sds((r, c)),
        in_specs=[pl.BlockSpec(memory_space=pltpu.VMEM)], out_specs=pl.BlockSpec(memory_space=pltpu.VMEM),
    )(parts)


WEIGHTS = ["norm_mix_pre", "w_in", "conv_dw_w", "conv_dw_b", "conv_ln_g", "conv_ln_b", "w_conv_branch",
           "b_conv_branch", "w_att_branch", "w_out", "norm_mix_post", "norm_ffn_pre", "w_ffn_up", "w_ffn_down",
           "norm_ffn_post"]
COL_SHARDED = ["w_conv_branch", "w_att_branch"]
ROW_SHARDED = ["w_out", "w_ffn_down"]
TRANSPOSED = ["w_in", "w_ffn_up"]
VECTORS = ["norm_mix_pre", "conv_dw_b", "conv_ln_g", "conv_ln_b", "b_conv_branch", "norm_mix_post",
           "norm_ffn_pre", "norm_ffn_post"]


def _cols_to_full(g):
    return g.transpose(1, 0, 2).reshape(g.shape[1], N_DEV * g.shape[2])


def _full_to_cols(f):
    return f.reshape(f.shape[0], N_DEV, f.shape[1] // N_DEV).transpose(1, 0, 2)


PACK_ROWS = 7


def _pack_vectors(vecs, extra=None):
    parts = [vecs[nm].reshape(-1) for nm in VECTORS]
    parts.append(jnp.zeros((1,), F32) if extra is None else extra.reshape(1))
    used = sum(p.size for p in parts)
    parts.append(jnp.zeros((PACK_ROWS * D_MODEL - used,), F32))
    return jnp.concatenate(parts).reshape(PACK_ROWS, D_MODEL)


def _unpack_vectors(packed, sizes):
    flat, out, at = packed.reshape(-1), {}, 0
    for nm in VECTORS:
        out[nm] = flat[at:at + sizes[nm]]
        at += sizes[nm]
    return out, flat[at]


def kernel(x, norm_mix_pre, w_in, conv_dw_w, conv_dw_b, conv_ln_g, conv_ln_b, w_conv_branch, b_conv_branch, w_att_branch, w_out, norm_mix_post, norm_ffn_pre, w_ffn_up, w_ffn_down, norm_ffn_post, loss_target, m_norm_mix_pre, m_w_in, m_conv_dw_w, m_conv_dw_b, m_conv_ln_g, m_conv_ln_b, m_w_conv_branch, m_b_conv_branch, m_w_att_branch, m_w_out, m_norm_mix_post, m_norm_ffn_pre, m_w_ffn_up, m_w_ffn_down, m_norm_ffn_post, v_norm_mix_pre, v_w_in, v_conv_dw_w, v_conv_dw_b, v_conv_ln_g, v_conv_ln_b, v_w_conv_branch, v_b_conv_branch, v_w_att_branch, v_w_out, v_norm_mix_post, v_norm_ffn_pre, v_w_ffn_up, v_w_ffn_down, v_norm_ffn_post):
    ws = dict(zip(WEIGHTS, [norm_mix_pre, w_in, conv_dw_w, conv_dw_b, conv_ln_g, conv_ln_b, w_conv_branch,
                            b_conv_branch, w_att_branch, w_out, norm_mix_post, norm_ffn_pre, w_ffn_up, w_ffn_down,
                            norm_ffn_post]))
    ms = dict(zip(WEIGHTS, [m_norm_mix_pre, m_w_in, m_conv_dw_w, m_conv_dw_b, m_conv_ln_g, m_conv_ln_b,
                            m_w_conv_branch, m_b_conv_branch, m_w_att_branch, m_w_out, m_norm_mix_post,
                            m_norm_ffn_pre, m_w_ffn_up, m_w_ffn_down, m_norm_ffn_post]))
    vs = dict(zip(WEIGHTS, [v_norm_mix_pre, v_w_in, v_conv_dw_w, v_conv_dw_b, v_conv_ln_g, v_conv_ln_b,
                            v_w_conv_branch, v_b_conv_branch, v_w_att_branch, v_w_out, v_norm_mix_post,
                            v_norm_ffn_pre, v_w_ffn_up, v_w_ffn_down, v_norm_ffn_post]))

    dw_block = jnp.pad(conv_dw_w, ((0, 1), (0, 0)))
    g_in, g_dw = _exchange_call("gather_first", _gather_exchange([w_in.T.astype(BF16), dw_block]))
    full = {"w_in": _full_weight("w_in", g_in), "conv_dw_w": _cols_to_full(g_dw)}
    for nm in VECTORS:
        full[nm] = ws[nm].reshape(1, -1)

    def as_kept(nm, a):
        return a.T if nm in TRANSPOSED else a

    ride_along = ["w_ffn_up", "w_out"]
    loss_local, grad_x, received, updated, grads = _local_step(
        x[0], loss_target[0], full, {nm: as_kept(nm, ws[nm]).astype(BF16) for nm in LATE},
        {nm: tuple(as_kept(nm, a[nm]) for a in (ws, ms, vs)) for nm in ride_along})

    small = _exchange_call("gather_small_grads", _gather_exchange(
        [_pack_vectors(grads, extra=loss_local), grads["conv_dw_w"]]))
    out_g, out_d, out_m, out_v = {}, {}, {}, {}
    for nm in LATE + ["w_in"]:
        res = updated[nm] if nm in updated else _sum_adamw(
            "adamw_" + nm, received[nm], *[as_kept(nm, a[nm]) for a in (ws, ms, vs)])
        out_g[nm], out_d[nm], out_m[nm], out_v[nm] = [as_kept(nm, r) for r in res]
    sizes = {nm: ws[nm].size for nm in VECTORS}
    vec = _sum_adamw("adamw_vectors", small[0], _pack_vectors(ws), _pack_vectors(ms), _pack_vectors(vs))
    for res, dst in zip(vec, (out_g, out_d, out_m, out_v)):
        dst.update(_unpack_vectors(res, sizes)[0])
    loss = _unpack_vectors(vec[0], sizes)[1]
    dw_full = _sum_parts("sum_dw_grads", small[1])
    me = _slot(*_place())
    dw_mine = lax.dynamic_slice(dw_full, (0, me * (CONV_DIM // N_DEV)), (CONV_WIDTH, CONV_DIM // N_DEV))
    nm = "conv_dw_w"
    out_g[nm], out_d[nm], out_m[nm], out_v[nm] = _sum_adamw("adamw_dw", dw_mine[None], ws[nm], ms[nm], vs[nm])

    outs = [loss, grad_x[None]]
    for group in (out_g, out_d, out_m, out_v):
        outs += [group[nm] for nm in WEIGHTS]
    return tuple(outs)
```

```python
import math

import jax
import jax.numpy as jnp
from jax import lax
from jax.experimental import pallas as pl
from jax.experimental.pallas import tpu as pltpu

F32 = jnp.float32
BF16 = jnp.bfloat16

N_DEV = 8
D_MODEL = 1024
CONV_DIM = 512
CONV_WIDTH = 31
N_HEADS = 8
HEAD_DIM = 64
ATT_DIM = N_HEADS * HEAD_DIM
D_FF = 2816
EPS = 1e-6
IN_SPLITS = (0, 1024, 1536, 2048, 2560, 3584, 4608)

ADAM_LR = 0.001
ADAM_B1 = 0.9
ADAM_B2 = 0.999
ADAM_EPS = 1e-08
ADAM_WD = 0.01
ADAM_STEP = 10

LANES = 128
SUBLANES = 8
HALO = 32
ATT_TILE = 256
ATT_PART = 192
DEAD_SUM = -120.0
VMEM_LIMIT = 56 * 1024 * 1024
MESH = pl.DeviceIdType.MESH
ANY = pl.BlockSpec(memory_space=pl.ANY)


def _pick(dim, target, align=LANES):
    t = min(dim, target)
    t -= t % align
    while t >= align:
        if dim % t == 0:
            return t
        t -= align
    return dim


def _params(semantics, collective_id=None):
    return pltpu.CompilerParams(dimension_semantics=semantics, vmem_limit_bytes=VMEM_LIMIT,
                                collective_id=collective_id)


def _matmul(a, b, *, name, ta=False, tb=False, out_dtype=F32):
    m, k = (a.shape[1], a.shape[0]) if ta else a.shape
    n, k2 = b.shape if tb else (b.shape[1], b.shape[0])
    assert k == k2, (a.shape, b.shape, ta, tb)
    tm, tn, tk = _pick(m, 1408 if ta else 512), _pick(n, 1536), _pick(k, 1536)
    nk = k // tk
    dims = (((0 if ta else 1,), (1 if tb else 0,)), ((), ()))

    def body(a_ref, b_ref, o_ref, *acc):
        part = lax.dot_general(a_ref[...], b_ref[...], dims, preferred_element_type=F32)
        if nk == 1:
            o_ref[...] = part.astype(o_ref.dtype)
            return
        acc_ref, = acc
        kk = pl.program_id(2)

        @pl.when(kk == 0)
        def _():
            acc_ref[...] = part

        @pl.when((kk > 0) & (kk < nk - 1))
        def _():
            acc_ref[...] += part

        @pl.when(kk == nk - 1)
        def _():
            o_ref[...] = (acc_ref[...] + part).astype(o_ref.dtype)

    a_spec = pl.BlockSpec((tk, tm), lambda j, i, kk: (kk, i)) if ta else pl.BlockSpec((tm, tk), lambda j, i, kk: (i, kk))
    b_spec = (pl.BlockSpec((tn, tk), lambda j, i, kk: (j, kk)) if tb
              else pl.BlockSpec((tk, tn), lambda j, i, kk: (kk, j)))
    return pl.pallas_call(
        body, name=name, grid=(n // tn, m // tm, nk),
        in_specs=[a_spec, b_spec],
        out_specs=pl.BlockSpec((tm, tn), lambda j, i, kk: (i, j)),
        out_shape=jax.ShapeDtypeStruct((m, n), out_dtype),
        scratch_shapes=[pltpu.VMEM((tm, tn), F32)] if nk > 1 else [],
        compiler_params=_params(("parallel", "parallel", "arbitrary")),
    )(a, b)


def _pieces_tn_matmul(pieces, b, *, name, tj=512):
    s, n = b.shape
    counts = [p.shape[1] // tj for p in pieces]
    starts = [sum(counts[:i]) for i in range(len(pieces))]
    assert all(p.shape == (s, c * tj) for p, c in zip(pieces, counts))

    def body(*refs):
        b_ref, o_ref = refs[len(pieces):]
        j = pl.program_id(0)
        for p_ref, first, count in zip(refs, starts, counts):
            @pl.when((j >= first) & (j < first + count))
            def _():
                o_ref[...] = lax.dot_general(p_ref[...].astype(BF16), b_ref[...], TN,
                                             preferred_element_type=F32).astype(o_ref.dtype)

    def piece_spec(first, count):
        return pl.BlockSpec((s, tj), lambda j: (0, jnp.clip(j - first, 0, count - 1)))

    return pl.pallas_call(
        body, name=name, grid=(sum(counts),),
        in_specs=[piece_spec(f, c) for f, c in zip(starts, counts)]
        + [pl.BlockSpec((s, n), lambda j: (0, 0), pipeline_mode=pl.Buffered(1))],
        out_specs=pl.BlockSpec((tj, n), lambda j: (j, 0)),
        out_shape=jax.ShapeDtypeStruct((sum(counts) * tj, n), BF16),
        compiler_params=_params(("arbitrary",)),
    )(*pieces, b)


NO_EXCHANGE = ([], [], [], None, None)


def _sweep_marks(nt):
    i = pl.program_id(0)
    return i == 0, i == (3 * nt) // 4, i == nt - 1


def _rowwise(name, fn, rows, bcasts, row_outs, red_outs=(), tm=256, exchange=NO_EXCHANGE):
    s = rows[0].shape[0]
    tm = _pick(s, tm, 16)
    nt = s // tm
    resident = pl.Buffered(1)
    nr, nb, no, nd = len(rows), len(bcasts), len(row_outs), len(red_outs)
    x_arrs, x_shape, x_scratch, _, x_id = exchange
    nx = len(x_arrs)
    first_out = nr + nb + nx

    def body(*refs):
        finish_exchange = _carry_exchange(exchange, refs, nr + nb, no + nd, *_sweep_marks(nt))
        ins = [r[...] for r in refs[:nr + nb]]
        outs, reds = fn(*ins)
        for ref, val in zip(refs[first_out:first_out + no], outs):
            ref[...] = val.astype(ref.dtype)
        i = pl.program_id(0)
        for ref, val in zip(refs[first_out + no:first_out + no + nd], reds):
            @pl.when(i == 0)
            def _():
                ref[...] = val

            @pl.when(i > 0)
            def _():
                ref[...] += val
        finish_exchange()

    def row_spec(a):
        assert a.shape[-2] % nt == 0, (name, a.shape, nt)
        if len(a.shape) == 3:
            return pl.BlockSpec((a.shape[0], a.shape[1] // nt, a.shape[2]), lambda i: (0, i, 0))
        return pl.BlockSpec((a.shape[0] // nt, a.shape[1]), lambda i: (i, 0))

    in_specs = [row_spec(r) for r in rows]
    in_specs += [pl.BlockSpec(b.shape, lambda i: (0, 0), pipeline_mode=resident) for b in bcasts]
    out_specs = [row_spec(o) for o in row_outs]
    out_specs += [pl.BlockSpec(d.shape, lambda i: (0, 0)) for d in red_outs]
    return pl.pallas_call(
        body, name=name, grid=(nt,), in_specs=in_specs + [ANY] * nx, out_specs=out_specs + [ANY] * nx,
        out_shape=list(row_outs) + list(red_outs) + x_shape, scratch_shapes=x_scratch,
        compiler_params=_params(("arbitrary",), x_id),
    )(*rows, *bcasts, *x_arrs)


def _sds(shape, dtype=F32):
    return jax.ShapeDtypeStruct(shape, dtype)


def _rms(x, g):
    y = x * lax.rsqrt(jnp.mean(x * x, axis=-1, keepdims=True) + EPS)
    return y * g


def _silu(x):
    return x * jax.nn.sigmoid(x)


def _swiglu(g, u):
    return _silu(g) * u


def _ln_silu(u, g, b):
    mu = jnp.mean(u, axis=-1, keepdims=True)
    var = jnp.mean(jnp.square(u - mu), axis=-1, keepdims=True)
    return _silu((u - mu) * lax.rsqrt(var + EPS) * g + b)


def _merge(conv_pre, att_out, g_conv, g_att, b_cb):
    return jax.nn.sigmoid(g_conv) * (conv_pre + b_cb) + jax.nn.sigmoid(g_att) * att_out


def _glu(t):
    return t[:, :CONV_DIM] * jax.nn.sigmoid(t[:, CONV_DIM:])


def _shifted_reader(buf, shifted, tm):
    for b in range(1, SUBLANES):
        shifted[b - 1, :, :] = buf[pl.ds(b, tm + HALO - SUBLANES), :]

    def read(o):
        a, b = divmod(o, SUBLANES)
        return buf[pl.ds(SUBLANES * a, tm), :] if b == 0 else shifted[b - 1, pl.ds(SUBLANES * a, tm), :]

    return read


def _conv_fwd(conv_in, w_pad, b, ln_g, ln_b, exchange, tm=256):
    s = conv_in.shape[0]
    tm = _pick(s, tm, HALO)
    ratio = tm // HALO
    x_arrs, x_shape, x_scratch, _, x_id = exchange
    nx = len(x_arrs)

    def body(*refs):
        main_ref, halo_ref, w_ref, b_ref, g_ref, be_ref = refs[:6]
        u3_ref, u1_ref = refs[6 + nx:8 + nx]
        buf, shifted = refs[-2:]
        finish_exchange = _carry_exchange(exchange, refs, 6, 2, *_sweep_marks(s // tm))
        i = pl.program_id(0)
        buf[0:HALO, :] = _glu(halo_ref[...]) * (i > 0).astype(F32)
        buf[HALO:HALO + tm, :] = _glu(main_ref[...])
        read = _shifted_reader(buf, shifted, tm)
        acc = jnp.zeros((tm, CONV_DIM), F32) + b_ref[...]
        for j in range(CONV_WIDTH):
            acc = acc + w_ref[j:j + 1, :] * read(HALO - (CONV_WIDTH - 1) + j)
        u1_ref[...] = acc
        u3_ref[...] = _ln_silu(acc, g_ref[...], be_ref[...]).astype(u3_ref.dtype)
        finish_exchange()

    res = pl.pallas_call(
        body, name="conv_fwd", grid=(s // tm,),
        in_specs=[pl.BlockSpec((tm, 2 * CONV_DIM), lambda i: (i, 0)),
                  pl.BlockSpec((HALO, 2 * CONV_DIM), lambda i: (jnp.maximum(i * ratio - 1, 0), 0)),
                  pl.BlockSpec(w_pad.shape, lambda i: (0, 0)),
                  pl.BlockSpec(b.shape, lambda i: (0, 0)),
                  pl.BlockSpec(ln_g.shape, lambda i: (0, 0)),
                  pl.BlockSpec(ln_b.shape, lambda i: (0, 0))] + [ANY] * nx,
        out_specs=[pl.BlockSpec((tm, CONV_DIM), lambda i: (i, 0)),
                   pl.BlockSpec((tm, CONV_DIM), lambda i: (i, 0))] + [ANY] * nx,
        out_shape=[_sds((s, CONV_DIM), BF16), _sds((s, CONV_DIM), F32)] + x_shape,
        scratch_shapes=x_scratch + [pltpu.VMEM((tm + HALO, CONV_DIM), F32),
                                    pltpu.VMEM((SUBLANES - 1, tm + HALO - SUBLANES, CONV_DIM), F32)],
        compiler_params=_params(("arbitrary",), x_id),
    )(conv_in, conv_in, w_pad, b, ln_g, ln_b, *x_arrs)
    return res[0], res[1], res[2:]


def _conv_bwd(conv_in, u1, du3, ln_g, ln_b, w_pad, exchange, tm=256):
    s = conv_in.shape[0]
    tm = _pick(s, tm, HALO)
    ratio = tm // HALO
    nt = s // tm
    last_halo = s // HALO - 1
    x_arrs, x_shape, x_scratch, _, x_id = exchange
    nx = len(x_arrs)

    def body(*refs):
        main_ref, halo_ref, u1_ref, u1n_ref, du3_ref, du3n_ref, g_ref, be_ref, w_ref = refs[:9]
        dci_ref, dw_ref, db_ref, dg_ref, dbe_ref = refs[9 + nx:14 + nx]
        ubuf, dbuf, ushift, dshift = refs[-4:]
        finish_exchange = _carry_exchange(exchange, refs, 9, 5, *_sweep_marks(nt))
        i = pl.program_id(0)
        main = main_ref[...]
        a = main[:, :CONV_DIM]
        sb = jax.nn.sigmoid(main[:, CONV_DIM:])
        ubuf[0:HALO, :] = _glu(halo_ref[...]) * (i > 0).astype(F32)
        ubuf[HALO:HALO + tm, :] = a * sb

        def ln_bwd(u1t, du3t):
            _, vjp = jax.vjp(_ln_silu, u1t, g_ref[...], be_ref[...])
            return vjp(du3t)

        du, dg, dbe = ln_bwd(u1_ref[...], du3_ref[...])
        dbuf[0:tm, :] = du
        dbuf[tm:tm + HALO, :] = ln_bwd(u1n_ref[...], du3n_ref[...])[0] * (i < nt - 1).astype(F32)

        @pl.when(i == 0)
        def _():
            dw_ref[...] = jnp.zeros_like(dw_ref)
            db_ref[...] = jnp.zeros_like(db_ref)
            dg_ref[...] = jnp.zeros_like(dg_ref)
            dbe_ref[...] = jnp.zeros_like(dbe_ref)

        dg_ref[...] += dg
        dbe_ref[...] += dbe

        read_u = _shifted_reader(ubuf, ushift, tm)
        read_d = _shifted_reader(dbuf, dshift, tm)
        du0 = jnp.zeros((tm, CONV_DIM), F32)
        for j in range(CONV_WIDTH):
            du0 = du0 + w_ref[j:j + 1, :] * read_d(CONV_WIDTH - 1 - j)
            dw_ref[j:j + 1, :] += jnp.sum(du * read_u(HALO - (CONV_WIDTH - 1) + j), axis=0, keepdims=True)
        db_ref[...] += jnp.sum(du, axis=0, keepdims=True)
        dci_ref[:, :CONV_DIM] = (du0 * sb).astype(dci_ref.dtype)
        dci_ref[:, CONV_DIM:] = (du0 * a * sb * (1.0 - sb)).astype(dci_ref.dtype)
        finish_exchange()

    res = pl.pallas_call(
        body, name="conv_bwd", grid=(nt,),
        in_specs=[pl.BlockSpec((tm, 2 * CONV_DIM), lambda i: (i, 0)),
                  pl.BlockSpec((HALO, 2 * CONV_DIM), lambda i: (jnp.maximum(i * ratio - 1, 0), 0))]
        + [pl.BlockSpec((tm, CONV_DIM), lambda i: (i, 0)),
           pl.BlockSpec((HALO, CONV_DIM), lambda i: (jnp.minimum((i + 1) * ratio, last_halo), 0))] * 2
        + [pl.BlockSpec((1, CONV_DIM), lambda i: (0, 0))] * 2 + [pl.BlockSpec(w_pad.shape, lambda i: (0, 0))]
        + [ANY] * nx,
        out_specs=[pl.BlockSpec((tm, 2 * CONV_DIM), lambda i: (i, 0)),
                   pl.BlockSpec(w_pad.shape, lambda i: (0, 0))]
        + [pl.BlockSpec((1, CONV_DIM), lambda i: (0, 0))] * 3 + [ANY] * nx,
        out_shape=[_sds((s, 2 * CONV_DIM), BF16), _sds(w_pad.shape)] + [_sds((1, CONV_DIM))] * 3 + x_shape,
        scratch_shapes=x_scratch + [pltpu.VMEM((tm + HALO, CONV_DIM), F32)] * 2
        + [pltpu.VMEM((SUBLANES - 1, tm + HALO - SUBLANES, CONV_DIM), F32)] * 2,
        compiler_params=_params(("arbitrary",), x_id),
    )(conv_in, conv_in, u1, u1, du3, du3, ln_g, ln_b, w_pad, *x_arrs)
    return res[:5], res[5:]


def _logsig_neg(z):
    return jnp.minimum(-z, 0.0) - jnp.log(1.0 + jnp.exp(-jnp.abs(z)))


def _split_dot(val, tri):
    hi = val.astype(BF16)
    lo = (val - hi.astype(F32)).astype(BF16)
    return jnp.dot(hi, tri, preferred_element_type=F32) + jnp.dot(lo, tri, preferred_element_type=F32)


def _attn_masks(t, later):
    row = lax.broadcasted_iota(jnp.int32, (t, t), 0)
    col = lax.broadcasted_iota(jnp.int32, (t, t), 1)
    tri = jnp.where(row > col if later else row <= col, 1.0, 0.0).astype(BF16)
    return col < row, tri


def _grid_marks(h, nq):
    hh, i = pl.program_id(0), pl.program_id(1)
    return (hh == 0) & (i == 0), (hh == (3 * h) // 4) & (i == 0), (hh == h - 1) & (i == nq - 1)


def _head_masks(shape):
    lane = lax.broadcasted_iota(jnp.int32, shape, len(shape) - 1)
    return lane < HEAD_DIM, lane >= HEAD_DIM


def _per_head(blk):
    m0, m1 = _head_masks(blk.shape)
    zero = jnp.zeros_like(blk)
    return jnp.where(m0, blk, zero), jnp.where(m1, blk, zero)


NT = (((1,), (1,)), ((), ()))
TN = (((0,), (0,)), ((), ()))


def _with_top(whole, top):
    rows = top.shape[0]
    return top if rows == whole.shape[0] else jnp.concatenate([top, whole[rows:]], axis=0)


def _attn_fwd(q, k, v, exchange):
    s = q.shape[0]
    hp = q.shape[1] // LANES
    t = ATT_TILE
    scale = 1.0 / math.sqrt(HEAD_DIM)
    x_arrs, x_shape, x_scratch, _, x_id = exchange
    nx = len(x_arrs)

    def body(*refs):
        q_ref, k_ref, v_ref = refs[:3]
        o_ref, lt_ref, nb_ref = refs[3 + nx:6 + nx]
        finish_exchange = _carry_exchange(exchange, refs, 3, 3, *_grid_marks(hp, s // t))
        i = pl.program_id(1)
        qs = _per_head((q_ref[...].astype(F32) * scale).astype(BF16))
        causal, tri = _attn_masks(t, later=True)

        def step(kb, carry, masked, rows):
            cs, acc = carry
            off = pl.multiple_of(kb * t, t)
            kblk = k_ref[pl.ds(off, t), :]
            vs = _per_head(v_ref[pl.ds(off, t), :])
            acc_top = acc[:rows]
            new_cs = []
            for hd in range(2):
                z = lax.dot_general(qs[hd][:rows], kblk, NT, preferred_element_type=F32)
                l = _logsig_neg(z)
                if masked:
                    l = jnp.where(causal, l, 0.0)
                e = z + l + _split_dot(l, tri) + cs[hd][:rows]
                if masked:
                    e = jnp.where(causal, e, -1e30)
                acc_top = acc_top + jnp.dot(jnp.exp(e).astype(BF16), vs[hd], preferred_element_type=F32)
                new_cs.append(_with_top(cs[hd], cs[hd][:rows] + jnp.sum(l, axis=1, keepdims=True)))
            return tuple(new_cs), _with_top(acc, acc_top)

        zero = jnp.zeros((t, 1), F32)
        carry = step(i, ((zero, zero), jnp.zeros((t, LANES), F32)), True, t)

        def live(cs, lo, hi):
            return jnp.maximum(jnp.max(cs[0][lo:hi]), jnp.max(cs[1][lo:hi])) > DEAD_SUM

        carry = lax.fori_loop(0, i, lambda n, cr: step(i - 1 - n, cr, False, t), carry)
        n_blocks = n_full = i
        m0, _ = _head_masks((t, LANES))
        lt_ref[...] = jnp.where(m0, carry[0][0], carry[0][1])
        o_ref[...] = carry[1].astype(o_ref.dtype)
        nb_ref[0, pl.program_id(0), i] = n_blocks.astype(F32)
        nb_ref[1, pl.program_id(0), i] = n_full.astype(F32)
        finish_exchange()

    res = pl.pallas_call(
        body, name="attn_fwd", grid=(hp, s // t),
        in_specs=[pl.BlockSpec((t, LANES), lambda p, i: (i, p)),
                  pl.BlockSpec((s, LANES), lambda p, i: (0, p)),
                  pl.BlockSpec((s, LANES), lambda p, i: (0, p))] + [ANY] * nx,
        out_specs=[pl.BlockSpec((t, LANES), lambda p, i: (i, p)),
                   pl.BlockSpec((None, t, LANES), lambda p, i: (p, i, 0)),
                   pl.BlockSpec(memory_space=pltpu.SMEM)] + [ANY] * nx,
        out_shape=[_sds(q.shape, BF16), _sds((hp, s, LANES), F32), _sds((2, hp, s // t), F32)] + x_shape,
        scratch_shapes=x_scratch,
        compiler_params=_params(("arbitrary", "arbitrary"), x_id),
    )(q, k, v, *x_arrs)
    return res[0], res[1], res[2], res[3:]


def _attn_bwd(q, k, v, do, ltot, n_blocks, exchange):
    s = q.shape[0]
    hp = q.shape[1] // LANES
    t = ATT_TILE
    scale = 1.0 / math.sqrt(HEAD_DIM)
    x_arrs, x_shape, x_scratch, _, x_id = exchange
    nx = len(x_arrs)

    def body(*refs):
        q_ref, k_ref, v_ref, do_ref, lt_ref, nb_ref = refs[:6]
        dq_ref, dk_ref, dv_ref = refs[6 + nx:9 + nx]
        finish_exchange = _carry_exchange(exchange, refs, 6, 3, *_grid_marks(hp, s // t))
        i = pl.program_id(1)
        n_blocks = n_full = i

        @pl.when(i == 0)
        def _():
            dk_ref[...] = jnp.zeros_like(dk_ref)
            dv_ref[...] = jnp.zeros_like(dv_ref)

        qb = q_ref[...]
        qm = _per_head(qb)
        qs = _per_head((qb.astype(F32) * scale).astype(BF16))
        dos = _per_head(do_ref[...])
        lts = (lt_ref[:, 0:1], lt_ref[:, HEAD_DIM:HEAD_DIM + 1])
        causal, tri = _attn_masks(t, later=False)

        def step(kb, carry, masked, rows):
            cls, cgs, dq = carry
            off = pl.multiple_of(kb * t, t)
            kblk = k_ref[pl.ds(off, t), :]
            vblk = v_ref[pl.ds(off, t), :]
            ks = _per_head(kblk)
            dq_top = dq[:rows]
            dk = jnp.zeros((t, LANES), F32)
            dv = jnp.zeros((t, LANES), F32)
            new_cls, new_cgs = [], []
            for hd in range(2):
                z = lax.dot_general(qs[hd][:rows], kblk, NT, preferred_element_type=F32)
                l = _logsig_neg(z)
                if masked:
                    l = jnp.where(causal, l, 0.0)
                e = z + l + ((lts[hd][:rows] - cls[hd][:rows]) - _split_dot(l, tri))
                if masked:
                    e = jnp.where(causal, e, -1e30)
                a = jnp.exp(e)
                g = lax.dot_general(dos[hd][:rows], vblk, NT, preferred_element_type=F32) * a
                p = cgs[hd][:rows] + jnp.dot(g.astype(BF16), tri, preferred_element_type=F32) - g
                el = jnp.exp(l)
                dz = g * el - p * (1.0 - el)
                if masked:
                    dz = jnp.where(causal, dz, 0.0)
                dzb = (dz * scale).astype(BF16)
                dq_top = dq_top + jnp.dot(dzb, ks[hd], preferred_element_type=F32)
                dk = dk + lax.dot_general(dzb, qm[hd][:rows], TN, preferred_element_type=F32)
                dv = dv + lax.dot_general(a.astype(BF16), dos[hd][:rows], TN, preferred_element_type=F32)
                new_cls.append(_with_top(cls[hd], cls[hd][:rows] + jnp.sum(l, axis=1, keepdims=True)))
                new_cgs.append(_with_top(cgs[hd], cgs[hd][:rows] + jnp.sum(g, axis=1, keepdims=True)))
            dk_ref[pl.ds(off, t), :] += dk
            dv_ref[pl.ds(off, t), :] += dv
            return tuple(new_cls), tuple(new_cgs), _with_top(dq, dq_top)

        zero = jnp.zeros((t, 1), F32)
        init = ((zero, zero), (zero, zero), jnp.zeros((t, LANES), F32))
        carry = lax.fori_loop(i - n_blocks, i - n_full, lambda kb, cr: step(kb, cr, False, ATT_PART), init)
        carry = lax.fori_loop(i - n_full, i, lambda kb, cr: step(kb, cr, False, t), carry)
        carry = step(i, carry, True, t)
        dq_ref[...] = carry[2]
        finish_exchange()

    blk = pl.BlockSpec((t, LANES), lambda p, i: (i, p))
    whole = pl.BlockSpec((s, LANES), lambda p, i: (0, p))
    res = pl.pallas_call(
        body, name="attn_bwd", grid=(hp, s // t),
        in_specs=[blk, whole, whole, blk, pl.BlockSpec((None, t, LANES), lambda p, i: (p, i, 0)),
                  pl.BlockSpec(memory_space=pltpu.SMEM)] + [ANY] * nx,
        out_specs=[blk, whole, whole] + [ANY] * nx,
        out_shape=[_sds(q.shape)] * 3 + x_shape,
        scratch_shapes=x_scratch,
        compiler_params=_params(("arbitrary", "arbitrary"), x_id),
    )(q, k, v, do, ltot, n_blocks, *x_arrs)
    return res[0], res[1], res[2], res[3:]


LATE = ["w_conv_branch", "w_att_branch", "w_out", "w_ffn_up", "w_ffn_down"]


def _full_weight(name, gathered):
    return _cols_to_full(gathered) if name in COL_SHARDED else gathered.reshape(-1, gathered.shape[2])


def _grad_slabs(name, grad):
    return _full_to_cols(grad) if name in COL_SHARDED else grad.reshape(N_DEV, -1, grad.shape[1])


def _side_slabs(name, grad):
    slabs = _grad_slabs(name, grad)
    return slabs.reshape((4, 2) + slabs.shape[1:])


def _local_step(x, target, w, late_blocks, opt):
    s = x.shape[0]
    w = dict(w)
    g1, g2, g3, g4 = w["norm_mix_pre"], w["norm_mix_post"], w["norm_ffn_pre"], w["norm_ffn_post"]

    w_in = w["w_in"]

    def proj_fn(xt, g1_, w_in_t):
        h = _rms(xt, g1_).astype(BF16)
        proj = lax.dot_general(h, w_in_t, NT, preferred_element_type=F32)
        return (h, *[proj[:, IN_SPLITS[n]:IN_SPLITS[n + 1]] for n in range(6)]), ()

    mix_weights = ["w_conv_branch", "w_att_branch", "w_out"]
    h1, conv_in, q, k, v, g_conv, g_att = _rowwise(
        "norm_proj", proj_fn, [x], [g1, w_in],
        [_sds((s, D_MODEL), BF16), _sds((s, 2 * CONV_DIM)), _sds((s, ATT_DIM), BF16), _sds((s, ATT_DIM), BF16),
         _sds((s, ATT_DIM), BF16), _sds((s, D_MODEL), BF16), _sds((s, D_MODEL), BF16)], tm=512)

    u3, u1, gathered = _conv_fwd(conv_in, w["conv_dw_w"], w["conv_dw_b"], w["conv_ln_g"], w["conv_ln_b"],
                                 _gather_exchange([late_blocks[nm] for nm in mix_weights]))
    for nm, g in zip(mix_weights, gathered):
        w[nm] = _full_weight(nm, g)
    att, ltot, n_blocks, (g_up,) = _attn_fwd(q, k, v, _gather_exchange([late_blocks["w_ffn_up"]]))
    w["w_ffn_up"] = _full_weight("w_ffn_up", g_up)

    def merge_fn(u3t, at, gc, ga, xt, w_cb, w_ab, b_cb, w_out, g2_, g3_):
        cp = jnp.dot(u3t, w_cb, preferred_element_type=F32)
        ao = jnp.dot(at, w_ab, preferred_element_type=F32)
        mg = _merge(cp, ao, gc.astype(F32), ga.astype(F32), b_cb).astype(BF16)
        mix_ = jnp.dot(mg, w_out, preferred_element_type=F32)
        x2_ = xt + _rms(mix_, g2_)
        return (mg, cp, ao, mix_, x2_, _rms(x2_, g3_)), ()

    merged, conv_pre, att_out, mix, x2, h2 = _rowwise(
        "branch_merge_mix", merge_fn, [u3, att, g_conv, g_att, x],
        [w["w_conv_branch"], w["w_att_branch"], w["b_conv_branch"], w["w_out"], g2, g3],
        [_sds((s, D_MODEL), BF16)] * 3 + [_sds((s, D_MODEL)), _sds((s, D_MODEL)), _sds((s, D_MODEL), BF16)], tm=512)

    def ffn_up_fn(ht, w_up_t):
        gu_ = lax.dot_general(ht, w_up_t, NT, preferred_element_type=F32)
        return (gu_, _swiglu(gu_[:, :D_FF], gu_[:, D_FF:])), ()

    gu, act, g_down = _rowwise("ffn_up", ffn_up_fn, [h2], [w["w_ffn_up"]],
                               [_sds((s, 2 * D_FF), BF16), _sds((s, D_FF), BF16)], tm=512,
                               exchange=_gather_exchange([late_blocks["w_ffn_down"]]))
    w["w_ffn_down"] = _full_weight("w_ffn_down", g_down)

    def final_fn(at, x2t, tgt, w_down, g4_):
        ff = jnp.dot(at, w_down, preferred_element_type=F32)
        n4, vjp = jax.vjp(_rms, ff, g4_)
        err = x2t + n4 - tgt
        dy = err * (1.0 / D_MODEL)
        dff, dg4 = vjp(dy)
        return (dy, dff), (jnp.sum(err * err, axis=0, keepdims=True), dg4)

    dy, dff, loss_cols, d_g4 = _rowwise("ffn_down_loss", final_fn, [act, x2, target], [w["w_ffn_down"], g4],
                                        [_sds((s, D_MODEL)), _sds((s, D_MODEL), BF16)],
                                        [_sds((1, D_MODEL)), _sds((1, D_MODEL))], tm=512)
    loss = 0.5 * jnp.sum(loss_cols) / D_MODEL

    d_w_down = _matmul(act, dff, ta=True, name="d_w_down", out_dtype=BF16)

    def act_bwd_fn(dfft, gut, w_down):
        d_act = lax.dot_general(dfft, w_down, NT, preferred_element_type=F32)
        gu_ = gut.astype(F32)
        _, vjp = jax.vjp(_swiglu, gu_[:, :D_FF], gu_[:, D_FF:])
        return (jnp.concatenate(vjp(d_act), axis=1),), ()

    down_slabs = _side_slabs("w_ffn_down", d_w_down)
    dgu, theirs = _rowwise("ffn_act_bwd", act_bwd_fn, [dff, gu], [w["w_ffn_down"]], [_sds((s, 2 * D_FF), BF16)],
                           exchange=_pair_exchange([down_slabs]))
    down_sums = _pair_sum("pair_sum_w_ffn_down", down_slabs, theirs)
    d_w_up = _matmul(dgu, h2, ta=True, name="d_w_up", out_dtype=BF16)
    received = {}
    up_slabs = _side_slabs("w_ffn_up", d_w_up)

    def mid_bwd_fn(dgut, xt, mt, dyt, w_up_t, g2_, g3_):
        dh = jnp.dot(dgut, w_up_t, preferred_element_type=F32)
        n2, vjp2 = jax.vjp(_rms, mt, g2_)
        x2_ = xt + n2
        _, vjp3 = jax.vjp(_rms, x2_, g3_)
        dx2_, dg3 = vjp3(dh)
        dx2_ = dx2_ + dyt
        dmix_, dg2 = vjp2(dx2_)
        return (dx2_, dmix_), (dg2, dg3)

    dx2, dmix, d_g2, d_g3, received["w_ffn_down"] = _rowwise(
        "ffn_up_mid_bwd", mid_bwd_fn, [dgu, x, mix, dy], [w["w_ffn_up"], g2, g3],
        [_sds((s, D_MODEL)), _sds((s, D_MODEL), BF16)], [_sds((1, D_MODEL)), _sds((1, D_MODEL))], tm=512,
        exchange=_chip_exchange([down_sums]))
    d_w_out = _matmul(merged, dmix, ta=True, name="d_w_out", out_dtype=BF16)

    def merge_bwd_fn(dmt, cp, ao, gc, ga, w_out, w_cb, w_ab, b_cb):
        dm = lax.dot_general(dmt, w_out, NT, preferred_element_type=F32)
        _, vjp = jax.vjp(_merge, cp.astype(F32), ao.astype(F32), gc.astype(F32), ga.astype(F32), b_cb)
        dcp, dao, dgc, dga, dbias = vjp(dm)
        dcp, dao = dcp.astype(BF16), dao.astype(BF16)
        du3_ = lax.dot_general(dcp, w_cb, NT, preferred_element_type=F32)
        datt_ = lax.dot_general(dao, w_ab, NT, preferred_element_type=F32)
        return (dcp, dao, dgc, dga, du3_, datt_), (dbias,)

    d_conv_out, d_att_out, d_g_conv, d_g_att, du3, d_att, d_b_cb, theirs = _rowwise(
        "merge_bwd", merge_bwd_fn, [dmix, conv_pre, att_out, g_conv, g_att],
        [w["w_out"], w["w_conv_branch"], w["w_att_branch"], w["b_conv_branch"]],
        [_sds((s, D_MODEL), BF16)] * 4 + [_sds((s, CONV_DIM)), _sds((s, ATT_DIM), BF16)], [_sds((1, D_MODEL))], tm=512,
        exchange=_pair_exchange([up_slabs]))

    d_w_cb = _matmul(u3, d_conv_out, ta=True, name="d_w_conv_branch", out_dtype=BF16)
    d_w_ab = _matmul(att, d_att_out, ta=True, name="d_w_att_branch", out_dtype=BF16)

    dq, dk, dv, (received["w_ffn_up"],) = _attn_bwd(
        q, k, v, d_att, ltot, n_blocks, _chip_exchange([_pair_sum("pair_sum_w_ffn_up", up_slabs, theirs)]))

    mix_grads = {"w_conv_branch": d_w_cb, "w_att_branch": d_w_ab, "w_out": d_w_out}
    (d_conv_in, d_dw_w, d_dw_b, d_ln_g, d_ln_b), landed = _conv_bwd(
        conv_in, u1, du3, w["conv_ln_g"], w["conv_ln_b"], w["conv_dw_w"],
        _scatter_exchange([_grad_slabs(nm, mix_grads[nm]) for nm in mix_weights]))
    received.update(zip(mix_weights, landed))

    d_proj = [d_conv_in, dq, dk, dv, d_g_conv, d_g_att]
    d_w_in = _pieces_tn_matmul(d_proj, h1, name="d_w_in")
    in_slabs = _side_slabs("w_in", d_w_in)
    (theirs,) = _exchange_call("pair_swap_w_in", _pair_exchange([in_slabs]))

    early = list(opt)

    def pre_bwd_fn(*args):
        groups, (xt, dx2t), jobs, (w_in_t, g_) = args[:6], args[6:8], args[8:-2], args[-2:]
        dh = sum(jnp.dot(grp.astype(BF16), w_in_t[IN_SPLITS[n]:IN_SPLITS[n + 1]], preferred_element_type=F32)
                 for n, grp in enumerate(groups))
        _, vjp = jax.vjp(_rms, xt, g_)
        dx_, dg_ = vjp(dh)
        updates = [_sum_adamw_tile(*jobs[4 * n:4 * n + 4]) for n in range(len(early))]
        return (dx_ + dx2t, *[u for four in updates for u in four]), (dg_,)

    res = _rowwise(
        "proj_norm_bwd", pre_bwd_fn,
        d_proj + [x, dx2] + [a for nm in early for a in (received[nm], *opt[nm])], [w_in, g1],
        [_sds((s, D_MODEL))] + [_sds(opt[nm][0].shape) for nm in early for _ in range(4)],
        [_sds((1, D_MODEL))], tm=512, exchange=_chip_exchange([_pair_sum("pair_sum_w_in", in_slabs, theirs)]))
    grad_x, d_g1, received["w_in"] = res[0], res[-2], res[-1]
    updated = {nm: res[1 + 4 * n:5 + 4 * n] for n, nm in enumerate(early)}

    grads = {
        "norm_mix_pre": d_g1, "conv_dw_w": d_dw_w, "conv_dw_b": d_dw_b,
        "conv_ln_g": d_ln_g, "conv_ln_b": d_ln_b, "b_conv_branch": d_b_cb,
        "norm_mix_post": d_g2, "norm_ffn_pre": d_g3, "norm_ffn_post": d_g4,
    }
    return loss, grad_x, received, updated, grads


def _place():
    x, y, c = lax.axis_index("x"), lax.axis_index("y"), lax.axis_index("c")
    return x, y, c


def _slot(px, py, pc):
    return 4 * px + 2 * py + pc


def _exchange_scratch(n):
    return [pltpu.SemaphoreType.DMA((7 * n,)), pltpu.SemaphoreType.DMA((7 * n,)), pltpu.SemaphoreType.DMA((n,))]


GATHER_ID, SCATTER_ID, PAIR_ID, CHIP_ID = 0, 1, 2, 3


def _handshake(peers):
    barrier = pltpu.get_barrier_semaphore()
    for peer in peers:
        pl.semaphore_signal(barrier, inc=1, device_id=peer, device_id_type=MESH)
    pl.semaphore_wait(barrier, len(peers))


def _gather_exchange(arrs):
    n = len(arrs)

    def phases(ins, outs, send_sems, recv_sems, local_sems):
        x, y, c = _place()
        me, sibling = (x, y, c), (x, y, 1 - c)
        chips = [(1 - x, y), (x, 1 - y), (1 - x, 1 - y)]

        def copy(a, kk, block, to, src=None):
            dst = outs[a].at[_slot(*block)]
            return pltpu.make_async_remote_copy(
                src_ref=dst if src is None else src, dst_ref=dst,
                send_sem=send_sems.at[a * 7 + kk], recv_sem=recv_sems.at[a * 7 + kk],
                device_id=to, device_id_type=MESH)

        mine = [pltpu.make_async_copy(ins[a], outs[a].at[_slot(*me)], local_sems.at[a]) for a in range(n)]
        first = []
        for a in range(n):
            first.append(copy(a, 0, me, sibling, src=ins[a]))
            first += [copy(a, 1 + j, me, (*chip, c), src=ins[a]) for j, chip in enumerate(chips)]
        passed = [copy(a, 4 + j, (*chip, c), sibling) for j, chip in enumerate(chips) for a in range(n)]

        def send():
            _handshake([sibling] + [(*chip, c) for chip in chips])
            for cp in mine + first:
                cp.start()

        def pass_on():
            for j, chip in enumerate(chips):
                for a in range(n):
                    copy(a, 1 + j, (*chip, c), me).wait_recv()
                    passed[j * n + a].start()

        def finish():
            for a in range(n):
                copy(a, 0, sibling, me).wait_recv()
                for j, chip in enumerate(chips):
                    copy(a, 4 + j, (*chip, 1 - c), me).wait_recv()
            for cp in first + passed:
                cp.wait_send()
            for cp in mine:
                cp.wait()

        return [send, pass_on, finish]

    return list(arrs), [_sds((N_DEV,) + a.shape, a.dtype) for a in arrs], _exchange_scratch(n), phases, GATHER_ID


def _scatter_exchange(arrs):
    n = len(arrs)
    flips = [(fx, fy, fc) for fx in (0, 1) for fy in (0, 1) for fc in (0, 1)][1:]

    def phases(ins, outs, send_sems, recv_sems, local_sems):
        x, y, c = _place()
        mine = _slot(x, y, c)
        local = [pltpu.make_async_copy(ins[a].at[mine], outs[a].at[mine], local_sems.at[a]) for a in range(n)]
        peers = [((1 - x) if fx else x, (1 - y) if fy else y, (1 - c) if fc else c) for fx, fy, fc in flips]

        def copy(a, kk, src_slot, dst_slot):
            return pltpu.make_async_remote_copy(
                src_ref=ins[a].at[src_slot], dst_ref=outs[a].at[dst_slot],
                send_sem=send_sems.at[a * 7 + kk], recv_sem=recv_sems.at[a * 7 + kk],
                device_id=peers[kk], device_id_type=MESH)

        sends = [copy(a, kk, _slot(*peers[kk]), mine) for a in range(n) for kk in range(7)]

        def send():
            _handshake(peers)
            for cp in local + sends:
                cp.start()

        def finish():
            for a in range(n):
                for kk in range(7):
                    copy(a, kk, mine, _slot(*peers[kk])).wait_recv()
            for cp in sends:
                cp.wait_send()
            for cp in local:
                cp.wait()

        return [send, finish]

    return list(arrs), [_sds(a.shape, a.dtype) for a in arrs], _exchange_scratch(n), phases, SCATTER_ID


def _pair_exchange(arrs):
    n = len(arrs)

    def phases(ins, outs, send_sems, recv_sems, local_sems):
        x, y, c = _place()

        def copy(a, chip, side):
            return pltpu.make_async_remote_copy(
                src_ref=ins[a].at[chip, side], dst_ref=outs[a].at[chip],
                send_sem=send_sems.at[a * 7 + chip], recv_sem=recv_sems.at[a * 7 + chip],
                device_id=(x, y, 1 - c), device_id_type=MESH)

        sends = [copy(a, chip, 1 - c) for a in range(n) for chip in range(4)]

        def send():
            _handshake([(x, y, 1 - c)])
            for cp in sends:
                cp.start()

        def finish():
            for a in range(n):
                for chip in range(4):
                    copy(a, chip, c).wait_recv()
            for cp in sends:
                cp.wait_send()

        return [send, finish]

    return list(arrs), [_sds((4,) + a.shape[2:], a.dtype) for a in arrs], _exchange_scratch(n), phases, PAIR_ID


def _chip_exchange(arrs):
    n = len(arrs)

    def phases(ins, outs, send_sems, recv_sems, local_sems):
        x, y, c = _place()
        mine = 2 * x + y
        chips = [(1 - x, y), (x, 1 - y), (1 - x, 1 - y)]
        local = [pltpu.make_async_copy(ins[a].at[mine], outs[a].at[mine], local_sems.at[a]) for a in range(n)]

        def copy(a, j, src_slot, dst_slot):
            return pltpu.make_async_remote_copy(
                src_ref=ins[a].at[src_slot], dst_ref=outs[a].at[dst_slot],
                send_sem=send_sems.at[a * 7 + j], recv_sem=recv_sems.at[a * 7 + j],
                device_id=(*chips[j], c), device_id_type=MESH)

        sends = [copy(a, j, 2 * chips[j][0] + chips[j][1], mine) for a in range(n) for j in range(3)]

        def send():
            _handshake([(*chip, c) for chip in chips])
            for cp in local + sends:
                cp.start()

        def finish():
            for a in range(n):
                for j in range(3):
                    copy(a, j, mine, 2 * chips[j][0] + chips[j][1]).wait_recv()
            for cp in sends:
                cp.wait_send()
            for cp in local:
                cp.wait()

        return [send, finish]

    return list(arrs), [_sds(a.shape, a.dtype) for a in arrs], _exchange_scratch(n), phases, CHIP_ID


def _pair_sum(name, mine, theirs):
    _, _, r, c = mine.shape

    def body(side_ref, m_ref, t_ref, o_ref):
        o_ref[...] = (m_ref[...].astype(F32) + t_ref[...].astype(F32)).astype(o_ref.dtype)

    return pl.pallas_call(
        body, name=name,
        grid_spec=pltpu.PrefetchScalarGridSpec(
            num_scalar_prefetch=1, grid=(4,),
            in_specs=[pl.BlockSpec((None, None, r, c), lambda j, side: (j, side[0], 0, 0)),
                      pl.BlockSpec((None, r, c), lambda j, side: (j, 0, 0))],
            out_specs=pl.BlockSpec((None, r, c), lambda j, side: (j, 0, 0))),
        out_shape=_sds(theirs.shape, theirs.dtype),
        compiler_params=_params(("parallel",)),
    )(lax.axis_index("c").astype(jnp.int32).reshape(1), mine, theirs)


def _exchange_call(name, exchange):
    arrs, out_shape, scratch, phases, collective_id = exchange
    n = len(arrs)

    def body(*refs):
        for step in phases(refs[:n], refs[n:2 * n], *refs[2 * n:]):
            step()

    return pl.pallas_call(body, name=name, in_specs=[ANY] * n, out_specs=[ANY] * n,
                          out_shape=out_shape, scratch_shapes=scratch,
                          compiler_params=pltpu.CompilerParams(collective_id=collective_id))(*arrs)


def _carry_exchange(exchange, refs, n_in, n_out, first, middle, last):
    arrs, _, _, phases, _ = exchange
    n = len(arrs)
    if n == 0:
        return lambda: None
    ins = refs[n_in:n_in + n]
    outs = refs[n_in + n + n_out:n_in + 2 * n + n_out]
    sems = n_in + 2 * n + n_out
    steps = phases(ins, outs, *refs[sems:sems + 3])
    pl.when(first)(steps[0])
    if len(steps) == 3:
        pl.when(middle)(steps[1])
    return lambda: pl.when(last)(steps[-1])


def _adamw_math(w, g, m, v):
    m2 = ADAM_B1 * m + (1.0 - ADAM_B1) * g
    v2 = ADAM_B2 * v + (1.0 - ADAM_B2) * jnp.square(g)
    m_hat = m2 / (1.0 - ADAM_B1 ** ADAM_STEP)
    v_hat = v2 / (1.0 - ADAM_B2 ** ADAM_STEP)
    delta = -ADAM_LR * (m_hat / (jnp.sqrt(v_hat) + ADAM_EPS) + ADAM_WD * w)
    return delta, m2, v2


def _sum_adamw_tile(parts, w, m, v):
    g = parts[0].astype(F32)
    for d in range(1, parts.shape[0]):
        g = g + parts[d].astype(F32)
    return (g, *_adamw_math(w, g, m, v))


def _sum_adamw(name, parts, w, m, v, tr=256):
    p, r, c = parts.shape
    tr = _pick(r, tr, 16)

    def body(p_ref, w_ref, m_ref, v_ref, g_ref, d_ref, m2_ref, v2_ref):
        g_ref[...], d_ref[...], m2_ref[...], v2_ref[...] = _sum_adamw_tile(p_ref[...], w_ref[...], m_ref[...], v_ref[...])

    tile = pl.BlockSpec((tr, c), lambda i: (i, 0))
    return pl.pallas_call(
        body, name=name, grid=(r // tr,),
        in_specs=[pl.BlockSpec((p, tr, c), lambda i: (0, i, 0)), tile, tile, tile],
        out_specs=[tile] * 4, out_shape=[_sds((r, c))] * 4,
        compiler_params=_params(("parallel",)),
    )(parts, w, m, v)


def _sum_parts(name, parts):
    p, r, c = parts.shape

    def body(p_ref, o_ref):
        g = p_ref[0]
        for d in range(1, p):
            g = g + p_ref[d]
        o_ref[...] = g

    return pl.pallas_call(
        body, name=name, out_shape=_sds((r, c)),
        in_specs=[pl.BlockSpec(memory_space=pltpu.VMEM)], out_specs=pl.BlockSpec(memory_space=pltpu.VMEM),
    )(parts)


WEIGHTS = ["norm_mix_pre", "w_in", "conv_dw_w", "conv_dw_b", "conv_ln_g", "conv_ln_b", "w_conv_branch",
           "b_conv_branch", "w_att_branch", "w_out", "norm_mix_post", "norm_ffn_pre", "w_ffn_up", "w_ffn_down",
           "norm_ffn_post"]
COL_SHARDED = ["w_conv_branch", "w_att_branch"]
ROW_SHARDED = ["w_out", "w_ffn_down"]
TRANSPOSED = ["w_in", "w_ffn_up"]
VECTORS = ["norm_mix_pre", "conv_dw_b", "conv_ln_g", "conv_ln_b", "b_conv_branch", "norm_mix_post",
           "norm_ffn_pre", "norm_ffn_post"]


def _cols_to_full(g):
    return g.transpose(1, 0, 2).reshape(g.shape[1], N_DEV * g.shape[2])


def _full_to_cols(f):
    return f.reshape(f.shape[0], N_DEV, f.shape[1] // N_DEV).transpose(1, 0, 2)


PACK_ROWS = 7


def _pack_vectors(vecs, extra=None):
    parts = [vecs[nm].reshape(-1) for nm in VECTORS]
    parts.append(jnp.zeros((1,), F32) if extra is None else extra.reshape(1))
    used = sum(p.size for p in parts)
    parts.append(jnp.zeros((PACK_ROWS * D_MODEL - used,), F32))
    return jnp.concatenate(parts).reshape(PACK_ROWS, D_MODEL)


def _unpack_vectors(packed, sizes):
    flat, out, at = packed.reshape(-1), {}, 0
    for nm in VECTORS:
        out[nm] = flat[at:at + sizes[nm]]
        at += sizes[nm]
    return out, flat[at]


def kernel(x, norm_mix_pre, w_in, conv_dw_w, conv_dw_b, conv_ln_g, conv_ln_b, w_conv_branch, b_conv_branch, w_att_branch, w_out, norm_mix_post, norm_ffn_pre, w_ffn_up, w_ffn_down, norm_ffn_post, loss_target, m_norm_mix_pre, m_w_in, m_conv_dw_w, m_conv_dw_b, m_conv_ln_g, m_conv_ln_b, m_w_conv_branch, m_b_conv_branch, m_w_att_branch, m_w_out, m_norm_mix_post, m_norm_ffn_pre, m_w_ffn_up, m_w_ffn_down, m_norm_ffn_post, v_norm_mix_pre, v_w_in, v_conv_dw_w, v_conv_dw_b, v_conv_ln_g, v_conv_ln_b, v_w_conv_branch, v_b_conv_branch, v_w_att_branch, v_w_out, v_norm_mix_post, v_norm_ffn_pre, v_w_ffn_up, v_w_ffn_down, v_norm_ffn_post):
    ws = dict(zip(WEIGHTS, [norm_mix_pre, w_in, conv_dw_w, conv_dw_b, conv_ln_g, conv_ln_b, w_conv_branch,
                            b_conv_branch, w_att_branch, w_out, norm_mix_post, norm_ffn_pre, w_ffn_up, w_ffn_down,
                            norm_ffn_post]))
    ms = dict(zip(WEIGHTS, [m_norm_mix_pre, m_w_in, m_conv_dw_w, m_conv_dw_b, m_conv_ln_g, m_conv_ln_b,
                            m_w_conv_branch, m_b_conv_branch, m_w_att_branch, m_w_out, m_norm_mix_post,
                            m_norm_ffn_pre, m_w_ffn_up, m_w_ffn_down, m_norm_ffn_post]))
    vs = dict(zip(WEIGHTS, [v_norm_mix_pre, v_w_in, v_conv_dw_w, v_conv_dw_b, v_conv_ln_g, v_conv_ln_b,
                            v_w_conv_branch, v_b_conv_branch, v_w_att_branch, v_w_out, v_norm_mix_post,
                            v_norm_ffn_pre, v_w_ffn_up, v_w_ffn_down, v_norm_ffn_post]))

    dw_block = jnp.pad(conv_dw_w, ((0, 1), (0, 0)))
    g_in, g_dw = _exchange_call("gather_first", _gather_exchange([w_in.T.astype(BF16), dw_block]))
    full = {"w_in": _full_weight("w_in", g_in), "conv_dw_w": _cols_to_full(g_dw)}
    for nm in VECTORS:
        full[nm] = ws[nm].reshape(1, -1)

    def as_kept(nm, a):
        return a.T if nm in TRANSPOSED else a

    ride_along = ["w_ffn_up", "w_out"]
    loss_local, grad_x, received, updated, grads = _local_step(
        x[0], loss_target[0], full, {nm: as_kept(nm, ws[nm]).astype(BF16) for nm in LATE},
        {nm: tuple(as_kept(nm, a[nm]) for a in (ws, ms, vs)) for nm in ride_along})

    small = _exchange_call("gather_small_grads", _gather_exchange(
        [_pack_vectors(grads, extra=loss_local), grads["conv_dw_w"]]))
    out_g, out_d, out_m, out_v = {}, {}, {}, {}
    for nm in LATE + ["w_in"]:
        res = updated[nm] if nm in updated else _sum_adamw(
            "adamw_" + nm, received[nm], *[as_kept(nm, a[nm]) for a in (ws, ms, vs)])
        out_g[nm], out_d[nm], out_m[nm], out_v[nm] = [as_kept(nm, r) for r in res]
    sizes = {nm: ws[nm].size for nm in VECTORS}
    vec = _sum_adamw("adamw_vectors", small[0], _pack_vectors(ws), _pack_vectors(ms), _pack_vectors(vs))
    for res, dst in zip(vec, (out_g, out_d, out_m, out_v)):
        dst.update(_unpack_vectors(res, sizes)[0])
    loss = _unpack_vectors(vec[0], sizes)[1]
    dw_full = _sum_parts("sum_dw_grads", small[1])
    me = _slot(*_place())
    dw_mine = lax.dynamic_slice(dw_full, (0, me * (CONV_DIM // N_DEV)), (CONV_WIDTH, CONV_DIM // N_DEV))
    nm = "conv_dw_w"
    out_g[nm], out_d[nm], out_m[nm], out_v[nm] = _sum_adamw("adamw_dw", dw_mine[None], ws[nm], ms[nm], vs[nm])

    outs = [loss, grad_x[None]]
    for group in (out_g, out_d, out_m, out_v):
        outs += [group[nm] for nm in WEIGHTS]
    return tuple(outs)
```

```python
import math

import jax
import jax.numpy as jnp
from jax import lax
from jax.experimental import pallas as pl
from jax.experimental.pallas import tpu as pltpu

F32 = jnp.float32
BF16 = jnp.bfloat16

N_DEV = 8
D_MODEL = 1024
CONV_DIM = 512
CONV_WIDTH = 31
N_HEADS = 8
HEAD_DIM = 64
ATT_DIM = N_HEADS * HEAD_DIM
D_FF = 2816
EPS = 1e-6
IN_SPLITS = (0, 1024, 1536, 2048, 2560, 3584, 4608)

ADAM_LR = 0.001
ADAM_B1 = 0.9
ADAM_B2 = 0.999
ADAM_EPS = 1e-08
ADAM_WD = 0.01
ADAM_STEP = 10

LANES = 128
SUBLANES = 8
HALO = 32
ATT_TILE = 256
ATT_PART = 192
DEAD_SUM = -120.0
VMEM_LIMIT = 56 * 1024 * 1024
MESH = pl.DeviceIdType.MESH
ANY = pl.BlockSpec(memory_space=pl.ANY)


def _pick(dim, target, align=LANES):
    t = min(dim, target)
    t -= t % align
    while t >= align:
        if dim % t == 0:
            return t
        t -= align
    return dim


def _params(semantics, collective_id=None):
    return pltpu.CompilerParams(dimension_semantics=semantics, vmem_limit_bytes=VMEM_LIMIT,
                                collective_id=collective_id)


def _matmul(a, b, *, name, ta=False, tb=False, out_dtype=F32):
    m, k = (a.shape[1], a.shape[0]) if ta else a.shape
    n, k2 = b.shape if tb else (b.shape[1], b.shape[0])
    assert k == k2, (a.shape, b.shape, ta, tb)
    tm, tn, tk = _pick(m, 1408 if ta else 512), _pick(n, 1536), _pick(k, 1536)
    nk = k // tk
    dims = (((0 if ta else 1,), (1 if tb else 0,)), ((), ()))

    def body(a_ref, b_ref, o_ref, *acc):
        part = lax.dot_general(a_ref[...], b_ref[...], dims, preferred_element_type=F32)
        if nk == 1:
            o_ref[...] = part.astype(o_ref.dtype)
            return
        acc_ref, = acc
        kk = pl.program_id(2)

        @pl.when(kk == 0)
        def _():
            acc_ref[...] = part

        @pl.when((kk > 0) & (kk < nk - 1))
        def _():
            acc_ref[...] += part

        @pl.when(kk == nk - 1)
        def _():
            o_ref[...] = (acc_ref[...] + part).astype(o_ref.dtype)

    a_spec = pl.BlockSpec((tk, tm), lambda j, i, kk: (kk, i)) if ta else pl.BlockSpec((tm, tk), lambda j, i, kk: (i, kk))
    b_spec = (pl.BlockSpec((tn, tk), lambda j, i, kk: (j, kk)) if tb
              else pl.BlockSpec((tk, tn), lambda j, i, kk: (kk, j)))
    return pl.pallas_call(
        body, name=name, grid=(n // tn, m // tm, nk),
        in_specs=[a_spec, b_spec],
        out_specs=pl.BlockSpec((tm, tn), lambda j, i, kk: (i, j)),
        out_shape=jax.ShapeDtypeStruct((m, n), out_dtype),
        scratch_shapes=[pltpu.VMEM((tm, tn), F32)] if nk > 1 else [],
        compiler_params=_params(("parallel", "parallel", "arbitrary")),
    )(a, b)


def _pieces_tn_matmul(pieces, b, *, name, tj=512):
    s, n = b.shape
    counts = [p.shape[1] // tj for p in pieces]
    starts = [sum(counts[:i]) for i in range(len(pieces))]
    assert all(p.shape == (s, c * tj) for p, c in zip(pieces, counts))

    def body(*refs):
        b_ref, o_ref = refs[len(pieces):]
        j = pl.program_id(0)
        for p_ref, first, count in zip(refs, starts, counts):
            @pl.when((j >= first) & (j < first + count))
            def _():
                o_ref[...] = lax.dot_general(p_ref[...].astype(BF16), b_ref[...], TN,
                                             preferred_element_type=F32).astype(o_ref.dtype)

    def piece_spec(first, count):
        return pl.BlockSpec((s, tj), lambda j: (0, jnp.clip(j - first, 0, count - 1)))

    return pl.pallas_call(
        body, name=name, grid=(sum(counts),),
        in_specs=[piece_spec(f, c) for f, c in zip(starts, counts)]
        + [pl.BlockSpec((s, n), lambda j: (0, 0), pipeline_mode=pl.Buffered(1))],
        out_specs=pl.BlockSpec((tj, n), lambda j: (j, 0)),
        out_shape=jax.ShapeDtypeStruct((sum(counts) * tj, n), BF16),
        compiler_params=_params(("arbitrary",)),
    )(*pieces, b)


NO_EXCHANGE = ([], [], [], None, None)


def _sweep_marks(nt):
    i = pl.program_id(0)
    return i == 0, i == (3 * nt) // 4, i == nt - 1


def _rowwise(name, fn, rows, bcasts, row_outs, red_outs=(), tm=256, exchange=NO_EXCHANGE):
    s = rows[0].shape[0]
    tm = _pick(s, tm, 16)
    nt = s // tm
    resident = pl.Buffered(1)
    nr, nb, no, nd = len(rows), len(bcasts), len(row_outs), len(red_outs)
    x_arrs, x_shape, x_scratch, _, x_id = exchange
    nx = len(x_arrs)
    first_out = nr + nb + nx

    def body(*refs):
        finish_exchange = _carry_exchange(exchange, refs, nr + nb, no + nd, *_sweep_marks(nt))
        ins = [r[...] for r in refs[:nr + nb]]
        outs, reds = fn(*ins)
        for ref, val in zip(refs[first_out:first_out + no], outs):
            ref[...] = val.astype(ref.dtype)
        i = pl.program_id(0)
        for ref, val in zip(refs[first_out + no:first_out + no + nd], reds):
            @pl.when(i == 0)
            def _():
                ref[...] = val

            @pl.when(i > 0)
            def _():
                ref[...] += val
        finish_exchange()

    def row_spec(a):
        assert a.shape[-2] % nt == 0, (name, a.shape, nt)
        if len(a.shape) == 3:
            return pl.BlockSpec((a.shape[0], a.shape[1] // nt, a.shape[2]), lambda i: (0, i, 0))
        return pl.BlockSpec((a.shape[0] // nt, a.shape[1]), lambda i: (i, 0))

    in_specs = [row_spec(r) for r in rows]
    in_specs += [pl.BlockSpec(b.shape, lambda i: (0, 0), pipeline_mode=resident) for b in bcasts]
    out_specs = [row_spec(o) for o in row_outs]
    out_specs += [pl.BlockSpec(d.shape, lambda i: (0, 0)) for d in red_outs]
    return pl.pallas_call(
        body, name=name, grid=(nt,), in_specs=in_specs + [ANY] * nx, out_specs=out_specs + [ANY] * nx,
        out_shape=list(row_outs) + list(red_outs) + x_shape, scratch_shapes=x_scratch,
        compiler_params=_params(("arbitrary",), x_id),
    )(*rows, *bcasts, *x_arrs)


def _sds(shape, dtype=F32):
    return jax.ShapeDtypeStruct(shape, dtype)


def _rms(x, g):
    y = x * lax.rsqrt(jnp.mean(x * x, axis=-1, keepdims=True) + EPS)
    return y * g


def _silu(x):
    return x * jax.nn.sigmoid(x)


def _swiglu(g, u):
    return _silu(g) * u


def _ln_silu(u, g, b):
    mu = jnp.mean(u, axis=-1, keepdims=True)
    var = jnp.mean(jnp.square(u - mu), axis=-1, keepdims=True)
    return _silu((u - mu) * lax.rsqrt(var + EPS) * g + b)


def _merge(conv_pre, att_out, g_conv, g_att, b_cb):
    return jax.nn.sigmoid(g_conv) * (conv_pre + b_cb) + jax.nn.sigmoid(g_att) * att_out


def _glu(t):
    return t[:, :CONV_DIM] * jax.nn.sigmoid(t[:, CONV_DIM:])


def _shifted_reader(buf, shifted, tm):
    for b in range(1, SUBLANES):
        shifted[b - 1, :, :] = buf[pl.ds(b, tm + HALO - SUBLANES), :]

    def read(o):
        a, b = divmod(o, SUBLANES)
        return buf[pl.ds(SUBLANES * a, tm), :] if b == 0 else shifted[b - 1, pl.ds(SUBLANES * a, tm), :]

    return read


def _conv_fwd(conv_in, w_pad, b, ln_g, ln_b, exchange, tm=256):
    s = conv_in.shape[0]
    tm = _pick(s, tm, HALO)
    ratio = tm // HALO
    x_arrs, x_shape, x_scratch, _, x_id = exchange
    nx = len(x_arrs)

    def body(*refs):
        main_ref, halo_ref, w_ref, b_ref, g_ref, be_ref = refs[:6]
        u3_ref, u1_ref = refs[6 + nx:8 + nx]
        buf, shifted = refs[-2:]
        finish_exchange = _carry_exchange(exchange, refs, 6, 2, *_sweep_marks(s // tm))
        i = pl.program_id(0)
        buf[0:HALO, :] = _glu(halo_ref[...]) * (i > 0).astype(F32)
        buf[HALO:HALO + tm, :] = _glu(main_ref[...])
        read = _shifted_reader(buf, shifted, tm)
        acc = jnp.zeros((tm, CONV_DIM), F32) + b_ref[...]
        for j in range(CONV_WIDTH):
            acc = acc + w_ref[j:j + 1, :] * read(HALO - (CONV_WIDTH - 1) + j)
        u1_ref[...] = acc
        u3_ref[...] = _ln_silu(acc, g_ref[...], be_ref[...]).astype(u3_ref.dtype)
        finish_exchange()

    res = pl.pallas_call(
        body, name="conv_fwd", grid=(s // tm,),
        in_specs=[pl.BlockSpec((tm, 2 * CONV_DIM), lambda i: (i, 0)),
                  pl.BlockSpec((HALO, 2 * CONV_DIM), lambda i: (jnp.maximum(i * ratio - 1, 0), 0)),
                  pl.BlockSpec(w_pad.shape, lambda i: (0, 0)),
                  pl.BlockSpec(b.shape, lambda i: (0, 0)),
                  pl.BlockSpec(ln_g.shape, lambda i: (0, 0)),
                  pl.BlockSpec(ln_b.shape, lambda i: (0, 0))] + [ANY] * nx,
        out_specs=[pl.BlockSpec((tm, CONV_DIM), lambda i: (i, 0)),
                   pl.BlockSpec((tm, CONV_DIM), lambda i: (i, 0))] + [ANY] * nx,
        out_shape=[_sds((s, CONV_DIM), BF16), _sds((s, CONV_DIM), F32)] + x_shape,
        scratch_shapes=x_scratch + [pltpu.VMEM((tm + HALO, CONV_DIM), F32),
                                    pltpu.VMEM((SUBLANES - 1, tm + HALO - SUBLANES, CONV_DIM), F32)],
        compiler_params=_params(("arbitrary",), x_id),
    )(conv_in, conv_in, w_pad, b, ln_g, ln_b, *x_arrs)
    return res[0], res[1], res[2:]


def _conv_bwd(conv_in, u1, du3, ln_g, ln_b, w_pad, exchange, tm=256):
    s = conv_in.shape[0]
    tm = _pick(s, tm, HALO)
    ratio = tm // HALO
    nt = s // tm
    last_halo = s // HALO - 1
    x_arrs, x_shape, x_scratch, _, x_id = exchange
    nx = len(x_arrs)

    def body(*refs):
        main_ref, halo_ref, u1_ref, u1n_ref, du3_ref, du3n_ref, g_ref, be_ref, w_ref = refs[:9]
        dci_ref, dw_ref, db_ref, dg_ref, dbe_ref = refs[9 + nx:14 + nx]
        ubuf, dbuf, ushift, dshift = refs[-4:]
        finish_exchange = _carry_exchange(exchange, refs, 9, 5, *_sweep_marks(nt))
        i = pl.program_id(0)
        main = main_ref[...]
        a = main[:, :CONV_DIM]
        sb = jax.nn.sigmoid(main[:, CONV_DIM:])
        ubuf[0:HALO, :] = _glu(halo_ref[...]) * (i > 0).astype(F32)
        ubuf[HALO:HALO + tm, :] = a * sb

        def ln_bwd(u1t, du3t):
            _, vjp = jax.vjp(_ln_silu, u1t, g_ref[...], be_ref[...])
            return vjp(du3t)

        du, dg, dbe = ln_bwd(u1_ref[...], du3_ref[...])
        dbuf[0:tm, :] = du
        dbuf[tm:tm + HALO, :] = ln_bwd(u1n_ref[...], du3n_ref[...])[0] * (i < nt - 1).astype(F32)

        @pl.when(i == 0)
        def _():
            dw_ref[...] = jnp.zeros_like(dw_ref)
            db_ref[...] = jnp.zeros_like(db_ref)
            dg_ref[...] = jnp.zeros_like(dg_ref)
            dbe_ref[...] = jnp.zeros_like(dbe_ref)

        dg_ref[...] += dg
        dbe_ref[...] += dbe

        read_u = _shifted_reader(ubuf, ushift, tm)
        read_d = _shifted_reader(dbuf, dshift, tm)
        du0 = jnp.zeros((tm, CONV_DIM), F32)
        for j in range(CONV_WIDTH):
            du0 = du0 + w_ref[j:j + 1, :] * read_d(CONV_WIDTH - 1 - j)
            dw_ref[j:j + 1, :] += jnp.sum(du * read_u(HALO - (CONV_WIDTH - 1) + j), axis=0, keepdims=True)
        db_ref[...] += jnp.sum(du, axis=0, keepdims=True)
        dci_ref[:, :CONV_DIM] = (du0 * sb).astype(dci_ref.dtype)
        dci_ref[:, CONV_DIM:] = (du0 * a * sb * (1.0 - sb)).astype(dci_ref.dtype)
        finish_exchange()

    res = pl.pallas_call(
        body, name="conv_bwd", grid=(nt,),
        in_specs=[pl.BlockSpec((tm, 2 * CONV_DIM), lambda i: (i, 0)),
                  pl.BlockSpec((HALO, 2 * CONV_DIM), lambda i: (jnp.maximum(i * ratio - 1, 0), 0))]
        + [pl.BlockSpec((tm, CONV_DIM), lambda i: (i, 0)),
           pl.BlockSpec((HALO, CONV_DIM), lambda i: (jnp.minimum((i + 1) * ratio, last_halo), 0))] * 2
        + [pl.BlockSpec((1, CONV_DIM), lambda i: (0, 0))] * 2 + [pl.BlockSpec(w_pad.shape, lambda i: (0, 0))]
        + [ANY] * nx,
        out_specs=[pl.BlockSpec((tm, 2 * CONV_DIM), lambda i: (i, 0)),
                   pl.BlockSpec(w_pad.shape, lambda i: (0, 0))]
        + [pl.BlockSpec((1, CONV_DIM), lambda i: (0, 0))] * 3 + [ANY] * nx,
        out_shape=[_sds((s, 2 * CONV_DIM), BF16), _sds(w_pad.shape)] + [_sds((1, CONV_DIM))] * 3 + x_shape,
        scratch_shapes=x_scratch + [pltpu.VMEM((tm + HALO, CONV_DIM), F32)] * 2
        + [pltpu.VMEM((SUBLANES - 1, tm + HALO - SUBLANES, CONV_DIM), F32)] * 2,
        compiler_params=_params(("arbitrary",), x_id),
    )(conv_in, conv_in, u1, u1, du3, du3, ln_g, ln_b, w_pad, *x_arrs)
    return res[:5], res[5:]


def _logsig_neg(z):
    return jnp.minimum(-z, 0.0) - jnp.log(1.0 + jnp.exp(-jnp.abs(z)))


def _split_dot(val, tri):
    hi = val.astype(BF16)
    lo = (val - hi.astype(F32)).astype(BF16)
    return jnp.dot(hi, tri, preferred_element_type=F32) + jnp.dot(lo, tri, preferred_element_type=F32)


def _attn_masks(t, later):
    row = lax.broadcasted_iota(jnp.int32, (t, t), 0)
    col = lax.broadcasted_iota(jnp.int32, (t, t), 1)
    tri = jnp.where(row > col if later else row <= col, 1.0, 0.0).astype(BF16)
    return col < row, tri


def _grid_marks(h, nq):
    hh, i = pl.program_id(0), pl.program_id(1)
    return (hh == 0) & (i == 0), (hh == (3 * h) // 4) & (i == 0), (hh == h - 1) & (i == nq - 1)


def _head_masks(shape):
    lane = lax.broadcasted_iota(jnp.int32, shape, len(shape) - 1)
    return lane < HEAD_DIM, lane >= HEAD_DIM


def _per_head(blk):
    m0, m1 = _head_masks(blk.shape)
    zero = jnp.zeros_like(blk)
    return jnp.where(m0, blk, zero), jnp.where(m1, blk, zero)


NT = (((1,), (1,)), ((), ()))
TN = (((0,), (0,)), ((), ()))


def _with_top(whole, top):
    rows = top.shape[0]
    return top if rows == whole.shape[0] else jnp.concatenate([top, whole[rows:]], axis=0)


def _attn_fwd(q, k, v, exchange):
    s = q.shape[0]
    hp = q.shape[1] // LANES
    t = ATT_TILE
    scale = 1.0 / math.sqrt(HEAD_DIM)
    x_arrs, x_shape, x_scratch, _, x_id = exchange
    nx = len(x_arrs)

    def body(*refs):
        q_ref, k_ref, v_ref = refs[:3]
        o_ref, lt_ref, nb_ref = refs[3 + nx:6 + nx]
        finish_exchange = _carry_exchange(exchange, refs, 3, 3, *_grid_marks(hp, s // t))
        i = pl.program_id(1)
        qs = _per_head((q_ref[...].astype(F32) * scale).astype(BF16))
        causal, tri = _attn_masks(t, later=True)

        def step(kb, carry, masked, rows):
            cs, acc = carry
            off = pl.multiple_of(kb * t, t)
            kblk = k_ref[pl.ds(off, t), :]
            vs = _per_head(v_ref[pl.ds(off, t), :])
            acc_top = acc[:rows]
            new_cs = []
            for hd in range(2):
                z = lax.dot_general(qs[hd][:rows], kblk, NT, preferred_element_type=F32)
                l = _logsig_neg(z)
                if masked:
                    l = jnp.where(causal, l, 0.0)
                e = z + l + _split_dot(l, tri) + cs[hd][:rows]
                if masked:
                    e = jnp.where(causal, e, -1e30)
                acc_top = acc_top + jnp.dot(jnp.exp(e).astype(BF16), vs[hd], preferred_element_type=F32)
                new_cs.append(_with_top(cs[hd], cs[hd][:rows] + jnp.sum(l, axis=1, keepdims=True)))
            return tuple(new_cs), _with_top(acc, acc_top)

        zero = jnp.zeros((t, 1), F32)
        carry = step(i, ((zero, zero), jnp.zeros((t, LANES), F32)), True, t)

        def live(cs, lo, hi):
            return jnp.maximum(jnp.max(cs[0][lo:hi]), jnp.max(cs[1][lo:hi])) > DEAD_SUM

        def more(state):
            n, _, (cs, _) = state
            return (n < i) & live(cs, 0, t)

        def sweep(state):
            n, n_full, cr = state
            whole = live(cr[0], ATT_PART, t)
            cr = lax.cond(whole, lambda c: step(i - 1 - n, c, False, t), lambda c: step(i - 1 - n, c, False, ATT_PART), cr)
            return n + 1, n_full + whole.astype(jnp.int32), cr

        n_blocks, n_full, carry = lax.while_loop(more, sweep, (jnp.int32(0), jnp.int32(0), carry))
        m0, _ = _head_masks((t, LANES))
        lt_ref[...] = jnp.where(m0, carry[0][0], carry[0][1])
        o_ref[...] = carry[1].astype(o_ref.dtype)
        nb_ref[0, pl.program_id(0), i] = n_blocks.astype(F32)
        nb_ref[1, pl.program_id(0), i] = n_full.astype(F32)
        finish_exchange()

    res = pl.pallas_call(
        body, name="attn_fwd", grid=(hp, s // t),
        in_specs=[pl.BlockSpec((t, LANES), lambda p, i: (i, p)),
                  pl.BlockSpec((s, LANES), lambda p, i: (0, p)),
                  pl.BlockSpec((s, LANES), lambda p, i: (0, p))] + [ANY] * nx,
        out_specs=[pl.BlockSpec((t, LANES), lambda p, i: (i, p)),
                   pl.BlockSpec((None, t, LANES), lambda p, i: (p, i, 0)),
                   pl.BlockSpec(memory_space=pltpu.SMEM)] + [ANY] * nx,
        out_shape=[_sds(q.shape, BF16), _sds((hp, s, LANES), F32), _sds((2, hp, s // t), F32)] + x_shape,
        scratch_shapes=x_scratch,
        compiler_params=_params(("arbitrary", "arbitrary"), x_id),
    )(q, k, v, *x_arrs)
    return res[0], res[1], res[2], res[3:]


def _attn_bwd(q, k, v, do, ltot, n_blocks, exchange):
    s = q.shape[0]
    hp = q.shape[1] // LANES
    t = ATT_TILE
    scale = 1.0 / math.sqrt(HEAD_DIM)
    x_arrs, x_shape, x_scratch, _, x_id = exchange
    nx = len(x_arrs)

    def body(*refs):
        q_ref, k_ref, v_ref, do_ref, lt_ref, nb_ref = refs[:6]
        dq_ref, dk_ref, dv_ref = refs[6 + nx:9 + nx]
        finish_exchange = _carry_exchange(exchange, refs, 6, 3, *_grid_marks(hp, s // t))
        i = pl.program_id(1)
        n_blocks = jnp.clip(nb_ref[0, pl.program_id(0), i].astype(jnp.int32), 0, i)
        n_full = jnp.clip(nb_ref[1, pl.program_id(0), i].astype(jnp.int32), 0, n_blocks)

        @pl.when(i == 0)
        def _():
            dk_ref[...] = jnp.zeros_like(dk_ref)
            dv_ref[...] = jnp.zeros_like(dv_ref)

        qb = q_ref[...]
        qm = _per_head(qb)
        qs = _per_head((qb.astype(F32) * scale).astype(BF16))
        dos = _per_head(do_ref[...])
        lts = (lt_ref[:, 0:1], lt_ref[:, HEAD_DIM:HEAD_DIM + 1])
        causal, tri = _attn_masks(t, later=False)

        def step(kb, carry, masked, rows):
            cls, cgs, dq = carry
            off = pl.multiple_of(kb * t, t)
            kblk = k_ref[pl.ds(off, t), :]
            vblk = v_ref[pl.ds(off, t), :]
            ks = _per_head(kblk)
            dq_top = dq[:rows]
            dk = jnp.zeros((t, LANES), F32)
            dv = jnp.zeros((t, LANES), F32)
            new_cls, new_cgs = [], []
            for hd in range(2):
                z = lax.dot_general(qs[hd][:rows], kblk, NT, preferred_element_type=F32)
                l = _logsig_neg(z)
                if masked:
                    l = jnp.where(causal, l, 0.0)
                e = z + l + ((lts[hd][:rows] - cls[hd][:rows]) - _split_dot(l, tri))
                if masked:
                    e = jnp.where(causal, e, -1e30)
                a = jnp.exp(e)
                g = lax.dot_general(dos[hd][:rows], vblk, NT, preferred_element_type=F32) * a
                p = cgs[hd][:rows] + jnp.dot(g.astype(BF16), tri, preferred_element_type=F32) - g
                el = jnp.exp(l)
                dz = g * el - p * (1.0 - el)
                if masked:
                    dz = jnp.where(causal, dz, 0.0)
                dzb = (dz * scale).astype(BF16)
                dq_top = dq_top + jnp.dot(dzb, ks[hd], preferred_element_type=F32)
                dk = dk + lax.dot_general(dzb, qm[hd][:rows], TN, preferred_element_type=F32)
                dv = dv + lax.dot_general(a.astype(BF16), dos[hd][:rows], TN, preferred_element_type=F32)
                new_cls.append(_with_top(cls[hd], cls[hd][:rows] + jnp.sum(l, axis=1, keepdims=True)))
                new_cgs.append(_with_top(cgs[hd], cgs[hd][:rows] + jnp.sum(g, axis=1, keepdims=True)))
            dk_ref[pl.ds(off, t), :] += dk
            dv_ref[pl.ds(off, t), :] += dv
            return tuple(new_cls), tuple(new_cgs), _with_top(dq, dq_top)

        zero = jnp.zeros((t, 1), F32)
        init = ((zero, zero), (zero, zero), jnp.zeros((t, LANES), F32))
        carry = lax.fori_loop(i - n_blocks, i - n_full, lambda kb, cr: step(kb, cr, False, ATT_PART), init)
        carry = lax.fori_loop(i - n_full, i, lambda kb, cr: step(kb, cr, False, t), carry)
        carry = step(i, carry, True, t)
        dq_ref[...] = carry[2]
        finish_exchange()

    blk = pl.BlockSpec((t, LANES), lambda p, i: (i, p))
    whole = pl.BlockSpec((s, LANES), lambda p, i: (0, p))
    res = pl.pallas_call(
        body, name="attn_bwd", grid=(hp, s // t),
        in_specs=[blk, whole, whole, blk, pl.BlockSpec((None, t, LANES), lambda p, i: (p, i, 0)),
                  pl.BlockSpec(memory_space=pltpu.SMEM)] + [ANY] * nx,
        out_specs=[blk, whole, whole] + [ANY] * nx,
        out_shape=[_sds(q.shape)] * 3 + x_shape,
        scratch_shapes=x_scratch,
        compiler_params=_params(("arbitrary", "arbitrary"), x_id),
    )(q, k, v, do, ltot, n_blocks, *x_arrs)
    return res[0], res[1], res[2], res[3:]


LATE = ["w_conv_branch", "w_att_branch", "w_out", "w_ffn_up", "w_ffn_down"]


def _full_weight(name, gathered):
    return _cols_to_full(gathered) if name in COL_SHARDED else gathered.reshape(-1, gathered.shape[2])


def _grad_slabs(name, grad):
    return _full_to_cols(grad) if name in COL_SHARDED else grad.reshape(N_DEV, -1, grad.shape[1])


def _side_slabs(name, grad):
    slabs = _grad_slabs(name, grad)
    return slabs.reshape((4, 2) + slabs.shape[1:])


def _local_step(x, target, w, late_blocks, opt):
    s = x.shape[0]
    w = dict(w)
    g1, g2, g3, g4 = w["norm_mix_pre"], w["norm_mix_post"], w["norm_ffn_pre"], w["norm_ffn_post"]

    w_in = w["w_in"]

    def proj_fn(xt, g1_, w_in_t):
        h = _rms(xt, g1_).astype(BF16)
        proj = lax.dot_general(h, w_in_t, NT, preferred_element_type=F32)
        return (h, *[proj[:, IN_SPLITS[n]:IN_SPLITS[n + 1]] for n in range(6)]), ()

    mix_weights = ["w_conv_branch", "w_att_branch", "w_out"]
    h1, conv_in, q, k, v, g_conv, g_att = _rowwise(
        "norm_proj", proj_fn, [x], [g1, w_in],
        [_sds((s, D_MODEL), BF16), _sds((s, 2 * CONV_DIM)), _sds((s, ATT_DIM), BF16), _sds((s, ATT_DIM), BF16),
         _sds((s, ATT_DIM), BF16), _sds((s, D_MODEL), BF16), _sds((s, D_MODEL), BF16)], tm=512)

    u3, u1, gathered = _conv_fwd(conv_in, w["conv_dw_w"], w["conv_dw_b"], w["conv_ln_g"], w["conv_ln_b"],
                                 _gather_exchange([late_blocks[nm] for nm in mix_weights]))
    for nm, g in zip(mix_weights, gathered):
        w[nm] = _full_weight(nm, g)
    att, ltot, n_blocks, (g_up,) = _attn_fwd(q, k, v, _gather_exchange([late_blocks["w_ffn_up"]]))
    w["w_ffn_up"] = _full_weight("w_ffn_up", g_up)

    def merge_fn(u3t, at, gc, ga, xt, w_cb, w_ab, b_cb, w_out, g2_, g3_):
        cp = jnp.dot(u3t, w_cb, preferred_element_type=F32)
        ao = jnp.dot(at, w_ab, preferred_element_type=F32)
        mg = _merge(cp, ao, gc.astype(F32), ga.astype(F32), b_cb).astype(BF16)
        mix_ = jnp.dot(mg, w_out, preferred_element_type=F32)
        x2_ = xt + _rms(mix_, g2_)
        return (mg, cp, ao, mix_, x2_, _rms(x2_, g3_)), ()

    merged, conv_pre, att_out, mix, x2, h2 = _rowwise(
        "branch_merge_mix", merge_fn, [u3, att, g_conv, g_att, x],
        [w["w_conv_branch"], w["w_att_branch"], w["b_conv_branch"], w["w_out"], g2, g3],
        [_sds((s, D_MODEL), BF16)] * 3 + [_sds((s, D_MODEL)), _sds((s, D_MODEL)), _sds((s, D_MODEL), BF16)], tm=512)

    def ffn_up_fn(ht, w_up_t):
        gu_ = lax.dot_general(ht, w_up_t, NT, preferred_element_type=F32)
        return (gu_, _swiglu(gu_[:, :D_FF], gu_[:, D_FF:])), ()

    gu, act, g_down = _rowwise("ffn_up", ffn_up_fn, [h2], [w["w_ffn_up"]],
                               [_sds((s, 2 * D_FF), BF16), _sds((s, D_FF), BF16)], tm=512,
                               exchange=_gather_exchange([late_blocks["w_ffn_down"]]))
    w["w_ffn_down"] = _full_weight("w_ffn_down", g_down)

    def final_fn(at, x2t, tgt, w_down, g4_):
        ff = jnp.dot(at, w_down, preferred_element_type=F32)
        n4, vjp = jax.vjp(_rms, ff, g4_)
        err = x2t + n4 - tgt
        dy = err * (1.0 / D_MODEL)
        dff, dg4 = vjp(dy)
        return (dy, dff), (jnp.sum(err * err, axis=0, keepdims=True), dg4)

    dy, dff, loss_cols, d_g4 = _rowwise("ffn_down_loss", final_fn, [act, x2, target], [w["w_ffn_down"], g4],
                                        [_sds((s, D_MODEL)), _sds((s, D_MODEL), BF16)],
                                        [_sds((1, D_MODEL)), _sds((1, D_MODEL))], tm=512)
    loss = 0.5 * jnp.sum(loss_cols) / D_MODEL

    d_w_down = _matmul(act, dff, ta=True, name="d_w_down", out_dtype=BF16)

    def act_bwd_fn(dfft, gut, w_down):
        d_act = lax.dot_general(dfft, w_down, NT, preferred_element_type=F32)
        gu_ = gut.astype(F32)
        _, vjp = jax.vjp(_swiglu, gu_[:, :D_FF], gu_[:, D_FF:])
        return (jnp.concatenate(vjp(d_act), axis=1),), ()

    down_slabs = _side_slabs("w_ffn_down", d_w_down)
    dgu, theirs = _rowwise("ffn_act_bwd", act_bwd_fn, [dff, gu], [w["w_ffn_down"]], [_sds((s, 2 * D_FF), BF16)],
                           exchange=_pair_exchange([down_slabs]))
    down_sums = _pair_sum("pair_sum_w_ffn_down", down_slabs, theirs)
    d_w_up = _matmul(dgu, h2, ta=True, name="d_w_up", out_dtype=BF16)
    received = {}
    up_slabs = _side_slabs("w_ffn_up", d_w_up)

    def mid_bwd_fn(dgut, xt, mt, dyt, w_up_t, g2_, g3_):
        dh = jnp.dot(dgut, w_up_t, preferred_element_type=F32)
        n2, vjp2 = jax.vjp(_rms, mt, g2_)
        x2_ = xt + n2
        _, vjp3 = jax.vjp(_rms, x2_, g3_)
        dx2_, dg3 = vjp3(dh)
        dx2_ = dx2_ + dyt
        dmix_, dg2 = vjp2(dx2_)
        return (dx2_, dmix_), (dg2, dg3)

    dx2, dmix, d_g2, d_g3, received["w_ffn_down"] = _rowwise(
        "ffn_up_mid_bwd", mid_bwd_fn, [dgu, x, mix, dy], [w["w_ffn_up"], g2, g3],
        [_sds((s, D_MODEL)), _sds((s, D_MODEL), BF16)], [_sds((1, D_MODEL)), _sds((1, D_MODEL))], tm=512,
        exchange=_chip_exchange([down_sums]))
    d_w_out = _matmul(merged, dmix, ta=True, name="d_w_out", out_dtype=BF16)

    def merge_bwd_fn(dmt, cp, ao, gc, ga, w_out, w_cb, w_ab, b_cb):
        dm = lax.dot_general(dmt, w_out, NT, preferred_element_type=F32)
        _, vjp = jax.vjp(_merge, cp.astype(F32), ao.astype(F32), gc.astype(F32), ga.astype(F32), b_cb)
        dcp, dao, dgc, dga, dbias = vjp(dm)
        dcp, dao = dcp.astype(BF16), dao.astype(BF16)
        du3_ = lax.dot_general(dcp, w_cb, NT, preferred_element_type=F32)
        datt_ = lax.dot_general(dao, w_ab, NT, preferred_element_type=F32)
        return (dcp, dao, dgc, dga, du3_, datt_), (dbias,)

    d_conv_out, d_att_out, d_g_conv, d_g_att, du3, d_att, d_b_cb, theirs = _rowwise(
        "merge_bwd", merge_bwd_fn, [dmix, conv_pre, att_out, g_conv, g_att],
        [w["w_out"], w["w_conv_branch"], w["w_att_branch"], w["b_conv_branch"]],
        [_sds((s, D_MODEL), BF16)] * 4 + [_sds((s, CONV_DIM)), _sds((s, ATT_DIM), BF16)], [_sds((1, D_MODEL))], tm=512,
        exchange=_pair_exchange([up_slabs]))

    d_w_cb = _matmul(u3, d_conv_out, ta=True, name="d_w_conv_branch", out_dtype=BF16)
    d_w_ab = _matmul(att, d_att_out, ta=True, name="d_w_att_branch", out_dtype=BF16)

    dq, dk, dv, (received["w_ffn_up"],) = _attn_bwd(
        q, k, v, d_att, ltot, n_blocks, _chip_exchange([_pair_sum("pair_sum_w_ffn_up", up_slabs, theirs)]))

    mix_grads = {"w_conv_branch": d_w_cb, "w_att_branch": d_w_ab, "w_out": d_w_out}
    (d_conv_in, d_dw_w, d_dw_b, d_ln_g, d_ln_b), landed = _conv_bwd(
        conv_in, u1, du3, w["conv_ln_g"], w["conv_ln_b"], w["conv_dw_w"],
        _scatter_exchange([_grad_slabs(nm, mix_grads[nm]) for nm in mix_weights]))
    received.update(zip(mix_weights, landed))

    d_proj = [d_conv_in, dq, dk, dv, d_g_conv, d_g_att]
    d_w_in = _pieces_tn_matmul(d_proj, h1, name="d_w_in")
    in_slabs = _side_slabs("w_in", d_w_in)
    (theirs,) = _exchange_call("pair_swap_w_in", _pair_exchange([in_slabs]))

    early = list(opt)

    def pre_bwd_fn(*args):
        groups, (xt, dx2t), jobs, (w_in_t, g_) = args[:6], args[6:8], args[8:-2], args[-2:]
        dh = sum(jnp.dot(grp.astype(BF16), w_in_t[IN_SPLITS[n]:IN_SPLITS[n + 1]], preferred_element_type=F32)
                 for n, grp in enumerate(groups))
        _, vjp = jax.vjp(_rms, xt, g_)
        dx_, dg_ = vjp(dh)
        updates = [_sum_adamw_tile(*jobs[4 * n:4 * n + 4]) for n in range(len(early))]
        return (dx_ + dx2t, *[u for four in updates for u in four]), (dg_,)

    res = _rowwise(
        "proj_norm_bwd", pre_bwd_fn,
        d_proj + [x, dx2] + [a for nm in early for a in (received[nm], *opt[nm])], [w_in, g1],
        [_sds((s, D_MODEL))] + [_sds(opt[nm][0].shape) for nm in early for _ in range(4)],
        [_sds((1, D_MODEL))], tm=512, exchange=_chip_exchange([_pair_sum("pair_sum_w_in", in_slabs, theirs)]))
    grad_x, d_g1, received["w_in"] = res[0], res[-2], res[-1]
    updated = {nm: res[1 + 4 * n:5 + 4 * n] for n, nm in enumerate(early)}

    grads = {
        "norm_mix_pre": d_g1, "conv_dw_w": d_dw_w, "conv_dw_b": d_dw_b,
        "conv_ln_g": d_ln_g, "conv_ln_b": d_ln_b, "b_conv_branch": d_b_cb,
        "norm_mix_post": d_g2, "norm_ffn_pre": d_g3, "norm_ffn_post": d_g4,
    }
    return loss, grad_x, received, updated, grads


def _place():
    x, y, c = lax.axis_index("x"), lax.axis_index("y"), lax.axis_index("c")
    return x, y, c


def _slot(px, py, pc):
    return 4 * px + 2 * py + pc


def _exchange_scratch(n):
    return [pltpu.SemaphoreType.DMA((7 * n,)), pltpu.SemaphoreType.DMA((7 * n,)), pltpu.SemaphoreType.DMA((n,))]


GATHER_ID, SCATTER_ID, PAIR_ID, CHIP_ID = 0, 1, 2, 3


def _handshake(peers):
    barrier = pltpu.get_barrier_semaphore()
    for peer in peers:
        pl.semaphore_signal(barrier, inc=1, device_id=peer, device_id_type=MESH)
    pl.semaphore_wait(barrier, len(peers))


def _gather_exchange(arrs):
    n = len(arrs)

    def phases(ins, outs, send_sems, recv_sems, local_sems):
        x, y, c = _place()
        me, sibling = (x, y, c), (x, y, 1 - c)
        chips = [(1 - x, y), (x, 1 - y), (1 - x, 1 - y)]

        def copy(a, kk, block, to, src=None):
            dst = outs[a].at[_slot(*block)]
            return pltpu.make_async_remote_copy(
                src_ref=dst if src is None else src, dst_ref=dst,
                send_sem=send_sems.at[a * 7 + kk], recv_sem=recv_sems.at[a * 7 + kk],
                device_id=to, device_id_type=MESH)

        mine = [pltpu.make_async_copy(ins[a], outs[a].at[_slot(*me)], local_sems.at[a]) for a in range(n)]
        first = []
        for a in range(n):
            first.append(copy(a, 0, me, sibling, src=ins[a]))
            first += [copy(a, 1 + j, me, (*chip, c), src=ins[a]) for j, chip in enumerate(chips)]
        passed = [copy(a, 4 + j, (*chip, c), sibling) for j, chip in enumerate(chips) for a in range(n)]

        def send():
            _handshake([sibling] + [(*chip, c) for chip in chips])
            for cp in mine + first:
                cp.start()

        def pass_on():
            for j, chip in enumerate(chips):
                for a in range(n):
                    copy(a, 1 + j, (*chip, c), me).wait_recv()
                    passed[j * n + a].start()

        def finish():
            for a in range(n):
                copy(a, 0, sibling, me).wait_recv()
                for j, chip in enumerate(chips):
                    copy(a, 4 + j, (*chip, 1 - c), me).wait_recv()
            for cp in first + passed:
                cp.wait_send()
            for cp in mine:
                cp.wait()

        return [send, pass_on, finish]

    return list(arrs), [_sds((N_DEV,) + a.shape, a.dtype) for a in arrs], _exchange_scratch(n), phases, GATHER_ID


def _scatter_exchange(arrs):
    n = len(arrs)
    flips = [(fx, fy, fc) for fx in (0, 1) for fy in (0, 1) for fc in (0, 1)][1:]

    def phases(ins, outs, send_sems, recv_sems, local_sems):
        x, y, c = _place()
        mine = _slot(x, y, c)
        local = [pltpu.make_async_copy(ins[a].at[mine], outs[a].at[mine], local_sems.at[a]) for a in range(n)]
        peers = [((1 - x) if fx else x, (1 - y) if fy else y, (1 - c) if fc else c) for fx, fy, fc in flips]

        def copy(a, kk, src_slot, dst_slot):
            return pltpu.make_async_remote_copy(
                src_ref=ins[a].at[src_slot], dst_ref=outs[a].at[dst_slot],
                send_sem=send_sems.at[a * 7 + kk], recv_sem=recv_sems.at[a * 7 + kk],
                device_id=peers[kk], device_id_type=MESH)

        sends = [copy(a, kk, _slot(*peers[kk]), mine) for a in range(n) for kk in range(7)]

        def send():
            _handshake(peers)
            for cp in local + sends:
                cp.start()

        def finish():
            for a in range(n):
                for kk in range(7):
                    copy(a, kk, mine, _slot(*peers[kk])).wait_recv()
            for cp in sends:
                cp.wait_send()
            for cp in local:
                cp.wait()

        return [send, finish]

    return list(arrs), [_sds(a.shape, a.dtype) for a in arrs], _exchange_scratch(n), phases, SCATTER_ID


def _pair_exchange(arrs):
    n = len(arrs)

    def phases(ins, outs, send_sems, recv_sems, local_sems):
        x, y, c = _place()

        def copy(a, chip, side):
            return pltpu.make_async_remote_copy(
                src_ref=ins[a].at[chip, side], dst_ref=outs[a].at[chip],
                send_sem=send_sems.at[a * 7 + chip], recv_sem=recv_sems.at[a * 7 + chip],
                device_id=(x, y, 1 - c), device_id_type=MESH)

        sends = [copy(a, chip, 1 - c) for a in range(n) for chip in range(4)]

        def send():
            _handshake([(x, y, 1 - c)])
            for cp in sends:
                cp.start()

        def finish():
            for a in range(n):
                for chip in range(4):
                    copy(a, chip, c).wait_recv()
            for cp in sends:
                cp.wait_send()

        return [send, finish]

    return list(arrs), [_sds((4,) + a.shape[2:], a.dtype) for a in arrs], _exchange_scratch(n), phases, PAIR_ID


def _chip_exchange(arrs):
    n = len(arrs)

    def phases(ins, outs, send_sems, recv_sems, local_sems):
        x, y, c = _place()
        mine = 2 * x + y
        chips = [(1 - x, y), (x, 1 - y), (1 - x, 1 - y)]
        local = [pltpu.make_async_copy(ins[a].at[mine], outs[a].at[mine], local_sems.at[a]) for a in range(n)]

        def copy(a, j, src_slot, dst_slot):
            return pltpu.make_async_remote_copy(
                src_ref=ins[a].at[src_slot], dst_ref=outs[a].at[dst_slot],
                send_sem=send_sems.at[a * 7 + j], recv_sem=recv_sems.at[a * 7 + j],
                device_id=(*chips[j], c), device_id_type=MESH)

        sends = [copy(a, j, 2 * chips[j][0] + chips[j][1], mine) for a in range(n) for j in range(3)]

        def send():
            _handshake([(*chip, c) for chip in chips])
            for cp in local + sends:
                cp.start()

        def finish():
            for a in range(n):
                for j in range(3):
                    copy(a, j, mine, 2 * chips[j][0] + chips[j][1]).wait_recv()
            for cp in sends:
                cp.wait_send()
            for cp in local:
                cp.wait()

        return [send, finish]

    return list(arrs), [_sds(a.shape, a.dtype) for a in arrs], _exchange_scratch(n), phases, CHIP_ID


def _pair_sum(name, mine, theirs):
    _, _, r, c = mine.shape

    def body(side_ref, m_ref, t_ref, o_ref):
        o_ref[...] = (m_ref[...].astype(F32) + t_ref[...].astype(F32)).astype(o_ref.dtype)

    return pl.pallas_call(
        body, name=name,
        grid_spec=pltpu.PrefetchScalarGridSpec(
            num_scalar_prefetch=1, grid=(4,),
            in_specs=[pl.BlockSpec((None, None, r, c), lambda j, side: (j, side[0], 0, 0)),
                      pl.BlockSpec((None, r, c), lambda j, side: (j, 0, 0))],
            out_specs=pl.BlockSpec((None, r, c), lambda j, side: (j, 0, 0))),
        out_shape=_sds(theirs.shape, theirs.dtype),
        compiler_params=_params(("parallel",)),
    )(lax.axis_index("c").astype(jnp.int32).reshape(1), mine, theirs)


def _exchange_call(name, exchange):
    arrs, out_shape, scratch, phases, collective_id = exchange
    n = len(arrs)

    def body(*refs):
        for step in phases(refs[:n], refs[n:2 * n], *refs[2 * n:]):
            step()

    return pl.pallas_call(body, name=name, in_specs=[ANY] * n, out_specs=[ANY] * n,
                          out_shape=out_shape, scratch_shapes=scratch,
                          compiler_params=pltpu.CompilerParams(collective_id=collective_id))(*arrs)


def _carry_exchange(exchange, refs, n_in, n_out, first, middle, last):
    arrs, _, _, phases, _ = exchange
    n = len(arrs)
    if n == 0:
        return lambda: None
    ins = refs[n_in:n_in + n]
    outs = refs[n_in + n + n_out:n_in + 2 * n + n_out]
    sems = n_in + 2 * n + n_out
    steps = phases(ins, outs, *refs[sems:sems + 3])
    pl.when(first)(steps[0])
    if len(steps) == 3:
        pl.when(middle)(steps[1])
    return lambda: pl.when(last)(steps[-1])


def _adamw_math(w, g, m, v):
    m2 = ADAM_B1 * m + (1.0 - ADAM_B1) * g
    v2 = ADAM_B2 * v + (1.0 - ADAM_B2) * jnp.square(g)
    m_hat = m2 / (1.0 - ADAM_B1 ** ADAM_STEP)
    v_hat = v2 / (1.0 - ADAM_B2 ** ADAM_STEP)
    delta = -ADAM_LR * (m_hat / (jnp.sqrt(v_hat) + ADAM_EPS) + ADAM_WD * w)
    return delta, m2, v2


def _sum_adamw_tile(parts, w, m, v):
    g = parts[0].astype(F32)
    for d in range(1, parts.shape[0]):
        g = g + parts[d].astype(F32)
    return (g, *_adamw_math(w, g, m, v))


def _sum_adamw(name, parts, w, m, v, tr=256):
    p, r, c = parts.shape
    tr = _pick(r, tr, 16)

    def body(p_ref, w_ref, m_ref, v_ref, g_ref, d_ref, m2_ref, v2_ref):
        g_ref[...], d_ref[...], m2_ref[...], v2_ref[...] = _sum_adamw_tile(p_ref[...], w_ref[...], m_ref[...], v_ref[...])

    tile = pl.BlockSpec((tr, c), lambda i: (i, 0))
    return pl.pallas_call(
        body, name=name, grid=(r // tr,),
        in_specs=[pl.BlockSpec((p, tr, c), lambda i: (0, i, 0)), tile, tile, tile],
        out_specs=[tile] * 4, out_shape=[_sds((r, c))] * 4,
        compiler_params=_params(("parallel",)),
    )(parts, w, m, v)


def _sum_parts(name, parts):
    p, r, c = parts.shape

    def body(p_ref, o_ref):
        g = p_ref[0]
        for d in range(1, p):
            g = g + p_ref[d]
        o_ref[...] = g

    return pl.pallas_call(
        body, name=name, out_shape=_sds((r, c)),
        in_specs=[pl.BlockSpec(memory_space=pltpu.VMEM)], out_specs=pl.BlockSpec(memory_space=pltpu.VMEM),
    )(parts)


WEIGHTS = ["norm_mix_pre", "w_in", "conv_dw_w", "conv_dw_b", "conv_ln_g", "conv_ln_b", "w_conv_branch",
           "b_conv_branch", "w_att_branch", "w_out", "norm_mix_post", "norm_ffn_pre", "w_ffn_up", "w_ffn_down",
           "norm_ffn_post"]
COL_SHARDED = ["w_conv_branch", "w_att_branch"]
ROW_SHARDED = ["w_out", "w_ffn_down"]
TRANSPOSED = ["w_in", "w_ffn_up"]
VECTORS = ["norm_mix_pre", "conv_dw_b", "conv_ln_g", "conv_ln_b", "b_conv_branch", "norm_mix_post",
           "norm_ffn_pre", "norm_ffn_post"]


def _cols_to_full(g):
    return g.transpose(1, 0, 2).reshape(g.shape[1], N_DEV * g.shape[2])


def _full_to_cols(f):
    return f.reshape(f.shape[0], N_DEV, f.shape[1] // N_DEV).transpose(1, 0, 2)


PACK_ROWS = 7


def _pack_vectors(vecs, extra=None):
    parts = [vecs[nm].reshape(-1) for nm in VECTORS]
    parts.append(jnp.zeros((1,), F32) if extra is None else extra.reshape(1))
    used = sum(p.size for p in parts)
    parts.append(jnp.zeros((PACK_ROWS * D_MODEL - used,), F32))
    return jnp.concatenate(parts).reshape(PACK_ROWS, D_MODEL)


def _unpack_vectors(packed, sizes):
    flat, out, at = packed.reshape(-1), {}, 0
    for nm in VECTORS:
        out[nm] = flat[at:at + sizes[nm]]
        at += sizes[nm]
    return out, flat[at]


def kernel(x, norm_mix_pre, w_in, conv_dw_w, conv_dw_b, conv_ln_g, conv_ln_b, w_conv_branch, b_conv_branch, w_att_branch, w_out, norm_mix_post, norm_ffn_pre, w_ffn_up, w_ffn_down, norm_ffn_post, loss_target, m_norm_mix_pre, m_w_in, m_conv_dw_w, m_conv_dw_b, m_conv_ln_g, m_conv_ln_b, m_w_conv_branch, m_b_conv_branch, m_w_att_branch, m_w_out, m_norm_mix_post, m_norm_ffn_pre, m_w_ffn_up, m_w_ffn_down, m_norm_ffn_post, v_norm_mix_pre, v_w_in, v_conv_dw_w, v_conv_dw_b, v_conv_ln_g, v_conv_ln_b, v_w_conv_branch, v_b_conv_branch, v_w_att_branch, v_w_out, v_norm_mix_post, v_norm_ffn_pre, v_w_ffn_up, v_w_ffn_down, v_norm_ffn_post):
    ws = dict(zip(WEIGHTS, [norm_mix_pre, w_in, conv_dw_w, conv_dw_b, conv_ln_g, conv_ln_b, w_conv_branch,
                            b_conv_branch, w_att_branch, w_out, norm_mix_post, norm_ffn_pre, w_ffn_up, w_ffn_down,
                            norm_ffn_post]))
    ms = dict(zip(WEIGHTS, [m_norm_mix_pre, m_w_in, m_conv_dw_w, m_conv_dw_b, m_conv_ln_g, m_conv_ln_b,
                            m_w_conv_branch, m_b_conv_branch, m_w_att_branch, m_w_out, m_norm_mix_post,
                            m_norm_ffn_pre, m_w_ffn_up, m_w_ffn_down, m_norm_ffn_post]))
    vs = dict(zip(WEIGHTS, [v_norm_mix_pre, v_w_in, v_conv_dw_w, v_conv_dw_b, v_conv_ln_g, v_conv_ln_b,
                            v_w_conv_branch, v_b_conv_branch, v_w_att_branch, v_w_out, v_norm_mix_post,
                            v_norm_ffn_pre, v_w_ffn_up, v_w_ffn_down, v_norm_ffn_post]))

    dw_block = jnp.pad(conv_dw_w, ((0, 1), (0, 0)))
    g_in, g_dw = _exchange_call("gather_first", _gather_exchange([w_in.T.astype(BF16), dw_block]))
    full = {"w_in": _full_weight("w_in", g_in), "conv_dw_w": _cols_to_full(g_dw)}
    for nm in VECTORS:
        full[nm] = ws[nm].reshape(1, -1)

    def as_kept(nm, a):
        return a.T if nm in TRANSPOSED else a

    ride_along = ["w_ffn_up", "w_out"]
    loss_local, grad_x, received, updated, grads = _local_step(
        x[0], loss_target[0], full, {nm: as_kept(nm, ws[nm]).astype(BF16) for nm in LATE},
        {nm: tuple(as_kept(nm, a[nm]) for a in (ws, ms, vs)) for nm in ride_along})

    small = _exchange_call("gather_small_grads", _gather_exchange(
        [_pack_vectors(grads, extra=loss_local), grads["conv_dw_w"]]))
    out_g, out_d, out_m, out_v = {}, {}, {}, {}
    for nm in LATE + ["w_in"]:
        res = updated[nm] if nm in updated else _sum_adamw(
            "adamw_" + nm, received[nm], *[as_kept(nm, a[nm]) for a in (ws, ms, vs)])
        out_g[nm], out_d[nm], out_m[nm], out_v[nm] = [as_kept(nm, r) for r in res]
    sizes = {nm: ws[nm].size for nm in VECTORS}
    vec = _sum_adamw("adamw_vectors", small[0], _pack_vectors(ws), _pack_vectors(ms), _pack_vectors(vs))
    for res, dst in zip(vec, (out_g, out_d, out_m, out_v)):
        dst.update(_unpack_vectors(res, sizes)[0])
    loss = _unpack_vectors(vec[0], sizes)[1]
    dw_full = _sum_parts("sum_dw_grads", small[1])
    me = _slot(*_place())
    dw_mine = lax.dynamic_slice(dw_full, (0, me * (CONV_DIM // N_DEV)), (CONV_WIDTH, CONV_DIM // N_DEV))
    nm = "conv_dw_w"
    out_g[nm], out_d[nm], out_m[nm], out_v[nm] = _sum_adamw("adamw_dw", dw_mine[None], ws[nm], ms[nm], vs[nm])

    outs = [loss, grad_x[None]]
    for group in (out_g, out_d, out_m, out_v):
        outs += [group[nm] for nm in WEIGHTS]
    return tuple(outs)
```

```python
import math

import jax
import jax.numpy as jnp
from jax import lax
from jax.experimental import pallas as pl
from jax.experimental.pallas import tpu as pltpu

F32 = jnp.float32
BF16 = jnp.bfloat16

N_DEV = 8
D_MODEL = 1024
CONV_DIM = 512
CONV_WIDTH = 31
N_HEADS = 8
HEAD_DIM = 64
ATT_DIM = N_HEADS * HEAD_DIM
D_FF = 2816
EPS = 1e-6
IN_SPLITS = (0, 1024, 1536, 2048, 2560, 3584, 4608)

ADAM_LR = 0.001
ADAM_B1 = 0.9
ADAM_B2 = 0.999
ADAM_EPS = 1e-08
ADAM_WD = 0.01
ADAM_STEP = 10

LANES = 128
SUBLANES = 8
HALO = 32
ATT_TILE = 256
ATT_PART = 176
DEAD_SUM = -120.0
VMEM_LIMIT = 56 * 1024 * 1024
MESH = pl.DeviceIdType.MESH
ANY = pl.BlockSpec(memory_space=pl.ANY)


def _pick(dim, target, align=LANES):
    t = min(dim, target)
    t -= t % align
    while t >= align:
        if dim % t == 0:
            return t
        t -= align
    return dim


def _params(semantics, collective_id=None):
    return pltpu.CompilerParams(dimension_semantics=semantics, vmem_limit_bytes=VMEM_LIMIT,
                                collective_id=collective_id)


def _matmul(a, b, *, name, ta=False, tb=False, out_dtype=F32):
    m, k = (a.shape[1], a.shape[0]) if ta else a.shape
    n, k2 = b.shape if tb else (b.shape[1], b.shape[0])
    assert k == k2, (a.shape, b.shape, ta, tb)
    tm, tn, tk = _pick(m, 1408 if ta else 512), _pick(n, 1536), _pick(k, 1536)
    nk = k // tk
    dims = (((0 if ta else 1,), (1 if tb else 0,)), ((), ()))

    def body(a_ref, b_ref, o_ref, *acc):
        part = lax.dot_general(a_ref[...], b_ref[...], dims, preferred_element_type=F32)
        if nk == 1:
            o_ref[...] = part.astype(o_ref.dtype)
            return
        acc_ref, = acc
        kk = pl.program_id(2)

        @pl.when(kk == 0)
        def _():
            acc_ref[...] = part

        @pl.when((kk > 0) & (kk < nk - 1))
        def _():
            acc_ref[...] += part

        @pl.when(kk == nk - 1)
        def _():
            o_ref[...] = (acc_ref[...] + part).astype(o_ref.dtype)

    a_spec = pl.BlockSpec((tk, tm), lambda j, i, kk: (kk, i)) if ta else pl.BlockSpec((tm, tk), lambda j, i, kk: (i, kk))
    b_spec = (pl.BlockSpec((tn, tk), lambda j, i, kk: (j, kk)) if tb
              else pl.BlockSpec((tk, tn), lambda j, i, kk: (kk, j)))
    return pl.pallas_call(
        body, name=name, grid=(n // tn, m // tm, nk),
        in_specs=[a_spec, b_spec],
        out_specs=pl.BlockSpec((tm, tn), lambda j, i, kk: (i, j)),
        out_shape=jax.ShapeDtypeStruct((m, n), out_dtype),
        scratch_shapes=[pltpu.VMEM((tm, tn), F32)] if nk > 1 else [],
        compiler_params=_params(("parallel", "parallel", "arbitrary")),
    )(a, b)


def _pieces_tn_matmul(pieces, b, *, name, tj=512):
    s, n = b.shape
    counts = [p.shape[1] // tj for p in pieces]
    starts = [sum(counts[:i]) for i in range(len(pieces))]
    assert all(p.shape == (s, c * tj) for p, c in zip(pieces, counts))

    def body(*refs):
        b_ref, o_ref = refs[len(pieces):]
        j = pl.program_id(0)
        for p_ref, first, count in zip(refs, starts, counts):
            @pl.when((j >= first) & (j < first + count))
            def _():
                o_ref[...] = lax.dot_general(p_ref[...].astype(BF16), b_ref[...], TN,
                                             preferred_element_type=F32).astype(o_ref.dtype)

    def piece_spec(first, count):
        return pl.BlockSpec((s, tj), lambda j: (0, jnp.clip(j - first, 0, count - 1)))

    return pl.pallas_call(
        body, name=name, grid=(sum(counts),),
        in_specs=[piece_spec(f, c) for f, c in zip(starts, counts)]
        + [pl.BlockSpec((s, n), lambda j: (0, 0), pipeline_mode=pl.Buffered(1))],
        out_specs=pl.BlockSpec((tj, n), lambda j: (j, 0)),
        out_shape=jax.ShapeDtypeStruct((sum(counts) * tj, n), BF16),
        compiler_params=_params(("arbitrary",)),
    )(*pieces, b)


NO_EXCHANGE = ([], [], [], None, None)


def _sweep_marks(nt):
    i = pl.program_id(0)
    return i == 0, i == (3 * nt) // 4, i == nt - 1


def _rowwise(name, fn, rows, bcasts, row_outs, red_outs=(), tm=256, exchange=NO_EXCHANGE):
    s = rows[0].shape[0]
    tm = _pick(s, tm, 16)
    nt = s // tm
    resident = pl.Buffered(1)
    nr, nb, no, nd = len(rows), len(bcasts), len(row_outs), len(red_outs)
    x_arrs, x_shape, x_scratch, _, x_id = exchange
    nx = len(x_arrs)
    first_out = nr + nb + nx

    def body(*refs):
        finish_exchange = _carry_exchange(exchange, refs, nr + nb, no + nd, *_sweep_marks(nt))
        ins = [r[...] for r in refs[:nr + nb]]
        outs, reds = fn(*ins)
        for ref, val in zip(refs[first_out:first_out + no], outs):
            ref[...] = val.astype(ref.dtype)
        i = pl.program_id(0)
        for ref, val in zip(refs[first_out + no:first_out + no + nd], reds):
            @pl.when(i == 0)
            def _():
                ref[...] = val

            @pl.when(i > 0)
            def _():
                ref[...] += val
        finish_exchange()

    def row_spec(a):
        assert a.shape[-2] % nt == 0, (name, a.shape, nt)
        if len(a.shape) == 3:
            return pl.BlockSpec((a.shape[0], a.shape[1] // nt, a.shape[2]), lambda i: (0, i, 0))
        return pl.BlockSpec((a.shape[0] // nt, a.shape[1]), lambda i: (i, 0))

    in_specs = [row_spec(r) for r in rows]
    in_specs += [pl.BlockSpec(b.shape, lambda i: (0, 0), pipeline_mode=resident) for b in bcasts]
    out_specs = [row_spec(o) for o in row_outs]
    out_specs += [pl.BlockSpec(d.shape, lambda i: (0, 0)) for d in red_outs]
    return pl.pallas_call(
        body, name=name, grid=(nt,), in_specs=in_specs + [ANY] * nx, out_specs=out_specs + [ANY] * nx,
        out_shape=list(row_outs) + list(red_outs) + x_shape, scratch_shapes=x_scratch,
        compiler_params=_params(("arbitrary",), x_id),
    )(*rows, *bcasts, *x_arrs)


def _sds(shape, dtype=F32):
    return jax.ShapeDtypeStruct(shape, dtype)


def _rms(x, g):
    y = x * lax.rsqrt(jnp.mean(x * x, axis=-1, keepdims=True) + EPS)
    return y * g


def _silu(x):
    return x * jax.nn.sigmoid(x)


def _swiglu(g, u):
    return _silu(g) * u


def _ln_silu(u, g, b):
    mu = jnp.mean(u, axis=-1, keepdims=True)
    var = jnp.mean(jnp.square(u - mu), axis=-1, keepdims=True)
    return _silu((u - mu) * lax.rsqrt(var + EPS) * g + b)


def _merge(conv_pre, att_out, g_conv, g_att, b_cb):
    return jax.nn.sigmoid(g_conv) * (conv_pre + b_cb) + jax.nn.sigmoid(g_att) * att_out


def _glu(t):
    return t[:, :CONV_DIM] * jax.nn.sigmoid(t[:, CONV_DIM:])


def _shifted_reader(buf, shifted, tm):
    for b in range(1, SUBLANES):
        shifted[b - 1, :, :] = buf[pl.ds(b, tm + HALO - SUBLANES), :]

    def read(o):
        a, b = divmod(o, SUBLANES)
        return buf[pl.ds(SUBLANES * a, tm), :] if b == 0 else shifted[b - 1, pl.ds(SUBLANES * a, tm), :]

    return read


def _conv_fwd(conv_in, w_pad, b, ln_g, ln_b, exchange, tm=256):
    s = conv_in.shape[0]
    tm = _pick(s, tm, HALO)
    ratio = tm // HALO
    x_arrs, x_shape, x_scratch, _, x_id = exchange
    nx = len(x_arrs)

    def body(*refs):
        main_ref, halo_ref, w_ref, b_ref, g_ref, be_ref = refs[:6]
        u3_ref, u1_ref = refs[6 + nx:8 + nx]
        buf, shifted = refs[-2:]
        finish_exchange = _carry_exchange(exchange, refs, 6, 2, *_sweep_marks(s // tm))
        i = pl.program_id(0)
        buf[0:HALO, :] = _glu(halo_ref[...]) * (i > 0).astype(F32)
        buf[HALO:HALO + tm, :] = _glu(main_ref[...])
        read = _shifted_reader(buf, shifted, tm)
        acc = jnp.zeros((tm, CONV_DIM), F32) + b_ref[...]
        for j in range(CONV_WIDTH):
            acc = acc + w_ref[j:j + 1, :] * read(HALO - (CONV_WIDTH - 1) + j)
        u1_ref[...] = acc
        u3_ref[...] = _ln_silu(acc, g_ref[...], be_ref[...]).astype(u3_ref.dtype)
        finish_exchange()

    res = pl.pallas_call(
        body, name="conv_fwd", grid=(s // tm,),
        in_specs=[pl.BlockSpec((tm, 2 * CONV_DIM), lambda i: (i, 0)),
                  pl.BlockSpec((HALO, 2 * CONV_DIM), lambda i: (jnp.maximum(i * ratio - 1, 0), 0)),
                  pl.BlockSpec(w_pad.shape, lambda i: (0, 0)),
                  pl.BlockSpec(b.shape, lambda i: (0, 0)),
                  pl.BlockSpec(ln_g.shape, lambda i: (0, 0)),
                  pl.BlockSpec(ln_b.shape, lambda i: (0, 0))] + [ANY] * nx,
        out_specs=[pl.BlockSpec((tm, CONV_DIM), lambda i: (i, 0)),
                   pl.BlockSpec((tm, CONV_DIM), lambda i: (i, 0))] + [ANY] * nx,
        out_shape=[_sds((s, CONV_DIM), BF16), _sds((s, CONV_DIM), F32)] + x_shape,
        scratch_shapes=x_scratch + [pltpu.VMEM((tm + HALO, CONV_DIM), F32),
                                    pltpu.VMEM((SUBLANES - 1, tm + HALO - SUBLANES, CONV_DIM), F32)],
        compiler_params=_params(("arbitrary",), x_id),
    )(conv_in, conv_in, w_pad, b, ln_g, ln_b, *x_arrs)
    return res[0], res[1], res[2:]


def _conv_bwd(conv_in, u1, du3, ln_g, ln_b, w_pad, exchange, tm=256):
    s = conv_in.shape[0]
    tm = _pick(s, tm, HALO)
    ratio = tm // HALO
    nt = s // tm
    last_halo = s // HALO - 1
    x_arrs, x_shape, x_scratch, _, x_id = exchange
    nx = len(x_arrs)

    def body(*refs):
        main_ref, halo_ref, u1_ref, u1n_ref, du3_ref, du3n_ref, g_ref, be_ref, w_ref = refs[:9]
        dci_ref, dw_ref, db_ref, dg_ref, dbe_ref = refs[9 + nx:14 + nx]
        ubuf, dbuf, ushift, dshift = refs[-4:]
        finish_exchange = _carry_exchange(exchange, refs, 9, 5, *_sweep_marks(nt))
        i = pl.program_id(0)
        main = main_ref[...]
        a = main[:, :CONV_DIM]
        sb = jax.nn.sigmoid(main[:, CONV_DIM:])
        ubuf[0:HALO, :] = _glu(halo_ref[...]) * (i > 0).astype(F32)
        ubuf[HALO:HALO + tm, :] = a * sb

        def ln_bwd(u1t, du3t):
            _, vjp = jax.vjp(_ln_silu, u1t, g_ref[...], be_ref[...])
            return vjp(du3t)

        du, dg, dbe = ln_bwd(u1_ref[...], du3_ref[...])
        dbuf[0:tm, :] = du
        dbuf[tm:tm + HALO, :] = ln_bwd(u1n_ref[...], du3n_ref[...])[0] * (i < nt - 1).astype(F32)

        @pl.when(i == 0)
        def _():
            dw_ref[...] = jnp.zeros_like(dw_ref)
            db_ref[...] = jnp.zeros_like(db_ref)
            dg_ref[...] = jnp.zeros_like(dg_ref)
            dbe_ref[...] = jnp.zeros_like(dbe_ref)

        dg_ref[...] += dg
        dbe_ref[...] += dbe

        read_u = _shifted_reader(ubuf, ushift, tm)
        read_d = _shifted_reader(dbuf, dshift, tm)
        du0 = jnp.zeros((tm, CONV_DIM), F32)
        for j in range(CONV_WIDTH):
            du0 = du0 + w_ref[j:j + 1, :] * read_d(CONV_WIDTH - 1 - j)
            dw_ref[j:j + 1, :] += jnp.sum(du * read_u(HALO - (CONV_WIDTH - 1) + j), axis=0, keepdims=True)
        db_ref[...] += jnp.sum(du, axis=0, keepdims=True)
        dci_ref[:, :CONV_DIM] = (du0 * sb).astype(dci_ref.dtype)
        dci_ref[:, CONV_DIM:] = (du0 * a * sb * (1.0 - sb)).astype(dci_ref.dtype)
        finish_exchange()

    res = pl.pallas_call(
        body, name="conv_bwd", grid=(nt,),
        in_specs=[pl.BlockSpec((tm, 2 * CONV_DIM), lambda i: (i, 0)),
                  pl.BlockSpec((HALO, 2 * CONV_DIM), lambda i: (jnp.maximum(i * ratio - 1, 0), 0))]
        + [pl.BlockSpec((tm, CONV_DIM), lambda i: (i, 0)),
           pl.BlockSpec((HALO, CONV_DIM), lambda i: (jnp.minimum((i + 1) * ratio, last_halo), 0))] * 2
        + [pl.BlockSpec((1, CONV_DIM), lambda i: (0, 0))] * 2 + [pl.BlockSpec(w_pad.shape, lambda i: (0, 0))]
        + [ANY] * nx,
        out_specs=[pl.BlockSpec((tm, 2 * CONV_DIM), lambda i: (i, 0)),
                   pl.BlockSpec(w_pad.shape, lambda i: (0, 0))]
        + [pl.BlockSpec((1, CONV_DIM), lambda i: (0, 0))] * 3 + [ANY] * nx,
        out_shape=[_sds((s, 2 * CONV_DIM), BF16), _sds(w_pad.shape)] + [_sds((1, CONV_DIM))] * 3 + x_shape,
        scratch_shapes=x_scratch + [pltpu.VMEM((tm + HALO, CONV_DIM), F32)] * 2
        + [pltpu.VMEM((SUBLANES - 1, tm + HALO - SUBLANES, CONV_DIM), F32)] * 2,
        compiler_params=_params(("arbitrary",), x_id),
    )(conv_in, conv_in, u1, u1, du3, du3, ln_g, ln_b, w_pad, *x_arrs)
    return res[:5], res[5:]


def _logsig_neg(z):
    return jnp.minimum(-z, 0.0) - jnp.log(1.0 + jnp.exp(-jnp.abs(z)))


def _split_dot(val, tri):
    hi = val.astype(BF16)
    lo = (val - hi.astype(F32)).astype(BF16)
    return jnp.dot(hi, tri, preferred_element_type=F32) + jnp.dot(lo, tri, preferred_element_type=F32)


def _attn_masks(t, later):
    row = lax.broadcasted_iota(jnp.int32, (t, t), 0)
    col = lax.broadcasted_iota(jnp.int32, (t, t), 1)
    tri = jnp.where(row > col if later else row <= col, 1.0, 0.0).astype(BF16)
    return col < row, tri


def _grid_marks(h, nq):
    hh, i = pl.program_id(0), pl.program_id(1)
    return (hh == 0) & (i == 0), (hh == h - 1) & (i == nq // 2), (hh == h - 1) & (i == nq - 1)


def _head_masks(shape):
    lane = lax.broadcasted_iota(jnp.int32, shape, len(shape) - 1)
    return lane < HEAD_DIM, lane >= HEAD_DIM


def _per_head(blk):
    m0, m1 = _head_masks(blk.shape)
    zero = jnp.zeros_like(blk)
    return jnp.where(m0, blk, zero), jnp.where(m1, blk, zero)


NT = (((1,), (1,)), ((), ()))
TN = (((0,), (0,)), ((), ()))


def _with_top(whole, top):
    rows = top.shape[0]
    return top if rows == whole.shape[0] else jnp.concatenate([top, whole[rows:]], axis=0)


def _attn_fwd(q, k, v, exchange):
    s = q.shape[0]
    hp = q.shape[1] // LANES
    t = ATT_TILE
    scale = 1.0 / math.sqrt(HEAD_DIM)
    x_arrs, x_shape, x_scratch, _, x_id = exchange
    nx = len(x_arrs)

    def body(*refs):
        q_ref, k_ref, v_ref = refs[:3]
        o_ref, lt_ref, nb_ref = refs[3 + nx:6 + nx]
        finish_exchange = _carry_exchange(exchange, refs, 3, 3, *_grid_marks(hp, s // t))
        i = pl.program_id(1)
        qs = _per_head((q_ref[...].astype(F32) * scale).astype(BF16))
        causal, tri = _attn_masks(t, later=True)

        def step(kb, carry, masked, rows):
            cs, acc = carry
            off = pl.multiple_of(kb * t, t)
            kblk = k_ref[pl.ds(off, t), :]
            vs = _per_head(v_ref[pl.ds(off, t), :])
            acc_top = acc[:rows]
            new_cs = []
            for hd in range(2):
                z = lax.dot_general(qs[hd][:rows], kblk, NT, preferred_element_type=F32)
                l = _logsig_neg(z)
                if masked:
                    l = jnp.where(causal, l, 0.0)
                e = z + l + _split_dot(l, tri) + cs[hd][:rows]
                if masked:
                    e = jnp.where(causal, e, -1e30)
                acc_top = acc_top + jnp.dot(jnp.exp(e).astype(BF16), vs[hd], preferred_element_type=F32)
                new_cs.append(_with_top(cs[hd], cs[hd][:rows] + jnp.sum(l, axis=1, keepdims=True)))
            return tuple(new_cs), _with_top(acc, acc_top)

        zero = jnp.zeros((t, 1), F32)
        carry = step(i, ((zero, zero), jnp.zeros((t, LANES), F32)), True, t)

        def live(cs, lo, hi):
            return jnp.maximum(jnp.max(cs[0][lo:hi]), jnp.max(cs[1][lo:hi])) > DEAD_SUM

        def more(state):
            n, _, (cs, _) = state
            return (n < i) & live(cs, 0, t)

        def sweep(state):
            n, n_full, cr = state
            whole = live(cr[0], ATT_PART, t)
            cr = lax.cond(whole, lambda c: step(i - 1 - n, c, False, t), lambda c: step(i - 1 - n, c, False, ATT_PART), cr)
            return n + 1, n_full + whole.astype(jnp.int32), cr

        n_blocks, n_full, carry = lax.while_loop(more, sweep, (jnp.int32(0), jnp.int32(0), carry))
        m0, _ = _head_masks((t, LANES))
        lt_ref[...] = jnp.where(m0, carry[0][0], carry[0][1])
        o_ref[...] = carry[1].astype(o_ref.dtype)
        nb_ref[0, pl.program_id(0), i] = n_blocks.astype(F32)
        nb_ref[1, pl.program_id(0), i] = n_full.astype(F32)
        finish_exchange()

    res = pl.pallas_call(
        body, name="attn_fwd", grid=(hp, s // t),
        in_specs=[pl.BlockSpec((t, LANES), lambda p, i: (i, p)),
                  pl.BlockSpec((s, LANES), lambda p, i: (0, p)),
                  pl.BlockSpec((s, LANES), lambda p, i: (0, p))] + [ANY] * nx,
        out_specs=[pl.BlockSpec((t, LANES), lambda p, i: (i, p)),
                   pl.BlockSpec((None, t, LANES), lambda p, i: (p, i, 0)),
                   pl.BlockSpec(memory_space=pltpu.SMEM)] + [ANY] * nx,
        out_shape=[_sds(q.shape, BF16), _sds((hp, s, LANES), F32), _sds((2, hp, s // t), F32)] + x_shape,
        scratch_shapes=x_scratch,
        compiler_params=_params(("arbitrary", "arbitrary"), x_id),
    )(q, k, v, *x_arrs)
    return res[0], res[1], res[2], res[3:]


def _attn_bwd(q, k, v, do, ltot, n_blocks, exchange):
    s = q.shape[0]
    hp = q.shape[1] // LANES
    t = ATT_TILE
    scale = 1.0 / math.sqrt(HEAD_DIM)
    x_arrs, x_shape, x_scratch, _, x_id = exchange
    nx = len(x_arrs)

    def body(*refs):
        q_ref, k_ref, v_ref, do_ref, lt_ref, nb_ref = refs[:6]
        dq_ref, dk_ref, dv_ref = refs[6 + nx:9 + nx]
        finish_exchange = _carry_exchange(exchange, refs, 6, 3, *_grid_marks(hp, s // t))
        i = pl.program_id(1)
        n_blocks = jnp.clip(nb_ref[0, pl.program_id(0), i].astype(jnp.int32), 0, i)
        n_full = jnp.clip(nb_ref[1, pl.program_id(0), i].astype(jnp.int32), 0, n_blocks)

        @pl.when(i == 0)
        def _():
            dk_ref[...] = jnp.zeros_like(dk_ref)
            dv_ref[...] = jnp.zeros_like(dv_ref)

        qb = q_ref[...]
        qm = _per_head(qb)
        qs = _per_head((qb.astype(F32) * scale).astype(BF16))
        dos = _per_head(do_ref[...])
        lts = (lt_ref[:, 0:1], lt_ref[:, HEAD_DIM:HEAD_DIM + 1])
        causal, tri = _attn_masks(t, later=False)

        def step(kb, carry, masked, rows):
            cls, cgs, dq = carry
            off = pl.multiple_of(kb * t, t)
            kblk = k_ref[pl.ds(off, t), :]
            vblk = v_ref[pl.ds(off, t), :]
            ks = _per_head(kblk)
            dq_top = dq[:rows]
            dk = jnp.zeros((t, LANES), F32)
            dv = jnp.zeros((t, LANES), F32)
            new_cls, new_cgs = [], []
            for hd in range(2):
                z = lax.dot_general(qs[hd][:rows], kblk, NT, preferred_element_type=F32)
                l = _logsig_neg(z)
                if masked:
                    l = jnp.where(causal, l, 0.0)
                e = z + l + ((lts[hd][:rows] - cls[hd][:rows]) - _split_dot(l, tri))
                if masked:
                    e = jnp.where(causal, e, -1e30)
                a = jnp.exp(e)
                g = lax.dot_general(dos[hd][:rows], vblk, NT, preferred_element_type=F32) * a
                p = cgs[hd][:rows] + jnp.dot(g.astype(BF16), tri, preferred_element_type=F32) - g
                el = jnp.exp(l)
                dz = g * el - p * (1.0 - el)
                if masked:
                    dz = jnp.where(causal, dz, 0.0)
                dzb = (dz * scale).astype(BF16)
                dq_top = dq_top + jnp.dot(dzb, ks[hd], preferred_element_type=F32)
                dk = dk + lax.dot_general(dzb, qm[hd][:rows], TN, preferred_element_type=F32)
                dv = dv + lax.dot_general(a.astype(BF16), dos[hd][:rows], TN, preferred_element_type=F32)
                new_cls.append(_with_top(cls[hd], cls[hd][:rows] + jnp.sum(l, axis=1, keepdims=True)))
                new_cgs.append(_with_top(cgs[hd], cgs[hd][:rows] + jnp.sum(g, axis=1, keepdims=True)))
            dk_ref[pl.ds(off, t), :] += dk
            dv_ref[pl.ds(off, t), :] += dv
            return tuple(new_cls), tuple(new_cgs), _with_top(dq, dq_top)

        zero = jnp.zeros((t, 1), F32)
        init = ((zero, zero), (zero, zero), jnp.zeros((t, LANES), F32))
        carry = lax.fori_loop(i - n_blocks, i - n_full, lambda kb, cr: step(kb, cr, False, ATT_PART), init)
        carry = lax.fori_loop(i - n_full, i, lambda kb, cr: step(kb, cr, False, t), carry)
        carry = step(i, carry, True, t)
        dq_ref[...] = carry[2]
        finish_exchange()

    blk = pl.BlockSpec((t, LANES), lambda p, i: (i, p))
    whole = pl.BlockSpec((s, LANES), lambda p, i: (0, p))
    res = pl.pallas_call(
        body, name="attn_bwd", grid=(hp, s // t),
        in_specs=[blk, whole, whole, blk, pl.BlockSpec((None, t, LANES), lambda p, i: (p, i, 0)),
                  pl.BlockSpec(memory_space=pltpu.SMEM)] + [ANY] * nx,
        out_specs=[blk, whole, whole] + [ANY] * nx,
        out_shape=[_sds(q.shape)] * 3 + x_shape,
        scratch_shapes=x_scratch,
        compiler_params=_params(("arbitrary", "arbitrary"), x_id),
    )(q, k, v, do, ltot, n_blocks, *x_arrs)
    return res[0], res[1], res[2], res[3:]


LATE = ["w_conv_branch", "w_att_branch", "w_out", "w_ffn_up", "w_ffn_down"]


def _full_weight(name, gathered):
    return _cols_to_full(gathered) if name in COL_SHARDED else gathered.reshape(-1, gathered.shape[2])


def _grad_slabs(name, grad):
    return _full_to_cols(grad) if name in COL_SHARDED else grad.reshape(N_DEV, -1, grad.shape[1])


def _side_slabs(name, grad):
    slabs = _grad_slabs(name, grad)
    return slabs.reshape((4, 2) + slabs.shape[1:])


def _local_step(x, target, w, late_blocks, opt):
    s = x.shape[0]
    w = dict(w)
    g1, g2, g3, g4 = w["norm_mix_pre"], w["norm_mix_post"], w["norm_ffn_pre"], w["norm_ffn_post"]

    w_in = w["w_in"]

    def proj_fn(xt, g1_, w_in_t):
        h = _rms(xt, g1_).astype(BF16)
        proj = lax.dot_general(h, w_in_t, NT, preferred_element_type=F32)
        return (h, *[proj[:, IN_SPLITS[n]:IN_SPLITS[n + 1]] for n in range(6)]), ()

    mix_weights = ["w_conv_branch", "w_att_branch", "w_out"]
    h1, conv_in, q, k, v, g_conv, g_att = _rowwise(
        "norm_proj", proj_fn, [x], [g1, w_in],
        [_sds((s, D_MODEL), BF16), _sds((s, 2 * CONV_DIM)), _sds((s, ATT_DIM), BF16), _sds((s, ATT_DIM), BF16),
         _sds((s, ATT_DIM), BF16), _sds((s, D_MODEL), BF16), _sds((s, D_MODEL), BF16)], tm=512)

    u3, u1, gathered = _conv_fwd(conv_in, w["conv_dw_w"], w["conv_dw_b"], w["conv_ln_g"], w["conv_ln_b"],
                                 _gather_exchange([late_blocks[nm] for nm in mix_weights]))
    for nm, g in zip(mix_weights, gathered):
        w[nm] = _full_weight(nm, g)
    att, ltot, n_blocks, (g_up,) = _attn_fwd(q, k, v, _gather_exchange([late_blocks["w_ffn_up"]]))
    w["w_ffn_up"] = _full_weight("w_ffn_up", g_up)

    def merge_fn(u3t, at, gc, ga, xt, w_cb, w_ab, b_cb, w_out, g2_, g3_):
        cp = jnp.dot(u3t, w_cb, preferred_element_type=F32)
        ao = jnp.dot(at, w_ab, preferred_element_type=F32)
        mg = _merge(cp, ao, gc.astype(F32), ga.astype(F32), b_cb).astype(BF16)
        mix_ = jnp.dot(mg, w_out, preferred_element_type=F32)
        x2_ = xt + _rms(mix_, g2_)
        return (mg, cp, ao, mix_, x2_, _rms(x2_, g3_)), ()

    merged, conv_pre, att_out, mix, x2, h2 = _rowwise(
        "branch_merge_mix", merge_fn, [u3, att, g_conv, g_att, x],
        [w["w_conv_branch"], w["w_att_branch"], w["b_conv_branch"], w["w_out"], g2, g3],
        [_sds((s, D_MODEL), BF16)] * 3 + [_sds((s, D_MODEL)), _sds((s, D_MODEL)), _sds((s, D_MODEL), BF16)], tm=512)

    def ffn_up_fn(ht, w_up_t):
        gu_ = lax.dot_general(ht, w_up_t, NT, preferred_element_type=F32)
        return (gu_, _swiglu(gu_[:, :D_FF], gu_[:, D_FF:])), ()

    gu, act, g_down = _rowwise("ffn_up", ffn_up_fn, [h2], [w["w_ffn_up"]],
                               [_sds((s, 2 * D_FF), BF16), _sds((s, D_FF), BF16)], tm=512,
                               exchange=_gather_exchange([late_blocks["w_ffn_down"]]))
    w["w_ffn_down"] = _full_weight("w_ffn_down", g_down)

    def final_fn(at, x2t, tgt, w_down, g4_):
        ff = jnp.dot(at, w_down, preferred_element_type=F32)
        n4, vjp = jax.vjp(_rms, ff, g4_)
        err = x2t + n4 - tgt
        dy = err * (1.0 / D_MODEL)
        dff, dg4 = vjp(dy)
        return (dy, dff), (jnp.sum(err * err, axis=0, keepdims=True), dg4)

    dy, dff, loss_cols, d_g4 = _rowwise("ffn_down_loss", final_fn, [act, x2, target], [w["w_ffn_down"], g4],
                                        [_sds((s, D_MODEL)), _sds((s, D_MODEL), BF16)],
                                        [_sds((1, D_MODEL)), _sds((1, D_MODEL))], tm=512)
    loss = 0.5 * jnp.sum(loss_cols) / D_MODEL

    d_w_down = _matmul(act, dff, ta=True, name="d_w_down", out_dtype=BF16)

    def act_bwd_fn(dfft, gut, w_down):
        d_act = lax.dot_general(dfft, w_down, NT, preferred_element_type=F32)
        gu_ = gut.astype(F32)
        _, vjp = jax.vjp(_swiglu, gu_[:, :D_FF], gu_[:, D_FF:])
        return (jnp.concatenate(vjp(d_act), axis=1),), ()

    down_slabs = _side_slabs("w_ffn_down", d_w_down)
    dgu, theirs = _rowwise("ffn_act_bwd", act_bwd_fn, [dff, gu], [w["w_ffn_down"]], [_sds((s, 2 * D_FF), BF16)],
                           exchange=_pair_exchange([down_slabs]))
    down_sums = _pair_sum("pair_sum_w_ffn_down", down_slabs, theirs)
    d_w_up = _matmul(dgu, h2, ta=True, name="d_w_up", out_dtype=BF16)
    received = {}
    up_slabs = _side_slabs("w_ffn_up", d_w_up)

    def mid_bwd_fn(dgut, xt, mt, dyt, w_up_t, g2_, g3_):
        dh = jnp.dot(dgut, w_up_t, preferred_element_type=F32)
        n2, vjp2 = jax.vjp(_rms, mt, g2_)
        x2_ = xt + n2
        _, vjp3 = jax.vjp(_rms, x2_, g3_)
        dx2_, dg3 = vjp3(dh)
        dx2_ = dx2_ + dyt
        dmix_, dg2 = vjp2(dx2_)
        return (dx2_, dmix_), (dg2, dg3)

    dx2, dmix, d_g2, d_g3, received["w_ffn_down"] = _rowwise(
        "ffn_up_mid_bwd", mid_bwd_fn, [dgu, x, mix, dy], [w["w_ffn_up"], g2, g3],
        [_sds((s, D_MODEL)), _sds((s, D_MODEL), BF16)], [_sds((1, D_MODEL)), _sds((1, D_MODEL))], tm=512,
        exchange=_chip_exchange([down_sums]))
    d_w_out = _matmul(merged, dmix, ta=True, name="d_w_out", out_dtype=BF16)

    def merge_bwd_fn(dmt, cp, ao, gc, ga, w_out, w_cb, w_ab, b_cb):
        dm = lax.dot_general(dmt, w_out, NT, preferred_element_type=F32)
        _, vjp = jax.vjp(_merge, cp.astype(F32), ao.astype(F32), gc.astype(F32), ga.astype(F32), b_cb)
        dcp, dao, dgc, dga, dbias = vjp(dm)
        dcp, dao = dcp.astype(BF16), dao.astype(BF16)
        du3_ = lax.dot_general(dcp, w_cb, NT, preferred_element_type=F32)
        datt_ = lax.dot_general(dao, w_ab, NT, preferred_element_type=F32)
        return (dcp, dao, dgc, dga, du3_, datt_), (dbias,)

    d_conv_out, d_att_out, d_g_conv, d_g_att, du3, d_att, d_b_cb, theirs = _rowwise(
        "merge_bwd", merge_bwd_fn, [dmix, conv_pre, att_out, g_conv, g_att],
        [w["w_out"], w["w_conv_branch"], w["w_att_branch"], w["b_conv_branch"]],
        [_sds((s, D_MODEL), BF16)] * 4 + [_sds((s, CONV_DIM)), _sds((s, ATT_DIM), BF16)], [_sds((1, D_MODEL))], tm=512,
        exchange=_pair_exchange([up_slabs]))

    d_w_cb = _matmul(u3, d_conv_out, ta=True, name="d_w_conv_branch", out_dtype=BF16)
    d_w_ab = _matmul(att, d_att_out, ta=True, name="d_w_att_branch", out_dtype=BF16)

    dq, dk, dv, (received["w_ffn_up"],) = _attn_bwd(
        q, k, v, d_att, ltot, n_blocks, _chip_exchange([_pair_sum("pair_sum_w_ffn_up", up_slabs, theirs)]))

    mix_grads = {"w_conv_branch": d_w_cb, "w_att_branch": d_w_ab, "w_out": d_w_out}
    (d_conv_in, d_dw_w, d_dw_b, d_ln_g, d_ln_b), landed = _conv_bwd(
        conv_in, u1, du3, w["conv_ln_g"], w["conv_ln_b"], w["conv_dw_w"],
        _scatter_exchange([_grad_slabs(nm, mix_grads[nm]) for nm in mix_weights]))
    received.update(zip(mix_weights, landed))

    d_proj = [d_conv_in, dq, dk, dv, d_g_conv, d_g_att]
    d_w_in = _pieces_tn_matmul(d_proj, h1, name="d_w_in")
    in_slabs = _side_slabs("w_in", d_w_in)
    (theirs,) = _exchange_call("pair_swap_w_in", _pair_exchange([in_slabs]))

    early = list(opt)

    def pre_bwd_fn(*args):
        groups, (xt, dx2t), jobs, (w_in_t, g_) = args[:6], args[6:8], args[8:-2], args[-2:]
        dh = sum(jnp.dot(grp.astype(BF16), w_in_t[IN_SPLITS[n]:IN_SPLITS[n + 1]], preferred_element_type=F32)
                 for n, grp in enumerate(groups))
        _, vjp = jax.vjp(_rms, xt, g_)
        dx_, dg_ = vjp(dh)
        updates = [_sum_adamw_tile(*jobs[4 * n:4 * n + 4]) for n in range(len(early))]
        return (dx_ + dx2t, *[u for four in updates for u in four]), (dg_,)

    res = _rowwise(
        "proj_norm_bwd", pre_bwd_fn,
        d_proj + [x, dx2] + [a for nm in early for a in (received[nm], *opt[nm])], [w_in, g1],
        [_sds((s, D_MODEL))] + [_sds(opt[nm][0].shape) for nm in early for _ in range(4)],
        [_sds((1, D_MODEL))], tm=512, exchange=_chip_exchange([_pair_sum("pair_sum_w_in", in_slabs, theirs)]))
    grad_x, d_g1, received["w_in"] = res[0], res[-2], res[-1]
    updated = {nm: res[1 + 4 * n:5 + 4 * n] for n, nm in enumerate(early)}

    grads = {
        "norm_mix_pre": d_g1, "conv_dw_w": d_dw_w, "conv_dw_b": d_dw_b,
        "conv_ln_g": d_ln_g, "conv_ln_b": d_ln_b, "b_conv_branch": d_b_cb,
        "norm_mix_post": d_g2, "norm_ffn_pre": d_g3, "norm_ffn_post": d_g4,
    }
    return loss, grad_x, received, updated, grads


def _place():
    x, y, c = lax.axis_index("x"), lax.axis_index("y"), lax.axis_index("c")
    return x, y, c


def _slot(px, py, pc):
    return 4 * px + 2 * py + pc


def _exchange_scratch(n):
    return [pltpu.SemaphoreType.DMA((7 * n,)), pltpu.SemaphoreType.DMA((7 * n,)), pltpu.SemaphoreType.DMA((n,))]


GATHER_ID, SCATTER_ID, PAIR_ID, CHIP_ID = 0, 1, 2, 3


def _handshake(peers):
    barrier = pltpu.get_barrier_semaphore()
    for peer in peers:
        pl.semaphore_signal(barrier, inc=1, device_id=peer, device_id_type=MESH)
    pl.semaphore_wait(barrier, len(peers))


def _gather_exchange(arrs):
    n = len(arrs)

    def phases(ins, outs, send_sems, recv_sems, local_sems):
        x, y, c = _place()
        me, sibling = (x, y, c), (x, y, 1 - c)
        chips = [(1 - x, y), (x, 1 - y), (1 - x, 1 - y)]

        def copy(a, kk, block, to, src=None):
            dst = outs[a].at[_slot(*block)]
            return pltpu.make_async_remote_copy(
                src_ref=dst if src is None else src, dst_ref=dst,
                send_sem=send_sems.at[a * 7 + kk], recv_sem=recv_sems.at[a * 7 + kk],
                device_id=to, device_id_type=MESH)

        mine = [pltpu.make_async_copy(ins[a], outs[a].at[_slot(*me)], local_sems.at[a]) for a in range(n)]
        first = []
        for a in range(n):
            first.append(copy(a, 0, me, sibling, src=ins[a]))
            first += [copy(a, 1 + j, me, (*chip, c), src=ins[a]) for j, chip in enumerate(chips)]
        passed = [copy(a, 4 + j, (*chip, c), sibling) for j, chip in enumerate(chips) for a in range(n)]

        def send():
            _handshake([sibling] + [(*chip, c) for chip in chips])
            for cp in mine + first:
                cp.start()

        def pass_on():
            for j, chip in enumerate(chips):
                for a in range(n):
                    copy(a, 1 + j, (*chip, c), me).wait_recv()
                    passed[j * n + a].start()

        def finish():
            for a in range(n):
                copy(a, 0, sibling, me).wait_recv()
                for j, chip in enumerate(chips):
                    copy(a, 4 + j, (*chip, 1 - c), me).wait_recv()
            for cp in first + passed:
                cp.wait_send()
            for cp in mine:
                cp.wait()

        return [send, pass_on, finish]

    return list(arrs), [_sds((N_DEV,) + a.shape, a.dtype) for a in arrs], _exchange_scratch(n), phases, GATHER_ID


def _scatter_exchange(arrs):
    n = len(arrs)
    flips = [(fx, fy, fc) for fx in (0, 1) for fy in (0, 1) for fc in (0, 1)][1:]

    def phases(ins, outs, send_sems, recv_sems, local_sems):
        x, y, c = _place()
        mine = _slot(x, y, c)
        local = [pltpu.make_async_copy(ins[a].at[mine], outs[a].at[mine], local_sems.at[a]) for a in range(n)]
        peers = [((1 - x) if fx else x, (1 - y) if fy else y, (1 - c) if fc else c) for fx, fy, fc in flips]

        def copy(a, kk, src_slot, dst_slot):
            return pltpu.make_async_remote_copy(
                src_ref=ins[a].at[src_slot], dst_ref=outs[a].at[dst_slot],
                send_sem=send_sems.at[a * 7 + kk], recv_sem=recv_sems.at[a * 7 + kk],
                device_id=peers[kk], device_id_type=MESH)

        sends = [copy(a, kk, _slot(*peers[kk]), mine) for a in range(n) for kk in range(7)]

        def send():
            _handshake(peers)
            for cp in local + sends:
                cp.start()

        def finish():
            for a in range(n):
                for kk in range(7):
                    copy(a, kk, mine, _slot(*peers[kk])).wait_recv()
            for cp in sends:
                cp.wait_send()
            for cp in local:
                cp.wait()

        return [send, finish]

    return list(arrs), [_sds(a.shape, a.dtype) for a in arrs], _exchange_scratch(n), phases, SCATTER_ID


def _pair_exchange(arrs):
    n = len(arrs)

    def phases(ins, outs, send_sems, recv_sems, local_sems):
        x, y, c = _place()

        def copy(a, chip, side):
            return pltpu.make_async_remote_copy(
                src_ref=ins[a].at[chip, side], dst_ref=outs[a].at[chip],
                send_sem=send_sems.at[a * 7 + chip], recv_sem=recv_sems.at[a * 7 + chip],
                device_id=(x, y, 1 - c), device_id_type=MESH)

        sends = [copy(a, chip, 1 - c) for a in range(n) for chip in range(4)]

        def send():
            _handshake([(x, y, 1 - c)])
            for cp in sends:
                cp.start()

        def finish():
            for a in range(n):
                for chip in range(4):
                    copy(a, chip, c).wait_recv()
            for cp in sends:
                cp.wait_send()

        return [send, finish]

    return list(arrs), [_sds((4,) + a.shape[2:], a.dtype) for a in arrs], _exchange_scratch(n), phases, PAIR_ID


def _chip_exchange(arrs):
    n = len(arrs)

    def phases(ins, outs, send_sems, recv_sems, local_sems):
        x, y, c = _place()
        mine = 2 * x + y
        chips = [(1 - x, y), (x, 1 - y), (1 - x, 1 - y)]
        local = [pltpu.make_async_copy(ins[a].at[mine], outs[a].at[mine], local_sems.at[a]) for a in range(n)]

        def copy(a, j, src_slot, dst_slot):
            return pltpu.make_async_remote_copy(
                src_ref=ins[a].at[src_slot], dst_ref=outs[a].at[dst_slot],
                send_sem=send_sems.at[a * 7 + j], recv_sem=recv_sems.at[a * 7 + j],
                device_id=(*chips[j], c), device_id_type=MESH)

        sends = [copy(a, j, 2 * chips[j][0] + chips[j][1], mine) for a in range(n) for j in range(3)]

        def send():
            _handshake([(*chip, c) for chip in chips])
            for cp in local + sends:
                cp.start()

        def finish():
            for a in range(n):
                for j in range(3):
                    copy(a, j, mine, 2 * chips[j][0] + chips[j][1]).wait_recv()
            for cp in sends:
                cp.wait_send()
            for cp in local:
                cp.wait()

        return [send, finish]

    return list(arrs), [_sds(a.shape, a.dtype) for a in arrs], _exchange_scratch(n), phases, CHIP_ID


def _pair_sum(name, mine, theirs):
    _, _, r, c = mine.shape

    def body(side_ref, m_ref, t_ref, o_ref):
        o_ref[...] = (m_ref[...].astype(F32) + t_ref[...].astype(F32)).astype(o_ref.dtype)

    return pl.pallas_call(
        body, name=name,
        grid_spec=pltpu.PrefetchScalarGridSpec(
            num_scalar_prefetch=1, grid=(4,),
            in_specs=[pl.BlockSpec((None, None, r, c), lambda j, side: (j, side[0], 0, 0)),
                      pl.BlockSpec((None, r, c), lambda j, side: (j, 0, 0))],
            out_specs=pl.BlockSpec((None, r, c), lambda j, side: (j, 0, 0))),
        out_shape=_sds(theirs.shape, theirs.dtype),
        compiler_params=_params(("parallel",)),
    )(lax.axis_index("c").astype(jnp.int32).reshape(1), mine, theirs)


def _exchange_call(name, exchange):
    arrs, out_shape, scratch, phases, collective_id = exchange
    n = len(arrs)

    def body(*refs):
        for step in phases(refs[:n], refs[n:2 * n], *refs[2 * n:]):
            step()

    return pl.pallas_call(body, name=name, in_specs=[ANY] * n, out_specs=[ANY] * n,
                          out_shape=out_shape, scratch_shapes=scratch,
                          compiler_params=pltpu.CompilerParams(collective_id=collective_id))(*arrs)


def _carry_exchange(exchange, refs, n_in, n_out, first, middle, last):
    arrs, _, _, phases, _ = exchange
    n = len(arrs)
    if n == 0:
        return lambda: None
    ins = refs[n_in:n_in + n]
    outs = refs[n_in + n + n_out:n_in + 2 * n + n_out]
    sems = n_in + 2 * n + n_out
    steps = phases(ins, outs, *refs[sems:sems + 3])
    pl.when(first)(steps[0])
    if len(steps) == 3:
        pl.when(middle)(steps[1])
    return lambda: pl.when(last)(steps[-1])


def _adamw_math(w, g, m, v):
    m2 = ADAM_B1 * m + (1.0 - ADAM_B1) * g
    v2 = ADAM_B2 * v + (1.0 - ADAM_B2) * jnp.square(g)
    m_hat = m2 / (1.0 - ADAM_B1 ** ADAM_STEP)
    v_hat = v2 / (1.0 - ADAM_B2 ** ADAM_STEP)
    delta = -ADAM_LR * (m_hat / (jnp.sqrt(v_hat) + ADAM_EPS) + ADAM_WD * w)
    return delta, m2, v2


def _sum_adamw_tile(parts, w, m, v):
    g = parts[0].astype(F32)
    for d in range(1, parts.shape[0]):
        g = g + parts[d].astype(F32)
    return (g, *_adamw_math(w, g, m, v))


def _sum_adamw(name, parts, w, m, v, tr=256):
    p, r, c = parts.shape
    tr = _pick(r, tr, 16)

    def body(p_ref, w_ref, m_ref, v_ref, g_ref, d_ref, m2_ref, v2_ref):
        g_ref[...], d_ref[...], m2_ref[...], v2_ref[...] = _sum_adamw_tile(p_ref[...], w_ref[...], m_ref[...], v_ref[...])

    tile = pl.BlockSpec((tr, c), lambda i: (i, 0))
    return pl.pallas_call(
        body, name=name, grid=(r // tr,),
        in_specs=[pl.BlockSpec((p, tr, c), lambda i: (0, i, 0)), tile, tile, tile],
        out_specs=[tile] * 4, out_shape=[_sds((r, c))] * 4,
        compiler_params=_params(("parallel",)),
    )(parts, w, m, v)


def _sum_parts(name, parts):
    p, r, c = parts.shape

    def body(p_ref, o_ref):
        g = p_ref[0]
        for d in range(1, p):
            g = g + p_ref[d]
        o_ref[...] = g

    return pl.pallas_call(
        body, name=name, out_shape=_sds((r, c)),
        in_specs=[pl.BlockSpec(memory_space=pltpu.VMEM)], out_specs=pl.BlockSpec(memory_space=pltpu.VMEM),
    )(parts)


WEIGHTS = ["norm_mix_pre", "w_in", "conv_dw_w", "conv_dw_b", "conv_ln_g", "conv_ln_b", "w_conv_branch",
           "b_conv_branch", "w_att_branch", "w_out", "norm_mix_post", "norm_ffn_pre", "w_ffn_up", "w_ffn_down",
           "norm_ffn_post"]
COL_SHARDED = ["w_conv_branch", "w_att_branch"]
TRANSPOSED = ["w_in", "w_ffn_up"]
VECTORS = ["norm_mix_pre", "conv_dw_b", "conv_ln_g", "conv_ln_b", "b_conv_branch", "norm_mix_post",
           "norm_ffn_pre", "norm_ffn_post"]


def _cols_to_full(g):
    return g.transpose(1, 0, 2).reshape(g.shape[1], N_DEV * g.shape[2])


def _full_to_cols(f):
    return f.reshape(f.shape[0], N_DEV, f.shape[1] // N_DEV).transpose(1, 0, 2)


PACK_ROWS = 7


def _pack_vectors(vecs, extra=None):
    parts = [vecs[nm].reshape(-1) for nm in VECTORS]
    parts.append(jnp.zeros((1,), F32) if extra is None else extra.reshape(1))
    used = sum(p.size for p in parts)
    parts.append(jnp.zeros((PACK_ROWS * D_MODEL - used,), F32))
    return jnp.concatenate(parts).reshape(PACK_ROWS, D_MODEL)


def _unpack_vectors(packed, sizes):
    flat, out, at = packed.reshape(-1), {}, 0
    for nm in VECTORS:
        out[nm] = flat[at:at + sizes[nm]]
        at += sizes[nm]
    return out, flat[at]


def kernel(x, norm_mix_pre, w_in, conv_dw_w, conv_dw_b, conv_ln_g, conv_ln_b, w_conv_branch, b_conv_branch, w_att_branch, w_out, norm_mix_post, norm_ffn_pre, w_ffn_up, w_ffn_down, norm_ffn_post, loss_target, m_norm_mix_pre, m_w_in, m_conv_dw_w, m_conv_dw_b, m_conv_ln_g, m_conv_ln_b, m_w_conv_branch, m_b_conv_branch, m_w_att_branch, m_w_out, m_norm_mix_post, m_norm_ffn_pre, m_w_ffn_up, m_w_ffn_down, m_norm_ffn_post, v_norm_mix_pre, v_w_in, v_conv_dw_w, v_conv_dw_b, v_conv_ln_g, v_conv_ln_b, v_w_conv_branch, v_b_conv_branch, v_w_att_branch, v_w_out, v_norm_mix_post, v_norm_ffn_pre, v_w_ffn_up, v_w_ffn_down, v_norm_ffn_post):
    ws = dict(zip(WEIGHTS, [norm_mix_pre, w_in, conv_dw_w, conv_dw_b, conv_ln_g, conv_ln_b, w_conv_branch,
                            b_conv_branch, w_att_branch, w_out, norm_mix_post, norm_ffn_pre, w_ffn_up, w_ffn_down,
                            norm_ffn_post]))
    ms = dict(zip(WEIGHTS, [m_norm_mix_pre, m_w_in, m_conv_dw_w, m_conv_dw_b, m_conv_ln_g, m_conv_ln_b,
                            m_w_conv_branch, m_b_conv_branch, m_w_att_branch, m_w_out, m_norm_mix_post,
                            m_norm_ffn_pre, m_w_ffn_up, m_w_ffn_down, m_norm_ffn_post]))
    vs = dict(zip(WEIGHTS, [v_norm_mix_pre, v_w_in, v_conv_dw_w, v_conv_dw_b, v_conv_ln_g, v_conv_ln_b,
                            v_w_conv_branch, v_b_conv_branch, v_w_att_branch, v_w_out, v_norm_mix_post,
                            v_norm_ffn_pre, v_w_ffn_up, v_w_ffn_down, v_norm_ffn_post]))

    dw_block = jnp.pad(conv_dw_w, ((0, 1), (0, 0)))
    g_in, g_dw = _exchange_call("gather_first", _gather_exchange([w_in.T.astype(BF16), dw_block]))
    full = {"w_in": _full_weight("w_in", g_in), "conv_dw_w": _cols_to_full(g_dw)}
    for nm in VECTORS:
        full[nm] = ws[nm].reshape(1, -1)

    def as_kept(nm, a):
        return a.T if nm in TRANSPOSED else a

    ride_along = ["w_ffn_up", "w_out"]
    loss_local, grad_x, received, updated, grads = _local_step(
        x[0], loss_target[0], full, {nm: as_kept(nm, ws[nm]).astype(BF16) for nm in LATE},
        {nm: tuple(as_kept(nm, a[nm]) for a in (ws, ms, vs)) for nm in ride_along})

    small = _exchange_call("gather_small_grads", _gather_exchange(
        [_pack_vectors(grads, extra=loss_local), grads["conv_dw_w"]]))
    out_g, out_d, out_m, out_v = {}, {}, {}, {}
    for nm in LATE + ["w_in"]:
        res = updated[nm] if nm in updated else _sum_adamw(
            "adamw_" + nm, received[nm], *[as_kept(nm, a[nm]) for a in (ws, ms, vs)])
        out_g[nm], out_d[nm], out_m[nm], out_v[nm] = [as_kept(nm, r) for r in res]
    sizes = {nm: ws[nm].size for nm in VECTORS}
    vec = _sum_adamw("adamw_vectors", small[0], _pack_vectors(ws), _pack_vectors(ms), _pack_vectors(vs))
    for res, dst in zip(vec, (out_g, out_d, out_m, out_v)):
        dst.update(_unpack_vectors(res, sizes)[0])
    loss = _unpack_vectors(vec[0], sizes)[1]
    dw_full = _sum_parts("sum_dw_grads", small[1])
    me = _slot(*_place())
    dw_mine = lax.dynamic_slice(dw_full, (0, me * (CONV_DIM // N_DEV)), (CONV_WIDTH, CONV_DIM // N_DEV))
    nm = "conv_dw_w"
    out_g[nm], out_d[nm], out_m[nm], out_v[nm] = _sum_adamw("adamw_dw", dw_mine[None], ws[nm], ms[nm], vs[nm])

    outs = [loss, grad_x[None]]
    for group in (out_g, out_d, out_m, out_v):
        outs += [group[nm] for nm in WEIGHTS]
    return tuple(outs)
```

```python
import math

import jax
import jax.numpy as jnp
from jax import lax
from jax.experimental import pallas as pl
from jax.experimental.pallas import tpu as pltpu

F32 = jnp.float32
BF16 = jnp.bfloat16

N_DEV = 8
D_MODEL = 1024
CONV_DIM = 512
CONV_WIDTH = 31
N_HEADS = 8
HEAD_DIM = 64
ATT_DIM = N_HEADS * HEAD_DIM
D_FF = 2816
EPS = 1e-6
IN_SPLITS = (0, 1024, 1536, 2048, 2560, 3584, 4608)

ADAM_LR = 0.001
ADAM_B1 = 0.9
ADAM_B2 = 0.999
ADAM_EPS = 1e-08
ADAM_WD = 0.01
ADAM_STEP = 10

LANES = 128
SUBLANES = 8
HALO = 32
ATT_TILE = 256
ATT_PART = 176
DEAD_SUM = -120.0
VMEM_LIMIT = 56 * 1024 * 1024
MESH = pl.DeviceIdType.MESH
ANY = pl.BlockSpec(memory_space=pl.ANY)


def _pick(dim, target, align=LANES):
    t = min(dim, target)
    t -= t % align
    while t >= align:
        if dim % t == 0:
            return t
        t -= align
    return dim


def _params(semantics, collective_id=None):
    return pltpu.CompilerParams(dimension_semantics=semantics, vmem_limit_bytes=VMEM_LIMIT,
                                collective_id=collective_id)


def _tn_matmul(a, b, *, name):
    return _pieces_tn_matmul([a], b, name=name, tj=_pick(a.shape[1], 1408))


def _pieces_tn_matmul(pieces, b, *, name, tj=512):
    s, n = b.shape
    counts = [p.shape[1] // tj for p in pieces]
    starts = [sum(counts[:i]) for i in range(len(pieces))]
    assert all(p.shape == (s, c * tj) for p, c in zip(pieces, counts))

    def body(*refs):
        b_ref, o_ref = refs[len(pieces):]
        j = pl.program_id(0)
        for p_ref, first, count in zip(refs, starts, counts):
            @pl.when((j >= first) & (j < first + count))
            def _():
                o_ref[...] = lax.dot_general(p_ref[...].astype(BF16), b_ref[...], TN,
                                             preferred_element_type=F32).astype(o_ref.dtype)

    def piece_spec(first, count):
        return pl.BlockSpec((s, tj), lambda j: (0, jnp.clip(j - first, 0, count - 1)))

    return pl.pallas_call(
        body, name=name, grid=(sum(counts),),
        in_specs=[piece_spec(f, c) for f, c in zip(starts, counts)]
        + [pl.BlockSpec((s, n), lambda j: (0, 0), pipeline_mode=pl.Buffered(1))],
        out_specs=pl.BlockSpec((tj, n), lambda j: (j, 0)),
        out_shape=jax.ShapeDtypeStruct((sum(counts) * tj, n), BF16),
        compiler_params=_params(("arbitrary",)),
    )(*pieces, b)


NO_EXCHANGE = ([], [], [], None, None)


def _sweep_marks(nt):
    i = pl.program_id(0)
    return i == 0, i == (3 * nt) // 4, i == nt - 1


def _rowwise(name, fn, rows, bcasts, row_outs, red_outs=(), tm=256, exchange=NO_EXCHANGE):
    s = rows[0].shape[0]
    tm = _pick(s, tm, 16)
    nt = s // tm
    resident = pl.Buffered(1)
    nr, nb, no, nd = len(rows), len(bcasts), len(row_outs), len(red_outs)
    x_arrs, x_shape, x_scratch, _, x_id = exchange
    nx = len(x_arrs)
    first_out = nr + nb + nx

    def body(*refs):
        finish_exchange = _carry_exchange(exchange, refs, nr + nb, no + nd, *_sweep_marks(nt))
        ins = [r[...] for r in refs[:nr + nb]]
        outs, reds = fn(*ins)
        for ref, val in zip(refs[first_out:first_out + no], outs):
            ref[...] = val.astype(ref.dtype)
        i = pl.program_id(0)
        for ref, val in zip(refs[first_out + no:first_out + no + nd], reds):
            @pl.when(i == 0)
            def _():
                ref[...] = val

            @pl.when(i > 0)
            def _():
                ref[...] += val
        finish_exchange()

    def row_spec(a):
        assert a.shape[-2] % nt == 0, (name, a.shape, nt)
        if len(a.shape) == 3:
            return pl.BlockSpec((a.shape[0], a.shape[1] // nt, a.shape[2]), lambda i: (0, i, 0))
        return pl.BlockSpec((a.shape[0] // nt, a.shape[1]), lambda i: (i, 0))

    in_specs = [row_spec(r) for r in rows]
    in_specs += [pl.BlockSpec(b.shape, lambda i: (0, 0), pipeline_mode=resident) for b in bcasts]
    out_specs = [row_spec(o) for o in row_outs]
    out_specs += [pl.BlockSpec(d.shape, lambda i: (0, 0)) for d in red_outs]
    return pl.pallas_call(
        body, name=name, grid=(nt,), in_specs=in_specs + [ANY] * nx, out_specs=out_specs + [ANY] * nx,
        out_shape=list(row_outs) + list(red_outs) + x_shape, scratch_shapes=x_scratch,
        compiler_params=_params(("arbitrary",), x_id),
    )(*rows, *bcasts, *x_arrs)


def _sds(shape, dtype=F32):
    return jax.ShapeDtypeStruct(shape, dtype)


def _rms(x, g):
    y = x * lax.rsqrt(jnp.mean(x * x, axis=-1, keepdims=True) + EPS)
    return y * g


def _silu(x):
    return x * jax.nn.sigmoid(x)


def _swiglu(g, u):
    return _silu(g) * u


def _ln_silu(u, g, b):
    mu = jnp.mean(u, axis=-1, keepdims=True)
    var = jnp.mean(jnp.square(u - mu), axis=-1, keepdims=True)
    return _silu((u - mu) * lax.rsqrt(var + EPS) * g + b)


def _merge(conv_pre, att_out, g_conv, g_att, b_cb):
    return jax.nn.sigmoid(g_conv) * (conv_pre + b_cb) + jax.nn.sigmoid(g_att) * att_out


def _glu(t):
    return t[:, :CONV_DIM] * jax.nn.sigmoid(t[:, CONV_DIM:])


def _shifted_reader(buf, shifted, tm):
    for b in range(1, SUBLANES):
        shifted[b - 1, :, :] = buf[pl.ds(b, tm + HALO - SUBLANES), :]

    def read(o):
        a, b = divmod(o, SUBLANES)
        return buf[pl.ds(SUBLANES * a, tm), :] if b == 0 else shifted[b - 1, pl.ds(SUBLANES * a, tm), :]

    return read


def _conv_fwd(conv_in, w_pad, b, ln_g, ln_b, exchange, tm=256):
    s = conv_in.shape[0]
    tm = _pick(s, tm, HALO)
    ratio = tm // HALO
    x_arrs, x_shape, x_scratch, _, x_id = exchange
    nx = len(x_arrs)

    def body(*refs):
        main_ref, halo_ref, w_ref, b_ref, g_ref, be_ref = refs[:6]
        u3_ref, u1_ref = refs[6 + nx:8 + nx]
        buf, shifted = refs[-2:]
        finish_exchange = _carry_exchange(exchange, refs, 6, 2, *_sweep_marks(s // tm))
        i = pl.program_id(0)
        buf[0:HALO, :] = _glu(halo_ref[...]) * (i > 0).astype(F32)
        buf[HALO:HALO + tm, :] = _glu(main_ref[...])
        read = _shifted_reader(buf, shifted, tm)
        acc = jnp.zeros((tm, CONV_DIM), F32) + b_ref[...]
        for j in range(CONV_WIDTH):
            acc = acc + w_ref[j:j + 1, :] * read(HALO - (CONV_WIDTH - 1) + j)
        u1_ref[...] = acc
        u3_ref[...] = _ln_silu(acc, g_ref[...], be_ref[...]).astype(u3_ref.dtype)
        finish_exchange()

    res = pl.pallas_call(
        body, name="conv_fwd", grid=(s // tm,),
        in_specs=[pl.BlockSpec((tm, 2 * CONV_DIM), lambda i: (i, 0)),
                  pl.BlockSpec((HALO, 2 * CONV_DIM), lambda i: (jnp.maximum(i * ratio - 1, 0), 0)),
                  pl.BlockSpec(w_pad.shape, lambda i: (0, 0)),
                  pl.BlockSpec(b.shape, lambda i: (0, 0)),
                  pl.BlockSpec(ln_g.shape, lambda i: (0, 0)),
                  pl.BlockSpec(ln_b.shape, lambda i: (0, 0))] + [ANY] * nx,
        out_specs=[pl.BlockSpec((tm, CONV_DIM), lambda i: (i, 0)),
                   pl.BlockSpec((tm, CONV_DIM), lambda i: (i, 0))] + [ANY] * nx,
        out_shape=[_sds((s, CONV_DIM), BF16), _sds((s, CONV_DIM), F32)] + x_shape,
        scratch_shapes=x_scratch + [pltpu.VMEM((tm + HALO, CONV_DIM), F32),
                                    pltpu.VMEM((SUBLANES - 1, tm + HALO - SUBLANES, CONV_DIM), F32)],
        compiler_params=_params(("arbitrary",), x_id),
    )(conv_in, conv_in, w_pad, b, ln_g, ln_b, *x_arrs)
    return res[0], res[1], res[2:]


def _conv_bwd(conv_in, u1, du3, ln_g, ln_b, w_pad, exchange, tm=256):
    s = conv_in.shape[0]
    tm = _pick(s, tm, HALO)
    ratio = tm // HALO
    nt = s // tm
    last_halo = s // HALO - 1
    x_arrs, x_shape, x_scratch, _, x_id = exchange
    nx = len(x_arrs)

    def body(*refs):
        main_ref, halo_ref, u1_ref, u1n_ref, du3_ref, du3n_ref, g_ref, be_ref, w_ref = refs[:9]
        dci_ref, dw_ref, db_ref, dg_ref, dbe_ref = refs[9 + nx:14 + nx]
        ubuf, dbuf, ushift, dshift = refs[-4:]
        finish_exchange = _carry_exchange(exchange, refs, 9, 5, *_sweep_marks(nt))
        i = pl.program_id(0)
        main = main_ref[...]
        a = main[:, :CONV_DIM]
        sb = jax.nn.sigmoid(main[:, CONV_DIM:])
        ubuf[0:HALO, :] = _glu(halo_ref[...]) * (i > 0).astype(F32)
        ubuf[HALO:HALO + tm, :] = a * sb

        def ln_bwd(u1t, du3t):
            _, vjp = jax.vjp(_ln_silu, u1t, g_ref[...], be_ref[...])
            return vjp(du3t)

        du, dg, dbe = ln_bwd(u1_ref[...], du3_ref[...])
        dbuf[0:tm, :] = du
        dbuf[tm:tm + HALO, :] = ln_bwd(u1n_ref[...], du3n_ref[...])[0] * (i < nt - 1).astype(F32)

        @pl.when(i == 0)
        def _():
            dw_ref[...] = jnp.zeros_like(dw_ref)
            db_ref[...] = jnp.zeros_like(db_ref)
            dg_ref[...] = jnp.zeros_like(dg_ref)
            dbe_ref[...] = jnp.zeros_like(dbe_ref)

        dg_ref[...] += dg
        dbe_ref[...] += dbe

        read_u = _shifted_reader(ubuf, ushift, tm)
        read_d = _shifted_reader(dbuf, dshift, tm)
        du0 = jnp.zeros((tm, CONV_DIM), F32)
        for j in range(CONV_WIDTH):
            du0 = du0 + w_ref[j:j + 1, :] * read_d(CONV_WIDTH - 1 - j)
            dw_ref[j:j + 1, :] += jnp.sum(du * read_u(HALO - (CONV_WIDTH - 1) + j), axis=0, keepdims=True)
        db_ref[...] += jnp.sum(du, axis=0, keepdims=True)
        dci_ref[:, :CONV_DIM] = (du0 * sb).astype(dci_ref.dtype)
        dci_ref[:, CONV_DIM:] = (du0 * a * sb * (1.0 - sb)).astype(dci_ref.dtype)
        finish_exchange()

    res = pl.pallas_call(
        body, name="conv_bwd", grid=(nt,),
        in_specs=[pl.BlockSpec((tm, 2 * CONV_DIM), lambda i: (i, 0)),
                  pl.BlockSpec((HALO, 2 * CONV_DIM), lambda i: (jnp.maximum(i * ratio - 1, 0), 0))]
        + [pl.BlockSpec((tm, CONV_DIM), lambda i: (i, 0)),
           pl.BlockSpec((HALO, CONV_DIM), lambda i: (jnp.minimum((i + 1) * ratio, last_halo), 0))] * 2
        + [pl.BlockSpec((1, CONV_DIM), lambda i: (0, 0))] * 2 + [pl.BlockSpec(w_pad.shape, lambda i: (0, 0))]
        + [ANY] * nx,
        out_specs=[pl.BlockSpec((tm, 2 * CONV_DIM), lambda i: (i, 0)),
                   pl.BlockSpec(w_pad.shape, lambda i: (0, 0))]
        + [pl.BlockSpec((1, CONV_DIM), lambda i: (0, 0))] * 3 + [ANY] * nx,
        out_shape=[_sds((s, 2 * CONV_DIM), BF16), _sds(w_pad.shape)] + [_sds((1, CONV_DIM))] * 3 + x_shape,
        scratch_shapes=x_scratch + [pltpu.VMEM((tm + HALO, CONV_DIM), F32)] * 2
        + [pltpu.VMEM((SUBLANES - 1, tm + HALO - SUBLANES, CONV_DIM), F32)] * 2,
        compiler_params=_params(("arbitrary",), x_id),
    )(conv_in, conv_in, u1, u1, du3, du3, ln_g, ln_b, w_pad, *x_arrs)
    return res[:5], res[5:]


def _logsig_neg(z):
    return jnp.minimum(-z, 0.0) - jnp.log(1.0 + jnp.exp(-jnp.abs(z)))


def _split_dot(val, tri):
    hi = val.astype(BF16)
    lo = (val - hi.astype(F32)).astype(BF16)
    return jnp.dot(hi, tri, preferred_element_type=F32) + jnp.dot(lo, tri, preferred_element_type=F32)


def _attn_masks(t, later):
    row = lax.broadcasted_iota(jnp.int32, (t, t), 0)
    col = lax.broadcasted_iota(jnp.int32, (t, t), 1)
    tri = jnp.where(row > col if later else row <= col, 1.0, 0.0).astype(BF16)
    return col < row, tri


def _grid_marks(h, nq):
    hh, i = pl.program_id(0), pl.program_id(1)
    return (hh == 0) & (i == 0), (hh == h - 1) & (i == nq // 2), (hh == h - 1) & (i == nq - 1)


def _head_masks(shape):
    lane = lax.broadcasted_iota(jnp.int32, shape, len(shape) - 1)
    return lane < HEAD_DIM, lane >= HEAD_DIM


def _per_head(blk):
    m0, m1 = _head_masks(blk.shape)
    zero = jnp.zeros_like(blk)
    return jnp.where(m0, blk, zero), jnp.where(m1, blk, zero)


NT = (((1,), (1,)), ((), ()))
TN = (((0,), (0,)), ((), ()))


def _with_top(whole, top):
    rows = top.shape[0]
    return top if rows == whole.shape[0] else jnp.concatenate([top, whole[rows:]], axis=0)


def _attn_fwd(q, k, v, exchange):
    s = q.shape[0]
    hp = q.shape[1] // LANES
    t = ATT_TILE
    scale = 1.0 / math.sqrt(HEAD_DIM)
    x_arrs, x_shape, x_scratch, _, x_id = exchange
    nx = len(x_arrs)

    def body(*refs):
        q_ref, k_ref, v_ref = refs[:3]
        o_ref, lt_ref, nb_ref = refs[3 + nx:6 + nx]
        finish_exchange = _carry_exchange(exchange, refs, 3, 3, *_grid_marks(hp, s // t))
        i = pl.program_id(1)
        qs = _per_head((q_ref[...].astype(F32) * scale).astype(BF16))
        causal, tri = _attn_masks(t, later=True)

        def step(kb, carry, masked, rows):
            cs, acc = carry
            off = pl.multiple_of(kb * t, t)
            kblk = k_ref[pl.ds(off, t), :]
            vs = _per_head(v_ref[pl.ds(off, t), :])
            acc_top = acc[:rows]
            new_cs = []
            for hd in range(2):
                z = lax.dot_general(qs[hd][:rows], kblk, NT, preferred_element_type=F32)
                l = _logsig_neg(z)
                if masked:
                    l = jnp.where(causal, l, 0.0)
                e = z + l + _split_dot(l, tri) + cs[hd][:rows]
                if masked:
                    e = jnp.where(causal, e, -1e30)
                acc_top = acc_top + jnp.dot(jnp.exp(e).astype(BF16), vs[hd], preferred_element_type=F32)
                new_cs.append(_with_top(cs[hd], cs[hd][:rows] + jnp.sum(l, axis=1, keepdims=True)))
            return tuple(new_cs), _with_top(acc, acc_top)

        zero = jnp.zeros((t, 1), F32)
        carry = step(i, ((zero, zero), jnp.zeros((t, LANES), F32)), True, t)

        def live(cs, lo, hi):
            return jnp.maximum(jnp.max(cs[0][lo:hi]), jnp.max(cs[1][lo:hi])) > DEAD_SUM

        def more(state):
            n, _, (cs, _) = state
            return (n < i) & live(cs, 0, t)

        def sweep(state):
            n, n_full, cr = state
            whole = live(cr[0], ATT_PART, t)
            cr = lax.cond(whole, lambda c: step(i - 1 - n, c, False, t), lambda c: step(i - 1 - n, c, False, ATT_PART), cr)
            return n + 1, n_full + whole.astype(jnp.int32), cr

        n_blocks, n_full, carry = lax.while_loop(more, sweep, (jnp.int32(0), jnp.int32(0), carry))
        m0, _ = _head_masks((t, LANES))
        lt_ref[...] = jnp.where(m0, carry[0][0], carry[0][1])
        o_ref[...] = carry[1].astype(o_ref.dtype)
        nb_ref[0, pl.program_id(0), i] = n_blocks.astype(F32)
        nb_ref[1, pl.program_id(0), i] = n_full.astype(F32)
        finish_exchange()

    res = pl.pallas_call(
        body, name="attn_fwd", grid=(hp, s // t),
        in_specs=[pl.BlockSpec((t, LANES), lambda p, i: (i, p)),
                  pl.BlockSpec((s, LANES), lambda p, i: (0, p)),
                  pl.BlockSpec((s, LANES), lambda p, i: (0, p))] + [ANY] * nx,
        out_specs=[pl.BlockSpec((t, LANES), lambda p, i: (i, p)),
                   pl.BlockSpec((None, t, LANES), lambda p, i: (p, i, 0)),
                   pl.BlockSpec(memory_space=pltpu.SMEM)] + [ANY] * nx,
        out_shape=[_sds(q.shape, BF16), _sds((hp, s, LANES), F32), _sds((2, hp, s // t), F32)] + x_shape,
        scratch_shapes=x_scratch,
        compiler_params=_params(("arbitrary", "arbitrary"), x_id),
    )(q, k, v, *x_arrs)
    return res[0], res[1], res[2], res[3:]


def _attn_bwd(q, k, v, do, ltot, n_blocks, exchange):
    s = q.shape[0]
    hp = q.shape[1] // LANES
    t = ATT_TILE
    scale = 1.0 / math.sqrt(HEAD_DIM)
    x_arrs, x_shape, x_scratch, _, x_id = exchange
    nx = len(x_arrs)

    def body(*refs):
        q_ref, k_ref, v_ref, do_ref, lt_ref, nb_ref = refs[:6]
        dq_ref, dk_ref, dv_ref = refs[6 + nx:9 + nx]
        finish_exchange = _carry_exchange(exchange, refs, 6, 3, *_grid_marks(hp, s // t))
        i = pl.program_id(1)
        n_blocks = jnp.clip(nb_ref[0, pl.program_id(0), i].astype(jnp.int32), 0, i)
        n_full = jnp.clip(nb_ref[1, pl.program_id(0), i].astype(jnp.int32), 0, n_blocks)

        @pl.when(i == 0)
        def _():
            dk_ref[...] = jnp.zeros_like(dk_ref)
            dv_ref[...] = jnp.zeros_like(dv_ref)

        qb = q_ref[...]
        qm = _per_head(qb)
        qs = _per_head((qb.astype(F32) * scale).astype(BF16))
        dos = _per_head(do_ref[...])
        lts = (lt_ref[:, 0:1], lt_ref[:, HEAD_DIM:HEAD_DIM + 1])
        causal, tri = _attn_masks(t, later=False)

        def step(kb, carry, masked, rows):
            cls, cgs, dq = carry
            off = pl.multiple_of(kb * t, t)
            kblk = k_ref[pl.ds(off, t), :]
            vblk = v_ref[pl.ds(off, t), :]
            ks = _per_head(kblk)
            dq_top = dq[:rows]
            dk = jnp.zeros((t, LANES), F32)
            dv = jnp.zeros((t, LANES), F32)
            new_cls, new_cgs = [], []
            for hd in range(2):
                z = lax.dot_general(qs[hd][:rows], kblk, NT, preferred_element_type=F32)
                l = _logsig_neg(z)
                if masked:
                    l = jnp.where(causal, l, 0.0)
                e = z + l + ((lts[hd][:rows] - cls[hd][:rows]) - _split_dot(l, tri))
                if masked:
                    e = jnp.where(causal, e, -1e30)
                a = jnp.exp(e)
                g = lax.dot_general(dos[hd][:rows], vblk, NT, preferred_element_type=F32) * a
                p = cgs[hd][:rows] + jnp.dot(g.astype(BF16), tri, preferred_element_type=F32) - g
                el = jnp.exp(l)
                dz = g * el - p * (1.0 - el)
                if masked:
                    dz = jnp.where(causal, dz, 0.0)
                dzb = (dz * scale).astype(BF16)
                dq_top = dq_top + jnp.dot(dzb, ks[hd], preferred_element_type=F32)
                dk = dk + lax.dot_general(dzb, qm[hd][:rows], TN, preferred_element_type=F32)
                dv = dv + lax.dot_general(a.astype(BF16), dos[hd][:rows], TN, preferred_element_type=F32)
                new_cls.append(_with_top(cls[hd], cls[hd][:rows] + jnp.sum(l, axis=1, keepdims=True)))
                new_cgs.append(_with_top(cgs[hd], cgs[hd][:rows] + jnp.sum(g, axis=1, keepdims=True)))
            dk_ref[pl.ds(off, t), :] += dk
            dv_ref[pl.ds(off, t), :] += dv
            return tuple(new_cls), tuple(new_cgs), _with_top(dq, dq_top)

        zero = jnp.zeros((t, 1), F32)
        init = ((zero, zero), (zero, zero), jnp.zeros((t, LANES), F32))
        carry = lax.fori_loop(i - n_blocks, i - n_full, lambda kb, cr: step(kb, cr, False, ATT_PART), init)
        carry = lax.fori_loop(i - n_full, i, lambda kb, cr: step(kb, cr, False, t), carry)
        carry = step(i, carry, True, t)
        dq_ref[...] = carry[2]
        finish_exchange()

    blk = pl.BlockSpec((t, LANES), lambda p, i: (i, p))
    whole = pl.BlockSpec((s, LANES), lambda p, i: (0, p))
    res = pl.pallas_call(
        body, name="attn_bwd", grid=(hp, s // t),
        in_specs=[blk, whole, whole, blk, pl.BlockSpec((None, t, LANES), lambda p, i: (p, i, 0)),
                  pl.BlockSpec(memory_space=pltpu.SMEM)] + [ANY] * nx,
        out_specs=[blk, whole, whole] + [ANY] * nx,
        out_shape=[_sds(q.shape)] * 3 + x_shape,
        scratch_shapes=x_scratch,
        compiler_params=_params(("arbitrary", "arbitrary"), x_id),
    )(q, k, v, do, ltot, n_blocks, *x_arrs)
    return res[0], res[1], res[2], res[3:]


LATE = ["w_conv_branch", "w_att_branch", "w_out", "w_ffn_up", "w_ffn_down"]


def _full_weight(name, gathered):
    return _cols_to_full(gathered) if name in COL_SHARDED else gathered.reshape(-1, gathered.shape[2])


def _grad_slabs(name, grad):
    return _full_to_cols(grad) if name in COL_SHARDED else grad.reshape(N_DEV, -1, grad.shape[1])


def _side_slabs(name, grad):
    slabs = _grad_slabs(name, grad)
    return slabs.reshape((4, 2) + slabs.shape[1:])


def _local_step(x, target, w, late_blocks, opt):
    s = x.shape[0]
    w = dict(w)
    g1, g2, g3, g4 = w["norm_mix_pre"], w["norm_mix_post"], w["norm_ffn_pre"], w["norm_ffn_post"]

    w_in = w["w_in"]

    def proj_fn(xt, g1_, w_in_t):
        h = _rms(xt, g1_).astype(BF16)
        proj = lax.dot_general(h, w_in_t, NT, preferred_element_type=F32)
        return (h, *[proj[:, IN_SPLITS[n]:IN_SPLITS[n + 1]] for n in range(6)]), ()

    mix_weights = ["w_conv_branch", "w_att_branch", "w_out"]
    h1, conv_in, q, k, v, g_conv, g_att = _rowwise(
        "norm_proj", proj_fn, [x], [g1, w_in],
        [_sds((s, D_MODEL), BF16), _sds((s, 2 * CONV_DIM)), _sds((s, ATT_DIM), BF16), _sds((s, ATT_DIM), BF16),
         _sds((s, ATT_DIM), BF16), _sds((s, D_MODEL), BF16), _sds((s, D_MODEL), BF16)], tm=512)

    u3, u1, gathered = _conv_fwd(conv_in, w["conv_dw_w"], w["conv_dw_b"], w["conv_ln_g"], w["conv_ln_b"],
                                 _gather_exchange([late_blocks[nm] for nm in mix_weights]))
    for nm, g in zip(mix_weights, gathered):
        w[nm] = _full_weight(nm, g)
    att, ltot, n_blocks, (g_up,) = _attn_fwd(q, k, v, _gather_exchange([late_blocks["w_ffn_up"]]))
    w["w_ffn_up"] = _full_weight("w_ffn_up", g_up)

    def merge_fn(u3t, at, gc, ga, xt, w_cb, w_ab, b_cb, w_out, g2_, g3_):
        cp = jnp.dot(u3t, w_cb, preferred_element_type=F32)
        ao = jnp.dot(at, w_ab, preferred_element_type=F32)
        mg = _merge(cp, ao, gc.astype(F32), ga.astype(F32), b_cb).astype(BF16)
        mix_ = jnp.dot(mg, w_out, preferred_element_type=F32)
        x2_ = xt + _rms(mix_, g2_)
        return (mg, cp, ao, mix_, x2_, _rms(x2_, g3_)), ()

    merged, conv_pre, att_out, mix, x2, h2 = _rowwise(
        "branch_merge_mix", merge_fn, [u3, att, g_conv, g_att, x],
        [w["w_conv_branch"], w["w_att_branch"], w["b_conv_branch"], w["w_out"], g2, g3],
        [_sds((s, D_MODEL), BF16)] * 3 + [_sds((s, D_MODEL)), _sds((s, D_MODEL)), _sds((s, D_MODEL), BF16)], tm=512)

    def ffn_up_fn(ht, w_up_t):
        gu_ = lax.dot_general(ht, w_up_t, NT, preferred_element_type=F32)
        return (gu_, _swiglu(gu_[:, :D_FF], gu_[:, D_FF:])), ()

    gu, act, g_down = _rowwise("ffn_up", ffn_up_fn, [h2], [w["w_ffn_up"]],
                               [_sds((s, 2 * D_FF), BF16), _sds((s, D_FF), BF16)], tm=512,
                               exchange=_gather_exchange([late_blocks["w_ffn_down"]]))
    w["w_ffn_down"] = _full_weight("w_ffn_down", g_down)

    def final_fn(at, x2t, tgt, w_down, g4_):
        ff = jnp.dot(at, w_down, preferred_element_type=F32)
        n4, vjp = jax.vjp(_rms, ff, g4_)
        err = x2t + n4 - tgt
        dy = err * (1.0 / D_MODEL)
        dff, dg4 = vjp(dy)
        return (dy, dff), (jnp.sum(err * err, axis=0, keepdims=True), dg4)

    dy, dff, loss_cols, d_g4 = _rowwise("ffn_down_loss", final_fn, [act, x2, target], [w["w_ffn_down"], g4],
                                        [_sds((s, D_MODEL)), _sds((s, D_MODEL), BF16)],
                                        [_sds((1, D_MODEL)), _sds((1, D_MODEL))], tm=512)
    loss = 0.5 * jnp.sum(loss_cols) / D_MODEL

    d_w_down = _tn_matmul(act, dff, name="d_w_down")

    def act_bwd_fn(dfft, gut, w_down):
        d_act = lax.dot_general(dfft, w_down, NT, preferred_element_type=F32)
        gu_ = gut.astype(F32)
        _, vjp = jax.vjp(_swiglu, gu_[:, :D_FF], gu_[:, D_FF:])
        return (jnp.concatenate(vjp(d_act), axis=1),), ()

    down_slabs = _side_slabs("w_ffn_down", d_w_down)
    dgu, theirs = _rowwise("ffn_act_bwd", act_bwd_fn, [dff, gu], [w["w_ffn_down"]], [_sds((s, 2 * D_FF), BF16)],
                           exchange=_pair_exchange([down_slabs]))
    down_sums = _pair_sum("pair_sum_w_ffn_down", down_slabs, theirs)
    d_w_up = _tn_matmul(dgu, h2, name="d_w_up")
    received = {}
    up_slabs = _side_slabs("w_ffn_up", d_w_up)

    def mid_bwd_fn(dgut, xt, mt, dyt, w_up_t, g2_, g3_):
        dh = jnp.dot(dgut, w_up_t, preferred_element_type=F32)
        n2, vjp2 = jax.vjp(_rms, mt, g2_)
        x2_ = xt + n2
        _, vjp3 = jax.vjp(_rms, x2_, g3_)
        dx2_, dg3 = vjp3(dh)
        dx2_ = dx2_ + dyt
        dmix_, dg2 = vjp2(dx2_)
        return (dx2_, dmix_), (dg2, dg3)

    dx2, dmix, d_g2, d_g3, received["w_ffn_down"] = _rowwise(
        "ffn_up_mid_bwd", mid_bwd_fn, [dgu, x, mix, dy], [w["w_ffn_up"], g2, g3],
        [_sds((s, D_MODEL)), _sds((s, D_MODEL), BF16)], [_sds((1, D_MODEL)), _sds((1, D_MODEL))], tm=512,
        exchange=_chip_exchange([down_sums]))
    d_w_out = _tn_matmul(merged, dmix, name="d_w_out")

    def merge_bwd_fn(dmt, cp, ao, gc, ga, w_out, w_cb, w_ab, b_cb):
        dm = lax.dot_general(dmt, w_out, NT, preferred_element_type=F32)
        _, vjp = jax.vjp(_merge, cp.astype(F32), ao.astype(F32), gc.astype(F32), ga.astype(F32), b_cb)
        dcp, dao, dgc, dga, dbias = vjp(dm)
        dcp, dao = dcp.astype(BF16), dao.astype(BF16)
        du3_ = lax.dot_general(dcp, w_cb, NT, preferred_element_type=F32)
        datt_ = lax.dot_general(dao, w_ab, NT, preferred_element_type=F32)
        return (dcp, dao, dgc, dga, du3_, datt_), (dbias,)

    d_conv_out, d_att_out, d_g_conv, d_g_att, du3, d_att, d_b_cb, theirs = _rowwise(
        "merge_bwd", merge_bwd_fn, [dmix, conv_pre, att_out, g_conv, g_att],
        [w["w_out"], w["w_conv_branch"], w["w_att_branch"], w["b_conv_branch"]],
        [_sds((s, D_MODEL), BF16)] * 4 + [_sds((s, CONV_DIM)), _sds((s, ATT_DIM), BF16)], [_sds((1, D_MODEL))], tm=512,
        exchange=_pair_exchange([up_slabs]))

    d_w_cb = _tn_matmul(u3, d_conv_out, name="d_w_conv_branch")
    d_w_ab = _tn_matmul(att, d_att_out, name="d_w_att_branch")

    dq, dk, dv, (received["w_ffn_up"],) = _attn_bwd(
        q, k, v, d_att, ltot, n_blocks, _chip_exchange([_pair_sum("pair_sum_w_ffn_up", up_slabs, theirs)]))

    mix_grads = {"w_conv_branch": d_w_cb, "w_att_branch": d_w_ab, "w_out": d_w_out}
    (d_conv_in, d_dw_w, d_dw_b, d_ln_g, d_ln_b), landed = _conv_bwd(
        conv_in, u1, du3, w["conv_ln_g"], w["conv_ln_b"], w["conv_dw_w"],
        _scatter_exchange([_grad_slabs(nm, mix_grads[nm]) for nm in mix_weights]))
    received.update(zip(mix_weights, landed))

    d_proj = [d_conv_in, dq, dk, dv, d_g_conv, d_g_att]
    d_w_in = _pieces_tn_matmul(d_proj, h1, name="d_w_in")
    in_slabs = _side_slabs("w_in", d_w_in)
    (theirs,) = _exchange_call("pair_swap_w_in", _pair_exchange([in_slabs]))

    early = list(opt)

    def pre_bwd_fn(*args):
        groups, (xt, dx2t), jobs, (w_in_t, g_) = args[:6], args[6:8], args[8:-2], args[-2:]
        dh = sum(jnp.dot(grp.astype(BF16), w_in_t[IN_SPLITS[n]:IN_SPLITS[n + 1]], preferred_element_type=F32)
                 for n, grp in enumerate(groups))
        _, vjp = jax.vjp(_rms, xt, g_)
        dx_, dg_ = vjp(dh)
        updates = [_sum_adamw_tile(*jobs[4 * n:4 * n + 4]) for n in range(len(early))]
        return (dx_ + dx2t, *[u for four in updates for u in four]), (dg_,)

    res = _rowwise(
        "proj_norm_bwd", pre_bwd_fn,
        d_proj + [x, dx2] + [a for nm in early for a in (received[nm], *opt[nm])], [w_in, g1],
        [_sds((s, D_MODEL))] + [_sds(opt[nm][0].shape) for nm in early for _ in range(4)],
        [_sds((1, D_MODEL))], tm=512, exchange=_chip_exchange([_pair_sum("pair_sum_w_in", in_slabs, theirs)]))
    grad_x, d_g1, received["w_in"] = res[0], res[-2], res[-1]
    updated = {nm: res[1 + 4 * n:5 + 4 * n] for n, nm in enumerate(early)}

    grads = {
        "norm_mix_pre": d_g1, "conv_dw_w": d_dw_w, "conv_dw_b": d_dw_b,
        "conv_ln_g": d_ln_g, "conv_ln_b": d_ln_b, "b_conv_branch": d_b_cb,
        "norm_mix_post": d_g2, "norm_ffn_pre": d_g3, "norm_ffn_post": d_g4,
    }
    return loss, grad_x, received, updated, grads


def _place():
    x, y, c = lax.axis_index("x"), lax.axis_index("y"), lax.axis_index("c")
    return x, y, c


def _slot(px, py, pc):
    return 4 * px + 2 * py + pc


def _exchange_scratch(n):
    return [pltpu.SemaphoreType.DMA((7 * n,)), pltpu.SemaphoreType.DMA((7 * n,)), pltpu.SemaphoreType.DMA((n,))]


GATHER_ID, SCATTER_ID, PAIR_ID, CHIP_ID = 0, 1, 2, 3


def _handshake(peers):
    barrier = pltpu.get_barrier_semaphore()
    for peer in peers:
        pl.semaphore_signal(barrier, inc=1, device_id=peer, device_id_type=MESH)
    pl.semaphore_wait(barrier, len(peers))


def _gather_exchange(arrs):
    n = len(arrs)

    def phases(ins, outs, send_sems, recv_sems, local_sems):
        x, y, c = _place()
        me, sibling = (x, y, c), (x, y, 1 - c)
        chips = [(1 - x, y), (x, 1 - y), (1 - x, 1 - y)]

        def copy(a, kk, block, to, src=None):
            dst = outs[a].at[_slot(*block)]
            return pltpu.make_async_remote_copy(
                src_ref=dst if src is None else src, dst_ref=dst,
                send_sem=send_sems.at[a * 7 + kk], recv_sem=recv_sems.at[a * 7 + kk],
                device_id=to, device_id_type=MESH)

        mine = [pltpu.make_async_copy(ins[a], outs[a].at[_slot(*me)], local_sems.at[a]) for a in range(n)]
        first = []
        for a in range(n):
            first.append(copy(a, 0, me, sibling, src=ins[a]))
            first += [copy(a, 1 + j, me, (*chip, c), src=ins[a]) for j, chip in enumerate(chips)]
        passed = [copy(a, 4 + j, (*chip, c), sibling) for j, chip in enumerate(chips) for a in range(n)]

        def send():
            _handshake([sibling] + [(*chip, c) for chip in chips])
            for cp in mine + first:
                cp.start()

        def pass_on():
            for j, chip in enumerate(chips):
                for a in range(n):
                    copy(a, 1 + j, (*chip, c), me).wait_recv()
                    passed[j * n + a].start()

        def finish():
            for a in range(n):
                copy(a, 0, sibling, me).wait_recv()
                for j, chip in enumerate(chips):
                    copy(a, 4 + j, (*chip, 1 - c), me).wait_recv()
            for cp in first + passed:
                cp.wait_send()
            for cp in mine:
                cp.wait()

        return [send, pass_on, finish]

    return list(arrs), [_sds((N_DEV,) + a.shape, a.dtype) for a in arrs], _exchange_scratch(n), phases, GATHER_ID


def _scatter_exchange(arrs):
    n = len(arrs)
    flips = [(fx, fy, fc) for fx in (0, 1) for fy in (0, 1) for fc in (0, 1)][1:]

    def phases(ins, outs, send_sems, recv_sems, local_sems):
        x, y, c = _place()
        mine = _slot(x, y, c)
        local = [pltpu.make_async_copy(ins[a].at[mine], outs[a].at[mine], local_sems.at[a]) for a in range(n)]
        peers = [((1 - x) if fx else x, (1 - y) if fy else y, (1 - c) if fc else c) for fx, fy, fc in flips]

        def copy(a, kk, src_slot, dst_slot):
            return pltpu.make_async_remote_copy(
                src_ref=ins[a].at[src_slot], dst_ref=outs[a].at[dst_slot],
                send_sem=send_sems.at[a * 7 + kk], recv_sem=recv_sems.at[a * 7 + kk],
                device_id=peers[kk], device_id_type=MESH)

        sends = [copy(a, kk, _slot(*peers[kk]), mine) for a in range(n) for kk in range(7)]

        def send():
            _handshake(peers)
            for cp in local + sends:
                cp.start()

        def finish():
            for a in range(n):
                for kk in range(7):
                    copy(a, kk, mine, _slot(*peers[kk])).wait_recv()
            for cp in sends:
                cp.wait_send()
            for cp in local:
                cp.wait()

        return [send, finish]

    return list(arrs), [_sds(a.shape, a.dtype) for a in arrs], _exchange_scratch(n), phases, SCATTER_ID


def _pair_exchange(arrs):
    n = len(arrs)

    def phases(ins, outs, send_sems, recv_sems, local_sems):
        x, y, c = _place()

        def copy(a, chip, side):
            return pltpu.make_async_remote_copy(
                src_ref=ins[a].at[chip, side], dst_ref=outs[a].at[chip],
                send_sem=send_sems.at[a * 7 + chip], recv_sem=recv_sems.at[a * 7 + chip],
                device_id=(x, y, 1 - c), device_id_type=MESH)

        sends = [copy(a, chip, 1 - c) for a in range(n) for chip in range(4)]

        def send():
            _handshake([(x, y, 1 - c)])
            for cp in sends:
                cp.start()

        def finish():
            for a in range(n):
                for chip in range(4):
                    copy(a, chip, c).wait_recv()
            for cp in sends:
                cp.wait_send()

        return [send, finish]

    return list(arrs), [_sds((4,) + a.shape[2:], a.dtype) for a in arrs], _exchange_scratch(n), phases, PAIR_ID


def _chip_exchange(arrs):
    n = len(arrs)

    def phases(ins, outs, send_sems, recv_sems, local_sems):
        x, y, c = _place()
        mine = 2 * x + y
        chips = [(1 - x, y), (x, 1 - y), (1 - x, 1 - y)]
        local = [pltpu.make_async_copy(ins[a].at[mine], outs[a].at[mine], local_sems.at[a]) for a in range(n)]

        def copy(a, j, src_slot, dst_slot):
            return pltpu.make_async_remote_copy(
                src_ref=ins[a].at[src_slot], dst_ref=outs[a].at[dst_slot],
                send_sem=send_sems.at[a * 7 + j], recv_sem=recv_sems.at[a * 7 + j],
                device_id=(*chips[j], c), device_id_type=MESH)

        sends = [copy(a, j, 2 * chips[j][0] + chips[j][1], mine) for a in range(n) for j in range(3)]

        def send():
            _handshake([(*chip, c) for chip in chips])
            for cp in local + sends:
                cp.start()

        def finish():
            for a in range(n):
                for j in range(3):
                    copy(a, j, mine, 2 * chips[j][0] + chips[j][1]).wait_recv()
            for cp in sends:
                cp.wait_send()
            for cp in local:
                cp.wait()

        return [send, finish]

    return list(arrs), [_sds(a.shape, a.dtype) for a in arrs], _exchange_scratch(n), phases, CHIP_ID


def _pair_sum(name, mine, theirs):
    _, _, r, c = mine.shape

    def body(side_ref, m_ref, t_ref, o_ref):
        o_ref[...] = (m_ref[...].astype(F32) + t_ref[...].astype(F32)).astype(o_ref.dtype)

    return pl.pallas_call(
        body, name=name,
        grid_spec=pltpu.PrefetchScalarGridSpec(
            num_scalar_prefetch=1, grid=(4,),
            in_specs=[pl.BlockSpec((None, None, r, c), lambda j, side: (j, side[0], 0, 0)),
                      pl.BlockSpec((None, r, c), lambda j, side: (j, 0, 0))],
            out_specs=pl.BlockSpec((None, r, c), lambda j, side: (j, 0, 0))),
        out_shape=_sds(theirs.shape, theirs.dtype),
        compiler_params=_params(("parallel",)),
    )(lax.axis_index("c").astype(jnp.int32).reshape(1), mine, theirs)


def _exchange_call(name, exchange):
    arrs, out_shape, scratch, phases, collective_id = exchange
    n = len(arrs)

    def body(*refs):
        for step in phases(refs[:n], refs[n:2 * n], *refs[2 * n:]):
            step()

    return pl.pallas_call(body, name=name, in_specs=[ANY] * n, out_specs=[ANY] * n,
                          out_shape=out_shape, scratch_shapes=scratch,
                          compiler_params=pltpu.CompilerParams(collective_id=collective_id))(*arrs)


def _carry_exchange(exchange, refs, n_in, n_out, first, middle, last):
    arrs, _, _, phases, _ = exchange
    n = len(arrs)
    if n == 0:
        return lambda: None
    ins = refs[n_in:n_in + n]
    outs = refs[n_in + n + n_out:n_in + 2 * n + n_out]
    sems = n_in + 2 * n + n_out
    steps = phases(ins, outs, *refs[sems:sems + 3])
    pl.when(first)(steps[0])
    if len(steps) == 3:
        pl.when(middle)(steps[1])
    return lambda: pl.when(last)(steps[-1])


def _adamw_math(w, g, m, v):
    m2 = ADAM_B1 * m + (1.0 - ADAM_B1) * g
    v2 = ADAM_B2 * v + (1.0 - ADAM_B2) * jnp.square(g)
    m_hat = m2 / (1.0 - ADAM_B1 ** ADAM_STEP)
    v_hat = v2 / (1.0 - ADAM_B2 ** ADAM_STEP)
    delta = -ADAM_LR * (m_hat / (jnp.sqrt(v_hat) + ADAM_EPS) + ADAM_WD * w)
    return delta, m2, v2


def _sum_adamw_tile(parts, w, m, v):
    g = parts[0].astype(F32)
    for d in range(1, parts.shape[0]):
        g = g + parts[d].astype(F32)
    return (g, *_adamw_math(w, g, m, v))


def _sum_adamw(name, parts, w, m, v, tr=256):
    p, r, c = parts.shape
    tr = _pick(r, tr, 16)

    def body(p_ref, w_ref, m_ref, v_ref, g_ref, d_ref, m2_ref, v2_ref):
        g_ref[...], d_ref[...], m2_ref[...], v2_ref[...] = _sum_adamw_tile(p_ref[...], w_ref[...], m_ref[...], v_ref[...])

    tile = pl.BlockSpec((tr, c), lambda i: (i, 0))
    return pl.pallas_call(
        body, name=name, grid=(r // tr,),
        in_specs=[pl.BlockSpec((p, tr, c), lambda i: (0, i, 0)), tile, tile, tile],
        out_specs=[tile] * 4, out_shape=[_sds((r, c))] * 4,
        compiler_params=_params(("parallel",)),
    )(parts, w, m, v)


def _sum_parts(name, parts):
    p, r, c = parts.shape

    def body(p_ref, o_ref):
        g = p_ref[0]
        for d in range(1, p):
            g = g + p_ref[d]
        o_ref[...] = g

    return pl.pallas_call(
        body, name=name, out_shape=_sds((r, c)),
        in_specs=[pl.BlockSpec(memory_space=pltpu.VMEM)], out_specs=pl.BlockSpec(memory_space=pltpu.VMEM),
    )(parts)


WEIGHTS = ["norm_mix_pre", "w_in", "conv_dw_w", "conv_dw_b", "conv_ln_g", "conv_ln_b", "w_conv_branch",
           "b_conv_branch", "w_att_branch", "w_out", "norm_mix_post", "norm_ffn_pre", "w_ffn_up", "w_ffn_down",
           "norm_ffn_post"]
COL_SHARDED = ["w_conv_branch", "w_att_branch"]
TRANSPOSED = ["w_in", "w_ffn_up"]
VECTORS = ["norm_mix_pre", "conv_dw_b", "conv_ln_g", "conv_ln_b", "b_conv_branch", "norm_mix_post",
           "norm_ffn_pre", "norm_ffn_post"]


def _cols_to_full(g):
    return g.transpose(1, 0, 2).reshape(g.shape[1], N_DEV * g.shape[2])


def _full_to_cols(f):
    return f.reshape(f.shape[0], N_DEV, f.shape[1] // N_DEV).transpose(1, 0, 2)


PACK_ROWS = 7


def _pack_vectors(vecs, extra=None):
    parts = [vecs[nm].reshape(-1) for nm in VECTORS]
    parts.append(jnp.zeros((1,), F32) if extra is None else extra.reshape(1))
    used = sum(p.size for p in parts)
    parts.append(jnp.zeros((PACK_ROWS * D_MODEL - used,), F32))
    return jnp.concatenate(parts).reshape(PACK_ROWS, D_MODEL)


def _unpack_vectors(packed, sizes):
    flat, out, at = packed.reshape(-1), {}, 0
    for nm in VECTORS:
        out[nm] = flat[at:at + sizes[nm]]
        at += sizes[nm]
    return out, flat[at]


def kernel(x, norm_mix_pre, w_in, conv_dw_w, conv_dw_b, conv_ln_g, conv_ln_b, w_conv_branch, b_conv_branch, w_att_branch, w_out, norm_mix_post, norm_ffn_pre, w_ffn_up, w_ffn_down, norm_ffn_post, loss_target, m_norm_mix_pre, m_w_in, m_conv_dw_w, m_conv_dw_b, m_conv_ln_g, m_conv_ln_b, m_w_conv_branch, m_b_conv_branch, m_w_att_branch, m_w_out, m_norm_mix_post, m_norm_ffn_pre, m_w_ffn_up, m_w_ffn_down, m_norm_ffn_post, v_norm_mix_pre, v_w_in, v_conv_dw_w, v_conv_dw_b, v_conv_ln_g, v_conv_ln_b, v_w_conv_branch, v_b_conv_branch, v_w_att_branch, v_w_out, v_norm_mix_post, v_norm_ffn_pre, v_w_ffn_up, v_w_ffn_down, v_norm_ffn_post):
    ws = dict(zip(WEIGHTS, [norm_mix_pre, w_in, conv_dw_w, conv_dw_b, conv_ln_g, conv_ln_b, w_conv_branch,
                            b_conv_branch, w_att_branch, w_out, norm_mix_post, norm_ffn_pre, w_ffn_up, w_ffn_down,
                            norm_ffn_post]))
    ms = dict(zip(WEIGHTS, [m_norm_mix_pre, m_w_in, m_conv_dw_w, m_conv_dw_b, m_conv_ln_g, m_conv_ln_b,
                            m_w_conv_branch, m_b_conv_branch, m_w_att_branch, m_w_out, m_norm_mix_post,
                            m_norm_ffn_pre, m_w_ffn_up, m_w_ffn_down, m_norm_ffn_post]))
    vs = dict(zip(WEIGHTS, [v_norm_mix_pre, v_w_in, v_conv_dw_w, v_conv_dw_b, v_conv_ln_g, v_conv_ln_b,
                            v_w_conv_branch, v_b_conv_branch, v_w_att_branch, v_w_out, v_norm_mix_post,
                            v_norm_ffn_pre, v_w_ffn_up, v_w_ffn_down, v_norm_ffn_post]))

    dw_block = jnp.pad(conv_dw_w, ((0, 1), (0, 0)))
    g_in, g_dw = _exchange_call("gather_first", _gather_exchange([w_in.T.astype(BF16), dw_block]))
    full = {"w_in": _full_weight("w_in", g_in), "conv_dw_w": _cols_to_full(g_dw)}
    for nm in VECTORS:
        full[nm] = ws[nm].reshape(1, -1)

    def as_kept(nm, a):
        return a.T if nm in TRANSPOSED else a

    ride_along = ["w_ffn_up", "w_out"]
    loss_local, grad_x, received, updated, grads = _local_step(
        x[0], loss_target[0], full, {nm: as_kept(nm, ws[nm]).astype(BF16) for nm in LATE},
        {nm: tuple(as_kept(nm, a[nm]) for a in (ws, ms, vs)) for nm in ride_along})

    small = _exchange_call("gather_small_grads", _gather_exchange(
        [_pack_vectors(grads, extra=loss_local), grads["conv_dw_w"]]))
    out_g, out_d, out_m, out_v = {}, {}, {}, {}
    for nm in LATE + ["w_in"]:
        res = updated[nm] if nm in updated else _sum_adamw(
            "adamw_" + nm, received[nm], *[as_kept(nm, a[nm]) for a in (ws, ms, vs)])
        out_g[nm], out_d[nm], out_m[nm], out_v[nm] = [as_kept(nm, r) for r in res]
    sizes = {nm: ws[nm].size for nm in VECTORS}
    vec = _sum_adamw("adamw_vectors", small[0], _pack_vectors(ws), _pack_vectors(ms), _pack_vectors(vs))
    for res, dst in zip(vec, (out_g, out_d, out_m, out_v)):
        dst.update(_unpack_vectors(res, sizes)[0])
    loss = _unpack_vectors(vec[0], sizes)[1]
    dw_full = _sum_parts("sum_dw_grads", small[1])
    me = _slot(*_place())
    dw_mine = lax.dynamic_slice(dw_full, (0, me * (CONV_DIM // N_DEV)), (CONV_WIDTH, CONV_DIM // N_DEV))
    nm = "conv_dw_w"
    out_g[nm], out_d[nm], out_m[nm], out_v[nm] = _sum_adamw("adamw_dw", dw_mine[None], ws[nm], ms[nm], vs[nm])

    outs = [loss, grad_x[None]]
    for group in (out_g, out_d, out_m, out_v):
        outs += [group[nm] for nm in WEIGHTS]
    return tuple(outs)
```

```python
import math

import jax
import jax.numpy as jnp
from jax import lax
from jax.experimental import pallas as pl
from jax.experimental.pallas import tpu as pltpu

F32 = jnp.float32
BF16 = jnp.bfloat16

N_DEV = 8
D_MODEL = 1024
CONV_DIM = 512
CONV_WIDTH = 31
N_HEADS = 8
HEAD_DIM = 64
ATT_DIM = N_HEADS * HEAD_DIM
D_FF = 2816
EPS = 1e-6
IN_SPLITS = (0, 1024, 1536, 2048, 2560, 3584, 4608)

ADAM_LR = 0.001
ADAM_B1 = 0.9
ADAM_B2 = 0.999
ADAM_EPS = 1e-08
ADAM_WD = 0.01
ADAM_STEP = 10

LANES = 128
SUBLANES = 8
HALO = 32
ATT_TILE = 256
ATT_PART = 176
DEAD_SUM = -120.0
VMEM_LIMIT = 56 * 1024 * 1024
MESH = pl.DeviceIdType.MESH
ANY = pl.BlockSpec(memory_space=pl.ANY)


def _pick(dim, target, align=LANES):
    t = min(dim, target)
    t -= t % align
    while t >= align:
        if dim % t == 0:
            return t
        t -= align
    return dim


def _params(semantics, collective_id=None):
    return pltpu.CompilerParams(dimension_semantics=semantics, vmem_limit_bytes=VMEM_LIMIT,
                                collective_id=collective_id)


def _tn_matmul(a, b, *, name):
    return _pieces_tn_matmul([a], b, name=name, tj=_pick(a.shape[1], 1408))


def _pieces_tn_matmul(pieces, b, *, name, tj=512):
    s, n = b.shape
    counts = [p.shape[1] // tj for p in pieces]
    starts = [sum(counts[:i]) for i in range(len(pieces))]
    assert all(p.shape == (s, c * tj) for p, c in zip(pieces, counts))

    def body(*refs):
        b_ref, o_ref = refs[len(pieces):]
        j = pl.program_id(0)
        for p_ref, first, count in zip(refs, starts, counts):
            @pl.when((j >= first) & (j < first + count))
            def _():
                o_ref[...] = lax.dot_general(p_ref[...].astype(BF16), b_ref[...], TN,
                                             preferred_element_type=F32).astype(o_ref.dtype)

    def piece_spec(first, count):
        return pl.BlockSpec((s, tj), lambda j: (0, jnp.clip(j - first, 0, count - 1)))

    return pl.pallas_call(
        body, name=name, grid=(sum(counts),),
        in_specs=[piece_spec(f, c) for f, c in zip(starts, counts)]
        + [pl.BlockSpec((s, n), lambda j: (0, 0), pipeline_mode=pl.Buffered(1))],
        out_specs=pl.BlockSpec((tj, n), lambda j: (j, 0)),
        out_shape=jax.ShapeDtypeStruct((sum(counts) * tj, n), BF16),
        compiler_params=_params(("arbitrary",)),
    )(*pieces, b)


NO_EXCHANGE = ([], [], [], None, None)


def _sweep_marks(nt):
    i = pl.program_id(0)
    return i == 0, i == nt - 1, i == nt - 1


def _rowwise(name, fn, rows, bcasts, row_outs, red_outs=(), tm=256, exchange=NO_EXCHANGE):
    s = rows[0].shape[0]
    tm = _pick(s, tm, 16)
    nt = s // tm
    resident = pl.Buffered(1)
    nr, nb, no, nd = len(rows), len(bcasts), len(row_outs), len(red_outs)
    x_arrs, x_shape, x_scratch, _, x_id = exchange
    nx = len(x_arrs)
    first_out = nr + nb + nx

    def body(*refs):
        finish_exchange = _carry_exchange(exchange, refs, nr + nb, no + nd, *_sweep_marks(nt))
        ins = [r[...] for r in refs[:nr + nb]]
        outs, reds = fn(*ins)
        for ref, val in zip(refs[first_out:first_out + no], outs):
            ref[...] = val.astype(ref.dtype)
        i = pl.program_id(0)
        for ref, val in zip(refs[first_out + no:first_out + no + nd], reds):
            @pl.when(i == 0)
            def _():
                ref[...] = val

            @pl.when(i > 0)
            def _():
                ref[...] += val
        finish_exchange()

    def row_spec(a):
        assert a.shape[-2] % nt == 0, (name, a.shape, nt)
        if len(a.shape) == 3:
            return pl.BlockSpec((a.shape[0], a.shape[1] // nt, a.shape[2]), lambda i: (0, i, 0))
        return pl.BlockSpec((a.shape[0] // nt, a.shape[1]), lambda i: (i, 0))

    in_specs = [row_spec(r) for r in rows]
    in_specs += [pl.BlockSpec(b.shape, lambda i: (0, 0), pipeline_mode=resident) for b in bcasts]
    out_specs = [row_spec(o) for o in row_outs]
    out_specs += [pl.BlockSpec(d.shape, lambda i: (0, 0)) for d in red_outs]
    return pl.pallas_call(
        body, name=name, grid=(nt,), in_specs=in_specs + [ANY] * nx, out_specs=out_specs + [ANY] * nx,
        out_shape=list(row_outs) + list(red_outs) + x_shape, scratch_shapes=x_scratch,
        compiler_params=_params(("arbitrary",), x_id),
    )(*rows, *bcasts, *x_arrs)


def _sds(shape, dtype=F32):
    return jax.ShapeDtypeStruct(shape, dtype)


def _rms(x, g):
    y = x * lax.rsqrt(jnp.mean(x * x, axis=-1, keepdims=True) + EPS)
    return y * g


def _silu(x):
    return x * jax.nn.sigmoid(x)


def _swiglu(g, u):
    return _silu(g) * u


def _ln_silu(u, g, b):
    mu = jnp.mean(u, axis=-1, keepdims=True)
    var = jnp.mean(jnp.square(u - mu), axis=-1, keepdims=True)
    return _silu((u - mu) * lax.rsqrt(var + EPS) * g + b)


def _merge(conv_pre, att_out, g_conv, g_att, b_cb):
    return jax.nn.sigmoid(g_conv) * (conv_pre + b_cb) + jax.nn.sigmoid(g_att) * att_out


def _glu(t):
    return t[:, :CONV_DIM] * jax.nn.sigmoid(t[:, CONV_DIM:])


def _shifted_reader(buf, shifted, tm):
    for b in range(1, SUBLANES):
        shifted[b - 1, :, :] = buf[pl.ds(b, tm + HALO - SUBLANES), :]

    def read(o):
        a, b = divmod(o, SUBLANES)
        return buf[pl.ds(SUBLANES * a, tm), :] if b == 0 else shifted[b - 1, pl.ds(SUBLANES * a, tm), :]

    return read


def _conv_fwd(conv_in, w_pad, b, ln_g, ln_b, exchange, tm=256):
    s = conv_in.shape[0]
    tm = _pick(s, tm, HALO)
    ratio = tm // HALO
    x_arrs, x_shape, x_scratch, _, x_id = exchange
    nx = len(x_arrs)

    def body(*refs):
        main_ref, halo_ref, w_ref, b_ref, g_ref, be_ref = refs[:6]
        u3_ref, u1_ref = refs[6 + nx:8 + nx]
        buf, shifted = refs[-2:]
        finish_exchange = _carry_exchange(exchange, refs, 6, 2, *_sweep_marks(s // tm))
        i = pl.program_id(0)
        buf[0:HALO, :] = _glu(halo_ref[...]) * (i > 0).astype(F32)
        buf[HALO:HALO + tm, :] = _glu(main_ref[...])
        read = _shifted_reader(buf, shifted, tm)
        acc = jnp.zeros((tm, CONV_DIM), F32) + b_ref[...]
        for j in range(CONV_WIDTH):
            acc = acc + w_ref[j:j + 1, :] * read(HALO - (CONV_WIDTH - 1) + j)
        u1_ref[...] = acc
        u3_ref[...] = _ln_silu(acc, g_ref[...], be_ref[...]).astype(u3_ref.dtype)
        finish_exchange()

    res = pl.pallas_call(
        body, name="conv_fwd", grid=(s // tm,),
        in_specs=[pl.BlockSpec((tm, 2 * CONV_DIM), lambda i: (i, 0)),
                  pl.BlockSpec((HALO, 2 * CONV_DIM), lambda i: (jnp.maximum(i * ratio - 1, 0), 0)),
                  pl.BlockSpec(w_pad.shape, lambda i: (0, 0)),
                  pl.BlockSpec(b.shape, lambda i: (0, 0)),
                  pl.BlockSpec(ln_g.shape, lambda i: (0, 0)),
                  pl.BlockSpec(ln_b.shape, lambda i: (0, 0))] + [ANY] * nx,
        out_specs=[pl.BlockSpec((tm, CONV_DIM), lambda i: (i, 0)),
                   pl.BlockSpec((tm, CONV_DIM), lambda i: (i, 0))] + [ANY] * nx,
        out_shape=[_sds((s, CONV_DIM), BF16), _sds((s, CONV_DIM), F32)] + x_shape,
        scratch_shapes=x_scratch + [pltpu.VMEM((tm + HALO, CONV_DIM), F32),
                                    pltpu.VMEM((SUBLANES - 1, tm + HALO - SUBLANES, CONV_DIM), F32)],
        compiler_params=_params(("arbitrary",), x_id),
    )(conv_in, conv_in, w_pad, b, ln_g, ln_b, *x_arrs)
    return res[0], res[1], res[2:]


def _conv_bwd(conv_in, u1, du3, ln_g, ln_b, w_pad, exchange, tm=256):
    s = conv_in.shape[0]
    tm = _pick(s, tm, HALO)
    ratio = tm // HALO
    nt = s // tm
    last_halo = s // HALO - 1
    x_arrs, x_shape, x_scratch, _, x_id = exchange
    nx = len(x_arrs)

    def body(*refs):
        main_ref, halo_ref, u1_ref, u1n_ref, du3_ref, du3n_ref, g_ref, be_ref, w_ref = refs[:9]
        dci_ref, dw_ref, db_ref, dg_ref, dbe_ref = refs[9 + nx:14 + nx]
        ubuf, dbuf, ushift, dshift = refs[-4:]
        finish_exchange = _carry_exchange(exchange, refs, 9, 5, *_sweep_marks(nt))
        i = pl.program_id(0)
        main = main_ref[...]
        a = main[:, :CONV_DIM]
        sb = jax.nn.sigmoid(main[:, CONV_DIM:])
        ubuf[0:HALO, :] = _glu(halo_ref[...]) * (i > 0).astype(F32)
        ubuf[HALO:HALO + tm, :] = a * sb

        def ln_bwd(u1t, du3t):
            _, vjp = jax.vjp(_ln_silu, u1t, g_ref[...], be_ref[...])
            return vjp(du3t)

        du, dg, dbe = ln_bwd(u1_ref[...], du3_ref[...])
        dbuf[0:tm, :] = du
        dbuf[tm:tm + HALO, :] = ln_bwd(u1n_ref[...], du3n_ref[...])[0] * (i < nt - 1).astype(F32)

        @pl.when(i == 0)
        def _():
            dw_ref[...] = jnp.zeros_like(dw_ref)
            db_ref[...] = jnp.zeros_like(db_ref)
            dg_ref[...] = jnp.zeros_like(dg_ref)
            dbe_ref[...] = jnp.zeros_like(dbe_ref)

        dg_ref[...] += dg
        dbe_ref[...] += dbe

        read_u = _shifted_reader(ubuf, ushift, tm)
        read_d = _shifted_reader(dbuf, dshift, tm)
        du0 = jnp.zeros((tm, CONV_DIM), F32)
        for j in range(CONV_WIDTH):
            du0 = du0 + w_ref[j:j + 1, :] * read_d(CONV_WIDTH - 1 - j)
            dw_ref[j:j + 1, :] += jnp.sum(du * read_u(HALO - (CONV_WIDTH - 1) + j), axis=0, keepdims=True)
        db_ref[...] += jnp.sum(du, axis=0, keepdims=True)
        dci_ref[:, :CONV_DIM] = (du0 * sb).astype(dci_ref.dtype)
        dci_ref[:, CONV_DIM:] = (du0 * a * sb * (1.0 - sb)).astype(dci_ref.dtype)
        finish_exchange()

    res = pl.pallas_call(
        body, name="conv_bwd", grid=(nt,),
        in_specs=[pl.BlockSpec((tm, 2 * CONV_DIM), lambda i: (i, 0)),
                  pl.BlockSpec((HALO, 2 * CONV_DIM), lambda i: (jnp.maximum(i * ratio - 1, 0), 0))]
        + [pl.BlockSpec((tm, CONV_DIM), lambda i: (i, 0)),
           pl.BlockSpec((HALO, CONV_DIM), lambda i: (jnp.minimum((i + 1) * ratio, last_halo), 0))] * 2
        + [pl.BlockSpec((1, CONV_DIM), lambda i: (0, 0))] * 2 + [pl.BlockSpec(w_pad.shape, lambda i: (0, 0))]
        + [ANY] * nx,
        out_specs=[pl.BlockSpec((tm, 2 * CONV_DIM), lambda i: (i, 0)),
                   pl.BlockSpec(w_pad.shape, lambda i: (0, 0))]
        + [pl.BlockSpec((1, CONV_DIM), lambda i: (0, 0))] * 3 + [ANY] * nx,
        out_shape=[_sds((s, 2 * CONV_DIM), BF16), _sds(w_pad.shape)] + [_sds((1, CONV_DIM))] * 3 + x_shape,
        scratch_shapes=x_scratch + [pltpu.VMEM((tm + HALO, CONV_DIM), F32)] * 2
        + [pltpu.VMEM((SUBLANES - 1, tm + HALO - SUBLANES, CONV_DIM), F32)] * 2,
        compiler_params=_params(("arbitrary",), x_id),
    )(conv_in, conv_in, u1, u1, du3, du3, ln_g, ln_b, w_pad, *x_arrs)
    return res[:5], res[5:]


def _logsig_neg(z):
    return jnp.minimum(-z, 0.0) - jnp.log(1.0 + jnp.exp(-jnp.abs(z)))


def _split_dot(val, tri):
    hi = val.astype(BF16)
    lo = (val - hi.astype(F32)).astype(BF16)
    return jnp.dot(hi, tri, preferred_element_type=F32) + jnp.dot(lo, tri, preferred_element_type=F32)


def _attn_masks(t, later):
    row = lax.broadcasted_iota(jnp.int32, (t, t), 0)
    col = lax.broadcasted_iota(jnp.int32, (t, t), 1)
    tri = jnp.where(row > col if later else row <= col, 1.0, 0.0).astype(BF16)
    return col < row, tri


def _grid_marks(h, nq):
    hh, i = pl.program_id(0), pl.program_id(1)
    return (hh == 0) & (i == 0), (hh == h - 1) & (i == nq // 2), (hh == h - 1) & (i == nq - 1)


def _head_masks(shape):
    lane = lax.broadcasted_iota(jnp.int32, shape, len(shape) - 1)
    return lane < HEAD_DIM, lane >= HEAD_DIM


def _per_head(blk):
    m0, m1 = _head_masks(blk.shape)
    zero = jnp.zeros_like(blk)
    return jnp.where(m0, blk, zero), jnp.where(m1, blk, zero)


NT = (((1,), (1,)), ((), ()))
TN = (((0,), (0,)), ((), ()))


def _with_top(whole, top):
    rows = top.shape[0]
    return top if rows == whole.shape[0] else jnp.concatenate([top, whole[rows:]], axis=0)


def _attn_fwd(q, k, v, exchange):
    s = q.shape[0]
    hp = q.shape[1] // LANES
    t = ATT_TILE
    scale = 1.0 / math.sqrt(HEAD_DIM)
    x_arrs, x_shape, x_scratch, _, x_id = exchange
    nx = len(x_arrs)

    def body(*refs):
        q_ref, k_ref, v_ref = refs[:3]
        o_ref, lt_ref, nb_ref = refs[3 + nx:6 + nx]
        finish_exchange = _carry_exchange(exchange, refs, 3, 3, *_grid_marks(hp, s // t))
        i = pl.program_id(1)
        qs = _per_head((q_ref[...].astype(F32) * scale).astype(BF16))
        causal, tri = _attn_masks(t, later=True)

        def step(kb, carry, masked, rows):
            cs, acc = carry
            off = pl.multiple_of(kb * t, t)
            kblk = k_ref[pl.ds(off, t), :]
            vs = _per_head(v_ref[pl.ds(off, t), :])
            acc_top = acc[:rows]
            new_cs = []
            for hd in range(2):
                z = lax.dot_general(qs[hd][:rows], kblk, NT, preferred_element_type=F32)
                l = _logsig_neg(z)
                if masked:
                    l = jnp.where(causal, l, 0.0)
                e = z + l + _split_dot(l, tri) + cs[hd][:rows]
                if masked:
                    e = jnp.where(causal, e, -1e30)
                acc_top = acc_top + jnp.dot(jnp.exp(e).astype(BF16), vs[hd], preferred_element_type=F32)
                new_cs.append(_with_top(cs[hd], cs[hd][:rows] + jnp.sum(l, axis=1, keepdims=True)))
            return tuple(new_cs), _with_top(acc, acc_top)

        zero = jnp.zeros((t, 1), F32)
        carry = step(i, ((zero, zero), jnp.zeros((t, LANES), F32)), True, t)

        def live(cs, lo, hi):
            return jnp.maximum(jnp.max(cs[0][lo:hi]), jnp.max(cs[1][lo:hi])) > DEAD_SUM

        def more(state):
            n, _, (cs, _) = state
            return (n < i) & live(cs, 0, t)

        def sweep(state):
            n, n_full, cr = state
            whole = live(cr[0], ATT_PART, t)
            cr = lax.cond(whole, lambda c: step(i - 1 - n, c, False, t), lambda c: step(i - 1 - n, c, False, ATT_PART), cr)
            return n + 1, n_full + whole.astype(jnp.int32), cr

        n_blocks, n_full, carry = lax.while_loop(more, sweep, (jnp.int32(0), jnp.int32(0), carry))
        m0, _ = _head_masks((t, LANES))
        lt_ref[...] = jnp.where(m0, carry[0][0], carry[0][1])
        o_ref[...] = carry[1].astype(o_ref.dtype)
        nb_ref[0, pl.program_id(0), i] = n_blocks.astype(F32)
        nb_ref[1, pl.program_id(0), i] = n_full.astype(F32)
        finish_exchange()

    res = pl.pallas_call(
        body, name="attn_fwd", grid=(hp, s // t),
        in_specs=[pl.BlockSpec((t, LANES), lambda p, i: (i, p)),
                  pl.BlockSpec((s, LANES), lambda p, i: (0, p)),
                  pl.BlockSpec((s, LANES), lambda p, i: (0, p))] + [ANY] * nx,
        out_specs=[pl.BlockSpec((t, LANES), lambda p, i: (i, p)),
                   pl.BlockSpec((None, t, LANES), lambda p, i: (p, i, 0)),
                   pl.BlockSpec(memory_space=pltpu.SMEM)] + [ANY] * nx,
        out_shape=[_sds(q.shape, BF16), _sds((hp, s, LANES), F32), _sds((2, hp, s // t), F32)] + x_shape,
        scratch_shapes=x_scratch,
        compiler_params=_params(("arbitrary", "arbitrary"), x_id),
    )(q, k, v, *x_arrs)
    return res[0], res[1], res[2], res[3:]


def _attn_bwd(q, k, v, do, ltot, n_blocks, exchange):
    s = q.shape[0]
    hp = q.shape[1] // LANES
    t = ATT_TILE
    scale = 1.0 / math.sqrt(HEAD_DIM)
    x_arrs, x_shape, x_scratch, _, x_id = exchange
    nx = len(x_arrs)

    def body(*refs):
        q_ref, k_ref, v_ref, do_ref, lt_ref, nb_ref = refs[:6]
        dq_ref, dk_ref, dv_ref = refs[6 + nx:9 + nx]
        finish_exchange = _carry_exchange(exchange, refs, 6, 3, *_grid_marks(hp, s // t))
        i = pl.program_id(1)
        n_blocks = jnp.clip(nb_ref[0, pl.program_id(0), i].astype(jnp.int32), 0, i)
        n_full = jnp.clip(nb_ref[1, pl.program_id(0), i].astype(jnp.int32), 0, n_blocks)

        @pl.when(i == 0)
        def _():
            dk_ref[...] = jnp.zeros_like(dk_ref)
            dv_ref[...] = jnp.zeros_like(dv_ref)

        qb = q_ref[...]
        qm = _per_head(qb)
        qs = _per_head((qb.astype(F32) * scale).astype(BF16))
        dos = _per_head(do_ref[...])
        lts = (lt_ref[:, 0:1], lt_ref[:, HEAD_DIM:HEAD_DIM + 1])
        causal, tri = _attn_masks(t, later=False)

        def step(kb, carry, masked, rows):
            cls, cgs, dq = carry
            off = pl.multiple_of(kb * t, t)
            kblk = k_ref[pl.ds(off, t), :]
            vblk = v_ref[pl.ds(off, t), :]
            ks = _per_head(kblk)
            dq_top = dq[:rows]
            dk = jnp.zeros((t, LANES), F32)
            dv = jnp.zeros((t, LANES), F32)
            new_cls, new_cgs = [], []
            for hd in range(2):
                z = lax.dot_general(qs[hd][:rows], kblk, NT, preferred_element_type=F32)
                l = _logsig_neg(z)
                if masked:
                    l = jnp.where(causal, l, 0.0)
                e = z + l + ((lts[hd][:rows] - cls[hd][:rows]) - _split_dot(l, tri))
                if masked:
                    e = jnp.where(causal, e, -1e30)
                a = jnp.exp(e)
                g = lax.dot_general(dos[hd][:rows], vblk, NT, preferred_element_type=F32) * a
                p = cgs[hd][:rows] + jnp.dot(g.astype(BF16), tri, preferred_element_type=F32) - g
                el = jnp.exp(l)
                dz = g * el - p * (1.0 - el)
                if masked:
                    dz = jnp.where(causal, dz, 0.0)
                dzb = (dz * scale).astype(BF16)
                dq_top = dq_top + jnp.dot(dzb, ks[hd], preferred_element_type=F32)
                dk = dk + lax.dot_general(dzb, qm[hd][:rows], TN, preferred_element_type=F32)
                dv = dv + lax.dot_general(a.astype(BF16), dos[hd][:rows], TN, preferred_element_type=F32)
                new_cls.append(_with_top(cls[hd], cls[hd][:rows] + jnp.sum(l, axis=1, keepdims=True)))
                new_cgs.append(_with_top(cgs[hd], cgs[hd][:rows] + jnp.sum(g, axis=1, keepdims=True)))
            dk_ref[pl.ds(off, t), :] += dk
            dv_ref[pl.ds(off, t), :] += dv
            return tuple(new_cls), tuple(new_cgs), _with_top(dq, dq_top)

        zero = jnp.zeros((t, 1), F32)
        init = ((zero, zero), (zero, zero), jnp.zeros((t, LANES), F32))
        carry = lax.fori_loop(i - n_blocks, i - n_full, lambda kb, cr: step(kb, cr, False, ATT_PART), init)
        carry = lax.fori_loop(i - n_full, i, lambda kb, cr: step(kb, cr, False, t), carry)
        carry = step(i, carry, True, t)
        dq_ref[...] = carry[2]
        finish_exchange()

    blk = pl.BlockSpec((t, LANES), lambda p, i: (i, p))
    whole = pl.BlockSpec((s, LANES), lambda p, i: (0, p))
    res = pl.pallas_call(
        body, name="attn_bwd", grid=(hp, s // t),
        in_specs=[blk, whole, whole, blk, pl.BlockSpec((None, t, LANES), lambda p, i: (p, i, 0)),
                  pl.BlockSpec(memory_space=pltpu.SMEM)] + [ANY] * nx,
        out_specs=[blk, whole, whole] + [ANY] * nx,
        out_shape=[_sds(q.shape)] * 3 + x_shape,
        scratch_shapes=x_scratch,
        compiler_params=_params(("arbitrary", "arbitrary"), x_id),
    )(q, k, v, do, ltot, n_blocks, *x_arrs)
    return res[0], res[1], res[2], res[3:]


LATE = ["w_conv_branch", "w_att_branch", "w_out", "w_ffn_up", "w_ffn_down"]


def _full_weight(name, gathered):
    return _cols_to_full(gathered) if name in COL_SHARDED else gathered.reshape(-1, gathered.shape[2])


def _grad_slabs(name, grad):
    return _full_to_cols(grad) if name in COL_SHARDED else grad.reshape(N_DEV, -1, grad.shape[1])


def _side_slabs(name, grad):
    slabs = _grad_slabs(name, grad)
    return slabs.reshape((4, 2) + slabs.shape[1:])


def _local_step(x, target, w, late_blocks, opt):
    s = x.shape[0]
    w = dict(w)
    g1, g2, g3, g4 = w["norm_mix_pre"], w["norm_mix_post"], w["norm_ffn_pre"], w["norm_ffn_post"]

    w_in = w["w_in"]

    def proj_fn(xt, g1_, w_in_t):
        h = _rms(xt, g1_).astype(BF16)
        proj = lax.dot_general(h, w_in_t, NT, preferred_element_type=F32)
        return (h, *[proj[:, IN_SPLITS[n]:IN_SPLITS[n + 1]] for n in range(6)]), ()

    mix_weights = ["w_conv_branch", "w_att_branch", "w_out"]
    h1, conv_in, q, k, v, g_conv, g_att = _rowwise(
        "norm_proj", proj_fn, [x], [g1, w_in],
        [_sds((s, D_MODEL), BF16), _sds((s, 2 * CONV_DIM)), _sds((s, ATT_DIM), BF16), _sds((s, ATT_DIM), BF16),
         _sds((s, ATT_DIM), BF16), _sds((s, D_MODEL), BF16), _sds((s, D_MODEL), BF16)], tm=512)

    u3, u1, gathered = _conv_fwd(conv_in, w["conv_dw_w"], w["conv_dw_b"], w["conv_ln_g"], w["conv_ln_b"],
                                 _gather_exchange([late_blocks[nm] for nm in mix_weights]))
    for nm, g in zip(mix_weights, gathered):
        w[nm] = _full_weight(nm, g)
    att, ltot, n_blocks, (g_up,) = _attn_fwd(q, k, v, _gather_exchange([late_blocks["w_ffn_up"]]))
    w["w_ffn_up"] = _full_weight("w_ffn_up", g_up)

    def merge_fn(u3t, at, gc, ga, xt, w_cb, w_ab, b_cb, w_out, g2_, g3_):
        cp = jnp.dot(u3t, w_cb, preferred_element_type=F32)
        ao = jnp.dot(at, w_ab, preferred_element_type=F32)
        mg = _merge(cp, ao, gc.astype(F32), ga.astype(F32), b_cb).astype(BF16)
        mix_ = jnp.dot(mg, w_out, preferred_element_type=F32)
        x2_ = xt + _rms(mix_, g2_)
        return (mg, cp, ao, mix_, x2_, _rms(x2_, g3_)), ()

    half = D_MODEL // 2
    down_block = late_blocks["w_ffn_down"]
    merged, conv_pre, att_out, mix, x2, h2, g_left = _rowwise(
        "branch_merge_mix", merge_fn, [u3, att, g_conv, g_att, x],
        [w["w_conv_branch"], w["w_att_branch"], w["b_conv_branch"], w["w_out"], g2, g3],
        [_sds((s, D_MODEL), BF16)] * 3 + [_sds((s, D_MODEL)), _sds((s, D_MODEL)), _sds((s, D_MODEL), BF16)], tm=512,
        exchange=_gather_exchange([down_block[:, :half]]))

    def ffn_up_fn(ht, w_up_t):
        gu_ = lax.dot_general(ht, w_up_t, NT, preferred_element_type=F32)
        return (gu_, _swiglu(gu_[:, :D_FF], gu_[:, D_FF:])), ()

    gu, act, g_right = _rowwise("ffn_up", ffn_up_fn, [h2], [w["w_ffn_up"]],
                                [_sds((s, 2 * D_FF), BF16), _sds((s, D_FF), BF16)], tm=512,
                                exchange=_gather_exchange([down_block[:, half:]]))
    w_down = [_full_weight("w_ffn_down", g) for g in (g_left, g_right)]

    def final_fn(at, x2t, tgt, w_left, w_right, g4_):
        ff = jnp.concatenate([jnp.dot(at, w_left, preferred_element_type=F32),
                              jnp.dot(at, w_right, preferred_element_type=F32)], axis=1)
        n4, vjp = jax.vjp(_rms, ff, g4_)
        err = x2t + n4 - tgt
        dy = err * (1.0 / D_MODEL)
        dff, dg4 = vjp(dy)
        return (dy, dff), (jnp.sum(err * err, axis=0, keepdims=True), dg4)

    dy, dff, loss_cols, d_g4 = _rowwise("ffn_down_loss", final_fn, [act, x2, target], [*w_down, g4],
                                        [_sds((s, D_MODEL)), _sds((s, D_MODEL), BF16)],
                                        [_sds((1, D_MODEL)), _sds((1, D_MODEL))], tm=512)
    loss = 0.5 * jnp.sum(loss_cols) / D_MODEL

    d_w_down = _tn_matmul(act, dff, name="d_w_down")

    def act_bwd_fn(dfft, gut, w_left, w_right):
        d_act = (lax.dot_general(dfft[:, :half], w_left, NT, preferred_element_type=F32)
                 + lax.dot_general(dfft[:, half:], w_right, NT, preferred_element_type=F32))
        gu_ = gut.astype(F32)
        _, vjp = jax.vjp(_swiglu, gu_[:, :D_FF], gu_[:, D_FF:])
        return (jnp.concatenate(vjp(d_act), axis=1),), ()

    down_slabs = _side_slabs("w_ffn_down", d_w_down)
    dgu, theirs = _rowwise("ffn_act_bwd", act_bwd_fn, [dff, gu], w_down, [_sds((s, 2 * D_FF), BF16)],
                           exchange=_pair_exchange([down_slabs]))
    down_sums = _pair_sum("pair_sum_w_ffn_down", down_slabs, theirs)
    d_w_up = _tn_matmul(dgu, h2, name="d_w_up")
    received = {}
    up_slabs = _side_slabs("w_ffn_up", d_w_up)

    def mid_bwd_fn(dgut, xt, mt, dyt, w_up_t, g2_, g3_):
        dh = jnp.dot(dgut, w_up_t, preferred_element_type=F32)
        n2, vjp2 = jax.vjp(_rms, mt, g2_)
        x2_ = xt + n2
        _, vjp3 = jax.vjp(_rms, x2_, g3_)
        dx2_, dg3 = vjp3(dh)
        dx2_ = dx2_ + dyt
        dmix_, dg2 = vjp2(dx2_)
        return (dx2_, dmix_), (dg2, dg3)

    dx2, dmix, d_g2, d_g3, received["w_ffn_down"] = _rowwise(
        "ffn_up_mid_bwd", mid_bwd_fn, [dgu, x, mix, dy], [w["w_ffn_up"], g2, g3],
        [_sds((s, D_MODEL)), _sds((s, D_MODEL), BF16)], [_sds((1, D_MODEL)), _sds((1, D_MODEL))], tm=512,
        exchange=_chip_exchange([down_sums]))
    d_w_out = _tn_matmul(merged, dmix, name="d_w_out")

    def merge_bwd_fn(dmt, cp, ao, gc, ga, w_out, w_cb, w_ab, b_cb):
        dm = lax.dot_general(dmt, w_out, NT, preferred_element_type=F32)
        _, vjp = jax.vjp(_merge, cp.astype(F32), ao.astype(F32), gc.astype(F32), ga.astype(F32), b_cb)
        dcp, dao, dgc, dga, dbias = vjp(dm)
        dcp, dao = dcp.astype(BF16), dao.astype(BF16)
        du3_ = lax.dot_general(dcp, w_cb, NT, preferred_element_type=F32)
        datt_ = lax.dot_general(dao, w_ab, NT, preferred_element_type=F32)
        return (dcp, dao, dgc, dga, du3_, datt_), (dbias,)

    d_conv_out, d_att_out, d_g_conv, d_g_att, du3, d_att, d_b_cb, theirs = _rowwise(
        "merge_bwd", merge_bwd_fn, [dmix, conv_pre, att_out, g_conv, g_att],
        [w["w_out"], w["w_conv_branch"], w["w_att_branch"], w["b_conv_branch"]],
        [_sds((s, D_MODEL), BF16)] * 4 + [_sds((s, CONV_DIM)), _sds((s, ATT_DIM), BF16)], [_sds((1, D_MODEL))], tm=512,
        exchange=_pair_exchange([up_slabs]))

    d_w_cb = _tn_matmul(u3, d_conv_out, name="d_w_conv_branch")
    d_w_ab = _tn_matmul(att, d_att_out, name="d_w_att_branch")

    dq, dk, dv, (received["w_ffn_up"],) = _attn_bwd(
        q, k, v, d_att, ltot, n_blocks, _chip_exchange([_pair_sum("pair_sum_w_ffn_up", up_slabs, theirs)]))

    mix_grads = {"w_conv_branch": d_w_cb, "w_att_branch": d_w_ab, "w_out": d_w_out}
    (d_conv_in, d_dw_w, d_dw_b, d_ln_g, d_ln_b), landed = _conv_bwd(
        conv_in, u1, du3, w["conv_ln_g"], w["conv_ln_b"], w["conv_dw_w"],
        _scatter_exchange([_grad_slabs(nm, mix_grads[nm]) for nm in mix_weights]))
    received.update(zip(mix_weights, landed))

    d_proj = [d_conv_in, dq, dk, dv, d_g_conv, d_g_att]
    d_w_in = _pieces_tn_matmul(d_proj, h1, name="d_w_in")
    in_slabs = _side_slabs("w_in", d_w_in)
    (theirs,) = _exchange_call("pair_swap_w_in", _pair_exchange([in_slabs]))

    early = list(opt)

    def pre_bwd_fn(*args):
        groups, (xt, dx2t), jobs, (w_in_t, g_) = args[:6], args[6:8], args[8:-2], args[-2:]
        dh = sum(jnp.dot(grp.astype(BF16), w_in_t[IN_SPLITS[n]:IN_SPLITS[n + 1]], preferred_element_type=F32)
                 for n, grp in enumerate(groups))
        _, vjp = jax.vjp(_rms, xt, g_)
        dx_, dg_ = vjp(dh)
        updates = [_sum_adamw_tile(*jobs[4 * n:4 * n + 4]) for n in range(len(early))]
        return (dx_ + dx2t, *[u for four in updates for u in four]), (dg_,)

    res = _rowwise(
        "proj_norm_bwd", pre_bwd_fn,
        d_proj + [x, dx2] + [a for nm in early for a in (received[nm], *opt[nm])], [w_in, g1],
        [_sds((s, D_MODEL))] + [_sds(opt[nm][0].shape) for nm in early for _ in range(4)],
        [_sds((1, D_MODEL))], tm=512, exchange=_chip_exchange([_pair_sum("pair_sum_w_in", in_slabs, theirs)]))
    grad_x, d_g1, received["w_in"] = res[0], res[-2], res[-1]
    updated = {nm: res[1 + 4 * n:5 + 4 * n] for n, nm in enumerate(early)}

    grads = {
        "norm_mix_pre": d_g1, "conv_dw_w": d_dw_w, "conv_dw_b": d_dw_b,
        "conv_ln_g": d_ln_g, "conv_ln_b": d_ln_b, "b_conv_branch": d_b_cb,
        "norm_mix_post": d_g2, "norm_ffn_pre": d_g3, "norm_ffn_post": d_g4,
    }
    return loss, grad_x, received, updated, grads


def _place():
    x, y, c = lax.axis_index("x"), lax.axis_index("y"), lax.axis_index("c")
    return x, y, c


def _slot(px, py, pc):
    return 4 * px + 2 * py + pc


def _exchange_scratch(n):
    return [pltpu.SemaphoreType.DMA((7 * n,)), pltpu.SemaphoreType.DMA((7 * n,)), pltpu.SemaphoreType.DMA((n,))]


GATHER_ID, SCATTER_ID, PAIR_ID, CHIP_ID = 0, 1, 2, 3


def _handshake(peers):
    barrier = pltpu.get_barrier_semaphore()
    for peer in peers:
        pl.semaphore_signal(barrier, inc=1, device_id=peer, device_id_type=MESH)
    pl.semaphore_wait(barrier, len(peers))


def _gather_exchange(arrs):
    n = len(arrs)

    def phases(ins, outs, send_sems, recv_sems, local_sems):
        x, y, c = _place()
        me, sibling = (x, y, c), (x, y, 1 - c)
        chips = [(1 - x, y), (x, 1 - y), (1 - x, 1 - y)]

        def copy(a, kk, block, to, src=None):
            dst = outs[a].at[_slot(*block)]
            return pltpu.make_async_remote_copy(
                src_ref=dst if src is None else src, dst_ref=dst,
                send_sem=send_sems.at[a * 7 + kk], recv_sem=recv_sems.at[a * 7 + kk],
                device_id=to, device_id_type=MESH)

        mine = [pltpu.make_async_copy(ins[a], outs[a].at[_slot(*me)], local_sems.at[a]) for a in range(n)]
        first = []
        for a in range(n):
            first.append(copy(a, 0, me, sibling, src=ins[a]))
            first += [copy(a, 1 + j, me, (*chip, c), src=ins[a]) for j, chip in enumerate(chips)]
        passed = [copy(a, 4 + j, (*chip, c), sibling) for j, chip in enumerate(chips) for a in range(n)]

        def send():
            _handshake([sibling] + [(*chip, c) for chip in chips])
            for cp in mine + first:
                cp.start()

        def pass_on():
            for j, chip in enumerate(chips):
                for a in range(n):
                    copy(a, 1 + j, (*chip, c), me).wait_recv()
                    passed[j * n + a].start()

        def finish():
            for a in range(n):
                copy(a, 0, sibling, me).wait_recv()
                for j, chip in enumerate(chips):
                    copy(a, 4 + j, (*chip, 1 - c), me).wait_recv()
            for cp in first + passed:
                cp.wait_send()
            for cp in mine:
                cp.wait()

        return [send, pass_on, finish]

    return list(arrs), [_sds((N_DEV,) + a.shape, a.dtype) for a in arrs], _exchange_scratch(n), phases, GATHER_ID


def _scatter_exchange(arrs):
    n = len(arrs)
    flips = [(fx, fy, fc) for fx in (0, 1) for fy in (0, 1) for fc in (0, 1)][1:]

    def phases(ins, outs, send_sems, recv_sems, local_sems):
        x, y, c = _place()
        mine = _slot(x, y, c)
        local = [pltpu.make_async_copy(ins[a].at[mine], outs[a].at[mine], local_sems.at[a]) for a in range(n)]
        peers = [((1 - x) if fx else x, (1 - y) if fy else y, (1 - c) if fc else c) for fx, fy, fc in flips]

        def copy(a, kk, src_slot, dst_slot):
            return pltpu.make_async_remote_copy(
                src_ref=ins[a].at[src_slot], dst_ref=outs[a].at[dst_slot],
                send_sem=send_sems.at[a * 7 + kk], recv_sem=recv_sems.at[a * 7 + kk],
                device_id=peers[kk], device_id_type=MESH)

        sends = [copy(a, kk, _slot(*peers[kk]), mine) for a in range(n) for kk in range(7)]

        def send():
            _handshake(peers)
            for cp in local + sends:
                cp.start()

        def finish():
            for a in range(n):
                for kk in range(7):
                    copy(a, kk, mine, _slot(*peers[kk])).wait_recv()
            for cp in sends:
                cp.wait_send()
            for cp in local:
                cp.wait()

        return [send, finish]

    return list(arrs), [_sds(a.shape, a.dtype) for a in arrs], _exchange_scratch(n), phases, SCATTER_ID


def _pair_exchange(arrs):
    n = len(arrs)

    def phases(ins, outs, send_sems, recv_sems, local_sems):
        x, y, c = _place()

        def copy(a, chip, side):
            return pltpu.make_async_remote_copy(
                src_ref=ins[a].at[chip, side], dst_ref=outs[a].at[chip],
                send_sem=send_sems.at[a * 7 + chip], recv_sem=recv_sems.at[a * 7 + chip],
                device_id=(x, y, 1 - c), device_id_type=MESH)

        sends = [copy(a, chip, 1 - c) for a in range(n) for chip in range(4)]

        def send():
            _handshake([(x, y, 1 - c)])
            for cp in sends:
                cp.start()

        def finish():
            for a in range(n):
                for chip in range(4):
                    copy(a, chip, c).wait_recv()
            for cp in sends:
                cp.wait_send()

        return [send, finish]

    return list(arrs), [_sds((4,) + a.shape[2:], a.dtype) for a in arrs], _exchange_scratch(n), phases, PAIR_ID


def _chip_exchange(arrs):
    n = len(arrs)

    def phases(ins, outs, send_sems, recv_sems, local_sems):
        x, y, c = _place()
        mine = 2 * x + y
        chips = [(1 - x, y), (x, 1 - y), (1 - x, 1 - y)]
        local = [pltpu.make_async_copy(ins[a].at[mine], outs[a].at[mine], local_sems.at[a]) for a in range(n)]

        def copy(a, j, src_slot, dst_slot):
            return pltpu.make_async_remote_copy(
                src_ref=ins[a].at[src_slot], dst_ref=outs[a].at[dst_slot],
                send_sem=send_sems.at[a * 7 + j], recv_sem=recv_sems.at[a * 7 + j],
                device_id=(*chips[j], c), device_id_type=MESH)

        sends = [copy(a, j, 2 * chips[j][0] + chips[j][1], mine) for a in range(n) for j in range(3)]

        def send():
            _handshake([(*chip, c) for chip in chips])
            for cp in local + sends:
                cp.start()

        def finish():
            for a in range(n):
                for j in range(3):
                    copy(a, j, mine, 2 * chips[j][0] + chips[j][1]).wait_recv()
            for cp in sends:
                cp.wait_send()
            for cp in local:
                cp.wait()

        return [send, finish]

    return list(arrs), [_sds(a.shape, a.dtype) for a in arrs], _exchange_scratch(n), phases, CHIP_ID


def _pair_sum(name, mine, theirs):
    _, _, r, c = mine.shape

    def body(side_ref, m_ref, t_ref, o_ref):
        o_ref[...] = (m_ref[...].astype(F32) + t_ref[...].astype(F32)).astype(o_ref.dtype)

    return pl.pallas_call(
        body, name=name,
        grid_spec=pltpu.PrefetchScalarGridSpec(
            num_scalar_prefetch=1, grid=(4,),
            in_specs=[pl.BlockSpec((None, None, r, c), lambda j, side: (j, side[0], 0, 0)),
                      pl.BlockSpec((None, r, c), lambda j, side: (j, 0, 0))],
            out_specs=pl.BlockSpec((None, r, c), lambda j, side: (j, 0, 0))),
        out_shape=_sds(theirs.shape, theirs.dtype),
        compiler_params=_params(("parallel",)),
    )(lax.axis_index("c").astype(jnp.int32).reshape(1), mine, theirs)


def _exchange_call(name, exchange):
    arrs, out_shape, scratch, phases, collective_id = exchange
    n = len(arrs)

    def body(*refs):
        for step in phases(refs[:n], refs[n:2 * n], *refs[2 * n:]):
            step()

    return pl.pallas_call(body, name=name, in_specs=[ANY] * n, out_specs=[ANY] * n,
                          out_shape=out_shape, scratch_shapes=scratch,
                          compiler_params=pltpu.CompilerParams(collective_id=collective_id))(*arrs)


def _carry_exchange(exchange, refs, n_in, n_out, first, middle, last):
    arrs, _, _, phases, _ = exchange
    n = len(arrs)
    if n == 0:
        return lambda: None
    ins = refs[n_in:n_in + n]
    outs = refs[n_in + n + n_out:n_in + 2 * n + n_out]
    sems = n_in + 2 * n + n_out
    steps = phases(ins, outs, *refs[sems:sems + 3])
    pl.when(first)(steps[0])
    if len(steps) == 3:
        pl.when(middle)(steps[1])
    return lambda: pl.when(last)(steps[-1])


def _adamw_math(w, g, m, v):
    m2 = ADAM_B1 * m + (1.0 - ADAM_B1) * g
    v2 = ADAM_B2 * v + (1.0 - ADAM_B2) * jnp.square(g)
    m_hat = m2 / (1.0 - ADAM_B1 ** ADAM_STEP)
    v_hat = v2 / (1.0 - ADAM_B2 ** ADAM_STEP)
    delta = -ADAM_LR * (m_hat / (jnp.sqrt(v_hat) + ADAM_EPS) + ADAM_WD * w)
    return delta, m2, v2


def _sum_adamw_tile(parts, w, m, v):
    g = parts[0].astype(F32)
    for d in range(1, parts.shape[0]):
        g = g + parts[d].astype(F32)
    return (g, *_adamw_math(w, g, m, v))


def _sum_adamw(name, parts, w, m, v, tr=256):
    p, r, c = parts.shape
    tr = _pick(r, tr, 16)

    def body(p_ref, w_ref, m_ref, v_ref, g_ref, d_ref, m2_ref, v2_ref):
        g_ref[...], d_ref[...], m2_ref[...], v2_ref[...] = _sum_adamw_tile(p_ref[...], w_ref[...], m_ref[...], v_ref[...])

    tile = pl.BlockSpec((tr, c), lambda i: (i, 0))
    return pl.pallas_call(
        body, name=name, grid=(r // tr,),
        in_specs=[pl.BlockSpec((p, tr, c), lambda i: (0, i, 0)), tile, tile, tile],
        out_specs=[tile] * 4, out_shape=[_sds((r, c))] * 4,
        compiler_params=_params(("parallel",)),
    )(parts, w, m, v)


def _sum_parts(name, parts):
    p, r, c = parts.shape

    def body(p_ref, o_ref):
        g = p_ref[0]
        for d in range(1, p):
            g = g + p_ref[d]
        o_ref[...] = g

    return pl.pallas_call(
        body, name=name, out_shape=_sds((r, c)),
        in_specs=[pl.BlockSpec(memory_space=pltpu.VMEM)], out_specs=pl.BlockSpec(memory_space=pltpu.VMEM),
    )(parts)


WEIGHTS = ["norm_mix_pre", "w_in", "conv_dw_w", "conv_dw_b", "conv_ln_g", "conv_ln_b", "w_conv_branch",
           "b_conv_branch", "w_att_branch", "w_out", "norm_mix_post", "norm_ffn_pre", "w_ffn_up", "w_ffn_down",
           "norm_ffn_post"]
COL_SHARDED = ["w_conv_branch", "w_att_branch"]
TRANSPOSED = ["w_in", "w_ffn_up"]
VECTORS = ["norm_mix_pre", "conv_dw_b", "conv_ln_g", "conv_ln_b", "b_conv_branch", "norm_mix_post",
           "norm_ffn_pre", "norm_ffn_post"]


def _cols_to_full(g):
    return g.transpose(1, 0, 2).reshape(g.shape[1], N_DEV * g.shape[2])


def _full_to_cols(f):
    return f.reshape(f.shape[0], N_DEV, f.shape[1] // N_DEV).transpose(1, 0, 2)


PACK_ROWS = 7


def _pack_vectors(vecs, extra=None):
    parts = [vecs[nm].reshape(-1) for nm in VECTORS]
    parts.append(jnp.zeros((1,), F32) if extra is None else extra.reshape(1))
    used = sum(p.size for p in parts)
    parts.append(jnp.zeros((PACK_ROWS * D_MODEL - used,), F32))
    return jnp.concatenate(parts).reshape(PACK_ROWS, D_MODEL)


def _unpack_vectors(packed, sizes):
    flat, out, at = packed.reshape(-1), {}, 0
    for nm in VECTORS:
        out[nm] = flat[at:at + sizes[nm]]
        at += sizes[nm]
    return out, flat[at]


def kernel(x, norm_mix_pre, w_in, conv_dw_w, conv_dw_b, conv_ln_g, conv_ln_b, w_conv_branch, b_conv_branch, w_att_branch, w_out, norm_mix_post, norm_ffn_pre, w_ffn_up, w_ffn_down, norm_ffn_post, loss_target, m_norm_mix_pre, m_w_in, m_conv_dw_w, m_conv_dw_b, m_conv_ln_g, m_conv_ln_b, m_w_conv_branch, m_b_conv_branch, m_w_att_branch, m_w_out, m_norm_mix_post, m_norm_ffn_pre, m_w_ffn_up, m_w_ffn_down, m_norm_ffn_post, v_norm_mix_pre, v_w_in, v_conv_dw_w, v_conv_dw_b, v_conv_ln_g, v_conv_ln_b, v_w_conv_branch, v_b_conv_branch, v_w_att_branch, v_w_out, v_norm_mix_post, v_norm_ffn_pre, v_w_ffn_up, v_w_ffn_down, v_norm_ffn_post):
    ws = dict(zip(WEIGHTS, [norm_mix_pre, w_in, conv_dw_w, conv_dw_b, conv_ln_g, conv_ln_b, w_conv_branch,
                            b_conv_branch, w_att_branch, w_out, norm_mix_post, norm_ffn_pre, w_ffn_up, w_ffn_down,
                            norm_ffn_post]))
    ms = dict(zip(WEIGHTS, [m_norm_mix_pre, m_w_in, m_conv_dw_w, m_conv_dw_b, m_conv_ln_g, m_conv_ln_b,
                            m_w_conv_branch, m_b_conv_branch, m_w_att_branch, m_w_out, m_norm_mix_post,
                            m_norm_ffn_pre, m_w_ffn_up, m_w_ffn_down, m_norm_ffn_post]))
    vs = dict(zip(WEIGHTS, [v_norm_mix_pre, v_w_in, v_conv_dw_w, v_conv_dw_b, v_conv_ln_g, v_conv_ln_b,
                            v_w_conv_branch, v_b_conv_branch, v_w_att_branch, v_w_out, v_norm_mix_post,
                            v_norm_ffn_pre, v_w_ffn_up, v_w_ffn_down, v_norm_ffn_post]))

    dw_block = jnp.pad(conv_dw_w, ((0, 1), (0, 0)))
    g_in, g_dw = _exchange_call("gather_first", _gather_exchange([w_in.T.astype(BF16), dw_block]))
    full = {"w_in": _full_weight("w_in", g_in), "conv_dw_w": _cols_to_full(g_dw)}
    for nm in VECTORS:
        full[nm] = ws[nm].reshape(1, -1)

    def as_kept(nm, a):
        return a.T if nm in TRANSPOSED else a

    ride_along = ["w_ffn_up", "w_out"]
    loss_local, grad_x, received, updated, grads = _local_step(
        x[0], loss_target[0], full, {nm: as_kept(nm, ws[nm]).astype(BF16) for nm in LATE},
        {nm: tuple(as_kept(nm, a[nm]) for a in (ws, ms, vs)) for nm in ride_along})

    small = _exchange_call("gather_small_grads", _gather_exchange(
        [_pack_vectors(grads, extra=loss_local), grads["conv_dw_w"]]))
    out_g, out_d, out_m, out_v = {}, {}, {}, {}
    for nm in LATE + ["w_in"]:
        res = updated[nm] if nm in updated else _sum_adamw(
            "adamw_" + nm, received[nm], *[as_kept(nm, a[nm]) for a in (ws, ms, vs)])
        out_g[nm], out_d[nm], out_m[nm], out_v[nm] = [as_kept(nm, r) for r in res]
    sizes = {nm: ws[nm].size for nm in VECTORS}
    vec = _sum_adamw("adamw_vectors", small[0], _pack_vectors(ws), _pack_vectors(ms), _pack_vectors(vs))
    for res, dst in zip(vec, (out_g, out_d, out_m, out_v)):
        dst.update(_unpack_vectors(res, sizes)[0])
    loss = _unpack_vectors(vec[0], sizes)[1]
    dw_full = _sum_parts("sum_dw_grads", small[1])
    me = _slot(*_place())
    dw_mine = lax.dynamic_slice(dw_full, (0, me * (CONV_DIM // N_DEV)), (CONV_WIDTH, CONV_DIM // N_DEV))
    nm = "conv_dw_w"
    out_g[nm], out_d[nm], out_m[nm], out_v[nm] = _sum_adamw("adamw_dw", dw_mine[None], ws[nm], ms[nm], vs[nm])

    outs = [loss, grad_x[None]]
    for group in (out_g, out_d, out_m, out_v):
        outs += [group[nm] for nm in WEIGHTS]
    return tuple(outs)
```

```python
import math

import jax
import jax.numpy as jnp
from jax import lax
from jax.experimental import pallas as pl
from jax.experimental.pallas import tpu as pltpu

F32 = jnp.float32
BF16 = jnp.bfloat16

N_DEV = 8
D_MODEL = 1024
CONV_DIM = 512
CONV_WIDTH = 31
N_HEADS = 8
HEAD_DIM = 64
ATT_DIM = N_HEADS * HEAD_DIM
D_FF = 2816
EPS = 1e-6
IN_SPLITS = (0, 1024, 1536, 2048, 2560, 3584, 4608)

ADAM_LR = 0.001
ADAM_B1 = 0.9
ADAM_B2 = 0.999
ADAM_EPS = 1e-08
ADAM_WD = 0.01
ADAM_STEP = 10

LANES = 128
SUBLANES = 8
HALO = 32
ATT_TILE = 256
ATT_PART = 176
DEAD_SUM = -120.0
VMEM_LIMIT = 56 * 1024 * 1024
MESH = pl.DeviceIdType.MESH
ANY = pl.BlockSpec(memory_space=pl.ANY)


def _pick(dim, target, align=LANES):
    t = min(dim, target)
    t -= t % align
    while t >= align:
        if dim % t == 0:
            return t
        t -= align
    return dim


def _params(semantics, collective_id=None):
    return pltpu.CompilerParams(dimension_semantics=semantics, vmem_limit_bytes=VMEM_LIMIT,
                                collective_id=collective_id)


def _tn_matmul(a, b, *, name):
    return _pieces_tn_matmul([a], b, name=name, tj=_pick(a.shape[1], 1408))


def _pieces_tn_matmul(pieces, b, *, name, tj=512):
    s, n = b.shape
    counts = [p.shape[1] // tj for p in pieces]
    starts = [sum(counts[:i]) for i in range(len(pieces))]
    assert all(p.shape == (s, c * tj) for p, c in zip(pieces, counts))

    def body(*refs):
        b_ref, o_ref = refs[len(pieces):]
        j = pl.program_id(0)
        for p_ref, first, count in zip(refs, starts, counts):
            @pl.when((j >= first) & (j < first + count))
            def _():
                o_ref[...] = lax.dot_general(p_ref[...].astype(BF16), b_ref[...], TN,
                                             preferred_element_type=F32).astype(o_ref.dtype)

    def piece_spec(first, count):
        return pl.BlockSpec((s, tj), lambda j: (0, jnp.clip(j - first, 0, count - 1)))

    return pl.pallas_call(
        body, name=name, grid=(sum(counts),),
        in_specs=[piece_spec(f, c) for f, c in zip(starts, counts)]
        + [pl.BlockSpec((s, n), lambda j: (0, 0), pipeline_mode=pl.Buffered(1))],
        out_specs=pl.BlockSpec((tj, n), lambda j: (j, 0)),
        out_shape=jax.ShapeDtypeStruct((sum(counts) * tj, n), BF16),
        compiler_params=_params(("arbitrary",)),
    )(*pieces, b)


NO_EXCHANGE = ([], [], [], None, None)


def _sweep_marks(nt):
    i = pl.program_id(0)
    return i == 0, i == nt - 1, i == nt - 1


def _rowwise(name, fn, rows, bcasts, row_outs, red_outs=(), tm=256, exchange=NO_EXCHANGE):
    s = rows[0].shape[0]
    tm = _pick(s, tm, 16)
    nt = s // tm
    resident = pl.Buffered(1)
    nr, nb, no, nd = len(rows), len(bcasts), len(row_outs), len(red_outs)
    x_arrs, x_shape, x_scratch, _, x_id = exchange
    nx = len(x_arrs)
    first_out = nr + nb + nx

    def body(*refs):
        finish_exchange = _carry_exchange(exchange, refs, nr + nb, no + nd, *_sweep_marks(nt))
        ins = [r[...] for r in refs[:nr + nb]]
        outs, reds = fn(*ins)
        for ref, val in zip(refs[first_out:first_out + no], outs):
            ref[...] = val.astype(ref.dtype)
        i = pl.program_id(0)
        for ref, val in zip(refs[first_out + no:first_out + no + nd], reds):
            @pl.when(i == 0)
            def _():
                ref[...] = val

            @pl.when(i > 0)
            def _():
                ref[...] += val
        finish_exchange()

    def row_spec(a):
        assert a.shape[-2] % nt == 0, (name, a.shape, nt)
        if len(a.shape) == 3:
            return pl.BlockSpec((a.shape[0], a.shape[1] // nt, a.shape[2]), lambda i: (0, i, 0))
        return pl.BlockSpec((a.shape[0] // nt, a.shape[1]), lambda i: (i, 0))

    in_specs = [row_spec(r) for r in rows]
    in_specs += [pl.BlockSpec(b.shape, lambda i: (0, 0), pipeline_mode=resident) for b in bcasts]
    out_specs = [row_spec(o) for o in row_outs]
    out_specs += [pl.BlockSpec(d.shape, lambda i: (0, 0)) for d in red_outs]
    return pl.pallas_call(
        body, name=name, grid=(nt,), in_specs=in_specs + [ANY] * nx, out_specs=out_specs + [ANY] * nx,
        out_shape=list(row_outs) + list(red_outs) + x_shape, scratch_shapes=x_scratch,
        compiler_params=_params(("arbitrary",), x_id),
    )(*rows, *bcasts, *x_arrs)


def _sds(shape, dtype=F32):
    return jax.ShapeDtypeStruct(shape, dtype)


def _rms(x, g):
    y = x * lax.rsqrt(jnp.mean(x * x, axis=-1, keepdims=True) + EPS)
    return y * g


def _silu(x):
    return x * jax.nn.sigmoid(x)


def _swiglu(g, u):
    return _silu(g) * u


def _ln_silu(u, g, b):
    mu = jnp.mean(u, axis=-1, keepdims=True)
    var = jnp.mean(jnp.square(u - mu), axis=-1, keepdims=True)
    return _silu((u - mu) * lax.rsqrt(var + EPS) * g + b)


def _merge(conv_pre, att_out, g_conv, g_att, b_cb):
    return jax.nn.sigmoid(g_conv) * (conv_pre + b_cb) + jax.nn.sigmoid(g_att) * att_out


def _glu(t):
    return t[:, :CONV_DIM] * jax.nn.sigmoid(t[:, CONV_DIM:])


def _shifted_reader(buf, shifted, tm):
    for b in range(1, SUBLANES):
        shifted[b - 1, :, :] = buf[pl.ds(b, tm + HALO - SUBLANES), :]

    def read(o):
        a, b = divmod(o, SUBLANES)
        return buf[pl.ds(SUBLANES * a, tm), :] if b == 0 else shifted[b - 1, pl.ds(SUBLANES * a, tm), :]

    return read


def _conv_fwd(conv_in, w_pad, b, ln_g, ln_b, exchange, tm=256):
    s = conv_in.shape[0]
    tm = _pick(s, tm, HALO)
    ratio = tm // HALO
    x_arrs, x_shape, x_scratch, _, x_id = exchange
    nx = len(x_arrs)

    def body(*refs):
        main_ref, halo_ref, w_ref, b_ref, g_ref, be_ref = refs[:6]
        u3_ref, u1_ref = refs[6 + nx:8 + nx]
        buf, shifted = refs[-2:]
        finish_exchange = _carry_exchange(exchange, refs, 6, 2, *_sweep_marks(s // tm))
        i = pl.program_id(0)
        buf[0:HALO, :] = _glu(halo_ref[...]) * (i > 0).astype(F32)
        buf[HALO:HALO + tm, :] = _glu(main_ref[...])
        read = _shifted_reader(buf, shifted, tm)
        acc = jnp.zeros((tm, CONV_DIM), F32) + b_ref[...]
        for j in range(CONV_WIDTH):
            acc = acc + w_ref[j:j + 1, :] * read(HALO - (CONV_WIDTH - 1) + j)
        u1_ref[...] = acc
        u3_ref[...] = _ln_silu(acc, g_ref[...], be_ref[...]).astype(u3_ref.dtype)
        finish_exchange()

    res = pl.pallas_call(
        body, name="conv_fwd", grid=(s // tm,),
        in_specs=[pl.BlockSpec((tm, 2 * CONV_DIM), lambda i: (i, 0)),
                  pl.BlockSpec((HALO, 2 * CONV_DIM), lambda i: (jnp.maximum(i * ratio - 1, 0), 0)),
                  pl.BlockSpec(w_pad.shape, lambda i: (0, 0)),
                  pl.BlockSpec(b.shape, lambda i: (0, 0)),
                  pl.BlockSpec(ln_g.shape, lambda i: (0, 0)),
                  pl.BlockSpec(ln_b.shape, lambda i: (0, 0))] + [ANY] * nx,
        out_specs=[pl.BlockSpec((tm, CONV_DIM), lambda i: (i, 0)),
                   pl.BlockSpec((tm, CONV_DIM), lambda i: (i, 0))] + [ANY] * nx,
        out_shape=[_sds((s, CONV_DIM), BF16), _sds((s, CONV_DIM), F32)] + x_shape,
        scratch_shapes=x_scratch + [pltpu.VMEM((tm + HALO, CONV_DIM), F32),
                                    pltpu.VMEM((SUBLANES - 1, tm + HALO - SUBLANES, CONV_DIM), F32)],
        compiler_params=_params(("arbitrary",), x_id),
    )(conv_in, conv_in, w_pad, b, ln_g, ln_b, *x_arrs)
    return res[0], res[1], res[2:]


def _conv_bwd(conv_in, u1, du3, ln_g, ln_b, w_pad, exchange, tm=256):
    s = conv_in.shape[0]
    tm = _pick(s, tm, HALO)
    ratio = tm // HALO
    nt = s // tm
    last_halo = s // HALO - 1
    x_arrs, x_shape, x_scratch, _, x_id = exchange
    nx = len(x_arrs)

    def body(*refs):
        main_ref, halo_ref, u1_ref, u1n_ref, du3_ref, du3n_ref, g_ref, be_ref, w_ref = refs[:9]
        dci_ref, dw_ref, db_ref, dg_ref, dbe_ref = refs[9 + nx:14 + nx]
        ubuf, dbuf, ushift, dshift = refs[-4:]
        finish_exchange = _carry_exchange(exchange, refs, 9, 5, *_sweep_marks(nt))
        i = pl.program_id(0)
        main = main_ref[...]
        a = main[:, :CONV_DIM]
        sb = jax.nn.sigmoid(main[:, CONV_DIM:])
        ubuf[0:HALO, :] = _glu(halo_ref[...]) * (i > 0).astype(F32)
        ubuf[HALO:HALO + tm, :] = a * sb

        def ln_bwd(u1t, du3t):
            _, vjp = jax.vjp(_ln_silu, u1t, g_ref[...], be_ref[...])
            return vjp(du3t)

        du, dg, dbe = ln_bwd(u1_ref[...], du3_ref[...])
        dbuf[0:tm, :] = du
        dbuf[tm:tm + HALO, :] = ln_bwd(u1n_ref[...], du3n_ref[...])[0] * (i < nt - 1).astype(F32)

        @pl.when(i == 0)
        def _():
            dw_ref[...] = jnp.zeros_like(dw_ref)
            db_ref[...] = jnp.zeros_like(db_ref)
            dg_ref[...] = jnp.zeros_like(dg_ref)
            dbe_ref[...] = jnp.zeros_like(dbe_ref)

        dg_ref[...] += dg
        dbe_ref[...] += dbe

        read_u = _shifted_reader(ubuf, ushift, tm)
        read_d = _shifted_reader(dbuf, dshift, tm)
        du0 = jnp.zeros((tm, CONV_DIM), F32)
        for j in range(CONV_WIDTH):
            du0 = du0 + w_ref[j:j + 1, :] * read_d(CONV_WIDTH - 1 - j)
            dw_ref[j:j + 1, :] += jnp.sum(du * read_u(HALO - (CONV_WIDTH - 1) + j), axis=0, keepdims=True)
        db_ref[...] += jnp.sum(du, axis=0, keepdims=True)
        dci_ref[:, :CONV_DIM] = (du0 * sb).astype(dci_ref.dtype)
        dci_ref[:, CONV_DIM:] = (du0 * a * sb * (1.0 - sb)).astype(dci_ref.dtype)
        finish_exchange()

    res = pl.pallas_call(
        body, name="conv_bwd", grid=(nt,),
        in_specs=[pl.BlockSpec((tm, 2 * CONV_DIM), lambda i: (i, 0)),
                  pl.BlockSpec((HALO, 2 * CONV_DIM), lambda i: (jnp.maximum(i * ratio - 1, 0), 0))]
        + [pl.BlockSpec((tm, CONV_DIM), lambda i: (i, 0)),
           pl.BlockSpec((HALO, CONV_DIM), lambda i: (jnp.minimum((i + 1) * ratio, last_halo), 0))] * 2
        + [pl.BlockSpec((1, CONV_DIM), lambda i: (0, 0))] * 2 + [pl.BlockSpec(w_pad.shape, lambda i: (0, 0))]
        + [ANY] * nx,
        out_specs=[pl.BlockSpec((tm, 2 * CONV_DIM), lambda i: (i, 0)),
                   pl.BlockSpec(w_pad.shape, lambda i: (0, 0))]
        + [pl.BlockSpec((1, CONV_DIM), lambda i: (0, 0))] * 3 + [ANY] * nx,
        out_shape=[_sds((s, 2 * CONV_DIM), BF16), _sds(w_pad.shape)] + [_sds((1, CONV_DIM))] * 3 + x_shape,
        scratch_shapes=x_scratch + [pltpu.VMEM((tm + HALO, CONV_DIM), F32)] * 2
        + [pltpu.VMEM((SUBLANES - 1, tm + HALO - SUBLANES, CONV_DIM), F32)] * 2,
        compiler_params=_params(("arbitrary",), x_id),
    )(conv_in, conv_in, u1, u1, du3, du3, ln_g, ln_b, w_pad, *x_arrs)
    return res[:5], res[5:]


def _logsig_neg(z):
    return jnp.minimum(-z, 0.0) - jnp.log(1.0 + jnp.exp(-jnp.abs(z)))


def _split_dot(val, tri):
    hi = val.astype(BF16)
    lo = (val - hi.astype(F32)).astype(BF16)
    return jnp.dot(hi, tri, preferred_element_type=F32) + jnp.dot(lo, tri, preferred_element_type=F32)


def _attn_masks(t, later):
    row = lax.broadcasted_iota(jnp.int32, (t, t), 0)
    col = lax.broadcasted_iota(jnp.int32, (t, t), 1)
    tri = jnp.where(row > col if later else row <= col, 1.0, 0.0).astype(BF16)
    return col < row, tri


def _grid_marks(h, nq):
    hh, i = pl.program_id(0), pl.program_id(1)
    return (hh == 0) & (i == 0), (hh == h - 1) & (i == nq // 2), (hh == h - 1) & (i == nq - 1)


def _head_masks(shape):
    lane = lax.broadcasted_iota(jnp.int32, shape, len(shape) - 1)
    return lane < HEAD_DIM, lane >= HEAD_DIM


def _per_head(blk):
    m0, m1 = _head_masks(blk.shape)
    zero = jnp.zeros_like(blk)
    return jnp.where(m0, blk, zero), jnp.where(m1, blk, zero)


NT = (((1,), (1,)), ((), ()))
TN = (((0,), (0,)), ((), ()))


def _with_top(whole, top):
    rows = top.shape[0]
    return top if rows == whole.shape[0] else jnp.concatenate([top, whole[rows:]], axis=0)


def _attn_fwd(q, k, v, exchange):
    s = q.shape[0]
    hp = q.shape[1] // LANES
    t = ATT_TILE
    scale = 1.0 / math.sqrt(HEAD_DIM)
    x_arrs, x_shape, x_scratch, _, x_id = exchange
    nx = len(x_arrs)

    def body(*refs):
        q_ref, k_ref, v_ref = refs[:3]
        o_ref, lt_ref, nb_ref = refs[3 + nx:6 + nx]
        finish_exchange = _carry_exchange(exchange, refs, 3, 3, *_grid_marks(hp, s // t))
        i = pl.program_id(1)
        qs = _per_head((q_ref[...].astype(F32) * scale).astype(BF16))
        causal, tri = _attn_masks(t, later=True)

        def step(kb, carry, masked, rows):
            cs, acc = carry
            off = pl.multiple_of(kb * t, t)
            kblk = k_ref[pl.ds(off, t), :]
            vs = _per_head(v_ref[pl.ds(off, t), :])
            acc_top = acc[:rows]
            new_cs = []
            for hd in range(2):
                z = lax.dot_general(qs[hd][:rows], kblk, NT, preferred_element_type=F32)
                l = _logsig_neg(z)
                if masked:
                    l = jnp.where(causal, l, 0.0)
                e = z + l + _split_dot(l, tri) + cs[hd][:rows]
                if masked:
                    e = jnp.where(causal, e, -1e30)
                acc_top = acc_top + jnp.dot(jnp.exp(e).astype(BF16), vs[hd], preferred_element_type=F32)
                new_cs.append(_with_top(cs[hd], cs[hd][:rows] + jnp.sum(l, axis=1, keepdims=True)))
            return tuple(new_cs), _with_top(acc, acc_top)

        zero = jnp.zeros((t, 1), F32)
        carry = step(i, ((zero, zero), jnp.zeros((t, LANES), F32)), True, t)

        def live(cs, lo, hi):
            return jnp.maximum(jnp.max(cs[0][lo:hi]), jnp.max(cs[1][lo:hi])) > DEAD_SUM

        def more(state):
            n, _, (cs, _) = state
            return (n < i) & live(cs, 0, t)

        def sweep(state):
            n, n_full, cr = state
            whole = live(cr[0], ATT_PART, t)
            cr = lax.cond(whole, lambda c: step(i - 1 - n, c, False, t), lambda c: step(i - 1 - n, c, False, ATT_PART), cr)
            return n + 1, n_full + whole.astype(jnp.int32), cr

        n_blocks, n_full, carry = lax.while_loop(more, sweep, (jnp.int32(0), jnp.int32(0), carry))
        m0, _ = _head_masks((t, LANES))
        lt_ref[...] = jnp.where(m0, carry[0][0], carry[0][1])
        o_ref[...] = carry[1].astype(o_ref.dtype)
        nb_ref[0, pl.program_id(0), i] = n_blocks.astype(F32)
        nb_ref[1, pl.program_id(0), i] = n_full.astype(F32)
        finish_exchange()

    res = pl.pallas_call(
        body, name="attn_fwd", grid=(hp, s // t),
        in_specs=[pl.BlockSpec((t, LANES), lambda p, i: (i, p)),
                  pl.BlockSpec((s, LANES), lambda p, i: (0, p)),
                  pl.BlockSpec((s, LANES), lambda p, i: (0, p))] + [ANY] * nx,
        out_specs=[pl.BlockSpec((t, LANES), lambda p, i: (i, p)),
                   pl.BlockSpec((None, t, LANES), lambda p, i: (p, i, 0)),
                   pl.BlockSpec(memory_space=pltpu.SMEM)] + [ANY] * nx,
        out_shape=[_sds(q.shape, BF16), _sds((hp, s, LANES), F32), _sds((2, hp, s // t), F32)] + x_shape,
        scratch_shapes=x_scratch,
        compiler_params=_params(("arbitrary", "arbitrary"), x_id),
    )(q, k, v, *x_arrs)
    return res[0], res[1], res[2], res[3:]


def _attn_bwd(q, k, v, do, ltot, n_blocks, exchange):
    s = q.shape[0]
    hp = q.shape[1] // LANES
    t = ATT_TILE
    scale = 1.0 / math.sqrt(HEAD_DIM)
    x_arrs, x_shape, x_scratch, _, x_id = exchange
    nx = len(x_arrs)

    def body(*refs):
        q_ref, k_ref, v_ref, do_ref, lt_ref, nb_ref = refs[:6]
        dq_ref, dk_ref, dv_ref = refs[6 + nx:9 + nx]
        finish_exchange = _carry_exchange(exchange, refs, 6, 3, *_grid_marks(hp, s // t))
        i = pl.program_id(1)
        n_blocks = jnp.clip(nb_ref[0, pl.program_id(0), i].astype(jnp.int32), 0, i)
        n_full = jnp.clip(nb_ref[1, pl.program_id(0), i].astype(jnp.int32), 0, n_blocks)

        @pl.when(i == 0)
        def _():
            dk_ref[...] = jnp.zeros_like(dk_ref)
            dv_ref[...] = jnp.zeros_like(dv_ref)

        qb = q_ref[...]
        qm = _per_head(qb)
        qs = _per_head((qb.astype(F32) * scale).astype(BF16))
        dos = _per_head(do_ref[...])
        lts = (lt_ref[:, 0:1], lt_ref[:, HEAD_DIM:HEAD_DIM + 1])
        causal, tri = _attn_masks(t, later=False)

        def step(kb, carry, masked, rows):
            cls, cgs, dq = carry
            off = pl.multiple_of(kb * t, t)
            kblk = k_ref[pl.ds(off, t), :]
            vblk = v_ref[pl.ds(off, t), :]
            ks = _per_head(kblk)
            dq_top = dq[:rows]
            dk = jnp.zeros((t, LANES), F32)
            dv = jnp.zeros((t, LANES), F32)
            new_cls, new_cgs = [], []
            for hd in range(2):
                z = lax.dot_general(qs[hd][:rows], kblk, NT, preferred_element_type=F32)
                l = _logsig_neg(z)
                if masked:
                    l = jnp.where(causal, l, 0.0)
                e = z + l + ((lts[hd][:rows] - cls[hd][:rows]) - _split_dot(l, tri))
                if masked:
                    e = jnp.where(causal, e, -1e30)
                a = jnp.exp(e)
                g = lax.dot_general(dos[hd][:rows], vblk, NT, preferred_element_type=F32) * a
                p = cgs[hd][:rows] + jnp.dot(g.astype(BF16), tri, preferred_element_type=F32) - g
                el = jnp.exp(l)
                dz = g * el - p * (1.0 - el)
                if masked:
                    dz = jnp.where(causal, dz, 0.0)
                dzb = (dz * scale).astype(BF16)
                dq_top = dq_top + jnp.dot(dzb, ks[hd], preferred_element_type=F32)
                dk = dk + lax.dot_general(dzb, qm[hd][:rows], TN, preferred_element_type=F32)
                dv = dv + lax.dot_general(a.astype(BF16), dos[hd][:rows], TN, preferred_element_type=F32)
                new_cls.append(_with_top(cls[hd], cls[hd][:rows] + jnp.sum(l, axis=1, keepdims=True)))
                new_cgs.append(_with_top(cgs[hd], cgs[hd][:rows] + jnp.sum(g, axis=1, keepdims=True)))
            dk_ref[pl.ds(off, t), :] += dk
            dv_ref[pl.ds(off, t), :] += dv
            return tuple(new_cls), tuple(new_cgs), _with_top(dq, dq_top)

        zero = jnp.zeros((t, 1), F32)
        init = ((zero, zero), (zero, zero), jnp.zeros((t, LANES), F32))
        carry = lax.fori_loop(i - n_blocks, i - n_full, lambda kb, cr: step(kb, cr, False, ATT_PART), init)
        carry = lax.fori_loop(i - n_full, i, lambda kb, cr: step(kb, cr, False, t), carry)
        carry = step(i, carry, True, t)
        dq_ref[...] = carry[2]
        finish_exchange()

    blk = pl.BlockSpec((t, LANES), lambda p, i: (i, p))
    whole = pl.BlockSpec((s, LANES), lambda p, i: (0, p))
    res = pl.pallas_call(
        body, name="attn_bwd", grid=(hp, s // t),
        in_specs=[blk, whole, whole, blk, pl.BlockSpec((None, t, LANES), lambda p, i: (p, i, 0)),
                  pl.BlockSpec(memory_space=pltpu.SMEM)] + [ANY] * nx,
        out_specs=[blk, whole, whole] + [ANY] * nx,
        out_shape=[_sds(q.shape)] * 3 + x_shape,
        scratch_shapes=x_scratch,
        compiler_params=_params(("arbitrary", "arbitrary"), x_id),
    )(q, k, v, do, ltot, n_blocks, *x_arrs)
    return res[0], res[1], res[2], res[3:]


LATE = ["w_conv_branch", "w_att_branch", "w_out", "w_ffn_up", "w_ffn_down"]


def _full_weight(name, gathered):
    return _cols_to_full(gathered) if name in COL_SHARDED else gathered.reshape(-1, gathered.shape[2])


def _grad_slabs(name, grad):
    return _full_to_cols(grad) if name in COL_SHARDED else grad.reshape(N_DEV, -1, grad.shape[1])


def _side_slabs(name, grad):
    slabs = _grad_slabs(name, grad)
    return slabs.reshape((4, 2) + slabs.shape[1:])


def _local_step(x, target, w, late_blocks, opt):
    s = x.shape[0]
    w = dict(w)
    g1, g2, g3, g4 = w["norm_mix_pre"], w["norm_mix_post"], w["norm_ffn_pre"], w["norm_ffn_post"]

    w_in = w["w_in"]

    def proj_fn(xt, g1_, w_in_t):
        h = _rms(xt, g1_).astype(BF16)
        proj = lax.dot_general(h, w_in_t, NT, preferred_element_type=F32)
        return (h, *[proj[:, IN_SPLITS[n]:IN_SPLITS[n + 1]] for n in range(6)]), ()

    mix_weights = ["w_conv_branch", "w_att_branch", "w_out"]
    h1, conv_in, q, k, v, g_conv, g_att, g_out = _rowwise(
        "norm_proj", proj_fn, [x], [g1, w_in],
        [_sds((s, D_MODEL), BF16), _sds((s, 2 * CONV_DIM)), _sds((s, ATT_DIM), BF16), _sds((s, ATT_DIM), BF16),
         _sds((s, ATT_DIM), BF16), _sds((s, D_MODEL), BF16), _sds((s, D_MODEL), BF16)], tm=512,
        exchange=_gather_exchange([late_blocks["w_out"]]))

    u3, u1, g_branches = _conv_fwd(conv_in, w["conv_dw_w"], w["conv_dw_b"], w["conv_ln_g"], w["conv_ln_b"],
                                   _gather_exchange([late_blocks[nm] for nm in mix_weights[:2]]))
    for nm, g in zip(mix_weights, [*g_branches, g_out]):
        w[nm] = _full_weight(nm, g)
    att, ltot, n_blocks, (g_up,) = _attn_fwd(q, k, v, _gather_exchange([late_blocks["w_ffn_up"]]))
    w["w_ffn_up"] = _full_weight("w_ffn_up", g_up)

    def merge_fn(u3t, at, gc, ga, xt, w_cb, w_ab, b_cb, w_out, g2_, g3_):
        cp = jnp.dot(u3t, w_cb, preferred_element_type=F32)
        ao = jnp.dot(at, w_ab, preferred_element_type=F32)
        mg = _merge(cp, ao, gc.astype(F32), ga.astype(F32), b_cb).astype(BF16)
        mix_ = jnp.dot(mg, w_out, preferred_element_type=F32)
        x2_ = xt + _rms(mix_, g2_)
        return (mg, cp, ao, mix_, x2_, _rms(x2_, g3_)), ()

    half = D_MODEL // 2
    down_block = late_blocks["w_ffn_down"]
    merged, conv_pre, att_out, mix, x2, h2, g_left = _rowwise(
        "branch_merge_mix", merge_fn, [u3, att, g_conv, g_att, x],
        [w["w_conv_branch"], w["w_att_branch"], w["b_conv_branch"], w["w_out"], g2, g3],
        [_sds((s, D_MODEL), BF16)] * 3 + [_sds((s, D_MODEL)), _sds((s, D_MODEL)), _sds((s, D_MODEL), BF16)], tm=512,
        exchange=_gather_exchange([down_block[:, :half]]))

    def ffn_up_fn(ht, w_up_t):
        gu_ = lax.dot_general(ht, w_up_t, NT, preferred_element_type=F32)
        return (gu_, _swiglu(gu_[:, :D_FF], gu_[:, D_FF:])), ()

    gu, act, g_right = _rowwise("ffn_up", ffn_up_fn, [h2], [w["w_ffn_up"]],
                                [_sds((s, 2 * D_FF), BF16), _sds((s, D_FF), BF16)], tm=512,
                                exchange=_gather_exchange([down_block[:, half:]]))
    w_down = [_full_weight("w_ffn_down", g) for g in (g_left, g_right)]

    def final_fn(at, x2t, tgt, w_left, w_right, g4_):
        ff = jnp.concatenate([jnp.dot(at, w_left, preferred_element_type=F32),
                              jnp.dot(at, w_right, preferred_element_type=F32)], axis=1)
        n4, vjp = jax.vjp(_rms, ff, g4_)
        err = x2t + n4 - tgt
        dy = err * (1.0 / D_MODEL)
        dff, dg4 = vjp(dy)
        return (dy, dff), (jnp.sum(err * err, axis=0, keepdims=True), dg4)

    dy, dff, loss_cols, d_g4 = _rowwise("ffn_down_loss", final_fn, [act, x2, target], [*w_down, g4],
                                        [_sds((s, D_MODEL)), _sds((s, D_MODEL), BF16)],
                                        [_sds((1, D_MODEL)), _sds((1, D_MODEL))], tm=512)
    loss = 0.5 * jnp.sum(loss_cols) / D_MODEL

    d_w_down = _tn_matmul(act, dff, name="d_w_down")

    def act_bwd_fn(dfft, gut, w_left, w_right):
        d_act = (lax.dot_general(dfft[:, :half], w_left, NT, preferred_element_type=F32)
                 + lax.dot_general(dfft[:, half:], w_right, NT, preferred_element_type=F32))
        gu_ = gut.astype(F32)
        _, vjp = jax.vjp(_swiglu, gu_[:, :D_FF], gu_[:, D_FF:])
        return (jnp.concatenate(vjp(d_act), axis=1),), ()

    down_slabs = _side_slabs("w_ffn_down", d_w_down)
    dgu, theirs = _rowwise("ffn_act_bwd", act_bwd_fn, [dff, gu], w_down, [_sds((s, 2 * D_FF), BF16)],
                           exchange=_pair_exchange([down_slabs]))
    down_sums = _pair_sum("pair_sum_w_ffn_down", down_slabs, theirs)
    d_w_up = _tn_matmul(dgu, h2, name="d_w_up")
    received = {}
    up_slabs = _side_slabs("w_ffn_up", d_w_up)

    def mid_bwd_fn(dgut, xt, mt, dyt, w_up_t, g2_, g3_):
        dh = jnp.dot(dgut, w_up_t, preferred_element_type=F32)
        n2, vjp2 = jax.vjp(_rms, mt, g2_)
        x2_ = xt + n2
        _, vjp3 = jax.vjp(_rms, x2_, g3_)
        dx2_, dg3 = vjp3(dh)
        dx2_ = dx2_ + dyt
        dmix_, dg2 = vjp2(dx2_)
        return (dx2_, dmix_), (dg2, dg3)

    dx2, dmix, d_g2, d_g3, received["w_ffn_down"] = _rowwise(
        "ffn_up_mid_bwd", mid_bwd_fn, [dgu, x, mix, dy], [w["w_ffn_up"], g2, g3],
        [_sds((s, D_MODEL)), _sds((s, D_MODEL), BF16)], [_sds((1, D_MODEL)), _sds((1, D_MODEL))], tm=512,
        exchange=_chip_exchange([down_sums]))
    d_w_out = _tn_matmul(merged, dmix, name="d_w_out")

    def merge_bwd_fn(dmt, cp, ao, gc, ga, w_out, w_cb, w_ab, b_cb):
        dm = lax.dot_general(dmt, w_out, NT, preferred_element_type=F32)
        _, vjp = jax.vjp(_merge, cp.astype(F32), ao.astype(F32), gc.astype(F32), ga.astype(F32), b_cb)
        dcp, dao, dgc, dga, dbias = vjp(dm)
        dcp, dao = dcp.astype(BF16), dao.astype(BF16)
        du3_ = lax.dot_general(dcp, w_cb, NT, preferred_element_type=F32)
        datt_ = lax.dot_general(dao, w_ab, NT, preferred_element_type=F32)
        return (dcp, dao, dgc, dga, du3_, datt_), (dbias,)

    d_conv_out, d_att_out, d_g_conv, d_g_att, du3, d_att, d_b_cb, theirs = _rowwise(
        "merge_bwd", merge_bwd_fn, [dmix, conv_pre, att_out, g_conv, g_att],
        [w["w_out"], w["w_conv_branch"], w["w_att_branch"], w["b_conv_branch"]],
        [_sds((s, D_MODEL), BF16)] * 4 + [_sds((s, CONV_DIM)), _sds((s, ATT_DIM), BF16)], [_sds((1, D_MODEL))], tm=512,
        exchange=_pair_exchange([up_slabs]))

    d_w_cb = _tn_matmul(u3, d_conv_out, name="d_w_conv_branch")
    d_w_ab = _tn_matmul(att, d_att_out, name="d_w_att_branch")

    dq, dk, dv, (received["w_ffn_up"],) = _attn_bwd(
        q, k, v, d_att, ltot, n_blocks, _chip_exchange([_pair_sum("pair_sum_w_ffn_up", up_slabs, theirs)]))

    mix_grads = {"w_conv_branch": d_w_cb, "w_att_branch": d_w_ab, "w_out": d_w_out}
    (d_conv_in, d_dw_w, d_dw_b, d_ln_g, d_ln_b), landed = _conv_bwd(
        conv_in, u1, du3, w["conv_ln_g"], w["conv_ln_b"], w["conv_dw_w"],
        _scatter_exchange([_grad_slabs(nm, mix_grads[nm]) for nm in mix_weights]))
    received.update(zip(mix_weights, landed))

    d_proj = [d_conv_in, dq, dk, dv, d_g_conv, d_g_att]
    d_w_in = _pieces_tn_matmul(d_proj, h1, name="d_w_in")
    in_slabs = _side_slabs("w_in", d_w_in)
    (theirs,) = _exchange_call("pair_swap_w_in", _pair_exchange([in_slabs]))

    early = list(opt)

    def pre_bwd_fn(*args):
        groups, (xt, dx2t), jobs, (w_in_t, g_) = args[:6], args[6:8], args[8:-2], args[-2:]
        dh = sum(jnp.dot(grp.astype(BF16), w_in_t[IN_SPLITS[n]:IN_SPLITS[n + 1]], preferred_element_type=F32)
                 for n, grp in enumerate(groups))
        _, vjp = jax.vjp(_rms, xt, g_)
        dx_, dg_ = vjp(dh)
        updates = [_sum_adamw_tile(*jobs[4 * n:4 * n + 4]) for n in range(len(early))]
        return (dx_ + dx2t, *[u for four in updates for u in four]), (dg_,)

    res = _rowwise(
        "proj_norm_bwd", pre_bwd_fn,
        d_proj + [x, dx2] + [a for nm in early for a in (received[nm], *opt[nm])], [w_in, g1],
        [_sds((s, D_MODEL))] + [_sds(opt[nm][0].shape) for nm in early for _ in range(4)],
        [_sds((1, D_MODEL))], tm=512, exchange=_chip_exchange([_pair_sum("pair_sum_w_in", in_slabs, theirs)]))
    grad_x, d_g1, received["w_in"] = res[0], res[-2], res[-1]
    updated = {nm: res[1 + 4 * n:5 + 4 * n] for n, nm in enumerate(early)}

    grads = {
        "norm_mix_pre": d_g1, "conv_dw_w": d_dw_w, "conv_dw_b": d_dw_b,
        "conv_ln_g": d_ln_g, "conv_ln_b": d_ln_b, "b_conv_branch": d_b_cb,
        "norm_mix_post": d_g2, "norm_ffn_pre": d_g3, "norm_ffn_post": d_g4,
    }
    return loss, grad_x, received, updated, grads


def _place():
    x, y, c = lax.axis_index("x"), lax.axis_index("y"), lax.axis_index("c")
    return x, y, c


def _slot(px, py, pc):
    return 4 * px + 2 * py + pc


def _exchange_scratch(n):
    return [pltpu.SemaphoreType.DMA((7 * n,)), pltpu.SemaphoreType.DMA((7 * n,)), pltpu.SemaphoreType.DMA((n,))]


GATHER_ID, SCATTER_ID, PAIR_ID, CHIP_ID = 0, 1, 2, 3


def _handshake(peers):
    barrier = pltpu.get_barrier_semaphore()
    for peer in peers:
        pl.semaphore_signal(barrier, inc=1, device_id=peer, device_id_type=MESH)
    pl.semaphore_wait(barrier, len(peers))


def _gather_exchange(arrs):
    n = len(arrs)

    def phases(ins, outs, send_sems, recv_sems, local_sems):
        x, y, c = _place()
        me, sibling = (x, y, c), (x, y, 1 - c)
        chips = [(1 - x, y), (x, 1 - y), (1 - x, 1 - y)]

        def copy(a, kk, block, to, src=None):
            dst = outs[a].at[_slot(*block)]
            return pltpu.make_async_remote_copy(
                src_ref=dst if src is None else src, dst_ref=dst,
                send_sem=send_sems.at[a * 7 + kk], recv_sem=recv_sems.at[a * 7 + kk],
                device_id=to, device_id_type=MESH)

        mine = [pltpu.make_async_copy(ins[a], outs[a].at[_slot(*me)], local_sems.at[a]) for a in range(n)]
        first = []
        for a in range(n):
            first.append(copy(a, 0, me, sibling, src=ins[a]))
            first += [copy(a, 1 + j, me, (*chip, c), src=ins[a]) for j, chip in enumerate(chips)]
        passed = [copy(a, 4 + j, (*chip, c), sibling) for j, chip in enumerate(chips) for a in range(n)]

        def send():
            _handshake([sibling] + [(*chip, c) for chip in chips])
            for cp in mine + first:
                cp.start()

        def pass_on():
            for j, chip in enumerate(chips):
                for a in range(n):
                    copy(a, 1 + j, (*chip, c), me).wait_recv()
                    passed[j * n + a].start()

        def finish():
            for a in range(n):
                copy(a, 0, sibling, me).wait_recv()
                for j, chip in enumerate(chips):
                    copy(a, 4 + j, (*chip, 1 - c), me).wait_recv()
            for cp in first + passed:
                cp.wait_send()
            for cp in mine:
                cp.wait()

        return [send, pass_on, finish]

    return list(arrs), [_sds((N_DEV,) + a.shape, a.dtype) for a in arrs], _exchange_scratch(n), phases, GATHER_ID


def _scatter_exchange(arrs):
    n = len(arrs)
    flips = [(fx, fy, fc) for fx in (0, 1) for fy in (0, 1) for fc in (0, 1)][1:]

    def phases(ins, outs, send_sems, recv_sems, local_sems):
        x, y, c = _place()
        mine = _slot(x, y, c)
        local = [pltpu.make_async_copy(ins[a].at[mine], outs[a].at[mine], local_sems.at[a]) for a in range(n)]
        peers = [((1 - x) if fx else x, (1 - y) if fy else y, (1 - c) if fc else c) for fx, fy, fc in flips]

        def copy(a, kk, src_slot, dst_slot):
            return pltpu.make_async_remote_copy(
                src_ref=ins[a].at[src_slot], dst_ref=outs[a].at[dst_slot],
                send_sem=send_sems.at[a * 7 + kk], recv_sem=recv_sems.at[a * 7 + kk],
                device_id=peers[kk], device_id_type=MESH)

        sends = [copy(a, kk, _slot(*peers[kk]), mine) for a in range(n) for kk in range(7)]

        def send():
            _handshake(peers)
            for cp in local + sends:
                cp.start()

        def finish():
            for a in range(n):
                for kk in range(7):
                    copy(a, kk, mine, _slot(*peers[kk])).wait_recv()
            for cp in sends:
                cp.wait_send()
            for cp in local:
                cp.wait()

        return [send, finish]

    return list(arrs), [_sds(a.shape, a.dtype) for a in arrs], _exchange_scratch(n), phases, SCATTER_ID


def _pair_exchange(arrs):
    n = len(arrs)

    def phases(ins, outs, send_sems, recv_sems, local_sems):
        x, y, c = _place()

        def copy(a, chip, side):
            return pltpu.make_async_remote_copy(
                src_ref=ins[a].at[chip, side], dst_ref=outs[a].at[chip],
                send_sem=send_sems.at[a * 7 + chip], recv_sem=recv_sems.at[a * 7 + chip],
                device_id=(x, y, 1 - c), device_id_type=MESH)

        sends = [copy(a, chip, 1 - c) for a in range(n) for chip in range(4)]

        def send():
            _handshake([(x, y, 1 - c)])
            for cp in sends:
                cp.start()

        def finish():
            for a in range(n):
                for chip in range(4):
                    copy(a, chip, c).wait_recv()
            for cp in sends:
                cp.wait_send()

        return [send, finish]

    return list(arrs), [_sds((4,) + a.shape[2:], a.dtype) for a in arrs], _exchange_scratch(n), phases, PAIR_ID


def _chip_exchange(arrs):
    n = len(arrs)

    def phases(ins, outs, send_sems, recv_sems, local_sems):
        x, y, c = _place()
        mine = 2 * x + y
        chips = [(1 - x, y), (x, 1 - y), (1 - x, 1 - y)]
        local = [pltpu.make_async_copy(ins[a].at[mine], outs[a].at[mine], local_sems.at[a]) for a in range(n)]

        def copy(a, j, src_slot, dst_slot):
            return pltpu.make_async_remote_copy(
                src_ref=ins[a].at[src_slot], dst_ref=outs[a].at[dst_slot],
                send_sem=send_sems.at[a * 7 + j], recv_sem=recv_sems.at[a * 7 + j],
                device_id=(*chips[j], c), device_id_type=MESH)

        sends = [copy(a, j, 2 * chips[j][0] + chips[j][1], mine) for a in range(n) for j in range(3)]

        def send():
            _handshake([(*chip, c) for chip in chips])
            for cp in local + sends:
                cp.start()

        def finish():
            for a in range(n):
                for j in range(3):
                    copy(a, j, mine, 2 * chips[j][0] + chips[j][1]).wait_recv()
            for cp in sends:
                cp.wait_send()
            for cp in local:
                cp.wait()

        return [send, finish]

    return list(arrs), [_sds(a.shape, a.dtype) for a in arrs], _exchange_scratch(n), phases, CHIP_ID


def _pair_sum(name, mine, theirs):
    _, _, r, c = mine.shape

    def body(side_ref, m_ref, t_ref, o_ref):
        o_ref[...] = (m_ref[...].astype(F32) + t_ref[...].astype(F32)).astype(o_ref.dtype)

    return pl.pallas_call(
        body, name=name,
        grid_spec=pltpu.PrefetchScalarGridSpec(
            num_scalar_prefetch=1, grid=(4,),
            in_specs=[pl.BlockSpec((None, None, r, c), lambda j, side: (j, side[0], 0, 0)),
                      pl.BlockSpec((None, r, c), lambda j, side: (j, 0, 0))],
            out_specs=pl.BlockSpec((None, r, c), lambda j, side: (j, 0, 0))),
        out_shape=_sds(theirs.shape, theirs.dtype),
        compiler_params=_params(("parallel",)),
    )(lax.axis_index("c").astype(jnp.int32).reshape(1), mine, theirs)


def _exchange_call(name, exchange):
    arrs, out_shape, scratch, phases, collective_id = exchange
    n = len(arrs)

    def body(*refs):
        for step in phases(refs[:n], refs[n:2 * n], *refs[2 * n:]):
            step()

    return pl.pallas_call(body, name=name, in_specs=[ANY] * n, out_specs=[ANY] * n,
                          out_shape=out_shape, scratch_shapes=scratch,
                          compiler_params=pltpu.CompilerParams(collective_id=collective_id))(*arrs)


def _carry_exchange(exchange, refs, n_in, n_out, first, middle, last):
    arrs, _, _, phases, _ = exchange
    n = len(arrs)
    if n == 0:
        return lambda: None
    ins = refs[n_in:n_in + n]
    outs = refs[n_in + n + n_out:n_in + 2 * n + n_out]
    sems = n_in + 2 * n + n_out
    steps = phases(ins, outs, *refs[sems:sems + 3])
    pl.when(first)(steps[0])
    if len(steps) == 3:
        pl.when(middle)(steps[1])
    return lambda: pl.when(last)(steps[-1])


def _adamw_math(w, g, m, v):
    m2 = ADAM_B1 * m + (1.0 - ADAM_B1) * g
    v2 = ADAM_B2 * v + (1.0 - ADAM_B2) * jnp.square(g)
    m_hat = m2 / (1.0 - ADAM_B1 ** ADAM_STEP)
    v_hat = v2 / (1.0 - ADAM_B2 ** ADAM_STEP)
    delta = -ADAM_LR * (m_hat / (jnp.sqrt(v_hat) + ADAM_EPS) + ADAM_WD * w)
    return delta, m2, v2


def _sum_adamw_tile(parts, w, m, v):
    g = parts[0].astype(F32)
    for d in range(1, parts.shape[0]):
        g = g + parts[d].astype(F32)
    return (g, *_adamw_math(w, g, m, v))


def _sum_adamw(name, parts, w, m, v, tr=256):
    p, r, c = parts.shape
    tr = _pick(r, tr, 16)

    def body(p_ref, w_ref, m_ref, v_ref, g_ref, d_ref, m2_ref, v2_ref):
        g_ref[...], d_ref[...], m2_ref[...], v2_ref[...] = _sum_adamw_tile(p_ref[...], w_ref[...], m_ref[...], v_ref[...])

    tile = pl.BlockSpec((tr, c), lambda i: (i, 0))
    return pl.pallas_call(
        body, name=name, grid=(r // tr,),
        in_specs=[pl.BlockSpec((p, tr, c), lambda i: (0, i, 0)), tile, tile, tile],
        out_specs=[tile] * 4, out_shape=[_sds((r, c))] * 4,
        compiler_params=_params(("parallel",)),
    )(parts, w, m, v)


def _sum_parts(name, parts):
    p, r, c = parts.shape

    def body(p_ref, o_ref):
        g = p_ref[0]
        for d in range(1, p):
            g = g + p_ref[d]
        o_ref[...] = g

    return pl.pallas_call(
        body, name=name, out_shape=_sds((r, c)),
        in_specs=[pl.BlockSpec(memory_space=pltpu.VMEM)], out_specs=pl.BlockSpec(memory_space=pltpu.VMEM),
    )(parts)


WEIGHTS = ["norm_mix_pre", "w_in", "conv_dw_w", "conv_dw_b", "conv_ln_g", "conv_ln_b", "w_conv_branch",
           "b_conv_branch", "w_att_branch", "w_out", "norm_mix_post", "norm_ffn_pre", "w_ffn_up", "w_ffn_down",
           "norm_ffn_post"]
COL_SHARDED = ["w_conv_branch", "w_att_branch"]
TRANSPOSED = ["w_in", "w_ffn_up"]
VECTORS = ["norm_mix_pre", "conv_dw_b", "conv_ln_g", "conv_ln_b", "b_conv_branch", "norm_mix_post",
           "norm_ffn_pre", "norm_ffn_post"]


def _cols_to_full(g):
    return g.transpose(1, 0, 2).reshape(g.shape[1], N_DEV * g.shape[2])


def _full_to_cols(f):
    return f.reshape(f.shape[0], N_DEV, f.shape[1] // N_DEV).transpose(1, 0, 2)


PACK_ROWS = 7


def _pack_vectors(vecs, extra=None):
    parts = [vecs[nm].reshape(-1) for nm in VECTORS]
    parts.append(jnp.zeros((1,), F32) if extra is None else extra.reshape(1))
    used = sum(p.size for p in parts)
    parts.append(jnp.zeros((PACK_ROWS * D_MODEL - used,), F32))
    return jnp.concatenate(parts).reshape(PACK_ROWS, D_MODEL)


def _unpack_vectors(packed, sizes):
    flat, out, at = packed.reshape(-1), {}, 0
    for nm in VECTORS:
        out[nm] = flat[at:at + sizes[nm]]
        at += sizes[nm]
    return out, flat[at]


def kernel(x, norm_mix_pre, w_in, conv_dw_w, conv_dw_b, conv_ln_g, conv_ln_b, w_conv_branch, b_conv_branch, w_att_branch, w_out, norm_mix_post, norm_ffn_pre, w_ffn_up, w_ffn_down, norm_ffn_post, loss_target, m_norm_mix_pre, m_w_in, m_conv_dw_w, m_conv_dw_b, m_conv_ln_g, m_conv_ln_b, m_w_conv_branch, m_b_conv_branch, m_w_att_branch, m_w_out, m_norm_mix_post, m_norm_ffn_pre, m_w_ffn_up, m_w_ffn_down, m_norm_ffn_post, v_norm_mix_pre, v_w_in, v_conv_dw_w, v_conv_dw_b, v_conv_ln_g, v_conv_ln_b, v_w_conv_branch, v_b_conv_branch, v_w_att_branch, v_w_out, v_norm_mix_post, v_norm_ffn_pre, v_w_ffn_up, v_w_ffn_down, v_norm_ffn_post):
    ws = dict(zip(WEIGHTS, [norm_mix_pre, w_in, conv_dw_w, conv_dw_b, conv_ln_g, conv_ln_b, w_conv_branch,
                            b_conv_branch, w_att_branch, w_out, norm_mix_post, norm_ffn_pre, w_ffn_up, w_ffn_down,
                            norm_ffn_post]))
    ms = dict(zip(WEIGHTS, [m_norm_mix_pre, m_w_in, m_conv_dw_w, m_conv_dw_b, m_conv_ln_g, m_conv_ln_b,
                            m_w_conv_branch, m_b_conv_branch, m_w_att_branch, m_w_out, m_norm_mix_post,
                            m_norm_ffn_pre, m_w_ffn_up, m_w_ffn_down, m_norm_ffn_post]))
    vs = dict(zip(WEIGHTS, [v_norm_mix_pre, v_w_in, v_conv_dw_w, v_conv_dw_b, v_conv_ln_g, v_conv_ln_b,
                            v_w_conv_branch, v_b_conv_branch, v_w_att_branch, v_w_out, v_norm_mix_post,
                            v_norm_ffn_pre, v_w_ffn_up, v_w_ffn_down, v_norm_ffn_post]))

    dw_block = jnp.pad(conv_dw_w, ((0, 1), (0, 0)))
    g_in, g_dw = _exchange_call("gather_first", _gather_exchange([w_in.T.astype(BF16), dw_block]))
    full = {"w_in": _full_weight("w_in", g_in), "conv_dw_w": _cols_to_full(g_dw)}
    for nm in VECTORS:
        full[nm] = ws[nm].reshape(1, -1)

    def as_kept(nm, a):
        return a.T if nm in TRANSPOSED else a

    ride_along = ["w_ffn_up", "w_out"]
    loss_local, grad_x, received, updated, grads = _local_step(
        x[0], loss_target[0], full, {nm: as_kept(nm, ws[nm]).astype(BF16) for nm in LATE},
        {nm: tuple(as_kept(nm, a[nm]) for a in (ws, ms, vs)) for nm in ride_along})

    small = _exchange_call("gather_small_grads", _gather_exchange(
        [_pack_vectors(grads, extra=loss_local), grads["conv_dw_w"]]))
    out_g, out_d, out_m, out_v = {}, {}, {}, {}
    for nm in LATE + ["w_in"]:
        res = updated[nm] if nm in updated else _sum_adamw(
            "adamw_" + nm, received[nm], *[as_kept(nm, a[nm]) for a in (ws, ms, vs)])
        out_g[nm], out_d[nm], out_m[nm], out_v[nm] = [as_kept(nm, r) for r in res]
    sizes = {nm: ws[nm].size for nm in VECTORS}
    vec = _sum_adamw("adamw_vectors", small[0], _pack_vectors(ws), _pack_vectors(ms), _pack_vectors(vs))
    for res, dst in zip(vec, (out_g, out_d, out_m, out_v)):
        dst.update(_unpack_vectors(res, sizes)[0])
    loss = _unpack_vectors(vec[0], sizes)[1]
    dw_full = _sum_parts("sum_dw_grads", small[1])
    me = _slot(*_place())
    dw_mine = lax.dynamic_slice(dw_full, (0, me * (CONV_DIM // N_DEV)), (CONV_WIDTH, CONV_DIM // N_DEV))
    nm = "conv_dw_w"
    out_g[nm], out_d[nm], out_m[nm], out_v[nm] = _sum_adamw("adamw_dw", dw_mine[None], ws[nm], ms[nm], vs[nm])

    outs = [loss, grad_x[None]]
    for group in (out_g, out_d, out_m, out_v):
        outs += [group[nm] for nm in WEIGHTS]
    return tuple(outs)
```

```python
import math

import jax
import jax.numpy as jnp
from jax import lax
from jax.experimental import pallas as pl
from jax.experimental.pallas import tpu as pltpu

F32 = jnp.float32
BF16 = jnp.bfloat16

N_DEV = 8
D_MODEL = 1024
CONV_DIM = 512
CONV_WIDTH = 31
N_HEADS = 8
HEAD_DIM = 64
ATT_DIM = N_HEADS * HEAD_DIM
D_FF = 2816
EPS = 1e-6
IN_SPLITS = (0, 1024, 1536, 2048, 2560, 3584, 4608)

ADAM_LR = 0.001
ADAM_B1 = 0.9
ADAM_B2 = 0.999
ADAM_EPS = 1e-08
ADAM_WD = 0.01
ADAM_STEP = 10

LANES = 128
SUBLANES = 8
HALO = 32
ATT_TILE = 256
ATT_PART = 176
DEAD_SUM = -120.0
VMEM_LIMIT = 56 * 1024 * 1024
MESH = pl.DeviceIdType.MESH
ANY = pl.BlockSpec(memory_space=pl.ANY)


def _pick(dim, target, align=LANES):
    t = min(dim, target)
    t -= t % align
    while t >= align:
        if dim % t == 0:
            return t
        t -= align
    return dim


def _params(semantics, collective_id=None):
    return pltpu.CompilerParams(dimension_semantics=semantics, vmem_limit_bytes=VMEM_LIMIT,
                                collective_id=collective_id)


def _tn_matmul(a, b, *, name):
    return _pieces_tn_matmul([a], b, name=name, tj=_pick(a.shape[1], 1408))


def _pieces_tn_matmul(pieces, b, *, name, tj=512, exchange=None):
    s, n = b.shape
    counts = [p.shape[1] // tj for p in pieces]
    starts = [sum(counts[:i]) for i in range(len(pieces))]
    assert all(p.shape == (s, c * tj) for p, c in zip(pieces, counts))
    x_arrs, x_shape, x_scratch, _, x_id = exchange or NO_EXCHANGE
    nx, n_in = len(x_arrs), len(pieces) + 1

    def body(*refs):
        b_ref, o_ref = refs[n_in - 1], refs[n_in + nx]
        finish_exchange = _carry_exchange(exchange or NO_EXCHANGE, refs, n_in, 1, *_sweep_marks(sum(counts)))
        j = pl.program_id(0)
        for p_ref, first, count in zip(refs, starts, counts):
            @pl.when((j >= first) & (j < first + count))
            def _():
                o_ref[...] = lax.dot_general(p_ref[...].astype(BF16), b_ref[...], TN,
                                             preferred_element_type=F32).astype(o_ref.dtype)
        finish_exchange()

    def piece_spec(first, count):
        return pl.BlockSpec((s, tj), lambda j: (0, jnp.clip(j - first, 0, count - 1)))

    res = pl.pallas_call(
        body, name=name, grid=(sum(counts),),
        in_specs=[piece_spec(f, c) for f, c in zip(starts, counts)]
        + [pl.BlockSpec((s, n), lambda j: (0, 0), pipeline_mode=pl.Buffered(1))] + [ANY] * nx,
        out_specs=[pl.BlockSpec((tj, n), lambda j: (j, 0))] + [ANY] * nx,
        out_shape=[jax.ShapeDtypeStruct((sum(counts) * tj, n), BF16)] + x_shape, scratch_shapes=x_scratch,
        compiler_params=_params(("arbitrary",), x_id),
    )(*pieces, b, *x_arrs)
    return res[0] if exchange is None else (res[0], res[1:])


NO_EXCHANGE = ([], [], [], None, None)


def _sweep_marks(nt):
    i = pl.program_id(0)
    return i == 0, i == nt - 1, i == nt - 1


def _rowwise(name, fn, rows, bcasts, row_outs, red_outs=(), tm=256, exchange=NO_EXCHANGE):
    s = rows[0].shape[0]
    tm = _pick(s, tm, 16)
    nt = s // tm
    resident = pl.Buffered(1)
    nr, nb, no, nd = len(rows), len(bcasts), len(row_outs), len(red_outs)
    x_arrs, x_shape, x_scratch, _, x_id = exchange
    nx = len(x_arrs)
    first_out = nr + nb + nx

    def body(*refs):
        finish_exchange = _carry_exchange(exchange, refs, nr + nb, no + nd, *_sweep_marks(nt))
        ins = [r[...] for r in refs[:nr + nb]]
        outs, reds = fn(*ins)
        for ref, val in zip(refs[first_out:first_out + no], outs):
            ref[...] = val.astype(ref.dtype)
        i = pl.program_id(0)
        for ref, val in zip(refs[first_out + no:first_out + no + nd], reds):
            @pl.when(i == 0)
            def _():
                ref[...] = val

            @pl.when(i > 0)
            def _():
                ref[...] += val
        finish_exchange()

    def row_spec(a):
        assert a.shape[-2] % nt == 0, (name, a.shape, nt)
        if len(a.shape) == 3:
            return pl.BlockSpec((a.shape[0], a.shape[1] // nt, a.shape[2]), lambda i: (0, i, 0))
        return pl.BlockSpec((a.shape[0] // nt, a.shape[1]), lambda i: (i, 0))

    in_specs = [row_spec(r) for r in rows]
    in_specs += [pl.BlockSpec(b.shape, lambda i: (0, 0), pipeline_mode=resident) for b in bcasts]
    out_specs = [row_spec(o) for o in row_outs]
    out_specs += [pl.BlockSpec(d.shape, lambda i: (0, 0)) for d in red_outs]
    return pl.pallas_call(
        body, name=name, grid=(nt,), in_specs=in_specs + [ANY] * nx, out_specs=out_specs + [ANY] * nx,
        out_shape=list(row_outs) + list(red_outs) + x_shape, scratch_shapes=x_scratch,
        compiler_params=_params(("arbitrary",), x_id),
    )(*rows, *bcasts, *x_arrs)


def _sds(shape, dtype=F32):
    return jax.ShapeDtypeStruct(shape, dtype)


def _rms(x, g):
    y = x * lax.rsqrt(jnp.mean(x * x, axis=-1, keepdims=True) + EPS)
    return y * g


def _silu(x):
    return x * jax.nn.sigmoid(x)


def _swiglu(g, u):
    return _silu(g) * u


def _ln_silu(u, g, b):
    mu = jnp.mean(u, axis=-1, keepdims=True)
    var = jnp.mean(jnp.square(u - mu), axis=-1, keepdims=True)
    return _silu((u - mu) * lax.rsqrt(var + EPS) * g + b)


def _merge(conv_pre, att_out, g_conv, g_att, b_cb):
    return jax.nn.sigmoid(g_conv) * (conv_pre + b_cb) + jax.nn.sigmoid(g_att) * att_out


def _glu(t):
    return t[:, :CONV_DIM] * jax.nn.sigmoid(t[:, CONV_DIM:])


def _shifted_reader(buf, shifted, tm):
    for b in range(1, SUBLANES):
        shifted[b - 1, :, :] = buf[pl.ds(b, tm + HALO - SUBLANES), :]

    def read(o):
        a, b = divmod(o, SUBLANES)
        return buf[pl.ds(SUBLANES * a, tm), :] if b == 0 else shifted[b - 1, pl.ds(SUBLANES * a, tm), :]

    return read


def _conv_fwd(conv_in, w_pad, b, ln_g, ln_b, exchange, tm=256):
    s = conv_in.shape[0]
    tm = _pick(s, tm, HALO)
    ratio = tm // HALO
    x_arrs, x_shape, x_scratch, _, x_id = exchange
    nx = len(x_arrs)

    def body(*refs):
        main_ref, halo_ref, w_ref, b_ref, g_ref, be_ref = refs[:6]
        u3_ref, u1_ref = refs[6 + nx:8 + nx]
        buf, shifted = refs[-2:]
        finish_exchange = _carry_exchange(exchange, refs, 6, 2, *_sweep_marks(s // tm))
        i = pl.program_id(0)
        buf[0:HALO, :] = _glu(halo_ref[...]) * (i > 0).astype(F32)
        buf[HALO:HALO + tm, :] = _glu(main_ref[...])
        read = _shifted_reader(buf, shifted, tm)
        acc = jnp.zeros((tm, CONV_DIM), F32) + b_ref[...]
        for j in range(CONV_WIDTH):
            acc = acc + w_ref[j:j + 1, :] * read(HALO - (CONV_WIDTH - 1) + j)
        u1_ref[...] = acc
        u3_ref[...] = _ln_silu(acc, g_ref[...], be_ref[...]).astype(u3_ref.dtype)
        finish_exchange()

    res = pl.pallas_call(
        body, name="conv_fwd", grid=(s // tm,),
        in_specs=[pl.BlockSpec((tm, 2 * CONV_DIM), lambda i: (i, 0)),
                  pl.BlockSpec((HALO, 2 * CONV_DIM), lambda i: (jnp.maximum(i * ratio - 1, 0), 0)),
                  pl.BlockSpec(w_pad.shape, lambda i: (0, 0)),
                  pl.BlockSpec(b.shape, lambda i: (0, 0)),
                  pl.BlockSpec(ln_g.shape, lambda i: (0, 0)),
                  pl.BlockSpec(ln_b.shape, lambda i: (0, 0))] + [ANY] * nx,
        out_specs=[pl.BlockSpec((tm, CONV_DIM), lambda i: (i, 0)),
                   pl.BlockSpec((tm, CONV_DIM), lambda i: (i, 0))] + [ANY] * nx,
        out_shape=[_sds((s, CONV_DIM), BF16), _sds((s, CONV_DIM), F32)] + x_shape,
        scratch_shapes=x_scratch + [pltpu.VMEM((tm + HALO, CONV_DIM), F32),
                                    pltpu.VMEM((SUBLANES - 1, tm + HALO - SUBLANES, CONV_DIM), F32)],
        compiler_params=_params(("arbitrary",), x_id),
    )(conv_in, conv_in, w_pad, b, ln_g, ln_b, *x_arrs)
    return res[0], res[1], res[2:]


def _conv_bwd(conv_in, u1, du3, ln_g, ln_b, w_pad, exchange, tm=256):
    s = conv_in.shape[0]
    tm = _pick(s, tm, HALO)
    ratio = tm // HALO
    nt = s // tm
    last_halo = s // HALO - 1
    x_arrs, x_shape, x_scratch, _, x_id = exchange
    nx = len(x_arrs)

    def body(*refs):
        main_ref, halo_ref, u1_ref, u1n_ref, du3_ref, du3n_ref, g_ref, be_ref, w_ref = refs[:9]
        dci_ref, dw_ref, db_ref, dg_ref, dbe_ref = refs[9 + nx:14 + nx]
        ubuf, dbuf, ushift, dshift = refs[-4:]
        finish_exchange = _carry_exchange(exchange, refs, 9, 5, *_sweep_marks(nt))
        i = pl.program_id(0)
        main = main_ref[...]
        a = main[:, :CONV_DIM]
        sb = jax.nn.sigmoid(main[:, CONV_DIM:])
        ubuf[0:HALO, :] = _glu(halo_ref[...]) * (i > 0).astype(F32)
        ubuf[HALO:HALO + tm, :] = a * sb

        def ln_bwd(u1t, du3t):
            _, vjp = jax.vjp(_ln_silu, u1t, g_ref[...], be_ref[...])
            return vjp(du3t)

        du, dg, dbe = ln_bwd(u1_ref[...], du3_ref[...])
        dbuf[0:tm, :] = du
        dbuf[tm:tm + HALO, :] = ln_bwd(u1n_ref[...], du3n_ref[...])[0] * (i < nt - 1).astype(F32)

        @pl.when(i == 0)
        def _():
            dw_ref[...] = jnp.zeros_like(dw_ref)
            db_ref[...] = jnp.zeros_like(db_ref)
            dg_ref[...] = jnp.zeros_like(dg_ref)
            dbe_ref[...] = jnp.zeros_like(dbe_ref)

        dg_ref[...] += dg
        dbe_ref[...] += dbe

        read_u = _shifted_reader(ubuf, ushift, tm)
        read_d = _shifted_reader(dbuf, dshift, tm)
        du0 = jnp.zeros((tm, CONV_DIM), F32)
        for j in range(CONV_WIDTH):
            du0 = du0 + w_ref[j:j + 1, :] * read_d(CONV_WIDTH - 1 - j)
            dw_ref[j:j + 1, :] += jnp.sum(du * read_u(HALO - (CONV_WIDTH - 1) + j), axis=0, keepdims=True)
        db_ref[...] += jnp.sum(du, axis=0, keepdims=True)
        dci_ref[:, :CONV_DIM] = (du0 * sb).astype(dci_ref.dtype)
        dci_ref[:, CONV_DIM:] = (du0 * a * sb * (1.0 - sb)).astype(dci_ref.dtype)
        finish_exchange()

    res = pl.pallas_call(
        body, name="conv_bwd", grid=(nt,),
        in_specs=[pl.BlockSpec((tm, 2 * CONV_DIM), lambda i: (i, 0)),
                  pl.BlockSpec((HALO, 2 * CONV_DIM), lambda i: (jnp.maximum(i * ratio - 1, 0), 0))]
        + [pl.BlockSpec((tm, CONV_DIM), lambda i: (i, 0)),
           pl.BlockSpec((HALO, CONV_DIM), lambda i: (jnp.minimum((i + 1) * ratio, last_halo), 0))] * 2
        + [pl.BlockSpec((1, CONV_DIM), lambda i: (0, 0))] * 2 + [pl.BlockSpec(w_pad.shape, lambda i: (0, 0))]
        + [ANY] * nx,
        out_specs=[pl.BlockSpec((tm, 2 * CONV_DIM), lambda i: (i, 0)),
                   pl.BlockSpec(w_pad.shape, lambda i: (0, 0))]
        + [pl.BlockSpec((1, CONV_DIM), lambda i: (0, 0))] * 3 + [ANY] * nx,
        out_shape=[_sds((s, 2 * CONV_DIM), BF16), _sds(w_pad.shape)] + [_sds((1, CONV_DIM))] * 3 + x_shape,
        scratch_shapes=x_scratch + [pltpu.VMEM((tm + HALO, CONV_DIM), F32)] * 2
        + [pltpu.VMEM((SUBLANES - 1, tm + HALO - SUBLANES, CONV_DIM), F32)] * 2,
        compiler_params=_params(("arbitrary",), x_id),
    )(conv_in, conv_in, u1, u1, du3, du3, ln_g, ln_b, w_pad, *x_arrs)
    return res[:5], res[5:]


def _logsig_neg(z):
    return jnp.minimum(-z, 0.0) - jnp.log(1.0 + jnp.exp(-jnp.abs(z)))


def _split_dot(val, tri):
    hi = val.astype(BF16)
    lo = (val - hi.astype(F32)).astype(BF16)
    return jnp.dot(hi, tri, preferred_element_type=F32) + jnp.dot(lo, tri, preferred_element_type=F32)


def _attn_masks(t, later):
    row = lax.broadcasted_iota(jnp.int32, (t, t), 0)
    col = lax.broadcasted_iota(jnp.int32, (t, t), 1)
    tri = jnp.where(row > col if later else row <= col, 1.0, 0.0).astype(BF16)
    return col < row, tri


def _grid_marks(h, nq):
    hh, i = pl.program_id(0), pl.program_id(1)
    return (hh == 0) & (i == 0), (hh == h - 1) & (i == nq // 2), (hh == h - 1) & (i == nq - 1)


def _head_masks(shape):
    lane = lax.broadcasted_iota(jnp.int32, shape, len(shape) - 1)
    return lane < HEAD_DIM, lane >= HEAD_DIM


def _per_head(blk):
    m0, m1 = _head_masks(blk.shape)
    zero = jnp.zeros_like(blk)
    return jnp.where(m0, blk, zero), jnp.where(m1, blk, zero)


NT = (((1,), (1,)), ((), ()))
TN = (((0,), (0,)), ((), ()))


def _with_top(whole, top):
    rows = top.shape[0]
    return top if rows == whole.shape[0] else jnp.concatenate([top, whole[rows:]], axis=0)


def _attn_fwd(q, k, v, exchange):
    s = q.shape[0]
    hp = q.shape[1] // LANES
    t = ATT_TILE
    scale = 1.0 / math.sqrt(HEAD_DIM)
    x_arrs, x_shape, x_scratch, _, x_id = exchange
    nx = len(x_arrs)

    def body(*refs):
        q_ref, k_ref, v_ref = refs[:3]
        o_ref, lt_ref, nb_ref = refs[3 + nx:6 + nx]
        finish_exchange = _carry_exchange(exchange, refs, 3, 3, *_grid_marks(hp, s // t))
        i = pl.program_id(1)
        qs = _per_head((q_ref[...].astype(F32) * scale).astype(BF16))
        causal, tri = _attn_masks(t, later=True)

        def step(kb, carry, masked, rows):
            cs, acc = carry
            off = pl.multiple_of(kb * t, t)
            kblk = k_ref[pl.ds(off, t), :]
            vs = _per_head(v_ref[pl.ds(off, t), :])
            acc_top = acc[:rows]
            new_cs = []
            for hd in range(2):
                z = lax.dot_general(qs[hd][:rows], kblk, NT, preferred_element_type=F32)
                l = _logsig_neg(z)
                if masked:
                    l = jnp.where(causal, l, 0.0)
                e = z + l + _split_dot(l, tri) + cs[hd][:rows]
                if masked:
                    e = jnp.where(causal, e, -1e30)
                acc_top = acc_top + jnp.dot(jnp.exp(e).astype(BF16), vs[hd], preferred_element_type=F32)
                new_cs.append(_with_top(cs[hd], cs[hd][:rows] + jnp.sum(l, axis=1, keepdims=True)))
            return tuple(new_cs), _with_top(acc, acc_top)

        zero = jnp.zeros((t, 1), F32)
        carry = step(i, ((zero, zero), jnp.zeros((t, LANES), F32)), True, t)

        def live(cs, lo, hi):
            return jnp.maximum(jnp.max(cs[0][lo:hi]), jnp.max(cs[1][lo:hi])) > DEAD_SUM

        def more(state):
            n, _, (cs, _) = state
            return (n < i) & live(cs, 0, t)

        def sweep(state):
            n, n_full, cr = state
            whole = live(cr[0], ATT_PART, t)
            cr = lax.cond(whole, lambda c: step(i - 1 - n, c, False, t), lambda c: step(i - 1 - n, c, False, ATT_PART), cr)
            return n + 1, n_full + whole.astype(jnp.int32), cr

        n_blocks, n_full, carry = lax.while_loop(more, sweep, (jnp.int32(0), jnp.int32(0), carry))
        m0, _ = _head_masks((t, LANES))
        lt_ref[...] = jnp.where(m0, carry[0][0], carry[0][1])
        o_ref[...] = carry[1].astype(o_ref.dtype)
        nb_ref[0, pl.program_id(0), i] = n_blocks.astype(F32)
        nb_ref[1, pl.program_id(0), i] = n_full.astype(F32)
        finish_exchange()

    res = pl.pallas_call(
        body, name="attn_fwd", grid=(hp, s // t),
        in_specs=[pl.BlockSpec((t, LANES), lambda p, i: (i, p)),
                  pl.BlockSpec((s, LANES), lambda p, i: (0, p)),
                  pl.BlockSpec((s, LANES), lambda p, i: (0, p))] + [ANY] * nx,
        out_specs=[pl.BlockSpec((t, LANES), lambda p, i: (i, p)),
                   pl.BlockSpec((None, t, LANES), lambda p, i: (p, i, 0)),
                   pl.BlockSpec(memory_space=pltpu.SMEM)] + [ANY] * nx,
        out_shape=[_sds(q.shape, BF16), _sds((hp, s, LANES), F32), _sds((2, hp, s // t), F32)] + x_shape,
        scratch_shapes=x_scratch,
        compiler_params=_params(("arbitrary", "arbitrary"), x_id),
    )(q, k, v, *x_arrs)
    return res[0], res[1], res[2], res[3:]


def _attn_bwd(q, k, v, do, ltot, n_blocks, exchange):
    s = q.shape[0]
    hp = q.shape[1] // LANES
    t = ATT_TILE
    scale = 1.0 / math.sqrt(HEAD_DIM)
    x_arrs, x_shape, x_scratch, _, x_id = exchange
    nx = len(x_arrs)

    def body(*refs):
        q_ref, k_ref, v_ref, do_ref, lt_ref, nb_ref = refs[:6]
        dq_ref, dk_ref, dv_ref = refs[6 + nx:9 + nx]
        finish_exchange = _carry_exchange(exchange, refs, 6, 3, *_grid_marks(hp, s // t))
        i = pl.program_id(1)
        n_blocks = jnp.clip(nb_ref[0, pl.program_id(0), i].astype(jnp.int32), 0, i)
        n_full = jnp.clip(nb_ref[1, pl.program_id(0), i].astype(jnp.int32), 0, n_blocks)

        @pl.when(i == 0)
        def _():
            dk_ref[...] = jnp.zeros_like(dk_ref)
            dv_ref[...] = jnp.zeros_like(dv_ref)

        qb = q_ref[...]
        qm = _per_head(qb)
        qs = _per_head((qb.astype(F32) * scale).astype(BF16))
        dos = _per_head(do_ref[...])
        lts = (lt_ref[:, 0:1], lt_ref[:, HEAD_DIM:HEAD_DIM + 1])
        causal, tri = _attn_masks(t, later=False)

        def step(kb, carry, masked, rows):
            cls, cgs, dq = carry
            off = pl.multiple_of(kb * t, t)
            kblk = k_ref[pl.ds(off, t), :]
            vblk = v_ref[pl.ds(off, t), :]
            ks = _per_head(kblk)
            dq_top = dq[:rows]
            dk = jnp.zeros((t, LANES), F32)
            dv = jnp.zeros((t, LANES), F32)
            new_cls, new_cgs = [], []
            for hd in range(2):
                z = lax.dot_general(qs[hd][:rows], kblk, NT, preferred_element_type=F32)
                l = _logsig_neg(z)
                if masked:
                    l = jnp.where(causal, l, 0.0)
                e = z + l + ((lts[hd][:rows] - cls[hd][:rows]) - _split_dot(l, tri))
                if masked:
                    e = jnp.where(causal, e, -1e30)
                a = jnp.exp(e)
                g = lax.dot_general(dos[hd][:rows], vblk, NT, preferred_element_type=F32) * a
                p = cgs[hd][:rows] + jnp.dot(g.astype(BF16), tri, preferred_element_type=F32) - g
                el = jnp.exp(l)
                dz = g * el - p * (1.0 - el)
                if masked:
                    dz = jnp.where(causal, dz, 0.0)
                dzb = (dz * scale).astype(BF16)
                dq_top = dq_top + jnp.dot(dzb, ks[hd], preferred_element_type=F32)
                dk = dk + lax.dot_general(dzb, qm[hd][:rows], TN, preferred_element_type=F32)
                dv = dv + lax.dot_general(a.astype(BF16), dos[hd][:rows], TN, preferred_element_type=F32)
                new_cls.append(_with_top(cls[hd], cls[hd][:rows] + jnp.sum(l, axis=1, keepdims=True)))
                new_cgs.append(_with_top(cgs[hd], cgs[hd][:rows] + jnp.sum(g, axis=1, keepdims=True)))
            dk_ref[pl.ds(off, t), :] += dk
            dv_ref[pl.ds(off, t), :] += dv
            return tuple(new_cls), tuple(new_cgs), _with_top(dq, dq_top)

        zero = jnp.zeros((t, 1), F32)
        init = ((zero, zero), (zero, zero), jnp.zeros((t, LANES), F32))
        carry = lax.fori_loop(i - n_blocks, i - n_full, lambda kb, cr: step(kb, cr, False, ATT_PART), init)
        carry = lax.fori_loop(i - n_full, i, lambda kb, cr: step(kb, cr, False, t), carry)
        carry = step(i, carry, True, t)
        dq_ref[...] = carry[2]
        finish_exchange()

    blk = pl.BlockSpec((t, LANES), lambda p, i: (i, p))
    whole = pl.BlockSpec((s, LANES), lambda p, i: (0, p))
    res = pl.pallas_call(
        body, name="attn_bwd", grid=(hp, s // t),
        in_specs=[blk, whole, whole, blk, pl.BlockSpec((None, t, LANES), lambda p, i: (p, i, 0)),
                  pl.BlockSpec(memory_space=pltpu.SMEM)] + [ANY] * nx,
        out_specs=[blk, whole, whole] + [ANY] * nx,
        out_shape=[_sds(q.shape)] * 3 + x_shape,
        scratch_shapes=x_scratch,
        compiler_params=_params(("arbitrary", "arbitrary"), x_id),
    )(q, k, v, do, ltot, n_blocks, *x_arrs)
    return res[0], res[1], res[2], res[3:]


LATE = ["w_conv_branch", "w_att_branch", "w_out", "w_ffn_up", "w_ffn_down"]


def _full_weight(name, gathered):
    return _cols_to_full(gathered) if name in COL_SHARDED else gathered.reshape(-1, gathered.shape[2])


def _grad_slabs(name, grad):
    return _full_to_cols(grad) if name in COL_SHARDED else grad.reshape(N_DEV, -1, grad.shape[1])


def _side_slabs(name, grad):
    slabs = _grad_slabs(name, grad)
    return slabs.reshape((4, 2) + slabs.shape[1:])


def _local_step(x, target, w, late_blocks, opt):
    s = x.shape[0]
    w = dict(w)
    g1, g2, g3, g4 = w["norm_mix_pre"], w["norm_mix_post"], w["norm_ffn_pre"], w["norm_ffn_post"]

    w_in = w["w_in"]

    def proj_fn(xt, g1_, w_in_t):
        h = _rms(xt, g1_).astype(BF16)
        proj = lax.dot_general(h, w_in_t, NT, preferred_element_type=F32)
        return (h, *[proj[:, IN_SPLITS[n]:IN_SPLITS[n + 1]] for n in range(6)]), ()

    mix_weights = ["w_conv_branch", "w_att_branch", "w_out"]
    h1, conv_in, q, k, v, g_conv, g_att, g_out = _rowwise(
        "norm_proj", proj_fn, [x], [g1, w_in],
        [_sds((s, D_MODEL), BF16), _sds((s, 2 * CONV_DIM)), _sds((s, ATT_DIM), BF16), _sds((s, ATT_DIM), BF16),
         _sds((s, ATT_DIM), BF16), _sds((s, D_MODEL), BF16), _sds((s, D_MODEL), BF16)], tm=512,
        exchange=_gather_exchange([late_blocks["w_out"]]))

    u3, u1, g_branches = _conv_fwd(conv_in, w["conv_dw_w"], w["conv_dw_b"], w["conv_ln_g"], w["conv_ln_b"],
                                   _gather_exchange([late_blocks[nm] for nm in mix_weights[:2]]))
    for nm, g in zip(mix_weights, [*g_branches, g_out]):
        w[nm] = _full_weight(nm, g)
    att, ltot, n_blocks, (g_up,) = _attn_fwd(q, k, v, _gather_exchange([late_blocks["w_ffn_up"]]))
    w["w_ffn_up"] = _full_weight("w_ffn_up", g_up)

    def merge_fn(u3t, at, gc, ga, xt, w_cb, w_ab, b_cb, w_out, g2_, g3_):
        cp = jnp.dot(u3t, w_cb, preferred_element_type=F32)
        ao = jnp.dot(at, w_ab, preferred_element_type=F32)
        mg = _merge(cp, ao, gc.astype(F32), ga.astype(F32), b_cb).astype(BF16)
        mix_ = jnp.dot(mg, w_out, preferred_element_type=F32)
        x2_ = xt + _rms(mix_, g2_)
        return (mg, cp, ao, mix_, x2_, _rms(x2_, g3_)), ()

    half = D_MODEL // 2
    down_block = late_blocks["w_ffn_down"]
    merged, conv_pre, att_out, mix, x2, h2, g_left = _rowwise(
        "branch_merge_mix", merge_fn, [u3, att, g_conv, g_att, x],
        [w["w_conv_branch"], w["w_att_branch"], w["b_conv_branch"], w["w_out"], g2, g3],
        [_sds((s, D_MODEL), BF16)] * 3 + [_sds((s, D_MODEL)), _sds((s, D_MODEL)), _sds((s, D_MODEL), BF16)], tm=512,
        exchange=_gather_exchange([down_block[:, :half]]))

    def ffn_up_fn(ht, w_up_t):
        gu_ = lax.dot_general(ht, w_up_t, NT, preferred_element_type=F32)
        return (gu_, _swiglu(gu_[:, :D_FF], gu_[:, D_FF:])), ()

    gu, act, g_right = _rowwise("ffn_up", ffn_up_fn, [h2], [w["w_ffn_up"]],
                                [_sds((s, 2 * D_FF), BF16), _sds((s, D_FF), BF16)], tm=512,
                                exchange=_gather_exchange([down_block[:, half:]]))
    w_down = [_full_weight("w_ffn_down", g) for g in (g_left, g_right)]

    def final_fn(at, x2t, tgt, w_left, w_right, g4_):
        ff = jnp.concatenate([jnp.dot(at, w_left, preferred_element_type=F32),
                              jnp.dot(at, w_right, preferred_element_type=F32)], axis=1)
        n4, vjp = jax.vjp(_rms, ff, g4_)
        err = x2t + n4 - tgt
        dy = err * (1.0 / D_MODEL)
        dff, dg4 = vjp(dy)
        return (dy, dff), (jnp.sum(err * err, axis=0, keepdims=True), dg4)

    dy, dff, loss_cols, d_g4 = _rowwise("ffn_down_loss", final_fn, [act, x2, target], [*w_down, g4],
                                        [_sds((s, D_MODEL)), _sds((s, D_MODEL), BF16)],
                                        [_sds((1, D_MODEL)), _sds((1, D_MODEL))], tm=512)
    loss = 0.5 * jnp.sum(loss_cols) / D_MODEL

    d_w_down = _tn_matmul(act, dff, name="d_w_down")

    def act_bwd_fn(dfft, gut, w_left, w_right):
        d_act = (lax.dot_general(dfft[:, :half], w_left, NT, preferred_element_type=F32)
                 + lax.dot_general(dfft[:, half:], w_right, NT, preferred_element_type=F32))
        gu_ = gut.astype(F32)
        _, vjp = jax.vjp(_swiglu, gu_[:, :D_FF], gu_[:, D_FF:])
        return (jnp.concatenate(vjp(d_act), axis=1),), ()

    down_slabs = _side_slabs("w_ffn_down", d_w_down)
    dgu, theirs = _rowwise("ffn_act_bwd", act_bwd_fn, [dff, gu], w_down, [_sds((s, 2 * D_FF), BF16)],
                           exchange=_pair_exchange([down_slabs]))
    down_sums = _pair_sum("pair_sum_w_ffn_down", down_slabs, theirs)
    d_w_up = _tn_matmul(dgu, h2, name="d_w_up")
    received = {}
    up_slabs = _side_slabs("w_ffn_up", d_w_up)

    def mid_bwd_fn(dgut, xt, mt, dyt, w_up_t, g2_, g3_):
        dh = jnp.dot(dgut, w_up_t, preferred_element_type=F32)
        n2, vjp2 = jax.vjp(_rms, mt, g2_)
        x2_ = xt + n2
        _, vjp3 = jax.vjp(_rms, x2_, g3_)
        dx2_, dg3 = vjp3(dh)
        dx2_ = dx2_ + dyt
        dmix_, dg2 = vjp2(dx2_)
        return (dx2_, dmix_), (dg2, dg3)

    dx2, dmix, d_g2, d_g3, received["w_ffn_down"] = _rowwise(
        "ffn_up_mid_bwd", mid_bwd_fn, [dgu, x, mix, dy], [w["w_ffn_up"], g2, g3],
        [_sds((s, D_MODEL)), _sds((s, D_MODEL), BF16)], [_sds((1, D_MODEL)), _sds((1, D_MODEL))], tm=512,
        exchange=_chip_exchange([down_sums]))
    d_w_out = _tn_matmul(merged, dmix, name="d_w_out")

    def merge_bwd_fn(dmt, cp, ao, gc, ga, w_out, w_cb, w_ab, b_cb):
        dm = lax.dot_general(dmt, w_out, NT, preferred_element_type=F32)
        _, vjp = jax.vjp(_merge, cp.astype(F32), ao.astype(F32), gc.astype(F32), ga.astype(F32), b_cb)
        dcp, dao, dgc, dga, dbias = vjp(dm)
        dcp, dao = dcp.astype(BF16), dao.astype(BF16)
        du3_ = lax.dot_general(dcp, w_cb, NT, preferred_element_type=F32)
        datt_ = lax.dot_general(dao, w_ab, NT, preferred_element_type=F32)
        return (dcp, dao, dgc, dga, du3_, datt_), (dbias,)

    d_conv_out, d_att_out, d_g_conv, d_g_att, du3, d_att, d_b_cb, theirs = _rowwise(
        "merge_bwd", merge_bwd_fn, [dmix, conv_pre, att_out, g_conv, g_att],
        [w["w_out"], w["w_conv_branch"], w["w_att_branch"], w["b_conv_branch"]],
        [_sds((s, D_MODEL), BF16)] * 4 + [_sds((s, CONV_DIM)), _sds((s, ATT_DIM), BF16)], [_sds((1, D_MODEL))], tm=512,
        exchange=_pair_exchange([up_slabs]))

    d_w_cb = _tn_matmul(u3, d_conv_out, name="d_w_conv_branch")
    d_w_ab = _tn_matmul(att, d_att_out, name="d_w_att_branch")

    dq, dk, dv, (received["w_ffn_up"],) = _attn_bwd(
        q, k, v, d_att, ltot, n_blocks, _chip_exchange([_pair_sum("pair_sum_w_ffn_up", up_slabs, theirs)]))

    mix_grads = {"w_conv_branch": d_w_cb, "w_att_branch": d_w_ab, "w_out": d_w_out}
    (d_conv_in, d_dw_w, d_dw_b, d_ln_g, d_ln_b), landed = _conv_bwd(
        conv_in, u1, du3, w["conv_ln_g"], w["conv_ln_b"], w["conv_dw_w"],
        _scatter_exchange([_grad_slabs(nm, mix_grads[nm]) for nm in mix_weights[:2]]))
    received.update(zip(mix_weights[:2], landed))

    d_proj = [d_conv_in, dq, dk, dv, d_g_conv, d_g_att]
    d_w_in, (received["w_out"],) = _pieces_tn_matmul(
        d_proj, h1, name="d_w_in", exchange=_scatter_exchange([_grad_slabs("w_out", d_w_out)]))
    in_slabs = _side_slabs("w_in", d_w_in)
    (theirs,) = _exchange_call("pair_swap_w_in", _pair_exchange([in_slabs]))

    early = list(opt)

    def pre_bwd_fn(*args):
        groups, (xt, dx2t), jobs, (w_in_t, g_) = args[:6], args[6:8], args[8:-2], args[-2:]
        dh = sum(jnp.dot(grp.astype(BF16), w_in_t[IN_SPLITS[n]:IN_SPLITS[n + 1]], preferred_element_type=F32)
                 for n, grp in enumerate(groups))
        _, vjp = jax.vjp(_rms, xt, g_)
        dx_, dg_ = vjp(dh)
        updates = [_sum_adamw_tile(*jobs[4 * n:4 * n + 4]) for n in range(len(early))]
        return (dx_ + dx2t, *[u for four in updates for u in four]), (dg_,)

    res = _rowwise(
        "proj_norm_bwd", pre_bwd_fn,
        d_proj + [x, dx2] + [a for nm in early for a in (received[nm], *opt[nm])], [w_in, g1],
        [_sds((s, D_MODEL))] + [_sds(opt[nm][0].shape) for nm in early for _ in range(4)],
        [_sds((1, D_MODEL))], tm=512, exchange=_chip_exchange([_pair_sum("pair_sum_w_in", in_slabs, theirs)]))
    grad_x, d_g1, received["w_in"] = res[0], res[-2], res[-1]
    updated = {nm: res[1 + 4 * n:5 + 4 * n] for n, nm in enumerate(early)}

    grads = {
        "norm_mix_pre": d_g1, "conv_dw_w": d_dw_w, "conv_dw_b": d_dw_b,
        "conv_ln_g": d_ln_g, "conv_ln_b": d_ln_b, "b_conv_branch": d_b_cb,
        "norm_mix_post": d_g2, "norm_ffn_pre": d_g3, "norm_ffn_post": d_g4,
    }
    return loss, grad_x, received, updated, grads


def _place():
    x, y, c = lax.axis_index("x"), lax.axis_index("y"), lax.axis_index("c")
    return x, y, c


def _slot(px, py, pc):
    return 4 * px + 2 * py + pc


def _exchange_scratch(n):
    return [pltpu.SemaphoreType.DMA((7 * n,)), pltpu.SemaphoreType.DMA((7 * n,)), pltpu.SemaphoreType.DMA((n,))]


GATHER_ID, SCATTER_ID, PAIR_ID, CHIP_ID = 0, 1, 2, 3


def _handshake(peers):
    barrier = pltpu.get_barrier_semaphore()
    for peer in peers:
        pl.semaphore_signal(barrier, inc=1, device_id=peer, device_id_type=MESH)
    pl.semaphore_wait(barrier, len(peers))


def _gather_exchange(arrs):
    n = len(arrs)

    def phases(ins, outs, send_sems, recv_sems, local_sems):
        x, y, c = _place()
        me, sibling = (x, y, c), (x, y, 1 - c)
        chips = [(1 - x, y), (x, 1 - y), (1 - x, 1 - y)]

        def copy(a, kk, block, to, src=None):
            dst = outs[a].at[_slot(*block)]
            return pltpu.make_async_remote_copy(
                src_ref=dst if src is None else src, dst_ref=dst,
                send_sem=send_sems.at[a * 7 + kk], recv_sem=recv_sems.at[a * 7 + kk],
                device_id=to, device_id_type=MESH)

        mine = [pltpu.make_async_copy(ins[a], outs[a].at[_slot(*me)], local_sems.at[a]) for a in range(n)]
        first = []
        for a in range(n):
            first.append(copy(a, 0, me, sibling, src=ins[a]))
            first += [copy(a, 1 + j, me, (*chip, c), src=ins[a]) for j, chip in enumerate(chips)]
        passed = [copy(a, 4 + j, (*chip, c), sibling) for j, chip in enumerate(chips) for a in range(n)]

        def send():
            _handshake([sibling] + [(*chip, c) for chip in chips])
            for cp in mine + first:
                cp.start()

        def pass_on():
            for j, chip in enumerate(chips):
                for a in range(n):
                    copy(a, 1 + j, (*chip, c), me).wait_recv()
                    passed[j * n + a].start()

        def finish():
            for a in range(n):
                copy(a, 0, sibling, me).wait_recv()
                for j, chip in enumerate(chips):
                    copy(a, 4 + j, (*chip, 1 - c), me).wait_recv()
            for cp in first + passed:
                cp.wait_send()
            for cp in mine:
                cp.wait()

        return [send, pass_on, finish]

    return list(arrs), [_sds((N_DEV,) + a.shape, a.dtype) for a in arrs], _exchange_scratch(n), phases, GATHER_ID


def _scatter_exchange(arrs):
    n = len(arrs)
    flips = [(fx, fy, fc) for fx in (0, 1) for fy in (0, 1) for fc in (0, 1)][1:]

    def phases(ins, outs, send_sems, recv_sems, local_sems):
        x, y, c = _place()
        mine = _slot(x, y, c)
        local = [pltpu.make_async_copy(ins[a].at[mine], outs[a].at[mine], local_sems.at[a]) for a in range(n)]
        peers = [((1 - x) if fx else x, (1 - y) if fy else y, (1 - c) if fc else c) for fx, fy, fc in flips]

        def copy(a, kk, src_slot, dst_slot):
            return pltpu.make_async_remote_copy(
                src_ref=ins[a].at[src_slot], dst_ref=outs[a].at[dst_slot],
                send_sem=send_sems.at[a * 7 + kk], recv_sem=recv_sems.at[a * 7 + kk],
                device_id=peers[kk], device_id_type=MESH)

        sends = [copy(a, kk, _slot(*peers[kk]), mine) for a in range(n) for kk in range(7)]

        def send():
            _handshake(peers)
            for cp in local + sends:
                cp.start()

        def finish():
            for a in range(n):
                for kk in range(7):
                    copy(a, kk, mine, _slot(*peers[kk])).wait_recv()
            for cp in sends:
                cp.wait_send()
            for cp in local:
                cp.wait()

        return [send, finish]

    return list(arrs), [_sds(a.shape, a.dtype) for a in arrs], _exchange_scratch(n), phases, SCATTER_ID


def _pair_exchange(arrs):
    n = len(arrs)

    def phases(ins, outs, send_sems, recv_sems, local_sems):
        x, y, c = _place()

        def copy(a, chip, side):
            return pltpu.make_async_remote_copy(
                src_ref=ins[a].at[chip, side], dst_ref=outs[a].at[chip],
                send_sem=send_sems.at[a * 7 + chip], recv_sem=recv_sems.at[a * 7 + chip],
                device_id=(x, y, 1 - c), device_id_type=MESH)

        sends = [copy(a, chip, 1 - c) for a in range(n) for chip in range(4)]

        def send():
            _handshake([(x, y, 1 - c)])
            for cp in sends:
                cp.start()

        def finish():
            for a in range(n):
                for chip in range(4):
                    copy(a, chip, c).wait_recv()
            for cp in sends:
                cp.wait_send()

        return [send, finish]

    return list(arrs), [_sds((4,) + a.shape[2:], a.dtype) for a in arrs], _exchange_scratch(n), phases, PAIR_ID


def _chip_exchange(arrs):
    n = len(arrs)

    def phases(ins, outs, send_sems, recv_sems, local_sems):
        x, y, c = _place()
        mine = 2 * x + y
        chips = [(1 - x, y), (x, 1 - y), (1 - x, 1 - y)]
        local = [pltpu.make_async_copy(ins[a].at[mine], outs[a].at[mine], local_sems.at[a]) for a in range(n)]

        def copy(a, j, src_slot, dst_slot):
            return pltpu.make_async_remote_copy(
                src_ref=ins[a].at[src_slot], dst_ref=outs[a].at[dst_slot],
                send_sem=send_sems.at[a * 7 + j], recv_sem=recv_sems.at[a * 7 + j],
                device_id=(*chips[j], c), device_id_type=MESH)

        sends = [copy(a, j, 2 * chips[j][0] + chips[j][1], mine) for a in range(n) for j in range(3)]

        def send():
            _handshake([(*chip, c) for chip in chips])
            for cp in local + sends:
                cp.start()

        def finish():
            for a in range(n):
                for j in range(3):
                    copy(a, j, mine, 2 * chips[j][0] + chips[j][1]).wait_recv()
            for cp in sends:
                cp.wait_send()
            for cp in local:
                cp.wait()

        return [send, finish]

    return list(arrs), [_sds(a.shape, a.dtype) for a in arrs], _exchange_scratch(n), phases, CHIP_ID


def _pair_sum(name, mine, theirs):
    _, _, r, c = mine.shape

    def body(side_ref, m_ref, t_ref, o_ref):
        o_ref[...] = (m_ref[...].astype(F32) + t_ref[...].astype(F32)).astype(o_ref.dtype)

    return pl.pallas_call(
        body, name=name,
        grid_spec=pltpu.PrefetchScalarGridSpec(
            num_scalar_prefetch=1, grid=(4,),
            in_specs=[pl.BlockSpec((None, None, r, c), lambda j, side: (j, side[0], 0, 0)),
                      pl.BlockSpec((None, r, c), lambda j, side: (j, 0, 0))],
            out_specs=pl.BlockSpec((None, r, c), lambda j, side: (j, 0, 0))),
        out_shape=_sds(theirs.shape, theirs.dtype),
        compiler_params=_params(("parallel",)),
    )(lax.axis_index("c").astype(jnp.int32).reshape(1), mine, theirs)


def _exchange_call(name, exchange):
    arrs, out_shape, scratch, phases, collective_id = exchange
    n = len(arrs)

    def body(*refs):
        for step in phases(refs[:n], refs[n:2 * n], *refs[2 * n:]):
            step()

    return pl.pallas_call(body, name=name, in_specs=[ANY] * n, out_specs=[ANY] * n,
                          out_shape=out_shape, scratch_shapes=scratch,
                          compiler_params=pltpu.CompilerParams(collective_id=collective_id))(*arrs)


def _carry_exchange(exchange, refs, n_in, n_out, first, middle, last):
    arrs, _, _, phases, _ = exchange
    n = len(arrs)
    if n == 0:
        return lambda: None
    ins = refs[n_in:n_in + n]
    outs = refs[n_in + n + n_out:n_in + 2 * n + n_out]
    sems = n_in + 2 * n + n_out
    steps = phases(ins, outs, *refs[sems:sems + 3])
    pl.when(first)(steps[0])
    if len(steps) == 3:
        pl.when(middle)(steps[1])
    return lambda: pl.when(last)(steps[-1])


def _adamw_math(w, g, m, v):
    m2 = ADAM_B1 * m + (1.0 - ADAM_B1) * g
    v2 = ADAM_B2 * v + (1.0 - ADAM_B2) * jnp.square(g)
    m_hat = m2 / (1.0 - ADAM_B1 ** ADAM_STEP)
    v_hat = v2 / (1.0 - ADAM_B2 ** ADAM_STEP)
    delta = -ADAM_LR * (m_hat / (jnp.sqrt(v_hat) + ADAM_EPS) + ADAM_WD * w)
    return delta, m2, v2


def _sum_adamw_tile(parts, w, m, v):
    g = parts[0].astype(F32)
    for d in range(1, parts.shape[0]):
        g = g + parts[d].astype(F32)
    return (g, *_adamw_math(w, g, m, v))


def _sum_adamw(name, parts, w, m, v, tr=256):
    p, r, c = parts.shape
    tr = _pick(r, tr, 16)

    def body(p_ref, w_ref, m_ref, v_ref, g_ref, d_ref, m2_ref, v2_ref):
        g_ref[...], d_ref[...], m2_ref[...], v2_ref[...] = _sum_adamw_tile(p_ref[...], w_ref[...], m_ref[...], v_ref[...])

    tile = pl.BlockSpec((tr, c), lambda i: (i, 0))
    return pl.pallas_call(
        body, name=name, grid=(r // tr,),
        in_specs=[pl.BlockSpec((p, tr, c), lambda i: (0, i, 0)), tile, tile, tile],
        out_specs=[tile] * 4, out_shape=[_sds((r, c))] * 4,
        compiler_params=_params(("parallel",)),
    )(parts, w, m, v)


def _sum_parts(name, parts):
    p, r, c = parts.shape

    def body(p_ref, o_ref):
        g = p_ref[0]
        for d in range(1, p):
            g = g + p_ref[d]
        o_ref[...] = g

    return pl.pallas_call(
        body, name=name, out_shape=_sds((r, c)),
        in_specs=[pl.BlockSpec(memory_space=pltpu.VMEM)], out_specs=pl.BlockSpec(memory_space=pltpu.VMEM),
    )(parts)


WEIGHTS = ["norm_mix_pre", "w_in", "conv_dw_w", "conv_dw_b", "conv_ln_g", "conv_ln_b", "w_conv_branch",
           "b_conv_branch", "w_att_branch", "w_out", "norm_mix_post", "norm_ffn_pre", "w_ffn_up", "w_ffn_down",
           "norm_ffn_post"]
COL_SHARDED = ["w_conv_branch", "w_att_branch"]
TRANSPOSED = ["w_in", "w_ffn_up"]
VECTORS = ["norm_mix_pre", "conv_dw_b", "conv_ln_g", "conv_ln_b", "b_conv_branch", "norm_mix_post",
           "norm_ffn_pre", "norm_ffn_post"]


def _cols_to_full(g):
    return g.transpose(1, 0, 2).reshape(g.shape[1], N_DEV * g.shape[2])


def _full_to_cols(f):
    return f.reshape(f.shape[0], N_DEV, f.shape[1] // N_DEV).transpose(1, 0, 2)


PACK_ROWS = 7


def _pack_vectors(vecs, extra=None):
    parts = [vecs[nm].reshape(-1) for nm in VECTORS]
    parts.append(jnp.zeros((1,), F32) if extra is None else extra.reshape(1))
    used = sum(p.size for p in parts)
    parts.append(jnp.zeros((PACK_ROWS * D_MODEL - used,), F32))
    return jnp.concatenate(parts).reshape(PACK_ROWS, D_MODEL)


def _unpack_vectors(packed, sizes):
    flat, out, at = packed.reshape(-1), {}, 0
    for nm in VECTORS:
        out[nm] = flat[at:at + sizes[nm]]
        at += sizes[nm]
    return out, flat[at]


def kernel(x, norm_mix_pre, w_in, conv_dw_w, conv_dw_b, conv_ln_g, conv_ln_b, w_conv_branch, b_conv_branch, w_att_branch, w_out, norm_mix_post, norm_ffn_pre, w_ffn_up, w_ffn_down, norm_ffn_post, loss_target, m_norm_mix_pre, m_w_in, m_conv_dw_w, m_conv_dw_b, m_conv_ln_g, m_conv_ln_b, m_w_conv_branch, m_b_conv_branch, m_w_att_branch, m_w_out, m_norm_mix_post, m_norm_ffn_pre, m_w_ffn_up, m_w_ffn_down, m_norm_ffn_post, v_norm_mix_pre, v_w_in, v_conv_dw_w, v_conv_dw_b, v_conv_ln_g, v_conv_ln_b, v_w_conv_branch, v_b_conv_branch, v_w_att_branch, v_w_out, v_norm_mix_post, v_norm_ffn_pre, v_w_ffn_up, v_w_ffn_down, v_norm_ffn_post):
    ws = dict(zip(WEIGHTS, [norm_mix_pre, w_in, conv_dw_w, conv_dw_b, conv_ln_g, conv_ln_b, w_conv_branch,
                            b_conv_branch, w_att_branch, w_out, norm_mix_post, norm_ffn_pre, w_ffn_up, w_ffn_down,
                            norm_ffn_post]))
    ms = dict(zip(WEIGHTS, [m_norm_mix_pre, m_w_in, m_conv_dw_w, m_conv_dw_b, m_conv_ln_g, m_conv_ln_b,
                            m_w_conv_branch, m_b_conv_branch, m_w_att_branch, m_w_out, m_norm_mix_post,
                            m_norm_ffn_pre, m_w_ffn_up, m_w_ffn_down, m_norm_ffn_post]))
    vs = dict(zip(WEIGHTS, [v_norm_mix_pre, v_w_in, v_conv_dw_w, v_conv_dw_b, v_conv_ln_g, v_conv_ln_b,
                            v_w_conv_branch, v_b_conv_branch, v_w_att_branch, v_w_out, v_norm_mix_post,
                            v_norm_ffn_pre, v_w_ffn_up, v_w_ffn_down, v_norm_ffn_post]))

    dw_block = jnp.pad(conv_dw_w, ((0, 1), (0, 0)))
    g_in, g_dw = _exchange_call("gather_first", _gather_exchange([w_in.T.astype(BF16), dw_block]))
    full = {"w_in": _full_weight("w_in", g_in), "conv_dw_w": _cols_to_full(g_dw)}
    for nm in VECTORS:
        full[nm] = ws[nm].reshape(1, -1)

    def as_kept(nm, a):
        return a.T if nm in TRANSPOSED else a

    ride_along = ["w_ffn_up", "w_out"]
    loss_local, grad_x, received, updated, grads = _local_step(
        x[0], loss_target[0], full, {nm: as_kept(nm, ws[nm]).astype(BF16) for nm in LATE},
        {nm: tuple(as_kept(nm, a[nm]) for a in (ws, ms, vs)) for nm in ride_along})

    small = _exchange_call("gather_small_grads", _gather_exchange(
        [_pack_vectors(grads, extra=loss_local), grads["conv_dw_w"]]))
    out_g, out_d, out_m, out_v = {}, {}, {}, {}
    for nm in LATE + ["w_in"]:
        res = updated[nm] if nm in updated else _sum_adamw(
            "adamw_" + nm, received[nm], *[as_kept(nm, a[nm]) for a in (ws, ms, vs)])
        out_g[nm], out_d[nm], out_m[nm], out_v[nm] = [as_kept(nm, r) for r in res]
    sizes = {nm: ws[nm].size for nm in VECTORS}
    vec = _sum_adamw("adamw_vectors", small[0], _pack_vectors(ws), _pack_vectors(ms), _pack_vectors(vs))
    for res, dst in zip(vec, (out_g, out_d, out_m, out_v)):
        dst.update(_unpack_vectors(res, sizes)[0])
    loss = _unpack_vectors(vec[0], sizes)[1]
    dw_full = _sum_parts("sum_dw_grads", small[1])
    me = _slot(*_place())
    dw_mine = lax.dynamic_slice(dw_full, (0, me * (CONV_DIM // N_DEV)), (CONV_WIDTH, CONV_DIM // N_DEV))
    nm = "conv_dw_w"
    out_g[nm], out_d[nm], out_m[nm], out_v[nm] = _sum_adamw("adamw_dw", dw_mine[None], ws[nm], ms[nm], vs[nm])

    outs = [loss, grad_x[None]]
    for group in (out_g, out_d, out_m, out_v):
        outs += [group[nm] for nm in WEIGHTS]
    return tuple(outs)
```

```python
import math

import jax
import jax.numpy as jnp
from jax import lax
from jax.experimental import pallas as pl
from jax.experimental.pallas import tpu as pltpu

F32 = jnp.float32
BF16 = jnp.bfloat16

N_DEV = 8
D_MODEL = 1024
CONV_DIM = 512
CONV_WIDTH = 31
N_HEADS = 8
HEAD_DIM = 64
ATT_DIM = N_HEADS * HEAD_DIM
D_FF = 2816
EPS = 1e-6
IN_SPLITS = (0, 1024, 1536, 2048, 2560, 3584, 4608)

ADAM_LR = 0.001
ADAM_B1 = 0.9
ADAM_B2 = 0.999
ADAM_EPS = 1e-08
ADAM_WD = 0.01
ADAM_STEP = 10

LANES = 128
SUBLANES = 8
HALO = 32
ATT_TILE = 256
ATT_PART = 176
DEAD_SUM = -120.0
VMEM_LIMIT = 56 * 1024 * 1024
MESH = pl.DeviceIdType.MESH
ANY = pl.BlockSpec(memory_space=pl.ANY)


def _pick(dim, target, align=LANES):
    t = min(dim, target)
    t -= t % align
    while t >= align:
        if dim % t == 0:
            return t
        t -= align
    return dim


def _params(semantics, collective_id=None):
    return pltpu.CompilerParams(dimension_semantics=semantics, vmem_limit_bytes=VMEM_LIMIT,
                                collective_id=collective_id)


def _tn_matmul(a, b, *, name):
    return _pieces_tn_matmul([a], b, name=name, tj=_pick(a.shape[1], 1408))


def _pieces_tn_matmul(pieces, b, *, name, tj=512, exchange=None):
    s, n = b.shape
    counts = [p.shape[1] // tj for p in pieces]
    starts = [sum(counts[:i]) for i in range(len(pieces))]
    assert all(p.shape == (s, c * tj) for p, c in zip(pieces, counts))
    x_arrs, x_shape, x_scratch, _, x_id = exchange or NO_EXCHANGE
    nx, n_in = len(x_arrs), len(pieces) + 1

    def body(*refs):
        b_ref, o_ref = refs[n_in - 1], refs[n_in + nx]
        finish_exchange = _carry_exchange(exchange or NO_EXCHANGE, refs, n_in, 1, *_sweep_marks(sum(counts)))
        j = pl.program_id(0)
        for p_ref, first, count in zip(refs, starts, counts):
            @pl.when((j >= first) & (j < first + count))
            def _():
                o_ref[...] = lax.dot_general(p_ref[...].astype(BF16), b_ref[...], TN,
                                             preferred_element_type=F32).astype(o_ref.dtype)
        finish_exchange()

    def piece_spec(first, count):
        return pl.BlockSpec((s, tj), lambda j: (0, jnp.clip(j - first, 0, count - 1)))

    res = pl.pallas_call(
        body, name=name, grid=(sum(counts),),
        in_specs=[piece_spec(f, c) for f, c in zip(starts, counts)]
        + [pl.BlockSpec((s, n), lambda j: (0, 0), pipeline_mode=pl.Buffered(1))] + [ANY] * nx,
        out_specs=[pl.BlockSpec((tj, n), lambda j: (j, 0))] + [ANY] * nx,
        out_shape=[jax.ShapeDtypeStruct((sum(counts) * tj, n), BF16)] + x_shape, scratch_shapes=x_scratch,
        compiler_params=_params(("arbitrary",), x_id),
    )(*pieces, b, *x_arrs)
    return res[0] if exchange is None else (res[0], res[1:])


NO_EXCHANGE = ([], [], [], None, None)


def _sweep_marks(nt):
    i = pl.program_id(0)
    return i == 0, i == nt - 1, i == nt - 1


def _rowwise(name, fn, rows, bcasts, row_outs, red_outs=(), tm=256, exchange=NO_EXCHANGE):
    s = rows[0].shape[0]
    tm = _pick(s, tm, 16)
    nt = s // tm
    resident = pl.Buffered(1)
    nr, nb, no, nd = len(rows), len(bcasts), len(row_outs), len(red_outs)
    x_arrs, x_shape, x_scratch, _, x_id = exchange
    nx = len(x_arrs)
    first_out = nr + nb + nx

    def body(*refs):
        finish_exchange = _carry_exchange(exchange, refs, nr + nb, no + nd, *_sweep_marks(nt))
        ins = [r[...] for r in refs[:nr + nb]]
        outs, reds = fn(*ins)
        for ref, val in zip(refs[first_out:first_out + no], outs):
            ref[...] = val.astype(ref.dtype)
        i = pl.program_id(0)
        for ref, val in zip(refs[first_out + no:first_out + no + nd], reds):
            @pl.when(i == 0)
            def _():
                ref[...] = val

            @pl.when(i > 0)
            def _():
                ref[...] += val
        finish_exchange()

    def row_spec(a):
        assert a.shape[-2] % nt == 0, (name, a.shape, nt)
        if len(a.shape) == 3:
            return pl.BlockSpec((a.shape[0], a.shape[1] // nt, a.shape[2]), lambda i: (0, i, 0))
        return pl.BlockSpec((a.shape[0] // nt, a.shape[1]), lambda i: (i, 0))

    in_specs = [row_spec(r) for r in rows]
    in_specs += [pl.BlockSpec(b.shape, lambda i: (0, 0), pipeline_mode=resident) for b in bcasts]
    out_specs = [row_spec(o) for o in row_outs]
    out_specs += [pl.BlockSpec(d.shape, lambda i: (0, 0)) for d in red_outs]
    return pl.pallas_call(
        body, name=name, grid=(nt,), in_specs=in_specs + [ANY] * nx, out_specs=out_specs + [ANY] * nx,
        out_shape=list(row_outs) + list(red_outs) + x_shape, scratch_shapes=x_scratch,
        compiler_params=_params(("arbitrary",), x_id),
    )(*rows, *bcasts, *x_arrs)


def _sds(shape, dtype=F32):
    return jax.ShapeDtypeStruct(shape, dtype)


def _rms(x, g):
    y = x * lax.rsqrt(jnp.mean(x * x, axis=-1, keepdims=True) + EPS)
    return y * g


def _silu(x):
    return x * jax.nn.sigmoid(x)


def _swiglu(g, u):
    return _silu(g) * u


def _ln_silu(u, g, b):
    mu = jnp.mean(u, axis=-1, keepdims=True)
    var = jnp.mean(jnp.square(u - mu), axis=-1, keepdims=True)
    return _silu((u - mu) * lax.rsqrt(var + EPS) * g + b)


def _merge(conv_pre, att_out, g_conv, g_att, b_cb):
    return jax.nn.sigmoid(g_conv) * (conv_pre + b_cb) + jax.nn.sigmoid(g_att) * att_out


def _glu(t):
    return t[:, :CONV_DIM] * jax.nn.sigmoid(t[:, CONV_DIM:])


def _shifted_reader(buf, shifted, tm):
    for b in range(1, SUBLANES):
        shifted[b - 1, :, :] = buf[pl.ds(b, tm + HALO - SUBLANES), :]

    def read(o):
        a, b = divmod(o, SUBLANES)
        return buf[pl.ds(SUBLANES * a, tm), :] if b == 0 else shifted[b - 1, pl.ds(SUBLANES * a, tm), :]

    return read


def _conv_fwd(conv_in, w_pad, b, ln_g, ln_b, exchange, tm=256):
    s = conv_in.shape[0]
    tm = _pick(s, tm, HALO)
    ratio = tm // HALO
    x_arrs, x_shape, x_scratch, _, x_id = exchange
    nx = len(x_arrs)

    def body(*refs):
        main_ref, halo_ref, w_ref, b_ref, g_ref, be_ref = refs[:6]
        u3_ref, u1_ref = refs[6 + nx:8 + nx]
        buf, shifted = refs[-2:]
        finish_exchange = _carry_exchange(exchange, refs, 6, 2, *_sweep_marks(s // tm))
        i = pl.program_id(0)
        buf[0:HALO, :] = _glu(halo_ref[...]) * (i > 0).astype(F32)
        buf[HALO:HALO + tm, :] = _glu(main_ref[...])
        read = _shifted_reader(buf, shifted, tm)
        acc = jnp.zeros((tm, CONV_DIM), F32) + b_ref[...]
        for j in range(CONV_WIDTH):
            acc = acc + w_ref[j:j + 1, :] * read(HALO - (CONV_WIDTH - 1) + j)
        u1_ref[...] = acc
        u3_ref[...] = _ln_silu(acc, g_ref[...], be_ref[...]).astype(u3_ref.dtype)
        finish_exchange()

    res = pl.pallas_call(
        body, name="conv_fwd", grid=(s // tm,),
        in_specs=[pl.BlockSpec((tm, 2 * CONV_DIM), lambda i: (i, 0)),
                  pl.BlockSpec((HALO, 2 * CONV_DIM), lambda i: (jnp.maximum(i * ratio - 1, 0), 0)),
                  pl.BlockSpec(w_pad.shape, lambda i: (0, 0)),
                  pl.BlockSpec(b.shape, lambda i: (0, 0)),
                  pl.BlockSpec(ln_g.shape, lambda i: (0, 0)),
                  pl.BlockSpec(ln_b.shape, lambda i: (0, 0))] + [ANY] * nx,
        out_specs=[pl.BlockSpec((tm, CONV_DIM), lambda i: (i, 0)),
                   pl.BlockSpec((tm, CONV_DIM), lambda i: (i, 0))] + [ANY] * nx,
        out_shape=[_sds((s, CONV_DIM), BF16), _sds((s, CONV_DIM), F32)] + x_shape,
        scratch_shapes=x_scratch + [pltpu.VMEM((tm + HALO, CONV_DIM), F32),
                                    pltpu.VMEM((SUBLANES - 1, tm + HALO - SUBLANES, CONV_DIM), F32)],
        compiler_params=_params(("arbitrary",), x_id),
    )(conv_in, conv_in, w_pad, b, ln_g, ln_b, *x_arrs)
    return res[0], res[1], res[2:]


def _conv_bwd(conv_in, u1, du3, ln_g, ln_b, w_pad, exchange, tm=256):
    s = conv_in.shape[0]
    tm = _pick(s, tm, HALO)
    ratio = tm // HALO
    nt = s // tm
    last_halo = s // HALO - 1
    x_arrs, x_shape, x_scratch, _, x_id = exchange
    nx = len(x_arrs)

    def body(*refs):
        main_ref, halo_ref, u1_ref, u1n_ref, du3_ref, du3n_ref, g_ref, be_ref, w_ref = refs[:9]
        dci_ref, dw_ref, db_ref, dg_ref, dbe_ref = refs[9 + nx:14 + nx]
        ubuf, dbuf, ushift, dshift = refs[-4:]
        finish_exchange = _carry_exchange(exchange, refs, 9, 5, *_sweep_marks(nt))
        i = pl.program_id(0)
        main = main_ref[...]
        a = main[:, :CONV_DIM]
        sb = jax.nn.sigmoid(main[:, CONV_DIM:])
        ubuf[0:HALO, :] = _glu(halo_ref[...]) * (i > 0).astype(F32)
        ubuf[HALO:HALO + tm, :] = a * sb

        def ln_bwd(u1t, du3t):
            _, vjp = jax.vjp(_ln_silu, u1t, g_ref[...], be_ref[...])
            return vjp(du3t)

        du, dg, dbe = ln_bwd(u1_ref[...], du3_ref[...])
        dbuf[0:tm, :] = du
        dbuf[tm:tm + HALO, :] = ln_bwd(u1n_ref[...], du3n_ref[...])[0] * (i < nt - 1).astype(F32)

        @pl.when(i == 0)
        def _():
            dw_ref[...] = jnp.zeros_like(dw_ref)
            db_ref[...] = jnp.zeros_like(db_ref)
            dg_ref[...] = jnp.zeros_like(dg_ref)
            dbe_ref[...] = jnp.zeros_like(dbe_ref)

        dg_ref[...] += dg
        dbe_ref[...] += dbe

        read_u = _shifted_reader(ubuf, ushift, tm)
        read_d = _shifted_reader(dbuf, dshift, tm)
        du0 = jnp.zeros((tm, CONV_DIM), F32)
        for j in range(CONV_WIDTH):
            du0 = du0 + w_ref[j:j + 1, :] * read_d(CONV_WIDTH - 1 - j)
            dw_ref[j:j + 1, :] += jnp.sum(du * read_u(HALO - (CONV_WIDTH - 1) + j), axis=0, keepdims=True)
        db_ref[...] += jnp.sum(du, axis=0, keepdims=True)
        dci_ref[:, :CONV_DIM] = (du0 * sb).astype(dci_ref.dtype)
        dci_ref[:, CONV_DIM:] = (du0 * a * sb * (1.0 - sb)).astype(dci_ref.dtype)
        finish_exchange()

    res = pl.pallas_call(
        body, name="conv_bwd", grid=(nt,),
        in_specs=[pl.BlockSpec((tm, 2 * CONV_DIM), lambda i: (i, 0)),
                  pl.BlockSpec((HALO, 2 * CONV_DIM), lambda i: (jnp.maximum(i * ratio - 1, 0), 0))]
        + [pl.BlockSpec((tm, CONV_DIM), lambda i: (i, 0)),
           pl.BlockSpec((HALO, CONV_DIM), lambda i: (jnp.minimum((i + 1) * ratio, last_halo), 0))] * 2
        + [pl.BlockSpec((1, CONV_DIM), lambda i: (0, 0))] * 2 + [pl.BlockSpec(w_pad.shape, lambda i: (0, 0))]
        + [ANY] * nx,
        out_specs=[pl.BlockSpec((tm, 2 * CONV_DIM), lambda i: (i, 0)),
                   pl.BlockSpec(w_pad.shape, lambda i: (0, 0))]
        + [pl.BlockSpec((1, CONV_DIM), lambda i: (0, 0))] * 3 + [ANY] * nx,
        out_shape=[_sds((s, 2 * CONV_DIM), BF16), _sds(w_pad.shape)] + [_sds((1, CONV_DIM))] * 3 + x_shape,
        scratch_shapes=x_scratch + [pltpu.VMEM((tm + HALO, CONV_DIM), F32)] * 2
        + [pltpu.VMEM((SUBLANES - 1, tm + HALO - SUBLANES, CONV_DIM), F32)] * 2,
        compiler_params=_params(("arbitrary",), x_id),
    )(conv_in, conv_in, u1, u1, du3, du3, ln_g, ln_b, w_pad, *x_arrs)
    return res[:5], res[5:]


def _logsig_neg(z):
    return jnp.minimum(-z, 0.0) - jnp.log(1.0 + jnp.exp(-jnp.abs(z)))


def _split_dot(val, tri):
    hi = val.astype(BF16)
    lo = (val - hi.astype(F32)).astype(BF16)
    return jnp.dot(hi, tri, preferred_element_type=F32) + jnp.dot(lo, tri, preferred_element_type=F32)


def _attn_masks(t, later):
    row = lax.broadcasted_iota(jnp.int32, (t, t), 0)
    col = lax.broadcasted_iota(jnp.int32, (t, t), 1)
    tri = jnp.where(row > col if later else row <= col, 1.0, 0.0).astype(BF16)
    return col < row, tri


def _grid_marks(h, nq):
    hh, i = pl.program_id(0), pl.program_id(1)
    return (hh == 0) & (i == 0), (hh == h - 1) & (i == nq // 2), (hh == h - 1) & (i == nq - 1)


def _head_masks(shape):
    lane = lax.broadcasted_iota(jnp.int32, shape, len(shape) - 1)
    return lane < HEAD_DIM, lane >= HEAD_DIM


def _per_head(blk):
    m0, m1 = _head_masks(blk.shape)
    zero = jnp.zeros_like(blk)
    return jnp.where(m0, blk, zero), jnp.where(m1, blk, zero)


NT = (((1,), (1,)), ((), ()))
TN = (((0,), (0,)), ((), ()))


def _with_top(whole, top):
    rows = top.shape[0]
    return top if rows == whole.shape[0] else jnp.concatenate([top, whole[rows:]], axis=0)


def _attn_fwd(q, k, v, exchange):
    s = q.shape[0]
    hp = q.shape[1] // LANES
    t = ATT_TILE
    scale = 1.0 / math.sqrt(HEAD_DIM)
    x_arrs, x_shape, x_scratch, _, x_id = exchange
    nx = len(x_arrs)

    def body(*refs):
        q_ref, k_ref, v_ref = refs[:3]
        o_ref, lt_ref, nb_ref = refs[3 + nx:6 + nx]
        finish_exchange = _carry_exchange(exchange, refs, 3, 3, *_grid_marks(hp, s // t))
        i = pl.program_id(1)
        qs = _per_head((q_ref[...].astype(F32) * scale).astype(BF16))
        causal, tri = _attn_masks(t, later=True)

        def step(kb, carry, masked, rows):
            cs, acc = carry
            off = pl.multiple_of(kb * t, t)
            kblk = k_ref[pl.ds(off, t), :]
            vs = _per_head(v_ref[pl.ds(off, t), :])
            acc_top = acc[:rows]
            new_cs = []
            for hd in range(2):
                z = lax.dot_general(qs[hd][:rows], kblk, NT, preferred_element_type=F32)
                l = _logsig_neg(z)
                if masked:
                    l = jnp.where(causal, l, 0.0)
                e = z + l + _split_dot(l, tri) + cs[hd][:rows]
                if masked:
                    e = jnp.where(causal, e, -1e30)
                acc_top = acc_top + jnp.dot(jnp.exp(e).astype(BF16), vs[hd], preferred_element_type=F32)
                new_cs.append(_with_top(cs[hd], cs[hd][:rows] + jnp.sum(l, axis=1, keepdims=True)))
            return tuple(new_cs), _with_top(acc, acc_top)

        zero = jnp.zeros((t, 1), F32)
        carry = step(i, ((zero, zero), jnp.zeros((t, LANES), F32)), True, t)

        def live(cs, lo, hi):
            return jnp.maximum(jnp.max(cs[0][lo:hi]), jnp.max(cs[1][lo:hi])) > DEAD_SUM

        def more(state):
            n, _, (cs, _) = state
            return (n < i) & live(cs, 0, t)

        def sweep(state):
            n, n_full, cr = state
            whole = live(cr[0], ATT_PART, t)
            cr = lax.cond(whole, lambda c: step(i - 1 - n, c, False, t), lambda c: step(i - 1 - n, c, False, ATT_PART), cr)
            return n + 1, n_full + whole.astype(jnp.int32), cr

        n_blocks, n_full, carry = lax.while_loop(more, sweep, (jnp.int32(0), jnp.int32(0), carry))
        m0, _ = _head_masks((t, LANES))
        lt_ref[...] = jnp.where(m0, carry[0][0], carry[0][1])
        o_ref[...] = carry[1].astype(o_ref.dtype)
        nb_ref[0, pl.program_id(0), i] = n_blocks.astype(F32)
        nb_ref[1, pl.program_id(0), i] = n_full.astype(F32)
        finish_exchange()

    res = pl.pallas_call(
        body, name="attn_fwd", grid=(hp, s // t),
        in_specs=[pl.BlockSpec((t, LANES), lambda p, i: (i, p)),
                  pl.BlockSpec((s, LANES), lambda p, i: (0, p)),
                  pl.BlockSpec((s, LANES), lambda p, i: (0, p))] + [ANY] * nx,
        out_specs=[pl.BlockSpec((t, LANES), lambda p, i: (i, p)),
                   pl.BlockSpec((None, t, LANES), lambda p, i: (p, i, 0)),
                   pl.BlockSpec(memory_space=pltpu.SMEM)] + [ANY] * nx,
        out_shape=[_sds(q.shape, BF16), _sds((hp, s, LANES), F32), _sds((2, hp, s // t), F32)] + x_shape,
        scratch_shapes=x_scratch,
        compiler_params=_params(("arbitrary", "arbitrary"), x_id),
    )(q, k, v, *x_arrs)
    return res[0], res[1], res[2], res[3:]


def _attn_bwd(q, k, v, do, ltot, n_blocks, exchange):
    s = q.shape[0]
    hp = q.shape[1] // LANES
    t = ATT_TILE
    scale = 1.0 / math.sqrt(HEAD_DIM)
    x_arrs, x_shape, x_scratch, _, x_id = exchange
    nx = len(x_arrs)

    def body(*refs):
        q_ref, k_ref, v_ref, do_ref, lt_ref, nb_ref = refs[:6]
        dq_ref, dk_ref, dv_ref = refs[6 + nx:9 + nx]
        finish_exchange = _carry_exchange(exchange, refs, 6, 3, *_grid_marks(hp, s // t))
        i = pl.program_id(1)
        n_blocks = jnp.clip(nb_ref[0, pl.program_id(0), i].astype(jnp.int32), 0, i)
        n_full = jnp.clip(nb_ref[1, pl.program_id(0), i].astype(jnp.int32), 0, n_blocks)

        @pl.when(i == 0)
        def _():
            dk_ref[...] = jnp.zeros_like(dk_ref)
            dv_ref[...] = jnp.zeros_like(dv_ref)

        qb = q_ref[...]
        qm = _per_head(qb)
        qs = _per_head((qb.astype(F32) * scale).astype(BF16))
        dos = _per_head(do_ref[...])
        lts = (lt_ref[:, 0:1], lt_ref[:, HEAD_DIM:HEAD_DIM + 1])
        causal, tri = _attn_masks(t, later=False)

        def step(kb, carry, masked, rows):
            cls, cgs, dq = carry
            off = pl.multiple_of(kb * t, t)
            kblk = k_ref[pl.ds(off, t), :]
            vblk = v_ref[pl.ds(off, t), :]
            ks = _per_head(kblk)
            dq_top = dq[:rows]
            dk = jnp.zeros((t, LANES), F32)
            dv = jnp.zeros((t, LANES), F32)
            new_cls, new_cgs = [], []
            for hd in range(2):
                z = lax.dot_general(qs[hd][:rows], kblk, NT, preferred_element_type=F32)
                l = _logsig_neg(z)
                if masked:
                    l = jnp.where(causal, l, 0.0)
                e = z + l + ((lts[hd][:rows] - cls[hd][:rows]) - _split_dot(l, tri))
                if masked:
                    e = jnp.where(causal, e, -1e30)
                a = jnp.exp(e)
                g = lax.dot_general(dos[hd][:rows], vblk, NT, preferred_element_type=F32) * a
                p = cgs[hd][:rows] + jnp.dot(g.astype(BF16), tri, preferred_element_type=F32) - g
                el = jnp.exp(l)
                dz = g * el - p * (1.0 - el)
                if masked:
                    dz = jnp.where(causal, dz, 0.0)
                dzb = (dz * scale).astype(BF16)
                dq_top = dq_top + jnp.dot(dzb, ks[hd], preferred_element_type=F32)
                dk = dk + lax.dot_general(dzb, qm[hd][:rows], TN, preferred_element_type=F32)
                dv = dv + lax.dot_general(a.astype(BF16), dos[hd][:rows], TN, preferred_element_type=F32)
                new_cls.append(_with_top(cls[hd], cls[hd][:rows] + jnp.sum(l, axis=1, keepdims=True)))
                new_cgs.append(_with_top(cgs[hd], cgs[hd][:rows] + jnp.sum(g, axis=1, keepdims=True)))
            dk_ref[pl.ds(off, t), :] += dk
            dv_ref[pl.ds(off, t), :] += dv
            return tuple(new_cls), tuple(new_cgs), _with_top(dq, dq_top)

        zero = jnp.zeros((t, 1), F32)
        init = ((zero, zero), (zero, zero), jnp.zeros((t, LANES), F32))
        carry = lax.fori_loop(i - n_blocks, i - n_full, lambda kb, cr: step(kb, cr, False, ATT_PART), init)
        carry = lax.fori_loop(i - n_full, i, lambda kb, cr: step(kb, cr, False, t), carry)
        carry = step(i, carry, True, t)
        dq_ref[...] = carry[2]
        finish_exchange()

    blk = pl.BlockSpec((t, LANES), lambda p, i: (i, p))
    whole = pl.BlockSpec((s, LANES), lambda p, i: (0, p))
    res = pl.pallas_call(
        body, name="attn_bwd", grid=(hp, s // t),
        in_specs=[blk, whole, whole, blk, pl.BlockSpec((None, t, LANES), lambda p, i: (p, i, 0)),
                  pl.BlockSpec(memory_space=pltpu.SMEM)] + [ANY] * nx,
        out_specs=[blk, whole, whole] + [ANY] * nx,
        out_shape=[_sds(q.shape)] * 3 + x_shape,
        scratch_shapes=x_scratch,
        compiler_params=_params(("arbitrary", "arbitrary"), x_id),
    )(q, k, v, do, ltot, n_blocks, *x_arrs)
    return res[0], res[1], res[2], res[3:]


LATE = ["w_conv_branch", "w_att_branch", "w_out", "w_ffn_up", "w_ffn_down"]


def _full_weight(name, gathered):
    return _cols_to_full(gathered) if name in COL_SHARDED else gathered.reshape(-1, gathered.shape[2])


def _grad_slabs(name, grad):
    return _full_to_cols(grad) if name in COL_SHARDED else grad.reshape(N_DEV, -1, grad.shape[1])


def _side_slabs(name, grad):
    slabs = _grad_slabs(name, grad)
    return slabs.reshape((4, 2) + slabs.shape[1:])


UP_SPLITS = (0, 256, D_MODEL)
DOWN_SPLITS = (0, 256, 512, D_MODEL)


def _col_pieces(block, splits):
    return [block[:, a:b] for a, b in zip(splits, splits[1:])]


def _pieces_dot(a, pieces):
    return jnp.concatenate([jnp.dot(a, p, preferred_element_type=F32) for p in pieces], axis=1)


def _pieces_dot_nt(a, pieces, splits):
    return sum(lax.dot_general(a[:, lo:hi], p, NT, preferred_element_type=F32)
               for p, lo, hi in zip(pieces, splits, splits[1:]))


def _local_step(x, target, w, late_blocks, opt):
    s = x.shape[0]
    w = dict(w)
    g1, g2, g3, g4 = w["norm_mix_pre"], w["norm_mix_post"], w["norm_ffn_pre"], w["norm_ffn_post"]

    w_in = w["w_in"]

    def proj_fn(xt, g1_, w_in_t):
        h = _rms(xt, g1_).astype(BF16)
        proj = lax.dot_general(h, w_in_t, NT, preferred_element_type=F32)
        return (h, *[proj[:, IN_SPLITS[n]:IN_SPLITS[n + 1]] for n in range(6)]), ()

    up_pieces = _col_pieces(late_blocks["w_ffn_up"], UP_SPLITS)
    down_pieces = _col_pieces(late_blocks["w_ffn_down"], DOWN_SPLITS)
    mix_weights = ["w_conv_branch", "w_att_branch", "w_out"]
    h1, conv_in, q, k, v, g_conv, g_att, g_out, g_up0 = _rowwise(
        "norm_proj", proj_fn, [x], [g1, w_in],
        [_sds((s, D_MODEL), BF16), _sds((s, 2 * CONV_DIM)), _sds((s, ATT_DIM), BF16), _sds((s, ATT_DIM), BF16),
         _sds((s, ATT_DIM), BF16), _sds((s, D_MODEL), BF16), _sds((s, D_MODEL), BF16)], tm=512,
        exchange=_gather_exchange([late_blocks["w_out"], up_pieces[0]]))

    u3, u1, g_branches = _conv_fwd(conv_in, w["conv_dw_w"], w["conv_dw_b"], w["conv_ln_g"], w["conv_ln_b"],
                                   _gather_exchange([late_blocks[nm] for nm in mix_weights[:2]]))
    for nm, g in zip(mix_weights, [*g_branches, g_out]):
        w[nm] = _full_weight(nm, g)
    att, ltot, n_blocks, (g_up1, g_down0) = _attn_fwd(q, k, v, _gather_exchange([up_pieces[1], down_pieces[0]]))
    w_up = [_full_weight("w_ffn_up", g) for g in (g_up0, g_up1)]

    def merge_fn(u3t, at, gc, ga, xt, w_cb, w_ab, b_cb, w_out, g2_, g3_):
        cp = jnp.dot(u3t, w_cb, preferred_element_type=F32)
        ao = jnp.dot(at, w_ab, preferred_element_type=F32)
        mg = _merge(cp, ao, gc.astype(F32), ga.astype(F32), b_cb).astype(BF16)
        mix_ = jnp.dot(mg, w_out, preferred_element_type=F32)
        x2_ = xt + _rms(mix_, g2_)
        return (mg, cp, ao, mix_, x2_, _rms(x2_, g3_)), ()

    merged, conv_pre, att_out, mix, x2, h2, g_down1 = _rowwise(
        "branch_merge_mix", merge_fn, [u3, att, g_conv, g_att, x],
        [w["w_conv_branch"], w["w_att_branch"], w["b_conv_branch"], w["w_out"], g2, g3],
        [_sds((s, D_MODEL), BF16)] * 3 + [_sds((s, D_MODEL)), _sds((s, D_MODEL)), _sds((s, D_MODEL), BF16)], tm=512,
        exchange=_gather_exchange([down_pieces[1]]))

    def ffn_up_fn(ht, *w_up_t):
        gu_ = _pieces_dot_nt(ht, w_up_t, UP_SPLITS)
        return (gu_, _swiglu(gu_[:, :D_FF], gu_[:, D_FF:])), ()

    gu, act, g_down2 = _rowwise("ffn_up", ffn_up_fn, [h2], w_up,
                                [_sds((s, 2 * D_FF), BF16), _sds((s, D_FF), BF16)], tm=512,
                                exchange=_gather_exchange([down_pieces[2]]))
    w_down = [_full_weight("w_ffn_down", g) for g in (g_down0, g_down1, g_down2)]

    def final_fn(at, x2t, tgt, *rest):
        ff, g4_ = _pieces_dot(at, rest[:-1]), rest[-1]
        n4, vjp = jax.vjp(_rms, ff, g4_)
        err = x2t + n4 - tgt
        dy = err * (1.0 / D_MODEL)
        dff, dg4 = vjp(dy)
        return (dy, dff), (jnp.sum(err * err, axis=0, keepdims=True), dg4)

    dy, dff, loss_cols, d_g4 = _rowwise("ffn_down_loss", final_fn, [act, x2, target], [*w_down, g4],
                                        [_sds((s, D_MODEL)), _sds((s, D_MODEL), BF16)],
                                        [_sds((1, D_MODEL)), _sds((1, D_MODEL))], tm=512)
    loss = 0.5 * jnp.sum(loss_cols) / D_MODEL

    d_w_down = _tn_matmul(act, dff, name="d_w_down")

    def act_bwd_fn(dfft, gut, *w_down_):
        d_act = _pieces_dot_nt(dfft, w_down_, DOWN_SPLITS)
        gu_ = gut.astype(F32)
        _, vjp = jax.vjp(_swiglu, gu_[:, :D_FF], gu_[:, D_FF:])
        return (jnp.concatenate(vjp(d_act), axis=1),), ()

    down_slabs = _side_slabs("w_ffn_down", d_w_down)
    dgu, theirs = _rowwise("ffn_act_bwd", act_bwd_fn, [dff, gu], w_down, [_sds((s, 2 * D_FF), BF16)],
                           exchange=_pair_exchange([down_slabs]))
    down_sums = _pair_sum("pair_sum_w_ffn_down", down_slabs, theirs)
    d_w_up = _tn_matmul(dgu, h2, name="d_w_up")
    received = {}
    up_slabs = _side_slabs("w_ffn_up", d_w_up)

    def mid_bwd_fn(dgut, xt, mt, dyt, *rest):
        dh, (g2_, g3_) = _pieces_dot(dgut, rest[:-2]), rest[-2:]
        n2, vjp2 = jax.vjp(_rms, mt, g2_)
        x2_ = xt + n2
        _, vjp3 = jax.vjp(_rms, x2_, g3_)
        dx2_, dg3 = vjp3(dh)
        dx2_ = dx2_ + dyt
        dmix_, dg2 = vjp2(dx2_)
        return (dx2_, dmix_), (dg2, dg3)

    dx2, dmix, d_g2, d_g3, received["w_ffn_down"] = _rowwise(
        "ffn_up_mid_bwd", mid_bwd_fn, [dgu, x, mix, dy], [*w_up, g2, g3],
        [_sds((s, D_MODEL)), _sds((s, D_MODEL), BF16)], [_sds((1, D_MODEL)), _sds((1, D_MODEL))], tm=512,
        exchange=_chip_exchange([down_sums]))
    d_w_out = _tn_matmul(merged, dmix, name="d_w_out")

    def merge_bwd_fn(dmt, cp, ao, gc, ga, w_out, w_cb, w_ab, b_cb):
        dm = lax.dot_general(dmt, w_out, NT, preferred_element_type=F32)
        _, vjp = jax.vjp(_merge, cp.astype(F32), ao.astype(F32), gc.astype(F32), ga.astype(F32), b_cb)
        dcp, dao, dgc, dga, dbias = vjp(dm)
        dcp, dao = dcp.astype(BF16), dao.astype(BF16)
        du3_ = lax.dot_general(dcp, w_cb, NT, preferred_element_type=F32)
        datt_ = lax.dot_general(dao, w_ab, NT, preferred_element_type=F32)
        return (dcp, dao, dgc, dga, du3_, datt_), (dbias,)

    d_conv_out, d_att_out, d_g_conv, d_g_att, du3, d_att, d_b_cb, theirs = _rowwise(
        "merge_bwd", merge_bwd_fn, [dmix, conv_pre, att_out, g_conv, g_att],
        [w["w_out"], w["w_conv_branch"], w["w_att_branch"], w["b_conv_branch"]],
        [_sds((s, D_MODEL), BF16)] * 4 + [_sds((s, CONV_DIM)), _sds((s, ATT_DIM), BF16)], [_sds((1, D_MODEL))], tm=512,
        exchange=_pair_exchange([up_slabs]))

    d_w_cb = _tn_matmul(u3, d_conv_out, name="d_w_conv_branch")
    d_w_ab = _tn_matmul(att, d_att_out, name="d_w_att_branch")

    dq, dk, dv, (received["w_ffn_up"],) = _attn_bwd(
        q, k, v, d_att, ltot, n_blocks, _chip_exchange([_pair_sum("pair_sum_w_ffn_up", up_slabs, theirs)]))

    mix_grads = {"w_conv_branch": d_w_cb, "w_att_branch": d_w_ab, "w_out": d_w_out}
    (d_conv_in, d_dw_w, d_dw_b, d_ln_g, d_ln_b), landed = _conv_bwd(
        conv_in, u1, du3, w["conv_ln_g"], w["conv_ln_b"], w["conv_dw_w"],
        _scatter_exchange([_grad_slabs(nm, mix_grads[nm]) for nm in mix_weights[:2]]))
    received.update(zip(mix_weights[:2], landed))

    d_proj = [d_conv_in, dq, dk, dv, d_g_conv, d_g_att]
    d_w_in, (received["w_out"],) = _pieces_tn_matmul(
        d_proj, h1, name="d_w_in", exchange=_scatter_exchange([_grad_slabs("w_out", d_w_out)]))
    in_slabs = _side_slabs("w_in", d_w_in)
    (theirs,) = _exchange_call("pair_swap_w_in", _pair_exchange([in_slabs]))

    early = list(opt)

    def pre_bwd_fn(*args):
        groups, (xt, dx2t), jobs, (w_in_t, g_) = args[:6], args[6:8], args[8:-2], args[-2:]
        dh = sum(jnp.dot(grp.astype(BF16), w_in_t[IN_SPLITS[n]:IN_SPLITS[n + 1]], preferred_element_type=F32)
                 for n, grp in enumerate(groups))
        _, vjp = jax.vjp(_rms, xt, g_)
        dx_, dg_ = vjp(dh)
        updates = [_sum_adamw_tile(*jobs[4 * n:4 * n + 4]) for n in range(len(early))]
        return (dx_ + dx2t, *[u for four in updates for u in four]), (dg_,)

    res = _rowwise(
        "proj_norm_bwd", pre_bwd_fn,
        d_proj + [x, dx2] + [a for nm in early for a in (received[nm], *opt[nm])], [w_in, g1],
        [_sds((s, D_MODEL))] + [_sds(opt[nm][0].shape) for nm in early for _ in range(4)],
        [_sds((1, D_MODEL))], tm=512, exchange=_chip_exchange([_pair_sum("pair_sum_w_in", in_slabs, theirs)]))
    grad_x, d_g1, received["w_in"] = res[0], res[-2], res[-1]
    updated = {nm: res[1 + 4 * n:5 + 4 * n] for n, nm in enumerate(early)}

    grads = {
        "norm_mix_pre": d_g1, "conv_dw_w": d_dw_w, "conv_dw_b": d_dw_b,
        "conv_ln_g": d_ln_g, "conv_ln_b": d_ln_b, "b_conv_branch": d_b_cb,
        "norm_mix_post": d_g2, "norm_ffn_pre": d_g3, "norm_ffn_post": d_g4,
    }
    return loss, grad_x, received, updated, grads


def _place():
    x, y, c = lax.axis_index("x"), lax.axis_index("y"), lax.axis_index("c")
    return x, y, c


def _slot(px, py, pc):
    return 4 * px + 2 * py + pc


def _exchange_scratch(n):
    return [pltpu.SemaphoreType.DMA((7 * n,)), pltpu.SemaphoreType.DMA((7 * n,)), pltpu.SemaphoreType.DMA((n,))]


GATHER_ID, SCATTER_ID, PAIR_ID, CHIP_ID = 0, 1, 2, 3


def _handshake(peers):
    barrier = pltpu.get_barrier_semaphore()
    for peer in peers:
        pl.semaphore_signal(barrier, inc=1, device_id=peer, device_id_type=MESH)
    pl.semaphore_wait(barrier, len(peers))


def _gather_exchange(arrs):
    n = len(arrs)

    def phases(ins, outs, send_sems, recv_sems, local_sems):
        x, y, c = _place()
        me, sibling = (x, y, c), (x, y, 1 - c)
        chips = [(1 - x, y), (x, 1 - y), (1 - x, 1 - y)]

        def copy(a, kk, block, to, src=None):
            dst = outs[a].at[_slot(*block)]
            return pltpu.make_async_remote_copy(
                src_ref=dst if src is None else src, dst_ref=dst,
                send_sem=send_sems.at[a * 7 + kk], recv_sem=recv_sems.at[a * 7 + kk],
                device_id=to, device_id_type=MESH)

        mine = [pltpu.make_async_copy(ins[a], outs[a].at[_slot(*me)], local_sems.at[a]) for a in range(n)]
        first = []
        for a in range(n):
            first.append(copy(a, 0, me, sibling, src=ins[a]))
            first += [copy(a, 1 + j, me, (*chip, c), src=ins[a]) for j, chip in enumerate(chips)]
        passed = [copy(a, 4 + j, (*chip, c), sibling) for j, chip in enumerate(chips) for a in range(n)]

        def send():
            _handshake([sibling] + [(*chip, c) for chip in chips])
            for cp in mine + first:
                cp.start()

        def pass_on():
            for j, chip in enumerate(chips):
                for a in range(n):
                    copy(a, 1 + j, (*chip, c), me).wait_recv()
                    passed[j * n + a].start()

        def finish():
            for a in range(n):
                copy(a, 0, sibling, me).wait_recv()
                for j, chip in enumerate(chips):
                    copy(a, 4 + j, (*chip, 1 - c), me).wait_recv()
            for cp in first + passed:
                cp.wait_send()
            for cp in mine:
                cp.wait()

        return [send, pass_on, finish]

    return list(arrs), [_sds((N_DEV,) + a.shape, a.dtype) for a in arrs], _exchange_scratch(n), phases, GATHER_ID


def _scatter_exchange(arrs):
    n = len(arrs)
    flips = [(fx, fy, fc) for fx in (0, 1) for fy in (0, 1) for fc in (0, 1)][1:]

    def phases(ins, outs, send_sems, recv_sems, local_sems):
        x, y, c = _place()
        mine = _slot(x, y, c)
        local = [pltpu.make_async_copy(ins[a].at[mine], outs[a].at[mine], local_sems.at[a]) for a in range(n)]
        peers = [((1 - x) if fx else x, (1 - y) if fy else y, (1 - c) if fc else c) for fx, fy, fc in flips]

        def copy(a, kk, src_slot, dst_slot):
            return pltpu.make_async_remote_copy(
                src_ref=ins[a].at[src_slot], dst_ref=outs[a].at[dst_slot],
                send_sem=send_sems.at[a * 7 + kk], recv_sem=recv_sems.at[a * 7 + kk],
                device_id=peers[kk], device_id_type=MESH)

        sends = [copy(a, kk, _slot(*peers[kk]), mine) for a in range(n) for kk in range(7)]

        def send():
            _handshake(peers)
            for cp in local + sends:
                cp.start()

        def finish():
            for a in range(n):
                for kk in range(7):
                    copy(a, kk, mine, _slot(*peers[kk])).wait_recv()
            for cp in sends:
                cp.wait_send()
            for cp in local:
                cp.wait()

        return [send, finish]

    return list(arrs), [_sds(a.shape, a.dtype) for a in arrs], _exchange_scratch(n), phases, SCATTER_ID


def _pair_exchange(arrs):
    n = len(arrs)

    def phases(ins, outs, send_sems, recv_sems, local_sems):
        x, y, c = _place()

        def copy(a, chip, side):
            return pltpu.make_async_remote_copy(
                src_ref=ins[a].at[chip, side], dst_ref=outs[a].at[chip],
                send_sem=send_sems.at[a * 7 + chip], recv_sem=recv_sems.at[a * 7 + chip],
                device_id=(x, y, 1 - c), device_id_type=MESH)

        sends = [copy(a, chip, 1 - c) for a in range(n) for chip in range(4)]

        def send():
            _handshake([(x, y, 1 - c)])
            for cp in sends:
                cp.start()

        def finish():
            for a in range(n):
                for chip in range(4):
                    copy(a, chip, c).wait_recv()
            for cp in sends:
                cp.wait_send()

        return [send, finish]

    return list(arrs), [_sds((4,) + a.shape[2:], a.dtype) for a in arrs], _exchange_scratch(n), phases, PAIR_ID


def _chip_exchange(arrs):
    n = len(arrs)

    def phases(ins, outs, send_sems, recv_sems, local_sems):
        x, y, c = _place()
        mine = 2 * x + y
        chips = [(1 - x, y), (x, 1 - y), (1 - x, 1 - y)]
        local = [pltpu.make_async_copy(ins[a].at[mine], outs[a].at[mine], local_sems.at[a]) for a in range(n)]

        def copy(a, j, src_slot, dst_slot):
            return pltpu.make_async_remote_copy(
                src_ref=ins[a].at[src_slot], dst_ref=outs[a].at[dst_slot],
                send_sem=send_sems.at[a * 7 + j], recv_sem=recv_sems.at[a * 7 + j],
                device_id=(*chips[j], c), device_id_type=MESH)

        sends = [copy(a, j, 2 * chips[j][0] + chips[j][1], mine) for a in range(n) for j in range(3)]

        def send():
            _handshake([(*chip, c) for chip in chips])
            for cp in local + sends:
                cp.start()

        def finish():
            for a in range(n):
                for j in range(3):
                    copy(a, j, mine, 2 * chips[j][0] + chips[j][1]).wait_recv()
            for cp in sends:
                cp.wait_send()
            for cp in local:
                cp.wait()

        return [send, finish]

    return list(arrs), [_sds(a.shape, a.dtype) for a in arrs], _exchange_scratch(n), phases, CHIP_ID


def _pair_sum(name, mine, theirs):
    _, _, r, c = mine.shape

    def body(side_ref, m_ref, t_ref, o_ref):
        o_ref[...] = (m_ref[...].astype(F32) + t_ref[...].astype(F32)).astype(o_ref.dtype)

    return pl.pallas_call(
        body, name=name,
        grid_spec=pltpu.PrefetchScalarGridSpec(
            num_scalar_prefetch=1, grid=(4,),
            in_specs=[pl.BlockSpec((None, None, r, c), lambda j, side: (j, side[0], 0, 0)),
                      pl.BlockSpec((None, r, c), lambda j, side: (j, 0, 0))],
            out_specs=pl.BlockSpec((None, r, c), lambda j, side: (j, 0, 0))),
        out_shape=_sds(theirs.shape, theirs.dtype),
        compiler_params=_params(("parallel",)),
    )(lax.axis_index("c").astype(jnp.int32).reshape(1), mine, theirs)


def _exchange_call(name, exchange):
    arrs, out_shape, scratch, phases, collective_id = exchange
    n = len(arrs)

    def body(*refs):
        for step in phases(refs[:n], refs[n:2 * n], *refs[2 * n:]):
            step()

    return pl.pallas_call(body, name=name, in_specs=[ANY] * n, out_specs=[ANY] * n,
                          out_shape=out_shape, scratch_shapes=scratch,
                          compiler_params=pltpu.CompilerParams(collective_id=collective_id))(*arrs)


def _carry_exchange(exchange, refs, n_in, n_out, first, middle, last):
    arrs, _, _, phases, _ = exchange
    n = len(arrs)
    if n == 0:
        return lambda: None
    ins = refs[n_in:n_in + n]
    outs = refs[n_in + n + n_out:n_in + 2 * n + n_out]
    sems = n_in + 2 * n + n_out
    steps = phases(ins, outs, *refs[sems:sems + 3])
    pl.when(first)(steps[0])
    if len(steps) == 3:
        pl.when(middle)(steps[1])
    return lambda: pl.when(last)(steps[-1])


def _adamw_math(w, g, m, v):
    m2 = ADAM_B1 * m + (1.0 - ADAM_B1) * g
    v2 = ADAM_B2 * v + (1.0 - ADAM_B2) * jnp.square(g)
    m_hat = m2 / (1.0 - ADAM_B1 ** ADAM_STEP)
    v_hat = v2 / (1.0 - ADAM_B2 ** ADAM_STEP)
    delta = -ADAM_LR * (m_hat / (jnp.sqrt(v_hat) + ADAM_EPS) + ADAM_WD * w)
    return delta, m2, v2


def _sum_adamw_tile(parts, w, m, v):
    g = parts[0].astype(F32)
    for d in range(1, parts.shape[0]):
        g = g + parts[d].astype(F32)
    return (g, *_adamw_math(w, g, m, v))


def _sum_adamw(name, parts, w, m, v, tr=256):
    p, r, c = parts.shape
    tr = _pick(r, tr, 16)

    def body(p_ref, w_ref, m_ref, v_ref, g_ref, d_ref, m2_ref, v2_ref):
        g_ref[...], d_ref[...], m2_ref[...], v2_ref[...] = _sum_adamw_tile(p_ref[...], w_ref[...], m_ref[...], v_ref[...])

    tile = pl.BlockSpec((tr, c), lambda i: (i, 0))
    return pl.pallas_call(
        body, name=name, grid=(r // tr,),
        in_specs=[pl.BlockSpec((p, tr, c), lambda i: (0, i, 0)), tile, tile, tile],
        out_specs=[tile] * 4, out_shape=[_sds((r, c))] * 4,
        compiler_params=_params(("parallel",)),
    )(parts, w, m, v)


def _sum_parts(name, parts):
    p, r, c = parts.shape

    def body(p_ref, o_ref):
        g = p_ref[0]
        for d in range(1, p):
            g = g + p_ref[d]
        o_ref[...] = g

    return pl.pallas_call(
        body, name=name, out_shape=_sds((r, c)),
        in_specs=[pl.BlockSpec(memory_space=pltpu.VMEM)], out_specs=pl.BlockSpec(memory_space=pltpu.VMEM),
    )(parts)


WEIGHTS = ["norm_mix_pre", "w_in", "conv_dw_w", "conv_dw_b", "conv_ln_g", "conv_ln_b", "w_conv_branch",
           "b_conv_branch", "w_att_branch", "w_out", "norm_mix_post", "norm_ffn_pre", "w_ffn_up", "w_ffn_down",
           "norm_ffn_post"]
COL_SHARDED = ["w_conv_branch", "w_att_branch"]
TRANSPOSED = ["w_in", "w_ffn_up"]
VECTORS = ["norm_mix_pre", "conv_dw_b", "conv_ln_g", "conv_ln_b", "b_conv_branch", "norm_mix_post",
           "norm_ffn_pre", "norm_ffn_post"]


def _cols_to_full(g):
    return g.transpose(1, 0, 2).reshape(g.shape[1], N_DEV * g.shape[2])


def _full_to_cols(f):
    return f.reshape(f.shape[0], N_DEV, f.shape[1] // N_DEV).transpose(1, 0, 2)


PACK_ROWS = 7


def _pack_vectors(vecs, extra=None):
    parts = [vecs[nm].reshape(-1) for nm in VECTORS]
    parts.append(jnp.zeros((1,), F32) if extra is None else extra.reshape(1))
    used = sum(p.size for p in parts)
    parts.append(jnp.zeros((PACK_ROWS * D_MODEL - used,), F32))
    return jnp.concatenate(parts).reshape(PACK_ROWS, D_MODEL)


def _unpack_vectors(packed, sizes):
    flat, out, at = packed.reshape(-1), {}, 0
    for nm in VECTORS:
        out[nm] = flat[at:at + sizes[nm]]
        at += sizes[nm]
    return out, flat[at]


def kernel(x, norm_mix_pre, w_in, conv_dw_w, conv_dw_b, conv_ln_g, conv_ln_b, w_conv_branch, b_conv_branch, w_att_branch, w_out, norm_mix_post, norm_ffn_pre, w_ffn_up, w_ffn_down, norm_ffn_post, loss_target, m_norm_mix_pre, m_w_in, m_conv_dw_w, m_conv_dw_b, m_conv_ln_g, m_conv_ln_b, m_w_conv_branch, m_b_conv_branch, m_w_att_branch, m_w_out, m_norm_mix_post, m_norm_ffn_pre, m_w_ffn_up, m_w_ffn_down, m_norm_ffn_post, v_norm_mix_pre, v_w_in, v_conv_dw_w, v_conv_dw_b, v_conv_ln_g, v_conv_ln_b, v_w_conv_branch, v_b_conv_branch, v_w_att_branch, v_w_out, v_norm_mix_post, v_norm_ffn_pre, v_w_ffn_up, v_w_ffn_down, v_norm_ffn_post):
    ws = dict(zip(WEIGHTS, [norm_mix_pre, w_in, conv_dw_w, conv_dw_b, conv_ln_g, conv_ln_b, w_conv_branch,
                            b_conv_branch, w_att_branch, w_out, norm_mix_post, norm_ffn_pre, w_ffn_up, w_ffn_down,
                            norm_ffn_post]))
    ms = dict(zip(WEIGHTS, [m_norm_mix_pre, m_w_in, m_conv_dw_w, m_conv_dw_b, m_conv_ln_g, m_conv_ln_b,
                            m_w_conv_branch, m_b_conv_branch, m_w_att_branch, m_w_out, m_norm_mix_post,
                            m_norm_ffn_pre, m_w_ffn_up, m_w_ffn_down, m_norm_ffn_post]))
    vs = dict(zip(WEIGHTS, [v_norm_mix_pre, v_w_in, v_conv_dw_w, v_conv_dw_b, v_conv_ln_g, v_conv_ln_b,
                            v_w_conv_branch, v_b_conv_branch, v_w_att_branch, v_w_out, v_norm_mix_post,
                            v_norm_ffn_pre, v_w_ffn_up, v_w_ffn_down, v_norm_ffn_post]))

    dw_block = jnp.pad(conv_dw_w, ((0, 1), (0, 0)))
    g_in, g_dw = _exchange_call("gather_first", _gather_exchange([w_in.T.astype(BF16), dw_block]))
    full = {"w_in": _full_weight("w_in", g_in), "conv_dw_w": _cols_to_full(g_dw)}
    for nm in VECTORS:
        full[nm] = ws[nm].reshape(1, -1)

    def as_kept(nm, a):
        return a.T if nm in TRANSPOSED else a

    ride_along = ["w_ffn_up", "w_out"]
    loss_local, grad_x, received, updated, grads = _local_step(
        x[0], loss_target[0], full, {nm: as_kept(nm, ws[nm]).astype(BF16) for nm in LATE},
        {nm: tuple(as_kept(nm, a[nm]) for a in (ws, ms, vs)) for nm in ride_along})

    small = _exchange_call("gather_small_grads", _gather_exchange(
        [_pack_vectors(grads, extra=loss_local), grads["conv_dw_w"]]))
    out_g, out_d, out_m, out_v = {}, {}, {}, {}
    for nm in LATE + ["w_in"]:
        res = updated[nm] if nm in updated else _sum_adamw(
            "adamw_" + nm, received[nm], *[as_kept(nm, a[nm]) for a in (ws, ms, vs)])
        out_g[nm], out_d[nm], out_m[nm], out_v[nm] = [as_kept(nm, r) for r in res]
    sizes = {nm: ws[nm].size for nm in VECTORS}
    vec = _sum_adamw("adamw_vectors", small[0], _pack_vectors(ws), _pack_vectors(ms), _pack_vectors(vs))
    for res, dst in zip(vec, (out_g, out_d, out_m, out_v)):
        dst.update(_unpack_vectors(res, sizes)[0])
    loss = _unpack_vectors(vec[0], sizes)[1]
    dw_full = _sum_parts("sum_dw_grads", small[1])
    me = _slot(*_place())
    dw_mine = lax.dynamic_slice(dw_full, (0, me * (CONV_DIM // N_DEV)), (CONV_WIDTH, CONV_DIM // N_DEV))
    nm = "conv_dw_w"
    out_g[nm], out_d[nm], out_m[nm], out_v[nm] = _sum_adamw("adamw_dw", dw_mine[None], ws[nm], ms[nm], vs[nm])

    outs = [loss, grad_x[None]]
    for group in (out_g, out_d, out_m, out_v):
        outs += [group[nm] for nm in WEIGHTS]
    return tuple(outs)
```

```python
import math

import jax
import jax.numpy as jnp
from jax import lax
from jax.experimental import pallas as pl
from jax.experimental.pallas import tpu as pltpu

F32 = jnp.float32
BF16 = jnp.bfloat16

N_DEV = 8
D_MODEL = 1024
CONV_DIM = 512
CONV_WIDTH = 31
N_HEADS = 8
HEAD_DIM = 64
ATT_DIM = N_HEADS * HEAD_DIM
D_FF = 2816
EPS = 1e-6
IN_SPLITS = (0, 1024, 1536, 2048, 2560, 3584, 4608)

ADAM_LR = 0.001
ADAM_B1 = 0.9
ADAM_B2 = 0.999
ADAM_EPS = 1e-08
ADAM_WD = 0.01
ADAM_STEP = 10

LANES = 128
SUBLANES = 8
HALO = 32
ATT_TILE = 256
ATT_PART = 176
DEAD_SUM = -120.0
VMEM_LIMIT = 56 * 1024 * 1024
MESH = pl.DeviceIdType.MESH
ANY = pl.BlockSpec(memory_space=pl.ANY)


def _pick(dim, target, align=LANES):
    t = min(dim, target)
    t -= t % align
    while t >= align:
        if dim % t == 0:
            return t
        t -= align
    return dim


def _params(semantics, collective_id=None):
    return pltpu.CompilerParams(dimension_semantics=semantics, vmem_limit_bytes=VMEM_LIMIT,
                                collective_id=collective_id)


def _tn_matmul(a, b, *, name):
    return _pieces_tn_matmul([a], b, name=name, tj=_pick(a.shape[1], 1408))


def _pieces_tn_matmul(pieces, b, *, name, tj=512, exchange=None):
    s, n = b.shape
    counts = [p.shape[1] // tj for p in pieces]
    starts = [sum(counts[:i]) for i in range(len(pieces))]
    assert all(p.shape == (s, c * tj) for p, c in zip(pieces, counts))
    x_arrs, x_shape, x_scratch, _, x_id = exchange or NO_EXCHANGE
    nx, n_in = len(x_arrs), len(pieces) + 1

    def body(*refs):
        b_ref, o_ref = refs[n_in - 1], refs[n_in + nx]
        finish_exchange = _carry_exchange(exchange or NO_EXCHANGE, refs, n_in, 1, *_sweep_marks(sum(counts)))
        j = pl.program_id(0)
        for p_ref, first, count in zip(refs, starts, counts):
            @pl.when((j >= first) & (j < first + count))
            def _():
                o_ref[...] = lax.dot_general(p_ref[...].astype(BF16), b_ref[...], TN,
                                             preferred_element_type=F32).astype(o_ref.dtype)
        finish_exchange()

    def piece_spec(first, count):
        return pl.BlockSpec((s, tj), lambda j: (0, jnp.clip(j - first, 0, count - 1)))

    res = pl.pallas_call(
        body, name=name, grid=(sum(counts),),
        in_specs=[piece_spec(f, c) for f, c in zip(starts, counts)]
        + [pl.BlockSpec((s, n), lambda j: (0, 0), pipeline_mode=pl.Buffered(1))] + [ANY] * nx,
        out_specs=[pl.BlockSpec((tj, n), lambda j: (j, 0))] + [ANY] * nx,
        out_shape=[jax.ShapeDtypeStruct((sum(counts) * tj, n), BF16)] + x_shape, scratch_shapes=x_scratch,
        compiler_params=_params(("arbitrary",), x_id),
    )(*pieces, b, *x_arrs)
    return res[0] if exchange is None else (res[0], res[1:])


NO_EXCHANGE = ([], [], [], None, None)


def _sweep_marks(nt):
    i = pl.program_id(0)
    return i == 0, i == nt - 1, i == nt - 1


def _rowwise(name, fn, rows, bcasts, row_outs, red_outs=(), tm=256, exchange=NO_EXCHANGE):
    s = rows[0].shape[0]
    tm = _pick(s, tm, 16)
    nt = s // tm
    resident = pl.Buffered(1)
    nr, nb, no, nd = len(rows), len(bcasts), len(row_outs), len(red_outs)
    x_arrs, x_shape, x_scratch, _, x_id = exchange
    nx = len(x_arrs)
    first_out = nr + nb + nx

    def body(*refs):
        finish_exchange = _carry_exchange(exchange, refs, nr + nb, no + nd, *_sweep_marks(nt))
        ins = [r[...] for r in refs[:nr + nb]]
        outs, reds = fn(*ins)
        for ref, val in zip(refs[first_out:first_out + no], outs):
            ref[...] = val.astype(ref.dtype)
        i = pl.program_id(0)
        for ref, val in zip(refs[first_out + no:first_out + no + nd], reds):
            @pl.when(i == 0)
            def _():
                ref[...] = val

            @pl.when(i > 0)
            def _():
                ref[...] += val
        finish_exchange()

    def row_spec(a):
        assert a.shape[-2] % nt == 0, (name, a.shape, nt)
        if len(a.shape) == 3:
            return pl.BlockSpec((a.shape[0], a.shape[1] // nt, a.shape[2]), lambda i: (0, i, 0))
        return pl.BlockSpec((a.shape[0] // nt, a.shape[1]), lambda i: (i, 0))

    in_specs = [row_spec(r) for r in rows]
    in_specs += [pl.BlockSpec(b.shape, lambda i: (0, 0), pipeline_mode=resident) for b in bcasts]
    out_specs = [row_spec(o) for o in row_outs]
    out_specs += [pl.BlockSpec(d.shape, lambda i: (0, 0)) for d in red_outs]
    return pl.pallas_call(
        body, name=name, grid=(nt,), in_specs=in_specs + [ANY] * nx, out_specs=out_specs + [ANY] * nx,
        out_shape=list(row_outs) + list(red_outs) + x_shape, scratch_shapes=x_scratch,
        compiler_params=_params(("arbitrary",), x_id),
    )(*rows, *bcasts, *x_arrs)


def _sds(shape, dtype=F32):
    return jax.ShapeDtypeStruct(shape, dtype)


def _rms(x, g):
    y = x * lax.rsqrt(jnp.mean(x * x, axis=-1, keepdims=True) + EPS)
    return y * g


def _silu(x):
    return x * jax.nn.sigmoid(x)


def _swiglu(g, u):
    return _silu(g) * u


def _ln_silu(u, g, b):
    mu = jnp.mean(u, axis=-1, keepdims=True)
    var = jnp.mean(jnp.square(u - mu), axis=-1, keepdims=True)
    return _silu((u - mu) * lax.rsqrt(var + EPS) * g + b)


def _merge(conv_pre, att_out, g_conv, g_att, b_cb):
    return jax.nn.sigmoid(g_conv) * (conv_pre + b_cb) + jax.nn.sigmoid(g_att) * att_out


def _glu(t):
    return t[:, :CONV_DIM] * jax.nn.sigmoid(t[:, CONV_DIM:])


def _shifted_reader(buf, shifted, tm):
    for b in range(1, SUBLANES):
        shifted[b - 1, :, :] = buf[pl.ds(b, tm + HALO - SUBLANES), :]

    def read(o):
        a, b = divmod(o, SUBLANES)
        return buf[pl.ds(SUBLANES * a, tm), :] if b == 0 else shifted[b - 1, pl.ds(SUBLANES * a, tm), :]

    return read


def _conv_fwd(conv_in, w_pad, b, ln_g, ln_b, exchange, tm=256):
    s = conv_in.shape[0]
    tm = _pick(s, tm, HALO)
    ratio = tm // HALO
    x_arrs, x_shape, x_scratch, _, x_id = exchange
    nx = len(x_arrs)

    def body(*refs):
        main_ref, halo_ref, w_ref, b_ref, g_ref, be_ref = refs[:6]
        u3_ref, u1_ref = refs[6 + nx:8 + nx]
        buf, shifted = refs[-2:]
        finish_exchange = _carry_exchange(exchange, refs, 6, 2, *_sweep_marks(s // tm))
        i = pl.program_id(0)
        buf[0:HALO, :] = _glu(halo_ref[...]) * (i > 0).astype(F32)
        buf[HALO:HALO + tm, :] = _glu(main_ref[...])
        read = _shifted_reader(buf, shifted, tm)
        acc = jnp.zeros((tm, CONV_DIM), F32) + b_ref[...]
        for j in range(CONV_WIDTH):
            acc = acc + w_ref[j:j + 1, :] * read(HALO - (CONV_WIDTH - 1) + j)
        u1_ref[...] = acc
        u3_ref[...] = _ln_silu(acc, g_ref[...], be_ref[...]).astype(u3_ref.dtype)
        finish_exchange()

    res = pl.pallas_call(
        body, name="conv_fwd", grid=(s // tm,),
        in_specs=[pl.BlockSpec((tm, 2 * CONV_DIM), lambda i: (i, 0)),
                  pl.BlockSpec((HALO, 2 * CONV_DIM), lambda i: (jnp.maximum(i * ratio - 1, 0), 0)),
                  pl.BlockSpec(w_pad.shape, lambda i: (0, 0)),
                  pl.BlockSpec(b.shape, lambda i: (0, 0)),
                  pl.BlockSpec(ln_g.shape, lambda i: (0, 0)),
                  pl.BlockSpec(ln_b.shape, lambda i: (0, 0))] + [ANY] * nx,
        out_specs=[pl.BlockSpec((tm, CONV_DIM), lambda i: (i, 0)),
                   pl.BlockSpec((tm, CONV_DIM), lambda i: (i, 0))] + [ANY] * nx,
        out_shape=[_sds((s, CONV_DIM), BF16), _sds((s, CONV_DIM), F32)] + x_shape,
        scratch_shapes=x_scratch + [pltpu.VMEM((tm + HALO, CONV_DIM), F32),
                                    pltpu.VMEM((SUBLANES - 1, tm + HALO - SUBLANES, CONV_DIM), F32)],
        compiler_params=_params(("arbitrary",), x_id),
    )(conv_in, conv_in, w_pad, b, ln_g, ln_b, *x_arrs)
    return res[0], res[1], res[2:]


def _conv_bwd(conv_in, u1, du3, ln_g, ln_b, w_pad, exchange, tm=256):
    s = conv_in.shape[0]
    tm = _pick(s, tm, HALO)
    ratio = tm // HALO
    nt = s // tm
    last_halo = s // HALO - 1
    x_arrs, x_shape, x_scratch, _, x_id = exchange
    nx = len(x_arrs)

    def body(*refs):
        main_ref, halo_ref, u1_ref, u1n_ref, du3_ref, du3n_ref, g_ref, be_ref, w_ref = refs[:9]
        dci_ref, dw_ref, db_ref, dg_ref, dbe_ref = refs[9 + nx:14 + nx]
        ubuf, dbuf, ushift, dshift = refs[-4:]
        finish_exchange = _carry_exchange(exchange, refs, 9, 5, *_sweep_marks(nt))
        i = pl.program_id(0)
        main = main_ref[...]
        a = main[:, :CONV_DIM]
        sb = jax.nn.sigmoid(main[:, CONV_DIM:])
        ubuf[0:HALO, :] = _glu(halo_ref[...]) * (i > 0).astype(F32)
        ubuf[HALO:HALO + tm, :] = a * sb

        def ln_bwd(u1t, du3t):
            _, vjp = jax.vjp(_ln_silu, u1t, g_ref[...], be_ref[...])
            return vjp(du3t)

        du, dg, dbe = ln_bwd(u1_ref[...], du3_ref[...])
        dbuf[0:tm, :] = du
        dbuf[tm:tm + HALO, :] = ln_bwd(u1n_ref[...], du3n_ref[...])[0] * (i < nt - 1).astype(F32)

        @pl.when(i == 0)
        def _():
            dw_ref[...] = jnp.zeros_like(dw_ref)
            db_ref[...] = jnp.zeros_like(db_ref)
            dg_ref[...] = jnp.zeros_like(dg_ref)
            dbe_ref[...] = jnp.zeros_like(dbe_ref)

        dg_ref[...] += dg
        dbe_ref[...] += dbe

        read_u = _shifted_reader(ubuf, ushift, tm)
        read_d = _shifted_reader(dbuf, dshift, tm)
        du0 = jnp.zeros((tm, CONV_DIM), F32)
        for j in range(CONV_WIDTH):
            du0 = du0 + w_ref[j:j + 1, :] * read_d(CONV_WIDTH - 1 - j)
            dw_ref[j:j + 1, :] += jnp.sum(du * read_u(HALO - (CONV_WIDTH - 1) + j), axis=0, keepdims=True)
        db_ref[...] += jnp.sum(du, axis=0, keepdims=True)
        dci_ref[:, :CONV_DIM] = (du0 * sb).astype(dci_ref.dtype)
        dci_ref[:, CONV_DIM:] = (du0 * a * sb * (1.0 - sb)).astype(dci_ref.dtype)
        finish_exchange()

    res = pl.pallas_call(
        body, name="conv_bwd", grid=(nt,),
        in_specs=[pl.BlockSpec((tm, 2 * CONV_DIM), lambda i: (i, 0)),
                  pl.BlockSpec((HALO, 2 * CONV_DIM), lambda i: (jnp.maximum(i * ratio - 1, 0), 0))]
        + [pl.BlockSpec((tm, CONV_DIM), lambda i: (i, 0)),
           pl.BlockSpec((HALO, CONV_DIM), lambda i: (jnp.minimum((i + 1) * ratio, last_halo), 0))] * 2
        + [pl.BlockSpec((1, CONV_DIM), lambda i: (0, 0))] * 2 + [pl.BlockSpec(w_pad.shape, lambda i: (0, 0))]
        + [ANY] * nx,
        out_specs=[pl.BlockSpec((tm, 2 * CONV_DIM), lambda i: (i, 0)),
                   pl.BlockSpec(w_pad.shape, lambda i: (0, 0))]
        + [pl.BlockSpec((1, CONV_DIM), lambda i: (0, 0))] * 3 + [ANY] * nx,
        out_shape=[_sds((s, 2 * CONV_DIM), BF16), _sds(w_pad.shape)] + [_sds((1, CONV_DIM))] * 3 + x_shape,
        scratch_shapes=x_scratch + [pltpu.VMEM((tm + HALO, CONV_DIM), F32)] * 2
        + [pltpu.VMEM((SUBLANES - 1, tm + HALO - SUBLANES, CONV_DIM), F32)] * 2,
        compiler_params=_params(("arbitrary",), x_id),
    )(conv_in, conv_in, u1, u1, du3, du3, ln_g, ln_b, w_pad, *x_arrs)
    return res[:5], res[5:]


def _logsig_neg(z):
    return jnp.minimum(-z, 0.0) - jnp.log(1.0 + jnp.exp(-jnp.abs(z)))


def _split_dot(val, tri):
    hi = val.astype(BF16)
    lo = (val - hi.astype(F32)).astype(BF16)
    return jnp.dot(hi, tri, preferred_element_type=F32) + jnp.dot(lo, tri, preferred_element_type=F32)


def _attn_masks(t, later):
    row = lax.broadcasted_iota(jnp.int32, (t, t), 0)
    col = lax.broadcasted_iota(jnp.int32, (t, t), 1)
    tri = jnp.where(row > col if later else row <= col, 1.0, 0.0).astype(BF16)
    return col < row, tri


def _grid_marks(h, nq):
    hh, i = pl.program_id(0), pl.program_id(1)
    return (hh == 0) & (i == 0), (hh == h - 1) & (i == nq // 2), (hh == h - 1) & (i == nq - 1)


def _head_masks(shape):
    lane = lax.broadcasted_iota(jnp.int32, shape, len(shape) - 1)
    return lane < HEAD_DIM, lane >= HEAD_DIM


def _per_head(blk):
    m0, m1 = _head_masks(blk.shape)
    zero = jnp.zeros_like(blk)
    return jnp.where(m0, blk, zero), jnp.where(m1, blk, zero)


NT = (((1,), (1,)), ((), ()))
TN = (((0,), (0,)), ((), ()))


def _with_top(whole, top):
    rows = top.shape[0]
    return top if rows == whole.shape[0] else jnp.concatenate([top, whole[rows:]], axis=0)


def _attn_fwd(q, k, v, exchange):
    s = q.shape[0]
    hp = q.shape[1] // LANES
    t = ATT_TILE
    scale = 1.0 / math.sqrt(HEAD_DIM)
    x_arrs, x_shape, x_scratch, _, x_id = exchange
    nx = len(x_arrs)

    def body(*refs):
        q_ref, k_ref, v_ref = refs[:3]
        o_ref, lt_ref, nb_ref = refs[3 + nx:6 + nx]
        finish_exchange = _carry_exchange(exchange, refs, 3, 3, *_grid_marks(hp, s // t))
        i = pl.program_id(1)
        qs = _per_head((q_ref[...].astype(F32) * scale).astype(BF16))
        causal, tri = _attn_masks(t, later=True)

        def step(kb, carry, masked, rows):
            cs, acc = carry
            off = pl.multiple_of(kb * t, t)
            kblk = k_ref[pl.ds(off, t), :]
            vs = _per_head(v_ref[pl.ds(off, t), :])
            acc_top = acc[:rows]
            new_cs = []
            for hd in range(2):
                z = lax.dot_general(qs[hd][:rows], kblk, NT, preferred_element_type=F32)
                l = _logsig_neg(z)
                if masked:
                    l = jnp.where(causal, l, 0.0)
                e = z + l + _split_dot(l, tri) + cs[hd][:rows]
                if masked:
                    e = jnp.where(causal, e, -1e30)
                acc_top = acc_top + jnp.dot(jnp.exp(e).astype(BF16), vs[hd], preferred_element_type=F32)
                new_cs.append(_with_top(cs[hd], cs[hd][:rows] + jnp.sum(l, axis=1, keepdims=True)))
            return tuple(new_cs), _with_top(acc, acc_top)

        zero = jnp.zeros((t, 1), F32)
        carry = step(i, ((zero, zero), jnp.zeros((t, LANES), F32)), True, t)

        def live(cs, lo, hi):
            return jnp.maximum(jnp.max(cs[0][lo:hi]), jnp.max(cs[1][lo:hi])) > DEAD_SUM

        def more(state):
            n, _, (cs, _) = state
            return (n < i) & live(cs, 0, t)

        def sweep(state):
            n, n_full, cr = state
            whole = live(cr[0], ATT_PART, t)
            cr = lax.cond(whole, lambda c: step(i - 1 - n, c, False, t), lambda c: step(i - 1 - n, c, False, ATT_PART), cr)
            return n + 1, n_full + whole.astype(jnp.int32), cr

        n_blocks, n_full, carry = lax.while_loop(more, sweep, (jnp.int32(0), jnp.int32(0), carry))
        m0, _ = _head_masks((t, LANES))
        lt_ref[...] = jnp.where(m0, carry[0][0], carry[0][1])
        o_ref[...] = carry[1].astype(o_ref.dtype)
        nb_ref[0, pl.program_id(0), i] = n_blocks.astype(F32)
        nb_ref[1, pl.program_id(0), i] = n_full.astype(F32)
        finish_exchange()

    res = pl.pallas_call(
        body, name="attn_fwd", grid=(hp, s // t),
        in_specs=[pl.BlockSpec((t, LANES), lambda p, i: (i, p)),
                  pl.BlockSpec((s, LANES), lambda p, i: (0, p)),
                  pl.BlockSpec((s, LANES), lambda p, i: (0, p))] + [ANY] * nx,
        out_specs=[pl.BlockSpec((t, LANES), lambda p, i: (i, p)),
                   pl.BlockSpec((None, t, LANES), lambda p, i: (p, i, 0)),
                   pl.BlockSpec(memory_space=pltpu.SMEM)] + [ANY] * nx,
        out_shape=[_sds(q.shape, BF16), _sds((hp, s, LANES), F32), _sds((2, hp, s // t), F32)] + x_shape,
        scratch_shapes=x_scratch,
        compiler_params=_params(("arbitrary", "arbitrary"), x_id),
    )(q, k, v, *x_arrs)
    return res[0], res[1], res[2], res[3:]


def _attn_bwd(q, k, v, do, ltot, n_blocks, exchange):
    s = q.shape[0]
    hp = q.shape[1] // LANES
    t = ATT_TILE
    scale = 1.0 / math.sqrt(HEAD_DIM)
    x_arrs, x_shape, x_scratch, _, x_id = exchange
    nx = len(x_arrs)

    def body(*refs):
        q_ref, k_ref, v_ref, do_ref, lt_ref, nb_ref = refs[:6]
        dq_ref, dk_ref, dv_ref = refs[6 + nx:9 + nx]
        finish_exchange = _carry_exchange(exchange, refs, 6, 3, *_grid_marks(hp, s // t))
        i = pl.program_id(1)
        n_blocks = jnp.clip(nb_ref[0, pl.program_id(0), i].astype(jnp.int32), 0, i)
        n_full = jnp.clip(nb_ref[1, pl.program_id(0), i].astype(jnp.int32), 0, n_blocks)

        @pl.when(i == 0)
        def _():
            dk_ref[...] = jnp.zeros_like(dk_ref)
            dv_ref[...] = jnp.zeros_like(dv_ref)

        qb = q_ref[...]
        qm = _per_head(qb)
        qs = _per_head((qb.astype(F32) * scale).astype(BF16))
        dos = _per_head(do_ref[...])
        lts = (lt_ref[:, 0:1], lt_ref[:, HEAD_DIM:HEAD_DIM + 1])
        causal, tri = _attn_masks(t, later=False)

        def step(kb, carry, masked, rows):
            cls, cgs, dq = carry
            off = pl.multiple_of(kb * t, t)
            kblk = k_ref[pl.ds(off, t), :]
            vblk = v_ref[pl.ds(off, t), :]
            ks = _per_head(kblk)
            dq_top = dq[:rows]
            dk = jnp.zeros((t, LANES), F32)
            dv = jnp.zeros((t, LANES), F32)
            new_cls, new_cgs = [], []
            for hd in range(2):
                z = lax.dot_general(qs[hd][:rows], kblk, NT, preferred_element_type=F32)
                l = _logsig_neg(z)
                if masked:
                    l = jnp.where(causal, l, 0.0)
                e = z + l + ((lts[hd][:rows] - cls[hd][:rows]) - _split_dot(l, tri))
                if masked:
                    e = jnp.where(causal, e, -1e30)
                a = jnp.exp(e)
                g = lax.dot_general(dos[hd][:rows], vblk, NT, preferred_element_type=F32) * a
                p = cgs[hd][:rows] + jnp.dot(g.astype(BF16), tri, preferred_element_type=F32) - g
                el = jnp.exp(l)
                dz = g * el - p * (1.0 - el)
                if masked:
                    dz = jnp.where(causal, dz, 0.0)
                dzb = (dz * scale).astype(BF16)
                dq_top = dq_top + jnp.dot(dzb, ks[hd], preferred_element_type=F32)
                dk = dk + lax.dot_general(dzb, qm[hd][:rows], TN, preferred_element_type=F32)
                dv = dv + lax.dot_general(a.astype(BF16), dos[hd][:rows], TN, preferred_element_type=F32)
                new_cls.append(_with_top(cls[hd], cls[hd][:rows] + jnp.sum(l, axis=1, keepdims=True)))
                new_cgs.append(_with_top(cgs[hd], cgs[hd][:rows] + jnp.sum(g, axis=1, keepdims=True)))
            dk_ref[pl.ds(off, t), :] += dk
            dv_ref[pl.ds(off, t), :] += dv
            return tuple(new_cls), tuple(new_cgs), _with_top(dq, dq_top)

        zero = jnp.zeros((t, 1), F32)
        init = ((zero, zero), (zero, zero), jnp.zeros((t, LANES), F32))
        carry = lax.fori_loop(i - n_blocks, i - n_full, lambda kb, cr: step(kb, cr, False, ATT_PART), init)
        carry = lax.fori_loop(i - n_full, i, lambda kb, cr: step(kb, cr, False, t), carry)
        carry = step(i, carry, True, t)
        dq_ref[...] = carry[2]
        finish_exchange()

    blk = pl.BlockSpec((t, LANES), lambda p, i: (i, p))
    whole = pl.BlockSpec((s, LANES), lambda p, i: (0, p))
    res = pl.pallas_call(
        body, name="attn_bwd", grid=(hp, s // t),
        in_specs=[blk, whole, whole, blk, pl.BlockSpec((None, t, LANES), lambda p, i: (p, i, 0)),
                  pl.BlockSpec(memory_space=pltpu.SMEM)] + [ANY] * nx,
        out_specs=[blk, whole, whole] + [ANY] * nx,
        out_shape=[_sds(q.shape)] * 3 + x_shape,
        scratch_shapes=x_scratch,
        compiler_params=_params(("arbitrary", "arbitrary"), x_id),
    )(q, k, v, do, ltot, n_blocks, *x_arrs)
    return res[0], res[1], res[2], res[3:]


LATE = ["w_conv_branch", "w_att_branch", "w_out", "w_ffn_up", "w_ffn_down"]


def _full_weight(name, gathered):
    return _cols_to_full(gathered) if name in COL_SHARDED else gathered.reshape(-1, gathered.shape[2])


def _grad_slabs(name, grad):
    return _full_to_cols(grad) if name in COL_SHARDED else grad.reshape(N_DEV, -1, grad.shape[1])


def _side_slabs(name, grad):
    slabs = _grad_slabs(name, grad)
    return slabs.reshape((4, 2) + slabs.shape[1:])


def _local_step(x, target, w, late_blocks, opt):
    s = x.shape[0]
    w = dict(w)
    g1, g2, g3, g4 = w["norm_mix_pre"], w["norm_mix_post"], w["norm_ffn_pre"], w["norm_ffn_post"]

    w_in = w["w_in"]

    def proj_fn(xt, g1_, w_in_t):
        h = _rms(xt, g1_).astype(BF16)
        proj = lax.dot_general(h, w_in_t, NT, preferred_element_type=F32)
        return (h, *[proj[:, IN_SPLITS[n]:IN_SPLITS[n + 1]] for n in range(6)]), ()

    mix_weights = ["w_conv_branch", "w_att_branch", "w_out"]
    h1, conv_in, q, k, v, g_conv, g_att, g_out = _rowwise(
        "norm_proj", proj_fn, [x], [g1, w_in],
        [_sds((s, D_MODEL), BF16), _sds((s, 2 * CONV_DIM)), _sds((s, ATT_DIM), BF16), _sds((s, ATT_DIM), BF16),
         _sds((s, ATT_DIM), BF16), _sds((s, D_MODEL), BF16), _sds((s, D_MODEL), BF16)], tm=512,
        exchange=_gather_exchange([late_blocks["w_out"]]))

    u3, u1, g_branches = _conv_fwd(conv_in, w["conv_dw_w"], w["conv_dw_b"], w["conv_ln_g"], w["conv_ln_b"],
                                   _gather_exchange([late_blocks[nm] for nm in mix_weights[:2]]))
    for nm, g in zip(mix_weights, [*g_branches, g_out]):
        w[nm] = _full_weight(nm, g)
    att, ltot, n_blocks, (g_up,) = _attn_fwd(q, k, v, _gather_exchange([late_blocks["w_ffn_up"]]))
    w["w_ffn_up"] = _full_weight("w_ffn_up", g_up)

    def merge_fn(u3t, at, gc, ga, xt, w_cb, w_ab, b_cb, w_out, g2_, g3_):
        cp = jnp.dot(u3t, w_cb, preferred_element_type=F32)
        ao = jnp.dot(at, w_ab, preferred_element_type=F32)
        mg = _merge(cp, ao, gc.astype(F32), ga.astype(F32), b_cb).astype(BF16)
        mix_ = jnp.dot(mg, w_out, preferred_element_type=F32)
        x2_ = xt + _rms(mix_, g2_)
        return (mg, cp, ao, mix_, x2_, _rms(x2_, g3_)), ()

    half = D_MODEL // 2
    down_block = late_blocks["w_ffn_down"]
    merged, conv_pre, att_out, mix, x2, h2, g_left = _rowwise(
        "branch_merge_mix", merge_fn, [u3, att, g_conv, g_att, x],
        [w["w_conv_branch"], w["w_att_branch"], w["b_conv_branch"], w["w_out"], g2, g3],
        [_sds((s, D_MODEL), BF16)] * 3 + [_sds((s, D_MODEL)), _sds((s, D_MODEL)), _sds((s, D_MODEL), BF16)], tm=512,
        exchange=_gather_exchange([down_block[:, :half]]))

    def ffn_up_fn(ht, w_up_t):
        gu_ = lax.dot_general(ht, w_up_t, NT, preferred_element_type=F32)
        return (gu_, _swiglu(gu_[:, :D_FF], gu_[:, D_FF:])), ()

    gu, act, g_right = _rowwise("ffn_up", ffn_up_fn, [h2], [w["w_ffn_up"]],
                                [_sds((s, 2 * D_FF), BF16), _sds((s, D_FF), BF16)], tm=512,
                                exchange=_gather_exchange([down_block[:, half:]]))
    w_down = [_full_weight("w_ffn_down", g) for g in (g_left, g_right)]

    def final_fn(at, x2t, tgt, w_left, w_right, g4_):
        ff = jnp.concatenate([jnp.dot(at, w_left, preferred_element_type=F32),
                              jnp.dot(at, w_right, preferred_element_type=F32)], axis=1)
        n4, vjp = jax.vjp(_rms, ff, g4_)
        err = x2t + n4 - tgt
        dy = err * (1.0 / D_MODEL)
        dff, dg4 = vjp(dy)
        return (dy, dff), (jnp.sum(err * err, axis=0, keepdims=True), dg4)

    dy, dff, loss_cols, d_g4 = _rowwise("ffn_down_loss", final_fn, [act, x2, target], [*w_down, g4],
                                        [_sds((s, D_MODEL)), _sds((s, D_MODEL), BF16)],
                                        [_sds((1, D_MODEL)), _sds((1, D_MODEL))], tm=512)
    loss = 0.5 * jnp.sum(loss_cols) / D_MODEL

    d_w_down = _tn_matmul(act, dff, name="d_w_down")

    def act_bwd_fn(dfft, gut, w_left, w_right):
        d_act = (lax.dot_general(dfft[:, :half], w_left, NT, preferred_element_type=F32)
                 + lax.dot_general(dfft[:, half:], w_right, NT, preferred_element_type=F32))
        gu_ = gut.astype(F32)
        _, vjp = jax.vjp(_swiglu, gu_[:, :D_FF], gu_[:, D_FF:])
        return (jnp.concatenate(vjp(d_act), axis=1),), ()

    down_slabs = _side_slabs("w_ffn_down", d_w_down)
    dgu, theirs = _rowwise("ffn_act_bwd", act_bwd_fn, [dff, gu], w_down, [_sds((s, 2 * D_FF), BF16)],
                           exchange=_pair_exchange([down_slabs]))
    down_sums = _pair_sum("pair_sum_w_ffn_down", down_slabs, theirs)
    d_w_up = _tn_matmul(dgu, h2, name="d_w_up")
    received = {}
    up_slabs = _side_slabs("w_ffn_up", d_w_up)

    def mid_bwd_fn(dgut, xt, mt, dyt, w_up_t, g2_, g3_):
        dh = jnp.dot(dgut, w_up_t, preferred_element_type=F32)
        n2, vjp2 = jax.vjp(_rms, mt, g2_)
        x2_ = xt + n2
        _, vjp3 = jax.vjp(_rms, x2_, g3_)
        dx2_, dg3 = vjp3(dh)
        dx2_ = dx2_ + dyt
        dmix_, dg2 = vjp2(dx2_)
        return (dx2_, dmix_), (dg2, dg3)

    dx2, dmix, d_g2, d_g3, received["w_ffn_down"] = _rowwise(
        "ffn_up_mid_bwd", mid_bwd_fn, [dgu, x, mix, dy], [w["w_ffn_up"], g2, g3],
        [_sds((s, D_MODEL)), _sds((s, D_MODEL), BF16)], [_sds((1, D_MODEL)), _sds((1, D_MODEL))], tm=512,
        exchange=_chip_exchange([down_sums]))
    d_w_out = _tn_matmul(merged, dmix, name="d_w_out")

    def merge_bwd_fn(dmt, cp, ao, gc, ga, w_out, w_cb, w_ab, b_cb):
        dm = lax.dot_general(dmt, w_out, NT, preferred_element_type=F32)
        _, vjp = jax.vjp(_merge, cp.astype(F32), ao.astype(F32), gc.astype(F32), ga.astype(F32), b_cb)
        dcp, dao, dgc, dga, dbias = vjp(dm)
        dcp, dao = dcp.astype(BF16), dao.astype(BF16)
        du3_ = lax.dot_general(dcp, w_cb, NT, preferred_element_type=F32)
        datt_ = lax.dot_general(dao, w_ab, NT, preferred_element_type=F32)
        return (dcp, dao, dgc, dga, du3_, datt_), (dbias,)

    d_conv_out, d_att_out, d_g_conv, d_g_att, du3, d_att, d_b_cb, theirs = _rowwise(
        "merge_bwd", merge_bwd_fn, [dmix, conv_pre, att_out, g_conv, g_att],
        [w["w_out"], w["w_conv_branch"], w["w_att_branch"], w["b_conv_branch"]],
        [_sds((s, D_MODEL), BF16)] * 4 + [_sds((s, CONV_DIM)), _sds((s, ATT_DIM), BF16)], [_sds((1, D_MODEL))], tm=512,
        exchange=_pair_exchange([up_slabs]))

    d_w_cb = _tn_matmul(u3, d_conv_out, name="d_w_conv_branch")
    d_w_ab = _tn_matmul(att, d_att_out, name="d_w_att_branch")

    dq, dk, dv, (received["w_ffn_up"],) = _attn_bwd(
        q, k, v, d_att, ltot, n_blocks, _chip_exchange([_pair_sum("pair_sum_w_ffn_up", up_slabs, theirs)]))

    mix_grads = {"w_conv_branch": d_w_cb, "w_att_branch": d_w_ab, "w_out": d_w_out}
    (d_conv_in, d_dw_w, d_dw_b, d_ln_g, d_ln_b), landed = _conv_bwd(
        conv_in, u1, du3, w["conv_ln_g"], w["conv_ln_b"], w["conv_dw_w"],
        _scatter_exchange([_grad_slabs(nm, mix_grads[nm]) for nm in mix_weights[:2]]))
    received.update(zip(mix_weights[:2], landed))

    d_proj = [d_conv_in, dq, dk, dv, d_g_conv, d_g_att]
    d_w_in, (received["w_out"],) = _pieces_tn_matmul(
        d_proj, h1, name="d_w_in", exchange=_scatter_exchange([_grad_slabs("w_out", d_w_out)]))
    in_slabs = _side_slabs("w_in", d_w_in)
    (theirs,) = _exchange_call("pair_swap_w_in", _pair_exchange([in_slabs]))

    early = list(opt)

    def pre_bwd_fn(*args):
        groups, (xt, dx2t), jobs, (w_in_t, g_) = args[:6], args[6:8], args[8:-2], args[-2:]
        dh = sum(jnp.dot(grp.astype(BF16), w_in_t[IN_SPLITS[n]:IN_SPLITS[n + 1]], preferred_element_type=F32)
                 for n, grp in enumerate(groups))
        _, vjp = jax.vjp(_rms, xt, g_)
        dx_, dg_ = vjp(dh)
        updates = [_sum_adamw_tile(*jobs[4 * n:4 * n + 4]) for n in range(len(early))]
        return (dx_ + dx2t, *[u for four in updates for u in four]), (dg_,)

    res = _rowwise(
        "proj_norm_bwd", pre_bwd_fn,
        d_proj + [x, dx2] + [a for nm in early for a in (received[nm], *opt[nm])], [w_in, g1],
        [_sds((s, D_MODEL))] + [_sds(opt[nm][0].shape) for nm in early for _ in range(4)],
        [_sds((1, D_MODEL))], tm=512, exchange=_chip_exchange([_pair_sum("pair_sum_w_in", in_slabs, theirs)]))
    grad_x, d_g1, received["w_in"] = res[0], res[-2], res[-1]
    updated = {nm: res[1 + 4 * n:5 + 4 * n] for n, nm in enumerate(early)}

    grads = {
        "norm_mix_pre": d_g1, "conv_dw_w": d_dw_w, "conv_dw_b": d_dw_b,
        "conv_ln_g": d_ln_g, "conv_ln_b": d_ln_b, "b_conv_branch": d_b_cb,
        "norm_mix_post": d_g2, "norm_ffn_pre": d_g3, "norm_ffn_post": d_g4,
    }
    return loss, grad_x, received, updated, grads


def _place():
    x, y, c = lax.axis_index("x"), lax.axis_index("y"), lax.axis_index("c")
    return x, y, c


def _slot(px, py, pc):
    return 4 * px + 2 * py + pc


def _exchange_scratch(n):
    return [pltpu.SemaphoreType.DMA((7 * n,)), pltpu.SemaphoreType.DMA((7 * n,)), pltpu.SemaphoreType.DMA((n,))]


GATHER_ID, SCATTER_ID, PAIR_ID, CHIP_ID, RELAY_ID = 0, 1, 2, 3, 4


def _handshake(peers):
    barrier = pltpu.get_barrier_semaphore()
    for peer in peers:
        pl.semaphore_signal(barrier, inc=1, device_id=peer, device_id_type=MESH)
    pl.semaphore_wait(barrier, len(peers))


def _gather_exchange(arrs):
    n = len(arrs)

    def phases(ins, outs, send_sems, recv_sems, local_sems):
        x, y, c = _place()
        me, sibling = (x, y, c), (x, y, 1 - c)
        chips = [(1 - x, y), (x, 1 - y), (1 - x, 1 - y)]

        def copy(a, kk, block, to, src=None):
            dst = outs[a].at[_slot(*block)]
            return pltpu.make_async_remote_copy(
                src_ref=dst if src is None else src, dst_ref=dst,
                send_sem=send_sems.at[a * 7 + kk], recv_sem=recv_sems.at[a * 7 + kk],
                device_id=to, device_id_type=MESH)

        mine = [pltpu.make_async_copy(ins[a], outs[a].at[_slot(*me)], local_sems.at[a]) for a in range(n)]
        first = []
        for a in range(n):
            first.append(copy(a, 0, me, sibling, src=ins[a]))
            first += [copy(a, 1 + j, me, (*chip, c), src=ins[a]) for j, chip in enumerate(chips)]
        passed = [copy(a, 4 + j, (*chip, c), sibling) for j, chip in enumerate(chips) for a in range(n)]

        def send():
            _handshake([sibling] + [(*chip, c) for chip in chips])
            for cp in mine + first:
                cp.start()

        def pass_on():
            for j, chip in enumerate(chips):
                for a in range(n):
                    copy(a, 1 + j, (*chip, c), me).wait_recv()
                    passed[j * n + a].start()

        def finish():
            for a in range(n):
                copy(a, 0, sibling, me).wait_recv()
                for j, chip in enumerate(chips):
                    copy(a, 4 + j, (*chip, 1 - c), me).wait_recv()
            for cp in first + passed:
                cp.wait_send()
            for cp in mine:
                cp.wait()

        return [send, pass_on, finish]

    return list(arrs), [_sds((N_DEV,) + a.shape, a.dtype) for a in arrs], _exchange_scratch(n), phases, GATHER_ID


def _relay_gather_exchange(arrs):
    n = len(arrs)
    per = 8

    def phases(ins, outs, send_sems, recv_sems, local_sems):
        x, y, c = _place()
        me, sibling = (x, y, c), (x, y, 1 - c)
        beside, below, across = (1 - x, y, c), (x, 1 - y, c), (1 - x, 1 - y, c)

        def copy(a, kk, block, to, src=None, rows=None):
            where = _slot(*block) if rows is None else (_slot(*block), rows)
            dst = outs[a].at[where]
            return pltpu.make_async_remote_copy(
                src_ref=dst if src is None else src, dst_ref=dst,
                send_sem=send_sems.at[a * per + kk], recv_sem=recv_sems.at[a * per + kk],
                device_id=to, device_id_type=MESH)

        def halves(a):
            h = ins[a].shape[0] // 2
            return pl.ds(0, h), pl.ds(h, ins[a].shape[0] - h)

        mine = [pltpu.make_async_copy(ins[a], outs[a].at[_slot(*me)], local_sems.at[a]) for a in range(n)]
        first = [copy(a, kk, me, to, src=ins[a]) for a in range(n) for kk, to in enumerate([sibling, beside, below])]
        relayed = [[copy(a, 3, beside, sibling), copy(a, 5, beside, below, rows=halves(a)[0])] for a in range(n)]
        relayed += [[copy(a, 4, below, sibling), copy(a, 6, below, beside, rows=halves(a)[1])] for a in range(n)]
        passed = [copy(a, 7, across, sibling) for a in range(n)]

        def send():
            _handshake([sibling, beside, below])
            for cp in mine + first:
                cp.start()

        def relay():
            for kk, block in ((1, beside), (2, below)):
                for a in range(n):
                    copy(a, kk, block, me).wait_recv()
                    for cp in relayed[(kk - 1) * n + a]:
                        cp.start()

        def pass_on():
            for a in range(n):
                copy(a, 5, across, me, rows=halves(a)[0]).wait_recv()
                copy(a, 6, across, me, rows=halves(a)[1]).wait_recv()
                passed[a].start()

        def finish():
            for a in range(n):
                for kk, block in ((0, me), (3, beside), (4, below), (7, across)):
                    copy(a, kk, (*block[:2], 1 - c), me).wait_recv()
            for cp in first + [cp for two in relayed for cp in two] + passed:
                cp.wait_send()
            for cp in mine:
                cp.wait()

        return [send, relay, pass_on, finish]

    scratch = [pltpu.SemaphoreType.DMA((per * n,)), pltpu.SemaphoreType.DMA((per * n,)), pltpu.SemaphoreType.DMA((n,))]
    return list(arrs), [_sds((N_DEV,) + a.shape, a.dtype) for a in arrs], scratch, phases, RELAY_ID


def _scatter_exchange(arrs):
    n = len(arrs)
    flips = [(fx, fy, fc) for fx in (0, 1) for fy in (0, 1) for fc in (0, 1)][1:]

    def phases(ins, outs, send_sems, recv_sems, local_sems):
        x, y, c = _place()
        mine = _slot(x, y, c)
        local = [pltpu.make_async_copy(ins[a].at[mine], outs[a].at[mine], local_sems.at[a]) for a in range(n)]
        peers = [((1 - x) if fx else x, (1 - y) if fy else y, (1 - c) if fc else c) for fx, fy, fc in flips]

        def copy(a, kk, src_slot, dst_slot):
            return pltpu.make_async_remote_copy(
                src_ref=ins[a].at[src_slot], dst_ref=outs[a].at[dst_slot],
                send_sem=send_sems.at[a * 7 + kk], recv_sem=recv_sems.at[a * 7 + kk],
                device_id=peers[kk], device_id_type=MESH)

        sends = [copy(a, kk, _slot(*peers[kk]), mine) for a in range(n) for kk in range(7)]

        def send():
            _handshake(peers)
            for cp in local + sends:
                cp.start()

        def finish():
            for a in range(n):
                for kk in range(7):
                    copy(a, kk, mine, _slot(*peers[kk])).wait_recv()
            for cp in sends:
                cp.wait_send()
            for cp in local:
                cp.wait()

        return [send, finish]

    return list(arrs), [_sds(a.shape, a.dtype) for a in arrs], _exchange_scratch(n), phases, SCATTER_ID


def _pair_exchange(arrs):
    n = len(arrs)

    def phases(ins, outs, send_sems, recv_sems, local_sems):
        x, y, c = _place()

        def copy(a, chip, side):
            return pltpu.make_async_remote_copy(
                src_ref=ins[a].at[chip, side], dst_ref=outs[a].at[chip],
                send_sem=send_sems.at[a * 7 + chip], recv_sem=recv_sems.at[a * 7 + chip],
                device_id=(x, y, 1 - c), device_id_type=MESH)

        sends = [copy(a, chip, 1 - c) for a in range(n) for chip in range(4)]

        def send():
            _handshake([(x, y, 1 - c)])
            for cp in sends:
                cp.start()

        def finish():
            for a in range(n):
                for chip in range(4):
                    copy(a, chip, c).wait_recv()
            for cp in sends:
                cp.wait_send()

        return [send, finish]

    return list(arrs), [_sds((4,) + a.shape[2:], a.dtype) for a in arrs], _exchange_scratch(n), phases, PAIR_ID


def _chip_exchange(arrs):
    n = len(arrs)

    def phases(ins, outs, send_sems, recv_sems, local_sems):
        x, y, c = _place()
        mine = 2 * x + y
        chips = [(1 - x, y), (x, 1 - y), (1 - x, 1 - y)]
        local = [pltpu.make_async_copy(ins[a].at[mine], outs[a].at[mine], local_sems.at[a]) for a in range(n)]

        def copy(a, j, src_slot, dst_slot):
            return pltpu.make_async_remote_copy(
                src_ref=ins[a].at[src_slot], dst_ref=outs[a].at[dst_slot],
                send_sem=send_sems.at[a * 7 + j], recv_sem=recv_sems.at[a * 7 + j],
                device_id=(*chips[j], c), device_id_type=MESH)

        sends = [copy(a, j, 2 * chips[j][0] + chips[j][1], mine) for a in range(n) for j in range(3)]

        def send():
            _handshake([(*chip, c) for chip in chips])
            for cp in local + sends:
                cp.start()

        def finish():
            for a in range(n):
                for j in range(3):
                    copy(a, j, mine, 2 * chips[j][0] + chips[j][1]).wait_recv()
            for cp in sends:
                cp.wait_send()
            for cp in local:
                cp.wait()

        return [send, finish]

    return list(arrs), [_sds(a.shape, a.dtype) for a in arrs], _exchange_scratch(n), phases, CHIP_ID


def _pair_sum(name, mine, theirs):
    _, _, r, c = mine.shape

    def body(side_ref, m_ref, t_ref, o_ref):
        o_ref[...] = (m_ref[...].astype(F32) + t_ref[...].astype(F32)).astype(o_ref.dtype)

    return pl.pallas_call(
        body, name=name,
        grid_spec=pltpu.PrefetchScalarGridSpec(
            num_scalar_prefetch=1, grid=(4,),
            in_specs=[pl.BlockSpec((None, None, r, c), lambda j, side: (j, side[0], 0, 0)),
                      pl.BlockSpec((None, r, c), lambda j, side: (j, 0, 0))],
            out_specs=pl.BlockSpec((None, r, c), lambda j, side: (j, 0, 0))),
        out_shape=_sds(theirs.shape, theirs.dtype),
        compiler_params=_params(("parallel",)),
    )(lax.axis_index("c").astype(jnp.int32).reshape(1), mine, theirs)


def _exchange_call(name, exchange):
    arrs, out_shape, scratch, phases, collective_id = exchange
    n = len(arrs)

    def body(*refs):
        for step in phases(refs[:n], refs[n:2 * n], *refs[2 * n:]):
            step()

    return pl.pallas_call(body, name=name, in_specs=[ANY] * n, out_specs=[ANY] * n,
                          out_shape=out_shape, scratch_shapes=scratch,
                          compiler_params=pltpu.CompilerParams(collective_id=collective_id))(*arrs)


def _carry_exchange(exchange, refs, n_in, n_out, first, middle, last):
    arrs, _, _, phases, _ = exchange
    n = len(arrs)
    if n == 0:
        return lambda: None
    ins = refs[n_in:n_in + n]
    outs = refs[n_in + n + n_out:n_in + 2 * n + n_out]
    sems = n_in + 2 * n + n_out
    steps = phases(ins, outs, *refs[sems:sems + 3])
    pl.when(first)(steps[0])
    if len(steps) == 3:
        pl.when(middle)(steps[1])
    return lambda: pl.when(last)(steps[-1])


def _adamw_math(w, g, m, v):
    m2 = ADAM_B1 * m + (1.0 - ADAM_B1) * g
    v2 = ADAM_B2 * v + (1.0 - ADAM_B2) * jnp.square(g)
    m_hat = m2 / (1.0 - ADAM_B1 ** ADAM_STEP)
    v_hat = v2 / (1.0 - ADAM_B2 ** ADAM_STEP)
    delta = -ADAM_LR * (m_hat / (jnp.sqrt(v_hat) + ADAM_EPS) + ADAM_WD * w)
    return delta, m2, v2


def _sum_adamw_tile(parts, w, m, v):
    g = parts[0].astype(F32)
    for d in range(1, parts.shape[0]):
        g = g + parts[d].astype(F32)
    return (g, *_adamw_math(w, g, m, v))


def _sum_adamw(name, parts, w, m, v, tr=256):
    p, r, c = parts.shape
    tr = _pick(r, tr, 16)

    def body(p_ref, w_ref, m_ref, v_ref, g_ref, d_ref, m2_ref, v2_ref):
        g_ref[...], d_ref[...], m2_ref[...], v2_ref[...] = _sum_adamw_tile(p_ref[...], w_ref[...], m_ref[...], v_ref[...])

    tile = pl.BlockSpec((tr, c), lambda i: (i, 0))
    return pl.pallas_call(
        body, name=name, grid=(r // tr,),
        in_specs=[pl.BlockSpec((p, tr, c), lambda i: (0, i, 0)), tile, tile, tile],
        out_specs=[tile] * 4, out_shape=[_sds((r, c))] * 4,
        compiler_params=_params(("parallel",)),
    )(parts, w, m, v)


def _sum_parts(name, parts):
    p, r, c = parts.shape

    def body(p_ref, o_ref):
        g = p_ref[0]
        for d in range(1, p):
            g = g + p_ref[d]
        o_ref[...] = g

    return pl.pallas_call(
        body, name=name, out_shape=_sds((r, c)),
        in_specs=[pl.BlockSpec(memory_space=pltpu.VMEM)], out_specs=pl.BlockSpec(memory_space=pltpu.VMEM),
    )(parts)


WEIGHTS = ["norm_mix_pre", "w_in", "conv_dw_w", "conv_dw_b", "conv_ln_g", "conv_ln_b", "w_conv_branch",
           "b_conv_branch", "w_att_branch", "w_out", "norm_mix_post", "norm_ffn_pre", "w_ffn_up", "w_ffn_down",
           "norm_ffn_post"]
COL_SHARDED = ["w_conv_branch", "w_att_branch"]
TRANSPOSED = ["w_in", "w_ffn_up"]
VECTORS = ["norm_mix_pre", "conv_dw_b", "conv_ln_g", "conv_ln_b", "b_conv_branch", "norm_mix_post",
           "norm_ffn_pre", "norm_ffn_post"]


def _cols_to_full(g):
    return g.transpose(1, 0, 2).reshape(g.shape[1], N_DEV * g.shape[2])


def _full_to_cols(f):
    return f.reshape(f.shape[0], N_DEV, f.shape[1] // N_DEV).transpose(1, 0, 2)


PACK_ROWS = 7


def _pack_vectors(vecs, extra=None):
    parts = [vecs[nm].reshape(-1) for nm in VECTORS]
    parts.append(jnp.zeros((1,), F32) if extra is None else extra.reshape(1))
    used = sum(p.size for p in parts)
    parts.append(jnp.zeros((PACK_ROWS * D_MODEL - used,), F32))
    return jnp.concatenate(parts).reshape(PACK_ROWS, D_MODEL)


def _unpack_vectors(packed, sizes):
    flat, out, at = packed.reshape(-1), {}, 0
    for nm in VECTORS:
        out[nm] = flat[at:at + sizes[nm]]
        at += sizes[nm]
    return out, flat[at]


def kernel(x, norm_mix_pre, w_in, conv_dw_w, conv_dw_b, conv_ln_g, conv_ln_b, w_conv_branch, b_conv_branch, w_att_branch, w_out, norm_mix_post, norm_ffn_pre, w_ffn_up, w_ffn_down, norm_ffn_post, loss_target, m_norm_mix_pre, m_w_in, m_conv_dw_w, m_conv_dw_b, m_conv_ln_g, m_conv_ln_b, m_w_conv_branch, m_b_conv_branch, m_w_att_branch, m_w_out, m_norm_mix_post, m_norm_ffn_pre, m_w_ffn_up, m_w_ffn_down, m_norm_ffn_post, v_norm_mix_pre, v_w_in, v_conv_dw_w, v_conv_dw_b, v_conv_ln_g, v_conv_ln_b, v_w_conv_branch, v_b_conv_branch, v_w_att_branch, v_w_out, v_norm_mix_post, v_norm_ffn_pre, v_w_ffn_up, v_w_ffn_down, v_norm_ffn_post):
    ws = dict(zip(WEIGHTS, [norm_mix_pre, w_in, conv_dw_w, conv_dw_b, conv_ln_g, conv_ln_b, w_conv_branch,
                            b_conv_branch, w_att_branch, w_out, norm_mix_post, norm_ffn_pre, w_ffn_up, w_ffn_down,
                            norm_ffn_post]))
    ms = dict(zip(WEIGHTS, [m_norm_mix_pre, m_w_in, m_conv_dw_w, m_conv_dw_b, m_conv_ln_g, m_conv_ln_b,
                            m_w_conv_branch, m_b_conv_branch, m_w_att_branch, m_w_out, m_norm_mix_post,
                            m_norm_ffn_pre, m_w_ffn_up, m_w_ffn_down, m_norm_ffn_post]))
    vs = dict(zip(WEIGHTS, [v_norm_mix_pre, v_w_in, v_conv_dw_w, v_conv_dw_b, v_conv_ln_g, v_conv_ln_b,
                            v_w_conv_branch, v_b_conv_branch, v_w_att_branch, v_w_out, v_norm_mix_post,
                            v_norm_ffn_pre, v_w_ffn_up, v_w_ffn_down, v_norm_ffn_post]))

    dw_block = jnp.pad(conv_dw_w, ((0, 1), (0, 0)))
    g_in, g_dw = _exchange_call("gather_first", _relay_gather_exchange([w_in.T.astype(BF16), dw_block]))
    full = {"w_in": _full_weight("w_in", g_in), "conv_dw_w": _cols_to_full(g_dw)}
    for nm in VECTORS:
        full[nm] = ws[nm].reshape(1, -1)

    def as_kept(nm, a):
        return a.T if nm in TRANSPOSED else a

    ride_along = ["w_ffn_up", "w_out"]
    loss_local, grad_x, received, updated, grads = _local_step(
        x[0], loss_target[0], full, {nm: as_kept(nm, ws[nm]).astype(BF16) for nm in LATE},
        {nm: tuple(as_kept(nm, a[nm]) for a in (ws, ms, vs)) for nm in ride_along})

    small = _exchange_call("gather_small_grads", _gather_exchange(
        [_pack_vectors(grads, extra=loss_local), grads["conv_dw_w"]]))
    out_g, out_d, out_m, out_v = {}, {}, {}, {}
    for nm in LATE + ["w_in"]:
        res = updated[nm] if nm in updated else _sum_adamw(
            "adamw_" + nm, received[nm], *[as_kept(nm, a[nm]) for a in (ws, ms, vs)])
        out_g[nm], out_d[nm], out_m[nm], out_v[nm] = [as_kept(nm, r) for r in res]
    sizes = {nm: ws[nm].size for nm in VECTORS}
    vec = _sum_adamw("adamw_vectors", small[0], _pack_vectors(ws), _pack_vectors(ms), _pack_vectors(vs))
    for res, dst in zip(vec, (out_g, out_d, out_m, out_v)):
        dst.update(_unpack_vectors(res, sizes)[0])
    loss = _unpack_vectors(vec[0], sizes)[1]
    dw_full = _sum_parts("sum_dw_grads", small[1])
    me = _slot(*_place())
    dw_mine = lax.dynamic_slice(dw_full, (0, me * (CONV_DIM // N_DEV)), (CONV_WIDTH, CONV_DIM // N_DEV))
    nm = "conv_dw_w"
    out_g[nm], out_d[nm], out_m[nm], out_v[nm] = _sum_adamw("adamw_dw", dw_mine[None], ws[nm], ms[nm], vs[nm])

    outs = [loss, grad_x[None]]
    for group in (out_g, out_d, out_m, out_v):
        outs += [group[nm] for nm in WEIGHTS]
    return tuple(outs)
```

```python
import math

import jax
import jax.numpy as jnp
from jax import lax
from jax.experimental import pallas as pl
from jax.experimental.pallas import tpu as pltpu

F32 = jnp.float32
BF16 = jnp.bfloat16

N_DEV = 8
D_MODEL = 1024
CONV_DIM = 512
CONV_WIDTH = 31
N_HEADS = 8
HEAD_DIM = 64
ATT_DIM = N_HEADS * HEAD_DIM
D_FF = 2816
EPS = 1e-6
IN_SPLITS = (0, 1024, 1536, 2048, 2560, 3584, 4608)

ADAM_LR = 0.001
ADAM_B1 = 0.9
ADAM_B2 = 0.999
ADAM_EPS = 1e-08
ADAM_WD = 0.01
ADAM_STEP = 10

LANES = 128
SUBLANES = 8
HALO = 32
ATT_TILE = 256
ATT_PART = 176
DEAD_SUM = -120.0
VMEM_LIMIT = 56 * 1024 * 1024
MESH = pl.DeviceIdType.MESH
ANY = pl.BlockSpec(memory_space=pl.ANY)


def _pick(dim, target, align=LANES):
    t = min(dim, target)
    t -= t % align
    while t >= align:
        if dim % t == 0:
            return t
        t -= align
    return dim


def _params(semantics, collective_id=None):
    return pltpu.CompilerParams(dimension_semantics=semantics, vmem_limit_bytes=VMEM_LIMIT,
                                collective_id=collective_id)


def _tn_matmul(a, b, *, name):
    return _pieces_tn_matmul([a], b, name=name, tj=_pick(a.shape[1], 1408))


def _pieces_tn_matmul(pieces, b, *, name, tj=512, exchange=None):
    s, n = b.shape
    counts = [p.shape[1] // tj for p in pieces]
    starts = [sum(counts[:i]) for i in range(len(pieces))]
    assert all(p.shape == (s, c * tj) for p, c in zip(pieces, counts))
    x_arrs, x_shape, x_scratch, _, x_id = exchange or NO_EXCHANGE
    nx, n_in = len(x_arrs), len(pieces) + 1

    def body(*refs):
        b_ref, o_ref = refs[n_in - 1], refs[n_in + nx]
        finish_exchange = _carry_exchange(exchange or NO_EXCHANGE, refs, n_in, 1, *_sweep_marks(sum(counts)))
        j = pl.program_id(0)
        for p_ref, first, count in zip(refs, starts, counts):
            @pl.when((j >= first) & (j < first + count))
            def _():
                o_ref[...] = lax.dot_general(p_ref[...].astype(BF16), b_ref[...], TN,
                                             preferred_element_type=F32).astype(o_ref.dtype)
        finish_exchange()

    def piece_spec(first, count):
        return pl.BlockSpec((s, tj), lambda j: (0, jnp.clip(j - first, 0, count - 1)))

    res = pl.pallas_call(
        body, name=name, grid=(sum(counts),),
        in_specs=[piece_spec(f, c) for f, c in zip(starts, counts)]
        + [pl.BlockSpec((s, n), lambda j: (0, 0), pipeline_mode=pl.Buffered(1))] + [ANY] * nx,
        out_specs=[pl.BlockSpec((tj, n), lambda j: (j, 0))] + [ANY] * nx,
        out_shape=[jax.ShapeDtypeStruct((sum(counts) * tj, n), BF16)] + x_shape, scratch_shapes=x_scratch,
        compiler_params=_params(("arbitrary",), x_id),
    )(*pieces, b, *x_arrs)
    return res[0] if exchange is None else (res[0], res[1:])


NO_EXCHANGE = ([], [], [], None, None)


def _sweep_marks(nt):
    i = pl.program_id(0)
    return i == 0, i == nt - 1, i == nt - 1, i == nt // 2


def _rowwise(name, fn, rows, bcasts, row_outs, red_outs=(), tm=256, exchange=NO_EXCHANGE):
    s = rows[0].shape[0]
    tm = _pick(s, tm, 16)
    nt = s // tm
    resident = pl.Buffered(1)
    nr, nb, no, nd = len(rows), len(bcasts), len(row_outs), len(red_outs)
    x_arrs, x_shape, x_scratch, _, x_id = exchange
    nx = len(x_arrs)
    first_out = nr + nb + nx

    def body(*refs):
        finish_exchange = _carry_exchange(exchange, refs, nr + nb, no + nd, *_sweep_marks(nt))
        ins = [r[...] for r in refs[:nr + nb]]
        outs, reds = fn(*ins)
        for ref, val in zip(refs[first_out:first_out + no], outs):
            ref[...] = val.astype(ref.dtype)
        i = pl.program_id(0)
        for ref, val in zip(refs[first_out + no:first_out + no + nd], reds):
            @pl.when(i == 0)
            def _():
                ref[...] = val

            @pl.when(i > 0)
            def _():
                ref[...] += val
        finish_exchange()

    def row_spec(a):
        assert a.shape[-2] % nt == 0, (name, a.shape, nt)
        if len(a.shape) == 3:
            return pl.BlockSpec((a.shape[0], a.shape[1] // nt, a.shape[2]), lambda i: (0, i, 0))
        return pl.BlockSpec((a.shape[0] // nt, a.shape[1]), lambda i: (i, 0))

    in_specs = [row_spec(r) for r in rows]
    in_specs += [pl.BlockSpec(b.shape, lambda i: (0, 0), pipeline_mode=resident) for b in bcasts]
    out_specs = [row_spec(o) for o in row_outs]
    out_specs += [pl.BlockSpec(d.shape, lambda i: (0, 0)) for d in red_outs]
    return pl.pallas_call(
        body, name=name, grid=(nt,), in_specs=in_specs + [ANY] * nx, out_specs=out_specs + [ANY] * nx,
        out_shape=list(row_outs) + list(red_outs) + x_shape, scratch_shapes=x_scratch,
        compiler_params=_params(("arbitrary",), x_id),
    )(*rows, *bcasts, *x_arrs)


def _sds(shape, dtype=F32):
    return jax.ShapeDtypeStruct(shape, dtype)


def _rms(x, g):
    y = x * lax.rsqrt(jnp.mean(x * x, axis=-1, keepdims=True) + EPS)
    return y * g


def _silu(x):
    return x * jax.nn.sigmoid(x)


def _swiglu(g, u):
    return _silu(g) * u


def _ln_silu(u, g, b):
    mu = jnp.mean(u, axis=-1, keepdims=True)
    var = jnp.mean(jnp.square(u - mu), axis=-1, keepdims=True)
    return _silu((u - mu) * lax.rsqrt(var + EPS) * g + b)


def _merge(conv_pre, att_out, g_conv, g_att, b_cb):
    return jax.nn.sigmoid(g_conv) * (conv_pre + b_cb) + jax.nn.sigmoid(g_att) * att_out


def _glu(t):
    return t[:, :CONV_DIM] * jax.nn.sigmoid(t[:, CONV_DIM:])


def _shifted_reader(buf, shifted, tm):
    for b in range(1, SUBLANES):
        shifted[b - 1, :, :] = buf[pl.ds(b, tm + HALO - SUBLANES), :]

    def read(o):
        a, b = divmod(o, SUBLANES)
        return buf[pl.ds(SUBLANES * a, tm), :] if b == 0 else shifted[b - 1, pl.ds(SUBLANES * a, tm), :]

    return read


def _conv_fwd(conv_in, w_pad, b, ln_g, ln_b, exchange, tm=256):
    s = conv_in.shape[0]
    tm = _pick(s, tm, HALO)
    ratio = tm // HALO
    x_arrs, x_shape, x_scratch, _, x_id = exchange
    nx = len(x_arrs)

    def body(*refs):
        main_ref, halo_ref, w_ref, b_ref, g_ref, be_ref = refs[:6]
        u3_ref, u1_ref = refs[6 + nx:8 + nx]
        buf, shifted = refs[-2:]
        finish_exchange = _carry_exchange(exchange, refs, 6, 2, *_sweep_marks(s // tm))
        i = pl.program_id(0)
        buf[0:HALO, :] = _glu(halo_ref[...]) * (i > 0).astype(F32)
        buf[HALO:HALO + tm, :] = _glu(main_ref[...])
        read = _shifted_reader(buf, shifted, tm)
        acc = jnp.zeros((tm, CONV_DIM), F32) + b_ref[...]
        for j in range(CONV_WIDTH):
            acc = acc + w_ref[j:j + 1, :] * read(HALO - (CONV_WIDTH - 1) + j)
        u1_ref[...] = acc
        u3_ref[...] = _ln_silu(acc, g_ref[...], be_ref[...]).astype(u3_ref.dtype)
        finish_exchange()

    res = pl.pallas_call(
        body, name="conv_fwd", grid=(s // tm,),
        in_specs=[pl.BlockSpec((tm, 2 * CONV_DIM), lambda i: (i, 0)),
                  pl.BlockSpec((HALO, 2 * CONV_DIM), lambda i: (jnp.maximum(i * ratio - 1, 0), 0)),
                  pl.BlockSpec(w_pad.shape, lambda i: (0, 0)),
                  pl.BlockSpec(b.shape, lambda i: (0, 0)),
                  pl.BlockSpec(ln_g.shape, lambda i: (0, 0)),
                  pl.BlockSpec(ln_b.shape, lambda i: (0, 0))] + [ANY] * nx,
        out_specs=[pl.BlockSpec((tm, CONV_DIM), lambda i: (i, 0)),
                   pl.BlockSpec((tm, CONV_DIM), lambda i: (i, 0))] + [ANY] * nx,
        out_shape=[_sds((s, CONV_DIM), BF16), _sds((s, CONV_DIM), F32)] + x_shape,
        scratch_shapes=x_scratch + [pltpu.VMEM((tm + HALO, CONV_DIM), F32),
                                    pltpu.VMEM((SUBLANES - 1, tm + HALO - SUBLANES, CONV_DIM), F32)],
        compiler_params=_params(("arbitrary",), x_id),
    )(conv_in, conv_in, w_pad, b, ln_g, ln_b, *x_arrs)
    return res[0], res[1], res[2:]


def _conv_bwd(conv_in, u1, du3, ln_g, ln_b, w_pad, exchange, tm=256):
    s = conv_in.shape[0]
    tm = _pick(s, tm, HALO)
    ratio = tm // HALO
    nt = s // tm
    last_halo = s // HALO - 1
    x_arrs, x_shape, x_scratch, _, x_id = exchange
    nx = len(x_arrs)

    def body(*refs):
        main_ref, halo_ref, u1_ref, u1n_ref, du3_ref, du3n_ref, g_ref, be_ref, w_ref = refs[:9]
        dci_ref, dw_ref, db_ref, dg_ref, dbe_ref = refs[9 + nx:14 + nx]
        ubuf, dbuf, ushift, dshift = refs[-4:]
        finish_exchange = _carry_exchange(exchange, refs, 9, 5, *_sweep_marks(nt))
        i = pl.program_id(0)
        main = main_ref[...]
        a = main[:, :CONV_DIM]
        sb = jax.nn.sigmoid(main[:, CONV_DIM:])
        ubuf[0:HALO, :] = _glu(halo_ref[...]) * (i > 0).astype(F32)
        ubuf[HALO:HALO + tm, :] = a * sb

        def ln_bwd(u1t, du3t):
            _, vjp = jax.vjp(_ln_silu, u1t, g_ref[...], be_ref[...])
            return vjp(du3t)

        du, dg, dbe = ln_bwd(u1_ref[...], du3_ref[...])
        dbuf[0:tm, :] = du
        dbuf[tm:tm + HALO, :] = ln_bwd(u1n_ref[...], du3n_ref[...])[0] * (i < nt - 1).astype(F32)

        @pl.when(i == 0)
        def _():
            dw_ref[...] = jnp.zeros_like(dw_ref)
            db_ref[...] = jnp.zeros_like(db_ref)
            dg_ref[...] = jnp.zeros_like(dg_ref)
            dbe_ref[...] = jnp.zeros_like(dbe_ref)

        dg_ref[...] += dg
        dbe_ref[...] += dbe

        read_u = _shifted_reader(ubuf, ushift, tm)
        read_d = _shifted_reader(dbuf, dshift, tm)
        du0 = jnp.zeros((tm, CONV_DIM), F32)
        for j in range(CONV_WIDTH):
            du0 = du0 + w_ref[j:j + 1, :] * read_d(CONV_WIDTH - 1 - j)
            dw_ref[j:j + 1, :] += jnp.sum(du * read_u(HALO - (CONV_WIDTH - 1) + j), axis=0, keepdims=True)
        db_ref[...] += jnp.sum(du, axis=0, keepdims=True)
        dci_ref[:, :CONV_DIM] = (du0 * sb).astype(dci_ref.dtype)
        dci_ref[:, CONV_DIM:] = (du0 * a * sb * (1.0 - sb)).astype(dci_ref.dtype)
        finish_exchange()

    res = pl.pallas_call(
        body, name="conv_bwd", grid=(nt,),
        in_specs=[pl.BlockSpec((tm, 2 * CONV_DIM), lambda i: (i, 0)),
                  pl.BlockSpec((HALO, 2 * CONV_DIM), lambda i: (jnp.maximum(i * ratio - 1, 0), 0))]
        + [pl.BlockSpec((tm, CONV_DIM), lambda i: (i, 0)),
           pl.BlockSpec((HALO, CONV_DIM), lambda i: (jnp.minimum((i + 1) * ratio, last_halo), 0))] * 2
        + [pl.BlockSpec((1, CONV_DIM), lambda i: (0, 0))] * 2 + [pl.BlockSpec(w_pad.shape, lambda i: (0, 0))]
        + [ANY] * nx,
        out_specs=[pl.BlockSpec((tm, 2 * CONV_DIM), lambda i: (i, 0)),
                   pl.BlockSpec(w_pad.shape, lambda i: (0, 0))]
        + [pl.BlockSpec((1, CONV_DIM), lambda i: (0, 0))] * 3 + [ANY] * nx,
        out_shape=[_sds((s, 2 * CONV_DIM), BF16), _sds(w_pad.shape)] + [_sds((1, CONV_DIM))] * 3 + x_shape,
        scratch_shapes=x_scratch + [pltpu.VMEM((tm + HALO, CONV_DIM), F32)] * 2
        + [pltpu.VMEM((SUBLANES - 1, tm + HALO - SUBLANES, CONV_DIM), F32)] * 2,
        compiler_params=_params(("arbitrary",), x_id),
    )(conv_in, conv_in, u1, u1, du3, du3, ln_g, ln_b, w_pad, *x_arrs)
    return res[:5], res[5:]


def _logsig_neg(z):
    return jnp.minimum(-z, 0.0) - jnp.log(1.0 + jnp.exp(-jnp.abs(z)))


def _split_dot(val, tri):
    hi = val.astype(BF16)
    lo = (val - hi.astype(F32)).astype(BF16)
    return jnp.dot(hi, tri, preferred_element_type=F32) + jnp.dot(lo, tri, preferred_element_type=F32)


def _attn_masks(t, later):
    row = lax.broadcasted_iota(jnp.int32, (t, t), 0)
    col = lax.broadcasted_iota(jnp.int32, (t, t), 1)
    tri = jnp.where(row > col if later else row <= col, 1.0, 0.0).astype(BF16)
    return col < row, tri


def _grid_marks(h, nq):
    hh, i = pl.program_id(0), pl.program_id(1)
    return ((hh == 0) & (i == 0), (hh == h - 1) & (i == nq // 2), (hh == h - 1) & (i == nq - 1),
            (hh == h // 2) & (i == 0))


def _head_masks(shape):
    lane = lax.broadcasted_iota(jnp.int32, shape, len(shape) - 1)
    return lane < HEAD_DIM, lane >= HEAD_DIM


def _per_head(blk):
    m0, m1 = _head_masks(blk.shape)
    zero = jnp.zeros_like(blk)
    return jnp.where(m0, blk, zero), jnp.where(m1, blk, zero)


NT = (((1,), (1,)), ((), ()))
TN = (((0,), (0,)), ((), ()))


def _with_top(whole, top):
    rows = top.shape[0]
    return top if rows == whole.shape[0] else jnp.concatenate([top, whole[rows:]], axis=0)


def _attn_fwd(q, k, v, exchange):
    s = q.shape[0]
    hp = q.shape[1] // LANES
    t = ATT_TILE
    scale = 1.0 / math.sqrt(HEAD_DIM)
    x_arrs, x_shape, x_scratch, _, x_id = exchange
    nx = len(x_arrs)

    def body(*refs):
        q_ref, k_ref, v_ref = refs[:3]
        o_ref, lt_ref, nb_ref = refs[3 + nx:6 + nx]
        finish_exchange = _carry_exchange(exchange, refs, 3, 3, *_grid_marks(hp, s // t))
        i = pl.program_id(1)
        qs = _per_head((q_ref[...].astype(F32) * scale).astype(BF16))
        causal, tri = _attn_masks(t, later=True)

        def step(kb, carry, masked, rows):
            cs, acc = carry
            off = pl.multiple_of(kb * t, t)
            kblk = k_ref[pl.ds(off, t), :]
            vs = _per_head(v_ref[pl.ds(off, t), :])
            acc_top = acc[:rows]
            new_cs = []
            for hd in range(2):
                z = lax.dot_general(qs[hd][:rows], kblk, NT, preferred_element_type=F32)
                l = _logsig_neg(z)
                if masked:
                    l = jnp.where(causal, l, 0.0)
                e = z + l + _split_dot(l, tri) + cs[hd][:rows]
                if masked:
                    e = jnp.where(causal, e, -1e30)
                acc_top = acc_top + jnp.dot(jnp.exp(e).astype(BF16), vs[hd], preferred_element_type=F32)
                new_cs.append(_with_top(cs[hd], cs[hd][:rows] + jnp.sum(l, axis=1, keepdims=True)))
            return tuple(new_cs), _with_top(acc, acc_top)

        zero = jnp.zeros((t, 1), F32)
        carry = step(i, ((zero, zero), jnp.zeros((t, LANES), F32)), True, t)

        def live(cs, lo, hi):
            return jnp.maximum(jnp.max(cs[0][lo:hi]), jnp.max(cs[1][lo:hi])) > DEAD_SUM

        def more(state):
            n, _, (cs, _) = state
            return (n < i) & live(cs, 0, t)

        def sweep(state):
            n, n_full, cr = state
            whole = live(cr[0], ATT_PART, t)
            cr = lax.cond(whole, lambda c: step(i - 1 - n, c, False, t), lambda c: step(i - 1 - n, c, False, ATT_PART), cr)
            return n + 1, n_full + whole.astype(jnp.int32), cr

        n_blocks, n_full, carry = lax.while_loop(more, sweep, (jnp.int32(0), jnp.int32(0), carry))
        m0, _ = _head_masks((t, LANES))
        lt_ref[...] = jnp.where(m0, carry[0][0], carry[0][1])
        o_ref[...] = carry[1].astype(o_ref.dtype)
        nb_ref[0, pl.program_id(0), i] = n_blocks.astype(F32)
        nb_ref[1, pl.program_id(0), i] = n_full.astype(F32)
        finish_exchange()

    res = pl.pallas_call(
        body, name="attn_fwd", grid=(hp, s // t),
        in_specs=[pl.BlockSpec((t, LANES), lambda p, i: (i, p)),
                  pl.BlockSpec((s, LANES), lambda p, i: (0, p)),
                  pl.BlockSpec((s, LANES), lambda p, i: (0, p))] + [ANY] * nx,
        out_specs=[pl.BlockSpec((t, LANES), lambda p, i: (i, p)),
                   pl.BlockSpec((None, t, LANES), lambda p, i: (p, i, 0)),
                   pl.BlockSpec(memory_space=pltpu.SMEM)] + [ANY] * nx,
        out_shape=[_sds(q.shape, BF16), _sds((hp, s, LANES), F32), _sds((2, hp, s // t), F32)] + x_shape,
        scratch_shapes=x_scratch,
        compiler_params=_params(("arbitrary", "arbitrary"), x_id),
    )(q, k, v, *x_arrs)
    return res[0], res[1], res[2], res[3:]


def _attn_bwd(q, k, v, do, ltot, n_blocks, exchange):
    s = q.shape[0]
    hp = q.shape[1] // LANES
    t = ATT_TILE
    scale = 1.0 / math.sqrt(HEAD_DIM)
    x_arrs, x_shape, x_scratch, _, x_id = exchange
    nx = len(x_arrs)

    def body(*refs):
        q_ref, k_ref, v_ref, do_ref, lt_ref, nb_ref = refs[:6]
        dq_ref, dk_ref, dv_ref = refs[6 + nx:9 + nx]
        finish_exchange = _carry_exchange(exchange, refs, 6, 3, *_grid_marks(hp, s // t))
        i = pl.program_id(1)
        n_blocks = jnp.clip(nb_ref[0, pl.program_id(0), i].astype(jnp.int32), 0, i)
        n_full = jnp.clip(nb_ref[1, pl.program_id(0), i].astype(jnp.int32), 0, n_blocks)

        @pl.when(i == 0)
        def _():
            dk_ref[...] = jnp.zeros_like(dk_ref)
            dv_ref[...] = jnp.zeros_like(dv_ref)

        qb = q_ref[...]
        qm = _per_head(qb)
        qs = _per_head((qb.astype(F32) * scale).astype(BF16))
        dos = _per_head(do_ref[...])
        lts = (lt_ref[:, 0:1], lt_ref[:, HEAD_DIM:HEAD_DIM + 1])
        causal, tri = _attn_masks(t, later=False)

        def step(kb, carry, masked, rows):
            cls, cgs, dq = carry
            off = pl.multiple_of(kb * t, t)
            kblk = k_ref[pl.ds(off, t), :]
            vblk = v_ref[pl.ds(off, t), :]
            ks = _per_head(kblk)
            dq_top = dq[:rows]
            dk = jnp.zeros((t, LANES), F32)
            dv = jnp.zeros((t, LANES), F32)
            new_cls, new_cgs = [], []
            for hd in range(2):
                z = lax.dot_general(qs[hd][:rows], kblk, NT, preferred_element_type=F32)
                l = _logsig_neg(z)
                if masked:
                    l = jnp.where(causal, l, 0.0)
                e = z + l + ((lts[hd][:rows] - cls[hd][:rows]) - _split_dot(l, tri))
                if masked:
                    e = jnp.where(causal, e, -1e30)
                a = jnp.exp(e)
                g = lax.dot_general(dos[hd][:rows], vblk, NT, preferred_element_type=F32) * a
                p = cgs[hd][:rows] + jnp.dot(g.astype(BF16), tri, preferred_element_type=F32) - g
                el = jnp.exp(l)
                dz = g * el - p * (1.0 - el)
                if masked:
                    dz = jnp.where(causal, dz, 0.0)
                dzb = (dz * scale).astype(BF16)
                dq_top = dq_top + jnp.dot(dzb, ks[hd], preferred_element_type=F32)
                dk = dk + lax.dot_general(dzb, qm[hd][:rows], TN, preferred_element_type=F32)
                dv = dv + lax.dot_general(a.astype(BF16), dos[hd][:rows], TN, preferred_element_type=F32)
                new_cls.append(_with_top(cls[hd], cls[hd][:rows] + jnp.sum(l, axis=1, keepdims=True)))
                new_cgs.append(_with_top(cgs[hd], cgs[hd][:rows] + jnp.sum(g, axis=1, keepdims=True)))
            dk_ref[pl.ds(off, t), :] += dk
            dv_ref[pl.ds(off, t), :] += dv
            return tuple(new_cls), tuple(new_cgs), _with_top(dq, dq_top)

        zero = jnp.zeros((t, 1), F32)
        init = ((zero, zero), (zero, zero), jnp.zeros((t, LANES), F32))
        carry = lax.fori_loop(i - n_blocks, i - n_full, lambda kb, cr: step(kb, cr, False, ATT_PART), init)
        carry = lax.fori_loop(i - n_full, i, lambda kb, cr: step(kb, cr, False, t), carry)
        carry = step(i, carry, True, t)
        dq_ref[...] = carry[2]
        finish_exchange()

    blk = pl.BlockSpec((t, LANES), lambda p, i: (i, p))
    whole = pl.BlockSpec((s, LANES), lambda p, i: (0, p))
    res = pl.pallas_call(
        body, name="attn_bwd", grid=(hp, s // t),
        in_specs=[blk, whole, whole, blk, pl.BlockSpec((None, t, LANES), lambda p, i: (p, i, 0)),
                  pl.BlockSpec(memory_space=pltpu.SMEM)] + [ANY] * nx,
        out_specs=[blk, whole, whole] + [ANY] * nx,
        out_shape=[_sds(q.shape)] * 3 + x_shape,
        scratch_shapes=x_scratch,
        compiler_params=_params(("arbitrary", "arbitrary"), x_id),
    )(q, k, v, do, ltot, n_blocks, *x_arrs)
    return res[0], res[1], res[2], res[3:]


LATE = ["w_conv_branch", "w_att_branch", "w_out", "w_ffn_up", "w_ffn_down"]


def _full_weight(name, gathered):
    return _cols_to_full(gathered) if name in COL_SHARDED else gathered.reshape(-1, gathered.shape[2])


def _grad_slabs(name, grad):
    return _full_to_cols(grad) if name in COL_SHARDED else grad.reshape(N_DEV, -1, grad.shape[1])


def _side_slabs(name, grad):
    slabs = _grad_slabs(name, grad)
    return slabs.reshape((4, 2) + slabs.shape[1:])


def _local_step(x, target, w, late_blocks, opt):
    s = x.shape[0]
    w = dict(w)
    g1, g2, g3, g4 = w["norm_mix_pre"], w["norm_mix_post"], w["norm_ffn_pre"], w["norm_ffn_post"]

    w_in = w["w_in"]

    def proj_fn(xt, g1_, w_in_t):
        h = _rms(xt, g1_).astype(BF16)
        proj = lax.dot_general(h, w_in_t, NT, preferred_element_type=F32)
        return (h, *[proj[:, IN_SPLITS[n]:IN_SPLITS[n + 1]] for n in range(6)]), ()

    mix_weights = ["w_conv_branch", "w_att_branch", "w_out"]
    h1, conv_in, q, k, v, g_conv, g_att, g_out = _rowwise(
        "norm_proj", proj_fn, [x], [g1, w_in],
        [_sds((s, D_MODEL), BF16), _sds((s, 2 * CONV_DIM)), _sds((s, ATT_DIM), BF16), _sds((s, ATT_DIM), BF16),
         _sds((s, ATT_DIM), BF16), _sds((s, D_MODEL), BF16), _sds((s, D_MODEL), BF16)], tm=512,
        exchange=_relay_gather_exchange([late_blocks["w_out"]]))

    u3, u1, g_branches = _conv_fwd(conv_in, w["conv_dw_w"], w["conv_dw_b"], w["conv_ln_g"], w["conv_ln_b"],
                                   _relay_gather_exchange([late_blocks[nm] for nm in mix_weights[:2]]))
    for nm, g in zip(mix_weights, [*g_branches, g_out]):
        w[nm] = _full_weight(nm, g)
    att, ltot, n_blocks, (g_up,) = _attn_fwd(q, k, v, _gather_exchange([late_blocks["w_ffn_up"]]))
    w["w_ffn_up"] = _full_weight("w_ffn_up", g_up)

    def merge_fn(u3t, at, gc, ga, xt, w_cb, w_ab, b_cb, w_out, g2_, g3_):
        cp = jnp.dot(u3t, w_cb, preferred_element_type=F32)
        ao = jnp.dot(at, w_ab, preferred_element_type=F32)
        mg = _merge(cp, ao, gc.astype(F32), ga.astype(F32), b_cb).astype(BF16)
        mix_ = jnp.dot(mg, w_out, preferred_element_type=F32)
        x2_ = xt + _rms(mix_, g2_)
        return (mg, cp, ao, mix_, x2_, _rms(x2_, g3_)), ()

    half = D_MODEL // 2
    down_block = late_blocks["w_ffn_down"]
    merged, conv_pre, att_out, mix, x2, h2, g_left = _rowwise(
        "branch_merge_mix", merge_fn, [u3, att, g_conv, g_att, x],
        [w["w_conv_branch"], w["w_att_branch"], w["b_conv_branch"], w["w_out"], g2, g3],
        [_sds((s, D_MODEL), BF16)] * 3 + [_sds((s, D_MODEL)), _sds((s, D_MODEL)), _sds((s, D_MODEL), BF16)], tm=512,
        exchange=_relay_gather_exchange([down_block[:, :half]]))

    def ffn_up_fn(ht, w_up_t):
        gu_ = lax.dot_general(ht, w_up_t, NT, preferred_element_type=F32)
        return (gu_, _swiglu(gu_[:, :D_FF], gu_[:, D_FF:])), ()

    gu, act, g_right = _rowwise("ffn_up", ffn_up_fn, [h2], [w["w_ffn_up"]],
                                [_sds((s, 2 * D_FF), BF16), _sds((s, D_FF), BF16)], tm=512,
                                exchange=_relay_gather_exchange([down_block[:, half:]]))
    w_down = [_full_weight("w_ffn_down", g) for g in (g_left, g_right)]

    def final_fn(at, x2t, tgt, w_left, w_right, g4_):
        ff = jnp.concatenate([jnp.dot(at, w_left, preferred_element_type=F32),
                              jnp.dot(at, w_right, preferred_element_type=F32)], axis=1)
        n4, vjp = jax.vjp(_rms, ff, g4_)
        err = x2t + n4 - tgt
        dy = err * (1.0 / D_MODEL)
        dff, dg4 = vjp(dy)
        return (dy, dff), (jnp.sum(err * err, axis=0, keepdims=True), dg4)

    dy, dff, loss_cols, d_g4 = _rowwise("ffn_down_loss", final_fn, [act, x2, target], [*w_down, g4],
                                        [_sds((s, D_MODEL)), _sds((s, D_MODEL), BF16)],
                                        [_sds((1, D_MODEL)), _sds((1, D_MODEL))], tm=512)
    loss = 0.5 * jnp.sum(loss_cols) / D_MODEL

    d_w_down = _tn_matmul(act, dff, name="d_w_down")

    def act_bwd_fn(dfft, gut, w_left, w_right):
        d_act = (lax.dot_general(dfft[:, :half], w_left, NT, preferred_element_type=F32)
                 + lax.dot_general(dfft[:, half:], w_right, NT, preferred_element_type=F32))
        gu_ = gut.astype(F32)
        _, vjp = jax.vjp(_swiglu, gu_[:, :D_FF], gu_[:, D_FF:])
        return (jnp.concatenate(vjp(d_act), axis=1),), ()

    down_slabs = _side_slabs("w_ffn_down", d_w_down)
    dgu, theirs = _rowwise("ffn_act_bwd", act_bwd_fn, [dff, gu], w_down, [_sds((s, 2 * D_FF), BF16)],
                           exchange=_pair_exchange([down_slabs]))
    down_sums = _pair_sum("pair_sum_w_ffn_down", down_slabs, theirs)
    d_w_up = _tn_matmul(dgu, h2, name="d_w_up")
    received = {}
    up_slabs = _side_slabs("w_ffn_up", d_w_up)

    def mid_bwd_fn(dgut, xt, mt, dyt, w_up_t, g2_, g3_):
        dh = jnp.dot(dgut, w_up_t, preferred_element_type=F32)
        n2, vjp2 = jax.vjp(_rms, mt, g2_)
        x2_ = xt + n2
        _, vjp3 = jax.vjp(_rms, x2_, g3_)
        dx2_, dg3 = vjp3(dh)
        dx2_ = dx2_ + dyt
        dmix_, dg2 = vjp2(dx2_)
        return (dx2_, dmix_), (dg2, dg3)

    dx2, dmix, d_g2, d_g3, received["w_ffn_down"] = _rowwise(
        "ffn_up_mid_bwd", mid_bwd_fn, [dgu, x, mix, dy], [w["w_ffn_up"], g2, g3],
        [_sds((s, D_MODEL)), _sds((s, D_MODEL), BF16)], [_sds((1, D_MODEL)), _sds((1, D_MODEL))], tm=512,
        exchange=_chip_exchange([down_sums]))
    d_w_out = _tn_matmul(merged, dmix, name="d_w_out")

    def merge_bwd_fn(dmt, cp, ao, gc, ga, w_out, w_cb, w_ab, b_cb):
        dm = lax.dot_general(dmt, w_out, NT, preferred_element_type=F32)
        _, vjp = jax.vjp(_merge, cp.astype(F32), ao.astype(F32), gc.astype(F32), ga.astype(F32), b_cb)
        dcp, dao, dgc, dga, dbias = vjp(dm)
        dcp, dao = dcp.astype(BF16), dao.astype(BF16)
        du3_ = lax.dot_general(dcp, w_cb, NT, preferred_element_type=F32)
        datt_ = lax.dot_general(dao, w_ab, NT, preferred_element_type=F32)
        return (dcp, dao, dgc, dga, du3_, datt_), (dbias,)

    d_conv_out, d_att_out, d_g_conv, d_g_att, du3, d_att, d_b_cb, theirs = _rowwise(
        "merge_bwd", merge_bwd_fn, [dmix, conv_pre, att_out, g_conv, g_att],
        [w["w_out"], w["w_conv_branch"], w["w_att_branch"], w["b_conv_branch"]],
        [_sds((s, D_MODEL), BF16)] * 4 + [_sds((s, CONV_DIM)), _sds((s, ATT_DIM), BF16)], [_sds((1, D_MODEL))], tm=512,
        exchange=_pair_exchange([up_slabs]))

    d_w_cb = _tn_matmul(u3, d_conv_out, name="d_w_conv_branch")
    d_w_ab = _tn_matmul(att, d_att_out, name="d_w_att_branch")

    dq, dk, dv, (received["w_ffn_up"],) = _attn_bwd(
        q, k, v, d_att, ltot, n_blocks, _chip_exchange([_pair_sum("pair_sum_w_ffn_up", up_slabs, theirs)]))

    mix_grads = {"w_conv_branch": d_w_cb, "w_att_branch": d_w_ab, "w_out": d_w_out}
    (d_conv_in, d_dw_w, d_dw_b, d_ln_g, d_ln_b), landed = _conv_bwd(
        conv_in, u1, du3, w["conv_ln_g"], w["conv_ln_b"], w["conv_dw_w"],
        _scatter_exchange([_grad_slabs(nm, mix_grads[nm]) for nm in mix_weights[:2]]))
    received.update(zip(mix_weights[:2], landed))

    d_proj = [d_conv_in, dq, dk, dv, d_g_conv, d_g_att]
    d_w_in, (received["w_out"],) = _pieces_tn_matmul(
        d_proj, h1, name="d_w_in", exchange=_scatter_exchange([_grad_slabs("w_out", d_w_out)]))
    in_slabs = _side_slabs("w_in", d_w_in)
    (theirs,) = _exchange_call("pair_swap_w_in", _pair_exchange([in_slabs]))

    early = list(opt)

    def pre_bwd_fn(*args):
        groups, (xt, dx2t), jobs, (w_in_t, g_) = args[:6], args[6:8], args[8:-2], args[-2:]
        dh = sum(jnp.dot(grp.astype(BF16), w_in_t[IN_SPLITS[n]:IN_SPLITS[n + 1]], preferred_element_type=F32)
                 for n, grp in enumerate(groups))
        _, vjp = jax.vjp(_rms, xt, g_)
        dx_, dg_ = vjp(dh)
        updates = [_sum_adamw_tile(*jobs[4 * n:4 * n + 4]) for n in range(len(early))]
        return (dx_ + dx2t, *[u for four in updates for u in four]), (dg_,)

    res = _rowwise(
        "proj_norm_bwd", pre_bwd_fn,
        d_proj + [x, dx2] + [a for nm in early for a in (received[nm], *opt[nm])], [w_in, g1],
        [_sds((s, D_MODEL))] + [_sds(opt[nm][0].shape) for nm in early for _ in range(4)],
        [_sds((1, D_MODEL))], tm=512, exchange=_chip_exchange([_pair_sum("pair_sum_w_in", in_slabs, theirs)]))
    grad_x, d_g1, received["w_in"] = res[0], res[-2], res[-1]
    updated = {nm: res[1 + 4 * n:5 + 4 * n] for n, nm in enumerate(early)}

    grads = {
        "norm_mix_pre": d_g1, "conv_dw_w": d_dw_w, "conv_dw_b": d_dw_b,
        "conv_ln_g": d_ln_g, "conv_ln_b": d_ln_b, "b_conv_branch": d_b_cb,
        "norm_mix_post": d_g2, "norm_ffn_pre": d_g3, "norm_ffn_post": d_g4,
    }
    return loss, grad_x, received, updated, grads


def _place():
    x, y, c = lax.axis_index("x"), lax.axis_index("y"), lax.axis_index("c")
    return x, y, c


def _slot(px, py, pc):
    return 4 * px + 2 * py + pc


def _exchange_scratch(n):
    return [pltpu.SemaphoreType.DMA((7 * n,)), pltpu.SemaphoreType.DMA((7 * n,)), pltpu.SemaphoreType.DMA((n,))]


GATHER_ID, SCATTER_ID, PAIR_ID, CHIP_ID, RELAY_ID = 0, 1, 2, 3, 4


def _handshake(peers):
    barrier = pltpu.get_barrier_semaphore()
    for peer in peers:
        pl.semaphore_signal(barrier, inc=1, device_id=peer, device_id_type=MESH)
    pl.semaphore_wait(barrier, len(peers))


def _gather_exchange(arrs):
    n = len(arrs)

    def phases(ins, outs, send_sems, recv_sems, local_sems):
        x, y, c = _place()
        me, sibling = (x, y, c), (x, y, 1 - c)
        chips = [(1 - x, y), (x, 1 - y), (1 - x, 1 - y)]

        def copy(a, kk, block, to, src=None):
            dst = outs[a].at[_slot(*block)]
            return pltpu.make_async_remote_copy(
                src_ref=dst if src is None else src, dst_ref=dst,
                send_sem=send_sems.at[a * 7 + kk], recv_sem=recv_sems.at[a * 7 + kk],
                device_id=to, device_id_type=MESH)

        mine = [pltpu.make_async_copy(ins[a], outs[a].at[_slot(*me)], local_sems.at[a]) for a in range(n)]
        first = []
        for a in range(n):
            first.append(copy(a, 0, me, sibling, src=ins[a]))
            first += [copy(a, 1 + j, me, (*chip, c), src=ins[a]) for j, chip in enumerate(chips)]
        passed = [copy(a, 4 + j, (*chip, c), sibling) for j, chip in enumerate(chips) for a in range(n)]

        def send():
            _handshake([sibling] + [(*chip, c) for chip in chips])
            for cp in mine + first:
                cp.start()

        def pass_on():
            for j, chip in enumerate(chips):
                for a in range(n):
                    copy(a, 1 + j, (*chip, c), me).wait_recv()
                    passed[j * n + a].start()

        def finish():
            for a in range(n):
                copy(a, 0, sibling, me).wait_recv()
                for j, chip in enumerate(chips):
                    copy(a, 4 + j, (*chip, 1 - c), me).wait_recv()
            for cp in first + passed:
                cp.wait_send()
            for cp in mine:
                cp.wait()

        return [send, pass_on, finish]

    return list(arrs), [_sds((N_DEV,) + a.shape, a.dtype) for a in arrs], _exchange_scratch(n), phases, GATHER_ID


def _relay_gather_exchange(arrs):
    n = len(arrs)
    per = 8

    def phases(ins, outs, send_sems, recv_sems, local_sems):
        x, y, c = _place()
        me, sibling = (x, y, c), (x, y, 1 - c)
        beside, below, across = (1 - x, y, c), (x, 1 - y, c), (1 - x, 1 - y, c)

        def copy(a, kk, block, to, src=None, rows=None):
            where = _slot(*block) if rows is None else (_slot(*block), rows)
            dst = outs[a].at[where]
            return pltpu.make_async_remote_copy(
                src_ref=dst if src is None else src, dst_ref=dst,
                send_sem=send_sems.at[a * per + kk], recv_sem=recv_sems.at[a * per + kk],
                device_id=to, device_id_type=MESH)

        def halves(a):
            h = ins[a].shape[0] // 2
            return pl.ds(0, h), pl.ds(h, ins[a].shape[0] - h)

        mine = [pltpu.make_async_copy(ins[a], outs[a].at[_slot(*me)], local_sems.at[a]) for a in range(n)]
        first = [copy(a, kk, me, to, src=ins[a]) for a in range(n) for kk, to in enumerate([sibling, beside, below])]
        relayed = [[copy(a, 3, beside, sibling), copy(a, 5, beside, below, rows=halves(a)[0])] for a in range(n)]
        relayed += [[copy(a, 4, below, sibling), copy(a, 6, below, beside, rows=halves(a)[1])] for a in range(n)]
        passed = [copy(a, 7, across, sibling) for a in range(n)]

        def send():
            _handshake([sibling, beside, below])
            for cp in mine + first:
                cp.start()

        def relay():
            for kk, block in ((1, beside), (2, below)):
                for a in range(n):
                    copy(a, kk, block, me).wait_recv()
                    for cp in relayed[(kk - 1) * n + a]:
                        cp.start()

        def pass_on():
            for a in range(n):
                copy(a, 5, across, me, rows=halves(a)[0]).wait_recv()
                copy(a, 6, across, me, rows=halves(a)[1]).wait_recv()
                passed[a].start()

        def finish():
            for a in range(n):
                for kk, block in ((0, me), (3, beside), (4, below), (7, across)):
                    copy(a, kk, (*block[:2], 1 - c), me).wait_recv()
            for cp in first + [cp for two in relayed for cp in two] + passed:
                cp.wait_send()
            for cp in mine:
                cp.wait()

        return [send, relay, pass_on, finish]

    scratch = [pltpu.SemaphoreType.DMA((per * n,)), pltpu.SemaphoreType.DMA((per * n,)), pltpu.SemaphoreType.DMA((n,))]
    return list(arrs), [_sds((N_DEV,) + a.shape, a.dtype) for a in arrs], scratch, phases, RELAY_ID


def _scatter_exchange(arrs):
    n = len(arrs)
    flips = [(fx, fy, fc) for fx in (0, 1) for fy in (0, 1) for fc in (0, 1)][1:]

    def phases(ins, outs, send_sems, recv_sems, local_sems):
        x, y, c = _place()
        mine = _slot(x, y, c)
        local = [pltpu.make_async_copy(ins[a].at[mine], outs[a].at[mine], local_sems.at[a]) for a in range(n)]
        peers = [((1 - x) if fx else x, (1 - y) if fy else y, (1 - c) if fc else c) for fx, fy, fc in flips]

        def copy(a, kk, src_slot, dst_slot):
            return pltpu.make_async_remote_copy(
                src_ref=ins[a].at[src_slot], dst_ref=outs[a].at[dst_slot],
                send_sem=send_sems.at[a * 7 + kk], recv_sem=recv_sems.at[a * 7 + kk],
                device_id=peers[kk], device_id_type=MESH)

        sends = [copy(a, kk, _slot(*peers[kk]), mine) for a in range(n) for kk in range(7)]

        def send():
            _handshake(peers)
            for cp in local + sends:
                cp.start()

        def finish():
            for a in range(n):
                for kk in range(7):
                    copy(a, kk, mine, _slot(*peers[kk])).wait_recv()
            for cp in sends:
                cp.wait_send()
            for cp in local:
                cp.wait()

        return [send, finish]

    return list(arrs), [_sds(a.shape, a.dtype) for a in arrs], _exchange_scratch(n), phases, SCATTER_ID


def _pair_exchange(arrs):
    n = len(arrs)

    def phases(ins, outs, send_sems, recv_sems, local_sems):
        x, y, c = _place()

        def copy(a, chip, side):
            return pltpu.make_async_remote_copy(
                src_ref=ins[a].at[chip, side], dst_ref=outs[a].at[chip],
                send_sem=send_sems.at[a * 7 + chip], recv_sem=recv_sems.at[a * 7 + chip],
                device_id=(x, y, 1 - c), device_id_type=MESH)

        sends = [copy(a, chip, 1 - c) for a in range(n) for chip in range(4)]

        def send():
            _handshake([(x, y, 1 - c)])
            for cp in sends:
                cp.start()

        def finish():
            for a in range(n):
                for chip in range(4):
                    copy(a, chip, c).wait_recv()
            for cp in sends:
                cp.wait_send()

        return [send, finish]

    return list(arrs), [_sds((4,) + a.shape[2:], a.dtype) for a in arrs], _exchange_scratch(n), phases, PAIR_ID


def _chip_exchange(arrs):
    n = len(arrs)

    def phases(ins, outs, send_sems, recv_sems, local_sems):
        x, y, c = _place()
        mine = 2 * x + y
        chips = [(1 - x, y), (x, 1 - y), (1 - x, 1 - y)]
        local = [pltpu.make_async_copy(ins[a].at[mine], outs[a].at[mine], local_sems.at[a]) for a in range(n)]

        def copy(a, j, src_slot, dst_slot):
            return pltpu.make_async_remote_copy(
                src_ref=ins[a].at[src_slot], dst_ref=outs[a].at[dst_slot],
                send_sem=send_sems.at[a * 7 + j], recv_sem=recv_sems.at[a * 7 + j],
                device_id=(*chips[j], c), device_id_type=MESH)

        sends = [copy(a, j, 2 * chips[j][0] + chips[j][1], mine) for a in range(n) for j in range(3)]

        def send():
            _handshake([(*chip, c) for chip in chips])
            for cp in local + sends:
                cp.start()

        def finish():
            for a in range(n):
                for j in range(3):
                    copy(a, j, mine, 2 * chips[j][0] + chips[j][1]).wait_recv()
            for cp in sends:
                cp.wait_send()
            for cp in local:
                cp.wait()

        return [send, finish]

    return list(arrs), [_sds(a.shape, a.dtype) for a in arrs], _exchange_scratch(n), phases, CHIP_ID


def _pair_sum(name, mine, theirs):
    _, _, r, c = mine.shape

    def body(side_ref, m_ref, t_ref, o_ref):
        o_ref[...] = (m_ref[...].astype(F32) + t_ref[...].astype(F32)).astype(o_ref.dtype)

    return pl.pallas_call(
        body, name=name,
        grid_spec=pltpu.PrefetchScalarGridSpec(
            num_scalar_prefetch=1, grid=(4,),
            in_specs=[pl.BlockSpec((None, None, r, c), lambda j, side: (j, side[0], 0, 0)),
                      pl.BlockSpec((None, r, c), lambda j, side: (j, 0, 0))],
            out_specs=pl.BlockSpec((None, r, c), lambda j, side: (j, 0, 0))),
        out_shape=_sds(theirs.shape, theirs.dtype),
        compiler_params=_params(("parallel",)),
    )(lax.axis_index("c").astype(jnp.int32).reshape(1), mine, theirs)


def _exchange_call(name, exchange):
    arrs, out_shape, scratch, phases, collective_id = exchange
    n = len(arrs)

    def body(*refs):
        for step in phases(refs[:n], refs[n:2 * n], *refs[2 * n:]):
            step()

    return pl.pallas_call(body, name=name, in_specs=[ANY] * n, out_specs=[ANY] * n,
                          out_shape=out_shape, scratch_shapes=scratch,
                          compiler_params=pltpu.CompilerParams(collective_id=collective_id))(*arrs)


def _carry_exchange(exchange, refs, n_in, n_out, first, middle, last, halfway):
    arrs, _, _, phases, _ = exchange
    n = len(arrs)
    if n == 0:
        return lambda: None
    ins = refs[n_in:n_in + n]
    outs = refs[n_in + n + n_out:n_in + 2 * n + n_out]
    sems = n_in + 2 * n + n_out
    steps = phases(ins, outs, *refs[sems:sems + 3])
    pl.when(first)(steps[0])
    if len(steps) == 4:
        pl.when(halfway)(steps[1])
    if len(steps) >= 3:
        pl.when(middle)(steps[-2])
    return lambda: pl.when(last)(steps[-1])


def _adamw_math(w, g, m, v):
    m2 = ADAM_B1 * m + (1.0 - ADAM_B1) * g
    v2 = ADAM_B2 * v + (1.0 - ADAM_B2) * jnp.square(g)
    m_hat = m2 / (1.0 - ADAM_B1 ** ADAM_STEP)
    v_hat = v2 / (1.0 - ADAM_B2 ** ADAM_STEP)
    delta = -ADAM_LR * (m_hat / (jnp.sqrt(v_hat) + ADAM_EPS) + ADAM_WD * w)
    return delta, m2, v2


def _sum_adamw_tile(parts, w, m, v):
    g = parts[0].astype(F32)
    for d in range(1, parts.shape[0]):
        g = g + parts[d].astype(F32)
    return (g, *_adamw_math(w, g, m, v))


def _sum_adamw(name, parts, w, m, v, tr=256):
    p, r, c = parts.shape
    tr = _pick(r, tr, 16)

    def body(p_ref, w_ref, m_ref, v_ref, g_ref, d_ref, m2_ref, v2_ref):
        g_ref[...], d_ref[...], m2_ref[...], v2_ref[...] = _sum_adamw_tile(p_ref[...], w_ref[...], m_ref[...], v_ref[...])

    tile = pl.BlockSpec((tr, c), lambda i: (i, 0))
    return pl.pallas_call(
        body, name=name, grid=(r // tr,),
        in_specs=[pl.BlockSpec((p, tr, c), lambda i: (0, i, 0)), tile, tile, tile],
        out_specs=[tile] * 4, out_shape=[_sds((r, c))] * 4,
        compiler_params=_params(("parallel",)),
    )(parts, w, m, v)


def _sum_parts(name, parts):
    p, r, c = parts.shape

    def body(p_ref, o_ref):
        g = p_ref[0]
        for d in range(1, p):
            g = g + p_ref[d]
        o_ref[...] = g

    return pl.pallas_call(
        body, name=name, out_shape=_sds((r, c)),
        in_specs=[pl.BlockSpec(memory_space=pltpu.VMEM)], out_specs=pl.BlockSpec(memory_space=pltpu.VMEM),
    )(parts)


WEIGHTS = ["norm_mix_pre", "w_in", "conv_dw_w", "conv_dw_b", "conv_ln_g", "conv_ln_b", "w_conv_branch",
           "b_conv_branch", "w_att_branch", "w_out", "norm_mix_post", "norm_ffn_pre", "w_ffn_up", "w_ffn_down",
           "norm_ffn_post"]
COL_SHARDED = ["w_conv_branch", "w_att_branch"]
TRANSPOSED = ["w_in", "w_ffn_up"]
VECTORS = ["norm_mix_pre", "conv_dw_b", "conv_ln_g", "conv_ln_b", "b_conv_branch", "norm_mix_post",
           "norm_ffn_pre", "norm_ffn_post"]


def _cols_to_full(g):
    return g.transpose(1, 0, 2).reshape(g.shape[1], N_DEV * g.shape[2])


def _full_to_cols(f):
    return f.reshape(f.shape[0], N_DEV, f.shape[1] // N_DEV).transpose(1, 0, 2)


PACK_ROWS = 7


def _pack_vectors(vecs, extra=None):
    parts = [vecs[nm].reshape(-1) for nm in VECTORS]
    parts.append(jnp.zeros((1,), F32) if extra is None else extra.reshape(1))
    used = sum(p.size for p in parts)
    parts.append(jnp.zeros((PACK_ROWS * D_MODEL - used,), F32))
    return jnp.concatenate(parts).reshape(PACK_ROWS, D_MODEL)


def _unpack_vectors(packed, sizes):
    flat, out, at = packed.reshape(-1), {}, 0
    for nm in VECTORS:
        out[nm] = flat[at:at + sizes[nm]]
        at += sizes[nm]
    return out, flat[at]


def kernel(x, norm_mix_pre, w_in, conv_dw_w, conv_dw_b, conv_ln_g, conv_ln_b, w_conv_branch, b_conv_branch, w_att_branch, w_out, norm_mix_post, norm_ffn_pre, w_ffn_up, w_ffn_down, norm_ffn_post, loss_target, m_norm_mix_pre, m_w_in, m_conv_dw_w, m_conv_dw_b, m_conv_ln_g, m_conv_ln_b, m_w_conv_branch, m_b_conv_branch, m_w_att_branch, m_w_out, m_norm_mix_post, m_norm_ffn_pre, m_w_ffn_up, m_w_ffn_down, m_norm_ffn_post, v_norm_mix_pre, v_w_in, v_conv_dw_w, v_conv_dw_b, v_conv_ln_g, v_conv_ln_b, v_w_conv_branch, v_b_conv_branch, v_w_att_branch, v_w_out, v_norm_mix_post, v_norm_ffn_pre, v_w_ffn_up, v_w_ffn_down, v_norm_ffn_post):
    ws = dict(zip(WEIGHTS, [norm_mix_pre, w_in, conv_dw_w, conv_dw_b, conv_ln_g, conv_ln_b, w_conv_branch,
                            b_conv_branch, w_att_branch, w_out, norm_mix_post, norm_ffn_pre, w_ffn_up, w_ffn_down,
                            norm_ffn_post]))
    ms = dict(zip(WEIGHTS, [m_norm_mix_pre, m_w_in, m_conv_dw_w, m_conv_dw_b, m_conv_ln_g, m_conv_ln_b,
                            m_w_conv_branch, m_b_conv_branch, m_w_att_branch, m_w_out, m_norm_mix_post,
                            m_norm_ffn_pre, m_w_ffn_up, m_w_ffn_down, m_norm_ffn_post]))
    vs = dict(zip(WEIGHTS, [v_norm_mix_pre, v_w_in, v_conv_dw_w, v_conv_dw_b, v_conv_ln_g, v_conv_ln_b,
                            v_w_conv_branch, v_b_conv_branch, v_w_att_branch, v_w_out, v_norm_mix_post,
                            v_norm_ffn_pre, v_w_ffn_up, v_w_ffn_down, v_norm_ffn_post]))

    dw_block = jnp.pad(conv_dw_w, ((0, 1), (0, 0)))
    g_in, g_dw = _exchange_call("gather_first", _relay_gather_exchange([w_in.T.astype(BF16), dw_block]))
    full = {"w_in": _full_weight("w_in", g_in), "conv_dw_w": _cols_to_full(g_dw)}
    for nm in VECTORS:
        full[nm] = ws[nm].reshape(1, -1)

    def as_kept(nm, a):
        return a.T if nm in TRANSPOSED else a

    ride_along = ["w_ffn_up", "w_out"]
    loss_local, grad_x, received, updated, grads = _local_step(
        x[0], loss_target[0], full, {nm: as_kept(nm, ws[nm]).astype(BF16) for nm in LATE},
        {nm: tuple(as_kept(nm, a[nm]) for a in (ws, ms, vs)) for nm in ride_along})

    small = _exchange_call("gather_small_grads", _gather_exchange(
        [_pack_vectors(grads, extra=loss_local), grads["conv_dw_w"]]))
    out_g, out_d, out_m, out_v = {}, {}, {}, {}
    for nm in LATE + ["w_in"]:
        res = updated[nm] if nm in updated else _sum_adamw(
            "adamw_" + nm, received[nm], *[as_kept(nm, a[nm]) for a in (ws, ms, vs)])
        out_g[nm], out_d[nm], out_m[nm], out_v[nm] = [as_kept(nm, r) for r in res]
    sizes = {nm: ws[nm].size for nm in VECTORS}
    vec = _sum_adamw("adamw_vectors", small[0], _pack_vectors(ws), _pack_vectors(ms), _pack_vectors(vs))
    for res, dst in zip(vec, (out_g, out_d, out_m, out_v)):
        dst.update(_unpack_vectors(res, sizes)[0])
    loss = _unpack_vectors(vec[0], sizes)[1]
    dw_full = _sum_parts("sum_dw_grads", small[1])
    me = _slot(*_place())
    dw_mine = lax.dynamic_slice(dw_full, (0, me * (CONV_DIM // N_DEV)), (CONV_WIDTH, CONV_DIM // N_DEV))
    nm = "conv_dw_w"
    out_g[nm], out_d[nm], out_m[nm], out_v[nm] = _sum_adamw("adamw_dw", dw_mine[None], ws[nm], ms[nm], vs[nm])

    outs = [loss, grad_x[None]]
    for group in (out_g, out_d, out_m, out_v):
        outs += [group[nm] for nm in WEIGHTS]
    return tuple(outs)
```

```python
import math

import jax
import jax.numpy as jnp
from jax import lax
from jax.experimental import pallas as pl
from jax.experimental.pallas import tpu as pltpu

F32 = jnp.float32
BF16 = jnp.bfloat16

N_DEV = 8
D_MODEL = 1024
CONV_DIM = 512
CONV_WIDTH = 31
N_HEADS = 8
HEAD_DIM = 64
ATT_DIM = N_HEADS * HEAD_DIM
D_FF = 2816
EPS = 1e-6
IN_SPLITS = (0, 1024, 1536, 2048, 2560, 3584, 4608)

ADAM_LR = 0.001
ADAM_B1 = 0.9
ADAM_B2 = 0.999
ADAM_EPS = 1e-08
ADAM_WD = 0.01
ADAM_STEP = 10

LANES = 128
SUBLANES = 8
HALO = 32
ATT_TILE = 256
ATT_PART = 176
DEAD_SUM = -120.0
VMEM_LIMIT = 56 * 1024 * 1024
MESH = pl.DeviceIdType.MESH
ANY = pl.BlockSpec(memory_space=pl.ANY)


def _pick(dim, target, align=LANES):
    t = min(dim, target)
    t -= t % align
    while t >= align:
        if dim % t == 0:
            return t
        t -= align
    return dim


def _params(semantics, collective_id=None):
    return pltpu.CompilerParams(dimension_semantics=semantics, vmem_limit_bytes=VMEM_LIMIT,
                                collective_id=collective_id)


def _tn_matmul(a, b, *, name):
    return _pieces_tn_matmul([a], b, name=name, tj=_pick(a.shape[1], 1408))


def _pieces_tn_matmul(pieces, b, *, name, tj=512, exchange=None):
    s, n = b.shape
    counts = [p.shape[1] // tj for p in pieces]
    starts = [sum(counts[:i]) for i in range(len(pieces))]
    assert all(p.shape == (s, c * tj) for p, c in zip(pieces, counts))
    x_arrs, x_shape, x_scratch, _, x_id = exchange or NO_EXCHANGE
    nx, n_in = len(x_arrs), len(pieces) + 1

    def body(*refs):
        b_ref, o_ref = refs[n_in - 1], refs[n_in + nx]
        finish_exchange = _carry_exchange(exchange or NO_EXCHANGE, refs, n_in, 1, *_sweep_marks(sum(counts)))
        j = pl.program_id(0)
        for p_ref, first, count in zip(refs, starts, counts):
            @pl.when((j >= first) & (j < first + count))
            def _():
                o_ref[...] = lax.dot_general(p_ref[...].astype(BF16), b_ref[...], TN,
                                             preferred_element_type=F32).astype(o_ref.dtype)
        finish_exchange()

    def piece_spec(first, count):
        return pl.BlockSpec((s, tj), lambda j: (0, jnp.clip(j - first, 0, count - 1)))

    res = pl.pallas_call(
        body, name=name, grid=(sum(counts),),
        in_specs=[piece_spec(f, c) for f, c in zip(starts, counts)]
        + [pl.BlockSpec((s, n), lambda j: (0, 0), pipeline_mode=pl.Buffered(1))] + [ANY] * nx,
        out_specs=[pl.BlockSpec((tj, n), lambda j: (j, 0))] + [ANY] * nx,
        out_shape=[jax.ShapeDtypeStruct((sum(counts) * tj, n), BF16)] + x_shape, scratch_shapes=x_scratch,
        compiler_params=_params(("arbitrary",), x_id),
    )(*pieces, b, *x_arrs)
    return res[0] if exchange is None else (res[0], res[1:])


NO_EXCHANGE = ([], [], [], None, None)


def _sweep_marks(nt):
    i = pl.program_id(0)
    return i == 0, i == nt - 1, i == nt - 1, i == nt // 2


def _rowwise(name, fn, rows, bcasts, row_outs, red_outs=(), tm=256, exchange=NO_EXCHANGE):
    s = rows[0].shape[0]
    tm = _pick(s, tm, 16)
    nt = s // tm
    resident = pl.Buffered(1)
    nr, nb, no, nd = len(rows), len(bcasts), len(row_outs), len(red_outs)
    x_arrs, x_shape, x_scratch, _, x_id = exchange
    nx = len(x_arrs)
    first_out = nr + nb + nx

    def body(*refs):
        finish_exchange = _carry_exchange(exchange, refs, nr + nb, no + nd, *_sweep_marks(nt))
        ins = [r[...] for r in refs[:nr + nb]]
        outs, reds = fn(*ins)
        for ref, val in zip(refs[first_out:first_out + no], outs):
            ref[...] = val.astype(ref.dtype)
        i = pl.program_id(0)
        for ref, val in zip(refs[first_out + no:first_out + no + nd], reds):
            @pl.when(i == 0)
            def _():
                ref[...] = val

            @pl.when(i > 0)
            def _():
                ref[...] += val
        finish_exchange()

    def row_spec(a):
        assert a.shape[-2] % nt == 0, (name, a.shape, nt)
        if len(a.shape) == 3:
            return pl.BlockSpec((a.shape[0], a.shape[1] // nt, a.shape[2]), lambda i: (0, i, 0))
        return pl.BlockSpec((a.shape[0] // nt, a.shape[1]), lambda i: (i, 0))

    in_specs = [row_spec(r) for r in rows]
    in_specs += [pl.BlockSpec(b.shape, lambda i: (0, 0), pipeline_mode=resident) for b in bcasts]
    out_specs = [row_spec(o) for o in row_outs]
    out_specs += [pl.BlockSpec(d.shape, lambda i: (0, 0)) for d in red_outs]
    return pl.pallas_call(
        body, name=name, grid=(nt,), in_specs=in_specs + [ANY] * nx, out_specs=out_specs + [ANY] * nx,
        out_shape=list(row_outs) + list(red_outs) + x_shape, scratch_shapes=x_scratch,
        compiler_params=_params(("arbitrary",), x_id),
    )(*rows, *bcasts, *x_arrs)


def _sds(shape, dtype=F32):
    return jax.ShapeDtypeStruct(shape, dtype)


def _rms(x, g):
    y = x * lax.rsqrt(jnp.mean(x * x, axis=-1, keepdims=True) + EPS)
    return y * g


def _silu(x):
    return x * jax.nn.sigmoid(x)


def _swiglu(g, u):
    return _silu(g) * u


def _ln_silu(u, g, b):
    mu = jnp.mean(u, axis=-1, keepdims=True)
    var = jnp.mean(jnp.square(u - mu), axis=-1, keepdims=True)
    return _silu((u - mu) * lax.rsqrt(var + EPS) * g + b)


def _merge(conv_pre, att_out, g_conv, g_att, b_cb):
    return jax.nn.sigmoid(g_conv) * (conv_pre + b_cb) + jax.nn.sigmoid(g_att) * att_out


def _glu(t):
    return t[:, :CONV_DIM] * jax.nn.sigmoid(t[:, CONV_DIM:])


def _shifted_reader(buf, shifted, tm):
    for b in range(1, SUBLANES):
        shifted[b - 1, :, :] = buf[pl.ds(b, tm + HALO - SUBLANES), :]

    def read(o):
        a, b = divmod(o, SUBLANES)
        return buf[pl.ds(SUBLANES * a, tm), :] if b == 0 else shifted[b - 1, pl.ds(SUBLANES * a, tm), :]

    return read


def _conv_fwd(conv_in, w_pad, b, ln_g, ln_b, exchange, tm=256):
    s = conv_in.shape[0]
    tm = _pick(s, tm, HALO)
    ratio = tm // HALO
    x_arrs, x_shape, x_scratch, _, x_id = exchange
    nx = len(x_arrs)

    def body(*refs):
        main_ref, halo_ref, w_ref, b_ref, g_ref, be_ref = refs[:6]
        u3_ref, u1_ref = refs[6 + nx:8 + nx]
        buf, shifted = refs[-2:]
        finish_exchange = _carry_exchange(exchange, refs, 6, 2, *_sweep_marks(s // tm))
        i = pl.program_id(0)
        buf[0:HALO, :] = _glu(halo_ref[...]) * (i > 0).astype(F32)
        buf[HALO:HALO + tm, :] = _glu(main_ref[...])
        read = _shifted_reader(buf, shifted, tm)
        acc = jnp.zeros((tm, CONV_DIM), F32) + b_ref[...]
        for j in range(CONV_WIDTH):
            acc = acc + w_ref[j:j + 1, :] * read(HALO - (CONV_WIDTH - 1) + j)
        u1_ref[...] = acc
        u3_ref[...] = _ln_silu(acc, g_ref[...], be_ref[...]).astype(u3_ref.dtype)
        finish_exchange()

    res = pl.pallas_call(
        body, name="conv_fwd", grid=(s // tm,),
        in_specs=[pl.BlockSpec((tm, 2 * CONV_DIM), lambda i: (i, 0)),
                  pl.BlockSpec((HALO, 2 * CONV_DIM), lambda i: (jnp.maximum(i * ratio - 1, 0), 0)),
                  pl.BlockSpec(w_pad.shape, lambda i: (0, 0)),
                  pl.BlockSpec(b.shape, lambda i: (0, 0)),
                  pl.BlockSpec(ln_g.shape, lambda i: (0, 0)),
                  pl.BlockSpec(ln_b.shape, lambda i: (0, 0))] + [ANY] * nx,
        out_specs=[pl.BlockSpec((tm, CONV_DIM), lambda i: (i, 0)),
                   pl.BlockSpec((tm, CONV_DIM), lambda i: (i, 0))] + [ANY] * nx,
        out_shape=[_sds((s, CONV_DIM), BF16), _sds((s, CONV_DIM), F32)] + x_shape,
        scratch_shapes=x_scratch + [pltpu.VMEM((tm + HALO, CONV_DIM), F32),
                                    pltpu.VMEM((SUBLANES - 1, tm + HALO - SUBLANES, CONV_DIM), F32)],
        compiler_params=_params(("arbitrary",), x_id),
    )(conv_in, conv_in, w_pad, b, ln_g, ln_b, *x_arrs)
    return res[0], res[1], res[2:]


def _conv_bwd(conv_in, u1, du3, ln_g, ln_b, w_pad, exchange, tm=256):
    s = conv_in.shape[0]
    tm = _pick(s, tm, HALO)
    ratio = tm // HALO
    nt = s // tm
    last_halo = s // HALO - 1
    x_arrs, x_shape, x_scratch, _, x_id = exchange
    nx = len(x_arrs)

    def body(*refs):
        main_ref, halo_ref, u1_ref, u1n_ref, du3_ref, du3n_ref, g_ref, be_ref, w_ref = refs[:9]
        dci_ref, dw_ref, db_ref, dg_ref, dbe_ref = refs[9 + nx:14 + nx]
        ubuf, dbuf, ushift, dshift = refs[-4:]
        finish_exchange = _carry_exchange(exchange, refs, 9, 5, *_sweep_marks(nt))
        i = pl.program_id(0)
        main = main_ref[...]
        a = main[:, :CONV_DIM]
        sb = jax.nn.sigmoid(main[:, CONV_DIM:])
        ubuf[0:HALO, :] = _glu(halo_ref[...]) * (i > 0).astype(F32)
        ubuf[HALO:HALO + tm, :] = a * sb

        def ln_bwd(u1t, du3t):
            _, vjp = jax.vjp(_ln_silu, u1t, g_ref[...], be_ref[...])
            return vjp(du3t)

        du, dg, dbe = ln_bwd(u1_ref[...], du3_ref[...])
        dbuf[0:tm, :] = du
        dbuf[tm:tm + HALO, :] = ln_bwd(u1n_ref[...], du3n_ref[...])[0] * (i < nt - 1).astype(F32)

        @pl.when(i == 0)
        def _():
            dw_ref[...] = jnp.zeros_like(dw_ref)
            db_ref[...] = jnp.zeros_like(db_ref)
            dg_ref[...] = jnp.zeros_like(dg_ref)
            dbe_ref[...] = jnp.zeros_like(dbe_ref)

        dg_ref[...] += dg
        dbe_ref[...] += dbe

        read_u = _shifted_reader(ubuf, ushift, tm)
        read_d = _shifted_reader(dbuf, dshift, tm)
        du0 = jnp.zeros((tm, CONV_DIM), F32)
        for j in range(CONV_WIDTH):
            du0 = du0 + w_ref[j:j + 1, :] * read_d(CONV_WIDTH - 1 - j)
            dw_ref[j:j + 1, :] += jnp.sum(du * read_u(HALO - (CONV_WIDTH - 1) + j), axis=0, keepdims=True)
        db_ref[...] += jnp.sum(du, axis=0, keepdims=True)
        dci_ref[:, :CONV_DIM] = (du0 * sb).astype(dci_ref.dtype)
        dci_ref[:, CONV_DIM:] = (du0 * a * sb * (1.0 - sb)).astype(dci_ref.dtype)
        finish_exchange()

    res = pl.pallas_call(
        body, name="conv_bwd", grid=(nt,),
        in_specs=[pl.BlockSpec((tm, 2 * CONV_DIM), lambda i: (i, 0)),
                  pl.BlockSpec((HALO, 2 * CONV_DIM), lambda i: (jnp.maximum(i * ratio - 1, 0), 0))]
        + [pl.BlockSpec((tm, CONV_DIM), lambda i: (i, 0)),
           pl.BlockSpec((HALO, CONV_DIM), lambda i: (jnp.minimum((i + 1) * ratio, last_halo), 0))] * 2
        + [pl.BlockSpec((1, CONV_DIM), lambda i: (0, 0))] * 2 + [pl.BlockSpec(w_pad.shape, lambda i: (0, 0))]
        + [ANY] * nx,
        out_specs=[pl.BlockSpec((tm, 2 * CONV_DIM), lambda i: (i, 0)),
                   pl.BlockSpec(w_pad.shape, lambda i: (0, 0))]
        + [pl.BlockSpec((1, CONV_DIM), lambda i: (0, 0))] * 3 + [ANY] * nx,
        out_shape=[_sds((s, 2 * CONV_DIM), BF16), _sds(w_pad.shape)] + [_sds((1, CONV_DIM))] * 3 + x_shape,
        scratch_shapes=x_scratch + [pltpu.VMEM((tm + HALO, CONV_DIM), F32)] * 2
        + [pltpu.VMEM((SUBLANES - 1, tm + HALO - SUBLANES, CONV_DIM), F32)] * 2,
        compiler_params=_params(("arbitrary",), x_id),
    )(conv_in, conv_in, u1, u1, du3, du3, ln_g, ln_b, w_pad, *x_arrs)
    return res[:5], res[5:]


def _logsig_neg(z):
    return jnp.minimum(-z, 0.0) - jnp.log(1.0 + jnp.exp(-jnp.abs(z)))


def _split_dot(val, tri):
    hi = val.astype(BF16)
    lo = (val - hi.astype(F32)).astype(BF16)
    return jnp.dot(hi, tri, preferred_element_type=F32) + jnp.dot(lo, tri, preferred_element_type=F32)


def _attn_masks(t, later):
    row = lax.broadcasted_iota(jnp.int32, (t, t), 0)
    col = lax.broadcasted_iota(jnp.int32, (t, t), 1)
    tri = jnp.where(row > col if later else row <= col, 1.0, 0.0).astype(BF16)
    return col < row, tri


def _grid_marks(h, nq):
    hh, i = pl.program_id(0), pl.program_id(1)
    return ((hh == 0) & (i == 0), (hh == h - 1) & (i == nq // 2), (hh == h - 1) & (i == nq - 1),
            (hh == h // 2) & (i == 0))


def _head_masks(shape):
    lane = lax.broadcasted_iota(jnp.int32, shape, len(shape) - 1)
    return lane < HEAD_DIM, lane >= HEAD_DIM


def _per_head(blk):
    m0, m1 = _head_masks(blk.shape)
    zero = jnp.zeros_like(blk)
    return jnp.where(m0, blk, zero), jnp.where(m1, blk, zero)


NT = (((1,), (1,)), ((), ()))
TN = (((0,), (0,)), ((), ()))


def _with_top(whole, top):
    rows = top.shape[0]
    return top if rows == whole.shape[0] else jnp.concatenate([top, whole[rows:]], axis=0)


def _attn_fwd(q, k, v, exchange):
    s = q.shape[0]
    hp = q.shape[1] // LANES
    t = ATT_TILE
    scale = 1.0 / math.sqrt(HEAD_DIM)
    x_arrs, x_shape, x_scratch, _, x_id = exchange
    nx = len(x_arrs)

    def body(*refs):
        q_ref, k_ref, v_ref = refs[:3]
        o_ref, lt_ref, nb_ref = refs[3 + nx:6 + nx]
        finish_exchange = _carry_exchange(exchange, refs, 3, 3, *_grid_marks(hp, s // t))
        i = pl.program_id(1)
        qs = _per_head((q_ref[...].astype(F32) * scale).astype(BF16))
        causal, tri = _attn_masks(t, later=True)

        def step(kb, carry, masked, rows):
            cs, acc = carry
            off = pl.multiple_of(kb * t, t)
            kblk = k_ref[pl.ds(off, t), :]
            vs = _per_head(v_ref[pl.ds(off, t), :])
            acc_top = acc[:rows]
            new_cs = []
            for hd in range(2):
                z = lax.dot_general(qs[hd][:rows], kblk, NT, preferred_element_type=F32)
                l = _logsig_neg(z)
                if masked:
                    l = jnp.where(causal, l, 0.0)
                e = z + l + _split_dot(l, tri) + cs[hd][:rows]
                if masked:
                    e = jnp.where(causal, e, -1e30)
                acc_top = acc_top + jnp.dot(jnp.exp(e).astype(BF16), vs[hd], preferred_element_type=F32)
                new_cs.append(_with_top(cs[hd], cs[hd][:rows] + jnp.sum(l, axis=1, keepdims=True)))
            return tuple(new_cs), _with_top(acc, acc_top)

        zero = jnp.zeros((t, 1), F32)
        carry = step(i, ((zero, zero), jnp.zeros((t, LANES), F32)), True, t)

        def live(cs, lo, hi):
            return jnp.maximum(jnp.max(cs[0][lo:hi]), jnp.max(cs[1][lo:hi])) > DEAD_SUM

        def more(state):
            n, _, (cs, _) = state
            return (n < i) & live(cs, 0, t)

        def sweep(state):
            n, n_full, cr = state
            whole = live(cr[0], ATT_PART, t)
            cr = lax.cond(whole, lambda c: step(i - 1 - n, c, False, t), lambda c: step(i - 1 - n, c, False, ATT_PART), cr)
            return n + 1, n_full + whole.astype(jnp.int32), cr

        n_blocks, n_full, carry = lax.while_loop(more, sweep, (jnp.int32(0), jnp.int32(0), carry))
        m0, _ = _head_masks((t, LANES))
        lt_ref[...] = jnp.where(m0, carry[0][0], carry[0][1])
        o_ref[...] = carry[1].astype(o_ref.dtype)
        nb_ref[0, pl.program_id(0), i] = n_blocks.astype(F32)
        nb_ref[1, pl.program_id(0), i] = n_full.astype(F32)
        finish_exchange()

    res = pl.pallas_call(
        body, name="attn_fwd", grid=(hp, s // t),
        in_specs=[pl.BlockSpec((t, LANES), lambda p, i: (i, p)),
                  pl.BlockSpec((s, LANES), lambda p, i: (0, p)),
                  pl.BlockSpec((s, LANES), lambda p, i: (0, p))] + [ANY] * nx,
        out_specs=[pl.BlockSpec((t, LANES), lambda p, i: (i, p)),
                   pl.BlockSpec((None, t, LANES), lambda p, i: (p, i, 0)),
                   pl.BlockSpec(memory_space=pltpu.SMEM)] + [ANY] * nx,
        out_shape=[_sds(q.shape, BF16), _sds((hp, s, LANES), F32), _sds((2, hp, s // t), F32)] + x_shape,
        scratch_shapes=x_scratch,
        compiler_params=_params(("arbitrary", "arbitrary"), x_id),
    )(q, k, v, *x_arrs)
    return res[0], res[1], res[2], res[3:]


def _attn_bwd(q, k, v, do, ltot, n_blocks, exchange):
    s = q.shape[0]
    hp = q.shape[1] // LANES
    t = ATT_TILE
    scale = 1.0 / math.sqrt(HEAD_DIM)
    x_arrs, x_shape, x_scratch, _, x_id = exchange
    nx = len(x_arrs)

    def body(*refs):
        q_ref, k_ref, v_ref, do_ref, lt_ref, nb_ref = refs[:6]
        dq_ref, dk_ref, dv_ref = refs[6 + nx:9 + nx]
        finish_exchange = _carry_exchange(exchange, refs, 6, 3, *_grid_marks(hp, s // t))
        i = pl.program_id(1)
        n_blocks = jnp.clip(nb_ref[0, pl.program_id(0), i].astype(jnp.int32), 0, i)
        n_full = jnp.clip(nb_ref[1, pl.program_id(0), i].astype(jnp.int32), 0, n_blocks)

        @pl.when(i == 0)
        def _():
            dk_ref[...] = jnp.zeros_like(dk_ref)
            dv_ref[...] = jnp.zeros_like(dv_ref)

        qb = q_ref[...]
        qm = _per_head(qb)
        qs = _per_head((qb.astype(F32) * scale).astype(BF16))
        dos = _per_head(do_ref[...])
        lts = (lt_ref[:, 0:1], lt_ref[:, HEAD_DIM:HEAD_DIM + 1])
        causal, tri = _attn_masks(t, later=False)

        def step(kb, carry, masked, rows):
            cls, cgs, dq = carry
            off = pl.multiple_of(kb * t, t)
            kblk = k_ref[pl.ds(off, t), :]
            vblk = v_ref[pl.ds(off, t), :]
            ks = _per_head(kblk)
            dq_top = dq[:rows]
            dk = jnp.zeros((t, LANES), F32)
            dv = jnp.zeros((t, LANES), F32)
            new_cls, new_cgs = [], []
            for hd in range(2):
                z = lax.dot_general(qs[hd][:rows], kblk, NT, preferred_element_type=F32)
                l = _logsig_neg(z)
                if masked:
                    l = jnp.where(causal, l, 0.0)
                e = z + l + ((lts[hd][:rows] - cls[hd][:rows]) - _split_dot(l, tri))
                if masked:
                    e = jnp.where(causal, e, -1e30)
                a = jnp.exp(e)
                g = lax.dot_general(dos[hd][:rows], vblk, NT, preferred_element_type=F32) * a
                p = cgs[hd][:rows] + jnp.dot(g.astype(BF16), tri, preferred_element_type=F32) - g
                el = jnp.exp(l)
                dz = g * el - p * (1.0 - el)
                if masked:
                    dz = jnp.where(causal, dz, 0.0)
                dzb = (dz * scale).astype(BF16)
                dq_top = dq_top + jnp.dot(dzb, ks[hd], preferred_element_type=F32)
                dk = dk + lax.dot_general(dzb, qm[hd][:rows], TN, preferred_element_type=F32)
                dv = dv + lax.dot_general(a.astype(BF16), dos[hd][:rows], TN, preferred_element_type=F32)
                new_cls.append(_with_top(cls[hd], cls[hd][:rows] + jnp.sum(l, axis=1, keepdims=True)))
                new_cgs.append(_with_top(cgs[hd], cgs[hd][:rows] + jnp.sum(g, axis=1, keepdims=True)))
            dk_ref[pl.ds(off, t), :] += dk
            dv_ref[pl.ds(off, t), :] += dv
            return tuple(new_cls), tuple(new_cgs), _with_top(dq, dq_top)

        zero = jnp.zeros((t, 1), F32)
        init = ((zero, zero), (zero, zero), jnp.zeros((t, LANES), F32))
        carry = lax.fori_loop(i - n_blocks, i - n_full, lambda kb, cr: step(kb, cr, False, ATT_PART), init)
        carry = lax.fori_loop(i - n_full, i, lambda kb, cr: step(kb, cr, False, t), carry)
        carry = step(i, carry, True, t)
        dq_ref[...] = carry[2]
        finish_exchange()

    blk = pl.BlockSpec((t, LANES), lambda p, i: (i, p))
    whole = pl.BlockSpec((s, LANES), lambda p, i: (0, p))
    res = pl.pallas_call(
        body, name="attn_bwd", grid=(hp, s // t),
        in_specs=[blk, whole, whole, blk, pl.BlockSpec((None, t, LANES), lambda p, i: (p, i, 0)),
                  pl.BlockSpec(memory_space=pltpu.SMEM)] + [ANY] * nx,
        out_specs=[blk, whole, whole] + [ANY] * nx,
        out_shape=[_sds(q.shape)] * 3 + x_shape,
        scratch_shapes=x_scratch,
        compiler_params=_params(("arbitrary", "arbitrary"), x_id),
    )(q, k, v, do, ltot, n_blocks, *x_arrs)
    return res[0], res[1], res[2], res[3:]


LATE = ["w_conv_branch", "w_att_branch", "w_out", "w_ffn_up", "w_ffn_down"]


def _full_weight(name, gathered):
    return _cols_to_full(gathered) if name in COL_SHARDED else gathered.reshape(-1, gathered.shape[2])


def _grad_slabs(name, grad):
    return _full_to_cols(grad) if name in COL_SHARDED else grad.reshape(N_DEV, -1, grad.shape[1])


def _side_slabs(name, grad):
    slabs = _grad_slabs(name, grad)
    return slabs.reshape((4, 2) + slabs.shape[1:])


def _local_step(x, target, w, late_blocks, opt):
    s = x.shape[0]
    w = dict(w)
    g1, g2, g3, g4 = w["norm_mix_pre"], w["norm_mix_post"], w["norm_ffn_pre"], w["norm_ffn_post"]

    w_in = w["w_in"]

    def proj_fn(xt, g1_, w_in_t):
        h = _rms(xt, g1_).astype(BF16)
        proj = lax.dot_general(h, w_in_t, NT, preferred_element_type=F32)
        return (h, *[proj[:, IN_SPLITS[n]:IN_SPLITS[n + 1]] for n in range(6)]), ()

    mix_weights = ["w_conv_branch", "w_att_branch", "w_out"]
    h1, conv_in, q, k, v, g_conv, g_att, g_out = _rowwise(
        "norm_proj", proj_fn, [x], [g1, w_in],
        [_sds((s, D_MODEL), BF16), _sds((s, 2 * CONV_DIM)), _sds((s, ATT_DIM), BF16), _sds((s, ATT_DIM), BF16),
         _sds((s, ATT_DIM), BF16), _sds((s, D_MODEL), BF16), _sds((s, D_MODEL), BF16)], tm=512,
        exchange=_relay_gather_exchange([late_blocks["w_out"]]))

    u3, u1, g_branches = _conv_fwd(conv_in, w["conv_dw_w"], w["conv_dw_b"], w["conv_ln_g"], w["conv_ln_b"],
                                   _relay_gather_exchange([late_blocks[nm] for nm in mix_weights[:2]]))
    for nm, g in zip(mix_weights, [*g_branches, g_out]):
        w[nm] = _full_weight(nm, g)
    att, ltot, n_blocks, (g_up,) = _attn_fwd(q, k, v, _gather_exchange([late_blocks["w_ffn_up"]]))
    w["w_ffn_up"] = _full_weight("w_ffn_up", g_up)

    def merge_fn(u3t, at, gc, ga, xt, w_cb, w_ab, b_cb, w_out, g2_, g3_):
        cp = jnp.dot(u3t, w_cb, preferred_element_type=F32)
        ao = jnp.dot(at, w_ab, preferred_element_type=F32)
        mg = _merge(cp, ao, gc.astype(F32), ga.astype(F32), b_cb).astype(BF16)
        mix_ = jnp.dot(mg, w_out, preferred_element_type=F32)
        x2_ = xt + _rms(mix_, g2_)
        return (mg, cp, ao, mix_, x2_, _rms(x2_, g3_)), ()

    half = D_MODEL // 2
    down_block = late_blocks["w_ffn_down"]
    merged, conv_pre, att_out, mix, x2, h2, g_left = _rowwise(
        "branch_merge_mix", merge_fn, [u3, att, g_conv, g_att, x],
        [w["w_conv_branch"], w["w_att_branch"], w["b_conv_branch"], w["w_out"], g2, g3],
        [_sds((s, D_MODEL), BF16)] * 3 + [_sds((s, D_MODEL)), _sds((s, D_MODEL)), _sds((s, D_MODEL), BF16)], tm=512,
        exchange=_relay_gather_exchange([down_block[:, :half]]))

    def ffn_up_fn(ht, w_up_t):
        gu_ = lax.dot_general(ht, w_up_t, NT, preferred_element_type=F32)
        return (gu_, _swiglu(gu_[:, :D_FF], gu_[:, D_FF:])), ()

    gu, act, g_right = _rowwise("ffn_up", ffn_up_fn, [h2], [w["w_ffn_up"]],
                                [_sds((s, 2 * D_FF), BF16), _sds((s, D_FF), BF16)], tm=512,
                                exchange=_relay_gather_exchange([down_block[:, half:]]))
    w_down = [_full_weight("w_ffn_down", g) for g in (g_left, g_right)]

    def final_fn(at, x2t, tgt, w_left, w_right, g4_):
        ff = jnp.concatenate([jnp.dot(at, w_left, preferred_element_type=F32),
                              jnp.dot(at, w_right, preferred_element_type=F32)], axis=1)
        n4, vjp = jax.vjp(_rms, ff, g4_)
        err = x2t + n4 - tgt
        dy = err * (1.0 / D_MODEL)
        dff, dg4 = vjp(dy)
        return (dy, dff), (jnp.sum(err * err, axis=0, keepdims=True), dg4)

    dy, dff, loss_cols, d_g4 = _rowwise("ffn_down_loss", final_fn, [act, x2, target], [*w_down, g4],
                                        [_sds((s, D_MODEL)), _sds((s, D_MODEL), BF16)],
                                        [_sds((1, D_MODEL)), _sds((1, D_MODEL))], tm=512)
    loss = 0.5 * jnp.sum(loss_cols) / D_MODEL

    d_w_down = _tn_matmul(act, dff, name="d_w_down")

    def act_bwd_fn(dfft, gut, w_left, w_right):
        d_act = (lax.dot_general(dfft[:, :half], w_left, NT, preferred_element_type=F32)
                 + lax.dot_general(dfft[:, half:], w_right, NT, preferred_element_type=F32))
        gu_ = gut.astype(F32)
        _, vjp = jax.vjp(_swiglu, gu_[:, :D_FF], gu_[:, D_FF:])
        return (jnp.concatenate(vjp(d_act), axis=1),), ()

    down_slabs = _side_slabs("w_ffn_down", d_w_down)
    dgu, theirs = _rowwise("ffn_act_bwd", act_bwd_fn, [dff, gu], w_down, [_sds((s, 2 * D_FF), BF16)],
                           exchange=_pair_exchange([down_slabs]))
    down_sums = _pair_sum("pair_sum_w_ffn_down", down_slabs, theirs)
    d_w_up = _tn_matmul(dgu, h2, name="d_w_up")
    received = {}
    up_slabs = _side_slabs("w_ffn_up", d_w_up)

    def mid_bwd_fn(dgut, xt, mt, dyt, w_up_t, g2_, g3_):
        dh = jnp.dot(dgut, w_up_t, preferred_element_type=F32)
        n2, vjp2 = jax.vjp(_rms, mt, g2_)
        x2_ = xt + n2
        _, vjp3 = jax.vjp(_rms, x2_, g3_)
        dx2_, dg3 = vjp3(dh)
        dx2_ = dx2_ + dyt
        dmix_, dg2 = vjp2(dx2_)
        return (dx2_, dmix_), (dg2, dg3)

    dx2, dmix, d_g2, d_g3, received["w_ffn_down"] = _rowwise(
        "ffn_up_mid_bwd", mid_bwd_fn, [dgu, x, mix, dy], [w["w_ffn_up"], g2, g3],
        [_sds((s, D_MODEL)), _sds((s, D_MODEL), BF16)], [_sds((1, D_MODEL)), _sds((1, D_MODEL))], tm=512,
        exchange=_chip_exchange([down_sums]))
    d_w_out = _tn_matmul(merged, dmix, name="d_w_out")

    def merge_bwd_fn(dmt, cp, ao, gc, ga, w_out, w_cb, w_ab, b_cb):
        dm = lax.dot_general(dmt, w_out, NT, preferred_element_type=F32)
        _, vjp = jax.vjp(_merge, cp.astype(F32), ao.astype(F32), gc.astype(F32), ga.astype(F32), b_cb)
        dcp, dao, dgc, dga, dbias = vjp(dm)
        dcp, dao = dcp.astype(BF16), dao.astype(BF16)
        du3_ = lax.dot_general(dcp, w_cb, NT, preferred_element_type=F32)
        datt_ = lax.dot_general(dao, w_ab, NT, preferred_element_type=F32)
        return (dcp, dao, dgc, dga, du3_, datt_), (dbias,)

    d_conv_out, d_att_out, d_g_conv, d_g_att, du3, d_att, d_b_cb, theirs = _rowwise(
        "merge_bwd", merge_bwd_fn, [dmix, conv_pre, att_out, g_conv, g_att],
        [w["w_out"], w["w_conv_branch"], w["w_att_branch"], w["b_conv_branch"]],
        [_sds((s, D_MODEL), BF16)] * 4 + [_sds((s, CONV_DIM)), _sds((s, ATT_DIM), BF16)], [_sds((1, D_MODEL))], tm=512,
        exchange=_pair_exchange([up_slabs]))

    d_w_cb = _tn_matmul(u3, d_conv_out, name="d_w_conv_branch")
    d_w_ab = _tn_matmul(att, d_att_out, name="d_w_att_branch")

    dq, dk, dv, (received["w_ffn_up"],) = _attn_bwd(
        q, k, v, d_att, ltot, n_blocks, _chip_exchange([_pair_sum("pair_sum_w_ffn_up", up_slabs, theirs)]))

    mix_grads = {"w_conv_branch": d_w_cb, "w_att_branch": d_w_ab, "w_out": d_w_out}
    (d_conv_in, d_dw_w, d_dw_b, d_ln_g, d_ln_b), landed = _conv_bwd(
        conv_in, u1, du3, w["conv_ln_g"], w["conv_ln_b"], w["conv_dw_w"],
        _scatter_exchange([_grad_slabs(nm, mix_grads[nm]) for nm in mix_weights[:2]]))
    received.update(zip(mix_weights[:2], landed))

    d_proj = [d_conv_in, dq, dk, dv, d_g_conv, d_g_att]
    d_w_in, (received["w_out"],) = _pieces_tn_matmul(
        d_proj, h1, name="d_w_in", exchange=_scatter_exchange([_grad_slabs("w_out", d_w_out)]))
    in_slabs = _side_slabs("w_in", d_w_in)
    (theirs,) = _exchange_call("pair_swap_w_in", _pair_exchange([in_slabs]))

    early = list(opt)

    def pre_bwd_fn(*args):
        groups, (xt, dx2t), jobs, (w_in_t, g_) = args[:6], args[6:8], args[8:-2], args[-2:]
        dh = sum(jnp.dot(grp.astype(BF16), w_in_t[IN_SPLITS[n]:IN_SPLITS[n + 1]], preferred_element_type=F32)
                 for n, grp in enumerate(groups))
        _, vjp = jax.vjp(_rms, xt, g_)
        dx_, dg_ = vjp(dh)
        updates = [_sum_adamw_tile(*jobs[4 * n:4 * n + 4]) for n in range(len(early))]
        return (dx_ + dx2t, *[u for four in updates for u in four]), (dg_,)

    res = _rowwise(
        "proj_norm_bwd", pre_bwd_fn,
        d_proj + [x, dx2] + [a for nm in early for a in (received[nm], *opt[nm])], [w_in, g1],
        [_sds((s, D_MODEL))] + [_sds(opt[nm][0].shape) for nm in early for _ in range(4)],
        [_sds((1, D_MODEL))], tm=512, exchange=_chip_exchange([_pair_sum("pair_sum_w_in", in_slabs, theirs)]))
    grad_x, d_g1, received["w_in"] = res[0], res[-2], res[-1]
    updated = {nm: res[1 + 4 * n:5 + 4 * n] for n, nm in enumerate(early)}

    grads = {
        "norm_mix_pre": d_g1, "conv_dw_w": d_dw_w, "conv_dw_b": d_dw_b,
        "conv_ln_g": d_ln_g, "conv_ln_b": d_ln_b, "b_conv_branch": d_b_cb,
        "norm_mix_post": d_g2, "norm_ffn_pre": d_g3, "norm_ffn_post": d_g4,
    }
    return loss, grad_x, received, updated, grads


def _place():
    x, y, c = lax.axis_index("x"), lax.axis_index("y"), lax.axis_index("c")
    return x, y, c


def _slot(px, py, pc):
    return 4 * px + 2 * py + pc


def _exchange_scratch(n):
    return [pltpu.SemaphoreType.DMA((7 * n,)), pltpu.SemaphoreType.DMA((7 * n,)), pltpu.SemaphoreType.DMA((n,))]


GATHER_ID, SCATTER_ID, PAIR_ID, CHIP_ID, RELAY_ID = 0, 1, 2, 3, 4


def _handshake(peers):
    barrier = pltpu.get_barrier_semaphore()
    for peer in peers:
        pl.semaphore_signal(barrier, inc=1, device_id=peer, device_id_type=MESH)
    pl.semaphore_wait(barrier, len(peers))


def _gather_exchange(arrs):
    n = len(arrs)

    def phases(ins, outs, send_sems, recv_sems, local_sems):
        x, y, c = _place()
        me, sibling = (x, y, c), (x, y, 1 - c)
        chips = [(1 - x, y), (x, 1 - y), (1 - x, 1 - y)]

        def copy(a, kk, block, to, src=None):
            dst = outs[a].at[_slot(*block)]
            return pltpu.make_async_remote_copy(
                src_ref=dst if src is None else src, dst_ref=dst,
                send_sem=send_sems.at[a * 7 + kk], recv_sem=recv_sems.at[a * 7 + kk],
                device_id=to, device_id_type=MESH)

        mine = [pltpu.make_async_copy(ins[a], outs[a].at[_slot(*me)], local_sems.at[a]) for a in range(n)]
        first = []
        for a in range(n):
            first.append(copy(a, 0, me, sibling, src=ins[a]))
            first += [copy(a, 1 + j, me, (*chip, c), src=ins[a]) for j, chip in enumerate(chips)]
        passed = [copy(a, 4 + j, (*chip, c), sibling) for j, chip in enumerate(chips) for a in range(n)]

        def send():
            _handshake([sibling] + [(*chip, c) for chip in chips])
            for cp in mine + first:
                cp.start()

        def pass_on():
            for j, chip in enumerate(chips):
                for a in range(n):
                    copy(a, 1 + j, (*chip, c), me).wait_recv()
                    passed[j * n + a].start()

        def finish():
            for a in range(n):
                copy(a, 0, sibling, me).wait_recv()
                for j, chip in enumerate(chips):
                    copy(a, 4 + j, (*chip, 1 - c), me).wait_recv()
            for cp in first + passed:
                cp.wait_send()
            for cp in mine:
                cp.wait()

        return [send, pass_on, finish]

    return list(arrs), [_sds((N_DEV,) + a.shape, a.dtype) for a in arrs], _exchange_scratch(n), phases, GATHER_ID


def _relay_gather_exchange(arrs):
    n = len(arrs)
    per = 8

    def phases(ins, outs, send_sems, recv_sems, local_sems):
        x, y, c = _place()
        me, sibling = (x, y, c), (x, y, 1 - c)
        beside, below, across = (1 - x, y, c), (x, 1 - y, c), (1 - x, 1 - y, c)

        def copy(a, kk, block, to, src=None, rows=None):
            where = _slot(*block) if rows is None else (_slot(*block), rows)
            dst = outs[a].at[where]
            return pltpu.make_async_remote_copy(
                src_ref=dst if src is None else src, dst_ref=dst,
                send_sem=send_sems.at[a * per + kk], recv_sem=recv_sems.at[a * per + kk],
                device_id=to, device_id_type=MESH)

        def halves(a):
            h = ins[a].shape[0] // 2
            return pl.ds(0, h), pl.ds(h, ins[a].shape[0] - h)

        mine = [pltpu.make_async_copy(ins[a], outs[a].at[_slot(*me)], local_sems.at[a]) for a in range(n)]
        first = [copy(a, kk, me, to, src=ins[a]) for a in range(n) for kk, to in enumerate([sibling, beside, below])]
        relayed = [[copy(a, 3, beside, sibling), copy(a, 5, beside, below, rows=halves(a)[0])] for a in range(n)]
        relayed += [[copy(a, 4, below, sibling), copy(a, 6, below, beside, rows=halves(a)[1])] for a in range(n)]
        passed = [copy(a, 7, across, sibling) for a in range(n)]

        def send():
            _handshake([sibling, beside, below])
            for cp in mine + first:
                cp.start()

        def relay():
            for kk, block in ((1, beside), (2, below)):
                for a in range(n):
                    copy(a, kk, block, me).wait_recv()
                    for cp in relayed[(kk - 1) * n + a]:
                        cp.start()

        def pass_on():
            for a in range(n):
                copy(a, 5, across, me, rows=halves(a)[0]).wait_recv()
                copy(a, 6, across, me, rows=halves(a)[1]).wait_recv()
                passed[a].start()

        def finish():
            for a in range(n):
                for kk, block in ((0, me), (3, beside), (4, below), (7, across)):
                    copy(a, kk, (*block[:2], 1 - c), me).wait_recv()
            for cp in first + [cp for two in relayed for cp in two] + passed:
                cp.wait_send()
            for cp in mine:
                cp.wait()

        return [send, relay, pass_on, finish]

    scratch = [pltpu.SemaphoreType.DMA((per * n,)), pltpu.SemaphoreType.DMA((per * n,)), pltpu.SemaphoreType.DMA((n,))]
    return list(arrs), [_sds((N_DEV,) + a.shape, a.dtype) for a in arrs], scratch, phases, RELAY_ID


def _scatter_exchange(arrs):
    n = len(arrs)
    flips = [(fx, fy, fc) for fx in (0, 1) for fy in (0, 1) for fc in (0, 1)][1:]

    def phases(ins, outs, send_sems, recv_sems, local_sems):
        x, y, c = _place()
        mine = _slot(x, y, c)
        local = [pltpu.make_async_copy(ins[a].at[mine], outs[a].at[mine], local_sems.at[a]) for a in range(n)]
        peers = [((1 - x) if fx else x, (1 - y) if fy else y, (1 - c) if fc else c) for fx, fy, fc in flips]

        def copy(a, kk, src_slot, dst_slot):
            return pltpu.make_async_remote_copy(
                src_ref=ins[a].at[src_slot], dst_ref=outs[a].at[dst_slot],
                send_sem=send_sems.at[a * 7 + kk], recv_sem=recv_sems.at[a * 7 + kk],
                device_id=peers[kk], device_id_type=MESH)

        sends = [copy(a, kk, _slot(*peers[kk]), mine) for a in range(n) for kk in range(7)]

        def send():
            _handshake(peers)
            for cp in local + sends:
                cp.start()

        def finish():
            for a in range(n):
                for kk in range(7):
                    copy(a, kk, mine, _slot(*peers[kk])).wait_recv()
            for cp in sends:
                cp.wait_send()
            for cp in local:
                cp.wait()

        return [send, finish]

    return list(arrs), [_sds(a.shape, a.dtype) for a in arrs], _exchange_scratch(n), phases, SCATTER_ID


def _pair_exchange(arrs):
    n = len(arrs)

    def phases(ins, outs, send_sems, recv_sems, local_sems):
        x, y, c = _place()

        def copy(a, chip, side):
            return pltpu.make_async_remote_copy(
                src_ref=ins[a].at[chip, side], dst_ref=outs[a].at[chip],
                send_sem=send_sems.at[a * 7 + chip], recv_sem=recv_sems.at[a * 7 + chip],
                device_id=(x, y, 1 - c), device_id_type=MESH)

        sends = [copy(a, chip, 1 - c) for a in range(n) for chip in range(4)]

        def send():
            _handshake([(x, y, 1 - c)])
            for cp in sends:
                cp.start()

        def finish():
            for a in range(n):
                for chip in range(4):
                    copy(a, chip, c).wait_recv()
            for cp in sends:
                cp.wait_send()

        return [send, finish]

    return list(arrs), [_sds((4,) + a.shape[2:], a.dtype) for a in arrs], _exchange_scratch(n), phases, PAIR_ID


def _chip_exchange(arrs):
    n = len(arrs)

    def phases(ins, outs, send_sems, recv_sems, local_sems):
        x, y, c = _place()
        mine = 2 * x + y
        chips = [(1 - x, y), (x, 1 - y), (1 - x, 1 - y)]
        local = [pltpu.make_async_copy(ins[a].at[mine], outs[a].at[mine], local_sems.at[a]) for a in range(n)]

        def copy(a, j, src_slot, dst_slot):
            return pltpu.make_async_remote_copy(
                src_ref=ins[a].at[src_slot], dst_ref=outs[a].at[dst_slot],
                send_sem=send_sems.at[a * 7 + j], recv_sem=recv_sems.at[a * 7 + j],
                device_id=(*chips[j], c), device_id_type=MESH)

        sends = [copy(a, j, 2 * chips[j][0] + chips[j][1], mine) for a in range(n) for j in range(3)]

        def send():
            _handshake([(*chip, c) for chip in chips])
            for cp in local + sends:
                cp.start()

        def finish():
            for a in range(n):
                for j in range(3):
                    copy(a, j, mine, 2 * chips[j][0] + chips[j][1]).wait_recv()
            for cp in sends:
                cp.wait_send()
            for cp in local:
                cp.wait()

        return [send, finish]

    return list(arrs), [_sds(a.shape, a.dtype) for a in arrs], _exchange_scratch(n), phases, CHIP_ID


def _pair_sum(name, mine, theirs):
    _, _, r, c = mine.shape

    def body(side_ref, m_ref, t_ref, o_ref):
        o_ref[...] = (m_ref[...].astype(F32) + t_ref[...].astype(F32)).astype(o_ref.dtype)

    return pl.pallas_call(
        body, name=name,
        grid_spec=pltpu.PrefetchScalarGridSpec(
            num_scalar_prefetch=1, grid=(4,),
            in_specs=[pl.BlockSpec((None, None, r, c), lambda j, side: (j, side[0], 0, 0)),
                      pl.BlockSpec((None, r, c), lambda j, side: (j, 0, 0))],
            out_specs=pl.BlockSpec((None, r, c), lambda j, side: (j, 0, 0))),
        out_shape=_sds(theirs.shape, theirs.dtype),
        compiler_params=_params(("parallel",)),
    )(lax.axis_index("c").astype(jnp.int32).reshape(1), mine, theirs)


def _exchange_call(name, exchange):
    arrs, out_shape, scratch, phases, collective_id = exchange
    n = len(arrs)

    def body(*refs):
        for step in phases(refs[:n], refs[n:2 * n], *refs[2 * n:]):
            step()

    return pl.pallas_call(body, name=name, in_specs=[ANY] * n, out_specs=[ANY] * n,
                          out_shape=out_shape, scratch_shapes=scratch,
                          compiler_params=pltpu.CompilerParams(collective_id=collective_id))(*arrs)


def _carry_exchange(exchange, refs, n_in, n_out, first, middle, last, halfway):
    arrs, _, _, phases, _ = exchange
    n = len(arrs)
    if n == 0:
        return lambda: None
    ins = refs[n_in:n_in + n]
    outs = refs[n_in + n + n_out:n_in + 2 * n + n_out]
    sems = n_in + 2 * n + n_out
    steps = phases(ins, outs, *refs[sems:sems + 3])
    pl.when(first)(steps[0])
    if len(steps) == 4:
        pl.when(halfway)(steps[1])
    if len(steps) >= 3:
        pl.when(middle)(steps[-2])
    return lambda: pl.when(last)(steps[-1])


def _adamw_math(w, g, m, v):
    m2 = ADAM_B1 * m + (1.0 - ADAM_B1) * g
    v2 = ADAM_B2 * v + (1.0 - ADAM_B2) * jnp.square(g)
    m_hat = m2 / (1.0 - ADAM_B1 ** ADAM_STEP)
    v_hat = v2 / (1.0 - ADAM_B2 ** ADAM_STEP)
    delta = -ADAM_LR * (m_hat / (jnp.sqrt(v_hat) + ADAM_EPS) + ADAM_WD * w)
    return delta, m2, v2


def _sum_adamw_tile(parts, w, m, v):
    g = parts[0].astype(F32)
    for d in range(1, parts.shape[0]):
        g = g + parts[d].astype(F32)
    return (g, *_adamw_math(w, g, m, v))


def _sum_adamw(name, parts, w, m, v, tr=256):
    p, r, c = parts.shape
    tr = _pick(r, tr, 16)

    def body(p_ref, w_ref, m_ref, v_ref, g_ref, d_ref, m2_ref, v2_ref):
        g_ref[...], d_ref[...], m2_ref[...], v2_ref[...] = _sum_adamw_tile(p_ref[...], w_ref[...], m_ref[...], v_ref[...])

    tile = pl.BlockSpec((tr, c), lambda i: (i, 0))
    return pl.pallas_call(
        body, name=name, grid=(r // tr,),
        in_specs=[pl.BlockSpec((p, tr, c), lambda i: (0, i, 0)), tile, tile, tile],
        out_specs=[tile] * 4, out_shape=[_sds((r, c))] * 4,
        compiler_params=_params(("parallel",)),
    )(parts, w, m, v)


def _sum_parts(name, parts):
    p, r, c = parts.shape

    def body(p_ref, o_ref):
        g = p_ref[0]
        for d in range(1, p):
            g = g + p_ref[d]
        o_ref[...] = g

    return pl.pallas_call(
        body, name=name, out_shape=_sds((r, c)),
        in_specs=[pl.BlockSpec(memory_space=pltpu.VMEM)], out_specs=pl.BlockSpec(memory_space=pltpu.VMEM),
    )(parts)


WEIGHTS = ["norm_mix_pre", "w_in", "conv_dw_w", "conv_dw_b", "conv_ln_g", "conv_ln_b", "w_conv_branch",
           "b_conv_branch", "w_att_branch", "w_out", "norm_mix_post", "norm_ffn_pre", "w_ffn_up", "w_ffn_down",
           "norm_ffn_post"]
COL_SHARDED = ["w_conv_branch", "w_att_branch"]
TRANSPOSED = ["w_in", "w_ffn_up"]
VECTORS = ["norm_mix_pre", "conv_dw_b", "conv_ln_g", "conv_ln_b", "b_conv_branch", "norm_mix_post",
           "norm_ffn_pre", "norm_ffn_post"]


def _cols_to_full(g):
    return g.transpose(1, 0, 2).reshape(g.shape[1], N_DEV * g.shape[2])


def _full_to_cols(f):
    return f.reshape(f.shape[0], N_DEV, f.shape[1] // N_DEV).transpose(1, 0, 2)


PACK_ROWS = 7


def _pack_vectors(vecs, extra=None):
    parts = [vecs[nm].reshape(-1) for nm in VECTORS]
    parts.append(jnp.zeros((1,), F32) if extra is None else extra.reshape(1))
    used = sum(p.size for p in parts)
    parts.append(jnp.zeros((PACK_ROWS * D_MODEL - used,), F32))
    return jnp.concatenate(parts).reshape(PACK_ROWS, D_MODEL)


def _unpack_vectors(packed, sizes):
    flat, out, at = packed.reshape(-1), {}, 0
    for nm in VECTORS:
        out[nm] = flat[at:at + sizes[nm]]
        at += sizes[nm]
    return out, flat[at]


def kernel(x, norm_mix_pre, w_in, conv_dw_w, conv_dw_b, conv_ln_g, conv_ln_b, w_conv_branch, b_conv_branch, w_att_branch, w_out, norm_mix_post, norm_ffn_pre, w_ffn_up, w_ffn_down, norm_ffn_post, loss_target, m_norm_mix_pre, m_w_in, m_conv_dw_w, m_conv_dw_b, m_conv_ln_g, m_conv_ln_b, m_w_conv_branch, m_b_conv_branch, m_w_att_branch, m_w_out, m_norm_mix_post, m_norm_ffn_pre, m_w_ffn_up, m_w_ffn_down, m_norm_ffn_post, v_norm_mix_pre, v_w_in, v_conv_dw_w, v_conv_dw_b, v_conv_ln_g, v_conv_ln_b, v_w_conv_branch, v_b_conv_branch, v_w_att_branch, v_w_out, v_norm_mix_post, v_norm_ffn_pre, v_w_ffn_up, v_w_ffn_down, v_norm_ffn_post):
    ws = dict(zip(WEIGHTS, [norm_mix_pre, w_in, conv_dw_w, conv_dw_b, conv_ln_g, conv_ln_b, w_conv_branch,
                            b_conv_branch, w_att_branch, w_out, norm_mix_post, norm_ffn_pre, w_ffn_up, w_ffn_down,
                            norm_ffn_post]))
    ms = dict(zip(WEIGHTS, [m_norm_mix_pre, m_w_in, m_conv_dw_w, m_conv_dw_b, m_conv_ln_g, m_conv_ln_b,
                            m_w_conv_branch, m_b_conv_branch, m_w_att_branch, m_w_out, m_norm_mix_post,
                            m_norm_ffn_pre, m_w_ffn_up, m_w_ffn_down, m_norm_ffn_post]))
    vs = dict(zip(WEIGHTS, [v_norm_mix_pre, v_w_in, v_conv_dw_w, v_conv_dw_b, v_conv_ln_g, v_conv_ln_b,
                            v_w_conv_branch, v_b_conv_branch, v_w_att_branch, v_w_out, v_norm_mix_post,
                            v_norm_ffn_pre, v_w_ffn_up, v_w_ffn_down, v_norm_ffn_post]))

    dw_block = jnp.pad(conv_dw_w, ((0, 1), (0, 0)))
    g_in, g_dw = _exchange_call("gather_first", _relay_gather_exchange([w_in.T.astype(BF16), dw_block]))
    full = {"w_in": _full_weight("w_in", g_in), "conv_dw_w": _cols_to_full(g_dw)}
    for nm in VECTORS:
        full[nm] = ws[nm].reshape(1, -1)

    def as_kept(nm, a):
        return a.T if nm in TRANSPOSED else a

    ride_along = []
    loss_local, grad_x, received, updated, grads = _local_step(
        x[0], loss_target[0], full, {nm: as_kept(nm, ws[nm]).astype(BF16) for nm in LATE},
        {nm: tuple(as_kept(nm, a[nm]) for a in (ws, ms, vs)) for nm in ride_along})

    small = _exchange_call("gather_small_grads", _gather_exchange(
        [_pack_vectors(grads, extra=loss_local), grads["conv_dw_w"]]))
    out_g, out_d, out_m, out_v = {}, {}, {}, {}
    for nm in LATE + ["w_in"]:
        res = updated[nm] if nm in updated else _sum_adamw(
            "adamw_" + nm, received[nm], *[as_kept(nm, a[nm]) for a in (ws, ms, vs)])
        out_g[nm], out_d[nm], out_m[nm], out_v[nm] = [as_kept(nm, r) for r in res]
    sizes = {nm: ws[nm].size for nm in VECTORS}
    vec = _sum_adamw("adamw_vectors", small[0], _pack_vectors(ws), _pack_vectors(ms), _pack_vectors(vs))
    for res, dst in zip(vec, (out_g, out_d, out_m, out_v)):
        dst.update(_unpack_vectors(res, sizes)[0])
    loss = _unpack_vectors(vec[0], sizes)[1]
    dw_full = _sum_parts("sum_dw_grads", small[1])
    me = _slot(*_place())
    dw_mine = lax.dynamic_slice(dw_full, (0, me * (CONV_DIM // N_DEV)), (CONV_WIDTH, CONV_DIM // N_DEV))
    nm = "conv_dw_w"
    out_g[nm], out_d[nm], out_m[nm], out_v[nm] = _sum_adamw("adamw_dw", dw_mine[None], ws[nm], ms[nm], vs[nm])

    outs = [loss, grad_x[None]]
    for group in (out_g, out_d, out_m, out_v):
        outs += [group[nm] for nm in WEIGHTS]
    return tuple(outs)
```

```python
import math

import jax
import jax.numpy as jnp
from jax import lax
from jax.experimental import pallas as pl
from jax.experimental.pallas import tpu as pltpu

F32 = jnp.float32
BF16 = jnp.bfloat16

N_DEV = 8
D_MODEL = 1024
CONV_DIM = 512
CONV_WIDTH = 31
N_HEADS = 8
HEAD_DIM = 64
ATT_DIM = N_HEADS * HEAD_DIM
D_FF = 2816
EPS = 1e-6
IN_SPLITS = (0, 1024, 1536, 2048, 2560, 3584, 4608)

ADAM_LR = 0.001
ADAM_B1 = 0.9
ADAM_B2 = 0.999
ADAM_EPS = 1e-08
ADAM_WD = 0.01
ADAM_STEP = 10

LANES = 128
SUBLANES = 8
HALO = 32
ATT_TILE = 256
ATT_PART = 176
DEAD_SUM = -120.0
VMEM_LIMIT = 56 * 1024 * 1024
MESH = pl.DeviceIdType.MESH
ANY = pl.BlockSpec(memory_space=pl.ANY)


def _pick(dim, target, align=LANES):
    t = min(dim, target)
    t -= t % align
    while t >= align:
        if dim % t == 0:
            return t
        t -= align
    return dim


def _params(semantics, collective_id=None):
    return pltpu.CompilerParams(dimension_semantics=semantics, vmem_limit_bytes=VMEM_LIMIT,
                                collective_id=collective_id)


def _tn_matmul(a, b, *, name):
    return _pieces_tn_matmul([a], b, name=name, tj=_pick(a.shape[1], 1408))


def _pieces_tn_matmul(pieces, b, *, name, tj=512, exchange=None):
    s, n = b.shape
    counts = [p.shape[1] // tj for p in pieces]
    starts = [sum(counts[:i]) for i in range(len(pieces))]
    assert all(p.shape == (s, c * tj) for p, c in zip(pieces, counts))
    x_arrs, x_shape, x_scratch, _, x_id = exchange or NO_EXCHANGE
    nx, n_in = len(x_arrs), len(pieces) + 1

    def body(*refs):
        b_ref, o_ref = refs[n_in - 1], refs[n_in + nx]
        finish_exchange = _carry_exchange(exchange or NO_EXCHANGE, refs, n_in, 1, *_sweep_marks(sum(counts)))
        j = pl.program_id(0)
        for p_ref, first, count in zip(refs, starts, counts):
            @pl.when((j >= first) & (j < first + count))
            def _():
                o_ref[...] = lax.dot_general(p_ref[...].astype(BF16), b_ref[...], TN,
                                             preferred_element_type=F32).astype(o_ref.dtype)
        finish_exchange()

    def piece_spec(first, count):
        return pl.BlockSpec((s, tj), lambda j: (0, jnp.clip(j - first, 0, count - 1)))

    res = pl.pallas_call(
        body, name=name, grid=(sum(counts),),
        in_specs=[piece_spec(f, c) for f, c in zip(starts, counts)]
        + [pl.BlockSpec((s, n), lambda j: (0, 0), pipeline_mode=pl.Buffered(1))] + [ANY] * nx,
        out_specs=[pl.BlockSpec((tj, n), lambda j: (j, 0))] + [ANY] * nx,
        out_shape=[jax.ShapeDtypeStruct((sum(counts) * tj, n), BF16)] + x_shape, scratch_shapes=x_scratch,
        compiler_params=_params(("arbitrary",), x_id),
    )(*pieces, b, *x_arrs)
    return res[0] if exchange is None else (res[0], res[1:])


NO_EXCHANGE = ([], [], [], None, None)


def _sweep_marks(nt):
    i = pl.program_id(0)
    return i == 0, i == nt - 1, i == nt - 1, i == nt // 2


def _rowwise(name, fn, rows, bcasts, row_outs, red_outs=(), tm=256, exchange=NO_EXCHANGE):
    s = rows[0].shape[0]
    tm = _pick(s, tm, 16)
    nt = s // tm
    resident = pl.Buffered(1)
    nr, nb, no, nd = len(rows), len(bcasts), len(row_outs), len(red_outs)
    x_arrs, x_shape, x_scratch, _, x_id = exchange
    nx = len(x_arrs)
    first_out = nr + nb + nx

    def body(*refs):
        finish_exchange = _carry_exchange(exchange, refs, nr + nb, no + nd, *_sweep_marks(nt))
        ins = [r[...] for r in refs[:nr + nb]]
        outs, reds = fn(*ins)
        for ref, val in zip(refs[first_out:first_out + no], outs):
            ref[...] = val.astype(ref.dtype)
        i = pl.program_id(0)
        for ref, val in zip(refs[first_out + no:first_out + no + nd], reds):
            @pl.when(i == 0)
            def _():
                ref[...] = val

            @pl.when(i > 0)
            def _():
                ref[...] += val
        finish_exchange()

    def row_spec(a):
        assert a.shape[-2] % nt == 0, (name, a.shape, nt)
        if len(a.shape) == 3:
            return pl.BlockSpec((a.shape[0], a.shape[1] // nt, a.shape[2]), lambda i: (0, i, 0))
        return pl.BlockSpec((a.shape[0] // nt, a.shape[1]), lambda i: (i, 0))

    in_specs = [row_spec(r) for r in rows]
    in_specs += [pl.BlockSpec(b.shape, lambda i: (0, 0), pipeline_mode=resident) for b in bcasts]
    out_specs = [row_spec(o) for o in row_outs]
    out_specs += [pl.BlockSpec(d.shape, lambda i: (0, 0)) for d in red_outs]
    return pl.pallas_call(
        body, name=name, grid=(nt,), in_specs=in_specs + [ANY] * nx, out_specs=out_specs + [ANY] * nx,
        out_shape=list(row_outs) + list(red_outs) + x_shape, scratch_shapes=x_scratch,
        compiler_params=_params(("arbitrary",), x_id),
    )(*rows, *bcasts, *x_arrs)


def _sds(shape, dtype=F32):
    return jax.ShapeDtypeStruct(shape, dtype)


def _rms(x, g):
    y = x * lax.rsqrt(jnp.mean(x * x, axis=-1, keepdims=True) + EPS)
    return y * g


def _silu(x):
    return x * jax.nn.sigmoid(x)


def _swiglu(g, u):
    return _silu(g) * u


def _ln_silu(u, g, b):
    mu = jnp.mean(u, axis=-1, keepdims=True)
    var = jnp.mean(jnp.square(u - mu), axis=-1, keepdims=True)
    return _silu((u - mu) * lax.rsqrt(var + EPS) * g + b)


def _merge(conv_pre, att_out, g_conv, g_att, b_cb):
    return jax.nn.sigmoid(g_conv) * (conv_pre + b_cb) + jax.nn.sigmoid(g_att) * att_out


def _glu(t):
    return t[:, :CONV_DIM] * jax.nn.sigmoid(t[:, CONV_DIM:])


def _shifted_reader(buf, shifted, tm):
    for b in range(1, SUBLANES):
        shifted[b - 1, :, :] = buf[pl.ds(b, tm + HALO - SUBLANES), :]

    def read(o):
        a, b = divmod(o, SUBLANES)
        return buf[pl.ds(SUBLANES * a, tm), :] if b == 0 else shifted[b - 1, pl.ds(SUBLANES * a, tm), :]

    return read


def _conv_fwd(conv_in, w_pad, b, ln_g, ln_b, exchange, tm=256):
    s = conv_in.shape[0]
    tm = _pick(s, tm, HALO)
    ratio = tm // HALO
    x_arrs, x_shape, x_scratch, _, x_id = exchange
    nx = len(x_arrs)

    def body(*refs):
        main_ref, halo_ref, w_ref, b_ref, g_ref, be_ref = refs[:6]
        u3_ref, u1_ref = refs[6 + nx:8 + nx]
        buf, shifted = refs[-2:]
        finish_exchange = _carry_exchange(exchange, refs, 6, 2, *_sweep_marks(s // tm))
        i = pl.program_id(0)
        buf[0:HALO, :] = _glu(halo_ref[...]) * (i > 0).astype(F32)
        buf[HALO:HALO + tm, :] = _glu(main_ref[...])
        read = _shifted_reader(buf, shifted, tm)
        acc = jnp.zeros((tm, CONV_DIM), F32) + b_ref[...]
        for j in range(CONV_WIDTH):
            acc = acc + w_ref[j:j + 1, :] * read(HALO - (CONV_WIDTH - 1) + j)
        u1_ref[...] = acc
        u3_ref[...] = _ln_silu(acc, g_ref[...], be_ref[...]).astype(u3_ref.dtype)
        finish_exchange()

    res = pl.pallas_call(
        body, name="conv_fwd", grid=(s // tm,),
        in_specs=[pl.BlockSpec((tm, 2 * CONV_DIM), lambda i: (i, 0)),
                  pl.BlockSpec((HALO, 2 * CONV_DIM), lambda i: (jnp.maximum(i * ratio - 1, 0), 0)),
                  pl.BlockSpec(w_pad.shape, lambda i: (0, 0)),
                  pl.BlockSpec(b.shape, lambda i: (0, 0)),
                  pl.BlockSpec(ln_g.shape, lambda i: (0, 0)),
                  pl.BlockSpec(ln_b.shape, lambda i: (0, 0))] + [ANY] * nx,
        out_specs=[pl.BlockSpec((tm, CONV_DIM), lambda i: (i, 0)),
                   pl.BlockSpec((tm, CONV_DIM), lambda i: (i, 0))] + [ANY] * nx,
        out_shape=[_sds((s, CONV_DIM), BF16), _sds((s, CONV_DIM), F32)] + x_shape,
        scratch_shapes=x_scratch + [pltpu.VMEM((tm + HALO, CONV_DIM), F32),
                                    pltpu.VMEM((SUBLANES - 1, tm + HALO - SUBLANES, CONV_DIM), F32)],
        compiler_params=_params(("arbitrary",), x_id),
    )(conv_in, conv_in, w_pad, b, ln_g, ln_b, *x_arrs)
    return res[0], res[1], res[2:]


def _conv_bwd(conv_in, u1, du3, ln_g, ln_b, w_pad, exchange, tm=256):
    s = conv_in.shape[0]
    tm = _pick(s, tm, HALO)
    ratio = tm // HALO
    nt = s // tm
    last_halo = s // HALO - 1
    x_arrs, x_shape, x_scratch, _, x_id = exchange
    nx = len(x_arrs)

    def body(*refs):
        main_ref, halo_ref, u1_ref, u1n_ref, du3_ref, du3n_ref, g_ref, be_ref, w_ref = refs[:9]
        dci_ref, dw_ref, db_ref, dg_ref, dbe_ref = refs[9 + nx:14 + nx]
        ubuf, dbuf, ushift, dshift = refs[-4:]
        finish_exchange = _carry_exchange(exchange, refs, 9, 5, *_sweep_marks(nt))
        i = pl.program_id(0)
        main = main_ref[...]
        a = main[:, :CONV_DIM]
        sb = jax.nn.sigmoid(main[:, CONV_DIM:])
        ubuf[0:HALO, :] = _glu(halo_ref[...]) * (i > 0).astype(F32)
        ubuf[HALO:HALO + tm, :] = a * sb

        def ln_bwd(u1t, du3t):
            _, vjp = jax.vjp(_ln_silu, u1t, g_ref[...], be_ref[...])
            return vjp(du3t)

        du, dg, dbe = ln_bwd(u1_ref[...], du3_ref[...])
        dbuf[0:tm, :] = du
        dbuf[tm:tm + HALO, :] = ln_bwd(u1n_ref[...], du3n_ref[...])[0] * (i < nt - 1).astype(F32)

        @pl.when(i == 0)
        def _():
            dw_ref[...] = jnp.zeros_like(dw_ref)
            db_ref[...] = jnp.zeros_like(db_ref)
            dg_ref[...] = jnp.zeros_like(dg_ref)
            dbe_ref[...] = jnp.zeros_like(dbe_ref)

        dg_ref[...] += dg
        dbe_ref[...] += dbe

        read_u = _shifted_reader(ubuf, ushift, tm)
        read_d = _shifted_reader(dbuf, dshift, tm)
        du0 = jnp.zeros((tm, CONV_DIM), F32)
        for j in range(CONV_WIDTH):
            du0 = du0 + w_ref[j:j + 1, :] * read_d(CONV_WIDTH - 1 - j)
            dw_ref[j:j + 1, :] += jnp.sum(du * read_u(HALO - (CONV_WIDTH - 1) + j), axis=0, keepdims=True)
        db_ref[...] += jnp.sum(du, axis=0, keepdims=True)
        dci_ref[:, :CONV_DIM] = (du0 * sb).astype(dci_ref.dtype)
        dci_ref[:, CONV_DIM:] = (du0 * a * sb * (1.0 - sb)).astype(dci_ref.dtype)
        finish_exchange()

    res = pl.pallas_call(
        body, name="conv_bwd", grid=(nt,),
        in_specs=[pl.BlockSpec((tm, 2 * CONV_DIM), lambda i: (i, 0)),
                  pl.BlockSpec((HALO, 2 * CONV_DIM), lambda i: (jnp.maximum(i * ratio - 1, 0), 0))]
        + [pl.BlockSpec((tm, CONV_DIM), lambda i: (i, 0)),
           pl.BlockSpec((HALO, CONV_DIM), lambda i: (jnp.minimum((i + 1) * ratio, last_halo), 0))] * 2
        + [pl.BlockSpec((1, CONV_DIM), lambda i: (0, 0))] * 2 + [pl.BlockSpec(w_pad.shape, lambda i: (0, 0))]
        + [ANY] * nx,
        out_specs=[pl.BlockSpec((tm, 2 * CONV_DIM), lambda i: (i, 0)),
                   pl.BlockSpec(w_pad.shape, lambda i: (0, 0))]
        + [pl.BlockSpec((1, CONV_DIM), lambda i: (0, 0))] * 3 + [ANY] * nx,
        out_shape=[_sds((s, 2 * CONV_DIM), BF16), _sds(w_pad.shape)] + [_sds((1, CONV_DIM))] * 3 + x_shape,
        scratch_shapes=x_scratch + [pltpu.VMEM((tm + HALO, CONV_DIM), F32)] * 2
        + [pltpu.VMEM((SUBLANES - 1, tm + HALO - SUBLANES, CONV_DIM), F32)] * 2,
        compiler_params=_params(("arbitrary",), x_id),
    )(conv_in, conv_in, u1, u1, du3, du3, ln_g, ln_b, w_pad, *x_arrs)
    return res[:5], res[5:]


def _logsig_neg(z):
    return jnp.minimum(-z, 0.0) - jnp.log(1.0 + jnp.exp(-jnp.abs(z)))


def _split_dot(val, tri):
    hi = val.astype(BF16)
    lo = (val - hi.astype(F32)).astype(BF16)
    return jnp.dot(hi, tri, preferred_element_type=F32) + jnp.dot(lo, tri, preferred_element_type=F32)


def _attn_masks(t, later):
    row = lax.broadcasted_iota(jnp.int32, (t, t), 0)
    col = lax.broadcasted_iota(jnp.int32, (t, t), 1)
    tri = jnp.where(row > col if later else row <= col, 1.0, 0.0).astype(BF16)
    return col < row, tri


def _grid_marks(h, nq):
    hh, i = pl.program_id(0), pl.program_id(1)
    return ((hh == 0) & (i == 0), (hh == h - 1) & (i == nq // 2), (hh == h - 1) & (i == nq - 1),
            (hh == h // 2) & (i == 0))


def _head_masks(shape):
    lane = lax.broadcasted_iota(jnp.int32, shape, len(shape) - 1)
    return lane < HEAD_DIM, lane >= HEAD_DIM


def _per_head(blk):
    m0, m1 = _head_masks(blk.shape)
    zero = jnp.zeros_like(blk)
    return jnp.where(m0, blk, zero), jnp.where(m1, blk, zero)


NT = (((1,), (1,)), ((), ()))
TN = (((0,), (0,)), ((), ()))


def _with_top(whole, top):
    rows = top.shape[0]
    return top if rows == whole.shape[0] else jnp.concatenate([top, whole[rows:]], axis=0)


def _attn_fwd(q, k, v, exchange):
    s = q.shape[0]
    hp = q.shape[1] // LANES
    t = ATT_TILE
    scale = 1.0 / math.sqrt(HEAD_DIM)
    x_arrs, x_shape, x_scratch, _, x_id = exchange
    nx = len(x_arrs)

    def body(*refs):
        q_ref, k_ref, v_ref = refs[:3]
        o_ref, lt_ref, nb_ref = refs[3 + nx:6 + nx]
        finish_exchange = _carry_exchange(exchange, refs, 3, 3, *_grid_marks(hp, s // t))
        i = pl.program_id(1)
        qs = _per_head((q_ref[...].astype(F32) * scale).astype(BF16))
        causal, tri = _attn_masks(t, later=True)

        def step(kb, carry, masked, rows):
            cs, acc = carry
            off = pl.multiple_of(kb * t, t)
            kblk = k_ref[pl.ds(off, t), :]
            vs = _per_head(v_ref[pl.ds(off, t), :])
            acc_top = acc[:rows]
            new_cs = []
            for hd in range(2):
                z = lax.dot_general(qs[hd][:rows], kblk, NT, preferred_element_type=F32)
                l = _logsig_neg(z)
                if masked:
                    l = jnp.where(causal, l, 0.0)
                e = z + l + _split_dot(l, tri) + cs[hd][:rows]
                if masked:
                    e = jnp.where(causal, e, -1e30)
                acc_top = acc_top + jnp.dot(jnp.exp(e).astype(BF16), vs[hd], preferred_element_type=F32)
                new_cs.append(_with_top(cs[hd], cs[hd][:rows] + jnp.sum(l, axis=1, keepdims=True)))
            return tuple(new_cs), _with_top(acc, acc_top)

        zero = jnp.zeros((t, 1), F32)
        carry = step(i, ((zero, zero), jnp.zeros((t, LANES), F32)), True, t)

        def live(cs, lo, hi):
            return jnp.maximum(jnp.max(cs[0][lo:hi]), jnp.max(cs[1][lo:hi])) > DEAD_SUM

        def more(state):
            n, _, (cs, _) = state
            return (n < i) & live(cs, 0, t)

        def sweep(state):
            n, n_full, cr = state
            whole = live(cr[0], ATT_PART, t)
            cr = lax.cond(whole, lambda c: step(i - 1 - n, c, False, t), lambda c: step(i - 1 - n, c, False, ATT_PART), cr)
            return n + 1, n_full + whole.astype(jnp.int32), cr

        n_blocks, n_full, carry = lax.while_loop(more, sweep, (jnp.int32(0), jnp.int32(0), carry))
        m0, _ = _head_masks((t, LANES))
        lt_ref[...] = jnp.where(m0, carry[0][0], carry[0][1])
        o_ref[...] = carry[1].astype(o_ref.dtype)
        nb_ref[0, pl.program_id(0), i] = n_blocks.astype(F32)
        nb_ref[1, pl.program_id(0), i] = n_full.astype(F32)
        finish_exchange()

    res = pl.pallas_call(
        body, name="attn_fwd", grid=(hp, s // t),
        in_specs=[pl.BlockSpec((t, LANES), lambda p, i: (i, p)),
                  pl.BlockSpec((s, LANES), lambda p, i: (0, p)),
                  pl.BlockSpec((s, LANES), lambda p, i: (0, p))] + [ANY] * nx,
        out_specs=[pl.BlockSpec((t, LANES), lambda p, i: (i, p)),
                   pl.BlockSpec((None, t, LANES), lambda p, i: (p, i, 0)),
                   pl.BlockSpec(memory_space=pltpu.SMEM)] + [ANY] * nx,
        out_shape=[_sds(q.shape, BF16), _sds((hp, s, LANES), F32), _sds((2, hp, s // t), F32)] + x_shape,
        scratch_shapes=x_scratch,
        compiler_params=_params(("arbitrary", "arbitrary"), x_id),
    )(q, k, v, *x_arrs)
    return res[0], res[1], res[2], res[3:]


def _attn_bwd(q, k, v, do, ltot, n_blocks, exchange):
    s = q.shape[0]
    hp = q.shape[1] // LANES
    t = ATT_TILE
    scale = 1.0 / math.sqrt(HEAD_DIM)
    x_arrs, x_shape, x_scratch, _, x_id = exchange
    nx = len(x_arrs)

    def body(*refs):
        q_ref, k_ref, v_ref, do_ref, lt_ref, nb_ref = refs[:6]
        dq_ref, dk_ref, dv_ref = refs[6 + nx:9 + nx]
        finish_exchange = _carry_exchange(exchange, refs, 6, 3, *_grid_marks(hp, s // t))
        i = pl.program_id(1)
        n_blocks = jnp.clip(nb_ref[0, pl.program_id(0), i].astype(jnp.int32), 0, i)
        n_full = jnp.clip(nb_ref[1, pl.program_id(0), i].astype(jnp.int32), 0, n_blocks)

        @pl.when(i == 0)
        def _():
            dk_ref[...] = jnp.zeros_like(dk_ref)
            dv_ref[...] = jnp.zeros_like(dv_ref)

        qb = q_ref[...]
        qm = _per_head(qb)
        qs = _per_head((qb.astype(F32) * scale).astype(BF16))
        dos = _per_head(do_ref[...])
        lts = (lt_ref[:, 0:1], lt_ref[:, HEAD_DIM:HEAD_DIM + 1])
        causal, tri = _attn_masks(t, later=False)

        def step(kb, carry, masked, rows):
            cls, cgs, dq = carry
            off = pl.multiple_of(kb * t, t)
            kblk = k_ref[pl.ds(off, t), :]
            vblk = v_ref[pl.ds(off, t), :]
            ks = _per_head(kblk)
            dq_top = dq[:rows]
            dk = jnp.zeros((t, LANES), F32)
            dv = jnp.zeros((t, LANES), F32)
            new_cls, new_cgs = [], []
            for hd in range(2):
                z = lax.dot_general(qs[hd][:rows], kblk, NT, preferred_element_type=F32)
                l = _logsig_neg(z)
                if masked:
                    l = jnp.where(causal, l, 0.0)
                e = z + l + ((lts[hd][:rows] - cls[hd][:rows]) - _split_dot(l, tri))
                if masked:
                    e = jnp.where(causal, e, -1e30)
                a = jnp.exp(e)
                g = lax.dot_general(dos[hd][:rows], vblk, NT, preferred_element_type=F32) * a
                p = cgs[hd][:rows] + jnp.dot(g.astype(BF16), tri, preferred_element_type=F32) - g
                el = jnp.exp(l)
                dz = g * el - p * (1.0 - el)
                if masked:
                    dz = jnp.where(causal, dz, 0.0)
                dzb = (dz * scale).astype(BF16)
                dq_top = dq_top + jnp.dot(dzb, ks[hd], preferred_element_type=F32)
                dk = dk + lax.dot_general(dzb, qm[hd][:rows], TN, preferred_element_type=F32)
                dv = dv + lax.dot_general(a.astype(BF16), dos[hd][:rows], TN, preferred_element_type=F32)
                new_cls.append(_with_top(cls[hd], cls[hd][:rows] + jnp.sum(l, axis=1, keepdims=True)))
                new_cgs.append(_with_top(cgs[hd], cgs[hd][:rows] + jnp.sum(g, axis=1, keepdims=True)))
            dk_ref[pl.ds(off, t), :] += dk
            dv_ref[pl.ds(off, t), :] += dv
            return tuple(new_cls), tuple(new_cgs), _with_top(dq, dq_top)

        zero = jnp.zeros((t, 1), F32)
        init = ((zero, zero), (zero, zero), jnp.zeros((t, LANES), F32))
        carry = lax.fori_loop(i - n_blocks, i - n_full, lambda kb, cr: step(kb, cr, False, ATT_PART), init)
        carry = lax.fori_loop(i - n_full, i, lambda kb, cr: step(kb, cr, False, t), carry)
        carry = step(i, carry, True, t)
        dq_ref[...] = carry[2]
        finish_exchange()

    blk = pl.BlockSpec((t, LANES), lambda p, i: (i, p))
    whole = pl.BlockSpec((s, LANES), lambda p, i: (0, p))
    res = pl.pallas_call(
        body, name="attn_bwd", grid=(hp, s // t),
        in_specs=[blk, whole, whole, blk, pl.BlockSpec((None, t, LANES), lambda p, i: (p, i, 0)),
                  pl.BlockSpec(memory_space=pltpu.SMEM)] + [ANY] * nx,
        out_specs=[blk, whole, whole] + [ANY] * nx,
        out_shape=[_sds(q.shape)] * 3 + x_shape,
        scratch_shapes=x_scratch,
        compiler_params=_params(("arbitrary", "arbitrary"), x_id),
    )(q, k, v, do, ltot, n_blocks, *x_arrs)
    return res[0], res[1], res[2], res[3:]


LATE = ["w_conv_branch", "w_att_branch", "w_out", "w_ffn_up", "w_ffn_down"]


def _full_weight(name, gathered):
    return _cols_to_full(gathered) if name in COL_SHARDED else gathered.reshape(-1, gathered.shape[2])


def _grad_slabs(name, grad):
    return _full_to_cols(grad) if name in COL_SHARDED else grad.reshape(N_DEV, -1, grad.shape[1])


def _side_slabs(name, grad):
    slabs = _grad_slabs(name, grad)
    return slabs.reshape((4, 2) + slabs.shape[1:])


def _local_step(x, target, w, late_blocks, opt):
    s = x.shape[0]
    w = dict(w)
    g1, g2, g3, g4 = w["norm_mix_pre"], w["norm_mix_post"], w["norm_ffn_pre"], w["norm_ffn_post"]

    w_in = w["w_in"]

    def proj_fn(xt, g1_, w_in_t):
        h = _rms(xt, g1_).astype(BF16)
        proj = lax.dot_general(h, w_in_t, NT, preferred_element_type=F32)
        return (h, *[proj[:, IN_SPLITS[n]:IN_SPLITS[n + 1]] for n in range(6)]), ()

    mix_weights = ["w_conv_branch", "w_att_branch", "w_out"]
    h1, conv_in, q, k, v, g_conv, g_att, g_out = _rowwise(
        "norm_proj", proj_fn, [x], [g1, w_in],
        [_sds((s, D_MODEL), BF16), _sds((s, 2 * CONV_DIM)), _sds((s, ATT_DIM), BF16), _sds((s, ATT_DIM), BF16),
         _sds((s, ATT_DIM), BF16), _sds((s, D_MODEL), BF16), _sds((s, D_MODEL), BF16)], tm=512,
        exchange=_relay_gather_exchange([late_blocks["w_out"]]))

    u3, u1, g_branches = _conv_fwd(conv_in, w["conv_dw_w"], w["conv_dw_b"], w["conv_ln_g"], w["conv_ln_b"],
                                   _relay_gather_exchange([late_blocks[nm] for nm in mix_weights[:2]]))
    for nm, g in zip(mix_weights, [*g_branches, g_out]):
        w[nm] = _full_weight(nm, g)
    att, ltot, n_blocks, (g_up,) = _attn_fwd(q, k, v, _gather_exchange([late_blocks["w_ffn_up"]]))
    w["w_ffn_up"] = _full_weight("w_ffn_up", g_up)

    def merge_fn(u3t, at, gc, ga, xt, w_cb, w_ab, b_cb, w_out, g2_, g3_):
        cp = jnp.dot(u3t, w_cb, preferred_element_type=F32)
        ao = jnp.dot(at, w_ab, preferred_element_type=F32)
        mg = _merge(cp, ao, gc.astype(F32), ga.astype(F32), b_cb).astype(BF16)
        mix_ = jnp.dot(mg, w_out, preferred_element_type=F32)
        x2_ = xt + _rms(mix_, g2_)
        return (mg, cp, ao, mix_, x2_, _rms(x2_, g3_)), ()

    half = D_MODEL // 2
    down_block = late_blocks["w_ffn_down"]
    merged, conv_pre, att_out, mix, x2, h2, g_left = _rowwise(
        "branch_merge_mix", merge_fn, [u3, att, g_conv, g_att, x],
        [w["w_conv_branch"], w["w_att_branch"], w["b_conv_branch"], w["w_out"], g2, g3],
        [_sds((s, D_MODEL), BF16)] * 3 + [_sds((s, D_MODEL)), _sds((s, D_MODEL)), _sds((s, D_MODEL), BF16)], tm=512,
        exchange=_relay_gather_exchange([down_block[:, :half]]))

    def ffn_up_fn(ht, w_up_t):
        gu_ = lax.dot_general(ht, w_up_t, NT, preferred_element_type=F32)
        return (gu_, _swiglu(gu_[:, :D_FF], gu_[:, D_FF:])), ()

    gu, act, g_right = _rowwise("ffn_up", ffn_up_fn, [h2], [w["w_ffn_up"]],
                                [_sds((s, 2 * D_FF), BF16), _sds((s, D_FF), BF16)], tm=512,
                                exchange=_relay_gather_exchange([down_block[:, half:]]))
    w_down = [_full_weight("w_ffn_down", g) for g in (g_left, g_right)]

    def final_fn(at, x2t, tgt, w_left, w_right, g4_):
        ff = jnp.concatenate([jnp.dot(at, w_left, preferred_element_type=F32),
                              jnp.dot(at, w_right, preferred_element_type=F32)], axis=1)
        n4, vjp = jax.vjp(_rms, ff, g4_)
        err = x2t + n4 - tgt
        dy = err * (1.0 / D_MODEL)
        dff, dg4 = vjp(dy)
        return (dy, dff), (jnp.sum(err * err, axis=0, keepdims=True), dg4)

    dy, dff, loss_cols, d_g4 = _rowwise("ffn_down_loss", final_fn, [act, x2, target], [*w_down, g4],
                                        [_sds((s, D_MODEL)), _sds((s, D_MODEL), BF16)],
                                        [_sds((1, D_MODEL)), _sds((1, D_MODEL))], tm=512)
    loss = 0.5 * jnp.sum(loss_cols) / D_MODEL

    d_w_down = _tn_matmul(act, dff, name="d_w_down")

    def act_bwd_fn(dfft, gut, w_left, w_right):
        d_act = (lax.dot_general(dfft[:, :half], w_left, NT, preferred_element_type=F32)
                 + lax.dot_general(dfft[:, half:], w_right, NT, preferred_element_type=F32))
        gu_ = gut.astype(F32)
        _, vjp = jax.vjp(_swiglu, gu_[:, :D_FF], gu_[:, D_FF:])
        return (jnp.concatenate(vjp(d_act), axis=1),), ()

    down_slabs = _side_slabs("w_ffn_down", d_w_down)
    dgu, theirs = _rowwise("ffn_act_bwd", act_bwd_fn, [dff, gu], w_down, [_sds((s, 2 * D_FF), BF16)],
                           exchange=_pair_exchange([down_slabs]))
    down_sums = _pair_sum("pair_sum_w_ffn_down", down_slabs, theirs)
    d_w_up = _tn_matmul(dgu, h2, name="d_w_up")
    received = {}
    up_slabs = _side_slabs("w_ffn_up", d_w_up)

    def mid_bwd_fn(dgut, xt, mt, dyt, w_up_t, g2_, g3_):
        dh = jnp.dot(dgut, w_up_t, preferred_element_type=F32)
        n2, vjp2 = jax.vjp(_rms, mt, g2_)
        x2_ = xt + n2
        _, vjp3 = jax.vjp(_rms, x2_, g3_)
        dx2_, dg3 = vjp3(dh)
        dx2_ = dx2_ + dyt
        dmix_, dg2 = vjp2(dx2_)
        return (dx2_, dmix_), (dg2, dg3)

    dx2, dmix, d_g2, d_g3, received["w_ffn_down"] = _rowwise(
        "ffn_up_mid_bwd", mid_bwd_fn, [dgu, x, mix, dy], [w["w_ffn_up"], g2, g3],
        [_sds((s, D_MODEL)), _sds((s, D_MODEL), BF16)], [_sds((1, D_MODEL)), _sds((1, D_MODEL))], tm=512,
        exchange=_chip_exchange([down_sums]))
    d_w_out = _tn_matmul(merged, dmix, name="d_w_out")

    def merge_bwd_fn(dmt, cp, ao, gc, ga, w_out, w_cb, w_ab, b_cb):
        dm = lax.dot_general(dmt, w_out, NT, preferred_element_type=F32)
        _, vjp = jax.vjp(_merge, cp.astype(F32), ao.astype(F32), gc.astype(F32), ga.astype(F32), b_cb)
        dcp, dao, dgc, dga, dbias = vjp(dm)
        dcp, dao = dcp.astype(BF16), dao.astype(BF16)
        du3_ = lax.dot_general(dcp, w_cb, NT, preferred_element_type=F32)
        datt_ = lax.dot_general(dao, w_ab, NT, preferred_element_type=F32)
        return (dcp, dao, dgc, dga, du3_, datt_), (dbias,)

    d_conv_out, d_att_out, d_g_conv, d_g_att, du3, d_att, d_b_cb, theirs = _rowwise(
        "merge_bwd", merge_bwd_fn, [dmix, conv_pre, att_out, g_conv, g_att],
        [w["w_out"], w["w_conv_branch"], w["w_att_branch"], w["b_conv_branch"]],
        [_sds((s, D_MODEL), BF16)] * 4 + [_sds((s, CONV_DIM)), _sds((s, ATT_DIM), BF16)], [_sds((1, D_MODEL))], tm=512,
        exchange=_pair_exchange([up_slabs]))

    d_w_cb = _tn_matmul(u3, d_conv_out, name="d_w_conv_branch")
    d_w_ab = _tn_matmul(att, d_att_out, name="d_w_att_branch")

    dq, dk, dv, (received["w_ffn_up"],) = _attn_bwd(
        q, k, v, d_att, ltot, n_blocks, _chip_exchange([_pair_sum("pair_sum_w_ffn_up", up_slabs, theirs)]))

    mix_grads = {"w_conv_branch": d_w_cb, "w_att_branch": d_w_ab, "w_out": d_w_out}
    (d_conv_in, d_dw_w, d_dw_b, d_ln_g, d_ln_b), landed = _conv_bwd(
        conv_in, u1, du3, w["conv_ln_g"], w["conv_ln_b"], w["conv_dw_w"],
        _scatter_exchange([_grad_slabs(nm, mix_grads[nm]) for nm in mix_weights[:2]]))
    received.update(zip(mix_weights[:2], landed))

    d_proj = [d_conv_in, dq, dk, dv, d_g_conv, d_g_att]
    d_w_in, (received["w_out"],) = _pieces_tn_matmul(
        d_proj, h1, name="d_w_in", exchange=_scatter_exchange([_grad_slabs("w_out", d_w_out)]))
    in_slabs = _side_slabs("w_in", d_w_in)
    (theirs,) = _exchange_call("pair_swap_w_in", _pair_exchange([in_slabs]))

    early = list(opt)

    def pre_bwd_fn(*args):
        groups, (xt, dx2t), jobs, (w_in_t, g_) = args[:6], args[6:8], args[8:-2], args[-2:]
        dh = sum(jnp.dot(grp.astype(BF16), w_in_t[IN_SPLITS[n]:IN_SPLITS[n + 1]], preferred_element_type=F32)
                 for n, grp in enumerate(groups))
        _, vjp = jax.vjp(_rms, xt, g_)
        dx_, dg_ = vjp(dh)
        updates = [_sum_adamw_tile(*jobs[4 * n:4 * n + 4]) for n in range(len(early))]
        return (dx_ + dx2t, *[u for four in updates for u in four]), (dg_,)

    res = _rowwise(
        "proj_norm_bwd", pre_bwd_fn,
        d_proj + [x, dx2] + [a for nm in early for a in (received[nm], *opt[nm])], [w_in, g1],
        [_sds((s, D_MODEL))] + [_sds(opt[nm][0].shape) for nm in early for _ in range(4)],
        [_sds((1, D_MODEL))], tm=512, exchange=_chip_exchange([_pair_sum("pair_sum_w_in", in_slabs, theirs)]))
    grad_x, d_g1, received["w_in"] = res[0], res[-2], res[-1]
    updated = {nm: res[1 + 4 * n:5 + 4 * n] for n, nm in enumerate(early)}

    grads = {
        "norm_mix_pre": d_g1, "conv_dw_w": d_dw_w, "conv_dw_b": d_dw_b,
        "conv_ln_g": d_ln_g, "conv_ln_b": d_ln_b, "b_conv_branch": d_b_cb,
        "norm_mix_post": d_g2, "norm_ffn_pre": d_g3, "norm_ffn_post": d_g4,
    }
    return loss, grad_x, received, updated, grads


def _place():
    x, y, c = lax.axis_index("x"), lax.axis_index("y"), lax.axis_index("c")
    return x, y, c


def _slot(px, py, pc):
    return 4 * px + 2 * py + pc


def _exchange_scratch(n):
    return [pltpu.SemaphoreType.DMA((7 * n,)), pltpu.SemaphoreType.DMA((7 * n,)), pltpu.SemaphoreType.DMA((n,))]


GATHER_ID, SCATTER_ID, PAIR_ID, CHIP_ID, RELAY_ID = 0, 1, 2, 3, 4


def _handshake(peers):
    barrier = pltpu.get_barrier_semaphore()
    for peer in peers:
        pl.semaphore_signal(barrier, inc=1, device_id=peer, device_id_type=MESH)
    pl.semaphore_wait(barrier, len(peers))


def _gather_exchange(arrs):
    n = len(arrs)

    def phases(ins, outs, send_sems, recv_sems, local_sems):
        x, y, c = _place()
        me, sibling = (x, y, c), (x, y, 1 - c)
        chips = [(1 - x, y), (x, 1 - y), (1 - x, 1 - y)]

        def copy(a, kk, block, to, src=None):
            dst = outs[a].at[_slot(*block)]
            return pltpu.make_async_remote_copy(
                src_ref=dst if src is None else src, dst_ref=dst,
                send_sem=send_sems.at[a * 7 + kk], recv_sem=recv_sems.at[a * 7 + kk],
                device_id=to, device_id_type=MESH)

        mine = [pltpu.make_async_copy(ins[a], outs[a].at[_slot(*me)], local_sems.at[a]) for a in range(n)]
        first = []
        for a in range(n):
            first.append(copy(a, 0, me, sibling, src=ins[a]))
            first += [copy(a, 1 + j, me, (*chip, c), src=ins[a]) for j, chip in enumerate(chips)]
        passed = [copy(a, 4 + j, (*chip, c), sibling) for j, chip in enumerate(chips) for a in range(n)]

        def send():
            _handshake([sibling] + [(*chip, c) for chip in chips])
            for cp in mine + first:
                cp.start()

        def pass_on():
            for j, chip in enumerate(chips):
                for a in range(n):
                    copy(a, 1 + j, (*chip, c), me).wait_recv()
                    passed[j * n + a].start()

        def finish():
            for a in range(n):
                copy(a, 0, sibling, me).wait_recv()
                for j, chip in enumerate(chips):
                    copy(a, 4 + j, (*chip, 1 - c), me).wait_recv()
            for cp in first + passed:
                cp.wait_send()
            for cp in mine:
                cp.wait()

        return [send, pass_on, finish]

    return list(arrs), [_sds((N_DEV,) + a.shape, a.dtype) for a in arrs], _exchange_scratch(n), phases, GATHER_ID


def _relay_gather_exchange(arrs):
    n = len(arrs)
    per = 8

    def phases(ins, outs, send_sems, recv_sems, local_sems):
        x, y, c = _place()
        me, sibling = (x, y, c), (x, y, 1 - c)
        beside, below, across = (1 - x, y, c), (x, 1 - y, c), (1 - x, 1 - y, c)

        def copy(a, kk, block, to, src=None, rows=None):
            where = _slot(*block) if rows is None else (_slot(*block), rows)
            dst = outs[a].at[where]
            return pltpu.make_async_remote_copy(
                src_ref=dst if src is None else src, dst_ref=dst,
                send_sem=send_sems.at[a * per + kk], recv_sem=recv_sems.at[a * per + kk],
                device_id=to, device_id_type=MESH)

        def halves(a):
            h = ins[a].shape[0] // 2
            return pl.ds(0, h), pl.ds(h, ins[a].shape[0] - h)

        mine = [pltpu.make_async_copy(ins[a], outs[a].at[_slot(*me)], local_sems.at[a]) for a in range(n)]
        first = [copy(a, kk, me, to, src=ins[a]) for a in range(n) for kk, to in enumerate([sibling, beside, below])]
        relayed = [[copy(a, 3, beside, sibling), copy(a, 5, beside, below, rows=halves(a)[0])] for a in range(n)]
        relayed += [[copy(a, 4, below, sibling), copy(a, 6, below, beside, rows=halves(a)[1])] for a in range(n)]
        passed = [copy(a, 7, across, sibling) for a in range(n)]

        def send():
            _handshake([sibling, beside, below])
            for cp in mine + first:
                cp.start()

        def relay():
            for kk, block in ((1, beside), (2, below)):
                for a in range(n):
                    copy(a, kk, block, me).wait_recv()
                    for cp in relayed[(kk - 1) * n + a]:
                        cp.start()

        def pass_on():
            for a in range(n):
                copy(a, 5, across, me, rows=halves(a)[0]).wait_recv()
                copy(a, 6, across, me, rows=halves(a)[1]).wait_recv()
                passed[a].start()

        def finish():
            for a in range(n):
                for kk, block in ((0, me), (3, beside), (4, below), (7, across)):
                    copy(a, kk, (*block[:2], 1 - c), me).wait_recv()
            for cp in first + [cp for two in relayed for cp in two] + passed:
                cp.wait_send()
            for cp in mine:
                cp.wait()

        return [send, relay, pass_on, finish]

    scratch = [pltpu.SemaphoreType.DMA((per * n,)), pltpu.SemaphoreType.DMA((per * n,)), pltpu.SemaphoreType.DMA((n,))]
    return list(arrs), [_sds((N_DEV,) + a.shape, a.dtype) for a in arrs], scratch, phases, RELAY_ID


def _scatter_exchange(arrs):
    n = len(arrs)
    flips = [(fx, fy, fc) for fx in (0, 1) for fy in (0, 1) for fc in (0, 1)][1:]

    def phases(ins, outs, send_sems, recv_sems, local_sems):
        x, y, c = _place()
        mine = _slot(x, y, c)
        local = [pltpu.make_async_copy(ins[a].at[mine], outs[a].at[mine], local_sems.at[a]) for a in range(n)]
        peers = [((1 - x) if fx else x, (1 - y) if fy else y, (1 - c) if fc else c) for fx, fy, fc in flips]

        def copy(a, kk, src_slot, dst_slot):
            return pltpu.make_async_remote_copy(
                src_ref=ins[a].at[src_slot], dst_ref=outs[a].at[dst_slot],
                send_sem=send_sems.at[a * 7 + kk], recv_sem=recv_sems.at[a * 7 + kk],
                device_id=peers[kk], device_id_type=MESH)

        sends = [copy(a, kk, _slot(*peers[kk]), mine) for a in range(n) for kk in range(7)]

        def send():
            _handshake(peers)
            for cp in local + sends:
                cp.start()

        def finish():
            for a in range(n):
                for kk in range(7):
                    copy(a, kk, mine, _slot(*peers[kk])).wait_recv()
            for cp in sends:
                cp.wait_send()
            for cp in local:
                cp.wait()

        return [send, finish]

    return list(arrs), [_sds(a.shape, a.dtype) for a in arrs], _exchange_scratch(n), phases, SCATTER_ID


def _pair_exchange(arrs):
    n = len(arrs)

    def phases(ins, outs, send_sems, recv_sems, local_sems):
        x, y, c = _place()

        def copy(a, chip, side):
            return pltpu.make_async_remote_copy(
                src_ref=ins[a].at[chip, side], dst_ref=outs[a].at[chip],
                send_sem=send_sems.at[a * 7 + chip], recv_sem=recv_sems.at[a * 7 + chip],
                device_id=(x, y, 1 - c), device_id_type=MESH)

        sends = [copy(a, chip, 1 - c) for a in range(n) for chip in range(4)]

        def send():
            _handshake([(x, y, 1 - c)])
            for cp in sends:
                cp.start()

        def finish():
            for a in range(n):
                for chip in range(4):
                    copy(a, chip, c).wait_recv()
            for cp in sends:
                cp.wait_send()

        return [send, finish]

    return list(arrs), [_sds((4,) + a.shape[2:], a.dtype) for a in arrs], _exchange_scratch(n), phases, PAIR_ID


def _chip_exchange(arrs):
    n = len(arrs)

    def phases(ins, outs, send_sems, recv_sems, local_sems):
        x, y, c = _place()
        mine = 2 * x + y
        chips = [(1 - x, y), (x, 1 - y), (1 - x, 1 - y)]
        local = [pltpu.make_async_copy(ins[a].at[mine], outs[a].at[mine], local_sems.at[a]) for a in range(n)]

        def copy(a, j, src_slot, dst_slot):
            return pltpu.make_async_remote_copy(
                src_ref=ins[a].at[src_slot], dst_ref=outs[a].at[dst_slot],
                send_sem=send_sems.at[a * 7 + j], recv_sem=recv_sems.at[a * 7 + j],
                device_id=(*chips[j], c), device_id_type=MESH)

        sends = [copy(a, j, 2 * chips[j][0] + chips[j][1], mine) for a in range(n) for j in range(3)]

        def send():
            _handshake([(*chip, c) for chip in chips])
            for cp in local + sends:
                cp.start()

        def finish():
            for a in range(n):
                for j in range(3):
                    copy(a, j, mine, 2 * chips[j][0] + chips[j][1]).wait_recv()
            for cp in sends:
                cp.wait_send()
            for cp in local:
                cp.wait()

        return [send, finish]

    return list(arrs), [_sds(a.shape, a.dtype) for a in arrs], _exchange_scratch(n), phases, CHIP_ID


def _pair_sum(name, mine, theirs):
    _, _, r, c = mine.shape

    def body(side_ref, m_ref, t_ref, o_ref):
        o_ref[...] = (m_ref[...].astype(F32) + t_ref[...].astype(F32)).astype(o_ref.dtype)

    return pl.pallas_call(
        body, name=name,
        grid_spec=pltpu.PrefetchScalarGridSpec(
            num_scalar_prefetch=1, grid=(4,),
            in_specs=[pl.BlockSpec((None, None, r, c), lambda j, side: (j, side[0], 0, 0)),
                      pl.BlockSpec((None, r, c), lambda j, side: (j, 0, 0))],
            out_specs=pl.BlockSpec((None, r, c), lambda j, side: (j, 0, 0))),
        out_shape=_sds(theirs.shape, theirs.dtype),
        compiler_params=_params(("parallel",)),
    )(lax.axis_index("c").astype(jnp.int32).reshape(1), mine, theirs)


def _exchange_call(name, exchange):
    arrs, out_shape, scratch, phases, collective_id = exchange
    n = len(arrs)

    def body(*refs):
        for step in phases(refs[:n], refs[n:2 * n], *refs[2 * n:]):
            step()

    return pl.pallas_call(body, name=name, in_specs=[ANY] * n, out_specs=[ANY] * n,
                          out_shape=out_shape, scratch_shapes=scratch,
                          compiler_params=pltpu.CompilerParams(collective_id=collective_id))(*arrs)


def _carry_exchange(exchange, refs, n_in, n_out, first, middle, last, halfway):
    arrs, _, _, phases, _ = exchange
    n = len(arrs)
    if n == 0:
        return lambda: None
    ins = refs[n_in:n_in + n]
    outs = refs[n_in + n + n_out:n_in + 2 * n + n_out]
    sems = n_in + 2 * n + n_out
    steps = phases(ins, outs, *refs[sems:sems + 3])
    pl.when(first)(steps[0])
    if len(steps) == 4:
        pl.when(halfway)(steps[1])
    if len(steps) >= 3:
        pl.when(middle)(steps[-2])
    return lambda: pl.when(last)(steps[-1])


def _adamw_math(w, g, m, v):
    m2 = ADAM_B1 * m + (1.0 - ADAM_B1) * g
    v2 = ADAM_B2 * v + (1.0 - ADAM_B2) * jnp.square(g)
    m_hat = m2 / (1.0 - ADAM_B1 ** ADAM_STEP)
    v_hat = v2 / (1.0 - ADAM_B2 ** ADAM_STEP)
    delta = -ADAM_LR * (m_hat / (jnp.sqrt(v_hat) + ADAM_EPS) + ADAM_WD * w)
    return delta, m2, v2


def _sum_adamw_tile(parts, w, m, v):
    g = parts[0].astype(F32)
    for d in range(1, parts.shape[0]):
        g = g + parts[d].astype(F32)
    return (g, *_adamw_math(w, g, m, v))


def _sum_adamw(name, parts, w, m, v, tr=256):
    p, r, c = parts.shape
    tr = _pick(r, tr, 16)

    def body(p_ref, w_ref, m_ref, v_ref, g_ref, d_ref, m2_ref, v2_ref):
        g_ref[...], d_ref[...], m2_ref[...], v2_ref[...] = _sum_adamw_tile(p_ref[...], w_ref[...], m_ref[...], v_ref[...])

    tile = pl.BlockSpec((tr, c), lambda i: (i, 0))
    return pl.pallas_call(
        body, name=name, grid=(r // tr,),
        in_specs=[pl.BlockSpec((p, tr, c), lambda i: (0, i, 0)), tile, tile, tile],
        out_specs=[tile] * 4, out_shape=[_sds((r, c))] * 4,
        compiler_params=_params(("parallel",)),
    )(parts, w, m, v)


def _sum_parts(name, parts):
    p, r, c = parts.shape

    def body(p_ref, o_ref):
        g = p_ref[0]
        for d in range(1, p):
            g = g + p_ref[d]
        o_ref[...] = g

    return pl.pallas_call(
        body, name=name, out_shape=_sds((r, c)),
        in_specs=[pl.BlockSpec(memory_space=pltpu.VMEM)], out_specs=pl.BlockSpec(memory_space=pltpu.VMEM),
    )(parts)


WEIGHTS = ["norm_mix_pre", "w_in", "conv_dw_w", "conv_dw_b", "conv_ln_g", "conv_ln_b", "w_conv_branch",
           "b_conv_branch", "w_att_branch", "w_out", "norm_mix_post", "norm_ffn_pre", "w_ffn_up", "w_ffn_down",
           "norm_ffn_post"]
COL_SHARDED = ["w_conv_branch", "w_att_branch"]
TRANSPOSED = ["w_in", "w_ffn_up"]
VECTORS = ["norm_mix_pre", "conv_dw_b", "conv_ln_g", "conv_ln_b", "b_conv_branch", "norm_mix_post",
           "norm_ffn_pre", "norm_ffn_post"]


def _cols_to_full(g):
    return g.transpose(1, 0, 2).reshape(g.shape[1], N_DEV * g.shape[2])


def _full_to_cols(f):
    return f.reshape(f.shape[0], N_DEV, f.shape[1] // N_DEV).transpose(1, 0, 2)


PACK_ROWS = 7


def _pack_vectors(vecs, extra=None):
    parts = [vecs[nm].reshape(-1) for nm in VECTORS]
    parts.append(jnp.zeros((1,), F32) if extra is None else extra.reshape(1))
    used = sum(p.size for p in parts)
    parts.append(jnp.zeros((PACK_ROWS * D_MODEL - used,), F32))
    return jnp.concatenate(parts).reshape(PACK_ROWS, D_MODEL)


def _unpack_vectors(packed, sizes):
    flat, out, at = packed.reshape(-1), {}, 0
    for nm in VECTORS:
        out[nm] = flat[at:at + sizes[nm]]
        at += sizes[nm]
    return out, flat[at]


def kernel(x, norm_mix_pre, w_in, conv_dw_w, conv_dw_b, conv_ln_g, conv_ln_b, w_conv_branch, b_conv_branch, w_att_branch, w_out, norm_mix_post, norm_ffn_pre, w_ffn_up, w_ffn_down, norm_ffn_post, loss_target, m_norm_mix_pre, m_w_in, m_conv_dw_w, m_conv_dw_b, m_conv_ln_g, m_conv_ln_b, m_w_conv_branch, m_b_conv_branch, m_w_att_branch, m_w_out, m_norm_mix_post, m_norm_ffn_pre, m_w_ffn_up, m_w_ffn_down, m_norm_ffn_post, v_norm_mix_pre, v_w_in, v_conv_dw_w, v_conv_dw_b, v_conv_ln_g, v_conv_ln_b, v_w_conv_branch, v_b_conv_branch, v_w_att_branch, v_w_out, v_norm_mix_post, v_norm_ffn_pre, v_w_ffn_up, v_w_ffn_down, v_norm_ffn_post):
    ws = dict(zip(WEIGHTS, [norm_mix_pre, w_in, conv_dw_w, conv_dw_b, conv_ln_g, conv_ln_b, w_conv_branch,
                            b_conv_branch, w_att_branch, w_out, norm_mix_post, norm_ffn_pre, w_ffn_up, w_ffn_down,
                            norm_ffn_post]))
    ms = dict(zip(WEIGHTS, [m_norm_mix_pre, m_w_in, m_conv_dw_w, m_conv_dw_b, m_conv_ln_g, m_conv_ln_b,
                            m_w_conv_branch, m_b_conv_branch, m_w_att_branch, m_w_out, m_norm_mix_post,
                            m_norm_ffn_pre, m_w_ffn_up, m_w_ffn_down, m_norm_ffn_post]))
    vs = dict(zip(WEIGHTS, [v_norm_mix_pre, v_w_in, v_conv_dw_w, v_conv_dw_b, v_conv_ln_g, v_conv_ln_b,
                            v_w_conv_branch, v_b_conv_branch, v_w_att_branch, v_w_out, v_norm_mix_post,
                            v_norm_ffn_pre, v_w_ffn_up, v_w_ffn_down, v_norm_ffn_post]))

    dw_block = jnp.pad(conv_dw_w, ((0, 1), (0, 0)))
    g_in, g_dw = _exchange_call("gather_first", _relay_gather_exchange([w_in.T.astype(BF16), dw_block]))
    full = {"w_in": _full_weight("w_in", g_in), "conv_dw_w": _cols_to_full(g_dw)}
    for nm in VECTORS:
        full[nm] = ws[nm].reshape(1, -1)

    def as_kept(nm, a):
        return a.T if nm in TRANSPOSED else a

    ride_along = ["w_ffn_up"]
    loss_local, grad_x, received, updated, grads = _local_step(
        x[0], loss_target[0], full, {nm: as_kept(nm, ws[nm]).astype(BF16) for nm in LATE},
        {nm: tuple(as_kept(nm, a[nm]) for a in (ws, ms, vs)) for nm in ride_along})

    small = _exchange_call("gather_small_grads", _gather_exchange(
        [_pack_vectors(grads, extra=loss_local), grads["conv_dw_w"]]))
    out_g, out_d, out_m, out_v = {}, {}, {}, {}
    for nm in LATE + ["w_in"]:
        res = updated[nm] if nm in updated else _sum_adamw(
            "adamw_" + nm, received[nm], *[as_kept(nm, a[nm]) for a in (ws, ms, vs)])
        out_g[nm], out_d[nm], out_m[nm], out_v[nm] = [as_kept(nm, r) for r in res]
    sizes = {nm: ws[nm].size for nm in VECTORS}
    vec = _sum_adamw("adamw_vectors", small[0], _pack_vectors(ws), _pack_vectors(ms), _pack_vectors(vs))
    for res, dst in zip(vec, (out_g, out_d, out_m, out_v)):
        dst.update(_unpack_vectors(res, sizes)[0])
    loss = _unpack_vectors(vec[0], sizes)[1]
    dw_full = _sum_parts("sum_dw_grads", small[1])
    me = _slot(*_place())
    dw_mine = lax.dynamic_slice(dw_full, (0, me * (CONV_DIM // N_DEV)), (CONV_WIDTH, CONV_DIM // N_DEV))
    nm = "conv_dw_w"
    out_g[nm], out_d[nm], out_m[nm], out_v[nm] = _sum_adamw("adamw_dw", dw_mine[None], ws[nm], ms[nm], vs[nm])

    outs = [loss, grad_x[None]]
    for group in (out_g, out_d, out_m, out_v):
        outs += [group[nm] for nm in WEIGHTS]
    return tuple(outs)
```

```python
import math

import jax
import jax.numpy as jnp
from jax import lax
from jax.experimental import pallas as pl
from jax.experimental.pallas import tpu as pltpu

F32 = jnp.float32
BF16 = jnp.bfloat16

N_DEV = 8
D_MODEL = 1024
CONV_DIM = 512
CONV_WIDTH = 31
N_HEADS = 8
HEAD_DIM = 64
ATT_DIM = N_HEADS * HEAD_DIM
D_FF = 2816
EPS = 1e-6
IN_SPLITS = (0, 1024, 1536, 2048, 2560, 3584, 4608)

ADAM_LR = 0.001
ADAM_B1 = 0.9
ADAM_B2 = 0.999
ADAM_EPS = 1e-08
ADAM_WD = 0.01
ADAM_STEP = 10

LANES = 128
SUBLANES = 8
HALO = 32
ATT_TILE = 256
ATT_PART = 176
DEAD_SUM = -120.0
VMEM_LIMIT = 56 * 1024 * 1024
MESH = pl.DeviceIdType.MESH
ANY = pl.BlockSpec(memory_space=pl.ANY)


def _pick(dim, target, align=LANES):
    t = min(dim, target)
    t -= t % align
    while t >= align:
        if dim % t == 0:
            return t
        t -= align
    return dim


def _params(semantics, collective_id=None):
    return pltpu.CompilerParams(dimension_semantics=semantics, vmem_limit_bytes=VMEM_LIMIT,
                                collective_id=collective_id)


def _tn_matmul(a, b, *, name):
    return _pieces_tn_matmul([a], b, name=name, tj=_pick(a.shape[1], 1408))


def _pieces_tn_matmul(pieces, b, *, name, tj=512, exchange=None):
    s, n = b.shape
    counts = [p.shape[1] // tj for p in pieces]
    starts = [sum(counts[:i]) for i in range(len(pieces))]
    assert all(p.shape == (s, c * tj) for p, c in zip(pieces, counts))
    x_arrs, x_shape, x_scratch, _, x_id = exchange or NO_EXCHANGE
    nx, n_in = len(x_arrs), len(pieces) + 1

    def body(*refs):
        b_ref, o_ref = refs[n_in - 1], refs[n_in + nx]
        finish_exchange = _carry_exchange(exchange or NO_EXCHANGE, refs, n_in, 1, *_sweep_marks(sum(counts)))
        j = pl.program_id(0)
        for p_ref, first, count in zip(refs, starts, counts):
            @pl.when((j >= first) & (j < first + count))
            def _():
                o_ref[...] = lax.dot_general(p_ref[...].astype(BF16), b_ref[...], TN,
                                             preferred_element_type=F32).astype(o_ref.dtype)
        finish_exchange()

    def piece_spec(first, count):
        return pl.BlockSpec((s, tj), lambda j: (0, jnp.clip(j - first, 0, count - 1)))

    res = pl.pallas_call(
        body, name=name, grid=(sum(counts),),
        in_specs=[piece_spec(f, c) for f, c in zip(starts, counts)]
        + [pl.BlockSpec((s, n), lambda j: (0, 0), pipeline_mode=pl.Buffered(1))] + [ANY] * nx,
        out_specs=[pl.BlockSpec((tj, n), lambda j: (j, 0))] + [ANY] * nx,
        out_shape=[jax.ShapeDtypeStruct((sum(counts) * tj, n), BF16)] + x_shape, scratch_shapes=x_scratch,
        compiler_params=_params(("arbitrary",), x_id),
    )(*pieces, b, *x_arrs)
    return res[0] if exchange is None else (res[0], res[1:])


NO_EXCHANGE = ([], [], [], None, None)


def _sweep_marks(nt):
    i = pl.program_id(0)
    return i == 0, i == nt - 1, i == nt - 1, i == nt // 2


def _rowwise(name, fn, rows, bcasts, row_outs, red_outs=(), tm=256, exchange=NO_EXCHANGE):
    s = rows[0].shape[0]
    tm = _pick(s, tm, 16)
    nt = s // tm
    resident = pl.Buffered(1)
    nr, nb, no, nd = len(rows), len(bcasts), len(row_outs), len(red_outs)
    x_arrs, x_shape, x_scratch, _, x_id = exchange
    nx = len(x_arrs)
    first_out = nr + nb + nx

    def body(*refs):
        finish_exchange = _carry_exchange(exchange, refs, nr + nb, no + nd, *_sweep_marks(nt))
        ins = [r[...] for r in refs[:nr + nb]]
        outs, reds = fn(*ins)
        for ref, val in zip(refs[first_out:first_out + no], outs):
            ref[...] = val.astype(ref.dtype)
        i = pl.program_id(0)
        for ref, val in zip(refs[first_out + no:first_out + no + nd], reds):
            @pl.when(i == 0)
            def _():
                ref[...] = val

            @pl.when(i > 0)
            def _():
                ref[...] += val
        finish_exchange()

    def row_spec(a):
        assert a.shape[-2] % nt == 0, (name, a.shape, nt)
        if len(a.shape) == 3:
            return pl.BlockSpec((a.shape[0], a.shape[1] // nt, a.shape[2]), lambda i: (0, i, 0))
        return pl.BlockSpec((a.shape[0] // nt, a.shape[1]), lambda i: (i, 0))

    in_specs = [row_spec(r) for r in rows]
    in_specs += [pl.BlockSpec(b.shape, lambda i: (0, 0), pipeline_mode=resident) for b in bcasts]
    out_specs = [row_spec(o) for o in row_outs]
    out_specs += [pl.BlockSpec(d.shape, lambda i: (0, 0)) for d in red_outs]
    return pl.pallas_call(
        body, name=name, grid=(nt,), in_specs=in_specs + [ANY] * nx, out_specs=out_specs + [ANY] * nx,
        out_shape=list(row_outs) + list(red_outs) + x_shape, scratch_shapes=x_scratch,
        compiler_params=_params(("arbitrary",), x_id),
    )(*rows, *bcasts, *x_arrs)


def _sds(shape, dtype=F32):
    return jax.ShapeDtypeStruct(shape, dtype)


def _rms(x, g):
    y = x * lax.rsqrt(jnp.mean(x * x, axis=-1, keepdims=True) + EPS)
    return y * g


def _silu(x):
    return x * jax.nn.sigmoid(x)


def _swiglu(g, u):
    return _silu(g) * u


def _ln_silu(u, g, b):
    mu = jnp.mean(u, axis=-1, keepdims=True)
    var = jnp.mean(jnp.square(u - mu), axis=-1, keepdims=True)
    return _silu((u - mu) * lax.rsqrt(var + EPS) * g + b)


def _merge(conv_pre, att_out, g_conv, g_att, b_cb):
    return jax.nn.sigmoid(g_conv) * (conv_pre + b_cb) + jax.nn.sigmoid(g_att) * att_out


def _glu(t):
    return t[:, :CONV_DIM] * jax.nn.sigmoid(t[:, CONV_DIM:])


def _shifted_reader(buf, shifted, tm):
    for b in range(1, SUBLANES):
        shifted[b - 1, :, :] = buf[pl.ds(b, tm + HALO - SUBLANES), :]

    def read(o):
        a, b = divmod(o, SUBLANES)
        return buf[pl.ds(SUBLANES * a, tm), :] if b == 0 else shifted[b - 1, pl.ds(SUBLANES * a, tm), :]

    return read


def _conv_fwd(conv_in, w_pad, b, ln_g, ln_b, exchange, tm=256):
    s = conv_in.shape[0]
    tm = _pick(s, tm, HALO)
    ratio = tm // HALO
    x_arrs, x_shape, x_scratch, _, x_id = exchange
    nx = len(x_arrs)

    def body(*refs):
        main_ref, halo_ref, w_ref, b_ref, g_ref, be_ref = refs[:6]
        u3_ref, u1_ref = refs[6 + nx:8 + nx]
        buf, shifted = refs[-2:]
        finish_exchange = _carry_exchange(exchange, refs, 6, 2, *_sweep_marks(s // tm))
        i = pl.program_id(0)
        buf[0:HALO, :] = _glu(halo_ref[...]) * (i > 0).astype(F32)
        buf[HALO:HALO + tm, :] = _glu(main_ref[...])
        read = _shifted_reader(buf, shifted, tm)
        acc = jnp.zeros((tm, CONV_DIM), F32) + b_ref[...]
        for j in range(CONV_WIDTH):
            acc = acc + w_ref[j:j + 1, :] * read(HALO - (CONV_WIDTH - 1) + j)
        u1_ref[...] = acc
        u3_ref[...] = _ln_silu(acc, g_ref[...], be_ref[...]).astype(u3_ref.dtype)
        finish_exchange()

    res = pl.pallas_call(
        body, name="conv_fwd", grid=(s // tm,),
        in_specs=[pl.BlockSpec((tm, 2 * CONV_DIM), lambda i: (i, 0)),
                  pl.BlockSpec((HALO, 2 * CONV_DIM), lambda i: (jnp.maximum(i * ratio - 1, 0), 0)),
                  pl.BlockSpec(w_pad.shape, lambda i: (0, 0)),
                  pl.BlockSpec(b.shape, lambda i: (0, 0)),
                  pl.BlockSpec(ln_g.shape, lambda i: (0, 0)),
                  pl.BlockSpec(ln_b.shape, lambda i: (0, 0))] + [ANY] * nx,
        out_specs=[pl.BlockSpec((tm, CONV_DIM), lambda i: (i, 0)),
                   pl.BlockSpec((tm, CONV_DIM), lambda i: (i, 0))] + [ANY] * nx,
        out_shape=[_sds((s, CONV_DIM), BF16), _sds((s, CONV_DIM), F32)] + x_shape,
        scratch_shapes=x_scratch + [pltpu.VMEM((tm + HALO, CONV_DIM), F32),
                                    pltpu.VMEM((SUBLANES - 1, tm + HALO - SUBLANES, CONV_DIM), F32)],
        compiler_params=_params(("arbitrary",), x_id),
    )(conv_in, conv_in, w_pad, b, ln_g, ln_b, *x_arrs)
    return res[0], res[1], res[2:]


def _conv_bwd(conv_in, u1, du3, ln_g, ln_b, w_pad, exchange, tm=256):
    s = conv_in.shape[0]
    tm = _pick(s, tm, HALO)
    ratio = tm // HALO
    nt = s // tm
    last_halo = s // HALO - 1
    x_arrs, x_shape, x_scratch, _, x_id = exchange
    nx = len(x_arrs)

    def body(*refs):
        main_ref, halo_ref, u1_ref, u1n_ref, du3_ref, du3n_ref, g_ref, be_ref, w_ref = refs[:9]
        dci_ref, dw_ref, db_ref, dg_ref, dbe_ref = refs[9 + nx:14 + nx]
        ubuf, dbuf, ushift, dshift = refs[-4:]
        finish_exchange = _carry_exchange(exchange, refs, 9, 5, *_sweep_marks(nt))
        i = pl.program_id(0)
        main = main_ref[...]
        a = main[:, :CONV_DIM]
        sb = jax.nn.sigmoid(main[:, CONV_DIM:])
        ubuf[0:HALO, :] = _glu(halo_ref[...]) * (i > 0).astype(F32)
        ubuf[HALO:HALO + tm, :] = a * sb

        def ln_bwd(u1t, du3t):
            _, vjp = jax.vjp(_ln_silu, u1t, g_ref[...], be_ref[...])
            return vjp(du3t)

        du, dg, dbe = ln_bwd(u1_ref[...], du3_ref[...])
        dbuf[0:tm, :] = du
        dbuf[tm:tm + HALO, :] = ln_bwd(u1n_ref[...], du3n_ref[...])[0] * (i < nt - 1).astype(F32)

        @pl.when(i == 0)
        def _():
            dw_ref[...] = jnp.zeros_like(dw_ref)
            db_ref[...] = jnp.zeros_like(db_ref)
            dg_ref[...] = jnp.zeros_like(dg_ref)
            dbe_ref[...] = jnp.zeros_like(dbe_ref)

        dg_ref[...] += dg
        dbe_ref[...] += dbe

        read_u = _shifted_reader(ubuf, ushift, tm)
        read_d = _shifted_reader(dbuf, dshift, tm)
        du0 = jnp.zeros((tm, CONV_DIM), F32)
        for j in range(CONV_WIDTH):
            du0 = du0 + w_ref[j:j + 1, :] * read_d(CONV_WIDTH - 1 - j)
            dw_ref[j:j + 1, :] += jnp.sum(du * read_u(HALO - (CONV_WIDTH - 1) + j), axis=0, keepdims=True)
        db_ref[...] += jnp.sum(du, axis=0, keepdims=True)
        dci_ref[:, :CONV_DIM] = (du0 * sb).astype(dci_ref.dtype)
        dci_ref[:, CONV_DIM:] = (du0 * a * sb * (1.0 - sb)).astype(dci_ref.dtype)
        finish_exchange()

    res = pl.pallas_call(
        body, name="conv_bwd", grid=(nt,),
        in_specs=[pl.BlockSpec((tm, 2 * CONV_DIM), lambda i: (i, 0)),
                  pl.BlockSpec((HALO, 2 * CONV_DIM), lambda i: (jnp.maximum(i * ratio - 1, 0), 0))]
        + [pl.BlockSpec((tm, CONV_DIM), lambda i: (i, 0)),
           pl.BlockSpec((HALO, CONV_DIM), lambda i: (jnp.minimum((i + 1) * ratio, last_halo), 0))] * 2
        + [pl.BlockSpec((1, CONV_DIM), lambda i: (0, 0))] * 2 + [pl.BlockSpec(w_pad.shape, lambda i: (0, 0))]
        + [ANY] * nx,
        out_specs=[pl.BlockSpec((tm, 2 * CONV_DIM), lambda i: (i, 0)),
                   pl.BlockSpec(w_pad.shape, lambda i: (0, 0))]
        + [pl.BlockSpec((1, CONV_DIM), lambda i: (0, 0))] * 3 + [ANY] * nx,
        out_shape=[_sds((s, 2 * CONV_DIM), BF16), _sds(w_pad.shape)] + [_sds((1, CONV_DIM))] * 3 + x_shape,
        scratch_shapes=x_scratch + [pltpu.VMEM((tm + HALO, CONV_DIM), F32)] * 2
        + [pltpu.VMEM((SUBLANES - 1, tm + HALO - SUBLANES, CONV_DIM), F32)] * 2,
        compiler_params=_params(("arbitrary",), x_id),
    )(conv_in, conv_in, u1, u1, du3, du3, ln_g, ln_b, w_pad, *x_arrs)
    return res[:5], res[5:]


def _logsig_neg(z):
    return jnp.minimum(-z, 0.0) - jnp.log(1.0 + jnp.exp(-jnp.abs(z)))


def _split_dot(val, tri):
    hi = val.astype(BF16)
    lo = (val - hi.astype(F32)).astype(BF16)
    return jnp.dot(hi, tri, preferred_element_type=F32) + jnp.dot(lo, tri, preferred_element_type=F32)


def _attn_masks(t, later):
    row = lax.broadcasted_iota(jnp.int32, (t, t), 0)
    col = lax.broadcasted_iota(jnp.int32, (t, t), 1)
    tri = jnp.where(row > col if later else row <= col, 1.0, 0.0).astype(BF16)
    return col < row, tri


def _grid_marks(h, nq):
    hh, i = pl.program_id(0), pl.program_id(1)
    return ((hh == 0) & (i == 0), (hh == h - 1) & (i == nq // 2), (hh == h - 1) & (i == nq - 1),
            (hh == h // 2) & (i == 0))


def _head_masks(shape):
    lane = lax.broadcasted_iota(jnp.int32, shape, len(shape) - 1)
    return lane < HEAD_DIM, lane >= HEAD_DIM


def _per_head(blk):
    m0, m1 = _head_masks(blk.shape)
    zero = jnp.zeros_like(blk)
    return jnp.where(m0, blk, zero), jnp.where(m1, blk, zero)


NT = (((1,), (1,)), ((), ()))
TN = (((0,), (0,)), ((), ()))


def _with_top(whole, top):
    rows = top.shape[0]
    return top if rows == whole.shape[0] else jnp.concatenate([top, whole[rows:]], axis=0)


def _attn_fwd(q, k, v, exchange):
    s = q.shape[0]
    hp = q.shape[1] // LANES
    t = ATT_TILE
    scale = 1.0 / math.sqrt(HEAD_DIM)
    x_arrs, x_shape, x_scratch, _, x_id = exchange
    nx = len(x_arrs)

    def body(*refs):
        q_ref, k_ref, v_ref = refs[:3]
        o_ref, lt_ref, nb_ref = refs[3 + nx:6 + nx]
        finish_exchange = _carry_exchange(exchange, refs, 3, 3, *_grid_marks(hp, s // t))
        i = pl.program_id(1)
        qs = _per_head((q_ref[...].astype(F32) * scale).astype(BF16))
        causal, tri = _attn_masks(t, later=True)

        def step(kb, carry, masked, rows):
            cs, acc = carry
            off = pl.multiple_of(kb * t, t)
            kblk = k_ref[pl.ds(off, t), :]
            vs = _per_head(v_ref[pl.ds(off, t), :])
            acc_top = acc[:rows]
            new_cs = []
            for hd in range(2):
                z = lax.dot_general(qs[hd][:rows], kblk, NT, preferred_element_type=F32)
                l = _logsig_neg(z)
                if masked:
                    l = jnp.where(causal, l, 0.0)
                e = z + l + _split_dot(l, tri) + cs[hd][:rows]
                if masked:
                    e = jnp.where(causal, e, -1e30)
                acc_top = acc_top + jnp.dot(jnp.exp(e).astype(BF16), vs[hd], preferred_element_type=F32)
                new_cs.append(_with_top(cs[hd], cs[hd][:rows] + jnp.sum(l, axis=1, keepdims=True)))
            return tuple(new_cs), _with_top(acc, acc_top)

        zero = jnp.zeros((t, 1), F32)
        carry = step(i, ((zero, zero), jnp.zeros((t, LANES), F32)), True, t)

        def live(cs, lo, hi):
            return jnp.maximum(jnp.max(cs[0][lo:hi]), jnp.max(cs[1][lo:hi])) > DEAD_SUM

        def more(state):
            n, _, (cs, _) = state
            return (n < i) & live(cs, 0, t)

        def sweep(state):
            n, n_full, cr = state
            whole = live(cr[0], ATT_PART, t)
            cr = lax.cond(whole, lambda c: step(i - 1 - n, c, False, t), lambda c: step(i - 1 - n, c, False, ATT_PART), cr)
            return n + 1, n_full + whole.astype(jnp.int32), cr

        n_blocks, n_full, carry = lax.while_loop(more, sweep, (jnp.int32(0), jnp.int32(0), carry))
        m0, _ = _head_masks((t, LANES))
        lt_ref[...] = jnp.where(m0, carry[0][0], carry[0][1])
        o_ref[...] = carry[1].astype(o_ref.dtype)
        nb_ref[0, pl.program_id(0), i] = n_blocks.astype(F32)
        nb_ref[1, pl.program_id(0), i] = n_full.astype(F32)
        finish_exchange()

    res = pl.pallas_call(
        body, name="attn_fwd", grid=(hp, s // t),
        in_specs=[pl.BlockSpec((t, LANES), lambda p, i: (i, p)),
                  pl.BlockSpec((s, LANES), lambda p, i: (0, p)),
                  pl.BlockSpec((s, LANES), lambda p, i: (0, p))] + [ANY] * nx,
        out_specs=[pl.BlockSpec((t, LANES), lambda p, i: (i, p)),
                   pl.BlockSpec((None, t, LANES), lambda p, i: (p, i, 0)),
                   pl.BlockSpec(memory_space=pltpu.SMEM)] + [ANY] * nx,
        out_shape=[_sds(q.shape, BF16), _sds((hp, s, LANES), F32), _sds((2, hp, s // t), F32)] + x_shape,
        scratch_shapes=x_scratch,
        compiler_params=_params(("arbitrary", "arbitrary"), x_id),
    )(q, k, v, *x_arrs)
    return res[0], res[1], res[2], res[3:]


def _attn_bwd(q, k, v, do, ltot, n_blocks, exchange):
    s = q.shape[0]
    hp = q.shape[1] // LANES
    t = ATT_TILE
    scale = 1.0 / math.sqrt(HEAD_DIM)
    x_arrs, x_shape, x_scratch, _, x_id = exchange
    nx = len(x_arrs)

    def body(*refs):
        q_ref, k_ref, v_ref, do_ref, lt_ref, nb_ref = refs[:6]
        dq_ref, dk_ref, dv_ref = refs[6 + nx:9 + nx]
        finish_exchange = _carry_exchange(exchange, refs, 6, 3, *_grid_marks(hp, s // t))
        i = pl.program_id(1)
        n_blocks = jnp.clip(nb_ref[0, pl.program_id(0), i].astype(jnp.int32), 0, i)
        n_full = jnp.clip(nb_ref[1, pl.program_id(0), i].astype(jnp.int32), 0, n_blocks)

        @pl.when(i == 0)
        def _():
            dk_ref[...] = jnp.zeros_like(dk_ref)
            dv_ref[...] = jnp.zeros_like(dv_ref)

        qb = q_ref[...]
        qm = _per_head(qb)
        qs = _per_head((qb.astype(F32) * scale).astype(BF16))
        dos = _per_head(do_ref[...])
        lts = (lt_ref[:, 0:1], lt_ref[:, HEAD_DIM:HEAD_DIM + 1])
        causal, tri = _attn_masks(t, later=False)

        def step(kb, carry, masked, rows):
            cls, cgs, dq = carry
            off = pl.multiple_of(kb * t, t)
            kblk = k_ref[pl.ds(off, t), :]
            vblk = v_ref[pl.ds(off, t), :]
            ks = _per_head(kblk)
            dq_top = dq[:rows]
            dk = jnp.zeros((t, LANES), F32)
            dv = jnp.zeros((t, LANES), F32)
            new_cls, new_cgs = [], []
            for hd in range(2):
                z = lax.dot_general(qs[hd][:rows], kblk, NT, preferred_element_type=F32)
                l = _logsig_neg(z)
                if masked:
                    l = jnp.where(causal, l, 0.0)
                e = z + l + ((lts[hd][:rows] - cls[hd][:rows]) - _split_dot(l, tri))
                if masked:
                    e = jnp.where(causal, e, -1e30)
                a = jnp.exp(e)
                g = lax.dot_general(dos[hd][:rows], vblk, NT, preferred_element_type=F32) * a
                p = cgs[hd][:rows] + jnp.dot(g.astype(BF16), tri, preferred_element_type=F32) - g
                el = jnp.exp(l)
                dz = g * el - p * (1.0 - el)
                if masked:
                    dz = jnp.where(causal, dz, 0.0)
                dzb = (dz * scale).astype(BF16)
                dq_top = dq_top + jnp.dot(dzb, ks[hd], preferred_element_type=F32)
                dk = dk + lax.dot_general(dzb, qm[hd][:rows], TN, preferred_element_type=F32)
                dv = dv + lax.dot_general(a.astype(BF16), dos[hd][:rows], TN, preferred_element_type=F32)
                new_cls.append(_with_top(cls[hd], cls[hd][:rows] + jnp.sum(l, axis=1, keepdims=True)))
                new_cgs.append(_with_top(cgs[hd], cgs[hd][:rows] + jnp.sum(g, axis=1, keepdims=True)))
            dk_ref[pl.ds(off, t), :] += dk
            dv_ref[pl.ds(off, t), :] += dv
            return tuple(new_cls), tuple(new_cgs), _with_top(dq, dq_top)

        zero = jnp.zeros((t, 1), F32)
        init = ((zero, zero), (zero, zero), jnp.zeros((t, LANES), F32))
        carry = lax.fori_loop(i - n_blocks, i - n_full, lambda kb, cr: step(kb, cr, False, ATT_PART), init)
        carry = lax.fori_loop(i - n_full, i, lambda kb, cr: step(kb, cr, False, t), carry)
        carry = step(i, carry, True, t)
        dq_ref[...] = carry[2]
        finish_exchange()

    blk = pl.BlockSpec((t, LANES), lambda p, i: (i, p))
    whole = pl.BlockSpec((s, LANES), lambda p, i: (0, p))
    res = pl.pallas_call(
        body, name="attn_bwd", grid=(hp, s // t),
        in_specs=[blk, whole, whole, blk, pl.BlockSpec((None, t, LANES), lambda p, i: (p, i, 0)),
                  pl.BlockSpec(memory_space=pltpu.SMEM)] + [ANY] * nx,
        out_specs=[blk, whole, whole] + [ANY] * nx,
        out_shape=[_sds(q.shape)] * 3 + x_shape,
        scratch_shapes=x_scratch,
        compiler_params=_params(("arbitrary", "arbitrary"), x_id),
    )(q, k, v, do, ltot, n_blocks, *x_arrs)
    return res[0], res[1], res[2], res[3:]


LATE = ["w_conv_branch", "w_att_branch", "w_out", "w_ffn_up", "w_ffn_down"]


def _full_weight(name, gathered):
    return _cols_to_full(gathered) if name in COL_SHARDED else gathered.reshape(-1, gathered.shape[2])


def _grad_slabs(name, grad):
    return _full_to_cols(grad) if name in COL_SHARDED else grad.reshape(N_DEV, -1, grad.shape[1])


def _side_slabs(name, grad):
    slabs = _grad_slabs(name, grad)
    return slabs.reshape((4, 2) + slabs.shape[1:])


def _local_step(x, target, w, late_blocks, opt):
    s = x.shape[0]
    w = dict(w)
    g1, g2, g3, g4 = w["norm_mix_pre"], w["norm_mix_post"], w["norm_ffn_pre"], w["norm_ffn_post"]

    w_in = w["w_in"]

    def proj_fn(xt, g1_, w_in_t):
        h = _rms(xt, g1_).astype(BF16)
        proj = lax.dot_general(h, w_in_t, NT, preferred_element_type=F32)
        return (h, *[proj[:, IN_SPLITS[n]:IN_SPLITS[n + 1]] for n in range(6)]), ()

    mix_weights = ["w_conv_branch", "w_att_branch", "w_out"]
    h1, conv_in, q, k, v, g_conv, g_att, g_out = _rowwise(
        "norm_proj", proj_fn, [x], [g1, w_in],
        [_sds((s, D_MODEL), BF16), _sds((s, 2 * CONV_DIM)), _sds((s, ATT_DIM), BF16), _sds((s, ATT_DIM), BF16),
         _sds((s, ATT_DIM), BF16), _sds((s, D_MODEL), BF16), _sds((s, D_MODEL), BF16)], tm=512,
        exchange=_relay_gather_exchange([late_blocks["w_out"]]))

    u3, u1, g_branches = _conv_fwd(conv_in, w["conv_dw_w"], w["conv_dw_b"], w["conv_ln_g"], w["conv_ln_b"],
                                   _relay_gather_exchange([late_blocks[nm] for nm in mix_weights[:2]]))
    for nm, g in zip(mix_weights, [*g_branches, g_out]):
        w[nm] = _full_weight(nm, g)
    half = D_MODEL // 2
    down_block = late_blocks["w_ffn_down"]
    att, ltot, n_blocks, (g_up, g_left) = _attn_fwd(
        q, k, v, _relay_gather_exchange([late_blocks["w_ffn_up"], down_block[:, :half]]))
    w["w_ffn_up"] = _full_weight("w_ffn_up", g_up)

    def merge_fn(u3t, at, gc, ga, xt, w_cb, w_ab, b_cb, w_out, g2_, g3_):
        cp = jnp.dot(u3t, w_cb, preferred_element_type=F32)
        ao = jnp.dot(at, w_ab, preferred_element_type=F32)
        mg = _merge(cp, ao, gc.astype(F32), ga.astype(F32), b_cb).astype(BF16)
        mix_ = jnp.dot(mg, w_out, preferred_element_type=F32)
        x2_ = xt + _rms(mix_, g2_)
        return (mg, cp, ao, mix_, x2_, _rms(x2_, g3_)), ()

    merged, conv_pre, att_out, mix, x2, h2 = _rowwise(
        "branch_merge_mix", merge_fn, [u3, att, g_conv, g_att, x],
        [w["w_conv_branch"], w["w_att_branch"], w["b_conv_branch"], w["w_out"], g2, g3],
        [_sds((s, D_MODEL), BF16)] * 3 + [_sds((s, D_MODEL)), _sds((s, D_MODEL)), _sds((s, D_MODEL), BF16)], tm=512)

    def ffn_up_fn(ht, w_up_t):
        gu_ = lax.dot_general(ht, w_up_t, NT, preferred_element_type=F32)
        return (gu_, _swiglu(gu_[:, :D_FF], gu_[:, D_FF:])), ()

    gu, act, g_right = _rowwise("ffn_up", ffn_up_fn, [h2], [w["w_ffn_up"]],
                                [_sds((s, 2 * D_FF), BF16), _sds((s, D_FF), BF16)], tm=512,
                                exchange=_relay_gather_exchange([down_block[:, half:]]))
    w_down = [_full_weight("w_ffn_down", g) for g in (g_left, g_right)]

    def final_fn(at, x2t, tgt, w_left, w_right, g4_):
        ff = jnp.concatenate([jnp.dot(at, w_left, preferred_element_type=F32),
                              jnp.dot(at, w_right, preferred_element_type=F32)], axis=1)
        n4, vjp = jax.vjp(_rms, ff, g4_)
        err = x2t + n4 - tgt
        dy = err * (1.0 / D_MODEL)
        dff, dg4 = vjp(dy)
        return (dy, dff), (jnp.sum(err * err, axis=0, keepdims=True), dg4)

    dy, dff, loss_cols, d_g4 = _rowwise("ffn_down_loss", final_fn, [act, x2, target], [*w_down, g4],
                                        [_sds((s, D_MODEL)), _sds((s, D_MODEL), BF16)],
                                        [_sds((1, D_MODEL)), _sds((1, D_MODEL))], tm=512)
    loss = 0.5 * jnp.sum(loss_cols) / D_MODEL

    d_w_down = _tn_matmul(act, dff, name="d_w_down")

    def act_bwd_fn(dfft, gut, w_left, w_right):
        d_act = (lax.dot_general(dfft[:, :half], w_left, NT, preferred_element_type=F32)
                 + lax.dot_general(dfft[:, half:], w_right, NT, preferred_element_type=F32))
        gu_ = gut.astype(F32)
        _, vjp = jax.vjp(_swiglu, gu_[:, :D_FF], gu_[:, D_FF:])
        return (jnp.concatenate(vjp(d_act), axis=1),), ()

    down_slabs = _side_slabs("w_ffn_down", d_w_down)
    dgu, theirs = _rowwise("ffn_act_bwd", act_bwd_fn, [dff, gu], w_down, [_sds((s, 2 * D_FF), BF16)],
                           exchange=_pair_exchange([down_slabs]))
    down_sums = _pair_sum("pair_sum_w_ffn_down", down_slabs, theirs)
    d_w_up = _tn_matmul(dgu, h2, name="d_w_up")
    received = {}
    up_slabs = _side_slabs("w_ffn_up", d_w_up)

    def mid_bwd_fn(dgut, xt, mt, dyt, w_up_t, g2_, g3_):
        dh = jnp.dot(dgut, w_up_t, preferred_element_type=F32)
        n2, vjp2 = jax.vjp(_rms, mt, g2_)
        x2_ = xt + n2
        _, vjp3 = jax.vjp(_rms, x2_, g3_)
        dx2_, dg3 = vjp3(dh)
        dx2_ = dx2_ + dyt
        dmix_, dg2 = vjp2(dx2_)
        return (dx2_, dmix_), (dg2, dg3)

    dx2, dmix, d_g2, d_g3, received["w_ffn_down"] = _rowwise(
        "ffn_up_mid_bwd", mid_bwd_fn, [dgu, x, mix, dy], [w["w_ffn_up"], g2, g3],
        [_sds((s, D_MODEL)), _sds((s, D_MODEL), BF16)], [_sds((1, D_MODEL)), _sds((1, D_MODEL))], tm=512,
        exchange=_chip_exchange([down_sums]))
    d_w_out = _tn_matmul(merged, dmix, name="d_w_out")

    def merge_bwd_fn(dmt, cp, ao, gc, ga, w_out, w_cb, w_ab, b_cb):
        dm = lax.dot_general(dmt, w_out, NT, preferred_element_type=F32)
        _, vjp = jax.vjp(_merge, cp.astype(F32), ao.astype(F32), gc.astype(F32), ga.astype(F32), b_cb)
        dcp, dao, dgc, dga, dbias = vjp(dm)
        dcp, dao = dcp.astype(BF16), dao.astype(BF16)
        du3_ = lax.dot_general(dcp, w_cb, NT, preferred_element_type=F32)
        datt_ = lax.dot_general(dao, w_ab, NT, preferred_element_type=F32)
        return (dcp, dao, dgc, dga, du3_, datt_), (dbias,)

    d_conv_out, d_att_out, d_g_conv, d_g_att, du3, d_att, d_b_cb, theirs = _rowwise(
        "merge_bwd", merge_bwd_fn, [dmix, conv_pre, att_out, g_conv, g_att],
        [w["w_out"], w["w_conv_branch"], w["w_att_branch"], w["b_conv_branch"]],
        [_sds((s, D_MODEL), BF16)] * 4 + [_sds((s, CONV_DIM)), _sds((s, ATT_DIM), BF16)], [_sds((1, D_MODEL))], tm=512,
        exchange=_pair_exchange([up_slabs]))

    d_w_cb = _tn_matmul(u3, d_conv_out, name="d_w_conv_branch")
    d_w_ab = _tn_matmul(att, d_att_out, name="d_w_att_branch")

    dq, dk, dv, (received["w_ffn_up"],) = _attn_bwd(
        q, k, v, d_att, ltot, n_blocks, _chip_exchange([_pair_sum("pair_sum_w_ffn_up", up_slabs, theirs)]))

    mix_grads = {"w_conv_branch": d_w_cb, "w_att_branch": d_w_ab, "w_out": d_w_out}
    (d_conv_in, d_dw_w, d_dw_b, d_ln_g, d_ln_b), landed = _conv_bwd(
        conv_in, u1, du3, w["conv_ln_g"], w["conv_ln_b"], w["conv_dw_w"],
        _scatter_exchange([_grad_slabs(nm, mix_grads[nm]) for nm in mix_weights[:2]]))
    received.update(zip(mix_weights[:2], landed))

    d_proj = [d_conv_in, dq, dk, dv, d_g_conv, d_g_att]
    d_w_in, (received["w_out"],) = _pieces_tn_matmul(
        d_proj, h1, name="d_w_in", exchange=_scatter_exchange([_grad_slabs("w_out", d_w_out)]))
    in_slabs = _side_slabs("w_in", d_w_in)
    (theirs,) = _exchange_call("pair_swap_w_in", _pair_exchange([in_slabs]))

    early = list(opt)

    def pre_bwd_fn(*args):
        groups, (xt, dx2t), jobs, (w_in_t, g_) = args[:6], args[6:8], args[8:-2], args[-2:]
        dh = sum(jnp.dot(grp.astype(BF16), w_in_t[IN_SPLITS[n]:IN_SPLITS[n + 1]], preferred_element_type=F32)
                 for n, grp in enumerate(groups))
        _, vjp = jax.vjp(_rms, xt, g_)
        dx_, dg_ = vjp(dh)
        updates = [_sum_adamw_tile(*jobs[4 * n:4 * n + 4]) for n in range(len(early))]
        return (dx_ + dx2t, *[u for four in updates for u in four]), (dg_,)

    res = _rowwise(
        "proj_norm_bwd", pre_bwd_fn,
        d_proj + [x, dx2] + [a for nm in early for a in (received[nm], *opt[nm])], [w_in, g1],
        [_sds((s, D_MODEL))] + [_sds(opt[nm][0].shape) for nm in early for _ in range(4)],
        [_sds((1, D_MODEL))], tm=512, exchange=_chip_exchange([_pair_sum("pair_sum_w_in", in_slabs, theirs)]))
    grad_x, d_g1, received["w_in"] = res[0], res[-2], res[-1]
    updated = {nm: res[1 + 4 * n:5 + 4 * n] for n, nm in enumerate(early)}

    grads = {
        "norm_mix_pre": d_g1, "conv_dw_w": d_dw_w, "conv_dw_b": d_dw_b,
        "conv_ln_g": d_ln_g, "conv_ln_b": d_ln_b, "b_conv_branch": d_b_cb,
        "norm_mix_post": d_g2, "norm_ffn_pre": d_g3, "norm_ffn_post": d_g4,
    }
    return loss, grad_x, received, updated, grads


def _place():
    x, y, c = lax.axis_index("x"), lax.axis_index("y"), lax.axis_index("c")
    return x, y, c


def _slot(px, py, pc):
    return 4 * px + 2 * py + pc


def _exchange_scratch(n):
    return [pltpu.SemaphoreType.DMA((7 * n,)), pltpu.SemaphoreType.DMA((7 * n,)), pltpu.SemaphoreType.DMA((n,))]


GATHER_ID, SCATTER_ID, PAIR_ID, CHIP_ID, RELAY_ID = 0, 1, 2, 3, 4


def _handshake(peers):
    barrier = pltpu.get_barrier_semaphore()
    for peer in peers:
        pl.semaphore_signal(barrier, inc=1, device_id=peer, device_id_type=MESH)
    pl.semaphore_wait(barrier, len(peers))


def _gather_exchange(arrs):
    n = len(arrs)

    def phases(ins, outs, send_sems, recv_sems, local_sems):
        x, y, c = _place()
        me, sibling = (x, y, c), (x, y, 1 - c)
        chips = [(1 - x, y), (x, 1 - y), (1 - x, 1 - y)]

        def copy(a, kk, block, to, src=None):
            dst = outs[a].at[_slot(*block)]
            return pltpu.make_async_remote_copy(
                src_ref=dst if src is None else src, dst_ref=dst,
                send_sem=send_sems.at[a * 7 + kk], recv_sem=recv_sems.at[a * 7 + kk],
                device_id=to, device_id_type=MESH)

        mine = [pltpu.make_async_copy(ins[a], outs[a].at[_slot(*me)], local_sems.at[a]) for a in range(n)]
        first = []
        for a in range(n):
            first.append(copy(a, 0, me, sibling, src=ins[a]))
            first += [copy(a, 1 + j, me, (*chip, c), src=ins[a]) for j, chip in enumerate(chips)]
        passed = [copy(a, 4 + j, (*chip, c), sibling) for j, chip in enumerate(chips) for a in range(n)]

        def send():
            _handshake([sibling] + [(*chip, c) for chip in chips])
            for cp in mine + first:
                cp.start()

        def pass_on():
            for j, chip in enumerate(chips):
                for a in range(n):
                    copy(a, 1 + j, (*chip, c), me).wait_recv()
                    passed[j * n + a].start()

        def finish():
            for a in range(n):
                copy(a, 0, sibling, me).wait_recv()
                for j, chip in enumerate(chips):
                    copy(a, 4 + j, (*chip, 1 - c), me).wait_recv()
            for cp in first + passed:
                cp.wait_send()
            for cp in mine:
                cp.wait()

        return [send, pass_on, finish]

    return list(arrs), [_sds((N_DEV,) + a.shape, a.dtype) for a in arrs], _exchange_scratch(n), phases, GATHER_ID


def _relay_gather_exchange(arrs):
    n = len(arrs)
    per = 8

    def phases(ins, outs, send_sems, recv_sems, local_sems):
        x, y, c = _place()
        me, sibling = (x, y, c), (x, y, 1 - c)
        beside, below, across = (1 - x, y, c), (x, 1 - y, c), (1 - x, 1 - y, c)

        def copy(a, kk, block, to, src=None, rows=None):
            where = _slot(*block) if rows is None else (_slot(*block), rows)
            dst = outs[a].at[where]
            return pltpu.make_async_remote_copy(
                src_ref=dst if src is None else src, dst_ref=dst,
                send_sem=send_sems.at[a * per + kk], recv_sem=recv_sems.at[a * per + kk],
                device_id=to, device_id_type=MESH)

        def halves(a):
            h = ins[a].shape[0] // 2
            return pl.ds(0, h), pl.ds(h, ins[a].shape[0] - h)

        mine = [pltpu.make_async_copy(ins[a], outs[a].at[_slot(*me)], local_sems.at[a]) for a in range(n)]
        first = [copy(a, kk, me, to, src=ins[a]) for a in range(n) for kk, to in enumerate([sibling, beside, below])]
        relayed = [[copy(a, 3, beside, sibling), copy(a, 5, beside, below, rows=halves(a)[0])] for a in range(n)]
        relayed += [[copy(a, 4, below, sibling), copy(a, 6, below, beside, rows=halves(a)[1])] for a in range(n)]
        passed = [copy(a, 7, across, sibling) for a in range(n)]

        def send():
            _handshake([sibling, beside, below])
            for cp in mine + first:
                cp.start()

        def relay():
            for kk, block in ((1, beside), (2, below)):
                for a in range(n):
                    copy(a, kk, block, me).wait_recv()
                    for cp in relayed[(kk - 1) * n + a]:
                        cp.start()

        def pass_on():
            for a in range(n):
                copy(a, 5, across, me, rows=halves(a)[0]).wait_recv()
                copy(a, 6, across, me, rows=halves(a)[1]).wait_recv()
                passed[a].start()

        def finish():
            for a in range(n):
                for kk, block in ((0, me), (3, beside), (4, below), (7, across)):
                    copy(a, kk, (*block[:2], 1 - c), me).wait_recv()
            for cp in first + [cp for two in relayed for cp in two] + passed:
                cp.wait_send()
            for cp in mine:
                cp.wait()

        return [send, relay, pass_on, finish]

    scratch = [pltpu.SemaphoreType.DMA((per * n,)), pltpu.SemaphoreType.DMA((per * n,)), pltpu.SemaphoreType.DMA((n,))]
    return list(arrs), [_sds((N_DEV,) + a.shape, a.dtype) for a in arrs], scratch, phases, RELAY_ID


def _scatter_exchange(arrs):
    n = len(arrs)
    flips = [(fx, fy, fc) for fx in (0, 1) for fy in (0, 1) for fc in (0, 1)][1:]

    def phases(ins, outs, send_sems, recv_sems, local_sems):
        x, y, c = _place()
        mine = _slot(x, y, c)
        local = [pltpu.make_async_copy(ins[a].at[mine], outs[a].at[mine], local_sems.at[a]) for a in range(n)]
        peers = [((1 - x) if fx else x, (1 - y) if fy else y, (1 - c) if fc else c) for fx, fy, fc in flips]

        def copy(a, kk, src_slot, dst_slot):
            return pltpu.make_async_remote_copy(
                src_ref=ins[a].at[src_slot], dst_ref=outs[a].at[dst_slot],
                send_sem=send_sems.at[a * 7 + kk], recv_sem=recv_sems.at[a * 7 + kk],
                device_id=peers[kk], device_id_type=MESH)

        sends = [copy(a, kk, _slot(*peers[kk]), mine) for a in range(n) for kk in range(7)]

        def send():
            _handshake(peers)
            for cp in local + sends:
                cp.start()

        def finish():
            for a in range(n):
                for kk in range(7):
                    copy(a, kk, mine, _slot(*peers[kk])).wait_recv()
            for cp in sends:
                cp.wait_send()
            for cp in local:
                cp.wait()

        return [send, finish]

    return list(arrs), [_sds(a.shape, a.dtype) for a in arrs], _exchange_scratch(n), phases, SCATTER_ID


def _pair_exchange(arrs):
    n = len(arrs)

    def phases(ins, outs, send_sems, recv_sems, local_sems):
        x, y, c = _place()

        def copy(a, chip, side):
            return pltpu.make_async_remote_copy(
                src_ref=ins[a].at[chip, side], dst_ref=outs[a].at[chip],
                send_sem=send_sems.at[a * 7 + chip], recv_sem=recv_sems.at[a * 7 + chip],
                device_id=(x, y, 1 - c), device_id_type=MESH)

        sends = [copy(a, chip, 1 - c) for a in range(n) for chip in range(4)]

        def send():
            _handshake([(x, y, 1 - c)])
            for cp in sends:
                cp.start()

        def finish():
            for a in range(n):
                for chip in range(4):
                    copy(a, chip, c).wait_recv()
            for cp in sends:
                cp.wait_send()

        return [send, finish]

    return list(arrs), [_sds((4,) + a.shape[2:], a.dtype) for a in arrs], _exchange_scratch(n), phases, PAIR_ID


def _chip_exchange(arrs):
    n = len(arrs)

    def phases(ins, outs, send_sems, recv_sems, local_sems):
        x, y, c = _place()
        mine = 2 * x + y
        chips = [(1 - x, y), (x, 1 - y), (1 - x, 1 - y)]
        local = [pltpu.make_async_copy(ins[a].at[mine], outs[a].at[mine], local_sems.at[a]) for a in range(n)]

        def copy(a, j, src_slot, dst_slot):
            return pltpu.make_async_remote_copy(
                src_ref=ins[a].at[src_slot], dst_ref=outs[a].at[dst_slot],
                send_sem=send_sems.at[a * 7 + j], recv_sem=recv_sems.at[a * 7 + j],
                device_id=(*chips[j], c), device_id_type=MESH)

        sends = [copy(a, j, 2 * chips[j][0] + chips[j][1], mine) for a in range(n) for j in range(3)]

        def send():
            _handshake([(*chip, c) for chip in chips])
            for cp in local + sends:
                cp.start()

        def finish():
            for a in range(n):
                for j in range(3):
                    copy(a, j, mine, 2 * chips[j][0] + chips[j][1]).wait_recv()
            for cp in sends:
                cp.wait_send()
            for cp in local:
                cp.wait()

        return [send, finish]

    return list(arrs), [_sds(a.shape, a.dtype) for a in arrs], _exchange_scratch(n), phases, CHIP_ID


def _pair_sum(name, mine, theirs):
    _, _, r, c = mine.shape

    def body(side_ref, m_ref, t_ref, o_ref):
        o_ref[...] = (m_ref[...].astype(F32) + t_ref[...].astype(F32)).astype(o_ref.dtype)

    return pl.pallas_call(
        body, name=name,
        grid_spec=pltpu.PrefetchScalarGridSpec(
            num_scalar_prefetch=1, grid=(4,),
            in_specs=[pl.BlockSpec((None, None, r, c), lambda j, side: (j, side[0], 0, 0)),
                      pl.BlockSpec((None, r, c), lambda j, side: (j, 0, 0))],
            out_specs=pl.BlockSpec((None, r, c), lambda j, side: (j, 0, 0))),
        out_shape=_sds(theirs.shape, theirs.dtype),
        compiler_params=_params(("parallel",)),
    )(lax.axis_index("c").astype(jnp.int32).reshape(1), mine, theirs)


def _exchange_call(name, exchange):
    arrs, out_shape, scratch, phases, collective_id = exchange
    n = len(arrs)

    def body(*refs):
        for step in phases(refs[:n], refs[n:2 * n], *refs[2 * n:]):
            step()

    return pl.pallas_call(body, name=name, in_specs=[ANY] * n, out_specs=[ANY] * n,
                          out_shape=out_shape, scratch_shapes=scratch,
                          compiler_params=pltpu.CompilerParams(collective_id=collective_id))(*arrs)


def _carry_exchange(exchange, refs, n_in, n_out, first, middle, last, halfway):
    arrs, _, _, phases, _ = exchange
    n = len(arrs)
    if n == 0:
        return lambda: None
    ins = refs[n_in:n_in + n]
    outs = refs[n_in + n + n_out:n_in + 2 * n + n_out]
    sems = n_in + 2 * n + n_out
    steps = phases(ins, outs, *refs[sems:sems + 3])
    pl.when(first)(steps[0])
    if len(steps) == 4:
        pl.when(halfway)(steps[1])
    if len(steps) >= 3:
        pl.when(middle)(steps[-2])
    return lambda: pl.when(last)(steps[-1])


def _adamw_math(w, g, m, v):
    m2 = ADAM_B1 * m + (1.0 - ADAM_B1) * g
    v2 = ADAM_B2 * v + (1.0 - ADAM_B2) * jnp.square(g)
    m_hat = m2 / (1.0 - ADAM_B1 ** ADAM_STEP)
    v_hat = v2 / (1.0 - ADAM_B2 ** ADAM_STEP)
    delta = -ADAM_LR * (m_hat / (jnp.sqrt(v_hat) + ADAM_EPS) + ADAM_WD * w)
    return delta, m2, v2


def _sum_adamw_tile(parts, w, m, v):
    g = parts[0].astype(F32)
    for d in range(1, parts.shape[0]):
        g = g + parts[d].astype(F32)
    return (g, *_adamw_math(w, g, m, v))


def _sum_adamw(name, parts, w, m, v, tr=256):
    p, r, c = parts.shape
    tr = _pick(r, tr, 16)

    def body(p_ref, w_ref, m_ref, v_ref, g_ref, d_ref, m2_ref, v2_ref):
        g_ref[...], d_ref[...], m2_ref[...], v2_ref[...] = _sum_adamw_tile(p_ref[...], w_ref[...], m_ref[...], v_ref[...])

    tile = pl.BlockSpec((tr, c), lambda i: (i, 0))
    return pl.pallas_call(
        body, name=name, grid=(r // tr,),
        in_specs=[pl.BlockSpec((p, tr, c), lambda i: (0, i, 0)), tile, tile, tile],
        out_specs=[tile] * 4, out_shape=[_sds((r, c))] * 4,
        compiler_params=_params(("parallel",)),
    )(parts, w, m, v)


def _sum_parts(name, parts):
    p, r, c = parts.shape

    def body(p_ref, o_ref):
        g = p_ref[0]
        for d in range(1, p):
            g = g + p_ref[d]
        o_ref[...] = g

    return pl.pallas_call(
        body, name=name, out_shape=_sds((r, c)),
        in_specs=[pl.BlockSpec(memory_space=pltpu.VMEM)], out_specs=pl.BlockSpec(memory_space=pltpu.VMEM),
    )(parts)


WEIGHTS = ["norm_mix_pre", "w_in", "conv_dw_w", "conv_dw_b", "conv_ln_g", "conv_ln_b", "w_conv_branch",
           "b_conv_branch", "w_att_branch", "w_out", "norm_mix_post", "norm_ffn_pre", "w_ffn_up", "w_ffn_down",
           "norm_ffn_post"]
COL_SHARDED = ["w_conv_branch", "w_att_branch"]
TRANSPOSED = ["w_in", "w_ffn_up"]
VECTORS = ["norm_mix_pre", "conv_dw_b", "conv_ln_g", "conv_ln_b", "b_conv_branch", "norm_mix_post",
           "norm_ffn_pre", "norm_ffn_post"]


def _cols_to_full(g):
    return g.transpose(1, 0, 2).reshape(g.shape[1], N_DEV * g.shape[2])


def _full_to_cols(f):
    return f.reshape(f.shape[0], N_DEV, f.shape[1] // N_DEV).transpose(1, 0, 2)


PACK_ROWS = 7


def _pack_vectors(vecs, extra=None):
    parts = [vecs[nm].reshape(-1) for nm in VECTORS]
    parts.append(jnp.zeros((1,), F32) if extra is None else extra.reshape(1))
    used = sum(p.size for p in parts)
    parts.append(jnp.zeros((PACK_ROWS * D_MODEL - used,), F32))
    return jnp.concatenate(parts).reshape(PACK_ROWS, D_MODEL)


def _unpack_vectors(packed, sizes):
    flat, out, at = packed.reshape(-1), {}, 0
    for nm in VECTORS:
        out[nm] = flat[at:at + sizes[nm]]
        at += sizes[nm]
    return out, flat[at]


def kernel(x, norm_mix_pre, w_in, conv_dw_w, conv_dw_b, conv_ln_g, conv_ln_b, w_conv_branch, b_conv_branch, w_att_branch, w_out, norm_mix_post, norm_ffn_pre, w_ffn_up, w_ffn_down, norm_ffn_post, loss_target, m_norm_mix_pre, m_w_in, m_conv_dw_w, m_conv_dw_b, m_conv_ln_g, m_conv_ln_b, m_w_conv_branch, m_b_conv_branch, m_w_att_branch, m_w_out, m_norm_mix_post, m_norm_ffn_pre, m_w_ffn_up, m_w_ffn_down, m_norm_ffn_post, v_norm_mix_pre, v_w_in, v_conv_dw_w, v_conv_dw_b, v_conv_ln_g, v_conv_ln_b, v_w_conv_branch, v_b_conv_branch, v_w_att_branch, v_w_out, v_norm_mix_post, v_norm_ffn_pre, v_w_ffn_up, v_w_ffn_down, v_norm_ffn_post):
    ws = dict(zip(WEIGHTS, [norm_mix_pre, w_in, conv_dw_w, conv_dw_b, conv_ln_g, conv_ln_b, w_conv_branch,
                            b_conv_branch, w_att_branch, w_out, norm_mix_post, norm_ffn_pre, w_ffn_up, w_ffn_down,
                            norm_ffn_post]))
    ms = dict(zip(WEIGHTS, [m_norm_mix_pre, m_w_in, m_conv_dw_w, m_conv_dw_b, m_conv_ln_g, m_conv_ln_b,
                            m_w_conv_branch, m_b_conv_branch, m_w_att_branch, m_w_out, m_norm_mix_post,
                            m_norm_ffn_pre, m_w_ffn_up, m_w_ffn_down, m_norm_ffn_post]))
    vs = dict(zip(WEIGHTS, [v_norm_mix_pre, v_w_in, v_conv_dw_w, v_conv_dw_b, v_conv_ln_g, v_conv_ln_b,
                            v_w_conv_branch, v_b_conv_branch, v_w_att_branch, v_w_out, v_norm_mix_post,
                            v_norm_ffn_pre, v_w_ffn_up, v_w_ffn_down, v_norm_ffn_post]))

    dw_block = jnp.pad(conv_dw_w, ((0, 1), (0, 0)))
    g_in, g_dw = _exchange_call("gather_first", _relay_gather_exchange([w_in.T.astype(BF16), dw_block]))
    full = {"w_in": _full_weight("w_in", g_in), "conv_dw_w": _cols_to_full(g_dw)}
    for nm in VECTORS:
        full[nm] = ws[nm].reshape(1, -1)

    def as_kept(nm, a):
        return a.T if nm in TRANSPOSED else a

    ride_along = ["w_ffn_up", "w_out"]
    loss_local, grad_x, received, updated, grads = _local_step(
        x[0], loss_target[0], full, {nm: as_kept(nm, ws[nm]).astype(BF16) for nm in LATE},
        {nm: tuple(as_kept(nm, a[nm]) for a in (ws, ms, vs)) for nm in ride_along})

    small = _exchange_call("gather_small_grads", _gather_exchange(
        [_pack_vectors(grads, extra=loss_local), grads["conv_dw_w"]]))
    out_g, out_d, out_m, out_v = {}, {}, {}, {}
    for nm in LATE + ["w_in"]:
        res = updated[nm] if nm in updated else _sum_adamw(
            "adamw_" + nm, received[nm], *[as_kept(nm, a[nm]) for a in (ws, ms, vs)])
        out_g[nm], out_d[nm], out_m[nm], out_v[nm] = [as_kept(nm, r) for r in res]
    sizes = {nm: ws[nm].size for nm in VECTORS}
    vec = _sum_adamw("adamw_vectors", small[0], _pack_vectors(ws), _pack_vectors(ms), _pack_vectors(vs))
    for res, dst in zip(vec, (out_g, out_d, out_m, out_v)):
        dst.update(_unpack_vectors(res, sizes)[0])
    loss = _unpack_vectors(vec[0], sizes)[1]
    dw_full = _sum_parts("sum_dw_grads", small[1])
    me = _slot(*_place())
    dw_mine = lax.dynamic_slice(dw_full, (0, me * (CONV_DIM // N_DEV)), (CONV_WIDTH, CONV_DIM // N_DEV))
    nm = "conv_dw_w"
    out_g[nm], out_d[nm], out_m[nm], out_v[nm] = _sum_adamw("adamw_dw", dw_mine[None], ws[nm], ms[nm], vs[nm])

    outs = [loss, grad_x[None]]
    for group in (out_g, out_d, out_m, out_v):
        outs += [group[nm] for nm in WEIGHTS]
    return tuple(outs)
```

```python
import math

import jax
import jax.numpy as jnp
from jax import lax
from jax.experimental import pallas as pl
from jax.experimental.pallas import tpu as pltpu

F32 = jnp.float32
BF16 = jnp.bfloat16

N_DEV = 8
D_MODEL = 1024
CONV_DIM = 512
CONV_WIDTH = 31
N_HEADS = 8
HEAD_DIM = 64
ATT_DIM = N_HEADS * HEAD_DIM
D_FF = 2816
EPS = 1e-6
IN_SPLITS = (0, 1024, 1536, 2048, 2560, 3584, 4608)

ADAM_LR = 0.001
ADAM_B1 = 0.9
ADAM_B2 = 0.999
ADAM_EPS = 1e-08
ADAM_WD = 0.01
ADAM_STEP = 10

LANES = 128
SUBLANES = 8
HALO = 32
ATT_TILE = 256
ATT_PART = 176
DEAD_SUM = -120.0
VMEM_LIMIT = 56 * 1024 * 1024
MESH = pl.DeviceIdType.MESH
ANY = pl.BlockSpec(memory_space=pl.ANY)


def _pick(dim, target, align=LANES):
    t = min(dim, target)
    t -= t % align
    while t >= align:
        if dim % t == 0:
            return t
        t -= align
    return dim


def _params(semantics, collective_id=None):
    return pltpu.CompilerParams(dimension_semantics=semantics, vmem_limit_bytes=VMEM_LIMIT,
                                collective_id=collective_id)


def _tn_matmul(a, b, *, name):
    return _pieces_tn_matmul([a], b, name=name, tj=_pick(a.shape[1], 1408))


def _pieces_tn_matmul(pieces, b, *, name, tj=512, exchange=None):
    s, n = b.shape
    counts = [p.shape[1] // tj for p in pieces]
    starts = [sum(counts[:i]) for i in range(len(pieces))]
    assert all(p.shape == (s, c * tj) for p, c in zip(pieces, counts))
    x_arrs, x_shape, x_scratch, _, x_id = exchange or NO_EXCHANGE
    nx, n_in = len(x_arrs), len(pieces) + 1

    def body(*refs):
        b_ref, o_ref = refs[n_in - 1], refs[n_in + nx]
        finish_exchange = _carry_exchange(exchange or NO_EXCHANGE, refs, n_in, 1, *_sweep_marks(sum(counts)))
        j = pl.program_id(0)
        for p_ref, first, count in zip(refs, starts, counts):
            @pl.when((j >= first) & (j < first + count))
            def _():
                o_ref[...] = lax.dot_general(p_ref[...].astype(BF16), b_ref[...], TN,
                                             preferred_element_type=F32).astype(o_ref.dtype)
        finish_exchange()

    def piece_spec(first, count):
        return pl.BlockSpec((s, tj), lambda j: (0, jnp.clip(j - first, 0, count - 1)))

    res = pl.pallas_call(
        body, name=name, grid=(sum(counts),),
        in_specs=[piece_spec(f, c) for f, c in zip(starts, counts)]
        + [pl.BlockSpec((s, n), lambda j: (0, 0), pipeline_mode=pl.Buffered(1))] + [ANY] * nx,
        out_specs=[pl.BlockSpec((tj, n), lambda j: (j, 0))] + [ANY] * nx,
        out_shape=[jax.ShapeDtypeStruct((sum(counts) * tj, n), BF16)] + x_shape, scratch_shapes=x_scratch,
        compiler_params=_params(("arbitrary",), x_id),
    )(*pieces, b, *x_arrs)
    return res[0] if exchange is None else (res[0], res[1:])


NO_EXCHANGE = ([], [], [], None, None)


def _sweep_marks(nt):
    i = pl.program_id(0)
    return i == 0, i == nt - 1, i == nt - 1, i == nt // 2


def _rowwise(name, fn, rows, bcasts, row_outs, red_outs=(), tm=256, exchange=NO_EXCHANGE):
    s = rows[0].shape[0]
    tm = _pick(s, tm, 16)
    nt = s // tm
    resident = pl.Buffered(1)
    nr, nb, no, nd = len(rows), len(bcasts), len(row_outs), len(red_outs)
    x_arrs, x_shape, x_scratch, _, x_id = exchange
    nx = len(x_arrs)
    first_out = nr + nb + nx

    def body(*refs):
        finish_exchange = _carry_exchange(exchange, refs, nr + nb, no + nd, *_sweep_marks(nt))
        ins = [r[...] for r in refs[:nr + nb]]
        outs, reds = fn(*ins)
        for ref, val in zip(refs[first_out:first_out + no], outs):
            ref[...] = val.astype(ref.dtype)
        i = pl.program_id(0)
        for ref, val in zip(refs[first_out + no:first_out + no + nd], reds):
            @pl.when(i == 0)
            def _():
                ref[...] = val

            @pl.when(i > 0)
            def _():
                ref[...] += val
        finish_exchange()

    def row_spec(a):
        assert a.shape[-2] % nt == 0, (name, a.shape, nt)
        if len(a.shape) == 3:
            return pl.BlockSpec((a.shape[0], a.shape[1] // nt, a.shape[2]), lambda i: (0, i, 0))
        return pl.BlockSpec((a.shape[0] // nt, a.shape[1]), lambda i: (i, 0))

    in_specs = [row_spec(r) for r in rows]
    in_specs += [pl.BlockSpec(b.shape, lambda i: (0, 0), pipeline_mode=resident) for b in bcasts]
    out_specs = [row_spec(o) for o in row_outs]
    out_specs += [pl.BlockSpec(d.shape, lambda i: (0, 0)) for d in red_outs]
    return pl.pallas_call(
        body, name=name, grid=(nt,), in_specs=in_specs + [ANY] * nx, out_specs=out_specs + [ANY] * nx,
        out_shape=list(row_outs) + list(red_outs) + x_shape, scratch_shapes=x_scratch,
        compiler_params=_params(("arbitrary",), x_id),
    )(*rows, *bcasts, *x_arrs)


def _sds(shape, dtype=F32):
    return jax.ShapeDtypeStruct(shape, dtype)


def _rms(x, g):
    y = x * lax.rsqrt(jnp.mean(x * x, axis=-1, keepdims=True) + EPS)
    return y * g


def _silu(x):
    return x * jax.nn.sigmoid(x)


def _swiglu(g, u):
    return _silu(g) * u


def _ln_silu(u, g, b):
    mu = jnp.mean(u, axis=-1, keepdims=True)
    var = jnp.mean(jnp.square(u - mu), axis=-1, keepdims=True)
    return _silu((u - mu) * lax.rsqrt(var + EPS) * g + b)


def _merge(conv_pre, att_out, g_conv, g_att, b_cb):
    return jax.nn.sigmoid(g_conv) * (conv_pre + b_cb) + jax.nn.sigmoid(g_att) * att_out


def _glu(t):
    return t[:, :CONV_DIM] * jax.nn.sigmoid(t[:, CONV_DIM:])


def _shifted_reader(buf, shifted, tm):
    for b in range(1, SUBLANES):
        shifted[b - 1, :, :] = buf[pl.ds(b, tm + HALO - SUBLANES), :]

    def read(o):
        a, b = divmod(o, SUBLANES)
        return buf[pl.ds(SUBLANES * a, tm), :] if b == 0 else shifted[b - 1, pl.ds(SUBLANES * a, tm), :]

    return read


def _conv_fwd(conv_in, w_pad, b, ln_g, ln_b, exchange, tm=256):
    s = conv_in.shape[0]
    tm = _pick(s, tm, HALO)
    ratio = tm // HALO
    x_arrs, x_shape, x_scratch, _, x_id = exchange
    nx = len(x_arrs)

    def body(*refs):
        main_ref, halo_ref, w_ref, b_ref, g_ref, be_ref = refs[:6]
        u3_ref, u1_ref = refs[6 + nx:8 + nx]
        buf, shifted = refs[-2:]
        finish_exchange = _carry_exchange(exchange, refs, 6, 2, *_sweep_marks(s // tm))
        i = pl.program_id(0)
        buf[0:HALO, :] = _glu(halo_ref[...]) * (i > 0).astype(F32)
        buf[HALO:HALO + tm, :] = _glu(main_ref[...])
        read = _shifted_reader(buf, shifted, tm)
        acc = jnp.zeros((tm, CONV_DIM), F32) + b_ref[...]
        for j in range(CONV_WIDTH):
            acc = acc + w_ref[j:j + 1, :] * read(HALO - (CONV_WIDTH - 1) + j)
        u1_ref[...] = acc
        u3_ref[...] = _ln_silu(acc, g_ref[...], be_ref[...]).astype(u3_ref.dtype)
        finish_exchange()

    res = pl.pallas_call(
        body, name="conv_fwd", grid=(s // tm,),
        in_specs=[pl.BlockSpec((tm, 2 * CONV_DIM), lambda i: (i, 0)),
                  pl.BlockSpec((HALO, 2 * CONV_DIM), lambda i: (jnp.maximum(i * ratio - 1, 0), 0)),
                  pl.BlockSpec(w_pad.shape, lambda i: (0, 0)),
                  pl.BlockSpec(b.shape, lambda i: (0, 0)),
                  pl.BlockSpec(ln_g.shape, lambda i: (0, 0)),
                  pl.BlockSpec(ln_b.shape, lambda i: (0, 0))] + [ANY] * nx,
        out_specs=[pl.BlockSpec((tm, CONV_DIM), lambda i: (i, 0)),
                   pl.BlockSpec((tm, CONV_DIM), lambda i: (i, 0))] + [ANY] * nx,
        out_shape=[_sds((s, CONV_DIM), BF16), _sds((s, CONV_DIM), F32)] + x_shape,
        scratch_shapes=x_scratch + [pltpu.VMEM((tm + HALO, CONV_DIM), F32),
                                    pltpu.VMEM((SUBLANES - 1, tm + HALO - SUBLANES, CONV_DIM), F32)],
        compiler_params=_params(("arbitrary",), x_id),
    )(conv_in, conv_in, w_pad, b, ln_g, ln_b, *x_arrs)
    return res[0], res[1], res[2:]


def _conv_bwd(conv_in, u1, du3, ln_g, ln_b, w_pad, exchange, tm=256):
    s = conv_in.shape[0]
    tm = _pick(s, tm, HALO)
    ratio = tm // HALO
    nt = s // tm
    last_halo = s // HALO - 1
    x_arrs, x_shape, x_scratch, _, x_id = exchange
    nx = len(x_arrs)

    def body(*refs):
        main_ref, halo_ref, u1_ref, u1n_ref, du3_ref, du3n_ref, g_ref, be_ref, w_ref = refs[:9]
        dci_ref, dw_ref, db_ref, dg_ref, dbe_ref = refs[9 + nx:14 + nx]
        ubuf, dbuf, ushift, dshift = refs[-4:]
        finish_exchange = _carry_exchange(exchange, refs, 9, 5, *_sweep_marks(nt))
        i = pl.program_id(0)
        main = main_ref[...]
        a = main[:, :CONV_DIM]
        sb = jax.nn.sigmoid(main[:, CONV_DIM:])
        ubuf[0:HALO, :] = _glu(halo_ref[...]) * (i > 0).astype(F32)
        ubuf[HALO:HALO + tm, :] = a * sb

        def ln_bwd(u1t, du3t):
            _, vjp = jax.vjp(_ln_silu, u1t, g_ref[...], be_ref[...])
            return vjp(du3t)

        du, dg, dbe = ln_bwd(u1_ref[...], du3_ref[...])
        dbuf[0:tm, :] = du
        dbuf[tm:tm + HALO, :] = ln_bwd(u1n_ref[...], du3n_ref[...])[0] * (i < nt - 1).astype(F32)

        @pl.when(i == 0)
        def _():
            dw_ref[...] = jnp.zeros_like(dw_ref)
            db_ref[...] = jnp.zeros_like(db_ref)
            dg_ref[...] = jnp.zeros_like(dg_ref)
            dbe_ref[...] = jnp.zeros_like(dbe_ref)

        dg_ref[...] += dg
        dbe_ref[...] += dbe

        read_u = _shifted_reader(ubuf, ushift, tm)
        read_d = _shifted_reader(dbuf, dshift, tm)
        du0 = jnp.zeros((tm, CONV_DIM), F32)
        for j in range(CONV_WIDTH):
            du0 = du0 + w_ref[j:j + 1, :] * read_d(CONV_WIDTH - 1 - j)
            dw_ref[j:j + 1, :] += jnp.sum(du * read_u(HALO - (CONV_WIDTH - 1) + j), axis=0, keepdims=True)
        db_ref[...] += jnp.sum(du, axis=0, keepdims=True)
        dci_ref[:, :CONV_DIM] = (du0 * sb).astype(dci_ref.dtype)
        dci_ref[:, CONV_DIM:] = (du0 * a * sb * (1.0 - sb)).astype(dci_ref.dtype)
        finish_exchange()

    res = pl.pallas_call(
        body, name="conv_bwd", grid=(nt,),
        in_specs=[pl.BlockSpec((tm, 2 * CONV_DIM), lambda i: (i, 0)),
                  pl.BlockSpec((HALO, 2 * CONV_DIM), lambda i: (jnp.maximum(i * ratio - 1, 0), 0))]
        + [pl.BlockSpec((tm, CONV_DIM), lambda i: (i, 0)),
           pl.BlockSpec((HALO, CONV_DIM), lambda i: (jnp.minimum((i + 1) * ratio, last_halo), 0))] * 2
        + [pl.BlockSpec((1, CONV_DIM), lambda i: (0, 0))] * 2 + [pl.BlockSpec(w_pad.shape, lambda i: (0, 0))]
        + [ANY] * nx,
        out_specs=[pl.BlockSpec((tm, 2 * CONV_DIM), lambda i: (i, 0)),
                   pl.BlockSpec(w_pad.shape, lambda i: (0, 0))]
        + [pl.BlockSpec((1, CONV_DIM), lambda i: (0, 0))] * 3 + [ANY] * nx,
        out_shape=[_sds((s, 2 * CONV_DIM), BF16), _sds(w_pad.shape)] + [_sds((1, CONV_DIM))] * 3 + x_shape,
        scratch_shapes=x_scratch + [pltpu.VMEM((tm + HALO, CONV_DIM), F32)] * 2
        + [pltpu.VMEM((SUBLANES - 1, tm + HALO - SUBLANES, CONV_DIM), F32)] * 2,
        compiler_params=_params(("arbitrary",), x_id),
    )(conv_in, conv_in, u1, u1, du3, du3, ln_g, ln_b, w_pad, *x_arrs)
    return res[:5], res[5:]


def _logsig_neg(z):
    return jnp.minimum(-z, 0.0) - jnp.log(1.0 + jnp.exp(-jnp.abs(z)))


def _split_dot(val, tri):
    hi = val.astype(BF16)
    lo = (val - hi.astype(F32)).astype(BF16)
    return jnp.dot(hi, tri, preferred_element_type=F32) + jnp.dot(lo, tri, preferred_element_type=F32)


def _attn_masks(t, later):
    row = lax.broadcasted_iota(jnp.int32, (t, t), 0)
    col = lax.broadcasted_iota(jnp.int32, (t, t), 1)
    tri = jnp.where(row > col if later else row <= col, 1.0, 0.0).astype(BF16)
    return col < row, tri


def _grid_marks(h, nq):
    hh, i = pl.program_id(0), pl.program_id(1)
    return ((hh == 0) & (i == 0), (hh == h - 1) & (i == nq // 2), (hh == h - 1) & (i == nq - 1),
            (hh == h // 2) & (i == nq // 2))


def _head_masks(shape):
    lane = lax.broadcasted_iota(jnp.int32, shape, len(shape) - 1)
    return lane < HEAD_DIM, lane >= HEAD_DIM


def _per_head(blk):
    m0, m1 = _head_masks(blk.shape)
    zero = jnp.zeros_like(blk)
    return jnp.where(m0, blk, zero), jnp.where(m1, blk, zero)


NT = (((1,), (1,)), ((), ()))
TN = (((0,), (0,)), ((), ()))


def _with_top(whole, top):
    rows = top.shape[0]
    return top if rows == whole.shape[0] else jnp.concatenate([top, whole[rows:]], axis=0)


def _attn_fwd(q, k, v, exchange):
    s = q.shape[0]
    hp = q.shape[1] // LANES
    t = ATT_TILE
    scale = 1.0 / math.sqrt(HEAD_DIM)
    x_arrs, x_shape, x_scratch, _, x_id = exchange
    nx = len(x_arrs)

    def body(*refs):
        q_ref, k_ref, v_ref = refs[:3]
        o_ref, lt_ref, nb_ref = refs[3 + nx:6 + nx]
        finish_exchange = _carry_exchange(exchange, refs, 3, 3, *_grid_marks(hp, s // t))
        i = pl.program_id(1)
        qs = _per_head((q_ref[...].astype(F32) * scale).astype(BF16))
        causal, tri = _attn_masks(t, later=True)

        def step(kb, carry, masked, rows):
            cs, acc = carry
            off = pl.multiple_of(kb * t, t)
            kblk = k_ref[pl.ds(off, t), :]
            vs = _per_head(v_ref[pl.ds(off, t), :])
            acc_top = acc[:rows]
            new_cs = []
            for hd in range(2):
                z = lax.dot_general(qs[hd][:rows], kblk, NT, preferred_element_type=F32)
                l = _logsig_neg(z)
                if masked:
                    l = jnp.where(causal, l, 0.0)
                e = z + l + _split_dot(l, tri) + cs[hd][:rows]
                if masked:
                    e = jnp.where(causal, e, -1e30)
                acc_top = acc_top + jnp.dot(jnp.exp(e).astype(BF16), vs[hd], preferred_element_type=F32)
                new_cs.append(_with_top(cs[hd], cs[hd][:rows] + jnp.sum(l, axis=1, keepdims=True)))
            return tuple(new_cs), _with_top(acc, acc_top)

        zero = jnp.zeros((t, 1), F32)
        carry = step(i, ((zero, zero), jnp.zeros((t, LANES), F32)), True, t)

        def live(cs, lo, hi):
            return jnp.maximum(jnp.max(cs[0][lo:hi]), jnp.max(cs[1][lo:hi])) > DEAD_SUM

        def more(state):
            n, _, (cs, _) = state
            return (n < i) & live(cs, 0, t)

        def sweep(state):
            n, n_full, cr = state
            whole = live(cr[0], ATT_PART, t)
            cr = lax.cond(whole, lambda c: step(i - 1 - n, c, False, t), lambda c: step(i - 1 - n, c, False, ATT_PART), cr)
            return n + 1, n_full + whole.astype(jnp.int32), cr

        n_blocks, n_full, carry = lax.while_loop(more, sweep, (jnp.int32(0), jnp.int32(0), carry))
        m0, _ = _head_masks((t, LANES))
        lt_ref[...] = jnp.where(m0, carry[0][0], carry[0][1])
        o_ref[...] = carry[1].astype(o_ref.dtype)
        nb_ref[0, pl.program_id(0), i] = n_blocks.astype(F32)
        nb_ref[1, pl.program_id(0), i] = n_full.astype(F32)
        finish_exchange()

    res = pl.pallas_call(
        body, name="attn_fwd", grid=(hp, s // t),
        in_specs=[pl.BlockSpec((t, LANES), lambda p, i: (i, p)),
                  pl.BlockSpec((s, LANES), lambda p, i: (0, p)),
                  pl.BlockSpec((s, LANES), lambda p, i: (0, p))] + [ANY] * nx,
        out_specs=[pl.BlockSpec((t, LANES), lambda p, i: (i, p)),
                   pl.BlockSpec((None, t, LANES), lambda p, i: (p, i, 0)),
                   pl.BlockSpec(memory_space=pltpu.SMEM)] + [ANY] * nx,
        out_shape=[_sds(q.shape, BF16), _sds((hp, s, LANES), F32), _sds((2, hp, s // t), F32)] + x_shape,
        scratch_shapes=x_scratch,
        compiler_params=_params(("arbitrary", "arbitrary"), x_id),
    )(q, k, v, *x_arrs)
    return res[0], res[1], res[2], res[3:]


def _attn_bwd(q, k, v, do, ltot, n_blocks, exchange):
    s = q.shape[0]
    hp = q.shape[1] // LANES
    t = ATT_TILE
    scale = 1.0 / math.sqrt(HEAD_DIM)
    x_arrs, x_shape, x_scratch, _, x_id = exchange
    nx = len(x_arrs)

    def body(*refs):
        q_ref, k_ref, v_ref, do_ref, lt_ref, nb_ref = refs[:6]
        dq_ref, dk_ref, dv_ref = refs[6 + nx:9 + nx]
        finish_exchange = _carry_exchange(exchange, refs, 6, 3, *_grid_marks(hp, s // t))
        i = pl.program_id(1)
        n_blocks = jnp.clip(nb_ref[0, pl.program_id(0), i].astype(jnp.int32), 0, i)
        n_full = jnp.clip(nb_ref[1, pl.program_id(0), i].astype(jnp.int32), 0, n_blocks)

        @pl.when(i == 0)
        def _():
            dk_ref[...] = jnp.zeros_like(dk_ref)
            dv_ref[...] = jnp.zeros_like(dv_ref)

        qb = q_ref[...]
        qm = _per_head(qb)
        qs = _per_head((qb.astype(F32) * scale).astype(BF16))
        dos = _per_head(do_ref[...])
        lts = (lt_ref[:, 0:1], lt_ref[:, HEAD_DIM:HEAD_DIM + 1])
        causal, tri = _attn_masks(t, later=False)

        def step(kb, carry, masked, rows):
            cls, cgs, dq = carry
            off = pl.multiple_of(kb * t, t)
            kblk = k_ref[pl.ds(off, t), :]
            vblk = v_ref[pl.ds(off, t), :]
            ks = _per_head(kblk)
            dq_top = dq[:rows]
            dk = jnp.zeros((t, LANES), F32)
            dv = jnp.zeros((t, LANES), F32)
            new_cls, new_cgs = [], []
            for hd in range(2):
                z = lax.dot_general(qs[hd][:rows], kblk, NT, preferred_element_type=F32)
                l = _logsig_neg(z)
                if masked:
                    l = jnp.where(causal, l, 0.0)
                e = z + l + ((lts[hd][:rows] - cls[hd][:rows]) - _split_dot(l, tri))
                if masked:
                    e = jnp.where(causal, e, -1e30)
                a = jnp.exp(e)
                g = lax.dot_general(dos[hd][:rows], vblk, NT, preferred_element_type=F32) * a
                p = cgs[hd][:rows] + jnp.dot(g.astype(BF16), tri, preferred_element_type=F32) - g
                el = jnp.exp(l)
                dz = g * el - p * (1.0 - el)
                if masked:
                    dz = jnp.where(causal, dz, 0.0)
                dzb = (dz * scale).astype(BF16)
                dq_top = dq_top + jnp.dot(dzb, ks[hd], preferred_element_type=F32)
                dk = dk + lax.dot_general(dzb, qm[hd][:rows], TN, preferred_element_type=F32)
                dv = dv + lax.dot_general(a.astype(BF16), dos[hd][:rows], TN, preferred_element_type=F32)
                new_cls.append(_with_top(cls[hd], cls[hd][:rows] + jnp.sum(l, axis=1, keepdims=True)))
                new_cgs.append(_with_top(cgs[hd], cgs[hd][:rows] + jnp.sum(g, axis=1, keepdims=True)))
            dk_ref[pl.ds(off, t), :] += dk
            dv_ref[pl.ds(off, t), :] += dv
            return tuple(new_cls), tuple(new_cgs), _with_top(dq, dq_top)

        zero = jnp.zeros((t, 1), F32)
        init = ((zero, zero), (zero, zero), jnp.zeros((t, LANES), F32))
        carry = lax.fori_loop(i - n_blocks, i - n_full, lambda kb, cr: step(kb, cr, False, ATT_PART), init)
        carry = lax.fori_loop(i - n_full, i, lambda kb, cr: step(kb, cr, False, t), carry)
        carry = step(i, carry, True, t)
        dq_ref[...] = carry[2]
        finish_exchange()

    blk = pl.BlockSpec((t, LANES), lambda p, i: (i, p))
    whole = pl.BlockSpec((s, LANES), lambda p, i: (0, p))
    res = pl.pallas_call(
        body, name="attn_bwd", grid=(hp, s // t),
        in_specs=[blk, whole, whole, blk, pl.BlockSpec((None, t, LANES), lambda p, i: (p, i, 0)),
                  pl.BlockSpec(memory_space=pltpu.SMEM)] + [ANY] * nx,
        out_specs=[blk, whole, whole] + [ANY] * nx,
        out_shape=[_sds(q.shape)] * 3 + x_shape,
        scratch_shapes=x_scratch,
        compiler_params=_params(("arbitrary", "arbitrary"), x_id),
    )(q, k, v, do, ltot, n_blocks, *x_arrs)
    return res[0], res[1], res[2], res[3:]


LATE = ["w_conv_branch", "w_att_branch", "w_out", "w_ffn_up", "w_ffn_down"]


def _full_weight(name, gathered):
    return _cols_to_full(gathered) if name in COL_SHARDED else gathered.reshape(-1, gathered.shape[2])


def _grad_slabs(name, grad):
    return _full_to_cols(grad) if name in COL_SHARDED else grad.reshape(N_DEV, -1, grad.shape[1])


def _side_slabs(name, grad):
    slabs = _grad_slabs(name, grad)
    return slabs.reshape((4, 2) + slabs.shape[1:])


def _local_step(x, target, w, late_blocks, opt):
    s = x.shape[0]
    w = dict(w)
    g1, g2, g3, g4 = w["norm_mix_pre"], w["norm_mix_post"], w["norm_ffn_pre"], w["norm_ffn_post"]

    w_in = w["w_in"]

    def proj_fn(xt, g1_, w_in_t):
        h = _rms(xt, g1_).astype(BF16)
        proj = lax.dot_general(h, w_in_t, NT, preferred_element_type=F32)
        return (h, *[proj[:, IN_SPLITS[n]:IN_SPLITS[n + 1]] for n in range(6)]), ()

    mix_weights = ["w_conv_branch", "w_att_branch", "w_out"]
    h1, conv_in, q, k, v, g_conv, g_att, g_out = _rowwise(
        "norm_proj", proj_fn, [x], [g1, w_in],
        [_sds((s, D_MODEL), BF16), _sds((s, 2 * CONV_DIM)), _sds((s, ATT_DIM), BF16), _sds((s, ATT_DIM), BF16),
         _sds((s, ATT_DIM), BF16), _sds((s, D_MODEL), BF16), _sds((s, D_MODEL), BF16)], tm=512,
        exchange=_relay_gather_exchange([late_blocks["w_out"]]))

    u3, u1, g_branches = _conv_fwd(conv_in, w["conv_dw_w"], w["conv_dw_b"], w["conv_ln_g"], w["conv_ln_b"],
                                   _relay_gather_exchange([late_blocks[nm] for nm in mix_weights[:2]]))
    for nm, g in zip(mix_weights, [*g_branches, g_out]):
        w[nm] = _full_weight(nm, g)
    half = D_MODEL // 2
    down_block = late_blocks["w_ffn_down"]
    att, ltot, n_blocks, (g_up, g_left) = _attn_fwd(
        q, k, v, _relay_gather_exchange([late_blocks["w_ffn_up"], down_block[:, :half]]))
    w["w_ffn_up"] = _full_weight("w_ffn_up", g_up)

    def merge_fn(u3t, at, gc, ga, xt, w_cb, w_ab, b_cb, w_out, g2_, g3_):
        cp = jnp.dot(u3t, w_cb, preferred_element_type=F32)
        ao = jnp.dot(at, w_ab, preferred_element_type=F32)
        mg = _merge(cp, ao, gc.astype(F32), ga.astype(F32), b_cb).astype(BF16)
        mix_ = jnp.dot(mg, w_out, preferred_element_type=F32)
        x2_ = xt + _rms(mix_, g2_)
        return (mg, cp, ao, mix_, x2_, _rms(x2_, g3_)), ()

    merged, conv_pre, att_out, mix, x2, h2 = _rowwise(
        "branch_merge_mix", merge_fn, [u3, att, g_conv, g_att, x],
        [w["w_conv_branch"], w["w_att_branch"], w["b_conv_branch"], w["w_out"], g2, g3],
        [_sds((s, D_MODEL), BF16)] * 3 + [_sds((s, D_MODEL)), _sds((s, D_MODEL)), _sds((s, D_MODEL), BF16)], tm=512)

    def ffn_up_fn(ht, w_up_t):
        gu_ = lax.dot_general(ht, w_up_t, NT, preferred_element_type=F32)
        return (gu_, _swiglu(gu_[:, :D_FF], gu_[:, D_FF:])), ()

    gu, act, g_right = _rowwise("ffn_up", ffn_up_fn, [h2], [w["w_ffn_up"]],
                                [_sds((s, 2 * D_FF), BF16), _sds((s, D_FF), BF16)], tm=512,
                                exchange=_relay_gather_exchange([down_block[:, half:]]))
    w_down = [_full_weight("w_ffn_down", g) for g in (g_left, g_right)]

    def final_fn(at, x2t, tgt, w_left, w_right, g4_):
        ff = jnp.concatenate([jnp.dot(at, w_left, preferred_element_type=F32),
                              jnp.dot(at, w_right, preferred_element_type=F32)], axis=1)
        n4, vjp = jax.vjp(_rms, ff, g4_)
        err = x2t + n4 - tgt
        dy = err * (1.0 / D_MODEL)
        dff, dg4 = vjp(dy)
        return (dy, dff), (jnp.sum(err * err, axis=0, keepdims=True), dg4)

    dy, dff, loss_cols, d_g4 = _rowwise("ffn_down_loss", final_fn, [act, x2, target], [*w_down, g4],
                                        [_sds((s, D_MODEL)), _sds((s, D_MODEL), BF16)],
                                        [_sds((1, D_MODEL)), _sds((1, D_MODEL))], tm=512)
    loss = 0.5 * jnp.sum(loss_cols) / D_MODEL

    d_w_down = _tn_matmul(act, dff, name="d_w_down")

    def act_bwd_fn(dfft, gut, w_left, w_right):
        d_act = (lax.dot_general(dfft[:, :half], w_left, NT, preferred_element_type=F32)
                 + lax.dot_general(dfft[:, half:], w_right, NT, preferred_element_type=F32))
        gu_ = gut.astype(F32)
        _, vjp = jax.vjp(_swiglu, gu_[:, :D_FF], gu_[:, D_FF:])
        return (jnp.concatenate(vjp(d_act), axis=1),), ()

    down_slabs = _side_slabs("w_ffn_down", d_w_down)
    dgu, theirs = _rowwise("ffn_act_bwd", act_bwd_fn, [dff, gu], w_down, [_sds((s, 2 * D_FF), BF16)],
                           exchange=_pair_exchange([down_slabs]))
    down_sums = _pair_sum("pair_sum_w_ffn_down", down_slabs, theirs)
    d_w_up = _tn_matmul(dgu, h2, name="d_w_up")
    received = {}
    up_slabs = _side_slabs("w_ffn_up", d_w_up)

    def mid_bwd_fn(dgut, xt, mt, dyt, w_up_t, g2_, g3_):
        dh = jnp.dot(dgut, w_up_t, preferred_element_type=F32)
        n2, vjp2 = jax.vjp(_rms, mt, g2_)
        x2_ = xt + n2
        _, vjp3 = jax.vjp(_rms, x2_, g3_)
        dx2_, dg3 = vjp3(dh)
        dx2_ = dx2_ + dyt
        dmix_, dg2 = vjp2(dx2_)
        return (dx2_, dmix_), (dg2, dg3)

    dx2, dmix, d_g2, d_g3, received["w_ffn_down"] = _rowwise(
        "ffn_up_mid_bwd", mid_bwd_fn, [dgu, x, mix, dy], [w["w_ffn_up"], g2, g3],
        [_sds((s, D_MODEL)), _sds((s, D_MODEL), BF16)], [_sds((1, D_MODEL)), _sds((1, D_MODEL))], tm=512,
        exchange=_chip_exchange([down_sums]))
    d_w_out = _tn_matmul(merged, dmix, name="d_w_out")

    def merge_bwd_fn(dmt, cp, ao, gc, ga, w_out, w_cb, w_ab, b_cb):
        dm = lax.dot_general(dmt, w_out, NT, preferred_element_type=F32)
        _, vjp = jax.vjp(_merge, cp.astype(F32), ao.astype(F32), gc.astype(F32), ga.astype(F32), b_cb)
        dcp, dao, dgc, dga, dbias = vjp(dm)
        dcp, dao = dcp.astype(BF16), dao.astype(BF16)
        du3_ = lax.dot_general(dcp, w_cb, NT, preferred_element_type=F32)
        datt_ = lax.dot_general(dao, w_ab, NT, preferred_element_type=F32)
        return (dcp, dao, dgc, dga, du3_, datt_), (dbias,)

    d_conv_out, d_att_out, d_g_conv, d_g_att, du3, d_att, d_b_cb, theirs = _rowwise(
        "merge_bwd", merge_bwd_fn, [dmix, conv_pre, att_out, g_conv, g_att],
        [w["w_out"], w["w_conv_branch"], w["w_att_branch"], w["b_conv_branch"]],
        [_sds((s, D_MODEL), BF16)] * 4 + [_sds((s, CONV_DIM)), _sds((s, ATT_DIM), BF16)], [_sds((1, D_MODEL))], tm=512,
        exchange=_pair_exchange([up_slabs]))

    d_w_cb = _tn_matmul(u3, d_conv_out, name="d_w_conv_branch")
    d_w_ab = _tn_matmul(att, d_att_out, name="d_w_att_branch")

    dq, dk, dv, (received["w_ffn_up"],) = _attn_bwd(
        q, k, v, d_att, ltot, n_blocks, _chip_exchange([_pair_sum("pair_sum_w_ffn_up", up_slabs, theirs)]))

    mix_grads = {"w_conv_branch": d_w_cb, "w_att_branch": d_w_ab, "w_out": d_w_out}
    (d_conv_in, d_dw_w, d_dw_b, d_ln_g, d_ln_b), landed = _conv_bwd(
        conv_in, u1, du3, w["conv_ln_g"], w["conv_ln_b"], w["conv_dw_w"],
        _scatter_exchange([_grad_slabs(nm, mix_grads[nm]) for nm in mix_weights[:2]]))
    received.update(zip(mix_weights[:2], landed))

    d_proj = [d_conv_in, dq, dk, dv, d_g_conv, d_g_att]
    d_w_in, (received["w_out"],) = _pieces_tn_matmul(
        d_proj, h1, name="d_w_in", exchange=_scatter_exchange([_grad_slabs("w_out", d_w_out)]))
    in_slabs = _side_slabs("w_in", d_w_in)
    (theirs,) = _exchange_call("pair_swap_w_in", _pair_exchange([in_slabs]))

    early = list(opt)

    def pre_bwd_fn(*args):
        groups, (xt, dx2t), jobs, (w_in_t, g_) = args[:6], args[6:8], args[8:-2], args[-2:]
        dh = sum(jnp.dot(grp.astype(BF16), w_in_t[IN_SPLITS[n]:IN_SPLITS[n + 1]], preferred_element_type=F32)
                 for n, grp in enumerate(groups))
        _, vjp = jax.vjp(_rms, xt, g_)
        dx_, dg_ = vjp(dh)
        updates = [_sum_adamw_tile(*jobs[4 * n:4 * n + 4]) for n in range(len(early))]
        return (dx_ + dx2t, *[u for four in updates for u in four]), (dg_,)

    res = _rowwise(
        "proj_norm_bwd", pre_bwd_fn,
        d_proj + [x, dx2] + [a for nm in early for a in (received[nm], *opt[nm])], [w_in, g1],
        [_sds((s, D_MODEL))] + [_sds(opt[nm][0].shape) for nm in early for _ in range(4)],
        [_sds((1, D_MODEL))], tm=512, exchange=_chip_exchange([_pair_sum("pair_sum_w_in", in_slabs, theirs)]))
    grad_x, d_g1, received["w_in"] = res[0], res[-2], res[-1]
    updated = {nm: res[1 + 4 * n:5 + 4 * n] for n, nm in enumerate(early)}

    grads = {
        "norm_mix_pre": d_g1, "conv_dw_w": d_dw_w, "conv_dw_b": d_dw_b,
        "conv_ln_g": d_ln_g, "conv_ln_b": d_ln_b, "b_conv_branch": d_b_cb,
        "norm_mix_post": d_g2, "norm_ffn_pre": d_g3, "norm_ffn_post": d_g4,
    }
    return loss, grad_x, received, updated, grads


def _place():
    x, y, c = lax.axis_index("x"), lax.axis_index("y"), lax.axis_index("c")
    return x, y, c


def _slot(px, py, pc):
    return 4 * px + 2 * py + pc


def _exchange_scratch(n):
    return [pltpu.SemaphoreType.DMA((7 * n,)), pltpu.SemaphoreType.DMA((7 * n,)), pltpu.SemaphoreType.DMA((n,))]


GATHER_ID, SCATTER_ID, PAIR_ID, CHIP_ID, RELAY_ID = 0, 1, 2, 3, 4


def _handshake(peers):
    barrier = pltpu.get_barrier_semaphore()
    for peer in peers:
        pl.semaphore_signal(barrier, inc=1, device_id=peer, device_id_type=MESH)
    pl.semaphore_wait(barrier, len(peers))


def _gather_exchange(arrs):
    n = len(arrs)

    def phases(ins, outs, send_sems, recv_sems, local_sems):
        x, y, c = _place()
        me, sibling = (x, y, c), (x, y, 1 - c)
        chips = [(1 - x, y), (x, 1 - y), (1 - x, 1 - y)]

        def copy(a, kk, block, to, src=None):
            dst = outs[a].at[_slot(*block)]
            return pltpu.make_async_remote_copy(
                src_ref=dst if src is None else src, dst_ref=dst,
                send_sem=send_sems.at[a * 7 + kk], recv_sem=recv_sems.at[a * 7 + kk],
                device_id=to, device_id_type=MESH)

        mine = [pltpu.make_async_copy(ins[a], outs[a].at[_slot(*me)], local_sems.at[a]) for a in range(n)]
        first = []
        for a in range(n):
            first.append(copy(a, 0, me, sibling, src=ins[a]))
            first += [copy(a, 1 + j, me, (*chip, c), src=ins[a]) for j, chip in enumerate(chips)]
        passed = [copy(a, 4 + j, (*chip, c), sibling) for j, chip in enumerate(chips) for a in range(n)]

        def send():
            _handshake([sibling] + [(*chip, c) for chip in chips])
            for cp in mine + first:
                cp.start()

        def pass_on():
            for j, chip in enumerate(chips):
                for a in range(n):
                    copy(a, 1 + j, (*chip, c), me).wait_recv()
                    passed[j * n + a].start()

        def finish():
            for a in range(n):
                copy(a, 0, sibling, me).wait_recv()
                for j, chip in enumerate(chips):
                    copy(a, 4 + j, (*chip, 1 - c), me).wait_recv()
            for cp in first + passed:
                cp.wait_send()
            for cp in mine:
                cp.wait()

        return [send, pass_on, finish]

    return list(arrs), [_sds((N_DEV,) + a.shape, a.dtype) for a in arrs], _exchange_scratch(n), phases, GATHER_ID


def _relay_gather_exchange(arrs):
    n = len(arrs)
    per = 8

    def phases(ins, outs, send_sems, recv_sems, local_sems):
        x, y, c = _place()
        me, sibling = (x, y, c), (x, y, 1 - c)
        beside, below, across = (1 - x, y, c), (x, 1 - y, c), (1 - x, 1 - y, c)

        def copy(a, kk, block, to, src=None, rows=None):
            where = _slot(*block) if rows is None else (_slot(*block), rows)
            dst = outs[a].at[where]
            return pltpu.make_async_remote_copy(
                src_ref=dst if src is None else src, dst_ref=dst,
                send_sem=send_sems.at[a * per + kk], recv_sem=recv_sems.at[a * per + kk],
                device_id=to, device_id_type=MESH)

        def halves(a):
            h = ins[a].shape[0] // 2
            return pl.ds(0, h), pl.ds(h, ins[a].shape[0] - h)

        mine = [pltpu.make_async_copy(ins[a], outs[a].at[_slot(*me)], local_sems.at[a]) for a in range(n)]
        first = [copy(a, kk, me, to, src=ins[a]) for a in range(n) for kk, to in enumerate([sibling, beside, below])]
        relayed = [[copy(a, 3, beside, sibling), copy(a, 5, beside, below, rows=halves(a)[0])] for a in range(n)]
        relayed += [[copy(a, 4, below, sibling), copy(a, 6, below, beside, rows=halves(a)[1])] for a in range(n)]
        passed = [copy(a, 7, across, sibling) for a in range(n)]

        def send():
            _handshake([sibling, beside, below])
            for cp in mine + first:
                cp.start()

        def relay():
            for kk, block in ((1, beside), (2, below)):
                for a in range(n):
                    copy(a, kk, block, me).wait_recv()
                    for cp in relayed[(kk - 1) * n + a]:
                        cp.start()

        def pass_on():
            for a in range(n):
                copy(a, 5, across, me, rows=halves(a)[0]).wait_recv()
                copy(a, 6, across, me, rows=halves(a)[1]).wait_recv()
                passed[a].start()

        def finish():
            for a in range(n):
                for kk, block in ((0, me), (3, beside), (4, below), (7, across)):
                    copy(a, kk, (*block[:2], 1 - c), me).wait_recv()
            for cp in first + [cp for two in relayed for cp in two] + passed:
                cp.wait_send()
            for cp in mine:
                cp.wait()

        return [send, relay, pass_on, finish]

    scratch = [pltpu.SemaphoreType.DMA((per * n,)), pltpu.SemaphoreType.DMA((per * n,)), pltpu.SemaphoreType.DMA((n,))]
    return list(arrs), [_sds((N_DEV,) + a.shape, a.dtype) for a in arrs], scratch, phases, RELAY_ID


def _scatter_exchange(arrs):
    n = len(arrs)
    flips = [(fx, fy, fc) for fx in (0, 1) for fy in (0, 1) for fc in (0, 1)][1:]

    def phases(ins, outs, send_sems, recv_sems, local_sems):
        x, y, c = _place()
        mine = _slot(x, y, c)
        local = [pltpu.make_async_copy(ins[a].at[mine], outs[a].at[mine], local_sems.at[a]) for a in range(n)]
        peers = [((1 - x) if fx else x, (1 - y) if fy else y, (1 - c) if fc else c) for fx, fy, fc in flips]

        def copy(a, kk, src_slot, dst_slot):
            return pltpu.make_async_remote_copy(
                src_ref=ins[a].at[src_slot], dst_ref=outs[a].at[dst_slot],
                send_sem=send_sems.at[a * 7 + kk], recv_sem=recv_sems.at[a * 7 + kk],
                device_id=peers[kk], device_id_type=MESH)

        sends = [copy(a, kk, _slot(*peers[kk]), mine) for a in range(n) for kk in range(7)]

        def send():
            _handshake(peers)
            for cp in local + sends:
                cp.start()

        def finish():
            for a in range(n):
                for kk in range(7):
                    copy(a, kk, mine, _slot(*peers[kk])).wait_recv()
            for cp in sends:
                cp.wait_send()
            for cp in local:
                cp.wait()

        return [send, finish]

    return list(arrs), [_sds(a.shape, a.dtype) for a in arrs], _exchange_scratch(n), phases, SCATTER_ID


def _pair_exchange(arrs):
    n = len(arrs)

    def phases(ins, outs, send_sems, recv_sems, local_sems):
        x, y, c = _place()

        def copy(a, chip, side):
            return pltpu.make_async_remote_copy(
                src_ref=ins[a].at[chip, side], dst_ref=outs[a].at[chip],
                send_sem=send_sems.at[a * 7 + chip], recv_sem=recv_sems.at[a * 7 + chip],
                device_id=(x, y, 1 - c), device_id_type=MESH)

        sends = [copy(a, chip, 1 - c) for a in range(n) for chip in range(4)]

        def send():
            _handshake([(x, y, 1 - c)])
            for cp in sends:
                cp.start()

        def finish():
            for a in range(n):
                for chip in range(4):
                    copy(a, chip, c).wait_recv()
            for cp in sends:
                cp.wait_send()

        return [send, finish]

    return list(arrs), [_sds((4,) + a.shape[2:], a.dtype) for a in arrs], _exchange_scratch(n), phases, PAIR_ID


def _chip_exchange(arrs):
    n = len(arrs)

    def phases(ins, outs, send_sems, recv_sems, local_sems):
        x, y, c = _place()
        mine = 2 * x + y
        chips = [(1 - x, y), (x, 1 - y), (1 - x, 1 - y)]
        local = [pltpu.make_async_copy(ins[a].at[mine], outs[a].at[mine], local_sems.at[a]) for a in range(n)]

        def copy(a, j, src_slot, dst_slot):
            return pltpu.make_async_remote_copy(
                src_ref=ins[a].at[src_slot], dst_ref=outs[a].at[dst_slot],
                send_sem=send_sems.at[a * 7 + j], recv_sem=recv_sems.at[a * 7 + j],
                device_id=(*chips[j], c), device_id_type=MESH)

        sends = [copy(a, j, 2 * chips[j][0] + chips[j][1], mine) for a in range(n) for j in range(3)]

        def send():
            _handshake([(*chip, c) for chip in chips])
            for cp in local + sends:
                cp.start()

        def finish():
            for a in range(n):
                for j in range(3):
                    copy(a, j, mine, 2 * chips[j][0] + chips[j][1]).wait_recv()
            for cp in sends:
                cp.wait_send()
            for cp in local:
                cp.wait()

        return [send, finish]

    return list(arrs), [_sds(a.shape, a.dtype) for a in arrs], _exchange_scratch(n), phases, CHIP_ID


def _pair_sum(name, mine, theirs):
    _, _, r, c = mine.shape

    def body(side_ref, m_ref, t_ref, o_ref):
        o_ref[...] = (m_ref[...].astype(F32) + t_ref[...].astype(F32)).astype(o_ref.dtype)

    return pl.pallas_call(
        body, name=name,
        grid_spec=pltpu.PrefetchScalarGridSpec(
            num_scalar_prefetch=1, grid=(4,),
            in_specs=[pl.BlockSpec((None, None, r, c), lambda j, side: (j, side[0], 0, 0)),
                      pl.BlockSpec((None, r, c), lambda j, side: (j, 0, 0))],
            out_specs=pl.BlockSpec((None, r, c), lambda j, side: (j, 0, 0))),
        out_shape=_sds(theirs.shape, theirs.dtype),
        compiler_params=_params(("parallel",)),
    )(lax.axis_index("c").astype(jnp.int32).reshape(1), mine, theirs)


def _exchange_call(name, exchange):
    arrs, out_shape, scratch, phases, collective_id = exchange
    n = len(arrs)

    def body(*refs):
        for step in phases(refs[:n], refs[n:2 * n], *refs[2 * n:]):
            step()

    return pl.pallas_call(body, name=name, in_specs=[ANY] * n, out_specs=[ANY] * n,
                          out_shape=out_shape, scratch_shapes=scratch,
                          compiler_params=pltpu.CompilerParams(collective_id=collective_id))(*arrs)


def _carry_exchange(exchange, refs, n_in, n_out, first, middle, last, halfway):
    arrs, _, _, phases, _ = exchange
    n = len(arrs)
    if n == 0:
        return lambda: None
    ins = refs[n_in:n_in + n]
    outs = refs[n_in + n + n_out:n_in + 2 * n + n_out]
    sems = n_in + 2 * n + n_out
    steps = phases(ins, outs, *refs[sems:sems + 3])
    pl.when(first)(steps[0])
    if len(steps) == 4:
        pl.when(halfway)(steps[1])
    if len(steps) >= 3:
        pl.when(middle)(steps[-2])
    return lambda: pl.when(last)(steps[-1])


def _adamw_math(w, g, m, v):
    m2 = ADAM_B1 * m + (1.0 - ADAM_B1) * g
    v2 = ADAM_B2 * v + (1.0 - ADAM_B2) * jnp.square(g)
    m_hat = m2 / (1.0 - ADAM_B1 ** ADAM_STEP)
    v_hat = v2 / (1.0 - ADAM_B2 ** ADAM_STEP)
    delta = -ADAM_LR * (m_hat / (jnp.sqrt(v_hat) + ADAM_EPS) + ADAM_WD * w)
    return delta, m2, v2


def _sum_adamw_tile(parts, w, m, v):
    g = parts[0].astype(F32)
    for d in range(1, parts.shape[0]):
        g = g + parts[d].astype(F32)
    return (g, *_adamw_math(w, g, m, v))


def _sum_adamw(name, parts, w, m, v, tr=256):
    p, r, c = parts.shape
    tr = _pick(r, tr, 16)

    def body(p_ref, w_ref, m_ref, v_ref, g_ref, d_ref, m2_ref, v2_ref):
        g_ref[...], d_ref[...], m2_ref[...], v2_ref[...] = _sum_adamw_tile(p_ref[...], w_ref[...], m_ref[...], v_ref[...])

    tile = pl.BlockSpec((tr, c), lambda i: (i, 0))
    return pl.pallas_call(
        body, name=name, grid=(r // tr,),
        in_specs=[pl.BlockSpec((p, tr, c), lambda i: (0, i, 0)), tile, tile, tile],
        out_specs=[tile] * 4, out_shape=[_sds((r, c))] * 4,
        compiler_params=_params(("parallel",)),
    )(parts, w, m, v)


def _sum_parts(name, parts):
    p, r, c = parts.shape

    def body(p_ref, o_ref):
        g = p_ref[0]
        for d in range(1, p):
            g = g + p_ref[d]
        o_ref[...] = g

    return pl.pallas_call(
        body, name=name, out_shape=_sds((r, c)),
        in_specs=[pl.BlockSpec(memory_space=pltpu.VMEM)], out_specs=pl.BlockSpec(memory_space=pltpu.VMEM),
    )(parts)


WEIGHTS = ["norm_mix_pre", "w_in", "conv_dw_w", "conv_dw_b", "conv_ln_g", "conv_ln_b", "w_conv_branch",
           "b_conv_branch", "w_att_branch", "w_out", "norm_mix_post", "norm_ffn_pre", "w_ffn_up", "w_ffn_down",
           "norm_ffn_post"]
COL_SHARDED = ["w_conv_branch", "w_att_branch"]
TRANSPOSED = ["w_in", "w_ffn_up"]
VECTORS = ["norm_mix_pre", "conv_dw_b", "conv_ln_g", "conv_ln_b", "b_conv_branch", "norm_mix_post",
           "norm_ffn_pre", "norm_ffn_post"]


def _cols_to_full(g):
    return g.transpose(1, 0, 2).reshape(g.shape[1], N_DEV * g.shape[2])


def _full_to_cols(f):
    return f.reshape(f.shape[0], N_DEV, f.shape[1] // N_DEV).transpose(1, 0, 2)


PACK_ROWS = 7


def _pack_vectors(vecs, extra=None):
    parts = [vecs[nm].reshape(-1) for nm in VECTORS]
    parts.append(jnp.zeros((1,), F32) if extra is None else extra.reshape(1))
    used = sum(p.size for p in parts)
    parts.append(jnp.zeros((PACK_ROWS * D_MODEL - used,), F32))
    return jnp.concatenate(parts).reshape(PACK_ROWS, D_MODEL)


def _unpack_vectors(packed, sizes):
    flat, out, at = packed.reshape(-1), {}, 0
    for nm in VECTORS:
        out[nm] = flat[at:at + sizes[nm]]
        at += sizes[nm]
    return out, flat[at]


def kernel(x, norm_mix_pre, w_in, conv_dw_w, conv_dw_b, conv_ln_g, conv_ln_b, w_conv_branch, b_conv_branch, w_att_branch, w_out, norm_mix_post, norm_ffn_pre, w_ffn_up, w_ffn_down, norm_ffn_post, loss_target, m_norm_mix_pre, m_w_in, m_conv_dw_w, m_conv_dw_b, m_conv_ln_g, m_conv_ln_b, m_w_conv_branch, m_b_conv_branch, m_w_att_branch, m_w_out, m_norm_mix_post, m_norm_ffn_pre, m_w_ffn_up, m_w_ffn_down, m_norm_ffn_post, v_norm_mix_pre, v_w_in, v_conv_dw_w, v_conv_dw_b, v_conv_ln_g, v_conv_ln_b, v_w_conv_branch, v_b_conv_branch, v_w_att_branch, v_w_out, v_norm_mix_post, v_norm_ffn_pre, v_w_ffn_up, v_w_ffn_down, v_norm_ffn_post):
    ws = dict(zip(WEIGHTS, [norm_mix_pre, w_in, conv_dw_w, conv_dw_b, conv_ln_g, conv_ln_b, w_conv_branch,
                            b_conv_branch, w_att_branch, w_out, norm_mix_post, norm_ffn_pre, w_ffn_up, w_ffn_down,
                            norm_ffn_post]))
    ms = dict(zip(WEIGHTS, [m_norm_mix_pre, m_w_in, m_conv_dw_w, m_conv_dw_b, m_conv_ln_g, m_conv_ln_b,
                            m_w_conv_branch, m_b_conv_branch, m_w_att_branch, m_w_out, m_norm_mix_post,
                            m_norm_ffn_pre, m_w_ffn_up, m_w_ffn_down, m_norm_ffn_post]))
    vs = dict(zip(WEIGHTS, [v_norm_mix_pre, v_w_in, v_conv_dw_w, v_conv_dw_b, v_conv_ln_g, v_conv_ln_b,
                            v_w_conv_branch, v_b_conv_branch, v_w_att_branch, v_w_out, v_norm_mix_post,
                            v_norm_ffn_pre, v_w_ffn_up, v_w_ffn_down, v_norm_ffn_post]))

    dw_block = jnp.pad(conv_dw_w, ((0, 1), (0, 0)))
    g_in, g_dw = _exchange_call("gather_first", _relay_gather_exchange([w_in.T.astype(BF16), dw_block]))
    full = {"w_in": _full_weight("w_in", g_in), "conv_dw_w": _cols_to_full(g_dw)}
    for nm in VECTORS:
        full[nm] = ws[nm].reshape(1, -1)

    def as_kept(nm, a):
        return a.T if nm in TRANSPOSED else a

    ride_along = ["w_ffn_up", "w_out"]
    loss_local, grad_x, received, updated, grads = _local_step(
        x[0], loss_target[0], full, {nm: as_kept(nm, ws[nm]).astype(BF16) for nm in LATE},
        {nm: tuple(as_kept(nm, a[nm]) for a in (ws, ms, vs)) for nm in ride_along})

    small = _exchange_call("gather_small_grads", _gather_exchange(
        [_pack_vectors(grads, extra=loss_local), grads["conv_dw_w"]]))
    out_g, out_d, out_m, out_v = {}, {}, {}, {}
    for nm in LATE + ["w_in"]:
        res = updated[nm] if nm in updated else _sum_adamw(
            "adamw_" + nm, received[nm], *[as_kept(nm, a[nm]) for a in (ws, ms, vs)])
        out_g[nm], out_d[nm], out_m[nm], out_v[nm] = [as_kept(nm, r) for r in res]
    sizes = {nm: ws[nm].size for nm in VECTORS}
    vec = _sum_adamw("adamw_vectors", small[0], _pack_vectors(ws), _pack_vectors(ms), _pack_vectors(vs))
    for res, dst in zip(vec, (out_g, out_d, out_m, out_v)):
        dst.update(_unpack_vectors(res, sizes)[0])
    loss = _unpack_vectors(vec[0], sizes)[1]
    dw_full = _sum_parts("sum_dw_grads", small[1])
    me = _slot(*_place())
    dw_mine = lax.dynamic_slice(dw_full, (0, me * (CONV_DIM // N_DEV)), (CONV_WIDTH, CONV_DIM // N_DEV))
    nm = "conv_dw_w"
    out_g[nm], out_d[nm], out_m[nm], out_v[nm] = _sum_adamw("adamw_dw", dw_mine[None], ws[nm], ms[nm], vs[nm])

    outs = [loss, grad_x[None]]
    for group in (out_g, out_d, out_m, out_v):
        outs += [group[nm] for nm in WEIGHTS]
    return tuple(outs)
```

```python
import math

import jax
import jax.numpy as jnp
from jax import lax
from jax.experimental import pallas as pl
from jax.experimental.pallas import tpu as pltpu

F32 = jnp.float32
BF16 = jnp.bfloat16

N_DEV = 8
D_MODEL = 1024
CONV_DIM = 512
CONV_WIDTH = 31
N_HEADS = 8
HEAD_DIM = 64
ATT_DIM = N_HEADS * HEAD_DIM
D_FF = 2816
EPS = 1e-6
IN_SPLITS = (0, 1024, 1536, 2048, 2560, 3584, 4608)

ADAM_LR = 0.001
ADAM_B1 = 0.9
ADAM_B2 = 0.999
ADAM_EPS = 1e-08
ADAM_WD = 0.01
ADAM_STEP = 10

LANES = 128
SUBLANES = 8
HALO = 32
ATT_TILE = 256
ATT_PART = 176
DEAD_SUM = -120.0
VMEM_LIMIT = 56 * 1024 * 1024
MESH = pl.DeviceIdType.MESH
ANY = pl.BlockSpec(memory_space=pl.ANY)


def _pick(dim, target, align=LANES):
    t = min(dim, target)
    t -= t % align
    while t >= align:
        if dim % t == 0:
            return t
        t -= align
    return dim


def _params(semantics, collective_id=None):
    return pltpu.CompilerParams(dimension_semantics=semantics, vmem_limit_bytes=VMEM_LIMIT,
                                collective_id=collective_id)


def _tn_matmul(a, b, *, name):
    return _pieces_tn_matmul([a], b, name=name, tj=_pick(a.shape[1], 1408))


def _pieces_tn_matmul(pieces, b, *, name, tj=512, exchange=None):
    s, n = b.shape
    counts = [p.shape[1] // tj for p in pieces]
    starts = [sum(counts[:i]) for i in range(len(pieces))]
    assert all(p.shape == (s, c * tj) for p, c in zip(pieces, counts))
    x_arrs, x_shape, x_scratch, _, x_id = exchange or NO_EXCHANGE
    nx, n_in = len(x_arrs), len(pieces) + 1

    def body(*refs):
        b_ref, o_ref = refs[n_in - 1], refs[n_in + nx]
        finish_exchange = _carry_exchange(exchange or NO_EXCHANGE, refs, n_in, 1, *_sweep_marks(sum(counts)))
        j = pl.program_id(0)
        for p_ref, first, count in zip(refs, starts, counts):
            @pl.when((j >= first) & (j < first + count))
            def _():
                o_ref[...] = lax.dot_general(p_ref[...].astype(BF16), b_ref[...], TN,
                                             preferred_element_type=F32).astype(o_ref.dtype)
        finish_exchange()

    def piece_spec(first, count):
        return pl.BlockSpec((s, tj), lambda j: (0, jnp.clip(j - first, 0, count - 1)))

    res = pl.pallas_call(
        body, name=name, grid=(sum(counts),),
        in_specs=[piece_spec(f, c) for f, c in zip(starts, counts)]
        + [pl.BlockSpec((s, n), lambda j: (0, 0), pipeline_mode=pl.Buffered(1))] + [ANY] * nx,
        out_specs=[pl.BlockSpec((tj, n), lambda j: (j, 0))] + [ANY] * nx,
        out_shape=[jax.ShapeDtypeStruct((sum(counts) * tj, n), BF16)] + x_shape, scratch_shapes=x_scratch,
        compiler_params=_params(("arbitrary",), x_id),
    )(*pieces, b, *x_arrs)
    return res[0] if exchange is None else (res[0], res[1:])


NO_EXCHANGE = ([], [], [], None, None)


def _sweep_marks(nt):
    i = pl.program_id(0)
    return i == 0, i == nt - 1, i == nt - 1, i == nt // 2


def _rowwise(name, fn, rows, bcasts, row_outs, red_outs=(), tm=256, exchange=NO_EXCHANGE):
    s = rows[0].shape[0]
    tm = _pick(s, tm, 16)
    nt = s // tm
    resident = pl.Buffered(1)
    nr, nb, no, nd = len(rows), len(bcasts), len(row_outs), len(red_outs)
    x_arrs, x_shape, x_scratch, _, x_id = exchange
    nx = len(x_arrs)
    first_out = nr + nb + nx

    def body(*refs):
        finish_exchange = _carry_exchange(exchange, refs, nr + nb, no + nd, *_sweep_marks(nt))
        ins = [r[...] for r in refs[:nr + nb]]
        outs, reds = fn(*ins)
        for ref, val in zip(refs[first_out:first_out + no], outs):
            ref[...] = val.astype(ref.dtype)
        i = pl.program_id(0)
        for ref, val in zip(refs[first_out + no:first_out + no + nd], reds):
            @pl.when(i == 0)
            def _():
                ref[...] = val

            @pl.when(i > 0)
            def _():
                ref[...] += val
        finish_exchange()

    def row_spec(a):
        assert a.shape[-2] % nt == 0, (name, a.shape, nt)
        if len(a.shape) == 3:
            return pl.BlockSpec((a.shape[0], a.shape[1] // nt, a.shape[2]), lambda i: (0, i, 0))
        return pl.BlockSpec((a.shape[0] // nt, a.shape[1]), lambda i: (i, 0))

    in_specs = [row_spec(r) for r in rows]
    in_specs += [pl.BlockSpec(b.shape, lambda i: (0, 0), pipeline_mode=resident) for b in bcasts]
    out_specs = [row_spec(o) for o in row_outs]
    out_specs += [pl.BlockSpec(d.shape, lambda i: (0, 0)) for d in red_outs]
    return pl.pallas_call(
        body, name=name, grid=(nt,), in_specs=in_specs + [ANY] * nx, out_specs=out_specs + [ANY] * nx,
        out_shape=list(row_outs) + list(red_outs) + x_shape, scratch_shapes=x_scratch,
        compiler_params=_params(("arbitrary",), x_id),
    )(*rows, *bcasts, *x_arrs)


def _sds(shape, dtype=F32):
    return jax.ShapeDtypeStruct(shape, dtype)


def _rms(x, g):
    y = x * lax.rsqrt(jnp.mean(x * x, axis=-1, keepdims=True) + EPS)
    return y * g


def _silu(x):
    return x * jax.nn.sigmoid(x)


def _swiglu(g, u):
    return _silu(g) * u


def _ln_silu(u, g, b):
    mu = jnp.mean(u, axis=-1, keepdims=True)
    var = jnp.mean(jnp.square(u - mu), axis=-1, keepdims=True)
    return _silu((u - mu) * lax.rsqrt(var + EPS) * g + b)


def _merge(conv_pre, att_out, g_conv, g_att, b_cb):
    return jax.nn.sigmoid(g_conv) * (conv_pre + b_cb) + jax.nn.sigmoid(g_att) * att_out


def _glu(t):
    return t[:, :CONV_DIM] * jax.nn.sigmoid(t[:, CONV_DIM:])


def _shifted_reader(buf, shifted, tm):
    for b in range(1, SUBLANES):
        shifted[b - 1, :, :] = buf[pl.ds(b, tm + HALO - SUBLANES), :]

    def read(o):
        a, b = divmod(o, SUBLANES)
        return buf[pl.ds(SUBLANES * a, tm), :] if b == 0 else shifted[b - 1, pl.ds(SUBLANES * a, tm), :]

    return read


def _conv_fwd(conv_in, w_pad, b, ln_g, ln_b, exchange, tm=256):
    s = conv_in.shape[0]
    tm = _pick(s, tm, HALO)
    ratio = tm // HALO
    x_arrs, x_shape, x_scratch, _, x_id = exchange
    nx = len(x_arrs)

    def body(*refs):
        main_ref, halo_ref, w_ref, b_ref, g_ref, be_ref = refs[:6]
        u3_ref, u1_ref = refs[6 + nx:8 + nx]
        buf, shifted = refs[-2:]
        finish_exchange = _carry_exchange(exchange, refs, 6, 2, *_sweep_marks(s // tm))
        i = pl.program_id(0)
        buf[0:HALO, :] = _glu(halo_ref[...]) * (i > 0).astype(F32)
        buf[HALO:HALO + tm, :] = _glu(main_ref[...])
        read = _shifted_reader(buf, shifted, tm)
        acc = jnp.zeros((tm, CONV_DIM), F32) + b_ref[...]
        for j in range(CONV_WIDTH):
            acc = acc + w_ref[j:j + 1, :] * read(HALO - (CONV_WIDTH - 1) + j)
        u1_ref[...] = acc
        u3_ref[...] = _ln_silu(acc, g_ref[...], be_ref[...]).astype(u3_ref.dtype)
        finish_exchange()

    res = pl.pallas_call(
        body, name="conv_fwd", grid=(s // tm,),
        in_specs=[pl.BlockSpec((tm, 2 * CONV_DIM), lambda i: (i, 0)),
                  pl.BlockSpec((HALO, 2 * CONV_DIM), lambda i: (jnp.maximum(i * ratio - 1, 0), 0)),
                  pl.BlockSpec(w_pad.shape, lambda i: (0, 0)),
                  pl.BlockSpec(b.shape, lambda i: (0, 0)),
                  pl.BlockSpec(ln_g.shape, lambda i: (0, 0)),
                  pl.BlockSpec(ln_b.shape, lambda i: (0, 0))] + [ANY] * nx,
        out_specs=[pl.BlockSpec((tm, CONV_DIM), lambda i: (i, 0)),
                   pl.BlockSpec((tm, CONV_DIM), lambda i: (i, 0))] + [ANY] * nx,
        out_shape=[_sds((s, CONV_DIM), BF16), _sds((s, CONV_DIM), F32)] + x_shape,
        scratch_shapes=x_scratch + [pltpu.VMEM((tm + HALO, CONV_DIM), F32),
                                    pltpu.VMEM((SUBLANES - 1, tm + HALO - SUBLANES, CONV_DIM), F32)],
        compiler_params=_params(("arbitrary",), x_id),
    )(conv_in, conv_in, w_pad, b, ln_g, ln_b, *x_arrs)
    return res[0], res[1], res[2:]


def _conv_bwd(conv_in, u1, du3, ln_g, ln_b, w_pad, exchange, tm=256):
    s = conv_in.shape[0]
    tm = _pick(s, tm, HALO)
    ratio = tm // HALO
    nt = s // tm
    last_halo = s // HALO - 1
    x_arrs, x_shape, x_scratch, _, x_id = exchange
    nx = len(x_arrs)

    def body(*refs):
        main_ref, halo_ref, u1_ref, u1n_ref, du3_ref, du3n_ref, g_ref, be_ref, w_ref = refs[:9]
        dci_ref, dw_ref, db_ref, dg_ref, dbe_ref = refs[9 + nx:14 + nx]
        ubuf, dbuf, ushift, dshift = refs[-4:]
        finish_exchange = _carry_exchange(exchange, refs, 9, 5, *_sweep_marks(nt))
        i = pl.program_id(0)
        main = main_ref[...]
        a = main[:, :CONV_DIM]
        sb = jax.nn.sigmoid(main[:, CONV_DIM:])
        ubuf[0:HALO, :] = _glu(halo_ref[...]) * (i > 0).astype(F32)
        ubuf[HALO:HALO + tm, :] = a * sb

        def ln_bwd(u1t, du3t):
            _, vjp = jax.vjp(_ln_silu, u1t, g_ref[...], be_ref[...])
            return vjp(du3t)

        du, dg, dbe = ln_bwd(u1_ref[...], du3_ref[...])
        dbuf[0:tm, :] = du
        dbuf[tm:tm + HALO, :] = ln_bwd(u1n_ref[...], du3n_ref[...])[0] * (i < nt - 1).astype(F32)

        @pl.when(i == 0)
        def _():
            dw_ref[...] = jnp.zeros_like(dw_ref)
            db_ref[...] = jnp.zeros_like(db_ref)
            dg_ref[...] = jnp.zeros_like(dg_ref)
            dbe_ref[...] = jnp.zeros_like(dbe_ref)

        dg_ref[...] += dg
        dbe_ref[...] += dbe

        read_u = _shifted_reader(ubuf, ushift, tm)
        read_d = _shifted_reader(dbuf, dshift, tm)
        du0 = jnp.zeros((tm, CONV_DIM), F32)
        for j in range(CONV_WIDTH):
            du0 = du0 + w_ref[j:j + 1, :] * read_d(CONV_WIDTH - 1 - j)
            dw_ref[j:j + 1, :] += jnp.sum(du * read_u(HALO - (CONV_WIDTH - 1) + j), axis=0, keepdims=True)
        db_ref[...] += jnp.sum(du, axis=0, keepdims=True)
        dci_ref[:, :CONV_DIM] = (du0 * sb).astype(dci_ref.dtype)
        dci_ref[:, CONV_DIM:] = (du0 * a * sb * (1.0 - sb)).astype(dci_ref.dtype)
        finish_exchange()

    res = pl.pallas_call(
        body, name="conv_bwd", grid=(nt,),
        in_specs=[pl.BlockSpec((tm, 2 * CONV_DIM), lambda i: (i, 0)),
                  pl.BlockSpec((HALO, 2 * CONV_DIM), lambda i: (jnp.maximum(i * ratio - 1, 0), 0))]
        + [pl.BlockSpec((tm, CONV_DIM), lambda i: (i, 0)),
           pl.BlockSpec((HALO, CONV_DIM), lambda i: (jnp.minimum((i + 1) * ratio, last_halo), 0))] * 2
        + [pl.BlockSpec((1, CONV_DIM), lambda i: (0, 0))] * 2 + [pl.BlockSpec(w_pad.shape, lambda i: (0, 0))]
        + [ANY] * nx,
        out_specs=[pl.BlockSpec((tm, 2 * CONV_DIM), lambda i: (i, 0)),
                   pl.BlockSpec(w_pad.shape, lambda i: (0, 0))]
        + [pl.BlockSpec((1, CONV_DIM), lambda i: (0, 0))] * 3 + [ANY] * nx,
        out_shape=[_sds((s, 2 * CONV_DIM), BF16), _sds(w_pad.shape)] + [_sds((1, CONV_DIM))] * 3 + x_shape,
        scratch_shapes=x_scratch + [pltpu.VMEM((tm + HALO, CONV_DIM), F32)] * 2
        + [pltpu.VMEM((SUBLANES - 1, tm + HALO - SUBLANES, CONV_DIM), F32)] * 2,
        compiler_params=_params(("arbitrary",), x_id),
    )(conv_in, conv_in, u1, u1, du3, du3, ln_g, ln_b, w_pad, *x_arrs)
    return res[:5], res[5:]


def _logsig_neg(z):
    return jnp.minimum(-z, 0.0) - jnp.log(1.0 + jnp.exp(-jnp.abs(z)))


def _split_dot(val, tri):
    hi = val.astype(BF16)
    lo = (val - hi.astype(F32)).astype(BF16)
    return jnp.dot(hi, tri, preferred_element_type=F32) + jnp.dot(lo, tri, preferred_element_type=F32)


def _attn_masks(t, later):
    row = lax.broadcasted_iota(jnp.int32, (t, t), 0)
    col = lax.broadcasted_iota(jnp.int32, (t, t), 1)
    tri = jnp.where(row > col if later else row <= col, 1.0, 0.0).astype(BF16)
    return col < row, tri


def _grid_marks(h, nq):
    hh, i = pl.program_id(0), pl.program_id(1)
    return ((hh == 0) & (i == 0), (hh == h - 1) & (i == nq // 2), (hh == h - 1) & (i == nq - 1),
            (hh == h // 2) & (i == nq // 2))


def _head_masks(shape):
    lane = lax.broadcasted_iota(jnp.int32, shape, len(shape) - 1)
    return lane < HEAD_DIM, lane >= HEAD_DIM


def _per_head(blk):
    m0, m1 = _head_masks(blk.shape)
    zero = jnp.zeros_like(blk)
    return jnp.where(m0, blk, zero), jnp.where(m1, blk, zero)


NT = (((1,), (1,)), ((), ()))
TN = (((0,), (0,)), ((), ()))


def _with_top(whole, top):
    rows = top.shape[0]
    return top if rows == whole.shape[0] else jnp.concatenate([top, whole[rows:]], axis=0)


def _attn_fwd(q, k, v, exchange):
    s = q.shape[0]
    hp = q.shape[1] // LANES
    t = ATT_TILE
    scale = 1.0 / math.sqrt(HEAD_DIM)
    x_arrs, x_shape, x_scratch, _, x_id = exchange
    nx = len(x_arrs)

    def body(*refs):
        q_ref, k_ref, v_ref = refs[:3]
        o_ref, lt_ref, nb_ref = refs[3 + nx:6 + nx]
        finish_exchange = _carry_exchange(exchange, refs, 3, 3, *_grid_marks(hp, s // t))
        i = pl.program_id(1)
        qs = _per_head((q_ref[...].astype(F32) * scale).astype(BF16))
        causal, tri = _attn_masks(t, later=True)

        def step(kb, carry, masked, rows):
            cs, acc = carry
            off = pl.multiple_of(kb * t, t)
            kblk = k_ref[pl.ds(off, t), :]
            vs = _per_head(v_ref[pl.ds(off, t), :])
            acc_top = acc[:rows]
            new_cs = []
            for hd in range(2):
                z = lax.dot_general(qs[hd][:rows], kblk, NT, preferred_element_type=F32)
                l = _logsig_neg(z)
                if masked:
                    l = jnp.where(causal, l, 0.0)
                e = z + l + _split_dot(l, tri) + cs[hd][:rows]
                if masked:
                    e = jnp.where(causal, e, -1e30)
                acc_top = acc_top + jnp.dot(jnp.exp(e).astype(BF16), vs[hd], preferred_element_type=F32)
                new_cs.append(_with_top(cs[hd], cs[hd][:rows] + jnp.sum(l, axis=1, keepdims=True)))
            return tuple(new_cs), _with_top(acc, acc_top)

        zero = jnp.zeros((t, 1), F32)
        carry = step(i, ((zero, zero), jnp.zeros((t, LANES), F32)), True, t)

        carry = lax.fori_loop(0, i, lambda n, cr: step(i - 1 - n, cr, False, t), carry)
        n_blocks = n_full = i
        m0, _ = _head_masks((t, LANES))
        lt_ref[...] = jnp.where(m0, carry[0][0], carry[0][1])
        o_ref[...] = carry[1].astype(o_ref.dtype)
        nb_ref[0, pl.program_id(0), i] = n_blocks.astype(F32)
        nb_ref[1, pl.program_id(0), i] = n_full.astype(F32)
        finish_exchange()

    res = pl.pallas_call(
        body, name="attn_fwd", grid=(hp, s // t),
        in_specs=[pl.BlockSpec((t, LANES), lambda p, i: (i, p)),
                  pl.BlockSpec((s, LANES), lambda p, i: (0, p)),
                  pl.BlockSpec((s, LANES), lambda p, i: (0, p))] + [ANY] * nx,
        out_specs=[pl.BlockSpec((t, LANES), lambda p, i: (i, p)),
                   pl.BlockSpec((None, t, LANES), lambda p, i: (p, i, 0)),
                   pl.BlockSpec(memory_space=pltpu.SMEM)] + [ANY] * nx,
        out_shape=[_sds(q.shape, BF16), _sds((hp, s, LANES), F32), _sds((2, hp, s // t), F32)] + x_shape,
        scratch_shapes=x_scratch,
        compiler_params=_params(("arbitrary", "arbitrary"), x_id),
    )(q, k, v, *x_arrs)
    return res[0], res[1], res[2], res[3:]


def _attn_bwd(q, k, v, do, ltot, n_blocks, exchange):
    s = q.shape[0]
    hp = q.shape[1] // LANES
    t = ATT_TILE
    scale = 1.0 / math.sqrt(HEAD_DIM)
    x_arrs, x_shape, x_scratch, _, x_id = exchange
    nx = len(x_arrs)

    def body(*refs):
        q_ref, k_ref, v_ref, do_ref, lt_ref, nb_ref = refs[:6]
        dq_ref, dk_ref, dv_ref = refs[6 + nx:9 + nx]
        finish_exchange = _carry_exchange(exchange, refs, 6, 3, *_grid_marks(hp, s // t))
        i = pl.program_id(1)
        n_blocks = n_full = i

        @pl.when(i == 0)
        def _():
            dk_ref[...] = jnp.zeros_like(dk_ref)
            dv_ref[...] = jnp.zeros_like(dv_ref)

        qb = q_ref[...]
        qm = _per_head(qb)
        qs = _per_head((qb.astype(F32) * scale).astype(BF16))
        dos = _per_head(do_ref[...])
        lts = (lt_ref[:, 0:1], lt_ref[:, HEAD_DIM:HEAD_DIM + 1])
        causal, tri = _attn_masks(t, later=False)

        def step(kb, carry, masked, rows):
            cls, cgs, dq = carry
            off = pl.multiple_of(kb * t, t)
            kblk = k_ref[pl.ds(off, t), :]
            vblk = v_ref[pl.ds(off, t), :]
            ks = _per_head(kblk)
            dq_top = dq[:rows]
            dk = jnp.zeros((t, LANES), F32)
            dv = jnp.zeros((t, LANES), F32)
            new_cls, new_cgs = [], []
            for hd in range(2):
                z = lax.dot_general(qs[hd][:rows], kblk, NT, preferred_element_type=F32)
                l = _logsig_neg(z)
                if masked:
                    l = jnp.where(causal, l, 0.0)
                e = z + l + ((lts[hd][:rows] - cls[hd][:rows]) - _split_dot(l, tri))
                if masked:
                    e = jnp.where(causal, e, -1e30)
                a = jnp.exp(e)
                g = lax.dot_general(dos[hd][:rows], vblk, NT, preferred_element_type=F32) * a
                p = cgs[hd][:rows] + jnp.dot(g.astype(BF16), tri, preferred_element_type=F32) - g
                el = jnp.exp(l)
                dz = g * el - p * (1.0 - el)
                if masked:
                    dz = jnp.where(causal, dz, 0.0)
                dzb = (dz * scale).astype(BF16)
                dq_top = dq_top + jnp.dot(dzb, ks[hd], preferred_element_type=F32)
                dk = dk + lax.dot_general(dzb, qm[hd][:rows], TN, preferred_element_type=F32)
                dv = dv + lax.dot_general(a.astype(BF16), dos[hd][:rows], TN, preferred_element_type=F32)
                new_cls.append(_with_top(cls[hd], cls[hd][:rows] + jnp.sum(l, axis=1, keepdims=True)))
                new_cgs.append(_with_top(cgs[hd], cgs[hd][:rows] + jnp.sum(g, axis=1, keepdims=True)))
            dk_ref[pl.ds(off, t), :] += dk
            dv_ref[pl.ds(off, t), :] += dv
            return tuple(new_cls), tuple(new_cgs), _with_top(dq, dq_top)

        zero = jnp.zeros((t, 1), F32)
        init = ((zero, zero), (zero, zero), jnp.zeros((t, LANES), F32))
        carry = lax.fori_loop(i - n_blocks, i - n_full, lambda kb, cr: step(kb, cr, False, ATT_PART), init)
        carry = lax.fori_loop(i - n_full, i, lambda kb, cr: step(kb, cr, False, t), carry)
        carry = step(i, carry, True, t)
        dq_ref[...] = carry[2]
        finish_exchange()

    blk = pl.BlockSpec((t, LANES), lambda p, i: (i, p))
    whole = pl.BlockSpec((s, LANES), lambda p, i: (0, p))
    res = pl.pallas_call(
        body, name="attn_bwd", grid=(hp, s // t),
        in_specs=[blk, whole, whole, blk, pl.BlockSpec((None, t, LANES), lambda p, i: (p, i, 0)),
                  pl.BlockSpec(memory_space=pltpu.SMEM)] + [ANY] * nx,
        out_specs=[blk, whole, whole] + [ANY] * nx,
        out_shape=[_sds(q.shape)] * 3 + x_shape,
        scratch_shapes=x_scratch,
        compiler_params=_params(("arbitrary", "arbitrary"), x_id),
    )(q, k, v, do, ltot, n_blocks, *x_arrs)
    return res[0], res[1], res[2], res[3:]


LATE = ["w_conv_branch", "w_att_branch", "w_out", "w_ffn_up", "w_ffn_down"]


def _full_weight(name, gathered):
    return _cols_to_full(gathered) if name in COL_SHARDED else gathered.reshape(-1, gathered.shape[2])


def _grad_slabs(name, grad):
    return _full_to_cols(grad) if name in COL_SHARDED else grad.reshape(N_DEV, -1, grad.shape[1])


def _side_slabs(name, grad):
    slabs = _grad_slabs(name, grad)
    return slabs.reshape((4, 2) + slabs.shape[1:])


def _local_step(x, target, w, late_blocks, opt):
    s = x.shape[0]
    w = dict(w)
    g1, g2, g3, g4 = w["norm_mix_pre"], w["norm_mix_post"], w["norm_ffn_pre"], w["norm_ffn_post"]

    w_in = w["w_in"]

    def proj_fn(xt, g1_, w_in_t):
        h = _rms(xt, g1_).astype(BF16)
        proj = lax.dot_general(h, w_in_t, NT, preferred_element_type=F32)
        return (h, *[proj[:, IN_SPLITS[n]:IN_SPLITS[n + 1]] for n in range(6)]), ()

    mix_weights = ["w_conv_branch", "w_att_branch", "w_out"]
    h1, conv_in, q, k, v, g_conv, g_att, g_out = _rowwise(
        "norm_proj", proj_fn, [x], [g1, w_in],
        [_sds((s, D_MODEL), BF16), _sds((s, 2 * CONV_DIM)), _sds((s, ATT_DIM), BF16), _sds((s, ATT_DIM), BF16),
         _sds((s, ATT_DIM), BF16), _sds((s, D_MODEL), BF16), _sds((s, D_MODEL), BF16)], tm=512,
        exchange=_relay_gather_exchange([late_blocks["w_out"]]))

    u3, u1, g_branches = _conv_fwd(conv_in, w["conv_dw_w"], w["conv_dw_b"], w["conv_ln_g"], w["conv_ln_b"],
                                   _relay_gather_exchange([late_blocks[nm] for nm in mix_weights[:2]]))
    for nm, g in zip(mix_weights, [*g_branches, g_out]):
        w[nm] = _full_weight(nm, g)
    half = D_MODEL // 2
    down_block = late_blocks["w_ffn_down"]
    att, ltot, n_blocks, (g_up, g_left) = _attn_fwd(
        q, k, v, _relay_gather_exchange([late_blocks["w_ffn_up"], down_block[:, :half]]))
    w["w_ffn_up"] = _full_weight("w_ffn_up", g_up)

    def merge_fn(u3t, at, gc, ga, xt, w_cb, w_ab, b_cb, w_out, g2_, g3_):
        cp = jnp.dot(u3t, w_cb, preferred_element_type=F32)
        ao = jnp.dot(at, w_ab, preferred_element_type=F32)
        mg = _merge(cp, ao, gc.astype(F32), ga.astype(F32), b_cb).astype(BF16)
        mix_ = jnp.dot(mg, w_out, preferred_element_type=F32)
        x2_ = xt + _rms(mix_, g2_)
        return (mg, cp, ao, mix_, x2_, _rms(x2_, g3_)), ()

    merged, conv_pre, att_out, mix, x2, h2 = _rowwise(
        "branch_merge_mix", merge_fn, [u3, att, g_conv, g_att, x],
        [w["w_conv_branch"], w["w_att_branch"], w["b_conv_branch"], w["w_out"], g2, g3],
        [_sds((s, D_MODEL), BF16)] * 3 + [_sds((s, D_MODEL)), _sds((s, D_MODEL)), _sds((s, D_MODEL), BF16)], tm=512)

    def ffn_up_fn(ht, w_up_t):
        gu_ = lax.dot_general(ht, w_up_t, NT, preferred_element_type=F32)
        return (gu_, _swiglu(gu_[:, :D_FF], gu_[:, D_FF:])), ()

    gu, act, g_right = _rowwise("ffn_up", ffn_up_fn, [h2], [w["w_ffn_up"]],
                                [_sds((s, 2 * D_FF), BF16), _sds((s, D_FF), BF16)], tm=512,
                                exchange=_relay_gather_exchange([down_block[:, half:]]))
    w_down = [_full_weight("w_ffn_down", g) for g in (g_left, g_right)]

    def final_fn(at, x2t, tgt, w_left, w_right, g4_):
        ff = jnp.concatenate([jnp.dot(at, w_left, preferred_element_type=F32),
                              jnp.dot(at, w_right, preferred_element_type=F32)], axis=1)
        n4, vjp = jax.vjp(_rms, ff, g4_)
        err = x2t + n4 - tgt
        dy = err * (1.0 / D_MODEL)
        dff, dg4 = vjp(dy)
        return (dy, dff), (jnp.sum(err * err, axis=0, keepdims=True), dg4)

    dy, dff, loss_cols, d_g4 = _rowwise("ffn_down_loss", final_fn, [act, x2, target], [*w_down, g4],
                                        [_sds((s, D_MODEL)), _sds((s, D_MODEL), BF16)],
                                        [_sds((1, D_MODEL)), _sds((1, D_MODEL))], tm=512)
    loss = 0.5 * jnp.sum(loss_cols) / D_MODEL

    d_w_down = _tn_matmul(act, dff, name="d_w_down")

    def act_bwd_fn(dfft, gut, w_left, w_right):
        d_act = (lax.dot_general(dfft[:, :half], w_left, NT, preferred_element_type=F32)
                 + lax.dot_general(dfft[:, half:], w_right, NT, preferred_element_type=F32))
        gu_ = gut.astype(F32)
        _, vjp = jax.vjp(_swiglu, gu_[:, :D_FF], gu_[:, D_FF:])
        return (jnp.concatenate(vjp(d_act), axis=1),), ()

    down_slabs = _side_slabs("w_ffn_down", d_w_down)
    dgu, theirs = _rowwise("ffn_act_bwd", act_bwd_fn, [dff, gu], w_down, [_sds((s, 2 * D_FF), BF16)],
                           exchange=_pair_exchange([down_slabs]))
    down_sums = _pair_sum("pair_sum_w_ffn_down", down_slabs, theirs)
    d_w_up = _tn_matmul(dgu, h2, name="d_w_up")
    received = {}
    up_slabs = _side_slabs("w_ffn_up", d_w_up)

    def mid_bwd_fn(dgut, xt, mt, dyt, w_up_t, g2_, g3_):
        dh = jnp.dot(dgut, w_up_t, preferred_element_type=F32)
        n2, vjp2 = jax.vjp(_rms, mt, g2_)
        x2_ = xt + n2
        _, vjp3 = jax.vjp(_rms, x2_, g3_)
        dx2_, dg3 = vjp3(dh)
        dx2_ = dx2_ + dyt
        dmix_, dg2 = vjp2(dx2_)
        return (dx2_, dmix_), (dg2, dg3)

    dx2, dmix, d_g2, d_g3, received["w_ffn_down"] = _rowwise(
        "ffn_up_mid_bwd", mid_bwd_fn, [dgu, x, mix, dy], [w["w_ffn_up"], g2, g3],
        [_sds((s, D_MODEL)), _sds((s, D_MODEL), BF16)], [_sds((1, D_MODEL)), _sds((1, D_MODEL))], tm=512,
        exchange=_chip_exchange([down_sums]))
    d_w_out = _tn_matmul(merged, dmix, name="d_w_out")

    def merge_bwd_fn(dmt, cp, ao, gc, ga, w_out, w_cb, w_ab, b_cb):
        dm = lax.dot_general(dmt, w_out, NT, preferred_element_type=F32)
        _, vjp = jax.vjp(_merge, cp.astype(F32), ao.astype(F32), gc.astype(F32), ga.astype(F32), b_cb)
        dcp, dao, dgc, dga, dbias = vjp(dm)
        dcp, dao = dcp.astype(BF16), dao.astype(BF16)
        du3_ = lax.dot_general(dcp, w_cb, NT, preferred_element_type=F32)
        datt_ = lax.dot_general(dao, w_ab, NT, preferred_element_type=F32)
        return (dcp, dao, dgc, dga, du3_, datt_), (dbias,)

    d_conv_out, d_att_out, d_g_conv, d_g_att, du3, d_att, d_b_cb, theirs = _rowwise(
        "merge_bwd", merge_bwd_fn, [dmix, conv_pre, att_out, g_conv, g_att],
        [w["w_out"], w["w_conv_branch"], w["w_att_branch"], w["b_conv_branch"]],
        [_sds((s, D_MODEL), BF16)] * 4 + [_sds((s, CONV_DIM)), _sds((s, ATT_DIM), BF16)], [_sds((1, D_MODEL))], tm=512,
        exchange=_pair_exchange([up_slabs]))

    d_w_cb = _tn_matmul(u3, d_conv_out, name="d_w_conv_branch")
    d_w_ab = _tn_matmul(att, d_att_out, name="d_w_att_branch")

    dq, dk, dv, (received["w_ffn_up"],) = _attn_bwd(
        q, k, v, d_att, ltot, n_blocks, _chip_exchange([_pair_sum("pair_sum_w_ffn_up", up_slabs, theirs)]))

    mix_grads = {"w_conv_branch": d_w_cb, "w_att_branch": d_w_ab, "w_out": d_w_out}
    (d_conv_in, d_dw_w, d_dw_b, d_ln_g, d_ln_b), landed = _conv_bwd(
        conv_in, u1, du3, w["conv_ln_g"], w["conv_ln_b"], w["conv_dw_w"],
        _scatter_exchange([_grad_slabs(nm, mix_grads[nm]) for nm in mix_weights[:2]]))
    received.update(zip(mix_weights[:2], landed))

    d_proj = [d_conv_in, dq, dk, dv, d_g_conv, d_g_att]
    d_w_in, (received["w_out"],) = _pieces_tn_matmul(
        d_proj, h1, name="d_w_in", exchange=_scatter_exchange([_grad_slabs("w_out", d_w_out)]))
    in_slabs = _side_slabs("w_in", d_w_in)
    (theirs,) = _exchange_call("pair_swap_w_in", _pair_exchange([in_slabs]))

    early = list(opt)

    def pre_bwd_fn(*args):
        groups, (xt, dx2t), jobs, (w_in_t, g_) = args[:6], args[6:8], args[8:-2], args[-2:]
        dh = sum(jnp.dot(grp.astype(BF16), w_in_t[IN_SPLITS[n]:IN_SPLITS[n + 1]], preferred_element_type=F32)
                 for n, grp in enumerate(groups))
        _, vjp = jax.vjp(_rms, xt, g_)
        dx_, dg_ = vjp(dh)
        updates = [_sum_adamw_tile(*jobs[4 * n:4 * n + 4]) for n in range(len(early))]
        return (dx_ + dx2t, *[u for four in updates for u in four]), (dg_,)

    res = _rowwise(
        "proj_norm_bwd", pre_bwd_fn,
        d_proj + [x, dx2] + [a for nm in early for a in (received[nm], *opt[nm])], [w_in, g1],
        [_sds((s, D_MODEL))] + [_sds(opt[nm][0].shape) for nm in early for _ in range(4)],
        [_sds((1, D_MODEL))], tm=512, exchange=_chip_exchange([_pair_sum("pair_sum_w_in", in_slabs, theirs)]))
    grad_x, d_g1, received["w_in"] = res[0], res[-2], res[-1]
    updated = {nm: res[1 + 4 * n:5 + 4 * n] for n, nm in enumerate(early)}

    grads = {
        "norm_mix_pre": d_g1, "conv_dw_w": d_dw_w, "conv_dw_b": d_dw_b,
        "conv_ln_g": d_ln_g, "conv_ln_b": d_ln_b, "b_conv_branch": d_b_cb,
        "norm_mix_post": d_g2, "norm_ffn_pre": d_g3, "norm_ffn_post": d_g4,
    }
    return loss, grad_x, received, updated, grads


def _place():
    x, y, c = lax.axis_index("x"), lax.axis_index("y"), lax.axis_index("c")
    return x, y, c


def _slot(px, py, pc):
    return 4 * px + 2 * py + pc


def _exchange_scratch(n):
    return [pltpu.SemaphoreType.DMA((7 * n,)), pltpu.SemaphoreType.DMA((7 * n,)), pltpu.SemaphoreType.DMA((n,))]


GATHER_ID, SCATTER_ID, PAIR_ID, CHIP_ID, RELAY_ID = 0, 1, 2, 3, 4


def _handshake(peers):
    barrier = pltpu.get_barrier_semaphore()
    for peer in peers:
        pl.semaphore_signal(barrier, inc=1, device_id=peer, device_id_type=MESH)
    pl.semaphore_wait(barrier, len(peers))


def _gather_exchange(arrs):
    n = len(arrs)

    def phases(ins, outs, send_sems, recv_sems, local_sems):
        x, y, c = _place()
        me, sibling = (x, y, c), (x, y, 1 - c)
        chips = [(1 - x, y), (x, 1 - y), (1 - x, 1 - y)]

        def copy(a, kk, block, to, src=None):
            dst = outs[a].at[_slot(*block)]
            return pltpu.make_async_remote_copy(
                src_ref=dst if src is None else src, dst_ref=dst,
                send_sem=send_sems.at[a * 7 + kk], recv_sem=recv_sems.at[a * 7 + kk],
                device_id=to, device_id_type=MESH)

        mine = [pltpu.make_async_copy(ins[a], outs[a].at[_slot(*me)], local_sems.at[a]) for a in range(n)]
        first = []
        for a in range(n):
            first.append(copy(a, 0, me, sibling, src=ins[a]))
            first += [copy(a, 1 + j, me, (*chip, c), src=ins[a]) for j, chip in enumerate(chips)]
        passed = [copy(a, 4 + j, (*chip, c), sibling) for j, chip in enumerate(chips) for a in range(n)]

        def send():
            _handshake([sibling] + [(*chip, c) for chip in chips])
            for cp in mine + first:
                cp.start()

        def pass_on():
            for j, chip in enumerate(chips):
                for a in range(n):
                    copy(a, 1 + j, (*chip, c), me).wait_recv()
                    passed[j * n + a].start()

        def finish():
            for a in range(n):
                copy(a, 0, sibling, me).wait_recv()
                for j, chip in enumerate(chips):
                    copy(a, 4 + j, (*chip, 1 - c), me).wait_recv()
            for cp in first + passed:
                cp.wait_send()
            for cp in mine:
                cp.wait()

        return [send, pass_on, finish]

    return list(arrs), [_sds((N_DEV,) + a.shape, a.dtype) for a in arrs], _exchange_scratch(n), phases, GATHER_ID


def _relay_gather_exchange(arrs):
    n = len(arrs)
    per = 8

    def phases(ins, outs, send_sems, recv_sems, local_sems):
        x, y, c = _place()
        me, sibling = (x, y, c), (x, y, 1 - c)
        beside, below, across = (1 - x, y, c), (x, 1 - y, c), (1 - x, 1 - y, c)

        def copy(a, kk, block, to, src=None, rows=None):
            where = _slot(*block) if rows is None else (_slot(*block), rows)
            dst = outs[a].at[where]
            return pltpu.make_async_remote_copy(
                src_ref=dst if src is None else src, dst_ref=dst,
                send_sem=send_sems.at[a * per + kk], recv_sem=recv_sems.at[a * per + kk],
                device_id=to, device_id_type=MESH)

        def halves(a):
            h = ins[a].shape[0] // 2
            return pl.ds(0, h), pl.ds(h, ins[a].shape[0] - h)

        mine = [pltpu.make_async_copy(ins[a], outs[a].at[_slot(*me)], local_sems.at[a]) for a in range(n)]
        first = [copy(a, kk, me, to, src=ins[a]) for a in range(n) for kk, to in enumerate([sibling, beside, below])]
        relayed = [[copy(a, 3, beside, sibling), copy(a, 5, beside, below, rows=halves(a)[0])] for a in range(n)]
        relayed += [[copy(a, 4, below, sibling), copy(a, 6, below, beside, rows=halves(a)[1])] for a in range(n)]
        passed = [copy(a, 7, across, sibling) for a in range(n)]

        def send():
            _handshake([sibling, beside, below])
            for cp in mine + first:
                cp.start()

        def relay():
            for kk, block in ((1, beside), (2, below)):
                for a in range(n):
                    copy(a, kk, block, me).wait_recv()
                    for cp in relayed[(kk - 1) * n + a]:
                        cp.start()

        def pass_on():
            for a in range(n):
                copy(a, 5, across, me, rows=halves(a)[0]).wait_recv()
                copy(a, 6, across, me, rows=halves(a)[1]).wait_recv()
                passed[a].start()

        def finish():
            for a in range(n):
                for kk, block in ((0, me), (3, beside), (4, below), (7, across)):
                    copy(a, kk, (*block[:2], 1 - c), me).wait_recv()
            for cp in first + [cp for two in relayed for cp in two] + passed:
                cp.wait_send()
            for cp in mine:
                cp.wait()

        return [send, relay, pass_on, finish]

    scratch = [pltpu.SemaphoreType.DMA((per * n,)), pltpu.SemaphoreType.DMA((per * n,)), pltpu.SemaphoreType.DMA((n,))]
    return list(arrs), [_sds((N_DEV,) + a.shape, a.dtype) for a in arrs], scratch, phases, RELAY_ID


def _scatter_exchange(arrs):
    n = len(arrs)
    flips = [(fx, fy, fc) for fx in (0, 1) for fy in (0, 1) for fc in (0, 1)][1:]

    def phases(ins, outs, send_sems, recv_sems, local_sems):
        x, y, c = _place()
        mine = _slot(x, y, c)
        local = [pltpu.make_async_copy(ins[a].at[mine], outs[a].at[mine], local_sems.at[a]) for a in range(n)]
        peers = [((1 - x) if fx else x, (1 - y) if fy else y, (1 - c) if fc else c) for fx, fy, fc in flips]

        def copy(a, kk, src_slot, dst_slot):
            return pltpu.make_async_remote_copy(
                src_ref=ins[a].at[src_slot], dst_ref=outs[a].at[dst_slot],
                send_sem=send_sems.at[a * 7 + kk], recv_sem=recv_sems.at[a * 7 + kk],
                device_id=peers[kk], device_id_type=MESH)

        sends = [copy(a, kk, _slot(*peers[kk]), mine) for a in range(n) for kk in range(7)]

        def send():
            _handshake(peers)
            for cp in local + sends:
                cp.start()

        def finish():
            for a in range(n):
                for kk in range(7):
                    copy(a, kk, mine, _slot(*peers[kk])).wait_recv()
            for cp in sends:
                cp.wait_send()
            for cp in local:
                cp.wait()

        return [send, finish]

    return list(arrs), [_sds(a.shape, a.dtype) for a in arrs], _exchange_scratch(n), phases, SCATTER_ID


def _pair_exchange(arrs):
    n = len(arrs)

    def phases(ins, outs, send_sems, recv_sems, local_sems):
        x, y, c = _place()

        def copy(a, chip, side):
            return pltpu.make_async_remote_copy(
                src_ref=ins[a].at[chip, side], dst_ref=outs[a].at[chip],
                send_sem=send_sems.at[a * 7 + chip], recv_sem=recv_sems.at[a * 7 + chip],
                device_id=(x, y, 1 - c), device_id_type=MESH)

        sends = [copy(a, chip, 1 - c) for a in range(n) for chip in range(4)]

        def send():
            _handshake([(x, y, 1 - c)])
            for cp in sends:
                cp.start()

        def finish():
            for a in range(n):
                for chip in range(4):
                    copy(a, chip, c).wait_recv()
            for cp in sends:
                cp.wait_send()

        return [send, finish]

    return list(arrs), [_sds((4,) + a.shape[2:], a.dtype) for a in arrs], _exchange_scratch(n), phases, PAIR_ID


def _chip_exchange(arrs):
    n = len(arrs)

    def phases(ins, outs, send_sems, recv_sems, local_sems):
        x, y, c = _place()
        mine = 2 * x + y
        chips = [(1 - x, y), (x, 1 - y), (1 - x, 1 - y)]
        local = [pltpu.make_async_copy(ins[a].at[mine], outs[a].at[mine], local_sems.at[a]) for a in range(n)]

        def copy(a, j, src_slot, dst_slot):
            return pltpu.make_async_remote_copy(
                src_ref=ins[a].at[src_slot], dst_ref=outs[a].at[dst_slot],
                send_sem=send_sems.at[a * 7 + j], recv_sem=recv_sems.at[a * 7 + j],
                device_id=(*chips[j], c), device_id_type=MESH)

        sends = [copy(a, j, 2 * chips[j][0] + chips[j][1], mine) for a in range(n) for j in range(3)]

        def send():
            _handshake([(*chip, c) for chip in chips])
            for cp in local + sends:
                cp.start()

        def finish():
            for a in range(n):
                for j in range(3):
                    copy(a, j, mine, 2 * chips[j][0] + chips[j][1]).wait_recv()
            for cp in sends:
                cp.wait_send()
            for cp in local:
                cp.wait()

        return [send, finish]

    return list(arrs), [_sds(a.shape, a.dtype) for a in arrs], _exchange_scratch(n), phases, CHIP_ID


def _pair_sum(name, mine, theirs):
    _, _, r, c = mine.shape

    def body(side_ref, m_ref, t_ref, o_ref):
        o_ref[...] = (m_ref[...].astype(F32) + t_ref[...].astype(F32)).astype(o_ref.dtype)

    return pl.pallas_call(
        body, name=name,
        grid_spec=pltpu.PrefetchScalarGridSpec(
            num_scalar_prefetch=1, grid=(4,),
            in_specs=[pl.BlockSpec((None, None, r, c), lambda j, side: (j, side[0], 0, 0)),
                      pl.BlockSpec((None, r, c), lambda j, side: (j, 0, 0))],
            out_specs=pl.BlockSpec((None, r, c), lambda j, side: (j, 0, 0))),
        out_shape=_sds(theirs.shape, theirs.dtype),
        compiler_params=_params(("parallel",)),
    )(lax.axis_index("c").astype(jnp.int32).reshape(1), mine, theirs)


def _exchange_call(name, exchange):
    arrs, out_shape, scratch, phases, collective_id = exchange
    n = len(arrs)

    def body(*refs):
        for step in phases(refs[:n], refs[n:2 * n], *refs[2 * n:]):
            step()

    return pl.pallas_call(body, name=name, in_specs=[ANY] * n, out_specs=[ANY] * n,
                          out_shape=out_shape, scratch_shapes=scratch,
                          compiler_params=pltpu.CompilerParams(collective_id=collective_id))(*arrs)


def _carry_exchange(exchange, refs, n_in, n_out, first, middle, last, halfway):
    arrs, _, _, phases, _ = exchange
    n = len(arrs)
    if n == 0:
        return lambda: None
    ins = refs[n_in:n_in + n]
    outs = refs[n_in + n + n_out:n_in + 2 * n + n_out]
    sems = n_in + 2 * n + n_out
    steps = phases(ins, outs, *refs[sems:sems + 3])
    pl.when(first)(steps[0])
    if len(steps) == 4:
        pl.when(halfway)(steps[1])
    if len(steps) >= 3:
        pl.when(middle)(steps[-2])
    return lambda: pl.when(last)(steps[-1])


def _adamw_math(w, g, m, v):
    m2 = ADAM_B1 * m + (1.0 - ADAM_B1) * g
    v2 = ADAM_B2 * v + (1.0 - ADAM_B2) * jnp.square(g)
    m_hat = m2 / (1.0 - ADAM_B1 ** ADAM_STEP)
    v_hat = v2 / (1.0 - ADAM_B2 ** ADAM_STEP)
    delta = -ADAM_LR * (m_hat / (jnp.sqrt(v_hat) + ADAM_EPS) + ADAM_WD * w)
    return delta, m2, v2


def _sum_adamw_tile(parts, w, m, v):
    g = parts[0].astype(F32)
    for d in range(1, parts.shape[0]):
        g = g + parts[d].astype(F32)
    return (g, *_adamw_math(w, g, m, v))


def _sum_adamw(name, parts, w, m, v, tr=256):
    p, r, c = parts.shape
    tr = _pick(r, tr, 16)

    def body(p_ref, w_ref, m_ref, v_ref, g_ref, d_ref, m2_ref, v2_ref):
        g_ref[...], d_ref[...], m2_ref[...], v2_ref[...] = _sum_adamw_tile(p_ref[...], w_ref[...], m_ref[...], v_ref[...])

    tile = pl.BlockSpec((tr, c), lambda i: (i, 0))
    return pl.pallas_call(
        body, name=name, grid=(r // tr,),
        in_specs=[pl.BlockSpec((p, tr, c), lambda i: (0, i, 0)), tile, tile, tile],
        out_specs=[tile] * 4, out_shape=[_sds((r, c))] * 4,
        compiler_params=_params(("parallel",)),
    )(parts, w, m, v)


def _sum_parts(name, parts):
    p, r, c = parts.shape

    def body(p_ref, o_ref):
        g = p_ref[0]
        for d in range(1, p):
            g = g + p_ref[d]
        o_ref[...] = g

    return pl.pallas_call(
        body, name=name, out_shape=_sds((r, c)),
        in_specs=[pl.BlockSpec(memory_space=pltpu.VMEM)], out_specs=pl.BlockSpec(memory_space=pltpu.VMEM),
    )(parts)


WEIGHTS = ["norm_mix_pre", "w_in", "conv_dw_w", "conv_dw_b", "conv_ln_g", "conv_ln_b", "w_conv_branch",
           "b_conv_branch", "w_att_branch", "w_out", "norm_mix_post", "norm_ffn_pre", "w_ffn_up", "w_ffn_down",
           "norm_ffn_post"]
COL_SHARDED = ["w_conv_branch", "w_att_branch"]
TRANSPOSED = ["w_in", "w_ffn_up"]
VECTORS = ["norm_mix_pre", "conv_dw_b", "conv_ln_g", "conv_ln_b", "b_conv_branch", "norm_mix_post",
           "norm_ffn_pre", "norm_ffn_post"]


def _cols_to_full(g):
    return g.transpose(1, 0, 2).reshape(g.shape[1], N_DEV * g.shape[2])


def _full_to_cols(f):
    return f.reshape(f.shape[0], N_DEV, f.shape[1] // N_DEV).transpose(1, 0, 2)


PACK_ROWS = 7


def _pack_vectors(vecs, extra=None):
    parts = [vecs[nm].reshape(-1) for nm in VECTORS]
    parts.append(jnp.zeros((1,), F32) if extra is None else extra.reshape(1))
    used = sum(p.size for p in parts)
    parts.append(jnp.zeros((PACK_ROWS * D_MODEL - used,), F32))
    return jnp.concatenate(parts).reshape(PACK_ROWS, D_MODEL)


def _unpack_vectors(packed, sizes):
    flat, out, at = packed.reshape(-1), {}, 0
    for nm in VECTORS:
        out[nm] = flat[at:at + sizes[nm]]
        at += sizes[nm]
    return out, flat[at]


def kernel(x, norm_mix_pre, w_in, conv_dw_w, conv_dw_b, conv_ln_g, conv_ln_b, w_conv_branch, b_conv_branch, w_att_branch, w_out, norm_mix_post, norm_ffn_pre, w_ffn_up, w_ffn_down, norm_ffn_post, loss_target, m_norm_mix_pre, m_w_in, m_conv_dw_w, m_conv_dw_b, m_conv_ln_g, m_conv_ln_b, m_w_conv_branch, m_b_conv_branch, m_w_att_branch, m_w_out, m_norm_mix_post, m_norm_ffn_pre, m_w_ffn_up, m_w_ffn_down, m_norm_ffn_post, v_norm_mix_pre, v_w_in, v_conv_dw_w, v_conv_dw_b, v_conv_ln_g, v_conv_ln_b, v_w_conv_branch, v_b_conv_branch, v_w_att_branch, v_w_out, v_norm_mix_post, v_norm_ffn_pre, v_w_ffn_up, v_w_ffn_down, v_norm_ffn_post):
    ws = dict(zip(WEIGHTS, [norm_mix_pre, w_in, conv_dw_w, conv_dw_b, conv_ln_g, conv_ln_b, w_conv_branch,
                            b_conv_branch, w_att_branch, w_out, norm_mix_post, norm_ffn_pre, w_ffn_up, w_ffn_down,
                            norm_ffn_post]))
    ms = dict(zip(WEIGHTS, [m_norm_mix_pre, m_w_in, m_conv_dw_w, m_conv_dw_b, m_conv_ln_g, m_conv_ln_b,
                            m_w_conv_branch, m_b_conv_branch, m_w_att_branch, m_w_out, m_norm_mix_post,
                            m_norm_ffn_pre, m_w_ffn_up, m_w_ffn_down, m_norm_ffn_post]))
    vs = dict(zip(WEIGHTS, [v_norm_mix_pre, v_w_in, v_conv_dw_w, v_conv_dw_b, v_conv_ln_g, v_conv_ln_b,
                            v_w_conv_branch, v_b_conv_branch, v_w_att_branch, v_w_out, v_norm_mix_post,
                            v_norm_ffn_pre, v_w_ffn_up, v_w_ffn_down, v_norm_ffn_post]))

    dw_block = jnp.pad(conv_dw_w, ((0, 1), (0, 0)))
    g_in, g_dw = _exchange_call("gather_first", _relay_gather_exchange([w_in.T.astype(BF16), dw_block]))
    full = {"w_in": _full_weight("w_in", g_in), "conv_dw_w": _cols_to_full(g_dw)}
    for nm in VECTORS:
        full[nm] = ws[nm].reshape(1, -1)

    def as_kept(nm, a):
        return a.T if nm in TRANSPOSED else a

    ride_along = ["w_ffn_up", "w_out"]
    loss_local, grad_x, received, updated, grads = _local_step(
        x[0], loss_target[0], full, {nm: as_kept(nm, ws[nm]).astype(BF16) for nm in LATE},
        {nm: tuple(as_kept(nm, a[nm]) for a in (ws, ms, vs)) for nm in ride_along})

    small = _exchange_call("gather_small_grads", _gather_exchange(
        [_pack_vectors(grads, extra=loss_local), grads["conv_dw_w"]]))
    out_g, out_d, out_m, out_v = {}, {}, {}, {}
    for nm in LATE + ["w_in"]:
        res = updated[nm] if nm in updated else _sum_adamw(
            "adamw_" + nm, received[nm], *[as_kept(nm, a[nm]) for a in (ws, ms, vs)])
        out_g[nm], out_d[nm], out_m[nm], out_v[nm] = [as_kept(nm, r) for r in res]
    sizes = {nm: ws[nm].size for nm in VECTORS}
    vec = _sum_adamw("adamw_vectors", small[0], _pack_vectors(ws), _pack_vectors(ms), _pack_vectors(vs))
    for res, dst in zip(vec, (out_g, out_d, out_m, out_v)):
        dst.update(_unpack_vectors(res, sizes)[0])
    loss = _unpack_vectors(vec[0], sizes)[1]
    dw_full = _sum_parts("sum_dw_grads", small[1])
    me = _slot(*_place())
    dw_mine = lax.dynamic_slice(dw_full, (0, me * (CONV_DIM // N_DEV)), (CONV_WIDTH, CONV_DIM // N_DEV))
    nm = "conv_dw_w"
    out_g[nm], out_d[nm], out_m[nm], out_v[nm] = _sum_adamw("adamw_dw", dw_mine[None], ws[nm], ms[nm], vs[nm])

    outs = [loss, grad_x[None]]
    for group in (out_g, out_d, out_m, out_v):
        outs += [group[nm] for nm in WEIGHTS]
    return tuple(outs)
```

```python
import math

import jax
import jax.numpy as jnp
from jax import lax
from jax.experimental import pallas as pl
from jax.experimental.pallas import tpu as pltpu

F32 = jnp.float32
BF16 = jnp.bfloat16

N_DEV = 8
D_MODEL = 1024
CONV_DIM = 512
CONV_WIDTH = 31
N_HEADS = 8
HEAD_DIM = 64
ATT_DIM = N_HEADS * HEAD_DIM
D_FF = 2816
EPS = 1e-6
IN_SPLITS = (0, 1024, 1536, 2048, 2560, 3584, 4608)

ADAM_LR = 0.001
ADAM_B1 = 0.9
ADAM_B2 = 0.999
ADAM_EPS = 1e-08
ADAM_WD = 0.01
ADAM_STEP = 10

LANES = 128
SUBLANES = 8
HALO = 32
ATT_TILE = 256
ATT_PART = 176
DEAD_SUM = -120.0
VMEM_LIMIT = 56 * 1024 * 1024
MESH = pl.DeviceIdType.MESH
ANY = pl.BlockSpec(memory_space=pl.ANY)


def _pick(dim, target, align=LANES):
    t = min(dim, target)
    t -= t % align
    while t >= align:
        if dim % t == 0:
            return t
        t -= align
    return dim


def _params(semantics, collective_id=None):
    return pltpu.CompilerParams(dimension_semantics=semantics, vmem_limit_bytes=VMEM_LIMIT,
                                collective_id=collective_id)


def _tn_matmul(a, b, *, name):
    return _pieces_tn_matmul([a], b, name=name, tj=_pick(a.shape[1], 1408))


def _pieces_tn_matmul(pieces, b, *, name, tj=512, exchange=None):
    s, n = b.shape
    counts = [p.shape[1] // tj for p in pieces]
    starts = [sum(counts[:i]) for i in range(len(pieces))]
    assert all(p.shape == (s, c * tj) for p, c in zip(pieces, counts))
    x_arrs, x_shape, x_scratch, _, x_id = exchange or NO_EXCHANGE
    nx, n_in = len(x_arrs), len(pieces) + 1

    def body(*refs):
        b_ref, o_ref = refs[n_in - 1], refs[n_in + nx]
        finish_exchange = _carry_exchange(exchange or NO_EXCHANGE, refs, n_in, 1, *_sweep_marks(sum(counts)))
        j = pl.program_id(0)
        for p_ref, first, count in zip(refs, starts, counts):
            @pl.when((j >= first) & (j < first + count))
            def _():
                o_ref[...] = lax.dot_general(p_ref[...].astype(BF16), b_ref[...], TN,
                                             preferred_element_type=F32).astype(o_ref.dtype)
        finish_exchange()

    def piece_spec(first, count):
        return pl.BlockSpec((s, tj), lambda j: (0, jnp.clip(j - first, 0, count - 1)))

    res = pl.pallas_call(
        body, name=name, grid=(sum(counts),),
        in_specs=[piece_spec(f, c) for f, c in zip(starts, counts)]
        + [pl.BlockSpec((s, n), lambda j: (0, 0), pipeline_mode=pl.Buffered(1))] + [ANY] * nx,
        out_specs=[pl.BlockSpec((tj, n), lambda j: (j, 0))] + [ANY] * nx,
        out_shape=[jax.ShapeDtypeStruct((sum(counts) * tj, n), BF16)] + x_shape, scratch_shapes=x_scratch,
        compiler_params=_params(("arbitrary",), x_id),
    )(*pieces, b, *x_arrs)
    return res[0] if exchange is None else (res[0], res[1:])


NO_EXCHANGE = ([], [], [], None, None)


def _sweep_marks(nt):
    i = pl.program_id(0)
    return i == 0, i == max(nt - 2, 0), i == nt - 1, i == max(nt // 2 - 1, 0)


def _rowwise(name, fn, rows, bcasts, row_outs, red_outs=(), tm=256, exchange=NO_EXCHANGE):
    s = rows[0].shape[0]
    tm = _pick(s, tm, 16)
    nt = s // tm
    resident = pl.Buffered(1)
    nr, nb, no, nd = len(rows), len(bcasts), len(row_outs), len(red_outs)
    x_arrs, x_shape, x_scratch, _, x_id = exchange
    nx = len(x_arrs)
    first_out = nr + nb + nx

    def body(*refs):
        finish_exchange = _carry_exchange(exchange, refs, nr + nb, no + nd, *_sweep_marks(nt))
        ins = [r[...] for r in refs[:nr + nb]]
        outs, reds = fn(*ins)
        for ref, val in zip(refs[first_out:first_out + no], outs):
            ref[...] = val.astype(ref.dtype)
        i = pl.program_id(0)
        for ref, val in zip(refs[first_out + no:first_out + no + nd], reds):
            @pl.when(i == 0)
            def _():
                ref[...] = val

            @pl.when(i > 0)
            def _():
                ref[...] += val
        finish_exchange()

    def row_spec(a):
        assert a.shape[-2] % nt == 0, (name, a.shape, nt)
        if len(a.shape) == 3:
            return pl.BlockSpec((a.shape[0], a.shape[1] // nt, a.shape[2]), lambda i: (0, i, 0))
        return pl.BlockSpec((a.shape[0] // nt, a.shape[1]), lambda i: (i, 0))

    in_specs = [row_spec(r) for r in rows]
    in_specs += [pl.BlockSpec(b.shape, lambda i: (0, 0), pipeline_mode=resident) for b in bcasts]
    out_specs = [row_spec(o) for o in row_outs]
    out_specs += [pl.BlockSpec(d.shape, lambda i: (0, 0)) for d in red_outs]
    return pl.pallas_call(
        body, name=name, grid=(nt,), in_specs=in_specs + [ANY] * nx, out_specs=out_specs + [ANY] * nx,
        out_shape=list(row_outs) + list(red_outs) + x_shape, scratch_shapes=x_scratch,
        compiler_params=_params(("arbitrary",), x_id),
    )(*rows, *bcasts, *x_arrs)


def _sds(shape, dtype=F32):
    return jax.ShapeDtypeStruct(shape, dtype)


def _rms(x, g):
    y = x * lax.rsqrt(jnp.mean(x * x, axis=-1, keepdims=True) + EPS)
    return y * g


def _silu(x):
    return x * jax.nn.sigmoid(x)


def _swiglu(g, u):
    return _silu(g) * u


def _ln_silu(u, g, b):
    mu = jnp.mean(u, axis=-1, keepdims=True)
    var = jnp.mean(jnp.square(u - mu), axis=-1, keepdims=True)
    return _silu((u - mu) * lax.rsqrt(var + EPS) * g + b)


def _merge(conv_pre, att_out, g_conv, g_att, b_cb):
    return jax.nn.sigmoid(g_conv) * (conv_pre + b_cb) + jax.nn.sigmoid(g_att) * att_out


def _glu(t):
    return t[:, :CONV_DIM] * jax.nn.sigmoid(t[:, CONV_DIM:])


def _shifted_reader(buf, shifted, tm):
    for b in range(1, SUBLANES):
        shifted[b - 1, :, :] = buf[pl.ds(b, tm + HALO - SUBLANES), :]

    def read(o):
        a, b = divmod(o, SUBLANES)
        return buf[pl.ds(SUBLANES * a, tm), :] if b == 0 else shifted[b - 1, pl.ds(SUBLANES * a, tm), :]

    return read


def _conv_fwd(conv_in, w_pad, b, ln_g, ln_b, exchange, tm=256):
    s = conv_in.shape[0]
    tm = _pick(s, tm, HALO)
    ratio = tm // HALO
    x_arrs, x_shape, x_scratch, _, x_id = exchange
    nx = len(x_arrs)

    def body(*refs):
        main_ref, halo_ref, w_ref, b_ref, g_ref, be_ref = refs[:6]
        u3_ref, u1_ref = refs[6 + nx:8 + nx]
        buf, shifted = refs[-2:]
        finish_exchange = _carry_exchange(exchange, refs, 6, 2, *_sweep_marks(s // tm))
        i = pl.program_id(0)
        buf[0:HALO, :] = _glu(halo_ref[...]) * (i > 0).astype(F32)
        buf[HALO:HALO + tm, :] = _glu(main_ref[...])
        read = _shifted_reader(buf, shifted, tm)
        acc = jnp.zeros((tm, CONV_DIM), F32) + b_ref[...]
        for j in range(CONV_WIDTH):
            acc = acc + w_ref[j:j + 1, :] * read(HALO - (CONV_WIDTH - 1) + j)
        u1_ref[...] = acc
        u3_ref[...] = _ln_silu(acc, g_ref[...], be_ref[...]).astype(u3_ref.dtype)
        finish_exchange()

    res = pl.pallas_call(
        body, name="conv_fwd", grid=(s // tm,),
        in_specs=[pl.BlockSpec((tm, 2 * CONV_DIM), lambda i: (i, 0)),
                  pl.BlockSpec((HALO, 2 * CONV_DIM), lambda i: (jnp.maximum(i * ratio - 1, 0), 0)),
                  pl.BlockSpec(w_pad.shape, lambda i: (0, 0)),
                  pl.BlockSpec(b.shape, lambda i: (0, 0)),
                  pl.BlockSpec(ln_g.shape, lambda i: (0, 0)),
                  pl.BlockSpec(ln_b.shape, lambda i: (0, 0))] + [ANY] * nx,
        out_specs=[pl.BlockSpec((tm, CONV_DIM), lambda i: (i, 0)),
                   pl.BlockSpec((tm, CONV_DIM), lambda i: (i, 0))] + [ANY] * nx,
        out_shape=[_sds((s, CONV_DIM), BF16), _sds((s, CONV_DIM), F32)] + x_shape,
        scratch_shapes=x_scratch + [pltpu.VMEM((tm + HALO, CONV_DIM), F32),
                                    pltpu.VMEM((SUBLANES - 1, tm + HALO - SUBLANES, CONV_DIM), F32)],
        compiler_params=_params(("arbitrary",), x_id),
    )(conv_in, conv_in, w_pad, b, ln_g, ln_b, *x_arrs)
    return res[0], res[1], res[2:]


def _conv_bwd(conv_in, u1, du3, ln_g, ln_b, w_pad, exchange, tm=256):
    s = conv_in.shape[0]
    tm = _pick(s, tm, HALO)
    ratio = tm // HALO
    nt = s // tm
    last_halo = s // HALO - 1
    x_arrs, x_shape, x_scratch, _, x_id = exchange
    nx = len(x_arrs)

    def body(*refs):
        main_ref, halo_ref, u1_ref, u1n_ref, du3_ref, du3n_ref, g_ref, be_ref, w_ref = refs[:9]
        dci_ref, dw_ref, db_ref, dg_ref, dbe_ref = refs[9 + nx:14 + nx]
        ubuf, dbuf, ushift, dshift = refs[-4:]
        finish_exchange = _carry_exchange(exchange, refs, 9, 5, *_sweep_marks(nt))
        i = pl.program_id(0)
        main = main_ref[...]
        a = main[:, :CONV_DIM]
        sb = jax.nn.sigmoid(main[:, CONV_DIM:])
        ubuf[0:HALO, :] = _glu(halo_ref[...]) * (i > 0).astype(F32)
        ubuf[HALO:HALO + tm, :] = a * sb

        def ln_bwd(u1t, du3t):
            _, vjp = jax.vjp(_ln_silu, u1t, g_ref[...], be_ref[...])
            return vjp(du3t)

        du, dg, dbe = ln_bwd(u1_ref[...], du3_ref[...])
        dbuf[0:tm, :] = du
        dbuf[tm:tm + HALO, :] = ln_bwd(u1n_ref[...], du3n_ref[...])[0] * (i < nt - 1).astype(F32)

        @pl.when(i == 0)
        def _():
            dw_ref[...] = jnp.zeros_like(dw_ref)
            db_ref[...] = jnp.zeros_like(db_ref)
            dg_ref[...] = jnp.zeros_like(dg_ref)
            dbe_ref[...] = jnp.zeros_like(dbe_ref)

        dg_ref[...] += dg
        dbe_ref[...] += dbe

        read_u = _shifted_reader(ubuf, ushift, tm)
        read_d = _shifted_reader(dbuf, dshift, tm)
        du0 = jnp.zeros((tm, CONV_DIM), F32)
        for j in range(CONV_WIDTH):
            du0 = du0 + w_ref[j:j + 1, :] * read_d(CONV_WIDTH - 1 - j)
            dw_ref[j:j + 1, :] += jnp.sum(du * read_u(HALO - (CONV_WIDTH - 1) + j), axis=0, keepdims=True)
        db_ref[...] += jnp.sum(du, axis=0, keepdims=True)
        dci_ref[:, :CONV_DIM] = (du0 * sb).astype(dci_ref.dtype)
        dci_ref[:, CONV_DIM:] = (du0 * a * sb * (1.0 - sb)).astype(dci_ref.dtype)
        finish_exchange()

    res = pl.pallas_call(
        body, name="conv_bwd", grid=(nt,),
        in_specs=[pl.BlockSpec((tm, 2 * CONV_DIM), lambda i: (i, 0)),
                  pl.BlockSpec((HALO, 2 * CONV_DIM), lambda i: (jnp.maximum(i * ratio - 1, 0), 0))]
        + [pl.BlockSpec((tm, CONV_DIM), lambda i: (i, 0)),
           pl.BlockSpec((HALO, CONV_DIM), lambda i: (jnp.minimum((i + 1) * ratio, last_halo), 0))] * 2
        + [pl.BlockSpec((1, CONV_DIM), lambda i: (0, 0))] * 2 + [pl.BlockSpec(w_pad.shape, lambda i: (0, 0))]
        + [ANY] * nx,
        out_specs=[pl.BlockSpec((tm, 2 * CONV_DIM), lambda i: (i, 0)),
                   pl.BlockSpec(w_pad.shape, lambda i: (0, 0))]
        + [pl.BlockSpec((1, CONV_DIM), lambda i: (0, 0))] * 3 + [ANY] * nx,
        out_shape=[_sds((s, 2 * CONV_DIM), BF16), _sds(w_pad.shape)] + [_sds((1, CONV_DIM))] * 3 + x_shape,
        scratch_shapes=x_scratch + [pltpu.VMEM((tm + HALO, CONV_DIM), F32)] * 2
        + [pltpu.VMEM((SUBLANES - 1, tm + HALO - SUBLANES, CONV_DIM), F32)] * 2,
        compiler_params=_params(("arbitrary",), x_id),
    )(conv_in, conv_in, u1, u1, du3, du3, ln_g, ln_b, w_pad, *x_arrs)
    return res[:5], res[5:]


def _logsig_neg(z):
    return jnp.minimum(-z, 0.0) - jnp.log(1.0 + jnp.exp(-jnp.abs(z)))


def _split_dot(val, tri):
    hi = val.astype(BF16)
    lo = (val - hi.astype(F32)).astype(BF16)
    return jnp.dot(hi, tri, preferred_element_type=F32) + jnp.dot(lo, tri, preferred_element_type=F32)


def _attn_masks(t, later):
    row = lax.broadcasted_iota(jnp.int32, (t, t), 0)
    col = lax.broadcasted_iota(jnp.int32, (t, t), 1)
    tri = jnp.where(row > col if later else row <= col, 1.0, 0.0).astype(BF16)
    return col < row, tri


def _grid_marks(h, nq):
    hh, i = pl.program_id(0), pl.program_id(1)
    return ((hh == 0) & (i == 0), (hh == h - 1) & (i == nq // 2), (hh == h - 1) & (i == nq - 1),
            (hh == h // 2) & (i == nq // 2))


def _head_masks(shape):
    lane = lax.broadcasted_iota(jnp.int32, shape, len(shape) - 1)
    return lane < HEAD_DIM, lane >= HEAD_DIM


def _per_head(blk):
    m0, m1 = _head_masks(blk.shape)
    zero = jnp.zeros_like(blk)
    return jnp.where(m0, blk, zero), jnp.where(m1, blk, zero)


NT = (((1,), (1,)), ((), ()))
TN = (((0,), (0,)), ((), ()))


def _with_top(whole, top):
    rows = top.shape[0]
    return top if rows == whole.shape[0] else jnp.concatenate([top, whole[rows:]], axis=0)


def _attn_fwd(q, k, v, exchange):
    s = q.shape[0]
    hp = q.shape[1] // LANES
    t = ATT_TILE
    scale = 1.0 / math.sqrt(HEAD_DIM)
    x_arrs, x_shape, x_scratch, _, x_id = exchange
    nx = len(x_arrs)

    def body(*refs):
        q_ref, k_ref, v_ref = refs[:3]
        o_ref, lt_ref, nb_ref = refs[3 + nx:6 + nx]
        finish_exchange = _carry_exchange(exchange, refs, 3, 3, *_grid_marks(hp, s // t))
        i = pl.program_id(1)
        qs = _per_head((q_ref[...].astype(F32) * scale).astype(BF16))
        causal, tri = _attn_masks(t, later=True)

        def step(kb, carry, masked, rows):
            cs, acc = carry
            off = pl.multiple_of(kb * t, t)
            kblk = k_ref[pl.ds(off, t), :]
            vs = _per_head(v_ref[pl.ds(off, t), :])
            acc_top = acc[:rows]
            new_cs = []
            for hd in range(2):
                z = lax.dot_general(qs[hd][:rows], kblk, NT, preferred_element_type=F32)
                l = _logsig_neg(z)
                if masked:
                    l = jnp.where(causal, l, 0.0)
                e = z + l + _split_dot(l, tri) + cs[hd][:rows]
                if masked:
                    e = jnp.where(causal, e, -1e30)
                acc_top = acc_top + jnp.dot(jnp.exp(e).astype(BF16), vs[hd], preferred_element_type=F32)
                new_cs.append(_with_top(cs[hd], cs[hd][:rows] + jnp.sum(l, axis=1, keepdims=True)))
            return tuple(new_cs), _with_top(acc, acc_top)

        zero = jnp.zeros((t, 1), F32)
        carry = step(i, ((zero, zero), jnp.zeros((t, LANES), F32)), True, t)

        def live(cs, lo, hi):
            return jnp.maximum(jnp.max(cs[0][lo:hi]), jnp.max(cs[1][lo:hi])) > DEAD_SUM

        def more(state):
            n, _, (cs, _) = state
            return (n < i) & live(cs, 0, t)

        def sweep(state):
            n, n_full, cr = state
            whole = live(cr[0], ATT_PART, t)
            cr = lax.cond(whole, lambda c: step(i - 1 - n, c, False, t), lambda c: step(i - 1 - n, c, False, ATT_PART), cr)
            return n + 1, n_full + whole.astype(jnp.int32), cr

        n_blocks, n_full, carry = lax.while_loop(more, sweep, (jnp.int32(0), jnp.int32(0), carry))
        m0, _ = _head_masks((t, LANES))
        lt_ref[...] = jnp.where(m0, carry[0][0], carry[0][1])
        o_ref[...] = carry[1].astype(o_ref.dtype)
        nb_ref[0, pl.program_id(0), i] = n_blocks.astype(F32)
        nb_ref[1, pl.program_id(0), i] = n_full.astype(F32)
        finish_exchange()

    res = pl.pallas_call(
        body, name="attn_fwd", grid=(hp, s // t),
        in_specs=[pl.BlockSpec((t, LANES), lambda p, i: (i, p)),
                  pl.BlockSpec((s, LANES), lambda p, i: (0, p)),
                  pl.BlockSpec((s, LANES), lambda p, i: (0, p))] + [ANY] * nx,
        out_specs=[pl.BlockSpec((t, LANES), lambda p, i: (i, p)),
                   pl.BlockSpec((None, t, LANES), lambda p, i: (p, i, 0)),
                   pl.BlockSpec(memory_space=pltpu.SMEM)] + [ANY] * nx,
        out_shape=[_sds(q.shape, BF16), _sds((hp, s, LANES), F32), _sds((2, hp, s // t), F32)] + x_shape,
        scratch_shapes=x_scratch,
        compiler_params=_params(("arbitrary", "arbitrary"), x_id),
    )(q, k, v, *x_arrs)
    return res[0], res[1], res[2], res[3:]


def _attn_bwd(q, k, v, do, ltot, n_blocks, exchange):
    s = q.shape[0]
    hp = q.shape[1] // LANES
    t = ATT_TILE
    scale = 1.0 / math.sqrt(HEAD_DIM)
    x_arrs, x_shape, x_scratch, _, x_id = exchange
    nx = len(x_arrs)

    def body(*refs):
        q_ref, k_ref, v_ref, do_ref, lt_ref, nb_ref = refs[:6]
        dq_ref, dk_ref, dv_ref = refs[6 + nx:9 + nx]
        finish_exchange = _carry_exchange(exchange, refs, 6, 3, *_grid_marks(hp, s // t))
        i = pl.program_id(1)
        n_blocks = jnp.clip(nb_ref[0, pl.program_id(0), i].astype(jnp.int32), 0, i)
        n_full = jnp.clip(nb_ref[1, pl.program_id(0), i].astype(jnp.int32), 0, n_blocks)

        @pl.when(i == 0)
        def _():
            dk_ref[...] = jnp.zeros_like(dk_ref)
            dv_ref[...] = jnp.zeros_like(dv_ref)

        qb = q_ref[...]
        qm = _per_head(qb)
        qs = _per_head((qb.astype(F32) * scale).astype(BF16))
        dos = _per_head(do_ref[...])
        lts = (lt_ref[:, 0:1], lt_ref[:, HEAD_DIM:HEAD_DIM + 1])
        causal, tri = _attn_masks(t, later=False)

        def step(kb, carry, masked, rows):
            cls, cgs, dq = carry
            off = pl.multiple_of(kb * t, t)
            kblk = k_ref[pl.ds(off, t), :]
            vblk = v_ref[pl.ds(off, t), :]
            ks = _per_head(kblk)
            dq_top = dq[:rows]
            dk = jnp.zeros((t, LANES), F32)
            dv = jnp.zeros((t, LANES), F32)
            new_cls, new_cgs = [], []
            for hd in range(2):
                z = lax.dot_general(qs[hd][:rows], kblk, NT, preferred_element_type=F32)
                l = _logsig_neg(z)
                if masked:
                    l = jnp.where(causal, l, 0.0)
                e = z + l + ((lts[hd][:rows] - cls[hd][:rows]) - _split_dot(l, tri))
                if masked:
                    e = jnp.where(causal, e, -1e30)
                a = jnp.exp(e)
                g = lax.dot_general(dos[hd][:rows], vblk, NT, preferred_element_type=F32) * a
                p = cgs[hd][:rows] + jnp.dot(g.astype(BF16), tri, preferred_element_type=F32) - g
                el = jnp.exp(l)
                dz = g * el - p * (1.0 - el)
                if masked:
                    dz = jnp.where(causal, dz, 0.0)
                dzb = (dz * scale).astype(BF16)
                dq_top = dq_top + jnp.dot(dzb, ks[hd], preferred_element_type=F32)
                dk = dk + lax.dot_general(dzb, qm[hd][:rows], TN, preferred_element_type=F32)
                dv = dv + lax.dot_general(a.astype(BF16), dos[hd][:rows], TN, preferred_element_type=F32)
                new_cls.append(_with_top(cls[hd], cls[hd][:rows] + jnp.sum(l, axis=1, keepdims=True)))
                new_cgs.append(_with_top(cgs[hd], cgs[hd][:rows] + jnp.sum(g, axis=1, keepdims=True)))
            dk_ref[pl.ds(off, t), :] += dk
            dv_ref[pl.ds(off, t), :] += dv
            return tuple(new_cls), tuple(new_cgs), _with_top(dq, dq_top)

        zero = jnp.zeros((t, 1), F32)
        init = ((zero, zero), (zero, zero), jnp.zeros((t, LANES), F32))
        carry = lax.fori_loop(i - n_blocks, i - n_full, lambda kb, cr: step(kb, cr, False, ATT_PART), init)
        carry = lax.fori_loop(i - n_full, i, lambda kb, cr: step(kb, cr, False, t), carry)
        carry = step(i, carry, True, t)
        dq_ref[...] = carry[2]
        finish_exchange()

    blk = pl.BlockSpec((t, LANES), lambda p, i: (i, p))
    whole = pl.BlockSpec((s, LANES), lambda p, i: (0, p))
    res = pl.pallas_call(
        body, name="attn_bwd", grid=(hp, s // t),
        in_specs=[blk, whole, whole, blk, pl.BlockSpec((None, t, LANES), lambda p, i: (p, i, 0)),
                  pl.BlockSpec(memory_space=pltpu.SMEM)] + [ANY] * nx,
        out_specs=[blk, whole, whole] + [ANY] * nx,
        out_shape=[_sds(q.shape)] * 3 + x_shape,
        scratch_shapes=x_scratch,
        compiler_params=_params(("arbitrary", "arbitrary"), x_id),
    )(q, k, v, do, ltot, n_blocks, *x_arrs)
    return res[0], res[1], res[2], res[3:]


LATE = ["w_conv_branch", "w_att_branch", "w_out", "w_ffn_up", "w_ffn_down"]


def _full_weight(name, gathered):
    return _cols_to_full(gathered) if name in COL_SHARDED else gathered.reshape(-1, gathered.shape[2])


def _grad_slabs(name, grad):
    return _full_to_cols(grad) if name in COL_SHARDED else grad.reshape(N_DEV, -1, grad.shape[1])


def _side_slabs(name, grad):
    slabs = _grad_slabs(name, grad)
    return slabs.reshape((4, 2) + slabs.shape[1:])


def _local_step(x, target, w, late_blocks, opt):
    s = x.shape[0]
    w = dict(w)
    g1, g2, g3, g4 = w["norm_mix_pre"], w["norm_mix_post"], w["norm_ffn_pre"], w["norm_ffn_post"]

    w_in = w["w_in"]

    def proj_fn(xt, g1_, w_in_t):
        h = _rms(xt, g1_).astype(BF16)
        proj = lax.dot_general(h, w_in_t, NT, preferred_element_type=F32)
        return (h, *[proj[:, IN_SPLITS[n]:IN_SPLITS[n + 1]] for n in range(6)]), ()

    mix_weights = ["w_conv_branch", "w_att_branch", "w_out"]
    h1, conv_in, q, k, v, g_conv, g_att, g_out = _rowwise(
        "norm_proj", proj_fn, [x], [g1, w_in],
        [_sds((s, D_MODEL), BF16), _sds((s, 2 * CONV_DIM)), _sds((s, ATT_DIM), BF16), _sds((s, ATT_DIM), BF16),
         _sds((s, ATT_DIM), BF16), _sds((s, D_MODEL), BF16), _sds((s, D_MODEL), BF16)], tm=512,
        exchange=_relay_gather_exchange([late_blocks["w_out"]]))

    u3, u1, g_branches = _conv_fwd(conv_in, w["conv_dw_w"], w["conv_dw_b"], w["conv_ln_g"], w["conv_ln_b"],
                                   _relay_gather_exchange([late_blocks[nm] for nm in mix_weights[:2]]))
    for nm, g in zip(mix_weights, [*g_branches, g_out]):
        w[nm] = _full_weight(nm, g)
    half = D_MODEL // 2
    down_block = late_blocks["w_ffn_down"]
    att, ltot, n_blocks, (g_up, g_left) = _attn_fwd(
        q, k, v, _relay_gather_exchange([late_blocks["w_ffn_up"], down_block[:, :half]]))
    w["w_ffn_up"] = _full_weight("w_ffn_up", g_up)

    def merge_fn(u3t, at, gc, ga, xt, w_cb, w_ab, b_cb, w_out, g2_, g3_):
        cp = jnp.dot(u3t, w_cb, preferred_element_type=F32)
        ao = jnp.dot(at, w_ab, preferred_element_type=F32)
        mg = _merge(cp, ao, gc.astype(F32), ga.astype(F32), b_cb).astype(BF16)
        mix_ = jnp.dot(mg, w_out, preferred_element_type=F32)
        x2_ = xt + _rms(mix_, g2_)
        return (mg, cp, ao, mix_, x2_, _rms(x2_, g3_)), ()

    merged, conv_pre, att_out, mix, x2, h2 = _rowwise(
        "branch_merge_mix", merge_fn, [u3, att, g_conv, g_att, x],
        [w["w_conv_branch"], w["w_att_branch"], w["b_conv_branch"], w["w_out"], g2, g3],
        [_sds((s, D_MODEL), BF16)] * 3 + [_sds((s, D_MODEL)), _sds((s, D_MODEL)), _sds((s, D_MODEL), BF16)], tm=512)

    def ffn_up_fn(ht, w_up_t):
        gu_ = lax.dot_general(ht, w_up_t, NT, preferred_element_type=F32)
        return (gu_, _swiglu(gu_[:, :D_FF], gu_[:, D_FF:])), ()

    gu, act, g_right = _rowwise("ffn_up", ffn_up_fn, [h2], [w["w_ffn_up"]],
                                [_sds((s, 2 * D_FF), BF16), _sds((s, D_FF), BF16)], tm=512,
                                exchange=_relay_gather_exchange([down_block[:, half:]]))
    w_down = [_full_weight("w_ffn_down", g) for g in (g_left, g_right)]

    def final_fn(at, x2t, tgt, w_left, w_right, g4_):
        ff = jnp.concatenate([jnp.dot(at, w_left, preferred_element_type=F32),
                              jnp.dot(at, w_right, preferred_element_type=F32)], axis=1)
        n4, vjp = jax.vjp(_rms, ff, g4_)
        err = x2t + n4 - tgt
        dy = err * (1.0 / D_MODEL)
        dff, dg4 = vjp(dy)
        return (dy, dff), (jnp.sum(err * err, axis=0, keepdims=True), dg4)

    dy, dff, loss_cols, d_g4 = _rowwise("ffn_down_loss", final_fn, [act, x2, target], [*w_down, g4],
                                        [_sds((s, D_MODEL)), _sds((s, D_MODEL), BF16)],
                                        [_sds((1, D_MODEL)), _sds((1, D_MODEL))], tm=512)
    loss = 0.5 * jnp.sum(loss_cols) / D_MODEL

    d_w_down = _tn_matmul(act, dff, name="d_w_down")

    def act_bwd_fn(dfft, gut, w_left, w_right):
        d_act = (lax.dot_general(dfft[:, :half], w_left, NT, preferred_element_type=F32)
                 + lax.dot_general(dfft[:, half:], w_right, NT, preferred_element_type=F32))
        gu_ = gut.astype(F32)
        _, vjp = jax.vjp(_swiglu, gu_[:, :D_FF], gu_[:, D_FF:])
        return (jnp.concatenate(vjp(d_act), axis=1),), ()

    down_slabs = _side_slabs("w_ffn_down", d_w_down)
    dgu, theirs = _rowwise("ffn_act_bwd", act_bwd_fn, [dff, gu], w_down, [_sds((s, 2 * D_FF), BF16)],
                           exchange=_pair_exchange([down_slabs]))
    down_sums = _pair_sum("pair_sum_w_ffn_down", down_slabs, theirs)
    d_w_up = _tn_matmul(dgu, h2, name="d_w_up")
    received = {}
    up_slabs = _side_slabs("w_ffn_up", d_w_up)

    def mid_bwd_fn(dgut, xt, mt, dyt, w_up_t, g2_, g3_):
        dh = jnp.dot(dgut, w_up_t, preferred_element_type=F32)
        n2, vjp2 = jax.vjp(_rms, mt, g2_)
        x2_ = xt + n2
        _, vjp3 = jax.vjp(_rms, x2_, g3_)
        dx2_, dg3 = vjp3(dh)
        dx2_ = dx2_ + dyt
        dmix_, dg2 = vjp2(dx2_)
        return (dx2_, dmix_), (dg2, dg3)

    dx2, dmix, d_g2, d_g3, received["w_ffn_down"] = _rowwise(
        "ffn_up_mid_bwd", mid_bwd_fn, [dgu, x, mix, dy], [w["w_ffn_up"], g2, g3],
        [_sds((s, D_MODEL)), _sds((s, D_MODEL), BF16)], [_sds((1, D_MODEL)), _sds((1, D_MODEL))], tm=512,
        exchange=_chip_exchange([down_sums]))
    d_w_out = _tn_matmul(merged, dmix, name="d_w_out")

    def merge_bwd_fn(dmt, cp, ao, gc, ga, w_out, w_cb, w_ab, b_cb):
        dm = lax.dot_general(dmt, w_out, NT, preferred_element_type=F32)
        _, vjp = jax.vjp(_merge, cp.astype(F32), ao.astype(F32), gc.astype(F32), ga.astype(F32), b_cb)
        dcp, dao, dgc, dga, dbias = vjp(dm)
        dcp, dao = dcp.astype(BF16), dao.astype(BF16)
        du3_ = lax.dot_general(dcp, w_cb, NT, preferred_element_type=F32)
        datt_ = lax.dot_general(dao, w_ab, NT, preferred_element_type=F32)
        return (dcp, dao, dgc, dga, du3_, datt_), (dbias,)

    d_conv_out, d_att_out, d_g_conv, d_g_att, du3, d_att, d_b_cb, theirs = _rowwise(
        "merge_bwd", merge_bwd_fn, [dmix, conv_pre, att_out, g_conv, g_att],
        [w["w_out"], w["w_conv_branch"], w["w_att_branch"], w["b_conv_branch"]],
        [_sds((s, D_MODEL), BF16)] * 4 + [_sds((s, CONV_DIM)), _sds((s, ATT_DIM), BF16)], [_sds((1, D_MODEL))], tm=512,
        exchange=_pair_exchange([up_slabs]))

    d_w_cb = _tn_matmul(u3, d_conv_out, name="d_w_conv_branch")
    d_w_ab = _tn_matmul(att, d_att_out, name="d_w_att_branch")

    dq, dk, dv, (received["w_ffn_up"],) = _attn_bwd(
        q, k, v, d_att, ltot, n_blocks, _chip_exchange([_pair_sum("pair_sum_w_ffn_up", up_slabs, theirs)]))

    mix_grads = {"w_conv_branch": d_w_cb, "w_att_branch": d_w_ab, "w_out": d_w_out}
    (d_conv_in, d_dw_w, d_dw_b, d_ln_g, d_ln_b), landed = _conv_bwd(
        conv_in, u1, du3, w["conv_ln_g"], w["conv_ln_b"], w["conv_dw_w"],
        _scatter_exchange([_grad_slabs(nm, mix_grads[nm]) for nm in mix_weights[:2]]))
    received.update(zip(mix_weights[:2], landed))

    d_proj = [d_conv_in, dq, dk, dv, d_g_conv, d_g_att]
    d_w_in, (received["w_out"],) = _pieces_tn_matmul(
        d_proj, h1, name="d_w_in", exchange=_scatter_exchange([_grad_slabs("w_out", d_w_out)]))
    in_slabs = _side_slabs("w_in", d_w_in)
    (theirs,) = _exchange_call("pair_swap_w_in", _pair_exchange([in_slabs]))

    early = list(opt)

    def pre_bwd_fn(*args):
        groups, (xt, dx2t), jobs, (w_in_t, g_) = args[:6], args[6:8], args[8:-2], args[-2:]
        dh = sum(jnp.dot(grp.astype(BF16), w_in_t[IN_SPLITS[n]:IN_SPLITS[n + 1]], preferred_element_type=F32)
                 for n, grp in enumerate(groups))
        _, vjp = jax.vjp(_rms, xt, g_)
        dx_, dg_ = vjp(dh)
        updates = [_sum_adamw_tile(*jobs[4 * n:4 * n + 4]) for n in range(len(early))]
        return (dx_ + dx2t, *[u for four in updates for u in four]), (dg_,)

    res = _rowwise(
        "proj_norm_bwd", pre_bwd_fn,
        d_proj + [x, dx2] + [a for nm in early for a in (received[nm], *opt[nm])], [w_in, g1],
        [_sds((s, D_MODEL))] + [_sds(opt[nm][0].shape) for nm in early for _ in range(4)],
        [_sds((1, D_MODEL))], tm=512, exchange=_chip_exchange([_pair_sum("pair_sum_w_in", in_slabs, theirs)]))
    grad_x, d_g1, received["w_in"] = res[0], res[-2], res[-1]
    updated = {nm: res[1 + 4 * n:5 + 4 * n] for n, nm in enumerate(early)}

    grads = {
        "norm_mix_pre": d_g1, "conv_dw_w": d_dw_w, "conv_dw_b": d_dw_b,
        "conv_ln_g": d_ln_g, "conv_ln_b": d_ln_b, "b_conv_branch": d_b_cb,
        "norm_mix_post": d_g2, "norm_ffn_pre": d_g3, "norm_ffn_post": d_g4,
    }
    return loss, grad_x, received, updated, grads


def _place():
    x, y, c = lax.axis_index("x"), lax.axis_index("y"), lax.axis_index("c")
    return x, y, c


def _slot(px, py, pc):
    return 4 * px + 2 * py + pc


def _exchange_scratch(n):
    return [pltpu.SemaphoreType.DMA((7 * n,)), pltpu.SemaphoreType.DMA((7 * n,)), pltpu.SemaphoreType.DMA((n,))]


GATHER_ID, SCATTER_ID, PAIR_ID, CHIP_ID, RELAY_ID = 0, 1, 2, 3, 4


def _handshake(peers):
    def announce():
        for peer in peers:
            pl.semaphore_signal(pltpu.get_barrier_semaphore(), inc=1, device_id=peer, device_id_type=MESH)

    def arrive():
        pl.semaphore_wait(pltpu.get_barrier_semaphore(), len(peers))

    return announce, arrive


def _gather_exchange(arrs):
    n = len(arrs)

    def phases(ins, outs, send_sems, recv_sems, local_sems):
        x, y, c = _place()
        me, sibling = (x, y, c), (x, y, 1 - c)
        chips = [(1 - x, y), (x, 1 - y), (1 - x, 1 - y)]

        def copy(a, kk, block, to, src=None):
            dst = outs[a].at[_slot(*block)]
            return pltpu.make_async_remote_copy(
                src_ref=dst if src is None else src, dst_ref=dst,
                send_sem=send_sems.at[a * 7 + kk], recv_sem=recv_sems.at[a * 7 + kk],
                device_id=to, device_id_type=MESH)

        mine = [pltpu.make_async_copy(ins[a], outs[a].at[_slot(*me)], local_sems.at[a]) for a in range(n)]
        first = []
        for a in range(n):
            first.append(copy(a, 0, me, sibling, src=ins[a]))
            first += [copy(a, 1 + j, me, (*chip, c), src=ins[a]) for j, chip in enumerate(chips)]
        passed = [copy(a, 4 + j, (*chip, c), sibling) for j, chip in enumerate(chips) for a in range(n)]

        announce, arrive = _handshake([sibling] + [(*chip, c) for chip in chips])

        def send():
            arrive()
            for cp in mine + first:
                cp.start()

        def pass_on():
            for j, chip in enumerate(chips):
                for a in range(n):
                    copy(a, 1 + j, (*chip, c), me).wait_recv()
                    passed[j * n + a].start()

        def finish():
            for a in range(n):
                copy(a, 0, sibling, me).wait_recv()
                for j, chip in enumerate(chips):
                    copy(a, 4 + j, (*chip, 1 - c), me).wait_recv()
            for cp in first + passed:
                cp.wait_send()
            for cp in mine:
                cp.wait()

        return [announce, send, pass_on, finish]

    return list(arrs), [_sds((N_DEV,) + a.shape, a.dtype) for a in arrs], _exchange_scratch(n), phases, GATHER_ID


def _relay_gather_exchange(arrs):
    n = len(arrs)
    per = 8

    def phases(ins, outs, send_sems, recv_sems, local_sems):
        x, y, c = _place()
        me, sibling = (x, y, c), (x, y, 1 - c)
        beside, below, across = (1 - x, y, c), (x, 1 - y, c), (1 - x, 1 - y, c)

        def copy(a, kk, block, to, src=None, rows=None):
            where = _slot(*block) if rows is None else (_slot(*block), rows)
            dst = outs[a].at[where]
            return pltpu.make_async_remote_copy(
                src_ref=dst if src is None else src, dst_ref=dst,
                send_sem=send_sems.at[a * per + kk], recv_sem=recv_sems.at[a * per + kk],
                device_id=to, device_id_type=MESH)

        def halves(a):
            h = ins[a].shape[0] // 2
            return pl.ds(0, h), pl.ds(h, ins[a].shape[0] - h)

        mine = [pltpu.make_async_copy(ins[a], outs[a].at[_slot(*me)], local_sems.at[a]) for a in range(n)]
        first = [copy(a, kk, me, to, src=ins[a]) for a in range(n) for kk, to in enumerate([sibling, beside, below])]
        relayed = [[copy(a, 3, beside, sibling), copy(a, 5, beside, below, rows=halves(a)[0])] for a in range(n)]
        relayed += [[copy(a, 4, below, sibling), copy(a, 6, below, beside, rows=halves(a)[1])] for a in range(n)]
        passed = [copy(a, 7, across, sibling) for a in range(n)]

        announce, arrive = _handshake([sibling, beside, below])

        def send():
            arrive()
            for cp in mine + first:
                cp.start()

        def relay():
            for kk, block in ((1, beside), (2, below)):
                for a in range(n):
                    copy(a, kk, block, me).wait_recv()
                    for cp in relayed[(kk - 1) * n + a]:
                        cp.start()

        def pass_on():
            for a in range(n):
                copy(a, 5, across, me, rows=halves(a)[0]).wait_recv()
                copy(a, 6, across, me, rows=halves(a)[1]).wait_recv()
                passed[a].start()

        def finish():
            for a in range(n):
                for kk, block in ((0, me), (3, beside), (4, below), (7, across)):
                    copy(a, kk, (*block[:2], 1 - c), me).wait_recv()
            for cp in first + [cp for two in relayed for cp in two] + passed:
                cp.wait_send()
            for cp in mine:
                cp.wait()

        return [announce, send, relay, pass_on, finish]

    scratch = [pltpu.SemaphoreType.DMA((per * n,)), pltpu.SemaphoreType.DMA((per * n,)), pltpu.SemaphoreType.DMA((n,))]
    return list(arrs), [_sds((N_DEV,) + a.shape, a.dtype) for a in arrs], scratch, phases, RELAY_ID


def _scatter_exchange(arrs):
    n = len(arrs)
    flips = [(fx, fy, fc) for fx in (0, 1) for fy in (0, 1) for fc in (0, 1)][1:]

    def phases(ins, outs, send_sems, recv_sems, local_sems):
        x, y, c = _place()
        mine = _slot(x, y, c)
        local = [pltpu.make_async_copy(ins[a].at[mine], outs[a].at[mine], local_sems.at[a]) for a in range(n)]
        peers = [((1 - x) if fx else x, (1 - y) if fy else y, (1 - c) if fc else c) for fx, fy, fc in flips]

        def copy(a, kk, src_slot, dst_slot):
            return pltpu.make_async_remote_copy(
                src_ref=ins[a].at[src_slot], dst_ref=outs[a].at[dst_slot],
                send_sem=send_sems.at[a * 7 + kk], recv_sem=recv_sems.at[a * 7 + kk],
                device_id=peers[kk], device_id_type=MESH)

        sends = [copy(a, kk, _slot(*peers[kk]), mine) for a in range(n) for kk in range(7)]

        announce, arrive = _handshake(peers)

        def send():
            arrive()
            for cp in local + sends:
                cp.start()

        def finish():
            for a in range(n):
                for kk in range(7):
                    copy(a, kk, mine, _slot(*peers[kk])).wait_recv()
            for cp in sends:
                cp.wait_send()
            for cp in local:
                cp.wait()

        return [announce, send, finish]

    return list(arrs), [_sds(a.shape, a.dtype) for a in arrs], _exchange_scratch(n), phases, SCATTER_ID


def _pair_exchange(arrs):
    n = len(arrs)

    def phases(ins, outs, send_sems, recv_sems, local_sems):
        x, y, c = _place()

        def copy(a, chip, side):
            return pltpu.make_async_remote_copy(
                src_ref=ins[a].at[chip, side], dst_ref=outs[a].at[chip],
                send_sem=send_sems.at[a * 7 + chip], recv_sem=recv_sems.at[a * 7 + chip],
                device_id=(x, y, 1 - c), device_id_type=MESH)

        sends = [copy(a, chip, 1 - c) for a in range(n) for chip in range(4)]

        announce, arrive = _handshake([(x, y, 1 - c)])

        def send():
            arrive()
            for cp in sends:
                cp.start()

        def finish():
            for a in range(n):
                for chip in range(4):
                    copy(a, chip, c).wait_recv()
            for cp in sends:
                cp.wait_send()

        return [announce, send, finish]

    return list(arrs), [_sds((4,) + a.shape[2:], a.dtype) for a in arrs], _exchange_scratch(n), phases, PAIR_ID


def _chip_exchange(arrs):
    n = len(arrs)

    def phases(ins, outs, send_sems, recv_sems, local_sems):
        x, y, c = _place()
        mine = 2 * x + y
        chips = [(1 - x, y), (x, 1 - y), (1 - x, 1 - y)]
        local = [pltpu.make_async_copy(ins[a].at[mine], outs[a].at[mine], local_sems.at[a]) for a in range(n)]

        def copy(a, j, src_slot, dst_slot):
            return pltpu.make_async_remote_copy(
                src_ref=ins[a].at[src_slot], dst_ref=outs[a].at[dst_slot],
                send_sem=send_sems.at[a * 7 + j], recv_sem=recv_sems.at[a * 7 + j],
                device_id=(*chips[j], c), device_id_type=MESH)

        sends = [copy(a, j, 2 * chips[j][0] + chips[j][1], mine) for a in range(n) for j in range(3)]

        announce, arrive = _handshake([(*chip, c) for chip in chips])

        def send():
            announce()
            arrive()
            for cp in local + sends:
                cp.start()

        def finish():
            for a in range(n):
                for j in range(3):
                    copy(a, j, mine, 2 * chips[j][0] + chips[j][1]).wait_recv()
            for cp in sends:
                cp.wait_send()
            for cp in local:
                cp.wait()

        return [send, lambda: None, finish]

    return list(arrs), [_sds(a.shape, a.dtype) for a in arrs], _exchange_scratch(n), phases, CHIP_ID


def _pair_sum(name, mine, theirs):
    _, _, r, c = mine.shape

    def body(side_ref, m_ref, t_ref, o_ref):
        o_ref[...] = (m_ref[...].astype(F32) + t_ref[...].astype(F32)).astype(o_ref.dtype)

    return pl.pallas_call(
        body, name=name,
        grid_spec=pltpu.PrefetchScalarGridSpec(
            num_scalar_prefetch=1, grid=(4,),
            in_specs=[pl.BlockSpec((None, None, r, c), lambda j, side: (j, side[0], 0, 0)),
                      pl.BlockSpec((None, r, c), lambda j, side: (j, 0, 0))],
            out_specs=pl.BlockSpec((None, r, c), lambda j, side: (j, 0, 0))),
        out_shape=_sds(theirs.shape, theirs.dtype),
        compiler_params=_params(("parallel",)),
    )(lax.axis_index("c").astype(jnp.int32).reshape(1), mine, theirs)


def _exchange_call(name, exchange):
    arrs, out_shape, scratch, phases, collective_id = exchange
    n = len(arrs)

    def body(*refs):
        for step in phases(refs[:n], refs[n:2 * n], *refs[2 * n:]):
            step()

    return pl.pallas_call(body, name=name, in_specs=[ANY] * n, out_specs=[ANY] * n,
                          out_shape=out_shape, scratch_shapes=scratch,
                          compiler_params=pltpu.CompilerParams(collective_id=collective_id))(*arrs)


def _carry_exchange(exchange, refs, n_in, n_out, first, middle, last, halfway):
    arrs, _, _, phases, _ = exchange
    n = len(arrs)
    if n == 0:
        return lambda: None
    ins = refs[n_in:n_in + n]
    outs = refs[n_in + n + n_out:n_in + 2 * n + n_out]
    sems = n_in + 2 * n + n_out
    steps = phases(ins, outs, *refs[sems:sems + 3])
    pl.when(first)(steps[0])

    def close():
        pl.when(first)(steps[1])
        if len(steps) == 5:
            pl.when(halfway)(steps[2])
        if len(steps) >= 4:
            pl.when(middle)(steps[-2])
        pl.when(last)(steps[-1])

    return close


def _adamw_math(w, g, m, v):
    m2 = ADAM_B1 * m + (1.0 - ADAM_B1) * g
    v2 = ADAM_B2 * v + (1.0 - ADAM_B2) * jnp.square(g)
    m_hat = m2 / (1.0 - ADAM_B1 ** ADAM_STEP)
    v_hat = v2 / (1.0 - ADAM_B2 ** ADAM_STEP)
    delta = -ADAM_LR * (m_hat / (jnp.sqrt(v_hat) + ADAM_EPS) + ADAM_WD * w)
    return delta, m2, v2


def _sum_adamw_tile(parts, w, m, v):
    g = parts[0].astype(F32)
    for d in range(1, parts.shape[0]):
        g = g + parts[d].astype(F32)
    return (g, *_adamw_math(w, g, m, v))


def _sum_adamw(name, parts, w, m, v, tr=256):
    p, r, c = parts.shape
    tr = _pick(r, tr, 16)

    def body(p_ref, w_ref, m_ref, v_ref, g_ref, d_ref, m2_ref, v2_ref):
        g_ref[...], d_ref[...], m2_ref[...], v2_ref[...] = _sum_adamw_tile(p_ref[...], w_ref[...], m_ref[...], v_ref[...])

    tile = pl.BlockSpec((tr, c), lambda i: (i, 0))
    return pl.pallas_call(
        body, name=name, grid=(r // tr,),
        in_specs=[pl.BlockSpec((p, tr, c), lambda i: (0, i, 0)), tile, tile, tile],
        out_specs=[tile] * 4, out_shape=[_sds((r, c))] * 4,
        compiler_params=_params(("parallel",)),
    )(parts, w, m, v)


def _sum_parts(name, parts):
    p, r, c = parts.shape

    def body(p_ref, o_ref):
        g = p_ref[0]
        for d in range(1, p):
            g = g + p_ref[d]
        o_ref[...] = g

    return pl.pallas_call(
        body, name=name, out_shape=_sds((r, c)),
        in_specs=[pl.BlockSpec(memory_space=pltpu.VMEM)], out_specs=pl.BlockSpec(memory_space=pltpu.VMEM),
    )(parts)


WEIGHTS = ["norm_mix_pre", "w_in", "conv_dw_w", "conv_dw_b", "conv_ln_g", "conv_ln_b", "w_conv_branch",
           "b_conv_branch", "w_att_branch", "w_out", "norm_mix_post", "norm_ffn_pre", "w_ffn_up", "w_ffn_down",
           "norm_ffn_post"]
COL_SHARDED = ["w_conv_branch", "w_att_branch"]
TRANSPOSED = ["w_in", "w_ffn_up"]
VECTORS = ["norm_mix_pre", "conv_dw_b", "conv_ln_g", "conv_ln_b", "b_conv_branch", "norm_mix_post",
           "norm_ffn_pre", "norm_ffn_post"]


def _cols_to_full(g):
    return g.transpose(1, 0, 2).reshape(g.shape[1], N_DEV * g.shape[2])


def _full_to_cols(f):
    return f.reshape(f.shape[0], N_DEV, f.shape[1] // N_DEV).transpose(1, 0, 2)


PACK_ROWS = 7


def _pack_vectors(vecs, extra=None):
    parts = [vecs[nm].reshape(-1) for nm in VECTORS]
    parts.append(jnp.zeros((1,), F32) if extra is None else extra.reshape(1))
    used = sum(p.size for p in parts)
    parts.append(jnp.zeros((PACK_ROWS * D_MODEL - used,), F32))
    return jnp.concatenate(parts).reshape(PACK_ROWS, D_MODEL)


def _unpack_vectors(packed, sizes):
    flat, out, at = packed.reshape(-1), {}, 0
    for nm in VECTORS:
        out[nm] = flat[at:at + sizes[nm]]
        at += sizes[nm]
    return out, flat[at]


def kernel(x, norm_mix_pre, w_in, conv_dw_w, conv_dw_b, conv_ln_g, conv_ln_b, w_conv_branch, b_conv_branch, w_att_branch, w_out, norm_mix_post, norm_ffn_pre, w_ffn_up, w_ffn_down, norm_ffn_post, loss_target, m_norm_mix_pre, m_w_in, m_conv_dw_w, m_conv_dw_b, m_conv_ln_g, m_conv_ln_b, m_w_conv_branch, m_b_conv_branch, m_w_att_branch, m_w_out, m_norm_mix_post, m_norm_ffn_pre, m_w_ffn_up, m_w_ffn_down, m_norm_ffn_post, v_norm_mix_pre, v_w_in, v_conv_dw_w, v_conv_dw_b, v_conv_ln_g, v_conv_ln_b, v_w_conv_branch, v_b_conv_branch, v_w_att_branch, v_w_out, v_norm_mix_post, v_norm_ffn_pre, v_w_ffn_up, v_w_ffn_down, v_norm_ffn_post):
    ws = dict(zip(WEIGHTS, [norm_mix_pre, w_in, conv_dw_w, conv_dw_b, conv_ln_g, conv_ln_b, w_conv_branch,
                            b_conv_branch, w_att_branch, w_out, norm_mix_post, norm_ffn_pre, w_ffn_up, w_ffn_down,
                            norm_ffn_post]))
    ms = dict(zip(WEIGHTS, [m_norm_mix_pre, m_w_in, m_conv_dw_w, m_conv_dw_b, m_conv_ln_g, m_conv_ln_b,
                            m_w_conv_branch, m_b_conv_branch, m_w_att_branch, m_w_out, m_norm_mix_post,
                            m_norm_ffn_pre, m_w_ffn_up, m_w_ffn_down, m_norm_ffn_post]))
    vs = dict(zip(WEIGHTS, [v_norm_mix_pre, v_w_in, v_conv_dw_w, v_conv_dw_b, v_conv_ln_g, v_conv_ln_b,
                            v_w_conv_branch, v_b_conv_branch, v_w_att_branch, v_w_out, v_norm_mix_post,
                            v_norm_ffn_pre, v_w_ffn_up, v_w_ffn_down, v_norm_ffn_post]))

    dw_block = jnp.pad(conv_dw_w, ((0, 1), (0, 0)))
    g_in, g_dw = _exchange_call("gather_first", _relay_gather_exchange([w_in.T.astype(BF16), dw_block]))
    full = {"w_in": _full_weight("w_in", g_in), "conv_dw_w": _cols_to_full(g_dw)}
    for nm in VECTORS:
        full[nm] = ws[nm].reshape(1, -1)

    def as_kept(nm, a):
        return a.T if nm in TRANSPOSED else a

    ride_along = ["w_ffn_up", "w_out"]
    loss_local, grad_x, received, updated, grads = _local_step(
        x[0], loss_target[0], full, {nm: as_kept(nm, ws[nm]).astype(BF16) for nm in LATE},
        {nm: tuple(as_kept(nm, a[nm]) for a in (ws, ms, vs)) for nm in ride_along})

    small = _exchange_call("gather_small_grads", _gather_exchange(
        [_pack_vectors(grads, extra=loss_local), grads["conv_dw_w"]]))
    out_g, out_d, out_m, out_v = {}, {}, {}, {}
    for nm in LATE + ["w_in"]:
        res = updated[nm] if nm in updated else _sum_adamw(
            "adamw_" + nm, received[nm], *[as_kept(nm, a[nm]) for a in (ws, ms, vs)])
        out_g[nm], out_d[nm], out_m[nm], out_v[nm] = [as_kept(nm, r) for r in res]
    sizes = {nm: ws[nm].size for nm in VECTORS}
    vec = _sum_adamw("adamw_vectors", small[0], _pack_vectors(ws), _pack_vectors(ms), _pack_vectors(vs))
    for res, dst in zip(vec, (out_g, out_d, out_m, out_v)):
        dst.update(_unpack_vectors(res, sizes)[0])
    loss = _unpack_vectors(vec[0], sizes)[1]
    dw_full = _sum_parts("sum_dw_grads", small[1])
    me = _slot(*_place())
    dw_mine = lax.dynamic_slice(dw_full, (0, me * (CONV_DIM // N_DEV)), (CONV_WIDTH, CONV_DIM // N_DEV))
    nm = "conv_dw_w"
    out_g[nm], out_d[nm], out_m[nm], out_v[nm] = _sum_adamw("adamw_dw", dw_mine[None], ws[nm], ms[nm], vs[nm])

    outs = [loss, grad_x[None]]
    for group in (out_g, out_d, out_m, out_v):
        outs += [group[nm] for nm in WEIGHTS]
    return tuple(outs)
```

```python
import math

import jax
import jax.numpy as jnp
from jax import lax
from jax.experimental import pallas as pl
from jax.experimental.pallas import tpu as pltpu

F32 = jnp.float32
BF16 = jnp.bfloat16

N_DEV = 8
D_MODEL = 1024
CONV_DIM = 512
CONV_WIDTH = 31
N_HEADS = 8
HEAD_DIM = 64
ATT_DIM = N_HEADS * HEAD_DIM
D_FF = 2816
EPS = 1e-6
IN_SPLITS = (0, 1024, 1536, 2048, 2560, 3584, 4608)

ADAM_LR = 0.001
ADAM_B1 = 0.9
ADAM_B2 = 0.999
ADAM_EPS = 1e-08
ADAM_WD = 0.01
ADAM_STEP = 10

LANES = 128
SUBLANES = 8
HALO = 32
ATT_TILE = 256
ATT_PART = 176
DEAD_SUM = -120.0
VMEM_LIMIT = 56 * 1024 * 1024
MESH = pl.DeviceIdType.MESH
ANY = pl.BlockSpec(memory_space=pl.ANY)


def _pick(dim, target, align=LANES):
    t = min(dim, target)
    t -= t % align
    while t >= align:
        if dim % t == 0:
            return t
        t -= align
    return dim


def _params(semantics, collective_id=None):
    return pltpu.CompilerParams(dimension_semantics=semantics, vmem_limit_bytes=VMEM_LIMIT,
                                collective_id=collective_id)


def _tn_matmul(a, b, *, name):
    return _pieces_tn_matmul([a], b, name=name, tj=_pick(a.shape[1], 1408))


def _pieces_tn_matmul(pieces, b, *, name, tj=512, exchange=None):
    s, n = b.shape
    counts = [p.shape[1] // tj for p in pieces]
    starts = [sum(counts[:i]) for i in range(len(pieces))]
    assert all(p.shape == (s, c * tj) for p, c in zip(pieces, counts))
    x_arrs, x_shape, x_scratch, _, x_id = exchange or NO_EXCHANGE
    nx, n_in = len(x_arrs), len(pieces) + 1

    def body(*refs):
        b_ref, o_ref = refs[n_in - 1], refs[n_in + nx]
        finish_exchange = _carry_exchange(exchange or NO_EXCHANGE, refs, n_in, 1, *_sweep_marks(sum(counts)))
        j = pl.program_id(0)
        for p_ref, first, count in zip(refs, starts, counts):
            @pl.when((j >= first) & (j < first + count))
            def _():
                o_ref[...] = lax.dot_general(p_ref[...].astype(BF16), b_ref[...], TN,
                                             preferred_element_type=F32).astype(o_ref.dtype)
        finish_exchange()

    def piece_spec(first, count):
        return pl.BlockSpec((s, tj), lambda j: (0, jnp.clip(j - first, 0, count - 1)))

    res = pl.pallas_call(
        body, name=name, grid=(sum(counts),),
        in_specs=[piece_spec(f, c) for f, c in zip(starts, counts)]
        + [pl.BlockSpec((s, n), lambda j: (0, 0), pipeline_mode=pl.Buffered(1))] + [ANY] * nx,
        out_specs=[pl.BlockSpec((tj, n), lambda j: (j, 0))] + [ANY] * nx,
        out_shape=[jax.ShapeDtypeStruct((sum(counts) * tj, n), BF16)] + x_shape, scratch_shapes=x_scratch,
        compiler_params=_params(("arbitrary",), x_id),
    )(*pieces, b, *x_arrs)
    return res[0] if exchange is None else (res[0], res[1:])


NO_EXCHANGE = ([], [], [], None, None)


def _sweep_marks(nt):
    i = pl.program_id(0)
    return i == 0, i == max(nt - 2, 0), i == nt - 1, i == max(nt // 2 - 1, 0)


def _rowwise(name, fn, rows, bcasts, row_outs, red_outs=(), tm=256, exchange=NO_EXCHANGE):
    s = rows[0].shape[0]
    tm = _pick(s, tm, 16)
    nt = s // tm
    resident = pl.Buffered(1)
    nr, nb, no, nd = len(rows), len(bcasts), len(row_outs), len(red_outs)
    x_arrs, x_shape, x_scratch, _, x_id = exchange
    nx = len(x_arrs)
    first_out = nr + nb + nx

    def body(*refs):
        finish_exchange = _carry_exchange(exchange, refs, nr + nb, no + nd, *_sweep_marks(nt), at_once=True)
        ins = [r[...] for r in refs[:nr + nb]]
        outs, reds = fn(*ins)
        for ref, val in zip(refs[first_out:first_out + no], outs):
            ref[...] = val.astype(ref.dtype)
        i = pl.program_id(0)
        for ref, val in zip(refs[first_out + no:first_out + no + nd], reds):
            @pl.when(i == 0)
            def _():
                ref[...] = val

            @pl.when(i > 0)
            def _():
                ref[...] += val
        finish_exchange()

    def row_spec(a):
        assert a.shape[-2] % nt == 0, (name, a.shape, nt)
        if len(a.shape) == 3:
            return pl.BlockSpec((a.shape[0], a.shape[1] // nt, a.shape[2]), lambda i: (0, i, 0))
        return pl.BlockSpec((a.shape[0] // nt, a.shape[1]), lambda i: (i, 0))

    in_specs = [row_spec(r) for r in rows]
    in_specs += [pl.BlockSpec(b.shape, lambda i: (0, 0), pipeline_mode=resident) for b in bcasts]
    out_specs = [row_spec(o) for o in row_outs]
    out_specs += [pl.BlockSpec(d.shape, lambda i: (0, 0)) for d in red_outs]
    return pl.pallas_call(
        body, name=name, grid=(nt,), in_specs=in_specs + [ANY] * nx, out_specs=out_specs + [ANY] * nx,
        out_shape=list(row_outs) + list(red_outs) + x_shape, scratch_shapes=x_scratch,
        compiler_params=_params(("arbitrary",), x_id),
    )(*rows, *bcasts, *x_arrs)


def _sds(shape, dtype=F32):
    return jax.ShapeDtypeStruct(shape, dtype)


def _rms(x, g):
    y = x * lax.rsqrt(jnp.mean(x * x, axis=-1, keepdims=True) + EPS)
    return y * g


def _silu(x):
    return x * jax.nn.sigmoid(x)


def _swiglu(g, u):
    return _silu(g) * u


def _ln_silu(u, g, b):
    mu = jnp.mean(u, axis=-1, keepdims=True)
    var = jnp.mean(jnp.square(u - mu), axis=-1, keepdims=True)
    return _silu((u - mu) * lax.rsqrt(var + EPS) * g + b)


def _merge(conv_pre, att_out, g_conv, g_att, b_cb):
    return jax.nn.sigmoid(g_conv) * (conv_pre + b_cb) + jax.nn.sigmoid(g_att) * att_out


def _glu(t):
    return t[:, :CONV_DIM] * jax.nn.sigmoid(t[:, CONV_DIM:])


def _shifted_reader(buf, shifted, tm):
    for b in range(1, SUBLANES):
        shifted[b - 1, :, :] = buf[pl.ds(b, tm + HALO - SUBLANES), :]

    def read(o):
        a, b = divmod(o, SUBLANES)
        return buf[pl.ds(SUBLANES * a, tm), :] if b == 0 else shifted[b - 1, pl.ds(SUBLANES * a, tm), :]

    return read


def _conv_fwd(conv_in, w_pad, b, ln_g, ln_b, exchange, tm=256):
    s = conv_in.shape[0]
    tm = _pick(s, tm, HALO)
    ratio = tm // HALO
    x_arrs, x_shape, x_scratch, _, x_id = exchange
    nx = len(x_arrs)

    def body(*refs):
        main_ref, halo_ref, w_ref, b_ref, g_ref, be_ref = refs[:6]
        u3_ref, u1_ref = refs[6 + nx:8 + nx]
        buf, shifted = refs[-2:]
        finish_exchange = _carry_exchange(exchange, refs, 6, 2, *_sweep_marks(s // tm))
        i = pl.program_id(0)
        buf[0:HALO, :] = _glu(halo_ref[...]) * (i > 0).astype(F32)
        buf[HALO:HALO + tm, :] = _glu(main_ref[...])
        read = _shifted_reader(buf, shifted, tm)
        acc = jnp.zeros((tm, CONV_DIM), F32) + b_ref[...]
        for j in range(CONV_WIDTH):
            acc = acc + w_ref[j:j + 1, :] * read(HALO - (CONV_WIDTH - 1) + j)
        u1_ref[...] = acc
        u3_ref[...] = _ln_silu(acc, g_ref[...], be_ref[...]).astype(u3_ref.dtype)
        finish_exchange()

    res = pl.pallas_call(
        body, name="conv_fwd", grid=(s // tm,),
        in_specs=[pl.BlockSpec((tm, 2 * CONV_DIM), lambda i: (i, 0)),
                  pl.BlockSpec((HALO, 2 * CONV_DIM), lambda i: (jnp.maximum(i * ratio - 1, 0), 0)),
                  pl.BlockSpec(w_pad.shape, lambda i: (0, 0)),
                  pl.BlockSpec(b.shape, lambda i: (0, 0)),
                  pl.BlockSpec(ln_g.shape, lambda i: (0, 0)),
                  pl.BlockSpec(ln_b.shape, lambda i: (0, 0))] + [ANY] * nx,
        out_specs=[pl.BlockSpec((tm, CONV_DIM), lambda i: (i, 0)),
                   pl.BlockSpec((tm, CONV_DIM), lambda i: (i, 0))] + [ANY] * nx,
        out_shape=[_sds((s, CONV_DIM), BF16), _sds((s, CONV_DIM), F32)] + x_shape,
        scratch_shapes=x_scratch + [pltpu.VMEM((tm + HALO, CONV_DIM), F32),
                                    pltpu.VMEM((SUBLANES - 1, tm + HALO - SUBLANES, CONV_DIM), F32)],
        compiler_params=_params(("arbitrary",), x_id),
    )(conv_in, conv_in, w_pad, b, ln_g, ln_b, *x_arrs)
    return res[0], res[1], res[2:]


def _conv_bwd(conv_in, u1, du3, ln_g, ln_b, w_pad, exchange, tm=256):
    s = conv_in.shape[0]
    tm = _pick(s, tm, HALO)
    ratio = tm // HALO
    nt = s // tm
    last_halo = s // HALO - 1
    x_arrs, x_shape, x_scratch, _, x_id = exchange
    nx = len(x_arrs)

    def body(*refs):
        main_ref, halo_ref, u1_ref, u1n_ref, du3_ref, du3n_ref, g_ref, be_ref, w_ref = refs[:9]
        dci_ref, dw_ref, db_ref, dg_ref, dbe_ref = refs[9 + nx:14 + nx]
        ubuf, dbuf, ushift, dshift = refs[-4:]
        finish_exchange = _carry_exchange(exchange, refs, 9, 5, *_sweep_marks(nt))
        i = pl.program_id(0)
        main = main_ref[...]
        a = main[:, :CONV_DIM]
        sb = jax.nn.sigmoid(main[:, CONV_DIM:])
        ubuf[0:HALO, :] = _glu(halo_ref[...]) * (i > 0).astype(F32)
        ubuf[HALO:HALO + tm, :] = a * sb

        def ln_bwd(u1t, du3t):
            _, vjp = jax.vjp(_ln_silu, u1t, g_ref[...], be_ref[...])
            return vjp(du3t)

        du, dg, dbe = ln_bwd(u1_ref[...], du3_ref[...])
        dbuf[0:tm, :] = du
        dbuf[tm:tm + HALO, :] = ln_bwd(u1n_ref[...], du3n_ref[...])[0] * (i < nt - 1).astype(F32)

        @pl.when(i == 0)
        def _():
            dw_ref[...] = jnp.zeros_like(dw_ref)
            db_ref[...] = jnp.zeros_like(db_ref)
            dg_ref[...] = jnp.zeros_like(dg_ref)
            dbe_ref[...] = jnp.zeros_like(dbe_ref)

        dg_ref[...] += dg
        dbe_ref[...] += dbe

        read_u = _shifted_reader(ubuf, ushift, tm)
        read_d = _shifted_reader(dbuf, dshift, tm)
        du0 = jnp.zeros((tm, CONV_DIM), F32)
        for j in range(CONV_WIDTH):
            du0 = du0 + w_ref[j:j + 1, :] * read_d(CONV_WIDTH - 1 - j)
            dw_ref[j:j + 1, :] += jnp.sum(du * read_u(HALO - (CONV_WIDTH - 1) + j), axis=0, keepdims=True)
        db_ref[...] += jnp.sum(du, axis=0, keepdims=True)
        dci_ref[:, :CONV_DIM] = (du0 * sb).astype(dci_ref.dtype)
        dci_ref[:, CONV_DIM:] = (du0 * a * sb * (1.0 - sb)).astype(dci_ref.dtype)
        finish_exchange()

    res = pl.pallas_call(
        body, name="conv_bwd", grid=(nt,),
        in_specs=[pl.BlockSpec((tm, 2 * CONV_DIM), lambda i: (i, 0)),
                  pl.BlockSpec((HALO, 2 * CONV_DIM), lambda i: (jnp.maximum(i * ratio - 1, 0), 0))]
        + [pl.BlockSpec((tm, CONV_DIM), lambda i: (i, 0)),
           pl.BlockSpec((HALO, CONV_DIM), lambda i: (jnp.minimum((i + 1) * ratio, last_halo), 0))] * 2
        + [pl.BlockSpec((1, CONV_DIM), lambda i: (0, 0))] * 2 + [pl.BlockSpec(w_pad.shape, lambda i: (0, 0))]
        + [ANY] * nx,
        out_specs=[pl.BlockSpec((tm, 2 * CONV_DIM), lambda i: (i, 0)),
                   pl.BlockSpec(w_pad.shape, lambda i: (0, 0))]
        + [pl.BlockSpec((1, CONV_DIM), lambda i: (0, 0))] * 3 + [ANY] * nx,
        out_shape=[_sds((s, 2 * CONV_DIM), BF16), _sds(w_pad.shape)] + [_sds((1, CONV_DIM))] * 3 + x_shape,
        scratch_shapes=x_scratch + [pltpu.VMEM((tm + HALO, CONV_DIM), F32)] * 2
        + [pltpu.VMEM((SUBLANES - 1, tm + HALO - SUBLANES, CONV_DIM), F32)] * 2,
        compiler_params=_params(("arbitrary",), x_id),
    )(conv_in, conv_in, u1, u1, du3, du3, ln_g, ln_b, w_pad, *x_arrs)
    return res[:5], res[5:]


def _logsig_neg(z):
    return jnp.minimum(-z, 0.0) - jnp.log(1.0 + jnp.exp(-jnp.abs(z)))


def _split_dot(val, tri):
    hi = val.astype(BF16)
    lo = (val - hi.astype(F32)).astype(BF16)
    return jnp.dot(hi, tri, preferred_element_type=F32) + jnp.dot(lo, tri, preferred_element_type=F32)


def _attn_masks(t, later):
    row = lax.broadcasted_iota(jnp.int32, (t, t), 0)
    col = lax.broadcasted_iota(jnp.int32, (t, t), 1)
    tri = jnp.where(row > col if later else row <= col, 1.0, 0.0).astype(BF16)
    return col < row, tri


def _grid_marks(h, nq):
    hh, i = pl.program_id(0), pl.program_id(1)
    return ((hh == 0) & (i == 0), (hh == h - 1) & (i == nq // 2), (hh == h - 1) & (i == nq - 1),
            (hh == h // 2) & (i == nq // 2))


def _head_masks(shape):
    lane = lax.broadcasted_iota(jnp.int32, shape, len(shape) - 1)
    return lane < HEAD_DIM, lane >= HEAD_DIM


def _per_head(blk):
    m0, m1 = _head_masks(blk.shape)
    zero = jnp.zeros_like(blk)
    return jnp.where(m0, blk, zero), jnp.where(m1, blk, zero)


NT = (((1,), (1,)), ((), ()))
TN = (((0,), (0,)), ((), ()))


def _with_top(whole, top):
    rows = top.shape[0]
    return top if rows == whole.shape[0] else jnp.concatenate([top, whole[rows:]], axis=0)


def _attn_fwd(q, k, v, exchange):
    s = q.shape[0]
    hp = q.shape[1] // LANES
    t = ATT_TILE
    scale = 1.0 / math.sqrt(HEAD_DIM)
    x_arrs, x_shape, x_scratch, _, x_id = exchange
    nx = len(x_arrs)

    def body(*refs):
        q_ref, k_ref, v_ref = refs[:3]
        o_ref, lt_ref, nb_ref = refs[3 + nx:6 + nx]
        finish_exchange = _carry_exchange(exchange, refs, 3, 3, *_grid_marks(hp, s // t))
        i = pl.program_id(1)
        qs = _per_head((q_ref[...].astype(F32) * scale).astype(BF16))
        causal, tri = _attn_masks(t, later=True)

        def step(kb, carry, masked, rows):
            cs, acc = carry
            off = pl.multiple_of(kb * t, t)
            kblk = k_ref[pl.ds(off, t), :]
            vs = _per_head(v_ref[pl.ds(off, t), :])
            acc_top = acc[:rows]
            new_cs = []
            for hd in range(2):
                z = lax.dot_general(qs[hd][:rows], kblk, NT, preferred_element_type=F32)
                l = _logsig_neg(z)
                if masked:
                    l = jnp.where(causal, l, 0.0)
                e = z + l + _split_dot(l, tri) + cs[hd][:rows]
                if masked:
                    e = jnp.where(causal, e, -1e30)
                acc_top = acc_top + jnp.dot(jnp.exp(e).astype(BF16), vs[hd], preferred_element_type=F32)
                new_cs.append(_with_top(cs[hd], cs[hd][:rows] + jnp.sum(l, axis=1, keepdims=True)))
            return tuple(new_cs), _with_top(acc, acc_top)

        zero = jnp.zeros((t, 1), F32)
        carry = step(i, ((zero, zero), jnp.zeros((t, LANES), F32)), True, t)

        def live(cs, lo, hi):
            return jnp.maximum(jnp.max(cs[0][lo:hi]), jnp.max(cs[1][lo:hi])) > DEAD_SUM

        def more(state):
            n, _, (cs, _) = state
            return (n < i) & live(cs, 0, t)

        def sweep(state):
            n, n_full, cr = state
            whole = live(cr[0], ATT_PART, t)
            cr = lax.cond(whole, lambda c: step(i - 1 - n, c, False, t), lambda c: step(i - 1 - n, c, False, ATT_PART), cr)
            return n + 1, n_full + whole.astype(jnp.int32), cr

        n_blocks, n_full, carry = lax.while_loop(more, sweep, (jnp.int32(0), jnp.int32(0), carry))
        m0, _ = _head_masks((t, LANES))
        lt_ref[...] = jnp.where(m0, carry[0][0], carry[0][1])
        o_ref[...] = carry[1].astype(o_ref.dtype)
        nb_ref[0, pl.program_id(0), i] = n_blocks.astype(F32)
        nb_ref[1, pl.program_id(0), i] = n_full.astype(F32)
        finish_exchange()

    res = pl.pallas_call(
        body, name="attn_fwd", grid=(hp, s // t),
        in_specs=[pl.BlockSpec((t, LANES), lambda p, i: (i, p)),
                  pl.BlockSpec((s, LANES), lambda p, i: (0, p)),
                  pl.BlockSpec((s, LANES), lambda p, i: (0, p))] + [ANY] * nx,
        out_specs=[pl.BlockSpec((t, LANES), lambda p, i: (i, p)),
                   pl.BlockSpec((None, t, LANES), lambda p, i: (p, i, 0)),
                   pl.BlockSpec(memory_space=pltpu.SMEM)] + [ANY] * nx,
        out_shape=[_sds(q.shape, BF16), _sds((hp, s, LANES), F32), _sds((2, hp, s // t), F32)] + x_shape,
        scratch_shapes=x_scratch,
        compiler_params=_params(("arbitrary", "arbitrary"), x_id),
    )(q, k, v, *x_arrs)
    return res[0], res[1], res[2], res[3:]


def _attn_bwd(q, k, v, do, ltot, n_blocks, exchange):
    s = q.shape[0]
    hp = q.shape[1] // LANES
    t = ATT_TILE
    scale = 1.0 / math.sqrt(HEAD_DIM)
    x_arrs, x_shape, x_scratch, _, x_id = exchange
    nx = len(x_arrs)

    def body(*refs):
        q_ref, k_ref, v_ref, do_ref, lt_ref, nb_ref = refs[:6]
        dq_ref, dk_ref, dv_ref = refs[6 + nx:9 + nx]
        finish_exchange = _carry_exchange(exchange, refs, 6, 3, *_grid_marks(hp, s // t))
        i = pl.program_id(1)
        n_blocks = jnp.clip(nb_ref[0, pl.program_id(0), i].astype(jnp.int32), 0, i)
        n_full = jnp.clip(nb_ref[1, pl.program_id(0), i].astype(jnp.int32), 0, n_blocks)

        @pl.when(i == 0)
        def _():
            dk_ref[...] = jnp.zeros_like(dk_ref)
            dv_ref[...] = jnp.zeros_like(dv_ref)

        qb = q_ref[...]
        qm = _per_head(qb)
        qs = _per_head((qb.astype(F32) * scale).astype(BF16))
        dos = _per_head(do_ref[...])
        lts = (lt_ref[:, 0:1], lt_ref[:, HEAD_DIM:HEAD_DIM + 1])
        causal, tri = _attn_masks(t, later=False)

        def step(kb, carry, masked, rows):
            cls, cgs, dq = carry
            off = pl.multiple_of(kb * t, t)
            kblk = k_ref[pl.ds(off, t), :]
            vblk = v_ref[pl.ds(off, t), :]
            ks = _per_head(kblk)
            dq_top = dq[:rows]
            dk = jnp.zeros((t, LANES), F32)
            dv = jnp.zeros((t, LANES), F32)
            new_cls, new_cgs = [], []
            for hd in range(2):
                z = lax.dot_general(qs[hd][:rows], kblk, NT, preferred_element_type=F32)
                l = _logsig_neg(z)
                if masked:
                    l = jnp.where(causal, l, 0.0)
                e = z + l + ((lts[hd][:rows] - cls[hd][:rows]) - _split_dot(l, tri))
                if masked:
                    e = jnp.where(causal, e, -1e30)
                a = jnp.exp(e)
                g = lax.dot_general(dos[hd][:rows], vblk, NT, preferred_element_type=F32) * a
                p = cgs[hd][:rows] + jnp.dot(g.astype(BF16), tri, preferred_element_type=F32) - g
                el = jnp.exp(l)
                dz = g * el - p * (1.0 - el)
                if masked:
                    dz = jnp.where(causal, dz, 0.0)
                dzb = (dz * scale).astype(BF16)
                dq_top = dq_top + jnp.dot(dzb, ks[hd], preferred_element_type=F32)
                dk = dk + lax.dot_general(dzb, qm[hd][:rows], TN, preferred_element_type=F32)
                dv = dv + lax.dot_general(a.astype(BF16), dos[hd][:rows], TN, preferred_element_type=F32)
                new_cls.append(_with_top(cls[hd], cls[hd][:rows] + jnp.sum(l, axis=1, keepdims=True)))
                new_cgs.append(_with_top(cgs[hd], cgs[hd][:rows] + jnp.sum(g, axis=1, keepdims=True)))
            dk_ref[pl.ds(off, t), :] += dk
            dv_ref[pl.ds(off, t), :] += dv
            return tuple(new_cls), tuple(new_cgs), _with_top(dq, dq_top)

        zero = jnp.zeros((t, 1), F32)
        init = ((zero, zero), (zero, zero), jnp.zeros((t, LANES), F32))
        carry = lax.fori_loop(i - n_blocks, i - n_full, lambda kb, cr: step(kb, cr, False, ATT_PART), init)
        carry = lax.fori_loop(i - n_full, i, lambda kb, cr: step(kb, cr, False, t), carry)
        carry = step(i, carry, True, t)
        dq_ref[...] = carry[2]
        finish_exchange()

    blk = pl.BlockSpec((t, LANES), lambda p, i: (i, p))
    whole = pl.BlockSpec((s, LANES), lambda p, i: (0, p))
    res = pl.pallas_call(
        body, name="attn_bwd", grid=(hp, s // t),
        in_specs=[blk, whole, whole, blk, pl.BlockSpec((None, t, LANES), lambda p, i: (p, i, 0)),
                  pl.BlockSpec(memory_space=pltpu.SMEM)] + [ANY] * nx,
        out_specs=[blk, whole, whole] + [ANY] * nx,
        out_shape=[_sds(q.shape)] * 3 + x_shape,
        scratch_shapes=x_scratch,
        compiler_params=_params(("arbitrary", "arbitrary"), x_id),
    )(q, k, v, do, ltot, n_blocks, *x_arrs)
    return res[0], res[1], res[2], res[3:]


LATE = ["w_conv_branch", "w_att_branch", "w_out", "w_ffn_up", "w_ffn_down"]


def _full_weight(name, gathered):
    return _cols_to_full(gathered) if name in COL_SHARDED else gathered.reshape(-1, gathered.shape[2])


def _grad_slabs(name, grad):
    return _full_to_cols(grad) if name in COL_SHARDED else grad.reshape(N_DEV, -1, grad.shape[1])


def _side_slabs(name, grad):
    slabs = _grad_slabs(name, grad)
    return slabs.reshape((4, 2) + slabs.shape[1:])


def _local_step(x, target, w, late_blocks, opt):
    s = x.shape[0]
    w = dict(w)
    g1, g2, g3, g4 = w["norm_mix_pre"], w["norm_mix_post"], w["norm_ffn_pre"], w["norm_ffn_post"]

    w_in = w["w_in"]

    def proj_fn(xt, g1_, w_in_t):
        h = _rms(xt, g1_).astype(BF16)
        proj = lax.dot_general(h, w_in_t, NT, preferred_element_type=F32)
        return (h, *[proj[:, IN_SPLITS[n]:IN_SPLITS[n + 1]] for n in range(6)]), ()

    mix_weights = ["w_conv_branch", "w_att_branch", "w_out"]
    h1, conv_in, q, k, v, g_conv, g_att, g_out = _rowwise(
        "norm_proj", proj_fn, [x], [g1, w_in],
        [_sds((s, D_MODEL), BF16), _sds((s, 2 * CONV_DIM)), _sds((s, ATT_DIM), BF16), _sds((s, ATT_DIM), BF16),
         _sds((s, ATT_DIM), BF16), _sds((s, D_MODEL), BF16), _sds((s, D_MODEL), BF16)], tm=512,
        exchange=_relay_gather_exchange([late_blocks["w_out"]]))

    u3, u1, g_branches = _conv_fwd(conv_in, w["conv_dw_w"], w["conv_dw_b"], w["conv_ln_g"], w["conv_ln_b"],
                                   _relay_gather_exchange([late_blocks[nm] for nm in mix_weights[:2]]))
    for nm, g in zip(mix_weights, [*g_branches, g_out]):
        w[nm] = _full_weight(nm, g)
    half = D_MODEL // 2
    down_block = late_blocks["w_ffn_down"]
    att, ltot, n_blocks, (g_up, g_left) = _attn_fwd(
        q, k, v, _relay_gather_exchange([late_blocks["w_ffn_up"], down_block[:, :half]]))
    w["w_ffn_up"] = _full_weight("w_ffn_up", g_up)

    def merge_fn(u3t, at, gc, ga, xt, w_cb, w_ab, b_cb, w_out, g2_, g3_):
        cp = jnp.dot(u3t, w_cb, preferred_element_type=F32)
        ao = jnp.dot(at, w_ab, preferred_element_type=F32)
        mg = _merge(cp, ao, gc.astype(F32), ga.astype(F32), b_cb).astype(BF16)
        mix_ = jnp.dot(mg, w_out, preferred_element_type=F32)
        x2_ = xt + _rms(mix_, g2_)
        return (mg, cp, ao, mix_, x2_, _rms(x2_, g3_)), ()

    merged, conv_pre, att_out, mix, x2, h2 = _rowwise(
        "branch_merge_mix", merge_fn, [u3, att, g_conv, g_att, x],
        [w["w_conv_branch"], w["w_att_branch"], w["b_conv_branch"], w["w_out"], g2, g3],
        [_sds((s, D_MODEL), BF16)] * 3 + [_sds((s, D_MODEL)), _sds((s, D_MODEL)), _sds((s, D_MODEL), BF16)], tm=512)

    def ffn_up_fn(ht, w_up_t):
        gu_ = lax.dot_general(ht, w_up_t, NT, preferred_element_type=F32)
        return (gu_, _swiglu(gu_[:, :D_FF], gu_[:, D_FF:])), ()

    gu, act, g_right = _rowwise("ffn_up", ffn_up_fn, [h2], [w["w_ffn_up"]],
                                [_sds((s, 2 * D_FF), BF16), _sds((s, D_FF), BF16)], tm=512,
                                exchange=_relay_gather_exchange([down_block[:, half:]]))
    w_down = [_full_weight("w_ffn_down", g) for g in (g_left, g_right)]

    def final_fn(at, x2t, tgt, w_left, w_right, g4_):
        ff = jnp.concatenate([jnp.dot(at, w_left, preferred_element_type=F32),
                              jnp.dot(at, w_right, preferred_element_type=F32)], axis=1)
        n4, vjp = jax.vjp(_rms, ff, g4_)
        err = x2t + n4 - tgt
        dy = err * (1.0 / D_MODEL)
        dff, dg4 = vjp(dy)
        return (dy, dff), (jnp.sum(err * err, axis=0, keepdims=True), dg4)

    dy, dff, loss_cols, d_g4 = _rowwise("ffn_down_loss", final_fn, [act, x2, target], [*w_down, g4],
                                        [_sds((s, D_MODEL)), _sds((s, D_MODEL), BF16)],
                                        [_sds((1, D_MODEL)), _sds((1, D_MODEL))], tm=512)
    loss = 0.5 * jnp.sum(loss_cols) / D_MODEL

    d_w_down = _tn_matmul(act, dff, name="d_w_down")

    def act_bwd_fn(dfft, gut, w_left, w_right):
        d_act = (lax.dot_general(dfft[:, :half], w_left, NT, preferred_element_type=F32)
                 + lax.dot_general(dfft[:, half:], w_right, NT, preferred_element_type=F32))
        gu_ = gut.astype(F32)
        _, vjp = jax.vjp(_swiglu, gu_[:, :D_FF], gu_[:, D_FF:])
        return (jnp.concatenate(vjp(d_act), axis=1),), ()

    down_slabs = _side_slabs("w_ffn_down", d_w_down)
    dgu, theirs = _rowwise("ffn_act_bwd", act_bwd_fn, [dff, gu], w_down, [_sds((s, 2 * D_FF), BF16)],
                           exchange=_pair_exchange([down_slabs]))
    down_sums = _pair_sum("pair_sum_w_ffn_down", down_slabs, theirs)
    d_w_up = _tn_matmul(dgu, h2, name="d_w_up")
    received = {}
    up_slabs = _side_slabs("w_ffn_up", d_w_up)

    def mid_bwd_fn(dgut, xt, mt, dyt, w_up_t, g2_, g3_):
        dh = jnp.dot(dgut, w_up_t, preferred_element_type=F32)
        n2, vjp2 = jax.vjp(_rms, mt, g2_)
        x2_ = xt + n2
        _, vjp3 = jax.vjp(_rms, x2_, g3_)
        dx2_, dg3 = vjp3(dh)
        dx2_ = dx2_ + dyt
        dmix_, dg2 = vjp2(dx2_)
        return (dx2_, dmix_), (dg2, dg3)

    dx2, dmix, d_g2, d_g3, received["w_ffn_down"] = _rowwise(
        "ffn_up_mid_bwd", mid_bwd_fn, [dgu, x, mix, dy], [w["w_ffn_up"], g2, g3],
        [_sds((s, D_MODEL)), _sds((s, D_MODEL), BF16)], [_sds((1, D_MODEL)), _sds((1, D_MODEL))], tm=512,
        exchange=_chip_exchange([down_sums]))
    d_w_out = _tn_matmul(merged, dmix, name="d_w_out")

    def merge_bwd_fn(dmt, cp, ao, gc, ga, w_out, w_cb, w_ab, b_cb):
        dm = lax.dot_general(dmt, w_out, NT, preferred_element_type=F32)
        _, vjp = jax.vjp(_merge, cp.astype(F32), ao.astype(F32), gc.astype(F32), ga.astype(F32), b_cb)
        dcp, dao, dgc, dga, dbias = vjp(dm)
        dcp, dao = dcp.astype(BF16), dao.astype(BF16)
        du3_ = lax.dot_general(dcp, w_cb, NT, preferred_element_type=F32)
        datt_ = lax.dot_general(dao, w_ab, NT, preferred_element_type=F32)
        return (dcp, dao, dgc, dga, du3_, datt_), (dbias,)

    d_conv_out, d_att_out, d_g_conv, d_g_att, du3, d_att, d_b_cb, theirs = _rowwise(
        "merge_bwd", merge_bwd_fn, [dmix, conv_pre, att_out, g_conv, g_att],
        [w["w_out"], w["w_conv_branch"], w["w_att_branch"], w["b_conv_branch"]],
        [_sds((s, D_MODEL), BF16)] * 4 + [_sds((s, CONV_DIM)), _sds((s, ATT_DIM), BF16)], [_sds((1, D_MODEL))], tm=512,
        exchange=_pair_exchange([up_slabs]))

    d_w_cb = _tn_matmul(u3, d_conv_out, name="d_w_conv_branch")
    d_w_ab = _tn_matmul(att, d_att_out, name="d_w_att_branch")

    dq, dk, dv, (received["w_ffn_up"],) = _attn_bwd(
        q, k, v, d_att, ltot, n_blocks, _chip_exchange([_pair_sum("pair_sum_w_ffn_up", up_slabs, theirs)]))

    mix_grads = {"w_conv_branch": d_w_cb, "w_att_branch": d_w_ab, "w_out": d_w_out}
    (d_conv_in, d_dw_w, d_dw_b, d_ln_g, d_ln_b), landed = _conv_bwd(
        conv_in, u1, du3, w["conv_ln_g"], w["conv_ln_b"], w["conv_dw_w"],
        _scatter_exchange([_grad_slabs(nm, mix_grads[nm]) for nm in mix_weights[:2]]))
    received.update(zip(mix_weights[:2], landed))

    d_proj = [d_conv_in, dq, dk, dv, d_g_conv, d_g_att]
    d_w_in, (received["w_out"],) = _pieces_tn_matmul(
        d_proj, h1, name="d_w_in", exchange=_scatter_exchange([_grad_slabs("w_out", d_w_out)]))
    in_slabs = _side_slabs("w_in", d_w_in)
    (theirs,) = _exchange_call("pair_swap_w_in", _pair_exchange([in_slabs]))

    early = list(opt)

    def pre_bwd_fn(*args):
        groups, (xt, dx2t), jobs, (w_in_t, g_) = args[:6], args[6:8], args[8:-2], args[-2:]
        dh = sum(jnp.dot(grp.astype(BF16), w_in_t[IN_SPLITS[n]:IN_SPLITS[n + 1]], preferred_element_type=F32)
                 for n, grp in enumerate(groups))
        _, vjp = jax.vjp(_rms, xt, g_)
        dx_, dg_ = vjp(dh)
        updates = [_sum_adamw_tile(*jobs[4 * n:4 * n + 4]) for n in range(len(early))]
        return (dx_ + dx2t, *[u for four in updates for u in four]), (dg_,)

    res = _rowwise(
        "proj_norm_bwd", pre_bwd_fn,
        d_proj + [x, dx2] + [a for nm in early for a in (received[nm], *opt[nm])], [w_in, g1],
        [_sds((s, D_MODEL))] + [_sds(opt[nm][0].shape) for nm in early for _ in range(4)],
        [_sds((1, D_MODEL))], tm=512, exchange=_chip_exchange([_pair_sum("pair_sum_w_in", in_slabs, theirs)]))
    grad_x, d_g1, received["w_in"] = res[0], res[-2], res[-1]
    updated = {nm: res[1 + 4 * n:5 + 4 * n] for n, nm in enumerate(early)}

    grads = {
        "norm_mix_pre": d_g1, "conv_dw_w": d_dw_w, "conv_dw_b": d_dw_b,
        "conv_ln_g": d_ln_g, "conv_ln_b": d_ln_b, "b_conv_branch": d_b_cb,
        "norm_mix_post": d_g2, "norm_ffn_pre": d_g3, "norm_ffn_post": d_g4,
    }
    return loss, grad_x, received, updated, grads


def _place():
    x, y, c = lax.axis_index("x"), lax.axis_index("y"), lax.axis_index("c")
    return x, y, c


def _slot(px, py, pc):
    return 4 * px + 2 * py + pc


def _exchange_scratch(n):
    return [pltpu.SemaphoreType.DMA((7 * n,)), pltpu.SemaphoreType.DMA((7 * n,)), pltpu.SemaphoreType.DMA((n,))]


GATHER_ID, SCATTER_ID, PAIR_ID, CHIP_ID, RELAY_ID = 0, 1, 2, 3, 4


def _handshake(peers):
    def announce():
        for peer in peers:
            pl.semaphore_signal(pltpu.get_barrier_semaphore(), inc=1, device_id=peer, device_id_type=MESH)

    def arrive():
        pl.semaphore_wait(pltpu.get_barrier_semaphore(), len(peers))

    return announce, arrive


def _gather_exchange(arrs):
    n = len(arrs)

    def phases(ins, outs, send_sems, recv_sems, local_sems):
        x, y, c = _place()
        me, sibling = (x, y, c), (x, y, 1 - c)
        chips = [(1 - x, y), (x, 1 - y), (1 - x, 1 - y)]

        def copy(a, kk, block, to, src=None):
            dst = outs[a].at[_slot(*block)]
            return pltpu.make_async_remote_copy(
                src_ref=dst if src is None else src, dst_ref=dst,
                send_sem=send_sems.at[a * 7 + kk], recv_sem=recv_sems.at[a * 7 + kk],
                device_id=to, device_id_type=MESH)

        mine = [pltpu.make_async_copy(ins[a], outs[a].at[_slot(*me)], local_sems.at[a]) for a in range(n)]
        first = []
        for a in range(n):
            first.append(copy(a, 0, me, sibling, src=ins[a]))
            first += [copy(a, 1 + j, me, (*chip, c), src=ins[a]) for j, chip in enumerate(chips)]
        passed = [copy(a, 4 + j, (*chip, c), sibling) for j, chip in enumerate(chips) for a in range(n)]

        announce, arrive = _handshake([sibling] + [(*chip, c) for chip in chips])

        def send():
            arrive()
            for cp in mine + first:
                cp.start()

        def pass_on():
            for j, chip in enumerate(chips):
                for a in range(n):
                    copy(a, 1 + j, (*chip, c), me).wait_recv()
                    passed[j * n + a].start()

        def finish():
            for a in range(n):
                copy(a, 0, sibling, me).wait_recv()
                for j, chip in enumerate(chips):
                    copy(a, 4 + j, (*chip, 1 - c), me).wait_recv()
            for cp in first + passed:
                cp.wait_send()
            for cp in mine:
                cp.wait()

        return [announce, send, pass_on, finish]

    return list(arrs), [_sds((N_DEV,) + a.shape, a.dtype) for a in arrs], _exchange_scratch(n), phases, GATHER_ID


def _relay_gather_exchange(arrs):
    n = len(arrs)
    per = 8

    def phases(ins, outs, send_sems, recv_sems, local_sems):
        x, y, c = _place()
        me, sibling = (x, y, c), (x, y, 1 - c)
        beside, below, across = (1 - x, y, c), (x, 1 - y, c), (1 - x, 1 - y, c)

        def copy(a, kk, block, to, src=None, rows=None):
            where = _slot(*block) if rows is None else (_slot(*block), rows)
            dst = outs[a].at[where]
            return pltpu.make_async_remote_copy(
                src_ref=dst if src is None else src, dst_ref=dst,
                send_sem=send_sems.at[a * per + kk], recv_sem=recv_sems.at[a * per + kk],
                device_id=to, device_id_type=MESH)

        def halves(a):
            h = ins[a].shape[0] // 2
            return pl.ds(0, h), pl.ds(h, ins[a].shape[0] - h)

        mine = [pltpu.make_async_copy(ins[a], outs[a].at[_slot(*me)], local_sems.at[a]) for a in range(n)]
        first = [copy(a, kk, me, to, src=ins[a]) for a in range(n) for kk, to in enumerate([sibling, beside, below])]
        relayed = [[copy(a, 3, beside, sibling), copy(a, 5, beside, below, rows=halves(a)[0])] for a in range(n)]
        relayed += [[copy(a, 4, below, sibling), copy(a, 6, below, beside, rows=halves(a)[1])] for a in range(n)]
        passed = [copy(a, 7, across, sibling) for a in range(n)]

        announce, arrive = _handshake([sibling, beside, below])

        def send():
            arrive()
            for cp in mine + first:
                cp.start()

        def relay():
            for kk, block in ((1, beside), (2, below)):
                for a in range(n):
                    copy(a, kk, block, me).wait_recv()
                    for cp in relayed[(kk - 1) * n + a]:
                        cp.start()

        def pass_on():
            for a in range(n):
                copy(a, 5, across, me, rows=halves(a)[0]).wait_recv()
                copy(a, 6, across, me, rows=halves(a)[1]).wait_recv()
                passed[a].start()

        def finish():
            for a in range(n):
                for kk, block in ((0, me), (3, beside), (4, below), (7, across)):
                    copy(a, kk, (*block[:2], 1 - c), me).wait_recv()
            for cp in first + [cp for two in relayed for cp in two] + passed:
                cp.wait_send()
            for cp in mine:
                cp.wait()

        return [announce, send, relay, pass_on, finish]

    scratch = [pltpu.SemaphoreType.DMA((per * n,)), pltpu.SemaphoreType.DMA((per * n,)), pltpu.SemaphoreType.DMA((n,))]
    return list(arrs), [_sds((N_DEV,) + a.shape, a.dtype) for a in arrs], scratch, phases, RELAY_ID


def _scatter_exchange(arrs):
    n = len(arrs)
    flips = [(fx, fy, fc) for fx in (0, 1) for fy in (0, 1) for fc in (0, 1)][1:]

    def phases(ins, outs, send_sems, recv_sems, local_sems):
        x, y, c = _place()
        mine = _slot(x, y, c)
        local = [pltpu.make_async_copy(ins[a].at[mine], outs[a].at[mine], local_sems.at[a]) for a in range(n)]
        peers = [((1 - x) if fx else x, (1 - y) if fy else y, (1 - c) if fc else c) for fx, fy, fc in flips]

        def copy(a, kk, src_slot, dst_slot):
            return pltpu.make_async_remote_copy(
                src_ref=ins[a].at[src_slot], dst_ref=outs[a].at[dst_slot],
                send_sem=send_sems.at[a * 7 + kk], recv_sem=recv_sems.at[a * 7 + kk],
                device_id=peers[kk], device_id_type=MESH)

        sends = [copy(a, kk, _slot(*peers[kk]), mine) for a in range(n) for kk in range(7)]

        announce, arrive = _handshake(peers)

        def send():
            arrive()
            for cp in local + sends:
                cp.start()

        def finish():
            for a in range(n):
                for kk in range(7):
                    copy(a, kk, mine, _slot(*peers[kk])).wait_recv()
            for cp in sends:
                cp.wait_send()
            for cp in local:
                cp.wait()

        return [announce, send, finish]

    return list(arrs), [_sds(a.shape, a.dtype) for a in arrs], _exchange_scratch(n), phases, SCATTER_ID


def _pair_exchange(arrs):
    n = len(arrs)

    def phases(ins, outs, send_sems, recv_sems, local_sems):
        x, y, c = _place()

        def copy(a, chip, side):
            return pltpu.make_async_remote_copy(
                src_ref=ins[a].at[chip, side], dst_ref=outs[a].at[chip],
                send_sem=send_sems.at[a * 7 + chip], recv_sem=recv_sems.at[a * 7 + chip],
                device_id=(x, y, 1 - c), device_id_type=MESH)

        sends = [copy(a, chip, 1 - c) for a in range(n) for chip in range(4)]

        announce, arrive = _handshake([(x, y, 1 - c)])

        def send():
            arrive()
            for cp in sends:
                cp.start()

        def finish():
            for a in range(n):
                for chip in range(4):
                    copy(a, chip, c).wait_recv()
            for cp in sends:
                cp.wait_send()

        return [announce, send, finish]

    return list(arrs), [_sds((4,) + a.shape[2:], a.dtype) for a in arrs], _exchange_scratch(n), phases, PAIR_ID


def _chip_exchange(arrs):
    n = len(arrs)

    def phases(ins, outs, send_sems, recv_sems, local_sems):
        x, y, c = _place()
        mine = 2 * x + y
        chips = [(1 - x, y), (x, 1 - y), (1 - x, 1 - y)]
        local = [pltpu.make_async_copy(ins[a].at[mine], outs[a].at[mine], local_sems.at[a]) for a in range(n)]

        def copy(a, j, src_slot, dst_slot):
            return pltpu.make_async_remote_copy(
                src_ref=ins[a].at[src_slot], dst_ref=outs[a].at[dst_slot],
                send_sem=send_sems.at[a * 7 + j], recv_sem=recv_sems.at[a * 7 + j],
                device_id=(*chips[j], c), device_id_type=MESH)

        sends = [copy(a, j, 2 * chips[j][0] + chips[j][1], mine) for a in range(n) for j in range(3)]

        announce, arrive = _handshake([(*chip, c) for chip in chips])

        def send():
            announce()
            arrive()
            for cp in local + sends:
                cp.start()

        def finish():
            for a in range(n):
                for j in range(3):
                    copy(a, j, mine, 2 * chips[j][0] + chips[j][1]).wait_recv()
            for cp in sends:
                cp.wait_send()
            for cp in local:
                cp.wait()

        return [send, lambda: None, finish]

    return list(arrs), [_sds(a.shape, a.dtype) for a in arrs], _exchange_scratch(n), phases, CHIP_ID


def _pair_sum(name, mine, theirs):
    _, _, r, c = mine.shape

    def body(side_ref, m_ref, t_ref, o_ref):
        o_ref[...] = (m_ref[...].astype(F32) + t_ref[...].astype(F32)).astype(o_ref.dtype)

    return pl.pallas_call(
        body, name=name,
        grid_spec=pltpu.PrefetchScalarGridSpec(
            num_scalar_prefetch=1, grid=(4,),
            in_specs=[pl.BlockSpec((None, None, r, c), lambda j, side: (j, side[0], 0, 0)),
                      pl.BlockSpec((None, r, c), lambda j, side: (j, 0, 0))],
            out_specs=pl.BlockSpec((None, r, c), lambda j, side: (j, 0, 0))),
        out_shape=_sds(theirs.shape, theirs.dtype),
        compiler_params=_params(("parallel",)),
    )(lax.axis_index("c").astype(jnp.int32).reshape(1), mine, theirs)


def _exchange_call(name, exchange):
    arrs, out_shape, scratch, phases, collective_id = exchange
    n = len(arrs)

    def body(*refs):
        for step in phases(refs[:n], refs[n:2 * n], *refs[2 * n:]):
            step()

    return pl.pallas_call(body, name=name, in_specs=[ANY] * n, out_specs=[ANY] * n,
                          out_shape=out_shape, scratch_shapes=scratch,
                          compiler_params=pltpu.CompilerParams(collective_id=collective_id))(*arrs)


def _carry_exchange(exchange, refs, n_in, n_out, first, middle, last, halfway, at_once=False):
    arrs, _, _, phases, _ = exchange
    n = len(arrs)
    if n == 0:
        return lambda: None
    ins = refs[n_in:n_in + n]
    outs = refs[n_in + n + n_out:n_in + 2 * n + n_out]
    sems = n_in + 2 * n + n_out
    steps = phases(ins, outs, *refs[sems:sems + 3])
    pl.when(first)(steps[0])
    if at_once:
        pl.when(first)(steps[1])

    def close():
        if not at_once:
            pl.when(first)(steps[1])
        if len(steps) == 5:
            pl.when(halfway)(steps[2])
        if len(steps) >= 4:
            pl.when(middle)(steps[-2])
        pl.when(last)(steps[-1])

    return close


def _adamw_math(w, g, m, v):
    m2 = ADAM_B1 * m + (1.0 - ADAM_B1) * g
    v2 = ADAM_B2 * v + (1.0 - ADAM_B2) * jnp.square(g)
    m_hat = m2 / (1.0 - ADAM_B1 ** ADAM_STEP)
    v_hat = v2 / (1.0 - ADAM_B2 ** ADAM_STEP)
    delta = -ADAM_LR * (m_hat / (jnp.sqrt(v_hat) + ADAM_EPS) + ADAM_WD * w)
    return delta, m2, v2


def _sum_adamw_tile(parts, w, m, v):
    g = parts[0].astype(F32)
    for d in range(1, parts.shape[0]):
        g = g + parts[d].astype(F32)
    return (g, *_adamw_math(w, g, m, v))


def _sum_adamw(name, parts, w, m, v, tr=256):
    p, r, c = parts.shape
    tr = _pick(r, tr, 16)

    def body(p_ref, w_ref, m_ref, v_ref, g_ref, d_ref, m2_ref, v2_ref):
        g_ref[...], d_ref[...], m2_ref[...], v2_ref[...] = _sum_adamw_tile(p_ref[...], w_ref[...], m_ref[...], v_ref[...])

    tile = pl.BlockSpec((tr, c), lambda i: (i, 0))
    return pl.pallas_call(
        body, name=name, grid=(r // tr,),
        in_specs=[pl.BlockSpec((p, tr, c), lambda i: (0, i, 0)), tile, tile, tile],
        out_specs=[tile] * 4, out_shape=[_sds((r, c))] * 4,
        compiler_params=_params(("parallel",)),
    )(parts, w, m, v)


def _sum_parts(name, parts):
    p, r, c = parts.shape

    def body(p_ref, o_ref):
        g = p_ref[0]
        for d in range(1, p):
            g = g + p_ref[d]
        o_ref[...] = g

    return pl.pallas_call(
        body, name=name, out_shape=_sds((r, c)),
        in_specs=[pl.BlockSpec(memory_space=pltpu.VMEM)], out_specs=pl.BlockSpec(memory_space=pltpu.VMEM),
    )(parts)


WEIGHTS = ["norm_mix_pre", "w_in", "conv_dw_w", "conv_dw_b", "conv_ln_g", "conv_ln_b", "w_conv_branch",
           "b_conv_branch", "w_att_branch", "w_out", "norm_mix_post", "norm_ffn_pre", "w_ffn_up", "w_ffn_down",
           "norm_ffn_post"]
COL_SHARDED = ["w_conv_branch", "w_att_branch"]
TRANSPOSED = ["w_in", "w_ffn_up"]
VECTORS = ["norm_mix_pre", "conv_dw_b", "conv_ln_g", "conv_ln_b", "b_conv_branch", "norm_mix_post",
           "norm_ffn_pre", "norm_ffn_post"]


def _cols_to_full(g):
    return g.transpose(1, 0, 2).reshape(g.shape[1], N_DEV * g.shape[2])


def _full_to_cols(f):
    return f.reshape(f.shape[0], N_DEV, f.shape[1] // N_DEV).transpose(1, 0, 2)


PACK_ROWS = 7


def _pack_vectors(vecs, extra=None):
    parts = [vecs[nm].reshape(-1) for nm in VECTORS]
    parts.append(jnp.zeros((1,), F32) if extra is None else extra.reshape(1))
    used = sum(p.size for p in parts)
    parts.append(jnp.zeros((PACK_ROWS * D_MODEL - used,), F32))
    return jnp.concatenate(parts).reshape(PACK_ROWS, D_MODEL)


def _unpack_vectors(packed, sizes):
    flat, out, at = packed.reshape(-1), {}, 0
    for nm in VECTORS:
        out[nm] = flat[at:at + sizes[nm]]
        at += sizes[nm]
    return out, flat[at]


def kernel(x, norm_mix_pre, w_in, conv_dw_w, conv_dw_b, conv_ln_g, conv_ln_b, w_conv_branch, b_conv_branch, w_att_branch, w_out, norm_mix_post, norm_ffn_pre, w_ffn_up, w_ffn_down, norm_ffn_post, loss_target, m_norm_mix_pre, m_w_in, m_conv_dw_w, m_conv_dw_b, m_conv_ln_g, m_conv_ln_b, m_w_conv_branch, m_b_conv_branch, m_w_att_branch, m_w_out, m_norm_mix_post, m_norm_ffn_pre, m_w_ffn_up, m_w_ffn_down, m_norm_ffn_post, v_norm_mix_pre, v_w_in, v_conv_dw_w, v_conv_dw_b, v_conv_ln_g, v_conv_ln_b, v_w_conv_branch, v_b_conv_branch, v_w_att_branch, v_w_out, v_norm_mix_post, v_norm_ffn_pre, v_w_ffn_up, v_w_ffn_down, v_norm_ffn_post):
    ws = dict(zip(WEIGHTS, [norm_mix_pre, w_in, conv_dw_w, conv_dw_b, conv_ln_g, conv_ln_b, w_conv_branch,
                            b_conv_branch, w_att_branch, w_out, norm_mix_post, norm_ffn_pre, w_ffn_up, w_ffn_down,
                            norm_ffn_post]))
    ms = dict(zip(WEIGHTS, [m_norm_mix_pre, m_w_in, m_conv_dw_w, m_conv_dw_b, m_conv_ln_g, m_conv_ln_b,
                            m_w_conv_branch, m_b_conv_branch, m_w_att_branch, m_w_out, m_norm_mix_post,
                            m_norm_ffn_pre, m_w_ffn_up, m_w_ffn_down, m_norm_ffn_post]))
    vs = dict(zip(WEIGHTS, [v_norm_mix_pre, v_w_in, v_conv_dw_w, v_conv_dw_b, v_conv_ln_g, v_conv_ln_b,
                            v_w_conv_branch, v_b_conv_branch, v_w_att_branch, v_w_out, v_norm_mix_post,
                            v_norm_ffn_pre, v_w_ffn_up, v_w_ffn_down, v_norm_ffn_post]))

    dw_block = jnp.pad(conv_dw_w, ((0, 1), (0, 0)))
    g_in, g_dw = _exchange_call("gather_first", _relay_gather_exchange([w_in.T.astype(BF16), dw_block]))
    full = {"w_in": _full_weight("w_in", g_in), "conv_dw_w": _cols_to_full(g_dw)}
    for nm in VECTORS:
        full[nm] = ws[nm].reshape(1, -1)

    def as_kept(nm, a):
        return a.T if nm in TRANSPOSED else a

    ride_along = ["w_ffn_up", "w_out"]
    loss_local, grad_x, received, updated, grads = _local_step(
        x[0], loss_target[0], full, {nm: as_kept(nm, ws[nm]).astype(BF16) for nm in LATE},
        {nm: tuple(as_kept(nm, a[nm]) for a in (ws, ms, vs)) for nm in ride_along})

    small = _exchange_call("gather_small_grads", _gather_exchange(
        [_pack_vectors(grads, extra=loss_local), grads["conv_dw_w"]]))
    out_g, out_d, out_m, out_v = {}, {}, {}, {}
    for nm in LATE + ["w_in"]:
        res = updated[nm] if nm in updated else _sum_adamw(
            "adamw_" + nm, received[nm], *[as_kept(nm, a[nm]) for a in (ws, ms, vs)])
        out_g[nm], out_d[nm], out_m[nm], out_v[nm] = [as_kept(nm, r) for r in res]
    sizes = {nm: ws[nm].size for nm in VECTORS}
    vec = _sum_adamw("adamw_vectors", small[0], _pack_vectors(ws), _pack_vectors(ms), _pack_vectors(vs))
    for res, dst in zip(vec, (out_g, out_d, out_m, out_v)):
        dst.update(_unpack_vectors(res, sizes)[0])
    loss = _unpack_vectors(vec[0], sizes)[1]
    dw_full = _sum_parts("sum_dw_grads", small[1])
    me = _slot(*_place())
    dw_mine = lax.dynamic_slice(dw_full, (0, me * (CONV_DIM // N_DEV)), (CONV_WIDTH, CONV_DIM // N_DEV))
    nm = "conv_dw_w"
    out_g[nm], out_d[nm], out_m[nm], out_v[nm] = _sum_adamw("adamw_dw", dw_mine[None], ws[nm], ms[nm], vs[nm])

    outs = [loss, grad_x[None]]
    for group in (out_g, out_d, out_m, out_v):
        outs += [group[nm] for nm in WEIGHTS]
    return tuple(outs)
```

```python
import math

import jax
import jax.numpy as jnp
from jax import lax
from jax.experimental import pallas as pl
from jax.experimental.pallas import tpu as pltpu

F32 = jnp.float32
BF16 = jnp.bfloat16

N_DEV = 8
D_MODEL = 1024
CONV_DIM = 512
CONV_WIDTH = 31
N_HEADS = 8
HEAD_DIM = 64
ATT_DIM = N_HEADS * HEAD_DIM
D_FF = 2816
EPS = 1e-6
IN_SPLITS = (0, 1024, 1536, 2048, 2560, 3584, 4608)

ADAM_LR = 0.001
ADAM_B1 = 0.9
ADAM_B2 = 0.999
ADAM_EPS = 1e-08
ADAM_WD = 0.01
ADAM_STEP = 10

LANES = 128
SUBLANES = 8
HALO = 32
ATT_TILE = 256
ATT_PART = 176
DEAD_SUM = -120.0
VMEM_LIMIT = 56 * 1024 * 1024
MESH = pl.DeviceIdType.MESH
ANY = pl.BlockSpec(memory_space=pl.ANY)


def _pick(dim, target, align=LANES):
    t = min(dim, target)
    t -= t % align
    while t >= align:
        if dim % t == 0:
            return t
        t -= align
    return dim


def _params(semantics, collective_id=None):
    return pltpu.CompilerParams(dimension_semantics=semantics, vmem_limit_bytes=VMEM_LIMIT,
                                collective_id=collective_id)


def _tn_matmul(a, b, *, name):
    return _pieces_tn_matmul([a], b, name=name, tj=_pick(a.shape[1], 1408))


def _pieces_tn_matmul(pieces, b, *, name, tj=512, exchange=None):
    s, n = b.shape
    counts = [p.shape[1] // tj for p in pieces]
    starts = [sum(counts[:i]) for i in range(len(pieces))]
    assert all(p.shape == (s, c * tj) for p, c in zip(pieces, counts))
    x_arrs, x_shape, x_scratch, _, x_id = exchange or NO_EXCHANGE
    nx, n_in = len(x_arrs), len(pieces) + 1

    def body(*refs):
        b_ref, o_ref = refs[n_in - 1], refs[n_in + nx]
        finish_exchange = _carry_exchange(exchange or NO_EXCHANGE, refs, n_in, 1, *_sweep_marks(sum(counts)))
        j = pl.program_id(0)
        for p_ref, first, count in zip(refs, starts, counts):
            @pl.when((j >= first) & (j < first + count))
            def _():
                o_ref[...] = lax.dot_general(p_ref[...].astype(BF16), b_ref[...], TN,
                                             preferred_element_type=F32).astype(o_ref.dtype)
        finish_exchange()

    def piece_spec(first, count):
        return pl.BlockSpec((s, tj), lambda j: (0, jnp.clip(j - first, 0, count - 1)))

    res = pl.pallas_call(
        body, name=name, grid=(sum(counts),),
        in_specs=[piece_spec(f, c) for f, c in zip(starts, counts)]
        + [pl.BlockSpec((s, n), lambda j: (0, 0), pipeline_mode=pl.Buffered(1))] + [ANY] * nx,
        out_specs=[pl.BlockSpec((tj, n), lambda j: (j, 0))] + [ANY] * nx,
        out_shape=[jax.ShapeDtypeStruct((sum(counts) * tj, n), BF16)] + x_shape, scratch_shapes=x_scratch,
        compiler_params=_params(("arbitrary",), x_id),
    )(*pieces, b, *x_arrs)
    return res[0] if exchange is None else (res[0], res[1:])


NO_EXCHANGE = ([], [], [], None, None)


def _sweep_marks(nt):
    i = pl.program_id(0)
    return i == 0, i == max(nt - 2, 0), i == nt - 1, i == max(nt // 2 - 1, 0)


def _rowwise(name, fn, rows, bcasts, row_outs, red_outs=(), tm=256, exchange=NO_EXCHANGE):
    s = rows[0].shape[0]
    tm = _pick(s, tm, 16)
    nt = s // tm
    resident = pl.Buffered(1)
    nr, nb, no, nd = len(rows), len(bcasts), len(row_outs), len(red_outs)
    x_arrs, x_shape, x_scratch, _, x_id = exchange
    nx = len(x_arrs)
    first_out = nr + nb + nx

    def body(*refs):
        finish_exchange = _carry_exchange(exchange, refs, nr + nb, no + nd, *_sweep_marks(nt), at_once=True)
        ins = [r[...] for r in refs[:nr + nb]]
        outs, reds = fn(*ins)
        for ref, val in zip(refs[first_out:first_out + no], outs):
            ref[...] = val.astype(ref.dtype)
        i = pl.program_id(0)
        for ref, val in zip(refs[first_out + no:first_out + no + nd], reds):
            @pl.when(i == 0)
            def _():
                ref[...] = val

            @pl.when(i > 0)
            def _():
                ref[...] += val
        finish_exchange()

    def row_spec(a):
        assert a.shape[-2] % nt == 0, (name, a.shape, nt)
        if len(a.shape) == 3:
            return pl.BlockSpec((a.shape[0], a.shape[1] // nt, a.shape[2]), lambda i: (0, i, 0))
        return pl.BlockSpec((a.shape[0] // nt, a.shape[1]), lambda i: (i, 0))

    in_specs = [row_spec(r) for r in rows]
    in_specs += [pl.BlockSpec(b.shape, lambda i: (0, 0), pipeline_mode=resident) for b in bcasts]
    out_specs = [row_spec(o) for o in row_outs]
    out_specs += [pl.BlockSpec(d.shape, lambda i: (0, 0)) for d in red_outs]
    return pl.pallas_call(
        body, name=name, grid=(nt,), in_specs=in_specs + [ANY] * nx, out_specs=out_specs + [ANY] * nx,
        out_shape=list(row_outs) + list(red_outs) + x_shape, scratch_shapes=x_scratch,
        compiler_params=_params(("arbitrary",), x_id),
    )(*rows, *bcasts, *x_arrs)


def _sds(shape, dtype=F32):
    return jax.ShapeDtypeStruct(shape, dtype)


def _rms(x, g):
    y = x * lax.rsqrt(jnp.mean(x * x, axis=-1, keepdims=True) + EPS)
    return y * g


def _silu(x):
    return x * jax.nn.sigmoid(x)


def _swiglu(g, u):
    return _silu(g) * u


def _ln_silu(u, g, b):
    mu = jnp.mean(u, axis=-1, keepdims=True)
    var = jnp.mean(jnp.square(u - mu), axis=-1, keepdims=True)
    return _silu((u - mu) * lax.rsqrt(var + EPS) * g + b)


def _merge(conv_pre, att_out, g_conv, g_att, b_cb):
    return jax.nn.sigmoid(g_conv) * (conv_pre + b_cb) + jax.nn.sigmoid(g_att) * att_out


def _glu(t):
    return t[:, :CONV_DIM] * jax.nn.sigmoid(t[:, CONV_DIM:])


def _shifted_reader(buf, shifted, tm):
    for b in range(1, SUBLANES):
        shifted[b - 1, :, :] = buf[pl.ds(b, tm + HALO - SUBLANES), :]

    def read(o):
        a, b = divmod(o, SUBLANES)
        return buf[pl.ds(SUBLANES * a, tm), :] if b == 0 else shifted[b - 1, pl.ds(SUBLANES * a, tm), :]

    return read


def _conv_fwd(conv_in, w_pad, b, ln_g, ln_b, exchange, tm=256):
    s = conv_in.shape[0]
    tm = _pick(s, tm, HALO)
    ratio = tm // HALO
    x_arrs, x_shape, x_scratch, _, x_id = exchange
    nx = len(x_arrs)

    def body(*refs):
        main_ref, halo_ref, w_ref, b_ref, g_ref, be_ref = refs[:6]
        u3_ref, u1_ref = refs[6 + nx:8 + nx]
        buf, shifted = refs[-2:]
        finish_exchange = _carry_exchange(exchange, refs, 6, 2, *_sweep_marks(s // tm), at_once=True)
        i = pl.program_id(0)
        buf[0:HALO, :] = _glu(halo_ref[...]) * (i > 0).astype(F32)
        buf[HALO:HALO + tm, :] = _glu(main_ref[...])
        read = _shifted_reader(buf, shifted, tm)
        acc = jnp.zeros((tm, CONV_DIM), F32) + b_ref[...]
        for j in range(CONV_WIDTH):
            acc = acc + w_ref[j:j + 1, :] * read(HALO - (CONV_WIDTH - 1) + j)
        u1_ref[...] = acc
        u3_ref[...] = _ln_silu(acc, g_ref[...], be_ref[...]).astype(u3_ref.dtype)
        finish_exchange()

    res = pl.pallas_call(
        body, name="conv_fwd", grid=(s // tm,),
        in_specs=[pl.BlockSpec((tm, 2 * CONV_DIM), lambda i: (i, 0)),
                  pl.BlockSpec((HALO, 2 * CONV_DIM), lambda i: (jnp.maximum(i * ratio - 1, 0), 0)),
                  pl.BlockSpec(w_pad.shape, lambda i: (0, 0)),
                  pl.BlockSpec(b.shape, lambda i: (0, 0)),
                  pl.BlockSpec(ln_g.shape, lambda i: (0, 0)),
                  pl.BlockSpec(ln_b.shape, lambda i: (0, 0))] + [ANY] * nx,
        out_specs=[pl.BlockSpec((tm, CONV_DIM), lambda i: (i, 0)),
                   pl.BlockSpec((tm, CONV_DIM), lambda i: (i, 0))] + [ANY] * nx,
        out_shape=[_sds((s, CONV_DIM), BF16), _sds((s, CONV_DIM), F32)] + x_shape,
        scratch_shapes=x_scratch + [pltpu.VMEM((tm + HALO, CONV_DIM), F32),
                                    pltpu.VMEM((SUBLANES - 1, tm + HALO - SUBLANES, CONV_DIM), F32)],
        compiler_params=_params(("arbitrary",), x_id),
    )(conv_in, conv_in, w_pad, b, ln_g, ln_b, *x_arrs)
    return res[0], res[1], res[2:]


def _conv_bwd(conv_in, u1, du3, ln_g, ln_b, w_pad, exchange, tm=256):
    s = conv_in.shape[0]
    tm = _pick(s, tm, HALO)
    ratio = tm // HALO
    nt = s // tm
    last_halo = s // HALO - 1
    x_arrs, x_shape, x_scratch, _, x_id = exchange
    nx = len(x_arrs)

    def body(*refs):
        main_ref, halo_ref, u1_ref, u1n_ref, du3_ref, du3n_ref, g_ref, be_ref, w_ref = refs[:9]
        dci_ref, dw_ref, db_ref, dg_ref, dbe_ref = refs[9 + nx:14 + nx]
        ubuf, dbuf, ushift, dshift = refs[-4:]
        finish_exchange = _carry_exchange(exchange, refs, 9, 5, *_sweep_marks(nt))
        i = pl.program_id(0)
        main = main_ref[...]
        a = main[:, :CONV_DIM]
        sb = jax.nn.sigmoid(main[:, CONV_DIM:])
        ubuf[0:HALO, :] = _glu(halo_ref[...]) * (i > 0).astype(F32)
        ubuf[HALO:HALO + tm, :] = a * sb

        def ln_bwd(u1t, du3t):
            _, vjp = jax.vjp(_ln_silu, u1t, g_ref[...], be_ref[...])
            return vjp(du3t)

        du, dg, dbe = ln_bwd(u1_ref[...], du3_ref[...])
        dbuf[0:tm, :] = du
        dbuf[tm:tm + HALO, :] = ln_bwd(u1n_ref[...], du3n_ref[...])[0] * (i < nt - 1).astype(F32)

        @pl.when(i == 0)
        def _():
            dw_ref[...] = jnp.zeros_like(dw_ref)
            db_ref[...] = jnp.zeros_like(db_ref)
            dg_ref[...] = jnp.zeros_like(dg_ref)
            dbe_ref[...] = jnp.zeros_like(dbe_ref)

        dg_ref[...] += dg
        dbe_ref[...] += dbe

        read_u = _shifted_reader(ubuf, ushift, tm)
        read_d = _shifted_reader(dbuf, dshift, tm)
        du0 = jnp.zeros((tm, CONV_DIM), F32)
        for j in range(CONV_WIDTH):
            du0 = du0 + w_ref[j:j + 1, :] * read_d(CONV_WIDTH - 1 - j)
            dw_ref[j:j + 1, :] += jnp.sum(du * read_u(HALO - (CONV_WIDTH - 1) + j), axis=0, keepdims=True)
        db_ref[...] += jnp.sum(du, axis=0, keepdims=True)
        dci_ref[:, :CONV_DIM] = (du0 * sb).astype(dci_ref.dtype)
        dci_ref[:, CONV_DIM:] = (du0 * a * sb * (1.0 - sb)).astype(dci_ref.dtype)
        finish_exchange()

    res = pl.pallas_call(
        body, name="conv_bwd", grid=(nt,),
        in_specs=[pl.BlockSpec((tm, 2 * CONV_DIM), lambda i: (i, 0)),
                  pl.BlockSpec((HALO, 2 * CONV_DIM), lambda i: (jnp.maximum(i * ratio - 1, 0), 0))]
        + [pl.BlockSpec((tm, CONV_DIM), lambda i: (i, 0)),
           pl.BlockSpec((HALO, CONV_DIM), lambda i: (jnp.minimum((i + 1) * ratio, last_halo), 0))] * 2
        + [pl.BlockSpec((1, CONV_DIM), lambda i: (0, 0))] * 2 + [pl.BlockSpec(w_pad.shape, lambda i: (0, 0))]
        + [ANY] * nx,
        out_specs=[pl.BlockSpec((tm, 2 * CONV_DIM), lambda i: (i, 0)),
                   pl.BlockSpec(w_pad.shape, lambda i: (0, 0))]
        + [pl.BlockSpec((1, CONV_DIM), lambda i: (0, 0))] * 3 + [ANY] * nx,
        out_shape=[_sds((s, 2 * CONV_DIM), BF16), _sds(w_pad.shape)] + [_sds((1, CONV_DIM))] * 3 + x_shape,
        scratch_shapes=x_scratch + [pltpu.VMEM((tm + HALO, CONV_DIM), F32)] * 2
        + [pltpu.VMEM((SUBLANES - 1, tm + HALO - SUBLANES, CONV_DIM), F32)] * 2,
        compiler_params=_params(("arbitrary",), x_id),
    )(conv_in, conv_in, u1, u1, du3, du3, ln_g, ln_b, w_pad, *x_arrs)
    return res[:5], res[5:]


def _logsig_neg(z):
    return jnp.minimum(-z, 0.0) - jnp.log(1.0 + jnp.exp(-jnp.abs(z)))


def _split_dot(val, tri):
    hi = val.astype(BF16)
    lo = (val - hi.astype(F32)).astype(BF16)
    return jnp.dot(hi, tri, preferred_element_type=F32) + jnp.dot(lo, tri, preferred_element_type=F32)


def _attn_masks(t, later):
    row = lax.broadcasted_iota(jnp.int32, (t, t), 0)
    col = lax.broadcasted_iota(jnp.int32, (t, t), 1)
    tri = jnp.where(row > col if later else row <= col, 1.0, 0.0).astype(BF16)
    return col < row, tri


def _grid_marks(h, nq):
    hh, i = pl.program_id(0), pl.program_id(1)
    return ((hh == 0) & (i == 0), (hh == h - 1) & (i == nq // 2), (hh == h - 1) & (i == nq - 1),
            (hh == h // 2) & (i == nq // 2))


def _head_masks(shape):
    lane = lax.broadcasted_iota(jnp.int32, shape, len(shape) - 1)
    return lane < HEAD_DIM, lane >= HEAD_DIM


def _per_head(blk):
    m0, m1 = _head_masks(blk.shape)
    zero = jnp.zeros_like(blk)
    return jnp.where(m0, blk, zero), jnp.where(m1, blk, zero)


NT = (((1,), (1,)), ((), ()))
TN = (((0,), (0,)), ((), ()))


def _with_top(whole, top):
    rows = top.shape[0]
    return top if rows == whole.shape[0] else jnp.concatenate([top, whole[rows:]], axis=0)


def _attn_fwd(q, k, v, exchange):
    s = q.shape[0]
    hp = q.shape[1] // LANES
    t = ATT_TILE
    scale = 1.0 / math.sqrt(HEAD_DIM)
    x_arrs, x_shape, x_scratch, _, x_id = exchange
    nx = len(x_arrs)

    def body(*refs):
        q_ref, k_ref, v_ref = refs[:3]
        o_ref, lt_ref, nb_ref = refs[3 + nx:6 + nx]
        finish_exchange = _carry_exchange(exchange, refs, 3, 3, *_grid_marks(hp, s // t))
        i = pl.program_id(1)
        qs = _per_head((q_ref[...].astype(F32) * scale).astype(BF16))
        causal, tri = _attn_masks(t, later=True)

        def step(kb, carry, masked, rows):
            cs, acc = carry
            off = pl.multiple_of(kb * t, t)
            kblk = k_ref[pl.ds(off, t), :]
            vs = _per_head(v_ref[pl.ds(off, t), :])
            acc_top = acc[:rows]
            new_cs = []
            for hd in range(2):
                z = lax.dot_general(qs[hd][:rows], kblk, NT, preferred_element_type=F32)
                l = _logsig_neg(z)
                if masked:
                    l = jnp.where(causal, l, 0.0)
                e = z + l + _split_dot(l, tri) + cs[hd][:rows]
                if masked:
                    e = jnp.where(causal, e, -1e30)
                acc_top = acc_top + jnp.dot(jnp.exp(e).astype(BF16), vs[hd], preferred_element_type=F32)
                new_cs.append(_with_top(cs[hd], cs[hd][:rows] + jnp.sum(l, axis=1, keepdims=True)))
            return tuple(new_cs), _with_top(acc, acc_top)

        zero = jnp.zeros((t, 1), F32)
        carry = step(i, ((zero, zero), jnp.zeros((t, LANES), F32)), True, t)

        def live(cs, lo, hi):
            return jnp.maximum(jnp.max(cs[0][lo:hi]), jnp.max(cs[1][lo:hi])) > DEAD_SUM

        def more(state):
            n, _, (cs, _) = state
            return (n < i) & live(cs, 0, t)

        def sweep(state):
            n, n_full, cr = state
            whole = live(cr[0], ATT_PART, t)
            cr = lax.cond(whole, lambda c: step(i - 1 - n, c, False, t), lambda c: step(i - 1 - n, c, False, ATT_PART), cr)
            return n + 1, n_full + whole.astype(jnp.int32), cr

        n_blocks, n_full, carry = lax.while_loop(more, sweep, (jnp.int32(0), jnp.int32(0), carry))
        m0, _ = _head_masks((t, LANES))
        lt_ref[...] = jnp.where(m0, carry[0][0], carry[0][1])
        o_ref[...] = carry[1].astype(o_ref.dtype)
        nb_ref[0, pl.program_id(0), i] = n_blocks.astype(F32)
        nb_ref[1, pl.program_id(0), i] = n_full.astype(F32)
        finish_exchange()

    res = pl.pallas_call(
        body, name="attn_fwd", grid=(hp, s // t),
        in_specs=[pl.BlockSpec((t, LANES), lambda p, i: (i, p)),
                  pl.BlockSpec((s, LANES), lambda p, i: (0, p)),
                  pl.BlockSpec((s, LANES), lambda p, i: (0, p))] + [ANY] * nx,
        out_specs=[pl.BlockSpec((t, LANES), lambda p, i: (i, p)),
                   pl.BlockSpec((None, t, LANES), lambda p, i: (p, i, 0)),
                   pl.BlockSpec(memory_space=pltpu.SMEM)] + [ANY] * nx,
        out_shape=[_sds(q.shape, BF16), _sds((hp, s, LANES), F32), _sds((2, hp, s // t), F32)] + x_shape,
        scratch_shapes=x_scratch,
        compiler_params=_params(("arbitrary", "arbitrary"), x_id),
    )(q, k, v, *x_arrs)
    return res[0], res[1], res[2], res[3:]


def _attn_bwd(q, k, v, do, ltot, n_blocks, exchange):
    s = q.shape[0]
    hp = q.shape[1] // LANES
    t = ATT_TILE
    scale = 1.0 / math.sqrt(HEAD_DIM)
    x_arrs, x_shape, x_scratch, _, x_id = exchange
    nx = len(x_arrs)

    def body(*refs):
        q_ref, k_ref, v_ref, do_ref, lt_ref, nb_ref = refs[:6]
        dq_ref, dk_ref, dv_ref = refs[6 + nx:9 + nx]
        finish_exchange = _carry_exchange(exchange, refs, 6, 3, *_grid_marks(hp, s // t))
        i = pl.program_id(1)
        n_blocks = jnp.clip(nb_ref[0, pl.program_id(0), i].astype(jnp.int32), 0, i)
        n_full = jnp.clip(nb_ref[1, pl.program_id(0), i].astype(jnp.int32), 0, n_blocks)

        @pl.when(i == 0)
        def _():
            dk_ref[...] = jnp.zeros_like(dk_ref)
            dv_ref[...] = jnp.zeros_like(dv_ref)

        qb = q_ref[...]
        qm = _per_head(qb)
        qs = _per_head((qb.astype(F32) * scale).astype(BF16))
        dos = _per_head(do_ref[...])
        lts = (lt_ref[:, 0:1], lt_ref[:, HEAD_DIM:HEAD_DIM + 1])
        causal, tri = _attn_masks(t, later=False)

        def step(kb, carry, masked, rows):
            cls, cgs, dq = carry
            off = pl.multiple_of(kb * t, t)
            kblk = k_ref[pl.ds(off, t), :]
            vblk = v_ref[pl.ds(off, t), :]
            ks = _per_head(kblk)
            dq_top = dq[:rows]
            dk = jnp.zeros((t, LANES), F32)
            dv = jnp.zeros((t, LANES), F32)
            new_cls, new_cgs = [], []
            for hd in range(2):
                z = lax.dot_general(qs[hd][:rows], kblk, NT, preferred_element_type=F32)
                l = _logsig_neg(z)
                if masked:
                    l = jnp.where(causal, l, 0.0)
                e = z + l + ((lts[hd][:rows] - cls[hd][:rows]) - _split_dot(l, tri))
                if masked:
                    e = jnp.where(causal, e, -1e30)
                a = jnp.exp(e)
                g = lax.dot_general(dos[hd][:rows], vblk, NT, preferred_element_type=F32) * a
                p = cgs[hd][:rows] + jnp.dot(g.astype(BF16), tri, preferred_element_type=F32) - g
                el = jnp.exp(l)
                dz = g * el - p * (1.0 - el)
                if masked:
                    dz = jnp.where(causal, dz, 0.0)
                dzb = (dz * scale).astype(BF16)
                dq_top = dq_top + jnp.dot(dzb, ks[hd], preferred_element_type=F32)
                dk = dk + lax.dot_general(dzb, qm[hd][:rows], TN, preferred_element_type=F32)
                dv = dv + lax.dot_general(a.astype(BF16), dos[hd][:rows], TN, preferred_element_type=F32)
                new_cls.append(_with_top(cls[hd], cls[hd][:rows] + jnp.sum(l, axis=1, keepdims=True)))
                new_cgs.append(_with_top(cgs[hd], cgs[hd][:rows] + jnp.sum(g, axis=1, keepdims=True)))
            dk_ref[pl.ds(off, t), :] += dk
            dv_ref[pl.ds(off, t), :] += dv
            return tuple(new_cls), tuple(new_cgs), _with_top(dq, dq_top)

        zero = jnp.zeros((t, 1), F32)
        init = ((zero, zero), (zero, zero), jnp.zeros((t, LANES), F32))
        carry = lax.fori_loop(i - n_blocks, i - n_full, lambda kb, cr: step(kb, cr, False, ATT_PART), init)
        carry = lax.fori_loop(i - n_full, i, lambda kb, cr: step(kb, cr, False, t), carry)
        carry = step(i, carry, True, t)
        dq_ref[...] = carry[2]
        finish_exchange()

    blk = pl.BlockSpec((t, LANES), lambda p, i: (i, p))
    whole = pl.BlockSpec((s, LANES), lambda p, i: (0, p))
    res = pl.pallas_call(
        body, name="attn_bwd", grid=(hp, s // t),
        in_specs=[blk, whole, whole, blk, pl.BlockSpec((None, t, LANES), lambda p, i: (p, i, 0)),
                  pl.BlockSpec(memory_space=pltpu.SMEM)] + [ANY] * nx,
        out_specs=[blk, whole, whole] + [ANY] * nx,
        out_shape=[_sds(q.shape)] * 3 + x_shape,
        scratch_shapes=x_scratch,
        compiler_params=_params(("arbitrary", "arbitrary"), x_id),
    )(q, k, v, do, ltot, n_blocks, *x_arrs)
    return res[0], res[1], res[2], res[3:]


LATE = ["w_conv_branch", "w_att_branch", "w_out", "w_ffn_up", "w_ffn_down"]


def _full_weight(name, gathered):
    return _cols_to_full(gathered) if name in COL_SHARDED else gathered.reshape(-1, gathered.shape[2])


def _grad_slabs(name, grad):
    return _full_to_cols(grad) if name in COL_SHARDED else grad.reshape(N_DEV, -1, grad.shape[1])


def _side_slabs(name, grad):
    slabs = _grad_slabs(name, grad)
    return slabs.reshape((4, 2) + slabs.shape[1:])


def _local_step(x, target, w, late_blocks, opt):
    s = x.shape[0]
    w = dict(w)
    g1, g2, g3, g4 = w["norm_mix_pre"], w["norm_mix_post"], w["norm_ffn_pre"], w["norm_ffn_post"]

    w_in = w["w_in"]

    def proj_fn(xt, g1_, w_in_t):
        h = _rms(xt, g1_).astype(BF16)
        proj = lax.dot_general(h, w_in_t, NT, preferred_element_type=F32)
        return (h, *[proj[:, IN_SPLITS[n]:IN_SPLITS[n + 1]] for n in range(6)]), ()

    mix_weights = ["w_conv_branch", "w_att_branch", "w_out"]
    h1, conv_in, q, k, v, g_conv, g_att, g_out = _rowwise(
        "norm_proj", proj_fn, [x], [g1, w_in],
        [_sds((s, D_MODEL), BF16), _sds((s, 2 * CONV_DIM)), _sds((s, ATT_DIM), BF16), _sds((s, ATT_DIM), BF16),
         _sds((s, ATT_DIM), BF16), _sds((s, D_MODEL), BF16), _sds((s, D_MODEL), BF16)], tm=512,
        exchange=_relay_gather_exchange([late_blocks["w_out"]]))

    u3, u1, g_branches = _conv_fwd(conv_in, w["conv_dw_w"], w["conv_dw_b"], w["conv_ln_g"], w["conv_ln_b"],
                                   _relay_gather_exchange([late_blocks[nm] for nm in mix_weights[:2]]))
    for nm, g in zip(mix_weights, [*g_branches, g_out]):
        w[nm] = _full_weight(nm, g)
    half = D_MODEL // 2
    down_block = late_blocks["w_ffn_down"]
    att, ltot, n_blocks, (g_up, g_left) = _attn_fwd(
        q, k, v, _relay_gather_exchange([late_blocks["w_ffn_up"], down_block[:, :half]]))
    w["w_ffn_up"] = _full_weight("w_ffn_up", g_up)

    def merge_fn(u3t, at, gc, ga, xt, w_cb, w_ab, b_cb, w_out, g2_, g3_):
        cp = jnp.dot(u3t, w_cb, preferred_element_type=F32)
        ao = jnp.dot(at, w_ab, preferred_element_type=F32)
        mg = _merge(cp, ao, gc.astype(F32), ga.astype(F32), b_cb).astype(BF16)
        mix_ = jnp.dot(mg, w_out, preferred_element_type=F32)
        x2_ = xt + _rms(mix_, g2_)
        return (mg, cp, ao, mix_, x2_, _rms(x2_, g3_)), ()

    merged, conv_pre, att_out, mix, x2, h2 = _rowwise(
        "branch_merge_mix", merge_fn, [u3, att, g_conv, g_att, x],
        [w["w_conv_branch"], w["w_att_branch"], w["b_conv_branch"], w["w_out"], g2, g3],
        [_sds((s, D_MODEL), BF16)] * 3 + [_sds((s, D_MODEL)), _sds((s, D_MODEL)), _sds((s, D_MODEL), BF16)], tm=512)

    def ffn_up_fn(ht, w_up_t):
        gu_ = lax.dot_general(ht, w_up_t, NT, preferred_element_type=F32)
        return (gu_, _swiglu(gu_[:, :D_FF], gu_[:, D_FF:])), ()

    gu, act, g_right = _rowwise("ffn_up", ffn_up_fn, [h2], [w["w_ffn_up"]],
                                [_sds((s, 2 * D_FF), BF16), _sds((s, D_FF), BF16)], tm=512,
                                exchange=_relay_gather_exchange([down_block[:, half:]]))
    w_down = [_full_weight("w_ffn_down", g) for g in (g_left, g_right)]

    def final_fn(at, x2t, tgt, w_left, w_right, g4_):
        ff = jnp.concatenate([jnp.dot(at, w_left, preferred_element_type=F32),
                              jnp.dot(at, w_right, preferred_element_type=F32)], axis=1)
        n4, vjp = jax.vjp(_rms, ff, g4_)
        err = x2t + n4 - tgt
        dy = err * (1.0 / D_MODEL)
        dff, dg4 = vjp(dy)
        return (dy, dff), (jnp.sum(err * err, axis=0, keepdims=True), dg4)

    dy, dff, loss_cols, d_g4 = _rowwise("ffn_down_loss", final_fn, [act, x2, target], [*w_down, g4],
                                        [_sds((s, D_MODEL)), _sds((s, D_MODEL), BF16)],
                                        [_sds((1, D_MODEL)), _sds((1, D_MODEL))], tm=512)
    loss = 0.5 * jnp.sum(loss_cols) / D_MODEL

    d_w_down = _tn_matmul(act, dff, name="d_w_down")

    def act_bwd_fn(dfft, gut, w_left, w_right):
        d_act = (lax.dot_general(dfft[:, :half], w_left, NT, preferred_element_type=F32)
                 + lax.dot_general(dfft[:, half:], w_right, NT, preferred_element_type=F32))
        gu_ = gut.astype(F32)
        _, vjp = jax.vjp(_swiglu, gu_[:, :D_FF], gu_[:, D_FF:])
        return (jnp.concatenate(vjp(d_act), axis=1),), ()

    down_slabs = _side_slabs("w_ffn_down", d_w_down)
    dgu, theirs = _rowwise("ffn_act_bwd", act_bwd_fn, [dff, gu], w_down, [_sds((s, 2 * D_FF), BF16)],
                           exchange=_pair_exchange([down_slabs]))
    down_sums = _pair_sum("pair_sum_w_ffn_down", down_slabs, theirs)
    d_w_up = _tn_matmul(dgu, h2, name="d_w_up")
    received = {}
    up_slabs = _side_slabs("w_ffn_up", d_w_up)

    def mid_bwd_fn(dgut, xt, mt, dyt, w_up_t, g2_, g3_):
        dh = jnp.dot(dgut, w_up_t, preferred_element_type=F32)
        n2, vjp2 = jax.vjp(_rms, mt, g2_)
        x2_ = xt + n2
        _, vjp3 = jax.vjp(_rms, x2_, g3_)
        dx2_, dg3 = vjp3(dh)
        dx2_ = dx2_ + dyt
        dmix_, dg2 = vjp2(dx2_)
        return (dx2_, dmix_), (dg2, dg3)

    dx2, dmix, d_g2, d_g3, received["w_ffn_down"] = _rowwise(
        "ffn_up_mid_bwd", mid_bwd_fn, [dgu, x, mix, dy], [w["w_ffn_up"], g2, g3],
        [_sds((s, D_MODEL)), _sds((s, D_MODEL), BF16)], [_sds((1, D_MODEL)), _sds((1, D_MODEL))], tm=512,
        exchange=_chip_exchange([down_sums]))
    d_w_out = _tn_matmul(merged, dmix, name="d_w_out")

    def merge_bwd_fn(dmt, cp, ao, gc, ga, w_out, w_cb, w_ab, b_cb):
        dm = lax.dot_general(dmt, w_out, NT, preferred_element_type=F32)
        _, vjp = jax.vjp(_merge, cp.astype(F32), ao.astype(F32), gc.astype(F32), ga.astype(F32), b_cb)
        dcp, dao, dgc, dga, dbias = vjp(dm)
        dcp, dao = dcp.astype(BF16), dao.astype(BF16)
        du3_ = lax.dot_general(dcp, w_cb, NT, preferred_element_type=F32)
        datt_ = lax.dot_general(dao, w_ab, NT, preferred_element_type=F32)
        return (dcp, dao, dgc, dga, du3_, datt_), (dbias,)

    d_conv_out, d_att_out, d_g_conv, d_g_att, du3, d_att, d_b_cb, theirs = _rowwise(
        "merge_bwd", merge_bwd_fn, [dmix, conv_pre, att_out, g_conv, g_att],
        [w["w_out"], w["w_conv_branch"], w["w_att_branch"], w["b_conv_branch"]],
        [_sds((s, D_MODEL), BF16)] * 4 + [_sds((s, CONV_DIM)), _sds((s, ATT_DIM), BF16)], [_sds((1, D_MODEL))], tm=512,
        exchange=_pair_exchange([up_slabs]))

    d_w_cb = _tn_matmul(u3, d_conv_out, name="d_w_conv_branch")
    d_w_ab = _tn_matmul(att, d_att_out, name="d_w_att_branch")

    dq, dk, dv, (received["w_ffn_up"],) = _attn_bwd(
        q, k, v, d_att, ltot, n_blocks, _chip_exchange([_pair_sum("pair_sum_w_ffn_up", up_slabs, theirs)]))

    mix_grads = {"w_conv_branch": d_w_cb, "w_att_branch": d_w_ab, "w_out": d_w_out}
    (d_conv_in, d_dw_w, d_dw_b, d_ln_g, d_ln_b), landed = _conv_bwd(
        conv_in, u1, du3, w["conv_ln_g"], w["conv_ln_b"], w["conv_dw_w"],
        _scatter_exchange([_grad_slabs(nm, mix_grads[nm]) for nm in mix_weights[:2]]))
    received.update(zip(mix_weights[:2], landed))

    d_proj = [d_conv_in, dq, dk, dv, d_g_conv, d_g_att]
    d_w_in, (received["w_out"],) = _pieces_tn_matmul(
        d_proj, h1, name="d_w_in", exchange=_scatter_exchange([_grad_slabs("w_out", d_w_out)]))
    in_slabs = _side_slabs("w_in", d_w_in)
    (theirs,) = _exchange_call("pair_swap_w_in", _pair_exchange([in_slabs]))

    early = list(opt)

    def pre_bwd_fn(*args):
        groups, (xt, dx2t), jobs, (w_in_t, g_) = args[:6], args[6:8], args[8:-2], args[-2:]
        dh = sum(jnp.dot(grp.astype(BF16), w_in_t[IN_SPLITS[n]:IN_SPLITS[n + 1]], preferred_element_type=F32)
                 for n, grp in enumerate(groups))
        _, vjp = jax.vjp(_rms, xt, g_)
        dx_, dg_ = vjp(dh)
        updates = [_sum_adamw_tile(*jobs[4 * n:4 * n + 4]) for n in range(len(early))]
        return (dx_ + dx2t, *[u for four in updates for u in four]), (dg_,)

    res = _rowwise(
        "proj_norm_bwd", pre_bwd_fn,
        d_proj + [x, dx2] + [a for nm in early for a in (received[nm], *opt[nm])], [w_in, g1],
        [_sds((s, D_MODEL))] + [_sds(opt[nm][0].shape) for nm in early for _ in range(4)],
        [_sds((1, D_MODEL))], tm=512, exchange=_chip_exchange([_pair_sum("pair_sum_w_in", in_slabs, theirs)]))
    grad_x, d_g1, received["w_in"] = res[0], res[-2], res[-1]
    updated = {nm: res[1 + 4 * n:5 + 4 * n] for n, nm in enumerate(early)}

    grads = {
        "norm_mix_pre": d_g1, "conv_dw_w": d_dw_w, "conv_dw_b": d_dw_b,
        "conv_ln_g": d_ln_g, "conv_ln_b": d_ln_b, "b_conv_branch": d_b_cb,
        "norm_mix_post": d_g2, "norm_ffn_pre": d_g3, "norm_ffn_post": d_g4,
    }
    return loss, grad_x, received, updated, grads


def _place():
    x, y, c = lax.axis_index("x"), lax.axis_index("y"), lax.axis_index("c")
    return x, y, c


def _slot(px, py, pc):
    return 4 * px + 2 * py + pc


def _exchange_scratch(n):
    return [pltpu.SemaphoreType.DMA((7 * n,)), pltpu.SemaphoreType.DMA((7 * n,)), pltpu.SemaphoreType.DMA((n,))]


GATHER_ID, SCATTER_ID, PAIR_ID, CHIP_ID, RELAY_ID = 0, 1, 2, 3, 4


def _handshake(peers):
    def announce():
        for peer in peers:
            pl.semaphore_signal(pltpu.get_barrier_semaphore(), inc=1, device_id=peer, device_id_type=MESH)

    def arrive():
        pl.semaphore_wait(pltpu.get_barrier_semaphore(), len(peers))

    return announce, arrive


def _gather_exchange(arrs):
    n = len(arrs)

    def phases(ins, outs, send_sems, recv_sems, local_sems):
        x, y, c = _place()
        me, sibling = (x, y, c), (x, y, 1 - c)
        chips = [(1 - x, y), (x, 1 - y), (1 - x, 1 - y)]

        def copy(a, kk, block, to, src=None):
            dst = outs[a].at[_slot(*block)]
            return pltpu.make_async_remote_copy(
                src_ref=dst if src is None else src, dst_ref=dst,
                send_sem=send_sems.at[a * 7 + kk], recv_sem=recv_sems.at[a * 7 + kk],
                device_id=to, device_id_type=MESH)

        mine = [pltpu.make_async_copy(ins[a], outs[a].at[_slot(*me)], local_sems.at[a]) for a in range(n)]
        first = []
        for a in range(n):
            first.append(copy(a, 0, me, sibling, src=ins[a]))
            first += [copy(a, 1 + j, me, (*chip, c), src=ins[a]) for j, chip in enumerate(chips)]
        passed = [copy(a, 4 + j, (*chip, c), sibling) for j, chip in enumerate(chips) for a in range(n)]

        announce, arrive = _handshake([sibling] + [(*chip, c) for chip in chips])

        def send():
            arrive()
            for cp in mine + first:
                cp.start()

        def pass_on():
            for j, chip in enumerate(chips):
                for a in range(n):
                    copy(a, 1 + j, (*chip, c), me).wait_recv()
                    passed[j * n + a].start()

        def finish():
            for a in range(n):
                copy(a, 0, sibling, me).wait_recv()
                for j, chip in enumerate(chips):
                    copy(a, 4 + j, (*chip, 1 - c), me).wait_recv()
            for cp in first + passed:
                cp.wait_send()
            for cp in mine:
                cp.wait()

        return [announce, send, pass_on, finish]

    return list(arrs), [_sds((N_DEV,) + a.shape, a.dtype) for a in arrs], _exchange_scratch(n), phases, GATHER_ID


def _relay_gather_exchange(arrs):
    n = len(arrs)
    per = 8

    def phases(ins, outs, send_sems, recv_sems, local_sems):
        x, y, c = _place()
        me, sibling = (x, y, c), (x, y, 1 - c)
        beside, below, across = (1 - x, y, c), (x, 1 - y, c), (1 - x, 1 - y, c)

        def copy(a, kk, block, to, src=None, rows=None):
            where = _slot(*block) if rows is None else (_slot(*block), rows)
            dst = outs[a].at[where]
            return pltpu.make_async_remote_copy(
                src_ref=dst if src is None else src, dst_ref=dst,
                send_sem=send_sems.at[a * per + kk], recv_sem=recv_sems.at[a * per + kk],
                device_id=to, device_id_type=MESH)

        def halves(a):
            h = ins[a].shape[0] // 2
            return pl.ds(0, h), pl.ds(h, ins[a].shape[0] - h)

        mine = [pltpu.make_async_copy(ins[a], outs[a].at[_slot(*me)], local_sems.at[a]) for a in range(n)]
        first = [copy(a, kk, me, to, src=ins[a]) for a in range(n) for kk, to in enumerate([sibling, beside, below])]
        relayed = [[copy(a, 3, beside, sibling), copy(a, 5, beside, below, rows=halves(a)[0])] for a in range(n)]
        relayed += [[copy(a, 4, below, sibling), copy(a, 6, below, beside, rows=halves(a)[1])] for a in range(n)]
        passed = [copy(a, 7, across, sibling) for a in range(n)]

        announce, arrive = _handshake([sibling, beside, below])

        def send():
            arrive()
            for cp in mine + first:
                cp.start()

        def relay():
            for kk, block in ((1, beside), (2, below)):
                for a in range(n):
                    copy(a, kk, block, me).wait_recv()
                    for cp in relayed[(kk - 1) * n + a]:
                        cp.start()

        def pass_on():
            for a in range(n):
                copy(a, 5, across, me, rows=halves(a)[0]).wait_recv()
                copy(a, 6, across, me, rows=halves(a)[1]).wait_recv()
                passed[a].start()

        def finish():
            for a in range(n):
                for kk, block in ((0, me), (3, beside), (4, below), (7, across)):
                    copy(a, kk, (*block[:2], 1 - c), me).wait_recv()
            for cp in first + [cp for two in relayed for cp in two] + passed:
                cp.wait_send()
            for cp in mine:
                cp.wait()

        return [announce, send, relay, pass_on, finish]

    scratch = [pltpu.SemaphoreType.DMA((per * n,)), pltpu.SemaphoreType.DMA((per * n,)), pltpu.SemaphoreType.DMA((n,))]
    return list(arrs), [_sds((N_DEV,) + a.shape, a.dtype) for a in arrs], scratch, phases, RELAY_ID


def _scatter_exchange(arrs):
    n = len(arrs)
    flips = [(fx, fy, fc) for fx in (0, 1) for fy in (0, 1) for fc in (0, 1)][1:]

    def phases(ins, outs, send_sems, recv_sems, local_sems):
        x, y, c = _place()
        mine = _slot(x, y, c)
        local = [pltpu.make_async_copy(ins[a].at[mine], outs[a].at[mine], local_sems.at[a]) for a in range(n)]
        peers = [((1 - x) if fx else x, (1 - y) if fy else y, (1 - c) if fc else c) for fx, fy, fc in flips]

        def copy(a, kk, src_slot, dst_slot):
            return pltpu.make_async_remote_copy(
                src_ref=ins[a].at[src_slot], dst_ref=outs[a].at[dst_slot],
                send_sem=send_sems.at[a * 7 + kk], recv_sem=recv_sems.at[a * 7 + kk],
                device_id=peers[kk], device_id_type=MESH)

        sends = [copy(a, kk, _slot(*peers[kk]), mine) for a in range(n) for kk in range(7)]

        announce, arrive = _handshake(peers)

        def send():
            arrive()
            for cp in local + sends:
                cp.start()

        def finish():
            for a in range(n):
                for kk in range(7):
                    copy(a, kk, mine, _slot(*peers[kk])).wait_recv()
            for cp in sends:
                cp.wait_send()
            for cp in local:
                cp.wait()

        return [announce, send, finish]

    return list(arrs), [_sds(a.shape, a.dtype) for a in arrs], _exchange_scratch(n), phases, SCATTER_ID


def _pair_exchange(arrs):
    n = len(arrs)

    def phases(ins, outs, send_sems, recv_sems, local_sems):
        x, y, c = _place()

        def copy(a, chip, side):
            return pltpu.make_async_remote_copy(
                src_ref=ins[a].at[chip, side], dst_ref=outs[a].at[chip],
                send_sem=send_sems.at[a * 7 + chip], recv_sem=recv_sems.at[a * 7 + chip],
                device_id=(x, y, 1 - c), device_id_type=MESH)

        sends = [copy(a, chip, 1 - c) for a in range(n) for chip in range(4)]

        announce, arrive = _handshake([(x, y, 1 - c)])

        def send():
            arrive()
            for cp in sends:
                cp.start()

        def finish():
            for a in range(n):
                for chip in range(4):
                    copy(a, chip, c).wait_recv()
            for cp in sends:
                cp.wait_send()

        return [announce, send, finish]

    return list(arrs), [_sds((4,) + a.shape[2:], a.dtype) for a in arrs], _exchange_scratch(n), phases, PAIR_ID


def _chip_exchange(arrs):
    n = len(arrs)

    def phases(ins, outs, send_sems, recv_sems, local_sems):
        x, y, c = _place()
        mine = 2 * x + y
        chips = [(1 - x, y), (x, 1 - y), (1 - x, 1 - y)]
        local = [pltpu.make_async_copy(ins[a].at[mine], outs[a].at[mine], local_sems.at[a]) for a in range(n)]

        def copy(a, j, src_slot, dst_slot):
            return pltpu.make_async_remote_copy(
                src_ref=ins[a].at[src_slot], dst_ref=outs[a].at[dst_slot],
                send_sem=send_sems.at[a * 7 + j], recv_sem=recv_sems.at[a * 7 + j],
                device_id=(*chips[j], c), device_id_type=MESH)

        sends = [copy(a, j, 2 * chips[j][0] + chips[j][1], mine) for a in range(n) for j in range(3)]

        announce, arrive = _handshake([(*chip, c) for chip in chips])

        def send():
            announce()
            arrive()
            for cp in local + sends:
                cp.start()

        def finish():
            for a in range(n):
                for j in range(3):
                    copy(a, j, mine, 2 * chips[j][0] + chips[j][1]).wait_recv()
            for cp in sends:
                cp.wait_send()
            for cp in local:
                cp.wait()

        return [send, lambda: None, finish]

    return list(arrs), [_sds(a.shape, a.dtype) for a in arrs], _exchange_scratch(n), phases, CHIP_ID


def _pair_sum(name, mine, theirs):
    _, _, r, c = mine.shape

    def body(side_ref, m_ref, t_ref, o_ref):
        o_ref[...] = (m_ref[...].astype(F32) + t_ref[...].astype(F32)).astype(o_ref.dtype)

    return pl.pallas_call(
        body, name=name,
        grid_spec=pltpu.PrefetchScalarGridSpec(
            num_scalar_prefetch=1, grid=(4,),
            in_specs=[pl.BlockSpec((None, None, r, c), lambda j, side: (j, side[0], 0, 0)),
                      pl.BlockSpec((None, r, c), lambda j, side: (j, 0, 0))],
            out_specs=pl.BlockSpec((None, r, c), lambda j, side: (j, 0, 0))),
        out_shape=_sds(theirs.shape, theirs.dtype),
        compiler_params=_params(("parallel",)),
    )(lax.axis_index("c").astype(jnp.int32).reshape(1), mine, theirs)


def _exchange_call(name, exchange):
    arrs, out_shape, scratch, phases, collective_id = exchange
    n = len(arrs)

    def body(*refs):
        for step in phases(refs[:n], refs[n:2 * n], *refs[2 * n:]):
            step()

    return pl.pallas_call(body, name=name, in_specs=[ANY] * n, out_specs=[ANY] * n,
                          out_shape=out_shape, scratch_shapes=scratch,
                          compiler_params=pltpu.CompilerParams(collective_id=collective_id))(*arrs)


def _carry_exchange(exchange, refs, n_in, n_out, first, middle, last, halfway, at_once=False):
    arrs, _, _, phases, _ = exchange
    n = len(arrs)
    if n == 0:
        return lambda: None
    ins = refs[n_in:n_in + n]
    outs = refs[n_in + n + n_out:n_in + 2 * n + n_out]
    sems = n_in + 2 * n + n_out
    steps = phases(ins, outs, *refs[sems:sems + 3])
    pl.when(first)(steps[0])
    if at_once:
        pl.when(first)(steps[1])

    def close():
        if not at_once:
            pl.when(first)(steps[1])
        if len(steps) == 5:
            pl.when(halfway)(steps[2])
        if len(steps) >= 4:
            pl.when(middle)(steps[-2])
        pl.when(last)(steps[-1])

    return close


def _adamw_math(w, g, m, v):
    m2 = ADAM_B1 * m + (1.0 - ADAM_B1) * g
    v2 = ADAM_B2 * v + (1.0 - ADAM_B2) * jnp.square(g)
    m_hat = m2 / (1.0 - ADAM_B1 ** ADAM_STEP)
    v_hat = v2 / (1.0 - ADAM_B2 ** ADAM_STEP)
    delta = -ADAM_LR * (m_hat / (jnp.sqrt(v_hat) + ADAM_EPS) + ADAM_WD * w)
    return delta, m2, v2


def _sum_adamw_tile(parts, w, m, v):
    g = parts[0].astype(F32)
    for d in range(1, parts.shape[0]):
        g = g + parts[d].astype(F32)
    return (g, *_adamw_math(w, g, m, v))


def _sum_adamw(name, parts, w, m, v, tr=256):
    p, r, c = parts.shape
    tr = _pick(r, tr, 16)

    def body(p_ref, w_ref, m_ref, v_ref, g_ref, d_ref, m2_ref, v2_ref):
        g_ref[...], d_ref[...], m2_ref[...], v2_ref[...] = _sum_adamw_tile(p_ref[...], w_ref[...], m_ref[...], v_ref[...])

    tile = pl.BlockSpec((tr, c), lambda i: (i, 0))
    return pl.pallas_call(
        body, name=name, grid=(r // tr,),
        in_specs=[pl.BlockSpec((p, tr, c), lambda i: (0, i, 0)), tile, tile, tile],
        out_specs=[tile] * 4, out_shape=[_sds((r, c))] * 4,
        compiler_params=_params(("parallel",)),
    )(parts, w, m, v)


def _sum_parts(name, parts):
    p, r, c = parts.shape

    def body(p_ref, o_ref):
        g = p_ref[0]
        for d in range(1, p):
            g = g + p_ref[d]
        o_ref[...] = g

    return pl.pallas_call(
        body, name=name, out_shape=_sds((r, c)),
        in_specs=[pl.BlockSpec(memory_space=pltpu.VMEM)], out_specs=pl.BlockSpec(memory_space=pltpu.VMEM),
    )(parts)


WEIGHTS = ["norm_mix_pre", "w_in", "conv_dw_w", "conv_dw_b", "conv_ln_g", "conv_ln_b", "w_conv_branch",
           "b_conv_branch", "w_att_branch", "w_out", "norm_mix_post", "norm_ffn_pre", "w_ffn_up", "w_ffn_down",
           "norm_ffn_post"]
COL_SHARDED = ["w_conv_branch", "w_att_branch"]
TRANSPOSED = ["w_in", "w_ffn_up"]
VECTORS = ["norm_mix_pre", "conv_dw_b", "conv_ln_g", "conv_ln_b", "b_conv_branch", "norm_mix_post",
           "norm_ffn_pre", "norm_ffn_post"]


def _cols_to_full(g):
    return g.transpose(1, 0, 2).reshape(g.shape[1], N_DEV * g.shape[2])


def _full_to_cols(f):
    return f.reshape(f.shape[0], N_DEV, f.shape[1] // N_DEV).transpose(1, 0, 2)


PACK_ROWS = 7


def _pack_vectors(vecs, extra=None):
    parts = [vecs[nm].reshape(-1) for nm in VECTORS]
    parts.append(jnp.zeros((1,), F32) if extra is None else extra.reshape(1))
    used = sum(p.size for p in parts)
    parts.append(jnp.zeros((PACK_ROWS * D_MODEL - used,), F32))
    return jnp.concatenate(parts).reshape(PACK_ROWS, D_MODEL)


def _unpack_vectors(packed, sizes):
    flat, out, at = packed.reshape(-1), {}, 0
    for nm in VECTORS:
        out[nm] = flat[at:at + sizes[nm]]
        at += sizes[nm]
    return out, flat[at]


def kernel(x, norm_mix_pre, w_in, conv_dw_w, conv_dw_b, conv_ln_g, conv_ln_b, w_conv_branch, b_conv_branch, w_att_branch, w_out, norm_mix_post, norm_ffn_pre, w_ffn_up, w_ffn_down, norm_ffn_post, loss_target, m_norm_mix_pre, m_w_in, m_conv_dw_w, m_conv_dw_b, m_conv_ln_g, m_conv_ln_b, m_w_conv_branch, m_b_conv_branch, m_w_att_branch, m_w_out, m_norm_mix_post, m_norm_ffn_pre, m_w_ffn_up, m_w_ffn_down, m_norm_ffn_post, v_norm_mix_pre, v_w_in, v_conv_dw_w, v_conv_dw_b, v_conv_ln_g, v_conv_ln_b, v_w_conv_branch, v_b_conv_branch, v_w_att_branch, v_w_out, v_norm_mix_post, v_norm_ffn_pre, v_w_ffn_up, v_w_ffn_down, v_norm_ffn_post):
    ws = dict(zip(WEIGHTS, [norm_mix_pre, w_in, conv_dw_w, conv_dw_b, conv_ln_g, conv_ln_b, w_conv_branch,
                            b_conv_branch, w_att_branch, w_out, norm_mix_post, norm_ffn_pre, w_ffn_up, w_ffn_down,
                            norm_ffn_post]))
    ms = dict(zip(WEIGHTS, [m_norm_mix_pre, m_w_in, m_conv_dw_w, m_conv_dw_b, m_conv_ln_g, m_conv_ln_b,
                            m_w_conv_branch, m_b_conv_branch, m_w_att_branch, m_w_out, m_norm_mix_post,
                            m_norm_ffn_pre, m_w_ffn_up, m_w_ffn_down, m_norm_ffn_post]))
    vs = dict(zip(WEIGHTS, [v_norm_mix_pre, v_w_in, v_conv_dw_w, v_conv_dw_b, v_conv_ln_g, v_conv_ln_b,
                            v_w_conv_branch, v_b_conv_branch, v_w_att_branch, v_w_out, v_norm_mix_post,
                            v_norm_ffn_pre, v_w_ffn_up, v_w_ffn_down, v_norm_ffn_post]))

    dw_block = jnp.pad(conv_dw_w, ((0, 1), (0, 0)))
    g_in, g_dw = _exchange_call("gather_first", _relay_gather_exchange([w_in.T.astype(BF16), dw_block]))
    full = {"w_in": _full_weight("w_in", g_in), "conv_dw_w": _cols_to_full(g_dw)}
    for nm in VECTORS:
        full[nm] = ws[nm].reshape(1, -1)

    def as_kept(nm, a):
        return a.T if nm in TRANSPOSED else a

    ride_along = ["w_ffn_up", "w_out"]
    loss_local, grad_x, received, updated, grads = _local_step(
        x[0], loss_target[0], full, {nm: as_kept(nm, ws[nm]).astype(BF16) for nm in LATE},
        {nm: tuple(as_kept(nm, a[nm]) for a in (ws, ms, vs)) for nm in ride_along})

    small = _exchange_call("gather_small_grads", _gather_exchange(
        [_pack_vectors(grads, extra=loss_local), grads["conv_dw_w"]]))
    out_g, out_d, out_m, out_v = {}, {}, {}, {}
    for nm in LATE + ["w_in"]:
        res = updated[nm] if nm in updated else _sum_adamw(
            "adamw_" + nm, received[nm], *[as_kept(nm, a[nm]) for a in (ws, ms, vs)])
        out_g[nm], out_d[nm], out_m[nm], out_v[nm] = [as_kept(nm, r) for r in res]
    sizes = {nm: ws[nm].size for nm in VECTORS}
    vec = _sum_adamw("adamw_vectors", small[0], _pack_vectors(ws), _pack_vectors(ms), _pack_vectors(vs))
    for res, dst in zip(vec, (out_g, out_d, out_m, out_v)):
        dst.update(_unpack_vectors(res, sizes)[0])
    loss = _unpack_vectors(vec[0], sizes)[1]
    dw_full = _sum_parts("sum_dw_grads", small[1])
    me = _slot(*_place())
    dw_mine = lax.dynamic_slice(dw_full, (0, me * (CONV_DIM // N_DEV)), (CONV_WIDTH, CONV_DIM // N_DEV))
    nm = "conv_dw_w"
    out_g[nm], out_d[nm], out_m[nm], out_v[nm] = _sum_adamw("adamw_dw", dw_mine[None], ws[nm], ms[nm], vs[nm])

    outs = [loss, grad_x[None]]
    for group in (out_g, out_d, out_m, out_v):
        outs += [group[nm] for nm in WEIGHTS]
    return tuple(outs)
```

```python
import math

import jax
import jax.numpy as jnp
from jax import lax
from jax.experimental import pallas as pl
from jax.experimental.pallas import tpu as pltpu

F32 = jnp.float32
BF16 = jnp.bfloat16

N_DEV = 8
D_MODEL = 1024
CONV_DIM = 512
CONV_WIDTH = 31
N_HEADS = 8
HEAD_DIM = 64
ATT_DIM = N_HEADS * HEAD_DIM
D_FF = 2816
EPS = 1e-6
IN_SPLITS = (0, 1024, 1536, 2048, 2560, 3584, 4608)

ADAM_LR = 0.001
ADAM_B1 = 0.9
ADAM_B2 = 0.999
ADAM_EPS = 1e-08
ADAM_WD = 0.01
ADAM_STEP = 10

LANES = 128
SUBLANES = 8
HALO = 32
ATT_TILE = 256
ATT_PART = 176
DEAD_SUM = -120.0
VMEM_LIMIT = 56 * 1024 * 1024
MESH = pl.DeviceIdType.MESH
ANY = pl.BlockSpec(memory_space=pl.ANY)


def _pick(dim, target, align=LANES):
    t = min(dim, target)
    t -= t % align
    while t >= align:
        if dim % t == 0:
            return t
        t -= align
    return dim


def _params(semantics, collective_id=None):
    return pltpu.CompilerParams(dimension_semantics=semantics, vmem_limit_bytes=VMEM_LIMIT,
                                collective_id=collective_id)


def _tn_matmul(a, b, *, name):
    return _pieces_tn_matmul([a], b, name=name, tj=_pick(a.shape[1], 1408))


def _pieces_tn_matmul(pieces, b, *, name, tj=512, exchange=None):
    s, n = b.shape
    counts = [p.shape[1] // tj for p in pieces]
    starts = [sum(counts[:i]) for i in range(len(pieces))]
    assert all(p.shape == (s, c * tj) for p, c in zip(pieces, counts))
    x_arrs, x_shape, x_scratch, _, x_id = exchange or NO_EXCHANGE
    nx, n_in = len(x_arrs), len(pieces) + 1

    def body(*refs):
        b_ref, o_ref = refs[n_in - 1], refs[n_in + nx]
        finish_exchange = _carry_exchange(exchange or NO_EXCHANGE, refs, n_in, 1, *_sweep_marks(sum(counts)))
        j = pl.program_id(0)
        for p_ref, first, count in zip(refs, starts, counts):
            @pl.when((j >= first) & (j < first + count))
            def _():
                o_ref[...] = lax.dot_general(p_ref[...].astype(BF16), b_ref[...], TN,
                                             preferred_element_type=F32).astype(o_ref.dtype)
        finish_exchange()

    def piece_spec(first, count):
        return pl.BlockSpec((s, tj), lambda j: (0, jnp.clip(j - first, 0, count - 1)))

    res = pl.pallas_call(
        body, name=name, grid=(sum(counts),),
        in_specs=[piece_spec(f, c) for f, c in zip(starts, counts)]
        + [pl.BlockSpec((s, n), lambda j: (0, 0), pipeline_mode=pl.Buffered(1))] + [ANY] * nx,
        out_specs=[pl.BlockSpec((tj, n), lambda j: (j, 0))] + [ANY] * nx,
        out_shape=[jax.ShapeDtypeStruct((sum(counts) * tj, n), BF16)] + x_shape, scratch_shapes=x_scratch,
        compiler_params=_params(("arbitrary",), x_id),
    )(*pieces, b, *x_arrs)
    return res[0] if exchange is None else (res[0], res[1:])


NO_EXCHANGE = ([], [], [], None, None)


def _sweep_marks(nt):
    i = pl.program_id(0)
    return i == 0, i == max(nt - 2, 0), i == nt - 1, i == max(nt // 2 - 1, 0)


def _rowwise(name, fn, rows, bcasts, row_outs, red_outs=(), tm=256, exchange=NO_EXCHANGE):
    s = rows[0].shape[0]
    tm = _pick(s, tm, 16)
    nt = s // tm
    resident = pl.Buffered(1)
    nr, nb, no, nd = len(rows), len(bcasts), len(row_outs), len(red_outs)
    x_arrs, x_shape, x_scratch, _, x_id = exchange
    nx = len(x_arrs)
    first_out = nr + nb + nx

    def body(*refs):
        finish_exchange = _carry_exchange(exchange, refs, nr + nb, no + nd, *_sweep_marks(nt), at_once=True)
        ins = [r[...] for r in refs[:nr + nb]]
        outs, reds = fn(*ins)
        for ref, val in zip(refs[first_out:first_out + no], outs):
            ref[...] = val.astype(ref.dtype)
        i = pl.program_id(0)
        for ref, val in zip(refs[first_out + no:first_out + no + nd], reds):
            @pl.when(i == 0)
            def _():
                ref[...] = val

            @pl.when(i > 0)
            def _():
                ref[...] += val
        finish_exchange()

    def row_spec(a):
        assert a.shape[-2] % nt == 0, (name, a.shape, nt)
        if len(a.shape) == 3:
            return pl.BlockSpec((a.shape[0], a.shape[1] // nt, a.shape[2]), lambda i: (0, i, 0))
        return pl.BlockSpec((a.shape[0] // nt, a.shape[1]), lambda i: (i, 0))

    in_specs = [row_spec(r) for r in rows]
    in_specs += [pl.BlockSpec(b.shape, lambda i: (0, 0), pipeline_mode=resident) for b in bcasts]
    out_specs = [row_spec(o) for o in row_outs]
    out_specs += [pl.BlockSpec(d.shape, lambda i: (0, 0)) for d in red_outs]
    return pl.pallas_call(
        body, name=name, grid=(nt,), in_specs=in_specs + [ANY] * nx, out_specs=out_specs + [ANY] * nx,
        out_shape=list(row_outs) + list(red_outs) + x_shape, scratch_shapes=x_scratch,
        compiler_params=_params(("arbitrary",), x_id),
    )(*rows, *bcasts, *x_arrs)


def _sds(shape, dtype=F32):
    return jax.ShapeDtypeStruct(shape, dtype)


def _rms(x, g):
    y = x * lax.rsqrt(jnp.mean(x * x, axis=-1, keepdims=True) + EPS)
    return y * g


def _silu(x):
    return x * jax.nn.sigmoid(x)


def _swiglu(g, u):
    return _silu(g) * u


def _ln_silu(u, g, b):
    mu = jnp.mean(u, axis=-1, keepdims=True)
    var = jnp.mean(jnp.square(u - mu), axis=-1, keepdims=True)
    return _silu((u - mu) * lax.rsqrt(var + EPS) * g + b)


def _merge(conv_pre, att_out, g_conv, g_att, b_cb):
    return jax.nn.sigmoid(g_conv) * (conv_pre + b_cb) + jax.nn.sigmoid(g_att) * att_out


def _glu(t):
    return t[:, :CONV_DIM] * jax.nn.sigmoid(t[:, CONV_DIM:])


def _shifted_reader(buf, shifted, tm):
    for b in range(1, SUBLANES):
        shifted[b - 1, :, :] = buf[pl.ds(b, tm + HALO - SUBLANES), :]

    def read(o):
        a, b = divmod(o, SUBLANES)
        return buf[pl.ds(SUBLANES * a, tm), :] if b == 0 else shifted[b - 1, pl.ds(SUBLANES * a, tm), :]

    return read


def _conv_fwd(conv_in, w_pad, b, ln_g, ln_b, exchange, tm=256):
    s = conv_in.shape[0]
    tm = _pick(s, tm, HALO)
    ratio = tm // HALO
    x_arrs, x_shape, x_scratch, _, x_id = exchange
    nx = len(x_arrs)

    def body(*refs):
        main_ref, halo_ref, w_ref, b_ref, g_ref, be_ref = refs[:6]
        u3_ref, u1_ref = refs[6 + nx:8 + nx]
        buf, shifted = refs[-2:]
        finish_exchange = _carry_exchange(exchange, refs, 6, 2, *_sweep_marks(s // tm), at_once=True)
        i = pl.program_id(0)
        buf[0:HALO, :] = _glu(halo_ref[...]) * (i > 0).astype(F32)
        buf[HALO:HALO + tm, :] = _glu(main_ref[...])
        read = _shifted_reader(buf, shifted, tm)
        acc = jnp.zeros((tm, CONV_DIM), F32) + b_ref[...]
        for j in range(CONV_WIDTH):
            acc = acc + w_ref[j:j + 1, :] * read(HALO - (CONV_WIDTH - 1) + j)
        u1_ref[...] = acc
        u3_ref[...] = _ln_silu(acc, g_ref[...], be_ref[...]).astype(u3_ref.dtype)
        finish_exchange()

    res = pl.pallas_call(
        body, name="conv_fwd", grid=(s // tm,),
        in_specs=[pl.BlockSpec((tm, 2 * CONV_DIM), lambda i: (i, 0)),
                  pl.BlockSpec((HALO, 2 * CONV_DIM), lambda i: (jnp.maximum(i * ratio - 1, 0), 0)),
                  pl.BlockSpec(w_pad.shape, lambda i: (0, 0)),
                  pl.BlockSpec(b.shape, lambda i: (0, 0)),
                  pl.BlockSpec(ln_g.shape, lambda i: (0, 0)),
                  pl.BlockSpec(ln_b.shape, lambda i: (0, 0))] + [ANY] * nx,
        out_specs=[pl.BlockSpec((tm, CONV_DIM), lambda i: (i, 0)),
                   pl.BlockSpec((tm, CONV_DIM), lambda i: (i, 0))] + [ANY] * nx,
        out_shape=[_sds((s, CONV_DIM), BF16), _sds((s, CONV_DIM), F32)] + x_shape,
        scratch_shapes=x_scratch + [pltpu.VMEM((tm + HALO, CONV_DIM), F32),
                                    pltpu.VMEM((SUBLANES - 1, tm + HALO - SUBLANES, CONV_DIM), F32)],
        compiler_params=_params(("arbitrary",), x_id),
    )(conv_in, conv_in, w_pad, b, ln_g, ln_b, *x_arrs)
    return res[0], res[1], res[2:]


def _conv_bwd(conv_in, u1, du3, ln_g, ln_b, w_pad, exchange, tm=256):
    s = conv_in.shape[0]
    tm = _pick(s, tm, HALO)
    ratio = tm // HALO
    nt = s // tm
    last_halo = s // HALO - 1
    x_arrs, x_shape, x_scratch, _, x_id = exchange
    nx = len(x_arrs)

    def body(*refs):
        main_ref, halo_ref, u1_ref, u1n_ref, du3_ref, du3n_ref, g_ref, be_ref, w_ref = refs[:9]
        dci_ref, dw_ref, db_ref, dg_ref, dbe_ref = refs[9 + nx:14 + nx]
        ubuf, dbuf, ushift, dshift = refs[-4:]
        finish_exchange = _carry_exchange(exchange, refs, 9, 5, *_sweep_marks(nt))
        i = pl.program_id(0)
        main = main_ref[...]
        a = main[:, :CONV_DIM]
        sb = jax.nn.sigmoid(main[:, CONV_DIM:])
        ubuf[0:HALO, :] = _glu(halo_ref[...]) * (i > 0).astype(F32)
        ubuf[HALO:HALO + tm, :] = a * sb

        def ln_bwd(u1t, du3t):
            _, vjp = jax.vjp(_ln_silu, u1t, g_ref[...], be_ref[...])
            return vjp(du3t)

        du, dg, dbe = ln_bwd(u1_ref[...], du3_ref[...])
        dbuf[0:tm, :] = du
        dbuf[tm:tm + HALO, :] = ln_bwd(u1n_ref[...], du3n_ref[...])[0] * (i < nt - 1).astype(F32)

        @pl.when(i == 0)
        def _():
            dw_ref[...] = jnp.zeros_like(dw_ref)
            db_ref[...] = jnp.zeros_like(db_ref)
            dg_ref[...] = jnp.zeros_like(dg_ref)
            dbe_ref[...] = jnp.zeros_like(dbe_ref)

        dg_ref[...] += dg
        dbe_ref[...] += dbe

        read_u = _shifted_reader(ubuf, ushift, tm)
        read_d = _shifted_reader(dbuf, dshift, tm)
        du0 = jnp.zeros((tm, CONV_DIM), F32)
        for j in range(CONV_WIDTH):
            du0 = du0 + w_ref[j:j + 1, :] * read_d(CONV_WIDTH - 1 - j)
            dw_ref[j:j + 1, :] += jnp.sum(du * read_u(HALO - (CONV_WIDTH - 1) + j), axis=0, keepdims=True)
        db_ref[...] += jnp.sum(du, axis=0, keepdims=True)
        dci_ref[:, :CONV_DIM] = (du0 * sb).astype(dci_ref.dtype)
        dci_ref[:, CONV_DIM:] = (du0 * a * sb * (1.0 - sb)).astype(dci_ref.dtype)
        finish_exchange()

    res = pl.pallas_call(
        body, name="conv_bwd", grid=(nt,),
        in_specs=[pl.BlockSpec((tm, 2 * CONV_DIM), lambda i: (i, 0)),
                  pl.BlockSpec((HALO, 2 * CONV_DIM), lambda i: (jnp.maximum(i * ratio - 1, 0), 0))]
        + [pl.BlockSpec((tm, CONV_DIM), lambda i: (i, 0)),
           pl.BlockSpec((HALO, CONV_DIM), lambda i: (jnp.minimum((i + 1) * ratio, last_halo), 0))] * 2
        + [pl.BlockSpec((1, CONV_DIM), lambda i: (0, 0))] * 2 + [pl.BlockSpec(w_pad.shape, lambda i: (0, 0))]
        + [ANY] * nx,
        out_specs=[pl.BlockSpec((tm, 2 * CONV_DIM), lambda i: (i, 0)),
                   pl.BlockSpec(w_pad.shape, lambda i: (0, 0))]
        + [pl.BlockSpec((1, CONV_DIM), lambda i: (0, 0))] * 3 + [ANY] * nx,
        out_shape=[_sds((s, 2 * CONV_DIM), BF16), _sds(w_pad.shape)] + [_sds((1, CONV_DIM))] * 3 + x_shape,
        scratch_shapes=x_scratch + [pltpu.VMEM((tm + HALO, CONV_DIM), F32)] * 2
        + [pltpu.VMEM((SUBLANES - 1, tm + HALO - SUBLANES, CONV_DIM), F32)] * 2,
        compiler_params=_params(("arbitrary",), x_id),
    )(conv_in, conv_in, u1, u1, du3, du3, ln_g, ln_b, w_pad, *x_arrs)
    return res[:5], res[5:]


def _logsig_neg(z):
    return jnp.minimum(-z, 0.0) - jnp.log(1.0 + jnp.exp(-jnp.abs(z)))


def _split_dot(val, tri):
    hi = val.astype(BF16)
    lo = (val - hi.astype(F32)).astype(BF16)
    return jnp.dot(hi, tri, preferred_element_type=F32) + jnp.dot(lo, tri, preferred_element_type=F32)


def _attn_masks(t, later):
    row = lax.broadcasted_iota(jnp.int32, (t, t), 0)
    col = lax.broadcasted_iota(jnp.int32, (t, t), 1)
    tri = jnp.where(row > col if later else row <= col, 1.0, 0.0).astype(BF16)
    return col < row, tri


def _grid_marks(h, nq):
    hh, i = pl.program_id(0), pl.program_id(1)
    return ((hh == 0) & (i == 0), (hh == h - 1) & (i == nq // 2), (hh == h - 1) & (i == nq - 1),
            (hh == h // 2) & (i == nq // 2))


def _head_masks(shape):
    lane = lax.broadcasted_iota(jnp.int32, shape, len(shape) - 1)
    return lane < HEAD_DIM, lane >= HEAD_DIM


def _per_head(blk):
    m0, m1 = _head_masks(blk.shape)
    zero = jnp.zeros_like(blk)
    return jnp.where(m0, blk, zero), jnp.where(m1, blk, zero)


NT = (((1,), (1,)), ((), ()))
TN = (((0,), (0,)), ((), ()))


def _with_top(whole, top):
    rows = top.shape[0]
    return top if rows == whole.shape[0] else jnp.concatenate([top, whole[rows:]], axis=0)


def _attn_fwd(q, k, v, exchange):
    s = q.shape[0]
    hp = q.shape[1] // LANES
    t = ATT_TILE
    scale = 1.0 / math.sqrt(HEAD_DIM)
    x_arrs, x_shape, x_scratch, _, x_id = exchange
    nx = len(x_arrs)

    def body(*refs):
        q_ref, k_ref, v_ref = refs[:3]
        o_ref, lt_ref, nb_ref = refs[3 + nx:6 + nx]
        finish_exchange = _carry_exchange(exchange, refs, 3, 3, *_grid_marks(hp, s // t))
        i = pl.program_id(1)
        qs = _per_head((q_ref[...].astype(F32) * scale).astype(BF16))
        causal, tri = _attn_masks(t, later=True)

        def step(kb, carry, masked, rows):
            cs, acc = carry
            off = pl.multiple_of(kb * t, t)
            kblk = k_ref[pl.ds(off, t), :]
            vs = _per_head(v_ref[pl.ds(off, t), :])
            acc_top = acc[:rows]
            new_cs = []
            for hd in range(2):
                z = lax.dot_general(qs[hd][:rows], kblk, NT, preferred_element_type=F32)
                l = _logsig_neg(z)
                if masked:
                    l = jnp.where(causal, l, 0.0)
                e = z + l + _split_dot(l, tri) + cs[hd][:rows]
                if masked:
                    e = jnp.where(causal, e, -1e30)
                acc_top = acc_top + jnp.dot(jnp.exp(e).astype(BF16), vs[hd], preferred_element_type=F32)
                new_cs.append(_with_top(cs[hd], cs[hd][:rows] + jnp.sum(l, axis=1, keepdims=True)))
            return tuple(new_cs), _with_top(acc, acc_top)

        zero = jnp.zeros((t, 1), F32)
        carry = step(i, ((zero, zero), jnp.zeros((t, LANES), F32)), True, t)

        carry = lax.fori_loop(0, i, lambda n, cr: step(i - 1 - n, cr, False, t), carry)
        n_blocks = n_full = i
        m0, _ = _head_masks((t, LANES))
        lt_ref[...] = jnp.where(m0, carry[0][0], carry[0][1])
        o_ref[...] = carry[1].astype(o_ref.dtype)
        nb_ref[0, pl.program_id(0), i] = n_blocks.astype(F32)
        nb_ref[1, pl.program_id(0), i] = n_full.astype(F32)
        finish_exchange()

    res = pl.pallas_call(
        body, name="attn_fwd", grid=(hp, s // t),
        in_specs=[pl.BlockSpec((t, LANES), lambda p, i: (i, p)),
                  pl.BlockSpec((s, LANES), lambda p, i: (0, p)),
                  pl.BlockSpec((s, LANES), lambda p, i: (0, p))] + [ANY] * nx,
        out_specs=[pl.BlockSpec((t, LANES), lambda p, i: (i, p)),
                   pl.BlockSpec((None, t, LANES), lambda p, i: (p, i, 0)),
                   pl.BlockSpec(memory_space=pltpu.SMEM)] + [ANY] * nx,
        out_shape=[_sds(q.shape, BF16), _sds((hp, s, LANES), F32), _sds((2, hp, s // t), F32)] + x_shape,
        scratch_shapes=x_scratch,
        compiler_params=_params(("arbitrary", "arbitrary"), x_id),
    )(q, k, v, *x_arrs)
    return res[0], res[1], res[2], res[3:]


def _attn_bwd(q, k, v, do, ltot, n_blocks, exchange):
    s = q.shape[0]
    hp = q.shape[1] // LANES
    t = ATT_TILE
    scale = 1.0 / math.sqrt(HEAD_DIM)
    x_arrs, x_shape, x_scratch, _, x_id = exchange
    nx = len(x_arrs)

    def body(*refs):
        q_ref, k_ref, v_ref, do_ref, lt_ref, nb_ref = refs[:6]
        dq_ref, dk_ref, dv_ref = refs[6 + nx:9 + nx]
        finish_exchange = _carry_exchange(exchange, refs, 6, 3, *_grid_marks(hp, s // t))
        i = pl.program_id(1)
        n_blocks = n_full = i

        @pl.when(i == 0)
        def _():
            dk_ref[...] = jnp.zeros_like(dk_ref)
            dv_ref[...] = jnp.zeros_like(dv_ref)

        qb = q_ref[...]
        qm = _per_head(qb)
        qs = _per_head((qb.astype(F32) * scale).astype(BF16))
        dos = _per_head(do_ref[...])
        lts = (lt_ref[:, 0:1], lt_ref[:, HEAD_DIM:HEAD_DIM + 1])
        causal, tri = _attn_masks(t, later=False)

        def step(kb, carry, masked, rows):
            cls, cgs, dq = carry
            off = pl.multiple_of(kb * t, t)
            kblk = k_ref[pl.ds(off, t), :]
            vblk = v_ref[pl.ds(off, t), :]
            ks = _per_head(kblk)
            dq_top = dq[:rows]
            dk = jnp.zeros((t, LANES), F32)
            dv = jnp.zeros((t, LANES), F32)
            new_cls, new_cgs = [], []
            for hd in range(2):
                z = lax.dot_general(qs[hd][:rows], kblk, NT, preferred_element_type=F32)
                l = _logsig_neg(z)
                if masked:
                    l = jnp.where(causal, l, 0.0)
                e = z + l + ((lts[hd][:rows] - cls[hd][:rows]) - _split_dot(l, tri))
                if masked:
                    e = jnp.where(causal, e, -1e30)
                a = jnp.exp(e)
                g = lax.dot_general(dos[hd][:rows], vblk, NT, preferred_element_type=F32) * a
                p = cgs[hd][:rows] + jnp.dot(g.astype(BF16), tri, preferred_element_type=F32) - g
                el = jnp.exp(l)
                dz = g * el - p * (1.0 - el)
                if masked:
                    dz = jnp.where(causal, dz, 0.0)
                dzb = (dz * scale).astype(BF16)
                dq_top = dq_top + jnp.dot(dzb, ks[hd], preferred_element_type=F32)
                dk = dk + lax.dot_general(dzb, qm[hd][:rows], TN, preferred_element_type=F32)
                dv = dv + lax.dot_general(a.astype(BF16), dos[hd][:rows], TN, preferred_element_type=F32)
                new_cls.append(_with_top(cls[hd], cls[hd][:rows] + jnp.sum(l, axis=1, keepdims=True)))
                new_cgs.append(_with_top(cgs[hd], cgs[hd][:rows] + jnp.sum(g, axis=1, keepdims=True)))
            dk_ref[pl.ds(off, t), :] += dk
            dv_ref[pl.ds(off, t), :] += dv
            return tuple(new_cls), tuple(new_cgs), _with_top(dq, dq_top)

        zero = jnp.zeros((t, 1), F32)
        init = ((zero, zero), (zero, zero), jnp.zeros((t, LANES), F32))
        carry = lax.fori_loop(i - n_blocks, i - n_full, lambda kb, cr: step(kb, cr, False, ATT_PART), init)
        carry = lax.fori_loop(i - n_full, i, lambda kb, cr: step(kb, cr, False, t), carry)
        carry = step(i, carry, True, t)
        dq_ref[...] = carry[2]
        finish_exchange()

    blk = pl.BlockSpec((t, LANES), lambda p, i: (i, p))
    whole = pl.BlockSpec((s, LANES), lambda p, i: (0, p))
    res = pl.pallas_call(
        body, name="attn_bwd", grid=(hp, s // t),
        in_specs=[blk, whole, whole, blk, pl.BlockSpec((None, t, LANES), lambda p, i: (p, i, 0)),
                  pl.BlockSpec(memory_space=pltpu.SMEM)] + [ANY] * nx,
        out_specs=[blk, whole, whole] + [ANY] * nx,
        out_shape=[_sds(q.shape)] * 3 + x_shape,
        scratch_shapes=x_scratch,
        compiler_params=_params(("arbitrary", "arbitrary"), x_id),
    )(q, k, v, do, ltot, n_blocks, *x_arrs)
    return res[0], res[1], res[2], res[3:]


LATE = ["w_conv_branch", "w_att_branch", "w_out", "w_ffn_up", "w_ffn_down"]


def _full_weight(name, gathered):
    return _cols_to_full(gathered) if name in COL_SHARDED else gathered.reshape(-1, gathered.shape[2])


def _grad_slabs(name, grad):
    return _full_to_cols(grad) if name in COL_SHARDED else grad.reshape(N_DEV, -1, grad.shape[1])


def _side_slabs(name, grad):
    slabs = _grad_slabs(name, grad)
    return slabs.reshape((4, 2) + slabs.shape[1:])


def _local_step(x, target, w, late_blocks, opt):
    s = x.shape[0]
    w = dict(w)
    g1, g2, g3, g4 = w["norm_mix_pre"], w["norm_mix_post"], w["norm_ffn_pre"], w["norm_ffn_post"]

    w_in = w["w_in"]

    def proj_fn(xt, g1_, w_in_t):
        h = _rms(xt, g1_).astype(BF16)
        proj = lax.dot_general(h, w_in_t, NT, preferred_element_type=F32)
        return (h, *[proj[:, IN_SPLITS[n]:IN_SPLITS[n + 1]] for n in range(6)]), ()

    mix_weights = ["w_conv_branch", "w_att_branch", "w_out"]
    h1, conv_in, q, k, v, g_conv, g_att, g_out = _rowwise(
        "norm_proj", proj_fn, [x], [g1, w_in],
        [_sds((s, D_MODEL), BF16), _sds((s, 2 * CONV_DIM)), _sds((s, ATT_DIM), BF16), _sds((s, ATT_DIM), BF16),
         _sds((s, ATT_DIM), BF16), _sds((s, D_MODEL), BF16), _sds((s, D_MODEL), BF16)], tm=512,
        exchange=_relay_gather_exchange([late_blocks["w_out"]]))

    u3, u1, g_branches = _conv_fwd(conv_in, w["conv_dw_w"], w["conv_dw_b"], w["conv_ln_g"], w["conv_ln_b"],
                                   _relay_gather_exchange([late_blocks[nm] for nm in mix_weights[:2]]))
    for nm, g in zip(mix_weights, [*g_branches, g_out]):
        w[nm] = _full_weight(nm, g)
    half = D_MODEL // 2
    down_block = late_blocks["w_ffn_down"]
    att, ltot, n_blocks, (g_up, g_left) = _attn_fwd(
        q, k, v, _relay_gather_exchange([late_blocks["w_ffn_up"], down_block[:, :half]]))
    w["w_ffn_up"] = _full_weight("w_ffn_up", g_up)

    def merge_fn(u3t, at, gc, ga, xt, w_cb, w_ab, b_cb, w_out, g2_, g3_):
        cp = jnp.dot(u3t, w_cb, preferred_element_type=F32)
        ao = jnp.dot(at, w_ab, preferred_element_type=F32)
        mg = _merge(cp, ao, gc.astype(F32), ga.astype(F32), b_cb).astype(BF16)
        mix_ = jnp.dot(mg, w_out, preferred_element_type=F32)
        x2_ = xt + _rms(mix_, g2_)
        return (mg, cp, ao, mix_, x2_, _rms(x2_, g3_)), ()

    merged, conv_pre, att_out, mix, x2, h2 = _rowwise(
        "branch_merge_mix", merge_fn, [u3, att, g_conv, g_att, x],
        [w["w_conv_branch"], w["w_att_branch"], w["b_conv_branch"], w["w_out"], g2, g3],
        [_sds((s, D_MODEL), BF16)] * 3 + [_sds((s, D_MODEL)), _sds((s, D_MODEL)), _sds((s, D_MODEL), BF16)], tm=512)

    def ffn_up_fn(ht, w_up_t):
        gu_ = lax.dot_general(ht, w_up_t, NT, preferred_element_type=F32)
        return (gu_, _swiglu(gu_[:, :D_FF], gu_[:, D_FF:])), ()

    gu, act, g_right = _rowwise("ffn_up", ffn_up_fn, [h2], [w["w_ffn_up"]],
                                [_sds((s, 2 * D_FF), BF16), _sds((s, D_FF), BF16)], tm=512,
                                exchange=_relay_gather_exchange([down_block[:, half:]]))
    w_down = [_full_weight("w_ffn_down", g) for g in (g_left, g_right)]

    def final_fn(at, x2t, tgt, w_left, w_right, g4_):
        ff = jnp.concatenate([jnp.dot(at, w_left, preferred_element_type=F32),
                              jnp.dot(at, w_right, preferred_element_type=F32)], axis=1)
        n4, vjp = jax.vjp(_rms, ff, g4_)
        err = x2t + n4 - tgt
        dy = err * (1.0 / D_MODEL)
        dff, dg4 = vjp(dy)
        return (dy, dff), (jnp.sum(err * err, axis=0, keepdims=True), dg4)

    dy, dff, loss_cols, d_g4 = _rowwise("ffn_down_loss", final_fn, [act, x2, target], [*w_down, g4],
                                        [_sds((s, D_MODEL)), _sds((s, D_MODEL), BF16)],
                                        [_sds((1, D_MODEL)), _sds((1, D_MODEL))], tm=512)
    loss = 0.5 * jnp.sum(loss_cols) / D_MODEL

    d_w_down = _tn_matmul(act, dff, name="d_w_down")

    def act_bwd_fn(dfft, gut, w_left, w_right):
        d_act = (lax.dot_general(dfft[:, :half], w_left, NT, preferred_element_type=F32)
                 + lax.dot_general(dfft[:, half:], w_right, NT, preferred_element_type=F32))
        gu_ = gut.astype(F32)
        _, vjp = jax.vjp(_swiglu, gu_[:, :D_FF], gu_[:, D_FF:])
        return (jnp.concatenate(vjp(d_act), axis=1),), ()

    down_slabs = _side_slabs("w_ffn_down", d_w_down)
    dgu, theirs = _rowwise("ffn_act_bwd", act_bwd_fn, [dff, gu], w_down, [_sds((s, 2 * D_FF), BF16)],
                           exchange=_pair_exchange([down_slabs]))
    down_sums = _pair_sum("pair_sum_w_ffn_down", down_slabs, theirs)
    d_w_up = _tn_matmul(dgu, h2, name="d_w_up")
    received = {}
    up_slabs = _side_slabs("w_ffn_up", d_w_up)

    def mid_bwd_fn(dgut, xt, mt, dyt, w_up_t, g2_, g3_):
        dh = jnp.dot(dgut, w_up_t, preferred_element_type=F32)
        n2, vjp2 = jax.vjp(_rms, mt, g2_)
        x2_ = xt + n2
        _, vjp3 = jax.vjp(_rms, x2_, g3_)
        dx2_, dg3 = vjp3(dh)
        dx2_ = dx2_ + dyt
        dmix_, dg2 = vjp2(dx2_)
        return (dx2_, dmix_), (dg2, dg3)

    dx2, dmix, d_g2, d_g3, received["w_ffn_down"] = _rowwise(
        "ffn_up_mid_bwd", mid_bwd_fn, [dgu, x, mix, dy], [w["w_ffn_up"], g2, g3],
        [_sds((s, D_MODEL)), _sds((s, D_MODEL), BF16)], [_sds((1, D_MODEL)), _sds((1, D_MODEL))], tm=512,
        exchange=_chip_exchange([down_sums]))
    d_w_out = _tn_matmul(merged, dmix, name="d_w_out")

    def merge_bwd_fn(dmt, cp, ao, gc, ga, w_out, w_cb, w_ab, b_cb):
        dm = lax.dot_general(dmt, w_out, NT, preferred_element_type=F32)
        _, vjp = jax.vjp(_merge, cp.astype(F32), ao.astype(F32), gc.astype(F32), ga.astype(F32), b_cb)
        dcp, dao, dgc, dga, dbias = vjp(dm)
        dcp, dao = dcp.astype(BF16), dao.astype(BF16)
        du3_ = lax.dot_general(dcp, w_cb, NT, preferred_element_type=F32)
        datt_ = lax.dot_general(dao, w_ab, NT, preferred_element_type=F32)
        return (dcp, dao, dgc, dga, du3_, datt_), (dbias,)

    d_conv_out, d_att_out, d_g_conv, d_g_att, du3, d_att, d_b_cb, theirs = _rowwise(
        "merge_bwd", merge_bwd_fn, [dmix, conv_pre, att_out, g_conv, g_att],
        [w["w_out"], w["w_conv_branch"], w["w_att_branch"], w["b_conv_branch"]],
        [_sds((s, D_MODEL), BF16)] * 4 + [_sds((s, CONV_DIM)), _sds((s, ATT_DIM), BF16)], [_sds((1, D_MODEL))], tm=512,
        exchange=_pair_exchange([up_slabs]))

    d_w_cb = _tn_matmul(u3, d_conv_out, name="d_w_conv_branch")
    d_w_ab = _tn_matmul(att, d_att_out, name="d_w_att_branch")

    dq, dk, dv, (received["w_ffn_up"],) = _attn_bwd(
        q, k, v, d_att, ltot, n_blocks, _chip_exchange([_pair_sum("pair_sum_w_ffn_up", up_slabs, theirs)]))

    mix_grads = {"w_conv_branch": d_w_cb, "w_att_branch": d_w_ab, "w_out": d_w_out}
    (d_conv_in, d_dw_w, d_dw_b, d_ln_g, d_ln_b), landed = _conv_bwd(
        conv_in, u1, du3, w["conv_ln_g"], w["conv_ln_b"], w["conv_dw_w"],
        _scatter_exchange([_grad_slabs(nm, mix_grads[nm]) for nm in mix_weights[:2]]))
    received.update(zip(mix_weights[:2], landed))

    d_proj = [d_conv_in, dq, dk, dv, d_g_conv, d_g_att]
    d_w_in, (received["w_out"],) = _pieces_tn_matmul(
        d_proj, h1, name="d_w_in", exchange=_scatter_exchange([_grad_slabs("w_out", d_w_out)]))
    in_slabs = _side_slabs("w_in", d_w_in)
    (theirs,) = _exchange_call("pair_swap_w_in", _pair_exchange([in_slabs]))

    early = list(opt)

    def pre_bwd_fn(*args):
        groups, (xt, dx2t), jobs, (w_in_t, g_) = args[:6], args[6:8], args[8:-2], args[-2:]
        dh = sum(jnp.dot(grp.astype(BF16), w_in_t[IN_SPLITS[n]:IN_SPLITS[n + 1]], preferred_element_type=F32)
                 for n, grp in enumerate(groups))
        _, vjp = jax.vjp(_rms, xt, g_)
        dx_, dg_ = vjp(dh)
        updates = [_sum_adamw_tile(*jobs[4 * n:4 * n + 4]) for n in range(len(early))]
        return (dx_ + dx2t, *[u for four in updates for u in four]), (dg_,)

    res = _rowwise(
        "proj_norm_bwd", pre_bwd_fn,
        d_proj + [x, dx2] + [a for nm in early for a in (received[nm], *opt[nm])], [w_in, g1],
        [_sds((s, D_MODEL))] + [_sds(opt[nm][0].shape) for nm in early for _ in range(4)],
        [_sds((1, D_MODEL))], tm=512, exchange=_chip_exchange([_pair_sum("pair_sum_w_in", in_slabs, theirs)]))
    grad_x, d_g1, received["w_in"] = res[0], res[-2], res[-1]
    updated = {nm: res[1 + 4 * n:5 + 4 * n] for n, nm in enumerate(early)}

    grads = {
        "norm_mix_pre": d_g1, "conv_dw_w": d_dw_w, "conv_dw_b": d_dw_b,
        "conv_ln_g": d_ln_g, "conv_ln_b": d_ln_b, "b_conv_branch": d_b_cb,
        "norm_mix_post": d_g2, "norm_ffn_pre": d_g3, "norm_ffn_post": d_g4,
    }
    return loss, grad_x, received, updated, grads


def _place():
    x, y, c = lax.axis_index("x"), lax.axis_index("y"), lax.axis_index("c")
    return x, y, c


def _slot(px, py, pc):
    return 4 * px + 2 * py + pc


def _exchange_scratch(n):
    return [pltpu.SemaphoreType.DMA((7 * n,)), pltpu.SemaphoreType.DMA((7 * n,)), pltpu.SemaphoreType.DMA((n,))]


GATHER_ID, SCATTER_ID, PAIR_ID, CHIP_ID, RELAY_ID = 0, 1, 2, 3, 4


def _handshake(peers):
    def announce():
        for peer in peers:
            pl.semaphore_signal(pltpu.get_barrier_semaphore(), inc=1, device_id=peer, device_id_type=MESH)

    def arrive():
        pl.semaphore_wait(pltpu.get_barrier_semaphore(), len(peers))

    return announce, arrive


def _gather_exchange(arrs):
    n = len(arrs)

    def phases(ins, outs, send_sems, recv_sems, local_sems):
        x, y, c = _place()
        me, sibling = (x, y, c), (x, y, 1 - c)
        chips = [(1 - x, y), (x, 1 - y), (1 - x, 1 - y)]

        def copy(a, kk, block, to, src=None):
            dst = outs[a].at[_slot(*block)]
            return pltpu.make_async_remote_copy(
                src_ref=dst if src is None else src, dst_ref=dst,
                send_sem=send_sems.at[a * 7 + kk], recv_sem=recv_sems.at[a * 7 + kk],
                device_id=to, device_id_type=MESH)

        mine = [pltpu.make_async_copy(ins[a], outs[a].at[_slot(*me)], local_sems.at[a]) for a in range(n)]
        first = []
        for a in range(n):
            first.append(copy(a, 0, me, sibling, src=ins[a]))
            first += [copy(a, 1 + j, me, (*chip, c), src=ins[a]) for j, chip in enumerate(chips)]
        passed = [copy(a, 4 + j, (*chip, c), sibling) for j, chip in enumerate(chips) for a in range(n)]

        announce, arrive = _handshake([sibling] + [(*chip, c) for chip in chips])

        def send():
            arrive()
            for cp in mine + first:
                cp.start()

        def pass_on():
            for j, chip in enumerate(chips):
                for a in range(n):
                    copy(a, 1 + j, (*chip, c), me).wait_recv()
                    passed[j * n + a].start()

        def finish():
            for a in range(n):
                copy(a, 0, sibling, me).wait_recv()
                for j, chip in enumerate(chips):
                    copy(a, 4 + j, (*chip, 1 - c), me).wait_recv()
            for cp in first + passed:
                cp.wait_send()
            for cp in mine:
                cp.wait()

        return [announce, send, pass_on, finish]

    return list(arrs), [_sds((N_DEV,) + a.shape, a.dtype) for a in arrs], _exchange_scratch(n), phases, GATHER_ID


def _relay_gather_exchange(arrs):
    n = len(arrs)
    per = 8

    def phases(ins, outs, send_sems, recv_sems, local_sems):
        x, y, c = _place()
        me, sibling = (x, y, c), (x, y, 1 - c)
        beside, below, across = (1 - x, y, c), (x, 1 - y, c), (1 - x, 1 - y, c)

        def copy(a, kk, block, to, src=None, rows=None):
            where = _slot(*block) if rows is None else (_slot(*block), rows)
            dst = outs[a].at[where]
            return pltpu.make_async_remote_copy(
                src_ref=dst if src is None else src, dst_ref=dst,
                send_sem=send_sems.at[a * per + kk], recv_sem=recv_sems.at[a * per + kk],
                device_id=to, device_id_type=MESH)

        def halves(a):
            h = ins[a].shape[0] // 2
            return pl.ds(0, h), pl.ds(h, ins[a].shape[0] - h)

        mine = [pltpu.make_async_copy(ins[a], outs[a].at[_slot(*me)], local_sems.at[a]) for a in range(n)]
        first = [copy(a, kk, me, to, src=ins[a]) for a in range(n) for kk, to in enumerate([sibling, beside, below])]
        relayed = [[copy(a, 3, beside, sibling), copy(a, 5, beside, below, rows=halves(a)[0])] for a in range(n)]
        relayed += [[copy(a, 4, below, sibling), copy(a, 6, below, beside, rows=halves(a)[1])] for a in range(n)]
        passed = [copy(a, 7, across, sibling) for a in range(n)]

        announce, arrive = _handshake([sibling, beside, below])

        def send():
            arrive()
            for cp in mine + first:
                cp.start()

        def relay():
            for kk, block in ((1, beside), (2, below)):
                for a in range(n):
                    copy(a, kk, block, me).wait_recv()
                    for cp in relayed[(kk - 1) * n + a]:
                        cp.start()

        def pass_on():
            for a in range(n):
                copy(a, 5, across, me, rows=halves(a)[0]).wait_recv()
                copy(a, 6, across, me, rows=halves(a)[1]).wait_recv()
                passed[a].start()

        def finish():
            for a in range(n):
                for kk, block in ((0, me), (3, beside), (4, below), (7, across)):
                    copy(a, kk, (*block[:2], 1 - c), me).wait_recv()
            for cp in first + [cp for two in relayed for cp in two] + passed:
                cp.wait_send()
            for cp in mine:
                cp.wait()

        return [announce, send, relay, pass_on, finish]

    scratch = [pltpu.SemaphoreType.DMA((per * n,)), pltpu.SemaphoreType.DMA((per * n,)), pltpu.SemaphoreType.DMA((n,))]
    return list(arrs), [_sds((N_DEV,) + a.shape, a.dtype) for a in arrs], scratch, phases, RELAY_ID


def _scatter_exchange(arrs):
    n = len(arrs)
    flips = [(fx, fy, fc) for fx in (0, 1) for fy in (0, 1) for fc in (0, 1)][1:]

    def phases(ins, outs, send_sems, recv_sems, local_sems):
        x, y, c = _place()
        mine = _slot(x, y, c)
        local = [pltpu.make_async_copy(ins[a].at[mine], outs[a].at[mine], local_sems.at[a]) for a in range(n)]
        peers = [((1 - x) if fx else x, (1 - y) if fy else y, (1 - c) if fc else c) for fx, fy, fc in flips]

        def copy(a, kk, src_slot, dst_slot):
            return pltpu.make_async_remote_copy(
                src_ref=ins[a].at[src_slot], dst_ref=outs[a].at[dst_slot],
                send_sem=send_sems.at[a * 7 + kk], recv_sem=recv_sems.at[a * 7 + kk],
                device_id=peers[kk], device_id_type=MESH)

        sends = [copy(a, kk, _slot(*peers[kk]), mine) for a in range(n) for kk in range(7)]

        announce, arrive = _handshake(peers)

        def send():
            arrive()
            for cp in local + sends:
                cp.start()

        def finish():
            for a in range(n):
                for kk in range(7):
                    copy(a, kk, mine, _slot(*peers[kk])).wait_recv()
            for cp in sends:
                cp.wait_send()
            for cp in local:
                cp.wait()

        return [announce, send, finish]

    return list(arrs), [_sds(a.shape, a.dtype) for a in arrs], _exchange_scratch(n), phases, SCATTER_ID


def _pair_exchange(arrs):
    n = len(arrs)

    def phases(ins, outs, send_sems, recv_sems, local_sems):
        x, y, c = _place()

        def copy(a, chip, side):
            return pltpu.make_async_remote_copy(
                src_ref=ins[a].at[chip, side], dst_ref=outs[a].at[chip],
                send_sem=send_sems.at[a * 7 + chip], recv_sem=recv_sems.at[a * 7 + chip],
                device_id=(x, y, 1 - c), device_id_type=MESH)

        sends = [copy(a, chip, 1 - c) for a in range(n) for chip in range(4)]

        announce, arrive = _handshake([(x, y, 1 - c)])

        def send():
            arrive()
            for cp in sends:
                cp.start()

        def finish():
            for a in range(n):
                for chip in range(4):
                    copy(a, chip, c).wait_recv()
            for cp in sends:
                cp.wait_send()

        return [announce, send, finish]

    return list(arrs), [_sds((4,) + a.shape[2:], a.dtype) for a in arrs], _exchange_scratch(n), phases, PAIR_ID


def _chip_exchange(arrs):
    n = len(arrs)

    def phases(ins, outs, send_sems, recv_sems, local_sems):
        x, y, c = _place()
        mine = 2 * x + y
        chips = [(1 - x, y), (x, 1 - y), (1 - x, 1 - y)]
        local = [pltpu.make_async_copy(ins[a].at[mine], outs[a].at[mine], local_sems.at[a]) for a in range(n)]

        def copy(a, j, src_slot, dst_slot):
            return pltpu.make_async_remote_copy(
                src_ref=ins[a].at[src_slot], dst_ref=outs[a].at[dst_slot],
                send_sem=send_sems.at[a * 7 + j], recv_sem=recv_sems.at[a * 7 + j],
                device_id=(*chips[j], c), device_id_type=MESH)

        sends = [copy(a, j, 2 * chips[j][0] + chips[j][1], mine) for a in range(n) for j in range(3)]

        announce, arrive = _handshake([(*chip, c) for chip in chips])

        def send():
            announce()
            arrive()
            for cp in local + sends:
                cp.start()

        def finish():
            for a in range(n):
                for j in range(3):
                    copy(a, j, mine, 2 * chips[j][0] + chips[j][1]).wait_recv()
            for cp in sends:
                cp.wait_send()
            for cp in local:
                cp.wait()

        return [send, lambda: None, finish]

    return list(arrs), [_sds(a.shape, a.dtype) for a in arrs], _exchange_scratch(n), phases, CHIP_ID


def _pair_sum(name, mine, theirs):
    _, _, r, c = mine.shape

    def body(side_ref, m_ref, t_ref, o_ref):
        o_ref[...] = (m_ref[...].astype(F32) + t_ref[...].astype(F32)).astype(o_ref.dtype)

    return pl.pallas_call(
        body, name=name,
        grid_spec=pltpu.PrefetchScalarGridSpec(
            num_scalar_prefetch=1, grid=(4,),
            in_specs=[pl.BlockSpec((None, None, r, c), lambda j, side: (j, side[0], 0, 0)),
                      pl.BlockSpec((None, r, c), lambda j, side: (j, 0, 0))],
            out_specs=pl.BlockSpec((None, r, c), lambda j, side: (j, 0, 0))),
        out_shape=_sds(theirs.shape, theirs.dtype),
        compiler_params=_params(("parallel",)),
    )(lax.axis_index("c").astype(jnp.int32).reshape(1), mine, theirs)


def _exchange_call(name, exchange):
    arrs, out_shape, scratch, phases, collective_id = exchange
    n = len(arrs)

    def body(*refs):
        for step in phases(refs[:n], refs[n:2 * n], *refs[2 * n:]):
            step()

    return pl.pallas_call(body, name=name, in_specs=[ANY] * n, out_specs=[ANY] * n,
                          out_shape=out_shape, scratch_shapes=scratch,
                          compiler_params=pltpu.CompilerParams(collective_id=collective_id))(*arrs)


def _carry_exchange(exchange, refs, n_in, n_out, first, middle, last, halfway, at_once=False):
    arrs, _, _, phases, _ = exchange
    n = len(arrs)
    if n == 0:
        return lambda: None
    ins = refs[n_in:n_in + n]
    outs = refs[n_in + n + n_out:n_in + 2 * n + n_out]
    sems = n_in + 2 * n + n_out
    steps = phases(ins, outs, *refs[sems:sems + 3])
    pl.when(first)(steps[0])
    if at_once:
        pl.when(first)(steps[1])

    def close():
        if not at_once:
            pl.when(first)(steps[1])
        if len(steps) == 5:
            pl.when(halfway)(steps[2])
        if len(steps) >= 4:
            pl.when(middle)(steps[-2])
        pl.when(last)(steps[-1])

    return close


def _adamw_math(w, g, m, v):
    m2 = ADAM_B1 * m + (1.0 - ADAM_B1) * g
    v2 = ADAM_B2 * v + (1.0 - ADAM_B2) * jnp.square(g)
    m_hat = m2 / (1.0 - ADAM_B1 ** ADAM_STEP)
    v_hat = v2 / (1.0 - ADAM_B2 ** ADAM_STEP)
    delta = -ADAM_LR * (m_hat / (jnp.sqrt(v_hat) + ADAM_EPS) + ADAM_WD * w)
    return delta, m2, v2


def _sum_adamw_tile(parts, w, m, v):
    g = parts[0].astype(F32)
    for d in range(1, parts.shape[0]):
        g = g + parts[d].astype(F32)
    return (g, *_adamw_math(w, g, m, v))


def _sum_adamw(name, parts, w, m, v, tr=256):
    p, r, c = parts.shape
    tr = _pick(r, tr, 16)

    def body(p_ref, w_ref, m_ref, v_ref, g_ref, d_ref, m2_ref, v2_ref):
        g_ref[...], d_ref[...], m2_ref[...], v2_ref[...] = _sum_adamw_tile(p_ref[...], w_ref[...], m_ref[...], v_ref[...])

    tile = pl.BlockSpec((tr, c), lambda i: (i, 0))
    return pl.pallas_call(
        body, name=name, grid=(r // tr,),
        in_specs=[pl.BlockSpec((p, tr, c), lambda i: (0, i, 0)), tile, tile, tile],
        out_specs=[tile] * 4, out_shape=[_sds((r, c))] * 4,
        compiler_params=_params(("parallel",)),
    )(parts, w, m, v)


def _sum_parts(name, parts):
    p, r, c = parts.shape

    def body(p_ref, o_ref):
        g = p_ref[0]
        for d in range(1, p):
            g = g + p_ref[d]
        o_ref[...] = g

    return pl.pallas_call(
        body, name=name, out_shape=_sds((r, c)),
        in_specs=[pl.BlockSpec(memory_space=pltpu.VMEM)], out_specs=pl.BlockSpec(memory_space=pltpu.VMEM),
    )(parts)


WEIGHTS = ["norm_mix_pre", "w_in", "conv_dw_w", "conv_dw_b", "conv_ln_g", "conv_ln_b", "w_conv_branch",
           "b_conv_branch", "w_att_branch", "w_out", "norm_mix_post", "norm_ffn_pre", "w_ffn_up", "w_ffn_down",
           "norm_ffn_post"]
COL_SHARDED = ["w_conv_branch", "w_att_branch"]
TRANSPOSED = ["w_in", "w_ffn_up"]
VECTORS = ["norm_mix_pre", "conv_dw_b", "conv_ln_g", "conv_ln_b", "b_conv_branch", "norm_mix_post",
           "norm_ffn_pre", "norm_ffn_post"]


def _cols_to_full(g):
    return g.transpose(1, 0, 2).reshape(g.shape[1], N_DEV * g.shape[2])


def _full_to_cols(f):
    return f.reshape(f.shape[0], N_DEV, f.shape[1] // N_DEV).transpose(1, 0, 2)


PACK_ROWS = 7


def _pack_vectors(vecs, extra=None):
    parts = [vecs[nm].reshape(-1) for nm in VECTORS]
    parts.append(jnp.zeros((1,), F32) if extra is None else extra.reshape(1))
    used = sum(p.size for p in parts)
    parts.append(jnp.zeros((PACK_ROWS * D_MODEL - used,), F32))
    return jnp.concatenate(parts).reshape(PACK_ROWS, D_MODEL)


def _unpack_vectors(packed, sizes):
    flat, out, at = packed.reshape(-1), {}, 0
    for nm in VECTORS:
        out[nm] = flat[at:at + sizes[nm]]
        at += sizes[nm]
    return out, flat[at]


def kernel(x, norm_mix_pre, w_in, conv_dw_w, conv_dw_b, conv_ln_g, conv_ln_b, w_conv_branch, b_conv_branch, w_att_branch, w_out, norm_mix_post, norm_ffn_pre, w_ffn_up, w_ffn_down, norm_ffn_post, loss_target, m_norm_mix_pre, m_w_in, m_conv_dw_w, m_conv_dw_b, m_conv_ln_g, m_conv_ln_b, m_w_conv_branch, m_b_conv_branch, m_w_att_branch, m_w_out, m_norm_mix_post, m_norm_ffn_pre, m_w_ffn_up, m_w_ffn_down, m_norm_ffn_post, v_norm_mix_pre, v_w_in, v_conv_dw_w, v_conv_dw_b, v_conv_ln_g, v_conv_ln_b, v_w_conv_branch, v_b_conv_branch, v_w_att_branch, v_w_out, v_norm_mix_post, v_norm_ffn_pre, v_w_ffn_up, v_w_ffn_down, v_norm_ffn_post):
    ws = dict(zip(WEIGHTS, [norm_mix_pre, w_in, conv_dw_w, conv_dw_b, conv_ln_g, conv_ln_b, w_conv_branch,
                            b_conv_branch, w_att_branch, w_out, norm_mix_post, norm_ffn_pre, w_ffn_up, w_ffn_down,
                            norm_ffn_post]))
    ms = dict(zip(WEIGHTS, [m_norm_mix_pre, m_w_in, m_conv_dw_w, m_conv_dw_b, m_conv_ln_g, m_conv_ln_b,
                            m_w_conv_branch, m_b_conv_branch, m_w_att_branch, m_w_out, m_norm_mix_post,
                            m_norm_ffn_pre, m_w_ffn_up, m_w_ffn_down, m_norm_ffn_post]))
    vs = dict(zip(WEIGHTS, [v_norm_mix_pre, v_w_in, v_conv_dw_w, v_conv_dw_b, v_conv_ln_g, v_conv_ln_b,
                            v_w_conv_branch, v_b_conv_branch, v_w_att_branch, v_w_out, v_norm_mix_post,
                            v_norm_ffn_pre, v_w_ffn_up, v_w_ffn_down, v_norm_ffn_post]))

    dw_block = jnp.pad(conv_dw_w, ((0, 1), (0, 0)))
    g_in, g_dw = _exchange_call("gather_first", _relay_gather_exchange([w_in.T.astype(BF16), dw_block]))
    full = {"w_in": _full_weight("w_in", g_in), "conv_dw_w": _cols_to_full(g_dw)}
    for nm in VECTORS:
        full[nm] = ws[nm].reshape(1, -1)

    def as_kept(nm, a):
        return a.T if nm in TRANSPOSED else a

    ride_along = ["w_ffn_up", "w_out"]
    loss_local, grad_x, received, updated, grads = _local_step(
        x[0], loss_target[0], full, {nm: as_kept(nm, ws[nm]).astype(BF16) for nm in LATE},
        {nm: tuple(as_kept(nm, a[nm]) for a in (ws, ms, vs)) for nm in ride_along})

    small = _exchange_call("gather_small_grads", _gather_exchange(
        [_pack_vectors(grads, extra=loss_local), grads["conv_dw_w"]]))
    out_g, out_d, out_m, out_v = {}, {}, {}, {}
    for nm in LATE + ["w_in"]:
        res = updated[nm] if nm in updated else _sum_adamw(
            "adamw_" + nm, received[nm], *[as_kept(nm, a[nm]) for a in (ws, ms, vs)])
        out_g[nm], out_d[nm], out_m[nm], out_v[nm] = [as_kept(nm, r) for r in res]
    sizes = {nm: ws[nm].size for nm in VECTORS}
    vec = _sum_adamw("adamw_vectors", small[0], _pack_vectors(ws), _pack_vectors(ms), _pack_vectors(vs))
    for res, dst in zip(vec, (out_g, out_d, out_m, out_v)):
        dst.update(_unpack_vectors(res, sizes)[0])
    loss = _unpack_vectors(vec[0], sizes)[1]
    dw_full = _sum_parts("sum_dw_grads", small[1])
    me = _slot(*_place())
    dw_mine = lax.dynamic_slice(dw_full, (0, me * (CONV_DIM // N_DEV)), (CONV_WIDTH, CONV_DIM // N_DEV))
    nm = "conv_dw_w"
    out_g[nm], out_d[nm], out_m[nm], out_v[nm] = _sum_adamw("adamw_dw", dw_mine[None], ws[nm], ms[nm], vs[nm])

    outs = [loss, grad_x[None]]
    for group in (out_g, out_d, out_m, out_v):
        outs += [group[nm] for nm in WEIGHTS]
    return tuple(outs)
```
